```python
import math
import jax, jax.numpy as jnp
from jax import lax
import numpy as np

D_MODEL = 1024
BATCH = 8
SEQ = 2048
DEPTH = 4

N_MIXERS = 2
N_GDN_LAYERS = (DEPTH + N_MIXERS - 1) // N_MIXERS
N_MLA_LAYERS = DEPTH // N_MIXERS

GDN_HEADS = 8
GDN_HEAD_DIM = 128
GDN_KEY_DIM = GDN_HEADS * GDN_HEAD_DIM
GDN_VALUE_DIM = GDN_HEADS * GDN_HEAD_DIM
GDN_CONV = 4
GDN_CHUNK = 64
GDN_IN = 2 * GDN_KEY_DIM + 2 * GDN_VALUE_DIM + 2 * GDN_HEADS

MLA_HEADS = 8
MLA_NOPE = 128
MLA_ROPE = 64
MLA_V = 128
MLA_Q_RANK = 384
MLA_KV_RANK = 256
MLA_IN = MLA_Q_RANK + MLA_KV_RANK + MLA_ROPE
ROPE_THETA = 10000.0
Q_BLOCK = 128

D_FF = ((8 * D_MODEL + 3 * 256 - 1) // (3 * 256)) * 256
N_MOD = 6
EPS = 1e-6

kernel_name = "hybrid_gdn_mla_adaln_trunk"


def rmsnorm(x, g):
    xf = x.astype(jnp.float32)
    y = xf * lax.rsqrt(jnp.mean(xf * xf, axis=-1, keepdims=True) + EPS)
    return (y * g.astype(jnp.float32)).astype(x.dtype)


def l2norm(x):
    return x * lax.rsqrt(jnp.sum(x * x, axis=-1, keepdims=True) + EPS)


def causal_depthwise_conv(x, w):
    K = w.shape[-1]
    kern = jnp.transpose(w)[:, None, :].astype(x.dtype)
    return lax.conv_general_dilated(
        x, kern, window_strides=(1,), padding=[(K - 1, 0)],
        dimension_numbers=("NWC", "WIO", "NWC"), feature_group_count=x.shape[-1])


def chunk_gated_delta_rule(q, k, v, g, beta):
    B, H, T, Dk = q.shape
    Dv = v.shape[-1]
    C = GDN_CHUNK
    N = T // C
    q = q.reshape(B, H, N, C, Dk)
    k = k.reshape(B, H, N, C, Dk)
    v = v.reshape(B, H, N, C, Dv)
    g_cum = jnp.cumsum(g.reshape(B, H, N, C), axis=-1)
    beta = beta.reshape(B, H, N, C)

    tril = jnp.tril(jnp.ones((C, C), dtype=bool))
    strict = jnp.tril(jnp.ones((C, C), dtype=bool), k=-1)
    diff = g_cum[..., :, None] - g_cum[..., None, :]
    decay = jnp.exp(jnp.where(tril, diff, -jnp.inf))

    kb = k * beta[..., None]
    L = jnp.where(strict, jnp.einsum("bhnid,bhnjd->bhnij", kb, k) * decay, 0.0)
    A = L + jnp.eye(C, dtype=L.dtype)
    u = lax.linalg.triangular_solve(A, v * beta[..., None], left_side=True, lower=True,
                                    unit_diagonal=True)
    w = lax.linalg.triangular_solve(A, kb * jnp.exp(g_cum)[..., None], left_side=True,
                                    lower=True, unit_diagonal=True)
    attn = jnp.einsum("bhnid,bhnjd->bhnij", q, k) * decay
    q_dec = q * jnp.exp(g_cum)[..., None]
    k_dec = k * jnp.exp(g_cum[..., -1:] - g_cum)[..., None]
    g_last = jnp.exp(g_cum[..., -1])

    def step(S, xs):
        q_i, k_i, w_i, u_i, a_i, gl_i = xs
        v_new = u_i - jnp.einsum("bhck,bhkv->bhcv", w_i, S)
        o_i = jnp.einsum("bhck,bhkv->bhcv", q_i, S) + jnp.einsum("bhij,bhjv->bhiv", a_i, v_new)
        S = S * gl_i[..., None, None] + jnp.einsum("bhck,bhcv->bhkv", k_i, v_new)
        return S, o_i

    xs = tuple(jnp.moveaxis(t, 2, 0) for t in (q_dec, k_dec, w, u, attn, g_last))
    S0 = jnp.zeros((B, H, Dk, Dv), dtype=jnp.float32)
    _, o = lax.scan(step, S0, xs)
    return jnp.moveaxis(o, 0, 2).reshape(B, H, T, Dv)


def gdn_mixer(h, w_in, conv_w, a_log, dt_bias, norm_g, w_out):
    B, T, _ = h.shape
    proj = h @ w_in
    qkv = proj[..., :2 * GDN_KEY_DIM + GDN_VALUE_DIM]
    o0 = 2 * GDN_KEY_DIM + GDN_VALUE_DIM
    gate = proj[..., o0:o0 + GDN_VALUE_DIM]
    a_in = proj[..., o0 + GDN_VALUE_DIM:o0 + GDN_VALUE_DIM + GDN_HEADS]
    b_in = proj[..., o0 + GDN_VALUE_DIM + GDN_HEADS:]

    qkv = jax.nn.silu(causal_depthwise_conv(qkv, conv_w)).astype(jnp.float32)
    q = qkv[..., :GDN_KEY_DIM].reshape(B, T, GDN_HEADS, GDN_HEAD_DIM)
    k = qkv[..., GDN_KEY_DIM:2 * GDN_KEY_DIM].reshape(B, T, GDN_HEADS, GDN_HEAD_DIM)
    v = qkv[..., 2 * GDN_KEY_DIM:].reshape(B, T, GDN_HEADS, GDN_HEAD_DIM)
    q = l2norm(q) * (GDN_HEAD_DIM ** -0.5)
    k = l2norm(k)
    beta = jax.nn.sigmoid(b_in.astype(jnp.float32))
    g = -jnp.exp(a_log.astype(jnp.float32)) * jax.nn.softplus(
        a_in.astype(jnp.float32) + dt_bias.astype(jnp.float32))

    to_bhtd = lambda t: jnp.transpose(t, (0, 2, 1, 3))
    o = chunk_gated_delta_rule(to_bhtd(q), to_bhtd(k), to_bhtd(v),
                               jnp.transpose(g, (0, 2, 1)), jnp.transpose(beta, (0, 2, 1)))
    o = jnp.transpose(o, (0, 2, 1, 3)).astype(h.dtype)
    o = rmsnorm(o, norm_g) * jax.nn.silu(gate.reshape(B, T, GDN_HEADS, GDN_HEAD_DIM))
    return o.reshape(B, T, GDN_VALUE_DIM) @ w_out


def apply_rope(x, cos, sin):
    half = x.shape[-1] // 2
    x1, x2 = x[..., :half], x[..., half:]
    return jnp.concatenate([x1 * cos - x2 * sin, x2 * cos + x1 * sin], axis=-1)


def causal_mla_attention(q_nope, q_rope, k_nope, k_rope, v):
    B, T, H, _ = q_nope.shape
    nb = T // Q_BLOCK
    scale = (MLA_NOPE + MLA_ROPE) ** -0.5
    qn = jnp.moveaxis(q_nope.reshape(B, nb, Q_BLOCK, H, MLA_NOPE), 1, 0)
    qr = jnp.moveaxis(q_rope.reshape(B, nb, Q_BLOCK, H, MLA_ROPE), 1, 0)
    kpos = jnp.arange(T)

    def block(args):
        qn_b, qr_b, start = args
        s = (jnp.einsum("bqhd,bkhd->bhqk", qn_b, k_nope)
             + jnp.einsum("bqhd,bkd->bhqk", qr_b, k_rope)).astype(jnp.float32) * scale
        qpos = start + jnp.arange(Q_BLOCK)
        s = jnp.where(kpos[None, :] <= qpos[:, None], s, -jnp.inf)
        p = jax.nn.softmax(s, axis=-1).astype(v.dtype)
        return jnp.einsum("bhqk,bkhd->bqhd", p, v)

    o = lax.map(block, (qn, qr, jnp.arange(nb) * Q_BLOCK))
    return jnp.moveaxis(o, 0, 1).reshape(B, T, H, MLA_V)


def mla_mixer(h, cos, sin, w_in, q_norm_g, kv_norm_g, w_uq, w_ukv, w_out):
    B, T, _ = h.shape
    proj = h @ w_in
    c_q = proj[..., :MLA_Q_RANK]
    c_kv = proj[..., MLA_Q_RANK:MLA_Q_RANK + MLA_KV_RANK]
    k_rope = proj[..., MLA_Q_RANK + MLA_KV_RANK:]
    q = (rmsnorm(c_q, q_norm_g) @ w_uq).reshape(B, T, MLA_HEADS, MLA_NOPE + MLA_ROPE)
    kv = (rmsnorm(c_kv, kv_norm_g) @ w_ukv).reshape(B, T, MLA_HEADS, MLA_NOPE + MLA_V)
    q_nope, q_rope = q[..., :MLA_NOPE], q[..., MLA_NOPE:]
    k_nope, v = kv[..., :MLA_NOPE], kv[..., MLA_NOPE:]
    q_rope = apply_rope(q_rope, cos[:, :, None, :], sin[:, :, None, :])
    k_rope = apply_rope(k_rope, cos, sin)
    o = causal_mla_attention(q_nope, q_rope, k_nope, k_rope, v)
    return o.reshape(B, T, MLA_HEADS * MLA_V) @ w_out


def swiglu(h, w_gate, w_up, w_down):
    return (jax.nn.silu(h @ w_gate) * (h @ w_up)) @ w_down


def _fwd_setup_inputs(seed: int = 0) -> dict:
    key = jax.random.key(seed)
    ks = jax.random.split(key, 24)
    f32 = jnp.float32
    nrm = lambda k, shape, s: jax.random.normal(k, shape, f32) * s
    x = jax.random.normal(ks[0], (BATCH, SEQ, D_MODEL), f32)
    c = jax.random.normal(ks[1], (BATCH, D_MODEL), f32)
    positions = (jnp.arange(SEQ, dtype=jnp.int32)[None, :]
                 + jax.random.randint(ks[2], (BATCH, 1), 0, 1024, dtype=jnp.int32))
    ada_w = nrm(ks[3], (DEPTH, D_MODEL, N_MOD * D_MODEL), 0.5 * D_MODEL ** -0.5)
    ada_b = nrm(ks[4], (DEPTH, N_MOD * D_MODEL), 0.02)
    norm_mix_g = 1.0 + nrm(ks[5], (DEPTH, D_MODEL), 0.05)
    norm_ffn_g = 1.0 + nrm(ks[6], (DEPTH, D_MODEL), 0.05)

    gdn_w_in = nrm(ks[7], (N_GDN_LAYERS, D_MODEL, GDN_IN), D_MODEL ** -0.5)
    gdn_conv_w = nrm(ks[8], (N_GDN_LAYERS, 2 * GDN_KEY_DIM + GDN_VALUE_DIM, GDN_CONV),
                     GDN_CONV ** -0.5)
    gdn_a_log = jnp.log(jax.random.uniform(ks[9], (N_GDN_LAYERS, GDN_HEADS), f32, 1.0, 16.0))
    dt = jnp.exp(jax.random.uniform(ks[10], (N_GDN_LAYERS, GDN_HEADS), f32,
                                    math.log(1e-3), math.log(1e-1)))
    gdn_dt_bias = dt + jnp.log(-jnp.expm1(-dt))
    gdn_norm_g = 1.0 + nrm(ks[11], (N_GDN_LAYERS, GDN_HEAD_DIM), 0.05)
    gdn_w_out = nrm(ks[12], (N_GDN_LAYERS, GDN_VALUE_DIM, D_MODEL), GDN_VALUE_DIM ** -0.5)

    mla_w_in = nrm(ks[13], (N_MLA_LAYERS, D_MODEL, MLA_IN), D_MODEL ** -0.5)
    mla_q_norm_g = 1.0 + nrm(ks[14], (N_MLA_LAYERS, MLA_Q_RANK), 0.05)
    mla_kv_norm_g = 1.0 + nrm(ks[15], (N_MLA_LAYERS, MLA_KV_RANK), 0.05)
    mla_w_uq = nrm(ks[16], (N_MLA_LAYERS, MLA_Q_RANK, MLA_HEADS * (MLA_NOPE + MLA_ROPE)),
                   MLA_Q_RANK ** -0.5)
    mla_w_ukv = nrm(ks[17], (N_MLA_LAYERS, MLA_KV_RANK, MLA_HEADS * (MLA_NOPE + MLA_V)),
                    MLA_KV_RANK ** -0.5)
    mla_w_out = nrm(ks[18], (N_MLA_LAYERS, MLA_HEADS * MLA_V, D_MODEL),
                    (MLA_HEADS * MLA_V) ** -0.5)

    ffn_w_gate = nrm(ks[19], (DEPTH, D_MODEL, D_FF), D_MODEL ** -0.5)
    ffn_w_up = nrm(ks[20], (DEPTH, D_MODEL, D_FF), D_MODEL ** -0.5)
    ffn_w_down = nrm(ks[21], (DEPTH, D_FF, D_MODEL), D_FF ** -0.5)
    final_norm_g = 1.0 + nrm(ks[22], (D_MODEL,), 0.05)
    return {
        "x": x, "c": c, "positions": positions,
        "ada_w": ada_w, "ada_b": ada_b, "norm_mix_g": norm_mix_g, "norm_ffn_g": norm_ffn_g,
        "gdn_w_in": gdn_w_in, "gdn_conv_w": gdn_conv_w, "gdn_a_log": gdn_a_log,
        "gdn_dt_bias": gdn_dt_bias, "gdn_norm_g": gdn_norm_g, "gdn_w_out": gdn_w_out,
        "mla_w_in": mla_w_in, "mla_q_norm_g": mla_q_norm_g, "mla_kv_norm_g": mla_kv_norm_g,
        "mla_w_uq": mla_w_uq, "mla_w_ukv": mla_w_ukv, "mla_w_out": mla_w_out,
        "ffn_w_gate": ffn_w_gate, "ffn_w_up": ffn_w_up, "ffn_w_down": ffn_w_down,
        "final_norm_g": final_norm_g,
    }


def _fwd_reference(x, c, positions, ada_w, ada_b, norm_mix_g, norm_ffn_g,
              gdn_w_in, gdn_conv_w, gdn_a_log, gdn_dt_bias, gdn_norm_g, gdn_w_out,
              mla_w_in, mla_q_norm_g, mla_kv_norm_g, mla_w_uq, mla_w_ukv, mla_w_out,
              ffn_w_gate, ffn_w_up, ffn_w_down, final_norm_g):
    inv_freq = ROPE_THETA ** (-jnp.arange(0, MLA_ROPE, 2, dtype=jnp.float32) / MLA_ROPE)
    ang = positions.astype(jnp.float32)[..., None] * inv_freq
    cos = jnp.cos(ang).astype(x.dtype)
    sin = jnp.sin(ang).astype(x.dtype)
    c_act = jax.nn.silu(c)

    for layer in range(DEPTH):
        mod = c_act @ ada_w[layer] + ada_b[layer]
        shift_m, scale_m, gate_m, shift_f, scale_f, gate_f = [
            m[:, None, :] for m in jnp.split(mod, N_MOD, axis=-1)]

        h = rmsnorm(x, norm_mix_g[layer]) * (1.0 + scale_m) + shift_m
        j = layer // N_MIXERS
        if layer % N_MIXERS == 0:
            y = gdn_mixer(h, gdn_w_in[j], gdn_conv_w[j], gdn_a_log[j], gdn_dt_bias[j],
                          gdn_norm_g[j], gdn_w_out[j])
        else:
            y = mla_mixer(h, cos, sin, mla_w_in[j], mla_q_norm_g[j], mla_kv_norm_g[j],
                          mla_w_uq[j], mla_w_ukv[j], mla_w_out[j])
        x = x + gate_m * y

        h = rmsnorm(x, norm_ffn_g[layer]) * (1.0 + scale_f) + shift_f
        x = x + gate_f * swiglu(h, ffn_w_gate[layer], ffn_w_up[layer], ffn_w_down[layer])

    return rmsnorm(x, final_norm_g)


import jax as _jax
import jax.numpy as _jnp

TWIN_FORMAT = 'train_step'
FWD_PARAMS = ['x', 'c', 'positions', 'ada_w', 'ada_b', 'norm_mix_g', 'norm_ffn_g', 'gdn_w_in', 'gdn_conv_w', 'gdn_a_log', 'gdn_dt_bias', 'gdn_norm_g', 'gdn_w_out', 'mla_w_in', 'mla_q_norm_g', 'mla_kv_norm_g', 'mla_w_uq', 'mla_w_ukv', 'mla_w_out', 'ffn_w_gate', 'ffn_w_up', 'ffn_w_down', 'final_norm_g']
TWIN_WEIGHTS = ['ada_w', 'ada_b', 'norm_mix_g', 'norm_ffn_g', 'gdn_w_in', 'gdn_conv_w', 'gdn_a_log', 'gdn_dt_bias', 'gdn_norm_g', 'gdn_w_out', 'mla_w_in', 'mla_q_norm_g', 'mla_kv_norm_g', 'mla_w_uq', 'mla_w_ukv', 'mla_w_out', 'ffn_w_gate', 'ffn_w_up', 'ffn_w_down', 'final_norm_g']
TWIN_DIFF_INPUT = 'x'
TWIN_INPUTS = ['x', 'c', 'positions', 'ada_w', 'ada_b', 'norm_mix_g', 'norm_ffn_g', 'gdn_w_in', 'gdn_conv_w', 'gdn_a_log', 'gdn_dt_bias', 'gdn_norm_g', 'gdn_w_out', 'mla_w_in', 'mla_q_norm_g', 'mla_kv_norm_g', 'mla_w_uq', 'mla_w_ukv', 'mla_w_out', 'ffn_w_gate', 'ffn_w_up', 'ffn_w_down', 'final_norm_g', 'loss_target', 'm_ada_w', 'm_ada_b', 'm_norm_mix_g', 'm_norm_ffn_g', 'm_gdn_w_in', 'm_gdn_conv_w', 'm_gdn_a_log', 'm_gdn_dt_bias', 'm_gdn_norm_g', 'm_gdn_w_out', 'm_mla_w_in', 'm_mla_q_norm_g', 'm_mla_kv_norm_g', 'm_mla_w_uq', 'm_mla_w_ukv', 'm_mla_w_out', 'm_ffn_w_gate', 'm_ffn_w_up', 'm_ffn_w_down', 'm_final_norm_g', 'v_ada_w', 'v_ada_b', 'v_norm_mix_g', 'v_norm_ffn_g', 'v_gdn_w_in', 'v_gdn_conv_w', 'v_gdn_a_log', 'v_gdn_dt_bias', 'v_gdn_norm_g', 'v_gdn_w_out', 'v_mla_w_in', 'v_mla_q_norm_g', 'v_mla_kv_norm_g', 'v_mla_w_uq', 'v_mla_w_ukv', 'v_mla_w_out', 'v_ffn_w_gate', 'v_ffn_w_up', 'v_ffn_w_down', 'v_final_norm_g']
TWIN_OUTPUTS = ['loss', 'grad_x', 'grad_ada_w', 'grad_ada_b', 'grad_norm_mix_g', 'grad_norm_ffn_g', 'grad_gdn_w_in', 'grad_gdn_conv_w', 'grad_gdn_a_log', 'grad_gdn_dt_bias', 'grad_gdn_norm_g', 'grad_gdn_w_out', 'grad_mla_w_in', 'grad_mla_q_norm_g', 'grad_mla_kv_norm_g', 'grad_mla_w_uq', 'grad_mla_w_ukv', 'grad_mla_w_out', 'grad_ffn_w_gate', 'grad_ffn_w_up', 'grad_ffn_w_down', 'grad_final_norm_g', 'delta_ada_w', 'delta_ada_b', 'delta_norm_mix_g', 'delta_norm_ffn_g', 'delta_gdn_w_in', 'delta_gdn_conv_w', 'delta_gdn_a_log', 'delta_gdn_dt_bias', 'delta_gdn_norm_g', 'delta_gdn_w_out', 'delta_mla_w_in', 'delta_mla_q_norm_g', 'delta_mla_kv_norm_g', 'delta_mla_w_uq', 'delta_mla_w_ukv', 'delta_mla_w_out', 'delta_ffn_w_gate', 'delta_ffn_w_up', 'delta_ffn_w_down', 'delta_final_norm_g', 'new_m_ada_w', 'new_m_ada_b', 'new_m_norm_mix_g', 'new_m_norm_ffn_g', 'new_m_gdn_w_in', 'new_m_gdn_conv_w', 'new_m_gdn_a_log', 'new_m_gdn_dt_bias', 'new_m_gdn_norm_g', 'new_m_gdn_w_out', 'new_m_mla_w_in', 'new_m_mla_q_norm_g', 'new_m_mla_kv_norm_g', 'new_m_mla_w_uq', 'new_m_mla_w_ukv', 'new_m_mla_w_out', 'new_m_ffn_w_gate', 'new_m_ffn_w_up', 'new_m_ffn_w_down', 'new_m_final_norm_g', 'new_v_ada_w', 'new_v_ada_b', 'new_v_norm_mix_g', 'new_v_norm_ffn_g', 'new_v_gdn_w_in', 'new_v_gdn_conv_w', 'new_v_gdn_a_log', 'new_v_gdn_dt_bias', 'new_v_gdn_norm_g', 'new_v_gdn_w_out', 'new_v_mla_w_in', 'new_v_mla_q_norm_g', 'new_v_mla_kv_norm_g', 'new_v_mla_w_uq', 'new_v_mla_w_ukv', 'new_v_mla_w_out', 'new_v_ffn_w_gate', 'new_v_ffn_w_up', 'new_v_ffn_w_down', 'new_v_final_norm_g']
TWIN_LEAF_KINDS = {'loss': 'loss', 'grad_x': 'grad_x', 'grad_ada_w': 'grad_w', 'grad_ada_b': 'grad_w', 'grad_norm_mix_g': 'grad_w', 'grad_norm_ffn_g': 'grad_w', 'grad_gdn_w_in': 'grad_w', 'grad_gdn_conv_w': 'grad_w', 'grad_gdn_a_log': 'grad_w', 'grad_gdn_dt_bias': 'grad_w', 'grad_gdn_norm_g': 'grad_w', 'grad_gdn_w_out': 'grad_w', 'grad_mla_w_in': 'grad_w', 'grad_mla_q_norm_g': 'grad_w', 'grad_mla_kv_norm_g': 'grad_w', 'grad_mla_w_uq': 'grad_w', 'grad_mla_w_ukv': 'grad_w', 'grad_mla_w_out': 'grad_w', 'grad_ffn_w_gate': 'grad_w', 'grad_ffn_w_up': 'grad_w', 'grad_ffn_w_down': 'grad_w', 'grad_final_norm_g': 'grad_w', 'delta_ada_w': 'delta_w', 'delta_ada_b': 'delta_w', 'delta_norm_mix_g': 'delta_w', 'delta_norm_ffn_g': 'delta_w', 'delta_gdn_w_in': 'delta_w', 'delta_gdn_conv_w': 'delta_w', 'delta_gdn_a_log': 'delta_w', 'delta_gdn_dt_bias': 'delta_w', 'delta_gdn_norm_g': 'delta_w', 'delta_gdn_w_out': 'delta_w', 'delta_mla_w_in': 'delta_w', 'delta_mla_q_norm_g': 'delta_w', 'delta_mla_kv_norm_g': 'delta_w', 'delta_mla_w_uq': 'delta_w', 'delta_mla_w_ukv': 'delta_w', 'delta_mla_w_out': 'delta_w', 'delta_ffn_w_gate': 'delta_w', 'delta_ffn_w_up': 'delta_w', 'delta_ffn_w_down': 'delta_w', 'delta_final_norm_g': 'delta_w', 'new_m_ada_w': 'new_m', 'new_m_ada_b': 'new_m', 'new_m_norm_mix_g': 'new_m', 'new_m_norm_ffn_g': 'new_m', 'new_m_gdn_w_in': 'new_m', 'new_m_gdn_conv_w': 'new_m', 'new_m_gdn_a_log': 'new_m', 'new_m_gdn_dt_bias': 'new_m', 'new_m_gdn_norm_g': 'new_m', 'new_m_gdn_w_out': 'new_m', 'new_m_mla_w_in': 'new_m', 'new_m_mla_q_norm_g': 'new_m', 'new_m_mla_kv_norm_g': 'new_m', 'new_m_mla_w_uq': 'new_m', 'new_m_mla_w_ukv': 'new_m', 'new_m_mla_w_out': 'new_m', 'new_m_ffn_w_gate': 'new_m', 'new_m_ffn_w_up': 'new_m', 'new_m_ffn_w_down': 'new_m', 'new_m_final_norm_g': 'new_m', 'new_v_ada_w': 'new_v', 'new_v_ada_b': 'new_v', 'new_v_norm_mix_g': 'new_v', 'new_v_norm_ffn_g': 'new_v', 'new_v_gdn_w_in': 'new_v', 'new_v_gdn_conv_w': 'new_v', 'new_v_gdn_a_log': 'new_v', 'new_v_gdn_dt_bias': 'new_v', 'new_v_gdn_norm_g': 'new_v', 'new_v_gdn_w_out': 'new_v', 'new_v_mla_w_in': 'new_v', 'new_v_mla_q_norm_g': 'new_v', 'new_v_mla_kv_norm_g': 'new_v', 'new_v_mla_w_uq': 'new_v', 'new_v_mla_w_ukv': 'new_v', 'new_v_mla_w_out': 'new_v', 'new_v_ffn_w_gate': 'new_v', 'new_v_ffn_w_up': 'new_v', 'new_v_ffn_w_down': 'new_v', 'new_v_final_norm_g': 'new_v'}


def _forward(args):
    return _fwd_reference(*[args[k] for k in FWD_PARAMS])


def _output_shape():
    out = _jax.eval_shape(lambda: _forward(_fwd_setup_inputs(0)))
    return out.shape, out.dtype

N_MICROBATCH = 1
ADAM_LR = 0.001
ADAM_B1 = 0.9
ADAM_B2 = 0.999
ADAM_EPS = 1e-08
ADAM_WD = 0.01
ADAM_STEP = 10
PER_EXAMPLE_BATCH_AXIS = {'x': 0, 'c': 0, 'positions': 0, 'loss_target': 0}
SHARED_INPUTS = []
_WEIGHT_DTYPES = {'ada_w': _jnp.float32, 'ada_b': _jnp.float32, 'norm_mix_g': _jnp.float32, 'norm_ffn_g': _jnp.float32, 'gdn_w_in': _jnp.float32, 'gdn_conv_w': _jnp.float32, 'gdn_a_log': _jnp.float32, 'gdn_dt_bias': _jnp.float32, 'gdn_norm_g': _jnp.float32, 'gdn_w_out': _jnp.float32, 'mla_w_in': _jnp.float32, 'mla_q_norm_g': _jnp.float32, 'mla_kv_norm_g': _jnp.float32, 'mla_w_uq': _jnp.float32, 'mla_w_ukv': _jnp.float32, 'mla_w_out': _jnp.float32, 'ffn_w_gate': _jnp.float32, 'ffn_w_up': _jnp.float32, 'ffn_w_down': _jnp.float32, 'final_norm_g': _jnp.float32}
MOMENT_SCALE = {'ada_w': 3.940970e-02, 'ada_b': 6.764546e-02, 'norm_mix_g': 3.029745e-02, 'norm_ffn_g': 3.799918e-02, 'gdn_w_in': 2.183678e-02, 'gdn_conv_w': 2.012827e-02, 'gdn_a_log': 1.012452e-01, 'gdn_dt_bias': 9.726995e-02, 'gdn_norm_g': 7.822848e-02, 'gdn_w_out': 2.635039e-02, 'mla_w_in': 2.038671e-02, 'mla_q_norm_g': 1.040344e-02, 'mla_kv_norm_g': 3.337838e-02, 'mla_w_uq': 5.321574e-03, 'mla_w_ukv': 1.057816e-02, 'mla_w_out': 1.400245e-02, 'ffn_w_gate': 1.703739e-02, 'ffn_w_up': 1.653478e-02, 'ffn_w_down': 2.745693e-02, 'final_norm_g': 1.604525e+01}


def _to_microbatches(a, axis):
    t = _jnp.moveaxis(a, axis, 0)
    t = t.reshape((N_MICROBATCH, t.shape[0] // N_MICROBATCH) + t.shape[1:])
    return _jnp.moveaxis(t, 1, axis + 1)


def setup_inputs(seed: int = 0) -> dict:
    inp = _fwd_setup_inputs(seed)
    key = _jax.random.fold_in(_jax.random.key(seed), 7919)
    shape, _ = _output_shape()
    out = dict(inp)
    out["loss_target"] = _jax.random.normal(_jax.random.fold_in(key, 0), shape, _jnp.float32)
    for i, name in enumerate(TWIN_WEIGHTS):
        w = inp[name].astype(_jnp.float32)
        if MOMENT_SCALE is None:
            s = _jnp.sqrt(_jnp.mean(_jnp.square(w)) + 1e-30)
        else:
            s = MOMENT_SCALE[name]
        km, kv = _jax.random.split(_jax.random.fold_in(key, i + 1))
        out[name] = w
        out["m_" + name] = s * _jax.random.normal(km, w.shape, _jnp.float32)
        out["v_" + name] = (s * s) * _jax.random.uniform(kv, w.shape, _jnp.float32, 0.5, 1.5)
    if N_MICROBATCH > 1:
        for name, axis in PER_EXAMPLE_BATCH_AXIS.items():
            out[name] = _to_microbatches(out[name], axis)
    return {'x': out['x'], 'c': out['c'], 'positions': out['positions'], 'ada_w': out['ada_w'], 'ada_b': out['ada_b'], 'norm_mix_g': out['norm_mix_g'], 'norm_ffn_g': out['norm_ffn_g'], 'gdn_w_in': out['gdn_w_in'], 'gdn_conv_w': out['gdn_conv_w'], 'gdn_a_log': out['gdn_a_log'], 'gdn_dt_bias': out['gdn_dt_bias'], 'gdn_norm_g': out['gdn_norm_g'], 'gdn_w_out': out['gdn_w_out'], 'mla_w_in': out['mla_w_in'], 'mla_q_norm_g': out['mla_q_norm_g'], 'mla_kv_norm_g': out['mla_kv_norm_g'], 'mla_w_uq': out['mla_w_uq'], 'mla_w_ukv': out['mla_w_ukv'], 'mla_w_out': out['mla_w_out'], 'ffn_w_gate': out['ffn_w_gate'], 'ffn_w_up': out['ffn_w_up'], 'ffn_w_down': out['ffn_w_down'], 'final_norm_g': out['final_norm_g'], 'loss_target': out['loss_target'], 'm_ada_w': out['m_ada_w'], 'm_ada_b': out['m_ada_b'], 'm_norm_mix_g': out['m_norm_mix_g'], 'm_norm_ffn_g': out['m_norm_ffn_g'], 'm_gdn_w_in': out['m_gdn_w_in'], 'm_gdn_conv_w': out['m_gdn_conv_w'], 'm_gdn_a_log': out['m_gdn_a_log'], 'm_gdn_dt_bias': out['m_gdn_dt_bias'], 'm_gdn_norm_g': out['m_gdn_norm_g'], 'm_gdn_w_out': out['m_gdn_w_out'], 'm_mla_w_in': out['m_mla_w_in'], 'm_mla_q_norm_g': out['m_mla_q_norm_g'], 'm_mla_kv_norm_g': out['m_mla_kv_norm_g'], 'm_mla_w_uq': out['m_mla_w_uq'], 'm_mla_w_ukv': out['m_mla_w_ukv'], 'm_mla_w_out': out['m_mla_w_out'], 'm_ffn_w_gate': out['m_ffn_w_gate'], 'm_ffn_w_up': out['m_ffn_w_up'], 'm_ffn_w_down': out['m_ffn_w_down'], 'm_final_norm_g': out['m_final_norm_g'], 'v_ada_w': out['v_ada_w'], 'v_ada_b': out['v_ada_b'], 'v_norm_mix_g': out['v_norm_mix_g'], 'v_norm_ffn_g': out['v_norm_ffn_g'], 'v_gdn_w_in': out['v_gdn_w_in'], 'v_gdn_conv_w': out['v_gdn_conv_w'], 'v_gdn_a_log': out['v_gdn_a_log'], 'v_gdn_dt_bias': out['v_gdn_dt_bias'], 'v_gdn_norm_g': out['v_gdn_norm_g'], 'v_gdn_w_out': out['v_gdn_w_out'], 'v_mla_w_in': out['v_mla_w_in'], 'v_mla_q_norm_g': out['v_mla_q_norm_g'], 'v_mla_kv_norm_g': out['v_mla_kv_norm_g'], 'v_mla_w_uq': out['v_mla_w_uq'], 'v_mla_w_ukv': out['v_mla_w_ukv'], 'v_mla_w_out': out['v_mla_w_out'], 'v_ffn_w_gate': out['v_ffn_w_gate'], 'v_ffn_w_up': out['v_ffn_w_up'], 'v_ffn_w_down': out['v_ffn_w_down'], 'v_final_norm_g': out['v_final_norm_g']}


def _loss(weights, diff, rest, loss_target):
    with _jax.named_scope("forward"):
        args = {**rest, TWIN_DIFF_INPUT: diff, **{k: w.astype(_WEIGHT_DTYPES[k]) for k, w in weights.items()}}
        y = _forward(args)
    with _jax.named_scope("loss_head"):
        err = _jnp.square(y.astype(_jnp.float32) - loss_target)
        return 0.5 * _jnp.sum(_jnp.mean(err, axis=-1)) if err.ndim else 0.5 * err


def _adamw(w, g, m, v):
    m = ADAM_B1 * m + (1.0 - ADAM_B1) * g
    v = ADAM_B2 * v + (1.0 - ADAM_B2) * _jnp.square(g)
    m_hat = m / (1.0 - ADAM_B1 ** ADAM_STEP)
    v_hat = v / (1.0 - ADAM_B2 ** ADAM_STEP)
    delta = -ADAM_LR * (m_hat / (_jnp.sqrt(v_hat) + ADAM_EPS) + ADAM_WD * w)
    return delta, m, v


def reference(x, c, positions, ada_w, ada_b, norm_mix_g, norm_ffn_g, gdn_w_in, gdn_conv_w, gdn_a_log, gdn_dt_bias, gdn_norm_g, gdn_w_out, mla_w_in, mla_q_norm_g, mla_kv_norm_g, mla_w_uq, mla_w_ukv, mla_w_out, ffn_w_gate, ffn_w_up, ffn_w_down, final_norm_g, loss_target, m_ada_w, m_ada_b, m_norm_mix_g, m_norm_ffn_g, m_gdn_w_in, m_gdn_conv_w, m_gdn_a_log, m_gdn_dt_bias, m_gdn_norm_g, m_gdn_w_out, m_mla_w_in, m_mla_q_norm_g, m_mla_kv_norm_g, m_mla_w_uq, m_mla_w_ukv, m_mla_w_out, m_ffn_w_gate, m_ffn_w_up, m_ffn_w_down, m_final_norm_g, v_ada_w, v_ada_b, v_norm_mix_g, v_norm_ffn_g, v_gdn_w_in, v_gdn_conv_w, v_gdn_a_log, v_gdn_dt_bias, v_gdn_norm_g, v_gdn_w_out, v_mla_w_in, v_mla_q_norm_g, v_mla_kv_norm_g, v_mla_w_uq, v_mla_w_ukv, v_mla_w_out, v_ffn_w_gate, v_ffn_w_up, v_ffn_w_down, v_final_norm_g):
    given = dict(x=x, c=c, positions=positions, ada_w=ada_w, ada_b=ada_b, norm_mix_g=norm_mix_g, norm_ffn_g=norm_ffn_g, gdn_w_in=gdn_w_in, gdn_conv_w=gdn_conv_w, gdn_a_log=gdn_a_log, gdn_dt_bias=gdn_dt_bias, gdn_norm_g=gdn_norm_g, gdn_w_out=gdn_w_out, mla_w_in=mla_w_in, mla_q_norm_g=mla_q_norm_g, mla_kv_norm_g=mla_kv_norm_g, mla_w_uq=mla_w_uq, mla_w_ukv=mla_w_ukv, mla_w_out=mla_w_out, ffn_w_gate=ffn_w_gate, ffn_w_up=ffn_w_up, ffn_w_down=ffn_w_down, final_norm_g=final_norm_g, loss_target=loss_target, m_ada_w=m_ada_w, m_ada_b=m_ada_b, m_norm_mix_g=m_norm_mix_g, m_norm_ffn_g=m_norm_ffn_g, m_gdn_w_in=m_gdn_w_in, m_gdn_conv_w=m_gdn_conv_w, m_gdn_a_log=m_gdn_a_log, m_gdn_dt_bias=m_gdn_dt_bias, m_gdn_norm_g=m_gdn_norm_g, m_gdn_w_out=m_gdn_w_out, m_mla_w_in=m_mla_w_in, m_mla_q_norm_g=m_mla_q_norm_g, m_mla_kv_norm_g=m_mla_kv_norm_g, m_mla_w_uq=m_mla_w_uq, m_mla_w_ukv=m_mla_w_ukv, m_mla_w_out=m_mla_w_out, m_ffn_w_gate=m_ffn_w_gate, m_ffn_w_up=m_ffn_w_up, m_ffn_w_down=m_ffn_w_down, m_final_norm_g=m_final_norm_g, v_ada_w=v_ada_w, v_ada_b=v_ada_b, v_norm_mix_g=v_norm_mix_g, v_norm_ffn_g=v_norm_ffn_g, v_gdn_w_in=v_gdn_w_in, v_gdn_conv_w=v_gdn_conv_w, v_gdn_a_log=v_gdn_a_log, v_gdn_dt_bias=v_gdn_dt_bias, v_gdn_norm_g=v_gdn_norm_g, v_gdn_w_out=v_gdn_w_out, v_mla_w_in=v_mla_w_in, v_mla_q_norm_g=v_mla_q_norm_g, v_mla_kv_norm_g=v_mla_kv_norm_g, v_mla_w_uq=v_mla_w_uq, v_mla_w_ukv=v_mla_w_ukv, v_mla_w_out=v_mla_w_out, v_ffn_w_gate=v_ffn_w_gate, v_ffn_w_up=v_ffn_w_up, v_ffn_w_down=v_ffn_w_down, v_final_norm_g=v_final_norm_g)
    weights = {n: given[n] for n in TWIN_WEIGHTS}
    shared = {n: given[n] for n in SHARED_INPUTS}
    per_example = {n: given[n] for n in ['x', 'c', 'positions']}
    grad_fn = _jax.value_and_grad(_loss, argnums=(0, 1))

    def one_microbatch(ex, loss_target):
        ex = dict(ex)
        diff = ex.pop(TWIN_DIFF_INPUT)
        return grad_fn(weights, diff, {**shared, **ex}, loss_target)

    if N_MICROBATCH == 1:
        loss, (grad_w, grad_x) = one_microbatch(per_example, given["loss_target"])
    else:
        def body(carry, xs):
            loss_sum, grad_sum = carry
            l_k, (gw_k, gx_k) = one_microbatch(xs[0], xs[1])
            with _jax.named_scope("update"):
                return (loss_sum + l_k, _jax.tree.map(_jnp.add, grad_sum, gw_k)), gx_k

        init = (_jnp.zeros((), _jnp.float32), _jax.tree.map(_jnp.zeros_like, weights))
        (loss, grad_w), grad_x = _jax.lax.scan(body, init, (per_example, given["loss_target"]))
    with _jax.named_scope("update"):
        delta_w, new_m, new_v = {}, {}, {}
        for n in TWIN_WEIGHTS:
            delta_w[n], new_m[n], new_v[n] = _adamw(weights[n], grad_w[n], given["m_" + n], given["v_" + n])
    return (loss, grad_x, *[grad_w[n] for n in TWIN_WEIGHTS], *[delta_w[n] for n in TWIN_WEIGHTS],
            *[new_m[n] for n in TWIN_WEIGHTS], *[new_v[n] for n in TWIN_WEIGHTS])
```

```python
import functools
import math

import jax
import jax.numpy as jnp
from jax import lax
from jax.experimental import pallas as pl
from jax.experimental.pallas import tpu as pltpu

F32 = jnp.float32
BF16 = jnp.bfloat16
MXU_DTYPE = jnp.bfloat16

N_DEV = 8
D_MODEL = 1024
DEPTH = 4
GDN_HEADS = 8
GDN_HEAD_DIM = 128
GDN_KEY_DIM = GDN_HEADS * GDN_HEAD_DIM
GDN_CHUNK = 64
GDN_CONV = 4
GDN_MAIN = 4 * GDN_KEY_DIM
MLA_HEADS = 8
MLA_NOPE = 128
MLA_ROPE = 64
MLA_V = 128
MLA_Q_RANK = 384
MLA_KV_RANK = 256
MLA_IN = MLA_Q_RANK + MLA_KV_RANK + MLA_ROPE
MLA_QK = MLA_NOPE + MLA_ROPE
ROPE_THETA = 10000.0
D_FF = 2816
N_MOD = 6
EPS = 1e-6
LANES = 128
VMEM_LIMIT = 48 * 1024 * 1024

ADAM_LR = 0.001
ADAM_B1 = 0.9
ADAM_B2 = 0.999
ADAM_EPS = 1e-08
ADAM_WD = 0.01
ADAM_STEP = 10
ADAM_BC1 = 1.0 - ADAM_B1 ** ADAM_STEP
ADAM_BC2 = 1.0 - ADAM_B2 ** ADAM_STEP

NN = (((1,), (0,)), ((), ()))
NT = (((1,), (1,)), ((), ()))
TN = (((0,), (0,)), ((), ()))
NEG = -1e30


def _dotb(a, b, dims):
    return lax.dot_general(a.astype(MXU_DTYPE), b.astype(MXU_DTYPE), dims, preferred_element_type=F32)


def _dotf(a, b, dims):
    return lax.dot_general(a, b, dims, precision=lax.Precision.HIGHEST, preferred_element_type=F32)


def _params(*sem):
    return pltpu.CompilerParams(dimension_semantics=sem, vmem_limit_bytes=VMEM_LIMIT)


def _pick(n, pref, mult=LANES):
    best = None
    t = mult
    while t <= min(n, pref):
        if n % t == 0:
            best = t
        t += mult
    return best if best is not None else n


def _sigmoid(z):
    return 1.0 / (1.0 + jnp.exp(-z))


def _exchange(arrays, *, scatter, name):
    n = len(arrays)
    out_shape = tuple(
        jax.ShapeDtypeStruct(a.shape if scatter else (N_DEV,) + a.shape, a.dtype) for a in arrays)

    def body(*refs):
        ins, outs = refs[:n], refs[n:2 * n]
        send_sems, recv_sems, local_sems = refs[2 * n:]
        x, y, c = lax.axis_index("x"), lax.axis_index("y"), lax.axis_index("c")
        me = 4 * x + 2 * y + c
        copies = []
        for k in range(n):
            src_own = ins[k].at[me] if scatter else ins[k]
            own = pltpu.make_async_copy(src_own, outs[k].at[me], local_sems.at[k])
            own.start()
            copies.append(own)
        sends = []
        for p in range(1, N_DEV):
            px, py, pc = x ^ ((p >> 2) & 1), y ^ ((p >> 1) & 1), c ^ (p & 1)
            peer = 4 * px + 2 * py + pc
            for k in range(n):
                cp = pltpu.make_async_remote_copy(
                    src_ref=ins[k].at[peer] if scatter else ins[k],
                    dst_ref=outs[k].at[me],
                    send_sem=send_sems.at[k, p - 1],
                    recv_sem=recv_sems.at[k, p - 1],
                    device_id=(px, py, pc),
                    device_id_type=pl.DeviceIdType.MESH,
                )
                cp.start()
                sends.append((cp, k, peer, p))
        for cp, k, peer, p in sends:
            pltpu.make_async_remote_copy(
                src_ref=ins[k].at[peer] if scatter else ins[k],
                dst_ref=outs[k].at[peer],
                send_sem=send_sems.at[k, p - 1],
                recv_sem=recv_sems.at[k, p - 1],
                device_id=(x, y, c),
                device_id_type=pl.DeviceIdType.MESH,
            ).wait_recv()
        for cp, _, _, _ in sends:
            cp.wait_send()
        for own in copies:
            own.wait()

    any_spec = pl.BlockSpec(memory_space=pl.ANY)
    outs = pl.pallas_call(
        body,
        name=name,
        out_shape=out_shape,
        in_specs=[any_spec] * n,
        out_specs=tuple([any_spec] * n),
        scratch_shapes=[
            pltpu.SemaphoreType.DMA((n, N_DEV - 1)),
            pltpu.SemaphoreType.DMA((n, N_DEV - 1)),
            pltpu.SemaphoreType.DMA((n,)),
        ],
        compiler_params=pltpu.CompilerParams(has_side_effects=True),
    )(*arrays)
    return list(outs)


def _mm(a, b, *, mode, out_dtype, name, add=None, tm=512, tn=512):
    if mode == "nn":
        (m, kd), (_, nd) = a.shape, b.shape
    elif mode == "nt":
        (m, kd), (nd, _) = a.shape, b.shape
    else:
        (kd, m), (_, nd) = a.shape, b.shape
    tm = _pick(m, tm, LANES if mode == "tn" else 16)
    tn = _pick(nd, tn)
    dims = {"nn": NN, "nt": NT, "tn": TN}[mode]
    a_spec = pl.BlockSpec((kd, tm), lambda i, j: (0, i)) if mode == "tn" else pl.BlockSpec((tm, kd), lambda i, j: (i, 0))
    b_spec = pl.BlockSpec((tn, kd), lambda i, j: (j, 0)) if mode == "nt" else pl.BlockSpec((kd, tn), lambda i, j: (0, j))
    o_spec = pl.BlockSpec((tm, tn), lambda i, j: (i, j))
    has_add = add is not None

    def body(*refs):
        a_ref, b_ref = refs[0], refs[1]
        o_ref = refs[-1]
        acc = _dotb(a_ref[...], b_ref[...], dims)
        if has_add:
            acc = acc + refs[2][...].astype(F32)
        o_ref[...] = acc.astype(o_ref.dtype)

    ins = [a, b] + ([add] if has_add else [])
    specs = [a_spec, b_spec] + ([o_spec] if has_add else [])
    return pl.pallas_call(
        body, name=name, grid=(m // tm, nd // tn), in_specs=specs, out_specs=o_spec,
        out_shape=jax.ShapeDtypeStruct((m, nd), out_dtype),
        compiler_params=_params("parallel", "parallel"),
    )(*ins)


def _mm_resid(a, b, x, gate, *, name, tm=512, tn=512):
    m, kd = a.shape
    nd = b.shape[1]
    tm = _pick(m, tm, 16)
    tn = _pick(nd, tn)
    o_spec = pl.BlockSpec((tm, tn), lambda i, j: (i, j))

    def body(a_ref, b_ref, x_ref, g_ref, xo_ref, y_ref):
        y = _dotb(a_ref[...], b_ref[...], NN)
        y_ref[...] = y
        xo_ref[...] = x_ref[...] + g_ref[...] * y

    return pl.pallas_call(
        body, name=name, grid=(m // tm, nd // tn),
        in_specs=[pl.BlockSpec((tm, kd), lambda i, j: (i, 0)), pl.BlockSpec((kd, tn), lambda i, j: (0, j)),
                  o_spec, pl.BlockSpec((1, tn), lambda i, j: (0, j))],
        out_specs=(o_spec, o_spec),
        out_shape=(jax.ShapeDtypeStruct((m, nd), F32), jax.ShapeDtypeStruct((m, nd), F32)),
        compiler_params=_params("parallel", "parallel"),
    )(a, b, x, gate)


ROWS = 256


def _row_spec(width, rows=ROWS):
    return pl.BlockSpec((rows, width), lambda i: (i, 0))


def _const_spec(shape):
    return pl.BlockSpec(shape, lambda i: tuple(0 for _ in shape))


def _adaln_fwd(x, g, scale, shift, *, name):
    t, d = x.shape

    def body(x_ref, g_ref, sc_ref, sh_ref, h_ref):
        xv = x_ref[...]
        r = lax.rsqrt(jnp.mean(xv * xv, axis=-1, keepdims=True) + EPS)
        h_ref[...] = (xv * r * g_ref[...] * (1.0 + sc_ref[...]) + sh_ref[...]).astype(h_ref.dtype)

    return pl.pallas_call(
        body, name=name, grid=(t // ROWS,),
        in_specs=[_row_spec(d), _const_spec((1, d)), _const_spec((1, d)), _const_spec((1, d))],
        out_specs=_row_spec(d), out_shape=jax.ShapeDtypeStruct((t, d), BF16),
        compiler_params=_params("parallel"),
    )(x, g, scale, shift)


def _adaln_bwd(x, g, scale, shift, dh, dres, *, name):
    t, d = x.shape

    def body(x_ref, g_ref, sc_ref, sh_ref, dh_ref, dr_ref, dx_ref, st_ref):
        @pl.when(pl.program_id(0) == 0)
        def _():
            st_ref[...] = jnp.zeros_like(st_ref)

        xv = x_ref[...]
        dhv = dh_ref[...].astype(F32)
        gv = g_ref[...]
        r = lax.rsqrt(jnp.mean(xv * xv, axis=-1, keepdims=True) + EPS)
        xh = xv * r
        nv = xh * gv
        dn = dhv * (1.0 + sc_ref[...])
        dxh = dn * gv
        dx_ref[...] = dr_ref[...] + r * (dxh - xh * jnp.mean(dxh * xh, axis=-1, keepdims=True))
        st_ref[0:1, :] += jnp.sum(dn * xh, axis=0, keepdims=True)
        st_ref[1:2, :] += jnp.sum(dhv * nv, axis=0, keepdims=True)
        st_ref[2:3, :] += jnp.sum(dhv, axis=0, keepdims=True)

    return pl.pallas_call(
        body, name=name, grid=(t // ROWS,),
        in_specs=[_row_spec(d), _const_spec((1, d)), _const_spec((1, d)), _const_spec((1, d)),
                  _row_spec(d), _row_spec(d)],
        out_specs=(_row_spec(d), _const_spec((8, d))),
        out_shape=(jax.ShapeDtypeStruct((t, d), F32), jax.ShapeDtypeStruct((8, d), F32)),
        compiler_params=_params("arbitrary"),
    )(x, g, scale, shift, dh, dres)


def _gate_bwd(dxo, y, gate, *, name):
    t, d = dxo.shape

    def body(dx_ref, y_ref, g_ref, dy_ref, st_ref):
        @pl.when(pl.program_id(0) == 0)
        def _():
            st_ref[...] = jnp.zeros_like(st_ref)

        dxv = dx_ref[...]
        dy_ref[...] = (dxv * g_ref[...]).astype(dy_ref.dtype)
        st_ref[0:1, :] += jnp.sum(dxv * y_ref[...], axis=0, keepdims=True)

    return pl.pallas_call(
        body, name=name, grid=(t // ROWS,),
        in_specs=[_row_spec(d), _row_spec(d), _const_spec((1, d))],
        out_specs=(_row_spec(d), _const_spec((8, d))),
        out_shape=(jax.ShapeDtypeStruct((t, d), BF16), jax.ShapeDtypeStruct((8, d), F32)),
        compiler_params=_params("arbitrary"),
    )(dxo, y, gate)


def _loss_head(x, g, target, *, name):
    t, d = x.shape

    def body(x_ref, g_ref, t_ref, dx_ref, st_ref, ls_ref):
        @pl.when(pl.program_id(0) == 0)
        def _():
            st_ref[...] = jnp.zeros_like(st_ref)
            ls_ref[...] = jnp.zeros_like(ls_ref)

        xv = x_ref[...]
        gv = g_ref[...]
        r = lax.rsqrt(jnp.mean(xv * xv, axis=-1, keepdims=True) + EPS)
        xh = xv * r
        err = xh * gv - t_ref[...]
        ls_ref[...] += 0.5 * jnp.sum(jnp.mean(err * err, axis=-1, keepdims=True))
        dy = err * (1.0 / d)
        dxh = dy * gv
        dx_ref[...] = r * (dxh - xh * jnp.mean(dxh * xh, axis=-1, keepdims=True))
        st_ref[0:1, :] += jnp.sum(dy * xh, axis=0, keepdims=True)

    return pl.pallas_call(
        body, name=name, grid=(t // ROWS,),
        in_specs=[_row_spec(d), _const_spec((1, d)), _row_spec(d)],
        out_specs=(_row_spec(d), _const_spec((8, d)), _const_spec((8, LANES))),
        out_shape=(jax.ShapeDtypeStruct((t, d), F32), jax.ShapeDtypeStruct((8, d), F32),
                   jax.ShapeDtypeStruct((8, LANES), F32)),
        compiler_params=_params("arbitrary"),
    )(x, g, target)


FFN_BLOCK = D_FF // 2


def _gu_to_kernel_layout(wg, wu):
    parts = []
    for b in range(D_FF // FFN_BLOCK):
        sl = slice(b * FFN_BLOCK, (b + 1) * FFN_BLOCK)
        parts += [wg[..., sl], wu[..., sl]]
    return jnp.concatenate(parts, axis=-1)


def _gu_from_kernel_layout(w):
    nb = D_FF // FFN_BLOCK
    wg = jnp.concatenate([w[..., 2 * b * FFN_BLOCK:(2 * b + 1) * FFN_BLOCK] for b in range(nb)], axis=-1)
    wu = jnp.concatenate([w[..., (2 * b + 1) * FFN_BLOCK:(2 * b + 2) * FFN_BLOCK] for b in range(nb)], axis=-1)
    return wg, wu


def _swiglu_fwd(ab, *, name):
    t = ab.shape[0]
    tn = FFN_BLOCK

    def body(ab_ref, s_ref):
        a = ab_ref[:, 0:tn]
        s_ref[...] = (a * _sigmoid(a) * ab_ref[:, tn:2 * tn]).astype(s_ref.dtype)

    return pl.pallas_call(
        body, name=name, grid=(t // ROWS, D_FF // tn),
        in_specs=[pl.BlockSpec((ROWS, 2 * tn), lambda i, j: (i, j))],
        out_specs=pl.BlockSpec((ROWS, tn), lambda i, j: (i, j)),
        out_shape=jax.ShapeDtypeStruct((t, D_FF), BF16),
        compiler_params=_params("parallel", "parallel"),
    )(ab)


def _swiglu_bwd(ab, ds, *, name):
    t = ab.shape[0]
    tn = FFN_BLOCK

    def body(ab_ref, ds_ref, d_ref):
        a = ab_ref[:, 0:tn]
        dsv = ds_ref[...]
        sg = _sigmoid(a)
        d_ref[:, 0:tn] = (dsv * ab_ref[:, tn:2 * tn] * sg * (1.0 + a * (1.0 - sg))).astype(d_ref.dtype)
        d_ref[:, tn:2 * tn] = (dsv * a * sg).astype(d_ref.dtype)

    return pl.pallas_call(
        body, name=name, grid=(t // ROWS, D_FF // tn),
        in_specs=[pl.BlockSpec((ROWS, 2 * tn), lambda i, j: (i, j)), pl.BlockSpec((ROWS, tn), lambda i, j: (i, j))],
        out_specs=pl.BlockSpec((ROWS, 2 * tn), lambda i, j: (i, j)),
        out_shape=jax.ShapeDtypeStruct((t, 2 * D_FF), BF16),
        compiler_params=_params("parallel", "parallel"),
    )(ab, ds)


def _shift_rows(v, s, rows):
    if s == 0:
        return v
    return jnp.where(rows >= s, pltpu.roll(v, s, 0), 0.0)


def _unshift_rows(v, s, rows, t):
    if s == 0:
        return v
    return jnp.where(rows < t - s, pltpu.roll(v, t - s, 0), 0.0)


def _conv_silu(x, w, rows):
    z = w[GDN_CONV - 1:GDN_CONV, :] * x
    for j in range(GDN_CONV - 1):
        z = z + w[j:j + 1, :] * _shift_rows(x, GDN_CONV - 1 - j, rows)
    sg = _sigmoid(z)
    return z, sg, z * sg


def _gdn_prep_fwd(proj, conv_wt, *, name):
    t = proj.shape[0]
    nh = GDN_HEADS

    def body(x_ref, w_ref, y_ref):
        j = pl.program_id(0)
        rows = lax.broadcasted_iota(jnp.int32, (t, LANES), 0)
        _, _, s = _conv_silu(x_ref[...], w_ref[...], rows)
        rs = lax.rsqrt(jnp.sum(s * s, axis=-1, keepdims=True) + EPS)
        qscale = jnp.where(j < nh, GDN_HEAD_DIM ** -0.5, 1.0)
        y_ref[...] = jnp.where(j < 2 * nh, s * rs * qscale, s)

    return pl.pallas_call(
        body, name=name, grid=(3 * nh,),
        in_specs=[pl.BlockSpec((t, LANES), lambda j: (0, j)), pl.BlockSpec((GDN_CONV, LANES), lambda j: (0, j))],
        out_specs=pl.BlockSpec((t, LANES), lambda j: (0, j)),
        out_shape=jax.ShapeDtypeStruct((t, 3 * GDN_KEY_DIM), F32),
        compiler_params=_params("parallel"),
    )(proj, conv_wt)


def _gdn_prep_bwd(proj, conv_wt, dy, *, name):
    t = proj.shape[0]
    nh = GDN_HEADS

    def body(x_ref, w_ref, dy_ref, dx_ref, dw_ref):
        j = pl.program_id(0)
        rows = lax.broadcasted_iota(jnp.int32, (t, LANES), 0)
        x = x_ref[...]
        w = w_ref[...]
        z, sg, s = _conv_silu(x, w, rows)
        rs = lax.rsqrt(jnp.sum(s * s, axis=-1, keepdims=True) + EPS)
        qscale = jnp.where(j < nh, GDN_HEAD_DIM ** -0.5, 1.0)
        dyv = dy_ref[...]
        nv = s * rs
        de = dyv * qscale
        ds_qk = rs * (de - nv * jnp.sum(de * nv, axis=-1, keepdims=True))
        ds = jnp.where(j < 2 * nh, ds_qk, dyv)
        dz = ds * sg * (1.0 + z * (1.0 - sg))
        dx = w[GDN_CONV - 1:GDN_CONV, :] * dz
        dw_ref[GDN_CONV - 1:GDN_CONV, :] = jnp.sum(dz * x, axis=0, keepdims=True)
        for k in range(GDN_CONV - 1):
            sh = GDN_CONV - 1 - k
            dx = dx + w[k:k + 1, :] * _unshift_rows(dz, sh, rows, t)
            dw_ref[k:k + 1, :] = jnp.sum(dz * _shift_rows(x, sh, rows), axis=0, keepdims=True)
        dx_ref[...] = dx.astype(dx_ref.dtype)

    return pl.pallas_call(
        body, name=name, grid=(3 * nh,),
        in_specs=[pl.BlockSpec((t, LANES), lambda j: (0, j)), pl.BlockSpec((GDN_CONV, LANES), lambda j: (0, j)),
                  pl.BlockSpec((t, LANES), lambda j: (0, j))],
        out_specs=(pl.BlockSpec((t, LANES), lambda j: (0, j)), pl.BlockSpec((GDN_CONV, LANES), lambda j: (0, j))),
        out_shape=(jax.ShapeDtypeStruct((t, 3 * GDN_KEY_DIM), BF16),
                   jax.ShapeDtypeStruct((GDN_CONV, 3 * GDN_KEY_DIM), F32)),
        compiler_params=_params("parallel"),
    )(proj, conv_wt, dy)


def _softplus(z):
    return jnp.maximum(z, 0.0) + jnp.log(1.0 + jnp.exp(-jnp.abs(z)))


def _gdn_gate_fwd(ab, prm, *, name):
    t = ab.shape[0]

    def body(ab_ref, p_ref, o_ref):
        v = ab_ref[...]
        lane = lax.broadcasted_iota(jnp.int32, v.shape, 1)
        g = -jnp.exp(p_ref[0:1, :]) * _softplus(v + p_ref[1:2, :])
        o_ref[...] = jnp.where(lane < GDN_HEADS, g, jnp.where(lane < 2 * GDN_HEADS, _sigmoid(v), 0.0))

    return pl.pallas_call(
        body, name=name, grid=(t // ROWS,),
        in_specs=[_row_spec(LANES), _const_spec((8, LANES))], out_specs=_row_spec(LANES),
        out_shape=jax.ShapeDtypeStruct((t, LANES), F32), compiler_params=_params("parallel"),
    )(ab, prm)


def _gdn_gate_bwd(ab, prm, dgb, *, name):
    t = ab.shape[0]

    def body(ab_ref, p_ref, d_ref, o_ref, st_ref):
        @pl.when(pl.program_id(0) == 0)
        def _():
            st_ref[...] = jnp.zeros_like(st_ref)

        v = ab_ref[...]
        dv = d_ref[...]
        lane = lax.broadcasted_iota(jnp.int32, v.shape, 1)
        is_a = lane < GDN_HEADS
        is_b = jnp.logical_and(lane >= GDN_HEADS, lane < 2 * GDN_HEADS)
        a_exp = jnp.exp(p_ref[0:1, :])
        zz = v + p_ref[1:2, :]
        g = -a_exp * _softplus(zz)
        da = dv * (-a_exp) * _sigmoid(zz)
        beta = _sigmoid(v)
        db = dv * beta * (1.0 - beta)
        o_ref[...] = jnp.where(is_a, da, jnp.where(is_b, db, 0.0)).astype(o_ref.dtype)
        st_ref[0:1, :] += jnp.sum(jnp.where(is_a, dv * g, 0.0), axis=0, keepdims=True)
        st_ref[1:2, :] += jnp.sum(jnp.where(is_a, da, 0.0), axis=0, keepdims=True)

    return pl.pallas_call(
        body, name=name, grid=(t // ROWS,),
        in_specs=[_row_spec(LANES), _const_spec((8, LANES)), _row_spec(LANES)],
        out_specs=(_row_spec(LANES), _const_spec((8, LANES))),
        out_shape=(jax.ShapeDtypeStruct((t, LANES), BF16), jax.ShapeDtypeStruct((8, LANES), F32)),
        compiler_params=_params("arbitrary"),
    )(ab, prm, dgb)


def _gdn_local(q, k, v, gb, bb):
    cs = q.shape[0]
    r = lax.broadcasted_iota(jnp.int32, (cs, cs), 0)
    c = lax.broadcasted_iota(jnp.int32, (cs, cs), 1)
    tril, strict, eye = r >= c, r > c, r == c
    g_colb = gb[:, :cs]
    g_row = jnp.sum(jnp.where(eye, g_colb, 0.0), axis=0, keepdims=True)
    gc_col = jnp.sum(jnp.where(tril, g_row, 0.0), axis=1, keepdims=True)
    gc_row = jnp.sum(jnp.where(r <= c, g_colb, 0.0), axis=0, keepdims=True)
    decay = jnp.exp(jnp.where(tril, gc_col - gc_row, NEG))
    gamma = jnp.exp(gc_col)
    gcl = gc_col[cs - 1:cs, :]
    gl = jnp.exp(gcl)
    kdec = jnp.exp(gcl - gc_col)
    kb = k * bb
    lmat = jnp.where(strict, _dotb(kb, k, NT) * decay, 0.0)
    xm = -lmat
    tinv = jnp.where(eye, 1.0, 0.0) + xm
    for _ in range(int(math.log2(cs)) - 1):
        xm = _dotf(xm, xm, NN)
        tinv = tinv + _dotf(tinv, xm, NN)
    vb = v * bb
    kg = kb * gamma
    u = _dotf(tinv, vb, NN)
    w = _dotf(tinv, kg, NN)
    pmat = jnp.where(tril, _dotb(q, k, NT) * decay, 0.0)
    return dict(tril=tril, strict=strict, eye=eye, r=r, c=c, decay=decay, gamma=gamma, gl=gl, kdec=kdec,
                kb=kb, lmat=lmat, tinv=tinv, vb=vb, kg=kg, u=u, w=w, pmat=pmat, qd=q * gamma, kd=k * kdec)


def _gdn_chunk_fwd(qkv, gbc, bbc, *, name):
    t = qkv.shape[0]
    nh, cs, hd = GDN_HEADS, GDN_CHUNK, GDN_HEAD_DIM
    nc = t // cs

    def body(q_ref, k_ref, v_ref, g_ref, b_ref, o_ref, st_ref, s_ref):
        @pl.when(pl.program_id(1) == 0)
        def _():
            s_ref[...] = jnp.zeros_like(s_ref)

        lo = _gdn_local(q_ref[...], k_ref[...], v_ref[...], g_ref[0], b_ref[0])
        s = s_ref[...]
        st_ref[0, 0] = s
        vn = lo["u"] - _dotb(lo["w"], s, NN)
        o_ref[...] = _dotb(lo["qd"], s, NN) + _dotb(lo["pmat"], vn, NN)
        s_ref[...] = s * lo["gl"] + _dotb(lo["kd"], vn, TN)

    gspec = pl.BlockSpec((1, cs, LANES), lambda h, n: (h, n, 0))
    return pl.pallas_call(
        body, name=name, grid=(nh, nc),
        in_specs=[pl.BlockSpec((cs, hd), lambda h, n: (n, h)), pl.BlockSpec((cs, hd), lambda h, n: (n, nh + h)),
                  pl.BlockSpec((cs, hd), lambda h, n: (n, 2 * nh + h)), gspec, gspec],
        out_specs=(pl.BlockSpec((cs, hd), lambda h, n: (n, h)),
                   pl.BlockSpec((1, 1, hd, hd), lambda h, n: (h, n, 0, 0))),
        out_shape=(jax.ShapeDtypeStruct((t, nh * hd), F32), jax.ShapeDtypeStruct((nh, nc, hd, hd), F32)),
        scratch_shapes=[pltpu.VMEM((hd, hd), F32)],
        compiler_params=_params("parallel", "arbitrary"),
    )(qkv, qkv, qkv, gbc, bbc)


def _gdn_chunk_bwd(qkv, gbc, bbc, states, do, *, name):
    t = qkv.shape[0]
    nh, cs, hd = GDN_HEADS, GDN_CHUNK, GDN_HEAD_DIM
    nc = t // cs

    def body(q_ref, k_ref, v_ref, g_ref, b_ref, st_ref, do_ref, dq_ref, dk_ref, dv_ref, dg_ref, db_ref, ds_ref):
        @pl.when(pl.program_id(1) == 0)
        def _():
            ds_ref[...] = jnp.zeros_like(ds_ref)

        q, k, v, bb = q_ref[...], k_ref[...], v_ref[...], b_ref[0]
        lo = _gdn_local(q, k, v, g_ref[0], bb)
        tril, strict, eye, r, c = lo["tril"], lo["strict"], lo["eye"], lo["r"], lo["c"]
        decay, gamma, gl, kdec = lo["decay"], lo["gamma"], lo["gl"], lo["kdec"]
        kb, tinv, w, pmat = lo["kb"], lo["tinv"], lo["w"], lo["pmat"]
        s = st_ref[0, 0]
        dsn = ds_ref[...]
        dov = do_ref[...]
        vn = lo["u"] - _dotb(w, s, NN)
        dvn = _dotb(pmat, dov, TN) + _dotb(lo["kd"], dsn, NN)
        dp = jnp.where(tril, _dotb(dov, vn, NT), 0.0)
        dqd = _dotb(dov, s, NT)
        dkd = _dotb(vn, dsn, NT)
        dgl = jnp.sum(jnp.sum(dsn * s, axis=1, keepdims=True), axis=0, keepdims=True)
        dw = -_dotb(dvn, s, NT)
        ds_ref[...] = gl * dsn + _dotb(lo["qd"], dov, TN) - _dotb(w, dvn, TN)
        dvb = _dotf(tinv, dvn, TN)
        dkg = _dotf(tinv, dw, TN)
        dt = _dotf(dvn, lo["vb"], NT) + _dotf(dw, lo["kg"], NT)
        dl = jnp.where(strict, -_dotf(_dotf(tinv, dt, TN), tinv, NT), 0.0)
        dkk = dl * decay
        dqk = dp * decay
        dkb = _dotb(dkk, k, NN) + dkg * gamma
        dk = _dotb(dkk, kb, TN) + _dotb(dqk, q, TN) + dkd * kdec + dkb * bb
        dq_ref[...] = _dotb(dqk, k, NN) + dqd * gamma
        dk_ref[...] = dk
        dv_ref[...] = dvb * bb
        db_ref[0] = jnp.broadcast_to(
            jnp.sum(dvb * v, axis=-1, keepdims=True) + jnp.sum(dkb * k, axis=-1, keepdims=True), (cs, LANES))
        e = dl * lo["lmat"] + dp * pmat
        e_col = jnp.sum(e, axis=0, keepdims=True)
        dgc = jnp.sum(e, axis=1, keepdims=True) - jnp.sum(jnp.where(eye, e_col, 0.0), axis=1, keepdims=True)
        dgamma = jnp.sum(dqd * q, axis=-1, keepdims=True) + jnp.sum(dkg * kb, axis=-1, keepdims=True)
        rk = jnp.sum(dkd * k, axis=-1, keepdims=True) * kdec
        dgcl = jnp.sum(rk, axis=0, keepdims=True) + dgl * gl
        rowi = lax.broadcasted_iota(jnp.int32, (cs, 1), 0)
        dgc = dgc + dgamma * gamma - rk + jnp.where(rowi == cs - 1, dgcl, 0.0)
        dgc_row = jnp.sum(jnp.where(eye, dgc, 0.0), axis=0, keepdims=True)
        dg = jnp.sum(jnp.where(c >= r, dgc_row, 0.0), axis=1, keepdims=True)
        dg_ref[0] = jnp.broadcast_to(dg, (cs, LANES))

    gspec = pl.BlockSpec((1, cs, LANES), lambda h, n: (h, nc - 1 - n, 0))
    col = lambda off: pl.BlockSpec((cs, hd), lambda h, n: (nc - 1 - n, off + h))
    return pl.pallas_call(
        body, name=name, grid=(nh, nc),
        in_specs=[col(0), col(nh), col(2 * nh), gspec, gspec,
                  pl.BlockSpec((1, 1, hd, hd), lambda h, n: (h, nc - 1 - n, 0, 0)), col(0)],
        out_specs=(col(0), col(0), col(0), gspec, gspec),
        out_shape=(jax.ShapeDtypeStruct((t, nh * hd), F32),) * 3
        + (jax.ShapeDtypeStruct((nh, t, LANES), F32),) * 2,
        scratch_shapes=[pltpu.VMEM((hd, hd), F32)],
        compiler_params=_params("parallel", "arbitrary"),
    )(qkv, qkv, qkv, gbc, bbc, states, do)


def _gdn_onorm_fwd(o, proj, norm_g, *, name):
    t = o.shape[0]
    w = GDN_KEY_DIM
    goff = 3 * GDN_KEY_DIM // w

    def body(o_ref, gp_ref, g_ref, y_ref):
        gv = g_ref[...]
        for h in range(GDN_HEADS):
            sl = slice(h * GDN_HEAD_DIM, (h + 1) * GDN_HEAD_DIM)
            oh = o_ref[:, sl]
            gp = gp_ref[:, sl]
            r = lax.rsqrt(jnp.mean(oh * oh, axis=-1, keepdims=True) + EPS)
            y_ref[:, sl] = (oh * r * gv * gp * _sigmoid(gp)).astype(y_ref.dtype)

    return pl.pallas_call(
        body, name=name, grid=(t // ROWS,),
        in_specs=[_row_spec(w), pl.BlockSpec((ROWS, w), lambda i: (i, goff)), _const_spec((1, GDN_HEAD_DIM))],
        out_specs=_row_spec(w), out_shape=jax.ShapeDtypeStruct((t, w), BF16),
        compiler_params=_params("parallel"),
    )(o, proj, norm_g)


def _gdn_onorm_bwd(o, proj, norm_g, dy, *, name):
    t = o.shape[0]
    w = GDN_KEY_DIM
    goff = 3 * GDN_KEY_DIM // w

    def body(o_ref, gp_ref, g_ref, dy_ref, do_ref, dgp_ref, st_ref):
        @pl.when(pl.program_id(0) == 0)
        def _():
            st_ref[...] = jnp.zeros_like(st_ref)

        gv = g_ref[...]
        acc = jnp.zeros((1, GDN_HEAD_DIM), F32)
        for h in range(GDN_HEADS):
            sl = slice(h * GDN_HEAD_DIM, (h + 1) * GDN_HEAD_DIM)
            oh = o_ref[:, sl]
            gp = gp_ref[:, sl]
            dyv = dy_ref[:, sl].astype(F32)
            r = lax.rsqrt(jnp.mean(oh * oh, axis=-1, keepdims=True) + EPS)
            xh = oh * r
            sg = _sigmoid(gp)
            dn = dyv * gp * sg
            dgp_ref[:, sl] = (dyv * xh * gv * sg * (1.0 + gp * (1.0 - sg))).astype(dgp_ref.dtype)
            acc = acc + jnp.sum(dn * xh, axis=0, keepdims=True)
            dxh = dn * gv
            do_ref[:, sl] = r * (dxh - xh * jnp.mean(dxh * xh, axis=-1, keepdims=True))
        st_ref[0:1, :] += acc

    return pl.pallas_call(
        body, name=name, grid=(t // ROWS,),
        in_specs=[_row_spec(w), pl.BlockSpec((ROWS, w), lambda i: (i, goff)), _const_spec((1, GDN_HEAD_DIM)),
                  _row_spec(w)],
        out_specs=(_row_spec(w), _row_spec(w), _const_spec((8, GDN_HEAD_DIM))),
        out_shape=(jax.ShapeDtypeStruct((t, w), F32), jax.ShapeDtypeStruct((t, w), BF16),
                   jax.ShapeDtypeStruct((8, GDN_HEAD_DIM), F32)),
        compiler_params=_params("arbitrary"),
    )(o, proj, norm_g, dy)


def _mla_prep_fwd(proj, qg, kvg, *, name):
    t = proj.shape[0]
    q1, k1 = MLA_Q_RANK, MLA_Q_RANK + MLA_KV_RANK

    def body(p_ref, qg_ref, kg_ref, cq_ref, ck_ref):
        cq = p_ref[:, 0:q1]
        ck = p_ref[:, q1:k1]
        cq_ref[...] = (cq * lax.rsqrt(jnp.mean(cq * cq, axis=-1, keepdims=True) + EPS) * qg_ref[...]).astype(BF16)
        ck_ref[...] = (ck * lax.rsqrt(jnp.mean(ck * ck, axis=-1, keepdims=True) + EPS) * kg_ref[...]).astype(BF16)

    return pl.pallas_call(
        body, name=name, grid=(t // ROWS,),
        in_specs=[_row_spec(MLA_IN), _const_spec((1, MLA_Q_RANK)), _const_spec((1, MLA_KV_RANK))],
        out_specs=(_row_spec(MLA_Q_RANK), _row_spec(MLA_KV_RANK)),
        out_shape=(jax.ShapeDtypeStruct((t, MLA_Q_RANK), BF16), jax.ShapeDtypeStruct((t, MLA_KV_RANK), BF16)),
        compiler_params=_params("parallel"),
    )(proj, qg, kvg)


def _mla_prep_bwd(proj, qg, kvg, dcq, dck, dkr, *, name):
    t = proj.shape[0]
    q1, k1 = MLA_Q_RANK, MLA_Q_RANK + MLA_KV_RANK

    def body(p_ref, qg_ref, kg_ref, dq_ref, dk_ref, dr_ref, dp_ref, st_ref):
        @pl.when(pl.program_id(0) == 0)
        def _():
            st_ref[...] = jnp.zeros_like(st_ref)

        for lo, hi, g_ref, d_ref in ((0, q1, qg_ref, dq_ref), (q1, k1, kg_ref, dk_ref)):
            xv = p_ref[:, lo:hi]
            dn = d_ref[...]
            r = lax.rsqrt(jnp.mean(xv * xv, axis=-1, keepdims=True) + EPS)
            xh = xv * r
            dxh = dn * g_ref[...]
            dp_ref[:, lo:hi] = (r * (dxh - xh * jnp.mean(dxh * xh, axis=-1, keepdims=True))).astype(dp_ref.dtype)
            st_ref[0:1, lo:hi] += jnp.sum(dn * xh, axis=0, keepdims=True)
        dp_ref[:, k1:MLA_IN] = dr_ref[...].astype(dp_ref.dtype)

    return pl.pallas_call(
        body, name=name, grid=(t // ROWS,),
        in_specs=[_row_spec(MLA_IN), _const_spec((1, MLA_Q_RANK)), _const_spec((1, MLA_KV_RANK)),
                  _row_spec(MLA_Q_RANK), _row_spec(MLA_KV_RANK), _row_spec(MLA_ROPE)],
        out_specs=(_row_spec(MLA_IN), _const_spec((8, MLA_IN))),
        out_shape=(jax.ShapeDtypeStruct((t, MLA_IN), BF16), jax.ShapeDtypeStruct((8, MLA_IN), F32)),
        compiler_params=_params("arbitrary"),
    )(proj, qg, kvg, dcq, dck, dkr)


def _rope(xr, cos_t, sin_t, *, name):
    t, w = xr.shape
    ns = w // LANES

    def body(x_ref, c_ref, s_ref, o_ref):
        cv, sv = c_ref[...], s_ref[...]
        lane = lax.broadcasted_iota(jnp.int32, (ROWS, LANES), 1)
        first = (lane % MLA_ROPE) < (MLA_ROPE // 2)
        for i in range(ns):
            sl = slice(i * LANES, (i + 1) * LANES)
            xv = x_ref[:, sl]
            sw = jnp.where(first, pltpu.roll(xv, LANES - MLA_ROPE // 2, 1), pltpu.roll(xv, MLA_ROPE // 2, 1))
            o_ref[:, sl] = xv * cv + sw * sv

    return pl.pallas_call(
        body, name=name, grid=(t // ROWS,),
        in_specs=[_row_spec(w), _row_spec(LANES), _row_spec(LANES)], out_specs=_row_spec(w),
        out_shape=jax.ShapeDtypeStruct((t, w), F32), compiler_params=_params("parallel"),
    )(xr, cos_t, sin_t)


def _rope_bwd(dr, cos_t, sin_t, *, name):
    t, w = dr.shape
    ns = w // LANES

    def body(d_ref, c_ref, s_ref, o_ref):
        cv, sv = c_ref[...], s_ref[...]
        lane = lax.broadcasted_iota(jnp.int32, (ROWS, LANES), 1)
        first = (lane % MLA_ROPE) < (MLA_ROPE // 2)
        for i in range(ns):
            sl = slice(i * LANES, (i + 1) * LANES)
            dv = d_ref[:, sl]
            ds = dv * sv
            sw = jnp.where(first, pltpu.roll(ds, LANES - MLA_ROPE // 2, 1), pltpu.roll(ds, MLA_ROPE // 2, 1))
            o_ref[:, sl] = dv * cv + sw

    return pl.pallas_call(
        body, name=name, grid=(t // ROWS,),
        in_specs=[_row_spec(w), _row_spec(LANES), _row_spec(LANES)], out_specs=_row_spec(w),
        out_shape=jax.ShapeDtypeStruct((t, w), F32), compiler_params=_params("parallel"),
    )(dr, cos_t, sin_t)


ATT_BLOCK = 256
ATT_SCALE = MLA_QK ** -0.5


def _causal_mask(i, j, blk):
    rows = i * blk + lax.broadcasted_iota(jnp.int32, (blk, blk), 0)
    cols = j * blk + lax.broadcasted_iota(jnp.int32, (blk, blk), 1)
    return cols <= rows


def _attn_fwd(q, k, v, *, name):
    nh, t, dk = q.shape
    dv = v.shape[-1]
    blk = min(ATT_BLOCK, t)

    def body(q_ref, k_ref, v_ref, o_ref, l_ref):
        i = pl.program_id(1)
        qv = q_ref[0]

        def step(j, carry):
            m, l, acc = carry
            off = pl.multiple_of(j * blk, blk)
            s = _dotb(qv, k_ref[0, pl.ds(off, blk), :], NT) * ATT_SCALE
            s = jnp.where(_causal_mask(i, j, blk), s, NEG)
            m_new = jnp.maximum(m, jnp.max(s, axis=-1, keepdims=True))
            p = jnp.exp(s - m_new)
            alpha = jnp.exp(m - m_new)
            l = alpha * l + jnp.sum(p, axis=-1, keepdims=True)
            acc = alpha * acc + _dotb(p, v_ref[0, pl.ds(off, blk), :], NN)
            return m_new, l, acc

        init = (jnp.full((blk, 1), NEG, F32), jnp.zeros((blk, 1), F32), jnp.zeros((blk, dv), F32))
        m, l, acc = lax.fori_loop(0, i + 1, step, init)
        o_ref[0] = acc / l
        l_ref[0] = jnp.broadcast_to(m + jnp.log(l), (blk, LANES))

    return pl.pallas_call(
        body, name=name, grid=(nh, t // blk),
        in_specs=[pl.BlockSpec((1, blk, dk), lambda h, i: (h, i, 0)), pl.BlockSpec((1, t, dk), lambda h, i: (h, 0, 0)),
                  pl.BlockSpec((1, t, dv), lambda h, i: (h, 0, 0))],
        out_specs=(pl.BlockSpec((1, blk, dv), lambda h, i: (h, i, 0)),
                   pl.BlockSpec((1, blk, LANES), lambda h, i: (h, i, 0))),
        out_shape=(jax.ShapeDtypeStruct((nh, t, dv), F32), jax.ShapeDtypeStruct((nh, t, LANES), F32)),
        compiler_params=_params("parallel", "parallel"),
    )(q, k, v)


def _attn_bwd(q, k, v, o, lse, do, *, name):
    nh, t, dk = q.shape
    dv = v.shape[-1]
    blk = min(ATT_BLOCK, t)
    nb = t // blk

    def body(q_ref, k_ref, v_ref, o_ref, l_ref, do_ref, dq_ref, dk_ref, dv_ref):
        j = pl.program_id(1)

        @pl.when(j == 0)
        def _():
            dq_ref[...] = jnp.zeros_like(dq_ref)

        kv, vv = k_ref[0], v_ref[0]

        def step(i, carry):
            dk_acc, dv_acc = carry
            off = pl.multiple_of(i * blk, blk)
            rows = pl.ds(off, blk)
            qv = q_ref[0, rows, :]
            dov = do_ref[0, rows, :]
            s = _dotb(qv, kv, NT) * ATT_SCALE
            s = jnp.where(_causal_mask(i, j, blk), s, NEG)
            p = jnp.exp(s - l_ref[0, rows, :][:, 0:1])
            dv_acc = dv_acc + _dotb(p, dov, TN)
            dp = _dotb(dov, vv, NT)
            delta = jnp.sum(dov * o_ref[0, rows, :], axis=-1, keepdims=True)
            ds = p * (dp - delta) * ATT_SCALE
            dk_acc = dk_acc + _dotb(ds, qv, TN)
            dq_ref[0, rows, :] += _dotb(ds, kv, NN)
            return dk_acc, dv_acc

        dk_acc, dv_acc = lax.fori_loop(j, nb, step, (jnp.zeros((blk, dk), F32), jnp.zeros((blk, dv), F32)))
        dk_ref[0] = dk_acc
        dv_ref[0] = dv_acc

    full = lambda w: pl.BlockSpec((1, t, w), lambda h, j: (h, 0, 0))
    part = lambda w: pl.BlockSpec((1, blk, w), lambda h, j: (h, j, 0))
    return pl.pallas_call(
        body, name=name, grid=(nh, nb),
        in_specs=[full(dk), part(dk), part(dv), full(dv), full(LANES), full(dv)],
        out_specs=(full(dk), part(dk), part(dv)),
        out_shape=(jax.ShapeDtypeStruct((nh, t, dk), F32), jax.ShapeDtypeStruct((nh, t, dk), F32),
                   jax.ShapeDtypeStruct((nh, t, dv), F32)),
        compiler_params=_params("parallel", "arbitrary"),
    )(q, k, v, o, lse, do)


def _ada_mod(c_all, ada_w, ada_b_cols, *, name):
    nl, d, wc = ada_w.shape

    def body(c_ref, w_ref, b_ref, o_ref):
        cv = c_ref[...]
        o_ref[0] = _dotb(cv * _sigmoid(cv), w_ref[0], NN) + b_ref[0]

    return pl.pallas_call(
        body, name=name, grid=(nl,),
        in_specs=[_const_spec((N_DEV, d)), pl.BlockSpec((1, d, wc), lambda l: (l, 0, 0)),
                  pl.BlockSpec((1, 1, wc), lambda l: (l, 0, 0))],
        out_specs=pl.BlockSpec((1, N_DEV, wc), lambda l: (l, 0, 0)),
        out_shape=jax.ShapeDtypeStruct((nl, N_DEV, wc), F32), compiler_params=_params("parallel"),
    )(c_all, ada_w, ada_b_cols)


def _adam_math(g, w, m, v):
    m2 = ADAM_B1 * m + (1.0 - ADAM_B1) * g
    v2 = ADAM_B2 * v + (1.0 - ADAM_B2) * (g * g)
    delta = -ADAM_LR * ((m2 / ADAM_BC1) / (jnp.sqrt(v2 / ADAM_BC2) + ADAM_EPS) + ADAM_WD * w)
    return delta, m2, v2


def _ada_grad_adamw(c_all, dmod_cols, w, m, v, *, name):
    nl, d, wc = w.shape
    tr = 256

    def body(c_ref, dm_ref, w_ref, m_ref, v_ref, g_ref, d_ref, m2_ref, v2_ref):
        cv = c_ref[...]
        g = _dotf(cv * _sigmoid(cv), dm_ref[0], TN)
        delta, m2, v2 = _adam_math(g, w_ref[0], m_ref[0], v_ref[0])
        g_ref[0], d_ref[0], m2_ref[0], v2_ref[0] = g, delta, m2, v2

    blk = pl.BlockSpec((1, tr, wc), lambda l, i: (l, i, 0))
    return pl.pallas_call(
        body, name=name, grid=(nl, d // tr),
        in_specs=[pl.BlockSpec((N_DEV, tr), lambda l, i: (0, i)), pl.BlockSpec((1, N_DEV, wc), lambda l, i: (l, 0, 0)),
                  blk, blk, blk],
        out_specs=(blk,) * 4, out_shape=(jax.ShapeDtypeStruct(w.shape, F32),) * 4,
        compiler_params=_params("parallel", "parallel"),
    )(c_all, dmod_cols, w, m, v)


def _adamw(parts, w, m, v, *, name):
    ns, nl, r, c = parts.shape
    tr = r if r <= 512 else _pick(r, 256, 16)

    def body(p_ref, w_ref, m_ref, v_ref, g_ref, d_ref, m2_ref, v2_ref):
        g = p_ref[0, 0].astype(F32)
        for s in range(1, ns):
            g = g + p_ref[s, 0].astype(F32)
        delta, m2, v2 = _adam_math(g, w_ref[0], m_ref[0], v_ref[0])
        g_ref[0], d_ref[0], m2_ref[0], v2_ref[0] = g, delta, m2, v2

    blk = pl.BlockSpec((1, tr, c), lambda l, i: (l, i, 0))
    return pl.pallas_call(
        body, name=name, grid=(nl, r // tr),
        in_specs=[pl.BlockSpec((ns, 1, tr, c), lambda l, i: (0, l, i, 0)), blk, blk, blk],
        out_specs=(blk,) * 4, out_shape=(jax.ShapeDtypeStruct(w.shape, F32),) * 4,
        compiler_params=_params("parallel", "parallel"),
    )(parts, w, m, v)


def _sum_parts(parts, *, name):
    ns, r, c = parts.shape

    def body(p_ref, o_ref):
        acc = p_ref[0]
        for s in range(1, ns):
            acc = acc + p_ref[s]
        o_ref[...] = acc

    return pl.pallas_call(
        body, name=name, out_shape=jax.ShapeDtypeStruct((r, c), F32),
        in_specs=[pl.BlockSpec(memory_space=pltpu.VMEM)], out_specs=pl.BlockSpec(memory_space=pltpu.VMEM),
    )(parts)


def _pack(arrs):
    flat = jnp.concatenate([a.reshape(-1).astype(F32) for a in arrs])
    pad = (-flat.shape[0]) % (8 * LANES)
    return jnp.pad(flat, (0, pad)).reshape(-1, LANES)


def _unpack(packed, shapes, lead=()):
    flat = packed.reshape(lead + (-1,))
    out, off = [], 0
    for s in shapes:
        n = math.prod(s)
        out.append(flat[..., off:off + n].reshape(lead + tuple(s)))
        off += n
    return out


def _gather_cols(g):
    _, nl, r, cs = g.shape
    return jnp.transpose(g, (1, 2, 0, 3)).reshape(nl, r, N_DEV * cs)


def _gather_rows(g):
    _, nl, rs, c = g.shape
    return jnp.transpose(g, (1, 0, 2, 3)).reshape(nl, N_DEV * rs, c)


def _scatter_cols(full):
    nl, r, c = full.shape
    return jnp.transpose(full.reshape(nl, r, N_DEV, c // N_DEV), (2, 0, 1, 3))


def _scatter_rows(full):
    nl, r, c = full.shape
    return jnp.transpose(full.reshape(nl, N_DEV, r // N_DEV, c), (1, 0, 2, 3))


def _row(v):
    return v.reshape(1, -1)


def _local_step(x, target, mod, cos_t, sin_t, wts):
    t = x.shape[0]
    saved = []
    gdn_keep = {}
    for layer in range(DEPTH):
        j = layer // 2
        tag = f"l{layer}"
        shift_m, scale_m, gate_m, shift_f, scale_f, gate_f = [_row(mod[layer, i]) for i in range(N_MOD)]
        rec = {"x0": x}
        h = _adaln_fwd(x, _row(wts["norm_mix_g"][layer]), scale_m, shift_m, name=f"adaln_mix_{tag}")
        rec["h"] = h
        if layer % 2 == 0:
            proj = _mm(h, wts["gdn_w_main"][j], mode="nn", out_dtype=F32, name=f"gdn_in_{tag}")
            ab = _mm(h, wts["gdn_w_ab"][j], mode="nn", out_dtype=F32, name=f"gdn_in_ab_{tag}")
            qkv = _gdn_prep_fwd(proj, wts["gdn_conv_wt"][j], name=f"gdn_prep_{tag}")
            gbeta = _gdn_gate_fwd(ab, wts["gdn_gate_prm"][j], name=f"gdn_gate_{tag}")
            gbc = jnp.broadcast_to(jnp.transpose(gbeta[:, 0:GDN_HEADS])[:, :, None], (GDN_HEADS, t, LANES))
            bbc = jnp.broadcast_to(jnp.transpose(gbeta[:, GDN_HEADS:2 * GDN_HEADS])[:, :, None],
                                   (GDN_HEADS, t, LANES))
            o, states = _gdn_chunk_fwd(qkv, gbc, bbc, name=f"gdn_chunk_{tag}")
            og = _gdn_onorm_fwd(o, proj, _row(wts["gdn_norm_g"][j]), name=f"gdn_onorm_{tag}")
            x, y = _mm_resid(og, wts["gdn_w_out"][j], x, gate_m, name=f"gdn_out_{tag}")
            rec.update(proj=proj, ab=ab, qkv=qkv, gbc=gbc, bbc=bbc, states=states, o=o, og=og, y=y)
        else:
            proj = _mm(h, wts["mla_w_in"][j], mode="nn", out_dtype=F32, name=f"mla_in_{tag}")
            cq, ck = _mla_prep_fwd(proj, _row(wts["mla_q_norm_g"][j]), _row(wts["mla_kv_norm_g"][j]),
                                   name=f"mla_prep_{tag}")
            qf = _mm(cq, wts["mla_w_uq"][j], mode="nn", out_dtype=F32, name=f"mla_uq_{tag}")
            kvf = _mm(ck, wts["mla_w_ukv"][j], mode="nn", out_dtype=F32, name=f"mla_ukv_{tag}")
            nrope = MLA_HEADS * MLA_ROPE
            krp = jnp.pad(proj[:, MLA_Q_RANK + MLA_KV_RANK:], ((0, 0), (0, LANES - MLA_ROPE)))
            roped = _rope(jnp.concatenate([qf[:, MLA_HEADS * MLA_NOPE:], krp], axis=1), cos_t, sin_t,
                          name=f"rope_{tag}")
            q_nope = qf[:, :MLA_HEADS * MLA_NOPE].reshape(t, MLA_HEADS, MLA_NOPE)
            q_rope = roped[:, :nrope].reshape(t, MLA_HEADS, MLA_ROPE)
            k_rope = jnp.broadcast_to(roped[:, None, nrope:nrope + MLA_ROPE], (t, MLA_HEADS, MLA_ROPE))
            kv3 = kvf.reshape(t, MLA_HEADS, MLA_NOPE + MLA_V)
            qc = jnp.transpose(jnp.concatenate([q_nope, q_rope], axis=-1), (1, 0, 2)).astype(BF16)
            kc = jnp.transpose(jnp.concatenate([kv3[..., :MLA_NOPE], k_rope], axis=-1), (1, 0, 2)).astype(BF16)
            vc = jnp.transpose(kv3[..., MLA_NOPE:], (1, 0, 2)).astype(BF16)
            oh, lse = _attn_fwd(qc, kc, vc, name=f"attn_{tag}")
            oc = jnp.transpose(oh, (1, 0, 2)).reshape(t, MLA_HEADS * MLA_V).astype(BF16)
            x, y = _mm_resid(oc, wts["mla_w_out"][j], x, gate_m, name=f"mla_out_{tag}")
            rec.update(proj=proj, cq=cq, ck=ck, qc=qc, kc=kc, vc=vc, oh=oh, lse=lse, oc=oc, y=y)
        rec["x1"] = x
        h2 = _adaln_fwd(x, _row(wts["norm_ffn_g"][layer]), scale_f, shift_f, name=f"adaln_ffn_{tag}")
        ab2 = _mm(h2, wts["ffn_w_gu"][layer], mode="nn", out_dtype=F32, name=f"ffn_gu_{tag}")
        s = _swiglu_fwd(ab2, name=f"swiglu_{tag}")
        x, y2 = _mm_resid(s, wts["ffn_w_down"][layer], x, gate_f, name=f"ffn_down_{tag}")
        rec.update(h2=h2, ab2=ab2, s=s, y2=y2)
        saved.append(rec)

    dx, st, ls = _loss_head(x, _row(wts["final_norm_g"]), target, name="loss_head")
    loss = ls[0, 0]
    grads = {"final_norm_g": st[0]}
    per_layer = {k: [None] * DEPTH for k in ("norm_mix_g", "norm_ffn_g", "ffn_w_gu", "ffn_w_down")}
    per_gdn = {k: [None] * 2 for k in ("gdn_w_in", "gdn_conv_wt", "gdn_a_log", "gdn_dt_bias", "gdn_norm_g",
                                        "gdn_w_out")}
    per_mla = {k: [None] * 2 for k in ("mla_w_in", "mla_q_norm_g", "mla_kv_norm_g", "mla_w_uq", "mla_w_ukv",
                                        "mla_w_out")}
    dmod = [None] * DEPTH

    for layer in reversed(range(DEPTH)):
        j = layer // 2
        tag = f"l{layer}"
        rec = saved[layer]
        shift_m, scale_m, gate_m, shift_f, scale_f, gate_f = [_row(mod[layer, i]) for i in range(N_MOD)]
        dy2, st_g = _gate_bwd(dx, rec["y2"], gate_f, name=f"gate_bwd_ffn_{tag}")
        dgate_f = st_g[0]
        per_layer["ffn_w_down"][layer] = _mm(rec["s"], dy2, mode="tn", out_dtype=BF16, name=f"ffn_down_dw_{tag}")
        ds = _mm(dy2, wts["ffn_w_down"][layer], mode="nt", out_dtype=F32, name=f"ffn_down_dx_{tag}")
        dab2 = _swiglu_bwd(rec["ab2"], ds, name=f"swiglu_bwd_{tag}")
        per_layer["ffn_w_gu"][layer] = _mm(rec["h2"], dab2, mode="tn", out_dtype=BF16, name=f"ffn_gu_dw_{tag}")
        dh2 = _mm(dab2, wts["ffn_w_gu"][layer], mode="nt", out_dtype=BF16, name=f"ffn_gu_dx_{tag}")
        dx, st_n = _adaln_bwd(rec["x1"], _row(wts["norm_ffn_g"][layer]), scale_f, shift_f, dh2, dx,
                              name=f"adaln_ffn_bwd_{tag}")
        per_layer["norm_ffn_g"][layer] = st_n[0]
        dscale_f, dshift_f = st_n[1], st_n[2]
        dy, st_g = _gate_bwd(dx, rec["y"], gate_m, name=f"gate_bwd_mix_{tag}")
        dgate_m = st_g[0]
        if layer % 2 == 0:
            per_gdn["gdn_w_out"][j] = _mm(rec["og"], dy, mode="tn", out_dtype=BF16, name=f"gdn_out_dw_{tag}")
            dog = _mm(dy, wts["gdn_w_out"][j], mode="nt", out_dtype=BF16, name=f"gdn_out_dx_{tag}")
            do, dgp, st_o = _gdn_onorm_bwd(rec["o"], rec["proj"], _row(wts["gdn_norm_g"][j]), dog,
                                           name=f"gdn_onorm_bwd_{tag}")
            per_gdn["gdn_norm_g"][j] = st_o[0]
            dqkv3 = _gdn_chunk_bwd(rec["qkv"], rec["gbc"], rec["bbc"], rec["states"], do, name=f"gdn_chunk_bwd_{tag}")
            dq_, dk_, dv_, dgc_, dbc_ = dqkv3
            dqkv = jnp.concatenate([dq_, dk_, dv_], axis=1)
            dgb = jnp.concatenate([jnp.transpose(dgc_[:, :, 0]), jnp.transpose(dbc_[:, :, 0])], axis=1)
            dgb = jnp.pad(dgb, ((0, 0), (0, LANES - 2 * GDN_HEADS)))
            dab, st_a = _gdn_gate_bwd(rec["ab"], wts["gdn_gate_prm"][j], dgb, name=f"gdn_gate_bwd_{tag}")
            per_gdn["gdn_a_log"][j] = st_a[0, :GDN_HEADS]
            per_gdn["gdn_dt_bias"][j] = st_a[1, :GDN_HEADS]
            dpre, dcw = _gdn_prep_bwd(rec["proj"], wts["gdn_conv_wt"][j], dqkv, name=f"gdn_prep_bwd_{tag}")
            per_gdn["gdn_conv_wt"][j] = dcw
            dproj = jnp.concatenate([dpre, dgp], axis=1)
            dw_main = _mm(rec["h"], dproj, mode="tn", out_dtype=BF16, name=f"gdn_in_dw_{tag}")
            dw_ab = _mm(rec["h"], dab, mode="tn", out_dtype=BF16, name=f"gdn_in_ab_dw_{tag}")
            per_gdn["gdn_w_in"][j] = jnp.concatenate([dw_main, dw_ab[:, :2 * GDN_HEADS]], axis=1)
            dh_ab = _mm(dab, wts["gdn_w_ab"][j], mode="nt", out_dtype=F32, name=f"gdn_in_ab_dx_{tag}")
            dh = _mm(dproj, wts["gdn_w_main"][j], mode="nt", out_dtype=BF16, add=dh_ab, name=f"gdn_in_dx_{tag}")
        else:
            per_mla["mla_w_out"][j] = _mm(rec["oc"], dy, mode="tn", out_dtype=BF16, name=f"mla_out_dw_{tag}")
            doc = _mm(dy, wts["mla_w_out"][j], mode="nt", out_dtype=F32, name=f"mla_out_dx_{tag}")
            doh = jnp.transpose(doc.reshape(t, MLA_HEADS, MLA_V), (1, 0, 2))
            dqc, dkc, dvc = _attn_bwd(rec["qc"], rec["kc"], rec["vc"], rec["oh"], rec["lse"], doh,
                                      name=f"attn_bwd_{tag}")
            dqn = jnp.transpose(dqc[..., :MLA_NOPE], (1, 0, 2)).reshape(t, MLA_HEADS * MLA_NOPE)
            dqr = jnp.transpose(dqc[..., MLA_NOPE:], (1, 0, 2)).reshape(t, MLA_HEADS * MLA_ROPE)
            dkr = jnp.pad(jnp.sum(dkc[..., MLA_NOPE:], axis=0), ((0, 0), (0, LANES - MLA_ROPE)))
            drope = _rope_bwd(jnp.concatenate([dqr, dkr], axis=1), cos_t, sin_t, name=f"rope_bwd_{tag}")
            nrope = MLA_HEADS * MLA_ROPE
            dqf = jnp.concatenate([dqn, drope[:, :nrope]], axis=1).astype(BF16)
            dkvf = jnp.concatenate([jnp.transpose(dkc[..., :MLA_NOPE], (1, 0, 2)), jnp.transpose(dvc, (1, 0, 2))],
                                   axis=-1).reshape(t, MLA_HEADS * (MLA_NOPE + MLA_V)).astype(BF16)
            per_mla["mla_w_uq"][j] = _mm(rec["cq"], dqf, mode="tn", out_dtype=BF16, name=f"mla_uq_dw_{tag}")
            per_mla["mla_w_ukv"][j] = _mm(rec["ck"], dkvf, mode="tn", out_dtype=BF16, name=f"mla_ukv_dw_{tag}")
            dcq = _mm(dqf, wts["mla_w_uq"][j], mode="nt", out_dtype=F32, name=f"mla_uq_dx_{tag}")
            dck = _mm(dkvf, wts["mla_w_ukv"][j], mode="nt", out_dtype=F32, name=f"mla_ukv_dx_{tag}")
            dproj, st_p = _mla_prep_bwd(rec["proj"], _row(wts["mla_q_norm_g"][j]), _row(wts["mla_kv_norm_g"][j]),
                                        dcq, dck, drope[:, nrope:nrope + MLA_ROPE], name=f"mla_prep_bwd_{tag}")
            per_mla["mla_q_norm_g"][j] = st_p[0, :MLA_Q_RANK]
            per_mla["mla_kv_norm_g"][j] = st_p[0, MLA_Q_RANK:MLA_Q_RANK + MLA_KV_RANK]
            per_mla["mla_w_in"][j] = _mm(rec["h"], dproj, mode="tn", out_dtype=BF16, name=f"mla_in_dw_{tag}")
            dh = _mm(dproj, wts["mla_w_in"][j], mode="nt", out_dtype=BF16, name=f"mla_in_dx_{tag}")
        dx, st_n = _adaln_bwd(rec["x0"], _row(wts["norm_mix_g"][layer]), scale_m, shift_m, dh, dx,
                              name=f"adaln_mix_bwd_{tag}")
        per_layer["norm_mix_g"][layer] = st_n[0]
        dmod[layer] = jnp.stack([st_n[2], st_n[1], dgate_m, dshift_f, dscale_f, dgate_f])

    for d in (per_layer, per_gdn, per_mla):
        for k, v in d.items():
            grads[k] = jnp.stack(v)
    return loss, dx, jnp.stack(dmod), grads


BIG = ("gdn_w_in", "gdn_w_out", "mla_w_in", "mla_w_uq", "mla_w_ukv", "mla_w_out", "ffn_w_gate", "ffn_w_up",
       "ffn_w_down")
COL_SHARDED = ("gdn_w_in", "mla_w_uq", "mla_w_ukv", "ffn_w_gate", "ffn_w_up")
SMALL = ("ada_b", "norm_mix_g", "norm_ffn_g", "gdn_conv_w", "gdn_a_log", "gdn_dt_bias", "gdn_norm_g",
         "mla_q_norm_g", "mla_kv_norm_g", "final_norm_g")
WEIGHTS = ("ada_w", "ada_b", "norm_mix_g", "norm_ffn_g", "gdn_w_in", "gdn_conv_w", "gdn_a_log", "gdn_dt_bias",
           "gdn_norm_g", "gdn_w_out", "mla_w_in", "mla_q_norm_g", "mla_kv_norm_g", "mla_w_uq", "mla_w_ukv",
           "mla_w_out", "ffn_w_gate", "ffn_w_up", "ffn_w_down", "final_norm_g")


def _uq_to_kernel_layout(w):
    lead = w.shape[:-1]
    w4 = w.reshape(lead + (MLA_HEADS, MLA_QK))
    return jnp.concatenate([w4[..., :MLA_NOPE].reshape(lead + (-1,)), w4[..., MLA_NOPE:].reshape(lead + (-1,))],
                           axis=-1)


def _uq_from_kernel_layout(w):
    lead = w.shape[:-1]
    nope = w[..., :MLA_HEADS * MLA_NOPE].reshape(lead + (MLA_HEADS, MLA_NOPE))
    rope = w[..., MLA_HEADS * MLA_NOPE:].reshape(lead + (MLA_HEADS, MLA_ROPE))
    return jnp.concatenate([nope, rope], axis=-1).reshape(lead + (-1,))


def _full_weights(gathered, tiny, rep):
    w_in = _gather_cols(gathered["gdn_w_in"])
    wts = {
        "gdn_w_main": w_in[..., :GDN_MAIN],
        "gdn_w_ab": jnp.pad(w_in[..., GDN_MAIN:], ((0, 0), (0, 0), (0, LANES - 2 * GDN_HEADS))),
        "gdn_w_out": _gather_rows(gathered["gdn_w_out"]),
        "mla_w_in": _gather_rows(gathered["mla_w_in"]),
        "mla_w_uq": _uq_to_kernel_layout(_gather_cols(gathered["mla_w_uq"])),
        "mla_w_ukv": _gather_cols(gathered["mla_w_ukv"]),
        "mla_w_out": _gather_rows(gathered["mla_w_out"]),
        "ffn_w_gu": _gu_to_kernel_layout(_gather_cols(gathered["ffn_w_gate"]), _gather_cols(gathered["ffn_w_up"])),
        "ffn_w_down": _gather_rows(gathered["ffn_w_down"]),
        "gdn_conv_wt": jnp.transpose(_gather_rows(tiny["gdn_conv_w"]), (0, 2, 1)),
        "mla_q_norm_g": jnp.transpose(tiny["mla_q_norm_g"], (1, 0, 2)).reshape(2, MLA_Q_RANK),
        "mla_kv_norm_g": jnp.transpose(tiny["mla_kv_norm_g"], (1, 0, 2)).reshape(2, MLA_KV_RANK),
    }
    prm = jnp.zeros((2, 8, LANES), F32)
    prm = prm.at[:, 0, :GDN_HEADS].set(rep["gdn_a_log"]).at[:, 1, :GDN_HEADS].set(rep["gdn_dt_bias"])
    wts["gdn_gate_prm"] = prm
    for k in ("norm_mix_g", "norm_ffn_g", "gdn_norm_g", "final_norm_g"):
        wts[k] = rep[k]
    return wts


def _rope_tables(positions):
    inv_freq = ROPE_THETA ** (-jnp.arange(0, MLA_ROPE, 2, dtype=F32) / MLA_ROPE)
    ang = positions.astype(F32)[:, None] * inv_freq
    cos, sin = jnp.cos(ang), jnp.sin(ang)
    reps = LANES // MLA_ROPE
    return jnp.tile(jnp.concatenate([cos, cos], axis=1), (1, reps)), jnp.tile(
        jnp.concatenate([-sin, sin], axis=1), (1, reps))


def kernel(x, c, positions, ada_w, ada_b, norm_mix_g, norm_ffn_g, gdn_w_in, gdn_conv_w, gdn_a_log, gdn_dt_bias, gdn_norm_g, gdn_w_out, mla_w_in, mla_q_norm_g, mla_kv_norm_g, mla_w_uq, mla_w_ukv, mla_w_out, ffn_w_gate, ffn_w_up, ffn_w_down, final_norm_g, loss_target, m_ada_w, m_ada_b, m_norm_mix_g, m_norm_ffn_g, m_gdn_w_in, m_gdn_conv_w, m_gdn_a_log, m_gdn_dt_bias, m_gdn_norm_g, m_gdn_w_out, m_mla_w_in, m_mla_q_norm_g, m_mla_kv_norm_g, m_mla_w_uq, m_mla_w_ukv, m_mla_w_out, m_ffn_w_gate, m_ffn_w_up, m_ffn_w_down, m_final_norm_g, v_ada_w, v_ada_b, v_norm_mix_g, v_norm_ffn_g, v_gdn_w_in, v_gdn_conv_w, v_gdn_a_log, v_gdn_dt_bias, v_gdn_norm_g, v_gdn_w_out, v_mla_w_in, v_mla_q_norm_g, v_mla_kv_norm_g, v_mla_w_uq, v_mla_w_ukv, v_mla_w_out, v_ffn_w_gate, v_ffn_w_up, v_ffn_w_down, v_final_norm_g):
    W = dict(ada_w=ada_w, ada_b=ada_b, norm_mix_g=norm_mix_g, norm_ffn_g=norm_ffn_g, gdn_w_in=gdn_w_in,
             gdn_conv_w=gdn_conv_w, gdn_a_log=gdn_a_log, gdn_dt_bias=gdn_dt_bias, gdn_norm_g=gdn_norm_g,
             gdn_w_out=gdn_w_out, mla_w_in=mla_w_in, mla_q_norm_g=mla_q_norm_g, mla_kv_norm_g=mla_kv_norm_g,
             mla_w_uq=mla_w_uq, mla_w_ukv=mla_w_ukv, mla_w_out=mla_w_out, ffn_w_gate=ffn_w_gate,
             ffn_w_up=ffn_w_up, ffn_w_down=ffn_w_down, final_norm_g=final_norm_g)
    M = dict(ada_w=m_ada_w, ada_b=m_ada_b, norm_mix_g=m_norm_mix_g, norm_ffn_g=m_norm_ffn_g, gdn_w_in=m_gdn_w_in,
             gdn_conv_w=m_gdn_conv_w, gdn_a_log=m_gdn_a_log, gdn_dt_bias=m_gdn_dt_bias, gdn_norm_g=m_gdn_norm_g,
             gdn_w_out=m_gdn_w_out, mla_w_in=m_mla_w_in, mla_q_norm_g=m_mla_q_norm_g,
             mla_kv_norm_g=m_mla_kv_norm_g, mla_w_uq=m_mla_w_uq, mla_w_ukv=m_mla_w_ukv, mla_w_out=m_mla_w_out,
             ffn_w_gate=m_ffn_w_gate, ffn_w_up=m_ffn_w_up, ffn_w_down=m_ffn_w_down, final_norm_g=m_final_norm_g)
    V = dict(ada_w=v_ada_w, ada_b=v_ada_b, norm_mix_g=v_norm_mix_g, norm_ffn_g=v_norm_ffn_g, gdn_w_in=v_gdn_w_in,
             gdn_conv_w=v_gdn_conv_w, gdn_a_log=v_gdn_a_log, gdn_dt_bias=v_gdn_dt_bias, gdn_norm_g=v_gdn_norm_g,
             gdn_w_out=v_gdn_w_out, mla_w_in=v_mla_w_in, mla_q_norm_g=v_mla_q_norm_g,
             mla_kv_norm_g=v_mla_kv_norm_g, mla_w_uq=v_mla_w_uq, mla_w_ukv=v_mla_w_ukv, mla_w_out=v_mla_w_out,
             ffn_w_gate=v_ffn_w_gate, ffn_w_up=v_ffn_w_up, ffn_w_down=v_ffn_w_down, final_norm_g=v_final_norm_g)
    me = 4 * lax.axis_index("x") + 2 * lax.axis_index("y") + lax.axis_index("c")
    t = x.shape[1]
    wc = ada_w.shape[-1]

    tiny_shapes = [c.shape, gdn_conv_w.shape, mla_q_norm_g.shape, mla_kv_norm_g.shape]
    tiny_pack = _pack([c, gdn_conv_w, mla_q_norm_g, mla_kv_norm_g])
    got = _exchange([W[k].astype(BF16) for k in BIG] + [tiny_pack], scatter=False, name="gather_weights")
    gathered = dict(zip(BIG, got[:-1]))
    c_g, conv_g, qn_g, kvn_g = _unpack(got[-1], tiny_shapes, lead=(N_DEV,))
    c_all = c_g.reshape(N_DEV, D_MODEL)
    wts = _full_weights(gathered, {"gdn_conv_w": conv_g, "mla_q_norm_g": qn_g, "mla_kv_norm_g": kvn_g}, W)

    b_cols = lax.dynamic_slice_in_dim(ada_b, me * wc, wc, axis=1).reshape(DEPTH, 1, wc)
    mod_part = _ada_mod(c_all, ada_w, b_cols, name="ada_mod")
    (mod_g,) = _exchange([mod_part], scatter=False, name="gather_mod")
    mod_mine = lax.dynamic_index_in_dim(mod_g, me, axis=2, keepdims=False)
    mod = jnp.transpose(mod_mine, (1, 0, 2)).reshape(DEPTH, N_MOD, D_MODEL)

    cos_t, sin_t = _rope_tables(positions[0])
    loss, dx, dmod, g = _local_step(x[0], loss_target[0], mod, cos_t, sin_t, wts)
    loss = lax.psum(loss, ("x", "y", "c"))

    d_gate, d_up = _gu_from_kernel_layout(g["ffn_w_gu"])
    full = {
        "gdn_w_in": g["gdn_w_in"], "gdn_w_out": g["gdn_w_out"], "mla_w_in": g["mla_w_in"],
        "mla_w_uq": _uq_from_kernel_layout(g["mla_w_uq"]), "mla_w_ukv": g["mla_w_ukv"], "mla_w_out": g["mla_w_out"],
        "ffn_w_gate": d_gate, "ffn_w_up": d_up, "ffn_w_down": g["ffn_w_down"],
    }
    slots = [(_scatter_cols if k in COL_SHARDED else _scatter_rows)(full[k]) for k in BIG]
    parts = dict(zip(BIG, _exchange(slots, scatter=True, name="scatter_grads")))

    small_local = [dmod.reshape(DEPTH, N_MOD * D_MODEL), g["norm_mix_g"], g["norm_ffn_g"],
                   jnp.transpose(g["gdn_conv_wt"], (0, 2, 1)), g["gdn_a_log"], g["gdn_dt_bias"], g["gdn_norm_g"],
                   g["mla_q_norm_g"], g["mla_kv_norm_g"], g["final_norm_g"]]
    small_shapes = [a.shape for a in small_local]
    (small_g,) = _exchange([_pack(small_local)], scatter=False, name="gather_small_grads")
    small_sum = _unpack(_sum_parts(small_g, name="sum_small_grads"), small_shapes)
    dmod_all = _unpack(small_g, small_shapes[:1], lead=(N_DEV,))[0]
    sg = dict(zip(SMALL, small_sum))
    sg["gdn_conv_w"] = lax.dynamic_slice_in_dim(sg["gdn_conv_w"], me * gdn_conv_w.shape[1], gdn_conv_w.shape[1], 1)
    sg["mla_q_norm_g"] = lax.dynamic_slice_in_dim(sg["mla_q_norm_g"], me * mla_q_norm_g.shape[1],
                                                  mla_q_norm_g.shape[1], 1)
    sg["mla_kv_norm_g"] = lax.dynamic_slice_in_dim(sg["mla_kv_norm_g"], me * mla_kv_norm_g.shape[1],
                                                   mla_kv_norm_g.shape[1], 1)

    res = {}
    dmod_cols = jnp.transpose(lax.dynamic_slice_in_dim(dmod_all, me * wc, wc, axis=2), (1, 0, 2))
    res["ada_w"] = _ada_grad_adamw(c_all, dmod_cols, ada_w, m_ada_w, v_ada_w, name="ada_w_grad_adamw")
    for k in BIG:
        res[k] = _adamw(parts[k], W[k], M[k], V[k], name=f"adamw_{k}")
    shapes = [W[k].shape for k in SMALL]
    packed = [_pack([d[k] for k in SMALL]) for d in (sg, W, M, V)]
    outs = _adamw(packed[0][None, None], packed[1][None], packed[2][None], packed[3][None], name="adamw_small")
    unpacked = [_unpack(o[0], shapes) for o in outs]
    for i, k in enumerate(SMALL):
        res[k] = tuple(u[i] for u in unpacked)

    return (loss, dx[None], *[res[k][0] for k in WEIGHTS], *[res[k][1] for k in WEIGHTS],
            *[res[k][2] for k in WEIGHTS], *[res[k][3] for k in WEIGHTS])
```

```python
import functools
import math

import jax
import jax.numpy as jnp
from jax import lax
from jax.experimental import pallas as pl
from jax.experimental.pallas import tpu as pltpu

F32 = jnp.float32
BF16 = jnp.bfloat16
MXU_DTYPE = jnp.bfloat16

N_DEV = 8
D_MODEL = 1024
DEPTH = 4
GDN_HEADS = 8
GDN_HEAD_DIM = 128
GDN_KEY_DIM = GDN_HEADS * GDN_HEAD_DIM
GDN_CHUNK = 64
GDN_CONV = 4
GDN_MAIN = 4 * GDN_KEY_DIM
MLA_HEADS = 8
MLA_NOPE = 128
MLA_ROPE = 64
MLA_V = 128
MLA_Q_RANK = 384
MLA_KV_RANK = 256
MLA_IN = MLA_Q_RANK + MLA_KV_RANK + MLA_ROPE
MLA_QK = MLA_NOPE + MLA_ROPE
ROPE_THETA = 10000.0
D_FF = 2816
N_MOD = 6
EPS = 1e-6
LANES = 128
VMEM_LIMIT = 48 * 1024 * 1024

ADAM_LR = 0.001
ADAM_B1 = 0.9
ADAM_B2 = 0.999
ADAM_EPS = 1e-08
ADAM_WD = 0.01
ADAM_STEP = 10
ADAM_BC1 = 1.0 - ADAM_B1 ** ADAM_STEP
ADAM_BC2 = 1.0 - ADAM_B2 ** ADAM_STEP

NN = (((1,), (0,)), ((), ()))
NT = (((1,), (1,)), ((), ()))
TN = (((0,), (0,)), ((), ()))
NEG = -1e30


def _dotb(a, b, dims):
    return lax.dot_general(a.astype(MXU_DTYPE), b.astype(MXU_DTYPE), dims, preferred_element_type=F32)


def _dotf(a, b, dims):
    return lax.dot_general(a, b, dims, precision=lax.Precision.HIGHEST, preferred_element_type=F32)


def _params(*sem):
    return pltpu.CompilerParams(dimension_semantics=sem, vmem_limit_bytes=VMEM_LIMIT)


def _pick(n, pref, mult=LANES):
    best = None
    t = mult
    while t <= min(n, pref):
        if n % t == 0:
            best = t
        t += mult
    return best if best is not None else n


def _sigmoid(z):
    return 1.0 / (1.0 + jnp.exp(-z))


def _exchange(arrays, *, scatter, name):
    n = len(arrays)
    out_shape = tuple(
        jax.ShapeDtypeStruct(a.shape if scatter else (N_DEV,) + a.shape, a.dtype) for a in arrays)

    def body(*refs):
        ins, outs = refs[:n], refs[n:2 * n]
        send_sems, recv_sems, local_sems = refs[2 * n:]
        x, y, c = lax.axis_index("x"), lax.axis_index("y"), lax.axis_index("c")
        me = 4 * x + 2 * y + c
        copies = []
        for k in range(n):
            src_own = ins[k].at[me] if scatter else ins[k]
            own = pltpu.make_async_copy(src_own, outs[k].at[me], local_sems.at[k])
            own.start()
            copies.append(own)
        sends = []
        for p in range(1, N_DEV):
            px, py, pc = x ^ ((p >> 2) & 1), y ^ ((p >> 1) & 1), c ^ (p & 1)
            peer = 4 * px + 2 * py + pc
            for k in range(n):
                cp = pltpu.make_async_remote_copy(
                    src_ref=ins[k].at[peer] if scatter else ins[k],
                    dst_ref=outs[k].at[me],
                    send_sem=send_sems.at[k, p - 1],
                    recv_sem=recv_sems.at[k, p - 1],
                    device_id=(px, py, pc),
                    device_id_type=pl.DeviceIdType.MESH,
                )
                cp.start()
                sends.append((cp, k, peer, p))
        for cp, k, peer, p in sends:
            pltpu.make_async_remote_copy(
                src_ref=ins[k].at[peer] if scatter else ins[k],
                dst_ref=outs[k].at[peer],
                send_sem=send_sems.at[k, p - 1],
                recv_sem=recv_sems.at[k, p - 1],
                device_id=(x, y, c),
                device_id_type=pl.DeviceIdType.MESH,
            ).wait_recv()
        for cp, _, _, _ in sends:
            cp.wait_send()
        for own in copies:
            own.wait()

    any_spec = pl.BlockSpec(memory_space=pl.ANY)
    outs = pl.pallas_call(
        body,
        name=name,
        out_shape=out_shape,
        in_specs=[any_spec] * n,
        out_specs=tuple([any_spec] * n),
        scratch_shapes=[
            pltpu.SemaphoreType.DMA((n, N_DEV - 1)),
            pltpu.SemaphoreType.DMA((n, N_DEV - 1)),
            pltpu.SemaphoreType.DMA((n,)),
        ],
        compiler_params=pltpu.CompilerParams(has_side_effects=True),
    )(*arrays)
    return list(outs)


def _peer(x, y, c, p):
    return x ^ ((p >> 2) & 1), y ^ ((p >> 1) & 1), c ^ (p & 1)


def _exchange_start(arrays, *, scatter, name):
    n = len(arrays)
    lands = [lax.empty(a.shape if scatter else (N_DEV,) + a.shape, a.dtype) for a in arrays]

    def body(*refs):
        ins, zones = refs[:n], refs[n:2 * n]
        send_sems, recv_sems = refs[2 * n], refs[2 * n + 1]
        token = refs[4 * n + 2]
        x, y, c = lax.axis_index("x"), lax.axis_index("y"), lax.axis_index("c")
        me = 4 * x + 2 * y + c
        for p in range(1, N_DEV):
            px, py, pc = _peer(x, y, c, p)
            for k in range(n):
                pltpu.make_async_remote_copy(
                    src_ref=ins[k].at[4 * px + 2 * py + pc] if scatter else ins[k],
                    dst_ref=zones[k].at[me],
                    send_sem=send_sems.at[k * (N_DEV - 1) + p - 1],
                    recv_sem=recv_sems.at[k * (N_DEV - 1) + p - 1],
                    device_id=(px, py, pc),
                    device_id_type=pl.DeviceIdType.MESH,
                ).start()
        token[...] = jnp.zeros_like(token)

    hbm = pl.BlockSpec(memory_space=pltpu.HBM)
    sem = pl.BlockSpec(memory_space=pltpu.SEMAPHORE)
    outs = pl.pallas_call(
        body,
        name=name,
        out_shape=(pltpu.SemaphoreType.DMA((n * (N_DEV - 1),)), pltpu.SemaphoreType.DMA((n * (N_DEV - 1),)),
                   *[pltpu.HBM(a.shape, a.dtype) for a in arrays], *[pltpu.HBM(z.shape, z.dtype) for z in lands],
                   jax.ShapeDtypeStruct((8, LANES), F32)),
        in_specs=[hbm] * (2 * n),
        out_specs=(sem, sem, *[hbm] * (2 * n), pl.BlockSpec(memory_space=pltpu.VMEM)),
        input_output_aliases={k: 2 + k for k in range(2 * n)},
        compiler_params=pltpu.CompilerParams(has_side_effects=pltpu.SideEffectType.DATAFLOW_SIDE_EFFECTING),
    )(*[pltpu.with_memory_space_constraint(a, pltpu.HBM) for a in arrays],
      *[pltpu.with_memory_space_constraint(z, pltpu.HBM) for z in lands])
    return outs[0], outs[1], list(outs[2:2 + n]), list(outs[2 + n:2 + 2 * n]), outs[-1]


def _exchange_wait(started, after, *, scatter, name):
    send_sems, recv_sems, srcs, lands, _ = started
    n = len(srcs)

    def body(*refs):
        ins, zones = refs[:n], refs[n:2 * n]
        s_sems, r_sems = refs[2 * n], refs[2 * n + 1]
        x, y, c = lax.axis_index("x"), lax.axis_index("y"), lax.axis_index("c")
        for p in range(1, N_DEV):
            px, py, pc = _peer(x, y, c, p)
            peer = 4 * px + 2 * py + pc
            for k in range(n):
                cp = pltpu.make_async_remote_copy(
                    src_ref=ins[k].at[peer] if scatter else ins[k],
                    dst_ref=zones[k].at[peer],
                    send_sem=s_sems.at[k * (N_DEV - 1) + p - 1],
                    recv_sem=r_sems.at[k * (N_DEV - 1) + p - 1],
                    device_id=(px, py, pc),
                    device_id_type=pl.DeviceIdType.MESH,
                )
                cp.wait_send()
                cp.wait_recv()

    hbm = pl.BlockSpec(memory_space=pltpu.HBM)
    sem = pl.BlockSpec(memory_space=pltpu.SEMAPHORE)
    outs = pl.pallas_call(
        body,
        name=name,
        out_shape=tuple(pltpu.HBM(a.shape, a.dtype) for a in srcs + lands),
        in_specs=[hbm] * (2 * n) + [sem, sem, pl.BlockSpec(memory_space=pl.ANY)],
        out_specs=tuple([hbm] * (2 * n)),
        input_output_aliases={k: k for k in range(2 * n)},
        compiler_params=pltpu.CompilerParams(has_side_effects=pltpu.SideEffectType.DATAFLOW_SIDE_EFFECTING),
    )(*srcs, *lands, send_sems, recv_sems, after)
    return list(outs[:n]), list(outs[n:])


def _mm(a, b, *, mode, out_dtype, name, add=None, tm=512, tn=512):
    if mode == "nn":
        (m, kd), (_, nd) = a.shape, b.shape
    elif mode == "nt":
        (m, kd), (nd, _) = a.shape, b.shape
    else:
        (kd, m), (_, nd) = a.shape, b.shape
    tm = _pick(m, tm, LANES if mode == "tn" else 16)
    tn = _pick(nd, tn)
    dims = {"nn": NN, "nt": NT, "tn": TN}[mode]
    a_spec = pl.BlockSpec((kd, tm), lambda i, j: (0, i)) if mode == "tn" else pl.BlockSpec((tm, kd), lambda i, j: (i, 0))
    b_spec = pl.BlockSpec((tn, kd), lambda i, j: (j, 0)) if mode == "nt" else pl.BlockSpec((kd, tn), lambda i, j: (0, j))
    o_spec = pl.BlockSpec((tm, tn), lambda i, j: (i, j))
    has_add = add is not None

    def body(*refs):
        a_ref, b_ref = refs[0], refs[1]
        o_ref = refs[-1]
        acc = _dotb(a_ref[...], b_ref[...], dims)
        if has_add:
            acc = acc + refs[2][...].astype(F32)
        o_ref[...] = acc.astype(o_ref.dtype)

    ins = [a, b] + ([add] if has_add else [])
    specs = [a_spec, b_spec] + ([o_spec] if has_add else [])
    return pl.pallas_call(
        body, name=name, grid=(m // tm, nd // tn), in_specs=specs, out_specs=o_spec,
        out_shape=jax.ShapeDtypeStruct((m, nd), out_dtype),
        compiler_params=_params("parallel", "parallel"),
    )(*ins)


def _mm_resid(a, b, x, gate, *, name, tm=512, tn=512):
    m, kd = a.shape
    nd = b.shape[1]
    tm = _pick(m, tm, 16)
    tn = _pick(nd, tn)
    o_spec = pl.BlockSpec((tm, tn), lambda i, j: (i, j))

    def body(a_ref, b_ref, x_ref, g_ref, xo_ref, y_ref):
        y = _dotb(a_ref[...], b_ref[...], NN)
        y_ref[...] = y
        xo_ref[...] = x_ref[...] + g_ref[...] * y

    return pl.pallas_call(
        body, name=name, grid=(m // tm, nd // tn),
        in_specs=[pl.BlockSpec((tm, kd), lambda i, j: (i, 0)), pl.BlockSpec((kd, tn), lambda i, j: (0, j)),
                  o_spec, pl.BlockSpec((1, tn), lambda i, j: (0, j))],
        out_specs=(o_spec, o_spec),
        out_shape=(jax.ShapeDtypeStruct((m, nd), F32), jax.ShapeDtypeStruct((m, nd), F32)),
        compiler_params=_params("parallel", "parallel"),
    )(a, b, x, gate)


ROWS = 256


def _row_spec(width, rows=ROWS):
    return pl.BlockSpec((rows, width), lambda i: (i, 0))


def _const_spec(shape):
    return pl.BlockSpec(shape, lambda i: tuple(0 for _ in shape))


def _adaln_fwd(x, g, scale, shift, *, name):
    t, d = x.shape

    def body(x_ref, g_ref, sc_ref, sh_ref, h_ref):
        xv = x_ref[...]
        r = lax.rsqrt(jnp.mean(xv * xv, axis=-1, keepdims=True) + EPS)
        h_ref[...] = (xv * r * g_ref[...] * (1.0 + sc_ref[...]) + sh_ref[...]).astype(h_ref.dtype)

    return pl.pallas_call(
        body, name=name, grid=(t // ROWS,),
        in_specs=[_row_spec(d), _const_spec((1, d)), _const_spec((1, d)), _const_spec((1, d))],
        out_specs=_row_spec(d), out_shape=jax.ShapeDtypeStruct((t, d), BF16),
        compiler_params=_params("parallel"),
    )(x, g, scale, shift)


def _adaln_bwd(x, g, scale, shift, dh, dres, *, name):
    t, d = x.shape

    def body(x_ref, g_ref, sc_ref, sh_ref, dh_ref, dr_ref, dx_ref, st_ref):
        @pl.when(pl.program_id(0) == 0)
        def _():
            st_ref[...] = jnp.zeros_like(st_ref)

        xv = x_ref[...]
        dhv = dh_ref[...].astype(F32)
        gv = g_ref[...]
        r = lax.rsqrt(jnp.mean(xv * xv, axis=-1, keepdims=True) + EPS)
        xh = xv * r
        nv = xh * gv
        dn = dhv * (1.0 + sc_ref[...])
        dxh = dn * gv
        dx_ref[...] = dr_ref[...] + r * (dxh - xh * jnp.mean(dxh * xh, axis=-1, keepdims=True))
        st_ref[0:1, :] += jnp.sum(dn * xh, axis=0, keepdims=True)
        st_ref[1:2, :] += jnp.sum(dhv * nv, axis=0, keepdims=True)
        st_ref[2:3, :] += jnp.sum(dhv, axis=0, keepdims=True)

    return pl.pallas_call(
        body, name=name, grid=(t // ROWS,),
        in_specs=[_row_spec(d), _const_spec((1, d)), _const_spec((1, d)), _const_spec((1, d)),
                  _row_spec(d), _row_spec(d)],
        out_specs=(_row_spec(d), _const_spec((8, d))),
        out_shape=(jax.ShapeDtypeStruct((t, d), F32), jax.ShapeDtypeStruct((8, d), F32)),
        compiler_params=_params("arbitrary"),
    )(x, g, scale, shift, dh, dres)


def _gate_bwd(dxo, y, gate, dep, *, name):
    t, d = dxo.shape

    def body(dx_ref, y_ref, g_ref, dep_ref, dy_ref, st_ref):
        @pl.when(pl.program_id(0) == 0)
        def _():
            st_ref[...] = jnp.zeros_like(st_ref)

        dxv = dx_ref[...]
        dy_ref[...] = (dxv * g_ref[...]).astype(dy_ref.dtype)
        st_ref[0:1, :] += jnp.sum(dxv * y_ref[...], axis=0, keepdims=True)

    return pl.pallas_call(
        body, name=name, grid=(t // ROWS,),
        in_specs=[_row_spec(d), _row_spec(d), _const_spec((1, d)), _const_spec((8, LANES))],
        out_specs=(_row_spec(d), _const_spec((8, d))),
        out_shape=(jax.ShapeDtypeStruct((t, d), BF16), jax.ShapeDtypeStruct((8, d), F32)),
        compiler_params=_params("arbitrary"),
    )(dxo, y, gate, dep)


def _loss_head(x, g, target, *, name):
    t, d = x.shape

    def body(x_ref, g_ref, t_ref, dx_ref, st_ref, ls_ref):
        @pl.when(pl.program_id(0) == 0)
        def _():
            st_ref[...] = jnp.zeros_like(st_ref)
            ls_ref[...] = jnp.zeros_like(ls_ref)

        xv = x_ref[...]
        gv = g_ref[...]
        r = lax.rsqrt(jnp.mean(xv * xv, axis=-1, keepdims=True) + EPS)
        xh = xv * r
        err = xh * gv - t_ref[...]
        ls_ref[...] += 0.5 * jnp.sum(jnp.mean(err * err, axis=-1, keepdims=True))
        dy = err * (1.0 / d)
        dxh = dy * gv
        dx_ref[...] = r * (dxh - xh * jnp.mean(dxh * xh, axis=-1, keepdims=True))
        st_ref[0:1, :] += jnp.sum(dy * xh, axis=0, keepdims=True)

    return pl.pallas_call(
        body, name=name, grid=(t // ROWS,),
        in_specs=[_row_spec(d), _const_spec((1, d)), _row_spec(d)],
        out_specs=(_row_spec(d), _const_spec((8, d)), _const_spec((8, LANES))),
        out_shape=(jax.ShapeDtypeStruct((t, d), F32), jax.ShapeDtypeStruct((8, d), F32),
                   jax.ShapeDtypeStruct((8, LANES), F32)),
        compiler_params=_params("arbitrary"),
    )(x, g, target)


FFN_BLOCK = D_FF // 2


def _gu_to_kernel_layout(wg, wu):
    parts = []
    for b in range(D_FF // FFN_BLOCK):
        sl = slice(b * FFN_BLOCK, (b + 1) * FFN_BLOCK)
        parts += [wg[..., sl], wu[..., sl]]
    return jnp.concatenate(parts, axis=-1)


def _gu_from_kernel_layout(w):
    nb = D_FF // FFN_BLOCK
    wg = jnp.concatenate([w[..., 2 * b * FFN_BLOCK:(2 * b + 1) * FFN_BLOCK] for b in range(nb)], axis=-1)
    wu = jnp.concatenate([w[..., (2 * b + 1) * FFN_BLOCK:(2 * b + 2) * FFN_BLOCK] for b in range(nb)], axis=-1)
    return wg, wu


def _swiglu_fwd(ab, *, name):
    t = ab.shape[0]
    tn = FFN_BLOCK

    def body(ab_ref, s_ref):
        a = ab_ref[:, 0:tn]
        s_ref[...] = (a * _sigmoid(a) * ab_ref[:, tn:2 * tn]).astype(s_ref.dtype)

    return pl.pallas_call(
        body, name=name, grid=(t // ROWS, D_FF // tn),
        in_specs=[pl.BlockSpec((ROWS, 2 * tn), lambda i, j: (i, j))],
        out_specs=pl.BlockSpec((ROWS, tn), lambda i, j: (i, j)),
        out_shape=jax.ShapeDtypeStruct((t, D_FF), BF16),
        compiler_params=_params("parallel", "parallel"),
    )(ab)


def _swiglu_bwd(ab, ds, *, name):
    t = ab.shape[0]
    tn = FFN_BLOCK

    def body(ab_ref, ds_ref, d_ref):
        a = ab_ref[:, 0:tn]
        dsv = ds_ref[...]
        sg = _sigmoid(a)
        d_ref[:, 0:tn] = (dsv * ab_ref[:, tn:2 * tn] * sg * (1.0 + a * (1.0 - sg))).astype(d_ref.dtype)
        d_ref[:, tn:2 * tn] = (dsv * a * sg).astype(d_ref.dtype)

    return pl.pallas_call(
        body, name=name, grid=(t // ROWS, D_FF // tn),
        in_specs=[pl.BlockSpec((ROWS, 2 * tn), lambda i, j: (i, j)), pl.BlockSpec((ROWS, tn), lambda i, j: (i, j))],
        out_specs=pl.BlockSpec((ROWS, 2 * tn), lambda i, j: (i, j)),
        out_shape=jax.ShapeDtypeStruct((t, 2 * D_FF), BF16),
        compiler_params=_params("parallel", "parallel"),
    )(ab, ds)


def _shift_rows(v, s, rows):
    if s == 0:
        return v
    return jnp.where(rows >= s, pltpu.roll(v, s, 0), 0.0)


def _unshift_rows(v, s, rows, t):
    if s == 0:
        return v
    return jnp.where(rows < t - s, pltpu.roll(v, t - s, 0), 0.0)


def _conv_silu(x, w, rows):
    z = w[GDN_CONV - 1:GDN_CONV, :] * x
    for j in range(GDN_CONV - 1):
        z = z + w[j:j + 1, :] * _shift_rows(x, GDN_CONV - 1 - j, rows)
    sg = _sigmoid(z)
    return z, sg, z * sg


def _gdn_prep_fwd(proj, conv_wt, *, name):
    t = proj.shape[0]
    nh = GDN_HEADS

    def body(x_ref, w_ref, y_ref):
        j = pl.program_id(0)
        rows = lax.broadcasted_iota(jnp.int32, (t, LANES), 0)
        _, _, s = _conv_silu(x_ref[...], w_ref[...], rows)
        rs = lax.rsqrt(jnp.sum(s * s, axis=-1, keepdims=True) + EPS)
        qscale = jnp.where(j < nh, GDN_HEAD_DIM ** -0.5, 1.0)
        y_ref[...] = jnp.where(j < 2 * nh, s * rs * qscale, s)

    return pl.pallas_call(
        body, name=name, grid=(3 * nh,),
        in_specs=[pl.BlockSpec((t, LANES), lambda j: (0, j)), pl.BlockSpec((GDN_CONV, LANES), lambda j: (0, j))],
        out_specs=pl.BlockSpec((t, LANES), lambda j: (0, j)),
        out_shape=jax.ShapeDtypeStruct((t, 3 * GDN_KEY_DIM), F32),
        compiler_params=_params("parallel"),
    )(proj, conv_wt)


def _gdn_prep_bwd(proj, conv_wt, dy, *, name):
    t = proj.shape[0]
    nh = GDN_HEADS

    def body(x_ref, w_ref, dy_ref, dx_ref, dw_ref):
        j = pl.program_id(0)
        rows = lax.broadcasted_iota(jnp.int32, (t, LANES), 0)
        x = x_ref[...]
        w = w_ref[...]
        z, sg, s = _conv_silu(x, w, rows)
        rs = lax.rsqrt(jnp.sum(s * s, axis=-1, keepdims=True) + EPS)
        qscale = jnp.where(j < nh, GDN_HEAD_DIM ** -0.5, 1.0)
        dyv = dy_ref[...]
        nv = s * rs
        de = dyv * qscale
        ds_qk = rs * (de - nv * jnp.sum(de * nv, axis=-1, keepdims=True))
        ds = jnp.where(j < 2 * nh, ds_qk, dyv)
        dz = ds * sg * (1.0 + z * (1.0 - sg))
        dx = w[GDN_CONV - 1:GDN_CONV, :] * dz
        dw_ref[GDN_CONV - 1:GDN_CONV, :] = jnp.sum(dz * x, axis=0, keepdims=True)
        for k in range(GDN_CONV - 1):
            sh = GDN_CONV - 1 - k
            dx = dx + w[k:k + 1, :] * _unshift_rows(dz, sh, rows, t)
            dw_ref[k:k + 1, :] = jnp.sum(dz * _shift_rows(x, sh, rows), axis=0, keepdims=True)
        dx_ref[...] = dx.astype(dx_ref.dtype)

    return pl.pallas_call(
        body, name=name, grid=(3 * nh,),
        in_specs=[pl.BlockSpec((t, LANES), lambda j: (0, j)), pl.BlockSpec((GDN_CONV, LANES), lambda j: (0, j)),
                  pl.BlockSpec((t, LANES), lambda j: (0, j))],
        out_specs=(pl.BlockSpec((t, LANES), lambda j: (0, j)), pl.BlockSpec((GDN_CONV, LANES), lambda j: (0, j))),
        out_shape=(jax.ShapeDtypeStruct((t, 3 * GDN_KEY_DIM), BF16),
                   jax.ShapeDtypeStruct((GDN_CONV, 3 * GDN_KEY_DIM), F32)),
        compiler_params=_params("parallel"),
    )(proj, conv_wt, dy)


def _softplus(z):
    return jnp.maximum(z, 0.0) + jnp.log(1.0 + jnp.exp(-jnp.abs(z)))


def _gdn_gate_fwd(ab, prm, *, name):
    t = ab.shape[0]

    def body(ab_ref, p_ref, o_ref):
        v = ab_ref[...]
        lane = lax.broadcasted_iota(jnp.int32, v.shape, 1)
        g = -jnp.exp(p_ref[0:1, :]) * _softplus(v + p_ref[1:2, :])
        o_ref[...] = jnp.where(lane < GDN_HEADS, g, jnp.where(lane < 2 * GDN_HEADS, _sigmoid(v), 0.0))

    return pl.pallas_call(
        body, name=name, grid=(t // ROWS,),
        in_specs=[_row_spec(LANES), _const_spec((8, LANES))], out_specs=_row_spec(LANES),
        out_shape=jax.ShapeDtypeStruct((t, LANES), F32), compiler_params=_params("parallel"),
    )(ab, prm)


def _gdn_gate_bwd(ab, prm, dgb, *, name):
    t = ab.shape[0]

    def body(ab_ref, p_ref, d_ref, o_ref, st_ref):
        @pl.when(pl.program_id(0) == 0)
        def _():
            st_ref[...] = jnp.zeros_like(st_ref)

        v = ab_ref[...]
        dv = d_ref[...]
        lane = lax.broadcasted_iota(jnp.int32, v.shape, 1)
        is_a = lane < GDN_HEADS
        is_b = jnp.logical_and(lane >= GDN_HEADS, lane < 2 * GDN_HEADS)
        a_exp = jnp.exp(p_ref[0:1, :])
        zz = v + p_ref[1:2, :]
        g = -a_exp * _softplus(zz)
        da = dv * (-a_exp) * _sigmoid(zz)
        beta = _sigmoid(v)
        db = dv * beta * (1.0 - beta)
        o_ref[...] = jnp.where(is_a, da, jnp.where(is_b, db, 0.0)).astype(o_ref.dtype)
        st_ref[0:1, :] += jnp.sum(jnp.where(is_a, dv * g, 0.0), axis=0, keepdims=True)
        st_ref[1:2, :] += jnp.sum(jnp.where(is_a, da, 0.0), axis=0, keepdims=True)

    return pl.pallas_call(
        body, name=name, grid=(t // ROWS,),
        in_specs=[_row_spec(LANES), _const_spec((8, LANES)), _row_spec(LANES)],
        out_specs=(_row_spec(LANES), _const_spec((8, LANES))),
        out_shape=(jax.ShapeDtypeStruct((t, LANES), BF16), jax.ShapeDtypeStruct((8, LANES), F32)),
        compiler_params=_params("arbitrary"),
    )(ab, prm, dgb)


def _gdn_local(q, k, v, gb, bb):
    cs = q.shape[0]
    r = lax.broadcasted_iota(jnp.int32, (cs, cs), 0)
    c = lax.broadcasted_iota(jnp.int32, (cs, cs), 1)
    tril, strict, eye = r >= c, r > c, r == c
    g_colb = gb[:, :cs]
    g_row = jnp.sum(jnp.where(eye, g_colb, 0.0), axis=0, keepdims=True)
    gc_col = jnp.sum(jnp.where(tril, g_row, 0.0), axis=1, keepdims=True)
    gc_row = jnp.sum(jnp.where(r <= c, g_colb, 0.0), axis=0, keepdims=True)
    decay = jnp.exp(jnp.where(tril, gc_col - gc_row, NEG))
    gamma = jnp.exp(gc_col)
    gcl = gc_col[cs - 1:cs, :]
    gl = jnp.exp(gcl)
    kdec = jnp.exp(gcl - gc_col)
    kb = k * bb
    lmat = jnp.where(strict, _dotb(kb, k, NT) * decay, 0.0)
    xm = -lmat
    tinv = jnp.where(eye, 1.0, 0.0) + xm
    for _ in range(int(math.log2(cs)) - 1):
        xm = _dotf(xm, xm, NN)
        tinv = tinv + _dotf(tinv, xm, NN)
    vb = v * bb
    kg = kb * gamma
    u = _dotf(tinv, vb, NN)
    w = _dotf(tinv, kg, NN)
    pmat = jnp.where(tril, _dotb(q, k, NT) * decay, 0.0)
    return dict(tril=tril, strict=strict, eye=eye, r=r, c=c, decay=decay, gamma=gamma, gl=gl, kdec=kdec,
                kb=kb, lmat=lmat, tinv=tinv, vb=vb, kg=kg, u=u, w=w, pmat=pmat, qd=q * gamma, kd=k * kdec)


def _gdn_chunk_fwd(qkv, gbc, bbc, *, name):
    t = qkv.shape[0]
    nh, cs, hd = GDN_HEADS, GDN_CHUNK, GDN_HEAD_DIM
    nc = t // cs

    def body(q_ref, k_ref, v_ref, g_ref, b_ref, o_ref, st_ref, s_ref):
        @pl.when(pl.program_id(1) == 0)
        def _():
            s_ref[...] = jnp.zeros_like(s_ref)

        lo = _gdn_local(q_ref[...], k_ref[...], v_ref[...], g_ref[0], b_ref[0])
        s = s_ref[...]
        st_ref[0, 0] = s
        vn = lo["u"] - _dotb(lo["w"], s, NN)
        o_ref[...] = _dotb(lo["qd"], s, NN) + _dotb(lo["pmat"], vn, NN)
        s_ref[...] = s * lo["gl"] + _dotb(lo["kd"], vn, TN)

    gspec = pl.BlockSpec((1, cs, LANES), lambda h, n: (h, n, 0))
    return pl.pallas_call(
        body, name=name, grid=(nh, nc),
        in_specs=[pl.BlockSpec((cs, hd), lambda h, n: (n, h)), pl.BlockSpec((cs, hd), lambda h, n: (n, nh + h)),
                  pl.BlockSpec((cs, hd), lambda h, n: (n, 2 * nh + h)), gspec, gspec],
        out_specs=(pl.BlockSpec((cs, hd), lambda h, n: (n, h)),
                   pl.BlockSpec((1, 1, hd, hd), lambda h, n: (h, n, 0, 0))),
        out_shape=(jax.ShapeDtypeStruct((t, nh * hd), F32), jax.ShapeDtypeStruct((nh, nc, hd, hd), F32)),
        scratch_shapes=[pltpu.VMEM((hd, hd), F32)],
        compiler_params=_params("parallel", "arbitrary"),
    )(qkv, qkv, qkv, gbc, bbc)


def _gdn_chunk_bwd(qkv, gbc, bbc, states, do, *, name):
    t = qkv.shape[0]
    nh, cs, hd = GDN_HEADS, GDN_CHUNK, GDN_HEAD_DIM
    nc = t // cs

    def body(q_ref, k_ref, v_ref, g_ref, b_ref, st_ref, do_ref, dq_ref, dk_ref, dv_ref, dg_ref, db_ref, ds_ref):
        @pl.when(pl.program_id(1) == 0)
        def _():
            ds_ref[...] = jnp.zeros_like(ds_ref)

        q, k, v, bb = q_ref[...], k_ref[...], v_ref[...], b_ref[0]
        lo = _gdn_local(q, k, v, g_ref[0], bb)
        tril, strict, eye, r, c = lo["tril"], lo["strict"], lo["eye"], lo["r"], lo["c"]
        decay, gamma, gl, kdec = lo["decay"], lo["gamma"], lo["gl"], lo["kdec"]
        kb, tinv, w, pmat = lo["kb"], lo["tinv"], lo["w"], lo["pmat"]
        s = st_ref[0, 0]
        dsn = ds_ref[...]
        dov = do_ref[...]
        vn = lo["u"] - _dotb(w, s, NN)
        dvn = _dotb(pmat, dov, TN) + _dotb(lo["kd"], dsn, NN)
        dp = jnp.where(tril, _dotb(dov, vn, NT), 0.0)
        dqd = _dotb(dov, s, NT)
        dkd = _dotb(vn, dsn, NT)
        dgl = jnp.sum(jnp.sum(dsn * s, axis=1, keepdims=True), axis=0, keepdims=True)
        dw = -_dotb(dvn, s, NT)
        ds_ref[...] = gl * dsn + _dotb(lo["qd"], dov, TN) - _dotb(w, dvn, TN)
        dvb = _dotf(tinv, dvn, TN)
        dkg = _dotf(tinv, dw, TN)
        dt = _dotf(dvn, lo["vb"], NT) + _dotf(dw, lo["kg"], NT)
        dl = jnp.where(strict, -_dotf(_dotf(tinv, dt, TN), tinv, NT), 0.0)
        dkk = dl * decay
        dqk = dp * decay
        dkb = _dotb(dkk, k, NN) + dkg * gamma
        dk = _dotb(dkk, kb, TN) + _dotb(dqk, q, TN) + dkd * kdec + dkb * bb
        dq_ref[...] = _dotb(dqk, k, NN) + dqd * gamma
        dk_ref[...] = dk
        dv_ref[...] = dvb * bb
        db_ref[0] = jnp.broadcast_to(
            jnp.sum(dvb * v, axis=-1, keepdims=True) + jnp.sum(dkb * k, axis=-1, keepdims=True), (cs, LANES))
        e = dl * lo["lmat"] + dp * pmat
        e_col = jnp.sum(e, axis=0, keepdims=True)
        dgc = jnp.sum(e, axis=1, keepdims=True) - jnp.sum(jnp.where(eye, e_col, 0.0), axis=1, keepdims=True)
        dgamma = jnp.sum(dqd * q, axis=-1, keepdims=True) + jnp.sum(dkg * kb, axis=-1, keepdims=True)
        rk = jnp.sum(dkd * k, axis=-1, keepdims=True) * kdec
        dgcl = jnp.sum(rk, axis=0, keepdims=True) + dgl * gl
        rowi = lax.broadcasted_iota(jnp.int32, (cs, 1), 0)
        dgc = dgc + dgamma * gamma - rk + jnp.where(rowi == cs - 1, dgcl, 0.0)
        dgc_row = jnp.sum(jnp.where(eye, dgc, 0.0), axis=0, keepdims=True)
        dg = jnp.sum(jnp.where(c >= r, dgc_row, 0.0), axis=1, keepdims=True)
        dg_ref[0] = jnp.broadcast_to(dg, (cs, LANES))

    gspec = pl.BlockSpec((1, cs, LANES), lambda h, n: (h, nc - 1 - n, 0))
    col = lambda off: pl.BlockSpec((cs, hd), lambda h, n: (nc - 1 - n, off + h))
    return pl.pallas_call(
        body, name=name, grid=(nh, nc),
        in_specs=[col(0), col(nh), col(2 * nh), gspec, gspec,
                  pl.BlockSpec((1, 1, hd, hd), lambda h, n: (h, nc - 1 - n, 0, 0)), col(0)],
        out_specs=(col(0), col(0), col(0), gspec, gspec),
        out_shape=(jax.ShapeDtypeStruct((t, nh * hd), F32),) * 3
        + (jax.ShapeDtypeStruct((nh, t, LANES), F32),) * 2,
        scratch_shapes=[pltpu.VMEM((hd, hd), F32)],
        compiler_params=_params("parallel", "arbitrary"),
    )(qkv, qkv, qkv, gbc, bbc, states, do)


def _gdn_onorm_fwd(o, proj, norm_g, *, name):
    t = o.shape[0]
    w = GDN_KEY_DIM
    goff = 3 * GDN_KEY_DIM // w

    def body(o_ref, gp_ref, g_ref, y_ref):
        gv = g_ref[...]
        for h in range(GDN_HEADS):
            sl = slice(h * GDN_HEAD_DIM, (h + 1) * GDN_HEAD_DIM)
            oh = o_ref[:, sl]
            gp = gp_ref[:, sl]
            r = lax.rsqrt(jnp.mean(oh * oh, axis=-1, keepdims=True) + EPS)
            y_ref[:, sl] = (oh * r * gv * gp * _sigmoid(gp)).astype(y_ref.dtype)

    return pl.pallas_call(
        body, name=name, grid=(t // ROWS,),
        in_specs=[_row_spec(w), pl.BlockSpec((ROWS, w), lambda i: (i, goff)), _const_spec((1, GDN_HEAD_DIM))],
        out_specs=_row_spec(w), out_shape=jax.ShapeDtypeStruct((t, w), BF16),
        compiler_params=_params("parallel"),
    )(o, proj, norm_g)


def _gdn_onorm_bwd(o, proj, norm_g, dy, *, name):
    t = o.shape[0]
    w = GDN_KEY_DIM
    goff = 3 * GDN_KEY_DIM // w

    def body(o_ref, gp_ref, g_ref, dy_ref, do_ref, dgp_ref, st_ref):
        @pl.when(pl.program_id(0) == 0)
        def _():
            st_ref[...] = jnp.zeros_like(st_ref)

        gv = g_ref[...]
        acc = jnp.zeros((1, GDN_HEAD_DIM), F32)
        for h in range(GDN_HEADS):
            sl = slice(h * GDN_HEAD_DIM, (h + 1) * GDN_HEAD_DIM)
            oh = o_ref[:, sl]
            gp = gp_ref[:, sl]
            dyv = dy_ref[:, sl].astype(F32)
            r = lax.rsqrt(jnp.mean(oh * oh, axis=-1, keepdims=True) + EPS)
            xh = oh * r
            sg = _sigmoid(gp)
            dn = dyv * gp * sg
            dgp_ref[:, sl] = (dyv * xh * gv * sg * (1.0 + gp * (1.0 - sg))).astype(dgp_ref.dtype)
            acc = acc + jnp.sum(dn * xh, axis=0, keepdims=True)
            dxh = dn * gv
            do_ref[:, sl] = r * (dxh - xh * jnp.mean(dxh * xh, axis=-1, keepdims=True))
        st_ref[0:1, :] += acc

    return pl.pallas_call(
        body, name=name, grid=(t // ROWS,),
        in_specs=[_row_spec(w), pl.BlockSpec((ROWS, w), lambda i: (i, goff)), _const_spec((1, GDN_HEAD_DIM)),
                  _row_spec(w)],
        out_specs=(_row_spec(w), _row_spec(w), _const_spec((8, GDN_HEAD_DIM))),
        out_shape=(jax.ShapeDtypeStruct((t, w), F32), jax.ShapeDtypeStruct((t, w), BF16),
                   jax.ShapeDtypeStruct((8, GDN_HEAD_DIM), F32)),
        compiler_params=_params("arbitrary"),
    )(o, proj, norm_g, dy)


def _mla_prep_fwd(proj, qg, kvg, *, name):
    t = proj.shape[0]
    q1, k1 = MLA_Q_RANK, MLA_Q_RANK + MLA_KV_RANK

    def body(p_ref, qg_ref, kg_ref, cq_ref, ck_ref):
        cq = p_ref[:, 0:q1]
        ck = p_ref[:, q1:k1]
        cq_ref[...] = (cq * lax.rsqrt(jnp.mean(cq * cq, axis=-1, keepdims=True) + EPS) * qg_ref[...]).astype(BF16)
        ck_ref[...] = (ck * lax.rsqrt(jnp.mean(ck * ck, axis=-1, keepdims=True) + EPS) * kg_ref[...]).astype(BF16)

    return pl.pallas_call(
        body, name=name, grid=(t // ROWS,),
        in_specs=[_row_spec(MLA_IN), _const_spec((1, MLA_Q_RANK)), _const_spec((1, MLA_KV_RANK))],
        out_specs=(_row_spec(MLA_Q_RANK), _row_spec(MLA_KV_RANK)),
        out_shape=(jax.ShapeDtypeStruct((t, MLA_Q_RANK), BF16), jax.ShapeDtypeStruct((t, MLA_KV_RANK), BF16)),
        compiler_params=_params("parallel"),
    )(proj, qg, kvg)


def _mla_prep_bwd(proj, qg, kvg, dcq, dck, dkr, *, name):
    t = proj.shape[0]
    q1, k1 = MLA_Q_RANK, MLA_Q_RANK + MLA_KV_RANK

    def body(p_ref, qg_ref, kg_ref, dq_ref, dk_ref, dr_ref, dp_ref, st_ref):
        @pl.when(pl.program_id(0) == 0)
        def _():
            st_ref[...] = jnp.zeros_like(st_ref)

        for lo, hi, g_ref, d_ref in ((0, q1, qg_ref, dq_ref), (q1, k1, kg_ref, dk_ref)):
            xv = p_ref[:, lo:hi]
            dn = d_ref[...]
            r = lax.rsqrt(jnp.mean(xv * xv, axis=-1, keepdims=True) + EPS)
            xh = xv * r
            dxh = dn * g_ref[...]
            dp_ref[:, lo:hi] = (r * (dxh - xh * jnp.mean(dxh * xh, axis=-1, keepdims=True))).astype(dp_ref.dtype)
            st_ref[0:1, lo:hi] += jnp.sum(dn * xh, axis=0, keepdims=True)
        dp_ref[:, k1:MLA_IN] = dr_ref[...].astype(dp_ref.dtype)

    return pl.pallas_call(
        body, name=name, grid=(t // ROWS,),
        in_specs=[_row_spec(MLA_IN), _const_spec((1, MLA_Q_RANK)), _const_spec((1, MLA_KV_RANK)),
                  _row_spec(MLA_Q_RANK), _row_spec(MLA_KV_RANK), _row_spec(MLA_ROPE)],
        out_specs=(_row_spec(MLA_IN), _const_spec((8, MLA_IN))),
        out_shape=(jax.ShapeDtypeStruct((t, MLA_IN), BF16), jax.ShapeDtypeStruct((8, MLA_IN), F32)),
        compiler_params=_params("arbitrary"),
    )(proj, qg, kvg, dcq, dck, dkr)


def _rope(xr, cos_t, sin_t, *, name):
    t, w = xr.shape
    ns = w // LANES

    def body(x_ref, c_ref, s_ref, o_ref):
        cv, sv = c_ref[...], s_ref[...]
        lane = lax.broadcasted_iota(jnp.int32, (ROWS, LANES), 1)
        first = (lane % MLA_ROPE) < (MLA_ROPE // 2)
        for i in range(ns):
            sl = slice(i * LANES, (i + 1) * LANES)
            xv = x_ref[:, sl]
            sw = jnp.where(first, pltpu.roll(xv, LANES - MLA_ROPE // 2, 1), pltpu.roll(xv, MLA_ROPE // 2, 1))
            o_ref[:, sl] = xv * cv + sw * sv

    return pl.pallas_call(
        body, name=name, grid=(t // ROWS,),
        in_specs=[_row_spec(w), _row_spec(LANES), _row_spec(LANES)], out_specs=_row_spec(w),
        out_shape=jax.ShapeDtypeStruct((t, w), F32), compiler_params=_params("parallel"),
    )(xr, cos_t, sin_t)


def _rope_bwd(dr, cos_t, sin_t, *, name):
    t, w = dr.shape
    ns = w // LANES

    def body(d_ref, c_ref, s_ref, o_ref):
        cv, sv = c_ref[...], s_ref[...]
        lane = lax.broadcasted_iota(jnp.int32, (ROWS, LANES), 1)
        first = (lane % MLA_ROPE) < (MLA_ROPE // 2)
        for i in range(ns):
            sl = slice(i * LANES, (i + 1) * LANES)
            dv = d_ref[:, sl]
            ds = dv * sv
            sw = jnp.where(first, pltpu.roll(ds, LANES - MLA_ROPE // 2, 1), pltpu.roll(ds, MLA_ROPE // 2, 1))
            o_ref[:, sl] = dv * cv + sw

    return pl.pallas_call(
        body, name=name, grid=(t // ROWS,),
        in_specs=[_row_spec(w), _row_spec(LANES), _row_spec(LANES)], out_specs=_row_spec(w),
        out_shape=jax.ShapeDtypeStruct((t, w), F32), compiler_params=_params("parallel"),
    )(dr, cos_t, sin_t)


ATT_BLOCK = 256
ATT_SCALE = MLA_QK ** -0.5


def _causal_mask(i, j, blk):
    rows = i * blk + lax.broadcasted_iota(jnp.int32, (blk, blk), 0)
    cols = j * blk + lax.broadcasted_iota(jnp.int32, (blk, blk), 1)
    return cols <= rows


def _attn_fwd(q, k, v, *, name):
    nh, t, dk = q.shape
    dv = v.shape[-1]
    blk = min(ATT_BLOCK, t)

    def body(q_ref, k_ref, v_ref, o_ref, l_ref):
        i = pl.program_id(1)
        qv = q_ref[0]

        def step(j, carry):
            m, l, acc = carry
            off = pl.multiple_of(j * blk, blk)
            s = _dotb(qv, k_ref[0, pl.ds(off, blk), :], NT) * ATT_SCALE
            s = jnp.where(_causal_mask(i, j, blk), s, NEG)
            m_new = jnp.maximum(m, jnp.max(s, axis=-1, keepdims=True))
            p = jnp.exp(s - m_new)
            alpha = jnp.exp(m - m_new)
            l = alpha * l + jnp.sum(p, axis=-1, keepdims=True)
            acc = alpha * acc + _dotb(p, v_ref[0, pl.ds(off, blk), :], NN)
            return m_new, l, acc

        init = (jnp.full((blk, 1), NEG, F32), jnp.zeros((blk, 1), F32), jnp.zeros((blk, dv), F32))
        m, l, acc = lax.fori_loop(0, i + 1, step, init)
        o_ref[0] = acc / l
        l_ref[0] = jnp.broadcast_to(m + jnp.log(l), (blk, LANES))

    return pl.pallas_call(
        body, name=name, grid=(nh, t // blk),
        in_specs=[pl.BlockSpec((1, blk, dk), lambda h, i: (h, i, 0)), pl.BlockSpec((1, t, dk), lambda h, i: (h, 0, 0)),
                  pl.BlockSpec((1, t, dv), lambda h, i: (h, 0, 0))],
        out_specs=(pl.BlockSpec((1, blk, dv), lambda h, i: (h, i, 0)),
                   pl.BlockSpec((1, blk, LANES), lambda h, i: (h, i, 0))),
        out_shape=(jax.ShapeDtypeStruct((nh, t, dv), F32), jax.ShapeDtypeStruct((nh, t, LANES), F32)),
        compiler_params=_params("parallel", "parallel"),
    )(q, k, v)


def _attn_bwd(q, k, v, o, lse, do, *, name):
    nh, t, dk = q.shape
    dv = v.shape[-1]
    blk = min(ATT_BLOCK, t)
    nb = t // blk

    def body(q_ref, k_ref, v_ref, o_ref, l_ref, do_ref, dq_ref, dk_ref, dv_ref):
        j = pl.program_id(1)

        @pl.when(j == 0)
        def _():
            dq_ref[...] = jnp.zeros_like(dq_ref)

        kv, vv = k_ref[0], v_ref[0]

        def step(i, carry):
            dk_acc, dv_acc = carry
            off = pl.multiple_of(i * blk, blk)
            rows = pl.ds(off, blk)
            qv = q_ref[0, rows, :]
            dov = do_ref[0, rows, :]
            s = _dotb(qv, kv, NT) * ATT_SCALE
            s = jnp.where(_causal_mask(i, j, blk), s, NEG)
            p = jnp.exp(s - l_ref[0, rows, :][:, 0:1])
            dv_acc = dv_acc + _dotb(p, dov, TN)
            dp = _dotb(dov, vv, NT)
            delta = jnp.sum(dov * o_ref[0, rows, :], axis=-1, keepdims=True)
            ds = p * (dp - delta) * ATT_SCALE
            dk_acc = dk_acc + _dotb(ds, qv, TN)
            dq_ref[0, rows, :] += _dotb(ds, kv, NN)
            return dk_acc, dv_acc

        dk_acc, dv_acc = lax.fori_loop(j, nb, step, (jnp.zeros((blk, dk), F32), jnp.zeros((blk, dv), F32)))
        dk_ref[0] = dk_acc
        dv_ref[0] = dv_acc

    full = lambda w: pl.BlockSpec((1, t, w), lambda h, j: (h, 0, 0))
    part = lambda w: pl.BlockSpec((1, blk, w), lambda h, j: (h, j, 0))
    return pl.pallas_call(
        body, name=name, grid=(nh, nb),
        in_specs=[full(dk), part(dk), part(dv), full(dv), full(LANES), full(dv)],
        out_specs=(full(dk), part(dk), part(dv)),
        out_shape=(jax.ShapeDtypeStruct((nh, t, dk), F32), jax.ShapeDtypeStruct((nh, t, dk), F32),
                   jax.ShapeDtypeStruct((nh, t, dv), F32)),
        compiler_params=_params("parallel", "arbitrary"),
    )(q, k, v, o, lse, do)


def _ada_mod(c_all, ada_w, ada_b_cols, *, name):
    nl, d, wc = ada_w.shape

    def body(c_ref, w_ref, b_ref, o_ref):
        cv = c_ref[...]
        o_ref[0] = _dotb(cv * _sigmoid(cv), w_ref[0], NN) + b_ref[0]

    return pl.pallas_call(
        body, name=name, grid=(nl,),
        in_specs=[_const_spec((N_DEV, d)), pl.BlockSpec((1, d, wc), lambda l: (l, 0, 0)),
                  pl.BlockSpec((1, 1, wc), lambda l: (l, 0, 0))],
        out_specs=pl.BlockSpec((1, N_DEV, wc), lambda l: (l, 0, 0)),
        out_shape=jax.ShapeDtypeStruct((nl, N_DEV, wc), F32), compiler_params=_params("parallel"),
    )(c_all, ada_w, ada_b_cols)


def _adam_math(g, w, m, v):
    m2 = ADAM_B1 * m + (1.0 - ADAM_B1) * g
    v2 = ADAM_B2 * v + (1.0 - ADAM_B2) * (g * g)
    delta = -ADAM_LR * ((m2 / ADAM_BC1) / (jnp.sqrt(v2 / ADAM_BC2) + ADAM_EPS) + ADAM_WD * w)
    return delta, m2, v2


def _ada_grad_adamw(c_all, dmod_cols, w, m, v, *, name):
    nl, d, wc = w.shape
    tr = 256

    def body(c_ref, dm_ref, w_ref, m_ref, v_ref, g_ref, d_ref, m2_ref, v2_ref):
        cv = c_ref[...]
        g = _dotf(cv * _sigmoid(cv), dm_ref[0], TN)
        delta, m2, v2 = _adam_math(g, w_ref[0], m_ref[0], v_ref[0])
        g_ref[0], d_ref[0], m2_ref[0], v2_ref[0] = g, delta, m2, v2

    blk = pl.BlockSpec((1, tr, wc), lambda l, i: (l, i, 0))
    return pl.pallas_call(
        body, name=name, grid=(nl, d // tr),
        in_specs=[pl.BlockSpec((N_DEV, tr), lambda l, i: (0, i)), pl.BlockSpec((1, N_DEV, wc), lambda l, i: (l, 0, 0)),
                  blk, blk, blk],
        out_specs=(blk,) * 4, out_shape=(jax.ShapeDtypeStruct(w.shape, F32),) * 4,
        compiler_params=_params("parallel", "parallel"),
    )(c_all, dmod_cols, w, m, v)


def _adamw(parts, w, m, v, *, name):
    nl, r, c = w.shape
    ns = parts[0].shape[0]
    lanes_padded = -(-c // LANES) * LANES
    row_bytes = 2 * nl * ns * lanes_padded * parts[0].dtype.itemsize
    tr = _pick(r, min(256, max(16, (VMEM_LIMIT // 2) // row_bytes)), 16)

    def body(*refs):
        p_refs = refs[:nl]
        w_ref, m_ref, v_ref, g_ref, d_ref, m2_ref, v2_ref = refs[nl:]
        layer = pl.program_id(0)
        for q in range(nl):
            @pl.when(layer == q)
            def _(q=q):
                g = p_refs[q][0].astype(F32)
                for s in range(1, ns):
                    g = g + p_refs[q][s].astype(F32)
                delta, m2, v2 = _adam_math(g, w_ref[0], m_ref[0], v_ref[0])
                g_ref[0], d_ref[0], m2_ref[0], v2_ref[0] = g, delta, m2, v2

    blk = pl.BlockSpec((1, tr, c), lambda l, i: (l, i, 0))
    p_specs = [pl.BlockSpec((ns, tr, c), lambda l, i, q=q: (0, jnp.where(l == q, i, 0), 0)) for q in range(nl)]
    return pl.pallas_call(
        body, name=name, grid=(nl, r // tr),
        in_specs=p_specs + [blk, blk, blk],
        out_specs=(blk,) * 4, out_shape=(jax.ShapeDtypeStruct(w.shape, F32),) * 4,
        compiler_params=_params("arbitrary", "arbitrary"),
    )(*parts, w, m, v)


def _sum_parts(parts, *, name):
    ns, r, c = parts.shape

    def body(p_ref, o_ref):
        acc = p_ref[0]
        for s in range(1, ns):
            acc = acc + p_ref[s]
        o_ref[...] = acc

    return pl.pallas_call(
        body, name=name, out_shape=jax.ShapeDtypeStruct((r, c), F32),
        in_specs=[pl.BlockSpec(memory_space=pltpu.VMEM)], out_specs=pl.BlockSpec(memory_space=pltpu.VMEM),
    )(parts)


def _pack(arrs):
    flat = jnp.concatenate([a.reshape(-1).astype(F32) for a in arrs])
    pad = (-flat.shape[0]) % (8 * LANES)
    return jnp.pad(flat, (0, pad)).reshape(-1, LANES)


def _unpack(packed, shapes, lead=()):
    flat = packed.reshape(lead + (-1,))
    out, off = [], 0
    for s in shapes:
        n = math.prod(s)
        out.append(flat[..., off:off + n].reshape(lead + tuple(s)))
        off += n
    return out


def _gather_cols(g):
    _, nl, r, cs = g.shape
    return jnp.transpose(g, (1, 2, 0, 3)).reshape(nl, r, N_DEV * cs)


def _gather_rows(g):
    _, nl, rs, c = g.shape
    return jnp.transpose(g, (1, 0, 2, 3)).reshape(nl, N_DEV * rs, c)


def _scatter_cols(full):
    nl, r, c = full.shape
    return jnp.transpose(full.reshape(nl, r, N_DEV, c // N_DEV), (2, 0, 1, 3))


def _scatter_rows(full):
    nl, r, c = full.shape
    return jnp.transpose(full.reshape(nl, N_DEV, r // N_DEV, c), (1, 0, 2, 3))


def _row(v):
    return v.reshape(1, -1)


def _local_step(x, target, mod, cos_t, sin_t, rep, get_weights, put_grads):
    t = x.shape[0]
    saved = []
    for layer in range(DEPTH):
        j = layer // 2
        tag = f"l{layer}"
        shift_m, scale_m, gate_m, shift_f, scale_f, gate_f = [_row(mod[layer, i]) for i in range(N_MOD)]
        lw = get_weights(layer, x)
        rec = {"x0": x, "lw": lw}
        h = _adaln_fwd(x, _row(rep["norm_mix_g"][layer]), scale_m, shift_m, name=f"adaln_mix_{tag}")
        rec["h"] = h
        if layer % 2 == 0:
            proj = _mm(h, lw["w_main"], mode="nn", out_dtype=F32, name=f"gdn_in_{tag}")
            ab = _mm(h, lw["w_ab"], mode="nn", out_dtype=F32, name=f"gdn_in_ab_{tag}")
            qkv = _gdn_prep_fwd(proj, rep["gdn_conv_wt"][j], name=f"gdn_prep_{tag}")
            gbeta = _gdn_gate_fwd(ab, rep["gdn_gate_prm"][j], name=f"gdn_gate_{tag}")
            gbc = jnp.broadcast_to(jnp.transpose(gbeta[:, 0:GDN_HEADS])[:, :, None], (GDN_HEADS, t, LANES))
            bbc = jnp.broadcast_to(jnp.transpose(gbeta[:, GDN_HEADS:2 * GDN_HEADS])[:, :, None],
                                   (GDN_HEADS, t, LANES))
            o, states = _gdn_chunk_fwd(qkv, gbc, bbc, name=f"gdn_chunk_{tag}")
            og = _gdn_onorm_fwd(o, proj, _row(rep["gdn_norm_g"][j]), name=f"gdn_onorm_{tag}")
            x, y = _mm_resid(og, lw["w_out"], x, gate_m, name=f"gdn_out_{tag}")
            rec.update(proj=proj, ab=ab, qkv=qkv, gbc=gbc, bbc=bbc, states=states, o=o, og=og, y=y)
        else:
            proj = _mm(h, lw["w_in"], mode="nn", out_dtype=F32, name=f"mla_in_{tag}")
            cq, ck = _mla_prep_fwd(proj, _row(rep["mla_q_norm_g"][j]), _row(rep["mla_kv_norm_g"][j]),
                                   name=f"mla_prep_{tag}")
            qf = _mm(cq, lw["w_uq"], mode="nn", out_dtype=F32, name=f"mla_uq_{tag}")
            kvf = _mm(ck, lw["w_ukv"], mode="nn", out_dtype=F32, name=f"mla_ukv_{tag}")
            nrope = MLA_HEADS * MLA_ROPE
            krp = jnp.pad(proj[:, MLA_Q_RANK + MLA_KV_RANK:], ((0, 0), (0, LANES - MLA_ROPE)))
            roped = _rope(jnp.concatenate([qf[:, MLA_HEADS * MLA_NOPE:], krp], axis=1), cos_t, sin_t,
                          name=f"rope_{tag}")
            q_nope = qf[:, :MLA_HEADS * MLA_NOPE].reshape(t, MLA_HEADS, MLA_NOPE)
            q_rope = roped[:, :nrope].reshape(t, MLA_HEADS, MLA_ROPE)
            k_rope = jnp.broadcast_to(roped[:, None, nrope:nrope + MLA_ROPE], (t, MLA_HEADS, MLA_ROPE))
            kv3 = kvf.reshape(t, MLA_HEADS, MLA_NOPE + MLA_V)
            qc = jnp.transpose(jnp.concatenate([q_nope, q_rope], axis=-1), (1, 0, 2)).astype(BF16)
            kc = jnp.transpose(jnp.concatenate([kv3[..., :MLA_NOPE], k_rope], axis=-1), (1, 0, 2)).astype(BF16)
            vc = jnp.transpose(kv3[..., MLA_NOPE:], (1, 0, 2)).astype(BF16)
            oh, lse = _attn_fwd(qc, kc, vc, name=f"attn_{tag}")
            oc = jnp.transpose(oh, (1, 0, 2)).reshape(t, MLA_HEADS * MLA_V).astype(BF16)
            x, y = _mm_resid(oc, lw["w_out"], x, gate_m, name=f"mla_out_{tag}")
            rec.update(proj=proj, cq=cq, ck=ck, qc=qc, kc=kc, vc=vc, oh=oh, lse=lse, oc=oc, y=y)
        rec["x1"] = x
        h2 = _adaln_fwd(x, _row(rep["norm_ffn_g"][layer]), scale_f, shift_f, name=f"adaln_ffn_{tag}")
        ab2 = _mm(h2, lw["w_gu"], mode="nn", out_dtype=F32, name=f"ffn_gu_{tag}")
        s = _swiglu_fwd(ab2, name=f"swiglu_{tag}")
        x, y2 = _mm_resid(s, lw["w_down"], x, gate_f, name=f"ffn_down_{tag}")
        rec.update(h2=h2, ab2=ab2, s=s, y2=y2)
        saved.append(rec)

    dx, st, ls = _loss_head(x, _row(rep["final_norm_g"]), target, name="loss_head")
    loss = ls[0, 0]
    grads = {"final_norm_g": st[0]}
    per_layer = {k: [None] * DEPTH for k in ("norm_mix_g", "norm_ffn_g")}
    per_gdn = {k: [None] * 2 for k in ("gdn_conv_wt", "gdn_a_log", "gdn_dt_bias", "gdn_norm_g")}
    per_mla = {k: [None] * 2 for k in ("mla_q_norm_g", "mla_kv_norm_g")}
    dmod = [None] * DEPTH
    dep = jnp.zeros((8, LANES), F32)

    for layer in reversed(range(DEPTH)):
        j = layer // 2
        tag = f"l{layer}"
        rec = saved[layer]
        lw = rec["lw"]
        shift_m, scale_m, gate_m, shift_f, scale_f, gate_f = [_row(mod[layer, i]) for i in range(N_MOD)]
        dy2, st_g = _gate_bwd(dx, rec["y2"], gate_f, dep, name=f"gate_bwd_ffn_{tag}")
        dgate_f = st_g[0]
        dw_down = _mm(rec["s"], dy2, mode="tn", out_dtype=BF16, name=f"ffn_down_dw_{tag}")
        ds = _mm(dy2, lw["w_down"], mode="nt", out_dtype=F32, name=f"ffn_down_dx_{tag}")
        dab2 = _swiglu_bwd(rec["ab2"], ds, name=f"swiglu_bwd_{tag}")
        dw_gu = _mm(rec["h2"], dab2, mode="tn", out_dtype=BF16, name=f"ffn_gu_dw_{tag}")
        dep = put_grads(layer, "ffn", {"w_gu": dw_gu, "w_down": dw_down})
        dh2 = _mm(dab2, lw["w_gu"], mode="nt", out_dtype=BF16, name=f"ffn_gu_dx_{tag}")
        dx, st_n = _adaln_bwd(rec["x1"], _row(rep["norm_ffn_g"][layer]), scale_f, shift_f, dh2, dx,
                              name=f"adaln_ffn_bwd_{tag}")
        per_layer["norm_ffn_g"][layer] = st_n[0]
        dscale_f, dshift_f = st_n[1], st_n[2]
        dy, st_g = _gate_bwd(dx, rec["y"], gate_m, dep, name=f"gate_bwd_mix_{tag}")
        dgate_m = st_g[0]
        big = {}
        if layer % 2 == 0:
            big["w_out"] = _mm(rec["og"], dy, mode="tn", out_dtype=BF16, name=f"gdn_out_dw_{tag}")
            dog = _mm(dy, lw["w_out"], mode="nt", out_dtype=BF16, name=f"gdn_out_dx_{tag}")
            do, dgp, st_o = _gdn_onorm_bwd(rec["o"], rec["proj"], _row(rep["gdn_norm_g"][j]), dog,
                                           name=f"gdn_onorm_bwd_{tag}")
            per_gdn["gdn_norm_g"][j] = st_o[0]
            dqkv3 = _gdn_chunk_bwd(rec["qkv"], rec["gbc"], rec["bbc"], rec["states"], do, name=f"gdn_chunk_bwd_{tag}")
            dq_, dk_, dv_, dgc_, dbc_ = dqkv3
            dqkv = jnp.concatenate([dq_, dk_, dv_], axis=1)
            dgb = jnp.concatenate([jnp.transpose(dgc_[:, :, 0]), jnp.transpose(dbc_[:, :, 0])], axis=1)
            dgb = jnp.pad(dgb, ((0, 0), (0, LANES - 2 * GDN_HEADS)))
            dab, st_a = _gdn_gate_bwd(rec["ab"], rep["gdn_gate_prm"][j], dgb, name=f"gdn_gate_bwd_{tag}")
            per_gdn["gdn_a_log"][j] = st_a[0, :GDN_HEADS]
            per_gdn["gdn_dt_bias"][j] = st_a[1, :GDN_HEADS]
            dpre, dcw = _gdn_prep_bwd(rec["proj"], rep["gdn_conv_wt"][j], dqkv, name=f"gdn_prep_bwd_{tag}")
            per_gdn["gdn_conv_wt"][j] = dcw
            dproj = jnp.concatenate([dpre, dgp], axis=1)
            dw_main = _mm(rec["h"], dproj, mode="tn", out_dtype=BF16, name=f"gdn_in_dw_{tag}")
            dw_ab = _mm(rec["h"], dab, mode="tn", out_dtype=BF16, name=f"gdn_in_ab_dw_{tag}")
            big["w_in"] = jnp.concatenate([dw_main, dw_ab[:, :2 * GDN_HEADS]], axis=1)
            dep = put_grads(layer, "gdn", big)
            dh_ab = _mm(dab, lw["w_ab"], mode="nt", out_dtype=F32, name=f"gdn_in_ab_dx_{tag}")
            dh = _mm(dproj, lw["w_main"], mode="nt", out_dtype=BF16, add=dh_ab, name=f"gdn_in_dx_{tag}")
        else:
            big["w_out"] = _mm(rec["oc"], dy, mode="tn", out_dtype=BF16, name=f"mla_out_dw_{tag}")
            doc = _mm(dy, lw["w_out"], mode="nt", out_dtype=F32, name=f"mla_out_dx_{tag}")
            doh = jnp.transpose(doc.reshape(t, MLA_HEADS, MLA_V), (1, 0, 2))
            dqc, dkc, dvc = _attn_bwd(rec["qc"], rec["kc"], rec["vc"], rec["oh"], rec["lse"], doh,
                                      name=f"attn_bwd_{tag}")
            dqn = jnp.transpose(dqc[..., :MLA_NOPE], (1, 0, 2)).reshape(t, MLA_HEADS * MLA_NOPE)
            dqr = jnp.transpose(dqc[..., MLA_NOPE:], (1, 0, 2)).reshape(t, MLA_HEADS * MLA_ROPE)
            dkr = jnp.pad(jnp.sum(dkc[..., MLA_NOPE:], axis=0), ((0, 0), (0, LANES - MLA_ROPE)))
            drope = _rope_bwd(jnp.concatenate([dqr, dkr], axis=1), cos_t, sin_t, name=f"rope_bwd_{tag}")
            nrope = MLA_HEADS * MLA_ROPE
            dqf = jnp.concatenate([dqn, drope[:, :nrope]], axis=1).astype(BF16)
            dkvf = jnp.concatenate([jnp.transpose(dkc[..., :MLA_NOPE], (1, 0, 2)), jnp.transpose(dvc, (1, 0, 2))],
                                   axis=-1).reshape(t, MLA_HEADS * (MLA_NOPE + MLA_V)).astype(BF16)
            big["w_uq"] = _mm(rec["cq"], dqf, mode="tn", out_dtype=BF16, name=f"mla_uq_dw_{tag}")
            big["w_ukv"] = _mm(rec["ck"], dkvf, mode="tn", out_dtype=BF16, name=f"mla_ukv_dw_{tag}")
            dcq = _mm(dqf, lw["w_uq"], mode="nt", out_dtype=F32, name=f"mla_uq_dx_{tag}")
            dck = _mm(dkvf, lw["w_ukv"], mode="nt", out_dtype=F32, name=f"mla_ukv_dx_{tag}")
            dproj, st_p = _mla_prep_bwd(rec["proj"], _row(rep["mla_q_norm_g"][j]), _row(rep["mla_kv_norm_g"][j]),
                                        dcq, dck, drope[:, nrope:nrope + MLA_ROPE], name=f"mla_prep_bwd_{tag}")
            per_mla["mla_q_norm_g"][j] = st_p[0, :MLA_Q_RANK]
            per_mla["mla_kv_norm_g"][j] = st_p[0, MLA_Q_RANK:MLA_Q_RANK + MLA_KV_RANK]
            big["w_in"] = _mm(rec["h"], dproj, mode="tn", out_dtype=BF16, name=f"mla_in_dw_{tag}")
            dep = put_grads(layer, "mla", big)
            dh = _mm(dproj, lw["w_in"], mode="nt", out_dtype=BF16, name=f"mla_in_dx_{tag}")
        dx, st_n = _adaln_bwd(rec["x0"], _row(rep["norm_mix_g"][layer]), scale_m, shift_m, dh, dx,
                              name=f"adaln_mix_bwd_{tag}")
        per_layer["norm_mix_g"][layer] = st_n[0]
        dmod[layer] = jnp.stack([st_n[2], st_n[1], dgate_m, dshift_f, dscale_f, dgate_f])

    for d in (per_layer, per_gdn, per_mla):
        for k, v in d.items():
            grads[k] = jnp.stack(v)
    return loss, dx, jnp.stack(dmod), grads


BIG = ("gdn_w_in", "gdn_w_out", "mla_w_in", "mla_w_uq", "mla_w_ukv", "mla_w_out", "ffn_w_gate", "ffn_w_up",
       "ffn_w_down")
COL_SHARDED = ("gdn_w_in", "mla_w_uq", "mla_w_ukv", "ffn_w_gate", "ffn_w_up")
SMALL = ("ada_b", "norm_mix_g", "norm_ffn_g", "gdn_conv_w", "gdn_a_log", "gdn_dt_bias", "gdn_norm_g",
         "mla_q_norm_g", "mla_kv_norm_g", "final_norm_g")
WEIGHTS = ("ada_w", "ada_b", "norm_mix_g", "norm_ffn_g", "gdn_w_in", "gdn_conv_w", "gdn_a_log", "gdn_dt_bias",
           "gdn_norm_g", "gdn_w_out", "mla_w_in", "mla_q_norm_g", "mla_kv_norm_g", "mla_w_uq", "mla_w_ukv",
           "mla_w_out", "ffn_w_gate", "ffn_w_up", "ffn_w_down", "final_norm_g")


def _uq_to_kernel_layout(w):
    lead = w.shape[:-1]
    w4 = w.reshape(lead + (MLA_HEADS, MLA_QK))
    return jnp.concatenate([w4[..., :MLA_NOPE].reshape(lead + (-1,)), w4[..., MLA_NOPE:].reshape(lead + (-1,))],
                           axis=-1)


def _uq_from_kernel_layout(w):
    lead = w.shape[:-1]
    nope = w[..., :MLA_HEADS * MLA_NOPE].reshape(lead + (MLA_HEADS, MLA_NOPE))
    rope = w[..., MLA_HEADS * MLA_NOPE:].reshape(lead + (MLA_HEADS, MLA_ROPE))
    return jnp.concatenate([nope, rope], axis=-1).reshape(lead + (-1,))


def _layer_names(layer):
    mixer = ("gdn_w_in", "gdn_w_out") if layer % 2 == 0 else ("mla_w_in", "mla_w_uq", "mla_w_ukv", "mla_w_out")
    return mixer + ("ffn_w_gate", "ffn_w_up", "ffn_w_down")


def _layer_index(name, layer):
    return layer if name.startswith("ffn") else layer // 2


def _cols(g):
    return jnp.transpose(g, (1, 0, 2)).reshape(g.shape[1], N_DEV * g.shape[2])


def _rows(g):
    return g.reshape(N_DEV * g.shape[1], g.shape[2])


def _uncols(full):
    r, c = full.shape
    return jnp.transpose(full.reshape(r, N_DEV, c // N_DEV), (1, 0, 2))


def _unrows(full):
    r, c = full.shape
    return full.reshape(N_DEV, r // N_DEV, c)


def _layer_weights(layer, got):
    lw = {"w_gu": _gu_to_kernel_layout(_cols(got["ffn_w_gate"]), _cols(got["ffn_w_up"])),
          "w_down": _rows(got["ffn_w_down"])}
    if layer % 2 == 0:
        w_in = _cols(got["gdn_w_in"])
        lw.update(w_main=w_in[:, :GDN_MAIN], w_ab=jnp.pad(w_in[:, GDN_MAIN:], ((0, 0), (0, LANES - 2 * GDN_HEADS))),
                  w_out=_rows(got["gdn_w_out"]))
    else:
        lw.update(w_in=_rows(got["mla_w_in"]), w_uq=_uq_to_kernel_layout(_cols(got["mla_w_uq"])),
                  w_ukv=_cols(got["mla_w_ukv"]), w_out=_rows(got["mla_w_out"]))
    return lw


def _layer_grad_slots(kind, big):
    if kind == "ffn":
        d_gate, d_up = _gu_from_kernel_layout(big["w_gu"])
        return {"ffn_w_gate": _uncols(d_gate), "ffn_w_up": _uncols(d_up), "ffn_w_down": _unrows(big["w_down"])}
    if kind == "gdn":
        return {"gdn_w_in": _uncols(big["w_in"]), "gdn_w_out": _unrows(big["w_out"])}
    return {"mla_w_in": _unrows(big["w_in"]), "mla_w_uq": _uncols(_uq_from_kernel_layout(big["w_uq"])),
            "mla_w_ukv": _uncols(big["w_ukv"]), "mla_w_out": _unrows(big["w_out"])}


def _small_weights(tiny, rep):
    prm = jnp.zeros((2, 8, LANES), F32)
    prm = prm.at[:, 0, :GDN_HEADS].set(rep["gdn_a_log"]).at[:, 1, :GDN_HEADS].set(rep["gdn_dt_bias"])
    out = {
        "gdn_conv_wt": jnp.transpose(_gather_rows(tiny["gdn_conv_w"]), (0, 2, 1)),
        "mla_q_norm_g": jnp.transpose(tiny["mla_q_norm_g"], (1, 0, 2)).reshape(2, MLA_Q_RANK),
        "mla_kv_norm_g": jnp.transpose(tiny["mla_kv_norm_g"], (1, 0, 2)).reshape(2, MLA_KV_RANK),
        "gdn_gate_prm": prm,
    }
    for k in ("norm_mix_g", "norm_ffn_g", "gdn_norm_g", "final_norm_g"):
        out[k] = rep[k]
    return out


def _rope_tables(positions):
    inv_freq = ROPE_THETA ** (-jnp.arange(0, MLA_ROPE, 2, dtype=F32) / MLA_ROPE)
    ang = positions.astype(F32)[:, None] * inv_freq
    cos, sin = jnp.cos(ang), jnp.sin(ang)
    reps = LANES // MLA_ROPE
    return jnp.tile(jnp.concatenate([cos, cos], axis=1), (1, reps)), jnp.tile(
        jnp.concatenate([-sin, sin], axis=1), (1, reps))


def kernel(x, c, positions, ada_w, ada_b, norm_mix_g, norm_ffn_g, gdn_w_in, gdn_conv_w, gdn_a_log, gdn_dt_bias, gdn_norm_g, gdn_w_out, mla_w_in, mla_q_norm_g, mla_kv_norm_g, mla_w_uq, mla_w_ukv, mla_w_out, ffn_w_gate, ffn_w_up, ffn_w_down, final_norm_g, loss_target, m_ada_w, m_ada_b, m_norm_mix_g, m_norm_ffn_g, m_gdn_w_in, m_gdn_conv_w, m_gdn_a_log, m_gdn_dt_bias, m_gdn_norm_g, m_gdn_w_out, m_mla_w_in, m_mla_q_norm_g, m_mla_kv_norm_g, m_mla_w_uq, m_mla_w_ukv, m_mla_w_out, m_ffn_w_gate, m_ffn_w_up, m_ffn_w_down, m_final_norm_g, v_ada_w, v_ada_b, v_norm_mix_g, v_norm_ffn_g, v_gdn_w_in, v_gdn_conv_w, v_gdn_a_log, v_gdn_dt_bias, v_gdn_norm_g, v_gdn_w_out, v_mla_w_in, v_mla_q_norm_g, v_mla_kv_norm_g, v_mla_w_uq, v_mla_w_ukv, v_mla_w_out, v_ffn_w_gate, v_ffn_w_up, v_ffn_w_down, v_final_norm_g):
    W = dict(ada_w=ada_w, ada_b=ada_b, norm_mix_g=norm_mix_g, norm_ffn_g=norm_ffn_g, gdn_w_in=gdn_w_in,
             gdn_conv_w=gdn_conv_w, gdn_a_log=gdn_a_log, gdn_dt_bias=gdn_dt_bias, gdn_norm_g=gdn_norm_g,
             gdn_w_out=gdn_w_out, mla_w_in=mla_w_in, mla_q_norm_g=mla_q_norm_g, mla_kv_norm_g=mla_kv_norm_g,
             mla_w_uq=mla_w_uq, mla_w_ukv=mla_w_ukv, mla_w_out=mla_w_out, ffn_w_gate=ffn_w_gate,
             ffn_w_up=ffn_w_up, ffn_w_down=ffn_w_down, final_norm_g=final_norm_g)
    M = dict(ada_w=m_ada_w, ada_b=m_ada_b, norm_mix_g=m_norm_mix_g, norm_ffn_g=m_norm_ffn_g, gdn_w_in=m_gdn_w_in,
             gdn_conv_w=m_gdn_conv_w, gdn_a_log=m_gdn_a_log, gdn_dt_bias=m_gdn_dt_bias, gdn_norm_g=m_gdn_norm_g,
             gdn_w_out=m_gdn_w_out, mla_w_in=m_mla_w_in, mla_q_norm_g=m_mla_q_norm_g,
             mla_kv_norm_g=m_mla_kv_norm_g, mla_w_uq=m_mla_w_uq, mla_w_ukv=m_mla_w_ukv, mla_w_out=m_mla_w_out,
             ffn_w_gate=m_ffn_w_gate, ffn_w_up=m_ffn_w_up, ffn_w_down=m_ffn_w_down, final_norm_g=m_final_norm_g)
    V = dict(ada_w=v_ada_w, ada_b=v_ada_b, norm_mix_g=v_norm_mix_g, norm_ffn_g=v_norm_ffn_g, gdn_w_in=v_gdn_w_in,
             gdn_conv_w=v_gdn_conv_w, gdn_a_log=v_gdn_a_log, gdn_dt_bias=v_gdn_dt_bias, gdn_norm_g=v_gdn_norm_g,
             gdn_w_out=v_gdn_w_out, mla_w_in=v_mla_w_in, mla_q_norm_g=v_mla_q_norm_g,
             mla_kv_norm_g=v_mla_kv_norm_g, mla_w_uq=v_mla_w_uq, mla_w_ukv=v_mla_w_ukv, mla_w_out=v_mla_w_out,
             ffn_w_gate=v_ffn_w_gate, ffn_w_up=v_ffn_w_up, ffn_w_down=v_ffn_w_down, final_norm_g=v_final_norm_g)
    me = 4 * lax.axis_index("x") + 2 * lax.axis_index("y") + lax.axis_index("c")
    t = x.shape[1]
    wc = ada_w.shape[-1]

    gather = []
    for layer in range(DEPTH):
        srcs = [W[k][_layer_index(k, layer)].astype(BF16) for k in _layer_names(layer)]
        gather.append(_exchange_start(srcs, scatter=False, name=f"gather_start_l{layer}"))
    all_started = gather[0][4][0:1, 0:1] + gather[1][4][0:1, 0:1] + gather[2][4][0:1, 0:1] + gather[3][4][0:1, 0:1]

    def get_weights(layer, after):
        srcs, lands = _exchange_wait(gather[layer], all_started if layer == 0 else after, scatter=False,
                                     name=f"gather_wait_l{layer}")
        got = {k: lax.dynamic_update_index_in_dim(z, s, me, 0)
               for k, s, z in zip(_layer_names(layer), srcs, lands)}
        return _layer_weights(layer, got)

    tiny_shapes = [c.shape, gdn_conv_w.shape, mla_q_norm_g.shape, mla_kv_norm_g.shape]
    (tiny_g,) = _exchange([_pack([c, gdn_conv_w, mla_q_norm_g, mla_kv_norm_g])], scatter=False, name="gather_tiny")
    c_g, conv_g, qn_g, kvn_g = _unpack(tiny_g, tiny_shapes, lead=(N_DEV,))
    c_all = c_g.reshape(N_DEV, D_MODEL)
    rep = _small_weights({"gdn_conv_w": conv_g, "mla_q_norm_g": qn_g, "mla_kv_norm_g": kvn_g}, W)
    b_cols = lax.dynamic_slice_in_dim(ada_b, me * wc, wc, axis=1).reshape(DEPTH, 1, wc)
    mod_part = _ada_mod(c_all, ada_w, b_cols, name="ada_mod")
    (mod_g,) = _exchange([mod_part], scatter=False, name="gather_mod")
    mod_mine = lax.dynamic_index_in_dim(mod_g, me, axis=2, keepdims=False)
    mod = jnp.transpose(mod_mine, (1, 0, 2)).reshape(DEPTH, N_MOD, D_MODEL)

    scatter = []

    def put_grads(layer, kind, big):
        slots = _layer_grad_slots(kind, big)
        started = _exchange_start(list(slots.values()), scatter=True, name=f"scatter_start_{kind}_l{layer}")
        scatter.append((layer, kind, list(slots.keys()), started))
        return started[4]

    cos_t, sin_t = _rope_tables(positions[0])
    loss, dx, dmod, g = _local_step(x[0], loss_target[0], mod, cos_t, sin_t, rep, get_weights, put_grads)
    loss = lax.psum(loss, ("x", "y", "c"))

    parts = {k: [None] * W[k].shape[0] for k in BIG}
    for layer, kind, names, started in scatter:
        srcs, lands = _exchange_wait(started, dx, scatter=True, name=f"scatter_wait_{kind}_l{layer}")
        for k, s, z in zip(names, srcs, lands):
            own = lax.dynamic_index_in_dim(s, me, 0, keepdims=False)
            parts[k][_layer_index(k, layer)] = lax.dynamic_update_index_in_dim(z, own, me, 0)

    small_local = [dmod.reshape(DEPTH, N_MOD * D_MODEL), g["norm_mix_g"], g["norm_ffn_g"],
                   jnp.transpose(g["gdn_conv_wt"], (0, 2, 1)), g["gdn_a_log"], g["gdn_dt_bias"], g["gdn_norm_g"],
                   g["mla_q_norm_g"], g["mla_kv_norm_g"], g["final_norm_g"]]
    small_shapes = [a.shape for a in small_local]
    (small_g,) = _exchange([_pack(small_local)], scatter=False, name="gather_small_grads")
    small_sum = _unpack(_sum_parts(small_g, name="sum_small_grads"), small_shapes)
    dmod_all = _unpack(small_g, small_shapes[:1], lead=(N_DEV,))[0]
    sg = dict(zip(SMALL, small_sum))
    sg["gdn_conv_w"] = lax.dynamic_slice_in_dim(sg["gdn_conv_w"], me * gdn_conv_w.shape[1], gdn_conv_w.shape[1], 1)
    sg["mla_q_norm_g"] = lax.dynamic_slice_in_dim(sg["mla_q_norm_g"], me * mla_q_norm_g.shape[1],
                                                  mla_q_norm_g.shape[1], 1)
    sg["mla_kv_norm_g"] = lax.dynamic_slice_in_dim(sg["mla_kv_norm_g"], me * mla_kv_norm_g.shape[1],
                                                   mla_kv_norm_g.shape[1], 1)

    res = {}
    dmod_cols = jnp.transpose(lax.dynamic_slice_in_dim(dmod_all, me * wc, wc, axis=2), (1, 0, 2))
    res["ada_w"] = _ada_grad_adamw(c_all, dmod_cols, ada_w, m_ada_w, v_ada_w, name="ada_w_grad_adamw")
    for k in BIG:
        res[k] = _adamw(parts[k], W[k], M[k], V[k], name=f"adamw_{k}")
    shapes = [W[k].shape for k in SMALL]
    packed = [_pack([d[k] for k in SMALL]) for d in (sg, W, M, V)]
    outs = _adamw([packed[0][None]], packed[1][None], packed[2][None], packed[3][None], name="adamw_small")
    unpacked = [_unpack(o[0], shapes) for o in outs]
    for i, k in enumerate(SMALL):
        res[k] = tuple(u[i] for u in unpacked)

    return (loss, dx[None], *[res[k][0] for k in WEIGHTS], *[res[k][1] for k in WEIGHTS],
            *[res[k][2] for k in WEIGHTS], *[res[k][3] for k in WEIGHTS])
```

```python
import functools
import math

import jax
import jax.numpy as jnp
from jax import lax
from jax.experimental import pallas as pl
from jax.experimental.pallas import tpu as pltpu

F32 = jnp.float32
BF16 = jnp.bfloat16
MXU_DTYPE = jnp.bfloat16

N_DEV = 8
D_MODEL = 1024
DEPTH = 4
GDN_HEADS = 8
GDN_HEAD_DIM = 128
GDN_KEY_DIM = GDN_HEADS * GDN_HEAD_DIM
GDN_CHUNK = 64
GDN_CONV = 4
GDN_MAIN = 4 * GDN_KEY_DIM
MLA_HEADS = 8
MLA_NOPE = 128
MLA_ROPE = 64
MLA_V = 128
MLA_Q_RANK = 384
MLA_KV_RANK = 256
MLA_IN = MLA_Q_RANK + MLA_KV_RANK + MLA_ROPE
MLA_QK = MLA_NOPE + MLA_ROPE
ROPE_THETA = 10000.0
D_FF = 2816
N_MOD = 6
EPS = 1e-6
LANES = 128
VMEM_LIMIT = 48 * 1024 * 1024

ADAM_LR = 0.001
ADAM_B1 = 0.9
ADAM_B2 = 0.999
ADAM_EPS = 1e-08
ADAM_WD = 0.01
ADAM_STEP = 10
ADAM_BC1 = 1.0 - ADAM_B1 ** ADAM_STEP
ADAM_BC2 = 1.0 - ADAM_B2 ** ADAM_STEP

NN = (((1,), (0,)), ((), ()))
NT = (((1,), (1,)), ((), ()))
TN = (((0,), (0,)), ((), ()))
NEG = -1e30


def _dotb(a, b, dims):
    return lax.dot_general(a.astype(MXU_DTYPE), b.astype(MXU_DTYPE), dims, preferred_element_type=F32)


def _dotf(a, b, dims):
    return lax.dot_general(a, b, dims, precision=lax.Precision.HIGHEST, preferred_element_type=F32)


def _params(*sem):
    return pltpu.CompilerParams(dimension_semantics=sem, vmem_limit_bytes=VMEM_LIMIT)


def _pick(n, pref, mult=LANES):
    best = None
    t = mult
    while t <= min(n, pref):
        if n % t == 0:
            best = t
        t += mult
    return best if best is not None else n


def _sigmoid(z):
    return 1.0 / (1.0 + jnp.exp(-z))


def _exchange(arrays, *, scatter, name):
    n = len(arrays)
    out_shape = tuple(
        jax.ShapeDtypeStruct(a.shape if scatter else (N_DEV,) + a.shape, a.dtype) for a in arrays)

    def body(*refs):
        ins, outs = refs[:n], refs[n:2 * n]
        send_sems, recv_sems, local_sems = refs[2 * n:]
        x, y, c = lax.axis_index("x"), lax.axis_index("y"), lax.axis_index("c")
        me = 4 * x + 2 * y + c
        copies = []
        for k in range(n):
            src_own = ins[k].at[me] if scatter else ins[k]
            own = pltpu.make_async_copy(src_own, outs[k].at[me], local_sems.at[k])
            own.start()
            copies.append(own)
        sends = []
        for p in range(1, N_DEV):
            px, py, pc = x ^ ((p >> 2) & 1), y ^ ((p >> 1) & 1), c ^ (p & 1)
            peer = 4 * px + 2 * py + pc
            for k in range(n):
                cp = pltpu.make_async_remote_copy(
                    src_ref=ins[k].at[peer] if scatter else ins[k],
                    dst_ref=outs[k].at[me],
                    send_sem=send_sems.at[k, p - 1],
                    recv_sem=recv_sems.at[k, p - 1],
                    device_id=(px, py, pc),
                    device_id_type=pl.DeviceIdType.MESH,
                )
                cp.start()
                sends.append((cp, k, peer, p))
        for cp, k, peer, p in sends:
            pltpu.make_async_remote_copy(
                src_ref=ins[k].at[peer] if scatter else ins[k],
                dst_ref=outs[k].at[peer],
                send_sem=send_sems.at[k, p - 1],
                recv_sem=recv_sems.at[k, p - 1],
                device_id=(x, y, c),
                device_id_type=pl.DeviceIdType.MESH,
            ).wait_recv()
        for cp, _, _, _ in sends:
            cp.wait_send()
        for own in copies:
            own.wait()

    any_spec = pl.BlockSpec(memory_space=pl.ANY)
    outs = pl.pallas_call(
        body,
        name=name,
        out_shape=out_shape,
        in_specs=[any_spec] * n,
        out_specs=tuple([any_spec] * n),
        scratch_shapes=[
            pltpu.SemaphoreType.DMA((n, N_DEV - 1)),
            pltpu.SemaphoreType.DMA((n, N_DEV - 1)),
            pltpu.SemaphoreType.DMA((n,)),
        ],
        compiler_params=pltpu.CompilerParams(has_side_effects=True),
    )(*arrays)
    return list(outs)


def _peer(x, y, c, p):
    return x ^ ((p >> 2) & 1), y ^ ((p >> 1) & 1), c ^ (p & 1)


def _exchange_start(arrays, *, scatter, name, dep=None):
    n = len(arrays)
    deps = [] if dep is None else [dep]
    lands = [lax.empty(a.shape if scatter else (N_DEV,) + a.shape, a.dtype) for a in arrays]

    def body(*refs):
        ins, zones = refs[:n], refs[n:2 * n]
        send_sems, recv_sems = refs[2 * n + len(deps)], refs[2 * n + len(deps) + 1]
        token = refs[-1]
        x, y, c = lax.axis_index("x"), lax.axis_index("y"), lax.axis_index("c")
        me = 4 * x + 2 * y + c
        for p in range(1, N_DEV):
            px, py, pc = _peer(x, y, c, p)
            for k in range(n):
                pltpu.make_async_remote_copy(
                    src_ref=ins[k].at[4 * px + 2 * py + pc] if scatter else ins[k],
                    dst_ref=zones[k].at[me],
                    send_sem=send_sems.at[k * (N_DEV - 1) + p - 1],
                    recv_sem=recv_sems.at[k * (N_DEV - 1) + p - 1],
                    device_id=(px, py, pc),
                    device_id_type=pl.DeviceIdType.MESH,
                ).start()
        token[...] = jnp.zeros_like(token)

    hbm = pl.BlockSpec(memory_space=pltpu.HBM)
    sem = pl.BlockSpec(memory_space=pltpu.SEMAPHORE)
    outs = pl.pallas_call(
        body,
        name=name,
        out_shape=(pltpu.SemaphoreType.DMA((n * (N_DEV - 1),)), pltpu.SemaphoreType.DMA((n * (N_DEV - 1),)),
                   *[pltpu.HBM(a.shape, a.dtype) for a in arrays], *[pltpu.HBM(z.shape, z.dtype) for z in lands],
                   jax.ShapeDtypeStruct((8, LANES), F32)),
        in_specs=[hbm] * (2 * n) + [pl.BlockSpec(memory_space=pl.ANY)] * len(deps),
        out_specs=(sem, sem, *[hbm] * (2 * n), pl.BlockSpec(memory_space=pltpu.VMEM)),
        input_output_aliases={k: 2 + k for k in range(2 * n)},
        compiler_params=pltpu.CompilerParams(has_side_effects=pltpu.SideEffectType.DATAFLOW_SIDE_EFFECTING),
    )(*[pltpu.with_memory_space_constraint(a, pltpu.HBM) for a in arrays],
      *[pltpu.with_memory_space_constraint(z, pltpu.HBM) for z in lands], *deps)
    return outs[0], outs[1], list(outs[2:2 + n]), list(outs[2 + n:2 + 2 * n]), outs[-1]


def _exchange_wait(started, after, *, scatter, name):
    send_sems, recv_sems, srcs, lands, _ = started
    n = len(srcs)

    def body(*refs):
        ins, zones = refs[:n], refs[n:2 * n]
        s_sems, r_sems = refs[2 * n], refs[2 * n + 1]
        x, y, c = lax.axis_index("x"), lax.axis_index("y"), lax.axis_index("c")
        for p in range(1, N_DEV):
            px, py, pc = _peer(x, y, c, p)
            peer = 4 * px + 2 * py + pc
            for k in range(n):
                cp = pltpu.make_async_remote_copy(
                    src_ref=ins[k].at[peer] if scatter else ins[k],
                    dst_ref=zones[k].at[peer],
                    send_sem=s_sems.at[k * (N_DEV - 1) + p - 1],
                    recv_sem=r_sems.at[k * (N_DEV - 1) + p - 1],
                    device_id=(px, py, pc),
                    device_id_type=pl.DeviceIdType.MESH,
                )
                cp.wait_send()
                cp.wait_recv()

    hbm = pl.BlockSpec(memory_space=pltpu.HBM)
    sem = pl.BlockSpec(memory_space=pltpu.SEMAPHORE)
    outs = pl.pallas_call(
        body,
        name=name,
        out_shape=tuple(pltpu.HBM(a.shape, a.dtype) for a in srcs + lands),
        in_specs=[hbm] * (2 * n) + [sem, sem, pl.BlockSpec(memory_space=pl.ANY)],
        out_specs=tuple([hbm] * (2 * n)),
        input_output_aliases={k: k for k in range(2 * n)},
        compiler_params=pltpu.CompilerParams(has_side_effects=pltpu.SideEffectType.DATAFLOW_SIDE_EFFECTING),
    )(*srcs, *lands, send_sems, recv_sems, after)
    return list(outs[:n]), list(outs[n:])


def _mm(a, b, *, mode, out_dtype, name, add=None, tm=512, tn=512):
    if mode == "nn":
        (m, kd), (_, nd) = a.shape, b.shape
    elif mode == "nt":
        (m, kd), (nd, _) = a.shape, b.shape
    else:
        (kd, m), (_, nd) = a.shape, b.shape
    tm = _pick(m, tm, LANES if mode == "tn" else 16)
    tn = _pick(nd, tn)
    dims = {"nn": NN, "nt": NT, "tn": TN}[mode]
    a_spec = pl.BlockSpec((kd, tm), lambda i, j: (0, i)) if mode == "tn" else pl.BlockSpec((tm, kd), lambda i, j: (i, 0))
    b_spec = pl.BlockSpec((tn, kd), lambda i, j: (j, 0)) if mode == "nt" else pl.BlockSpec((kd, tn), lambda i, j: (0, j))
    o_spec = pl.BlockSpec((tm, tn), lambda i, j: (i, j))
    has_add = add is not None

    def body(*refs):
        a_ref, b_ref = refs[0], refs[1]
        o_ref = refs[-1]
        acc = _dotb(a_ref[...], b_ref[...], dims)
        if has_add:
            acc = acc + refs[2][...].astype(F32)
        o_ref[...] = acc.astype(o_ref.dtype)

    ins = [a, b] + ([add] if has_add else [])
    specs = [a_spec, b_spec] + ([o_spec] if has_add else [])
    return pl.pallas_call(
        body, name=name, grid=(m // tm, nd // tn), in_specs=specs, out_specs=o_spec,
        out_shape=jax.ShapeDtypeStruct((m, nd), out_dtype),
        compiler_params=_params("parallel", "parallel"),
    )(*ins)


def _mm_resid(a, b, x, gate, *, name, tm=512, tn=512):
    m, kd = a.shape
    nd = b.shape[1]
    tm = _pick(m, tm, 16)
    tn = _pick(nd, tn)
    o_spec = pl.BlockSpec((tm, tn), lambda i, j: (i, j))

    def body(a_ref, b_ref, x_ref, g_ref, xo_ref, y_ref):
        y = _dotb(a_ref[...], b_ref[...], NN)
        y_ref[...] = y
        xo_ref[...] = x_ref[...] + g_ref[...] * y

    return pl.pallas_call(
        body, name=name, grid=(m // tm, nd // tn),
        in_specs=[pl.BlockSpec((tm, kd), lambda i, j: (i, 0)), pl.BlockSpec((kd, tn), lambda i, j: (0, j)),
                  o_spec, pl.BlockSpec((1, tn), lambda i, j: (0, j))],
        out_specs=(o_spec, o_spec),
        out_shape=(jax.ShapeDtypeStruct((m, nd), F32), jax.ShapeDtypeStruct((m, nd), F32)),
        compiler_params=_params("parallel", "parallel"),
    )(a, b, x, gate)


ROWS = 256


def _row_spec(width, rows=ROWS):
    return pl.BlockSpec((rows, width), lambda i: (i, 0))


def _const_spec(shape):
    return pl.BlockSpec(shape, lambda i: tuple(0 for _ in shape))


def _adaln_fwd(x, g, scale, shift, *, name):
    t, d = x.shape

    def body(x_ref, g_ref, sc_ref, sh_ref, h_ref):
        xv = x_ref[...]
        r = lax.rsqrt(jnp.mean(xv * xv, axis=-1, keepdims=True) + EPS)
        h_ref[...] = (xv * r * g_ref[...] * (1.0 + sc_ref[...]) + sh_ref[...]).astype(h_ref.dtype)

    return pl.pallas_call(
        body, name=name, grid=(t // ROWS,),
        in_specs=[_row_spec(d), _const_spec((1, d)), _const_spec((1, d)), _const_spec((1, d))],
        out_specs=_row_spec(d), out_shape=jax.ShapeDtypeStruct((t, d), BF16),
        compiler_params=_params("parallel"),
    )(x, g, scale, shift)


def _adaln_bwd(x, g, scale, shift, dh, dres, *, name):
    t, d = x.shape

    def body(x_ref, g_ref, sc_ref, sh_ref, dh_ref, dr_ref, dx_ref, st_ref):
        @pl.when(pl.program_id(0) == 0)
        def _():
            st_ref[...] = jnp.zeros_like(st_ref)

        xv = x_ref[...]
        dhv = dh_ref[...].astype(F32)
        gv = g_ref[...]
        r = lax.rsqrt(jnp.mean(xv * xv, axis=-1, keepdims=True) + EPS)
        xh = xv * r
        nv = xh * gv
        dn = dhv * (1.0 + sc_ref[...])
        dxh = dn * gv
        dx_ref[...] = dr_ref[...] + r * (dxh - xh * jnp.mean(dxh * xh, axis=-1, keepdims=True))
        st_ref[0:1, :] += jnp.sum(dn * xh, axis=0, keepdims=True)
        st_ref[1:2, :] += jnp.sum(dhv * nv, axis=0, keepdims=True)
        st_ref[2:3, :] += jnp.sum(dhv, axis=0, keepdims=True)

    return pl.pallas_call(
        body, name=name, grid=(t // ROWS,),
        in_specs=[_row_spec(d), _const_spec((1, d)), _const_spec((1, d)), _const_spec((1, d)),
                  _row_spec(d), _row_spec(d)],
        out_specs=(_row_spec(d), _const_spec((8, d))),
        out_shape=(jax.ShapeDtypeStruct((t, d), F32), jax.ShapeDtypeStruct((8, d), F32)),
        compiler_params=_params("arbitrary"),
    )(x, g, scale, shift, dh, dres)


def _gate_bwd(dxo, y, gate, dep, *, name):
    t, d = dxo.shape

    def body(dx_ref, y_ref, g_ref, dep_ref, dy_ref, st_ref):
        @pl.when(pl.program_id(0) == 0)
        def _():
            st_ref[...] = jnp.zeros_like(st_ref)

        dxv = dx_ref[...]
        dy_ref[...] = (dxv * g_ref[...]).astype(dy_ref.dtype)
        st_ref[0:1, :] += jnp.sum(dxv * y_ref[...], axis=0, keepdims=True)

    return pl.pallas_call(
        body, name=name, grid=(t // ROWS,),
        in_specs=[_row_spec(d), _row_spec(d), _const_spec((1, d)), _const_spec((8, LANES))],
        out_specs=(_row_spec(d), _const_spec((8, d))),
        out_shape=(jax.ShapeDtypeStruct((t, d), BF16), jax.ShapeDtypeStruct((8, d), F32)),
        compiler_params=_params("arbitrary"),
    )(dxo, y, gate, dep)


def _loss_head(x, g, target, *, name):
    t, d = x.shape

    def body(x_ref, g_ref, t_ref, dx_ref, st_ref, ls_ref):
        @pl.when(pl.program_id(0) == 0)
        def _():
            st_ref[...] = jnp.zeros_like(st_ref)
            ls_ref[...] = jnp.zeros_like(ls_ref)

        xv = x_ref[...]
        gv = g_ref[...]
        r = lax.rsqrt(jnp.mean(xv * xv, axis=-1, keepdims=True) + EPS)
        xh = xv * r
        err = xh * gv - t_ref[...]
        ls_ref[...] += 0.5 * jnp.sum(jnp.mean(err * err, axis=-1, keepdims=True))
        dy = err * (1.0 / d)
        dxh = dy * gv
        dx_ref[...] = r * (dxh - xh * jnp.mean(dxh * xh, axis=-1, keepdims=True))
        st_ref[0:1, :] += jnp.sum(dy * xh, axis=0, keepdims=True)

    return pl.pallas_call(
        body, name=name, grid=(t // ROWS,),
        in_specs=[_row_spec(d), _const_spec((1, d)), _row_spec(d)],
        out_specs=(_row_spec(d), _const_spec((8, d)), _const_spec((8, LANES))),
        out_shape=(jax.ShapeDtypeStruct((t, d), F32), jax.ShapeDtypeStruct((8, d), F32),
                   jax.ShapeDtypeStruct((8, LANES), F32)),
        compiler_params=_params("arbitrary"),
    )(x, g, target)


FFN_BLOCK = D_FF // 2


def _gu_to_kernel_layout(wg, wu):
    parts = []
    for b in range(D_FF // FFN_BLOCK):
        sl = slice(b * FFN_BLOCK, (b + 1) * FFN_BLOCK)
        parts += [wg[..., sl], wu[..., sl]]
    return jnp.concatenate(parts, axis=-1)


def _gu_from_kernel_layout(w):
    nb = D_FF // FFN_BLOCK
    wg = jnp.concatenate([w[..., 2 * b * FFN_BLOCK:(2 * b + 1) * FFN_BLOCK] for b in range(nb)], axis=-1)
    wu = jnp.concatenate([w[..., (2 * b + 1) * FFN_BLOCK:(2 * b + 2) * FFN_BLOCK] for b in range(nb)], axis=-1)
    return wg, wu


def _swiglu_fwd(ab, *, name):
    t = ab.shape[0]
    tn = FFN_BLOCK

    def body(ab_ref, s_ref):
        a = ab_ref[:, 0:tn]
        s_ref[...] = (a * _sigmoid(a) * ab_ref[:, tn:2 * tn]).astype(s_ref.dtype)

    return pl.pallas_call(
        body, name=name, grid=(t // ROWS, D_FF // tn),
        in_specs=[pl.BlockSpec((ROWS, 2 * tn), lambda i, j: (i, j))],
        out_specs=pl.BlockSpec((ROWS, tn), lambda i, j: (i, j)),
        out_shape=jax.ShapeDtypeStruct((t, D_FF), BF16),
        compiler_params=_params("parallel", "parallel"),
    )(ab)


def _swiglu_bwd(ab, ds, *, name):
    t = ab.shape[0]
    tn = FFN_BLOCK

    def body(ab_ref, ds_ref, d_ref):
        a = ab_ref[:, 0:tn]
        dsv = ds_ref[...]
        sg = _sigmoid(a)
        d_ref[:, 0:tn] = (dsv * ab_ref[:, tn:2 * tn] * sg * (1.0 + a * (1.0 - sg))).astype(d_ref.dtype)
        d_ref[:, tn:2 * tn] = (dsv * a * sg).astype(d_ref.dtype)

    return pl.pallas_call(
        body, name=name, grid=(t // ROWS, D_FF // tn),
        in_specs=[pl.BlockSpec((ROWS, 2 * tn), lambda i, j: (i, j)), pl.BlockSpec((ROWS, tn), lambda i, j: (i, j))],
        out_specs=pl.BlockSpec((ROWS, 2 * tn), lambda i, j: (i, j)),
        out_shape=jax.ShapeDtypeStruct((t, 2 * D_FF), BF16),
        compiler_params=_params("parallel", "parallel"),
    )(ab, ds)


def _shift_rows(v, s, rows):
    if s == 0:
        return v
    return jnp.where(rows >= s, pltpu.roll(v, s, 0), 0.0)


def _unshift_rows(v, s, rows, t):
    if s == 0:
        return v
    return jnp.where(rows < t - s, pltpu.roll(v, t - s, 0), 0.0)


def _conv_silu(x, w, rows):
    z = w[GDN_CONV - 1:GDN_CONV, :] * x
    for j in range(GDN_CONV - 1):
        z = z + w[j:j + 1, :] * _shift_rows(x, GDN_CONV - 1 - j, rows)
    sg = _sigmoid(z)
    return z, sg, z * sg


def _gdn_prep_fwd(proj, conv_wt, *, name):
    t = proj.shape[0]
    nh = GDN_HEADS

    def body(x_ref, w_ref, y_ref):
        j = pl.program_id(0)
        rows = lax.broadcasted_iota(jnp.int32, (t, LANES), 0)
        _, _, s = _conv_silu(x_ref[...], w_ref[...], rows)
        rs = lax.rsqrt(jnp.sum(s * s, axis=-1, keepdims=True) + EPS)
        qscale = jnp.where(j < nh, GDN_HEAD_DIM ** -0.5, 1.0)
        y_ref[...] = jnp.where(j < 2 * nh, s * rs * qscale, s)

    return pl.pallas_call(
        body, name=name, grid=(3 * nh,),
        in_specs=[pl.BlockSpec((t, LANES), lambda j: (0, j)), pl.BlockSpec((GDN_CONV, LANES), lambda j: (0, j))],
        out_specs=pl.BlockSpec((t, LANES), lambda j: (0, j)),
        out_shape=jax.ShapeDtypeStruct((t, 3 * GDN_KEY_DIM), F32),
        compiler_params=_params("parallel"),
    )(proj, conv_wt)


def _gdn_prep_bwd(proj, conv_wt, dy, *, name):
    t = proj.shape[0]
    nh = GDN_HEADS

    def body(x_ref, w_ref, dy_ref, dx_ref, dw_ref):
        j = pl.program_id(0)
        rows = lax.broadcasted_iota(jnp.int32, (t, LANES), 0)
        x = x_ref[...]
        w = w_ref[...]
        z, sg, s = _conv_silu(x, w, rows)
        rs = lax.rsqrt(jnp.sum(s * s, axis=-1, keepdims=True) + EPS)
        qscale = jnp.where(j < nh, GDN_HEAD_DIM ** -0.5, 1.0)
        dyv = dy_ref[...]
        nv = s * rs
        de = dyv * qscale
        ds_qk = rs * (de - nv * jnp.sum(de * nv, axis=-1, keepdims=True))
        ds = jnp.where(j < 2 * nh, ds_qk, dyv)
        dz = ds * sg * (1.0 + z * (1.0 - sg))
        dx = w[GDN_CONV - 1:GDN_CONV, :] * dz
        dw_ref[GDN_CONV - 1:GDN_CONV, :] = jnp.sum(dz * x, axis=0, keepdims=True)
        for k in range(GDN_CONV - 1):
            sh = GDN_CONV - 1 - k
            dx = dx + w[k:k + 1, :] * _unshift_rows(dz, sh, rows, t)
            dw_ref[k:k + 1, :] = jnp.sum(dz * _shift_rows(x, sh, rows), axis=0, keepdims=True)
        dx_ref[...] = dx.astype(dx_ref.dtype)

    return pl.pallas_call(
        body, name=name, grid=(3 * nh,),
        in_specs=[pl.BlockSpec((t, LANES), lambda j: (0, j)), pl.BlockSpec((GDN_CONV, LANES), lambda j: (0, j)),
                  pl.BlockSpec((t, LANES), lambda j: (0, j))],
        out_specs=(pl.BlockSpec((t, LANES), lambda j: (0, j)), pl.BlockSpec((GDN_CONV, LANES), lambda j: (0, j))),
        out_shape=(jax.ShapeDtypeStruct((t, 3 * GDN_KEY_DIM), BF16),
                   jax.ShapeDtypeStruct((GDN_CONV, 3 * GDN_KEY_DIM), F32)),
        compiler_params=_params("parallel"),
    )(proj, conv_wt, dy)


def _softplus(z):
    return jnp.maximum(z, 0.0) + jnp.log(1.0 + jnp.exp(-jnp.abs(z)))


def _gdn_gate_fwd(ab, prm, *, name):
    t = ab.shape[0]

    def body(ab_ref, p_ref, o_ref):
        v = ab_ref[...]
        lane = lax.broadcasted_iota(jnp.int32, v.shape, 1)
        g = -jnp.exp(p_ref[0:1, :]) * _softplus(v + p_ref[1:2, :])
        o_ref[...] = jnp.where(lane < GDN_HEADS, g, jnp.where(lane < 2 * GDN_HEADS, _sigmoid(v), 0.0))

    return pl.pallas_call(
        body, name=name, grid=(t // ROWS,),
        in_specs=[_row_spec(LANES), _const_spec((8, LANES))], out_specs=_row_spec(LANES),
        out_shape=jax.ShapeDtypeStruct((t, LANES), F32), compiler_params=_params("parallel"),
    )(ab, prm)


def _gdn_gate_bwd(ab, prm, dgb, *, name):
    t = ab.shape[0]

    def body(ab_ref, p_ref, d_ref, o_ref, st_ref):
        @pl.when(pl.program_id(0) == 0)
        def _():
            st_ref[...] = jnp.zeros_like(st_ref)

        v = ab_ref[...]
        dv = d_ref[...]
        lane = lax.broadcasted_iota(jnp.int32, v.shape, 1)
        is_a = lane < GDN_HEADS
        is_b = jnp.logical_and(lane >= GDN_HEADS, lane < 2 * GDN_HEADS)
        a_exp = jnp.exp(p_ref[0:1, :])
        zz = v + p_ref[1:2, :]
        g = -a_exp * _softplus(zz)
        da = dv * (-a_exp) * _sigmoid(zz)
        beta = _sigmoid(v)
        db = dv * beta * (1.0 - beta)
        o_ref[...] = jnp.where(is_a, da, jnp.where(is_b, db, 0.0)).astype(o_ref.dtype)
        st_ref[0:1, :] += jnp.sum(jnp.where(is_a, dv * g, 0.0), axis=0, keepdims=True)
        st_ref[1:2, :] += jnp.sum(jnp.where(is_a, da, 0.0), axis=0, keepdims=True)

    return pl.pallas_call(
        body, name=name, grid=(t // ROWS,),
        in_specs=[_row_spec(LANES), _const_spec((8, LANES)), _row_spec(LANES)],
        out_specs=(_row_spec(LANES), _const_spec((8, LANES))),
        out_shape=(jax.ShapeDtypeStruct((t, LANES), BF16), jax.ShapeDtypeStruct((8, LANES), F32)),
        compiler_params=_params("arbitrary"),
    )(ab, prm, dgb)


def _gdn_local(q, k, v, gb, bb):
    cs = q.shape[0]
    r = lax.broadcasted_iota(jnp.int32, (cs, cs), 0)
    c = lax.broadcasted_iota(jnp.int32, (cs, cs), 1)
    tril, strict, eye = r >= c, r > c, r == c
    g_colb = gb[:, :cs]
    g_row = jnp.sum(jnp.where(eye, g_colb, 0.0), axis=0, keepdims=True)
    gc_col = jnp.sum(jnp.where(tril, g_row, 0.0), axis=1, keepdims=True)
    gc_row = jnp.sum(jnp.where(r <= c, g_colb, 0.0), axis=0, keepdims=True)
    decay = jnp.exp(jnp.where(tril, gc_col - gc_row, NEG))
    gamma = jnp.exp(gc_col)
    gcl = gc_col[cs - 1:cs, :]
    gl = jnp.exp(gcl)
    kdec = jnp.exp(gcl - gc_col)
    kb = k * bb
    lmat = jnp.where(strict, _dotb(kb, k, NT) * decay, 0.0)
    xm = -lmat
    tinv = jnp.where(eye, 1.0, 0.0) + xm
    for _ in range(int(math.log2(cs)) - 1):
        xm = _dotf(xm, xm, NN)
        tinv = tinv + _dotf(tinv, xm, NN)
    vb = v * bb
    kg = kb * gamma
    u = _dotf(tinv, vb, NN)
    w = _dotf(tinv, kg, NN)
    pmat = jnp.where(tril, _dotb(q, k, NT) * decay, 0.0)
    return dict(tril=tril, strict=strict, eye=eye, r=r, c=c, decay=decay, gamma=gamma, gl=gl, kdec=kdec,
                kb=kb, lmat=lmat, tinv=tinv, vb=vb, kg=kg, u=u, w=w, pmat=pmat, qd=q * gamma, kd=k * kdec)


def _gdn_chunk_fwd(qkv, gbc, bbc, *, name):
    t = qkv.shape[0]
    nh, cs, hd = GDN_HEADS, GDN_CHUNK, GDN_HEAD_DIM
    nc = t // cs

    def body(q_ref, k_ref, v_ref, g_ref, b_ref, o_ref, st_ref, s_ref):
        @pl.when(pl.program_id(1) == 0)
        def _():
            s_ref[...] = jnp.zeros_like(s_ref)

        lo = _gdn_local(q_ref[...], k_ref[...], v_ref[...], g_ref[0], b_ref[0])
        s = s_ref[...]
        st_ref[0, 0] = s
        vn = lo["u"] - _dotb(lo["w"], s, NN)
        o_ref[...] = _dotb(lo["qd"], s, NN) + _dotb(lo["pmat"], vn, NN)
        s_ref[...] = s * lo["gl"] + _dotb(lo["kd"], vn, TN)

    gspec = pl.BlockSpec((1, cs, LANES), lambda h, n: (h, n, 0))
    return pl.pallas_call(
        body, name=name, grid=(nh, nc),
        in_specs=[pl.BlockSpec((cs, hd), lambda h, n: (n, h)), pl.BlockSpec((cs, hd), lambda h, n: (n, nh + h)),
                  pl.BlockSpec((cs, hd), lambda h, n: (n, 2 * nh + h)), gspec, gspec],
        out_specs=(pl.BlockSpec((cs, hd), lambda h, n: (n, h)),
                   pl.BlockSpec((1, 1, hd, hd), lambda h, n: (h, n, 0, 0))),
        out_shape=(jax.ShapeDtypeStruct((t, nh * hd), F32), jax.ShapeDtypeStruct((nh, nc, hd, hd), F32)),
        scratch_shapes=[pltpu.VMEM((hd, hd), F32)],
        compiler_params=_params("parallel", "arbitrary"),
    )(qkv, qkv, qkv, gbc, bbc)


def _gdn_chunk_bwd(qkv, gbc, bbc, states, do, *, name):
    t = qkv.shape[0]
    nh, cs, hd = GDN_HEADS, GDN_CHUNK, GDN_HEAD_DIM
    nc = t // cs

    def body(q_ref, k_ref, v_ref, g_ref, b_ref, st_ref, do_ref, dq_ref, dk_ref, dv_ref, dg_ref, db_ref, ds_ref):
        @pl.when(pl.program_id(1) == 0)
        def _():
            ds_ref[...] = jnp.zeros_like(ds_ref)

        q, k, v, bb = q_ref[...], k_ref[...], v_ref[...], b_ref[0]
        lo = _gdn_local(q, k, v, g_ref[0], bb)
        tril, strict, eye, r, c = lo["tril"], lo["strict"], lo["eye"], lo["r"], lo["c"]
        decay, gamma, gl, kdec = lo["decay"], lo["gamma"], lo["gl"], lo["kdec"]
        kb, tinv, w, pmat = lo["kb"], lo["tinv"], lo["w"], lo["pmat"]
        s = st_ref[0, 0]
        dsn = ds_ref[...]
        dov = do_ref[...]
        vn = lo["u"] - _dotb(w, s, NN)
        dvn = _dotb(pmat, dov, TN) + _dotb(lo["kd"], dsn, NN)
        dp = jnp.where(tril, _dotb(dov, vn, NT), 0.0)
        dqd = _dotb(dov, s, NT)
        dkd = _dotb(vn, dsn, NT)
        dgl = jnp.sum(jnp.sum(dsn * s, axis=1, keepdims=True), axis=0, keepdims=True)
        dw = -_dotb(dvn, s, NT)
        ds_ref[...] = gl * dsn + _dotb(lo["qd"], dov, TN) - _dotb(w, dvn, TN)
        dvb = _dotf(tinv, dvn, TN)
        dkg = _dotf(tinv, dw, TN)
        dt = _dotf(dvn, lo["vb"], NT) + _dotf(dw, lo["kg"], NT)
        dl = jnp.where(strict, -_dotf(_dotf(tinv, dt, TN), tinv, NT), 0.0)
        dkk = dl * decay
        dqk = dp * decay
        dkb = _dotb(dkk, k, NN) + dkg * gamma
        dk = _dotb(dkk, kb, TN) + _dotb(dqk, q, TN) + dkd * kdec + dkb * bb
        dq_ref[...] = _dotb(dqk, k, NN) + dqd * gamma
        dk_ref[...] = dk
        dv_ref[...] = dvb * bb
        db_ref[0] = jnp.broadcast_to(
            jnp.sum(dvb * v, axis=-1, keepdims=True) + jnp.sum(dkb * k, axis=-1, keepdims=True), (cs, LANES))
        e = dl * lo["lmat"] + dp * pmat
        e_col = jnp.sum(e, axis=0, keepdims=True)
        dgc = jnp.sum(e, axis=1, keepdims=True) - jnp.sum(jnp.where(eye, e_col, 0.0), axis=1, keepdims=True)
        dgamma = jnp.sum(dqd * q, axis=-1, keepdims=True) + jnp.sum(dkg * kb, axis=-1, keepdims=True)
        rk = jnp.sum(dkd * k, axis=-1, keepdims=True) * kdec
        dgcl = jnp.sum(rk, axis=0, keepdims=True) + dgl * gl
        rowi = lax.broadcasted_iota(jnp.int32, (cs, 1), 0)
        dgc = dgc + dgamma * gamma - rk + jnp.where(rowi == cs - 1, dgcl, 0.0)
        dgc_row = jnp.sum(jnp.where(eye, dgc, 0.0), axis=0, keepdims=True)
        dg = jnp.sum(jnp.where(c >= r, dgc_row, 0.0), axis=1, keepdims=True)
        dg_ref[0] = jnp.broadcast_to(dg, (cs, LANES))

    gspec = pl.BlockSpec((1, cs, LANES), lambda h, n: (h, nc - 1 - n, 0))
    col = lambda off: pl.BlockSpec((cs, hd), lambda h, n: (nc - 1 - n, off + h))
    return pl.pallas_call(
        body, name=name, grid=(nh, nc),
        in_specs=[col(0), col(nh), col(2 * nh), gspec, gspec,
                  pl.BlockSpec((1, 1, hd, hd), lambda h, n: (h, nc - 1 - n, 0, 0)), col(0)],
        out_specs=(col(0), col(0), col(0), gspec, gspec),
        out_shape=(jax.ShapeDtypeStruct((t, nh * hd), F32),) * 3
        + (jax.ShapeDtypeStruct((nh, t, LANES), F32),) * 2,
        scratch_shapes=[pltpu.VMEM((hd, hd), F32)],
        compiler_params=_params("parallel", "arbitrary"),
    )(qkv, qkv, qkv, gbc, bbc, states, do)


def _gdn_onorm_fwd(o, proj, norm_g, *, name):
    t = o.shape[0]
    w = GDN_KEY_DIM
    goff = 3 * GDN_KEY_DIM // w

    def body(o_ref, gp_ref, g_ref, y_ref):
        gv = g_ref[...]
        for h in range(GDN_HEADS):
            sl = slice(h * GDN_HEAD_DIM, (h + 1) * GDN_HEAD_DIM)
            oh = o_ref[:, sl]
            gp = gp_ref[:, sl]
            r = lax.rsqrt(jnp.mean(oh * oh, axis=-1, keepdims=True) + EPS)
            y_ref[:, sl] = (oh * r * gv * gp * _sigmoid(gp)).astype(y_ref.dtype)

    return pl.pallas_call(
        body, name=name, grid=(t // ROWS,),
        in_specs=[_row_spec(w), pl.BlockSpec((ROWS, w), lambda i: (i, goff)), _const_spec((1, GDN_HEAD_DIM))],
        out_specs=_row_spec(w), out_shape=jax.ShapeDtypeStruct((t, w), BF16),
        compiler_params=_params("parallel"),
    )(o, proj, norm_g)


def _gdn_onorm_bwd(o, proj, norm_g, dy, *, name):
    t = o.shape[0]
    w = GDN_KEY_DIM
    goff = 3 * GDN_KEY_DIM // w

    def body(o_ref, gp_ref, g_ref, dy_ref, do_ref, dgp_ref, st_ref):
        @pl.when(pl.program_id(0) == 0)
        def _():
            st_ref[...] = jnp.zeros_like(st_ref)

        gv = g_ref[...]
        acc = jnp.zeros((1, GDN_HEAD_DIM), F32)
        for h in range(GDN_HEADS):
            sl = slice(h * GDN_HEAD_DIM, (h + 1) * GDN_HEAD_DIM)
            oh = o_ref[:, sl]
            gp = gp_ref[:, sl]
            dyv = dy_ref[:, sl].astype(F32)
            r = lax.rsqrt(jnp.mean(oh * oh, axis=-1, keepdims=True) + EPS)
            xh = oh * r
            sg = _sigmoid(gp)
            dn = dyv * gp * sg
            dgp_ref[:, sl] = (dyv * xh * gv * sg * (1.0 + gp * (1.0 - sg))).astype(dgp_ref.dtype)
            acc = acc + jnp.sum(dn * xh, axis=0, keepdims=True)
            dxh = dn * gv
            do_ref[:, sl] = r * (dxh - xh * jnp.mean(dxh * xh, axis=-1, keepdims=True))
        st_ref[0:1, :] += acc

    return pl.pallas_call(
        body, name=name, grid=(t // ROWS,),
        in_specs=[_row_spec(w), pl.BlockSpec((ROWS, w), lambda i: (i, goff)), _const_spec((1, GDN_HEAD_DIM)),
                  _row_spec(w)],
        out_specs=(_row_spec(w), _row_spec(w), _const_spec((8, GDN_HEAD_DIM))),
        out_shape=(jax.ShapeDtypeStruct((t, w), F32), jax.ShapeDtypeStruct((t, w), BF16),
                   jax.ShapeDtypeStruct((8, GDN_HEAD_DIM), F32)),
        compiler_params=_params("arbitrary"),
    )(o, proj, norm_g, dy)


def _mla_prep_fwd(proj, qg, kvg, *, name):
    t = proj.shape[0]
    q1, k1 = MLA_Q_RANK, MLA_Q_RANK + MLA_KV_RANK

    def body(p_ref, qg_ref, kg_ref, cq_ref, ck_ref):
        cq = p_ref[:, 0:q1]
        ck = p_ref[:, q1:k1]
        cq_ref[...] = (cq * lax.rsqrt(jnp.mean(cq * cq, axis=-1, keepdims=True) + EPS) * qg_ref[...]).astype(BF16)
        ck_ref[...] = (ck * lax.rsqrt(jnp.mean(ck * ck, axis=-1, keepdims=True) + EPS) * kg_ref[...]).astype(BF16)

    return pl.pallas_call(
        body, name=name, grid=(t // ROWS,),
        in_specs=[_row_spec(MLA_IN), _const_spec((1, MLA_Q_RANK)), _const_spec((1, MLA_KV_RANK))],
        out_specs=(_row_spec(MLA_Q_RANK), _row_spec(MLA_KV_RANK)),
        out_shape=(jax.ShapeDtypeStruct((t, MLA_Q_RANK), BF16), jax.ShapeDtypeStruct((t, MLA_KV_RANK), BF16)),
        compiler_params=_params("parallel"),
    )(proj, qg, kvg)


def _mla_prep_bwd(proj, qg, kvg, dcq, dck, dkr, *, name):
    t = proj.shape[0]
    q1, k1 = MLA_Q_RANK, MLA_Q_RANK + MLA_KV_RANK

    def body(p_ref, qg_ref, kg_ref, dq_ref, dk_ref, dr_ref, dp_ref, st_ref):
        @pl.when(pl.program_id(0) == 0)
        def _():
            st_ref[...] = jnp.zeros_like(st_ref)

        for lo, hi, g_ref, d_ref in ((0, q1, qg_ref, dq_ref), (q1, k1, kg_ref, dk_ref)):
            xv = p_ref[:, lo:hi]
            dn = d_ref[...]
            r = lax.rsqrt(jnp.mean(xv * xv, axis=-1, keepdims=True) + EPS)
            xh = xv * r
            dxh = dn * g_ref[...]
            dp_ref[:, lo:hi] = (r * (dxh - xh * jnp.mean(dxh * xh, axis=-1, keepdims=True))).astype(dp_ref.dtype)
            st_ref[0:1, lo:hi] += jnp.sum(dn * xh, axis=0, keepdims=True)
        dp_ref[:, k1:MLA_IN] = dr_ref[...].astype(dp_ref.dtype)

    return pl.pallas_call(
        body, name=name, grid=(t // ROWS,),
        in_specs=[_row_spec(MLA_IN), _const_spec((1, MLA_Q_RANK)), _const_spec((1, MLA_KV_RANK)),
                  _row_spec(MLA_Q_RANK), _row_spec(MLA_KV_RANK), _row_spec(MLA_ROPE)],
        out_specs=(_row_spec(MLA_IN), _const_spec((8, MLA_IN))),
        out_shape=(jax.ShapeDtypeStruct((t, MLA_IN), BF16), jax.ShapeDtypeStruct((8, MLA_IN), F32)),
        compiler_params=_params("arbitrary"),
    )(proj, qg, kvg, dcq, dck, dkr)


def _rope(xr, cos_t, sin_t, *, name):
    t, w = xr.shape
    ns = w // LANES

    def body(x_ref, c_ref, s_ref, o_ref):
        cv, sv = c_ref[...], s_ref[...]
        lane = lax.broadcasted_iota(jnp.int32, (ROWS, LANES), 1)
        first = (lane % MLA_ROPE) < (MLA_ROPE // 2)
        for i in range(ns):
            sl = slice(i * LANES, (i + 1) * LANES)
            xv = x_ref[:, sl]
            sw = jnp.where(first, pltpu.roll(xv, LANES - MLA_ROPE // 2, 1), pltpu.roll(xv, MLA_ROPE // 2, 1))
            o_ref[:, sl] = xv * cv + sw * sv

    return pl.pallas_call(
        body, name=name, grid=(t // ROWS,),
        in_specs=[_row_spec(w), _row_spec(LANES), _row_spec(LANES)], out_specs=_row_spec(w),
        out_shape=jax.ShapeDtypeStruct((t, w), F32), compiler_params=_params("parallel"),
    )(xr, cos_t, sin_t)


def _rope_bwd(dr, cos_t, sin_t, *, name):
    t, w = dr.shape
    ns = w // LANES

    def body(d_ref, c_ref, s_ref, o_ref):
        cv, sv = c_ref[...], s_ref[...]
        lane = lax.broadcasted_iota(jnp.int32, (ROWS, LANES), 1)
        first = (lane % MLA_ROPE) < (MLA_ROPE // 2)
        for i in range(ns):
            sl = slice(i * LANES, (i + 1) * LANES)
            dv = d_ref[:, sl]
            ds = dv * sv
            sw = jnp.where(first, pltpu.roll(ds, LANES - MLA_ROPE // 2, 1), pltpu.roll(ds, MLA_ROPE // 2, 1))
            o_ref[:, sl] = dv * cv + sw

    return pl.pallas_call(
        body, name=name, grid=(t // ROWS,),
        in_specs=[_row_spec(w), _row_spec(LANES), _row_spec(LANES)], out_specs=_row_spec(w),
        out_shape=jax.ShapeDtypeStruct((t, w), F32), compiler_params=_params("parallel"),
    )(dr, cos_t, sin_t)


ATT_BLOCK = 256
ATT_SCALE = MLA_QK ** -0.5


def _causal_mask(i, j, blk):
    rows = i * blk + lax.broadcasted_iota(jnp.int32, (blk, blk), 0)
    cols = j * blk + lax.broadcasted_iota(jnp.int32, (blk, blk), 1)
    return cols <= rows


def _attn_fwd(q, k, v, *, name):
    nh, t, dk = q.shape
    dv = v.shape[-1]
    blk = min(ATT_BLOCK, t)

    def body(q_ref, k_ref, v_ref, o_ref, l_ref):
        i = pl.program_id(1)
        qv = q_ref[0]

        def step(j, carry):
            m, l, acc = carry
            off = pl.multiple_of(j * blk, blk)
            s = _dotb(qv, k_ref[0, pl.ds(off, blk), :], NT) * ATT_SCALE
            s = jnp.where(_causal_mask(i, j, blk), s, NEG)
            m_new = jnp.maximum(m, jnp.max(s, axis=-1, keepdims=True))
            p = jnp.exp(s - m_new)
            alpha = jnp.exp(m - m_new)
            l = alpha * l + jnp.sum(p, axis=-1, keepdims=True)
            acc = alpha * acc + _dotb(p, v_ref[0, pl.ds(off, blk), :], NN)
            return m_new, l, acc

        init = (jnp.full((blk, 1), NEG, F32), jnp.zeros((blk, 1), F32), jnp.zeros((blk, dv), F32))
        m, l, acc = lax.fori_loop(0, i + 1, step, init)
        o_ref[0] = acc / l
        l_ref[0] = jnp.broadcast_to(m + jnp.log(l), (blk, LANES))

    return pl.pallas_call(
        body, name=name, grid=(nh, t // blk),
        in_specs=[pl.BlockSpec((1, blk, dk), lambda h, i: (h, i, 0)), pl.BlockSpec((1, t, dk), lambda h, i: (h, 0, 0)),
                  pl.BlockSpec((1, t, dv), lambda h, i: (h, 0, 0))],
        out_specs=(pl.BlockSpec((1, blk, dv), lambda h, i: (h, i, 0)),
                   pl.BlockSpec((1, blk, LANES), lambda h, i: (h, i, 0))),
        out_shape=(jax.ShapeDtypeStruct((nh, t, dv), F32), jax.ShapeDtypeStruct((nh, t, LANES), F32)),
        compiler_params=_params("parallel", "parallel"),
    )(q, k, v)


def _attn_bwd(q, k, v, o, lse, do, *, name):
    nh, t, dk = q.shape
    dv = v.shape[-1]
    blk = min(ATT_BLOCK, t)
    nb = t // blk

    def body(q_ref, k_ref, v_ref, o_ref, l_ref, do_ref, dq_ref, dk_ref, dv_ref):
        j = pl.program_id(1)

        @pl.when(j == 0)
        def _():
            dq_ref[...] = jnp.zeros_like(dq_ref)

        kv, vv = k_ref[0], v_ref[0]

        def step(i, carry):
            dk_acc, dv_acc = carry
            off = pl.multiple_of(i * blk, blk)
            rows = pl.ds(off, blk)
            qv = q_ref[0, rows, :]
            dov = do_ref[0, rows, :]
            s = _dotb(qv, kv, NT) * ATT_SCALE
            s = jnp.where(_causal_mask(i, j, blk), s, NEG)
            p = jnp.exp(s - l_ref[0, rows, :][:, 0:1])
            dv_acc = dv_acc + _dotb(p, dov, TN)
            dp = _dotb(dov, vv, NT)
            delta = jnp.sum(dov * o_ref[0, rows, :], axis=-1, keepdims=True)
            ds = p * (dp - delta) * ATT_SCALE
            dk_acc = dk_acc + _dotb(ds, qv, TN)
            dq_ref[0, rows, :] += _dotb(ds, kv, NN)
            return dk_acc, dv_acc

        dk_acc, dv_acc = lax.fori_loop(j, nb, step, (jnp.zeros((blk, dk), F32), jnp.zeros((blk, dv), F32)))
        dk_ref[0] = dk_acc
        dv_ref[0] = dv_acc

    full = lambda w: pl.BlockSpec((1, t, w), lambda h, j: (h, 0, 0))
    part = lambda w: pl.BlockSpec((1, blk, w), lambda h, j: (h, j, 0))
    return pl.pallas_call(
        body, name=name, grid=(nh, nb),
        in_specs=[full(dk), part(dk), part(dv), full(dv), full(LANES), full(dv)],
        out_specs=(full(dk), part(dk), part(dv)),
        out_shape=(jax.ShapeDtypeStruct((nh, t, dk), F32), jax.ShapeDtypeStruct((nh, t, dk), F32),
                   jax.ShapeDtypeStruct((nh, t, dv), F32)),
        compiler_params=_params("parallel", "arbitrary"),
    )(q, k, v, o, lse, do)


def _ada_mod(c_all, ada_w, ada_b_cols, *, name):
    nl, d, wc = ada_w.shape

    def body(c_ref, w_ref, b_ref, o_ref):
        cv = c_ref[...]
        o_ref[0] = _dotb(cv * _sigmoid(cv), w_ref[0], NN) + b_ref[0]

    return pl.pallas_call(
        body, name=name, grid=(nl,),
        in_specs=[_const_spec((N_DEV, d)), pl.BlockSpec((1, d, wc), lambda l: (l, 0, 0)),
                  pl.BlockSpec((1, 1, wc), lambda l: (l, 0, 0))],
        out_specs=pl.BlockSpec((1, N_DEV, wc), lambda l: (l, 0, 0)),
        out_shape=jax.ShapeDtypeStruct((nl, N_DEV, wc), F32), compiler_params=_params("parallel"),
    )(c_all, ada_w, ada_b_cols)


def _adam_math(g, w, m, v):
    m2 = ADAM_B1 * m + (1.0 - ADAM_B1) * g
    v2 = ADAM_B2 * v + (1.0 - ADAM_B2) * (g * g)
    delta = -ADAM_LR * ((m2 / ADAM_BC1) / (jnp.sqrt(v2 / ADAM_BC2) + ADAM_EPS) + ADAM_WD * w)
    return delta, m2, v2


def _ada_grad_adamw(c_all, dmod_cols, w, m, v, *, name):
    nl, d, wc = w.shape
    tr = 256

    def body(c_ref, dm_ref, w_ref, m_ref, v_ref, g_ref, d_ref, m2_ref, v2_ref):
        cv = c_ref[...]
        g = _dotf(cv * _sigmoid(cv), dm_ref[0], TN)
        delta, m2, v2 = _adam_math(g, w_ref[0], m_ref[0], v_ref[0])
        g_ref[0], d_ref[0], m2_ref[0], v2_ref[0] = g, delta, m2, v2

    blk = pl.BlockSpec((1, tr, wc), lambda l, i: (l, i, 0))
    return pl.pallas_call(
        body, name=name, grid=(nl, d // tr),
        in_specs=[pl.BlockSpec((N_DEV, tr), lambda l, i: (0, i)), pl.BlockSpec((1, N_DEV, wc), lambda l, i: (l, 0, 0)),
                  blk, blk, blk],
        out_specs=(blk,) * 4, out_shape=(jax.ShapeDtypeStruct(w.shape, F32),) * 4,
        compiler_params=_params("parallel", "parallel"),
    )(c_all, dmod_cols, w, m, v)


def _adamw(parts, w, m, v, *, name):
    nl, r, c = w.shape
    ns = parts[0].shape[0]
    lanes_padded = -(-c // LANES) * LANES
    row_bytes = 2 * nl * ns * lanes_padded * parts[0].dtype.itemsize
    tr = _pick(r, min(256, max(16, (VMEM_LIMIT // 2) // row_bytes)), 16)

    def body(*refs):
        p_refs = refs[:nl]
        w_ref, m_ref, v_ref, g_ref, d_ref, m2_ref, v2_ref = refs[nl:]
        layer = pl.program_id(0)
        for q in range(nl):
            @pl.when(layer == q)
            def _(q=q):
                g = p_refs[q][0].astype(F32)
                for s in range(1, ns):
                    g = g + p_refs[q][s].astype(F32)
                delta, m2, v2 = _adam_math(g, w_ref[0], m_ref[0], v_ref[0])
                g_ref[0], d_ref[0], m2_ref[0], v2_ref[0] = g, delta, m2, v2

    blk = pl.BlockSpec((1, tr, c), lambda l, i: (l, i, 0))
    p_specs = [pl.BlockSpec((ns, tr, c), lambda l, i, q=q: (0, jnp.where(l == q, i, 0), 0)) for q in range(nl)]
    return pl.pallas_call(
        body, name=name, grid=(nl, r // tr),
        in_specs=p_specs + [blk, blk, blk],
        out_specs=(blk,) * 4, out_shape=(jax.ShapeDtypeStruct(w.shape, F32),) * 4,
        compiler_params=_params("arbitrary", "arbitrary"),
    )(*parts, w, m, v)


def _sum_parts(parts, *, name):
    ns, r, c = parts.shape

    def body(p_ref, o_ref):
        acc = p_ref[0]
        for s in range(1, ns):
            acc = acc + p_ref[s]
        o_ref[...] = acc

    return pl.pallas_call(
        body, name=name, out_shape=jax.ShapeDtypeStruct((r, c), F32),
        in_specs=[pl.BlockSpec(memory_space=pltpu.VMEM)], out_specs=pl.BlockSpec(memory_space=pltpu.VMEM),
    )(parts)


def _pack(arrs):
    flat = jnp.concatenate([a.reshape(-1).astype(F32) for a in arrs])
    pad = (-flat.shape[0]) % (8 * LANES)
    return jnp.pad(flat, (0, pad)).reshape(-1, LANES)


def _unpack(packed, shapes, lead=()):
    flat = packed.reshape(lead + (-1,))
    out, off = [], 0
    for s in shapes:
        n = math.prod(s)
        out.append(flat[..., off:off + n].reshape(lead + tuple(s)))
        off += n
    return out


def _gather_cols(g):
    _, nl, r, cs = g.shape
    return jnp.transpose(g, (1, 2, 0, 3)).reshape(nl, r, N_DEV * cs)


def _gather_rows(g):
    _, nl, rs, c = g.shape
    return jnp.transpose(g, (1, 0, 2, 3)).reshape(nl, N_DEV * rs, c)


def _scatter_cols(full):
    nl, r, c = full.shape
    return jnp.transpose(full.reshape(nl, r, N_DEV, c // N_DEV), (2, 0, 1, 3))


def _scatter_rows(full):
    nl, r, c = full.shape
    return jnp.transpose(full.reshape(nl, N_DEV, r // N_DEV, c), (1, 0, 2, 3))


def _row(v):
    return v.reshape(1, -1)


def _local_step(x, target, mod, cos_t, sin_t, rep, get_weights, put_grads):
    t = x.shape[0]
    saved = []
    for layer in range(DEPTH):
        j = layer // 2
        tag = f"l{layer}"
        shift_m, scale_m, gate_m, shift_f, scale_f, gate_f = [_row(mod[layer, i]) for i in range(N_MOD)]
        lw = dict(get_weights(layer, "mix", x))
        rec = {"x0": x, "lw": lw}
        h = _adaln_fwd(x, _row(rep["norm_mix_g"][layer]), scale_m, shift_m, name=f"adaln_mix_{tag}")
        rec["h"] = h
        if layer % 2 == 0:
            proj = _mm(h, lw["w_main"], mode="nn", out_dtype=F32, name=f"gdn_in_{tag}")
            ab = _mm(h, lw["w_ab"], mode="nn", out_dtype=F32, name=f"gdn_in_ab_{tag}")
            qkv = _gdn_prep_fwd(proj, rep["gdn_conv_wt"][j], name=f"gdn_prep_{tag}")
            gbeta = _gdn_gate_fwd(ab, rep["gdn_gate_prm"][j], name=f"gdn_gate_{tag}")
            gbc = jnp.broadcast_to(jnp.transpose(gbeta[:, 0:GDN_HEADS])[:, :, None], (GDN_HEADS, t, LANES))
            bbc = jnp.broadcast_to(jnp.transpose(gbeta[:, GDN_HEADS:2 * GDN_HEADS])[:, :, None],
                                   (GDN_HEADS, t, LANES))
            o, states = _gdn_chunk_fwd(qkv, gbc, bbc, name=f"gdn_chunk_{tag}")
            og = _gdn_onorm_fwd(o, proj, _row(rep["gdn_norm_g"][j]), name=f"gdn_onorm_{tag}")
            x, y = _mm_resid(og, lw["w_out"], x, gate_m, name=f"gdn_out_{tag}")
            rec.update(proj=proj, ab=ab, qkv=qkv, gbc=gbc, bbc=bbc, states=states, o=o, og=og, y=y)
        else:
            proj = _mm(h, lw["w_in"], mode="nn", out_dtype=F32, name=f"mla_in_{tag}")
            cq, ck = _mla_prep_fwd(proj, _row(rep["mla_q_norm_g"][j]), _row(rep["mla_kv_norm_g"][j]),
                                   name=f"mla_prep_{tag}")
            qf = _mm(cq, lw["w_uq"], mode="nn", out_dtype=F32, name=f"mla_uq_{tag}")
            kvf = _mm(ck, lw["w_ukv"], mode="nn", out_dtype=F32, name=f"mla_ukv_{tag}")
            nrope = MLA_HEADS * MLA_ROPE
            krp = jnp.pad(proj[:, MLA_Q_RANK + MLA_KV_RANK:], ((0, 0), (0, LANES - MLA_ROPE)))
            roped = _rope(jnp.concatenate([qf[:, MLA_HEADS * MLA_NOPE:], krp], axis=1), cos_t, sin_t,
                          name=f"rope_{tag}")
            q_nope = qf[:, :MLA_HEADS * MLA_NOPE].reshape(t, MLA_HEADS, MLA_NOPE)
            q_rope = roped[:, :nrope].reshape(t, MLA_HEADS, MLA_ROPE)
            k_rope = jnp.broadcast_to(roped[:, None, nrope:nrope + MLA_ROPE], (t, MLA_HEADS, MLA_ROPE))
            kv3 = kvf.reshape(t, MLA_HEADS, MLA_NOPE + MLA_V)
            qc = jnp.transpose(jnp.concatenate([q_nope, q_rope], axis=-1), (1, 0, 2)).astype(BF16)
            kc = jnp.transpose(jnp.concatenate([kv3[..., :MLA_NOPE], k_rope], axis=-1), (1, 0, 2)).astype(BF16)
            vc = jnp.transpose(kv3[..., MLA_NOPE:], (1, 0, 2)).astype(BF16)
            oh, lse = _attn_fwd(qc, kc, vc, name=f"attn_{tag}")
            oc = jnp.transpose(oh, (1, 0, 2)).reshape(t, MLA_HEADS * MLA_V).astype(BF16)
            x, y = _mm_resid(oc, lw["w_out"], x, gate_m, name=f"mla_out_{tag}")
            rec.update(proj=proj, cq=cq, ck=ck, qc=qc, kc=kc, vc=vc, oh=oh, lse=lse, oc=oc, y=y)
        rec["x1"] = x
        lw.update(get_weights(layer, "ffn", x))
        h2 = _adaln_fwd(x, _row(rep["norm_ffn_g"][layer]), scale_f, shift_f, name=f"adaln_ffn_{tag}")
        ab2 = _mm(h2, lw["w_gu"], mode="nn", out_dtype=F32, name=f"ffn_gu_{tag}")
        s = _swiglu_fwd(ab2, name=f"swiglu_{tag}")
        x, y2 = _mm_resid(s, lw["w_down"], x, gate_f, name=f"ffn_down_{tag}")
        rec.update(h2=h2, ab2=ab2, s=s, y2=y2)
        saved.append(rec)

    dx, st, ls = _loss_head(x, _row(rep["final_norm_g"]), target, name="loss_head")
    loss = ls[0, 0]
    grads = {"final_norm_g": st[0]}
    per_layer = {k: [None] * DEPTH for k in ("norm_mix_g", "norm_ffn_g")}
    per_gdn = {k: [None] * 2 for k in ("gdn_conv_wt", "gdn_a_log", "gdn_dt_bias", "gdn_norm_g")}
    per_mla = {k: [None] * 2 for k in ("mla_q_norm_g", "mla_kv_norm_g")}
    dmod = [None] * DEPTH
    dep = jnp.zeros((8, LANES), F32)

    for layer in reversed(range(DEPTH)):
        j = layer // 2
        tag = f"l{layer}"
        rec = saved[layer]
        lw = rec["lw"]
        shift_m, scale_m, gate_m, shift_f, scale_f, gate_f = [_row(mod[layer, i]) for i in range(N_MOD)]
        dy2, st_g = _gate_bwd(dx, rec["y2"], gate_f, dep, name=f"gate_bwd_ffn_{tag}")
        dgate_f = st_g[0]
        dw_down = _mm(rec["s"], dy2, mode="tn", out_dtype=BF16, name=f"ffn_down_dw_{tag}")
        ds = _mm(dy2, lw["w_down"], mode="nt", out_dtype=F32, name=f"ffn_down_dx_{tag}")
        dab2 = _swiglu_bwd(rec["ab2"], ds, name=f"swiglu_bwd_{tag}")
        dw_gu = _mm(rec["h2"], dab2, mode="tn", out_dtype=BF16, name=f"ffn_gu_dw_{tag}")
        dep = put_grads(layer, "ffn", {"w_gu": dw_gu, "w_down": dw_down})
        dh2 = _mm(dab2, lw["w_gu"], mode="nt", out_dtype=BF16, name=f"ffn_gu_dx_{tag}")
        dx, st_n = _adaln_bwd(rec["x1"], _row(rep["norm_ffn_g"][layer]), scale_f, shift_f, dh2, dx,
                              name=f"adaln_ffn_bwd_{tag}")
        per_layer["norm_ffn_g"][layer] = st_n[0]
        dscale_f, dshift_f = st_n[1], st_n[2]
        dy, st_g = _gate_bwd(dx, rec["y"], gate_m, dep, name=f"gate_bwd_mix_{tag}")
        dgate_m = st_g[0]
        big = {}
        if layer % 2 == 0:
            big["w_out"] = _mm(rec["og"], dy, mode="tn", out_dtype=BF16, name=f"gdn_out_dw_{tag}")
            dog = _mm(dy, lw["w_out"], mode="nt", out_dtype=BF16, name=f"gdn_out_dx_{tag}")
            do, dgp, st_o = _gdn_onorm_bwd(rec["o"], rec["proj"], _row(rep["gdn_norm_g"][j]), dog,
                                           name=f"gdn_onorm_bwd_{tag}")
            per_gdn["gdn_norm_g"][j] = st_o[0]
            dqkv3 = _gdn_chunk_bwd(rec["qkv"], rec["gbc"], rec["bbc"], rec["states"], do, name=f"gdn_chunk_bwd_{tag}")
            dq_, dk_, dv_, dgc_, dbc_ = dqkv3
            dqkv = jnp.concatenate([dq_, dk_, dv_], axis=1)
            dgb = jnp.concatenate([jnp.transpose(dgc_[:, :, 0]), jnp.transpose(dbc_[:, :, 0])], axis=1)
            dgb = jnp.pad(dgb, ((0, 0), (0, LANES - 2 * GDN_HEADS)))
            dab, st_a = _gdn_gate_bwd(rec["ab"], rep["gdn_gate_prm"][j], dgb, name=f"gdn_gate_bwd_{tag}")
            per_gdn["gdn_a_log"][j] = st_a[0, :GDN_HEADS]
            per_gdn["gdn_dt_bias"][j] = st_a[1, :GDN_HEADS]
            dpre, dcw = _gdn_prep_bwd(rec["proj"], rep["gdn_conv_wt"][j], dqkv, name=f"gdn_prep_bwd_{tag}")
            per_gdn["gdn_conv_wt"][j] = dcw
            dproj = jnp.concatenate([dpre, dgp], axis=1)
            dw_main = _mm(rec["h"], dproj, mode="tn", out_dtype=BF16, name=f"gdn_in_dw_{tag}")
            dw_ab = _mm(rec["h"], dab, mode="tn", out_dtype=BF16, name=f"gdn_in_ab_dw_{tag}")
            big["w_in"] = jnp.concatenate([dw_main, dw_ab[:, :2 * GDN_HEADS]], axis=1)
            dep = put_grads(layer, "gdn", big)
            dh_ab = _mm(dab, lw["w_ab"], mode="nt", out_dtype=F32, name=f"gdn_in_ab_dx_{tag}")
            dh = _mm(dproj, lw["w_main"], mode="nt", out_dtype=BF16, add=dh_ab, name=f"gdn_in_dx_{tag}")
        else:
            big["w_out"] = _mm(rec["oc"], dy, mode="tn", out_dtype=BF16, name=f"mla_out_dw_{tag}")
            doc = _mm(dy, lw["w_out"], mode="nt", out_dtype=F32, name=f"mla_out_dx_{tag}")
            doh = jnp.transpose(doc.reshape(t, MLA_HEADS, MLA_V), (1, 0, 2))
            dqc, dkc, dvc = _attn_bwd(rec["qc"], rec["kc"], rec["vc"], rec["oh"], rec["lse"], doh,
                                      name=f"attn_bwd_{tag}")
            dqn = jnp.transpose(dqc[..., :MLA_NOPE], (1, 0, 2)).reshape(t, MLA_HEADS * MLA_NOPE)
            dqr = jnp.transpose(dqc[..., MLA_NOPE:], (1, 0, 2)).reshape(t, MLA_HEADS * MLA_ROPE)
            dkr = jnp.pad(jnp.sum(dkc[..., MLA_NOPE:], axis=0), ((0, 0), (0, LANES - MLA_ROPE)))
            drope = _rope_bwd(jnp.concatenate([dqr, dkr], axis=1), cos_t, sin_t, name=f"rope_bwd_{tag}")
            nrope = MLA_HEADS * MLA_ROPE
            dqf = jnp.concatenate([dqn, drope[:, :nrope]], axis=1).astype(BF16)
            dkvf = jnp.concatenate([jnp.transpose(dkc[..., :MLA_NOPE], (1, 0, 2)), jnp.transpose(dvc, (1, 0, 2))],
                                   axis=-1).reshape(t, MLA_HEADS * (MLA_NOPE + MLA_V)).astype(BF16)
            big["w_uq"] = _mm(rec["cq"], dqf, mode="tn", out_dtype=BF16, name=f"mla_uq_dw_{tag}")
            big["w_ukv"] = _mm(rec["ck"], dkvf, mode="tn", out_dtype=BF16, name=f"mla_ukv_dw_{tag}")
            dcq = _mm(dqf, lw["w_uq"], mode="nt", out_dtype=F32, name=f"mla_uq_dx_{tag}")
            dck = _mm(dkvf, lw["w_ukv"], mode="nt", out_dtype=F32, name=f"mla_ukv_dx_{tag}")
            dproj, st_p = _mla_prep_bwd(rec["proj"], _row(rep["mla_q_norm_g"][j]), _row(rep["mla_kv_norm_g"][j]),
                                        dcq, dck, drope[:, nrope:nrope + MLA_ROPE], name=f"mla_prep_bwd_{tag}")
            per_mla["mla_q_norm_g"][j] = st_p[0, :MLA_Q_RANK]
            per_mla["mla_kv_norm_g"][j] = st_p[0, MLA_Q_RANK:MLA_Q_RANK + MLA_KV_RANK]
            big["w_in"] = _mm(rec["h"], dproj, mode="tn", out_dtype=BF16, name=f"mla_in_dw_{tag}")
            dep = put_grads(layer, "mla", big)
            dh = _mm(dproj, lw["w_in"], mode="nt", out_dtype=BF16, name=f"mla_in_dx_{tag}")
        dx, st_n = _adaln_bwd(rec["x0"], _row(rep["norm_mix_g"][layer]), scale_m, shift_m, dh, dx,
                              name=f"adaln_mix_bwd_{tag}")
        per_layer["norm_mix_g"][layer] = st_n[0]
        dmod[layer] = jnp.stack([st_n[2], st_n[1], dgate_m, dshift_f, dscale_f, dgate_f])

    for d in (per_layer, per_gdn, per_mla):
        for k, v in d.items():
            grads[k] = jnp.stack(v)
    return loss, dx, jnp.stack(dmod), grads


BIG = ("gdn_w_in", "gdn_w_out", "mla_w_in", "mla_w_uq", "mla_w_ukv", "mla_w_out", "ffn_w_gate", "ffn_w_up",
       "ffn_w_down")
COL_SHARDED = ("gdn_w_in", "mla_w_uq", "mla_w_ukv", "ffn_w_gate", "ffn_w_up")
SMALL = ("ada_b", "norm_mix_g", "norm_ffn_g", "gdn_conv_w", "gdn_a_log", "gdn_dt_bias", "gdn_norm_g",
         "mla_q_norm_g", "mla_kv_norm_g", "final_norm_g")
WEIGHTS = ("ada_w", "ada_b", "norm_mix_g", "norm_ffn_g", "gdn_w_in", "gdn_conv_w", "gdn_a_log", "gdn_dt_bias",
           "gdn_norm_g", "gdn_w_out", "mla_w_in", "mla_q_norm_g", "mla_kv_norm_g", "mla_w_uq", "mla_w_ukv",
           "mla_w_out", "ffn_w_gate", "ffn_w_up", "ffn_w_down", "final_norm_g")


def _uq_to_kernel_layout(w):
    lead = w.shape[:-1]
    w4 = w.reshape(lead + (MLA_HEADS, MLA_QK))
    return jnp.concatenate([w4[..., :MLA_NOPE].reshape(lead + (-1,)), w4[..., MLA_NOPE:].reshape(lead + (-1,))],
                           axis=-1)


def _uq_from_kernel_layout(w):
    lead = w.shape[:-1]
    nope = w[..., :MLA_HEADS * MLA_NOPE].reshape(lead + (MLA_HEADS, MLA_NOPE))
    rope = w[..., MLA_HEADS * MLA_NOPE:].reshape(lead + (MLA_HEADS, MLA_ROPE))
    return jnp.concatenate([nope, rope], axis=-1).reshape(lead + (-1,))


def _group_names(layer, kind):
    if kind == "ffn":
        return ("ffn_w_gate", "ffn_w_up", "ffn_w_down")
    return ("gdn_w_in", "gdn_w_out") if layer % 2 == 0 else ("mla_w_in", "mla_w_uq", "mla_w_ukv", "mla_w_out")


def _layer_index(name, layer):
    return layer if name.startswith("ffn") else layer // 2


def _cols(g):
    return jnp.transpose(g, (1, 0, 2)).reshape(g.shape[1], N_DEV * g.shape[2])


def _rows(g):
    return g.reshape(N_DEV * g.shape[1], g.shape[2])


def _uncols(full):
    r, c = full.shape
    return jnp.transpose(full.reshape(r, N_DEV, c // N_DEV), (1, 0, 2))


def _unrows(full):
    r, c = full.shape
    return full.reshape(N_DEV, r // N_DEV, c)


def _group_weights(layer, kind, got, zero):
    if kind == "ffn":
        return {"w_gu": _gu_to_kernel_layout(_cols(got["ffn_w_gate"]), _cols(got["ffn_w_up"])) + zero,
                "w_down": _rows(got["ffn_w_down"])}
    if layer % 2 == 0:
        w_in = _cols(got["gdn_w_in"]) + zero
        return dict(w_main=w_in[:, :GDN_MAIN], w_ab=jnp.pad(w_in[:, GDN_MAIN:], ((0, 0), (0, LANES - 2 * GDN_HEADS))),
                    w_out=_rows(got["gdn_w_out"]))
    return dict(w_in=_rows(got["mla_w_in"]), w_uq=_uq_to_kernel_layout(_cols(got["mla_w_uq"])) + zero,
                w_ukv=_cols(got["mla_w_ukv"]), w_out=_rows(got["mla_w_out"]))


def _layer_grad_slots(kind, big):
    if kind == "ffn":
        d_gate, d_up = _gu_from_kernel_layout(big["w_gu"])
        return {"ffn_w_gate": _uncols(d_gate), "ffn_w_up": _uncols(d_up), "ffn_w_down": _unrows(big["w_down"])}
    if kind == "gdn":
        return {"gdn_w_in": _uncols(big["w_in"]), "gdn_w_out": _unrows(big["w_out"])}
    return {"mla_w_in": _unrows(big["w_in"]), "mla_w_uq": _uncols(_uq_from_kernel_layout(big["w_uq"])),
            "mla_w_ukv": _uncols(big["w_ukv"]), "mla_w_out": _unrows(big["w_out"])}


def _small_weights(tiny, rep):
    prm = jnp.zeros((2, 8, LANES), F32)
    prm = prm.at[:, 0, :GDN_HEADS].set(rep["gdn_a_log"]).at[:, 1, :GDN_HEADS].set(rep["gdn_dt_bias"])
    out = {
        "gdn_conv_wt": jnp.transpose(_gather_rows(tiny["gdn_conv_w"]), (0, 2, 1)),
        "mla_q_norm_g": jnp.transpose(tiny["mla_q_norm_g"], (1, 0, 2)).reshape(2, MLA_Q_RANK),
        "mla_kv_norm_g": jnp.transpose(tiny["mla_kv_norm_g"], (1, 0, 2)).reshape(2, MLA_KV_RANK),
        "gdn_gate_prm": prm,
    }
    for k in ("norm_mix_g", "norm_ffn_g", "gdn_norm_g", "final_norm_g"):
        out[k] = rep[k]
    return out


def _rope_tables(positions):
    inv_freq = ROPE_THETA ** (-jnp.arange(0, MLA_ROPE, 2, dtype=F32) / MLA_ROPE)
    ang = positions.astype(F32)[:, None] * inv_freq
    cos, sin = jnp.cos(ang), jnp.sin(ang)
    reps = LANES // MLA_ROPE
    return jnp.tile(jnp.concatenate([cos, cos], axis=1), (1, reps)), jnp.tile(
        jnp.concatenate([-sin, sin], axis=1), (1, reps))


def kernel(x, c, positions, ada_w, ada_b, norm_mix_g, norm_ffn_g, gdn_w_in, gdn_conv_w, gdn_a_log, gdn_dt_bias, gdn_norm_g, gdn_w_out, mla_w_in, mla_q_norm_g, mla_kv_norm_g, mla_w_uq, mla_w_ukv, mla_w_out, ffn_w_gate, ffn_w_up, ffn_w_down, final_norm_g, loss_target, m_ada_w, m_ada_b, m_norm_mix_g, m_norm_ffn_g, m_gdn_w_in, m_gdn_conv_w, m_gdn_a_log, m_gdn_dt_bias, m_gdn_norm_g, m_gdn_w_out, m_mla_w_in, m_mla_q_norm_g, m_mla_kv_norm_g, m_mla_w_uq, m_mla_w_ukv, m_mla_w_out, m_ffn_w_gate, m_ffn_w_up, m_ffn_w_down, m_final_norm_g, v_ada_w, v_ada_b, v_norm_mix_g, v_norm_ffn_g, v_gdn_w_in, v_gdn_conv_w, v_gdn_a_log, v_gdn_dt_bias, v_gdn_norm_g, v_gdn_w_out, v_mla_w_in, v_mla_q_norm_g, v_mla_kv_norm_g, v_mla_w_uq, v_mla_w_ukv, v_mla_w_out, v_ffn_w_gate, v_ffn_w_up, v_ffn_w_down, v_final_norm_g):
    W = dict(ada_w=ada_w, ada_b=ada_b, norm_mix_g=norm_mix_g, norm_ffn_g=norm_ffn_g, gdn_w_in=gdn_w_in,
             gdn_conv_w=gdn_conv_w, gdn_a_log=gdn_a_log, gdn_dt_bias=gdn_dt_bias, gdn_norm_g=gdn_norm_g,
             gdn_w_out=gdn_w_out, mla_w_in=mla_w_in, mla_q_norm_g=mla_q_norm_g, mla_kv_norm_g=mla_kv_norm_g,
             mla_w_uq=mla_w_uq, mla_w_ukv=mla_w_ukv, mla_w_out=mla_w_out, ffn_w_gate=ffn_w_gate,
             ffn_w_up=ffn_w_up, ffn_w_down=ffn_w_down, final_norm_g=final_norm_g)
    M = dict(ada_w=m_ada_w, ada_b=m_ada_b, norm_mix_g=m_norm_mix_g, norm_ffn_g=m_norm_ffn_g, gdn_w_in=m_gdn_w_in,
             gdn_conv_w=m_gdn_conv_w, gdn_a_log=m_gdn_a_log, gdn_dt_bias=m_gdn_dt_bias, gdn_norm_g=m_gdn_norm_g,
             gdn_w_out=m_gdn_w_out, mla_w_in=m_mla_w_in, mla_q_norm_g=m_mla_q_norm_g,
             mla_kv_norm_g=m_mla_kv_norm_g, mla_w_uq=m_mla_w_uq, mla_w_ukv=m_mla_w_ukv, mla_w_out=m_mla_w_out,
             ffn_w_gate=m_ffn_w_gate, ffn_w_up=m_ffn_w_up, ffn_w_down=m_ffn_w_down, final_norm_g=m_final_norm_g)
    V = dict(ada_w=v_ada_w, ada_b=v_ada_b, norm_mix_g=v_norm_mix_g, norm_ffn_g=v_norm_ffn_g, gdn_w_in=v_gdn_w_in,
             gdn_conv_w=v_gdn_conv_w, gdn_a_log=v_gdn_a_log, gdn_dt_bias=v_gdn_dt_bias, gdn_norm_g=v_gdn_norm_g,
             gdn_w_out=v_gdn_w_out, mla_w_in=v_mla_w_in, mla_q_norm_g=v_mla_q_norm_g,
             mla_kv_norm_g=v_mla_kv_norm_g, mla_w_uq=v_mla_w_uq, mla_w_ukv=v_mla_w_ukv, mla_w_out=v_mla_w_out,
             ffn_w_gate=v_ffn_w_gate, ffn_w_up=v_ffn_w_up, ffn_w_down=v_ffn_w_down, final_norm_g=v_final_norm_g)
    me = 4 * lax.axis_index("x") + 2 * lax.axis_index("y") + lax.axis_index("c")
    t = x.shape[1]
    wc = ada_w.shape[-1]

    tiny_shapes = [c.shape, gdn_conv_w.shape, mla_q_norm_g.shape, mla_kv_norm_g.shape]
    (tiny_g,) = _exchange([_pack([c, gdn_conv_w, mla_q_norm_g, mla_kv_norm_g])], scatter=False, name="gather_tiny")
    c_g, conv_g, qn_g, kvn_g = _unpack(tiny_g, tiny_shapes, lead=(N_DEV,))
    c_all = c_g.reshape(N_DEV, D_MODEL)
    rep = _small_weights({"gdn_conv_w": conv_g, "mla_q_norm_g": qn_g, "mla_kv_norm_g": kvn_g}, W)

    groups = [(layer, kind) for layer in range(DEPTH) for kind in ("mix", "ffn")]

    def start_group(i, dep):
        layer, kind = groups[i]
        srcs = [W[k][_layer_index(k, layer)].astype(BF16) for k in _group_names(layer, kind)]
        return _exchange_start(srcs, scatter=False, name=f"gather_start_{kind}_l{layer}", dep=dep)

    gather = {0: start_group(0, tiny_g)}

    b_cols = lax.dynamic_slice_in_dim(ada_b, me * wc, wc, axis=1).reshape(DEPTH, 1, wc)
    mod_part = _ada_mod(c_all, ada_w, b_cols, name="ada_mod")
    (mod_g,) = _exchange([mod_part], scatter=False, name="gather_mod")
    mod_mine = lax.dynamic_index_in_dim(mod_g, me, axis=2, keepdims=False)
    mod = jnp.transpose(mod_mine, (1, 0, 2)).reshape(DEPTH, N_MOD, D_MODEL)

    def get_weights(layer, kind, after):
        i = groups.index((layer, kind))
        srcs, lands = _exchange_wait(gather[i], mod if i == 0 else after, scatter=False,
                                     name=f"gather_wait_{kind}_l{layer}")
        zero = jnp.zeros((), BF16)
        if i + 1 < len(groups):
            gather[i + 1] = start_group(i + 1, lands[0])
            zero = gather[i + 1][4][0, 0].astype(BF16)
        got = {k: lax.dynamic_update_index_in_dim(z, s, me, 0)
               for k, s, z in zip(_group_names(layer, kind), srcs, lands)}
        return _group_weights(layer, kind, got, zero)

    scatter = []

    def put_grads(layer, kind, big):
        slots = _layer_grad_slots(kind, big)
        started = _exchange_start(list(slots.values()), scatter=True, name=f"scatter_start_{kind}_l{layer}")
        scatter.append((layer, kind, list(slots.keys()), started))
        return started[4]

    cos_t, sin_t = _rope_tables(positions[0])
    loss, dx, dmod, g = _local_step(x[0], loss_target[0], mod, cos_t, sin_t, rep, get_weights, put_grads)

    parts = {k: [None] * W[k].shape[0] for k in BIG}
    res = {}

    def wait_group(entry, after):
        layer, kind, names, started = entry
        srcs, lands = _exchange_wait(started, after, scatter=True, name=f"scatter_wait_{kind}_l{layer}")
        for k, s, z in zip(names, srcs, lands):
            own = lax.dynamic_index_in_dim(s, me, 0, keepdims=False)
            parts[k][_layer_index(k, layer)] = lax.dynamic_update_index_in_dim(z, own, me, 0)

    for entry in scatter[:-1]:
        wait_group(entry, dx)
    early = [k for k in BIG if k not in scatter[-1][2]]
    for k in early:
        res[k] = _adamw(parts[k], W[k], M[k], V[k], name=f"adamw_{k}")

    small_local = [dmod.reshape(DEPTH, N_MOD * D_MODEL), g["norm_mix_g"], g["norm_ffn_g"],
                   jnp.transpose(g["gdn_conv_wt"], (0, 2, 1)), g["gdn_a_log"], g["gdn_dt_bias"], g["gdn_norm_g"],
                   g["mla_q_norm_g"], g["mla_kv_norm_g"], g["final_norm_g"], loss.reshape(1)]
    small_shapes = [a.shape for a in small_local]
    (small_g,) = _exchange([_pack(small_local)], scatter=False, name="gather_small_grads")
    small_sum = _unpack(_sum_parts(small_g, name="sum_small_grads"), small_shapes)
    loss = small_sum[-1][0]
    dmod_all = _unpack(small_g, small_shapes[:1], lead=(N_DEV,))[0]
    sg = dict(zip(SMALL, small_sum))
    wait_group(scatter[-1], small_g)
    sg["gdn_conv_w"] = lax.dynamic_slice_in_dim(sg["gdn_conv_w"], me * gdn_conv_w.shape[1], gdn_conv_w.shape[1], 1)
    sg["mla_q_norm_g"] = lax.dynamic_slice_in_dim(sg["mla_q_norm_g"], me * mla_q_norm_g.shape[1],
                                                  mla_q_norm_g.shape[1], 1)
    sg["mla_kv_norm_g"] = lax.dynamic_slice_in_dim(sg["mla_kv_norm_g"], me * mla_kv_norm_g.shape[1],
                                                   mla_kv_norm_g.shape[1], 1)

    dmod_cols = jnp.transpose(lax.dynamic_slice_in_dim(dmod_all, me * wc, wc, axis=2), (1, 0, 2))
    res["ada_w"] = _ada_grad_adamw(c_all, dmod_cols, ada_w, m_ada_w, v_ada_w, name="ada_w_grad_adamw")
    for k in BIG:
        if k not in early:
            res[k] = _adamw(parts[k], W[k], M[k], V[k], name=f"adamw_{k}")
    shapes = [W[k].shape for k in SMALL]
    packed = [_pack([d[k] for k in SMALL]) for d in (sg, W, M, V)]
    outs = _adamw([packed[0][None]], packed[1][None], packed[2][None], packed[3][None], name="adamw_small")
    unpacked = [_unpack(o[0], shapes) for o in outs]
    for i, k in enumerate(SMALL):
        res[k] = tuple(u[i] for u in unpacked)

    return (loss, dx[None], *[res[k][0] for k in WEIGHTS], *[res[k][1] for k in WEIGHTS],
            *[res[k][2] for k in WEIGHTS], *[res[k][3] for k in WEIGHTS])
```

```python
import functools
import math

import jax
import jax.numpy as jnp
from jax import lax
from jax.experimental import pallas as pl
from jax.experimental.pallas import tpu as pltpu

F32 = jnp.float32
BF16 = jnp.bfloat16
MXU_DTYPE = jnp.bfloat16

N_DEV = 8
D_MODEL = 1024
DEPTH = 4
GDN_HEADS = 8
GDN_HEAD_DIM = 128
GDN_KEY_DIM = GDN_HEADS * GDN_HEAD_DIM
GDN_CHUNK = 64
GDN_HEAD_BATCH = 8
GDN_CONV = 4
GDN_MAIN = 4 * GDN_KEY_DIM
MLA_HEADS = 8
MLA_NOPE = 128
MLA_ROPE = 64
MLA_V = 128
MLA_Q_RANK = 384
MLA_KV_RANK = 256
MLA_IN = MLA_Q_RANK + MLA_KV_RANK + MLA_ROPE
MLA_QK = MLA_NOPE + MLA_ROPE
ROPE_THETA = 10000.0
D_FF = 2816
N_MOD = 6
EPS = 1e-6
LANES = 128
VMEM_LIMIT = 48 * 1024 * 1024

ADAM_LR = 0.001
ADAM_B1 = 0.9
ADAM_B2 = 0.999
ADAM_EPS = 1e-08
ADAM_WD = 0.01
ADAM_STEP = 10
ADAM_BC1 = 1.0 - ADAM_B1 ** ADAM_STEP
ADAM_BC2 = 1.0 - ADAM_B2 ** ADAM_STEP

NN = (((1,), (0,)), ((), ()))
NT = (((1,), (1,)), ((), ()))
TN = (((0,), (0,)), ((), ()))
NEG = -1e30


def _dotb(a, b, dims):
    return lax.dot_general(a.astype(MXU_DTYPE), b.astype(MXU_DTYPE), dims, preferred_element_type=F32)


def _split(a):
    hi = a.astype(BF16)
    return hi, (a - hi.astype(F32)).astype(BF16)


def _dotf(a, b, dims):
    ah, al = _split(a)
    bh, bl = _split(b)
    dot = lambda u, v: lax.dot_general(u, v, dims, preferred_element_type=F32)
    return dot(ah, bh) + (dot(ah, bl) + dot(al, bh))


def _params(*sem):
    return pltpu.CompilerParams(dimension_semantics=sem, vmem_limit_bytes=VMEM_LIMIT)


def _pick(n, pref, mult=LANES):
    best = None
    t = mult
    while t <= min(n, pref):
        if n % t == 0:
            best = t
        t += mult
    return best if best is not None else n


def _sigmoid(z):
    return 1.0 / (1.0 + jnp.exp(-z))


def _exchange(arrays, *, scatter, name):
    n = len(arrays)
    out_shape = tuple(
        jax.ShapeDtypeStruct(a.shape if scatter else (N_DEV,) + a.shape, a.dtype) for a in arrays)

    def body(*refs):
        ins, outs = refs[:n], refs[n:2 * n]
        send_sems, recv_sems, local_sems = refs[2 * n:]
        x, y, c = lax.axis_index("x"), lax.axis_index("y"), lax.axis_index("c")
        me = 4 * x + 2 * y + c
        copies = []
        for k in range(n):
            src_own = ins[k].at[me] if scatter else ins[k]
            own = pltpu.make_async_copy(src_own, outs[k].at[me], local_sems.at[k])
            own.start()
            copies.append(own)
        sends = []
        for p in range(1, N_DEV):
            px, py, pc = x ^ ((p >> 2) & 1), y ^ ((p >> 1) & 1), c ^ (p & 1)
            peer = 4 * px + 2 * py + pc
            for k in range(n):
                cp = pltpu.make_async_remote_copy(
                    src_ref=ins[k].at[peer] if scatter else ins[k],
                    dst_ref=outs[k].at[me],
                    send_sem=send_sems.at[k, p - 1],
                    recv_sem=recv_sems.at[k, p - 1],
                    device_id=(px, py, pc),
                    device_id_type=pl.DeviceIdType.MESH,
                )
                cp.start()
                sends.append((cp, k, peer, p))
        for cp, k, peer, p in sends:
            pltpu.make_async_remote_copy(
                src_ref=ins[k].at[peer] if scatter else ins[k],
                dst_ref=outs[k].at[peer],
                send_sem=send_sems.at[k, p - 1],
                recv_sem=recv_sems.at[k, p - 1],
                device_id=(x, y, c),
                device_id_type=pl.DeviceIdType.MESH,
            ).wait_recv()
        for cp, _, _, _ in sends:
            cp.wait_send()
        for own in copies:
            own.wait()

    any_spec = pl.BlockSpec(memory_space=pl.ANY)
    outs = pl.pallas_call(
        body,
        name=name,
        out_shape=out_shape,
        in_specs=[any_spec] * n,
        out_specs=tuple([any_spec] * n),
        scratch_shapes=[
            pltpu.SemaphoreType.DMA((n, N_DEV - 1)),
            pltpu.SemaphoreType.DMA((n, N_DEV - 1)),
            pltpu.SemaphoreType.DMA((n,)),
        ],
        compiler_params=pltpu.CompilerParams(has_side_effects=True),
    )(*arrays)
    return list(outs)


def _peer(x, y, c, p):
    return x ^ ((p >> 2) & 1), y ^ ((p >> 1) & 1), c ^ (p & 1)


def _exchange_start(arrays, *, scatter, name, dep=None):
    n = len(arrays)
    deps = [] if dep is None else [dep]
    lands = [lax.empty(a.shape if scatter else (N_DEV,) + a.shape, a.dtype) for a in arrays]

    def body(*refs):
        ins, zones = refs[:n], refs[n:2 * n]
        send_sems, recv_sems = refs[2 * n + len(deps)], refs[2 * n + len(deps) + 1]
        token = refs[-1]
        x, y, c = lax.axis_index("x"), lax.axis_index("y"), lax.axis_index("c")
        me = 4 * x + 2 * y + c
        for p in range(1, N_DEV):
            px, py, pc = _peer(x, y, c, p)
            for k in range(n):
                pltpu.make_async_remote_copy(
                    src_ref=ins[k].at[4 * px + 2 * py + pc] if scatter else ins[k],
                    dst_ref=zones[k].at[me],
                    send_sem=send_sems.at[k * (N_DEV - 1) + p - 1],
                    recv_sem=recv_sems.at[k * (N_DEV - 1) + p - 1],
                    device_id=(px, py, pc),
                    device_id_type=pl.DeviceIdType.MESH,
                ).start()
        token[...] = jnp.zeros_like(token)

    hbm = pl.BlockSpec(memory_space=pltpu.HBM)
    sem = pl.BlockSpec(memory_space=pltpu.SEMAPHORE)
    outs = pl.pallas_call(
        body,
        name=name,
        out_shape=(pltpu.SemaphoreType.DMA((n * (N_DEV - 1),)), pltpu.SemaphoreType.DMA((n * (N_DEV - 1),)),
                   *[pltpu.HBM(a.shape, a.dtype) for a in arrays], *[pltpu.HBM(z.shape, z.dtype) for z in lands],
                   jax.ShapeDtypeStruct((8, LANES), F32)),
        in_specs=[hbm] * (2 * n) + [pl.BlockSpec(memory_space=pl.ANY)] * len(deps),
        out_specs=(sem, sem, *[hbm] * (2 * n), pl.BlockSpec(memory_space=pltpu.VMEM)),
        input_output_aliases={k: 2 + k for k in range(2 * n)},
        compiler_params=pltpu.CompilerParams(has_side_effects=pltpu.SideEffectType.DATAFLOW_SIDE_EFFECTING),
    )(*[pltpu.with_memory_space_constraint(a, pltpu.HBM) for a in arrays],
      *[pltpu.with_memory_space_constraint(z, pltpu.HBM) for z in lands], *deps)
    return outs[0], outs[1], list(outs[2:2 + n]), list(outs[2 + n:2 + 2 * n]), outs[-1]


def _exchange_wait(started, after, *, scatter, name):
    send_sems, recv_sems, srcs, lands, _ = started
    n = len(srcs)

    def body(*refs):
        ins, zones = refs[:n], refs[n:2 * n]
        s_sems, r_sems = refs[2 * n], refs[2 * n + 1]
        x, y, c = lax.axis_index("x"), lax.axis_index("y"), lax.axis_index("c")
        for p in range(1, N_DEV):
            px, py, pc = _peer(x, y, c, p)
            peer = 4 * px + 2 * py + pc
            for k in range(n):
                cp = pltpu.make_async_remote_copy(
                    src_ref=ins[k].at[peer] if scatter else ins[k],
                    dst_ref=zones[k].at[peer],
                    send_sem=s_sems.at[k * (N_DEV - 1) + p - 1],
                    recv_sem=r_sems.at[k * (N_DEV - 1) + p - 1],
                    device_id=(px, py, pc),
                    device_id_type=pl.DeviceIdType.MESH,
                )
                cp.wait_send()
                cp.wait_recv()

    hbm = pl.BlockSpec(memory_space=pltpu.HBM)
    sem = pl.BlockSpec(memory_space=pltpu.SEMAPHORE)
    outs = pl.pallas_call(
        body,
        name=name,
        out_shape=tuple(pltpu.HBM(a.shape, a.dtype) for a in srcs + lands),
        in_specs=[hbm] * (2 * n) + [sem, sem, pl.BlockSpec(memory_space=pl.ANY)],
        out_specs=tuple([hbm] * (2 * n)),
        input_output_aliases={k: k for k in range(2 * n)},
        compiler_params=pltpu.CompilerParams(has_side_effects=pltpu.SideEffectType.DATAFLOW_SIDE_EFFECTING),
    )(*srcs, *lands, send_sems, recv_sems, after)
    return list(outs[:n]), list(outs[n:])


def _mm(a, b, *, mode, out_dtype, name, add=None, tm=512, tn=512):
    if mode == "nn":
        (m, kd), (_, nd) = a.shape, b.shape
    elif mode == "nt":
        (m, kd), (nd, _) = a.shape, b.shape
    else:
        (kd, m), (_, nd) = a.shape, b.shape
    tm = _pick(m, tm, LANES if mode == "tn" else 16)
    tn = _pick(nd, tn)
    dims = {"nn": NN, "nt": NT, "tn": TN}[mode]
    a_spec = pl.BlockSpec((kd, tm), lambda i, j: (0, i)) if mode == "tn" else pl.BlockSpec((tm, kd), lambda i, j: (i, 0))
    b_spec = pl.BlockSpec((tn, kd), lambda i, j: (j, 0)) if mode == "nt" else pl.BlockSpec((kd, tn), lambda i, j: (0, j))
    o_spec = pl.BlockSpec((tm, tn), lambda i, j: (i, j))
    has_add = add is not None

    def body(*refs):
        a_ref, b_ref = refs[0], refs[1]
        o_ref = refs[-1]
        acc = _dotb(a_ref[...], b_ref[...], dims)
        if has_add:
            acc = acc + refs[2][...].astype(F32)
        o_ref[...] = acc.astype(o_ref.dtype)

    ins = [a, b] + ([add] if has_add else [])
    specs = [a_spec, b_spec] + ([o_spec] if has_add else [])
    return pl.pallas_call(
        body, name=name, grid=(m // tm, nd // tn), in_specs=specs, out_specs=o_spec,
        out_shape=jax.ShapeDtypeStruct((m, nd), out_dtype),
        compiler_params=_params("parallel", "parallel"),
    )(*ins)


def _mm_resid(a, b, x, gate, *, name, tm=512, tn=512):
    m, kd = a.shape
    nd = b.shape[1]
    tm = _pick(m, tm, 16)
    tn = _pick(nd, tn)
    o_spec = pl.BlockSpec((tm, tn), lambda i, j: (i, j))

    def body(a_ref, b_ref, x_ref, g_ref, xo_ref, y_ref):
        y = _dotb(a_ref[...], b_ref[...], NN)
        y_ref[...] = y
        xo_ref[...] = x_ref[...] + g_ref[...] * y

    return pl.pallas_call(
        body, name=name, grid=(m // tm, nd // tn),
        in_specs=[pl.BlockSpec((tm, kd), lambda i, j: (i, 0)), pl.BlockSpec((kd, tn), lambda i, j: (0, j)),
                  o_spec, pl.BlockSpec((1, tn), lambda i, j: (0, j))],
        out_specs=(o_spec, o_spec),
        out_shape=(jax.ShapeDtypeStruct((m, nd), F32), jax.ShapeDtypeStruct((m, nd), F32)),
        compiler_params=_params("parallel", "parallel"),
    )(a, b, x, gate)


ROWS = 256


def _row_spec(width, rows=ROWS):
    return pl.BlockSpec((rows, width), lambda i: (i, 0))


def _const_spec(shape):
    return pl.BlockSpec(shape, lambda i: tuple(0 for _ in shape))


def _adaln_fwd(x, g, scale, shift, *, name):
    t, d = x.shape

    def body(x_ref, g_ref, sc_ref, sh_ref, h_ref):
        xv = x_ref[...]
        r = lax.rsqrt(jnp.mean(xv * xv, axis=-1, keepdims=True) + EPS)
        h_ref[...] = (xv * r * g_ref[...] * (1.0 + sc_ref[...]) + sh_ref[...]).astype(h_ref.dtype)

    return pl.pallas_call(
        body, name=name, grid=(t // ROWS,),
        in_specs=[_row_spec(d), _const_spec((1, d)), _const_spec((1, d)), _const_spec((1, d))],
        out_specs=_row_spec(d), out_shape=jax.ShapeDtypeStruct((t, d), BF16),
        compiler_params=_params("parallel"),
    )(x, g, scale, shift)


def _adaln_bwd(x, g, scale, shift, dh, dres, *, name):
    t, d = x.shape

    def body(x_ref, g_ref, sc_ref, sh_ref, dh_ref, dr_ref, dx_ref, st_ref):
        @pl.when(pl.program_id(0) == 0)
        def _():
            st_ref[...] = jnp.zeros_like(st_ref)

        xv = x_ref[...]
        dhv = dh_ref[...].astype(F32)
        gv = g_ref[...]
        r = lax.rsqrt(jnp.mean(xv * xv, axis=-1, keepdims=True) + EPS)
        xh = xv * r
        nv = xh * gv
        dn = dhv * (1.0 + sc_ref[...])
        dxh = dn * gv
        dx_ref[...] = dr_ref[...] + r * (dxh - xh * jnp.mean(dxh * xh, axis=-1, keepdims=True))
        st_ref[0:1, :] += jnp.sum(dn * xh, axis=0, keepdims=True)
        st_ref[1:2, :] += jnp.sum(dhv * nv, axis=0, keepdims=True)
        st_ref[2:3, :] += jnp.sum(dhv, axis=0, keepdims=True)

    return pl.pallas_call(
        body, name=name, grid=(t // ROWS,),
        in_specs=[_row_spec(d), _const_spec((1, d)), _const_spec((1, d)), _const_spec((1, d)),
                  _row_spec(d), _row_spec(d)],
        out_specs=(_row_spec(d), _const_spec((8, d))),
        out_shape=(jax.ShapeDtypeStruct((t, d), F32), jax.ShapeDtypeStruct((8, d), F32)),
        compiler_params=_params("arbitrary"),
    )(x, g, scale, shift, dh, dres)


def _gate_bwd(dxo, y, gate, dep, *, name):
    t, d = dxo.shape

    def body(dx_ref, y_ref, g_ref, dep_ref, dy_ref, st_ref):
        @pl.when(pl.program_id(0) == 0)
        def _():
            st_ref[...] = jnp.zeros_like(st_ref)

        dxv = dx_ref[...]
        dy_ref[...] = (dxv * g_ref[...]).astype(dy_ref.dtype)
        st_ref[0:1, :] += jnp.sum(dxv * y_ref[...], axis=0, keepdims=True)

    return pl.pallas_call(
        body, name=name, grid=(t // ROWS,),
        in_specs=[_row_spec(d), _row_spec(d), _const_spec((1, d)), _const_spec((8, LANES))],
        out_specs=(_row_spec(d), _const_spec((8, d))),
        out_shape=(jax.ShapeDtypeStruct((t, d), BF16), jax.ShapeDtypeStruct((8, d), F32)),
        compiler_params=_params("arbitrary"),
    )(dxo, y, gate, dep)


def _loss_head(x, g, target, *, name):
    t, d = x.shape

    def body(x_ref, g_ref, t_ref, dx_ref, st_ref, ls_ref):
        @pl.when(pl.program_id(0) == 0)
        def _():
            st_ref[...] = jnp.zeros_like(st_ref)
            ls_ref[...] = jnp.zeros_like(ls_ref)

        xv = x_ref[...]
        gv = g_ref[...]
        r = lax.rsqrt(jnp.mean(xv * xv, axis=-1, keepdims=True) + EPS)
        xh = xv * r
        err = xh * gv - t_ref[...]
        ls_ref[...] += 0.5 * jnp.sum(jnp.mean(err * err, axis=-1, keepdims=True))
        dy = err * (1.0 / d)
        dxh = dy * gv
        dx_ref[...] = r * (dxh - xh * jnp.mean(dxh * xh, axis=-1, keepdims=True))
        st_ref[0:1, :] += jnp.sum(dy * xh, axis=0, keepdims=True)

    return pl.pallas_call(
        body, name=name, grid=(t // ROWS,),
        in_specs=[_row_spec(d), _const_spec((1, d)), _row_spec(d)],
        out_specs=(_row_spec(d), _const_spec((8, d)), _const_spec((8, LANES))),
        out_shape=(jax.ShapeDtypeStruct((t, d), F32), jax.ShapeDtypeStruct((8, d), F32),
                   jax.ShapeDtypeStruct((8, LANES), F32)),
        compiler_params=_params("arbitrary"),
    )(x, g, target)


FFN_BLOCK = D_FF // 2


def _gu_to_kernel_layout(wg, wu):
    parts = []
    for b in range(D_FF // FFN_BLOCK):
        sl = slice(b * FFN_BLOCK, (b + 1) * FFN_BLOCK)
        parts += [wg[..., sl], wu[..., sl]]
    return jnp.concatenate(parts, axis=-1)


def _gu_from_kernel_layout(w):
    nb = D_FF // FFN_BLOCK
    wg = jnp.concatenate([w[..., 2 * b * FFN_BLOCK:(2 * b + 1) * FFN_BLOCK] for b in range(nb)], axis=-1)
    wu = jnp.concatenate([w[..., (2 * b + 1) * FFN_BLOCK:(2 * b + 2) * FFN_BLOCK] for b in range(nb)], axis=-1)
    return wg, wu


def _swiglu_fwd(ab, *, name):
    t = ab.shape[0]
    tn = FFN_BLOCK

    def body(ab_ref, s_ref):
        a = ab_ref[:, 0:tn]
        s_ref[...] = (a * _sigmoid(a) * ab_ref[:, tn:2 * tn]).astype(s_ref.dtype)

    return pl.pallas_call(
        body, name=name, grid=(t // ROWS, D_FF // tn),
        in_specs=[pl.BlockSpec((ROWS, 2 * tn), lambda i, j: (i, j))],
        out_specs=pl.BlockSpec((ROWS, tn), lambda i, j: (i, j)),
        out_shape=jax.ShapeDtypeStruct((t, D_FF), BF16),
        compiler_params=_params("parallel", "parallel"),
    )(ab)


def _swiglu_bwd(ab, ds, *, name):
    t = ab.shape[0]
    tn = FFN_BLOCK

    def body(ab_ref, ds_ref, d_ref):
        a = ab_ref[:, 0:tn]
        dsv = ds_ref[...]
        sg = _sigmoid(a)
        d_ref[:, 0:tn] = (dsv * ab_ref[:, tn:2 * tn] * sg * (1.0 + a * (1.0 - sg))).astype(d_ref.dtype)
        d_ref[:, tn:2 * tn] = (dsv * a * sg).astype(d_ref.dtype)

    return pl.pallas_call(
        body, name=name, grid=(t // ROWS, D_FF // tn),
        in_specs=[pl.BlockSpec((ROWS, 2 * tn), lambda i, j: (i, j)), pl.BlockSpec((ROWS, tn), lambda i, j: (i, j))],
        out_specs=pl.BlockSpec((ROWS, 2 * tn), lambda i, j: (i, j)),
        out_shape=jax.ShapeDtypeStruct((t, 2 * D_FF), BF16),
        compiler_params=_params("parallel", "parallel"),
    )(ab, ds)


def _shift_rows(v, s, rows):
    if s == 0:
        return v
    return jnp.where(rows >= s, pltpu.roll(v, s, 0), 0.0)


def _unshift_rows(v, s, rows, t):
    if s == 0:
        return v
    return jnp.where(rows < t - s, pltpu.roll(v, t - s, 0), 0.0)


def _conv_silu(x, w, rows):
    z = w[GDN_CONV - 1:GDN_CONV, :] * x
    for j in range(GDN_CONV - 1):
        z = z + w[j:j + 1, :] * _shift_rows(x, GDN_CONV - 1 - j, rows)
    sg = _sigmoid(z)
    return z, sg, z * sg


def _gdn_prep_fwd(proj, conv_wt, *, name):
    t = proj.shape[0]
    nh = GDN_HEADS

    def body(x_ref, w_ref, y_ref):
        j = pl.program_id(0)
        rows = lax.broadcasted_iota(jnp.int32, (t, LANES), 0)
        _, _, s = _conv_silu(x_ref[...], w_ref[...], rows)
        rs = lax.rsqrt(jnp.sum(s * s, axis=-1, keepdims=True) + EPS)
        qscale = jnp.where(j < nh, GDN_HEAD_DIM ** -0.5, 1.0)
        y_ref[...] = jnp.where(j < 2 * nh, s * rs * qscale, s)

    return pl.pallas_call(
        body, name=name, grid=(3 * nh,),
        in_specs=[pl.BlockSpec((t, LANES), lambda j: (0, j)), pl.BlockSpec((GDN_CONV, LANES), lambda j: (0, j))],
        out_specs=pl.BlockSpec((t, LANES), lambda j: (0, j)),
        out_shape=jax.ShapeDtypeStruct((t, 3 * GDN_KEY_DIM), F32),
        compiler_params=_params("parallel"),
    )(proj, conv_wt)


def _gdn_prep_bwd(proj, conv_wt, dy, *, name):
    t = proj.shape[0]
    nh = GDN_HEADS

    def body(x_ref, w_ref, dy_ref, dx_ref, dw_ref):
        j = pl.program_id(0)
        rows = lax.broadcasted_iota(jnp.int32, (t, LANES), 0)
        x = x_ref[...]
        w = w_ref[...]
        z, sg, s = _conv_silu(x, w, rows)
        rs = lax.rsqrt(jnp.sum(s * s, axis=-1, keepdims=True) + EPS)
        qscale = jnp.where(j < nh, GDN_HEAD_DIM ** -0.5, 1.0)
        dyv = dy_ref[...]
        nv = s * rs
        de = dyv * qscale
        ds_qk = rs * (de - nv * jnp.sum(de * nv, axis=-1, keepdims=True))
        ds = jnp.where(j < 2 * nh, ds_qk, dyv)
        dz = ds * sg * (1.0 + z * (1.0 - sg))
        dx = w[GDN_CONV - 1:GDN_CONV, :] * dz
        dw_ref[GDN_CONV - 1:GDN_CONV, :] = jnp.sum(dz * x, axis=0, keepdims=True)
        for k in range(GDN_CONV - 1):
            sh = GDN_CONV - 1 - k
            dx = dx + w[k:k + 1, :] * _unshift_rows(dz, sh, rows, t)
            dw_ref[k:k + 1, :] = jnp.sum(dz * _shift_rows(x, sh, rows), axis=0, keepdims=True)
        dx_ref[...] = dx.astype(dx_ref.dtype)

    return pl.pallas_call(
        body, name=name, grid=(3 * nh,),
        in_specs=[pl.BlockSpec((t, LANES), lambda j: (0, j)), pl.BlockSpec((GDN_CONV, LANES), lambda j: (0, j)),
                  pl.BlockSpec((t, LANES), lambda j: (0, j))],
        out_specs=(pl.BlockSpec((t, LANES), lambda j: (0, j)), pl.BlockSpec((GDN_CONV, LANES), lambda j: (0, j))),
        out_shape=(jax.ShapeDtypeStruct((t, 3 * GDN_KEY_DIM), BF16),
                   jax.ShapeDtypeStruct((GDN_CONV, 3 * GDN_KEY_DIM), F32)),
        compiler_params=_params("parallel"),
    )(proj, conv_wt, dy)


def _softplus(z):
    return jnp.maximum(z, 0.0) + jnp.log(1.0 + jnp.exp(-jnp.abs(z)))


def _gdn_gate_fwd(ab, prm, *, name):
    t = ab.shape[0]

    def body(ab_ref, p_ref, o_ref):
        v = ab_ref[...]
        lane = lax.broadcasted_iota(jnp.int32, v.shape, 1)
        g = -jnp.exp(p_ref[0:1, :]) * _softplus(v + p_ref[1:2, :])
        o_ref[...] = jnp.where(lane < GDN_HEADS, g, jnp.where(lane < 2 * GDN_HEADS, _sigmoid(v), 0.0))

    return pl.pallas_call(
        body, name=name, grid=(t // ROWS,),
        in_specs=[_row_spec(LANES), _const_spec((8, LANES))], out_specs=_row_spec(LANES),
        out_shape=jax.ShapeDtypeStruct((t, LANES), F32), compiler_params=_params("parallel"),
    )(ab, prm)


def _gdn_gate_bwd(ab, prm, dgb, *, name):
    t = ab.shape[0]

    def body(ab_ref, p_ref, d_ref, o_ref, st_ref):
        @pl.when(pl.program_id(0) == 0)
        def _():
            st_ref[...] = jnp.zeros_like(st_ref)

        v = ab_ref[...]
        dv = d_ref[...]
        lane = lax.broadcasted_iota(jnp.int32, v.shape, 1)
        is_a = lane < GDN_HEADS
        is_b = jnp.logical_and(lane >= GDN_HEADS, lane < 2 * GDN_HEADS)
        a_exp = jnp.exp(p_ref[0:1, :])
        zz = v + p_ref[1:2, :]
        g = -a_exp * _softplus(zz)
        da = dv * (-a_exp) * _sigmoid(zz)
        beta = _sigmoid(v)
        db = dv * beta * (1.0 - beta)
        o_ref[...] = jnp.where(is_a, da, jnp.where(is_b, db, 0.0)).astype(o_ref.dtype)
        st_ref[0:1, :] += jnp.sum(jnp.where(is_a, dv * g, 0.0), axis=0, keepdims=True)
        st_ref[1:2, :] += jnp.sum(jnp.where(is_a, da, 0.0), axis=0, keepdims=True)

    return pl.pallas_call(
        body, name=name, grid=(t // ROWS,),
        in_specs=[_row_spec(LANES), _const_spec((8, LANES)), _row_spec(LANES)],
        out_specs=(_row_spec(LANES), _const_spec((8, LANES))),
        out_shape=(jax.ShapeDtypeStruct((t, LANES), BF16), jax.ShapeDtypeStruct((8, LANES), F32)),
        compiler_params=_params("arbitrary"),
    )(ab, prm, dgb)


def _gdn_local(qs, ks, vs, gbs, bbs):
    nh = len(qs)
    cs = qs[0].shape[0]
    hs = range(nh)
    r = lax.broadcasted_iota(jnp.int32, (cs, cs), 0)
    c = lax.broadcasted_iota(jnp.int32, (cs, cs), 1)
    tril, strict, eye = r >= c, r > c, r == c
    ident = jnp.where(eye, 1.0, 0.0)
    g_colb = [gbs[h][:, :cs] for h in hs]
    g_row = [jnp.sum(jnp.where(eye, g_colb[h], 0.0), axis=0, keepdims=True) for h in hs]
    gc_col = [jnp.sum(jnp.where(tril, g_row[h], 0.0), axis=1, keepdims=True) for h in hs]
    gc_row = [jnp.sum(jnp.where(r <= c, g_colb[h], 0.0), axis=0, keepdims=True) for h in hs]
    decay = [jnp.exp(jnp.where(tril, gc_col[h] - gc_row[h], NEG)) for h in hs]
    gamma = [jnp.exp(gc_col[h]) for h in hs]
    gcl = [gc_col[h][cs - 1:cs, :] for h in hs]
    gl = [jnp.exp(gcl[h]) for h in hs]
    kdec = [jnp.exp(gcl[h] - gc_col[h]) for h in hs]
    kb = [ks[h] * bbs[h] for h in hs]
    kk = [_dotb(kb[h], ks[h], NT) for h in hs]
    qk = [_dotb(qs[h], ks[h], NT) for h in hs]
    lmat = [jnp.where(strict, kk[h] * decay[h], 0.0) for h in hs]
    pmat = [jnp.where(tril, qk[h] * decay[h], 0.0) for h in hs]
    xm = [-lmat[h] for h in hs]
    tinv = [ident + xm[h] for h in hs]
    for _ in range(int(math.log2(cs)) - 1):
        xm = [_dotf(xm[h], xm[h], NN) for h in hs]
        tinv = [tinv[h] + _dotf(tinv[h], xm[h], NN) for h in hs]
    vb = [vs[h] * bbs[h] for h in hs]
    kg = [kb[h] * gamma[h] for h in hs]
    u = [_dotf(tinv[h], vb[h], NN) for h in hs]
    w = [_dotf(tinv[h], kg[h], NN) for h in hs]
    return [dict(tril=tril, strict=strict, eye=eye, r=r, c=c, decay=decay[h], gamma=gamma[h], gl=gl[h], kdec=kdec[h],
                 kb=kb[h], lmat=lmat[h], tinv=tinv[h], vb=vb[h], kg=kg[h], u=u[h], w=w[h], pmat=pmat[h],
                 qd=qs[h] * gamma[h], kd=ks[h] * kdec[h]) for h in hs]


def _gdn_chunk_fwd(qkv, gbc, bbc, *, name):
    t = qkv.shape[0]
    nh, cs, hd = GDN_HEADS, GDN_CHUNK, GDN_HEAD_DIM
    nc = t // cs

    hb = GDN_HEAD_BATCH
    ng = nh // hb

    def body(q_ref, k_ref, v_ref, g_ref, b_ref, o_ref, st_ref, s_ref):
        @pl.when(pl.program_id(1) == 0)
        def _():
            s_ref[...] = jnp.zeros_like(s_ref)

        sls = [slice(i * hd, (i + 1) * hd) for i in range(hb)]
        hs = range(hb)
        s = [s_ref[i] for i in hs]
        lo = _gdn_local([q_ref[:, sl] for sl in sls], [k_ref[:, sl] for sl in sls], [v_ref[:, sl] for sl in sls],
                        [g_ref[i] for i in hs], [b_ref[i] for i in hs])
        ws = [_dotb(lo[i]["w"], s[i], NN) for i in hs]
        qs = [_dotb(lo[i]["qd"], s[i], NN) for i in hs]
        vn = [lo[i]["u"] - ws[i] for i in hs]
        pv = [_dotb(lo[i]["pmat"], vn[i], NN) for i in hs]
        kv = [_dotb(lo[i]["kd"], vn[i], TN) for i in hs]
        for i, sl in enumerate(sls):
            st_ref[i, 0] = s[i]
            o_ref[:, sl] = qs[i] + pv[i]
            s_ref[i] = s[i] * lo[i]["gl"] + kv[i]

    gspec = pl.BlockSpec((hb, cs, LANES), lambda h, n: (h, n, 0))
    col = lambda off: pl.BlockSpec((cs, hb * hd), lambda h, n: (n, off + h))
    return pl.pallas_call(
        body, name=name, grid=(ng, nc),
        in_specs=[col(0), col(ng), col(2 * ng), gspec, gspec],
        out_specs=(col(0), pl.BlockSpec((hb, 1, hd, hd), lambda h, n: (h, n, 0, 0))),
        out_shape=(jax.ShapeDtypeStruct((t, nh * hd), F32), jax.ShapeDtypeStruct((nh, nc, hd, hd), F32)),
        scratch_shapes=[pltpu.VMEM((hb, hd, hd), F32)],
        compiler_params=_params("parallel", "arbitrary"),
    )(qkv, qkv, qkv, gbc, bbc)


def _gdn_chunk_bwd(qkv, gbc, bbc, states, do, *, name):
    t = qkv.shape[0]
    nh, cs, hd = GDN_HEADS, GDN_CHUNK, GDN_HEAD_DIM
    nc = t // cs

    hb = GDN_HEAD_BATCH
    ng = nh // hb

    def heads_bwd(q, k, v, gb, bb, s, dsn, dov):
        hs = range(len(q))
        lo = _gdn_local(q, k, v, gb, bb)
        tril, strict, eye, r, c = lo[0]["tril"], lo[0]["strict"], lo[0]["eye"], lo[0]["r"], lo[0]["c"]
        rowi = lax.broadcasted_iota(jnp.int32, (cs, 1), 0)
        get = lambda name: [lo[h][name] for h in hs]
        decay, gamma, gl, kdec = get("decay"), get("gamma"), get("gl"), get("kdec")
        kb, tinv, w, pmat, kd, qd = get("kb"), get("tinv"), get("w"), get("pmat"), get("kd"), get("qd")
        ws = [_dotb(w[h], s[h], NN) for h in hs]
        pdo = [_dotb(pmat[h], dov[h], TN) for h in hs]
        kds = [_dotb(kd[h], dsn[h], NN) for h in hs]
        dqd = [_dotb(dov[h], s[h], NT) for h in hs]
        qdo = [_dotb(qd[h], dov[h], TN) for h in hs]
        vn = [lo[h]["u"] - ws[h] for h in hs]
        dvn = [pdo[h] + kds[h] for h in hs]
        dp = [jnp.where(tril, _dotb(dov[h], vn[h], NT), 0.0) for h in hs]
        dkd = [_dotb(vn[h], dsn[h], NT) for h in hs]
        dw = [-_dotb(dvn[h], s[h], NT) for h in hs]
        wdv = [_dotb(w[h], dvn[h], TN) for h in hs]
        dvb = [_dotf(tinv[h], dvn[h], TN) for h in hs]
        dt1 = [_dotf(dvn[h], lo[h]["vb"], NT) for h in hs]
        dkg = [_dotf(tinv[h], dw[h], TN) for h in hs]
        dt2 = [_dotf(dw[h], lo[h]["kg"], NT) for h in hs]
        tdt = [_dotf(tinv[h], dt1[h] + dt2[h], TN) for h in hs]
        dl = [jnp.where(strict, -_dotf(tdt[h], tinv[h], NT), 0.0) for h in hs]
        dkk = [dl[h] * decay[h] for h in hs]
        dqk = [dp[h] * decay[h] for h in hs]
        dkb = [_dotb(dkk[h], k[h], NN) + dkg[h] * gamma[h] for h in hs]
        dk1 = [_dotb(dkk[h], kb[h], TN) for h in hs]
        dk2 = [_dotb(dqk[h], q[h], TN) for h in hs]
        dq1 = [_dotb(dqk[h], k[h], NN) for h in hs]
        out = []
        for h in hs:
            dgl = jnp.sum(jnp.sum(dsn[h] * s[h], axis=1, keepdims=True), axis=0, keepdims=True)
            ds_prev = gl[h] * dsn[h] + qdo[h] - wdv[h]
            dk = dk1[h] + dk2[h] + dkd[h] * kdec[h] + dkb[h] * bb[h]
            dq = dq1[h] + dqd[h] * gamma[h]
            dbeta = jnp.sum(dvb[h] * v[h], axis=-1, keepdims=True) + jnp.sum(dkb[h] * k[h], axis=-1, keepdims=True)
            e = dl[h] * lo[h]["lmat"] + dp[h] * pmat[h]
            e_col = jnp.sum(e, axis=0, keepdims=True)
            dgc = jnp.sum(e, axis=1, keepdims=True) - jnp.sum(jnp.where(eye, e_col, 0.0), axis=1, keepdims=True)
            dgamma = (jnp.sum(dqd[h] * q[h], axis=-1, keepdims=True)
                      + jnp.sum(dkg[h] * kb[h], axis=-1, keepdims=True))
            rk = jnp.sum(dkd[h] * k[h], axis=-1, keepdims=True) * kdec[h]
            dgcl = jnp.sum(rk, axis=0, keepdims=True) + dgl * gl[h]
            dgc = dgc + dgamma * gamma[h] - rk + jnp.where(rowi == cs - 1, dgcl, 0.0)
            dgc_row = jnp.sum(jnp.where(eye, dgc, 0.0), axis=0, keepdims=True)
            dg = jnp.sum(jnp.where(c >= r, dgc_row, 0.0), axis=1, keepdims=True)
            out.append((dq, dk, dvb[h] * bb[h], dbeta, dg, ds_prev))
        return out

    def body(q_ref, k_ref, v_ref, g_ref, b_ref, st_ref, do_ref, dq_ref, dk_ref, dv_ref, dg_ref, db_ref, ds_ref):
        @pl.when(pl.program_id(1) == 0)
        def _():
            ds_ref[...] = jnp.zeros_like(ds_ref)

        sls = [slice(i * hd, (i + 1) * hd) for i in range(hb)]
        hs = range(hb)
        outs = heads_bwd([q_ref[:, sl] for sl in sls], [k_ref[:, sl] for sl in sls], [v_ref[:, sl] for sl in sls],
                         [g_ref[i] for i in hs], [b_ref[i] for i in hs], [st_ref[i, 0] for i in hs],
                         [ds_ref[i] for i in hs], [do_ref[:, sl] for sl in sls])
        for i, sl in enumerate(sls):
            dq, dk, dv, dbeta, dg, ds_prev = outs[i]
            dq_ref[:, sl], dk_ref[:, sl], dv_ref[:, sl] = dq, dk, dv
            db_ref[i] = jnp.broadcast_to(dbeta, (cs, LANES))
            dg_ref[i] = jnp.broadcast_to(dg, (cs, LANES))
            ds_ref[i] = ds_prev

    gspec = pl.BlockSpec((hb, cs, LANES), lambda h, n: (h, nc - 1 - n, 0))
    col = lambda off: pl.BlockSpec((cs, hb * hd), lambda h, n: (nc - 1 - n, off + h))
    return pl.pallas_call(
        body, name=name, grid=(ng, nc),
        in_specs=[col(0), col(ng), col(2 * ng), gspec, gspec,
                  pl.BlockSpec((hb, 1, hd, hd), lambda h, n: (h, nc - 1 - n, 0, 0)), col(0)],
        out_specs=(col(0), col(0), col(0), gspec, gspec),
        out_shape=(jax.ShapeDtypeStruct((t, nh * hd), F32),) * 3
        + (jax.ShapeDtypeStruct((nh, t, LANES), F32),) * 2,
        scratch_shapes=[pltpu.VMEM((hb, hd, hd), F32)],
        compiler_params=_params("parallel", "arbitrary"),
    )(qkv, qkv, qkv, gbc, bbc, states, do)


def _gdn_onorm_fwd(o, proj, norm_g, *, name):
    t = o.shape[0]
    w = GDN_KEY_DIM
    goff = 3 * GDN_KEY_DIM // w

    def body(o_ref, gp_ref, g_ref, y_ref):
        gv = g_ref[...]
        for h in range(GDN_HEADS):
            sl = slice(h * GDN_HEAD_DIM, (h + 1) * GDN_HEAD_DIM)
            oh = o_ref[:, sl]
            gp = gp_ref[:, sl]
            r = lax.rsqrt(jnp.mean(oh * oh, axis=-1, keepdims=True) + EPS)
            y_ref[:, sl] = (oh * r * gv * gp * _sigmoid(gp)).astype(y_ref.dtype)

    return pl.pallas_call(
        body, name=name, grid=(t // ROWS,),
        in_specs=[_row_spec(w), pl.BlockSpec((ROWS, w), lambda i: (i, goff)), _const_spec((1, GDN_HEAD_DIM))],
        out_specs=_row_spec(w), out_shape=jax.ShapeDtypeStruct((t, w), BF16),
        compiler_params=_params("parallel"),
    )(o, proj, norm_g)


def _gdn_onorm_bwd(o, proj, norm_g, dy, *, name):
    t = o.shape[0]
    w = GDN_KEY_DIM
    goff = 3 * GDN_KEY_DIM // w

    def body(o_ref, gp_ref, g_ref, dy_ref, do_ref, dgp_ref, st_ref):
        @pl.when(pl.program_id(0) == 0)
        def _():
            st_ref[...] = jnp.zeros_like(st_ref)

        gv = g_ref[...]
        acc = jnp.zeros((1, GDN_HEAD_DIM), F32)
        for h in range(GDN_HEADS):
            sl = slice(h * GDN_HEAD_DIM, (h + 1) * GDN_HEAD_DIM)
            oh = o_ref[:, sl]
            gp = gp_ref[:, sl]
            dyv = dy_ref[:, sl].astype(F32)
            r = lax.rsqrt(jnp.mean(oh * oh, axis=-1, keepdims=True) + EPS)
            xh = oh * r
            sg = _sigmoid(gp)
            dn = dyv * gp * sg
            dgp_ref[:, sl] = (dyv * xh * gv * sg * (1.0 + gp * (1.0 - sg))).astype(dgp_ref.dtype)
            acc = acc + jnp.sum(dn * xh, axis=0, keepdims=True)
            dxh = dn * gv
            do_ref[:, sl] = r * (dxh - xh * jnp.mean(dxh * xh, axis=-1, keepdims=True))
        st_ref[0:1, :] += acc

    return pl.pallas_call(
        body, name=name, grid=(t // ROWS,),
        in_specs=[_row_spec(w), pl.BlockSpec((ROWS, w), lambda i: (i, goff)), _const_spec((1, GDN_HEAD_DIM)),
                  _row_spec(w)],
        out_specs=(_row_spec(w), _row_spec(w), _const_spec((8, GDN_HEAD_DIM))),
        out_shape=(jax.ShapeDtypeStruct((t, w), F32), jax.ShapeDtypeStruct((t, w), BF16),
                   jax.ShapeDtypeStruct((8, GDN_HEAD_DIM), F32)),
        compiler_params=_params("arbitrary"),
    )(o, proj, norm_g, dy)


def _mla_prep_fwd(proj, qg, kvg, *, name):
    t = proj.shape[0]
    q1, k1 = MLA_Q_RANK, MLA_Q_RANK + MLA_KV_RANK

    def body(p_ref, qg_ref, kg_ref, cq_ref, ck_ref):
        cq = p_ref[:, 0:q1]
        ck = p_ref[:, q1:k1]
        cq_ref[...] = (cq * lax.rsqrt(jnp.mean(cq * cq, axis=-1, keepdims=True) + EPS) * qg_ref[...]).astype(BF16)
        ck_ref[...] = (ck * lax.rsqrt(jnp.mean(ck * ck, axis=-1, keepdims=True) + EPS) * kg_ref[...]).astype(BF16)

    return pl.pallas_call(
        body, name=name, grid=(t // ROWS,),
        in_specs=[_row_spec(MLA_IN), _const_spec((1, MLA_Q_RANK)), _const_spec((1, MLA_KV_RANK))],
        out_specs=(_row_spec(MLA_Q_RANK), _row_spec(MLA_KV_RANK)),
        out_shape=(jax.ShapeDtypeStruct((t, MLA_Q_RANK), BF16), jax.ShapeDtypeStruct((t, MLA_KV_RANK), BF16)),
        compiler_params=_params("parallel"),
    )(proj, qg, kvg)


def _mla_prep_bwd(proj, qg, kvg, dcq, dck, dkr, *, name):
    t = proj.shape[0]
    q1, k1 = MLA_Q_RANK, MLA_Q_RANK + MLA_KV_RANK

    def body(p_ref, qg_ref, kg_ref, dq_ref, dk_ref, dr_ref, dp_ref, st_ref):
        @pl.when(pl.program_id(0) == 0)
        def _():
            st_ref[...] = jnp.zeros_like(st_ref)

        for lo, hi, g_ref, d_ref in ((0, q1, qg_ref, dq_ref), (q1, k1, kg_ref, dk_ref)):
            xv = p_ref[:, lo:hi]
            dn = d_ref[...]
            r = lax.rsqrt(jnp.mean(xv * xv, axis=-1, keepdims=True) + EPS)
            xh = xv * r
            dxh = dn * g_ref[...]
            dp_ref[:, lo:hi] = (r * (dxh - xh * jnp.mean(dxh * xh, axis=-1, keepdims=True))).astype(dp_ref.dtype)
            st_ref[0:1, lo:hi] += jnp.sum(dn * xh, axis=0, keepdims=True)
        dp_ref[:, k1:MLA_IN] = dr_ref[...].astype(dp_ref.dtype)

    return pl.pallas_call(
        body, name=name, grid=(t // ROWS,),
        in_specs=[_row_spec(MLA_IN), _const_spec((1, MLA_Q_RANK)), _const_spec((1, MLA_KV_RANK)),
                  _row_spec(MLA_Q_RANK), _row_spec(MLA_KV_RANK), _row_spec(MLA_ROPE)],
        out_specs=(_row_spec(MLA_IN), _const_spec((8, MLA_IN))),
        out_shape=(jax.ShapeDtypeStruct((t, MLA_IN), BF16), jax.ShapeDtypeStruct((8, MLA_IN), F32)),
        compiler_params=_params("arbitrary"),
    )(proj, qg, kvg, dcq, dck, dkr)


def _rope(xr, cos_t, sin_t, *, name):
    t, w = xr.shape
    ns = w // LANES

    def body(x_ref, c_ref, s_ref, o_ref):
        cv, sv = c_ref[...], s_ref[...]
        lane = lax.broadcasted_iota(jnp.int32, (ROWS, LANES), 1)
        first = (lane % MLA_ROPE) < (MLA_ROPE // 2)
        for i in range(ns):
            sl = slice(i * LANES, (i + 1) * LANES)
            xv = x_ref[:, sl]
            sw = jnp.where(first, pltpu.roll(xv, LANES - MLA_ROPE // 2, 1), pltpu.roll(xv, MLA_ROPE // 2, 1))
            o_ref[:, sl] = xv * cv + sw * sv

    return pl.pallas_call(
        body, name=name, grid=(t // ROWS,),
        in_specs=[_row_spec(w), _row_spec(LANES), _row_spec(LANES)], out_specs=_row_spec(w),
        out_shape=jax.ShapeDtypeStruct((t, w), F32), compiler_params=_params("parallel"),
    )(xr, cos_t, sin_t)


def _rope_bwd(dr, cos_t, sin_t, *, name):
    t, w = dr.shape
    ns = w // LANES

    def body(d_ref, c_ref, s_ref, o_ref):
        cv, sv = c_ref[...], s_ref[...]
        lane = lax.broadcasted_iota(jnp.int32, (ROWS, LANES), 1)
        first = (lane % MLA_ROPE) < (MLA_ROPE // 2)
        for i in range(ns):
            sl = slice(i * LANES, (i + 1) * LANES)
            dv = d_ref[:, sl]
            ds = dv * sv
            sw = jnp.where(first, pltpu.roll(ds, LANES - MLA_ROPE // 2, 1), pltpu.roll(ds, MLA_ROPE // 2, 1))
            o_ref[:, sl] = dv * cv + sw

    return pl.pallas_call(
        body, name=name, grid=(t // ROWS,),
        in_specs=[_row_spec(w), _row_spec(LANES), _row_spec(LANES)], out_specs=_row_spec(w),
        out_shape=jax.ShapeDtypeStruct((t, w), F32), compiler_params=_params("parallel"),
    )(dr, cos_t, sin_t)


ATT_BLOCK = 256
ATT_SCALE = MLA_QK ** -0.5


def _causal_mask(i, j, blk):
    rows = i * blk + lax.broadcasted_iota(jnp.int32, (blk, blk), 0)
    cols = j * blk + lax.broadcasted_iota(jnp.int32, (blk, blk), 1)
    return cols <= rows


def _attn_fwd(q, k, v, *, name):
    nh, t, dk = q.shape
    dv = v.shape[-1]
    blk = min(ATT_BLOCK, t)

    def body(q_ref, k_ref, v_ref, o_ref, l_ref):
        i = pl.program_id(1)
        qv = q_ref[0]

        def step(j, carry):
            m, l, acc = carry
            off = pl.multiple_of(j * blk, blk)
            s = _dotb(qv, k_ref[0, pl.ds(off, blk), :], NT) * ATT_SCALE
            s = jnp.where(_causal_mask(i, j, blk), s, NEG)
            m_new = jnp.maximum(m, jnp.max(s, axis=-1, keepdims=True))
            p = jnp.exp(s - m_new)
            alpha = jnp.exp(m - m_new)
            l = alpha * l + jnp.sum(p, axis=-1, keepdims=True)
            acc = alpha * acc + _dotb(p, v_ref[0, pl.ds(off, blk), :], NN)
            return m_new, l, acc

        init = (jnp.full((blk, 1), NEG, F32), jnp.zeros((blk, 1), F32), jnp.zeros((blk, dv), F32))
        m, l, acc = lax.fori_loop(0, i + 1, step, init)
        o_ref[0] = acc / l
        l_ref[0] = jnp.broadcast_to(m + jnp.log(l), (blk, LANES))

    return pl.pallas_call(
        body, name=name, grid=(nh, t // blk),
        in_specs=[pl.BlockSpec((1, blk, dk), lambda h, i: (h, i, 0)), pl.BlockSpec((1, t, dk), lambda h, i: (h, 0, 0)),
                  pl.BlockSpec((1, t, dv), lambda h, i: (h, 0, 0))],
        out_specs=(pl.BlockSpec((1, blk, dv), lambda h, i: (h, i, 0)),
                   pl.BlockSpec((1, blk, LANES), lambda h, i: (h, i, 0))),
        out_shape=(jax.ShapeDtypeStruct((nh, t, dv), F32), jax.ShapeDtypeStruct((nh, t, LANES), F32)),
        compiler_params=_params("parallel", "parallel"),
    )(q, k, v)


def _attn_bwd(q, k, v, o, lse, do, *, name):
    nh, t, dk = q.shape
    dv = v.shape[-1]
    blk = min(ATT_BLOCK, t)
    nb = t // blk

    def body(q_ref, k_ref, v_ref, o_ref, l_ref, do_ref, dq_ref, dk_ref, dv_ref):
        j = pl.program_id(1)

        @pl.when(j == 0)
        def _():
            dq_ref[...] = jnp.zeros_like(dq_ref)

        kv, vv = k_ref[0], v_ref[0]

        def step(i, carry):
            dk_acc, dv_acc = carry
            off = pl.multiple_of(i * blk, blk)
            rows = pl.ds(off, blk)
            qv = q_ref[0, rows, :]
            dov = do_ref[0, rows, :]
            s = _dotb(qv, kv, NT) * ATT_SCALE
            s = jnp.where(_causal_mask(i, j, blk), s, NEG)
            p = jnp.exp(s - l_ref[0, rows, :][:, 0:1])
            dv_acc = dv_acc + _dotb(p, dov, TN)
            dp = _dotb(dov, vv, NT)
            delta = jnp.sum(dov * o_ref[0, rows, :], axis=-1, keepdims=True)
            ds = p * (dp - delta) * ATT_SCALE
            dk_acc = dk_acc + _dotb(ds, qv, TN)
            dq_ref[0, rows, :] += _dotb(ds, kv, NN)
            return dk_acc, dv_acc

        dk_acc, dv_acc = lax.fori_loop(j, nb, step, (jnp.zeros((blk, dk), F32), jnp.zeros((blk, dv), F32)))
        dk_ref[0] = dk_acc
        dv_ref[0] = dv_acc

    full = lambda w: pl.BlockSpec((1, t, w), lambda h, j: (h, 0, 0))
    part = lambda w: pl.BlockSpec((1, blk, w), lambda h, j: (h, j, 0))
    return pl.pallas_call(
        body, name=name, grid=(nh, nb),
        in_specs=[full(dk), part(dk), part(dv), full(dv), full(LANES), full(dv)],
        out_specs=(full(dk), part(dk), part(dv)),
        out_shape=(jax.ShapeDtypeStruct((nh, t, dk), F32), jax.ShapeDtypeStruct((nh, t, dk), F32),
                   jax.ShapeDtypeStruct((nh, t, dv), F32)),
        compiler_params=_params("parallel", "arbitrary"),
    )(q, k, v, o, lse, do)


def _ada_mod(c_all, ada_w, ada_b_cols, *, name):
    nl, d, wc = ada_w.shape

    def body(c_ref, w_ref, b_ref, o_ref):
        cv = c_ref[...]
        o_ref[0] = _dotb(cv * _sigmoid(cv), w_ref[0], NN) + b_ref[0]

    return pl.pallas_call(
        body, name=name, grid=(nl,),
        in_specs=[_const_spec((N_DEV, d)), pl.BlockSpec((1, d, wc), lambda l: (l, 0, 0)),
                  pl.BlockSpec((1, 1, wc), lambda l: (l, 0, 0))],
        out_specs=pl.BlockSpec((1, N_DEV, wc), lambda l: (l, 0, 0)),
        out_shape=jax.ShapeDtypeStruct((nl, N_DEV, wc), F32), compiler_params=_params("parallel"),
    )(c_all, ada_w, ada_b_cols)


def _adam_math(g, w, m, v):
    m2 = ADAM_B1 * m + (1.0 - ADAM_B1) * g
    v2 = ADAM_B2 * v + (1.0 - ADAM_B2) * (g * g)
    delta = -ADAM_LR * ((m2 / ADAM_BC1) / (jnp.sqrt(v2 / ADAM_BC2) + ADAM_EPS) + ADAM_WD * w)
    return delta, m2, v2


def _ada_grad_adamw(c_all, dmod_cols, w, m, v, *, name):
    nl, d, wc = w.shape
    tr = 256

    def body(c_ref, dm_ref, w_ref, m_ref, v_ref, g_ref, d_ref, m2_ref, v2_ref):
        cv = c_ref[...]
        g = _dotf(cv * _sigmoid(cv), dm_ref[0], TN)
        delta, m2, v2 = _adam_math(g, w_ref[0], m_ref[0], v_ref[0])
        g_ref[0], d_ref[0], m2_ref[0], v2_ref[0] = g, delta, m2, v2

    blk = pl.BlockSpec((1, tr, wc), lambda l, i: (l, i, 0))
    return pl.pallas_call(
        body, name=name, grid=(nl, d // tr),
        in_specs=[pl.BlockSpec((N_DEV, tr), lambda l, i: (0, i)), pl.BlockSpec((1, N_DEV, wc), lambda l, i: (l, 0, 0)),
                  blk, blk, blk],
        out_specs=(blk,) * 4, out_shape=(jax.ShapeDtypeStruct(w.shape, F32),) * 4,
        compiler_params=_params("parallel", "parallel"),
    )(c_all, dmod_cols, w, m, v)


def _adamw(parts, w, m, v, *, name):
    nl, r, c = w.shape
    ns = parts[0].shape[0]
    lanes_padded = -(-c // LANES) * LANES
    row_bytes = 2 * nl * ns * lanes_padded * parts[0].dtype.itemsize
    tr = _pick(r, min(256, max(16, (VMEM_LIMIT // 2) // row_bytes)), 16)

    def body(*refs):
        p_refs = refs[:nl]
        w_ref, m_ref, v_ref, g_ref, d_ref, m2_ref, v2_ref = refs[nl:]
        layer = pl.program_id(0)
        for q in range(nl):
            @pl.when(layer == q)
            def _(q=q):
                g = p_refs[q][0].astype(F32)
                for s in range(1, ns):
                    g = g + p_refs[q][s].astype(F32)
                delta, m2, v2 = _adam_math(g, w_ref[0], m_ref[0], v_ref[0])
                g_ref[0], d_ref[0], m2_ref[0], v2_ref[0] = g, delta, m2, v2

    blk = pl.BlockSpec((1, tr, c), lambda l, i: (l, i, 0))
    p_specs = [pl.BlockSpec((ns, tr, c), lambda l, i, q=q: (0, jnp.where(l == q, i, 0), 0)) for q in range(nl)]
    return pl.pallas_call(
        body, name=name, grid=(nl, r // tr),
        in_specs=p_specs + [blk, blk, blk],
        out_specs=(blk,) * 4, out_shape=(jax.ShapeDtypeStruct(w.shape, F32),) * 4,
        compiler_params=_params("arbitrary", "arbitrary"),
    )(*parts, w, m, v)


def _sum_parts(parts, *, name):
    ns, r, c = parts.shape

    def body(p_ref, o_ref):
        acc = p_ref[0]
        for s in range(1, ns):
            acc = acc + p_ref[s]
        o_ref[...] = acc

    return pl.pallas_call(
        body, name=name, out_shape=jax.ShapeDtypeStruct((r, c), F32),
        in_specs=[pl.BlockSpec(memory_space=pltpu.VMEM)], out_specs=pl.BlockSpec(memory_space=pltpu.VMEM),
    )(parts)


def _pack(arrs):
    flat = jnp.concatenate([a.reshape(-1).astype(F32) for a in arrs])
    pad = (-flat.shape[0]) % (8 * LANES)
    return jnp.pad(flat, (0, pad)).reshape(-1, LANES)


def _unpack(packed, shapes, lead=()):
    flat = packed.reshape(lead + (-1,))
    out, off = [], 0
    for s in shapes:
        n = math.prod(s)
        out.append(flat[..., off:off + n].reshape(lead + tuple(s)))
        off += n
    return out


def _gather_cols(g):
    _, nl, r, cs = g.shape
    return jnp.transpose(g, (1, 2, 0, 3)).reshape(nl, r, N_DEV * cs)


def _gather_rows(g):
    _, nl, rs, c = g.shape
    return jnp.transpose(g, (1, 0, 2, 3)).reshape(nl, N_DEV * rs, c)


def _scatter_cols(full):
    nl, r, c = full.shape
    return jnp.transpose(full.reshape(nl, r, N_DEV, c // N_DEV), (2, 0, 1, 3))


def _scatter_rows(full):
    nl, r, c = full.shape
    return jnp.transpose(full.reshape(nl, N_DEV, r // N_DEV, c), (1, 0, 2, 3))


def _row(v):
    return v.reshape(1, -1)


def _local_step(x, target, mod, cos_t, sin_t, rep, get_weights, put_grads):
    t = x.shape[0]
    saved = []
    for layer in range(DEPTH):
        j = layer // 2
        tag = f"l{layer}"
        shift_m, scale_m, gate_m, shift_f, scale_f, gate_f = [_row(mod[layer, i]) for i in range(N_MOD)]
        lw = dict(get_weights(layer, "mix", x))
        rec = {"x0": x, "lw": lw}
        h = _adaln_fwd(x, _row(rep["norm_mix_g"][layer]), scale_m, shift_m, name=f"adaln_mix_{tag}")
        rec["h"] = h
        if layer % 2 == 0:
            proj = _mm(h, lw["w_main"], mode="nn", out_dtype=F32, name=f"gdn_in_{tag}")
            ab = _mm(h, lw["w_ab"], mode="nn", out_dtype=F32, name=f"gdn_in_ab_{tag}")
            qkv = _gdn_prep_fwd(proj, rep["gdn_conv_wt"][j], name=f"gdn_prep_{tag}")
            gbeta = _gdn_gate_fwd(ab, rep["gdn_gate_prm"][j], name=f"gdn_gate_{tag}")
            gbc = jnp.broadcast_to(jnp.transpose(gbeta[:, 0:GDN_HEADS])[:, :, None], (GDN_HEADS, t, LANES))
            bbc = jnp.broadcast_to(jnp.transpose(gbeta[:, GDN_HEADS:2 * GDN_HEADS])[:, :, None],
                                   (GDN_HEADS, t, LANES))
            o, states = _gdn_chunk_fwd(qkv, gbc, bbc, name=f"gdn_chunk_{tag}")
            og = _gdn_onorm_fwd(o, proj, _row(rep["gdn_norm_g"][j]), name=f"gdn_onorm_{tag}")
            x, y = _mm_resid(og, lw["w_out"], x, gate_m, name=f"gdn_out_{tag}")
            rec.update(proj=proj, ab=ab, qkv=qkv, gbc=gbc, bbc=bbc, states=states, o=o, og=og, y=y)
        else:
            proj = _mm(h, lw["w_in"], mode="nn", out_dtype=F32, name=f"mla_in_{tag}")
            cq, ck = _mla_prep_fwd(proj, _row(rep["mla_q_norm_g"][j]), _row(rep["mla_kv_norm_g"][j]),
                                   name=f"mla_prep_{tag}")
            qf = _mm(cq, lw["w_uq"], mode="nn", out_dtype=F32, name=f"mla_uq_{tag}")
            kvf = _mm(ck, lw["w_ukv"], mode="nn", out_dtype=F32, name=f"mla_ukv_{tag}")
            nrope = MLA_HEADS * MLA_ROPE
            krp = jnp.pad(proj[:, MLA_Q_RANK + MLA_KV_RANK:], ((0, 0), (0, LANES - MLA_ROPE)))
            roped = _rope(jnp.concatenate([qf[:, MLA_HEADS * MLA_NOPE:], krp], axis=1), cos_t, sin_t,
                          name=f"rope_{tag}")
            q_nope = qf[:, :MLA_HEADS * MLA_NOPE].reshape(t, MLA_HEADS, MLA_NOPE)
            q_rope = roped[:, :nrope].reshape(t, MLA_HEADS, MLA_ROPE)
            k_rope = jnp.broadcast_to(roped[:, None, nrope:nrope + MLA_ROPE], (t, MLA_HEADS, MLA_ROPE))
            kv3 = kvf.reshape(t, MLA_HEADS, MLA_NOPE + MLA_V)
            qc = jnp.transpose(jnp.concatenate([q_nope, q_rope], axis=-1), (1, 0, 2)).astype(BF16)
            kc = jnp.transpose(jnp.concatenate([kv3[..., :MLA_NOPE], k_rope], axis=-1), (1, 0, 2)).astype(BF16)
            vc = jnp.transpose(kv3[..., MLA_NOPE:], (1, 0, 2)).astype(BF16)
            oh, lse = _attn_fwd(qc, kc, vc, name=f"attn_{tag}")
            oc = jnp.transpose(oh, (1, 0, 2)).reshape(t, MLA_HEADS * MLA_V).astype(BF16)
            x, y = _mm_resid(oc, lw["w_out"], x, gate_m, name=f"mla_out_{tag}")
            rec.update(proj=proj, cq=cq, ck=ck, qc=qc, kc=kc, vc=vc, oh=oh, lse=lse, oc=oc, y=y)
        rec["x1"] = x
        lw.update(get_weights(layer, "ffn", x))
        h2 = _adaln_fwd(x, _row(rep["norm_ffn_g"][layer]), scale_f, shift_f, name=f"adaln_ffn_{tag}")
        ab2 = _mm(h2, lw["w_gu"], mode="nn", out_dtype=F32, name=f"ffn_gu_{tag}")
        s = _swiglu_fwd(ab2, name=f"swiglu_{tag}")
        x, y2 = _mm_resid(s, lw["w_down"], x, gate_f, name=f"ffn_down_{tag}")
        rec.update(h2=h2, ab2=ab2, s=s, y2=y2)
        saved.append(rec)

    dx, st, ls = _loss_head(x, _row(rep["final_norm_g"]), target, name="loss_head")
    loss = ls[0, 0]
    grads = {"final_norm_g": st[0]}
    per_layer = {k: [None] * DEPTH for k in ("norm_mix_g", "norm_ffn_g")}
    per_gdn = {k: [None] * 2 for k in ("gdn_conv_wt", "gdn_a_log", "gdn_dt_bias", "gdn_norm_g")}
    per_mla = {k: [None] * 2 for k in ("mla_q_norm_g", "mla_kv_norm_g")}
    dmod = [None] * DEPTH
    dep = jnp.zeros((8, LANES), F32)

    for layer in reversed(range(DEPTH)):
        j = layer // 2
        tag = f"l{layer}"
        rec = saved[layer]
        lw = rec["lw"]
        shift_m, scale_m, gate_m, shift_f, scale_f, gate_f = [_row(mod[layer, i]) for i in range(N_MOD)]
        dy2, st_g = _gate_bwd(dx, rec["y2"], gate_f, dep, name=f"gate_bwd_ffn_{tag}")
        dgate_f = st_g[0]
        dw_down = _mm(rec["s"], dy2, mode="tn", out_dtype=BF16, name=f"ffn_down_dw_{tag}")
        ds = _mm(dy2, lw["w_down"], mode="nt", out_dtype=F32, name=f"ffn_down_dx_{tag}")
        dab2 = _swiglu_bwd(rec["ab2"], ds, name=f"swiglu_bwd_{tag}")
        dw_gu = _mm(rec["h2"], dab2, mode="tn", out_dtype=BF16, name=f"ffn_gu_dw_{tag}")
        dep = put_grads(layer, "ffn", {"w_gu": dw_gu, "w_down": dw_down})
        dh2 = _mm(dab2, lw["w_gu"], mode="nt", out_dtype=BF16, name=f"ffn_gu_dx_{tag}")
        dx, st_n = _adaln_bwd(rec["x1"], _row(rep["norm_ffn_g"][layer]), scale_f, shift_f, dh2, dx,
                              name=f"adaln_ffn_bwd_{tag}")
        per_layer["norm_ffn_g"][layer] = st_n[0]
        dscale_f, dshift_f = st_n[1], st_n[2]
        dy, st_g = _gate_bwd(dx, rec["y"], gate_m, dep, name=f"gate_bwd_mix_{tag}")
        dgate_m = st_g[0]
        big = {}
        if layer % 2 == 0:
            big["w_out"] = _mm(rec["og"], dy, mode="tn", out_dtype=BF16, name=f"gdn_out_dw_{tag}")
            dog = _mm(dy, lw["w_out"], mode="nt", out_dtype=BF16, name=f"gdn_out_dx_{tag}")
            do, dgp, st_o = _gdn_onorm_bwd(rec["o"], rec["proj"], _row(rep["gdn_norm_g"][j]), dog,
                                           name=f"gdn_onorm_bwd_{tag}")
            per_gdn["gdn_norm_g"][j] = st_o[0]
            dqkv3 = _gdn_chunk_bwd(rec["qkv"], rec["gbc"], rec["bbc"], rec["states"], do, name=f"gdn_chunk_bwd_{tag}")
            dq_, dk_, dv_, dgc_, dbc_ = dqkv3
            dqkv = jnp.concatenate([dq_, dk_, dv_], axis=1)
            dgb = jnp.concatenate([jnp.transpose(dgc_[:, :, 0]), jnp.transpose(dbc_[:, :, 0])], axis=1)
            dgb = jnp.pad(dgb, ((0, 0), (0, LANES - 2 * GDN_HEADS)))
            dab, st_a = _gdn_gate_bwd(rec["ab"], rep["gdn_gate_prm"][j], dgb, name=f"gdn_gate_bwd_{tag}")
            per_gdn["gdn_a_log"][j] = st_a[0, :GDN_HEADS]
            per_gdn["gdn_dt_bias"][j] = st_a[1, :GDN_HEADS]
            dpre, dcw = _gdn_prep_bwd(rec["proj"], rep["gdn_conv_wt"][j], dqkv, name=f"gdn_prep_bwd_{tag}")
            per_gdn["gdn_conv_wt"][j] = dcw
            dproj = jnp.concatenate([dpre, dgp], axis=1)
            dw_main = _mm(rec["h"], dproj, mode="tn", out_dtype=BF16, name=f"gdn_in_dw_{tag}")
            dw_ab = _mm(rec["h"], dab, mode="tn", out_dtype=BF16, name=f"gdn_in_ab_dw_{tag}")
            big["w_in"] = jnp.concatenate([dw_main, dw_ab[:, :2 * GDN_HEADS]], axis=1)
            dep = put_grads(layer, "gdn", big)
            dh_ab = _mm(dab, lw["w_ab"], mode="nt", out_dtype=F32, name=f"gdn_in_ab_dx_{tag}")
            dh = _mm(dproj, lw["w_main"], mode="nt", out_dtype=BF16, add=dh_ab, name=f"gdn_in_dx_{tag}")
        else:
            big["w_out"] = _mm(rec["oc"], dy, mode="tn", out_dtype=BF16, name=f"mla_out_dw_{tag}")
            doc = _mm(dy, lw["w_out"], mode="nt", out_dtype=F32, name=f"mla_out_dx_{tag}")
            doh = jnp.transpose(doc.reshape(t, MLA_HEADS, MLA_V), (1, 0, 2))
            dqc, dkc, dvc = _attn_bwd(rec["qc"], rec["kc"], rec["vc"], rec["oh"], rec["lse"], doh,
                                      name=f"attn_bwd_{tag}")
            dqn = jnp.transpose(dqc[..., :MLA_NOPE], (1, 0, 2)).reshape(t, MLA_HEADS * MLA_NOPE)
            dqr = jnp.transpose(dqc[..., MLA_NOPE:], (1, 0, 2)).reshape(t, MLA_HEADS * MLA_ROPE)
            dkr = jnp.pad(jnp.sum(dkc[..., MLA_NOPE:], axis=0), ((0, 0), (0, LANES - MLA_ROPE)))
            drope = _rope_bwd(jnp.concatenate([dqr, dkr], axis=1), cos_t, sin_t, name=f"rope_bwd_{tag}")
            nrope = MLA_HEADS * MLA_ROPE
            dqf = jnp.concatenate([dqn, drope[:, :nrope]], axis=1).astype(BF16)
            dkvf = jnp.concatenate([jnp.transpose(dkc[..., :MLA_NOPE], (1, 0, 2)), jnp.transpose(dvc, (1, 0, 2))],
                                   axis=-1).reshape(t, MLA_HEADS * (MLA_NOPE + MLA_V)).astype(BF16)
            big["w_uq"] = _mm(rec["cq"], dqf, mode="tn", out_dtype=BF16, name=f"mla_uq_dw_{tag}")
            big["w_ukv"] = _mm(rec["ck"], dkvf, mode="tn", out_dtype=BF16, name=f"mla_ukv_dw_{tag}")
            dcq = _mm(dqf, lw["w_uq"], mode="nt", out_dtype=F32, name=f"mla_uq_dx_{tag}")
            dck = _mm(dkvf, lw["w_ukv"], mode="nt", out_dtype=F32, name=f"mla_ukv_dx_{tag}")
            dproj, st_p = _mla_prep_bwd(rec["proj"], _row(rep["mla_q_norm_g"][j]), _row(rep["mla_kv_norm_g"][j]),
                                        dcq, dck, drope[:, nrope:nrope + MLA_ROPE], name=f"mla_prep_bwd_{tag}")
            per_mla["mla_q_norm_g"][j] = st_p[0, :MLA_Q_RANK]
            per_mla["mla_kv_norm_g"][j] = st_p[0, MLA_Q_RANK:MLA_Q_RANK + MLA_KV_RANK]
            big["w_in"] = _mm(rec["h"], dproj, mode="tn", out_dtype=BF16, name=f"mla_in_dw_{tag}")
            dep = put_grads(layer, "mla", big)
            dh = _mm(dproj, lw["w_in"], mode="nt", out_dtype=BF16, name=f"mla_in_dx_{tag}")
        dx, st_n = _adaln_bwd(rec["x0"], _row(rep["norm_mix_g"][layer]), scale_m, shift_m, dh, dx,
                              name=f"adaln_mix_bwd_{tag}")
        per_layer["norm_mix_g"][layer] = st_n[0]
        dmod[layer] = jnp.stack([st_n[2], st_n[1], dgate_m, dshift_f, dscale_f, dgate_f])

    for d in (per_layer, per_gdn, per_mla):
        for k, v in d.items():
            grads[k] = jnp.stack(v)
    return loss, dx, jnp.stack(dmod), grads


BIG = ("gdn_w_in", "gdn_w_out", "mla_w_in", "mla_w_uq", "mla_w_ukv", "mla_w_out", "ffn_w_gate", "ffn_w_up",
       "ffn_w_down")
COL_SHARDED = ("gdn_w_in", "mla_w_uq", "mla_w_ukv", "ffn_w_gate", "ffn_w_up")
SMALL = ("ada_b", "norm_mix_g", "norm_ffn_g", "gdn_conv_w", "gdn_a_log", "gdn_dt_bias", "gdn_norm_g",
         "mla_q_norm_g", "mla_kv_norm_g", "final_norm_g")
WEIGHTS = ("ada_w", "ada_b", "norm_mix_g", "norm_ffn_g", "gdn_w_in", "gdn_conv_w", "gdn_a_log", "gdn_dt_bias",
           "gdn_norm_g", "gdn_w_out", "mla_w_in", "mla_q_norm_g", "mla_kv_norm_g", "mla_w_uq", "mla_w_ukv",
           "mla_w_out", "ffn_w_gate", "ffn_w_up", "ffn_w_down", "final_norm_g")


def _uq_to_kernel_layout(w):
    lead = w.shape[:-1]
    w4 = w.reshape(lead + (MLA_HEADS, MLA_QK))
    return jnp.concatenate([w4[..., :MLA_NOPE].reshape(lead + (-1,)), w4[..., MLA_NOPE:].reshape(lead + (-1,))],
                           axis=-1)


def _uq_from_kernel_layout(w):
    lead = w.shape[:-1]
    nope = w[..., :MLA_HEADS * MLA_NOPE].reshape(lead + (MLA_HEADS, MLA_NOPE))
    rope = w[..., MLA_HEADS * MLA_NOPE:].reshape(lead + (MLA_HEADS, MLA_ROPE))
    return jnp.concatenate([nope, rope], axis=-1).reshape(lead + (-1,))


def _group_names(layer, kind):
    if kind == "ffn":
        return ("ffn_w_gate", "ffn_w_up", "ffn_w_down")
    return ("gdn_w_in", "gdn_w_out") if layer % 2 == 0 else ("mla_w_in", "mla_w_uq", "mla_w_ukv", "mla_w_out")


def _layer_index(name, layer):
    return layer if name.startswith("ffn") else layer // 2


def _cols(g):
    return jnp.transpose(g, (1, 0, 2)).reshape(g.shape[1], N_DEV * g.shape[2])


def _rows(g):
    return g.reshape(N_DEV * g.shape[1], g.shape[2])


def _uncols(full):
    r, c = full.shape
    return jnp.transpose(full.reshape(r, N_DEV, c // N_DEV), (1, 0, 2))


def _unrows(full):
    r, c = full.shape
    return full.reshape(N_DEV, r // N_DEV, c)


def _group_weights(layer, kind, got, zero):
    if kind == "ffn":
        return {"w_gu": _gu_to_kernel_layout(_cols(got["ffn_w_gate"]), _cols(got["ffn_w_up"])) + zero,
                "w_down": _rows(got["ffn_w_down"])}
    if layer % 2 == 0:
        w_in = _cols(got["gdn_w_in"]) + zero
        return dict(w_main=w_in[:, :GDN_MAIN], w_ab=jnp.pad(w_in[:, GDN_MAIN:], ((0, 0), (0, LANES - 2 * GDN_HEADS))),
                    w_out=_rows(got["gdn_w_out"]))
    return dict(w_in=_rows(got["mla_w_in"]), w_uq=_uq_to_kernel_layout(_cols(got["mla_w_uq"])) + zero,
                w_ukv=_cols(got["mla_w_ukv"]), w_out=_rows(got["mla_w_out"]))


def _layer_grad_slots(kind, big):
    if kind == "ffn":
        d_gate, d_up = _gu_from_kernel_layout(big["w_gu"])
        return {"ffn_w_gate": _uncols(d_gate), "ffn_w_up": _uncols(d_up), "ffn_w_down": _unrows(big["w_down"])}
    if kind == "gdn":
        return {"gdn_w_in": _uncols(big["w_in"]), "gdn_w_out": _unrows(big["w_out"])}
    return {"mla_w_in": _unrows(big["w_in"]), "mla_w_uq": _uncols(_uq_from_kernel_layout(big["w_uq"])),
            "mla_w_ukv": _uncols(big["w_ukv"]), "mla_w_out": _unrows(big["w_out"])}


def _small_weights(tiny, rep):
    prm = jnp.zeros((2, 8, LANES), F32)
    prm = prm.at[:, 0, :GDN_HEADS].set(rep["gdn_a_log"]).at[:, 1, :GDN_HEADS].set(rep["gdn_dt_bias"])
    out = {
        "gdn_conv_wt": jnp.transpose(_gather_rows(tiny["gdn_conv_w"]), (0, 2, 1)),
        "mla_q_norm_g": jnp.transpose(tiny["mla_q_norm_g"], (1, 0, 2)).reshape(2, MLA_Q_RANK),
        "mla_kv_norm_g": jnp.transpose(tiny["mla_kv_norm_g"], (1, 0, 2)).reshape(2, MLA_KV_RANK),
        "gdn_gate_prm": prm,
    }
    for k in ("norm_mix_g", "norm_ffn_g", "gdn_norm_g", "final_norm_g"):
        out[k] = rep[k]
    return out


def _rope_tables(positions):
    inv_freq = ROPE_THETA ** (-jnp.arange(0, MLA_ROPE, 2, dtype=F32) / MLA_ROPE)
    ang = positions.astype(F32)[:, None] * inv_freq
    cos, sin = jnp.cos(ang), jnp.sin(ang)
    reps = LANES // MLA_ROPE
    return jnp.tile(jnp.concatenate([cos, cos], axis=1), (1, reps)), jnp.tile(
        jnp.concatenate([-sin, sin], axis=1), (1, reps))


def kernel(x, c, positions, ada_w, ada_b, norm_mix_g, norm_ffn_g, gdn_w_in, gdn_conv_w, gdn_a_log, gdn_dt_bias, gdn_norm_g, gdn_w_out, mla_w_in, mla_q_norm_g, mla_kv_norm_g, mla_w_uq, mla_w_ukv, mla_w_out, ffn_w_gate, ffn_w_up, ffn_w_down, final_norm_g, loss_target, m_ada_w, m_ada_b, m_norm_mix_g, m_norm_ffn_g, m_gdn_w_in, m_gdn_conv_w, m_gdn_a_log, m_gdn_dt_bias, m_gdn_norm_g, m_gdn_w_out, m_mla_w_in, m_mla_q_norm_g, m_mla_kv_norm_g, m_mla_w_uq, m_mla_w_ukv, m_mla_w_out, m_ffn_w_gate, m_ffn_w_up, m_ffn_w_down, m_final_norm_g, v_ada_w, v_ada_b, v_norm_mix_g, v_norm_ffn_g, v_gdn_w_in, v_gdn_conv_w, v_gdn_a_log, v_gdn_dt_bias, v_gdn_norm_g, v_gdn_w_out, v_mla_w_in, v_mla_q_norm_g, v_mla_kv_norm_g, v_mla_w_uq, v_mla_w_ukv, v_mla_w_out, v_ffn_w_gate, v_ffn_w_up, v_ffn_w_down, v_final_norm_g):
    W = dict(ada_w=ada_w, ada_b=ada_b, norm_mix_g=norm_mix_g, norm_ffn_g=norm_ffn_g, gdn_w_in=gdn_w_in,
             gdn_conv_w=gdn_conv_w, gdn_a_log=gdn_a_log, gdn_dt_bias=gdn_dt_bias, gdn_norm_g=gdn_norm_g,
             gdn_w_out=gdn_w_out, mla_w_in=mla_w_in, mla_q_norm_g=mla_q_norm_g, mla_kv_norm_g=mla_kv_norm_g,
             mla_w_uq=mla_w_uq, mla_w_ukv=mla_w_ukv, mla_w_out=mla_w_out, ffn_w_gate=ffn_w_gate,
             ffn_w_up=ffn_w_up, ffn_w_down=ffn_w_down, final_norm_g=final_norm_g)
    M = dict(ada_w=m_ada_w, ada_b=m_ada_b, norm_mix_g=m_norm_mix_g, norm_ffn_g=m_norm_ffn_g, gdn_w_in=m_gdn_w_in,
             gdn_conv_w=m_gdn_conv_w, gdn_a_log=m_gdn_a_log, gdn_dt_bias=m_gdn_dt_bias, gdn_norm_g=m_gdn_norm_g,
             gdn_w_out=m_gdn_w_out, mla_w_in=m_mla_w_in, mla_q_norm_g=m_mla_q_norm_g,
             mla_kv_norm_g=m_mla_kv_norm_g, mla_w_uq=m_mla_w_uq, mla_w_ukv=m_mla_w_ukv, mla_w_out=m_mla_w_out,
             ffn_w_gate=m_ffn_w_gate, ffn_w_up=m_ffn_w_up, ffn_w_down=m_ffn_w_down, final_norm_g=m_final_norm_g)
    V = dict(ada_w=v_ada_w, ada_b=v_ada_b, norm_mix_g=v_norm_mix_g, norm_ffn_g=v_norm_ffn_g, gdn_w_in=v_gdn_w_in,
             gdn_conv_w=v_gdn_conv_w, gdn_a_log=v_gdn_a_log, gdn_dt_bias=v_gdn_dt_bias, gdn_norm_g=v_gdn_norm_g,
             gdn_w_out=v_gdn_w_out, mla_w_in=v_mla_w_in, mla_q_norm_g=v_mla_q_norm_g,
             mla_kv_norm_g=v_mla_kv_norm_g, mla_w_uq=v_mla_w_uq, mla_w_ukv=v_mla_w_ukv, mla_w_out=v_mla_w_out,
             ffn_w_gate=v_ffn_w_gate, ffn_w_up=v_ffn_w_up, ffn_w_down=v_ffn_w_down, final_norm_g=v_final_norm_g)
    me = 4 * lax.axis_index("x") + 2 * lax.axis_index("y") + lax.axis_index("c")
    t = x.shape[1]
    wc = ada_w.shape[-1]

    tiny_shapes = [c.shape, gdn_conv_w.shape, mla_q_norm_g.shape, mla_kv_norm_g.shape]
    (tiny_g,) = _exchange([_pack([c, gdn_conv_w, mla_q_norm_g, mla_kv_norm_g])], scatter=False, name="gather_tiny")
    c_g, conv_g, qn_g, kvn_g = _unpack(tiny_g, tiny_shapes, lead=(N_DEV,))
    c_all = c_g.reshape(N_DEV, D_MODEL)
    rep = _small_weights({"gdn_conv_w": conv_g, "mla_q_norm_g": qn_g, "mla_kv_norm_g": kvn_g}, W)

    groups = [(layer, kind) for layer in range(DEPTH) for kind in ("mix", "ffn")]

    def start_group(i, dep):
        layer, kind = groups[i]
        srcs = [W[k][_layer_index(k, layer)].astype(BF16) for k in _group_names(layer, kind)]
        return _exchange_start(srcs, scatter=False, name=f"gather_start_{kind}_l{layer}", dep=dep)

    gather = {0: start_group(0, tiny_g)}

    b_cols = lax.dynamic_slice_in_dim(ada_b, me * wc, wc, axis=1).reshape(DEPTH, 1, wc)
    mod_part = _ada_mod(c_all, ada_w, b_cols, name="ada_mod")
    (mod_g,) = _exchange([mod_part], scatter=False, name="gather_mod")
    mod_mine = lax.dynamic_index_in_dim(mod_g, me, axis=2, keepdims=False)
    mod = jnp.transpose(mod_mine, (1, 0, 2)).reshape(DEPTH, N_MOD, D_MODEL)

    def get_weights(layer, kind, after):
        i = groups.index((layer, kind))
        srcs, lands = _exchange_wait(gather[i], mod if i == 0 else after, scatter=False,
                                     name=f"gather_wait_{kind}_l{layer}")
        zero = jnp.zeros((), BF16)
        if i + 1 < len(groups):
            gather[i + 1] = start_group(i + 1, lands[0])
            zero = gather[i + 1][4][0, 0].astype(BF16)
        got = {k: lax.dynamic_update_index_in_dim(z, s, me, 0)
               for k, s, z in zip(_group_names(layer, kind), srcs, lands)}
        return _group_weights(layer, kind, got, zero)

    scatter = []

    def put_grads(layer, kind, big):
        slots = _layer_grad_slots(kind, big)
        started = _exchange_start(list(slots.values()), scatter=True, name=f"scatter_start_{kind}_l{layer}")
        scatter.append((layer, kind, list(slots.keys()), started))
        return started[4]

    cos_t, sin_t = _rope_tables(positions[0])
    loss, dx, dmod, g = _local_step(x[0], loss_target[0], mod, cos_t, sin_t, rep, get_weights, put_grads)

    parts = {k: [None] * W[k].shape[0] for k in BIG}
    res = {}

    def wait_group(entry, after):
        layer, kind, names, started = entry
        srcs, lands = _exchange_wait(started, after, scatter=True, name=f"scatter_wait_{kind}_l{layer}")
        for k, s, z in zip(names, srcs, lands):
            own = lax.dynamic_index_in_dim(s, me, 0, keepdims=False)
            parts[k][_layer_index(k, layer)] = lax.dynamic_update_index_in_dim(z, own, me, 0)

    for entry in scatter[:-1]:
        wait_group(entry, dx)
    early = [k for k in BIG if k not in scatter[-1][2]]
    for k in early:
        res[k] = _adamw(parts[k], W[k], M[k], V[k], name=f"adamw_{k}")

    small_local = [dmod.reshape(DEPTH, N_MOD * D_MODEL), g["norm_mix_g"], g["norm_ffn_g"],
                   jnp.transpose(g["gdn_conv_wt"], (0, 2, 1)), g["gdn_a_log"], g["gdn_dt_bias"], g["gdn_norm_g"],
                   g["mla_q_norm_g"], g["mla_kv_norm_g"], g["final_norm_g"], loss.reshape(1)]
    small_shapes = [a.shape for a in small_local]
    (small_g,) = _exchange([_pack(small_local)], scatter=False, name="gather_small_grads")
    small_sum = _unpack(_sum_parts(small_g, name="sum_small_grads"), small_shapes)
    loss = small_sum[-1][0]
    dmod_all = _unpack(small_g, small_shapes[:1], lead=(N_DEV,))[0]
    sg = dict(zip(SMALL, small_sum))
    wait_group(scatter[-1], small_g)
    sg["gdn_conv_w"] = lax.dynamic_slice_in_dim(sg["gdn_conv_w"], me * gdn_conv_w.shape[1], gdn_conv_w.shape[1], 1)
    sg["mla_q_norm_g"] = lax.dynamic_slice_in_dim(sg["mla_q_norm_g"], me * mla_q_norm_g.shape[1],
                                                  mla_q_norm_g.shape[1], 1)
    sg["mla_kv_norm_g"] = lax.dynamic_slice_in_dim(sg["mla_kv_norm_g"], me * mla_kv_norm_g.shape[1],
                                                   mla_kv_norm_g.shape[1], 1)

    dmod_cols = jnp.transpose(lax.dynamic_slice_in_dim(dmod_all, me * wc, wc, axis=2), (1, 0, 2))
    res["ada_w"] = _ada_grad_adamw(c_all, dmod_cols, ada_w, m_ada_w, v_ada_w, name="ada_w_grad_adamw")
    for k in BIG:
        if k not in early:
            res[k] = _adamw(parts[k], W[k], M[k], V[k], name=f"adamw_{k}")
    shapes = [W[k].shape for k in SMALL]
    packed = [_pack([d[k] for k in SMALL]) for d in (sg, W, M, V)]
    outs = _adamw([packed[0][None]], packed[1][None], packed[2][None], packed[3][None], name="adamw_small")
    unpacked = [_unpack(o[0], shapes) for o in outs]
    for i, k in enumerate(SMALL):
        res[k] = tuple(u[i] for u in unpacked)

    return (loss, dx[None], *[res[k][0] for k in WEIGHTS], *[res[k][1] for k in WEIGHTS],
            *[res[k][2] for k in WEIGHTS], *[res[k][3] for k in WEIGHTS])
```

```python
import functools
import math

import jax
import jax.numpy as jnp
from jax import lax
from jax.experimental import pallas as pl
from jax.experimental.pallas import tpu as pltpu

F32 = jnp.float32
BF16 = jnp.bfloat16
MXU_DTYPE = jnp.bfloat16

N_DEV = 8
D_MODEL = 1024
DEPTH = 4
GDN_HEADS = 8
GDN_HEAD_DIM = 128
GDN_KEY_DIM = GDN_HEADS * GDN_HEAD_DIM
GDN_CHUNK = 64
GDN_HEAD_BATCH = 8
GDN_CONV = 4
GDN_MAIN = 4 * GDN_KEY_DIM
MLA_HEADS = 8
MLA_NOPE = 128
MLA_ROPE = 64
MLA_V = 128
MLA_Q_RANK = 384
MLA_KV_RANK = 256
MLA_IN = MLA_Q_RANK + MLA_KV_RANK + MLA_ROPE
MLA_QK = MLA_NOPE + MLA_ROPE
ROPE_THETA = 10000.0
D_FF = 2816
N_MOD = 6
EPS = 1e-6
LANES = 128
VMEM_LIMIT = 48 * 1024 * 1024

ADAM_LR = 0.001
ADAM_B1 = 0.9
ADAM_B2 = 0.999
ADAM_EPS = 1e-08
ADAM_WD = 0.01
ADAM_STEP = 10
ADAM_BC1 = 1.0 - ADAM_B1 ** ADAM_STEP
ADAM_BC2 = 1.0 - ADAM_B2 ** ADAM_STEP

NN = (((1,), (0,)), ((), ()))
NT = (((1,), (1,)), ((), ()))
TN = (((0,), (0,)), ((), ()))
NEG = -1e30


def _dotb(a, b, dims):
    return lax.dot_general(a.astype(MXU_DTYPE), b.astype(MXU_DTYPE), dims, preferred_element_type=F32)


def _split(a):
    hi = a.astype(BF16)
    return hi, (a - hi.astype(F32)).astype(BF16)


def _dotf(a, b, dims):
    ah, al = _split(a)
    bh, bl = _split(b)
    dot = lambda u, v: lax.dot_general(u, v, dims, preferred_element_type=F32)
    return dot(ah, bh) + (dot(ah, bl) + dot(al, bh))


def _params(*sem):
    return pltpu.CompilerParams(dimension_semantics=sem, vmem_limit_bytes=VMEM_LIMIT)


def _pick(n, pref, mult=LANES):
    best = None
    t = mult
    while t <= min(n, pref):
        if n % t == 0:
            best = t
        t += mult
    return best if best is not None else n


def _sigmoid(z):
    return 1.0 / (1.0 + jnp.exp(-z))


def _exchange(arrays, *, scatter, name):
    n = len(arrays)
    out_shape = tuple(
        jax.ShapeDtypeStruct(a.shape if scatter else (N_DEV,) + a.shape, a.dtype) for a in arrays)

    def body(*refs):
        ins, outs = refs[:n], refs[n:2 * n]
        send_sems, recv_sems, local_sems = refs[2 * n:]
        x, y, c = lax.axis_index("x"), lax.axis_index("y"), lax.axis_index("c")
        me = 4 * x + 2 * y + c
        copies = []
        for k in range(n):
            src_own = ins[k].at[me] if scatter else ins[k]
            own = pltpu.make_async_copy(src_own, outs[k].at[me], local_sems.at[k])
            own.start()
            copies.append(own)
        sends = []
        for p in range(1, N_DEV):
            px, py, pc = x ^ ((p >> 2) & 1), y ^ ((p >> 1) & 1), c ^ (p & 1)
            peer = 4 * px + 2 * py + pc
            for k in range(n):
                cp = pltpu.make_async_remote_copy(
                    src_ref=ins[k].at[peer] if scatter else ins[k],
                    dst_ref=outs[k].at[me],
                    send_sem=send_sems.at[k, p - 1],
                    recv_sem=recv_sems.at[k, p - 1],
                    device_id=(px, py, pc),
                    device_id_type=pl.DeviceIdType.MESH,
                )
                cp.start()
                sends.append((cp, k, peer, p))
        for cp, k, peer, p in sends:
            pltpu.make_async_remote_copy(
                src_ref=ins[k].at[peer] if scatter else ins[k],
                dst_ref=outs[k].at[peer],
                send_sem=send_sems.at[k, p - 1],
                recv_sem=recv_sems.at[k, p - 1],
                device_id=(x, y, c),
                device_id_type=pl.DeviceIdType.MESH,
            ).wait_recv()
        for cp, _, _, _ in sends:
            cp.wait_send()
        for own in copies:
            own.wait()

    any_spec = pl.BlockSpec(memory_space=pl.ANY)
    outs = pl.pallas_call(
        body,
        name=name,
        out_shape=out_shape,
        in_specs=[any_spec] * n,
        out_specs=tuple([any_spec] * n),
        scratch_shapes=[
            pltpu.SemaphoreType.DMA((n, N_DEV - 1)),
            pltpu.SemaphoreType.DMA((n, N_DEV - 1)),
            pltpu.SemaphoreType.DMA((n,)),
        ],
        compiler_params=pltpu.CompilerParams(has_side_effects=True),
    )(*arrays)
    return list(outs)


def _peer(x, y, c, p):
    return x ^ ((p >> 2) & 1), y ^ ((p >> 1) & 1), c ^ (p & 1)


def _exchange_start(arrays, *, scatter, name, dep=None):
    n = len(arrays)
    deps = [] if dep is None else [dep]
    lands = [lax.empty(a.shape if scatter else (N_DEV,) + a.shape, a.dtype) for a in arrays]

    def body(*refs):
        ins, zones = refs[:n], refs[n:2 * n]
        send_sems, recv_sems = refs[2 * n + len(deps)], refs[2 * n + len(deps) + 1]
        token = refs[-1]
        x, y, c = lax.axis_index("x"), lax.axis_index("y"), lax.axis_index("c")
        me = 4 * x + 2 * y + c
        for p in range(1, N_DEV):
            px, py, pc = _peer(x, y, c, p)
            for k in range(n):
                pltpu.make_async_remote_copy(
                    src_ref=ins[k].at[4 * px + 2 * py + pc] if scatter else ins[k],
                    dst_ref=zones[k].at[me],
                    send_sem=send_sems.at[k * (N_DEV - 1) + p - 1],
                    recv_sem=recv_sems.at[k * (N_DEV - 1) + p - 1],
                    device_id=(px, py, pc),
                    device_id_type=pl.DeviceIdType.MESH,
                ).start()
        token[...] = jnp.zeros_like(token)

    hbm = pl.BlockSpec(memory_space=pltpu.HBM)
    sem = pl.BlockSpec(memory_space=pltpu.SEMAPHORE)
    outs = pl.pallas_call(
        body,
        name=name,
        out_shape=(pltpu.SemaphoreType.DMA((n * (N_DEV - 1),)), pltpu.SemaphoreType.DMA((n * (N_DEV - 1),)),
                   *[pltpu.HBM(a.shape, a.dtype) for a in arrays], *[pltpu.HBM(z.shape, z.dtype) for z in lands],
                   jax.ShapeDtypeStruct((8, LANES), F32)),
        in_specs=[hbm] * (2 * n) + [pl.BlockSpec(memory_space=pl.ANY)] * len(deps),
        out_specs=(sem, sem, *[hbm] * (2 * n), pl.BlockSpec(memory_space=pltpu.VMEM)),
        input_output_aliases={k: 2 + k for k in range(2 * n)},
        compiler_params=pltpu.CompilerParams(has_side_effects=pltpu.SideEffectType.DATAFLOW_SIDE_EFFECTING),
    )(*[pltpu.with_memory_space_constraint(a, pltpu.HBM) for a in arrays],
      *[pltpu.with_memory_space_constraint(z, pltpu.HBM) for z in lands], *deps)
    return outs[0], outs[1], list(outs[2:2 + n]), list(outs[2 + n:2 + 2 * n]), outs[-1]


def _exchange_wait(started, after, *, scatter, name):
    send_sems, recv_sems, srcs, lands, _ = started
    n = len(srcs)

    def body(*refs):
        ins, zones = refs[:n], refs[n:2 * n]
        s_sems, r_sems = refs[2 * n], refs[2 * n + 1]
        x, y, c = lax.axis_index("x"), lax.axis_index("y"), lax.axis_index("c")
        for p in range(1, N_DEV):
            px, py, pc = _peer(x, y, c, p)
            peer = 4 * px + 2 * py + pc
            for k in range(n):
                cp = pltpu.make_async_remote_copy(
                    src_ref=ins[k].at[peer] if scatter else ins[k],
                    dst_ref=zones[k].at[peer],
                    send_sem=s_sems.at[k * (N_DEV - 1) + p - 1],
                    recv_sem=r_sems.at[k * (N_DEV - 1) + p - 1],
                    device_id=(px, py, pc),
                    device_id_type=pl.DeviceIdType.MESH,
                )
                cp.wait_send()
                cp.wait_recv()

    hbm = pl.BlockSpec(memory_space=pltpu.HBM)
    sem = pl.BlockSpec(memory_space=pltpu.SEMAPHORE)
    outs = pl.pallas_call(
        body,
        name=name,
        out_shape=tuple(pltpu.HBM(a.shape, a.dtype) for a in srcs + lands),
        in_specs=[hbm] * (2 * n) + [sem, sem, pl.BlockSpec(memory_space=pl.ANY)],
        out_specs=tuple([hbm] * (2 * n)),
        input_output_aliases={k: k for k in range(2 * n)},
        compiler_params=pltpu.CompilerParams(has_side_effects=pltpu.SideEffectType.DATAFLOW_SIDE_EFFECTING),
    )(*srcs, *lands, send_sems, recv_sems, after)
    return list(outs[:n]), list(outs[n:])


def _mm(a, b, *, mode, out_dtype, name, add=None, tm=512, tn=512):
    if mode == "nn":
        (m, kd), (_, nd) = a.shape, b.shape
    elif mode == "nt":
        (m, kd), (nd, _) = a.shape, b.shape
    else:
        (kd, m), (_, nd) = a.shape, b.shape
    tm = _pick(m, tm, LANES if mode == "tn" else 16)
    tn = _pick(nd, tn)
    dims = {"nn": NN, "nt": NT, "tn": TN}[mode]
    ni, nj = m // tm, nd // tn
    a_bytes, b_bytes = a.size * a.dtype.itemsize, b.size * b.dtype.itemsize
    i_outer = a_bytes + ni * b_bytes <= b_bytes + nj * a_bytes
    ij = (lambda g0, g1: (g0, g1)) if i_outer else (lambda g0, g1: (g1, g0))
    a_spec = (pl.BlockSpec((kd, tm), lambda g0, g1: (0, ij(g0, g1)[0])) if mode == "tn"
              else pl.BlockSpec((tm, kd), lambda g0, g1: (ij(g0, g1)[0], 0)))
    b_spec = (pl.BlockSpec((tn, kd), lambda g0, g1: (ij(g0, g1)[1], 0)) if mode == "nt"
              else pl.BlockSpec((kd, tn), lambda g0, g1: (0, ij(g0, g1)[1])))
    o_spec = pl.BlockSpec((tm, tn), lambda g0, g1: ij(g0, g1))
    has_add = add is not None

    def body(*refs):
        a_ref, b_ref = refs[0], refs[1]
        o_ref = refs[-1]
        acc = _dotb(a_ref[...], b_ref[...], dims)
        if has_add:
            acc = acc + refs[2][...].astype(F32)
        o_ref[...] = acc.astype(o_ref.dtype)

    ins = [a, b] + ([add] if has_add else [])
    specs = [a_spec, b_spec] + ([o_spec] if has_add else [])
    return pl.pallas_call(
        body, name=name, grid=(ni, nj) if i_outer else (nj, ni), in_specs=specs, out_specs=o_spec,
        out_shape=jax.ShapeDtypeStruct((m, nd), out_dtype),
        compiler_params=_params("parallel", "parallel"),
    )(*ins)


def _mm_resid(a, b, x, gate, *, name, tm=256, tn=1024):
    m, kd = a.shape
    nd = b.shape[1]
    tm = _pick(m, tm, 16)
    tn = _pick(nd, tn)
    o_spec = pl.BlockSpec((tm, tn), lambda i, j: (i, j))

    def body(a_ref, b_ref, x_ref, g_ref, xo_ref, y_ref):
        y = _dotb(a_ref[...], b_ref[...], NN)
        y_ref[...] = y
        xo_ref[...] = x_ref[...] + g_ref[...] * y

    return pl.pallas_call(
        body, name=name, grid=(m // tm, nd // tn),
        in_specs=[pl.BlockSpec((tm, kd), lambda i, j: (i, 0)), pl.BlockSpec((kd, tn), lambda i, j: (0, j)),
                  o_spec, pl.BlockSpec((1, tn), lambda i, j: (0, j))],
        out_specs=(o_spec, o_spec),
        out_shape=(jax.ShapeDtypeStruct((m, nd), F32), jax.ShapeDtypeStruct((m, nd), F32)),
        compiler_params=_params("parallel", "parallel"),
    )(a, b, x, gate)


ROWS = 256


def _row_spec(width, rows=ROWS):
    return pl.BlockSpec((rows, width), lambda i: (i, 0))


def _const_spec(shape):
    return pl.BlockSpec(shape, lambda i: tuple(0 for _ in shape))


def _adaln_fwd(x, g, scale, shift, *, name):
    t, d = x.shape

    def body(x_ref, g_ref, sc_ref, sh_ref, h_ref):
        xv = x_ref[...]
        r = lax.rsqrt(jnp.mean(xv * xv, axis=-1, keepdims=True) + EPS)
        h_ref[...] = (xv * r * g_ref[...] * (1.0 + sc_ref[...]) + sh_ref[...]).astype(h_ref.dtype)

    return pl.pallas_call(
        body, name=name, grid=(t // ROWS,),
        in_specs=[_row_spec(d), _const_spec((1, d)), _const_spec((1, d)), _const_spec((1, d))],
        out_specs=_row_spec(d), out_shape=jax.ShapeDtypeStruct((t, d), BF16),
        compiler_params=_params("parallel"),
    )(x, g, scale, shift)


def _adaln_bwd(x, g, scale, shift, dh, dres, *, name):
    t, d = x.shape

    def body(x_ref, g_ref, sc_ref, sh_ref, dh_ref, dr_ref, dx_ref, st_ref):
        @pl.when(pl.program_id(0) == 0)
        def _():
            st_ref[...] = jnp.zeros_like(st_ref)

        xv = x_ref[...]
        dhv = dh_ref[...].astype(F32)
        gv = g_ref[...]
        r = lax.rsqrt(jnp.mean(xv * xv, axis=-1, keepdims=True) + EPS)
        xh = xv * r
        nv = xh * gv
        dn = dhv * (1.0 + sc_ref[...])
        dxh = dn * gv
        dx_ref[...] = dr_ref[...] + r * (dxh - xh * jnp.mean(dxh * xh, axis=-1, keepdims=True))
        st_ref[0:1, :] += jnp.sum(dn * xh, axis=0, keepdims=True)
        st_ref[1:2, :] += jnp.sum(dhv * nv, axis=0, keepdims=True)
        st_ref[2:3, :] += jnp.sum(dhv, axis=0, keepdims=True)

    return pl.pallas_call(
        body, name=name, grid=(t // ROWS,),
        in_specs=[_row_spec(d), _const_spec((1, d)), _const_spec((1, d)), _const_spec((1, d)),
                  _row_spec(d), _row_spec(d)],
        out_specs=(_row_spec(d), _const_spec((8, d))),
        out_shape=(jax.ShapeDtypeStruct((t, d), F32), jax.ShapeDtypeStruct((8, d), F32)),
        compiler_params=_params("arbitrary"),
    )(x, g, scale, shift, dh, dres)


def _gate_bwd(dxo, y, gate, dep, *, name):
    t, d = dxo.shape

    def body(dx_ref, y_ref, g_ref, dep_ref, dy_ref, st_ref):
        @pl.when(pl.program_id(0) == 0)
        def _():
            st_ref[...] = jnp.zeros_like(st_ref)

        dxv = dx_ref[...]
        dy_ref[...] = (dxv * g_ref[...]).astype(dy_ref.dtype)
        st_ref[0:1, :] += jnp.sum(dxv * y_ref[...], axis=0, keepdims=True)

    return pl.pallas_call(
        body, name=name, grid=(t // ROWS,),
        in_specs=[_row_spec(d), _row_spec(d), _const_spec((1, d)), _const_spec((8, LANES))],
        out_specs=(_row_spec(d), _const_spec((8, d))),
        out_shape=(jax.ShapeDtypeStruct((t, d), BF16), jax.ShapeDtypeStruct((8, d), F32)),
        compiler_params=_params("arbitrary"),
    )(dxo, y, gate, dep)


def _loss_head(x, g, target, *, name):
    t, d = x.shape

    def body(x_ref, g_ref, t_ref, dx_ref, st_ref, ls_ref):
        @pl.when(pl.program_id(0) == 0)
        def _():
            st_ref[...] = jnp.zeros_like(st_ref)
            ls_ref[...] = jnp.zeros_like(ls_ref)

        xv = x_ref[...]
        gv = g_ref[...]
        r = lax.rsqrt(jnp.mean(xv * xv, axis=-1, keepdims=True) + EPS)
        xh = xv * r
        err = xh * gv - t_ref[...]
        ls_ref[...] += 0.5 * jnp.sum(jnp.mean(err * err, axis=-1, keepdims=True))
        dy = err * (1.0 / d)
        dxh = dy * gv
        dx_ref[...] = r * (dxh - xh * jnp.mean(dxh * xh, axis=-1, keepdims=True))
        st_ref[0:1, :] += jnp.sum(dy * xh, axis=0, keepdims=True)

    return pl.pallas_call(
        body, name=name, grid=(t // ROWS,),
        in_specs=[_row_spec(d), _const_spec((1, d)), _row_spec(d)],
        out_specs=(_row_spec(d), _const_spec((8, d)), _const_spec((8, LANES))),
        out_shape=(jax.ShapeDtypeStruct((t, d), F32), jax.ShapeDtypeStruct((8, d), F32),
                   jax.ShapeDtypeStruct((8, LANES), F32)),
        compiler_params=_params("arbitrary"),
    )(x, g, target)


FFN_BLOCK = D_FF // 2


def _gu_to_kernel_layout(wg, wu):
    parts = []
    for b in range(D_FF // FFN_BLOCK):
        sl = slice(b * FFN_BLOCK, (b + 1) * FFN_BLOCK)
        parts += [wg[..., sl], wu[..., sl]]
    return jnp.concatenate(parts, axis=-1)


def _gu_from_kernel_layout(w):
    nb = D_FF // FFN_BLOCK
    wg = jnp.concatenate([w[..., 2 * b * FFN_BLOCK:(2 * b + 1) * FFN_BLOCK] for b in range(nb)], axis=-1)
    wu = jnp.concatenate([w[..., (2 * b + 1) * FFN_BLOCK:(2 * b + 2) * FFN_BLOCK] for b in range(nb)], axis=-1)
    return wg, wu


def _ffn_gu_fwd(h, w_gu, *, name):
    t, d = h.shape
    tn = FFN_BLOCK

    def body(h_ref, w_ref, s_ref, ab_ref):
        acc = _dotb(h_ref[...], w_ref[...], NN)
        a = acc[:, 0:tn]
        s_ref[...] = (a * _sigmoid(a) * acc[:, tn:2 * tn]).astype(s_ref.dtype)
        ab_ref[...] = acc.astype(ab_ref.dtype)

    return pl.pallas_call(
        body, name=name, grid=(D_FF // tn, t // ROWS),
        in_specs=[pl.BlockSpec((ROWS, d), lambda j, i: (i, 0)), pl.BlockSpec((d, 2 * tn), lambda j, i: (0, j))],
        out_specs=(pl.BlockSpec((ROWS, tn), lambda j, i: (i, j)), pl.BlockSpec((ROWS, 2 * tn), lambda j, i: (i, j))),
        out_shape=(jax.ShapeDtypeStruct((t, D_FF), BF16), jax.ShapeDtypeStruct((t, 2 * D_FF), BF16)),
        compiler_params=_params("parallel", "parallel"),
    )(h, w_gu)


def _ffn_down_dx(dy, w_down, ab, *, name):
    t, d = dy.shape
    tn = FFN_BLOCK

    def body(dy_ref, w_ref, ab_ref, d_ref):
        ds = _dotb(dy_ref[...], w_ref[...], NT)
        a = ab_ref[:, 0:tn].astype(F32)
        b = ab_ref[:, tn:2 * tn].astype(F32)
        sg = _sigmoid(a)
        d_ref[:, 0:tn] = (ds * b * sg * (1.0 + a * (1.0 - sg))).astype(d_ref.dtype)
        d_ref[:, tn:2 * tn] = (ds * a * sg).astype(d_ref.dtype)

    return pl.pallas_call(
        body, name=name, grid=(D_FF // tn, t // ROWS),
        in_specs=[pl.BlockSpec((ROWS, d), lambda j, i: (i, 0)), pl.BlockSpec((tn, d), lambda j, i: (j, 0)),
                  pl.BlockSpec((ROWS, 2 * tn), lambda j, i: (i, j))],
        out_specs=pl.BlockSpec((ROWS, 2 * tn), lambda j, i: (i, j)),
        out_shape=jax.ShapeDtypeStruct((t, 2 * D_FF), BF16),
        compiler_params=_params("parallel", "parallel"),
    )(dy, w_down, ab)


def _shift_rows(v, s, rows):
    if s == 0:
        return v
    return jnp.where(rows >= s, pltpu.roll(v, s, 0), 0.0)


def _unshift_rows(v, s, rows, t):
    if s == 0:
        return v
    return jnp.where(rows < t - s, pltpu.roll(v, t - s, 0), 0.0)


def _conv_silu(x, w, rows):
    z = w[GDN_CONV - 1:GDN_CONV, :] * x
    for j in range(GDN_CONV - 1):
        z = z + w[j:j + 1, :] * _shift_rows(x, GDN_CONV - 1 - j, rows)
    sg = _sigmoid(z)
    return z, sg, z * sg


def _gdn_prep_fwd(proj, conv_wt, *, name):
    t = proj.shape[0]
    nh = GDN_HEADS

    def body(x_ref, w_ref, y_ref):
        j = pl.program_id(0)
        rows = lax.broadcasted_iota(jnp.int32, (t, LANES), 0)
        _, _, s = _conv_silu(x_ref[...], w_ref[...], rows)
        rs = lax.rsqrt(jnp.sum(s * s, axis=-1, keepdims=True) + EPS)
        qscale = jnp.where(j < nh, GDN_HEAD_DIM ** -0.5, 1.0)
        y_ref[...] = jnp.where(j < 2 * nh, s * rs * qscale, s)

    return pl.pallas_call(
        body, name=name, grid=(3 * nh,),
        in_specs=[pl.BlockSpec((t, LANES), lambda j: (0, j)), pl.BlockSpec((GDN_CONV, LANES), lambda j: (0, j))],
        out_specs=pl.BlockSpec((t, LANES), lambda j: (0, j)),
        out_shape=jax.ShapeDtypeStruct((t, 3 * GDN_KEY_DIM), F32),
        compiler_params=_params("parallel"),
    )(proj, conv_wt)


def _gdn_prep_bwd(proj, conv_wt, dy, *, name):
    t = proj.shape[0]
    nh = GDN_HEADS

    def body(x_ref, w_ref, dy_ref, dx_ref, dw_ref):
        j = pl.program_id(0)
        rows = lax.broadcasted_iota(jnp.int32, (t, LANES), 0)
        x = x_ref[...]
        w = w_ref[...]
        z, sg, s = _conv_silu(x, w, rows)
        rs = lax.rsqrt(jnp.sum(s * s, axis=-1, keepdims=True) + EPS)
        qscale = jnp.where(j < nh, GDN_HEAD_DIM ** -0.5, 1.0)
        dyv = dy_ref[...]
        nv = s * rs
        de = dyv * qscale
        ds_qk = rs * (de - nv * jnp.sum(de * nv, axis=-1, keepdims=True))
        ds = jnp.where(j < 2 * nh, ds_qk, dyv)
        dz = ds * sg * (1.0 + z * (1.0 - sg))
        dx = w[GDN_CONV - 1:GDN_CONV, :] * dz
        dw_ref[GDN_CONV - 1:GDN_CONV, :] = jnp.sum(dz * x, axis=0, keepdims=True)
        for k in range(GDN_CONV - 1):
            sh = GDN_CONV - 1 - k
            dx = dx + w[k:k + 1, :] * _unshift_rows(dz, sh, rows, t)
            dw_ref[k:k + 1, :] = jnp.sum(dz * _shift_rows(x, sh, rows), axis=0, keepdims=True)
        dx_ref[...] = dx.astype(dx_ref.dtype)

    return pl.pallas_call(
        body, name=name, grid=(3 * nh,),
        in_specs=[pl.BlockSpec((t, LANES), lambda j: (0, j)), pl.BlockSpec((GDN_CONV, LANES), lambda j: (0, j)),
                  pl.BlockSpec((t, LANES), lambda j: (0, j))],
        out_specs=(pl.BlockSpec((t, LANES), lambda j: (0, j)), pl.BlockSpec((GDN_CONV, LANES), lambda j: (0, j))),
        out_shape=(jax.ShapeDtypeStruct((t, 3 * GDN_KEY_DIM), BF16),
                   jax.ShapeDtypeStruct((GDN_CONV, 3 * GDN_KEY_DIM), F32)),
        compiler_params=_params("parallel"),
    )(proj, conv_wt, dy)


def _softplus(z):
    return jnp.maximum(z, 0.0) + jnp.log(1.0 + jnp.exp(-jnp.abs(z)))


def _gdn_gate_fwd(ab, prm, *, name):
    t = ab.shape[0]

    def body(ab_ref, p_ref, o_ref):
        v = ab_ref[...]
        lane = lax.broadcasted_iota(jnp.int32, v.shape, 1)
        g = -jnp.exp(p_ref[0:1, :]) * _softplus(v + p_ref[1:2, :])
        o_ref[...] = jnp.where(lane < GDN_HEADS, g, jnp.where(lane < 2 * GDN_HEADS, _sigmoid(v), 0.0))

    return pl.pallas_call(
        body, name=name, grid=(t // ROWS,),
        in_specs=[_row_spec(LANES), _const_spec((8, LANES))], out_specs=_row_spec(LANES),
        out_shape=jax.ShapeDtypeStruct((t, LANES), F32), compiler_params=_params("parallel"),
    )(ab, prm)


def _gdn_gate_bwd(ab, prm, dgb, *, name):
    t = ab.shape[0]

    def body(ab_ref, p_ref, d_ref, o_ref, st_ref):
        @pl.when(pl.program_id(0) == 0)
        def _():
            st_ref[...] = jnp.zeros_like(st_ref)

        v = ab_ref[...]
        dv = d_ref[...]
        lane = lax.broadcasted_iota(jnp.int32, v.shape, 1)
        is_a = lane < GDN_HEADS
        is_b = jnp.logical_and(lane >= GDN_HEADS, lane < 2 * GDN_HEADS)
        a_exp = jnp.exp(p_ref[0:1, :])
        zz = v + p_ref[1:2, :]
        g = -a_exp * _softplus(zz)
        da = dv * (-a_exp) * _sigmoid(zz)
        beta = _sigmoid(v)
        db = dv * beta * (1.0 - beta)
        o_ref[...] = jnp.where(is_a, da, jnp.where(is_b, db, 0.0)).astype(o_ref.dtype)
        st_ref[0:1, :] += jnp.sum(jnp.where(is_a, dv * g, 0.0), axis=0, keepdims=True)
        st_ref[1:2, :] += jnp.sum(jnp.where(is_a, da, 0.0), axis=0, keepdims=True)

    return pl.pallas_call(
        body, name=name, grid=(t // ROWS,),
        in_specs=[_row_spec(LANES), _const_spec((8, LANES)), _row_spec(LANES)],
        out_specs=(_row_spec(LANES), _const_spec((8, LANES))),
        out_shape=(jax.ShapeDtypeStruct((t, LANES), BF16), jax.ShapeDtypeStruct((8, LANES), F32)),
        compiler_params=_params("arbitrary"),
    )(ab, prm, dgb)


def _gdn_local(qs, ks, vs, gbs, bbs):
    nh = len(qs)
    cs = qs[0].shape[0]
    hs = range(nh)
    r = lax.broadcasted_iota(jnp.int32, (cs, cs), 0)
    c = lax.broadcasted_iota(jnp.int32, (cs, cs), 1)
    tril, strict, eye = r >= c, r > c, r == c
    ident = jnp.where(eye, 1.0, 0.0)
    g_colb = [gbs[h][:, :cs] for h in hs]
    g_row = [jnp.sum(jnp.where(eye, g_colb[h], 0.0), axis=0, keepdims=True) for h in hs]
    gc_col = [jnp.sum(jnp.where(tril, g_row[h], 0.0), axis=1, keepdims=True) for h in hs]
    gc_row = [jnp.sum(jnp.where(r <= c, g_colb[h], 0.0), axis=0, keepdims=True) for h in hs]
    decay = [jnp.exp(jnp.where(tril, gc_col[h] - gc_row[h], NEG)) for h in hs]
    gamma = [jnp.exp(gc_col[h]) for h in hs]
    gcl = [gc_col[h][cs - 1:cs, :] for h in hs]
    gl = [jnp.exp(gcl[h]) for h in hs]
    kdec = [jnp.exp(gcl[h] - gc_col[h]) for h in hs]
    kb = [ks[h] * bbs[h] for h in hs]
    kk = [_dotb(kb[h], ks[h], NT) for h in hs]
    qk = [_dotb(qs[h], ks[h], NT) for h in hs]
    lmat = [jnp.where(strict, kk[h] * decay[h], 0.0) for h in hs]
    pmat = [jnp.where(tril, qk[h] * decay[h], 0.0) for h in hs]
    xm = [-lmat[h] for h in hs]
    tinv = [ident + xm[h] for h in hs]
    for _ in range(int(math.log2(cs)) - 1):
        xm = [_dotf(xm[h], xm[h], NN) for h in hs]
        tinv = [tinv[h] + _dotf(tinv[h], xm[h], NN) for h in hs]
    vb = [vs[h] * bbs[h] for h in hs]
    kg = [kb[h] * gamma[h] for h in hs]
    u = [_dotf(tinv[h], vb[h], NN) for h in hs]
    w = [_dotf(tinv[h], kg[h], NN) for h in hs]
    return [dict(tril=tril, strict=strict, eye=eye, r=r, c=c, decay=decay[h], gamma=gamma[h], gl=gl[h], kdec=kdec[h],
                 kb=kb[h], lmat=lmat[h], tinv=tinv[h], vb=vb[h], kg=kg[h], u=u[h], w=w[h], pmat=pmat[h],
                 qd=qs[h] * gamma[h], kd=ks[h] * kdec[h]) for h in hs]


def _gdn_chunk_fwd(qkv, gbc, bbc, *, name):
    t = qkv.shape[0]
    nh, cs, hd = GDN_HEADS, GDN_CHUNK, GDN_HEAD_DIM
    nc = t // cs

    hb = GDN_HEAD_BATCH
    ng = nh // hb

    def body(q_ref, k_ref, v_ref, g_ref, b_ref, o_ref, st_ref, s_ref):
        @pl.when(pl.program_id(1) == 0)
        def _():
            s_ref[...] = jnp.zeros_like(s_ref)

        sls = [slice(i * hd, (i + 1) * hd) for i in range(hb)]
        hs = range(hb)
        s = [s_ref[i] for i in hs]
        lo = _gdn_local([q_ref[:, sl] for sl in sls], [k_ref[:, sl] for sl in sls], [v_ref[:, sl] for sl in sls],
                        [g_ref[i] for i in hs], [b_ref[i] for i in hs])
        ws = [_dotb(lo[i]["w"], s[i], NN) for i in hs]
        qs = [_dotb(lo[i]["qd"], s[i], NN) for i in hs]
        vn = [lo[i]["u"] - ws[i] for i in hs]
        pv = [_dotb(lo[i]["pmat"], vn[i], NN) for i in hs]
        kv = [_dotb(lo[i]["kd"], vn[i], TN) for i in hs]
        for i, sl in enumerate(sls):
            st_ref[i, 0] = s[i]
            o_ref[:, sl] = qs[i] + pv[i]
            s_ref[i] = s[i] * lo[i]["gl"] + kv[i]

    gspec = pl.BlockSpec((hb, cs, LANES), lambda h, n: (h, n, 0))
    col = lambda off: pl.BlockSpec((cs, hb * hd), lambda h, n: (n, off + h))
    return pl.pallas_call(
        body, name=name, grid=(ng, nc),
        in_specs=[col(0), col(ng), col(2 * ng), gspec, gspec],
        out_specs=(col(0), pl.BlockSpec((hb, 1, hd, hd), lambda h, n: (h, n, 0, 0))),
        out_shape=(jax.ShapeDtypeStruct((t, nh * hd), F32), jax.ShapeDtypeStruct((nh, nc, hd, hd), F32)),
        scratch_shapes=[pltpu.VMEM((hb, hd, hd), F32)],
        compiler_params=_params("parallel", "arbitrary"),
    )(qkv, qkv, qkv, gbc, bbc)


def _gdn_chunk_bwd(qkv, gbc, bbc, states, do, *, name):
    t = qkv.shape[0]
    nh, cs, hd = GDN_HEADS, GDN_CHUNK, GDN_HEAD_DIM
    nc = t // cs

    hb = GDN_HEAD_BATCH
    ng = nh // hb

    def heads_bwd(q, k, v, gb, bb, s, dsn, dov):
        hs = range(len(q))
        lo = _gdn_local(q, k, v, gb, bb)
        tril, strict, eye, r, c = lo[0]["tril"], lo[0]["strict"], lo[0]["eye"], lo[0]["r"], lo[0]["c"]
        rowi = lax.broadcasted_iota(jnp.int32, (cs, 1), 0)
        get = lambda name: [lo[h][name] for h in hs]
        decay, gamma, gl, kdec = get("decay"), get("gamma"), get("gl"), get("kdec")
        kb, tinv, w, pmat, kd, qd = get("kb"), get("tinv"), get("w"), get("pmat"), get("kd"), get("qd")
        ws = [_dotb(w[h], s[h], NN) for h in hs]
        pdo = [_dotb(pmat[h], dov[h], TN) for h in hs]
        kds = [_dotb(kd[h], dsn[h], NN) for h in hs]
        dqd = [_dotb(dov[h], s[h], NT) for h in hs]
        qdo = [_dotb(qd[h], dov[h], TN) for h in hs]
        vn = [lo[h]["u"] - ws[h] for h in hs]
        dvn = [pdo[h] + kds[h] for h in hs]
        dp = [jnp.where(tril, _dotb(dov[h], vn[h], NT), 0.0) for h in hs]
        dkd = [_dotb(vn[h], dsn[h], NT) for h in hs]
        dw = [-_dotb(dvn[h], s[h], NT) for h in hs]
        wdv = [_dotb(w[h], dvn[h], TN) for h in hs]
        dvb = [_dotf(tinv[h], dvn[h], TN) for h in hs]
        dt1 = [_dotf(dvn[h], lo[h]["vb"], NT) for h in hs]
        dkg = [_dotf(tinv[h], dw[h], TN) for h in hs]
        dt2 = [_dotf(dw[h], lo[h]["kg"], NT) for h in hs]
        tdt = [_dotf(tinv[h], dt1[h] + dt2[h], TN) for h in hs]
        dl = [jnp.where(strict, -_dotf(tdt[h], tinv[h], NT), 0.0) for h in hs]
        dkk = [dl[h] * decay[h] for h in hs]
        dqk = [dp[h] * decay[h] for h in hs]
        dkb = [_dotb(dkk[h], k[h], NN) + dkg[h] * gamma[h] for h in hs]
        dk1 = [_dotb(dkk[h], kb[h], TN) for h in hs]
        dk2 = [_dotb(dqk[h], q[h], TN) for h in hs]
        dq1 = [_dotb(dqk[h], k[h], NN) for h in hs]
        out = []
        for h in hs:
            dgl = jnp.sum(jnp.sum(dsn[h] * s[h], axis=1, keepdims=True), axis=0, keepdims=True)
            ds_prev = gl[h] * dsn[h] + qdo[h] - wdv[h]
            dk = dk1[h] + dk2[h] + dkd[h] * kdec[h] + dkb[h] * bb[h]
            dq = dq1[h] + dqd[h] * gamma[h]
            dbeta = jnp.sum(dvb[h] * v[h], axis=-1, keepdims=True) + jnp.sum(dkb[h] * k[h], axis=-1, keepdims=True)
            e = dl[h] * lo[h]["lmat"] + dp[h] * pmat[h]
            e_col = jnp.sum(e, axis=0, keepdims=True)
            dgc = jnp.sum(e, axis=1, keepdims=True) - jnp.sum(jnp.where(eye, e_col, 0.0), axis=1, keepdims=True)
            dgamma = (jnp.sum(dqd[h] * q[h], axis=-1, keepdims=True)
                      + jnp.sum(dkg[h] * kb[h], axis=-1, keepdims=True))
            rk = jnp.sum(dkd[h] * k[h], axis=-1, keepdims=True) * kdec[h]
            dgcl = jnp.sum(rk, axis=0, keepdims=True) + dgl * gl[h]
            dgc = dgc + dgamma * gamma[h] - rk + jnp.where(rowi == cs - 1, dgcl, 0.0)
            dgc_row = jnp.sum(jnp.where(eye, dgc, 0.0), axis=0, keepdims=True)
            dg = jnp.sum(jnp.where(c >= r, dgc_row, 0.0), axis=1, keepdims=True)
            out.append((dq, dk, dvb[h] * bb[h], dbeta, dg, ds_prev))
        return out

    def body(q_ref, k_ref, v_ref, g_ref, b_ref, st_ref, do_ref, dq_ref, dk_ref, dv_ref, dg_ref, db_ref, ds_ref):
        @pl.when(pl.program_id(1) == 0)
        def _():
            ds_ref[...] = jnp.zeros_like(ds_ref)

        sls = [slice(i * hd, (i + 1) * hd) for i in range(hb)]
        hs = range(hb)
        outs = heads_bwd([q_ref[:, sl] for sl in sls], [k_ref[:, sl] for sl in sls], [v_ref[:, sl] for sl in sls],
                         [g_ref[i] for i in hs], [b_ref[i] for i in hs], [st_ref[i, 0] for i in hs],
                         [ds_ref[i] for i in hs], [do_ref[:, sl] for sl in sls])
        for i, sl in enumerate(sls):
            dq, dk, dv, dbeta, dg, ds_prev = outs[i]
            dq_ref[:, sl], dk_ref[:, sl], dv_ref[:, sl] = dq, dk, dv
            db_ref[i] = jnp.broadcast_to(dbeta, (cs, LANES))
            dg_ref[i] = jnp.broadcast_to(dg, (cs, LANES))
            ds_ref[i] = ds_prev

    gspec = pl.BlockSpec((hb, cs, LANES), lambda h, n: (h, nc - 1 - n, 0))
    col = lambda off: pl.BlockSpec((cs, hb * hd), lambda h, n: (nc - 1 - n, off + h))
    return pl.pallas_call(
        body, name=name, grid=(ng, nc),
        in_specs=[col(0), col(ng), col(2 * ng), gspec, gspec,
                  pl.BlockSpec((hb, 1, hd, hd), lambda h, n: (h, nc - 1 - n, 0, 0)), col(0)],
        out_specs=(col(0), col(0), col(0), gspec, gspec),
        out_shape=(jax.ShapeDtypeStruct((t, nh * hd), F32),) * 3
        + (jax.ShapeDtypeStruct((nh, t, LANES), F32),) * 2,
        scratch_shapes=[pltpu.VMEM((hb, hd, hd), F32)],
        compiler_params=_params("parallel", "arbitrary"),
    )(qkv, qkv, qkv, gbc, bbc, states, do)


def _gdn_onorm_fwd(o, proj, norm_g, *, name):
    t = o.shape[0]
    w = GDN_KEY_DIM
    goff = 3 * GDN_KEY_DIM // w

    def body(o_ref, gp_ref, g_ref, y_ref):
        gv = g_ref[...]
        for h in range(GDN_HEADS):
            sl = slice(h * GDN_HEAD_DIM, (h + 1) * GDN_HEAD_DIM)
            oh = o_ref[:, sl]
            gp = gp_ref[:, sl]
            r = lax.rsqrt(jnp.mean(oh * oh, axis=-1, keepdims=True) + EPS)
            y_ref[:, sl] = (oh * r * gv * gp * _sigmoid(gp)).astype(y_ref.dtype)

    return pl.pallas_call(
        body, name=name, grid=(t // ROWS,),
        in_specs=[_row_spec(w), pl.BlockSpec((ROWS, w), lambda i: (i, goff)), _const_spec((1, GDN_HEAD_DIM))],
        out_specs=_row_spec(w), out_shape=jax.ShapeDtypeStruct((t, w), BF16),
        compiler_params=_params("parallel"),
    )(o, proj, norm_g)


def _gdn_onorm_bwd(o, proj, norm_g, dy, *, name):
    t = o.shape[0]
    w = GDN_KEY_DIM
    goff = 3 * GDN_KEY_DIM // w

    def body(o_ref, gp_ref, g_ref, dy_ref, do_ref, dgp_ref, st_ref):
        @pl.when(pl.program_id(0) == 0)
        def _():
            st_ref[...] = jnp.zeros_like(st_ref)

        gv = g_ref[...]
        acc = jnp.zeros((1, GDN_HEAD_DIM), F32)
        for h in range(GDN_HEADS):
            sl = slice(h * GDN_HEAD_DIM, (h + 1) * GDN_HEAD_DIM)
            oh = o_ref[:, sl]
            gp = gp_ref[:, sl]
            dyv = dy_ref[:, sl].astype(F32)
            r = lax.rsqrt(jnp.mean(oh * oh, axis=-1, keepdims=True) + EPS)
            xh = oh * r
            sg = _sigmoid(gp)
            dn = dyv * gp * sg
            dgp_ref[:, sl] = (dyv * xh * gv * sg * (1.0 + gp * (1.0 - sg))).astype(dgp_ref.dtype)
            acc = acc + jnp.sum(dn * xh, axis=0, keepdims=True)
            dxh = dn * gv
            do_ref[:, sl] = r * (dxh - xh * jnp.mean(dxh * xh, axis=-1, keepdims=True))
        st_ref[0:1, :] += acc

    return pl.pallas_call(
        body, name=name, grid=(t // ROWS,),
        in_specs=[_row_spec(w), pl.BlockSpec((ROWS, w), lambda i: (i, goff)), _const_spec((1, GDN_HEAD_DIM)),
                  _row_spec(w)],
        out_specs=(_row_spec(w), _row_spec(w), _const_spec((8, GDN_HEAD_DIM))),
        out_shape=(jax.ShapeDtypeStruct((t, w), F32), jax.ShapeDtypeStruct((t, w), BF16),
                   jax.ShapeDtypeStruct((8, GDN_HEAD_DIM), F32)),
        compiler_params=_params("arbitrary"),
    )(o, proj, norm_g, dy)


def _mla_prep_fwd(proj, qg, kvg, *, name):
    t = proj.shape[0]
    q1, k1 = MLA_Q_RANK, MLA_Q_RANK + MLA_KV_RANK

    def body(p_ref, qg_ref, kg_ref, cq_ref, ck_ref):
        cq = p_ref[:, 0:q1]
        ck = p_ref[:, q1:k1]
        cq_ref[...] = (cq * lax.rsqrt(jnp.mean(cq * cq, axis=-1, keepdims=True) + EPS) * qg_ref[...]).astype(BF16)
        ck_ref[...] = (ck * lax.rsqrt(jnp.mean(ck * ck, axis=-1, keepdims=True) + EPS) * kg_ref[...]).astype(BF16)

    return pl.pallas_call(
        body, name=name, grid=(t // ROWS,),
        in_specs=[_row_spec(MLA_IN), _const_spec((1, MLA_Q_RANK)), _const_spec((1, MLA_KV_RANK))],
        out_specs=(_row_spec(MLA_Q_RANK), _row_spec(MLA_KV_RANK)),
        out_shape=(jax.ShapeDtypeStruct((t, MLA_Q_RANK), BF16), jax.ShapeDtypeStruct((t, MLA_KV_RANK), BF16)),
        compiler_params=_params("parallel"),
    )(proj, qg, kvg)


def _mla_prep_bwd(proj, qg, kvg, dcq, dck, dkr, *, name):
    t = proj.shape[0]
    q1, k1 = MLA_Q_RANK, MLA_Q_RANK + MLA_KV_RANK

    def body(p_ref, qg_ref, kg_ref, dq_ref, dk_ref, dr_ref, dp_ref, st_ref):
        @pl.when(pl.program_id(0) == 0)
        def _():
            st_ref[...] = jnp.zeros_like(st_ref)

        for lo, hi, g_ref, d_ref in ((0, q1, qg_ref, dq_ref), (q1, k1, kg_ref, dk_ref)):
            xv = p_ref[:, lo:hi]
            dn = d_ref[...]
            r = lax.rsqrt(jnp.mean(xv * xv, axis=-1, keepdims=True) + EPS)
            xh = xv * r
            dxh = dn * g_ref[...]
            dp_ref[:, lo:hi] = (r * (dxh - xh * jnp.mean(dxh * xh, axis=-1, keepdims=True))).astype(dp_ref.dtype)
            st_ref[0:1, lo:hi] += jnp.sum(dn * xh, axis=0, keepdims=True)
        dp_ref[:, k1:MLA_IN] = dr_ref[...].astype(dp_ref.dtype)

    return pl.pallas_call(
        body, name=name, grid=(t // ROWS,),
        in_specs=[_row_spec(MLA_IN), _const_spec((1, MLA_Q_RANK)), _const_spec((1, MLA_KV_RANK)),
                  _row_spec(MLA_Q_RANK), _row_spec(MLA_KV_RANK), _row_spec(MLA_ROPE)],
        out_specs=(_row_spec(MLA_IN), _const_spec((8, MLA_IN))),
        out_shape=(jax.ShapeDtypeStruct((t, MLA_IN), BF16), jax.ShapeDtypeStruct((8, MLA_IN), F32)),
        compiler_params=_params("arbitrary"),
    )(proj, qg, kvg, dcq, dck, dkr)


def _rope(xr, cos_t, sin_t, *, name):
    t, w = xr.shape
    ns = w // LANES

    def body(x_ref, c_ref, s_ref, o_ref):
        cv, sv = c_ref[...], s_ref[...]
        lane = lax.broadcasted_iota(jnp.int32, (ROWS, LANES), 1)
        first = (lane % MLA_ROPE) < (MLA_ROPE // 2)
        for i in range(ns):
            sl = slice(i * LANES, (i + 1) * LANES)
            xv = x_ref[:, sl]
            sw = jnp.where(first, pltpu.roll(xv, LANES - MLA_ROPE // 2, 1), pltpu.roll(xv, MLA_ROPE // 2, 1))
            o_ref[:, sl] = xv * cv + sw * sv

    return pl.pallas_call(
        body, name=name, grid=(t // ROWS,),
        in_specs=[_row_spec(w), _row_spec(LANES), _row_spec(LANES)], out_specs=_row_spec(w),
        out_shape=jax.ShapeDtypeStruct((t, w), F32), compiler_params=_params("parallel"),
    )(xr, cos_t, sin_t)


def _rope_bwd(dr, cos_t, sin_t, *, name):
    t, w = dr.shape
    ns = w // LANES

    def body(d_ref, c_ref, s_ref, o_ref):
        cv, sv = c_ref[...], s_ref[...]
        lane = lax.broadcasted_iota(jnp.int32, (ROWS, LANES), 1)
        first = (lane % MLA_ROPE) < (MLA_ROPE // 2)
        for i in range(ns):
            sl = slice(i * LANES, (i + 1) * LANES)
            dv = d_ref[:, sl]
            ds = dv * sv
            sw = jnp.where(first, pltpu.roll(ds, LANES - MLA_ROPE // 2, 1), pltpu.roll(ds, MLA_ROPE // 2, 1))
            o_ref[:, sl] = dv * cv + sw

    return pl.pallas_call(
        body, name=name, grid=(t // ROWS,),
        in_specs=[_row_spec(w), _row_spec(LANES), _row_spec(LANES)], out_specs=_row_spec(w),
        out_shape=jax.ShapeDtypeStruct((t, w), F32), compiler_params=_params("parallel"),
    )(dr, cos_t, sin_t)


ATT_BLOCK = 256
ATT_HEAD_BATCH = 4
ATT_HEAD_BATCH_BWD = 2
ATT_SCALE = MLA_QK ** -0.5


def _causal_mask(i, j, blk):
    rows = i * blk + lax.broadcasted_iota(jnp.int32, (blk, blk), 0)
    cols = j * blk + lax.broadcasted_iota(jnp.int32, (blk, blk), 1)
    return cols <= rows


def _attn_fwd(q, k, v, *, name):
    nh, t, dk = q.shape
    dv = v.shape[-1]
    blk = min(ATT_BLOCK, t)

    hb = ATT_HEAD_BATCH
    hs = range(hb)

    def body(q_ref, k_ref, v_ref, o_ref, l_ref):
        i = pl.program_id(1)
        qv = [q_ref[h] for h in hs]

        def step(j, carry):
            m, l, acc = carry[:hb], carry[hb:2 * hb], carry[2 * hb:]
            off = pl.multiple_of(j * blk, blk)
            mask = _causal_mask(i, j, blk)
            s = [_dotb(qv[h], k_ref[h, pl.ds(off, blk), :], NT) for h in hs]
            s = [jnp.where(mask, s[h] * ATT_SCALE, NEG) for h in hs]
            m_new = [jnp.maximum(m[h], jnp.max(s[h], axis=-1, keepdims=True)) for h in hs]
            p = [jnp.exp(s[h] - m_new[h]) for h in hs]
            pv = [_dotb(p[h], v_ref[h, pl.ds(off, blk), :], NN) for h in hs]
            alpha = [jnp.exp(m[h] - m_new[h]) for h in hs]
            l = [alpha[h] * l[h] + jnp.sum(p[h], axis=-1, keepdims=True) for h in hs]
            acc = [alpha[h] * acc[h] + pv[h] for h in hs]
            return tuple(m_new) + tuple(l) + tuple(acc)

        init = ((jnp.full((blk, 1), NEG, F32),) * hb + (jnp.zeros((blk, 1), F32),) * hb
                + (jnp.zeros((blk, dv), F32),) * hb)
        out = lax.fori_loop(0, i + 1, step, init)
        for h in hs:
            m, l, acc = out[h], out[hb + h], out[2 * hb + h]
            o_ref[h] = acc / l
            l_ref[h] = jnp.broadcast_to(m + jnp.log(l), (blk, LANES))

    return pl.pallas_call(
        body, name=name, grid=(nh // hb, t // blk),
        in_specs=[pl.BlockSpec((hb, blk, dk), lambda h, i: (h, i, 0)), pl.BlockSpec((hb, t, dk), lambda h, i: (h, 0, 0)),
                  pl.BlockSpec((hb, t, dv), lambda h, i: (h, 0, 0))],
        out_specs=(pl.BlockSpec((hb, blk, dv), lambda h, i: (h, i, 0)),
                   pl.BlockSpec((hb, blk, LANES), lambda h, i: (h, i, 0))),
        out_shape=(jax.ShapeDtypeStruct((nh, t, dv), F32), jax.ShapeDtypeStruct((nh, t, LANES), F32)),
        compiler_params=_params("parallel", "parallel"),
    )(q, k, v)


def _attn_bwd(q, k, v, o, lse, do, *, name):
    nh, t, dk = q.shape
    dv = v.shape[-1]
    blk = min(ATT_BLOCK, t)
    nb = t // blk

    hb = ATT_HEAD_BATCH_BWD
    hs = range(hb)

    def body(q_ref, k_ref, v_ref, o_ref, l_ref, do_ref, dq_ref, dk_ref, dv_ref):
        j = pl.program_id(1)

        @pl.when(j == 0)
        def _():
            dq_ref[...] = jnp.zeros_like(dq_ref)

        kv = [k_ref[h] for h in hs]
        vv = [v_ref[h] for h in hs]

        def step(i, carry):
            dk_acc, dv_acc = carry[:hb], carry[hb:]
            off = pl.multiple_of(i * blk, blk)
            rows = pl.ds(off, blk)
            mask = _causal_mask(i, j, blk)
            qv = [q_ref[h, rows, :] for h in hs]
            dov = [do_ref[h, rows, :] for h in hs]
            s = [_dotb(qv[h], kv[h], NT) for h in hs]
            dp = [_dotb(dov[h], vv[h], NT) for h in hs]
            p = [jnp.exp(jnp.where(mask, s[h] * ATT_SCALE, NEG) - l_ref[h, rows, :][:, 0:1]) for h in hs]
            delta = [jnp.sum(dov[h] * o_ref[h, rows, :], axis=-1, keepdims=True) for h in hs]
            ds = [p[h] * (dp[h] - delta[h]) * ATT_SCALE for h in hs]
            dvn = [_dotb(p[h], dov[h], TN) for h in hs]
            dkn = [_dotb(ds[h], qv[h], TN) for h in hs]
            dqn = [_dotb(ds[h], kv[h], NN) for h in hs]
            for h in hs:
                dq_ref[h, rows, :] += dqn[h]
            return tuple(dk_acc[h] + dkn[h] for h in hs) + tuple(dv_acc[h] + dvn[h] for h in hs)

        out = lax.fori_loop(j, nb, step, (jnp.zeros((blk, dk), F32),) * hb + (jnp.zeros((blk, dv), F32),) * hb)
        for h in hs:
            dk_ref[h] = out[h]
            dv_ref[h] = out[hb + h]

    full = lambda w: pl.BlockSpec((hb, t, w), lambda h, j: (h, 0, 0))
    part = lambda w: pl.BlockSpec((hb, blk, w), lambda h, j: (h, j, 0))
    return pl.pallas_call(
        body, name=name, grid=(nh // hb, nb),
        in_specs=[full(dk), part(dk), part(dv), full(dv), full(LANES), full(dv)],
        out_specs=(full(dk), part(dk), part(dv)),
        out_shape=(jax.ShapeDtypeStruct((nh, t, dk), F32), jax.ShapeDtypeStruct((nh, t, dk), F32),
                   jax.ShapeDtypeStruct((nh, t, dv), F32)),
        compiler_params=_params("parallel", "arbitrary"),
    )(q, k, v, o, lse, do)


def _ada_mod(c_all, ada_w, ada_b_cols, *, name):
    nl, d, wc = ada_w.shape

    def body(c_ref, w_ref, b_ref, o_ref):
        cv = c_ref[...]
        o_ref[0] = _dotb(cv * _sigmoid(cv), w_ref[0], NN) + b_ref[0]

    return pl.pallas_call(
        body, name=name, grid=(nl,),
        in_specs=[_const_spec((N_DEV, d)), pl.BlockSpec((1, d, wc), lambda l: (l, 0, 0)),
                  pl.BlockSpec((1, 1, wc), lambda l: (l, 0, 0))],
        out_specs=pl.BlockSpec((1, N_DEV, wc), lambda l: (l, 0, 0)),
        out_shape=jax.ShapeDtypeStruct((nl, N_DEV, wc), F32), compiler_params=_params("parallel"),
    )(c_all, ada_w, ada_b_cols)


def _adam_math(g, w, m, v):
    m2 = ADAM_B1 * m + (1.0 - ADAM_B1) * g
    v2 = ADAM_B2 * v + (1.0 - ADAM_B2) * (g * g)
    delta = -ADAM_LR * ((m2 / ADAM_BC1) / (jnp.sqrt(v2 / ADAM_BC2) + ADAM_EPS) + ADAM_WD * w)
    return delta, m2, v2


def _ada_grad_adamw(c_all, dmod_cols, w, m, v, *, name):
    nl, d, wc = w.shape
    tr = 256

    def body(c_ref, dm_ref, w_ref, m_ref, v_ref, g_ref, d_ref, m2_ref, v2_ref):
        cv = c_ref[...]
        g = _dotf(cv * _sigmoid(cv), dm_ref[0], TN)
        delta, m2, v2 = _adam_math(g, w_ref[0], m_ref[0], v_ref[0])
        g_ref[0], d_ref[0], m2_ref[0], v2_ref[0] = g, delta, m2, v2

    blk = pl.BlockSpec((1, tr, wc), lambda l, i: (l, i, 0))
    return pl.pallas_call(
        body, name=name, grid=(nl, d // tr),
        in_specs=[pl.BlockSpec((N_DEV, tr), lambda l, i: (0, i)), pl.BlockSpec((1, N_DEV, wc), lambda l, i: (l, 0, 0)),
                  blk, blk, blk],
        out_specs=(blk,) * 4, out_shape=(jax.ShapeDtypeStruct(w.shape, F32),) * 4,
        compiler_params=_params("parallel", "parallel"),
    )(c_all, dmod_cols, w, m, v)


def _adamw(parts, w, m, v, *, name):
    nl, r, c = w.shape
    ns = parts[0].shape[0]
    lanes_padded = -(-c // LANES) * LANES
    row_bytes = 2 * nl * ns * lanes_padded * parts[0].dtype.itemsize
    tr = _pick(r, min(256, max(16, (VMEM_LIMIT // 2) // row_bytes)), 16)

    def body(*refs):
        p_refs = refs[:nl]
        w_ref, m_ref, v_ref, g_ref, d_ref, m2_ref, v2_ref = refs[nl:]
        layer = pl.program_id(0)
        for q in range(nl):
            @pl.when(layer == q)
            def _(q=q):
                g = p_refs[q][0].astype(F32)
                for s in range(1, ns):
                    g = g + p_refs[q][s].astype(F32)
                delta, m2, v2 = _adam_math(g, w_ref[0], m_ref[0], v_ref[0])
                g_ref[0], d_ref[0], m2_ref[0], v2_ref[0] = g, delta, m2, v2

    blk = pl.BlockSpec((1, tr, c), lambda l, i: (l, i, 0))
    p_specs = [pl.BlockSpec((ns, tr, c), lambda l, i, q=q: (0, jnp.where(l == q, i, 0), 0)) for q in range(nl)]
    return pl.pallas_call(
        body, name=name, grid=(nl, r // tr),
        in_specs=p_specs + [blk, blk, blk],
        out_specs=(blk,) * 4, out_shape=(jax.ShapeDtypeStruct(w.shape, F32),) * 4,
        compiler_params=_params("arbitrary", "arbitrary"),
    )(*parts, w, m, v)


def _sum_parts(parts, *, name):
    ns, r, c = parts.shape

    def body(p_ref, o_ref):
        acc = p_ref[0]
        for s in range(1, ns):
            acc = acc + p_ref[s]
        o_ref[...] = acc

    return pl.pallas_call(
        body, name=name, out_shape=jax.ShapeDtypeStruct((r, c), F32),
        in_specs=[pl.BlockSpec(memory_space=pltpu.VMEM)], out_specs=pl.BlockSpec(memory_space=pltpu.VMEM),
    )(parts)


def _pack(arrs):
    flat = jnp.concatenate([a.reshape(-1).astype(F32) for a in arrs])
    pad = (-flat.shape[0]) % (8 * LANES)
    return jnp.pad(flat, (0, pad)).reshape(-1, LANES)


def _unpack(packed, shapes, lead=()):
    flat = packed.reshape(lead + (-1,))
    out, off = [], 0
    for s in shapes:
        n = math.prod(s)
        out.append(flat[..., off:off + n].reshape(lead + tuple(s)))
        off += n
    return out


def _gather_cols(g):
    _, nl, r, cs = g.shape
    return jnp.transpose(g, (1, 2, 0, 3)).reshape(nl, r, N_DEV * cs)


def _gather_rows(g):
    _, nl, rs, c = g.shape
    return jnp.transpose(g, (1, 0, 2, 3)).reshape(nl, N_DEV * rs, c)


def _scatter_cols(full):
    nl, r, c = full.shape
    return jnp.transpose(full.reshape(nl, r, N_DEV, c // N_DEV), (2, 0, 1, 3))


def _scatter_rows(full):
    nl, r, c = full.shape
    return jnp.transpose(full.reshape(nl, N_DEV, r // N_DEV, c), (1, 0, 2, 3))


def _row(v):
    return v.reshape(1, -1)


def _local_step(x, target, mod, cos_t, sin_t, rep, get_weights, put_grads):
    t = x.shape[0]
    saved = []
    for layer in range(DEPTH):
        j = layer // 2
        tag = f"l{layer}"
        shift_m, scale_m, gate_m, shift_f, scale_f, gate_f = [_row(mod[layer, i]) for i in range(N_MOD)]
        lw = dict(get_weights(layer, "mix", x))
        rec = {"x0": x, "lw": lw}
        h = _adaln_fwd(x, _row(rep["norm_mix_g"][layer]), scale_m, shift_m, name=f"adaln_mix_{tag}")
        rec["h"] = h
        if layer % 2 == 0:
            proj = _mm(h, lw["w_main"], mode="nn", out_dtype=F32, tm=256, tn=GDN_MAIN, name=f"gdn_in_{tag}")
            ab = _mm(h, lw["w_ab"], mode="nn", out_dtype=F32, name=f"gdn_in_ab_{tag}")
            qkv = _gdn_prep_fwd(proj, rep["gdn_conv_wt"][j], name=f"gdn_prep_{tag}")
            gbeta = _gdn_gate_fwd(ab, rep["gdn_gate_prm"][j], name=f"gdn_gate_{tag}")
            gbc = jnp.broadcast_to(jnp.transpose(gbeta[:, 0:GDN_HEADS])[:, :, None], (GDN_HEADS, t, LANES))
            bbc = jnp.broadcast_to(jnp.transpose(gbeta[:, GDN_HEADS:2 * GDN_HEADS])[:, :, None],
                                   (GDN_HEADS, t, LANES))
            o, states = _gdn_chunk_fwd(qkv, gbc, bbc, name=f"gdn_chunk_{tag}")
            og = _gdn_onorm_fwd(o, proj, _row(rep["gdn_norm_g"][j]), name=f"gdn_onorm_{tag}")
            x, y = _mm_resid(og, lw["w_out"], x, gate_m, name=f"gdn_out_{tag}")
            rec.update(proj=proj, ab=ab, qkv=qkv, gbc=gbc, bbc=bbc, states=states, o=o, og=og, y=y)
        else:
            proj = _mm(h, lw["w_in"], mode="nn", out_dtype=F32, name=f"mla_in_{tag}")
            cq, ck = _mla_prep_fwd(proj, _row(rep["mla_q_norm_g"][j]), _row(rep["mla_kv_norm_g"][j]),
                                   name=f"mla_prep_{tag}")
            qf = _mm(cq, lw["w_uq"], mode="nn", out_dtype=F32, name=f"mla_uq_{tag}")
            kvf = _mm(ck, lw["w_ukv"], mode="nn", out_dtype=F32, name=f"mla_ukv_{tag}")
            nrope = MLA_HEADS * MLA_ROPE
            krp = jnp.pad(proj[:, MLA_Q_RANK + MLA_KV_RANK:], ((0, 0), (0, LANES - MLA_ROPE)))
            roped = _rope(jnp.concatenate([qf[:, MLA_HEADS * MLA_NOPE:], krp], axis=1), cos_t, sin_t,
                          name=f"rope_{tag}")
            q_nope = qf[:, :MLA_HEADS * MLA_NOPE].reshape(t, MLA_HEADS, MLA_NOPE)
            q_rope = roped[:, :nrope].reshape(t, MLA_HEADS, MLA_ROPE)
            k_rope = jnp.broadcast_to(roped[:, None, nrope:nrope + MLA_ROPE], (t, MLA_HEADS, MLA_ROPE))
            kv3 = kvf.reshape(t, MLA_HEADS, MLA_NOPE + MLA_V)
            qc = jnp.transpose(jnp.concatenate([q_nope, q_rope], axis=-1), (1, 0, 2)).astype(BF16)
            kc = jnp.transpose(jnp.concatenate([kv3[..., :MLA_NOPE], k_rope], axis=-1), (1, 0, 2)).astype(BF16)
            vc = jnp.transpose(kv3[..., MLA_NOPE:], (1, 0, 2)).astype(BF16)
            oh, lse = _attn_fwd(qc, kc, vc, name=f"attn_{tag}")
            oc = jnp.transpose(oh, (1, 0, 2)).reshape(t, MLA_HEADS * MLA_V).astype(BF16)
            x, y = _mm_resid(oc, lw["w_out"], x, gate_m, name=f"mla_out_{tag}")
            rec.update(proj=proj, cq=cq, ck=ck, qc=qc, kc=kc, vc=vc, oh=oh, lse=lse, oc=oc, y=y)
        rec["x1"] = x
        lw.update(get_weights(layer, "ffn", x))
        h2 = _adaln_fwd(x, _row(rep["norm_ffn_g"][layer]), scale_f, shift_f, name=f"adaln_ffn_{tag}")
        s, ab2 = _ffn_gu_fwd(h2, lw["w_gu"], name=f"ffn_gu_{tag}")
        x, y2 = _mm_resid(s, lw["w_down"], x, gate_f, name=f"ffn_down_{tag}")
        rec.update(h2=h2, ab2=ab2, s=s, y2=y2)
        saved.append(rec)

    dx, st, ls = _loss_head(x, _row(rep["final_norm_g"]), target, name="loss_head")
    loss = ls[0, 0]
    grads = {"final_norm_g": st[0]}
    per_layer = {k: [None] * DEPTH for k in ("norm_mix_g", "norm_ffn_g")}
    per_gdn = {k: [None] * 2 for k in ("gdn_conv_wt", "gdn_a_log", "gdn_dt_bias", "gdn_norm_g")}
    per_mla = {k: [None] * 2 for k in ("mla_q_norm_g", "mla_kv_norm_g")}
    dmod = [None] * DEPTH
    dep = jnp.zeros((8, LANES), F32)

    for layer in reversed(range(DEPTH)):
        j = layer // 2
        tag = f"l{layer}"
        rec = saved[layer]
        lw = rec["lw"]
        shift_m, scale_m, gate_m, shift_f, scale_f, gate_f = [_row(mod[layer, i]) for i in range(N_MOD)]
        dy2, st_g = _gate_bwd(dx, rec["y2"], gate_f, dep, name=f"gate_bwd_ffn_{tag}")
        dgate_f = st_g[0]
        dw_down = _mm(rec["s"], dy2, mode="tn", out_dtype=BF16, tm=256, tn=1024, name=f"ffn_down_dw_{tag}")
        dab2 = _ffn_down_dx(dy2, lw["w_down"], rec["ab2"], name=f"ffn_down_dx_{tag}")
        dw_gu = _mm(rec["h2"], dab2, mode="tn", out_dtype=BF16, tm=1024, tn=512, name=f"ffn_gu_dw_{tag}")
        dep = put_grads(layer, "ffn", {"w_gu": dw_gu, "w_down": dw_down})
        dh2 = _mm(dab2, lw["w_gu"], mode="nt", out_dtype=BF16, tm=256, tn=1024, name=f"ffn_gu_dx_{tag}")
        dx, st_n = _adaln_bwd(rec["x1"], _row(rep["norm_ffn_g"][layer]), scale_f, shift_f, dh2, dx,
                              name=f"adaln_ffn_bwd_{tag}")
        per_layer["norm_ffn_g"][layer] = st_n[0]
        dscale_f, dshift_f = st_n[1], st_n[2]
        dy, st_g = _gate_bwd(dx, rec["y"], gate_m, dep, name=f"gate_bwd_mix_{tag}")
        dgate_m = st_g[0]
        big = {}
        if layer % 2 == 0:
            big["w_out"] = _mm(rec["og"], dy, mode="tn", out_dtype=BF16, name=f"gdn_out_dw_{tag}")
            dog = _mm(dy, lw["w_out"], mode="nt", out_dtype=BF16, name=f"gdn_out_dx_{tag}")
            do, dgp, st_o = _gdn_onorm_bwd(rec["o"], rec["proj"], _row(rep["gdn_norm_g"][j]), dog,
                                           name=f"gdn_onorm_bwd_{tag}")
            per_gdn["gdn_norm_g"][j] = st_o[0]
            dqkv3 = _gdn_chunk_bwd(rec["qkv"], rec["gbc"], rec["bbc"], rec["states"], do, name=f"gdn_chunk_bwd_{tag}")
            dq_, dk_, dv_, dgc_, dbc_ = dqkv3
            dqkv = jnp.concatenate([dq_, dk_, dv_], axis=1)
            dgb = jnp.concatenate([jnp.transpose(dgc_[:, :, 0]), jnp.transpose(dbc_[:, :, 0])], axis=1)
            dgb = jnp.pad(dgb, ((0, 0), (0, LANES - 2 * GDN_HEADS)))
            dab, st_a = _gdn_gate_bwd(rec["ab"], rep["gdn_gate_prm"][j], dgb, name=f"gdn_gate_bwd_{tag}")
            per_gdn["gdn_a_log"][j] = st_a[0, :GDN_HEADS]
            per_gdn["gdn_dt_bias"][j] = st_a[1, :GDN_HEADS]
            dpre, dcw = _gdn_prep_bwd(rec["proj"], rep["gdn_conv_wt"][j], dqkv, name=f"gdn_prep_bwd_{tag}")
            per_gdn["gdn_conv_wt"][j] = dcw
            dproj = jnp.concatenate([dpre, dgp], axis=1)
            dw_main = _mm(rec["h"], dproj, mode="tn", out_dtype=BF16, tm=1024, tn=512, name=f"gdn_in_dw_{tag}")
            dw_ab = _mm(rec["h"], dab, mode="tn", out_dtype=BF16, name=f"gdn_in_ab_dw_{tag}")
            big["w_in"] = jnp.concatenate([dw_main, dw_ab[:, :2 * GDN_HEADS]], axis=1)
            dep = put_grads(layer, "gdn", big)
            dh_ab = _mm(dab, lw["w_ab"], mode="nt", out_dtype=F32, name=f"gdn_in_ab_dx_{tag}")
            dh = _mm(dproj, lw["w_main"], mode="nt", out_dtype=BF16, add=dh_ab, tm=256, tn=1024,
                     name=f"gdn_in_dx_{tag}")
        else:
            big["w_out"] = _mm(rec["oc"], dy, mode="tn", out_dtype=BF16, name=f"mla_out_dw_{tag}")
            doc = _mm(dy, lw["w_out"], mode="nt", out_dtype=F32, name=f"mla_out_dx_{tag}")
            doh = jnp.transpose(doc.reshape(t, MLA_HEADS, MLA_V), (1, 0, 2))
            dqc, dkc, dvc = _attn_bwd(rec["qc"], rec["kc"], rec["vc"], rec["oh"], rec["lse"], doh,
                                      name=f"attn_bwd_{tag}")
            dqn = jnp.transpose(dqc[..., :MLA_NOPE], (1, 0, 2)).reshape(t, MLA_HEADS * MLA_NOPE)
            dqr = jnp.transpose(dqc[..., MLA_NOPE:], (1, 0, 2)).reshape(t, MLA_HEADS * MLA_ROPE)
            dkr = jnp.pad(jnp.sum(dkc[..., MLA_NOPE:], axis=0), ((0, 0), (0, LANES - MLA_ROPE)))
            drope = _rope_bwd(jnp.concatenate([dqr, dkr], axis=1), cos_t, sin_t, name=f"rope_bwd_{tag}")
            nrope = MLA_HEADS * MLA_ROPE
            dqf = jnp.concatenate([dqn, drope[:, :nrope]], axis=1).astype(BF16)
            dkvf = jnp.concatenate([jnp.transpose(dkc[..., :MLA_NOPE], (1, 0, 2)), jnp.transpose(dvc, (1, 0, 2))],
                                   axis=-1).reshape(t, MLA_HEADS * (MLA_NOPE + MLA_V)).astype(BF16)
            big["w_uq"] = _mm(rec["cq"], dqf, mode="tn", out_dtype=BF16, name=f"mla_uq_dw_{tag}")
            big["w_ukv"] = _mm(rec["ck"], dkvf, mode="tn", out_dtype=BF16, name=f"mla_ukv_dw_{tag}")
            dcq = _mm(dqf, lw["w_uq"], mode="nt", out_dtype=F32, name=f"mla_uq_dx_{tag}")
            dck = _mm(dkvf, lw["w_ukv"], mode="nt", out_dtype=F32, name=f"mla_ukv_dx_{tag}")
            dproj, st_p = _mla_prep_bwd(rec["proj"], _row(rep["mla_q_norm_g"][j]), _row(rep["mla_kv_norm_g"][j]),
                                        dcq, dck, drope[:, nrope:nrope + MLA_ROPE], name=f"mla_prep_bwd_{tag}")
            per_mla["mla_q_norm_g"][j] = st_p[0, :MLA_Q_RANK]
            per_mla["mla_kv_norm_g"][j] = st_p[0, MLA_Q_RANK:MLA_Q_RANK + MLA_KV_RANK]
            big["w_in"] = _mm(rec["h"], dproj, mode="tn", out_dtype=BF16, name=f"mla_in_dw_{tag}")
            dep = put_grads(layer, "mla", big)
            dh = _mm(dproj, lw["w_in"], mode="nt", out_dtype=BF16, name=f"mla_in_dx_{tag}")
        dx, st_n = _adaln_bwd(rec["x0"], _row(rep["norm_mix_g"][layer]), scale_m, shift_m, dh, dx,
                              name=f"adaln_mix_bwd_{tag}")
        per_layer["norm_mix_g"][layer] = st_n[0]
        dmod[layer] = jnp.stack([st_n[2], st_n[1], dgate_m, dshift_f, dscale_f, dgate_f])

    for d in (per_layer, per_gdn, per_mla):
        for k, v in d.items():
            grads[k] = jnp.stack(v)
    return loss, dx, jnp.stack(dmod), grads


BIG = ("gdn_w_in", "gdn_w_out", "mla_w_in", "mla_w_uq", "mla_w_ukv", "mla_w_out", "ffn_w_gate", "ffn_w_up",
       "ffn_w_down")
COL_SHARDED = ("gdn_w_in", "mla_w_uq", "mla_w_ukv", "ffn_w_gate", "ffn_w_up")
SMALL = ("ada_b", "norm_mix_g", "norm_ffn_g", "gdn_conv_w", "gdn_a_log", "gdn_dt_bias", "gdn_norm_g",
         "mla_q_norm_g", "mla_kv_norm_g", "final_norm_g")
WEIGHTS = ("ada_w", "ada_b", "norm_mix_g", "norm_ffn_g", "gdn_w_in", "gdn_conv_w", "gdn_a_log", "gdn_dt_bias",
           "gdn_norm_g", "gdn_w_out", "mla_w_in", "mla_q_norm_g", "mla_kv_norm_g", "mla_w_uq", "mla_w_ukv",
           "mla_w_out", "ffn_w_gate", "ffn_w_up", "ffn_w_down", "final_norm_g")


def _uq_to_kernel_layout(w):
    lead = w.shape[:-1]
    w4 = w.reshape(lead + (MLA_HEADS, MLA_QK))
    return jnp.concatenate([w4[..., :MLA_NOPE].reshape(lead + (-1,)), w4[..., MLA_NOPE:].reshape(lead + (-1,))],
                           axis=-1)


def _uq_from_kernel_layout(w):
    lead = w.shape[:-1]
    nope = w[..., :MLA_HEADS * MLA_NOPE].reshape(lead + (MLA_HEADS, MLA_NOPE))
    rope = w[..., MLA_HEADS * MLA_NOPE:].reshape(lead + (MLA_HEADS, MLA_ROPE))
    return jnp.concatenate([nope, rope], axis=-1).reshape(lead + (-1,))


def _group_names(layer, kind):
    if kind == "ffn":
        return ("ffn_w_gate", "ffn_w_up", "ffn_w_down")
    return ("gdn_w_in", "gdn_w_out") if layer % 2 == 0 else ("mla_w_in", "mla_w_uq", "mla_w_ukv", "mla_w_out")


def _layer_index(name, layer):
    return layer if name.startswith("ffn") else layer // 2


def _cols(g):
    return jnp.transpose(g, (1, 0, 2)).reshape(g.shape[1], N_DEV * g.shape[2])


def _rows(g):
    return g.reshape(N_DEV * g.shape[1], g.shape[2])


def _uncols(full):
    r, c = full.shape
    return jnp.transpose(full.reshape(r, N_DEV, c // N_DEV), (1, 0, 2))


def _unrows(full):
    r, c = full.shape
    return full.reshape(N_DEV, r // N_DEV, c)


def _group_weights(layer, kind, got, zero):
    if kind == "ffn":
        return {"w_gu": _gu_to_kernel_layout(_cols(got["ffn_w_gate"]), _cols(got["ffn_w_up"])) + zero,
                "w_down": _rows(got["ffn_w_down"])}
    if layer % 2 == 0:
        w_in = _cols(got["gdn_w_in"]) + zero
        return dict(w_main=w_in[:, :GDN_MAIN], w_ab=jnp.pad(w_in[:, GDN_MAIN:], ((0, 0), (0, LANES - 2 * GDN_HEADS))),
                    w_out=_rows(got["gdn_w_out"]))
    return dict(w_in=_rows(got["mla_w_in"]), w_uq=_uq_to_kernel_layout(_cols(got["mla_w_uq"])) + zero,
                w_ukv=_cols(got["mla_w_ukv"]), w_out=_rows(got["mla_w_out"]))


def _layer_grad_slots(kind, big):
    if kind == "ffn":
        d_gate, d_up = _gu_from_kernel_layout(big["w_gu"])
        return {"ffn_w_gate": _uncols(d_gate), "ffn_w_up": _uncols(d_up), "ffn_w_down": _unrows(big["w_down"])}
    if kind == "gdn":
        return {"gdn_w_in": _uncols(big["w_in"]), "gdn_w_out": _unrows(big["w_out"])}
    return {"mla_w_in": _unrows(big["w_in"]), "mla_w_uq": _uncols(_uq_from_kernel_layout(big["w_uq"])),
            "mla_w_ukv": _uncols(big["w_ukv"]), "mla_w_out": _unrows(big["w_out"])}


def _small_weights(tiny, rep):
    prm = jnp.zeros((2, 8, LANES), F32)
    prm = prm.at[:, 0, :GDN_HEADS].set(rep["gdn_a_log"]).at[:, 1, :GDN_HEADS].set(rep["gdn_dt_bias"])
    out = {
        "gdn_conv_wt": jnp.transpose(_gather_rows(tiny["gdn_conv_w"]), (0, 2, 1)),
        "mla_q_norm_g": jnp.transpose(tiny["mla_q_norm_g"], (1, 0, 2)).reshape(2, MLA_Q_RANK),
        "mla_kv_norm_g": jnp.transpose(tiny["mla_kv_norm_g"], (1, 0, 2)).reshape(2, MLA_KV_RANK),
        "gdn_gate_prm": prm,
    }
    for k in ("norm_mix_g", "norm_ffn_g", "gdn_norm_g", "final_norm_g"):
        out[k] = rep[k]
    return out


def _rope_tables(positions):
    inv_freq = ROPE_THETA ** (-jnp.arange(0, MLA_ROPE, 2, dtype=F32) / MLA_ROPE)
    ang = positions.astype(F32)[:, None] * inv_freq
    cos, sin = jnp.cos(ang), jnp.sin(ang)
    reps = LANES // MLA_ROPE
    return jnp.tile(jnp.concatenate([cos, cos], axis=1), (1, reps)), jnp.tile(
        jnp.concatenate([-sin, sin], axis=1), (1, reps))


def kernel(x, c, positions, ada_w, ada_b, norm_mix_g, norm_ffn_g, gdn_w_in, gdn_conv_w, gdn_a_log, gdn_dt_bias, gdn_norm_g, gdn_w_out, mla_w_in, mla_q_norm_g, mla_kv_norm_g, mla_w_uq, mla_w_ukv, mla_w_out, ffn_w_gate, ffn_w_up, ffn_w_down, final_norm_g, loss_target, m_ada_w, m_ada_b, m_norm_mix_g, m_norm_ffn_g, m_gdn_w_in, m_gdn_conv_w, m_gdn_a_log, m_gdn_dt_bias, m_gdn_norm_g, m_gdn_w_out, m_mla_w_in, m_mla_q_norm_g, m_mla_kv_norm_g, m_mla_w_uq, m_mla_w_ukv, m_mla_w_out, m_ffn_w_gate, m_ffn_w_up, m_ffn_w_down, m_final_norm_g, v_ada_w, v_ada_b, v_norm_mix_g, v_norm_ffn_g, v_gdn_w_in, v_gdn_conv_w, v_gdn_a_log, v_gdn_dt_bias, v_gdn_norm_g, v_gdn_w_out, v_mla_w_in, v_mla_q_norm_g, v_mla_kv_norm_g, v_mla_w_uq, v_mla_w_ukv, v_mla_w_out, v_ffn_w_gate, v_ffn_w_up, v_ffn_w_down, v_final_norm_g):
    W = dict(ada_w=ada_w, ada_b=ada_b, norm_mix_g=norm_mix_g, norm_ffn_g=norm_ffn_g, gdn_w_in=gdn_w_in,
             gdn_conv_w=gdn_conv_w, gdn_a_log=gdn_a_log, gdn_dt_bias=gdn_dt_bias, gdn_norm_g=gdn_norm_g,
             gdn_w_out=gdn_w_out, mla_w_in=mla_w_in, mla_q_norm_g=mla_q_norm_g, mla_kv_norm_g=mla_kv_norm_g,
             mla_w_uq=mla_w_uq, mla_w_ukv=mla_w_ukv, mla_w_out=mla_w_out, ffn_w_gate=ffn_w_gate,
             ffn_w_up=ffn_w_up, ffn_w_down=ffn_w_down, final_norm_g=final_norm_g)
    M = dict(ada_w=m_ada_w, ada_b=m_ada_b, norm_mix_g=m_norm_mix_g, norm_ffn_g=m_norm_ffn_g, gdn_w_in=m_gdn_w_in,
             gdn_conv_w=m_gdn_conv_w, gdn_a_log=m_gdn_a_log, gdn_dt_bias=m_gdn_dt_bias, gdn_norm_g=m_gdn_norm_g,
             gdn_w_out=m_gdn_w_out, mla_w_in=m_mla_w_in, mla_q_norm_g=m_mla_q_norm_g,
             mla_kv_norm_g=m_mla_kv_norm_g, mla_w_uq=m_mla_w_uq, mla_w_ukv=m_mla_w_ukv, mla_w_out=m_mla_w_out,
             ffn_w_gate=m_ffn_w_gate, ffn_w_up=m_ffn_w_up, ffn_w_down=m_ffn_w_down, final_norm_g=m_final_norm_g)
    V = dict(ada_w=v_ada_w, ada_b=v_ada_b, norm_mix_g=v_norm_mix_g, norm_ffn_g=v_norm_ffn_g, gdn_w_in=v_gdn_w_in,
             gdn_conv_w=v_gdn_conv_w, gdn_a_log=v_gdn_a_log, gdn_dt_bias=v_gdn_dt_bias, gdn_norm_g=v_gdn_norm_g,
             gdn_w_out=v_gdn_w_out, mla_w_in=v_mla_w_in, mla_q_norm_g=v_mla_q_norm_g,
             mla_kv_norm_g=v_mla_kv_norm_g, mla_w_uq=v_mla_w_uq, mla_w_ukv=v_mla_w_ukv, mla_w_out=v_mla_w_out,
             ffn_w_gate=v_ffn_w_gate, ffn_w_up=v_ffn_w_up, ffn_w_down=v_ffn_w_down, final_norm_g=v_final_norm_g)
    me = 4 * lax.axis_index("x") + 2 * lax.axis_index("y") + lax.axis_index("c")
    t = x.shape[1]
    wc = ada_w.shape[-1]

    tiny_shapes = [c.shape, gdn_conv_w.shape, mla_q_norm_g.shape, mla_kv_norm_g.shape]
    (tiny_g,) = _exchange([_pack([c, gdn_conv_w, mla_q_norm_g, mla_kv_norm_g])], scatter=False, name="gather_tiny")
    c_g, conv_g, qn_g, kvn_g = _unpack(tiny_g, tiny_shapes, lead=(N_DEV,))
    c_all = c_g.reshape(N_DEV, D_MODEL)
    rep = _small_weights({"gdn_conv_w": conv_g, "mla_q_norm_g": qn_g, "mla_kv_norm_g": kvn_g}, W)

    groups = [(layer, kind) for layer in range(DEPTH) for kind in ("mix", "ffn")]

    def start_group(i, dep):
        layer, kind = groups[i]
        srcs = [W[k][_layer_index(k, layer)].astype(BF16) for k in _group_names(layer, kind)]
        return _exchange_start(srcs, scatter=False, name=f"gather_start_{kind}_l{layer}", dep=dep)

    gather = {0: start_group(0, tiny_g)}

    b_cols = lax.dynamic_slice_in_dim(ada_b, me * wc, wc, axis=1).reshape(DEPTH, 1, wc)
    mod_part = _ada_mod(c_all, ada_w, b_cols, name="ada_mod")
    (mod_g,) = _exchange([mod_part], scatter=False, name="gather_mod")
    mod_mine = lax.dynamic_index_in_dim(mod_g, me, axis=2, keepdims=False)
    mod = jnp.transpose(mod_mine, (1, 0, 2)).reshape(DEPTH, N_MOD, D_MODEL)

    def get_weights(layer, kind, after):
        i = groups.index((layer, kind))
        srcs, lands = _exchange_wait(gather[i], mod if i == 0 else after, scatter=False,
                                     name=f"gather_wait_{kind}_l{layer}")
        zero = jnp.zeros((), BF16)
        if i + 1 < len(groups):
            gather[i + 1] = start_group(i + 1, lands[0])
            zero = gather[i + 1][4][0, 0].astype(BF16)
        got = {k: lax.dynamic_update_index_in_dim(z, s, me, 0)
               for k, s, z in zip(_group_names(layer, kind), srcs, lands)}
        return _group_weights(layer, kind, got, zero)

    scatter = []

    def put_grads(layer, kind, big):
        slots = _layer_grad_slots(kind, big)
        started = _exchange_start(list(slots.values()), scatter=True, name=f"scatter_start_{kind}_l{layer}")
        scatter.append((layer, kind, list(slots.keys()), started))
        return started[4]

    cos_t, sin_t = _rope_tables(positions[0])
    loss, dx, dmod, g = _local_step(x[0], loss_target[0], mod, cos_t, sin_t, rep, get_weights, put_grads)

    parts = {k: [None] * W[k].shape[0] for k in BIG}
    res = {}

    def wait_group(entry, after):
        layer, kind, names, started = entry
        srcs, lands = _exchange_wait(started, after, scatter=True, name=f"scatter_wait_{kind}_l{layer}")
        for k, s, z in zip(names, srcs, lands):
            own = lax.dynamic_index_in_dim(s, me, 0, keepdims=False)
            parts[k][_layer_index(k, layer)] = lax.dynamic_update_index_in_dim(z, own, me, 0)

    for entry in scatter[:-1]:
        wait_group(entry, dx)
    early = [k for k in BIG if k not in scatter[-1][2]]
    for k in early:
        res[k] = _adamw(parts[k], W[k], M[k], V[k], name=f"adamw_{k}")

    small_local = [dmod.reshape(DEPTH, N_MOD * D_MODEL), g["norm_mix_g"], g["norm_ffn_g"],
                   jnp.transpose(g["gdn_conv_wt"], (0, 2, 1)), g["gdn_a_log"], g["gdn_dt_bias"], g["gdn_norm_g"],
                   g["mla_q_norm_g"], g["mla_kv_norm_g"], g["final_norm_g"], loss.reshape(1)]
    small_shapes = [a.shape for a in small_local]
    (small_g,) = _exchange([_pack(small_local)], scatter=False, name="gather_small_grads")
    small_sum = _unpack(_sum_parts(small_g, name="sum_small_grads"), small_shapes)
    loss = small_sum[-1][0]
    dmod_all = _unpack(small_g, small_shapes[:1], lead=(N_DEV,))[0]
    sg = dict(zip(SMALL, small_sum))
    wait_group(scatter[-1], small_g)
    sg["gdn_conv_w"] = lax.dynamic_slice_in_dim(sg["gdn_conv_w"], me * gdn_conv_w.shape[1], gdn_conv_w.shape[1], 1)
    sg["mla_q_norm_g"] = lax.dynamic_slice_in_dim(sg["mla_q_norm_g"], me * mla_q_norm_g.shape[1],
                                                  mla_q_norm_g.shape[1], 1)
    sg["mla_kv_norm_g"] = lax.dynamic_slice_in_dim(sg["mla_kv_norm_g"], me * mla_kv_norm_g.shape[1],
                                                   mla_kv_norm_g.shape[1], 1)

    dmod_cols = jnp.transpose(lax.dynamic_slice_in_dim(dmod_all, me * wc, wc, axis=2), (1, 0, 2))
    res["ada_w"] = _ada_grad_adamw(c_all, dmod_cols, ada_w, m_ada_w, v_ada_w, name="ada_w_grad_adamw")
    for k in BIG:
        if k not in early:
            res[k] = _adamw(parts[k], W[k], M[k], V[k], name=f"adamw_{k}")
    shapes = [W[k].shape for k in SMALL]
    packed = [_pack([d[k] for k in SMALL]) for d in (sg, W, M, V)]
    outs = _adamw([packed[0][None]], packed[1][None], packed[2][None], packed[3][None], name="adamw_small")
    unpacked = [_unpack(o[0], shapes) for o in outs]
    for i, k in enumerate(SMALL):
        res[k] = tuple(u[i] for u in unpacked)

    return (loss, dx[None], *[res[k][0] for k in WEIGHTS], *[res[k][1] for k in WEIGHTS],
            *[res[k][2] for k in WEIGHTS], *[res[k][3] for k in WEIGHTS])
```

```python
import functools
import math

import jax
import jax.numpy as jnp
from jax import lax
from jax.experimental import pallas as pl
from jax.experimental.pallas import tpu as pltpu

F32 = jnp.float32
BF16 = jnp.bfloat16
MXU_DTYPE = jnp.bfloat16

N_DEV = 8
D_MODEL = 1024
DEPTH = 4
GDN_HEADS = 8
GDN_HEAD_DIM = 128
GDN_KEY_DIM = GDN_HEADS * GDN_HEAD_DIM
GDN_CHUNK = 64
GDN_HEAD_BATCH = 8
GDN_CONV = 4
GDN_MAIN = 4 * GDN_KEY_DIM
MLA_HEADS = 8
MLA_NOPE = 128
MLA_ROPE = 64
MLA_V = 128
MLA_Q_RANK = 384
MLA_KV_RANK = 256
MLA_IN = MLA_Q_RANK + MLA_KV_RANK + MLA_ROPE
MLA_QK = MLA_NOPE + MLA_ROPE
ROPE_THETA = 10000.0
D_FF = 2816
N_MOD = 6
EPS = 1e-6
LANES = 128
VMEM_LIMIT = 48 * 1024 * 1024

ADAM_LR = 0.001
ADAM_B1 = 0.9
ADAM_B2 = 0.999
ADAM_EPS = 1e-08
ADAM_WD = 0.01
ADAM_STEP = 10
ADAM_BC1 = 1.0 - ADAM_B1 ** ADAM_STEP
ADAM_BC2 = 1.0 - ADAM_B2 ** ADAM_STEP

NN = (((1,), (0,)), ((), ()))
NT = (((1,), (1,)), ((), ()))
TN = (((0,), (0,)), ((), ()))
NEG = -1e30


def _dotb(a, b, dims):
    return lax.dot_general(a.astype(MXU_DTYPE), b.astype(MXU_DTYPE), dims, preferred_element_type=F32)


def _split(a):
    hi = a.astype(BF16)
    return hi, (a - hi.astype(F32)).astype(BF16)


def _dotf(a, b, dims):
    ah, al = _split(a)
    bh, bl = _split(b)
    dot = lambda u, v: lax.dot_general(u, v, dims, preferred_element_type=F32)
    return dot(ah, bh) + (dot(ah, bl) + dot(al, bh))


def _params(*sem):
    return pltpu.CompilerParams(dimension_semantics=sem, vmem_limit_bytes=VMEM_LIMIT)


def _pick(n, pref, mult=LANES):
    best = None
    t = mult
    while t <= min(n, pref):
        if n % t == 0:
            best = t
        t += mult
    return best if best is not None else n


def _sigmoid(z):
    return 1.0 / (1.0 + jnp.exp(-z))


def _exchange(arrays, *, scatter, name):
    n = len(arrays)
    out_shape = tuple(
        jax.ShapeDtypeStruct(a.shape if scatter else (N_DEV,) + a.shape, a.dtype) for a in arrays)

    def body(*refs):
        ins, outs = refs[:n], refs[n:2 * n]
        send_sems, recv_sems, local_sems = refs[2 * n:]
        x, y, c = lax.axis_index("x"), lax.axis_index("y"), lax.axis_index("c")
        me = 4 * x + 2 * y + c
        copies = []
        for k in range(n):
            src_own = ins[k].at[me] if scatter else ins[k]
            own = pltpu.make_async_copy(src_own, outs[k].at[me], local_sems.at[k])
            own.start()
            copies.append(own)
        sends = []
        for p in range(1, N_DEV):
            px, py, pc = x ^ ((p >> 2) & 1), y ^ ((p >> 1) & 1), c ^ (p & 1)
            peer = 4 * px + 2 * py + pc
            for k in range(n):
                cp = pltpu.make_async_remote_copy(
                    src_ref=ins[k].at[peer] if scatter else ins[k],
                    dst_ref=outs[k].at[me],
                    send_sem=send_sems.at[k, p - 1],
                    recv_sem=recv_sems.at[k, p - 1],
                    device_id=(px, py, pc),
                    device_id_type=pl.DeviceIdType.MESH,
                )
                cp.start()
                sends.append((cp, k, peer, p))
        for cp, k, peer, p in sends:
            pltpu.make_async_remote_copy(
                src_ref=ins[k].at[peer] if scatter else ins[k],
                dst_ref=outs[k].at[peer],
                send_sem=send_sems.at[k, p - 1],
                recv_sem=recv_sems.at[k, p - 1],
                device_id=(x, y, c),
                device_id_type=pl.DeviceIdType.MESH,
            ).wait_recv()
        for cp, _, _, _ in sends:
            cp.wait_send()
        for own in copies:
            own.wait()

    any_spec = pl.BlockSpec(memory_space=pl.ANY)
    outs = pl.pallas_call(
        body,
        name=name,
        out_shape=out_shape,
        in_specs=[any_spec] * n,
        out_specs=tuple([any_spec] * n),
        scratch_shapes=[
            pltpu.SemaphoreType.DMA((n, N_DEV - 1)),
            pltpu.SemaphoreType.DMA((n, N_DEV - 1)),
            pltpu.SemaphoreType.DMA((n,)),
        ],
        compiler_params=pltpu.CompilerParams(has_side_effects=True),
    )(*arrays)
    return list(outs)


def _peer(x, y, c, p):
    return x ^ ((p >> 2) & 1), y ^ ((p >> 1) & 1), c ^ (p & 1)


def _exchange_start(arrays, *, scatter, name, dep=None):
    n = len(arrays)
    deps = [] if dep is None else [dep]
    lands = [lax.empty(a.shape if scatter else (N_DEV,) + a.shape, a.dtype) for a in arrays]

    def body(*refs):
        ins, zones = refs[:n], refs[n:2 * n]
        send_sems, recv_sems = refs[2 * n + len(deps)], refs[2 * n + len(deps) + 1]
        token = refs[-1]
        x, y, c = lax.axis_index("x"), lax.axis_index("y"), lax.axis_index("c")
        me = 4 * x + 2 * y + c
        for p in range(1, N_DEV):
            px, py, pc = _peer(x, y, c, p)
            for k in range(n):
                pltpu.make_async_remote_copy(
                    src_ref=ins[k].at[4 * px + 2 * py + pc] if scatter else ins[k],
                    dst_ref=zones[k].at[me],
                    send_sem=send_sems.at[k * (N_DEV - 1) + p - 1],
                    recv_sem=recv_sems.at[k * (N_DEV - 1) + p - 1],
                    device_id=(px, py, pc),
                    device_id_type=pl.DeviceIdType.MESH,
                ).start()
        token[...] = jnp.zeros_like(token)

    hbm = pl.BlockSpec(memory_space=pltpu.HBM)
    sem = pl.BlockSpec(memory_space=pltpu.SEMAPHORE)
    outs = pl.pallas_call(
        body,
        name=name,
        out_shape=(pltpu.SemaphoreType.DMA((n * (N_DEV - 1),)), pltpu.SemaphoreType.DMA((n * (N_DEV - 1),)),
                   *[pltpu.HBM(a.shape, a.dtype) for a in arrays], *[pltpu.HBM(z.shape, z.dtype) for z in lands],
                   jax.ShapeDtypeStruct((8, LANES), F32)),
        in_specs=[hbm] * (2 * n) + [pl.BlockSpec(memory_space=pl.ANY)] * len(deps),
        out_specs=(sem, sem, *[hbm] * (2 * n), pl.BlockSpec(memory_space=pltpu.VMEM)),
        input_output_aliases={k: 2 + k for k in range(2 * n)},
        compiler_params=pltpu.CompilerParams(has_side_effects=pltpu.SideEffectType.DATAFLOW_SIDE_EFFECTING),
    )(*[pltpu.with_memory_space_constraint(a, pltpu.HBM) for a in arrays],
      *[pltpu.with_memory_space_constraint(z, pltpu.HBM) for z in lands], *deps)
    return outs[0], outs[1], list(outs[2:2 + n]), list(outs[2 + n:2 + 2 * n]), outs[-1]


def _exchange_wait(started, after, *, scatter, name):
    send_sems, recv_sems, srcs, lands, _ = started
    n = len(srcs)

    def body(*refs):
        ins, zones = refs[:n], refs[n:2 * n]
        s_sems, r_sems = refs[2 * n], refs[2 * n + 1]
        x, y, c = lax.axis_index("x"), lax.axis_index("y"), lax.axis_index("c")
        for p in range(1, N_DEV):
            px, py, pc = _peer(x, y, c, p)
            peer = 4 * px + 2 * py + pc
            for k in range(n):
                cp = pltpu.make_async_remote_copy(
                    src_ref=ins[k].at[peer] if scatter else ins[k],
                    dst_ref=zones[k].at[peer],
                    send_sem=s_sems.at[k * (N_DEV - 1) + p - 1],
                    recv_sem=r_sems.at[k * (N_DEV - 1) + p - 1],
                    device_id=(px, py, pc),
                    device_id_type=pl.DeviceIdType.MESH,
                )
                cp.wait_send()
                cp.wait_recv()

    hbm = pl.BlockSpec(memory_space=pltpu.HBM)
    sem = pl.BlockSpec(memory_space=pltpu.SEMAPHORE)
    outs = pl.pallas_call(
        body,
        name=name,
        out_shape=tuple(pltpu.HBM(a.shape, a.dtype) for a in srcs + lands),
        in_specs=[hbm] * (2 * n) + [sem, sem, pl.BlockSpec(memory_space=pl.ANY)],
        out_specs=tuple([hbm] * (2 * n)),
        input_output_aliases={k: k for k in range(2 * n)},
        compiler_params=pltpu.CompilerParams(has_side_effects=pltpu.SideEffectType.DATAFLOW_SIDE_EFFECTING),
    )(*srcs, *lands, send_sems, recv_sems, after)
    return list(outs[:n]), list(outs[n:])


def _mm(a, b, *, mode, out_dtype, name, add=None, tm=512, tn=512):
    if mode == "nn":
        (m, kd), (_, nd) = a.shape, b.shape
    elif mode == "nt":
        (m, kd), (nd, _) = a.shape, b.shape
    else:
        (kd, m), (_, nd) = a.shape, b.shape
    tm = _pick(m, tm, LANES if mode == "tn" else 16)
    tn = _pick(nd, tn)
    dims = {"nn": NN, "nt": NT, "tn": TN}[mode]
    ni, nj = m // tm, nd // tn
    a_bytes, b_bytes = a.size * a.dtype.itemsize, b.size * b.dtype.itemsize
    i_outer = a_bytes + ni * b_bytes <= b_bytes + nj * a_bytes
    ij = (lambda g0, g1: (g0, g1)) if i_outer else (lambda g0, g1: (g1, g0))
    a_spec = (pl.BlockSpec((kd, tm), lambda g0, g1: (0, ij(g0, g1)[0])) if mode == "tn"
              else pl.BlockSpec((tm, kd), lambda g0, g1: (ij(g0, g1)[0], 0)))
    b_spec = (pl.BlockSpec((tn, kd), lambda g0, g1: (ij(g0, g1)[1], 0)) if mode == "nt"
              else pl.BlockSpec((kd, tn), lambda g0, g1: (0, ij(g0, g1)[1])))
    o_spec = pl.BlockSpec((tm, tn), lambda g0, g1: ij(g0, g1))
    has_add = add is not None

    def body(*refs):
        a_ref, b_ref = refs[0], refs[1]
        o_ref = refs[-1]
        acc = _dotb(a_ref[...], b_ref[...], dims)
        if has_add:
            acc = acc + refs[2][...].astype(F32)
        o_ref[...] = acc.astype(o_ref.dtype)

    ins = [a, b] + ([add] if has_add else [])
    specs = [a_spec, b_spec] + ([o_spec] if has_add else [])
    return pl.pallas_call(
        body, name=name, grid=(ni, nj) if i_outer else (nj, ni), in_specs=specs, out_specs=o_spec,
        out_shape=jax.ShapeDtypeStruct((m, nd), out_dtype),
        compiler_params=_params("parallel", "parallel"),
    )(*ins)


def _mm_resid(a, b, x, gate, *, name, tm=256, tn=1024):
    m, kd = a.shape
    nd = b.shape[1]
    tm = _pick(m, tm, 16)
    tn = _pick(nd, tn)
    o_spec = pl.BlockSpec((tm, tn), lambda i, j: (i, j))

    def body(a_ref, b_ref, x_ref, g_ref, xo_ref, y_ref):
        y = _dotb(a_ref[...], b_ref[...], NN)
        y_ref[...] = y
        xo_ref[...] = x_ref[...] + g_ref[...] * y

    return pl.pallas_call(
        body, name=name, grid=(m // tm, nd // tn),
        in_specs=[pl.BlockSpec((tm, kd), lambda i, j: (i, 0)), pl.BlockSpec((kd, tn), lambda i, j: (0, j)),
                  o_spec, pl.BlockSpec((1, tn), lambda i, j: (0, j))],
        out_specs=(o_spec, o_spec),
        out_shape=(jax.ShapeDtypeStruct((m, nd), F32), jax.ShapeDtypeStruct((m, nd), F32)),
        compiler_params=_params("parallel", "parallel"),
    )(a, b, x, gate)


ROWS = 256


def _row_spec(width, rows=ROWS):
    return pl.BlockSpec((rows, width), lambda i: (i, 0))


def _const_spec(shape):
    return pl.BlockSpec(shape, lambda i: tuple(0 for _ in shape))


def _adaln_fwd(x, g, scale, shift, *, name):
    t, d = x.shape

    def body(x_ref, g_ref, sc_ref, sh_ref, h_ref):
        xv = x_ref[...]
        r = lax.rsqrt(jnp.mean(xv * xv, axis=-1, keepdims=True) + EPS)
        h_ref[...] = (xv * r * g_ref[...] * (1.0 + sc_ref[...]) + sh_ref[...]).astype(h_ref.dtype)

    return pl.pallas_call(
        body, name=name, grid=(t // ROWS,),
        in_specs=[_row_spec(d), _const_spec((1, d)), _const_spec((1, d)), _const_spec((1, d))],
        out_specs=_row_spec(d), out_shape=jax.ShapeDtypeStruct((t, d), BF16),
        compiler_params=_params("parallel"),
    )(x, g, scale, shift)


def _adaln_bwd(x, g, scale, shift, dh, dres, dep, *, name):
    t, d = x.shape

    def body(x_ref, g_ref, sc_ref, sh_ref, dh_ref, dr_ref, dep_ref, dx_ref, st_ref):
        @pl.when(pl.program_id(0) == 0)
        def _():
            st_ref[...] = jnp.zeros_like(st_ref)

        xv = x_ref[...]
        dhv = dh_ref[...].astype(F32)
        gv = g_ref[...]
        r = lax.rsqrt(jnp.mean(xv * xv, axis=-1, keepdims=True) + EPS)
        xh = xv * r
        nv = xh * gv
        dn = dhv * (1.0 + sc_ref[...])
        dxh = dn * gv
        dx_ref[...] = dr_ref[...] + r * (dxh - xh * jnp.mean(dxh * xh, axis=-1, keepdims=True))
        st_ref[0:1, :] += jnp.sum(dn * xh, axis=0, keepdims=True)
        st_ref[1:2, :] += jnp.sum(dhv * nv, axis=0, keepdims=True)
        st_ref[2:3, :] += jnp.sum(dhv, axis=0, keepdims=True)

    return pl.pallas_call(
        body, name=name, grid=(t // ROWS,),
        in_specs=[_row_spec(d), _const_spec((1, d)), _const_spec((1, d)), _const_spec((1, d)),
                  _row_spec(d), _row_spec(d), _const_spec((8, LANES))],
        out_specs=(_row_spec(d), _const_spec((8, d))),
        out_shape=(jax.ShapeDtypeStruct((t, d), F32), jax.ShapeDtypeStruct((8, d), F32)),
        compiler_params=_params("arbitrary"),
    )(x, g, scale, shift, dh, dres, dep)


def _gate_bwd(dxo, y, gate, dep, *, name):
    t, d = dxo.shape

    def body(dx_ref, y_ref, g_ref, dep_ref, dy_ref, st_ref):
        @pl.when(pl.program_id(0) == 0)
        def _():
            st_ref[...] = jnp.zeros_like(st_ref)

        dxv = dx_ref[...]
        dy_ref[...] = (dxv * g_ref[...]).astype(dy_ref.dtype)
        st_ref[0:1, :] += jnp.sum(dxv * y_ref[...], axis=0, keepdims=True)

    return pl.pallas_call(
        body, name=name, grid=(t // ROWS,),
        in_specs=[_row_spec(d), _row_spec(d), _const_spec((1, d)), _const_spec((8, LANES))],
        out_specs=(_row_spec(d), _const_spec((8, d))),
        out_shape=(jax.ShapeDtypeStruct((t, d), BF16), jax.ShapeDtypeStruct((8, d), F32)),
        compiler_params=_params("arbitrary"),
    )(dxo, y, gate, dep)


def _loss_head(x, g, target, *, name):
    t, d = x.shape

    def body(x_ref, g_ref, t_ref, dx_ref, st_ref, ls_ref):
        @pl.when(pl.program_id(0) == 0)
        def _():
            st_ref[...] = jnp.zeros_like(st_ref)
            ls_ref[...] = jnp.zeros_like(ls_ref)

        xv = x_ref[...]
        gv = g_ref[...]
        r = lax.rsqrt(jnp.mean(xv * xv, axis=-1, keepdims=True) + EPS)
        xh = xv * r
        err = xh * gv - t_ref[...]
        ls_ref[...] += 0.5 * jnp.sum(jnp.mean(err * err, axis=-1, keepdims=True))
        dy = err * (1.0 / d)
        dxh = dy * gv
        dx_ref[...] = r * (dxh - xh * jnp.mean(dxh * xh, axis=-1, keepdims=True))
        st_ref[0:1, :] += jnp.sum(dy * xh, axis=0, keepdims=True)

    return pl.pallas_call(
        body, name=name, grid=(t // ROWS,),
        in_specs=[_row_spec(d), _const_spec((1, d)), _row_spec(d)],
        out_specs=(_row_spec(d), _const_spec((8, d)), _const_spec((8, LANES))),
        out_shape=(jax.ShapeDtypeStruct((t, d), F32), jax.ShapeDtypeStruct((8, d), F32),
                   jax.ShapeDtypeStruct((8, LANES), F32)),
        compiler_params=_params("arbitrary"),
    )(x, g, target)


FFN_BLOCK = D_FF // 2


def _ffn_gu_fwd(h, wg, wu, *, name):
    t, d = h.shape
    tn = FFN_BLOCK

    def body(h_ref, wg_ref, wu_ref, s_ref, a_ref, b_ref):
        hv = h_ref[...]
        a = _dotb(hv, wg_ref[...], NN)
        b = _dotb(hv, wu_ref[...], NN)
        s_ref[...] = (a * _sigmoid(a) * b).astype(s_ref.dtype)
        a_ref[...] = a.astype(a_ref.dtype)
        b_ref[...] = b.astype(b_ref.dtype)

    w_spec = pl.BlockSpec((d, tn), lambda j, i: (0, j))
    o_spec = pl.BlockSpec((ROWS, tn), lambda j, i: (i, j))
    return pl.pallas_call(
        body, name=name, grid=(D_FF // tn, t // ROWS),
        in_specs=[pl.BlockSpec((ROWS, d), lambda j, i: (i, 0)), w_spec, w_spec],
        out_specs=(o_spec, o_spec, o_spec),
        out_shape=(jax.ShapeDtypeStruct((t, D_FF), BF16),) * 3,
        compiler_params=_params("parallel", "parallel"),
    )(h, wg, wu)


def _ffn_down_dx(dy, w_down, a, b, *, name):
    t, d = dy.shape
    tn = FFN_BLOCK

    def body(dy_ref, w_ref, a_ref, b_ref, da_ref, db_ref):
        ds = _dotb(dy_ref[...], w_ref[...], NT)
        av = a_ref[...].astype(F32)
        sg = _sigmoid(av)
        da_ref[...] = (ds * b_ref[...].astype(F32) * sg * (1.0 + av * (1.0 - sg))).astype(da_ref.dtype)
        db_ref[...] = (ds * av * sg).astype(db_ref.dtype)

    o_spec = pl.BlockSpec((ROWS, tn), lambda j, i: (i, j))
    return pl.pallas_call(
        body, name=name, grid=(D_FF // tn, t // ROWS),
        in_specs=[pl.BlockSpec((ROWS, d), lambda j, i: (i, 0)), pl.BlockSpec((tn, d), lambda j, i: (j, 0)),
                  o_spec, o_spec],
        out_specs=(o_spec, o_spec),
        out_shape=(jax.ShapeDtypeStruct((t, D_FF), BF16),) * 2,
        compiler_params=_params("parallel", "parallel"),
    )(dy, w_down, a, b)


def _shift_rows(v, s, rows):
    if s == 0:
        return v
    return jnp.where(rows >= s, pltpu.roll(v, s, 0), 0.0)


def _unshift_rows(v, s, rows, t):
    if s == 0:
        return v
    return jnp.where(rows < t - s, pltpu.roll(v, t - s, 0), 0.0)


def _conv_silu(x, w, rows):
    z = w[GDN_CONV - 1:GDN_CONV, :] * x
    for j in range(GDN_CONV - 1):
        z = z + w[j:j + 1, :] * _shift_rows(x, GDN_CONV - 1 - j, rows)
    sg = _sigmoid(z)
    return z, sg, z * sg


def _gdn_prep_fwd(proj, conv_wt, *, name):
    t = proj.shape[0]
    nh = GDN_HEADS

    def body(x_ref, w_ref, y_ref):
        j = pl.program_id(0)
        rows = lax.broadcasted_iota(jnp.int32, (t, LANES), 0)
        _, _, s = _conv_silu(x_ref[...], w_ref[...], rows)
        rs = lax.rsqrt(jnp.sum(s * s, axis=-1, keepdims=True) + EPS)
        qscale = jnp.where(j < nh, GDN_HEAD_DIM ** -0.5, 1.0)
        y_ref[...] = jnp.where(j < 2 * nh, s * rs * qscale, s)

    return pl.pallas_call(
        body, name=name, grid=(3 * nh,),
        in_specs=[pl.BlockSpec((t, LANES), lambda j: (0, j)), pl.BlockSpec((GDN_CONV, LANES), lambda j: (0, j))],
        out_specs=pl.BlockSpec((t, LANES), lambda j: (0, j)),
        out_shape=jax.ShapeDtypeStruct((t, 3 * GDN_KEY_DIM), F32),
        compiler_params=_params("parallel"),
    )(proj, conv_wt)


def _gdn_prep_bwd(proj, conv_wt, dy, *, name):
    t = proj.shape[0]
    nh = GDN_HEADS

    def body(x_ref, w_ref, dy_ref, dx_ref, dw_ref):
        j = pl.program_id(0)
        rows = lax.broadcasted_iota(jnp.int32, (t, LANES), 0)
        x = x_ref[...]
        w = w_ref[...]
        z, sg, s = _conv_silu(x, w, rows)
        rs = lax.rsqrt(jnp.sum(s * s, axis=-1, keepdims=True) + EPS)
        qscale = jnp.where(j < nh, GDN_HEAD_DIM ** -0.5, 1.0)
        dyv = dy_ref[...]
        nv = s * rs
        de = dyv * qscale
        ds_qk = rs * (de - nv * jnp.sum(de * nv, axis=-1, keepdims=True))
        ds = jnp.where(j < 2 * nh, ds_qk, dyv)
        dz = ds * sg * (1.0 + z * (1.0 - sg))
        dx = w[GDN_CONV - 1:GDN_CONV, :] * dz
        dw_ref[GDN_CONV - 1:GDN_CONV, :] = jnp.sum(dz * x, axis=0, keepdims=True)
        for k in range(GDN_CONV - 1):
            sh = GDN_CONV - 1 - k
            dx = dx + w[k:k + 1, :] * _unshift_rows(dz, sh, rows, t)
            dw_ref[k:k + 1, :] = jnp.sum(dz * _shift_rows(x, sh, rows), axis=0, keepdims=True)
        dx_ref[...] = dx.astype(dx_ref.dtype)

    return pl.pallas_call(
        body, name=name, grid=(3 * nh,),
        in_specs=[pl.BlockSpec((t, LANES), lambda j: (0, j)), pl.BlockSpec((GDN_CONV, LANES), lambda j: (0, j)),
                  pl.BlockSpec((None, t, LANES), lambda j: (j // nh, 0, j % nh))],
        out_specs=(pl.BlockSpec((t, LANES), lambda j: (0, j)), pl.BlockSpec((GDN_CONV, LANES), lambda j: (0, j))),
        out_shape=(jax.ShapeDtypeStruct((t, 3 * GDN_KEY_DIM), BF16),
                   jax.ShapeDtypeStruct((GDN_CONV, 3 * GDN_KEY_DIM), F32)),
        compiler_params=_params("parallel"),
    )(proj, conv_wt, dy)


def _softplus(z):
    return jnp.maximum(z, 0.0) + jnp.log(1.0 + jnp.exp(-jnp.abs(z)))


def _gdn_gate_fwd(ab, prm, *, name):
    t = ab.shape[0]

    def body(ab_ref, p_ref, o_ref):
        v = ab_ref[...]
        lane = lax.broadcasted_iota(jnp.int32, v.shape, 1)
        g = -jnp.exp(p_ref[0:1, :]) * _softplus(v + p_ref[1:2, :])
        o_ref[...] = jnp.where(lane < GDN_HEADS, g, jnp.where(lane < 2 * GDN_HEADS, _sigmoid(v), 0.0))

    return pl.pallas_call(
        body, name=name, grid=(t // ROWS,),
        in_specs=[_row_spec(LANES), _const_spec((8, LANES))], out_specs=_row_spec(LANES),
        out_shape=jax.ShapeDtypeStruct((t, LANES), F32), compiler_params=_params("parallel"),
    )(ab, prm)


def _gdn_gate_bwd(ab, prm, dgb, *, name):
    t = ab.shape[0]

    def body(ab_ref, p_ref, d_ref, o_ref, st_ref):
        @pl.when(pl.program_id(0) == 0)
        def _():
            st_ref[...] = jnp.zeros_like(st_ref)

        v = ab_ref[...]
        dv = d_ref[...]
        lane = lax.broadcasted_iota(jnp.int32, v.shape, 1)
        is_a = lane < GDN_HEADS
        is_b = jnp.logical_and(lane >= GDN_HEADS, lane < 2 * GDN_HEADS)
        a_exp = jnp.exp(p_ref[0:1, :])
        zz = v + p_ref[1:2, :]
        g = -a_exp * _softplus(zz)
        da = dv * (-a_exp) * _sigmoid(zz)
        beta = _sigmoid(v)
        db = dv * beta * (1.0 - beta)
        o_ref[...] = jnp.where(is_a, da, jnp.where(is_b, db, 0.0)).astype(o_ref.dtype)
        st_ref[0:1, :] += jnp.sum(jnp.where(is_a, dv * g, 0.0), axis=0, keepdims=True)
        st_ref[1:2, :] += jnp.sum(jnp.where(is_a, da, 0.0), axis=0, keepdims=True)

    return pl.pallas_call(
        body, name=name, grid=(t // ROWS,),
        in_specs=[_row_spec(LANES), _const_spec((8, LANES)), _row_spec(LANES)],
        out_specs=(_row_spec(LANES), _const_spec((8, LANES))),
        out_shape=(jax.ShapeDtypeStruct((t, LANES), BF16), jax.ShapeDtypeStruct((8, LANES), F32)),
        compiler_params=_params("arbitrary"),
    )(ab, prm, dgb)


def _gdn_local(qs, ks, vs, gbs, bbs):
    nh = len(qs)
    cs = qs[0].shape[0]
    hs = range(nh)
    r = lax.broadcasted_iota(jnp.int32, (cs, cs), 0)
    c = lax.broadcasted_iota(jnp.int32, (cs, cs), 1)
    tril, strict, eye = r >= c, r > c, r == c
    ident = jnp.where(eye, 1.0, 0.0)
    g_colb = [gbs[h][:, :cs] for h in hs]
    g_row = [jnp.sum(jnp.where(eye, g_colb[h], 0.0), axis=0, keepdims=True) for h in hs]
    gc_col = [jnp.sum(jnp.where(tril, g_row[h], 0.0), axis=1, keepdims=True) for h in hs]
    gc_row = [jnp.sum(jnp.where(r <= c, g_colb[h], 0.0), axis=0, keepdims=True) for h in hs]
    decay = [jnp.exp(jnp.where(tril, gc_col[h] - gc_row[h], NEG)) for h in hs]
    gamma = [jnp.exp(gc_col[h]) for h in hs]
    gcl = [gc_col[h][cs - 1:cs, :] for h in hs]
    gl = [jnp.exp(gcl[h]) for h in hs]
    kdec = [jnp.exp(gcl[h] - gc_col[h]) for h in hs]
    kb = [ks[h] * bbs[h] for h in hs]
    kk = [_dotb(kb[h], ks[h], NT) for h in hs]
    qk = [_dotb(qs[h], ks[h], NT) for h in hs]
    lmat = [jnp.where(strict, kk[h] * decay[h], 0.0) for h in hs]
    pmat = [jnp.where(tril, qk[h] * decay[h], 0.0) for h in hs]
    xm = [-lmat[h] for h in hs]
    tinv = [ident + xm[h] for h in hs]
    for _ in range(int(math.log2(cs)) - 1):
        xm = [_dotf(xm[h], xm[h], NN) for h in hs]
        tinv = [tinv[h] + _dotf(tinv[h], xm[h], NN) for h in hs]
    vb = [vs[h] * bbs[h] for h in hs]
    kg = [kb[h] * gamma[h] for h in hs]
    u = [_dotf(tinv[h], vb[h], NN) for h in hs]
    w = [_dotf(tinv[h], kg[h], NN) for h in hs]
    return [dict(tril=tril, strict=strict, eye=eye, r=r, c=c, decay=decay[h], gamma=gamma[h], gl=gl[h], kdec=kdec[h],
                 kb=kb[h], lmat=lmat[h], tinv=tinv[h], vb=vb[h], kg=kg[h], u=u[h], w=w[h], pmat=pmat[h],
                 qd=qs[h] * gamma[h], kd=ks[h] * kdec[h]) for h in hs]


def _gdn_chunk_fwd(qkv, gbc, bbc, *, name):
    t = qkv.shape[0]
    nh, cs, hd = GDN_HEADS, GDN_CHUNK, GDN_HEAD_DIM
    nc = t // cs

    hb = GDN_HEAD_BATCH
    ng = nh // hb

    def body(q_ref, k_ref, v_ref, g_ref, b_ref, o_ref, st_ref, s_ref):
        @pl.when(pl.program_id(1) == 0)
        def _():
            s_ref[...] = jnp.zeros_like(s_ref)

        sls = [slice(i * hd, (i + 1) * hd) for i in range(hb)]
        hs = range(hb)
        s = [s_ref[i] for i in hs]
        lo = _gdn_local([q_ref[:, sl] for sl in sls], [k_ref[:, sl] for sl in sls], [v_ref[:, sl] for sl in sls],
                        [g_ref[i] for i in hs], [b_ref[i] for i in hs])
        ws = [_dotb(lo[i]["w"], s[i], NN) for i in hs]
        qs = [_dotb(lo[i]["qd"], s[i], NN) for i in hs]
        vn = [lo[i]["u"] - ws[i] for i in hs]
        pv = [_dotb(lo[i]["pmat"], vn[i], NN) for i in hs]
        kv = [_dotb(lo[i]["kd"], vn[i], TN) for i in hs]
        for i, sl in enumerate(sls):
            st_ref[i, 0] = s[i]
            o_ref[:, sl] = qs[i] + pv[i]
            s_ref[i] = s[i] * lo[i]["gl"] + kv[i]

    gspec = pl.BlockSpec((hb, cs, LANES), lambda h, n: (h, n, 0))
    col = lambda off: pl.BlockSpec((cs, hb * hd), lambda h, n: (n, off + h))
    return pl.pallas_call(
        body, name=name, grid=(ng, nc),
        in_specs=[col(0), col(ng), col(2 * ng), gspec, gspec],
        out_specs=(col(0), pl.BlockSpec((hb, 1, hd, hd), lambda h, n: (h, n, 0, 0))),
        out_shape=(jax.ShapeDtypeStruct((t, nh * hd), F32), jax.ShapeDtypeStruct((nh, nc, hd, hd), F32)),
        scratch_shapes=[pltpu.VMEM((hb, hd, hd), F32)],
        compiler_params=_params("parallel", "arbitrary"),
    )(qkv, qkv, qkv, gbc, bbc)


def _gdn_chunk_bwd(qkv, gbc, bbc, states, do, *, name):
    t = qkv.shape[0]
    nh, cs, hd = GDN_HEADS, GDN_CHUNK, GDN_HEAD_DIM
    nc = t // cs

    hb = GDN_HEAD_BATCH
    ng = nh // hb

    def heads_bwd(q, k, v, gb, bb, s, dsn, dov):
        hs = range(len(q))
        lo = _gdn_local(q, k, v, gb, bb)
        tril, strict, eye, r, c = lo[0]["tril"], lo[0]["strict"], lo[0]["eye"], lo[0]["r"], lo[0]["c"]
        rowi = lax.broadcasted_iota(jnp.int32, (cs, 1), 0)
        get = lambda name: [lo[h][name] for h in hs]
        decay, gamma, gl, kdec = get("decay"), get("gamma"), get("gl"), get("kdec")
        kb, tinv, w, pmat, kd, qd = get("kb"), get("tinv"), get("w"), get("pmat"), get("kd"), get("qd")
        ws = [_dotb(w[h], s[h], NN) for h in hs]
        pdo = [_dotb(pmat[h], dov[h], TN) for h in hs]
        kds = [_dotb(kd[h], dsn[h], NN) for h in hs]
        dqd = [_dotb(dov[h], s[h], NT) for h in hs]
        qdo = [_dotb(qd[h], dov[h], TN) for h in hs]
        vn = [lo[h]["u"] - ws[h] for h in hs]
        dvn = [pdo[h] + kds[h] for h in hs]
        dp = [jnp.where(tril, _dotb(dov[h], vn[h], NT), 0.0) for h in hs]
        dkd = [_dotb(vn[h], dsn[h], NT) for h in hs]
        dw = [-_dotb(dvn[h], s[h], NT) for h in hs]
        wdv = [_dotb(w[h], dvn[h], TN) for h in hs]
        dvb = [_dotf(tinv[h], dvn[h], TN) for h in hs]
        dt1 = [_dotf(dvn[h], lo[h]["vb"], NT) for h in hs]
        dkg = [_dotf(tinv[h], dw[h], TN) for h in hs]
        dt2 = [_dotf(dw[h], lo[h]["kg"], NT) for h in hs]
        tdt = [_dotf(tinv[h], dt1[h] + dt2[h], TN) for h in hs]
        dl = [jnp.where(strict, -_dotf(tdt[h], tinv[h], NT), 0.0) for h in hs]
        dkk = [dl[h] * decay[h] for h in hs]
        dqk = [dp[h] * decay[h] for h in hs]
        dkb = [_dotb(dkk[h], k[h], NN) + dkg[h] * gamma[h] for h in hs]
        dk1 = [_dotb(dkk[h], kb[h], TN) for h in hs]
        dk2 = [_dotb(dqk[h], q[h], TN) for h in hs]
        dq1 = [_dotb(dqk[h], k[h], NN) for h in hs]
        out = []
        for h in hs:
            dgl = jnp.sum(jnp.sum(dsn[h] * s[h], axis=1, keepdims=True), axis=0, keepdims=True)
            ds_prev = gl[h] * dsn[h] + qdo[h] - wdv[h]
            dk = dk1[h] + dk2[h] + dkd[h] * kdec[h] + dkb[h] * bb[h]
            dq = dq1[h] + dqd[h] * gamma[h]
            dbeta = jnp.sum(dvb[h] * v[h], axis=-1, keepdims=True) + jnp.sum(dkb[h] * k[h], axis=-1, keepdims=True)
            e = dl[h] * lo[h]["lmat"] + dp[h] * pmat[h]
            e_col = jnp.sum(e, axis=0, keepdims=True)
            dgc = jnp.sum(e, axis=1, keepdims=True) - jnp.sum(jnp.where(eye, e_col, 0.0), axis=1, keepdims=True)
            dgamma = (jnp.sum(dqd[h] * q[h], axis=-1, keepdims=True)
                      + jnp.sum(dkg[h] * kb[h], axis=-1, keepdims=True))
            rk = jnp.sum(dkd[h] * k[h], axis=-1, keepdims=True) * kdec[h]
            dgcl = jnp.sum(rk, axis=0, keepdims=True) + dgl * gl[h]
            dgc = dgc + dgamma * gamma[h] - rk + jnp.where(rowi == cs - 1, dgcl, 0.0)
            dgc_row = jnp.sum(jnp.where(eye, dgc, 0.0), axis=0, keepdims=True)
            dg = jnp.sum(jnp.where(c >= r, dgc_row, 0.0), axis=1, keepdims=True)
            out.append((dq, dk, dvb[h] * bb[h], dbeta, dg, ds_prev))
        return out

    def body(q_ref, k_ref, v_ref, g_ref, b_ref, st_ref, do_ref, d_ref, dg_ref, db_ref, ds_ref):
        @pl.when(pl.program_id(1) == 0)
        def _():
            ds_ref[...] = jnp.zeros_like(ds_ref)

        sls = [slice(i * hd, (i + 1) * hd) for i in range(hb)]
        hs = range(hb)
        outs = heads_bwd([q_ref[:, sl] for sl in sls], [k_ref[:, sl] for sl in sls], [v_ref[:, sl] for sl in sls],
                         [g_ref[i] for i in hs], [b_ref[i] for i in hs], [st_ref[i, 0] for i in hs],
                         [ds_ref[i] for i in hs], [do_ref[:, sl] for sl in sls])
        for i, sl in enumerate(sls):
            dq, dk, dv, dbeta, dg, ds_prev = outs[i]
            d_ref[0, :, sl], d_ref[1, :, sl], d_ref[2, :, sl] = dq, dk, dv
            db_ref[i] = jnp.broadcast_to(dbeta, (cs, LANES))
            dg_ref[i] = jnp.broadcast_to(dg, (cs, LANES))
            ds_ref[i] = ds_prev

    gspec = pl.BlockSpec((hb, cs, LANES), lambda h, n: (h, nc - 1 - n, 0))
    col = lambda off: pl.BlockSpec((cs, hb * hd), lambda h, n: (nc - 1 - n, off + h))
    return pl.pallas_call(
        body, name=name, grid=(ng, nc),
        in_specs=[col(0), col(ng), col(2 * ng), gspec, gspec,
                  pl.BlockSpec((hb, 1, hd, hd), lambda h, n: (h, nc - 1 - n, 0, 0)), col(0)],
        out_specs=(pl.BlockSpec((3, cs, hb * hd), lambda h, n: (0, nc - 1 - n, h)), gspec, gspec),
        out_shape=(jax.ShapeDtypeStruct((3, t, nh * hd), F32),) + (jax.ShapeDtypeStruct((nh, t, LANES), F32),) * 2,
        scratch_shapes=[pltpu.VMEM((hb, hd, hd), F32)],
        compiler_params=_params("parallel", "arbitrary"),
    )(qkv, qkv, qkv, gbc, bbc, states, do)


def _gdn_onorm_fwd(o, proj, norm_g, *, name):
    t = o.shape[0]
    w = GDN_KEY_DIM
    goff = 3 * GDN_KEY_DIM // w

    def body(o_ref, gp_ref, g_ref, y_ref):
        gv = g_ref[...]
        for h in range(GDN_HEADS):
            sl = slice(h * GDN_HEAD_DIM, (h + 1) * GDN_HEAD_DIM)
            oh = o_ref[:, sl]
            gp = gp_ref[:, sl]
            r = lax.rsqrt(jnp.mean(oh * oh, axis=-1, keepdims=True) + EPS)
            y_ref[:, sl] = (oh * r * gv * gp * _sigmoid(gp)).astype(y_ref.dtype)

    return pl.pallas_call(
        body, name=name, grid=(t // ROWS,),
        in_specs=[_row_spec(w), pl.BlockSpec((ROWS, w), lambda i: (i, goff)), _const_spec((1, GDN_HEAD_DIM))],
        out_specs=_row_spec(w), out_shape=jax.ShapeDtypeStruct((t, w), BF16),
        compiler_params=_params("parallel"),
    )(o, proj, norm_g)


def _gdn_onorm_bwd(o, proj, norm_g, dy, *, name):
    t = o.shape[0]
    w = GDN_KEY_DIM
    goff = 3 * GDN_KEY_DIM // w

    def body(o_ref, gp_ref, g_ref, dy_ref, do_ref, dgp_ref, st_ref):
        @pl.when(pl.program_id(0) == 0)
        def _():
            st_ref[...] = jnp.zeros_like(st_ref)

        gv = g_ref[...]
        acc = jnp.zeros((1, GDN_HEAD_DIM), F32)
        for h in range(GDN_HEADS):
            sl = slice(h * GDN_HEAD_DIM, (h + 1) * GDN_HEAD_DIM)
            oh = o_ref[:, sl]
            gp = gp_ref[:, sl]
            dyv = dy_ref[:, sl].astype(F32)
            r = lax.rsqrt(jnp.mean(oh * oh, axis=-1, keepdims=True) + EPS)
            xh = oh * r
            sg = _sigmoid(gp)
            dn = dyv * gp * sg
            dgp_ref[:, sl] = (dyv * xh * gv * sg * (1.0 + gp * (1.0 - sg))).astype(dgp_ref.dtype)
            acc = acc + jnp.sum(dn * xh, axis=0, keepdims=True)
            dxh = dn * gv
            do_ref[:, sl] = r * (dxh - xh * jnp.mean(dxh * xh, axis=-1, keepdims=True))
        st_ref[0:1, :] += acc

    return pl.pallas_call(
        body, name=name, grid=(t // ROWS,),
        in_specs=[_row_spec(w), pl.BlockSpec((ROWS, w), lambda i: (i, goff)), _const_spec((1, GDN_HEAD_DIM)),
                  _row_spec(w)],
        out_specs=(_row_spec(w), _row_spec(w), _const_spec((8, GDN_HEAD_DIM))),
        out_shape=(jax.ShapeDtypeStruct((t, w), F32), jax.ShapeDtypeStruct((t, w), BF16),
                   jax.ShapeDtypeStruct((8, GDN_HEAD_DIM), F32)),
        compiler_params=_params("arbitrary"),
    )(o, proj, norm_g, dy)


def _mla_prep_fwd(proj, qg, kvg, *, name):
    t = proj.shape[0]
    q1, k1 = MLA_Q_RANK, MLA_Q_RANK + MLA_KV_RANK

    def body(p_ref, qg_ref, kg_ref, cq_ref, ck_ref):
        cq = p_ref[:, 0:q1]
        ck = p_ref[:, q1:k1]
        cq_ref[...] = (cq * lax.rsqrt(jnp.mean(cq * cq, axis=-1, keepdims=True) + EPS) * qg_ref[...]).astype(BF16)
        ck_ref[...] = (ck * lax.rsqrt(jnp.mean(ck * ck, axis=-1, keepdims=True) + EPS) * kg_ref[...]).astype(BF16)

    return pl.pallas_call(
        body, name=name, grid=(t // ROWS,),
        in_specs=[_row_spec(MLA_IN), _const_spec((1, MLA_Q_RANK)), _const_spec((1, MLA_KV_RANK))],
        out_specs=(_row_spec(MLA_Q_RANK), _row_spec(MLA_KV_RANK)),
        out_shape=(jax.ShapeDtypeStruct((t, MLA_Q_RANK), BF16), jax.ShapeDtypeStruct((t, MLA_KV_RANK), BF16)),
        compiler_params=_params("parallel"),
    )(proj, qg, kvg)


def _mla_prep_bwd(proj, qg, kvg, dcq, dck, dkr, *, name):
    t = proj.shape[0]
    q1, k1 = MLA_Q_RANK, MLA_Q_RANK + MLA_KV_RANK

    def body(p_ref, qg_ref, kg_ref, dq_ref, dk_ref, dr_ref, dp_ref, st_ref):
        @pl.when(pl.program_id(0) == 0)
        def _():
            st_ref[...] = jnp.zeros_like(st_ref)

        for lo, hi, g_ref, d_ref in ((0, q1, qg_ref, dq_ref), (q1, k1, kg_ref, dk_ref)):
            xv = p_ref[:, lo:hi]
            dn = d_ref[...]
            r = lax.rsqrt(jnp.mean(xv * xv, axis=-1, keepdims=True) + EPS)
            xh = xv * r
            dxh = dn * g_ref[...]
            dp_ref[:, lo:hi] = (r * (dxh - xh * jnp.mean(dxh * xh, axis=-1, keepdims=True))).astype(dp_ref.dtype)
            st_ref[0:1, lo:hi] += jnp.sum(dn * xh, axis=0, keepdims=True)
        dp_ref[:, k1:MLA_IN] = dr_ref[...].astype(dp_ref.dtype)

    return pl.pallas_call(
        body, name=name, grid=(t // ROWS,),
        in_specs=[_row_spec(MLA_IN), _const_spec((1, MLA_Q_RANK)), _const_spec((1, MLA_KV_RANK)),
                  _row_spec(MLA_Q_RANK), _row_spec(MLA_KV_RANK), _row_spec(MLA_ROPE)],
        out_specs=(_row_spec(MLA_IN), _const_spec((8, MLA_IN))),
        out_shape=(jax.ShapeDtypeStruct((t, MLA_IN), BF16), jax.ShapeDtypeStruct((8, MLA_IN), F32)),
        compiler_params=_params("arbitrary"),
    )(proj, qg, kvg, dcq, dck, dkr)


def _rope(xr, cos_t, sin_t, *, name):
    t, w = xr.shape
    ns = w // LANES

    def body(x_ref, c_ref, s_ref, o_ref):
        cv, sv = c_ref[...], s_ref[...]
        lane = lax.broadcasted_iota(jnp.int32, (ROWS, LANES), 1)
        first = (lane % MLA_ROPE) < (MLA_ROPE // 2)
        for i in range(ns):
            sl = slice(i * LANES, (i + 1) * LANES)
            xv = x_ref[:, sl]
            sw = jnp.where(first, pltpu.roll(xv, LANES - MLA_ROPE // 2, 1), pltpu.roll(xv, MLA_ROPE // 2, 1))
            o_ref[:, sl] = xv * cv + sw * sv

    return pl.pallas_call(
        body, name=name, grid=(t // ROWS,),
        in_specs=[_row_spec(w), _row_spec(LANES), _row_spec(LANES)], out_specs=_row_spec(w),
        out_shape=jax.ShapeDtypeStruct((t, w), F32), compiler_params=_params("parallel"),
    )(xr, cos_t, sin_t)


def _rope_bwd(dr, cos_t, sin_t, *, name):
    t, w = dr.shape
    ns = w // LANES

    def body(d_ref, c_ref, s_ref, o_ref):
        cv, sv = c_ref[...], s_ref[...]
        lane = lax.broadcasted_iota(jnp.int32, (ROWS, LANES), 1)
        first = (lane % MLA_ROPE) < (MLA_ROPE // 2)
        for i in range(ns):
            sl = slice(i * LANES, (i + 1) * LANES)
            dv = d_ref[:, sl]
            ds = dv * sv
            sw = jnp.where(first, pltpu.roll(ds, LANES - MLA_ROPE // 2, 1), pltpu.roll(ds, MLA_ROPE // 2, 1))
            o_ref[:, sl] = dv * cv + sw

    return pl.pallas_call(
        body, name=name, grid=(t // ROWS,),
        in_specs=[_row_spec(w), _row_spec(LANES), _row_spec(LANES)], out_specs=_row_spec(w),
        out_shape=jax.ShapeDtypeStruct((t, w), F32), compiler_params=_params("parallel"),
    )(dr, cos_t, sin_t)


ATT_BLOCK = 256
ATT_HEAD_BATCH = 4
ATT_HEAD_BATCH_BWD = 2
ATT_SCALE = MLA_QK ** -0.5


def _causal_mask(i, j, blk):
    rows = i * blk + lax.broadcasted_iota(jnp.int32, (blk, blk), 0)
    cols = j * blk + lax.broadcasted_iota(jnp.int32, (blk, blk), 1)
    return cols <= rows


def _attn_fwd(q, k, v, *, name):
    nh, t, dk = q.shape
    dv = v.shape[-1]
    blk = min(ATT_BLOCK, t)

    hb = ATT_HEAD_BATCH
    hs = range(hb)

    def body(q_ref, k_ref, v_ref, o_ref, l_ref):
        i = pl.program_id(1)
        qv = [q_ref[h] for h in hs]

        def step(j, carry):
            m, l, acc = carry[:hb], carry[hb:2 * hb], carry[2 * hb:]
            off = pl.multiple_of(j * blk, blk)
            mask = _causal_mask(i, j, blk)
            s = [_dotb(qv[h], k_ref[h, pl.ds(off, blk), :], NT) for h in hs]
            s = [jnp.where(mask, s[h] * ATT_SCALE, NEG) for h in hs]
            m_new = [jnp.maximum(m[h], jnp.max(s[h], axis=-1, keepdims=True)) for h in hs]
            p = [jnp.exp(s[h] - m_new[h]) for h in hs]
            pv = [_dotb(p[h], v_ref[h, pl.ds(off, blk), :], NN) for h in hs]
            alpha = [jnp.exp(m[h] - m_new[h]) for h in hs]
            l = [alpha[h] * l[h] + jnp.sum(p[h], axis=-1, keepdims=True) for h in hs]
            acc = [alpha[h] * acc[h] + pv[h] for h in hs]
            return tuple(m_new) + tuple(l) + tuple(acc)

        init = ((jnp.full((blk, 1), NEG, F32),) * hb + (jnp.zeros((blk, 1), F32),) * hb
                + (jnp.zeros((blk, dv), F32),) * hb)
        out = lax.fori_loop(0, i + 1, step, init)
        for h in hs:
            m, l, acc = out[h], out[hb + h], out[2 * hb + h]
            o_ref[h] = acc / l
            l_ref[h] = jnp.broadcast_to(m + jnp.log(l), (blk, LANES))

    return pl.pallas_call(
        body, name=name, grid=(nh // hb, t // blk),
        in_specs=[pl.BlockSpec((hb, blk, dk), lambda h, i: (h, i, 0)), pl.BlockSpec((hb, t, dk), lambda h, i: (h, 0, 0)),
                  pl.BlockSpec((hb, t, dv), lambda h, i: (h, 0, 0))],
        out_specs=(pl.BlockSpec((hb, blk, dv), lambda h, i: (h, i, 0)),
                   pl.BlockSpec((hb, blk, LANES), lambda h, i: (h, i, 0))),
        out_shape=(jax.ShapeDtypeStruct((nh, t, dv), F32), jax.ShapeDtypeStruct((nh, t, LANES), F32)),
        compiler_params=_params("parallel", "parallel"),
    )(q, k, v)


def _attn_bwd(q, k, v, o, lse, do, *, name):
    nh, t, dk = q.shape
    dv = v.shape[-1]
    blk = min(ATT_BLOCK, t)
    nb = t // blk

    hb = ATT_HEAD_BATCH_BWD
    hs = range(hb)

    def body(q_ref, k_ref, v_ref, o_ref, l_ref, do_ref, dq_ref, dk_ref, dv_ref):
        j = pl.program_id(1)

        @pl.when(j == 0)
        def _():
            dq_ref[...] = jnp.zeros_like(dq_ref)

        kv = [k_ref[h] for h in hs]
        vv = [v_ref[h] for h in hs]

        def step(i, carry):
            dk_acc, dv_acc = carry[:hb], carry[hb:]
            off = pl.multiple_of(i * blk, blk)
            rows = pl.ds(off, blk)
            mask = _causal_mask(i, j, blk)
            qv = [q_ref[h, rows, :] for h in hs]
            dov = [do_ref[h, rows, :] for h in hs]
            s = [_dotb(qv[h], kv[h], NT) for h in hs]
            dp = [_dotb(dov[h], vv[h], NT) for h in hs]
            p = [jnp.exp(jnp.where(mask, s[h] * ATT_SCALE, NEG) - l_ref[h, rows, :][:, 0:1]) for h in hs]
            delta = [jnp.sum(dov[h] * o_ref[h, rows, :], axis=-1, keepdims=True) for h in hs]
            ds = [p[h] * (dp[h] - delta[h]) * ATT_SCALE for h in hs]
            dvn = [_dotb(p[h], dov[h], TN) for h in hs]
            dkn = [_dotb(ds[h], qv[h], TN) for h in hs]
            dqn = [_dotb(ds[h], kv[h], NN) for h in hs]
            for h in hs:
                dq_ref[h, rows, :] += dqn[h]
            return tuple(dk_acc[h] + dkn[h] for h in hs) + tuple(dv_acc[h] + dvn[h] for h in hs)

        out = lax.fori_loop(j, nb, step, (jnp.zeros((blk, dk), F32),) * hb + (jnp.zeros((blk, dv), F32),) * hb)
        for h in hs:
            dk_ref[h] = out[h]
            dv_ref[h] = out[hb + h]

    full = lambda w: pl.BlockSpec((hb, t, w), lambda h, j: (h, 0, 0))
    part = lambda w: pl.BlockSpec((hb, blk, w), lambda h, j: (h, j, 0))
    return pl.pallas_call(
        body, name=name, grid=(nh // hb, nb),
        in_specs=[full(dk), part(dk), part(dv), full(dv), full(LANES), full(dv)],
        out_specs=(full(dk), part(dk), part(dv)),
        out_shape=(jax.ShapeDtypeStruct((nh, t, dk), F32), jax.ShapeDtypeStruct((nh, t, dk), F32),
                   jax.ShapeDtypeStruct((nh, t, dv), F32)),
        compiler_params=_params("parallel", "arbitrary"),
    )(q, k, v, o, lse, do)


def _ada_mod(c_all, ada_w, ada_b_cols, *, name):
    nl, d, wc = ada_w.shape

    def body(c_ref, w_ref, b_ref, o_ref):
        cv = c_ref[...]
        o_ref[0] = _dotb(cv * _sigmoid(cv), w_ref[0], NN) + b_ref[0]

    return pl.pallas_call(
        body, name=name, grid=(nl,),
        in_specs=[_const_spec((N_DEV, d)), pl.BlockSpec((1, d, wc), lambda l: (l, 0, 0)),
                  pl.BlockSpec((1, 1, wc), lambda l: (l, 0, 0))],
        out_specs=pl.BlockSpec((1, N_DEV, wc), lambda l: (l, 0, 0)),
        out_shape=jax.ShapeDtypeStruct((nl, N_DEV, wc), F32), compiler_params=_params("parallel"),
    )(c_all, ada_w, ada_b_cols)


def _adam_math(g, w, m, v):
    m2 = ADAM_B1 * m + (1.0 - ADAM_B1) * g
    v2 = ADAM_B2 * v + (1.0 - ADAM_B2) * (g * g)
    delta = -ADAM_LR * ((m2 / ADAM_BC1) / (jnp.sqrt(v2 / ADAM_BC2) + ADAM_EPS) + ADAM_WD * w)
    return delta, m2, v2


def _ada_grad_adamw(c_all, dmod_cols, w, m, v, *, name):
    nl, d, wc = w.shape
    tr = 256

    def body(c_ref, dm_ref, w_ref, m_ref, v_ref, g_ref, d_ref, m2_ref, v2_ref):
        cv = c_ref[...]
        g = _dotf(cv * _sigmoid(cv), dm_ref[0], TN)
        delta, m2, v2 = _adam_math(g, w_ref[0], m_ref[0], v_ref[0])
        g_ref[0], d_ref[0], m2_ref[0], v2_ref[0] = g, delta, m2, v2

    blk = pl.BlockSpec((1, tr, wc), lambda l, i: (l, i, 0))
    return pl.pallas_call(
        body, name=name, grid=(nl, d // tr),
        in_specs=[pl.BlockSpec((N_DEV, tr), lambda l, i: (0, i)), pl.BlockSpec((1, N_DEV, wc), lambda l, i: (l, 0, 0)),
                  blk, blk, blk],
        out_specs=(blk,) * 4, out_shape=(jax.ShapeDtypeStruct(w.shape, F32),) * 4,
        compiler_params=_params("parallel", "parallel"),
    )(c_all, dmod_cols, w, m, v)


def _adamw(parts, w, m, v, *, name):
    nl, r, c = w.shape
    ns = parts[0].shape[0]
    lanes_padded = -(-c // LANES) * LANES
    row_bytes = 2 * nl * ns * lanes_padded * parts[0].dtype.itemsize
    tr = _pick(r, min(256, max(16, (VMEM_LIMIT // 2) // row_bytes)), 16)

    def body(*refs):
        p_refs = refs[:nl]
        w_ref, m_ref, v_ref, g_ref, d_ref, m2_ref, v2_ref = refs[nl:]
        layer = pl.program_id(0)
        for q in range(nl):
            @pl.when(layer == q)
            def _(q=q):
                g = p_refs[q][0].astype(F32)
                for s in range(1, ns):
                    g = g + p_refs[q][s].astype(F32)
                delta, m2, v2 = _adam_math(g, w_ref[0], m_ref[0], v_ref[0])
                g_ref[0], d_ref[0], m2_ref[0], v2_ref[0] = g, delta, m2, v2

    blk = pl.BlockSpec((1, tr, c), lambda l, i: (l, i, 0))
    p_specs = [pl.BlockSpec((ns, tr, c), lambda l, i, q=q: (0, jnp.where(l == q, i, 0), 0)) for q in range(nl)]
    return pl.pallas_call(
        body, name=name, grid=(nl, r // tr),
        in_specs=p_specs + [blk, blk, blk],
        out_specs=(blk,) * 4, out_shape=(jax.ShapeDtypeStruct(w.shape, F32),) * 4,
        compiler_params=_params("arbitrary", "arbitrary"),
    )(*parts, w, m, v)


def _sum_parts(parts, *, name):
    ns, r, c = parts.shape

    def body(p_ref, o_ref):
        acc = p_ref[0]
        for s in range(1, ns):
            acc = acc + p_ref[s]
        o_ref[...] = acc

    return pl.pallas_call(
        body, name=name, out_shape=jax.ShapeDtypeStruct((r, c), F32),
        in_specs=[pl.BlockSpec(memory_space=pltpu.VMEM)], out_specs=pl.BlockSpec(memory_space=pltpu.VMEM),
    )(parts)


def _pack(arrs):
    flat = jnp.concatenate([a.reshape(-1).astype(F32) for a in arrs])
    pad = (-flat.shape[0]) % (8 * LANES)
    return jnp.pad(flat, (0, pad)).reshape(-1, LANES)


def _unpack(packed, shapes, lead=()):
    flat = packed.reshape(lead + (-1,))
    out, off = [], 0
    for s in shapes:
        n = math.prod(s)
        out.append(flat[..., off:off + n].reshape(lead + tuple(s)))
        off += n
    return out


def _gather_cols(g):
    _, nl, r, cs = g.shape
    return jnp.transpose(g, (1, 2, 0, 3)).reshape(nl, r, N_DEV * cs)


def _gather_rows(g):
    _, nl, rs, c = g.shape
    return jnp.transpose(g, (1, 0, 2, 3)).reshape(nl, N_DEV * rs, c)


def _scatter_cols(full):
    nl, r, c = full.shape
    return jnp.transpose(full.reshape(nl, r, N_DEV, c // N_DEV), (2, 0, 1, 3))


def _scatter_rows(full):
    nl, r, c = full.shape
    return jnp.transpose(full.reshape(nl, N_DEV, r // N_DEV, c), (1, 0, 2, 3))


def _row(v):
    return v.reshape(1, -1)


def _local_step(x, target, mod, cos_t, sin_t, rep, get_weights, put_grads):
    t = x.shape[0]
    saved = []
    for layer in range(DEPTH):
        j = layer // 2
        tag = f"l{layer}"
        shift_m, scale_m, gate_m, shift_f, scale_f, gate_f = [_row(mod[layer, i]) for i in range(N_MOD)]
        lw = dict(get_weights(layer, "mix", x))
        rec = {"x0": x, "lw": lw}
        h = _adaln_fwd(x, _row(rep["norm_mix_g"][layer]), scale_m, shift_m, name=f"adaln_mix_{tag}")
        rec["h"] = h
        if layer % 2 == 0:
            proj = _mm(h, lw["w_main"], mode="nn", out_dtype=F32, tm=256, tn=GDN_MAIN, name=f"gdn_in_{tag}")
            ab = _mm(h, lw["w_ab"], mode="nn", out_dtype=F32, name=f"gdn_in_ab_{tag}")
            qkv = _gdn_prep_fwd(proj, rep["gdn_conv_wt"][j], name=f"gdn_prep_{tag}")
            gbeta = _gdn_gate_fwd(ab, rep["gdn_gate_prm"][j], name=f"gdn_gate_{tag}")
            gbc = jnp.broadcast_to(jnp.transpose(gbeta[:, 0:GDN_HEADS])[:, :, None], (GDN_HEADS, t, LANES))
            bbc = jnp.broadcast_to(jnp.transpose(gbeta[:, GDN_HEADS:2 * GDN_HEADS])[:, :, None],
                                   (GDN_HEADS, t, LANES))
            o, states = _gdn_chunk_fwd(qkv, gbc, bbc, name=f"gdn_chunk_{tag}")
            og = _gdn_onorm_fwd(o, proj, _row(rep["gdn_norm_g"][j]), name=f"gdn_onorm_{tag}")
            x, y = _mm_resid(og, lw["w_out"], x, gate_m, name=f"gdn_out_{tag}")
            rec.update(proj=proj, ab=ab, qkv=qkv, gbc=gbc, bbc=bbc, states=states, o=o, og=og, y=y)
        else:
            proj = _mm(h, lw["w_in"], mode="nn", out_dtype=F32, name=f"mla_in_{tag}")
            cq, ck = _mla_prep_fwd(proj, _row(rep["mla_q_norm_g"][j]), _row(rep["mla_kv_norm_g"][j]),
                                   name=f"mla_prep_{tag}")
            qf = _mm(cq, lw["w_uq"], mode="nn", out_dtype=F32, name=f"mla_uq_{tag}")
            kvf = _mm(ck, lw["w_ukv"], mode="nn", out_dtype=F32, name=f"mla_ukv_{tag}")
            nrope = MLA_HEADS * MLA_ROPE
            krp = jnp.pad(proj[:, MLA_Q_RANK + MLA_KV_RANK:], ((0, 0), (0, LANES - MLA_ROPE)))
            roped = _rope(jnp.concatenate([qf[:, MLA_HEADS * MLA_NOPE:], krp], axis=1), cos_t, sin_t,
                          name=f"rope_{tag}")
            q_nope = qf[:, :MLA_HEADS * MLA_NOPE].reshape(t, MLA_HEADS, MLA_NOPE)
            q_rope = roped[:, :nrope].reshape(t, MLA_HEADS, MLA_ROPE)
            k_rope = jnp.broadcast_to(roped[:, None, nrope:nrope + MLA_ROPE], (t, MLA_HEADS, MLA_ROPE))
            kv3 = kvf.reshape(t, MLA_HEADS, MLA_NOPE + MLA_V)
            qc = jnp.transpose(jnp.concatenate([q_nope, q_rope], axis=-1), (1, 0, 2)).astype(BF16)
            kc = jnp.transpose(jnp.concatenate([kv3[..., :MLA_NOPE], k_rope], axis=-1), (1, 0, 2)).astype(BF16)
            vc = jnp.transpose(kv3[..., MLA_NOPE:], (1, 0, 2)).astype(BF16)
            oh, lse = _attn_fwd(qc, kc, vc, name=f"attn_{tag}")
            oc = jnp.transpose(oh, (1, 0, 2)).reshape(t, MLA_HEADS * MLA_V).astype(BF16)
            x, y = _mm_resid(oc, lw["w_out"], x, gate_m, name=f"mla_out_{tag}")
            rec.update(proj=proj, cq=cq, ck=ck, qc=qc, kc=kc, vc=vc, oh=oh, lse=lse, oc=oc, y=y)
        rec["x1"] = x
        lw.update(get_weights(layer, "ffn", x))
        h2 = _adaln_fwd(x, _row(rep["norm_ffn_g"][layer]), scale_f, shift_f, name=f"adaln_ffn_{tag}")
        s, a2, b2 = _ffn_gu_fwd(h2, lw["w_g"], lw["w_u"], name=f"ffn_gu_{tag}")
        x, y2 = _mm_resid(s, lw["w_down"], x, gate_f, name=f"ffn_down_{tag}")
        rec.update(h2=h2, a2=a2, b2=b2, s=s, y2=y2)
        saved.append(rec)

    dx, st, ls = _loss_head(x, _row(rep["final_norm_g"]), target, name="loss_head")
    loss = ls[0, 0]
    grads = {"final_norm_g": st[0]}
    per_layer = {k: [None] * DEPTH for k in ("norm_mix_g", "norm_ffn_g")}
    per_gdn = {k: [None] * 2 for k in ("gdn_conv_wt", "gdn_a_log", "gdn_dt_bias", "gdn_norm_g")}
    per_mla = {k: [None] * 2 for k in ("mla_q_norm_g", "mla_kv_norm_g")}
    dmod = [None] * DEPTH
    dep = jnp.zeros((8, LANES), F32)

    for layer in reversed(range(DEPTH)):
        j = layer // 2
        tag = f"l{layer}"
        rec = saved[layer]
        lw = rec["lw"]
        shift_m, scale_m, gate_m, shift_f, scale_f, gate_f = [_row(mod[layer, i]) for i in range(N_MOD)]
        dy2, st_g = _gate_bwd(dx, rec["y2"], gate_f, dep, name=f"gate_bwd_ffn_{tag}")
        dgate_f = st_g[0]
        dw_down = _mm(rec["s"], dy2, mode="tn", out_dtype=BF16, tm=256, tn=1024, name=f"ffn_down_dw_{tag}")
        da2, db2 = _ffn_down_dx(dy2, lw["w_down"], rec["a2"], rec["b2"], name=f"ffn_down_dx_{tag}")
        dw_g = _mm(rec["h2"], da2, mode="tn", out_dtype=BF16, tm=1024, tn=512, name=f"ffn_g_dw_{tag}")
        dw_u = _mm(rec["h2"], db2, mode="tn", out_dtype=BF16, tm=1024, tn=512, name=f"ffn_u_dw_{tag}")
        dep = put_grads(layer, "ffn", {"w_g": dw_g, "w_u": dw_u, "w_down": dw_down})
        dh2 = _mm(da2, lw["w_g"], mode="nt", out_dtype=F32, tm=256, tn=1024, name=f"ffn_g_dx_{tag}")
        dh2 = _mm(db2, lw["w_u"], mode="nt", out_dtype=BF16, add=dh2, tm=256, tn=1024, name=f"ffn_u_dx_{tag}")
        dx, st_n = _adaln_bwd(rec["x1"], _row(rep["norm_ffn_g"][layer]), scale_f, shift_f, dh2, dx, dep,
                              name=f"adaln_ffn_bwd_{tag}")
        per_layer["norm_ffn_g"][layer] = st_n[0]
        dscale_f, dshift_f = st_n[1], st_n[2]
        dy, st_g = _gate_bwd(dx, rec["y"], gate_m, dep, name=f"gate_bwd_mix_{tag}")
        dgate_m = st_g[0]
        big = {}
        if layer % 2 == 0:
            big["w_out"] = _mm(rec["og"], dy, mode="tn", out_dtype=BF16, name=f"gdn_out_dw_{tag}")
            dog = _mm(dy, lw["w_out"], mode="nt", out_dtype=BF16, name=f"gdn_out_dx_{tag}")
            do, dgp, st_o = _gdn_onorm_bwd(rec["o"], rec["proj"], _row(rep["gdn_norm_g"][j]), dog,
                                           name=f"gdn_onorm_bwd_{tag}")
            per_gdn["gdn_norm_g"][j] = st_o[0]
            dqkv, dgc_, dbc_ = _gdn_chunk_bwd(rec["qkv"], rec["gbc"], rec["bbc"], rec["states"], do,
                                               name=f"gdn_chunk_bwd_{tag}")
            dgb = jnp.concatenate([jnp.transpose(dgc_[:, :, 0]), jnp.transpose(dbc_[:, :, 0])], axis=1)
            dgb = jnp.pad(dgb, ((0, 0), (0, LANES - 2 * GDN_HEADS)))
            dab, st_a = _gdn_gate_bwd(rec["ab"], rep["gdn_gate_prm"][j], dgb, name=f"gdn_gate_bwd_{tag}")
            per_gdn["gdn_a_log"][j] = st_a[0, :GDN_HEADS]
            per_gdn["gdn_dt_bias"][j] = st_a[1, :GDN_HEADS]
            dpre, dcw = _gdn_prep_bwd(rec["proj"], rep["gdn_conv_wt"][j], dqkv, name=f"gdn_prep_bwd_{tag}")
            per_gdn["gdn_conv_wt"][j] = dcw
            dproj = jnp.concatenate([dpre, dgp], axis=1)
            dw_main = _mm(rec["h"], dproj, mode="tn", out_dtype=BF16, tm=1024, tn=512, name=f"gdn_in_dw_{tag}")
            dw_ab = _mm(rec["h"], dab, mode="tn", out_dtype=BF16, name=f"gdn_in_ab_dw_{tag}")
            big["w_in"] = jnp.concatenate([dw_main, dw_ab[:, :2 * GDN_HEADS]], axis=1)
            dep = put_grads(layer, "gdn", big)
            dh_ab = _mm(dab, lw["w_ab"], mode="nt", out_dtype=F32, name=f"gdn_in_ab_dx_{tag}")
            dh = _mm(dproj, lw["w_main"], mode="nt", out_dtype=BF16, add=dh_ab, tm=256, tn=1024,
                     name=f"gdn_in_dx_{tag}")
        else:
            big["w_out"] = _mm(rec["oc"], dy, mode="tn", out_dtype=BF16, name=f"mla_out_dw_{tag}")
            doc = _mm(dy, lw["w_out"], mode="nt", out_dtype=F32, name=f"mla_out_dx_{tag}")
            doh = jnp.transpose(doc.reshape(t, MLA_HEADS, MLA_V), (1, 0, 2))
            dqc, dkc, dvc = _attn_bwd(rec["qc"], rec["kc"], rec["vc"], rec["oh"], rec["lse"], doh,
                                      name=f"attn_bwd_{tag}")
            dqn = jnp.transpose(dqc[..., :MLA_NOPE], (1, 0, 2)).reshape(t, MLA_HEADS * MLA_NOPE)
            dqr = jnp.transpose(dqc[..., MLA_NOPE:], (1, 0, 2)).reshape(t, MLA_HEADS * MLA_ROPE)
            dkr = jnp.pad(jnp.sum(dkc[..., MLA_NOPE:], axis=0), ((0, 0), (0, LANES - MLA_ROPE)))
            drope = _rope_bwd(jnp.concatenate([dqr, dkr], axis=1), cos_t, sin_t, name=f"rope_bwd_{tag}")
            nrope = MLA_HEADS * MLA_ROPE
            dqf = jnp.concatenate([dqn, drope[:, :nrope]], axis=1).astype(BF16)
            dkvf = jnp.concatenate([jnp.transpose(dkc[..., :MLA_NOPE], (1, 0, 2)), jnp.transpose(dvc, (1, 0, 2))],
                                   axis=-1).reshape(t, MLA_HEADS * (MLA_NOPE + MLA_V)).astype(BF16)
            big["w_uq"] = _mm(rec["cq"], dqf, mode="tn", out_dtype=BF16, name=f"mla_uq_dw_{tag}")
            big["w_ukv"] = _mm(rec["ck"], dkvf, mode="tn", out_dtype=BF16, name=f"mla_ukv_dw_{tag}")
            dcq = _mm(dqf, lw["w_uq"], mode="nt", out_dtype=F32, name=f"mla_uq_dx_{tag}")
            dck = _mm(dkvf, lw["w_ukv"], mode="nt", out_dtype=F32, name=f"mla_ukv_dx_{tag}")
            dproj, st_p = _mla_prep_bwd(rec["proj"], _row(rep["mla_q_norm_g"][j]), _row(rep["mla_kv_norm_g"][j]),
                                        dcq, dck, drope[:, nrope:nrope + MLA_ROPE], name=f"mla_prep_bwd_{tag}")
            per_mla["mla_q_norm_g"][j] = st_p[0, :MLA_Q_RANK]
            per_mla["mla_kv_norm_g"][j] = st_p[0, MLA_Q_RANK:MLA_Q_RANK + MLA_KV_RANK]
            big["w_in"] = _mm(rec["h"], dproj, mode="tn", out_dtype=BF16, name=f"mla_in_dw_{tag}")
            dep = put_grads(layer, "mla", big)
            dh = _mm(dproj, lw["w_in"], mode="nt", out_dtype=BF16, name=f"mla_in_dx_{tag}")
        dx, st_n = _adaln_bwd(rec["x0"], _row(rep["norm_mix_g"][layer]), scale_m, shift_m, dh, dx, dep,
                              name=f"adaln_mix_bwd_{tag}")
        per_layer["norm_mix_g"][layer] = st_n[0]
        dmod[layer] = jnp.stack([st_n[2], st_n[1], dgate_m, dshift_f, dscale_f, dgate_f])

    for d in (per_layer, per_gdn, per_mla):
        for k, v in d.items():
            grads[k] = jnp.stack(v)
    return loss, dx, jnp.stack(dmod), grads


BIG = ("gdn_w_in", "gdn_w_out", "mla_w_in", "mla_w_uq", "mla_w_ukv", "mla_w_out", "ffn_w_gate", "ffn_w_up",
       "ffn_w_down")
COL_SHARDED = ("gdn_w_in", "mla_w_uq", "mla_w_ukv", "ffn_w_gate", "ffn_w_up")
SMALL = ("ada_b", "norm_mix_g", "norm_ffn_g", "gdn_conv_w", "gdn_a_log", "gdn_dt_bias", "gdn_norm_g",
         "mla_q_norm_g", "mla_kv_norm_g", "final_norm_g")
WEIGHTS = ("ada_w", "ada_b", "norm_mix_g", "norm_ffn_g", "gdn_w_in", "gdn_conv_w", "gdn_a_log", "gdn_dt_bias",
           "gdn_norm_g", "gdn_w_out", "mla_w_in", "mla_q_norm_g", "mla_kv_norm_g", "mla_w_uq", "mla_w_ukv",
           "mla_w_out", "ffn_w_gate", "ffn_w_up", "ffn_w_down", "final_norm_g")


def _uq_to_kernel_layout(w):
    lead = w.shape[:-1]
    w4 = w.reshape(lead + (MLA_HEADS, MLA_QK))
    return jnp.concatenate([w4[..., :MLA_NOPE].reshape(lead + (-1,)), w4[..., MLA_NOPE:].reshape(lead + (-1,))],
                           axis=-1)


def _uq_from_kernel_layout(w):
    lead = w.shape[:-1]
    nope = w[..., :MLA_HEADS * MLA_NOPE].reshape(lead + (MLA_HEADS, MLA_NOPE))
    rope = w[..., MLA_HEADS * MLA_NOPE:].reshape(lead + (MLA_HEADS, MLA_ROPE))
    return jnp.concatenate([nope, rope], axis=-1).reshape(lead + (-1,))


def _group_names(layer, kind):
    if kind == "ffn":
        return ("ffn_w_gate", "ffn_w_up", "ffn_w_down")
    return ("gdn_w_in", "gdn_w_out") if layer % 2 == 0 else ("mla_w_in", "mla_w_uq", "mla_w_ukv", "mla_w_out")


def _layer_index(name, layer):
    return layer if name.startswith("ffn") else layer // 2


def _cols(g):
    return jnp.transpose(g, (1, 0, 2)).reshape(g.shape[1], N_DEV * g.shape[2])


def _rows(g):
    return g.reshape(N_DEV * g.shape[1], g.shape[2])


def _uncols(full):
    r, c = full.shape
    return jnp.transpose(full.reshape(r, N_DEV, c // N_DEV), (1, 0, 2))


def _unrows(full):
    r, c = full.shape
    return full.reshape(N_DEV, r // N_DEV, c)


def _group_weights(layer, kind, got, zero):
    if kind == "ffn":
        return {"w_g": _cols(got["ffn_w_gate"]) + zero, "w_u": _cols(got["ffn_w_up"]),
                "w_down": _rows(got["ffn_w_down"])}
    if layer % 2 == 0:
        w_in = _cols(got["gdn_w_in"]) + zero
        return dict(w_main=w_in[:, :GDN_MAIN], w_ab=jnp.pad(w_in[:, GDN_MAIN:], ((0, 0), (0, LANES - 2 * GDN_HEADS))),
                    w_out=_rows(got["gdn_w_out"]))
    return dict(w_in=_rows(got["mla_w_in"]), w_uq=_uq_to_kernel_layout(_cols(got["mla_w_uq"])) + zero,
                w_ukv=_cols(got["mla_w_ukv"]), w_out=_rows(got["mla_w_out"]))


def _layer_grad_slots(kind, big):
    if kind == "ffn":
        return {"ffn_w_gate": _uncols(big["w_g"]), "ffn_w_up": _uncols(big["w_u"]),
                "ffn_w_down": _unrows(big["w_down"])}
    if kind == "gdn":
        return {"gdn_w_in": _uncols(big["w_in"]), "gdn_w_out": _unrows(big["w_out"])}
    return {"mla_w_in": _unrows(big["w_in"]), "mla_w_uq": _uncols(_uq_from_kernel_layout(big["w_uq"])),
            "mla_w_ukv": _uncols(big["w_ukv"]), "mla_w_out": _unrows(big["w_out"])}


def _small_weights(tiny, rep):
    prm = jnp.zeros((2, 8, LANES), F32)
    prm = prm.at[:, 0, :GDN_HEADS].set(rep["gdn_a_log"]).at[:, 1, :GDN_HEADS].set(rep["gdn_dt_bias"])
    out = {
        "gdn_conv_wt": jnp.transpose(_gather_rows(tiny["gdn_conv_w"]), (0, 2, 1)),
        "mla_q_norm_g": jnp.transpose(tiny["mla_q_norm_g"], (1, 0, 2)).reshape(2, MLA_Q_RANK),
        "mla_kv_norm_g": jnp.transpose(tiny["mla_kv_norm_g"], (1, 0, 2)).reshape(2, MLA_KV_RANK),
        "gdn_gate_prm": prm,
    }
    for k in ("norm_mix_g", "norm_ffn_g", "gdn_norm_g", "final_norm_g"):
        out[k] = rep[k]
    return out


def _rope_tables(positions):
    inv_freq = ROPE_THETA ** (-jnp.arange(0, MLA_ROPE, 2, dtype=F32) / MLA_ROPE)
    ang = positions.astype(F32)[:, None] * inv_freq
    cos, sin = jnp.cos(ang), jnp.sin(ang)
    reps = LANES // MLA_ROPE
    return jnp.tile(jnp.concatenate([cos, cos], axis=1), (1, reps)), jnp.tile(
        jnp.concatenate([-sin, sin], axis=1), (1, reps))


def kernel(x, c, positions, ada_w, ada_b, norm_mix_g, norm_ffn_g, gdn_w_in, gdn_conv_w, gdn_a_log, gdn_dt_bias, gdn_norm_g, gdn_w_out, mla_w_in, mla_q_norm_g, mla_kv_norm_g, mla_w_uq, mla_w_ukv, mla_w_out, ffn_w_gate, ffn_w_up, ffn_w_down, final_norm_g, loss_target, m_ada_w, m_ada_b, m_norm_mix_g, m_norm_ffn_g, m_gdn_w_in, m_gdn_conv_w, m_gdn_a_log, m_gdn_dt_bias, m_gdn_norm_g, m_gdn_w_out, m_mla_w_in, m_mla_q_norm_g, m_mla_kv_norm_g, m_mla_w_uq, m_mla_w_ukv, m_mla_w_out, m_ffn_w_gate, m_ffn_w_up, m_ffn_w_down, m_final_norm_g, v_ada_w, v_ada_b, v_norm_mix_g, v_norm_ffn_g, v_gdn_w_in, v_gdn_conv_w, v_gdn_a_log, v_gdn_dt_bias, v_gdn_norm_g, v_gdn_w_out, v_mla_w_in, v_mla_q_norm_g, v_mla_kv_norm_g, v_mla_w_uq, v_mla_w_ukv, v_mla_w_out, v_ffn_w_gate, v_ffn_w_up, v_ffn_w_down, v_final_norm_g):
    W = dict(ada_w=ada_w, ada_b=ada_b, norm_mix_g=norm_mix_g, norm_ffn_g=norm_ffn_g, gdn_w_in=gdn_w_in,
             gdn_conv_w=gdn_conv_w, gdn_a_log=gdn_a_log, gdn_dt_bias=gdn_dt_bias, gdn_norm_g=gdn_norm_g,
             gdn_w_out=gdn_w_out, mla_w_in=mla_w_in, mla_q_norm_g=mla_q_norm_g, mla_kv_norm_g=mla_kv_norm_g,
             mla_w_uq=mla_w_uq, mla_w_ukv=mla_w_ukv, mla_w_out=mla_w_out, ffn_w_gate=ffn_w_gate,
             ffn_w_up=ffn_w_up, ffn_w_down=ffn_w_down, final_norm_g=final_norm_g)
    M = dict(ada_w=m_ada_w, ada_b=m_ada_b, norm_mix_g=m_norm_mix_g, norm_ffn_g=m_norm_ffn_g, gdn_w_in=m_gdn_w_in,
             gdn_conv_w=m_gdn_conv_w, gdn_a_log=m_gdn_a_log, gdn_dt_bias=m_gdn_dt_bias, gdn_norm_g=m_gdn_norm_g,
             gdn_w_out=m_gdn_w_out, mla_w_in=m_mla_w_in, mla_q_norm_g=m_mla_q_norm_g,
             mla_kv_norm_g=m_mla_kv_norm_g, mla_w_uq=m_mla_w_uq, mla_w_ukv=m_mla_w_ukv, mla_w_out=m_mla_w_out,
             ffn_w_gate=m_ffn_w_gate, ffn_w_up=m_ffn_w_up, ffn_w_down=m_ffn_w_down, final_norm_g=m_final_norm_g)
    V = dict(ada_w=v_ada_w, ada_b=v_ada_b, norm_mix_g=v_norm_mix_g, norm_ffn_g=v_norm_ffn_g, gdn_w_in=v_gdn_w_in,
             gdn_conv_w=v_gdn_conv_w, gdn_a_log=v_gdn_a_log, gdn_dt_bias=v_gdn_dt_bias, gdn_norm_g=v_gdn_norm_g,
             gdn_w_out=v_gdn_w_out, mla_w_in=v_mla_w_in, mla_q_norm_g=v_mla_q_norm_g,
             mla_kv_norm_g=v_mla_kv_norm_g, mla_w_uq=v_mla_w_uq, mla_w_ukv=v_mla_w_ukv, mla_w_out=v_mla_w_out,
             ffn_w_gate=v_ffn_w_gate, ffn_w_up=v_ffn_w_up, ffn_w_down=v_ffn_w_down, final_norm_g=v_final_norm_g)
    me = 4 * lax.axis_index("x") + 2 * lax.axis_index("y") + lax.axis_index("c")
    t = x.shape[1]
    wc = ada_w.shape[-1]

    tiny_shapes = [c.shape, gdn_conv_w.shape, mla_q_norm_g.shape, mla_kv_norm_g.shape]
    (tiny_g,) = _exchange([_pack([c, gdn_conv_w, mla_q_norm_g, mla_kv_norm_g])], scatter=False, name="gather_tiny")
    c_g, conv_g, qn_g, kvn_g = _unpack(tiny_g, tiny_shapes, lead=(N_DEV,))
    c_all = c_g.reshape(N_DEV, D_MODEL)
    rep = _small_weights({"gdn_conv_w": conv_g, "mla_q_norm_g": qn_g, "mla_kv_norm_g": kvn_g}, W)

    groups = [(layer, kind) for layer in range(DEPTH) for kind in ("mix", "ffn")]

    def start_group(i, dep):
        layer, kind = groups[i]
        srcs = [W[k][_layer_index(k, layer)].astype(BF16) for k in _group_names(layer, kind)]
        return _exchange_start(srcs, scatter=False, name=f"gather_start_{kind}_l{layer}", dep=dep)

    gather = {0: start_group(0, tiny_g)}

    b_cols = lax.dynamic_slice_in_dim(ada_b, me * wc, wc, axis=1).reshape(DEPTH, 1, wc)
    mod_part = _ada_mod(c_all, ada_w, b_cols, name="ada_mod")
    (mod_g,) = _exchange([mod_part], scatter=False, name="gather_mod")
    mod_mine = lax.dynamic_index_in_dim(mod_g, me, axis=2, keepdims=False)
    mod = jnp.transpose(mod_mine, (1, 0, 2)).reshape(DEPTH, N_MOD, D_MODEL)

    def get_weights(layer, kind, after):
        i = groups.index((layer, kind))
        srcs, lands = _exchange_wait(gather[i], mod if i == 0 else after, scatter=False,
                                     name=f"gather_wait_{kind}_l{layer}")
        zero = jnp.zeros((), BF16)
        if i + 1 < len(groups):
            gather[i + 1] = start_group(i + 1, lands[0])
            zero = gather[i + 1][4][0, 0].astype(BF16)
        got = {k: lax.dynamic_update_index_in_dim(z, s, me, 0)
               for k, s, z in zip(_group_names(layer, kind), srcs, lands)}
        return _group_weights(layer, kind, got, zero)

    scatter = []

    def put_grads(layer, kind, big):
        slots = _layer_grad_slots(kind, big)
        started = _exchange_start(list(slots.values()), scatter=True, name=f"scatter_start_{kind}_l{layer}")
        scatter.append((layer, kind, list(slots.keys()), started))
        return started[4]

    cos_t, sin_t = _rope_tables(positions[0])
    loss, dx, dmod, g = _local_step(x[0], loss_target[0], mod, cos_t, sin_t, rep, get_weights, put_grads)

    parts = {k: [None] * W[k].shape[0] for k in BIG}
    res = {}

    def wait_group(entry, after):
        layer, kind, names, started = entry
        srcs, lands = _exchange_wait(started, after, scatter=True, name=f"scatter_wait_{kind}_l{layer}")
        for k, s, z in zip(names, srcs, lands):
            own = lax.dynamic_index_in_dim(s, me, 0, keepdims=False)
            parts[k][_layer_index(k, layer)] = lax.dynamic_update_index_in_dim(z, own, me, 0)

    for entry in scatter[:-1]:
        wait_group(entry, dx)
    early = [k for k in BIG if k not in scatter[-1][2]]
    for k in early:
        res[k] = _adamw(parts[k], W[k], M[k], V[k], name=f"adamw_{k}")
    loss, dmod, done = lax.optimization_barrier((loss, dmod, [res[k] for k in early]))
    for k, r in zip(early, done):
        res[k] = r

    small_local = [dmod.reshape(DEPTH, N_MOD * D_MODEL), g["norm_mix_g"], g["norm_ffn_g"],
                   jnp.transpose(g["gdn_conv_wt"], (0, 2, 1)), g["gdn_a_log"], g["gdn_dt_bias"], g["gdn_norm_g"],
                   g["mla_q_norm_g"], g["mla_kv_norm_g"], g["final_norm_g"], loss.reshape(1)]
    small_shapes = [a.shape for a in small_local]
    (small_g,) = _exchange([_pack(small_local)], scatter=False, name="gather_small_grads")
    small_sum = _unpack(_sum_parts(small_g, name="sum_small_grads"), small_shapes)
    loss = small_sum[-1][0]
    dmod_all = _unpack(small_g, small_shapes[:1], lead=(N_DEV,))[0]
    sg = dict(zip(SMALL, small_sum))
    wait_group(scatter[-1], small_g)
    sg["gdn_conv_w"] = lax.dynamic_slice_in_dim(sg["gdn_conv_w"], me * gdn_conv_w.shape[1], gdn_conv_w.shape[1], 1)
    sg["mla_q_norm_g"] = lax.dynamic_slice_in_dim(sg["mla_q_norm_g"], me * mla_q_norm_g.shape[1],
                                                  mla_q_norm_g.shape[1], 1)
    sg["mla_kv_norm_g"] = lax.dynamic_slice_in_dim(sg["mla_kv_norm_g"], me * mla_kv_norm_g.shape[1],
                                                   mla_kv_norm_g.shape[1], 1)

    dmod_cols = jnp.transpose(lax.dynamic_slice_in_dim(dmod_all, me * wc, wc, axis=2), (1, 0, 2))
    res["ada_w"] = _ada_grad_adamw(c_all, dmod_cols, ada_w, m_ada_w, v_ada_w, name="ada_w_grad_adamw")
    for k in BIG:
        if k not in early:
            res[k] = _adamw(parts[k], W[k], M[k], V[k], name=f"adamw_{k}")
    shapes = [W[k].shape for k in SMALL]
    packed = [_pack([d[k] for k in SMALL]) for d in (sg, W, M, V)]
    outs = _adamw([packed[0][None]], packed[1][None], packed[2][None], packed[3][None], name="adamw_small")
    unpacked = [_unpack(o[0], shapes) for o in outs]
    for i, k in enumerate(SMALL):
        res[k] = tuple(u[i] for u in unpacked)

    return (loss, dx[None], *[res[k][0] for k in WEIGHTS], *[res[k][1] for k in WEIGHTS],
            *[res[k][2] for k in WEIGHTS], *[res[k][3] for k in WEIGHTS])
```

```python
import functools
import math

import jax
import jax.numpy as jnp
from jax import lax
from jax.experimental import pallas as pl
from jax.experimental.pallas import tpu as pltpu

F32 = jnp.float32
BF16 = jnp.bfloat16
MXU_DTYPE = jnp.bfloat16

N_DEV = 8
D_MODEL = 1024
DEPTH = 4
GDN_HEADS = 8
GDN_HEAD_DIM = 128
GDN_KEY_DIM = GDN_HEADS * GDN_HEAD_DIM
GDN_CHUNK = 64
GDN_HEAD_BATCH = 8
GDN_CONV = 4
GDN_PREP_HEADS = 2
GDN_MAIN = 4 * GDN_KEY_DIM
MLA_HEADS = 8
MLA_NOPE = 128
MLA_ROPE = 64
MLA_V = 128
MLA_Q_RANK = 384
MLA_KV_RANK = 256
MLA_IN = MLA_Q_RANK + MLA_KV_RANK + MLA_ROPE
MLA_QK = MLA_NOPE + MLA_ROPE
ROPE_THETA = 10000.0
D_FF = 2816
N_MOD = 6
EPS = 1e-6
LANES = 128
VMEM_LIMIT = 48 * 1024 * 1024

ADAM_LR = 0.001
ADAM_B1 = 0.9
ADAM_B2 = 0.999
ADAM_EPS = 1e-08
ADAM_WD = 0.01
ADAM_STEP = 10
ADAM_BC1 = 1.0 - ADAM_B1 ** ADAM_STEP
ADAM_BC2 = 1.0 - ADAM_B2 ** ADAM_STEP

NN = (((1,), (0,)), ((), ()))
NT = (((1,), (1,)), ((), ()))
TN = (((0,), (0,)), ((), ()))
NEG = -1e30


def _dotb(a, b, dims):
    return lax.dot_general(a.astype(MXU_DTYPE), b.astype(MXU_DTYPE), dims, preferred_element_type=F32)


def _split(a):
    hi = a.astype(BF16)
    return hi, (a - hi.astype(F32)).astype(BF16)


def _dotf(a, b, dims):
    ah, al = _split(a)
    bh, bl = _split(b)
    dot = lambda u, v: lax.dot_general(u, v, dims, preferred_element_type=F32)
    return dot(ah, bh) + (dot(ah, bl) + dot(al, bh))


def _params(*sem):
    return pltpu.CompilerParams(dimension_semantics=sem, vmem_limit_bytes=VMEM_LIMIT)


def _pick(n, pref, mult=LANES):
    best = None
    t = mult
    while t <= min(n, pref):
        if n % t == 0:
            best = t
        t += mult
    return best if best is not None else n


def _sigmoid(z):
    return 1.0 / (1.0 + jnp.exp(-z))


def _exchange(arrays, *, scatter, name):
    n = len(arrays)
    out_shape = tuple(
        jax.ShapeDtypeStruct(a.shape if scatter else (N_DEV,) + a.shape, a.dtype) for a in arrays)

    def body(*refs):
        ins, outs = refs[:n], refs[n:2 * n]
        send_sems, recv_sems, local_sems = refs[2 * n:]
        x, y, c = lax.axis_index("x"), lax.axis_index("y"), lax.axis_index("c")
        me = 4 * x + 2 * y + c
        copies = []
        for k in range(n):
            src_own = ins[k].at[me] if scatter else ins[k]
            own = pltpu.make_async_copy(src_own, outs[k].at[me], local_sems.at[k])
            own.start()
            copies.append(own)
        sends = []
        for p in range(1, N_DEV):
            px, py, pc = x ^ ((p >> 2) & 1), y ^ ((p >> 1) & 1), c ^ (p & 1)
            peer = 4 * px + 2 * py + pc
            for k in range(n):
                cp = pltpu.make_async_remote_copy(
                    src_ref=ins[k].at[peer] if scatter else ins[k],
                    dst_ref=outs[k].at[me],
                    send_sem=send_sems.at[k, p - 1],
                    recv_sem=recv_sems.at[k, p - 1],
                    device_id=(px, py, pc),
                    device_id_type=pl.DeviceIdType.MESH,
                )
                cp.start()
                sends.append((cp, k, peer, p))
        for cp, k, peer, p in sends:
            pltpu.make_async_remote_copy(
                src_ref=ins[k].at[peer] if scatter else ins[k],
                dst_ref=outs[k].at[peer],
                send_sem=send_sems.at[k, p - 1],
                recv_sem=recv_sems.at[k, p - 1],
                device_id=(x, y, c),
                device_id_type=pl.DeviceIdType.MESH,
            ).wait_recv()
        for cp, _, _, _ in sends:
            cp.wait_send()
        for own in copies:
            own.wait()

    any_spec = pl.BlockSpec(memory_space=pl.ANY)
    outs = pl.pallas_call(
        body,
        name=name,
        out_shape=out_shape,
        in_specs=[any_spec] * n,
        out_specs=tuple([any_spec] * n),
        scratch_shapes=[
            pltpu.SemaphoreType.DMA((n, N_DEV - 1)),
            pltpu.SemaphoreType.DMA((n, N_DEV - 1)),
            pltpu.SemaphoreType.DMA((n,)),
        ],
        compiler_params=pltpu.CompilerParams(has_side_effects=True),
    )(*arrays)
    return list(outs)


def _gather_two_level(arrays, *, name):
    n = len(arrays)
    out_shape = tuple(jax.ShapeDtypeStruct((N_DEV,) + a.shape, a.dtype) for a in arrays)

    def body(*refs):
        ins, outs = refs[:n], refs[n:2 * n]
        send_sems, recv_sems, local_sems = refs[2 * n:]
        x, y, c = lax.axis_index("x"), lax.axis_index("y"), lax.axis_index("c")
        me = 4 * x + 2 * y + c
        sibling = (x, y, 1 - c)
        chips = [(1 - x, y), (x, 1 - y), (1 - x, 1 - y)]

        def slot(px, py, pc):
            return 4 * px + 2 * py + pc

        def copy(k, q, block, to, src=None):
            return pltpu.make_async_remote_copy(
                src_ref=outs[k].at[slot(*block)] if src is None else src,
                dst_ref=outs[k].at[slot(*block)],
                send_sem=send_sems.at[k, q], recv_sem=recv_sems.at[k, q],
                device_id=to, device_id_type=pl.DeviceIdType.MESH)

        own = [pltpu.make_async_copy(ins[k], outs[k].at[me], local_sems.at[k]) for k in range(n)]
        for cp in own:
            cp.start()
        first = []
        for k in range(n):
            first.append(copy(k, 0, (x, y, c), sibling, src=ins[k]))
            first += [copy(k, 1 + j, (x, y, c), (*chip, c), src=ins[k]) for j, chip in enumerate(chips)]
        for cp in first:
            cp.start()
        passed = []
        for j, chip in enumerate(chips):
            for k in range(n):
                copy(k, 1 + j, (*chip, c), (x, y, c)).wait_recv()
                fwd = copy(k, 4 + j, (*chip, c), sibling)
                fwd.start()
                passed.append(fwd)
        for k in range(n):
            copy(k, 0, sibling, (x, y, c)).wait_recv()
            for j, chip in enumerate(chips):
                copy(k, 4 + j, (*chip, 1 - c), (x, y, c)).wait_recv()
        for cp in first + passed:
            cp.wait_send()
        for cp in own:
            cp.wait()

    any_spec = pl.BlockSpec(memory_space=pl.ANY)
    outs = pl.pallas_call(
        body, name=name, out_shape=out_shape, in_specs=[any_spec] * n, out_specs=tuple([any_spec] * n),
        scratch_shapes=[pltpu.SemaphoreType.DMA((n, N_DEV - 1)), pltpu.SemaphoreType.DMA((n, N_DEV - 1)),
                        pltpu.SemaphoreType.DMA((n,))],
        compiler_params=pltpu.CompilerParams(has_side_effects=True),
    )(*arrays)
    return list(outs)


def _peer(x, y, c, p):
    return x ^ ((p >> 2) & 1), y ^ ((p >> 1) & 1), c ^ (p & 1)


def _exchange_start(arrays, *, scatter, name, dep=None):
    n = len(arrays)
    deps = [] if dep is None else [dep]
    lands = [lax.empty(a.shape if scatter else (N_DEV,) + a.shape, a.dtype) for a in arrays]

    def body(*refs):
        ins, zones = refs[:n], refs[n:2 * n]
        send_sems, recv_sems = refs[2 * n + len(deps)], refs[2 * n + len(deps) + 1]
        token = refs[-1]
        x, y, c = lax.axis_index("x"), lax.axis_index("y"), lax.axis_index("c")
        me = 4 * x + 2 * y + c
        for p in range(1, N_DEV):
            px, py, pc = _peer(x, y, c, p)
            for k in range(n):
                pltpu.make_async_remote_copy(
                    src_ref=ins[k].at[4 * px + 2 * py + pc] if scatter else ins[k],
                    dst_ref=zones[k].at[me],
                    send_sem=send_sems.at[k * (N_DEV - 1) + p - 1],
                    recv_sem=recv_sems.at[k * (N_DEV - 1) + p - 1],
                    device_id=(px, py, pc),
                    device_id_type=pl.DeviceIdType.MESH,
                ).start()
        token[...] = jnp.zeros_like(token)

    hbm = pl.BlockSpec(memory_space=pltpu.HBM)
    sem = pl.BlockSpec(memory_space=pltpu.SEMAPHORE)
    outs = pl.pallas_call(
        body,
        name=name,
        out_shape=(pltpu.SemaphoreType.DMA((n * (N_DEV - 1),)), pltpu.SemaphoreType.DMA((n * (N_DEV - 1),)),
                   *[pltpu.HBM(a.shape, a.dtype) for a in arrays], *[pltpu.HBM(z.shape, z.dtype) for z in lands],
                   jax.ShapeDtypeStruct((8, LANES), F32)),
        in_specs=[hbm] * (2 * n) + [pl.BlockSpec(memory_space=pl.ANY)] * len(deps),
        out_specs=(sem, sem, *[hbm] * (2 * n), pl.BlockSpec(memory_space=pltpu.VMEM)),
        input_output_aliases={k: 2 + k for k in range(2 * n)},
        compiler_params=pltpu.CompilerParams(has_side_effects=pltpu.SideEffectType.DATAFLOW_SIDE_EFFECTING),
    )(*[pltpu.with_memory_space_constraint(a, pltpu.HBM) for a in arrays],
      *[pltpu.with_memory_space_constraint(z, pltpu.HBM) for z in lands], *deps)
    return outs[0], outs[1], list(outs[2:2 + n]), list(outs[2 + n:2 + 2 * n]), outs[-1]


def _exchange_wait(started, after, *, scatter, name):
    send_sems, recv_sems, srcs, lands, _ = started
    n = len(srcs)

    def body(*refs):
        ins, zones = refs[:n], refs[n:2 * n]
        s_sems, r_sems = refs[2 * n], refs[2 * n + 1]
        x, y, c = lax.axis_index("x"), lax.axis_index("y"), lax.axis_index("c")
        for p in range(1, N_DEV):
            px, py, pc = _peer(x, y, c, p)
            peer = 4 * px + 2 * py + pc
            for k in range(n):
                cp = pltpu.make_async_remote_copy(
                    src_ref=ins[k].at[peer] if scatter else ins[k],
                    dst_ref=zones[k].at[peer],
                    send_sem=s_sems.at[k * (N_DEV - 1) + p - 1],
                    recv_sem=r_sems.at[k * (N_DEV - 1) + p - 1],
                    device_id=(px, py, pc),
                    device_id_type=pl.DeviceIdType.MESH,
                )
                cp.wait_send()
                cp.wait_recv()

    hbm = pl.BlockSpec(memory_space=pltpu.HBM)
    sem = pl.BlockSpec(memory_space=pltpu.SEMAPHORE)
    outs = pl.pallas_call(
        body,
        name=name,
        out_shape=tuple(pltpu.HBM(a.shape, a.dtype) for a in srcs + lands),
        in_specs=[hbm] * (2 * n) + [sem, sem, pl.BlockSpec(memory_space=pl.ANY)],
        out_specs=tuple([hbm] * (2 * n)),
        input_output_aliases={k: k for k in range(2 * n)},
        compiler_params=pltpu.CompilerParams(has_side_effects=pltpu.SideEffectType.DATAFLOW_SIDE_EFFECTING),
    )(*srcs, *lands, send_sems, recv_sems, after)
    return list(outs[:n]), list(outs[n:])


def _mm(a, b, *, mode, out_dtype, name, add=None, tm=512, tn=512):
    if mode == "nn":
        (m, kd), (_, nd) = a.shape, b.shape
    elif mode == "nt":
        (m, kd), (nd, _) = a.shape, b.shape
    else:
        (kd, m), (_, nd) = a.shape, b.shape
    tm = _pick(m, tm, LANES if mode == "tn" else 16)
    tn = _pick(nd, tn)
    dims = {"nn": NN, "nt": NT, "tn": TN}[mode]
    ni, nj = m // tm, nd // tn
    a_bytes, b_bytes = a.size * a.dtype.itemsize, b.size * b.dtype.itemsize
    i_outer = a_bytes + ni * b_bytes <= b_bytes + nj * a_bytes
    ij = (lambda g0, g1: (g0, g1)) if i_outer else (lambda g0, g1: (g1, g0))
    a_spec = (pl.BlockSpec((kd, tm), lambda g0, g1: (0, ij(g0, g1)[0])) if mode == "tn"
              else pl.BlockSpec((tm, kd), lambda g0, g1: (ij(g0, g1)[0], 0)))
    b_spec = (pl.BlockSpec((tn, kd), lambda g0, g1: (ij(g0, g1)[1], 0)) if mode == "nt"
              else pl.BlockSpec((kd, tn), lambda g0, g1: (0, ij(g0, g1)[1])))
    o_spec = pl.BlockSpec((tm, tn), lambda g0, g1: ij(g0, g1))
    has_add = add is not None

    def body(*refs):
        a_ref, b_ref = refs[0], refs[1]
        o_ref = refs[-1]
        acc = _dotb(a_ref[...], b_ref[...], dims)
        if has_add:
            acc = acc + refs[2][...].astype(F32)
        o_ref[...] = acc.astype(o_ref.dtype)

    ins = [a, b] + ([add] if has_add else [])
    specs = [a_spec, b_spec] + ([o_spec] if has_add else [])
    return pl.pallas_call(
        body, name=name, grid=(ni, nj) if i_outer else (nj, ni), in_specs=specs, out_specs=o_spec,
        out_shape=jax.ShapeDtypeStruct((m, nd), out_dtype),
        compiler_params=_params("parallel", "parallel"),
    )(*ins)


def _mm_resid(a, b, x, gate, *, name, tm=256, tn=1024):
    m, kd = a.shape
    nd = b.shape[1]
    tm = _pick(m, tm, 16)
    tn = _pick(nd, tn)
    o_spec = pl.BlockSpec((tm, tn), lambda i, j: (i, j))

    def body(a_ref, b_ref, x_ref, g_ref, xo_ref, y_ref):
        y = _dotb(a_ref[...], b_ref[...], NN)
        y_ref[...] = y
        xo_ref[...] = x_ref[...] + g_ref[...] * y

    return pl.pallas_call(
        body, name=name, grid=(m // tm, nd // tn),
        in_specs=[pl.BlockSpec((tm, kd), lambda i, j: (i, 0)), pl.BlockSpec((kd, tn), lambda i, j: (0, j)),
                  o_spec, pl.BlockSpec((1, tn), lambda i, j: (0, j))],
        out_specs=(o_spec, o_spec),
        out_shape=(jax.ShapeDtypeStruct((m, nd), F32), jax.ShapeDtypeStruct((m, nd), F32)),
        compiler_params=_params("parallel", "parallel"),
    )(a, b, x, gate)


ROWS = 256


def _row_spec(width, rows=ROWS):
    return pl.BlockSpec((rows, width), lambda i: (i, 0))


def _const_spec(shape):
    return pl.BlockSpec(shape, lambda i: tuple(0 for _ in shape))


def _adaln_fwd(x, g, scale, shift, *, name):
    t, d = x.shape

    def body(x_ref, g_ref, sc_ref, sh_ref, h_ref):
        xv = x_ref[...]
        r = lax.rsqrt(jnp.mean(xv * xv, axis=-1, keepdims=True) + EPS)
        h_ref[...] = (xv * r * g_ref[...] * (1.0 + sc_ref[...]) + sh_ref[...]).astype(h_ref.dtype)

    return pl.pallas_call(
        body, name=name, grid=(t // ROWS,),
        in_specs=[_row_spec(d), _const_spec((1, d)), _const_spec((1, d)), _const_spec((1, d))],
        out_specs=_row_spec(d), out_shape=jax.ShapeDtypeStruct((t, d), BF16),
        compiler_params=_params("parallel"),
    )(x, g, scale, shift)


def _adaln_bwd(x, g, scale, shift, dh, dres, dep, *, name):
    t, d = x.shape

    def body(x_ref, g_ref, sc_ref, sh_ref, dh_ref, dr_ref, dep_ref, dx_ref, st_ref):
        @pl.when(pl.program_id(0) == 0)
        def _():
            st_ref[...] = jnp.zeros_like(st_ref)

        xv = x_ref[...]
        dhv = dh_ref[...].astype(F32)
        gv = g_ref[...]
        r = lax.rsqrt(jnp.mean(xv * xv, axis=-1, keepdims=True) + EPS)
        xh = xv * r
        nv = xh * gv
        dn = dhv * (1.0 + sc_ref[...])
        dxh = dn * gv
        dx_ref[...] = dr_ref[...] + r * (dxh - xh * jnp.mean(dxh * xh, axis=-1, keepdims=True))
        st_ref[0:1, :] += jnp.sum(dn * xh, axis=0, keepdims=True)
        st_ref[1:2, :] += jnp.sum(dhv * nv, axis=0, keepdims=True)
        st_ref[2:3, :] += jnp.sum(dhv, axis=0, keepdims=True)

    return pl.pallas_call(
        body, name=name, grid=(t // ROWS,),
        in_specs=[_row_spec(d), _const_spec((1, d)), _const_spec((1, d)), _const_spec((1, d)),
                  _row_spec(d), _row_spec(d), _const_spec((8, LANES))],
        out_specs=(_row_spec(d), _const_spec((8, d))),
        out_shape=(jax.ShapeDtypeStruct((t, d), F32), jax.ShapeDtypeStruct((8, d), F32)),
        compiler_params=_params("arbitrary"),
    )(x, g, scale, shift, dh, dres, dep)


def _gate_bwd(dxo, y, gate, dep, *, name):
    t, d = dxo.shape

    def body(dx_ref, y_ref, g_ref, dep_ref, dy_ref, st_ref):
        @pl.when(pl.program_id(0) == 0)
        def _():
            st_ref[...] = jnp.zeros_like(st_ref)

        dxv = dx_ref[...]
        dy_ref[...] = (dxv * g_ref[...]).astype(dy_ref.dtype)
        st_ref[0:1, :] += jnp.sum(dxv * y_ref[...], axis=0, keepdims=True)

    return pl.pallas_call(
        body, name=name, grid=(t // ROWS,),
        in_specs=[_row_spec(d), _row_spec(d), _const_spec((1, d)), _const_spec((8, LANES))],
        out_specs=(_row_spec(d), _const_spec((8, d))),
        out_shape=(jax.ShapeDtypeStruct((t, d), BF16), jax.ShapeDtypeStruct((8, d), F32)),
        compiler_params=_params("arbitrary"),
    )(dxo, y, gate, dep)


def _loss_head(x, g, target, *, name):
    t, d = x.shape

    def body(x_ref, g_ref, t_ref, dx_ref, st_ref, ls_ref):
        @pl.when(pl.program_id(0) == 0)
        def _():
            st_ref[...] = jnp.zeros_like(st_ref)
            ls_ref[...] = jnp.zeros_like(ls_ref)

        xv = x_ref[...]
        gv = g_ref[...]
        r = lax.rsqrt(jnp.mean(xv * xv, axis=-1, keepdims=True) + EPS)
        xh = xv * r
        err = xh * gv - t_ref[...]
        ls_ref[...] += 0.5 * jnp.sum(jnp.mean(err * err, axis=-1, keepdims=True))
        dy = err * (1.0 / d)
        dxh = dy * gv
        dx_ref[...] = r * (dxh - xh * jnp.mean(dxh * xh, axis=-1, keepdims=True))
        st_ref[0:1, :] += jnp.sum(dy * xh, axis=0, keepdims=True)

    return pl.pallas_call(
        body, name=name, grid=(t // ROWS,),
        in_specs=[_row_spec(d), _const_spec((1, d)), _row_spec(d)],
        out_specs=(_row_spec(d), _const_spec((8, d)), _const_spec((8, LANES))),
        out_shape=(jax.ShapeDtypeStruct((t, d), F32), jax.ShapeDtypeStruct((8, d), F32),
                   jax.ShapeDtypeStruct((8, LANES), F32)),
        compiler_params=_params("arbitrary"),
    )(x, g, target)


FFN_BLOCK = D_FF // 2


def _ffn_gu_fwd(h, wg, wu, *, name):
    t, d = h.shape
    tn = FFN_BLOCK

    def body(h_ref, wg_ref, wu_ref, s_ref, a_ref, b_ref):
        hv = h_ref[...]
        a = _dotb(hv, wg_ref[...], NN)
        b = _dotb(hv, wu_ref[...], NN)
        s_ref[...] = (a * _sigmoid(a) * b).astype(s_ref.dtype)
        a_ref[...] = a.astype(a_ref.dtype)
        b_ref[...] = b.astype(b_ref.dtype)

    w_spec = pl.BlockSpec((d, tn), lambda j, i: (0, j))
    o_spec = pl.BlockSpec((ROWS, tn), lambda j, i: (i, j))
    return pl.pallas_call(
        body, name=name, grid=(D_FF // tn, t // ROWS),
        in_specs=[pl.BlockSpec((ROWS, d), lambda j, i: (i, 0)), w_spec, w_spec],
        out_specs=(o_spec, o_spec, o_spec),
        out_shape=(jax.ShapeDtypeStruct((t, D_FF), BF16),) * 3,
        compiler_params=_params("parallel", "parallel"),
    )(h, wg, wu)


def _ffn_down_dx(dy, w_down, a, b, *, name):
    t, d = dy.shape
    tn = FFN_BLOCK

    def body(dy_ref, w_ref, a_ref, b_ref, da_ref, db_ref):
        ds = _dotb(dy_ref[...], w_ref[...], NT)
        av = a_ref[...].astype(F32)
        sg = _sigmoid(av)
        da_ref[...] = (ds * b_ref[...].astype(F32) * sg * (1.0 + av * (1.0 - sg))).astype(da_ref.dtype)
        db_ref[...] = (ds * av * sg).astype(db_ref.dtype)

    o_spec = pl.BlockSpec((ROWS, tn), lambda j, i: (i, j))
    return pl.pallas_call(
        body, name=name, grid=(D_FF // tn, t // ROWS),
        in_specs=[pl.BlockSpec((ROWS, d), lambda j, i: (i, 0)), pl.BlockSpec((tn, d), lambda j, i: (j, 0)),
                  o_spec, o_spec],
        out_specs=(o_spec, o_spec),
        out_shape=(jax.ShapeDtypeStruct((t, D_FF), BF16),) * 2,
        compiler_params=_params("parallel", "parallel"),
    )(dy, w_down, a, b)


def _shift_rows(v, s, rows):
    if s == 0:
        return v
    return jnp.where(rows >= s, pltpu.roll(v, s, 0), 0.0)


def _unshift_rows(v, s, rows, t):
    if s == 0:
        return v
    return jnp.where(rows < t - s, pltpu.roll(v, t - s, 0), 0.0)


def _conv_silu(x, w, rows):
    z = w[GDN_CONV - 1:GDN_CONV, :] * x
    for j in range(GDN_CONV - 1):
        z = z + w[j:j + 1, :] * _shift_rows(x, GDN_CONV - 1 - j, rows)
    sg = _sigmoid(z)
    return z, sg, z * sg


def _gdn_prep_fwd(proj, conv_wt, *, name):
    t = proj.shape[0]
    nh = GDN_HEADS

    hp = GDN_PREP_HEADS
    wd = hp * LANES

    def body(x_ref, w_ref, y_ref):
        j = pl.program_id(0) * hp
        rows = lax.broadcasted_iota(jnp.int32, (t, LANES), 0)
        qscale = jnp.where(j < nh, GDN_HEAD_DIM ** -0.5, 1.0)
        for i in range(hp):
            sl = slice(i * LANES, (i + 1) * LANES)
            _, _, s = _conv_silu(x_ref[:, sl], w_ref[:, sl], rows)
            rs = lax.rsqrt(jnp.sum(s * s, axis=-1, keepdims=True) + EPS)
            y_ref[:, sl] = jnp.where(j < 2 * nh, s * rs * qscale, s)

    return pl.pallas_call(
        body, name=name, grid=(3 * nh // hp,),
        in_specs=[pl.BlockSpec((t, wd), lambda j: (0, j)), pl.BlockSpec((GDN_CONV, wd), lambda j: (0, j))],
        out_specs=pl.BlockSpec((t, wd), lambda j: (0, j)),
        out_shape=jax.ShapeDtypeStruct((t, 3 * GDN_KEY_DIM), F32),
        compiler_params=_params("parallel"),
    )(proj, conv_wt)


def _gdn_prep_bwd(proj, conv_wt, dy, *, name):
    t = proj.shape[0]
    nh = GDN_HEADS

    hp = GDN_PREP_HEADS
    wd = hp * LANES
    per_seg = nh // hp

    def body(x_ref, w_ref, dy_ref, dx_ref, dw_ref):
        j = pl.program_id(0) * hp
        rows = lax.broadcasted_iota(jnp.int32, (t, LANES), 0)
        qscale = jnp.where(j < nh, GDN_HEAD_DIM ** -0.5, 1.0)
        for i in range(hp):
            sl = slice(i * LANES, (i + 1) * LANES)
            x = x_ref[:, sl]
            w = w_ref[:, sl]
            z, sg, s = _conv_silu(x, w, rows)
            rs = lax.rsqrt(jnp.sum(s * s, axis=-1, keepdims=True) + EPS)
            dyv = dy_ref[:, sl]
            nv = s * rs
            de = dyv * qscale
            ds_qk = rs * (de - nv * jnp.sum(de * nv, axis=-1, keepdims=True))
            ds = jnp.where(j < 2 * nh, ds_qk, dyv)
            dz = ds * sg * (1.0 + z * (1.0 - sg))
            dx = w[GDN_CONV - 1:GDN_CONV, :] * dz
            dw_ref[GDN_CONV - 1:GDN_CONV, sl] = jnp.sum(dz * x, axis=0, keepdims=True)
            for k in range(GDN_CONV - 1):
                sh = GDN_CONV - 1 - k
                dx = dx + w[k:k + 1, :] * _unshift_rows(dz, sh, rows, t)
                dw_ref[k:k + 1, sl] = jnp.sum(dz * _shift_rows(x, sh, rows), axis=0, keepdims=True)
            dx_ref[:, sl] = dx.astype(dx_ref.dtype)

    return pl.pallas_call(
        body, name=name, grid=(3 * nh // hp,),
        in_specs=[pl.BlockSpec((t, wd), lambda j: (0, j)), pl.BlockSpec((GDN_CONV, wd), lambda j: (0, j)),
                  pl.BlockSpec((None, t, wd), lambda j: (j // per_seg, 0, j % per_seg))],
        out_specs=(pl.BlockSpec((t, wd), lambda j: (0, j)), pl.BlockSpec((GDN_CONV, wd), lambda j: (0, j))),
        out_shape=(jax.ShapeDtypeStruct((t, 3 * GDN_KEY_DIM), BF16),
                   jax.ShapeDtypeStruct((GDN_CONV, 3 * GDN_KEY_DIM), F32)),
        compiler_params=_params("parallel"),
    )(proj, conv_wt, dy)


def _softplus(z):
    return jnp.maximum(z, 0.0) + jnp.log(1.0 + jnp.exp(-jnp.abs(z)))


def _gdn_gate_fwd(ab, prm, *, name):
    t = ab.shape[0]

    def body(ab_ref, p_ref, o_ref):
        v = ab_ref[...]
        lane = lax.broadcasted_iota(jnp.int32, v.shape, 1)
        g = -jnp.exp(p_ref[0:1, :]) * _softplus(v + p_ref[1:2, :])
        o_ref[...] = jnp.where(lane < GDN_HEADS, g, jnp.where(lane < 2 * GDN_HEADS, _sigmoid(v), 0.0))

    return pl.pallas_call(
        body, name=name, grid=(t // ROWS,),
        in_specs=[_row_spec(LANES), _const_spec((8, LANES))], out_specs=_row_spec(LANES),
        out_shape=jax.ShapeDtypeStruct((t, LANES), F32), compiler_params=_params("parallel"),
    )(ab, prm)


def _gdn_gate_bwd(ab, prm, dgb, *, name):
    t = ab.shape[0]

    def body(ab_ref, p_ref, d_ref, o_ref, st_ref):
        @pl.when(pl.program_id(0) == 0)
        def _():
            st_ref[...] = jnp.zeros_like(st_ref)

        v = ab_ref[...]
        dv = d_ref[...]
        lane = lax.broadcasted_iota(jnp.int32, v.shape, 1)
        is_a = lane < GDN_HEADS
        is_b = jnp.logical_and(lane >= GDN_HEADS, lane < 2 * GDN_HEADS)
        a_exp = jnp.exp(p_ref[0:1, :])
        zz = v + p_ref[1:2, :]
        g = -a_exp * _softplus(zz)
        da = dv * (-a_exp) * _sigmoid(zz)
        beta = _sigmoid(v)
        db = dv * beta * (1.0 - beta)
        o_ref[...] = jnp.where(is_a, da, jnp.where(is_b, db, 0.0)).astype(o_ref.dtype)
        st_ref[0:1, :] += jnp.sum(jnp.where(is_a, dv * g, 0.0), axis=0, keepdims=True)
        st_ref[1:2, :] += jnp.sum(jnp.where(is_a, da, 0.0), axis=0, keepdims=True)

    return pl.pallas_call(
        body, name=name, grid=(t // ROWS,),
        in_specs=[_row_spec(LANES), _const_spec((8, LANES)), _row_spec(LANES)],
        out_specs=(_row_spec(LANES), _const_spec((8, LANES))),
        out_shape=(jax.ShapeDtypeStruct((t, LANES), BF16), jax.ShapeDtypeStruct((8, LANES), F32)),
        compiler_params=_params("arbitrary"),
    )(ab, prm, dgb)


def _gdn_local(qs, ks, vs, gbs, bbs):
    nh = len(qs)
    cs = qs[0].shape[0]
    hs = range(nh)
    r = lax.broadcasted_iota(jnp.int32, (cs, cs), 0)
    c = lax.broadcasted_iota(jnp.int32, (cs, cs), 1)
    tril, strict, eye = r >= c, r > c, r == c
    ident = jnp.where(eye, 1.0, 0.0)
    g_colb = [gbs[h][:, :cs] for h in hs]
    g_row = [jnp.sum(jnp.where(eye, g_colb[h], 0.0), axis=0, keepdims=True) for h in hs]
    gc_col = [jnp.sum(jnp.where(tril, g_row[h], 0.0), axis=1, keepdims=True) for h in hs]
    gc_row = [jnp.sum(jnp.where(r <= c, g_colb[h], 0.0), axis=0, keepdims=True) for h in hs]
    decay = [jnp.exp(jnp.where(tril, gc_col[h] - gc_row[h], NEG)) for h in hs]
    gamma = [jnp.exp(gc_col[h]) for h in hs]
    gcl = [gc_col[h][cs - 1:cs, :] for h in hs]
    gl = [jnp.exp(gcl[h]) for h in hs]
    kdec = [jnp.exp(gcl[h] - gc_col[h]) for h in hs]
    kb = [ks[h] * bbs[h] for h in hs]
    kk = [_dotb(kb[h], ks[h], NT) for h in hs]
    qk = [_dotb(qs[h], ks[h], NT) for h in hs]
    lmat = [jnp.where(strict, kk[h] * decay[h], 0.0) for h in hs]
    pmat = [jnp.where(tril, qk[h] * decay[h], 0.0) for h in hs]
    xm = [-lmat[h] for h in hs]
    tinv = [ident + xm[h] for h in hs]
    for _ in range(int(math.log2(cs)) - 1):
        xm = [_dotf(xm[h], xm[h], NN) for h in hs]
        tinv = [tinv[h] + _dotf(tinv[h], xm[h], NN) for h in hs]
    vb = [vs[h] * bbs[h] for h in hs]
    kg = [kb[h] * gamma[h] for h in hs]
    u = [_dotf(tinv[h], vb[h], NN) for h in hs]
    w = [_dotf(tinv[h], kg[h], NN) for h in hs]
    return [dict(tril=tril, strict=strict, eye=eye, r=r, c=c, decay=decay[h], gamma=gamma[h], gl=gl[h], kdec=kdec[h],
                 kb=kb[h], lmat=lmat[h], tinv=tinv[h], vb=vb[h], kg=kg[h], u=u[h], w=w[h], pmat=pmat[h],
                 qd=qs[h] * gamma[h], kd=ks[h] * kdec[h]) for h in hs]


def _gdn_chunk_fwd(qkv, gbc, bbc, *, name):
    t = qkv.shape[0]
    nh, cs, hd = GDN_HEADS, GDN_CHUNK, GDN_HEAD_DIM
    nc = t // cs

    hb = GDN_HEAD_BATCH
    ng = nh // hb

    def body(q_ref, k_ref, v_ref, g_ref, b_ref, o_ref, st_ref, s_ref):
        @pl.when(pl.program_id(1) == 0)
        def _():
            s_ref[...] = jnp.zeros_like(s_ref)

        sls = [slice(i * hd, (i + 1) * hd) for i in range(hb)]
        hs = range(hb)
        s = [s_ref[i] for i in hs]
        lo = _gdn_local([q_ref[:, sl] for sl in sls], [k_ref[:, sl] for sl in sls], [v_ref[:, sl] for sl in sls],
                        [g_ref[i] for i in hs], [b_ref[i] for i in hs])
        ws = [_dotb(lo[i]["w"], s[i], NN) for i in hs]
        qs = [_dotb(lo[i]["qd"], s[i], NN) for i in hs]
        vn = [lo[i]["u"] - ws[i] for i in hs]
        pv = [_dotb(lo[i]["pmat"], vn[i], NN) for i in hs]
        kv = [_dotb(lo[i]["kd"], vn[i], TN) for i in hs]
        for i, sl in enumerate(sls):
            st_ref[i, 0] = s[i]
            o_ref[:, sl] = qs[i] + pv[i]
            s_ref[i] = s[i] * lo[i]["gl"] + kv[i]

    gspec = pl.BlockSpec((hb, cs, LANES), lambda h, n: (h, n, 0))
    col = lambda off: pl.BlockSpec((cs, hb * hd), lambda h, n: (n, off + h))
    return pl.pallas_call(
        body, name=name, grid=(ng, nc),
        in_specs=[col(0), col(ng), col(2 * ng), gspec, gspec],
        out_specs=(col(0), pl.BlockSpec((hb, 1, hd, hd), lambda h, n: (h, n, 0, 0))),
        out_shape=(jax.ShapeDtypeStruct((t, nh * hd), F32), jax.ShapeDtypeStruct((nh, nc, hd, hd), F32)),
        scratch_shapes=[pltpu.VMEM((hb, hd, hd), F32)],
        compiler_params=_params("parallel", "arbitrary"),
    )(qkv, qkv, qkv, gbc, bbc)


def _gdn_chunk_bwd(qkv, gbc, bbc, states, do, *, name):
    t = qkv.shape[0]
    nh, cs, hd = GDN_HEADS, GDN_CHUNK, GDN_HEAD_DIM
    nc = t // cs

    hb = GDN_HEAD_BATCH
    ng = nh // hb

    def heads_bwd(q, k, v, gb, bb, s, dsn, dov):
        hs = range(len(q))
        lo = _gdn_local(q, k, v, gb, bb)
        tril, strict, eye, r, c = lo[0]["tril"], lo[0]["strict"], lo[0]["eye"], lo[0]["r"], lo[0]["c"]
        rowi = lax.broadcasted_iota(jnp.int32, (cs, 1), 0)
        get = lambda name: [lo[h][name] for h in hs]
        decay, gamma, gl, kdec = get("decay"), get("gamma"), get("gl"), get("kdec")
        kb, tinv, w, pmat, kd, qd = get("kb"), get("tinv"), get("w"), get("pmat"), get("kd"), get("qd")
        ws = [_dotb(w[h], s[h], NN) for h in hs]
        pdo = [_dotb(pmat[h], dov[h], TN) for h in hs]
        kds = [_dotb(kd[h], dsn[h], NN) for h in hs]
        dqd = [_dotb(dov[h], s[h], NT) for h in hs]
        qdo = [_dotb(qd[h], dov[h], TN) for h in hs]
        vn = [lo[h]["u"] - ws[h] for h in hs]
        dvn = [pdo[h] + kds[h] for h in hs]
        dp = [jnp.where(tril, _dotb(dov[h], vn[h], NT), 0.0) for h in hs]
        dkd = [_dotb(vn[h], dsn[h], NT) for h in hs]
        dw = [-_dotb(dvn[h], s[h], NT) for h in hs]
        wdv = [_dotb(w[h], dvn[h], TN) for h in hs]
        dvb = [_dotf(tinv[h], dvn[h], TN) for h in hs]
        dt1 = [_dotf(dvn[h], lo[h]["vb"], NT) for h in hs]
        dkg = [_dotf(tinv[h], dw[h], TN) for h in hs]
        dt2 = [_dotf(dw[h], lo[h]["kg"], NT) for h in hs]
        tdt = [_dotf(tinv[h], dt1[h] + dt2[h], TN) for h in hs]
        dl = [jnp.where(strict, -_dotf(tdt[h], tinv[h], NT), 0.0) for h in hs]
        dkk = [dl[h] * decay[h] for h in hs]
        dqk = [dp[h] * decay[h] for h in hs]
        dkb = [_dotb(dkk[h], k[h], NN) + dkg[h] * gamma[h] for h in hs]
        dk1 = [_dotb(dkk[h], kb[h], TN) for h in hs]
        dk2 = [_dotb(dqk[h], q[h], TN) for h in hs]
        dq1 = [_dotb(dqk[h], k[h], NN) for h in hs]
        out = []
        for h in hs:
            dgl = jnp.sum(jnp.sum(dsn[h] * s[h], axis=1, keepdims=True), axis=0, keepdims=True)
            ds_prev = gl[h] * dsn[h] + qdo[h] - wdv[h]
            dk = dk1[h] + dk2[h] + dkd[h] * kdec[h] + dkb[h] * bb[h]
            dq = dq1[h] + dqd[h] * gamma[h]
            dbeta = jnp.sum(dvb[h] * v[h], axis=-1, keepdims=True) + jnp.sum(dkb[h] * k[h], axis=-1, keepdims=True)
            e = dl[h] * lo[h]["lmat"] + dp[h] * pmat[h]
            e_col = jnp.sum(e, axis=0, keepdims=True)
            dgc = jnp.sum(e, axis=1, keepdims=True) - jnp.sum(jnp.where(eye, e_col, 0.0), axis=1, keepdims=True)
            dgamma = (jnp.sum(dqd[h] * q[h], axis=-1, keepdims=True)
                      + jnp.sum(dkg[h] * kb[h], axis=-1, keepdims=True))
            rk = jnp.sum(dkd[h] * k[h], axis=-1, keepdims=True) * kdec[h]
            dgcl = jnp.sum(rk, axis=0, keepdims=True) + dgl * gl[h]
            dgc = dgc + dgamma * gamma[h] - rk + jnp.where(rowi == cs - 1, dgcl, 0.0)
            dgc_row = jnp.sum(jnp.where(eye, dgc, 0.0), axis=0, keepdims=True)
            dg = jnp.sum(jnp.where(c >= r, dgc_row, 0.0), axis=1, keepdims=True)
            out.append((dq, dk, dvb[h] * bb[h], dbeta, dg, ds_prev))
        return out

    def body(q_ref, k_ref, v_ref, g_ref, b_ref, st_ref, do_ref, d_ref, dg_ref, db_ref, ds_ref):
        @pl.when(pl.program_id(1) == 0)
        def _():
            ds_ref[...] = jnp.zeros_like(ds_ref)

        sls = [slice(i * hd, (i + 1) * hd) for i in range(hb)]
        hs = range(hb)
        outs = heads_bwd([q_ref[:, sl] for sl in sls], [k_ref[:, sl] for sl in sls], [v_ref[:, sl] for sl in sls],
                         [g_ref[i] for i in hs], [b_ref[i] for i in hs], [st_ref[i, 0] for i in hs],
                         [ds_ref[i] for i in hs], [do_ref[:, sl] for sl in sls])
        for i, sl in enumerate(sls):
            dq, dk, dv, dbeta, dg, ds_prev = outs[i]
            d_ref[0, :, sl], d_ref[1, :, sl], d_ref[2, :, sl] = dq, dk, dv
            db_ref[i] = jnp.broadcast_to(dbeta, (cs, LANES))
            dg_ref[i] = jnp.broadcast_to(dg, (cs, LANES))
            ds_ref[i] = ds_prev

    gspec = pl.BlockSpec((hb, cs, LANES), lambda h, n: (h, nc - 1 - n, 0))
    col = lambda off: pl.BlockSpec((cs, hb * hd), lambda h, n: (nc - 1 - n, off + h))
    return pl.pallas_call(
        body, name=name, grid=(ng, nc),
        in_specs=[col(0), col(ng), col(2 * ng), gspec, gspec,
                  pl.BlockSpec((hb, 1, hd, hd), lambda h, n: (h, nc - 1 - n, 0, 0)), col(0)],
        out_specs=(pl.BlockSpec((3, cs, hb * hd), lambda h, n: (0, nc - 1 - n, h)), gspec, gspec),
        out_shape=(jax.ShapeDtypeStruct((3, t, nh * hd), F32),) + (jax.ShapeDtypeStruct((nh, t, LANES), F32),) * 2,
        scratch_shapes=[pltpu.VMEM((hb, hd, hd), F32)],
        compiler_params=_params("parallel", "arbitrary"),
    )(qkv, qkv, qkv, gbc, bbc, states, do)


def _gdn_onorm_fwd(o, proj, norm_g, *, name):
    t = o.shape[0]
    w = GDN_KEY_DIM
    goff = 3 * GDN_KEY_DIM // w

    def body(o_ref, gp_ref, g_ref, y_ref):
        gv = g_ref[...]
        for h in range(GDN_HEADS):
            sl = slice(h * GDN_HEAD_DIM, (h + 1) * GDN_HEAD_DIM)
            oh = o_ref[:, sl]
            gp = gp_ref[:, sl]
            r = lax.rsqrt(jnp.mean(oh * oh, axis=-1, keepdims=True) + EPS)
            y_ref[:, sl] = (oh * r * gv * gp * _sigmoid(gp)).astype(y_ref.dtype)

    return pl.pallas_call(
        body, name=name, grid=(t // ROWS,),
        in_specs=[_row_spec(w), pl.BlockSpec((ROWS, w), lambda i: (i, goff)), _const_spec((1, GDN_HEAD_DIM))],
        out_specs=_row_spec(w), out_shape=jax.ShapeDtypeStruct((t, w), BF16),
        compiler_params=_params("parallel"),
    )(o, proj, norm_g)


def _gdn_onorm_bwd(o, proj, norm_g, dy, *, name):
    t = o.shape[0]
    w = GDN_KEY_DIM
    goff = 3 * GDN_KEY_DIM // w

    def body(o_ref, gp_ref, g_ref, dy_ref, do_ref, dgp_ref, st_ref):
        @pl.when(pl.program_id(0) == 0)
        def _():
            st_ref[...] = jnp.zeros_like(st_ref)

        gv = g_ref[...]
        acc = jnp.zeros((1, GDN_HEAD_DIM), F32)
        for h in range(GDN_HEADS):
            sl = slice(h * GDN_HEAD_DIM, (h + 1) * GDN_HEAD_DIM)
            oh = o_ref[:, sl]
            gp = gp_ref[:, sl]
            dyv = dy_ref[:, sl].astype(F32)
            r = lax.rsqrt(jnp.mean(oh * oh, axis=-1, keepdims=True) + EPS)
            xh = oh * r
            sg = _sigmoid(gp)
            dn = dyv * gp * sg
            dgp_ref[:, sl] = (dyv * xh * gv * sg * (1.0 + gp * (1.0 - sg))).astype(dgp_ref.dtype)
            acc = acc + jnp.sum(dn * xh, axis=0, keepdims=True)
            dxh = dn * gv
            do_ref[:, sl] = r * (dxh - xh * jnp.mean(dxh * xh, axis=-1, keepdims=True))
        st_ref[0:1, :] += acc

    return pl.pallas_call(
        body, name=name, grid=(t // ROWS,),
        in_specs=[_row_spec(w), pl.BlockSpec((ROWS, w), lambda i: (i, goff)), _const_spec((1, GDN_HEAD_DIM)),
                  _row_spec(w)],
        out_specs=(_row_spec(w), _row_spec(w), _const_spec((8, GDN_HEAD_DIM))),
        out_shape=(jax.ShapeDtypeStruct((t, w), F32), jax.ShapeDtypeStruct((t, w), BF16),
                   jax.ShapeDtypeStruct((8, GDN_HEAD_DIM), F32)),
        compiler_params=_params("arbitrary"),
    )(o, proj, norm_g, dy)


def _mla_prep_fwd(proj, qg, kvg, *, name):
    t = proj.shape[0]
    q1, k1 = MLA_Q_RANK, MLA_Q_RANK + MLA_KV_RANK

    def body(p_ref, qg_ref, kg_ref, cq_ref, ck_ref):
        cq = p_ref[:, 0:q1]
        ck = p_ref[:, q1:k1]
        cq_ref[...] = (cq * lax.rsqrt(jnp.mean(cq * cq, axis=-1, keepdims=True) + EPS) * qg_ref[...]).astype(BF16)
        ck_ref[...] = (ck * lax.rsqrt(jnp.mean(ck * ck, axis=-1, keepdims=True) + EPS) * kg_ref[...]).astype(BF16)

    return pl.pallas_call(
        body, name=name, grid=(t // ROWS,),
        in_specs=[_row_spec(MLA_IN), _const_spec((1, MLA_Q_RANK)), _const_spec((1, MLA_KV_RANK))],
        out_specs=(_row_spec(MLA_Q_RANK), _row_spec(MLA_KV_RANK)),
        out_shape=(jax.ShapeDtypeStruct((t, MLA_Q_RANK), BF16), jax.ShapeDtypeStruct((t, MLA_KV_RANK), BF16)),
        compiler_params=_params("parallel"),
    )(proj, qg, kvg)


def _mla_prep_bwd(proj, qg, kvg, dcq, dck, dkr, *, name):
    t = proj.shape[0]
    q1, k1 = MLA_Q_RANK, MLA_Q_RANK + MLA_KV_RANK

    def body(p_ref, qg_ref, kg_ref, dq_ref, dk_ref, dr_ref, dp_ref, st_ref):
        @pl.when(pl.program_id(0) == 0)
        def _():
            st_ref[...] = jnp.zeros_like(st_ref)

        for lo, hi, g_ref, d_ref in ((0, q1, qg_ref, dq_ref), (q1, k1, kg_ref, dk_ref)):
            xv = p_ref[:, lo:hi]
            dn = d_ref[...]
            r = lax.rsqrt(jnp.mean(xv * xv, axis=-1, keepdims=True) + EPS)
            xh = xv * r
            dxh = dn * g_ref[...]
            dp_ref[:, lo:hi] = (r * (dxh - xh * jnp.mean(dxh * xh, axis=-1, keepdims=True))).astype(dp_ref.dtype)
            st_ref[0:1, lo:hi] += jnp.sum(dn * xh, axis=0, keepdims=True)
        dp_ref[:, k1:MLA_IN] = dr_ref[...].astype(dp_ref.dtype)

    return pl.pallas_call(
        body, name=name, grid=(t // ROWS,),
        in_specs=[_row_spec(MLA_IN), _const_spec((1, MLA_Q_RANK)), _const_spec((1, MLA_KV_RANK)),
                  _row_spec(MLA_Q_RANK), _row_spec(MLA_KV_RANK), _row_spec(MLA_ROPE)],
        out_specs=(_row_spec(MLA_IN), _const_spec((8, MLA_IN))),
        out_shape=(jax.ShapeDtypeStruct((t, MLA_IN), BF16), jax.ShapeDtypeStruct((8, MLA_IN), F32)),
        compiler_params=_params("arbitrary"),
    )(proj, qg, kvg, dcq, dck, dkr)


def _rope(xr, cos_t, sin_t, *, name):
    t, w = xr.shape
    ns = w // LANES

    def body(x_ref, c_ref, s_ref, o_ref):
        cv, sv = c_ref[...], s_ref[...]
        lane = lax.broadcasted_iota(jnp.int32, (ROWS, LANES), 1)
        first = (lane % MLA_ROPE) < (MLA_ROPE // 2)
        for i in range(ns):
            sl = slice(i * LANES, (i + 1) * LANES)
            xv = x_ref[:, sl]
            sw = jnp.where(first, pltpu.roll(xv, LANES - MLA_ROPE // 2, 1), pltpu.roll(xv, MLA_ROPE // 2, 1))
            o_ref[:, sl] = xv * cv + sw * sv

    return pl.pallas_call(
        body, name=name, grid=(t // ROWS,),
        in_specs=[_row_spec(w), _row_spec(LANES), _row_spec(LANES)], out_specs=_row_spec(w),
        out_shape=jax.ShapeDtypeStruct((t, w), F32), compiler_params=_params("parallel"),
    )(xr, cos_t, sin_t)


def _rope_bwd(dr, cos_t, sin_t, *, name):
    t, w = dr.shape
    ns = w // LANES

    def body(d_ref, c_ref, s_ref, o_ref):
        cv, sv = c_ref[...], s_ref[...]
        lane = lax.broadcasted_iota(jnp.int32, (ROWS, LANES), 1)
        first = (lane % MLA_ROPE) < (MLA_ROPE // 2)
        for i in range(ns):
            sl = slice(i * LANES, (i + 1) * LANES)
            dv = d_ref[:, sl]
            ds = dv * sv
            sw = jnp.where(first, pltpu.roll(ds, LANES - MLA_ROPE // 2, 1), pltpu.roll(ds, MLA_ROPE // 2, 1))
            o_ref[:, sl] = dv * cv + sw

    return pl.pallas_call(
        body, name=name, grid=(t // ROWS,),
        in_specs=[_row_spec(w), _row_spec(LANES), _row_spec(LANES)], out_specs=_row_spec(w),
        out_shape=jax.ShapeDtypeStruct((t, w), F32), compiler_params=_params("parallel"),
    )(dr, cos_t, sin_t)


ATT_BLOCK = 256
ATT_HEAD_BATCH = 4
ATT_HEAD_BATCH_BWD = 2
ATT_SCALE = MLA_QK ** -0.5


def _causal_mask(i, j, blk):
    rows = i * blk + lax.broadcasted_iota(jnp.int32, (blk, blk), 0)
    cols = j * blk + lax.broadcasted_iota(jnp.int32, (blk, blk), 1)
    return cols <= rows


def _attn_fwd(q, k, v, *, name):
    nh, t, dk = q.shape
    dv = v.shape[-1]
    blk = min(ATT_BLOCK, t)

    hb = ATT_HEAD_BATCH
    hs = range(hb)

    def body(q_ref, k_ref, v_ref, o_ref, l_ref):
        i = pl.program_id(1)
        qv = [q_ref[h] for h in hs]

        def step(j, carry):
            m, l, acc = carry[:hb], carry[hb:2 * hb], carry[2 * hb:]
            off = pl.multiple_of(j * blk, blk)
            mask = _causal_mask(i, j, blk)
            s = [_dotb(qv[h], k_ref[h, pl.ds(off, blk), :], NT) for h in hs]
            s = [jnp.where(mask, s[h] * ATT_SCALE, NEG) for h in hs]
            m_new = [jnp.maximum(m[h], jnp.max(s[h], axis=-1, keepdims=True)) for h in hs]
            p = [jnp.exp(s[h] - m_new[h]) for h in hs]
            pv = [_dotb(p[h], v_ref[h, pl.ds(off, blk), :], NN) for h in hs]
            alpha = [jnp.exp(m[h] - m_new[h]) for h in hs]
            l = [alpha[h] * l[h] + jnp.sum(p[h], axis=-1, keepdims=True) for h in hs]
            acc = [alpha[h] * acc[h] + pv[h] for h in hs]
            return tuple(m_new) + tuple(l) + tuple(acc)

        init = ((jnp.full((blk, 1), NEG, F32),) * hb + (jnp.zeros((blk, 1), F32),) * hb
                + (jnp.zeros((blk, dv), F32),) * hb)
        out = lax.fori_loop(0, i + 1, step, init)
        for h in hs:
            m, l, acc = out[h], out[hb + h], out[2 * hb + h]
            o_ref[h] = acc / l
            l_ref[h] = jnp.broadcast_to(m + jnp.log(l), (blk, LANES))

    return pl.pallas_call(
        body, name=name, grid=(nh // hb, t // blk),
        in_specs=[pl.BlockSpec((hb, blk, dk), lambda h, i: (h, i, 0)), pl.BlockSpec((hb, t, dk), lambda h, i: (h, 0, 0)),
                  pl.BlockSpec((hb, t, dv), lambda h, i: (h, 0, 0))],
        out_specs=(pl.BlockSpec((hb, blk, dv), lambda h, i: (h, i, 0)),
                   pl.BlockSpec((hb, blk, LANES), lambda h, i: (h, i, 0))),
        out_shape=(jax.ShapeDtypeStruct((nh, t, dv), F32), jax.ShapeDtypeStruct((nh, t, LANES), F32)),
        compiler_params=_params("parallel", "parallel"),
    )(q, k, v)


def _attn_bwd(q, k, v, o, lse, do, *, name):
    nh, t, dk = q.shape
    dv = v.shape[-1]
    blk = min(ATT_BLOCK, t)
    nb = t // blk

    hb = ATT_HEAD_BATCH_BWD
    hs = range(hb)

    def body(q_ref, k_ref, v_ref, o_ref, l_ref, do_ref, dq_ref, dk_ref, dv_ref):
        j = pl.program_id(1)

        @pl.when(j == 0)
        def _():
            dq_ref[...] = jnp.zeros_like(dq_ref)

        kv = [k_ref[h] for h in hs]
        vv = [v_ref[h] for h in hs]

        def step(i, carry):
            dk_acc, dv_acc = carry[:hb], carry[hb:]
            off = pl.multiple_of(i * blk, blk)
            rows = pl.ds(off, blk)
            mask = _causal_mask(i, j, blk)
            qv = [q_ref[h, rows, :] for h in hs]
            dov = [do_ref[h, rows, :] for h in hs]
            s = [_dotb(qv[h], kv[h], NT) for h in hs]
            dp = [_dotb(dov[h], vv[h], NT) for h in hs]
            p = [jnp.exp(jnp.where(mask, s[h] * ATT_SCALE, NEG) - l_ref[h, rows, :][:, 0:1]) for h in hs]
            delta = [jnp.sum(dov[h] * o_ref[h, rows, :], axis=-1, keepdims=True) for h in hs]
            ds = [p[h] * (dp[h] - delta[h]) * ATT_SCALE for h in hs]
            dvn = [_dotb(p[h], dov[h], TN) for h in hs]
            dkn = [_dotb(ds[h], qv[h], TN) for h in hs]
            dqn = [_dotb(ds[h], kv[h], NN) for h in hs]
            for h in hs:
                dq_ref[h, rows, :] += dqn[h]
            return tuple(dk_acc[h] + dkn[h] for h in hs) + tuple(dv_acc[h] + dvn[h] for h in hs)

        out = lax.fori_loop(j, nb, step, (jnp.zeros((blk, dk), F32),) * hb + (jnp.zeros((blk, dv), F32),) * hb)
        for h in hs:
            dk_ref[h] = out[h]
            dv_ref[h] = out[hb + h]

    full = lambda w: pl.BlockSpec((hb, t, w), lambda h, j: (h, 0, 0))
    part = lambda w: pl.BlockSpec((hb, blk, w), lambda h, j: (h, j, 0))
    return pl.pallas_call(
        body, name=name, grid=(nh // hb, nb),
        in_specs=[full(dk), part(dk), part(dv), full(dv), full(LANES), full(dv)],
        out_specs=(full(dk), part(dk), part(dv)),
        out_shape=(jax.ShapeDtypeStruct((nh, t, dk), F32), jax.ShapeDtypeStruct((nh, t, dk), F32),
                   jax.ShapeDtypeStruct((nh, t, dv), F32)),
        compiler_params=_params("parallel", "arbitrary"),
    )(q, k, v, o, lse, do)


def _ada_mod(c_all, ada_w, ada_b_cols, *, name):
    nl, d, wc = ada_w.shape

    def body(c_ref, w_ref, b_ref, o_ref):
        cv = c_ref[...]
        o_ref[0] = _dotb(cv * _sigmoid(cv), w_ref[0], NN) + b_ref[0]

    return pl.pallas_call(
        body, name=name, grid=(nl,),
        in_specs=[_const_spec((N_DEV, d)), pl.BlockSpec((1, d, wc), lambda l: (l, 0, 0)),
                  pl.BlockSpec((1, 1, wc), lambda l: (l, 0, 0))],
        out_specs=pl.BlockSpec((1, N_DEV, wc), lambda l: (l, 0, 0)),
        out_shape=jax.ShapeDtypeStruct((nl, N_DEV, wc), F32), compiler_params=_params("parallel"),
    )(c_all, ada_w, ada_b_cols)


def _adam_math(g, w, m, v):
    m2 = ADAM_B1 * m + (1.0 - ADAM_B1) * g
    v2 = ADAM_B2 * v + (1.0 - ADAM_B2) * (g * g)
    delta = -ADAM_LR * ((m2 / ADAM_BC1) / (jnp.sqrt(v2 / ADAM_BC2) + ADAM_EPS) + ADAM_WD * w)
    return delta, m2, v2


def _ada_grad_adamw(c_all, dmod_cols, w, m, v, *, name):
    nl, d, wc = w.shape
    tr = 256

    def body(c_ref, dm_ref, w_ref, m_ref, v_ref, g_ref, d_ref, m2_ref, v2_ref):
        cv = c_ref[...]
        g = _dotf(cv * _sigmoid(cv), dm_ref[0], TN)
        delta, m2, v2 = _adam_math(g, w_ref[0], m_ref[0], v_ref[0])
        g_ref[0], d_ref[0], m2_ref[0], v2_ref[0] = g, delta, m2, v2

    blk = pl.BlockSpec((1, tr, wc), lambda l, i: (l, i, 0))
    return pl.pallas_call(
        body, name=name, grid=(nl, d // tr),
        in_specs=[pl.BlockSpec((N_DEV, tr), lambda l, i: (0, i)), pl.BlockSpec((1, N_DEV, wc), lambda l, i: (l, 0, 0)),
                  blk, blk, blk],
        out_specs=(blk,) * 4, out_shape=(jax.ShapeDtypeStruct(w.shape, F32),) * 4,
        compiler_params=_params("parallel", "parallel"),
    )(c_all, dmod_cols, w, m, v)


def _adamw(parts, w, m, v, *, name):
    nl, r, c = w.shape
    ns = parts[0].shape[0]
    lanes_padded = -(-c // LANES) * LANES
    row_bytes = 2 * nl * ns * lanes_padded * parts[0].dtype.itemsize
    tr = _pick(r, min(256, max(16, (VMEM_LIMIT // 2) // row_bytes)), 16)

    def body(*refs):
        p_refs = refs[:nl]
        w_ref, m_ref, v_ref, g_ref, d_ref, m2_ref, v2_ref = refs[nl:]
        layer = pl.program_id(0)
        for q in range(nl):
            @pl.when(layer == q)
            def _(q=q):
                g = p_refs[q][0].astype(F32)
                for s in range(1, ns):
                    g = g + p_refs[q][s].astype(F32)
                delta, m2, v2 = _adam_math(g, w_ref[0], m_ref[0], v_ref[0])
                g_ref[0], d_ref[0], m2_ref[0], v2_ref[0] = g, delta, m2, v2

    blk = pl.BlockSpec((1, tr, c), lambda l, i: (l, i, 0))
    p_specs = [pl.BlockSpec((ns, tr, c), lambda l, i, q=q: (0, jnp.where(l == q, i, 0), 0)) for q in range(nl)]
    return pl.pallas_call(
        body, name=name, grid=(nl, r // tr),
        in_specs=p_specs + [blk, blk, blk],
        out_specs=(blk,) * 4, out_shape=(jax.ShapeDtypeStruct(w.shape, F32),) * 4,
        compiler_params=_params("arbitrary", "arbitrary"),
    )(*parts, w, m, v)


def _sum_parts(parts, *, name):
    ns, r, c = parts.shape

    def body(p_ref, o_ref):
        acc = p_ref[0]
        for s in range(1, ns):
            acc = acc + p_ref[s]
        o_ref[...] = acc

    return pl.pallas_call(
        body, name=name, out_shape=jax.ShapeDtypeStruct((r, c), F32),
        in_specs=[pl.BlockSpec(memory_space=pltpu.VMEM)], out_specs=pl.BlockSpec(memory_space=pltpu.VMEM),
    )(parts)


def _pack(arrs):
    flat = jnp.concatenate([a.reshape(-1).astype(F32) for a in arrs])
    pad = (-flat.shape[0]) % (8 * LANES)
    return jnp.pad(flat, (0, pad)).reshape(-1, LANES)


def _unpack(packed, shapes, lead=()):
    flat = packed.reshape(lead + (-1,))
    out, off = [], 0
    for s in shapes:
        n = math.prod(s)
        out.append(flat[..., off:off + n].reshape(lead + tuple(s)))
        off += n
    return out


def _gather_cols(g):
    _, nl, r, cs = g.shape
    return jnp.transpose(g, (1, 2, 0, 3)).reshape(nl, r, N_DEV * cs)


def _gather_rows(g):
    _, nl, rs, c = g.shape
    return jnp.transpose(g, (1, 0, 2, 3)).reshape(nl, N_DEV * rs, c)


def _scatter_cols(full):
    nl, r, c = full.shape
    return jnp.transpose(full.reshape(nl, r, N_DEV, c // N_DEV), (2, 0, 1, 3))


def _scatter_rows(full):
    nl, r, c = full.shape
    return jnp.transpose(full.reshape(nl, N_DEV, r // N_DEV, c), (1, 0, 2, 3))


def _row(v):
    return v.reshape(1, -1)


def _local_step(x, target, mod, cos_t, sin_t, rep, get_weights, put_grads):
    t = x.shape[0]
    saved = []
    for layer in range(DEPTH):
        j = layer // 2
        tag = f"l{layer}"
        shift_m, scale_m, gate_m, shift_f, scale_f, gate_f = [_row(mod[layer, i]) for i in range(N_MOD)]
        lw = dict(get_weights(layer, "mix", x))
        rec = {"x0": x, "lw": lw}
        h = _adaln_fwd(x, _row(rep["norm_mix_g"][layer]), scale_m, shift_m, name=f"adaln_mix_{tag}")
        rec["h"] = h
        if layer % 2 == 0:
            proj = _mm(h, lw["w_main"], mode="nn", out_dtype=F32, tm=256, tn=GDN_MAIN, name=f"gdn_in_{tag}")
            ab = _mm(h, lw["w_ab"], mode="nn", out_dtype=F32, name=f"gdn_in_ab_{tag}")
            qkv = _gdn_prep_fwd(proj, rep["gdn_conv_wt"][j], name=f"gdn_prep_{tag}")
            gbeta = _gdn_gate_fwd(ab, rep["gdn_gate_prm"][j], name=f"gdn_gate_{tag}")
            gbc = jnp.broadcast_to(jnp.transpose(gbeta[:, 0:GDN_HEADS])[:, :, None], (GDN_HEADS, t, LANES))
            bbc = jnp.broadcast_to(jnp.transpose(gbeta[:, GDN_HEADS:2 * GDN_HEADS])[:, :, None],
                                   (GDN_HEADS, t, LANES))
            o, states = _gdn_chunk_fwd(qkv, gbc, bbc, name=f"gdn_chunk_{tag}")
            og = _gdn_onorm_fwd(o, proj, _row(rep["gdn_norm_g"][j]), name=f"gdn_onorm_{tag}")
            x, y = _mm_resid(og, lw["w_out"], x, gate_m, name=f"gdn_out_{tag}")
            rec.update(proj=proj, ab=ab, qkv=qkv, gbc=gbc, bbc=bbc, states=states, o=o, og=og, y=y)
        else:
            proj = _mm(h, lw["w_in"], mode="nn", out_dtype=F32, name=f"mla_in_{tag}")
            cq, ck = _mla_prep_fwd(proj, _row(rep["mla_q_norm_g"][j]), _row(rep["mla_kv_norm_g"][j]),
                                   name=f"mla_prep_{tag}")
            qf = _mm(cq, lw["w_uq"], mode="nn", out_dtype=F32, name=f"mla_uq_{tag}")
            kvf = _mm(ck, lw["w_ukv"], mode="nn", out_dtype=F32, name=f"mla_ukv_{tag}")
            nrope = MLA_HEADS * MLA_ROPE
            krp = jnp.pad(proj[:, MLA_Q_RANK + MLA_KV_RANK:], ((0, 0), (0, LANES - MLA_ROPE)))
            roped = _rope(jnp.concatenate([qf[:, MLA_HEADS * MLA_NOPE:], krp], axis=1), cos_t, sin_t,
                          name=f"rope_{tag}")
            q_nope = qf[:, :MLA_HEADS * MLA_NOPE].reshape(t, MLA_HEADS, MLA_NOPE)
            q_rope = roped[:, :nrope].reshape(t, MLA_HEADS, MLA_ROPE)
            k_rope = jnp.broadcast_to(roped[:, None, nrope:nrope + MLA_ROPE], (t, MLA_HEADS, MLA_ROPE))
            kv3 = kvf.reshape(t, MLA_HEADS, MLA_NOPE + MLA_V)
            qc = jnp.transpose(jnp.concatenate([q_nope, q_rope], axis=-1), (1, 0, 2)).astype(BF16)
            kc = jnp.transpose(jnp.concatenate([kv3[..., :MLA_NOPE], k_rope], axis=-1), (1, 0, 2)).astype(BF16)
            vc = jnp.transpose(kv3[..., MLA_NOPE:], (1, 0, 2)).astype(BF16)
            oh, lse = _attn_fwd(qc, kc, vc, name=f"attn_{tag}")
            oc = jnp.transpose(oh, (1, 0, 2)).reshape(t, MLA_HEADS * MLA_V).astype(BF16)
            x, y = _mm_resid(oc, lw["w_out"], x, gate_m, name=f"mla_out_{tag}")
            rec.update(proj=proj, cq=cq, ck=ck, qc=qc, kc=kc, vc=vc, oh=oh, lse=lse, oc=oc, y=y)
        rec["x1"] = x
        lw.update(get_weights(layer, "ffn", x))
        h2 = _adaln_fwd(x, _row(rep["norm_ffn_g"][layer]), scale_f, shift_f, name=f"adaln_ffn_{tag}")
        s, a2, b2 = _ffn_gu_fwd(h2, lw["w_g"], lw["w_u"], name=f"ffn_gu_{tag}")
        x, y2 = _mm_resid(s, lw["w_down"], x, gate_f, name=f"ffn_down_{tag}")
        rec.update(h2=h2, a2=a2, b2=b2, s=s, y2=y2)
        saved.append(rec)

    dx, st, ls = _loss_head(x, _row(rep["final_norm_g"]), target, name="loss_head")
    loss = ls[0, 0]
    grads = {"final_norm_g": st[0]}
    per_layer = {k: [None] * DEPTH for k in ("norm_mix_g", "norm_ffn_g")}
    per_gdn = {k: [None] * 2 for k in ("gdn_conv_wt", "gdn_a_log", "gdn_dt_bias", "gdn_norm_g")}
    per_mla = {k: [None] * 2 for k in ("mla_q_norm_g", "mla_kv_norm_g")}
    dmod = [None] * DEPTH
    dep = jnp.zeros((8, LANES), F32)

    for layer in reversed(range(DEPTH)):
        j = layer // 2
        tag = f"l{layer}"
        rec = saved[layer]
        lw = rec["lw"]
        shift_m, scale_m, gate_m, shift_f, scale_f, gate_f = [_row(mod[layer, i]) for i in range(N_MOD)]
        dy2, st_g = _gate_bwd(dx, rec["y2"], gate_f, dep, name=f"gate_bwd_ffn_{tag}")
        dgate_f = st_g[0]
        dw_down = _mm(rec["s"], dy2, mode="tn", out_dtype=BF16, tm=256, tn=1024, name=f"ffn_down_dw_{tag}")
        da2, db2 = _ffn_down_dx(dy2, lw["w_down"], rec["a2"], rec["b2"], name=f"ffn_down_dx_{tag}")
        dw_g = _mm(rec["h2"], da2, mode="tn", out_dtype=BF16, tm=1024, tn=512, name=f"ffn_g_dw_{tag}")
        dw_u = _mm(rec["h2"], db2, mode="tn", out_dtype=BF16, tm=1024, tn=512, name=f"ffn_u_dw_{tag}")
        dep = put_grads(layer, "ffn", {"w_g": dw_g, "w_u": dw_u, "w_down": dw_down})
        dh2 = _mm(da2, lw["w_g"], mode="nt", out_dtype=F32, tm=256, tn=1024, name=f"ffn_g_dx_{tag}")
        dh2 = _mm(db2, lw["w_u"], mode="nt", out_dtype=BF16, add=dh2, tm=256, tn=1024, name=f"ffn_u_dx_{tag}")
        dx, st_n = _adaln_bwd(rec["x1"], _row(rep["norm_ffn_g"][layer]), scale_f, shift_f, dh2, dx, dep,
                              name=f"adaln_ffn_bwd_{tag}")
        per_layer["norm_ffn_g"][layer] = st_n[0]
        dscale_f, dshift_f = st_n[1], st_n[2]
        dy, st_g = _gate_bwd(dx, rec["y"], gate_m, dep, name=f"gate_bwd_mix_{tag}")
        dgate_m = st_g[0]
        big = {}
        if layer % 2 == 0:
            big["w_out"] = _mm(rec["og"], dy, mode="tn", out_dtype=BF16, name=f"gdn_out_dw_{tag}")
            dog = _mm(dy, lw["w_out"], mode="nt", out_dtype=BF16, name=f"gdn_out_dx_{tag}")
            do, dgp, st_o = _gdn_onorm_bwd(rec["o"], rec["proj"], _row(rep["gdn_norm_g"][j]), dog,
                                           name=f"gdn_onorm_bwd_{tag}")
            per_gdn["gdn_norm_g"][j] = st_o[0]
            dqkv, dgc_, dbc_ = _gdn_chunk_bwd(rec["qkv"], rec["gbc"], rec["bbc"], rec["states"], do,
                                               name=f"gdn_chunk_bwd_{tag}")
            dgb = jnp.concatenate([jnp.transpose(dgc_[:, :, 0]), jnp.transpose(dbc_[:, :, 0])], axis=1)
            dgb = jnp.pad(dgb, ((0, 0), (0, LANES - 2 * GDN_HEADS)))
            dab, st_a = _gdn_gate_bwd(rec["ab"], rep["gdn_gate_prm"][j], dgb, name=f"gdn_gate_bwd_{tag}")
            per_gdn["gdn_a_log"][j] = st_a[0, :GDN_HEADS]
            per_gdn["gdn_dt_bias"][j] = st_a[1, :GDN_HEADS]
            dpre, dcw = _gdn_prep_bwd(rec["proj"], rep["gdn_conv_wt"][j], dqkv, name=f"gdn_prep_bwd_{tag}")
            per_gdn["gdn_conv_wt"][j] = dcw
            dproj = jnp.concatenate([dpre, dgp], axis=1)
            dw_main = _mm(rec["h"], dproj, mode="tn", out_dtype=BF16, tm=1024, tn=512, name=f"gdn_in_dw_{tag}")
            dw_ab = _mm(rec["h"], dab, mode="tn", out_dtype=BF16, name=f"gdn_in_ab_dw_{tag}")
            big["w_in"] = jnp.concatenate([dw_main, dw_ab[:, :2 * GDN_HEADS]], axis=1)
            dep = put_grads(layer, "gdn", big)
            dh_ab = _mm(dab, lw["w_ab"], mode="nt", out_dtype=F32, name=f"gdn_in_ab_dx_{tag}")
            dh = _mm(dproj, lw["w_main"], mode="nt", out_dtype=BF16, add=dh_ab, tm=256, tn=1024,
                     name=f"gdn_in_dx_{tag}")
        else:
            big["w_out"] = _mm(rec["oc"], dy, mode="tn", out_dtype=BF16, name=f"mla_out_dw_{tag}")
            doc = _mm(dy, lw["w_out"], mode="nt", out_dtype=F32, name=f"mla_out_dx_{tag}")
            doh = jnp.transpose(doc.reshape(t, MLA_HEADS, MLA_V), (1, 0, 2))
            dqc, dkc, dvc = _attn_bwd(rec["qc"], rec["kc"], rec["vc"], rec["oh"], rec["lse"], doh,
                                      name=f"attn_bwd_{tag}")
            dqn = jnp.transpose(dqc[..., :MLA_NOPE], (1, 0, 2)).reshape(t, MLA_HEADS * MLA_NOPE)
            dqr = jnp.transpose(dqc[..., MLA_NOPE:], (1, 0, 2)).reshape(t, MLA_HEADS * MLA_ROPE)
            dkr = jnp.pad(jnp.sum(dkc[..., MLA_NOPE:], axis=0), ((0, 0), (0, LANES - MLA_ROPE)))
            drope = _rope_bwd(jnp.concatenate([dqr, dkr], axis=1), cos_t, sin_t, name=f"rope_bwd_{tag}")
            nrope = MLA_HEADS * MLA_ROPE
            dqf = jnp.concatenate([dqn, drope[:, :nrope]], axis=1).astype(BF16)
            dkvf = jnp.concatenate([jnp.transpose(dkc[..., :MLA_NOPE], (1, 0, 2)), jnp.transpose(dvc, (1, 0, 2))],
                                   axis=-1).reshape(t, MLA_HEADS * (MLA_NOPE + MLA_V)).astype(BF16)
            big["w_uq"] = _mm(rec["cq"], dqf, mode="tn", out_dtype=BF16, name=f"mla_uq_dw_{tag}")
            big["w_ukv"] = _mm(rec["ck"], dkvf, mode="tn", out_dtype=BF16, name=f"mla_ukv_dw_{tag}")
            dcq = _mm(dqf, lw["w_uq"], mode="nt", out_dtype=F32, name=f"mla_uq_dx_{tag}")
            dck = _mm(dkvf, lw["w_ukv"], mode="nt", out_dtype=F32, name=f"mla_ukv_dx_{tag}")
            dproj, st_p = _mla_prep_bwd(rec["proj"], _row(rep["mla_q_norm_g"][j]), _row(rep["mla_kv_norm_g"][j]),
                                        dcq, dck, drope[:, nrope:nrope + MLA_ROPE], name=f"mla_prep_bwd_{tag}")
            per_mla["mla_q_norm_g"][j] = st_p[0, :MLA_Q_RANK]
            per_mla["mla_kv_norm_g"][j] = st_p[0, MLA_Q_RANK:MLA_Q_RANK + MLA_KV_RANK]
            big["w_in"] = _mm(rec["h"], dproj, mode="tn", out_dtype=BF16, name=f"mla_in_dw_{tag}")
            dep = put_grads(layer, "mla", big)
            dh = _mm(dproj, lw["w_in"], mode="nt", out_dtype=BF16, name=f"mla_in_dx_{tag}")
        dx, st_n = _adaln_bwd(rec["x0"], _row(rep["norm_mix_g"][layer]), scale_m, shift_m, dh, dx, dep,
                              name=f"adaln_mix_bwd_{tag}")
        per_layer["norm_mix_g"][layer] = st_n[0]
        dmod[layer] = jnp.stack([st_n[2], st_n[1], dgate_m, dshift_f, dscale_f, dgate_f])

    for d in (per_layer, per_gdn, per_mla):
        for k, v in d.items():
            grads[k] = jnp.stack(v)
    return loss, dx, jnp.stack(dmod), grads


BIG = ("gdn_w_in", "gdn_w_out", "mla_w_in", "mla_w_uq", "mla_w_ukv", "mla_w_out", "ffn_w_gate", "ffn_w_up",
       "ffn_w_down")
COL_SHARDED = ("gdn_w_in", "mla_w_uq", "mla_w_ukv", "ffn_w_gate", "ffn_w_up")
SMALL = ("ada_b", "norm_mix_g", "norm_ffn_g", "gdn_conv_w", "gdn_a_log", "gdn_dt_bias", "gdn_norm_g",
         "mla_q_norm_g", "mla_kv_norm_g", "final_norm_g")
WEIGHTS = ("ada_w", "ada_b", "norm_mix_g", "norm_ffn_g", "gdn_w_in", "gdn_conv_w", "gdn_a_log", "gdn_dt_bias",
           "gdn_norm_g", "gdn_w_out", "mla_w_in", "mla_q_norm_g", "mla_kv_norm_g", "mla_w_uq", "mla_w_ukv",
           "mla_w_out", "ffn_w_gate", "ffn_w_up", "ffn_w_down", "final_norm_g")


def _uq_to_kernel_layout(w):
    lead = w.shape[:-1]
    w4 = w.reshape(lead + (MLA_HEADS, MLA_QK))
    return jnp.concatenate([w4[..., :MLA_NOPE].reshape(lead + (-1,)), w4[..., MLA_NOPE:].reshape(lead + (-1,))],
                           axis=-1)


def _uq_from_kernel_layout(w):
    lead = w.shape[:-1]
    nope = w[..., :MLA_HEADS * MLA_NOPE].reshape(lead + (MLA_HEADS, MLA_NOPE))
    rope = w[..., MLA_HEADS * MLA_NOPE:].reshape(lead + (MLA_HEADS, MLA_ROPE))
    return jnp.concatenate([nope, rope], axis=-1).reshape(lead + (-1,))


def _group_names(layer, kind):
    if kind == "ffn":
        return ("ffn_w_gate", "ffn_w_up", "ffn_w_down")
    return ("gdn_w_in", "gdn_w_out") if layer % 2 == 0 else ("mla_w_in", "mla_w_uq", "mla_w_ukv", "mla_w_out")


def _layer_index(name, layer):
    return layer if name.startswith("ffn") else layer // 2


def _cols(g):
    return jnp.transpose(g, (1, 0, 2)).reshape(g.shape[1], N_DEV * g.shape[2])


def _rows(g):
    return g.reshape(N_DEV * g.shape[1], g.shape[2])


def _uncols(full):
    r, c = full.shape
    return jnp.transpose(full.reshape(r, N_DEV, c // N_DEV), (1, 0, 2))


def _unrows(full):
    r, c = full.shape
    return full.reshape(N_DEV, r // N_DEV, c)


def _group_weights(layer, kind, got, zero):
    if kind == "ffn":
        return {"w_g": _cols(got["ffn_w_gate"]) + zero, "w_u": _cols(got["ffn_w_up"]),
                "w_down": _rows(got["ffn_w_down"])}
    if layer % 2 == 0:
        w_in = _cols(got["gdn_w_in"]) + zero
        return dict(w_main=w_in[:, :GDN_MAIN], w_ab=jnp.pad(w_in[:, GDN_MAIN:], ((0, 0), (0, LANES - 2 * GDN_HEADS))),
                    w_out=_rows(got["gdn_w_out"]))
    return dict(w_in=_rows(got["mla_w_in"]), w_uq=_uq_to_kernel_layout(_cols(got["mla_w_uq"])) + zero,
                w_ukv=_cols(got["mla_w_ukv"]), w_out=_rows(got["mla_w_out"]))


def _layer_grad_slots(kind, big):
    if kind == "ffn":
        return {"ffn_w_gate": _uncols(big["w_g"]), "ffn_w_up": _uncols(big["w_u"]),
                "ffn_w_down": _unrows(big["w_down"])}
    if kind == "gdn":
        return {"gdn_w_in": _uncols(big["w_in"]), "gdn_w_out": _unrows(big["w_out"])}
    return {"mla_w_in": _unrows(big["w_in"]), "mla_w_uq": _uncols(_uq_from_kernel_layout(big["w_uq"])),
            "mla_w_ukv": _uncols(big["w_ukv"]), "mla_w_out": _unrows(big["w_out"])}


def _small_weights(tiny, rep):
    prm = jnp.zeros((2, 8, LANES), F32)
    prm = prm.at[:, 0, :GDN_HEADS].set(rep["gdn_a_log"]).at[:, 1, :GDN_HEADS].set(rep["gdn_dt_bias"])
    out = {
        "gdn_conv_wt": jnp.transpose(_gather_rows(tiny["gdn_conv_w"]), (0, 2, 1)),
        "mla_q_norm_g": jnp.transpose(tiny["mla_q_norm_g"], (1, 0, 2)).reshape(2, MLA_Q_RANK),
        "mla_kv_norm_g": jnp.transpose(tiny["mla_kv_norm_g"], (1, 0, 2)).reshape(2, MLA_KV_RANK),
        "gdn_gate_prm": prm,
    }
    for k in ("norm_mix_g", "norm_ffn_g", "gdn_norm_g", "final_norm_g"):
        out[k] = rep[k]
    return out


def _rope_tables(positions):
    inv_freq = ROPE_THETA ** (-jnp.arange(0, MLA_ROPE, 2, dtype=F32) / MLA_ROPE)
    ang = positions.astype(F32)[:, None] * inv_freq
    cos, sin = jnp.cos(ang), jnp.sin(ang)
    reps = LANES // MLA_ROPE
    return jnp.tile(jnp.concatenate([cos, cos], axis=1), (1, reps)), jnp.tile(
        jnp.concatenate([-sin, sin], axis=1), (1, reps))


def kernel(x, c, positions, ada_w, ada_b, norm_mix_g, norm_ffn_g, gdn_w_in, gdn_conv_w, gdn_a_log, gdn_dt_bias, gdn_norm_g, gdn_w_out, mla_w_in, mla_q_norm_g, mla_kv_norm_g, mla_w_uq, mla_w_ukv, mla_w_out, ffn_w_gate, ffn_w_up, ffn_w_down, final_norm_g, loss_target, m_ada_w, m_ada_b, m_norm_mix_g, m_norm_ffn_g, m_gdn_w_in, m_gdn_conv_w, m_gdn_a_log, m_gdn_dt_bias, m_gdn_norm_g, m_gdn_w_out, m_mla_w_in, m_mla_q_norm_g, m_mla_kv_norm_g, m_mla_w_uq, m_mla_w_ukv, m_mla_w_out, m_ffn_w_gate, m_ffn_w_up, m_ffn_w_down, m_final_norm_g, v_ada_w, v_ada_b, v_norm_mix_g, v_norm_ffn_g, v_gdn_w_in, v_gdn_conv_w, v_gdn_a_log, v_gdn_dt_bias, v_gdn_norm_g, v_gdn_w_out, v_mla_w_in, v_mla_q_norm_g, v_mla_kv_norm_g, v_mla_w_uq, v_mla_w_ukv, v_mla_w_out, v_ffn_w_gate, v_ffn_w_up, v_ffn_w_down, v_final_norm_g):
    W = dict(ada_w=ada_w, ada_b=ada_b, norm_mix_g=norm_mix_g, norm_ffn_g=norm_ffn_g, gdn_w_in=gdn_w_in,
             gdn_conv_w=gdn_conv_w, gdn_a_log=gdn_a_log, gdn_dt_bias=gdn_dt_bias, gdn_norm_g=gdn_norm_g,
             gdn_w_out=gdn_w_out, mla_w_in=mla_w_in, mla_q_norm_g=mla_q_norm_g, mla_kv_norm_g=mla_kv_norm_g,
             mla_w_uq=mla_w_uq, mla_w_ukv=mla_w_ukv, mla_w_out=mla_w_out, ffn_w_gate=ffn_w_gate,
             ffn_w_up=ffn_w_up, ffn_w_down=ffn_w_down, final_norm_g=final_norm_g)
    M = dict(ada_w=m_ada_w, ada_b=m_ada_b, norm_mix_g=m_norm_mix_g, norm_ffn_g=m_norm_ffn_g, gdn_w_in=m_gdn_w_in,
             gdn_conv_w=m_gdn_conv_w, gdn_a_log=m_gdn_a_log, gdn_dt_bias=m_gdn_dt_bias, gdn_norm_g=m_gdn_norm_g,
             gdn_w_out=m_gdn_w_out, mla_w_in=m_mla_w_in, mla_q_norm_g=m_mla_q_norm_g,
             mla_kv_norm_g=m_mla_kv_norm_g, mla_w_uq=m_mla_w_uq, mla_w_ukv=m_mla_w_ukv, mla_w_out=m_mla_w_out,
             ffn_w_gate=m_ffn_w_gate, ffn_w_up=m_ffn_w_up, ffn_w_down=m_ffn_w_down, final_norm_g=m_final_norm_g)
    V = dict(ada_w=v_ada_w, ada_b=v_ada_b, norm_mix_g=v_norm_mix_g, norm_ffn_g=v_norm_ffn_g, gdn_w_in=v_gdn_w_in,
             gdn_conv_w=v_gdn_conv_w, gdn_a_log=v_gdn_a_log, gdn_dt_bias=v_gdn_dt_bias, gdn_norm_g=v_gdn_norm_g,
             gdn_w_out=v_gdn_w_out, mla_w_in=v_mla_w_in, mla_q_norm_g=v_mla_q_norm_g,
             mla_kv_norm_g=v_mla_kv_norm_g, mla_w_uq=v_mla_w_uq, mla_w_ukv=v_mla_w_ukv, mla_w_out=v_mla_w_out,
             ffn_w_gate=v_ffn_w_gate, ffn_w_up=v_ffn_w_up, ffn_w_down=v_ffn_w_down, final_norm_g=v_final_norm_g)
    me = 4 * lax.axis_index("x") + 2 * lax.axis_index("y") + lax.axis_index("c")
    t = x.shape[1]
    wc = ada_w.shape[-1]

    groups = [(layer, kind) for layer in range(DEPTH) for kind in ("mix", "ffn")]

    def group_srcs(i):
        layer, kind = groups[i]
        return [W[k][_layer_index(k, layer)].astype(BF16) for k in _group_names(layer, kind)]

    tiny_shapes = [c.shape, gdn_conv_w.shape, mla_q_norm_g.shape, mla_kv_norm_g.shape]
    first = _gather_two_level([_pack([c, gdn_conv_w, mla_q_norm_g, mla_kv_norm_g])] + group_srcs(0),
                              name="gather_first")
    tiny_g = first[0]
    c_g, conv_g, qn_g, kvn_g = _unpack(tiny_g, tiny_shapes, lead=(N_DEV,))
    c_all = c_g.reshape(N_DEV, D_MODEL)
    rep = _small_weights({"gdn_conv_w": conv_g, "mla_q_norm_g": qn_g, "mla_kv_norm_g": kvn_g}, W)

    def start_group(i, dep):
        layer, kind = groups[i]
        return _exchange_start(group_srcs(i), scatter=False, name=f"gather_start_{kind}_l{layer}", dep=dep)


    b_cols = lax.dynamic_slice_in_dim(ada_b, me * wc, wc, axis=1).reshape(DEPTH, 1, wc)
    mod_part = _ada_mod(c_all, ada_w, b_cols, name="ada_mod")
    (mod_g,) = _exchange([mod_part], scatter=False, name="gather_mod")
    mod_mine = lax.dynamic_index_in_dim(mod_g, me, axis=2, keepdims=False)
    mod = jnp.transpose(mod_mine, (1, 0, 2)).reshape(DEPTH, N_MOD, D_MODEL)
    gather = {1: start_group(1, mod_g)}

    def get_weights(layer, kind, after):
        i = groups.index((layer, kind))
        names = _group_names(layer, kind)
        if i == 0:
            return _group_weights(layer, kind, dict(zip(names, first[1:])), gather[1][4][0, 0].astype(BF16))
        srcs, lands = _exchange_wait(gather[i], after, scatter=False, name=f"gather_wait_{kind}_l{layer}")
        zero = jnp.zeros((), BF16)
        if i + 1 < len(groups):
            gather[i + 1] = start_group(i + 1, lands[0])
            zero = gather[i + 1][4][0, 0].astype(BF16)
        got = {k: lax.dynamic_update_index_in_dim(z, s, me, 0) for k, s, z in zip(names, srcs, lands)}
        return _group_weights(layer, kind, got, zero)

    scatter = []

    def put_grads(layer, kind, big):
        slots = _layer_grad_slots(kind, big)
        started = _exchange_start(list(slots.values()), scatter=True, name=f"scatter_start_{kind}_l{layer}")
        scatter.append((layer, kind, list(slots.keys()), started))
        return started[4]

    cos_t, sin_t = _rope_tables(positions[0])
    loss, dx, dmod, g = _local_step(x[0], loss_target[0], mod, cos_t, sin_t, rep, get_weights, put_grads)

    parts = {k: [None] * W[k].shape[0] for k in BIG}
    res = {}

    def wait_group(entry, after):
        layer, kind, names, started = entry
        srcs, lands = _exchange_wait(started, after, scatter=True, name=f"scatter_wait_{kind}_l{layer}")
        for k, s, z in zip(names, srcs, lands):
            own = lax.dynamic_index_in_dim(s, me, 0, keepdims=False)
            parts[k][_layer_index(k, layer)] = lax.dynamic_update_index_in_dim(z, own, me, 0)

    for entry in scatter[:-1]:
        wait_group(entry, dx)
    early = [k for k in BIG if k not in scatter[-1][2]]
    for k in early:
        res[k] = _adamw(parts[k], W[k], M[k], V[k], name=f"adamw_{k}")
    loss, dmod, done = lax.optimization_barrier((loss, dmod, [res[k] for k in early]))
    for k, r in zip(early, done):
        res[k] = r

    small_local = [dmod.reshape(DEPTH, N_MOD * D_MODEL), g["norm_mix_g"], g["norm_ffn_g"],
                   jnp.transpose(g["gdn_conv_wt"], (0, 2, 1)), g["gdn_a_log"], g["gdn_dt_bias"], g["gdn_norm_g"],
                   g["mla_q_norm_g"], g["mla_kv_norm_g"], g["final_norm_g"], loss.reshape(1)]
    small_shapes = [a.shape for a in small_local]
    (small_g,) = _exchange([_pack(small_local)], scatter=False, name="gather_small_grads")
    small_sum = _unpack(_sum_parts(small_g, name="sum_small_grads"), small_shapes)
    loss = small_sum[-1][0]
    dmod_all = _unpack(small_g, small_shapes[:1], lead=(N_DEV,))[0]
    sg = dict(zip(SMALL, small_sum))
    wait_group(scatter[-1], small_g)
    sg["gdn_conv_w"] = lax.dynamic_slice_in_dim(sg["gdn_conv_w"], me * gdn_conv_w.shape[1], gdn_conv_w.shape[1], 1)
    sg["mla_q_norm_g"] = lax.dynamic_slice_in_dim(sg["mla_q_norm_g"], me * mla_q_norm_g.shape[1],
                                                  mla_q_norm_g.shape[1], 1)
    sg["mla_kv_norm_g"] = lax.dynamic_slice_in_dim(sg["mla_kv_norm_g"], me * mla_kv_norm_g.shape[1],
                                                   mla_kv_norm_g.shape[1], 1)

    dmod_cols = jnp.transpose(lax.dynamic_slice_in_dim(dmod_all, me * wc, wc, axis=2), (1, 0, 2))
    res["ada_w"] = _ada_grad_adamw(c_all, dmod_cols, ada_w, m_ada_w, v_ada_w, name="ada_w_grad_adamw")
    for k in BIG:
        if k not in early:
            res[k] = _adamw(parts[k], W[k], M[k], V[k], name=f"adamw_{k}")
    shapes = [W[k].shape for k in SMALL]
    packed = [_pack([d[k] for k in SMALL]) for d in (sg, W, M, V)]
    outs = _adamw([packed[0][None]], packed[1][None], packed[2][None], packed[3][None], name="adamw_small")
    unpacked = [_unpack(o[0], shapes) for o in outs]
    for i, k in enumerate(SMALL):
        res[k] = tuple(u[i] for u in unpacked)

    return (loss, dx[None], *[res[k][0] for k in WEIGHTS], *[res[k][1] for k in WEIGHTS],
            *[res[k][2] for k in WEIGHTS], *[res[k][3] for k in WEIGHTS])
```

```python
import functools
import math

import jax
import jax.numpy as jnp
from jax import lax
from jax.experimental import pallas as pl
from jax.experimental.pallas import tpu as pltpu

F32 = jnp.float32
BF16 = jnp.bfloat16
MXU_DTYPE = jnp.bfloat16

N_DEV = 8
D_MODEL = 1024
DEPTH = 4
GDN_HEADS = 8
GDN_HEAD_DIM = 128
GDN_KEY_DIM = GDN_HEADS * GDN_HEAD_DIM
GDN_CHUNK = 64
GDN_HEAD_BATCH = 8
GDN_CONV = 4
GDN_PREP_HEADS = 2
GDN_MAIN = 4 * GDN_KEY_DIM
MLA_HEADS = 8
MLA_NOPE = 128
MLA_ROPE = 64
MLA_V = 128
MLA_Q_RANK = 384
MLA_KV_RANK = 256
MLA_IN = MLA_Q_RANK + MLA_KV_RANK + MLA_ROPE
MLA_QK = MLA_NOPE + MLA_ROPE
ROPE_THETA = 10000.0
D_FF = 2816
N_MOD = 6
EPS = 1e-6
LANES = 128
VMEM_LIMIT = 48 * 1024 * 1024

ADAM_LR = 0.001
ADAM_B1 = 0.9
ADAM_B2 = 0.999
ADAM_EPS = 1e-08
ADAM_WD = 0.01
ADAM_STEP = 10
ADAM_BC1 = 1.0 - ADAM_B1 ** ADAM_STEP
ADAM_BC2 = 1.0 - ADAM_B2 ** ADAM_STEP

NN = (((1,), (0,)), ((), ()))
NT = (((1,), (1,)), ((), ()))
TN = (((0,), (0,)), ((), ()))
NEG = -1e30


def _dotb(a, b, dims):
    return lax.dot_general(a.astype(MXU_DTYPE), b.astype(MXU_DTYPE), dims, preferred_element_type=F32)


def _split(a):
    hi = a.astype(BF16)
    return hi, (a - hi.astype(F32)).astype(BF16)


def _dotf(a, b, dims):
    ah, al = _split(a)
    bh, bl = _split(b)
    dot = lambda u, v: lax.dot_general(u, v, dims, preferred_element_type=F32)
    return dot(ah, bh) + (dot(ah, bl) + dot(al, bh))


def _params(*sem):
    return pltpu.CompilerParams(dimension_semantics=sem, vmem_limit_bytes=VMEM_LIMIT)


def _pick(n, pref, mult=LANES):
    best = None
    t = mult
    while t <= min(n, pref):
        if n % t == 0:
            best = t
        t += mult
    return best if best is not None else n


def _sigmoid(z):
    return 1.0 / (1.0 + jnp.exp(-z))


def _exchange(arrays, *, scatter, name):
    n = len(arrays)
    out_shape = tuple(
        jax.ShapeDtypeStruct(a.shape if scatter else (N_DEV,) + a.shape, a.dtype) for a in arrays)

    def body(*refs):
        ins, outs = refs[:n], refs[n:2 * n]
        send_sems, recv_sems, local_sems = refs[2 * n:]
        x, y, c = lax.axis_index("x"), lax.axis_index("y"), lax.axis_index("c")
        me = 4 * x + 2 * y + c
        copies = []
        for k in range(n):
            src_own = ins[k].at[me] if scatter else ins[k]
            own = pltpu.make_async_copy(src_own, outs[k].at[me], local_sems.at[k])
            own.start()
            copies.append(own)
        sends = []
        for p in range(1, N_DEV):
            px, py, pc = x ^ ((p >> 2) & 1), y ^ ((p >> 1) & 1), c ^ (p & 1)
            peer = 4 * px + 2 * py + pc
            for k in range(n):
                cp = pltpu.make_async_remote_copy(
                    src_ref=ins[k].at[peer] if scatter else ins[k],
                    dst_ref=outs[k].at[me],
                    send_sem=send_sems.at[k, p - 1],
                    recv_sem=recv_sems.at[k, p - 1],
                    device_id=(px, py, pc),
                    device_id_type=pl.DeviceIdType.MESH,
                )
                cp.start()
                sends.append((cp, k, peer, p))
        for cp, k, peer, p in sends:
            pltpu.make_async_remote_copy(
                src_ref=ins[k].at[peer] if scatter else ins[k],
                dst_ref=outs[k].at[peer],
                send_sem=send_sems.at[k, p - 1],
                recv_sem=recv_sems.at[k, p - 1],
                device_id=(x, y, c),
                device_id_type=pl.DeviceIdType.MESH,
            ).wait_recv()
        for cp, _, _, _ in sends:
            cp.wait_send()
        for own in copies:
            own.wait()

    any_spec = pl.BlockSpec(memory_space=pl.ANY)
    outs = pl.pallas_call(
        body,
        name=name,
        out_shape=out_shape,
        in_specs=[any_spec] * n,
        out_specs=tuple([any_spec] * n),
        scratch_shapes=[
            pltpu.SemaphoreType.DMA((n, N_DEV - 1)),
            pltpu.SemaphoreType.DMA((n, N_DEV - 1)),
            pltpu.SemaphoreType.DMA((n,)),
        ],
        compiler_params=pltpu.CompilerParams(has_side_effects=True),
    )(*arrays)
    return list(outs)


def _gather_two_level(arrays, *, name):
    n = len(arrays)
    out_shape = tuple(jax.ShapeDtypeStruct((N_DEV,) + a.shape, a.dtype) for a in arrays)

    def body(*refs):
        ins, outs = refs[:n], refs[n:2 * n]
        send_sems, recv_sems, local_sems = refs[2 * n:]
        x, y, c = lax.axis_index("x"), lax.axis_index("y"), lax.axis_index("c")
        me = 4 * x + 2 * y + c
        sibling = (x, y, 1 - c)
        chips = [(1 - x, y), (x, 1 - y), (1 - x, 1 - y)]

        def slot(px, py, pc):
            return 4 * px + 2 * py + pc

        def copy(k, q, block, to, src=None):
            return pltpu.make_async_remote_copy(
                src_ref=outs[k].at[slot(*block)] if src is None else src,
                dst_ref=outs[k].at[slot(*block)],
                send_sem=send_sems.at[k, q], recv_sem=recv_sems.at[k, q],
                device_id=to, device_id_type=pl.DeviceIdType.MESH)

        own = [pltpu.make_async_copy(ins[k], outs[k].at[me], local_sems.at[k]) for k in range(n)]
        for cp in own:
            cp.start()
        first = []
        for k in range(n):
            first.append(copy(k, 0, (x, y, c), sibling, src=ins[k]))
            first += [copy(k, 1 + j, (x, y, c), (*chip, c), src=ins[k]) for j, chip in enumerate(chips)]
        for cp in first:
            cp.start()
        passed = []
        for j, chip in enumerate(chips):
            for k in range(n):
                copy(k, 1 + j, (*chip, c), (x, y, c)).wait_recv()
                fwd = copy(k, 4 + j, (*chip, c), sibling)
                fwd.start()
                passed.append(fwd)
        for k in range(n):
            copy(k, 0, sibling, (x, y, c)).wait_recv()
            for j, chip in enumerate(chips):
                copy(k, 4 + j, (*chip, 1 - c), (x, y, c)).wait_recv()
        for cp in first + passed:
            cp.wait_send()
        for cp in own:
            cp.wait()

    any_spec = pl.BlockSpec(memory_space=pl.ANY)
    outs = pl.pallas_call(
        body, name=name, out_shape=out_shape, in_specs=[any_spec] * n, out_specs=tuple([any_spec] * n),
        scratch_shapes=[pltpu.SemaphoreType.DMA((n, N_DEV - 1)), pltpu.SemaphoreType.DMA((n, N_DEV - 1)),
                        pltpu.SemaphoreType.DMA((n,))],
        compiler_params=pltpu.CompilerParams(has_side_effects=True),
    )(*arrays)
    return list(outs)


def _peer(x, y, c, p):
    return x ^ ((p >> 2) & 1), y ^ ((p >> 1) & 1), c ^ (p & 1)


def _exchange_start(arrays, *, scatter, name, dep=None):
    n = len(arrays)
    deps = [] if dep is None else [dep]
    lands = [lax.empty(a.shape if scatter else (N_DEV,) + a.shape, a.dtype) for a in arrays]

    def body(*refs):
        ins, zones = refs[:n], refs[n:2 * n]
        send_sems, recv_sems = refs[2 * n + len(deps)], refs[2 * n + len(deps) + 1]
        token = refs[-1]
        x, y, c = lax.axis_index("x"), lax.axis_index("y"), lax.axis_index("c")
        me = 4 * x + 2 * y + c
        for p in range(1, N_DEV):
            px, py, pc = _peer(x, y, c, p)
            for k in range(n):
                pltpu.make_async_remote_copy(
                    src_ref=ins[k].at[4 * px + 2 * py + pc] if scatter else ins[k],
                    dst_ref=zones[k].at[me],
                    send_sem=send_sems.at[k * (N_DEV - 1) + p - 1],
                    recv_sem=recv_sems.at[k * (N_DEV - 1) + p - 1],
                    device_id=(px, py, pc),
                    device_id_type=pl.DeviceIdType.MESH,
                ).start()
        token[...] = jnp.zeros_like(token)

    hbm = pl.BlockSpec(memory_space=pltpu.HBM)
    sem = pl.BlockSpec(memory_space=pltpu.SEMAPHORE)
    outs = pl.pallas_call(
        body,
        name=name,
        out_shape=(pltpu.SemaphoreType.DMA((n * (N_DEV - 1),)), pltpu.SemaphoreType.DMA((n * (N_DEV - 1),)),
                   *[pltpu.HBM(a.shape, a.dtype) for a in arrays], *[pltpu.HBM(z.shape, z.dtype) for z in lands],
                   jax.ShapeDtypeStruct((8, LANES), F32)),
        in_specs=[hbm] * (2 * n) + [pl.BlockSpec(memory_space=pl.ANY)] * len(deps),
        out_specs=(sem, sem, *[hbm] * (2 * n), pl.BlockSpec(memory_space=pltpu.VMEM)),
        input_output_aliases={k: 2 + k for k in range(2 * n)},
        compiler_params=pltpu.CompilerParams(has_side_effects=pltpu.SideEffectType.DATAFLOW_SIDE_EFFECTING),
    )(*[pltpu.with_memory_space_constraint(a, pltpu.HBM) for a in arrays],
      *[pltpu.with_memory_space_constraint(z, pltpu.HBM) for z in lands], *deps)
    return outs[0], outs[1], list(outs[2:2 + n]), list(outs[2 + n:2 + 2 * n]), outs[-1]


def _exchange_wait(started, after, *, scatter, name):
    send_sems, recv_sems, srcs, lands, _ = started
    n = len(srcs)

    def body(*refs):
        ins, zones = refs[:n], refs[n:2 * n]
        s_sems, r_sems = refs[2 * n], refs[2 * n + 1]
        x, y, c = lax.axis_index("x"), lax.axis_index("y"), lax.axis_index("c")
        for p in range(1, N_DEV):
            px, py, pc = _peer(x, y, c, p)
            peer = 4 * px + 2 * py + pc
            for k in range(n):
                cp = pltpu.make_async_remote_copy(
                    src_ref=ins[k].at[peer] if scatter else ins[k],
                    dst_ref=zones[k].at[peer],
                    send_sem=s_sems.at[k * (N_DEV - 1) + p - 1],
                    recv_sem=r_sems.at[k * (N_DEV - 1) + p - 1],
                    device_id=(px, py, pc),
                    device_id_type=pl.DeviceIdType.MESH,
                )
                cp.wait_send()
                cp.wait_recv()

    hbm = pl.BlockSpec(memory_space=pltpu.HBM)
    sem = pl.BlockSpec(memory_space=pltpu.SEMAPHORE)
    outs = pl.pallas_call(
        body,
        name=name,
        out_shape=tuple(pltpu.HBM(a.shape, a.dtype) for a in srcs + lands),
        in_specs=[hbm] * (2 * n) + [sem, sem, pl.BlockSpec(memory_space=pl.ANY)],
        out_specs=tuple([hbm] * (2 * n)),
        input_output_aliases={k: k for k in range(2 * n)},
        compiler_params=pltpu.CompilerParams(has_side_effects=pltpu.SideEffectType.DATAFLOW_SIDE_EFFECTING),
    )(*srcs, *lands, send_sems, recv_sems, after)
    return list(outs[:n]), list(outs[n:])


def _mm(a, b, *, mode, out_dtype, name, add=None, tm=512, tn=512, b_rows=None):
    rows_b = b.shape[0] if b_rows is None else b_rows
    if mode == "nn":
        (m, kd), nd = a.shape, b.shape[1]
        assert kd == rows_b
    elif mode == "nt":
        (m, kd), nd = a.shape, rows_b
    else:
        (kd, m), nd = a.shape, b.shape[1]
    tm = _pick(m, tm, LANES if mode == "tn" else 16)
    tn = _pick(nd, tn)
    dims = {"nn": NN, "nt": NT, "tn": TN}[mode]
    ni, nj = m // tm, nd // tn
    a_bytes, b_bytes = a.size * a.dtype.itemsize, b.size * b.dtype.itemsize
    i_outer = a_bytes + ni * b_bytes <= b_bytes + nj * a_bytes
    ij = (lambda g0, g1: (g0, g1)) if i_outer else (lambda g0, g1: (g1, g0))
    a_spec = (pl.BlockSpec((kd, tm), lambda g0, g1: (0, ij(g0, g1)[0])) if mode == "tn"
              else pl.BlockSpec((tm, kd), lambda g0, g1: (ij(g0, g1)[0], 0)))
    b_spec = (pl.BlockSpec((tn, kd), lambda g0, g1: (ij(g0, g1)[1], 0)) if mode == "nt"
              else pl.BlockSpec((kd, tn), lambda g0, g1: (0, ij(g0, g1)[1])))
    o_spec = pl.BlockSpec((tm, tn), lambda g0, g1: ij(g0, g1))
    has_add = add is not None

    def body(*refs):
        a_ref, b_ref = refs[0], refs[1]
        o_ref = refs[-1]
        acc = _dotb(a_ref[...], b_ref[...], dims)
        if has_add:
            acc = acc + refs[2][...].astype(F32)
        o_ref[...] = acc.astype(o_ref.dtype)

    ins = [a, b] + ([add] if has_add else [])
    specs = [a_spec, b_spec] + ([o_spec] if has_add else [])
    return pl.pallas_call(
        body, name=name, grid=(ni, nj) if i_outer else (nj, ni), in_specs=specs, out_specs=o_spec,
        out_shape=jax.ShapeDtypeStruct((m, nd), out_dtype),
        compiler_params=_params("parallel", "parallel"),
    )(*ins)


def _mm_resid(a, b, x, gate, *, name, tm=256, tn=1024):
    m, kd = a.shape
    nd = b.shape[1]
    tm = _pick(m, tm, 16)
    tn = _pick(nd, tn)
    o_spec = pl.BlockSpec((tm, tn), lambda i, j: (i, j))

    def body(a_ref, b_ref, x_ref, g_ref, xo_ref, y_ref):
        y = _dotb(a_ref[...], b_ref[...], NN)
        y_ref[...] = y
        xo_ref[...] = x_ref[...] + g_ref[...] * y

    return pl.pallas_call(
        body, name=name, grid=(m // tm, nd // tn),
        in_specs=[pl.BlockSpec((tm, kd), lambda i, j: (i, 0)), pl.BlockSpec((kd, tn), lambda i, j: (0, j)),
                  o_spec, pl.BlockSpec((1, tn), lambda i, j: (0, j))],
        out_specs=(o_spec, o_spec),
        out_shape=(jax.ShapeDtypeStruct((m, nd), F32), jax.ShapeDtypeStruct((m, nd), F32)),
        compiler_params=_params("parallel", "parallel"),
    )(a, b, x, gate)


ROWS = 256


def _row_spec(width, rows=ROWS):
    return pl.BlockSpec((rows, width), lambda i: (i, 0))


def _const_spec(shape):
    return pl.BlockSpec(shape, lambda i: tuple(0 for _ in shape))


def _adaln_fwd(x, g, scale, shift, *, name):
    t, d = x.shape

    def body(x_ref, g_ref, sc_ref, sh_ref, h_ref):
        xv = x_ref[...]
        r = lax.rsqrt(jnp.mean(xv * xv, axis=-1, keepdims=True) + EPS)
        h_ref[...] = (xv * r * g_ref[...] * (1.0 + sc_ref[...]) + sh_ref[...]).astype(h_ref.dtype)

    return pl.pallas_call(
        body, name=name, grid=(t // ROWS,),
        in_specs=[_row_spec(d), _const_spec((1, d)), _const_spec((1, d)), _const_spec((1, d))],
        out_specs=_row_spec(d), out_shape=jax.ShapeDtypeStruct((t, d), BF16),
        compiler_params=_params("parallel"),
    )(x, g, scale, shift)


def _adaln_bwd(x, g, scale, shift, dh, dres, dep, *, name):
    t, d = x.shape

    def body(x_ref, g_ref, sc_ref, sh_ref, dh_ref, dr_ref, dep_ref, dx_ref, st_ref):
        @pl.when(pl.program_id(0) == 0)
        def _():
            st_ref[...] = jnp.zeros_like(st_ref)

        xv = x_ref[...]
        dhv = dh_ref[...].astype(F32)
        gv = g_ref[...]
        r = lax.rsqrt(jnp.mean(xv * xv, axis=-1, keepdims=True) + EPS)
        xh = xv * r
        nv = xh * gv
        dn = dhv * (1.0 + sc_ref[...])
        dxh = dn * gv
        dx_ref[...] = dr_ref[...] + r * (dxh - xh * jnp.mean(dxh * xh, axis=-1, keepdims=True))
        st_ref[0:1, :] += jnp.sum(dn * xh, axis=0, keepdims=True)
        st_ref[1:2, :] += jnp.sum(dhv * nv, axis=0, keepdims=True)
        st_ref[2:3, :] += jnp.sum(dhv, axis=0, keepdims=True)

    return pl.pallas_call(
        body, name=name, grid=(t // ROWS,),
        in_specs=[_row_spec(d), _const_spec((1, d)), _const_spec((1, d)), _const_spec((1, d)),
                  _row_spec(d), _row_spec(d), _const_spec((8, LANES))],
        out_specs=(_row_spec(d), _const_spec((8, d))),
        out_shape=(jax.ShapeDtypeStruct((t, d), F32), jax.ShapeDtypeStruct((8, d), F32)),
        compiler_params=_params("arbitrary"),
    )(x, g, scale, shift, dh, dres, dep)


def _gate_bwd(dxo, y, gate, dep, *, name):
    t, d = dxo.shape

    def body(dx_ref, y_ref, g_ref, dep_ref, dy_ref, st_ref):
        @pl.when(pl.program_id(0) == 0)
        def _():
            st_ref[...] = jnp.zeros_like(st_ref)

        dxv = dx_ref[...]
        dy_ref[...] = (dxv * g_ref[...]).astype(dy_ref.dtype)
        st_ref[0:1, :] += jnp.sum(dxv * y_ref[...], axis=0, keepdims=True)

    return pl.pallas_call(
        body, name=name, grid=(t // ROWS,),
        in_specs=[_row_spec(d), _row_spec(d), _const_spec((1, d)), _const_spec((8, LANES))],
        out_specs=(_row_spec(d), _const_spec((8, d))),
        out_shape=(jax.ShapeDtypeStruct((t, d), BF16), jax.ShapeDtypeStruct((8, d), F32)),
        compiler_params=_params("arbitrary"),
    )(dxo, y, gate, dep)


def _loss_head(x, g, target, *, name):
    t, d = x.shape

    def body(x_ref, g_ref, t_ref, dx_ref, st_ref, ls_ref):
        @pl.when(pl.program_id(0) == 0)
        def _():
            st_ref[...] = jnp.zeros_like(st_ref)
            ls_ref[...] = jnp.zeros_like(ls_ref)

        xv = x_ref[...]
        gv = g_ref[...]
        r = lax.rsqrt(jnp.mean(xv * xv, axis=-1, keepdims=True) + EPS)
        xh = xv * r
        err = xh * gv - t_ref[...]
        ls_ref[...] += 0.5 * jnp.sum(jnp.mean(err * err, axis=-1, keepdims=True))
        dy = err * (1.0 / d)
        dxh = dy * gv
        dx_ref[...] = r * (dxh - xh * jnp.mean(dxh * xh, axis=-1, keepdims=True))
        st_ref[0:1, :] += jnp.sum(dy * xh, axis=0, keepdims=True)

    return pl.pallas_call(
        body, name=name, grid=(t // ROWS,),
        in_specs=[_row_spec(d), _const_spec((1, d)), _row_spec(d)],
        out_specs=(_row_spec(d), _const_spec((8, d)), _const_spec((8, LANES))),
        out_shape=(jax.ShapeDtypeStruct((t, d), F32), jax.ShapeDtypeStruct((8, d), F32),
                   jax.ShapeDtypeStruct((8, LANES), F32)),
        compiler_params=_params("arbitrary"),
    )(x, g, target)


FFN_BLOCK = D_FF // 2


def _ffn_gu_fwd(h, wg, wu, *, name):
    t, d = h.shape
    tn = FFN_BLOCK

    def body(h_ref, wg_ref, wu_ref, s_ref, a_ref, b_ref):
        hv = h_ref[...]
        a = _dotb(hv, wg_ref[...], NT)
        b = _dotb(hv, wu_ref[...], NT)
        s_ref[...] = (a * _sigmoid(a) * b).astype(s_ref.dtype)
        a_ref[...] = a.astype(a_ref.dtype)
        b_ref[...] = b.astype(b_ref.dtype)

    w_spec = pl.BlockSpec((tn, d), lambda j, i: (j, 0))
    o_spec = pl.BlockSpec((ROWS, tn), lambda j, i: (i, j))
    return pl.pallas_call(
        body, name=name, grid=(D_FF // tn, t // ROWS),
        in_specs=[pl.BlockSpec((ROWS, d), lambda j, i: (i, 0)), w_spec, w_spec],
        out_specs=(o_spec, o_spec, o_spec),
        out_shape=(jax.ShapeDtypeStruct((t, D_FF), BF16),) * 3,
        compiler_params=_params("parallel", "parallel"),
    )(h, wg, wu)


def _ffn_down_dx(dy, w_down, a, b, *, name):
    t, d = dy.shape
    tn = FFN_BLOCK

    def body(dy_ref, w_ref, a_ref, b_ref, da_ref, db_ref):
        ds = _dotb(dy_ref[...], w_ref[...], NT)
        av = a_ref[...].astype(F32)
        sg = _sigmoid(av)
        da_ref[...] = (ds * b_ref[...].astype(F32) * sg * (1.0 + av * (1.0 - sg))).astype(da_ref.dtype)
        db_ref[...] = (ds * av * sg).astype(db_ref.dtype)

    o_spec = pl.BlockSpec((ROWS, tn), lambda j, i: (i, j))
    return pl.pallas_call(
        body, name=name, grid=(D_FF // tn, t // ROWS),
        in_specs=[pl.BlockSpec((ROWS, d), lambda j, i: (i, 0)), pl.BlockSpec((tn, d), lambda j, i: (j, 0)),
                  o_spec, o_spec],
        out_specs=(o_spec, o_spec),
        out_shape=(jax.ShapeDtypeStruct((t, D_FF), BF16),) * 2,
        compiler_params=_params("parallel", "parallel"),
    )(dy, w_down, a, b)


def _shift_rows(v, s, rows):
    if s == 0:
        return v
    return jnp.where(rows >= s, pltpu.roll(v, s, 0), 0.0)


def _unshift_rows(v, s, rows, t):
    if s == 0:
        return v
    return jnp.where(rows < t - s, pltpu.roll(v, t - s, 0), 0.0)


def _conv_silu(x, w, rows):
    z = w[GDN_CONV - 1:GDN_CONV, :] * x
    for j in range(GDN_CONV - 1):
        z = z + w[j:j + 1, :] * _shift_rows(x, GDN_CONV - 1 - j, rows)
    sg = _sigmoid(z)
    return z, sg, z * sg


def _gdn_prep_fwd(proj, conv_wt, *, name):
    t = proj.shape[0]
    nh = GDN_HEADS

    hp = GDN_PREP_HEADS
    wd = hp * LANES

    def body(x_ref, w_ref, y_ref):
        j = pl.program_id(0) * hp
        rows = lax.broadcasted_iota(jnp.int32, (t, LANES), 0)
        qscale = jnp.where(j < nh, GDN_HEAD_DIM ** -0.5, 1.0)
        for i in range(hp):
            sl = slice(i * LANES, (i + 1) * LANES)
            _, _, s = _conv_silu(x_ref[:, sl], w_ref[:, sl], rows)
            rs = lax.rsqrt(jnp.sum(s * s, axis=-1, keepdims=True) + EPS)
            y_ref[:, sl] = jnp.where(j < 2 * nh, s * rs * qscale, s)

    return pl.pallas_call(
        body, name=name, grid=(3 * nh // hp,),
        in_specs=[pl.BlockSpec((t, wd), lambda j: (0, j)), pl.BlockSpec((GDN_CONV, wd), lambda j: (0, j))],
        out_specs=pl.BlockSpec((t, wd), lambda j: (0, j)),
        out_shape=jax.ShapeDtypeStruct((t, 3 * GDN_KEY_DIM), F32),
        compiler_params=_params("parallel"),
    )(proj, conv_wt)


def _gdn_prep_bwd(proj, conv_wt, dy, *, name):
    t = proj.shape[0]
    nh = GDN_HEADS

    hp = GDN_PREP_HEADS
    wd = hp * LANES
    per_seg = nh // hp

    def body(x_ref, w_ref, dy_ref, dx_ref, dw_ref):
        j = pl.program_id(0) * hp
        rows = lax.broadcasted_iota(jnp.int32, (t, LANES), 0)
        qscale = jnp.where(j < nh, GDN_HEAD_DIM ** -0.5, 1.0)
        for i in range(hp):
            sl = slice(i * LANES, (i + 1) * LANES)
            x = x_ref[:, sl]
            w = w_ref[:, sl]
            z, sg, s = _conv_silu(x, w, rows)
            rs = lax.rsqrt(jnp.sum(s * s, axis=-1, keepdims=True) + EPS)
            dyv = dy_ref[:, sl]
            nv = s * rs
            de = dyv * qscale
            ds_qk = rs * (de - nv * jnp.sum(de * nv, axis=-1, keepdims=True))
            ds = jnp.where(j < 2 * nh, ds_qk, dyv)
            dz = ds * sg * (1.0 + z * (1.0 - sg))
            dx = w[GDN_CONV - 1:GDN_CONV, :] * dz
            dw_ref[GDN_CONV - 1:GDN_CONV, sl] = jnp.sum(dz * x, axis=0, keepdims=True)
            for k in range(GDN_CONV - 1):
                sh = GDN_CONV - 1 - k
                dx = dx + w[k:k + 1, :] * _unshift_rows(dz, sh, rows, t)
                dw_ref[k:k + 1, sl] = jnp.sum(dz * _shift_rows(x, sh, rows), axis=0, keepdims=True)
            dx_ref[:, sl] = dx.astype(dx_ref.dtype)

    return pl.pallas_call(
        body, name=name, grid=(3 * nh // hp,),
        in_specs=[pl.BlockSpec((t, wd), lambda j: (0, j)), pl.BlockSpec((GDN_CONV, wd), lambda j: (0, j)),
                  pl.BlockSpec((None, t, wd), lambda j: (j // per_seg, 0, j % per_seg))],
        out_specs=(pl.BlockSpec((t, wd), lambda j: (0, j)), pl.BlockSpec((GDN_CONV, wd), lambda j: (0, j))),
        out_shape=(jax.ShapeDtypeStruct((t, 3 * GDN_KEY_DIM), BF16),
                   jax.ShapeDtypeStruct((GDN_CONV, 3 * GDN_KEY_DIM), F32)),
        compiler_params=_params("parallel"),
    )(proj, conv_wt, dy)


def _softplus(z):
    return jnp.maximum(z, 0.0) + jnp.log(1.0 + jnp.exp(-jnp.abs(z)))


def _gdn_gate_fwd(ab, prm, *, name):
    t = ab.shape[0]

    def body(ab_ref, p_ref, o_ref):
        v = ab_ref[...]
        lane = lax.broadcasted_iota(jnp.int32, v.shape, 1)
        g = -jnp.exp(p_ref[0:1, :]) * _softplus(v + p_ref[1:2, :])
        o_ref[...] = jnp.where(lane < GDN_HEADS, g, jnp.where(lane < 2 * GDN_HEADS, _sigmoid(v), 0.0))

    return pl.pallas_call(
        body, name=name, grid=(t // ROWS,),
        in_specs=[_row_spec(LANES), _const_spec((8, LANES))], out_specs=_row_spec(LANES),
        out_shape=jax.ShapeDtypeStruct((t, LANES), F32), compiler_params=_params("parallel"),
    )(ab, prm)


def _gdn_gate_bwd(ab, prm, dgb, *, name):
    t = ab.shape[0]

    def body(ab_ref, p_ref, d_ref, o_ref, st_ref):
        @pl.when(pl.program_id(0) == 0)
        def _():
            st_ref[...] = jnp.zeros_like(st_ref)

        v = ab_ref[...]
        dv = d_ref[...]
        lane = lax.broadcasted_iota(jnp.int32, v.shape, 1)
        is_a = lane < GDN_HEADS
        is_b = jnp.logical_and(lane >= GDN_HEADS, lane < 2 * GDN_HEADS)
        a_exp = jnp.exp(p_ref[0:1, :])
        zz = v + p_ref[1:2, :]
        g = -a_exp * _softplus(zz)
        da = dv * (-a_exp) * _sigmoid(zz)
        beta = _sigmoid(v)
        db = dv * beta * (1.0 - beta)
        o_ref[...] = jnp.where(is_a, da, jnp.where(is_b, db, 0.0)).astype(o_ref.dtype)
        st_ref[0:1, :] += jnp.sum(jnp.where(is_a, dv * g, 0.0), axis=0, keepdims=True)
        st_ref[1:2, :] += jnp.sum(jnp.where(is_a, da, 0.0), axis=0, keepdims=True)

    return pl.pallas_call(
        body, name=name, grid=(t // ROWS,),
        in_specs=[_row_spec(LANES), _const_spec((8, LANES)), _row_spec(LANES)],
        out_specs=(_row_spec(LANES), _const_spec((8, LANES))),
        out_shape=(jax.ShapeDtypeStruct((t, LANES), BF16), jax.ShapeDtypeStruct((8, LANES), F32)),
        compiler_params=_params("arbitrary"),
    )(ab, prm, dgb)


def _gdn_local(qs, ks, vs, gbs, bbs):
    nh = len(qs)
    cs = qs[0].shape[0]
    hs = range(nh)
    r = lax.broadcasted_iota(jnp.int32, (cs, cs), 0)
    c = lax.broadcasted_iota(jnp.int32, (cs, cs), 1)
    tril, strict, eye = r >= c, r > c, r == c
    ident = jnp.where(eye, 1.0, 0.0)
    g_colb = [gbs[h][:, :cs] for h in hs]
    g_row = [jnp.sum(jnp.where(eye, g_colb[h], 0.0), axis=0, keepdims=True) for h in hs]
    gc_col = [jnp.sum(jnp.where(tril, g_row[h], 0.0), axis=1, keepdims=True) for h in hs]
    gc_row = [jnp.sum(jnp.where(r <= c, g_colb[h], 0.0), axis=0, keepdims=True) for h in hs]
    decay = [jnp.exp(jnp.where(tril, gc_col[h] - gc_row[h], NEG)) for h in hs]
    gamma = [jnp.exp(gc_col[h]) for h in hs]
    gcl = [gc_col[h][cs - 1:cs, :] for h in hs]
    gl = [jnp.exp(gcl[h]) for h in hs]
    kdec = [jnp.exp(gcl[h] - gc_col[h]) for h in hs]
    kb = [ks[h] * bbs[h] for h in hs]
    kk = [_dotb(kb[h], ks[h], NT) for h in hs]
    qk = [_dotb(qs[h], ks[h], NT) for h in hs]
    lmat = [jnp.where(strict, kk[h] * decay[h], 0.0) for h in hs]
    pmat = [jnp.where(tril, qk[h] * decay[h], 0.0) for h in hs]
    xm = [-lmat[h] for h in hs]
    tinv = [ident + xm[h] for h in hs]
    for _ in range(int(math.log2(cs)) - 1):
        xm = [_dotf(xm[h], xm[h], NN) for h in hs]
        tinv = [tinv[h] + _dotf(tinv[h], xm[h], NN) for h in hs]
    vb = [vs[h] * bbs[h] for h in hs]
    kg = [kb[h] * gamma[h] for h in hs]
    u = [_dotf(tinv[h], vb[h], NN) for h in hs]
    w = [_dotf(tinv[h], kg[h], NN) for h in hs]
    return [dict(tril=tril, strict=strict, eye=eye, r=r, c=c, decay=decay[h], gamma=gamma[h], gl=gl[h], kdec=kdec[h],
                 kb=kb[h], lmat=lmat[h], tinv=tinv[h], vb=vb[h], kg=kg[h], u=u[h], w=w[h], pmat=pmat[h],
                 qd=qs[h] * gamma[h], kd=ks[h] * kdec[h]) for h in hs]


def _gdn_chunk_fwd(qkv, gbc, bbc, *, name):
    t = qkv.shape[0]
    nh, cs, hd = GDN_HEADS, GDN_CHUNK, GDN_HEAD_DIM
    nc = t // cs

    hb = GDN_HEAD_BATCH
    ng = nh // hb

    def body(q_ref, k_ref, v_ref, g_ref, b_ref, o_ref, st_ref, s_ref):
        @pl.when(pl.program_id(1) == 0)
        def _():
            s_ref[...] = jnp.zeros_like(s_ref)

        sls = [slice(i * hd, (i + 1) * hd) for i in range(hb)]
        hs = range(hb)
        s = [s_ref[i] for i in hs]
        lo = _gdn_local([q_ref[:, sl] for sl in sls], [k_ref[:, sl] for sl in sls], [v_ref[:, sl] for sl in sls],
                        [g_ref[i] for i in hs], [b_ref[i] for i in hs])
        ws = [_dotb(lo[i]["w"], s[i], NN) for i in hs]
        qs = [_dotb(lo[i]["qd"], s[i], NN) for i in hs]
        vn = [lo[i]["u"] - ws[i] for i in hs]
        pv = [_dotb(lo[i]["pmat"], vn[i], NN) for i in hs]
        kv = [_dotb(lo[i]["kd"], vn[i], TN) for i in hs]
        for i, sl in enumerate(sls):
            st_ref[i, 0] = s[i]
            o_ref[:, sl] = qs[i] + pv[i]
            s_ref[i] = s[i] * lo[i]["gl"] + kv[i]

    gspec = pl.BlockSpec((hb, cs, LANES), lambda h, n: (h, n, 0))
    col = lambda off: pl.BlockSpec((cs, hb * hd), lambda h, n: (n, off + h))
    return pl.pallas_call(
        body, name=name, grid=(ng, nc),
        in_specs=[col(0), col(ng), col(2 * ng), gspec, gspec],
        out_specs=(col(0), pl.BlockSpec((hb, 1, hd, hd), lambda h, n: (h, n, 0, 0))),
        out_shape=(jax.ShapeDtypeStruct((t, nh * hd), F32), jax.ShapeDtypeStruct((nh, nc, hd, hd), F32)),
        scratch_shapes=[pltpu.VMEM((hb, hd, hd), F32)],
        compiler_params=_params("parallel", "arbitrary"),
    )(qkv, qkv, qkv, gbc, bbc)


def _gdn_chunk_bwd(qkv, gbc, bbc, states, do, *, name):
    t = qkv.shape[0]
    nh, cs, hd = GDN_HEADS, GDN_CHUNK, GDN_HEAD_DIM
    nc = t // cs

    hb = GDN_HEAD_BATCH
    ng = nh // hb

    def heads_bwd(q, k, v, gb, bb, s, dsn, dov):
        hs = range(len(q))
        lo = _gdn_local(q, k, v, gb, bb)
        tril, strict, eye, r, c = lo[0]["tril"], lo[0]["strict"], lo[0]["eye"], lo[0]["r"], lo[0]["c"]
        rowi = lax.broadcasted_iota(jnp.int32, (cs, 1), 0)
        get = lambda name: [lo[h][name] for h in hs]
        decay, gamma, gl, kdec = get("decay"), get("gamma"), get("gl"), get("kdec")
        kb, tinv, w, pmat, kd, qd = get("kb"), get("tinv"), get("w"), get("pmat"), get("kd"), get("qd")
        ws = [_dotb(w[h], s[h], NN) for h in hs]
        pdo = [_dotb(pmat[h], dov[h], TN) for h in hs]
        kds = [_dotb(kd[h], dsn[h], NN) for h in hs]
        dqd = [_dotb(dov[h], s[h], NT) for h in hs]
        qdo = [_dotb(qd[h], dov[h], TN) for h in hs]
        vn = [lo[h]["u"] - ws[h] for h in hs]
        dvn = [pdo[h] + kds[h] for h in hs]
        dp = [jnp.where(tril, _dotb(dov[h], vn[h], NT), 0.0) for h in hs]
        dkd = [_dotb(vn[h], dsn[h], NT) for h in hs]
        dw = [-_dotb(dvn[h], s[h], NT) for h in hs]
        wdv = [_dotb(w[h], dvn[h], TN) for h in hs]
        dvb = [_dotf(tinv[h], dvn[h], TN) for h in hs]
        dt1 = [_dotf(dvn[h], lo[h]["vb"], NT) for h in hs]
        dkg = [_dotf(tinv[h], dw[h], TN) for h in hs]
        dt2 = [_dotf(dw[h], lo[h]["kg"], NT) for h in hs]
        tdt = [_dotf(tinv[h], dt1[h] + dt2[h], TN) for h in hs]
        dl = [jnp.where(strict, -_dotf(tdt[h], tinv[h], NT), 0.0) for h in hs]
        dkk = [dl[h] * decay[h] for h in hs]
        dqk = [dp[h] * decay[h] for h in hs]
        dkb = [_dotb(dkk[h], k[h], NN) + dkg[h] * gamma[h] for h in hs]
        dk1 = [_dotb(dkk[h], kb[h], TN) for h in hs]
        dk2 = [_dotb(dqk[h], q[h], TN) for h in hs]
        dq1 = [_dotb(dqk[h], k[h], NN) for h in hs]
        out = []
        for h in hs:
            dgl = jnp.sum(jnp.sum(dsn[h] * s[h], axis=1, keepdims=True), axis=0, keepdims=True)
            ds_prev = gl[h] * dsn[h] + qdo[h] - wdv[h]
            dk = dk1[h] + dk2[h] + dkd[h] * kdec[h] + dkb[h] * bb[h]
            dq = dq1[h] + dqd[h] * gamma[h]
            dbeta = jnp.sum(dvb[h] * v[h], axis=-1, keepdims=True) + jnp.sum(dkb[h] * k[h], axis=-1, keepdims=True)
            e = dl[h] * lo[h]["lmat"] + dp[h] * pmat[h]
            e_col = jnp.sum(e, axis=0, keepdims=True)
            dgc = jnp.sum(e, axis=1, keepdims=True) - jnp.sum(jnp.where(eye, e_col, 0.0), axis=1, keepdims=True)
            dgamma = (jnp.sum(dqd[h] * q[h], axis=-1, keepdims=True)
                      + jnp.sum(dkg[h] * kb[h], axis=-1, keepdims=True))
            rk = jnp.sum(dkd[h] * k[h], axis=-1, keepdims=True) * kdec[h]
            dgcl = jnp.sum(rk, axis=0, keepdims=True) + dgl * gl[h]
            dgc = dgc + dgamma * gamma[h] - rk + jnp.where(rowi == cs - 1, dgcl, 0.0)
            dgc_row = jnp.sum(jnp.where(eye, dgc, 0.0), axis=0, keepdims=True)
            dg = jnp.sum(jnp.where(c >= r, dgc_row, 0.0), axis=1, keepdims=True)
            out.append((dq, dk, dvb[h] * bb[h], dbeta, dg, ds_prev))
        return out

    def body(q_ref, k_ref, v_ref, g_ref, b_ref, st_ref, do_ref, d_ref, dg_ref, db_ref, ds_ref):
        @pl.when(pl.program_id(1) == 0)
        def _():
            ds_ref[...] = jnp.zeros_like(ds_ref)

        sls = [slice(i * hd, (i + 1) * hd) for i in range(hb)]
        hs = range(hb)
        outs = heads_bwd([q_ref[:, sl] for sl in sls], [k_ref[:, sl] for sl in sls], [v_ref[:, sl] for sl in sls],
                         [g_ref[i] for i in hs], [b_ref[i] for i in hs], [st_ref[i, 0] for i in hs],
                         [ds_ref[i] for i in hs], [do_ref[:, sl] for sl in sls])
        for i, sl in enumerate(sls):
            dq, dk, dv, dbeta, dg, ds_prev = outs[i]
            d_ref[0, :, sl], d_ref[1, :, sl], d_ref[2, :, sl] = dq, dk, dv
            db_ref[i] = jnp.broadcast_to(dbeta, (cs, LANES))
            dg_ref[i] = jnp.broadcast_to(dg, (cs, LANES))
            ds_ref[i] = ds_prev

    gspec = pl.BlockSpec((hb, cs, LANES), lambda h, n: (h, nc - 1 - n, 0))
    col = lambda off: pl.BlockSpec((cs, hb * hd), lambda h, n: (nc - 1 - n, off + h))
    return pl.pallas_call(
        body, name=name, grid=(ng, nc),
        in_specs=[col(0), col(ng), col(2 * ng), gspec, gspec,
                  pl.BlockSpec((hb, 1, hd, hd), lambda h, n: (h, nc - 1 - n, 0, 0)), col(0)],
        out_specs=(pl.BlockSpec((3, cs, hb * hd), lambda h, n: (0, nc - 1 - n, h)), gspec, gspec),
        out_shape=(jax.ShapeDtypeStruct((3, t, nh * hd), F32),) + (jax.ShapeDtypeStruct((nh, t, LANES), F32),) * 2,
        scratch_shapes=[pltpu.VMEM((hb, hd, hd), F32)],
        compiler_params=_params("parallel", "arbitrary"),
    )(qkv, qkv, qkv, gbc, bbc, states, do)


def _gdn_onorm_fwd(o, proj, norm_g, *, name):
    t = o.shape[0]
    w = GDN_KEY_DIM
    goff = 3 * GDN_KEY_DIM // w

    def body(o_ref, gp_ref, g_ref, y_ref):
        gv = g_ref[...]
        for h in range(GDN_HEADS):
            sl = slice(h * GDN_HEAD_DIM, (h + 1) * GDN_HEAD_DIM)
            oh = o_ref[:, sl]
            gp = gp_ref[:, sl]
            r = lax.rsqrt(jnp.mean(oh * oh, axis=-1, keepdims=True) + EPS)
            y_ref[:, sl] = (oh * r * gv * gp * _sigmoid(gp)).astype(y_ref.dtype)

    return pl.pallas_call(
        body, name=name, grid=(t // ROWS,),
        in_specs=[_row_spec(w), pl.BlockSpec((ROWS, w), lambda i: (i, goff)), _const_spec((1, GDN_HEAD_DIM))],
        out_specs=_row_spec(w), out_shape=jax.ShapeDtypeStruct((t, w), BF16),
        compiler_params=_params("parallel"),
    )(o, proj, norm_g)


def _gdn_onorm_bwd(o, proj, norm_g, dy, *, name):
    t = o.shape[0]
    w = GDN_KEY_DIM
    goff = 3 * GDN_KEY_DIM // w

    def body(o_ref, gp_ref, g_ref, dy_ref, do_ref, dgp_ref, st_ref):
        @pl.when(pl.program_id(0) == 0)
        def _():
            st_ref[...] = jnp.zeros_like(st_ref)

        gv = g_ref[...]
        acc = jnp.zeros((1, GDN_HEAD_DIM), F32)
        for h in range(GDN_HEADS):
            sl = slice(h * GDN_HEAD_DIM, (h + 1) * GDN_HEAD_DIM)
            oh = o_ref[:, sl]
            gp = gp_ref[:, sl]
            dyv = dy_ref[:, sl].astype(F32)
            r = lax.rsqrt(jnp.mean(oh * oh, axis=-1, keepdims=True) + EPS)
            xh = oh * r
            sg = _sigmoid(gp)
            dn = dyv * gp * sg
            dgp_ref[:, sl] = (dyv * xh * gv * sg * (1.0 + gp * (1.0 - sg))).astype(dgp_ref.dtype)
            acc = acc + jnp.sum(dn * xh, axis=0, keepdims=True)
            dxh = dn * gv
            do_ref[:, sl] = r * (dxh - xh * jnp.mean(dxh * xh, axis=-1, keepdims=True))
        st_ref[0:1, :] += acc

    return pl.pallas_call(
        body, name=name, grid=(t // ROWS,),
        in_specs=[_row_spec(w), pl.BlockSpec((ROWS, w), lambda i: (i, goff)), _const_spec((1, GDN_HEAD_DIM)),
                  _row_spec(w)],
        out_specs=(_row_spec(w), _row_spec(w), _const_spec((8, GDN_HEAD_DIM))),
        out_shape=(jax.ShapeDtypeStruct((t, w), F32), jax.ShapeDtypeStruct((t, w), BF16),
                   jax.ShapeDtypeStruct((8, GDN_HEAD_DIM), F32)),
        compiler_params=_params("arbitrary"),
    )(o, proj, norm_g, dy)


def _mla_prep_fwd(proj, qg, kvg, *, name):
    t = proj.shape[0]
    q1, k1 = MLA_Q_RANK, MLA_Q_RANK + MLA_KV_RANK

    def body(p_ref, qg_ref, kg_ref, cq_ref, ck_ref):
        cq = p_ref[:, 0:q1]
        ck = p_ref[:, q1:k1]
        cq_ref[...] = (cq * lax.rsqrt(jnp.mean(cq * cq, axis=-1, keepdims=True) + EPS) * qg_ref[...]).astype(BF16)
        ck_ref[...] = (ck * lax.rsqrt(jnp.mean(ck * ck, axis=-1, keepdims=True) + EPS) * kg_ref[...]).astype(BF16)

    return pl.pallas_call(
        body, name=name, grid=(t // ROWS,),
        in_specs=[_row_spec(MLA_IN), _const_spec((1, MLA_Q_RANK)), _const_spec((1, MLA_KV_RANK))],
        out_specs=(_row_spec(MLA_Q_RANK), _row_spec(MLA_KV_RANK)),
        out_shape=(jax.ShapeDtypeStruct((t, MLA_Q_RANK), BF16), jax.ShapeDtypeStruct((t, MLA_KV_RANK), BF16)),
        compiler_params=_params("parallel"),
    )(proj, qg, kvg)


def _mla_prep_bwd(proj, qg, kvg, dcq, dck, dkr, *, name):
    t = proj.shape[0]
    q1, k1 = MLA_Q_RANK, MLA_Q_RANK + MLA_KV_RANK

    def body(p_ref, qg_ref, kg_ref, dq_ref, dk_ref, dr_ref, dp_ref, st_ref):
        @pl.when(pl.program_id(0) == 0)
        def _():
            st_ref[...] = jnp.zeros_like(st_ref)

        for lo, hi, g_ref, d_ref in ((0, q1, qg_ref, dq_ref), (q1, k1, kg_ref, dk_ref)):
            xv = p_ref[:, lo:hi]
            dn = d_ref[...]
            r = lax.rsqrt(jnp.mean(xv * xv, axis=-1, keepdims=True) + EPS)
            xh = xv * r
            dxh = dn * g_ref[...]
            dp_ref[:, lo:hi] = (r * (dxh - xh * jnp.mean(dxh * xh, axis=-1, keepdims=True))).astype(dp_ref.dtype)
            st_ref[0:1, lo:hi] += jnp.sum(dn * xh, axis=0, keepdims=True)
        dp_ref[:, k1:MLA_IN] = dr_ref[...].astype(dp_ref.dtype)

    return pl.pallas_call(
        body, name=name, grid=(t // ROWS,),
        in_specs=[_row_spec(MLA_IN), _const_spec((1, MLA_Q_RANK)), _const_spec((1, MLA_KV_RANK)),
                  _row_spec(MLA_Q_RANK), _row_spec(MLA_KV_RANK), _row_spec(MLA_ROPE)],
        out_specs=(_row_spec(MLA_IN), _const_spec((8, MLA_IN))),
        out_shape=(jax.ShapeDtypeStruct((t, MLA_IN), BF16), jax.ShapeDtypeStruct((8, MLA_IN), F32)),
        compiler_params=_params("arbitrary"),
    )(proj, qg, kvg, dcq, dck, dkr)


def _rope(xr, cos_t, sin_t, *, name):
    t, w = xr.shape
    ns = w // LANES

    def body(x_ref, c_ref, s_ref, o_ref):
        cv, sv = c_ref[...], s_ref[...]
        lane = lax.broadcasted_iota(jnp.int32, (ROWS, LANES), 1)
        first = (lane % MLA_ROPE) < (MLA_ROPE // 2)
        for i in range(ns):
            sl = slice(i * LANES, (i + 1) * LANES)
            xv = x_ref[:, sl]
            sw = jnp.where(first, pltpu.roll(xv, LANES - MLA_ROPE // 2, 1), pltpu.roll(xv, MLA_ROPE // 2, 1))
            o_ref[:, sl] = xv * cv + sw * sv

    return pl.pallas_call(
        body, name=name, grid=(t // ROWS,),
        in_specs=[_row_spec(w), _row_spec(LANES), _row_spec(LANES)], out_specs=_row_spec(w),
        out_shape=jax.ShapeDtypeStruct((t, w), F32), compiler_params=_params("parallel"),
    )(xr, cos_t, sin_t)


def _rope_bwd(dr, cos_t, sin_t, *, name):
    t, w = dr.shape
    ns = w // LANES

    def body(d_ref, c_ref, s_ref, o_ref):
        cv, sv = c_ref[...], s_ref[...]
        lane = lax.broadcasted_iota(jnp.int32, (ROWS, LANES), 1)
        first = (lane % MLA_ROPE) < (MLA_ROPE // 2)
        for i in range(ns):
            sl = slice(i * LANES, (i + 1) * LANES)
            dv = d_ref[:, sl]
            ds = dv * sv
            sw = jnp.where(first, pltpu.roll(ds, LANES - MLA_ROPE // 2, 1), pltpu.roll(ds, MLA_ROPE // 2, 1))
            o_ref[:, sl] = dv * cv + sw

    return pl.pallas_call(
        body, name=name, grid=(t // ROWS,),
        in_specs=[_row_spec(w), _row_spec(LANES), _row_spec(LANES)], out_specs=_row_spec(w),
        out_shape=jax.ShapeDtypeStruct((t, w), F32), compiler_params=_params("parallel"),
    )(dr, cos_t, sin_t)


ATT_BLOCK = 256
ATT_HEAD_BATCH = 4
ATT_HEAD_BATCH_BWD = 2
ATT_SCALE = MLA_QK ** -0.5


def _causal_mask(i, j, blk):
    rows = i * blk + lax.broadcasted_iota(jnp.int32, (blk, blk), 0)
    cols = j * blk + lax.broadcasted_iota(jnp.int32, (blk, blk), 1)
    return cols <= rows


def _attn_fwd(q, k, v, *, name):
    nh, t, dk = q.shape
    dv = v.shape[-1]
    blk = min(ATT_BLOCK, t)

    hb = ATT_HEAD_BATCH
    hs = range(hb)

    def body(q_ref, k_ref, v_ref, o_ref, l_ref):
        i = pl.program_id(1)
        qv = [q_ref[h] for h in hs]

        def step(j, carry):
            m, l, acc = carry[:hb], carry[hb:2 * hb], carry[2 * hb:]
            off = pl.multiple_of(j * blk, blk)
            mask = _causal_mask(i, j, blk)
            s = [_dotb(qv[h], k_ref[h, pl.ds(off, blk), :], NT) for h in hs]
            s = [jnp.where(mask, s[h] * ATT_SCALE, NEG) for h in hs]
            m_new = [jnp.maximum(m[h], jnp.max(s[h], axis=-1, keepdims=True)) for h in hs]
            p = [jnp.exp(s[h] - m_new[h]) for h in hs]
            pv = [_dotb(p[h], v_ref[h, pl.ds(off, blk), :], NN) for h in hs]
            alpha = [jnp.exp(m[h] - m_new[h]) for h in hs]
            l = [alpha[h] * l[h] + jnp.sum(p[h], axis=-1, keepdims=True) for h in hs]
            acc = [alpha[h] * acc[h] + pv[h] for h in hs]
            return tuple(m_new) + tuple(l) + tuple(acc)

        init = ((jnp.full((blk, 1), NEG, F32),) * hb + (jnp.zeros((blk, 1), F32),) * hb
                + (jnp.zeros((blk, dv), F32),) * hb)
        out = lax.fori_loop(0, i + 1, step, init)
        for h in hs:
            m, l, acc = out[h], out[hb + h], out[2 * hb + h]
            o_ref[h] = acc / l
            l_ref[h] = jnp.broadcast_to(m + jnp.log(l), (blk, LANES))

    return pl.pallas_call(
        body, name=name, grid=(nh // hb, t // blk),
        in_specs=[pl.BlockSpec((hb, blk, dk), lambda h, i: (h, i, 0)), pl.BlockSpec((hb, t, dk), lambda h, i: (h, 0, 0)),
                  pl.BlockSpec((hb, t, dv), lambda h, i: (h, 0, 0))],
        out_specs=(pl.BlockSpec((hb, blk, dv), lambda h, i: (h, i, 0)),
                   pl.BlockSpec((hb, blk, LANES), lambda h, i: (h, i, 0))),
        out_shape=(jax.ShapeDtypeStruct((nh, t, dv), F32), jax.ShapeDtypeStruct((nh, t, LANES), F32)),
        compiler_params=_params("parallel", "parallel"),
    )(q, k, v)


def _attn_bwd(q, k, v, o, lse, do, *, name):
    nh, t, dk = q.shape
    dv = v.shape[-1]
    blk = min(ATT_BLOCK, t)
    nb = t // blk

    hb = ATT_HEAD_BATCH_BWD
    hs = range(hb)

    def body(q_ref, k_ref, v_ref, o_ref, l_ref, do_ref, dq_ref, dk_ref, dv_ref):
        j = pl.program_id(1)

        @pl.when(j == 0)
        def _():
            dq_ref[...] = jnp.zeros_like(dq_ref)

        kv = [k_ref[h] for h in hs]
        vv = [v_ref[h] for h in hs]

        def step(i, carry):
            dk_acc, dv_acc = carry[:hb], carry[hb:]
            off = pl.multiple_of(i * blk, blk)
            rows = pl.ds(off, blk)
            mask = _causal_mask(i, j, blk)
            qv = [q_ref[h, rows, :] for h in hs]
            dov = [do_ref[h, rows, :] for h in hs]
            s = [_dotb(qv[h], kv[h], NT) for h in hs]
            dp = [_dotb(dov[h], vv[h], NT) for h in hs]
            p = [jnp.exp(jnp.where(mask, s[h] * ATT_SCALE, NEG) - l_ref[h, rows, :][:, 0:1]) for h in hs]
            delta = [jnp.sum(dov[h] * o_ref[h, rows, :], axis=-1, keepdims=True) for h in hs]
            ds = [p[h] * (dp[h] - delta[h]) * ATT_SCALE for h in hs]
            dvn = [_dotb(p[h], dov[h], TN) for h in hs]
            dkn = [_dotb(ds[h], qv[h], TN) for h in hs]
            dqn = [_dotb(ds[h], kv[h], NN) for h in hs]
            for h in hs:
                dq_ref[h, rows, :] += dqn[h]
            return tuple(dk_acc[h] + dkn[h] for h in hs) + tuple(dv_acc[h] + dvn[h] for h in hs)

        out = lax.fori_loop(j, nb, step, (jnp.zeros((blk, dk), F32),) * hb + (jnp.zeros((blk, dv), F32),) * hb)
        for h in hs:
            dk_ref[h] = out[h]
            dv_ref[h] = out[hb + h]

    full = lambda w: pl.BlockSpec((hb, t, w), lambda h, j: (h, 0, 0))
    part = lambda w: pl.BlockSpec((hb, blk, w), lambda h, j: (h, j, 0))
    return pl.pallas_call(
        body, name=name, grid=(nh // hb, nb),
        in_specs=[full(dk), part(dk), part(dv), full(dv), full(LANES), full(dv)],
        out_specs=(full(dk), part(dk), part(dv)),
        out_shape=(jax.ShapeDtypeStruct((nh, t, dk), F32), jax.ShapeDtypeStruct((nh, t, dk), F32),
                   jax.ShapeDtypeStruct((nh, t, dv), F32)),
        compiler_params=_params("parallel", "arbitrary"),
    )(q, k, v, o, lse, do)


def _ada_mod(c_all, ada_w, ada_b_cols, *, name):
    nl, d, wc = ada_w.shape

    def body(c_ref, w_ref, b_ref, o_ref):
        cv = c_ref[...]
        o_ref[0] = _dotb(cv * _sigmoid(cv), w_ref[0], NN) + b_ref[0]

    return pl.pallas_call(
        body, name=name, grid=(nl,),
        in_specs=[_const_spec((N_DEV, d)), pl.BlockSpec((1, d, wc), lambda l: (l, 0, 0)),
                  pl.BlockSpec((1, 1, wc), lambda l: (l, 0, 0))],
        out_specs=pl.BlockSpec((1, N_DEV, wc), lambda l: (l, 0, 0)),
        out_shape=jax.ShapeDtypeStruct((nl, N_DEV, wc), F32), compiler_params=_params("parallel"),
    )(c_all, ada_w, ada_b_cols)


def _adam_math(g, w, m, v):
    m2 = ADAM_B1 * m + (1.0 - ADAM_B1) * g
    v2 = ADAM_B2 * v + (1.0 - ADAM_B2) * (g * g)
    delta = -ADAM_LR * ((m2 / ADAM_BC1) / (jnp.sqrt(v2 / ADAM_BC2) + ADAM_EPS) + ADAM_WD * w)
    return delta, m2, v2


def _ada_grad_adamw(c_all, dmod_cols, w, m, v, *, name):
    nl, d, wc = w.shape
    tr = 256

    def body(c_ref, dm_ref, w_ref, m_ref, v_ref, g_ref, d_ref, m2_ref, v2_ref):
        cv = c_ref[...]
        g = _dotf(cv * _sigmoid(cv), dm_ref[0], TN)
        delta, m2, v2 = _adam_math(g, w_ref[0], m_ref[0], v_ref[0])
        g_ref[0], d_ref[0], m2_ref[0], v2_ref[0] = g, delta, m2, v2

    blk = pl.BlockSpec((1, tr, wc), lambda l, i: (l, i, 0))
    return pl.pallas_call(
        body, name=name, grid=(nl, d // tr),
        in_specs=[pl.BlockSpec((N_DEV, tr), lambda l, i: (0, i)), pl.BlockSpec((1, N_DEV, wc), lambda l, i: (l, 0, 0)),
                  blk, blk, blk],
        out_specs=(blk,) * 4, out_shape=(jax.ShapeDtypeStruct(w.shape, F32),) * 4,
        compiler_params=_params("parallel", "parallel"),
    )(c_all, dmod_cols, w, m, v)


def _adamw(parts, w, m, v, *, name):
    nl, r, c = w.shape
    ns = parts[0].shape[0]
    lanes_padded = -(-c // LANES) * LANES
    row_bytes = 2 * nl * ns * lanes_padded * parts[0].dtype.itemsize
    tr = _pick(r, min(256, max(16, (VMEM_LIMIT // 2) // row_bytes)), 16)
    tc = c
    if tr * row_bytes > VMEM_LIMIT // 2:
        tc = _pick(c, max(LANES, c * (VMEM_LIMIT // 2) // (tr * row_bytes)))

    def body(*refs):
        p_refs = refs[:nl]
        w_ref, m_ref, v_ref, g_ref, d_ref, m2_ref, v2_ref = refs[nl:]
        layer = pl.program_id(0)
        for q in range(nl):
            @pl.when(layer == q)
            def _(q=q):
                g = p_refs[q][0].astype(F32)
                for s in range(1, ns):
                    g = g + p_refs[q][s].astype(F32)
                delta, m2, v2 = _adam_math(g, w_ref[0], m_ref[0], v_ref[0])
                g_ref[0], d_ref[0], m2_ref[0], v2_ref[0] = g, delta, m2, v2

    blk = pl.BlockSpec((1, tr, tc), lambda l, i, j: (l, i, j))
    p_specs = [pl.BlockSpec((ns, tr, tc), lambda l, i, j, q=q: (0, jnp.where(l == q, i, 0), jnp.where(l == q, j, 0)))
               for q in range(nl)]
    return pl.pallas_call(
        body, name=name, grid=(nl, r // tr, c // tc),
        in_specs=p_specs + [blk, blk, blk],
        out_specs=(blk,) * 4, out_shape=(jax.ShapeDtypeStruct(w.shape, F32),) * 4,
        compiler_params=_params("arbitrary", "arbitrary", "arbitrary"),
    )(*parts, w, m, v)


def _sum_parts(parts, *, name):
    ns, r, c = parts.shape

    def body(p_ref, o_ref):
        acc = p_ref[0]
        for s in range(1, ns):
            acc = acc + p_ref[s]
        o_ref[...] = acc

    return pl.pallas_call(
        body, name=name, out_shape=jax.ShapeDtypeStruct((r, c), F32),
        in_specs=[pl.BlockSpec(memory_space=pltpu.VMEM)], out_specs=pl.BlockSpec(memory_space=pltpu.VMEM),
    )(parts)


def _pack(arrs):
    flat = jnp.concatenate([a.reshape(-1).astype(F32) for a in arrs])
    pad = (-flat.shape[0]) % (8 * LANES)
    return jnp.pad(flat, (0, pad)).reshape(-1, LANES)


def _unpack(packed, shapes, lead=()):
    flat = packed.reshape(lead + (-1,))
    out, off = [], 0
    for s in shapes:
        n = math.prod(s)
        out.append(flat[..., off:off + n].reshape(lead + tuple(s)))
        off += n
    return out


def _gather_cols(g):
    _, nl, r, cs = g.shape
    return jnp.transpose(g, (1, 2, 0, 3)).reshape(nl, r, N_DEV * cs)


def _gather_rows(g):
    _, nl, rs, c = g.shape
    return jnp.transpose(g, (1, 0, 2, 3)).reshape(nl, N_DEV * rs, c)


def _scatter_cols(full):
    nl, r, c = full.shape
    return jnp.transpose(full.reshape(nl, r, N_DEV, c // N_DEV), (2, 0, 1, 3))


def _scatter_rows(full):
    nl, r, c = full.shape
    return jnp.transpose(full.reshape(nl, N_DEV, r // N_DEV, c), (1, 0, 2, 3))


def _row(v):
    return v.reshape(1, -1)


def _local_step(x, target, mod, cos_t, sin_t, rep, get_weights, put_grads):
    t = x.shape[0]
    saved = []
    for layer in range(DEPTH):
        j = layer // 2
        tag = f"l{layer}"
        shift_m, scale_m, gate_m, shift_f, scale_f, gate_f = [_row(mod[layer, i]) for i in range(N_MOD)]
        lw = dict(get_weights(layer, "mix", x))
        rec = {"x0": x, "lw": lw}
        h = _adaln_fwd(x, _row(rep["norm_mix_g"][layer]), scale_m, shift_m, name=f"adaln_mix_{tag}")
        rec["h"] = h
        if layer % 2 == 0:
            proj = _mm(h, lw["wt_in"], mode="nt", out_dtype=F32, tm=256, tn=GDN_MAIN, b_rows=GDN_MAIN,
                       name=f"gdn_in_{tag}")
            ab = _mm(h, lw["wt_ab"], mode="nt", out_dtype=F32, name=f"gdn_in_ab_{tag}")
            qkv = _gdn_prep_fwd(proj, rep["gdn_conv_wt"][j], name=f"gdn_prep_{tag}")
            gbeta = _gdn_gate_fwd(ab, rep["gdn_gate_prm"][j], name=f"gdn_gate_{tag}")
            gbc = jnp.broadcast_to(jnp.transpose(gbeta[:, 0:GDN_HEADS])[:, :, None], (GDN_HEADS, t, LANES))
            bbc = jnp.broadcast_to(jnp.transpose(gbeta[:, GDN_HEADS:2 * GDN_HEADS])[:, :, None],
                                   (GDN_HEADS, t, LANES))
            o, states = _gdn_chunk_fwd(qkv, gbc, bbc, name=f"gdn_chunk_{tag}")
            og = _gdn_onorm_fwd(o, proj, _row(rep["gdn_norm_g"][j]), name=f"gdn_onorm_{tag}")
            x, y = _mm_resid(og, lw["w_out"], x, gate_m, name=f"gdn_out_{tag}")
            rec.update(proj=proj, ab=ab, qkv=qkv, gbc=gbc, bbc=bbc, states=states, o=o, og=og, y=y)
        else:
            proj = _mm(h, lw["w_in"], mode="nn", out_dtype=F32, name=f"mla_in_{tag}")
            cq, ck = _mla_prep_fwd(proj, _row(rep["mla_q_norm_g"][j]), _row(rep["mla_kv_norm_g"][j]),
                                   name=f"mla_prep_{tag}")
            qf = _mm(cq, lw["wt_uq"], mode="nt", out_dtype=F32, name=f"mla_uq_{tag}")
            kvf = _mm(ck, lw["w_ukv"], mode="nn", out_dtype=F32, name=f"mla_ukv_{tag}")
            nrope = MLA_HEADS * MLA_ROPE
            krp = jnp.pad(proj[:, MLA_Q_RANK + MLA_KV_RANK:], ((0, 0), (0, LANES - MLA_ROPE)))
            roped = _rope(jnp.concatenate([qf[:, MLA_HEADS * MLA_NOPE:], krp], axis=1), cos_t, sin_t,
                          name=f"rope_{tag}")
            q_nope = qf[:, :MLA_HEADS * MLA_NOPE].reshape(t, MLA_HEADS, MLA_NOPE)
            q_rope = roped[:, :nrope].reshape(t, MLA_HEADS, MLA_ROPE)
            k_rope = jnp.broadcast_to(roped[:, None, nrope:nrope + MLA_ROPE], (t, MLA_HEADS, MLA_ROPE))
            kv3 = kvf.reshape(t, MLA_HEADS, MLA_NOPE + MLA_V)
            qc = jnp.transpose(jnp.concatenate([q_nope, q_rope], axis=-1), (1, 0, 2)).astype(BF16)
            kc = jnp.transpose(jnp.concatenate([kv3[..., :MLA_NOPE], k_rope], axis=-1), (1, 0, 2)).astype(BF16)
            vc = jnp.transpose(kv3[..., MLA_NOPE:], (1, 0, 2)).astype(BF16)
            oh, lse = _attn_fwd(qc, kc, vc, name=f"attn_{tag}")
            oc = jnp.transpose(oh, (1, 0, 2)).reshape(t, MLA_HEADS * MLA_V).astype(BF16)
            x, y = _mm_resid(oc, lw["w_out"], x, gate_m, name=f"mla_out_{tag}")
            rec.update(proj=proj, cq=cq, ck=ck, qc=qc, kc=kc, vc=vc, oh=oh, lse=lse, oc=oc, y=y)
        rec["x1"] = x
        lw.update(get_weights(layer, "ffn", x))
        h2 = _adaln_fwd(x, _row(rep["norm_ffn_g"][layer]), scale_f, shift_f, name=f"adaln_ffn_{tag}")
        s, a2, b2 = _ffn_gu_fwd(h2, lw["wt_g"], lw["wt_u"], name=f"ffn_gu_{tag}")
        x, y2 = _mm_resid(s, lw["w_down"], x, gate_f, name=f"ffn_down_{tag}")
        rec.update(h2=h2, a2=a2, b2=b2, s=s, y2=y2)
        saved.append(rec)

    dx, st, ls = _loss_head(x, _row(rep["final_norm_g"]), target, name="loss_head")
    loss = ls[0, 0]
    grads = {"final_norm_g": st[0]}
    per_layer = {k: [None] * DEPTH for k in ("norm_mix_g", "norm_ffn_g")}
    per_gdn = {k: [None] * 2 for k in ("gdn_conv_wt", "gdn_a_log", "gdn_dt_bias", "gdn_norm_g")}
    per_mla = {k: [None] * 2 for k in ("mla_q_norm_g", "mla_kv_norm_g")}
    dmod = [None] * DEPTH
    dep = jnp.zeros((8, LANES), F32)

    for layer in reversed(range(DEPTH)):
        j = layer // 2
        tag = f"l{layer}"
        rec = saved[layer]
        lw = rec["lw"]
        shift_m, scale_m, gate_m, shift_f, scale_f, gate_f = [_row(mod[layer, i]) for i in range(N_MOD)]
        dy2, st_g = _gate_bwd(dx, rec["y2"], gate_f, dep, name=f"gate_bwd_ffn_{tag}")
        dgate_f = st_g[0]
        dw_down = _mm(rec["s"], dy2, mode="tn", out_dtype=BF16, tm=256, tn=1024, name=f"ffn_down_dw_{tag}")
        da2, db2 = _ffn_down_dx(dy2, lw["w_down"], rec["a2"], rec["b2"], name=f"ffn_down_dx_{tag}")
        dwt_g = _mm(da2, rec["h2"], mode="tn", out_dtype=BF16, tm=256, tn=1024, name=f"ffn_g_dw_{tag}")
        dwt_u = _mm(db2, rec["h2"], mode="tn", out_dtype=BF16, tm=256, tn=1024, name=f"ffn_u_dw_{tag}")
        dep = put_grads(layer, "ffn", {"wt_g": dwt_g, "wt_u": dwt_u, "w_down": dw_down})
        dh2 = _mm(da2, lw["wt_g"], mode="nn", out_dtype=F32, tm=256, tn=1024, name=f"ffn_g_dx_{tag}")
        dh2 = _mm(db2, lw["wt_u"], mode="nn", out_dtype=BF16, add=dh2, tm=256, tn=1024, name=f"ffn_u_dx_{tag}")
        dx, st_n = _adaln_bwd(rec["x1"], _row(rep["norm_ffn_g"][layer]), scale_f, shift_f, dh2, dx, dep,
                              name=f"adaln_ffn_bwd_{tag}")
        per_layer["norm_ffn_g"][layer] = st_n[0]
        dscale_f, dshift_f = st_n[1], st_n[2]
        dy, st_g = _gate_bwd(dx, rec["y"], gate_m, dep, name=f"gate_bwd_mix_{tag}")
        dgate_m = st_g[0]
        big = {}
        if layer % 2 == 0:
            big["w_out"] = _mm(rec["og"], dy, mode="tn", out_dtype=BF16, name=f"gdn_out_dw_{tag}")
            dog = _mm(dy, lw["w_out"], mode="nt", out_dtype=BF16, name=f"gdn_out_dx_{tag}")
            do, dgp, st_o = _gdn_onorm_bwd(rec["o"], rec["proj"], _row(rep["gdn_norm_g"][j]), dog,
                                           name=f"gdn_onorm_bwd_{tag}")
            per_gdn["gdn_norm_g"][j] = st_o[0]
            dqkv, dgc_, dbc_ = _gdn_chunk_bwd(rec["qkv"], rec["gbc"], rec["bbc"], rec["states"], do,
                                               name=f"gdn_chunk_bwd_{tag}")
            dgb = jnp.concatenate([jnp.transpose(dgc_[:, :, 0]), jnp.transpose(dbc_[:, :, 0])], axis=1)
            dgb = jnp.pad(dgb, ((0, 0), (0, LANES - 2 * GDN_HEADS)))
            dab, st_a = _gdn_gate_bwd(rec["ab"], rep["gdn_gate_prm"][j], dgb, name=f"gdn_gate_bwd_{tag}")
            per_gdn["gdn_a_log"][j] = st_a[0, :GDN_HEADS]
            per_gdn["gdn_dt_bias"][j] = st_a[1, :GDN_HEADS]
            dpre, dcw = _gdn_prep_bwd(rec["proj"], rep["gdn_conv_wt"][j], dqkv, name=f"gdn_prep_bwd_{tag}")
            per_gdn["gdn_conv_wt"][j] = dcw
            dproj = jnp.concatenate([dpre, dgp], axis=1)
            dw_main = _mm(dproj, rec["h"], mode="tn", out_dtype=BF16, tm=512, tn=1024, name=f"gdn_in_dw_{tag}")
            dw_ab = _mm(dab, rec["h"], mode="tn", out_dtype=BF16, tn=1024, name=f"gdn_in_ab_dw_{tag}")
            big["wt_in"] = jnp.concatenate([dw_main, dw_ab[:2 * GDN_HEADS]], axis=0)
            dep = put_grads(layer, "gdn", big)
            dh_ab = _mm(dab, lw["wt_ab"], mode="nn", out_dtype=F32, tn=1024, name=f"gdn_in_ab_dx_{tag}")
            dh = _mm(dproj, lw["wt_in"], mode="nn", out_dtype=BF16, add=dh_ab, tm=256, tn=1024, b_rows=GDN_MAIN,
                     name=f"gdn_in_dx_{tag}")
        else:
            big["w_out"] = _mm(rec["oc"], dy, mode="tn", out_dtype=BF16, name=f"mla_out_dw_{tag}")
            doc = _mm(dy, lw["w_out"], mode="nt", out_dtype=F32, name=f"mla_out_dx_{tag}")
            doh = jnp.transpose(doc.reshape(t, MLA_HEADS, MLA_V), (1, 0, 2))
            dqc, dkc, dvc = _attn_bwd(rec["qc"], rec["kc"], rec["vc"], rec["oh"], rec["lse"], doh,
                                      name=f"attn_bwd_{tag}")
            dqn = jnp.transpose(dqc[..., :MLA_NOPE], (1, 0, 2)).reshape(t, MLA_HEADS * MLA_NOPE)
            dqr = jnp.transpose(dqc[..., MLA_NOPE:], (1, 0, 2)).reshape(t, MLA_HEADS * MLA_ROPE)
            dkr = jnp.pad(jnp.sum(dkc[..., MLA_NOPE:], axis=0), ((0, 0), (0, LANES - MLA_ROPE)))
            drope = _rope_bwd(jnp.concatenate([dqr, dkr], axis=1), cos_t, sin_t, name=f"rope_bwd_{tag}")
            nrope = MLA_HEADS * MLA_ROPE
            dqf = jnp.concatenate([dqn, drope[:, :nrope]], axis=1).astype(BF16)
            dkvf = jnp.concatenate([jnp.transpose(dkc[..., :MLA_NOPE], (1, 0, 2)), jnp.transpose(dvc, (1, 0, 2))],
                                   axis=-1).reshape(t, MLA_HEADS * (MLA_NOPE + MLA_V)).astype(BF16)
            big["wt_uq"] = _mm(dqf, rec["cq"], mode="tn", out_dtype=BF16, name=f"mla_uq_dw_{tag}")
            big["w_ukv"] = _mm(rec["ck"], dkvf, mode="tn", out_dtype=BF16, name=f"mla_ukv_dw_{tag}")
            dcq = _mm(dqf, lw["wt_uq"], mode="nn", out_dtype=F32, name=f"mla_uq_dx_{tag}")
            dck = _mm(dkvf, lw["w_ukv"], mode="nt", out_dtype=F32, name=f"mla_ukv_dx_{tag}")
            dproj, st_p = _mla_prep_bwd(rec["proj"], _row(rep["mla_q_norm_g"][j]), _row(rep["mla_kv_norm_g"][j]),
                                        dcq, dck, drope[:, nrope:nrope + MLA_ROPE], name=f"mla_prep_bwd_{tag}")
            per_mla["mla_q_norm_g"][j] = st_p[0, :MLA_Q_RANK]
            per_mla["mla_kv_norm_g"][j] = st_p[0, MLA_Q_RANK:MLA_Q_RANK + MLA_KV_RANK]
            big["w_in"] = _mm(rec["h"], dproj, mode="tn", out_dtype=BF16, name=f"mla_in_dw_{tag}")
            dep = put_grads(layer, "mla", big)
            dh = _mm(dproj, lw["w_in"], mode="nt", out_dtype=BF16, name=f"mla_in_dx_{tag}")
        dx, st_n = _adaln_bwd(rec["x0"], _row(rep["norm_mix_g"][layer]), scale_m, shift_m, dh, dx, dep,
                              name=f"adaln_mix_bwd_{tag}")
        per_layer["norm_mix_g"][layer] = st_n[0]
        dmod[layer] = jnp.stack([st_n[2], st_n[1], dgate_m, dshift_f, dscale_f, dgate_f])

    for d in (per_layer, per_gdn, per_mla):
        for k, v in d.items():
            grads[k] = jnp.stack(v)
    return loss, dx, jnp.stack(dmod), grads


BIG = ("gdn_w_in", "gdn_w_out", "mla_w_in", "mla_w_uq", "mla_w_ukv", "mla_w_out", "ffn_w_gate", "ffn_w_up",
       "ffn_w_down")
TRANSPOSED = ("gdn_w_in", "mla_w_uq", "ffn_w_gate", "ffn_w_up")


def _view(k, a):
    return jnp.transpose(a, (0, 2, 1)) if k in TRANSPOSED else a
SMALL = ("ada_b", "norm_mix_g", "norm_ffn_g", "gdn_conv_w", "gdn_a_log", "gdn_dt_bias", "gdn_norm_g",
         "mla_q_norm_g", "mla_kv_norm_g", "final_norm_g")
WEIGHTS = ("ada_w", "ada_b", "norm_mix_g", "norm_ffn_g", "gdn_w_in", "gdn_conv_w", "gdn_a_log", "gdn_dt_bias",
           "gdn_norm_g", "gdn_w_out", "mla_w_in", "mla_q_norm_g", "mla_kv_norm_g", "mla_w_uq", "mla_w_ukv",
           "mla_w_out", "ffn_w_gate", "ffn_w_up", "ffn_w_down", "final_norm_g")


def _uq_to_kernel_layout(w, axis=-1):
    axis = axis % w.ndim
    lead, tail = w.shape[:axis], w.shape[axis + 1:]
    w4 = w.reshape(lead + (MLA_HEADS, MLA_QK) + tail)
    nope = lax.slice_in_dim(w4, 0, MLA_NOPE, axis=axis + 1).reshape(lead + (-1,) + tail)
    rope = lax.slice_in_dim(w4, MLA_NOPE, MLA_QK, axis=axis + 1).reshape(lead + (-1,) + tail)
    return jnp.concatenate([nope, rope], axis=axis)


def _uq_from_kernel_layout(w, axis=-1):
    axis = axis % w.ndim
    lead, tail = w.shape[:axis], w.shape[axis + 1:]
    nope = lax.slice_in_dim(w, 0, MLA_HEADS * MLA_NOPE, axis=axis).reshape(lead + (MLA_HEADS, MLA_NOPE) + tail)
    rope = lax.slice_in_dim(w, MLA_HEADS * MLA_NOPE, MLA_HEADS * MLA_QK, axis=axis).reshape(
        lead + (MLA_HEADS, MLA_ROPE) + tail)
    return jnp.concatenate([nope, rope], axis=axis + 1).reshape(lead + (-1,) + tail)


def _group_names(layer, kind):
    if kind == "ffn":
        return ("ffn_w_gate", "ffn_w_up", "ffn_w_down")
    return ("gdn_w_in", "gdn_w_out") if layer % 2 == 0 else ("mla_w_in", "mla_w_uq", "mla_w_ukv", "mla_w_out")


def _layer_index(name, layer):
    return layer if name.startswith("ffn") else layer // 2


def _cols(g):
    return jnp.transpose(g, (1, 0, 2)).reshape(g.shape[1], N_DEV * g.shape[2])


def _rows(g):
    return g.reshape(N_DEV * g.shape[1], g.shape[2])


def _uncols(full):
    r, c = full.shape
    return jnp.transpose(full.reshape(r, N_DEV, c // N_DEV), (1, 0, 2))


def _unrows(full):
    r, c = full.shape
    return full.reshape(N_DEV, r // N_DEV, c)


def _group_weights(layer, kind, got, zero):
    if kind == "ffn":
        return {"wt_g": _rows(got["ffn_w_gate"]) + zero, "wt_u": _rows(got["ffn_w_up"]),
                "w_down": _rows(got["ffn_w_down"])}
    if layer % 2 == 0:
        wt_in = _rows(got["gdn_w_in"]) + zero
        return dict(wt_in=wt_in, wt_ab=jnp.pad(wt_in[GDN_MAIN:], ((0, LANES - 2 * GDN_HEADS), (0, 0))),
                    w_out=_rows(got["gdn_w_out"]))
    return dict(w_in=_rows(got["mla_w_in"]), wt_uq=_uq_to_kernel_layout(_rows(got["mla_w_uq"]), axis=0) + zero,
                w_ukv=_cols(got["mla_w_ukv"]), w_out=_rows(got["mla_w_out"]))


def _layer_grad_slots(kind, big):
    if kind == "ffn":
        return {"ffn_w_gate": _unrows(big["wt_g"]), "ffn_w_up": _unrows(big["wt_u"]),
                "ffn_w_down": _unrows(big["w_down"])}
    if kind == "gdn":
        return {"gdn_w_in": _unrows(big["wt_in"]), "gdn_w_out": _unrows(big["w_out"])}
    return {"mla_w_in": _unrows(big["w_in"]), "mla_w_uq": _unrows(_uq_from_kernel_layout(big["wt_uq"], axis=0)),
            "mla_w_ukv": _uncols(big["w_ukv"]), "mla_w_out": _unrows(big["w_out"])}


def _small_weights(tiny, rep):
    prm = jnp.zeros((2, 8, LANES), F32)
    prm = prm.at[:, 0, :GDN_HEADS].set(rep["gdn_a_log"]).at[:, 1, :GDN_HEADS].set(rep["gdn_dt_bias"])
    out = {
        "gdn_conv_wt": jnp.transpose(_gather_rows(tiny["gdn_conv_w"]), (0, 2, 1)),
        "mla_q_norm_g": jnp.transpose(tiny["mla_q_norm_g"], (1, 0, 2)).reshape(2, MLA_Q_RANK),
        "mla_kv_norm_g": jnp.transpose(tiny["mla_kv_norm_g"], (1, 0, 2)).reshape(2, MLA_KV_RANK),
        "gdn_gate_prm": prm,
    }
    for k in ("norm_mix_g", "norm_ffn_g", "gdn_norm_g", "final_norm_g"):
        out[k] = rep[k]
    return out


def _rope_tables(positions):
    inv_freq = ROPE_THETA ** (-jnp.arange(0, MLA_ROPE, 2, dtype=F32) / MLA_ROPE)
    ang = positions.astype(F32)[:, None] * inv_freq
    cos, sin = jnp.cos(ang), jnp.sin(ang)
    reps = LANES // MLA_ROPE
    return jnp.tile(jnp.concatenate([cos, cos], axis=1), (1, reps)), jnp.tile(
        jnp.concatenate([-sin, sin], axis=1), (1, reps))


def kernel(x, c, positions, ada_w, ada_b, norm_mix_g, norm_ffn_g, gdn_w_in, gdn_conv_w, gdn_a_log, gdn_dt_bias, gdn_norm_g, gdn_w_out, mla_w_in, mla_q_norm_g, mla_kv_norm_g, mla_w_uq, mla_w_ukv, mla_w_out, ffn_w_gate, ffn_w_up, ffn_w_down, final_norm_g, loss_target, m_ada_w, m_ada_b, m_norm_mix_g, m_norm_ffn_g, m_gdn_w_in, m_gdn_conv_w, m_gdn_a_log, m_gdn_dt_bias, m_gdn_norm_g, m_gdn_w_out, m_mla_w_in, m_mla_q_norm_g, m_mla_kv_norm_g, m_mla_w_uq, m_mla_w_ukv, m_mla_w_out, m_ffn_w_gate, m_ffn_w_up, m_ffn_w_down, m_final_norm_g, v_ada_w, v_ada_b, v_norm_mix_g, v_norm_ffn_g, v_gdn_w_in, v_gdn_conv_w, v_gdn_a_log, v_gdn_dt_bias, v_gdn_norm_g, v_gdn_w_out, v_mla_w_in, v_mla_q_norm_g, v_mla_kv_norm_g, v_mla_w_uq, v_mla_w_ukv, v_mla_w_out, v_ffn_w_gate, v_ffn_w_up, v_ffn_w_down, v_final_norm_g):
    W = dict(ada_w=ada_w, ada_b=ada_b, norm_mix_g=norm_mix_g, norm_ffn_g=norm_ffn_g, gdn_w_in=gdn_w_in,
             gdn_conv_w=gdn_conv_w, gdn_a_log=gdn_a_log, gdn_dt_bias=gdn_dt_bias, gdn_norm_g=gdn_norm_g,
             gdn_w_out=gdn_w_out, mla_w_in=mla_w_in, mla_q_norm_g=mla_q_norm_g, mla_kv_norm_g=mla_kv_norm_g,
             mla_w_uq=mla_w_uq, mla_w_ukv=mla_w_ukv, mla_w_out=mla_w_out, ffn_w_gate=ffn_w_gate,
             ffn_w_up=ffn_w_up, ffn_w_down=ffn_w_down, final_norm_g=final_norm_g)
    M = dict(ada_w=m_ada_w, ada_b=m_ada_b, norm_mix_g=m_norm_mix_g, norm_ffn_g=m_norm_ffn_g, gdn_w_in=m_gdn_w_in,
             gdn_conv_w=m_gdn_conv_w, gdn_a_log=m_gdn_a_log, gdn_dt_bias=m_gdn_dt_bias, gdn_norm_g=m_gdn_norm_g,
             gdn_w_out=m_gdn_w_out, mla_w_in=m_mla_w_in, mla_q_norm_g=m_mla_q_norm_g,
             mla_kv_norm_g=m_mla_kv_norm_g, mla_w_uq=m_mla_w_uq, mla_w_ukv=m_mla_w_ukv, mla_w_out=m_mla_w_out,
             ffn_w_gate=m_ffn_w_gate, ffn_w_up=m_ffn_w_up, ffn_w_down=m_ffn_w_down, final_norm_g=m_final_norm_g)
    V = dict(ada_w=v_ada_w, ada_b=v_ada_b, norm_mix_g=v_norm_mix_g, norm_ffn_g=v_norm_ffn_g, gdn_w_in=v_gdn_w_in,
             gdn_conv_w=v_gdn_conv_w, gdn_a_log=v_gdn_a_log, gdn_dt_bias=v_gdn_dt_bias, gdn_norm_g=v_gdn_norm_g,
             gdn_w_out=v_gdn_w_out, mla_w_in=v_mla_w_in, mla_q_norm_g=v_mla_q_norm_g,
             mla_kv_norm_g=v_mla_kv_norm_g, mla_w_uq=v_mla_w_uq, mla_w_ukv=v_mla_w_ukv, mla_w_out=v_mla_w_out,
             ffn_w_gate=v_ffn_w_gate, ffn_w_up=v_ffn_w_up, ffn_w_down=v_ffn_w_down, final_norm_g=v_final_norm_g)
    me = 4 * lax.axis_index("x") + 2 * lax.axis_index("y") + lax.axis_index("c")
    t = x.shape[1]
    wc = ada_w.shape[-1]

    groups = [(layer, kind) for layer in range(DEPTH) for kind in ("mix", "ffn")]

    def group_srcs(i):
        layer, kind = groups[i]
        return [_view(k, W[k])[_layer_index(k, layer)].astype(BF16) for k in _group_names(layer, kind)]

    tiny_shapes = [c.shape, gdn_conv_w.shape, mla_q_norm_g.shape, mla_kv_norm_g.shape]
    first = _gather_two_level([_pack([c, gdn_conv_w, mla_q_norm_g, mla_kv_norm_g])] + group_srcs(0),
                              name="gather_first")
    tiny_g = first[0]
    c_g, conv_g, qn_g, kvn_g = _unpack(tiny_g, tiny_shapes, lead=(N_DEV,))
    c_all = c_g.reshape(N_DEV, D_MODEL)
    rep = _small_weights({"gdn_conv_w": conv_g, "mla_q_norm_g": qn_g, "mla_kv_norm_g": kvn_g}, W)

    def start_group(i, dep):
        layer, kind = groups[i]
        return _exchange_start(group_srcs(i), scatter=False, name=f"gather_start_{kind}_l{layer}", dep=dep)


    b_cols = lax.dynamic_slice_in_dim(ada_b, me * wc, wc, axis=1).reshape(DEPTH, 1, wc)
    mod_part = _ada_mod(c_all, ada_w, b_cols, name="ada_mod")
    (mod_g,) = _exchange([mod_part], scatter=False, name="gather_mod")
    mod_mine = lax.dynamic_index_in_dim(mod_g, me, axis=2, keepdims=False)
    mod = jnp.transpose(mod_mine, (1, 0, 2)).reshape(DEPTH, N_MOD, D_MODEL)
    gather = {1: start_group(1, mod_g)}

    def get_weights(layer, kind, after):
        i = groups.index((layer, kind))
        names = _group_names(layer, kind)
        if i == 0:
            return _group_weights(layer, kind, dict(zip(names, first[1:])), gather[1][4][0, 0].astype(BF16))
        srcs, lands = _exchange_wait(gather[i], after, scatter=False, name=f"gather_wait_{kind}_l{layer}")
        zero = jnp.zeros((), BF16)
        if i + 1 < len(groups):
            gather[i + 1] = start_group(i + 1, lands[0])
            zero = gather[i + 1][4][0, 0].astype(BF16)
        got = {k: lax.dynamic_update_index_in_dim(z, s, me, 0) for k, s, z in zip(names, srcs, lands)}
        return _group_weights(layer, kind, got, zero)

    scatter = []

    def put_grads(layer, kind, big):
        slots = _layer_grad_slots(kind, big)
        started = _exchange_start(list(slots.values()), scatter=True, name=f"scatter_start_{kind}_l{layer}")
        scatter.append((layer, kind, list(slots.keys()), started))
        return started[4]

    cos_t, sin_t = _rope_tables(positions[0])
    loss, dx, dmod, g = _local_step(x[0], loss_target[0], mod, cos_t, sin_t, rep, get_weights, put_grads)

    parts = {k: [None] * W[k].shape[0] for k in BIG}
    res = {}

    def wait_group(entry, after):
        layer, kind, names, started = entry
        srcs, lands = _exchange_wait(started, after, scatter=True, name=f"scatter_wait_{kind}_l{layer}")
        for k, s, z in zip(names, srcs, lands):
            own = lax.dynamic_index_in_dim(s, me, 0, keepdims=False)
            parts[k][_layer_index(k, layer)] = lax.dynamic_update_index_in_dim(z, own, me, 0)

    for entry in scatter[:-1]:
        wait_group(entry, dx)
    early = [k for k in BIG if k not in scatter[-1][2]]
    def update(k):
        outs = _adamw(parts[k], _view(k, W[k]), _view(k, M[k]), _view(k, V[k]), name=f"adamw_{k}")
        return tuple(_view(k, o) for o in outs)

    for k in early:
        res[k] = update(k)
    loss, dmod, done = lax.optimization_barrier((loss, dmod, [res[k] for k in early]))
    for k, r in zip(early, done):
        res[k] = r

    small_local = [dmod.reshape(DEPTH, N_MOD * D_MODEL), g["norm_mix_g"], g["norm_ffn_g"],
                   jnp.transpose(g["gdn_conv_wt"], (0, 2, 1)), g["gdn_a_log"], g["gdn_dt_bias"], g["gdn_norm_g"],
                   g["mla_q_norm_g"], g["mla_kv_norm_g"], g["final_norm_g"], loss.reshape(1)]
    small_shapes = [a.shape for a in small_local]
    (small_g,) = _exchange([_pack(small_local)], scatter=False, name="gather_small_grads")
    small_sum = _unpack(_sum_parts(small_g, name="sum_small_grads"), small_shapes)
    loss = small_sum[-1][0]
    dmod_all = _unpack(small_g, small_shapes[:1], lead=(N_DEV,))[0]
    sg = dict(zip(SMALL, small_sum))
    wait_group(scatter[-1], small_g)
    sg["gdn_conv_w"] = lax.dynamic_slice_in_dim(sg["gdn_conv_w"], me * gdn_conv_w.shape[1], gdn_conv_w.shape[1], 1)
    sg["mla_q_norm_g"] = lax.dynamic_slice_in_dim(sg["mla_q_norm_g"], me * mla_q_norm_g.shape[1],
                                                  mla_q_norm_g.shape[1], 1)
    sg["mla_kv_norm_g"] = lax.dynamic_slice_in_dim(sg["mla_kv_norm_g"], me * mla_kv_norm_g.shape[1],
                                                   mla_kv_norm_g.shape[1], 1)

    dmod_cols = jnp.transpose(lax.dynamic_slice_in_dim(dmod_all, me * wc, wc, axis=2), (1, 0, 2))
    res["ada_w"] = _ada_grad_adamw(c_all, dmod_cols, ada_w, m_ada_w, v_ada_w, name="ada_w_grad_adamw")
    for k in BIG:
        if k not in early:
            res[k] = update(k)
    shapes = [W[k].shape for k in SMALL]
    packed = [_pack([d[k] for k in SMALL]) for d in (sg, W, M, V)]
    outs = _adamw([packed[0][None]], packed[1][None], packed[2][None], packed[3][None], name="adamw_small")
    unpacked = [_unpack(o[0], shapes) for o in outs]
    for i, k in enumerate(SMALL):
        res[k] = tuple(u[i] for u in unpacked)

    return (loss, dx[None], *[res[k][0] for k in WEIGHTS], *[res[k][1] for k in WEIGHTS],
            *[res[k][2] for k in WEIGHTS], *[res[k][3] for k in WEIGHTS])
```

```python
import functools
import math

import jax
import jax.numpy as jnp
from jax import lax
from jax.experimental import pallas as pl
from jax.experimental.pallas import tpu as pltpu

F32 = jnp.float32
BF16 = jnp.bfloat16
MXU_DTYPE = jnp.bfloat16

N_DEV = 8
D_MODEL = 1024
DEPTH = 4
GDN_HEADS = 8
GDN_HEAD_DIM = 128
GDN_KEY_DIM = GDN_HEADS * GDN_HEAD_DIM
GDN_CHUNK = 64
GDN_HEAD_BATCH = 8
GDN_CONV = 4
GDN_PREP_HEADS = 2
GDN_MAIN = 4 * GDN_KEY_DIM
MLA_HEADS = 8
MLA_NOPE = 128
MLA_ROPE = 64
MLA_V = 128
MLA_Q_RANK = 384
MLA_KV_RANK = 256
MLA_IN = MLA_Q_RANK + MLA_KV_RANK + MLA_ROPE
MLA_QK = MLA_NOPE + MLA_ROPE
ROPE_THETA = 10000.0
D_FF = 2816
N_MOD = 6
EPS = 1e-6
LANES = 128
VMEM_LIMIT = 48 * 1024 * 1024

ADAM_LR = 0.001
ADAM_B1 = 0.9
ADAM_B2 = 0.999
ADAM_EPS = 1e-08
ADAM_WD = 0.01
ADAM_STEP = 10
ADAM_BC1 = 1.0 - ADAM_B1 ** ADAM_STEP
ADAM_BC2 = 1.0 - ADAM_B2 ** ADAM_STEP

NN = (((1,), (0,)), ((), ()))
NT = (((1,), (1,)), ((), ()))
TN = (((0,), (0,)), ((), ()))
NEG = -1e30


def _dotb(a, b, dims):
    return lax.dot_general(a.astype(MXU_DTYPE), b.astype(MXU_DTYPE), dims, preferred_element_type=F32)


def _split(a):
    hi = a.astype(BF16)
    return hi, (a - hi.astype(F32)).astype(BF16)


def _dotf(a, b, dims):
    ah, al = _split(a)
    bh, bl = _split(b)
    dot = lambda u, v: lax.dot_general(u, v, dims, preferred_element_type=F32)
    return dot(ah, bh) + (dot(ah, bl) + dot(al, bh))


def _params(*sem):
    return pltpu.CompilerParams(dimension_semantics=sem, vmem_limit_bytes=VMEM_LIMIT)


def _pick(n, pref, mult=LANES):
    best = None
    t = mult
    while t <= min(n, pref):
        if n % t == 0:
            best = t
        t += mult
    return best if best is not None else n


def _sigmoid(z):
    return 1.0 / (1.0 + jnp.exp(-z))


def _exchange(arrays, *, scatter, name):
    n = len(arrays)
    out_shape = tuple(
        jax.ShapeDtypeStruct(a.shape if scatter else (N_DEV,) + a.shape, a.dtype) for a in arrays)

    def body(*refs):
        ins, outs = refs[:n], refs[n:2 * n]
        send_sems, recv_sems, local_sems = refs[2 * n:]
        x, y, c = lax.axis_index("x"), lax.axis_index("y"), lax.axis_index("c")
        me = 4 * x + 2 * y + c
        copies = []
        for k in range(n):
            src_own = ins[k].at[me] if scatter else ins[k]
            own = pltpu.make_async_copy(src_own, outs[k].at[me], local_sems.at[k])
            own.start()
            copies.append(own)
        sends = []
        for p in range(1, N_DEV):
            px, py, pc = x ^ ((p >> 2) & 1), y ^ ((p >> 1) & 1), c ^ (p & 1)
            peer = 4 * px + 2 * py + pc
            for k in range(n):
                cp = pltpu.make_async_remote_copy(
                    src_ref=ins[k].at[peer] if scatter else ins[k],
                    dst_ref=outs[k].at[me],
                    send_sem=send_sems.at[k, p - 1],
                    recv_sem=recv_sems.at[k, p - 1],
                    device_id=(px, py, pc),
                    device_id_type=pl.DeviceIdType.MESH,
                )
                cp.start()
                sends.append((cp, k, peer, p))
        for cp, k, peer, p in sends:
            pltpu.make_async_remote_copy(
                src_ref=ins[k].at[peer] if scatter else ins[k],
                dst_ref=outs[k].at[peer],
                send_sem=send_sems.at[k, p - 1],
                recv_sem=recv_sems.at[k, p - 1],
                device_id=(x, y, c),
                device_id_type=pl.DeviceIdType.MESH,
            ).wait_recv()
        for cp, _, _, _ in sends:
            cp.wait_send()
        for own in copies:
            own.wait()

    any_spec = pl.BlockSpec(memory_space=pl.ANY)
    outs = pl.pallas_call(
        body,
        name=name,
        out_shape=out_shape,
        in_specs=[any_spec] * n,
        out_specs=tuple([any_spec] * n),
        scratch_shapes=[
            pltpu.SemaphoreType.DMA((n, N_DEV - 1)),
            pltpu.SemaphoreType.DMA((n, N_DEV - 1)),
            pltpu.SemaphoreType.DMA((n,)),
        ],
        compiler_params=pltpu.CompilerParams(has_side_effects=True),
    )(*arrays)
    return list(outs)


def _gather_two_level(arrays, *, name):
    n = len(arrays)
    out_shape = tuple(jax.ShapeDtypeStruct((N_DEV,) + a.shape, a.dtype) for a in arrays)

    def body(*refs):
        ins, outs = refs[:n], refs[n:2 * n]
        send_sems, recv_sems, local_sems = refs[2 * n:]
        x, y, c = lax.axis_index("x"), lax.axis_index("y"), lax.axis_index("c")
        me = 4 * x + 2 * y + c
        sibling = (x, y, 1 - c)
        chips = [(1 - x, y), (x, 1 - y), (1 - x, 1 - y)]

        def slot(px, py, pc):
            return 4 * px + 2 * py + pc

        def copy(k, q, block, to, src=None):
            return pltpu.make_async_remote_copy(
                src_ref=outs[k].at[slot(*block)] if src is None else src,
                dst_ref=outs[k].at[slot(*block)],
                send_sem=send_sems.at[k, q], recv_sem=recv_sems.at[k, q],
                device_id=to, device_id_type=pl.DeviceIdType.MESH)

        own = [pltpu.make_async_copy(ins[k], outs[k].at[me], local_sems.at[k]) for k in range(n)]
        for cp in own:
            cp.start()
        first = []
        for k in range(n):
            first.append(copy(k, 0, (x, y, c), sibling, src=ins[k]))
            first += [copy(k, 1 + j, (x, y, c), (*chip, c), src=ins[k]) for j, chip in enumerate(chips)]
        for cp in first:
            cp.start()
        passed = []
        for j, chip in enumerate(chips):
            for k in range(n):
                copy(k, 1 + j, (*chip, c), (x, y, c)).wait_recv()
                fwd = copy(k, 4 + j, (*chip, c), sibling)
                fwd.start()
                passed.append(fwd)
        for k in range(n):
            copy(k, 0, sibling, (x, y, c)).wait_recv()
            for j, chip in enumerate(chips):
                copy(k, 4 + j, (*chip, 1 - c), (x, y, c)).wait_recv()
        for cp in first + passed:
            cp.wait_send()
        for cp in own:
            cp.wait()

    any_spec = pl.BlockSpec(memory_space=pl.ANY)
    outs = pl.pallas_call(
        body, name=name, out_shape=out_shape, in_specs=[any_spec] * n, out_specs=tuple([any_spec] * n),
        scratch_shapes=[pltpu.SemaphoreType.DMA((n, N_DEV - 1)), pltpu.SemaphoreType.DMA((n, N_DEV - 1)),
                        pltpu.SemaphoreType.DMA((n,))],
        compiler_params=pltpu.CompilerParams(has_side_effects=True),
    )(*arrays)
    return list(outs)


def _peer(x, y, c, p):
    return x ^ ((p >> 2) & 1), y ^ ((p >> 1) & 1), c ^ (p & 1)


def _exchange_start(arrays, *, scatter, name, dep=None):
    n = len(arrays)
    deps = [] if dep is None else [dep]
    lands = [lax.empty(a.shape if scatter else (N_DEV,) + a.shape, a.dtype) for a in arrays]

    def body(*refs):
        ins, zones = refs[:n], refs[n:2 * n]
        send_sems, recv_sems = refs[2 * n + len(deps)], refs[2 * n + len(deps) + 1]
        token = refs[-1]
        x, y, c = lax.axis_index("x"), lax.axis_index("y"), lax.axis_index("c")
        me = 4 * x + 2 * y + c
        for p in range(1, N_DEV):
            px, py, pc = _peer(x, y, c, p)
            for k in range(n):
                pltpu.make_async_remote_copy(
                    src_ref=ins[k].at[4 * px + 2 * py + pc] if scatter else ins[k],
                    dst_ref=zones[k].at[me],
                    send_sem=send_sems.at[k * (N_DEV - 1) + p - 1],
                    recv_sem=recv_sems.at[k * (N_DEV - 1) + p - 1],
                    device_id=(px, py, pc),
                    device_id_type=pl.DeviceIdType.MESH,
                ).start()
        token[...] = jnp.zeros_like(token)

    hbm = pl.BlockSpec(memory_space=pltpu.HBM)
    sem = pl.BlockSpec(memory_space=pltpu.SEMAPHORE)
    outs = pl.pallas_call(
        body,
        name=name,
        out_shape=(pltpu.SemaphoreType.DMA((n * (N_DEV - 1),)), pltpu.SemaphoreType.DMA((n * (N_DEV - 1),)),
                   *[pltpu.HBM(a.shape, a.dtype) for a in arrays], *[pltpu.HBM(z.shape, z.dtype) for z in lands],
                   jax.ShapeDtypeStruct((8, LANES), F32)),
        in_specs=[hbm] * (2 * n) + [pl.BlockSpec(memory_space=pl.ANY)] * len(deps),
        out_specs=(sem, sem, *[hbm] * (2 * n), pl.BlockSpec(memory_space=pltpu.VMEM)),
        input_output_aliases={k: 2 + k for k in range(2 * n)},
        compiler_params=pltpu.CompilerParams(has_side_effects=pltpu.SideEffectType.DATAFLOW_SIDE_EFFECTING),
    )(*[pltpu.with_memory_space_constraint(a, pltpu.HBM) for a in arrays],
      *[pltpu.with_memory_space_constraint(z, pltpu.HBM) for z in lands], *deps)
    return outs[0], outs[1], list(outs[2:2 + n]), list(outs[2 + n:2 + 2 * n]), outs[-1]


def _exchange_wait(started, after, *, scatter, name):
    send_sems, recv_sems, srcs, lands, _ = started
    n = len(srcs)

    def body(*refs):
        ins, zones = refs[:n], refs[n:2 * n]
        s_sems, r_sems = refs[2 * n], refs[2 * n + 1]
        x, y, c = lax.axis_index("x"), lax.axis_index("y"), lax.axis_index("c")
        for p in range(1, N_DEV):
            px, py, pc = _peer(x, y, c, p)
            peer = 4 * px + 2 * py + pc
            for k in range(n):
                cp = pltpu.make_async_remote_copy(
                    src_ref=ins[k].at[peer] if scatter else ins[k],
                    dst_ref=zones[k].at[peer],
                    send_sem=s_sems.at[k * (N_DEV - 1) + p - 1],
                    recv_sem=r_sems.at[k * (N_DEV - 1) + p - 1],
                    device_id=(px, py, pc),
                    device_id_type=pl.DeviceIdType.MESH,
                )
                cp.wait_send()
                cp.wait_recv()

    hbm = pl.BlockSpec(memory_space=pltpu.HBM)
    sem = pl.BlockSpec(memory_space=pltpu.SEMAPHORE)
    outs = pl.pallas_call(
        body,
        name=name,
        out_shape=tuple(pltpu.HBM(a.shape, a.dtype) for a in srcs + lands),
        in_specs=[hbm] * (2 * n) + [sem, sem, pl.BlockSpec(memory_space=pl.ANY)],
        out_specs=tuple([hbm] * (2 * n)),
        input_output_aliases={k: k for k in range(2 * n)},
        compiler_params=pltpu.CompilerParams(has_side_effects=pltpu.SideEffectType.DATAFLOW_SIDE_EFFECTING),
    )(*srcs, *lands, send_sems, recv_sems, after)
    return list(outs[:n]), list(outs[n:])


def _mm(a, b, *, mode, out_dtype, name, add=None, tm=512, tn=512, b_rows=None, dep=None):
    rows_b = b.shape[0] if b_rows is None else b_rows
    if mode == "nn":
        (m, kd), nd = a.shape, b.shape[1]
        assert kd == rows_b
    elif mode == "nt":
        (m, kd), nd = a.shape, rows_b
    else:
        (kd, m), nd = a.shape, b.shape[1]
    tm = _pick(m, tm, LANES if mode == "tn" else 16)
    tn = _pick(nd, tn)
    dims = {"nn": NN, "nt": NT, "tn": TN}[mode]
    ni, nj = m // tm, nd // tn
    a_bytes, b_bytes = a.size * a.dtype.itemsize, b.size * b.dtype.itemsize
    i_outer = a_bytes + ni * b_bytes <= b_bytes + nj * a_bytes
    ij = (lambda g0, g1: (g0, g1)) if i_outer else (lambda g0, g1: (g1, g0))
    a_spec = (pl.BlockSpec((kd, tm), lambda g0, g1: (0, ij(g0, g1)[0])) if mode == "tn"
              else pl.BlockSpec((tm, kd), lambda g0, g1: (ij(g0, g1)[0], 0)))
    b_spec = (pl.BlockSpec((tn, kd), lambda g0, g1: (ij(g0, g1)[1], 0)) if mode == "nt"
              else pl.BlockSpec((kd, tn), lambda g0, g1: (0, ij(g0, g1)[1])))
    o_spec = pl.BlockSpec((tm, tn), lambda g0, g1: ij(g0, g1))
    has_add = add is not None

    def body(*refs):
        a_ref, b_ref = refs[0], refs[1]
        o_ref = refs[-1]
        acc = _dotb(a_ref[...], b_ref[...], dims)
        if has_add:
            acc = acc + refs[2][...].astype(F32)
        o_ref[...] = acc.astype(o_ref.dtype)

    ins = [a, b] + ([add] if has_add else []) + ([] if dep is None else [dep])
    specs = ([a_spec, b_spec] + ([o_spec] if has_add else [])
             + ([] if dep is None else [pl.BlockSpec((8, LANES), lambda g0, g1: (0, 0))]))
    return pl.pallas_call(
        body, name=name, grid=(ni, nj) if i_outer else (nj, ni), in_specs=specs, out_specs=o_spec,
        out_shape=jax.ShapeDtypeStruct((m, nd), out_dtype),
        compiler_params=_params("parallel", "parallel"),
    )(*ins)


def _mm_resid(a, b, x, gate, *, name, tm=256, tn=1024):
    m, kd = a.shape
    nd = b.shape[1]
    tm = _pick(m, tm, 16)
    tn = _pick(nd, tn)
    o_spec = pl.BlockSpec((tm, tn), lambda i, j: (i, j))

    def body(a_ref, b_ref, x_ref, g_ref, xo_ref, y_ref):
        y = _dotb(a_ref[...], b_ref[...], NN)
        y_ref[...] = y
        xo_ref[...] = x_ref[...] + g_ref[...] * y

    return pl.pallas_call(
        body, name=name, grid=(m // tm, nd // tn),
        in_specs=[pl.BlockSpec((tm, kd), lambda i, j: (i, 0)), pl.BlockSpec((kd, tn), lambda i, j: (0, j)),
                  o_spec, pl.BlockSpec((1, tn), lambda i, j: (0, j))],
        out_specs=(o_spec, o_spec),
        out_shape=(jax.ShapeDtypeStruct((m, nd), F32), jax.ShapeDtypeStruct((m, nd), F32)),
        compiler_params=_params("parallel", "parallel"),
    )(a, b, x, gate)


ROWS = 256


def _row_spec(width, rows=ROWS):
    return pl.BlockSpec((rows, width), lambda i: (i, 0))


def _const_spec(shape):
    return pl.BlockSpec(shape, lambda i: tuple(0 for _ in shape))


def _adaln_fwd(x, g, scale, shift, *, name):
    t, d = x.shape

    def body(x_ref, g_ref, sc_ref, sh_ref, h_ref):
        xv = x_ref[...]
        r = lax.rsqrt(jnp.mean(xv * xv, axis=-1, keepdims=True) + EPS)
        h_ref[...] = (xv * r * g_ref[...] * (1.0 + sc_ref[...]) + sh_ref[...]).astype(h_ref.dtype)

    return pl.pallas_call(
        body, name=name, grid=(t // ROWS,),
        in_specs=[_row_spec(d), _const_spec((1, d)), _const_spec((1, d)), _const_spec((1, d))],
        out_specs=_row_spec(d), out_shape=jax.ShapeDtypeStruct((t, d), BF16),
        compiler_params=_params("parallel"),
    )(x, g, scale, shift)


def _adaln_bwd(x, g, scale, shift, dh, dres, dep, *, name):
    t, d = x.shape

    def body(x_ref, g_ref, sc_ref, sh_ref, dh_ref, dr_ref, dep_ref, dx_ref, st_ref):
        @pl.when(pl.program_id(0) == 0)
        def _():
            st_ref[...] = jnp.zeros_like(st_ref)

        xv = x_ref[...]
        dhv = dh_ref[...].astype(F32)
        gv = g_ref[...]
        r = lax.rsqrt(jnp.mean(xv * xv, axis=-1, keepdims=True) + EPS)
        xh = xv * r
        nv = xh * gv
        dn = dhv * (1.0 + sc_ref[...])
        dxh = dn * gv
        dx_ref[...] = dr_ref[...] + r * (dxh - xh * jnp.mean(dxh * xh, axis=-1, keepdims=True))
        st_ref[0:1, :] += jnp.sum(dn * xh, axis=0, keepdims=True)
        st_ref[1:2, :] += jnp.sum(dhv * nv, axis=0, keepdims=True)
        st_ref[2:3, :] += jnp.sum(dhv, axis=0, keepdims=True)

    return pl.pallas_call(
        body, name=name, grid=(t // ROWS,),
        in_specs=[_row_spec(d), _const_spec((1, d)), _const_spec((1, d)), _const_spec((1, d)),
                  _row_spec(d), _row_spec(d), _const_spec((8, LANES))],
        out_specs=(_row_spec(d), _const_spec((8, d))),
        out_shape=(jax.ShapeDtypeStruct((t, d), F32), jax.ShapeDtypeStruct((8, d), F32)),
        compiler_params=_params("arbitrary"),
    )(x, g, scale, shift, dh, dres, dep)


def _gate_bwd(dxo, y, gate, dep, *, name):
    t, d = dxo.shape

    def body(dx_ref, y_ref, g_ref, dep_ref, dy_ref, st_ref):
        @pl.when(pl.program_id(0) == 0)
        def _():
            st_ref[...] = jnp.zeros_like(st_ref)

        dxv = dx_ref[...]
        dy_ref[...] = (dxv * g_ref[...]).astype(dy_ref.dtype)
        st_ref[0:1, :] += jnp.sum(dxv * y_ref[...], axis=0, keepdims=True)

    return pl.pallas_call(
        body, name=name, grid=(t // ROWS,),
        in_specs=[_row_spec(d), _row_spec(d), _const_spec((1, d)), _const_spec((8, LANES))],
        out_specs=(_row_spec(d), _const_spec((8, d))),
        out_shape=(jax.ShapeDtypeStruct((t, d), BF16), jax.ShapeDtypeStruct((8, d), F32)),
        compiler_params=_params("arbitrary"),
    )(dxo, y, gate, dep)


def _loss_head(x, g, target, *, name):
    t, d = x.shape

    def body(x_ref, g_ref, t_ref, dx_ref, st_ref, ls_ref):
        @pl.when(pl.program_id(0) == 0)
        def _():
            st_ref[...] = jnp.zeros_like(st_ref)
            ls_ref[...] = jnp.zeros_like(ls_ref)

        xv = x_ref[...]
        gv = g_ref[...]
        r = lax.rsqrt(jnp.mean(xv * xv, axis=-1, keepdims=True) + EPS)
        xh = xv * r
        err = xh * gv - t_ref[...]
        ls_ref[...] += 0.5 * jnp.sum(jnp.mean(err * err, axis=-1, keepdims=True))
        dy = err * (1.0 / d)
        dxh = dy * gv
        dx_ref[...] = r * (dxh - xh * jnp.mean(dxh * xh, axis=-1, keepdims=True))
        st_ref[0:1, :] += jnp.sum(dy * xh, axis=0, keepdims=True)

    return pl.pallas_call(
        body, name=name, grid=(t // ROWS,),
        in_specs=[_row_spec(d), _const_spec((1, d)), _row_spec(d)],
        out_specs=(_row_spec(d), _const_spec((8, d)), _const_spec((8, LANES))),
        out_shape=(jax.ShapeDtypeStruct((t, d), F32), jax.ShapeDtypeStruct((8, d), F32),
                   jax.ShapeDtypeStruct((8, LANES), F32)),
        compiler_params=_params("arbitrary"),
    )(x, g, target)


FFN_BLOCK = D_FF // 2


def _ffn_gu_fwd(h, wg, wu, dep, *, name):
    t, d = h.shape
    tn = FFN_BLOCK

    def body(h_ref, wg_ref, wu_ref, dep_ref, s_ref, a_ref, b_ref):
        hv = h_ref[...]
        a = _dotb(hv, wg_ref[...], NT)
        b = _dotb(hv, wu_ref[...], NT)
        s_ref[...] = (a * _sigmoid(a) * b).astype(s_ref.dtype)
        a_ref[...] = a.astype(a_ref.dtype)
        b_ref[...] = b.astype(b_ref.dtype)

    w_spec = pl.BlockSpec((tn, d), lambda j, i: (j, 0))
    o_spec = pl.BlockSpec((ROWS, tn), lambda j, i: (i, j))
    return pl.pallas_call(
        body, name=name, grid=(D_FF // tn, t // ROWS),
        in_specs=[pl.BlockSpec((ROWS, d), lambda j, i: (i, 0)), w_spec, w_spec,
                  pl.BlockSpec((8, LANES), lambda j, i: (0, 0))],
        out_specs=(o_spec, o_spec, o_spec),
        out_shape=(jax.ShapeDtypeStruct((t, D_FF), BF16),) * 3,
        compiler_params=_params("parallel", "parallel"),
    )(h, wg, wu, dep)


def _ffn_down_dx(dy, w_down, a, b, *, name):
    t, d = dy.shape
    tn = FFN_BLOCK

    def body(dy_ref, w_ref, a_ref, b_ref, da_ref, db_ref):
        ds = _dotb(dy_ref[...], w_ref[...], NT)
        av = a_ref[...].astype(F32)
        sg = _sigmoid(av)
        da_ref[...] = (ds * b_ref[...].astype(F32) * sg * (1.0 + av * (1.0 - sg))).astype(da_ref.dtype)
        db_ref[...] = (ds * av * sg).astype(db_ref.dtype)

    o_spec = pl.BlockSpec((ROWS, tn), lambda j, i: (i, j))
    return pl.pallas_call(
        body, name=name, grid=(D_FF // tn, t // ROWS),
        in_specs=[pl.BlockSpec((ROWS, d), lambda j, i: (i, 0)), pl.BlockSpec((tn, d), lambda j, i: (j, 0)),
                  o_spec, o_spec],
        out_specs=(o_spec, o_spec),
        out_shape=(jax.ShapeDtypeStruct((t, D_FF), BF16),) * 2,
        compiler_params=_params("parallel", "parallel"),
    )(dy, w_down, a, b)


def _shift_rows(v, s, rows):
    if s == 0:
        return v
    return jnp.where(rows >= s, pltpu.roll(v, s, 0), 0.0)


def _unshift_rows(v, s, rows, t):
    if s == 0:
        return v
    return jnp.where(rows < t - s, pltpu.roll(v, t - s, 0), 0.0)


def _conv_silu(x, w, rows):
    z = w[GDN_CONV - 1:GDN_CONV, :] * x
    for j in range(GDN_CONV - 1):
        z = z + w[j:j + 1, :] * _shift_rows(x, GDN_CONV - 1 - j, rows)
    sg = _sigmoid(z)
    return z, sg, z * sg


def _gdn_prep_fwd(proj, conv_wt, *, name):
    t = proj.shape[0]
    nh = GDN_HEADS

    hp = GDN_PREP_HEADS
    wd = hp * LANES

    def body(x_ref, w_ref, y_ref):
        j = pl.program_id(0) * hp
        rows = lax.broadcasted_iota(jnp.int32, (t, LANES), 0)
        qscale = jnp.where(j < nh, GDN_HEAD_DIM ** -0.5, 1.0)
        for i in range(hp):
            sl = slice(i * LANES, (i + 1) * LANES)
            _, _, s = _conv_silu(x_ref[:, sl], w_ref[:, sl], rows)
            rs = lax.rsqrt(jnp.sum(s * s, axis=-1, keepdims=True) + EPS)
            y_ref[:, sl] = jnp.where(j < 2 * nh, s * rs * qscale, s)

    return pl.pallas_call(
        body, name=name, grid=(3 * nh // hp,),
        in_specs=[pl.BlockSpec((t, wd), lambda j: (0, j)), pl.BlockSpec((GDN_CONV, wd), lambda j: (0, j))],
        out_specs=pl.BlockSpec((t, wd), lambda j: (0, j)),
        out_shape=jax.ShapeDtypeStruct((t, 3 * GDN_KEY_DIM), F32),
        compiler_params=_params("parallel"),
    )(proj, conv_wt)


def _gdn_prep_bwd(proj, conv_wt, dy, *, name):
    t = proj.shape[0]
    nh = GDN_HEADS

    hp = GDN_PREP_HEADS
    wd = hp * LANES
    per_seg = nh // hp

    def body(x_ref, w_ref, dy_ref, dx_ref, dw_ref):
        j = pl.program_id(0) * hp
        rows = lax.broadcasted_iota(jnp.int32, (t, LANES), 0)
        qscale = jnp.where(j < nh, GDN_HEAD_DIM ** -0.5, 1.0)
        for i in range(hp):
            sl = slice(i * LANES, (i + 1) * LANES)
            x = x_ref[:, sl]
            w = w_ref[:, sl]
            z, sg, s = _conv_silu(x, w, rows)
            rs = lax.rsqrt(jnp.sum(s * s, axis=-1, keepdims=True) + EPS)
            dyv = dy_ref[:, sl]
            nv = s * rs
            de = dyv * qscale
            ds_qk = rs * (de - nv * jnp.sum(de * nv, axis=-1, keepdims=True))
            ds = jnp.where(j < 2 * nh, ds_qk, dyv)
            dz = ds * sg * (1.0 + z * (1.0 - sg))
            dx = w[GDN_CONV - 1:GDN_CONV, :] * dz
            dw_ref[GDN_CONV - 1:GDN_CONV, sl] = jnp.sum(dz * x, axis=0, keepdims=True)
            for k in range(GDN_CONV - 1):
                sh = GDN_CONV - 1 - k
                dx = dx + w[k:k + 1, :] * _unshift_rows(dz, sh, rows, t)
                dw_ref[k:k + 1, sl] = jnp.sum(dz * _shift_rows(x, sh, rows), axis=0, keepdims=True)
            dx_ref[:, sl] = dx.astype(dx_ref.dtype)

    return pl.pallas_call(
        body, name=name, grid=(3 * nh // hp,),
        in_specs=[pl.BlockSpec((t, wd), lambda j: (0, j)), pl.BlockSpec((GDN_CONV, wd), lambda j: (0, j)),
                  pl.BlockSpec((None, t, wd), lambda j: (j // per_seg, 0, j % per_seg))],
        out_specs=(pl.BlockSpec((t, wd), lambda j: (0, j)), pl.BlockSpec((GDN_CONV, wd), lambda j: (0, j))),
        out_shape=(jax.ShapeDtypeStruct((t, 3 * GDN_KEY_DIM), BF16),
                   jax.ShapeDtypeStruct((GDN_CONV, 3 * GDN_KEY_DIM), F32)),
        compiler_params=_params("parallel"),
    )(proj, conv_wt, dy)


def _softplus(z):
    return jnp.maximum(z, 0.0) + jnp.log(1.0 + jnp.exp(-jnp.abs(z)))


def _gdn_gate_fwd(ab, prm, *, name):
    t = ab.shape[0]

    def body(ab_ref, p_ref, o_ref):
        v = ab_ref[...]
        lane = lax.broadcasted_iota(jnp.int32, v.shape, 1)
        g = -jnp.exp(p_ref[0:1, :]) * _softplus(v + p_ref[1:2, :])
        o_ref[...] = jnp.where(lane < GDN_HEADS, g, jnp.where(lane < 2 * GDN_HEADS, _sigmoid(v), 0.0))

    return pl.pallas_call(
        body, name=name, grid=(t // ROWS,),
        in_specs=[_row_spec(LANES), _const_spec((8, LANES))], out_specs=_row_spec(LANES),
        out_shape=jax.ShapeDtypeStruct((t, LANES), F32), compiler_params=_params("parallel"),
    )(ab, prm)


def _gdn_gate_bwd(ab, prm, dgb, *, name):
    t = ab.shape[0]

    def body(ab_ref, p_ref, d_ref, o_ref, st_ref):
        @pl.when(pl.program_id(0) == 0)
        def _():
            st_ref[...] = jnp.zeros_like(st_ref)

        v = ab_ref[...]
        dv = d_ref[...]
        lane = lax.broadcasted_iota(jnp.int32, v.shape, 1)
        is_a = lane < GDN_HEADS
        is_b = jnp.logical_and(lane >= GDN_HEADS, lane < 2 * GDN_HEADS)
        a_exp = jnp.exp(p_ref[0:1, :])
        zz = v + p_ref[1:2, :]
        g = -a_exp * _softplus(zz)
        da = dv * (-a_exp) * _sigmoid(zz)
        beta = _sigmoid(v)
        db = dv * beta * (1.0 - beta)
        o_ref[...] = jnp.where(is_a, da, jnp.where(is_b, db, 0.0)).astype(o_ref.dtype)
        st_ref[0:1, :] += jnp.sum(jnp.where(is_a, dv * g, 0.0), axis=0, keepdims=True)
        st_ref[1:2, :] += jnp.sum(jnp.where(is_a, da, 0.0), axis=0, keepdims=True)

    return pl.pallas_call(
        body, name=name, grid=(t // ROWS,),
        in_specs=[_row_spec(LANES), _const_spec((8, LANES)), _row_spec(LANES)],
        out_specs=(_row_spec(LANES), _const_spec((8, LANES))),
        out_shape=(jax.ShapeDtypeStruct((t, LANES), BF16), jax.ShapeDtypeStruct((8, LANES), F32)),
        compiler_params=_params("arbitrary"),
    )(ab, prm, dgb)


def _gdn_local(qs, ks, vs, gbs, bbs):
    nh = len(qs)
    cs = qs[0].shape[0]
    hs = range(nh)
    r = lax.broadcasted_iota(jnp.int32, (cs, cs), 0)
    c = lax.broadcasted_iota(jnp.int32, (cs, cs), 1)
    tril, strict, eye = r >= c, r > c, r == c
    ident = jnp.where(eye, 1.0, 0.0)
    g_colb = [gbs[h][:, :cs] for h in hs]
    g_row = [jnp.sum(jnp.where(eye, g_colb[h], 0.0), axis=0, keepdims=True) for h in hs]
    gc_col = [jnp.sum(jnp.where(tril, g_row[h], 0.0), axis=1, keepdims=True) for h in hs]
    gc_row = [jnp.sum(jnp.where(r <= c, g_colb[h], 0.0), axis=0, keepdims=True) for h in hs]
    decay = [jnp.exp(jnp.where(tril, gc_col[h] - gc_row[h], NEG)) for h in hs]
    gamma = [jnp.exp(gc_col[h]) for h in hs]
    gcl = [gc_col[h][cs - 1:cs, :] for h in hs]
    gl = [jnp.exp(gcl[h]) for h in hs]
    kdec = [jnp.exp(gcl[h] - gc_col[h]) for h in hs]
    kb = [ks[h] * bbs[h] for h in hs]
    kk = [_dotb(kb[h], ks[h], NT) for h in hs]
    qk = [_dotb(qs[h], ks[h], NT) for h in hs]
    lmat = [jnp.where(strict, kk[h] * decay[h], 0.0) for h in hs]
    pmat = [jnp.where(tril, qk[h] * decay[h], 0.0) for h in hs]
    xm = [-lmat[h] for h in hs]
    tinv = [ident + xm[h] for h in hs]
    for _ in range(int(math.log2(cs)) - 1):
        xm = [_dotf(xm[h], xm[h], NN) for h in hs]
        tinv = [tinv[h] + _dotf(tinv[h], xm[h], NN) for h in hs]
    vb = [vs[h] * bbs[h] for h in hs]
    kg = [kb[h] * gamma[h] for h in hs]
    u = [_dotf(tinv[h], vb[h], NN) for h in hs]
    w = [_dotf(tinv[h], kg[h], NN) for h in hs]
    return [dict(tril=tril, strict=strict, eye=eye, r=r, c=c, decay=decay[h], gamma=gamma[h], gl=gl[h], kdec=kdec[h],
                 kb=kb[h], lmat=lmat[h], tinv=tinv[h], vb=vb[h], kg=kg[h], u=u[h], w=w[h], pmat=pmat[h],
                 qd=qs[h] * gamma[h], kd=ks[h] * kdec[h]) for h in hs]


def _gdn_chunk_fwd(qkv, gbc, bbc, *, name):
    t = qkv.shape[0]
    nh, cs, hd = GDN_HEADS, GDN_CHUNK, GDN_HEAD_DIM
    nc = t // cs

    hb = GDN_HEAD_BATCH
    ng = nh // hb

    def body(q_ref, k_ref, v_ref, g_ref, b_ref, o_ref, st_ref, s_ref):
        @pl.when(pl.program_id(1) == 0)
        def _():
            s_ref[...] = jnp.zeros_like(s_ref)

        sls = [slice(i * hd, (i + 1) * hd) for i in range(hb)]
        hs = range(hb)
        s = [s_ref[i] for i in hs]
        lo = _gdn_local([q_ref[:, sl] for sl in sls], [k_ref[:, sl] for sl in sls], [v_ref[:, sl] for sl in sls],
                        [g_ref[i] for i in hs], [b_ref[i] for i in hs])
        ws = [_dotb(lo[i]["w"], s[i], NN) for i in hs]
        qs = [_dotb(lo[i]["qd"], s[i], NN) for i in hs]
        vn = [lo[i]["u"] - ws[i] for i in hs]
        pv = [_dotb(lo[i]["pmat"], vn[i], NN) for i in hs]
        kv = [_dotb(lo[i]["kd"], vn[i], TN) for i in hs]
        for i, sl in enumerate(sls):
            st_ref[i, 0] = s[i]
            o_ref[:, sl] = qs[i] + pv[i]
            s_ref[i] = s[i] * lo[i]["gl"] + kv[i]

    gspec = pl.BlockSpec((hb, cs, LANES), lambda h, n: (h, n, 0))
    col = lambda off: pl.BlockSpec((cs, hb * hd), lambda h, n: (n, off + h))
    return pl.pallas_call(
        body, name=name, grid=(ng, nc),
        in_specs=[col(0), col(ng), col(2 * ng), gspec, gspec],
        out_specs=(col(0), pl.BlockSpec((hb, 1, hd, hd), lambda h, n: (h, n, 0, 0))),
        out_shape=(jax.ShapeDtypeStruct((t, nh * hd), F32), jax.ShapeDtypeStruct((nh, nc, hd, hd), F32)),
        scratch_shapes=[pltpu.VMEM((hb, hd, hd), F32)],
        compiler_params=_params("parallel", "arbitrary"),
    )(qkv, qkv, qkv, gbc, bbc)


def _gdn_chunk_bwd(qkv, gbc, bbc, states, do, *, name):
    t = qkv.shape[0]
    nh, cs, hd = GDN_HEADS, GDN_CHUNK, GDN_HEAD_DIM
    nc = t // cs

    hb = GDN_HEAD_BATCH
    ng = nh // hb

    def heads_bwd(q, k, v, gb, bb, s, dsn, dov):
        hs = range(len(q))
        lo = _gdn_local(q, k, v, gb, bb)
        tril, strict, eye, r, c = lo[0]["tril"], lo[0]["strict"], lo[0]["eye"], lo[0]["r"], lo[0]["c"]
        rowi = lax.broadcasted_iota(jnp.int32, (cs, 1), 0)
        get = lambda name: [lo[h][name] for h in hs]
        decay, gamma, gl, kdec = get("decay"), get("gamma"), get("gl"), get("kdec")
        kb, tinv, w, pmat, kd, qd = get("kb"), get("tinv"), get("w"), get("pmat"), get("kd"), get("qd")
        ws = [_dotb(w[h], s[h], NN) for h in hs]
        pdo = [_dotb(pmat[h], dov[h], TN) for h in hs]
        kds = [_dotb(kd[h], dsn[h], NN) for h in hs]
        dqd = [_dotb(dov[h], s[h], NT) for h in hs]
        qdo = [_dotb(qd[h], dov[h], TN) for h in hs]
        vn = [lo[h]["u"] - ws[h] for h in hs]
        dvn = [pdo[h] + kds[h] for h in hs]
        dp = [jnp.where(tril, _dotb(dov[h], vn[h], NT), 0.0) for h in hs]
        dkd = [_dotb(vn[h], dsn[h], NT) for h in hs]
        dw = [-_dotb(dvn[h], s[h], NT) for h in hs]
        wdv = [_dotb(w[h], dvn[h], TN) for h in hs]
        dvb = [_dotf(tinv[h], dvn[h], TN) for h in hs]
        dt1 = [_dotf(dvn[h], lo[h]["vb"], NT) for h in hs]
        dkg = [_dotf(tinv[h], dw[h], TN) for h in hs]
        dt2 = [_dotf(dw[h], lo[h]["kg"], NT) for h in hs]
        tdt = [_dotf(tinv[h], dt1[h] + dt2[h], TN) for h in hs]
        dl = [jnp.where(strict, -_dotf(tdt[h], tinv[h], NT), 0.0) for h in hs]
        dkk = [dl[h] * decay[h] for h in hs]
        dqk = [dp[h] * decay[h] for h in hs]
        dkb = [_dotb(dkk[h], k[h], NN) + dkg[h] * gamma[h] for h in hs]
        dk1 = [_dotb(dkk[h], kb[h], TN) for h in hs]
        dk2 = [_dotb(dqk[h], q[h], TN) for h in hs]
        dq1 = [_dotb(dqk[h], k[h], NN) for h in hs]
        out = []
        for h in hs:
            dgl = jnp.sum(jnp.sum(dsn[h] * s[h], axis=1, keepdims=True), axis=0, keepdims=True)
            ds_prev = gl[h] * dsn[h] + qdo[h] - wdv[h]
            dk = dk1[h] + dk2[h] + dkd[h] * kdec[h] + dkb[h] * bb[h]
            dq = dq1[h] + dqd[h] * gamma[h]
            dbeta = jnp.sum(dvb[h] * v[h], axis=-1, keepdims=True) + jnp.sum(dkb[h] * k[h], axis=-1, keepdims=True)
            e = dl[h] * lo[h]["lmat"] + dp[h] * pmat[h]
            e_col = jnp.sum(e, axis=0, keepdims=True)
            dgc = jnp.sum(e, axis=1, keepdims=True) - jnp.sum(jnp.where(eye, e_col, 0.0), axis=1, keepdims=True)
            dgamma = (jnp.sum(dqd[h] * q[h], axis=-1, keepdims=True)
                      + jnp.sum(dkg[h] * kb[h], axis=-1, keepdims=True))
            rk = jnp.sum(dkd[h] * k[h], axis=-1, keepdims=True) * kdec[h]
            dgcl = jnp.sum(rk, axis=0, keepdims=True) + dgl * gl[h]
            dgc = dgc + dgamma * gamma[h] - rk + jnp.where(rowi == cs - 1, dgcl, 0.0)
            dgc_row = jnp.sum(jnp.where(eye, dgc, 0.0), axis=0, keepdims=True)
            dg = jnp.sum(jnp.where(c >= r, dgc_row, 0.0), axis=1, keepdims=True)
            out.append((dq, dk, dvb[h] * bb[h], dbeta, dg, ds_prev))
        return out

    def body(q_ref, k_ref, v_ref, g_ref, b_ref, st_ref, do_ref, d_ref, dg_ref, db_ref, ds_ref):
        @pl.when(pl.program_id(1) == 0)
        def _():
            ds_ref[...] = jnp.zeros_like(ds_ref)

        sls = [slice(i * hd, (i + 1) * hd) for i in range(hb)]
        hs = range(hb)
        outs = heads_bwd([q_ref[:, sl] for sl in sls], [k_ref[:, sl] for sl in sls], [v_ref[:, sl] for sl in sls],
                         [g_ref[i] for i in hs], [b_ref[i] for i in hs], [st_ref[i, 0] for i in hs],
                         [ds_ref[i] for i in hs], [do_ref[:, sl] for sl in sls])
        for i, sl in enumerate(sls):
            dq, dk, dv, dbeta, dg, ds_prev = outs[i]
            d_ref[0, :, sl], d_ref[1, :, sl], d_ref[2, :, sl] = dq, dk, dv
            db_ref[i] = jnp.broadcast_to(dbeta, (cs, LANES))
            dg_ref[i] = jnp.broadcast_to(dg, (cs, LANES))
            ds_ref[i] = ds_prev

    gspec = pl.BlockSpec((hb, cs, LANES), lambda h, n: (h, nc - 1 - n, 0))
    col = lambda off: pl.BlockSpec((cs, hb * hd), lambda h, n: (nc - 1 - n, off + h))
    return pl.pallas_call(
        body, name=name, grid=(ng, nc),
        in_specs=[col(0), col(ng), col(2 * ng), gspec, gspec,
                  pl.BlockSpec((hb, 1, hd, hd), lambda h, n: (h, nc - 1 - n, 0, 0)), col(0)],
        out_specs=(pl.BlockSpec((3, cs, hb * hd), lambda h, n: (0, nc - 1 - n, h)), gspec, gspec),
        out_shape=(jax.ShapeDtypeStruct((3, t, nh * hd), F32),) + (jax.ShapeDtypeStruct((nh, t, LANES), F32),) * 2,
        scratch_shapes=[pltpu.VMEM((hb, hd, hd), F32)],
        compiler_params=_params("parallel", "arbitrary"),
    )(qkv, qkv, qkv, gbc, bbc, states, do)


def _gdn_onorm_fwd(o, proj, norm_g, *, name):
    t = o.shape[0]
    w = GDN_KEY_DIM
    goff = 3 * GDN_KEY_DIM // w

    def body(o_ref, gp_ref, g_ref, y_ref):
        gv = g_ref[...]
        for h in range(GDN_HEADS):
            sl = slice(h * GDN_HEAD_DIM, (h + 1) * GDN_HEAD_DIM)
            oh = o_ref[:, sl]
            gp = gp_ref[:, sl]
            r = lax.rsqrt(jnp.mean(oh * oh, axis=-1, keepdims=True) + EPS)
            y_ref[:, sl] = (oh * r * gv * gp * _sigmoid(gp)).astype(y_ref.dtype)

    return pl.pallas_call(
        body, name=name, grid=(t // ROWS,),
        in_specs=[_row_spec(w), pl.BlockSpec((ROWS, w), lambda i: (i, goff)), _const_spec((1, GDN_HEAD_DIM))],
        out_specs=_row_spec(w), out_shape=jax.ShapeDtypeStruct((t, w), BF16),
        compiler_params=_params("parallel"),
    )(o, proj, norm_g)


def _gdn_onorm_bwd(o, proj, norm_g, dy, *, name):
    t = o.shape[0]
    w = GDN_KEY_DIM
    goff = 3 * GDN_KEY_DIM // w

    def body(o_ref, gp_ref, g_ref, dy_ref, do_ref, dgp_ref, st_ref):
        @pl.when(pl.program_id(0) == 0)
        def _():
            st_ref[...] = jnp.zeros_like(st_ref)

        gv = g_ref[...]
        acc = jnp.zeros((1, GDN_HEAD_DIM), F32)
        for h in range(GDN_HEADS):
            sl = slice(h * GDN_HEAD_DIM, (h + 1) * GDN_HEAD_DIM)
            oh = o_ref[:, sl]
            gp = gp_ref[:, sl]
            dyv = dy_ref[:, sl].astype(F32)
            r = lax.rsqrt(jnp.mean(oh * oh, axis=-1, keepdims=True) + EPS)
            xh = oh * r
            sg = _sigmoid(gp)
            dn = dyv * gp * sg
            dgp_ref[:, sl] = (dyv * xh * gv * sg * (1.0 + gp * (1.0 - sg))).astype(dgp_ref.dtype)
            acc = acc + jnp.sum(dn * xh, axis=0, keepdims=True)
            dxh = dn * gv
            do_ref[:, sl] = r * (dxh - xh * jnp.mean(dxh * xh, axis=-1, keepdims=True))
        st_ref[0:1, :] += acc

    return pl.pallas_call(
        body, name=name, grid=(t // ROWS,),
        in_specs=[_row_spec(w), pl.BlockSpec((ROWS, w), lambda i: (i, goff)), _const_spec((1, GDN_HEAD_DIM)),
                  _row_spec(w)],
        out_specs=(_row_spec(w), _row_spec(w), _const_spec((8, GDN_HEAD_DIM))),
        out_shape=(jax.ShapeDtypeStruct((t, w), F32), jax.ShapeDtypeStruct((t, w), BF16),
                   jax.ShapeDtypeStruct((8, GDN_HEAD_DIM), F32)),
        compiler_params=_params("arbitrary"),
    )(o, proj, norm_g, dy)


def _mla_prep_fwd(proj, qg, kvg, *, name):
    t = proj.shape[0]
    q1, k1 = MLA_Q_RANK, MLA_Q_RANK + MLA_KV_RANK

    def body(p_ref, qg_ref, kg_ref, cq_ref, ck_ref):
        cq = p_ref[:, 0:q1]
        ck = p_ref[:, q1:k1]
        cq_ref[...] = (cq * lax.rsqrt(jnp.mean(cq * cq, axis=-1, keepdims=True) + EPS) * qg_ref[...]).astype(BF16)
        ck_ref[...] = (ck * lax.rsqrt(jnp.mean(ck * ck, axis=-1, keepdims=True) + EPS) * kg_ref[...]).astype(BF16)

    return pl.pallas_call(
        body, name=name, grid=(t // ROWS,),
        in_specs=[_row_spec(MLA_IN), _const_spec((1, MLA_Q_RANK)), _const_spec((1, MLA_KV_RANK))],
        out_specs=(_row_spec(MLA_Q_RANK), _row_spec(MLA_KV_RANK)),
        out_shape=(jax.ShapeDtypeStruct((t, MLA_Q_RANK), BF16), jax.ShapeDtypeStruct((t, MLA_KV_RANK), BF16)),
        compiler_params=_params("parallel"),
    )(proj, qg, kvg)


def _mla_prep_bwd(proj, qg, kvg, dcq, dck, dkr, *, name):
    t = proj.shape[0]
    q1, k1 = MLA_Q_RANK, MLA_Q_RANK + MLA_KV_RANK

    def body(p_ref, qg_ref, kg_ref, dq_ref, dk_ref, dr_ref, dp_ref, st_ref):
        @pl.when(pl.program_id(0) == 0)
        def _():
            st_ref[...] = jnp.zeros_like(st_ref)

        for lo, hi, g_ref, d_ref in ((0, q1, qg_ref, dq_ref), (q1, k1, kg_ref, dk_ref)):
            xv = p_ref[:, lo:hi]
            dn = d_ref[...]
            r = lax.rsqrt(jnp.mean(xv * xv, axis=-1, keepdims=True) + EPS)
            xh = xv * r
            dxh = dn * g_ref[...]
            dp_ref[:, lo:hi] = (r * (dxh - xh * jnp.mean(dxh * xh, axis=-1, keepdims=True))).astype(dp_ref.dtype)
            st_ref[0:1, lo:hi] += jnp.sum(dn * xh, axis=0, keepdims=True)
        dp_ref[:, k1:MLA_IN] = dr_ref[:, 0:MLA_ROPE].astype(dp_ref.dtype)

    return pl.pallas_call(
        body, name=name, grid=(t // ROWS,),
        in_specs=[_row_spec(MLA_IN), _const_spec((1, MLA_Q_RANK)), _const_spec((1, MLA_KV_RANK)),
                  _row_spec(MLA_Q_RANK), _row_spec(MLA_KV_RANK), _row_spec(LANES)],
        out_specs=(_row_spec(MLA_IN), _const_spec((8, MLA_IN))),
        out_shape=(jax.ShapeDtypeStruct((t, MLA_IN), BF16), jax.ShapeDtypeStruct((8, MLA_IN), F32)),
        compiler_params=_params("arbitrary"),
    )(proj, qg, kvg, dcq, dck, dkr)


def _rope(xr, cos_t, sin_t, *, name):
    t, w = xr.shape
    ns = w // LANES

    def body(x_ref, c_ref, s_ref, o_ref):
        cv, sv = c_ref[...], s_ref[...]
        lane = lax.broadcasted_iota(jnp.int32, (ROWS, LANES), 1)
        first = (lane % MLA_ROPE) < (MLA_ROPE // 2)
        for i in range(ns):
            sl = slice(i * LANES, (i + 1) * LANES)
            xv = x_ref[:, sl]
            sw = jnp.where(first, pltpu.roll(xv, LANES - MLA_ROPE // 2, 1), pltpu.roll(xv, MLA_ROPE // 2, 1))
            o_ref[:, sl] = xv * cv + sw * sv

    return pl.pallas_call(
        body, name=name, grid=(t // ROWS,),
        in_specs=[_row_spec(w), _row_spec(LANES), _row_spec(LANES)], out_specs=_row_spec(w),
        out_shape=jax.ShapeDtypeStruct((t, w), F32), compiler_params=_params("parallel"),
    )(xr, cos_t, sin_t)


def _rope_bwd(dr, cos_t, sin_t, *, name):
    t, w = dr.shape
    ns = w // LANES

    def body(d_ref, c_ref, s_ref, o_ref):
        cv, sv = c_ref[...], s_ref[...]
        lane = lax.broadcasted_iota(jnp.int32, (ROWS, LANES), 1)
        first = (lane % MLA_ROPE) < (MLA_ROPE // 2)
        for i in range(ns):
            sl = slice(i * LANES, (i + 1) * LANES)
            dv = d_ref[:, sl]
            ds = dv * sv
            sw = jnp.where(first, pltpu.roll(ds, LANES - MLA_ROPE // 2, 1), pltpu.roll(ds, MLA_ROPE // 2, 1))
            o_ref[:, sl] = dv * cv + sw

    return pl.pallas_call(
        body, name=name, grid=(t // ROWS,),
        in_specs=[_row_spec(w), _row_spec(LANES), _row_spec(LANES)], out_specs=_row_spec(w),
        out_shape=jax.ShapeDtypeStruct((t, w), F32), compiler_params=_params("parallel"),
    )(dr, cos_t, sin_t)


ATT_BLOCK = 256
ATT_HEAD_BATCH = 4
ATT_HEAD_BATCH_BWD = 2
ATT_SCALE = MLA_QK ** -0.5


def _causal_mask(i, j, blk):
    rows = i * blk + lax.broadcasted_iota(jnp.int32, (blk, blk), 0)
    cols = j * blk + lax.broadcasted_iota(jnp.int32, (blk, blk), 1)
    return cols <= rows


def _attn_fwd(q, k, v, *, name):
    nh, t, dk = q.shape
    dv = v.shape[-1]
    blk = min(ATT_BLOCK, t)

    hb = ATT_HEAD_BATCH
    hs = range(hb)

    def body(q_ref, k_ref, v_ref, o_ref, l_ref):
        i = pl.program_id(1)
        qv = [q_ref[h] for h in hs]

        def step(j, carry):
            m, l, acc = carry[:hb], carry[hb:2 * hb], carry[2 * hb:]
            off = pl.multiple_of(j * blk, blk)
            mask = _causal_mask(i, j, blk)
            s = [_dotb(qv[h], k_ref[h, pl.ds(off, blk), :], NT) for h in hs]
            s = [jnp.where(mask, s[h] * ATT_SCALE, NEG) for h in hs]
            m_new = [jnp.maximum(m[h], jnp.max(s[h], axis=-1, keepdims=True)) for h in hs]
            p = [jnp.exp(s[h] - m_new[h]) for h in hs]
            pv = [_dotb(p[h], v_ref[h, pl.ds(off, blk), :], NN) for h in hs]
            alpha = [jnp.exp(m[h] - m_new[h]) for h in hs]
            l = [alpha[h] * l[h] + jnp.sum(p[h], axis=-1, keepdims=True) for h in hs]
            acc = [alpha[h] * acc[h] + pv[h] for h in hs]
            return tuple(m_new) + tuple(l) + tuple(acc)

        init = ((jnp.full((blk, 1), NEG, F32),) * hb + (jnp.zeros((blk, 1), F32),) * hb
                + (jnp.zeros((blk, dv), F32),) * hb)
        out = lax.fori_loop(0, i + 1, step, init)
        for h in hs:
            m, l, acc = out[h], out[hb + h], out[2 * hb + h]
            o_ref[h] = acc / l
            l_ref[h] = jnp.broadcast_to(m + jnp.log(l), (blk, LANES))

    return pl.pallas_call(
        body, name=name, grid=(nh // hb, t // blk),
        in_specs=[pl.BlockSpec((hb, blk, dk), lambda h, i: (h, i, 0)), pl.BlockSpec((hb, t, dk), lambda h, i: (h, 0, 0)),
                  pl.BlockSpec((hb, t, dv), lambda h, i: (h, 0, 0))],
        out_specs=(pl.BlockSpec((hb, blk, dv), lambda h, i: (h, i, 0)),
                   pl.BlockSpec((hb, blk, LANES), lambda h, i: (h, i, 0))),
        out_shape=(jax.ShapeDtypeStruct((nh, t, dv), F32), jax.ShapeDtypeStruct((nh, t, LANES), F32)),
        compiler_params=_params("parallel", "parallel"),
    )(q, k, v)


def _attn_bwd(q, k, v, o, lse, do, *, name):
    nh, t, dk = q.shape
    dv = v.shape[-1]
    blk = min(ATT_BLOCK, t)
    nb = t // blk

    hb = ATT_HEAD_BATCH_BWD
    hs = range(hb)

    def body(q_ref, k_ref, v_ref, o_ref, l_ref, do_ref, dq_ref, dk_ref, dv_ref):
        j = pl.program_id(1)

        @pl.when(j == 0)
        def _():
            dq_ref[...] = jnp.zeros_like(dq_ref)

        kv = [k_ref[h] for h in hs]
        vv = [v_ref[h] for h in hs]

        def step(i, carry):
            dk_acc, dv_acc = carry[:hb], carry[hb:]
            off = pl.multiple_of(i * blk, blk)
            rows = pl.ds(off, blk)
            mask = _causal_mask(i, j, blk)
            qv = [q_ref[h, rows, :] for h in hs]
            dov = [do_ref[h, rows, :] for h in hs]
            s = [_dotb(qv[h], kv[h], NT) for h in hs]
            dp = [_dotb(dov[h], vv[h], NT) for h in hs]
            p = [jnp.exp(jnp.where(mask, s[h] * ATT_SCALE, NEG) - l_ref[h, rows, :][:, 0:1]) for h in hs]
            delta = [jnp.sum(dov[h] * o_ref[h, rows, :], axis=-1, keepdims=True) for h in hs]
            ds = [p[h] * (dp[h] - delta[h]) * ATT_SCALE for h in hs]
            dvn = [_dotb(p[h], dov[h], TN) for h in hs]
            dkn = [_dotb(ds[h], qv[h], TN) for h in hs]
            dqn = [_dotb(ds[h], kv[h], NN) for h in hs]
            for h in hs:
                dq_ref[h, rows, :] += dqn[h]
            return tuple(dk_acc[h] + dkn[h] for h in hs) + tuple(dv_acc[h] + dvn[h] for h in hs)

        out = lax.fori_loop(j, nb, step, (jnp.zeros((blk, dk), F32),) * hb + (jnp.zeros((blk, dv), F32),) * hb)
        for h in hs:
            dk_ref[h] = out[h]
            dv_ref[h] = out[hb + h]

    full = lambda w: pl.BlockSpec((hb, t, w), lambda h, j: (h, 0, 0))
    part = lambda w: pl.BlockSpec((hb, blk, w), lambda h, j: (h, j, 0))
    return pl.pallas_call(
        body, name=name, grid=(nh // hb, nb),
        in_specs=[full(dk), part(dk), part(dv), full(dv), full(LANES), full(dv)],
        out_specs=(full(dk), part(dk), part(dv)),
        out_shape=(jax.ShapeDtypeStruct((nh, t, dk), F32), jax.ShapeDtypeStruct((nh, t, dk), F32),
                   jax.ShapeDtypeStruct((nh, t, dv), F32)),
        compiler_params=_params("parallel", "arbitrary"),
    )(q, k, v, o, lse, do)


def _swap_halves(xv, first):
    return jnp.where(first, pltpu.roll(xv, LANES - MLA_ROPE // 2, 1), pltpu.roll(xv, MLA_ROPE // 2, 1))


def _rope_qk(qf, proj, cos_t, sin_t, *, name):
    t = qf.shape[0]
    nrope = MLA_HEADS * MLA_ROPE
    q_blk = MLA_HEADS * MLA_NOPE // nrope
    k_blk = (MLA_Q_RANK + MLA_KV_RANK) // LANES

    def body(q_ref, p_ref, c_ref, s_ref, qo_ref, ko_ref):
        cv, sv = c_ref[...], s_ref[...]
        lane = lax.broadcasted_iota(jnp.int32, (ROWS, LANES), 1)
        first = (lane % MLA_ROPE) < (MLA_ROPE // 2)
        for i in range(nrope // LANES):
            sl = slice(i * LANES, (i + 1) * LANES)
            xv = q_ref[:, sl]
            qo_ref[:, sl] = (xv * cv + _swap_halves(xv, first) * sv).astype(qo_ref.dtype)
        kv = jnp.where(lane < MLA_ROPE, p_ref[...], 0.0)
        ko_ref[...] = (kv * cv + _swap_halves(kv, first) * sv).astype(ko_ref.dtype)

    return pl.pallas_call(
        body, name=name, grid=(t // ROWS,),
        in_specs=[pl.BlockSpec((ROWS, nrope), lambda i: (i, q_blk)), pl.BlockSpec((ROWS, LANES), lambda i: (i, k_blk)),
                  _row_spec(LANES), _row_spec(LANES)],
        out_specs=(_row_spec(nrope), _row_spec(LANES)),
        out_shape=(jax.ShapeDtypeStruct((t, nrope), BF16), jax.ShapeDtypeStruct((t, LANES), BF16)),
        compiler_params=_params("parallel"),
    )(qf, proj, cos_t, sin_t)


def _rope_qk_bwd(dqr, dkr_parts, cos_t, sin_t, *, name):
    t, nrope = dqr.shape
    ng = dkr_parts.shape[0]

    def body(d_ref, k_ref, c_ref, s_ref, qo_ref, ko_ref):
        cv, sv = c_ref[...], s_ref[...]
        lane = lax.broadcasted_iota(jnp.int32, (ROWS, LANES), 1)
        first = (lane % MLA_ROPE) < (MLA_ROPE // 2)
        for i in range(nrope // LANES):
            sl = slice(i * LANES, (i + 1) * LANES)
            dv = d_ref[:, sl]
            qo_ref[:, sl] = (dv * cv + _swap_halves(dv * sv, first)).astype(qo_ref.dtype)
        dk = k_ref[0]
        for g in range(1, ng):
            dk = dk + k_ref[g]
        dk = jnp.where(lane < MLA_ROPE, dk, 0.0)
        ko_ref[...] = jnp.where(lane < MLA_ROPE, dk * cv + _swap_halves(dk * sv, first), 0.0)

    return pl.pallas_call(
        body, name=name, grid=(t // ROWS,),
        in_specs=[_row_spec(nrope), pl.BlockSpec((ng, ROWS, LANES), lambda i: (0, i, 0)), _row_spec(LANES),
                  _row_spec(LANES)],
        out_specs=(_row_spec(nrope), _row_spec(LANES)),
        out_shape=(jax.ShapeDtypeStruct((t, nrope), BF16), jax.ShapeDtypeStruct((t, LANES), F32)),
        compiler_params=_params("parallel"),
    )(dqr, dkr_parts, cos_t, sin_t)


def _attn_tm_fwd(qf, qr, kvf, kr, *, name):
    t = qf.shape[0]
    nh, dn, dr, dv = MLA_HEADS, MLA_NOPE, MLA_ROPE, MLA_V
    blk = min(ATT_BLOCK, t)
    hb = ATT_HEAD_BATCH
    hs = range(hb)

    def body(q_ref, qr_ref, kv_ref, kr_ref, o_ref, l_ref):
        i = pl.program_id(1)
        qn = [q_ref[:, h * dn:(h + 1) * dn].astype(MXU_DTYPE) for h in hs]
        qrh = [qr_ref[:, h * dr:(h + 1) * dr] for h in hs]

        def step(j, carry):
            m, l, acc = carry[:hb], carry[hb:2 * hb], carry[2 * hb:]
            rows = pl.ds(pl.multiple_of(j * blk, blk), blk)
            mask = _causal_mask(i, j, blk)
            krj = kr_ref[rows, 0:dr]
            s = [_dotb(qn[h], kv_ref[rows, h * (dn + dv):h * (dn + dv) + dn], NT) for h in hs]
            sr = [_dotb(qrh[h], krj, NT) for h in hs]
            s = [jnp.where(mask, (s[h] + sr[h]) * ATT_SCALE, NEG) for h in hs]
            m_new = [jnp.maximum(m[h], jnp.max(s[h], axis=-1, keepdims=True)) for h in hs]
            p = [jnp.exp(s[h] - m_new[h]) for h in hs]
            pv = [_dotb(p[h], kv_ref[rows, h * (dn + dv) + dn:(h + 1) * (dn + dv)], NN) for h in hs]
            alpha = [jnp.exp(m[h] - m_new[h]) for h in hs]
            l = [alpha[h] * l[h] + jnp.sum(p[h], axis=-1, keepdims=True) for h in hs]
            acc = [alpha[h] * acc[h] + pv[h] for h in hs]
            return tuple(m_new) + tuple(l) + tuple(acc)

        init = ((jnp.full((blk, 1), NEG, F32),) * hb + (jnp.zeros((blk, 1), F32),) * hb
                + (jnp.zeros((blk, dv), F32),) * hb)
        out = lax.fori_loop(0, i + 1, step, init)
        for h in hs:
            m, l, acc = out[h], out[hb + h], out[2 * hb + h]
            o_ref[:, h * dv:(h + 1) * dv] = (acc / l).astype(o_ref.dtype)
            l_ref[h] = jnp.broadcast_to(m + jnp.log(l), (blk, LANES))

    return pl.pallas_call(
        body, name=name, grid=(nh // hb, t // blk),
        in_specs=[pl.BlockSpec((blk, hb * dn), lambda g, i: (i, g)), pl.BlockSpec((blk, hb * dr), lambda g, i: (i, g)),
                  pl.BlockSpec((t, hb * (dn + dv)), lambda g, i: (0, g)), pl.BlockSpec((t, LANES), lambda g, i: (0, 0))],
        out_specs=(pl.BlockSpec((blk, hb * dv), lambda g, i: (i, g)),
                   pl.BlockSpec((hb, blk, LANES), lambda g, i: (g, i, 0))),
        out_shape=(jax.ShapeDtypeStruct((t, nh * dv), BF16), jax.ShapeDtypeStruct((nh, t, LANES), F32)),
        compiler_params=_params("parallel", "parallel"),
    )(qf, qr, kvf, kr)


def _attn_tm_bwd(qf, qr, kvf, kr, o, lse, do, *, name):
    t = qf.shape[0]
    nh, dn, dr, dv = MLA_HEADS, MLA_NOPE, MLA_ROPE, MLA_V
    blk = min(ATT_BLOCK, t)
    nb = t // blk
    hb = ATT_HEAD_BATCH_BWD
    hs = range(hb)
    ng = nh // hb

    def body(q_ref, qr_ref, kv_ref, kr_ref, o_ref, l_ref, do_ref, dqn_ref, dqr_ref, dkv_ref, dkr_ref):
        j = pl.program_id(1)

        @pl.when(j == 0)
        def _():
            dqn_ref[...] = jnp.zeros_like(dqn_ref)
            dqr_ref[...] = jnp.zeros_like(dqr_ref)

        kn = [kv_ref[:, h * (dn + dv):h * (dn + dv) + dn] for h in hs]
        vv = [kv_ref[:, h * (dn + dv) + dn:(h + 1) * (dn + dv)] for h in hs]
        krj = kr_ref[:, 0:dr]

        def step(i, carry):
            dkn_acc, dv_acc, dkr_acc = carry[:hb], carry[hb:2 * hb], carry[2 * hb]
            rows = pl.ds(pl.multiple_of(i * blk, blk), blk)
            mask = _causal_mask(i, j, blk)
            qn = [q_ref[rows, h * dn:(h + 1) * dn].astype(MXU_DTYPE) for h in hs]
            qrh = [qr_ref[rows, h * dr:(h + 1) * dr] for h in hs]
            dov = [do_ref[rows, h * dv:(h + 1) * dv] for h in hs]
            s = [_dotb(qn[h], kn[h], NT) for h in hs]
            sr = [_dotb(qrh[h], krj, NT) for h in hs]
            dp = [_dotb(dov[h], vv[h], NT) for h in hs]
            p = [jnp.exp(jnp.where(mask, (s[h] + sr[h]) * ATT_SCALE, NEG) - l_ref[h, rows, :][:, 0:1]) for h in hs]
            delta = [jnp.sum(dov[h].astype(F32) * o_ref[rows, h * dv:(h + 1) * dv].astype(F32), axis=-1, keepdims=True)
                     for h in hs]
            ds = [p[h] * (dp[h] - delta[h]) * ATT_SCALE for h in hs]
            dvn = [_dotb(p[h], dov[h], TN) for h in hs]
            dknn = [_dotb(ds[h], qn[h], TN) for h in hs]
            dkrn = [_dotb(ds[h], qrh[h], TN) for h in hs]
            dqnn = [_dotb(ds[h], kn[h], NN) for h in hs]
            dqrn = [_dotb(ds[h], krj, NN) for h in hs]
            for h in hs:
                dqn_ref[rows, h * dn:(h + 1) * dn] += dqnn[h]
                dqr_ref[rows, h * dr:(h + 1) * dr] += dqrn[h]
            dkr_new = dkr_acc
            for h in hs:
                dkr_new = dkr_new + dkrn[h]
            return (tuple(dkn_acc[h] + dknn[h] for h in hs) + tuple(dv_acc[h] + dvn[h] for h in hs) + (dkr_new,))

        init = (jnp.zeros((blk, dn), F32),) * hb + (jnp.zeros((blk, dv), F32),) * hb + (jnp.zeros((blk, dr), F32),)
        out = lax.fori_loop(j, nb, step, init)
        for h in hs:
            dkv_ref[:, h * (dn + dv):h * (dn + dv) + dn] = out[h].astype(dkv_ref.dtype)
            dkv_ref[:, h * (dn + dv) + dn:(h + 1) * (dn + dv)] = out[hb + h].astype(dkv_ref.dtype)
        dkr_ref[0, :, 0:dr] = out[2 * hb]
        dkr_ref[0, :, dr:LANES] = jnp.zeros((blk, LANES - dr), F32)

    full = lambda w: pl.BlockSpec((t, w), lambda g, j: (0, g))
    return pl.pallas_call(
        body, name=name, grid=(ng, nb),
        in_specs=[full(hb * dn), full(hb * dr), pl.BlockSpec((blk, hb * (dn + dv)), lambda g, j: (j, g)),
                  pl.BlockSpec((blk, LANES), lambda g, j: (j, 0)), full(hb * dv),
                  pl.BlockSpec((hb, t, LANES), lambda g, j: (g, 0, 0)), full(hb * dv)],
        out_specs=(full(hb * dn), full(hb * dr), pl.BlockSpec((blk, hb * (dn + dv)), lambda g, j: (j, g)),
                   pl.BlockSpec((1, blk, LANES), lambda g, j: (g, j, 0))),
        out_shape=(jax.ShapeDtypeStruct((t, nh * dn), F32), jax.ShapeDtypeStruct((t, nh * dr), F32),
                   jax.ShapeDtypeStruct((t, nh * (dn + dv)), BF16), jax.ShapeDtypeStruct((ng, t, LANES), F32)),
        compiler_params=_params("parallel", "arbitrary"),
    )(qf, qr, kvf, kr, o, lse, do)


def _ada_mod(c_all, ada_w, ada_b_cols, *, name):
    nl, d, wc = ada_w.shape

    def body(c_ref, w_ref, b_ref, o_ref):
        cv = c_ref[...]
        o_ref[0] = _dotb(cv * _sigmoid(cv), w_ref[0], NN) + b_ref[0]

    return pl.pallas_call(
        body, name=name, grid=(nl,),
        in_specs=[_const_spec((N_DEV, d)), pl.BlockSpec((1, d, wc), lambda l: (l, 0, 0)),
                  pl.BlockSpec((1, 1, wc), lambda l: (l, 0, 0))],
        out_specs=pl.BlockSpec((1, N_DEV, wc), lambda l: (l, 0, 0)),
        out_shape=jax.ShapeDtypeStruct((nl, N_DEV, wc), F32), compiler_params=_params("parallel"),
    )(c_all, ada_w, ada_b_cols)


def _adam_math(g, w, m, v):
    m2 = ADAM_B1 * m + (1.0 - ADAM_B1) * g
    v2 = ADAM_B2 * v + (1.0 - ADAM_B2) * (g * g)
    delta = -ADAM_LR * ((m2 / ADAM_BC1) / (jnp.sqrt(v2 / ADAM_BC2) + ADAM_EPS) + ADAM_WD * w)
    return delta, m2, v2


def _ada_grad_adamw(c_all, dmod_cols, w, m, v, *, name):
    nl, d, wc = w.shape
    tr = 256

    def body(c_ref, dm_ref, w_ref, m_ref, v_ref, g_ref, d_ref, m2_ref, v2_ref):
        cv = c_ref[...]
        g = _dotf(cv * _sigmoid(cv), dm_ref[0], TN)
        delta, m2, v2 = _adam_math(g, w_ref[0], m_ref[0], v_ref[0])
        g_ref[0], d_ref[0], m2_ref[0], v2_ref[0] = g, delta, m2, v2

    blk = pl.BlockSpec((1, tr, wc), lambda l, i: (l, i, 0))
    return pl.pallas_call(
        body, name=name, grid=(nl, d // tr),
        in_specs=[pl.BlockSpec((N_DEV, tr), lambda l, i: (0, i)), pl.BlockSpec((1, N_DEV, wc), lambda l, i: (l, 0, 0)),
                  blk, blk, blk],
        out_specs=(blk,) * 4, out_shape=(jax.ShapeDtypeStruct(w.shape, F32),) * 4,
        compiler_params=_params("parallel", "parallel"),
    )(c_all, dmod_cols, w, m, v)


def _adamw(parts, w, m, v, *, name):
    nl, r, c = w.shape
    ns = parts[0].shape[0]
    lanes_padded = -(-c // LANES) * LANES
    row_bytes = 2 * nl * ns * lanes_padded * parts[0].dtype.itemsize
    tr = _pick(r, min(256, max(16, (VMEM_LIMIT // 2) // row_bytes)), 16)
    tc = c
    if tr * row_bytes > VMEM_LIMIT // 2:
        tc = _pick(c, max(LANES, c * (VMEM_LIMIT // 2) // (tr * row_bytes)))

    def body(*refs):
        p_refs = refs[:nl]
        w_ref, m_ref, v_ref, g_ref, d_ref, m2_ref, v2_ref = refs[nl:]
        layer = pl.program_id(0)
        for q in range(nl):
            @pl.when(layer == q)
            def _(q=q):
                g = p_refs[q][0].astype(F32)
                for s in range(1, ns):
                    g = g + p_refs[q][s].astype(F32)
                delta, m2, v2 = _adam_math(g, w_ref[0], m_ref[0], v_ref[0])
                g_ref[0], d_ref[0], m2_ref[0], v2_ref[0] = g, delta, m2, v2

    blk = pl.BlockSpec((1, tr, tc), lambda l, i, j: (l, i, j))
    p_specs = [pl.BlockSpec((ns, tr, tc), lambda l, i, j, q=q: (0, jnp.where(l == q, i, 0), jnp.where(l == q, j, 0)))
               for q in range(nl)]
    return pl.pallas_call(
        body, name=name, grid=(nl, r // tr, c // tc),
        in_specs=p_specs + [blk, blk, blk],
        out_specs=(blk,) * 4, out_shape=(jax.ShapeDtypeStruct(w.shape, F32),) * 4,
        compiler_params=_params("arbitrary", "arbitrary", "arbitrary"),
    )(*parts, w, m, v)


def _sum_parts(parts, *, name):
    ns, r, c = parts.shape

    def body(p_ref, o_ref):
        acc = p_ref[0]
        for s in range(1, ns):
            acc = acc + p_ref[s]
        o_ref[...] = acc

    return pl.pallas_call(
        body, name=name, out_shape=jax.ShapeDtypeStruct((r, c), F32),
        in_specs=[pl.BlockSpec(memory_space=pltpu.VMEM)], out_specs=pl.BlockSpec(memory_space=pltpu.VMEM),
    )(parts)


def _pack(arrs):
    flat = jnp.concatenate([a.reshape(-1).astype(F32) for a in arrs])
    pad = (-flat.shape[0]) % (8 * LANES)
    return jnp.pad(flat, (0, pad)).reshape(-1, LANES)


def _unpack(packed, shapes, lead=()):
    flat = packed.reshape(lead + (-1,))
    out, off = [], 0
    for s in shapes:
        n = math.prod(s)
        out.append(flat[..., off:off + n].reshape(lead + tuple(s)))
        off += n
    return out


def _gather_cols(g):
    _, nl, r, cs = g.shape
    return jnp.transpose(g, (1, 2, 0, 3)).reshape(nl, r, N_DEV * cs)


def _gather_rows(g):
    _, nl, rs, c = g.shape
    return jnp.transpose(g, (1, 0, 2, 3)).reshape(nl, N_DEV * rs, c)


def _scatter_cols(full):
    nl, r, c = full.shape
    return jnp.transpose(full.reshape(nl, r, N_DEV, c // N_DEV), (2, 0, 1, 3))


def _scatter_rows(full):
    nl, r, c = full.shape
    return jnp.transpose(full.reshape(nl, N_DEV, r // N_DEV, c), (1, 0, 2, 3))


def _row(v):
    return v.reshape(1, -1)


def _local_step(x, target, mod, cos_t, sin_t, rep, get_weights, put_grads):
    t = x.shape[0]
    saved = []
    for layer in range(DEPTH):
        j = layer // 2
        tag = f"l{layer}"
        shift_m, scale_m, gate_m, shift_f, scale_f, gate_f = [_row(mod[layer, i]) for i in range(N_MOD)]
        lw = dict(get_weights(layer, "mix", x))
        rec = {"x0": x, "lw": lw}
        h = _adaln_fwd(x, _row(rep["norm_mix_g"][layer]), scale_m, shift_m, name=f"adaln_mix_{tag}")
        rec["h"] = h
        if layer % 2 == 0:
            proj = _mm(h, lw["wt_in"], mode="nt", out_dtype=F32, tm=256, tn=GDN_MAIN, b_rows=GDN_MAIN,
                       dep=lw["dep_mix"], name=f"gdn_in_{tag}")
            ab = _mm(h, lw["wt_ab"], mode="nt", out_dtype=F32, name=f"gdn_in_ab_{tag}")
            qkv = _gdn_prep_fwd(proj, rep["gdn_conv_wt"][j], name=f"gdn_prep_{tag}")
            gbeta = _gdn_gate_fwd(ab, rep["gdn_gate_prm"][j], name=f"gdn_gate_{tag}")
            gbc = jnp.broadcast_to(jnp.transpose(gbeta[:, 0:GDN_HEADS])[:, :, None], (GDN_HEADS, t, LANES))
            bbc = jnp.broadcast_to(jnp.transpose(gbeta[:, GDN_HEADS:2 * GDN_HEADS])[:, :, None],
                                   (GDN_HEADS, t, LANES))
            o, states = _gdn_chunk_fwd(qkv, gbc, bbc, name=f"gdn_chunk_{tag}")
            og = _gdn_onorm_fwd(o, proj, _row(rep["gdn_norm_g"][j]), name=f"gdn_onorm_{tag}")
            x, y = _mm_resid(og, lw["w_out"], x, gate_m, name=f"gdn_out_{tag}")
            rec.update(proj=proj, ab=ab, qkv=qkv, gbc=gbc, bbc=bbc, states=states, o=o, og=og, y=y)
        else:
            proj = _mm(h, lw["w_in"], mode="nn", out_dtype=F32, dep=lw["dep_mix"], name=f"mla_in_{tag}")
            cq, ck = _mla_prep_fwd(proj, _row(rep["mla_q_norm_g"][j]), _row(rep["mla_kv_norm_g"][j]),
                                   name=f"mla_prep_{tag}")
            qf = _mm(cq, lw["wt_uq"], mode="nt", out_dtype=F32, name=f"mla_uq_{tag}")
            kvf = _mm(ck, lw["w_ukv"], mode="nn", out_dtype=BF16, name=f"mla_ukv_{tag}")
            qr, kr = _rope_qk(qf, proj, cos_t, sin_t, name=f"rope_{tag}")
            oc, lse = _attn_tm_fwd(qf, qr, kvf, kr, name=f"attn_{tag}")
            x, y = _mm_resid(oc, lw["w_out"], x, gate_m, name=f"mla_out_{tag}")
            rec.update(proj=proj, cq=cq, ck=ck, qf=qf, qr=qr, kvf=kvf, kr=kr, lse=lse, oc=oc, y=y)
        rec["x1"] = x
        lw.update(get_weights(layer, "ffn", x))
        h2 = _adaln_fwd(x, _row(rep["norm_ffn_g"][layer]), scale_f, shift_f, name=f"adaln_ffn_{tag}")
        s, a2, b2 = _ffn_gu_fwd(h2, lw["wt_g"], lw["wt_u"], lw["dep_ffn"], name=f"ffn_gu_{tag}")
        x, y2 = _mm_resid(s, lw["w_down"], x, gate_f, name=f"ffn_down_{tag}")
        rec.update(h2=h2, a2=a2, b2=b2, s=s, y2=y2)
        saved.append(rec)

    dx, st, ls = _loss_head(x, _row(rep["final_norm_g"]), target, name="loss_head")
    loss = ls[0, 0]
    grads = {"final_norm_g": st[0]}
    per_layer = {k: [None] * DEPTH for k in ("norm_mix_g", "norm_ffn_g")}
    per_gdn = {k: [None] * 2 for k in ("gdn_conv_wt", "gdn_a_log", "gdn_dt_bias", "gdn_norm_g")}
    per_mla = {k: [None] * 2 for k in ("mla_q_norm_g", "mla_kv_norm_g")}
    dmod = [None] * DEPTH
    dep = jnp.zeros((8, LANES), F32)

    for layer in reversed(range(DEPTH)):
        j = layer // 2
        tag = f"l{layer}"
        rec = saved[layer]
        lw = rec["lw"]
        shift_m, scale_m, gate_m, shift_f, scale_f, gate_f = [_row(mod[layer, i]) for i in range(N_MOD)]
        dy2, st_g = _gate_bwd(dx, rec["y2"], gate_f, dep, name=f"gate_bwd_ffn_{tag}")
        dgate_f = st_g[0]
        dw_down = _mm(rec["s"], dy2, mode="tn", out_dtype=BF16, tm=256, tn=1024, name=f"ffn_down_dw_{tag}")
        da2, db2 = _ffn_down_dx(dy2, lw["w_down"], rec["a2"], rec["b2"], name=f"ffn_down_dx_{tag}")
        dwt_g = _mm(da2, rec["h2"], mode="tn", out_dtype=BF16, tm=256, tn=1024, name=f"ffn_g_dw_{tag}")
        dwt_u = _mm(db2, rec["h2"], mode="tn", out_dtype=BF16, tm=256, tn=1024, name=f"ffn_u_dw_{tag}")
        dep = put_grads(layer, "ffn", {"wt_g": dwt_g, "wt_u": dwt_u, "w_down": dw_down})
        dh2 = _mm(da2, lw["wt_g"], mode="nn", out_dtype=F32, tm=256, tn=1024, name=f"ffn_g_dx_{tag}")
        dh2 = _mm(db2, lw["wt_u"], mode="nn", out_dtype=BF16, add=dh2, tm=256, tn=1024, name=f"ffn_u_dx_{tag}")
        dx, st_n = _adaln_bwd(rec["x1"], _row(rep["norm_ffn_g"][layer]), scale_f, shift_f, dh2, dx, dep,
                              name=f"adaln_ffn_bwd_{tag}")
        per_layer["norm_ffn_g"][layer] = st_n[0]
        dscale_f, dshift_f = st_n[1], st_n[2]
        dy, st_g = _gate_bwd(dx, rec["y"], gate_m, dep, name=f"gate_bwd_mix_{tag}")
        dgate_m = st_g[0]
        big = {}
        if layer % 2 == 0:
            big["w_out"] = _mm(rec["og"], dy, mode="tn", out_dtype=BF16, name=f"gdn_out_dw_{tag}")
            dog = _mm(dy, lw["w_out"], mode="nt", out_dtype=BF16, name=f"gdn_out_dx_{tag}")
            do, dgp, st_o = _gdn_onorm_bwd(rec["o"], rec["proj"], _row(rep["gdn_norm_g"][j]), dog,
                                           name=f"gdn_onorm_bwd_{tag}")
            per_gdn["gdn_norm_g"][j] = st_o[0]
            dqkv, dgc_, dbc_ = _gdn_chunk_bwd(rec["qkv"], rec["gbc"], rec["bbc"], rec["states"], do,
                                               name=f"gdn_chunk_bwd_{tag}")
            dgb = jnp.concatenate([jnp.transpose(dgc_[:, :, 0]), jnp.transpose(dbc_[:, :, 0])], axis=1)
            dgb = jnp.pad(dgb, ((0, 0), (0, LANES - 2 * GDN_HEADS)))
            dab, st_a = _gdn_gate_bwd(rec["ab"], rep["gdn_gate_prm"][j], dgb, name=f"gdn_gate_bwd_{tag}")
            per_gdn["gdn_a_log"][j] = st_a[0, :GDN_HEADS]
            per_gdn["gdn_dt_bias"][j] = st_a[1, :GDN_HEADS]
            dpre, dcw = _gdn_prep_bwd(rec["proj"], rep["gdn_conv_wt"][j], dqkv, name=f"gdn_prep_bwd_{tag}")
            per_gdn["gdn_conv_wt"][j] = dcw
            dproj = jnp.concatenate([dpre, dgp], axis=1)
            dw_main = _mm(dproj, rec["h"], mode="tn", out_dtype=BF16, tm=512, tn=1024, name=f"gdn_in_dw_{tag}")
            dw_ab = _mm(dab, rec["h"], mode="tn", out_dtype=BF16, tn=1024, name=f"gdn_in_ab_dw_{tag}")
            big["wt_in"] = jnp.concatenate([dw_main, dw_ab[:2 * GDN_HEADS]], axis=0)
            dep = put_grads(layer, "gdn", big)
            dh_ab = _mm(dab, lw["wt_ab"], mode="nn", out_dtype=F32, tn=1024, name=f"gdn_in_ab_dx_{tag}")
            dh = _mm(dproj, lw["wt_in"], mode="nn", out_dtype=BF16, add=dh_ab, tm=256, tn=1024, b_rows=GDN_MAIN,
                     name=f"gdn_in_dx_{tag}")
        else:
            big["w_out"] = _mm(rec["oc"], dy, mode="tn", out_dtype=BF16, name=f"mla_out_dw_{tag}")
            doc = _mm(dy, lw["w_out"], mode="nt", out_dtype=BF16, name=f"mla_out_dx_{tag}")
            dqn, dqr, dkvf, dkr_parts = _attn_tm_bwd(rec["qf"], rec["qr"], rec["kvf"], rec["kr"], rec["oc"],
                                                     rec["lse"], doc, name=f"attn_bwd_{tag}")
            dqr_un, dkr_un = _rope_qk_bwd(dqr, dkr_parts, cos_t, sin_t, name=f"rope_bwd_{tag}")
            n_nope = MLA_HEADS * MLA_NOPE
            big["wt_uq"] = jnp.concatenate(
                [_mm(dqn, rec["cq"], mode="tn", out_dtype=BF16, name=f"mla_uq_dw_nope_{tag}"),
                 _mm(dqr_un, rec["cq"], mode="tn", out_dtype=BF16, name=f"mla_uq_dw_rope_{tag}")], axis=0)
            big["w_ukv"] = _mm(rec["ck"], dkvf, mode="tn", out_dtype=BF16, name=f"mla_ukv_dw_{tag}")
            dcq = _mm(dqr_un, lw["wt_uq"][n_nope:], mode="nn", out_dtype=F32, name=f"mla_uq_dx_rope_{tag}")
            dcq = _mm(dqn, lw["wt_uq"], mode="nn", out_dtype=F32, add=dcq, b_rows=n_nope,
                      name=f"mla_uq_dx_nope_{tag}")
            dck = _mm(dkvf, lw["w_ukv"], mode="nt", out_dtype=F32, name=f"mla_ukv_dx_{tag}")
            dproj, st_p = _mla_prep_bwd(rec["proj"], _row(rep["mla_q_norm_g"][j]), _row(rep["mla_kv_norm_g"][j]),
                                        dcq, dck, dkr_un, name=f"mla_prep_bwd_{tag}")
            per_mla["mla_q_norm_g"][j] = st_p[0, :MLA_Q_RANK]
            per_mla["mla_kv_norm_g"][j] = st_p[0, MLA_Q_RANK:MLA_Q_RANK + MLA_KV_RANK]
            big["w_in"] = _mm(rec["h"], dproj, mode="tn", out_dtype=BF16, name=f"mla_in_dw_{tag}")
            dep = put_grads(layer, "mla", big)
            dh = _mm(dproj, lw["w_in"], mode="nt", out_dtype=BF16, name=f"mla_in_dx_{tag}")
        dx, st_n = _adaln_bwd(rec["x0"], _row(rep["norm_mix_g"][layer]), scale_m, shift_m, dh, dx, dep,
                              name=f"adaln_mix_bwd_{tag}")
        per_layer["norm_mix_g"][layer] = st_n[0]
        dmod[layer] = jnp.stack([st_n[2], st_n[1], dgate_m, dshift_f, dscale_f, dgate_f])

    for d in (per_layer, per_gdn, per_mla):
        for k, v in d.items():
            grads[k] = jnp.stack(v)
    return loss, dx, jnp.stack(dmod), grads


BIG = ("gdn_w_in", "gdn_w_out", "mla_w_in", "mla_w_uq", "mla_w_ukv", "mla_w_out", "ffn_w_gate", "ffn_w_up",
       "ffn_w_down")
TRANSPOSED = ("gdn_w_in", "mla_w_uq", "ffn_w_gate", "ffn_w_up")


def _view(k, a):
    return jnp.transpose(a, (0, 2, 1)) if k in TRANSPOSED else a
SMALL = ("ada_b", "norm_mix_g", "norm_ffn_g", "gdn_conv_w", "gdn_a_log", "gdn_dt_bias", "gdn_norm_g",
         "mla_q_norm_g", "mla_kv_norm_g", "final_norm_g")
WEIGHTS = ("ada_w", "ada_b", "norm_mix_g", "norm_ffn_g", "gdn_w_in", "gdn_conv_w", "gdn_a_log", "gdn_dt_bias",
           "gdn_norm_g", "gdn_w_out", "mla_w_in", "mla_q_norm_g", "mla_kv_norm_g", "mla_w_uq", "mla_w_ukv",
           "mla_w_out", "ffn_w_gate", "ffn_w_up", "ffn_w_down", "final_norm_g")


def _uq_to_kernel_layout(w, axis=-1):
    axis = axis % w.ndim
    lead, tail = w.shape[:axis], w.shape[axis + 1:]
    w4 = w.reshape(lead + (MLA_HEADS, MLA_QK) + tail)
    nope = lax.slice_in_dim(w4, 0, MLA_NOPE, axis=axis + 1).reshape(lead + (-1,) + tail)
    rope = lax.slice_in_dim(w4, MLA_NOPE, MLA_QK, axis=axis + 1).reshape(lead + (-1,) + tail)
    return jnp.concatenate([nope, rope], axis=axis)


def _uq_from_kernel_layout(w, axis=-1):
    axis = axis % w.ndim
    lead, tail = w.shape[:axis], w.shape[axis + 1:]
    nope = lax.slice_in_dim(w, 0, MLA_HEADS * MLA_NOPE, axis=axis).reshape(lead + (MLA_HEADS, MLA_NOPE) + tail)
    rope = lax.slice_in_dim(w, MLA_HEADS * MLA_NOPE, MLA_HEADS * MLA_QK, axis=axis).reshape(
        lead + (MLA_HEADS, MLA_ROPE) + tail)
    return jnp.concatenate([nope, rope], axis=axis + 1).reshape(lead + (-1,) + tail)


def _group_names(layer, kind):
    if kind == "ffn":
        return ("ffn_w_gate", "ffn_w_up", "ffn_w_down")
    return ("gdn_w_in", "gdn_w_out") if layer % 2 == 0 else ("mla_w_in", "mla_w_uq", "mla_w_ukv", "mla_w_out")


def _layer_index(name, layer):
    return layer if name.startswith("ffn") else layer // 2


def _cols(g):
    return jnp.transpose(g, (1, 0, 2)).reshape(g.shape[1], N_DEV * g.shape[2])


def _rows(g):
    return g.reshape(N_DEV * g.shape[1], g.shape[2])


def _uncols(full):
    r, c = full.shape
    return jnp.transpose(full.reshape(r, N_DEV, c // N_DEV), (1, 0, 2))


def _unrows(full):
    r, c = full.shape
    return full.reshape(N_DEV, r // N_DEV, c)


def _group_weights(layer, kind, got, token):
    if kind == "ffn":
        return {"wt_g": _rows(got["ffn_w_gate"]), "wt_u": _rows(got["ffn_w_up"]), "w_down": _rows(got["ffn_w_down"]),
                "dep_ffn": token}
    if layer % 2 == 0:
        wt_in = _rows(got["gdn_w_in"])
        return dict(wt_in=wt_in, wt_ab=jnp.pad(wt_in[GDN_MAIN:], ((0, LANES - 2 * GDN_HEADS), (0, 0))),
                    w_out=_rows(got["gdn_w_out"]), dep_mix=token)
    return dict(w_in=_rows(got["mla_w_in"]), wt_uq=_uq_to_kernel_layout(_rows(got["mla_w_uq"]), axis=0),
                w_ukv=_cols(got["mla_w_ukv"]), w_out=_rows(got["mla_w_out"]), dep_mix=token)


def _layer_grad_slots(kind, big):
    if kind == "ffn":
        return {"ffn_w_gate": _unrows(big["wt_g"]), "ffn_w_up": _unrows(big["wt_u"]),
                "ffn_w_down": _unrows(big["w_down"])}
    if kind == "gdn":
        return {"gdn_w_in": _unrows(big["wt_in"]), "gdn_w_out": _unrows(big["w_out"])}
    return {"mla_w_in": _unrows(big["w_in"]), "mla_w_uq": _unrows(_uq_from_kernel_layout(big["wt_uq"], axis=0)),
            "mla_w_ukv": _uncols(big["w_ukv"]), "mla_w_out": _unrows(big["w_out"])}


def _small_weights(tiny, rep):
    prm = jnp.zeros((2, 8, LANES), F32)
    prm = prm.at[:, 0, :GDN_HEADS].set(rep["gdn_a_log"]).at[:, 1, :GDN_HEADS].set(rep["gdn_dt_bias"])
    out = {
        "gdn_conv_wt": jnp.transpose(_gather_rows(tiny["gdn_conv_w"]), (0, 2, 1)),
        "mla_q_norm_g": jnp.transpose(tiny["mla_q_norm_g"], (1, 0, 2)).reshape(2, MLA_Q_RANK),
        "mla_kv_norm_g": jnp.transpose(tiny["mla_kv_norm_g"], (1, 0, 2)).reshape(2, MLA_KV_RANK),
        "gdn_gate_prm": prm,
    }
    for k in ("norm_mix_g", "norm_ffn_g", "gdn_norm_g", "final_norm_g"):
        out[k] = rep[k]
    return out


def _rope_tables(positions):
    inv_freq = ROPE_THETA ** (-jnp.arange(0, MLA_ROPE, 2, dtype=F32) / MLA_ROPE)
    ang = positions.astype(F32)[:, None] * inv_freq
    cos, sin = jnp.cos(ang), jnp.sin(ang)
    reps = LANES // MLA_ROPE
    return jnp.tile(jnp.concatenate([cos, cos], axis=1), (1, reps)), jnp.tile(
        jnp.concatenate([-sin, sin], axis=1), (1, reps))


def kernel(x, c, positions, ada_w, ada_b, norm_mix_g, norm_ffn_g, gdn_w_in, gdn_conv_w, gdn_a_log, gdn_dt_bias, gdn_norm_g, gdn_w_out, mla_w_in, mla_q_norm_g, mla_kv_norm_g, mla_w_uq, mla_w_ukv, mla_w_out, ffn_w_gate, ffn_w_up, ffn_w_down, final_norm_g, loss_target, m_ada_w, m_ada_b, m_norm_mix_g, m_norm_ffn_g, m_gdn_w_in, m_gdn_conv_w, m_gdn_a_log, m_gdn_dt_bias, m_gdn_norm_g, m_gdn_w_out, m_mla_w_in, m_mla_q_norm_g, m_mla_kv_norm_g, m_mla_w_uq, m_mla_w_ukv, m_mla_w_out, m_ffn_w_gate, m_ffn_w_up, m_ffn_w_down, m_final_norm_g, v_ada_w, v_ada_b, v_norm_mix_g, v_norm_ffn_g, v_gdn_w_in, v_gdn_conv_w, v_gdn_a_log, v_gdn_dt_bias, v_gdn_norm_g, v_gdn_w_out, v_mla_w_in, v_mla_q_norm_g, v_mla_kv_norm_g, v_mla_w_uq, v_mla_w_ukv, v_mla_w_out, v_ffn_w_gate, v_ffn_w_up, v_ffn_w_down, v_final_norm_g):
    W = dict(ada_w=ada_w, ada_b=ada_b, norm_mix_g=norm_mix_g, norm_ffn_g=norm_ffn_g, gdn_w_in=gdn_w_in,
             gdn_conv_w=gdn_conv_w, gdn_a_log=gdn_a_log, gdn_dt_bias=gdn_dt_bias, gdn_norm_g=gdn_norm_g,
             gdn_w_out=gdn_w_out, mla_w_in=mla_w_in, mla_q_norm_g=mla_q_norm_g, mla_kv_norm_g=mla_kv_norm_g,
             mla_w_uq=mla_w_uq, mla_w_ukv=mla_w_ukv, mla_w_out=mla_w_out, ffn_w_gate=ffn_w_gate,
             ffn_w_up=ffn_w_up, ffn_w_down=ffn_w_down, final_norm_g=final_norm_g)
    M = dict(ada_w=m_ada_w, ada_b=m_ada_b, norm_mix_g=m_norm_mix_g, norm_ffn_g=m_norm_ffn_g, gdn_w_in=m_gdn_w_in,
             gdn_conv_w=m_gdn_conv_w, gdn_a_log=m_gdn_a_log, gdn_dt_bias=m_gdn_dt_bias, gdn_norm_g=m_gdn_norm_g,
             gdn_w_out=m_gdn_w_out, mla_w_in=m_mla_w_in, mla_q_norm_g=m_mla_q_norm_g,
             mla_kv_norm_g=m_mla_kv_norm_g, mla_w_uq=m_mla_w_uq, mla_w_ukv=m_mla_w_ukv, mla_w_out=m_mla_w_out,
             ffn_w_gate=m_ffn_w_gate, ffn_w_up=m_ffn_w_up, ffn_w_down=m_ffn_w_down, final_norm_g=m_final_norm_g)
    V = dict(ada_w=v_ada_w, ada_b=v_ada_b, norm_mix_g=v_norm_mix_g, norm_ffn_g=v_norm_ffn_g, gdn_w_in=v_gdn_w_in,
             gdn_conv_w=v_gdn_conv_w, gdn_a_log=v_gdn_a_log, gdn_dt_bias=v_gdn_dt_bias, gdn_norm_g=v_gdn_norm_g,
             gdn_w_out=v_gdn_w_out, mla_w_in=v_mla_w_in, mla_q_norm_g=v_mla_q_norm_g,
             mla_kv_norm_g=v_mla_kv_norm_g, mla_w_uq=v_mla_w_uq, mla_w_ukv=v_mla_w_ukv, mla_w_out=v_mla_w_out,
             ffn_w_gate=v_ffn_w_gate, ffn_w_up=v_ffn_w_up, ffn_w_down=v_ffn_w_down, final_norm_g=v_final_norm_g)
    me = 4 * lax.axis_index("x") + 2 * lax.axis_index("y") + lax.axis_index("c")
    t = x.shape[1]
    wc = ada_w.shape[-1]

    groups = [(layer, kind) for layer in range(DEPTH) for kind in ("mix", "ffn")]

    def group_srcs(i):
        layer, kind = groups[i]
        return [_view(k, W[k])[_layer_index(k, layer)].astype(BF16) for k in _group_names(layer, kind)]

    tiny_shapes = [c.shape, gdn_conv_w.shape, mla_q_norm_g.shape, mla_kv_norm_g.shape]
    first = _gather_two_level([_pack([c, gdn_conv_w, mla_q_norm_g, mla_kv_norm_g])] + group_srcs(0),
                              name="gather_first")
    tiny_g = first[0]
    c_g, conv_g, qn_g, kvn_g = _unpack(tiny_g, tiny_shapes, lead=(N_DEV,))
    c_all = c_g.reshape(N_DEV, D_MODEL)
    rep = _small_weights({"gdn_conv_w": conv_g, "mla_q_norm_g": qn_g, "mla_kv_norm_g": kvn_g}, W)

    def start_group(i, dep):
        layer, kind = groups[i]
        return _exchange_start(group_srcs(i), scatter=False, name=f"gather_start_{kind}_l{layer}", dep=dep)


    b_cols = lax.dynamic_slice_in_dim(ada_b, me * wc, wc, axis=1).reshape(DEPTH, 1, wc)
    mod_part = _ada_mod(c_all, ada_w, b_cols, name="ada_mod")
    (mod_g,) = _exchange([mod_part], scatter=False, name="gather_mod")
    mod_mine = lax.dynamic_index_in_dim(mod_g, me, axis=2, keepdims=False)
    mod = jnp.transpose(mod_mine, (1, 0, 2)).reshape(DEPTH, N_MOD, D_MODEL)
    gather = {1: start_group(1, mod_g)}

    def get_weights(layer, kind, after):
        i = groups.index((layer, kind))
        names = _group_names(layer, kind)
        if i == 0:
            return _group_weights(layer, kind, dict(zip(names, first[1:])), gather[1][4])
        srcs, lands = _exchange_wait(gather[i], after, scatter=False, name=f"gather_wait_{kind}_l{layer}")
        token = jnp.zeros((8, LANES), F32)
        if i + 1 < len(groups):
            gather[i + 1] = start_group(i + 1, lands[0])
            token = gather[i + 1][4]
        got = {k: lax.dynamic_update_index_in_dim(z, s, me, 0) for k, s, z in zip(names, srcs, lands)}
        return _group_weights(layer, kind, got, token)

    scatter = []

    def put_grads(layer, kind, big):
        slots = _layer_grad_slots(kind, big)
        started = _exchange_start(list(slots.values()), scatter=True, name=f"scatter_start_{kind}_l{layer}")
        scatter.append((layer, kind, list(slots.keys()), started))
        return started[4]

    cos_t, sin_t = _rope_tables(positions[0])
    loss, dx, dmod, g = _local_step(x[0], loss_target[0], mod, cos_t, sin_t, rep, get_weights, put_grads)

    parts = {k: [None] * W[k].shape[0] for k in BIG}
    res = {}

    def wait_group(entry, after):
        layer, kind, names, started = entry
        srcs, lands = _exchange_wait(started, after, scatter=True, name=f"scatter_wait_{kind}_l{layer}")
        for k, s, z in zip(names, srcs, lands):
            own = lax.dynamic_index_in_dim(s, me, 0, keepdims=False)
            parts[k][_layer_index(k, layer)] = lax.dynamic_update_index_in_dim(z, own, me, 0)

    for entry in scatter[:-1]:
        wait_group(entry, dx)
    early = [k for k in BIG if k not in scatter[-1][2]]
    def update(k):
        outs = _adamw(parts[k], _view(k, W[k]), _view(k, M[k]), _view(k, V[k]), name=f"adamw_{k}")
        return tuple(_view(k, o) for o in outs)

    for k in early:
        res[k] = update(k)
    loss, dmod, done = lax.optimization_barrier((loss, dmod, [res[k] for k in early]))
    for k, r in zip(early, done):
        res[k] = r

    small_local = [dmod.reshape(DEPTH, N_MOD * D_MODEL), g["norm_mix_g"], g["norm_ffn_g"],
                   jnp.transpose(g["gdn_conv_wt"], (0, 2, 1)), g["gdn_a_log"], g["gdn_dt_bias"], g["gdn_norm_g"],
                   g["mla_q_norm_g"], g["mla_kv_norm_g"], g["final_norm_g"], loss.reshape(1)]
    small_shapes = [a.shape for a in small_local]
    (small_g,) = _exchange([_pack(small_local)], scatter=False, name="gather_small_grads")
    small_sum = _unpack(_sum_parts(small_g, name="sum_small_grads"), small_shapes)
    loss = small_sum[-1][0]
    dmod_all = _unpack(small_g, small_shapes[:1], lead=(N_DEV,))[0]
    sg = dict(zip(SMALL, small_sum))
    wait_group(scatter[-1], small_g)
    sg["gdn_conv_w"] = lax.dynamic_slice_in_dim(sg["gdn_conv_w"], me * gdn_conv_w.shape[1], gdn_conv_w.shape[1], 1)
    sg["mla_q_norm_g"] = lax.dynamic_slice_in_dim(sg["mla_q_norm_g"], me * mla_q_norm_g.shape[1],
                                                  mla_q_norm_g.shape[1], 1)
    sg["mla_kv_norm_g"] = lax.dynamic_slice_in_dim(sg["mla_kv_norm_g"], me * mla_kv_norm_g.shape[1],
                                                   mla_kv_norm_g.shape[1], 1)

    dmod_cols = jnp.transpose(lax.dynamic_slice_in_dim(dmod_all, me * wc, wc, axis=2), (1, 0, 2))
    res["ada_w"] = _ada_grad_adamw(c_all, dmod_cols, ada_w, m_ada_w, v_ada_w, name="ada_w_grad_adamw")
    for k in BIG:
        if k not in early:
            res[k] = update(k)
    shapes = [W[k].shape for k in SMALL]
    packed = [_pack([d[k] for k in SMALL]) for d in (sg, W, M, V)]
    outs = _adamw([packed[0][None]], packed[1][None], packed[2][None], packed[3][None], name="adamw_small")
    unpacked = [_unpack(o[0], shapes) for o in outs]
    for i, k in enumerate(SMALL):
        res[k] = tuple(u[i] for u in unpacked)

    return (loss, dx[None], *[res[k][0] for k in WEIGHTS], *[res[k][1] for k in WEIGHTS],
            *[res[k][2] for k in WEIGHTS], *[res[k][3] for k in WEIGHTS])
```

```python
import functools
import math

import jax
import jax.numpy as jnp
from jax import lax
from jax.experimental import pallas as pl
from jax.experimental.pallas import tpu as pltpu

F32 = jnp.float32
BF16 = jnp.bfloat16
MXU_DTYPE = jnp.bfloat16

N_DEV = 8
D_MODEL = 1024
DEPTH = 4
GDN_HEADS = 8
GDN_HEAD_DIM = 128
GDN_KEY_DIM = GDN_HEADS * GDN_HEAD_DIM
GDN_CHUNK = 64
GDN_HEAD_BATCH = 8
GDN_CONV = 4
GDN_PREP_HEADS = 2
GDN_MAIN = 4 * GDN_KEY_DIM
MLA_HEADS = 8
MLA_NOPE = 128
MLA_ROPE = 64
MLA_V = 128
MLA_Q_RANK = 384
MLA_KV_RANK = 256
MLA_IN = MLA_Q_RANK + MLA_KV_RANK + MLA_ROPE
MLA_QK = MLA_NOPE + MLA_ROPE
ROPE_THETA = 10000.0
D_FF = 2816
N_MOD = 6
EPS = 1e-6
LANES = 128
VMEM_LIMIT = 48 * 1024 * 1024

ADAM_LR = 0.001
ADAM_B1 = 0.9
ADAM_B2 = 0.999
ADAM_EPS = 1e-08
ADAM_WD = 0.01
ADAM_STEP = 10
ADAM_BC1 = 1.0 - ADAM_B1 ** ADAM_STEP
ADAM_BC2 = 1.0 - ADAM_B2 ** ADAM_STEP

NN = (((1,), (0,)), ((), ()))
NT = (((1,), (1,)), ((), ()))
TN = (((0,), (0,)), ((), ()))
NEG = -1e30


def _dotb(a, b, dims):
    return lax.dot_general(a.astype(MXU_DTYPE), b.astype(MXU_DTYPE), dims, preferred_element_type=F32)


def _split(a):
    hi = a.astype(BF16)
    return hi, (a - hi.astype(F32)).astype(BF16)


def _dotf(a, b, dims):
    ah, al = _split(a)
    bh, bl = _split(b)
    dot = lambda u, v: lax.dot_general(u, v, dims, preferred_element_type=F32)
    return dot(ah, bh) + (dot(ah, bl) + dot(al, bh))


def _params(*sem):
    return pltpu.CompilerParams(dimension_semantics=sem, vmem_limit_bytes=VMEM_LIMIT)


def _pick(n, pref, mult=LANES):
    best = None
    t = mult
    while t <= min(n, pref):
        if n % t == 0:
            best = t
        t += mult
    return best if best is not None else n


def _sigmoid(z):
    return 1.0 / (1.0 + jnp.exp(-z))


def _exchange(arrays, *, scatter, name):
    n = len(arrays)
    out_shape = tuple(
        jax.ShapeDtypeStruct(a.shape if scatter else (N_DEV,) + a.shape, a.dtype) for a in arrays)

    def body(*refs):
        ins, outs = refs[:n], refs[n:2 * n]
        send_sems, recv_sems, local_sems = refs[2 * n:]
        x, y, c = lax.axis_index("x"), lax.axis_index("y"), lax.axis_index("c")
        me = 4 * x + 2 * y + c
        copies = []
        for k in range(n):
            src_own = ins[k].at[me] if scatter else ins[k]
            own = pltpu.make_async_copy(src_own, outs[k].at[me], local_sems.at[k])
            own.start()
            copies.append(own)
        sends = []
        for p in range(1, N_DEV):
            px, py, pc = x ^ ((p >> 2) & 1), y ^ ((p >> 1) & 1), c ^ (p & 1)
            peer = 4 * px + 2 * py + pc
            for k in range(n):
                cp = pltpu.make_async_remote_copy(
                    src_ref=ins[k].at[peer] if scatter else ins[k],
                    dst_ref=outs[k].at[me],
                    send_sem=send_sems.at[k, p - 1],
                    recv_sem=recv_sems.at[k, p - 1],
                    device_id=(px, py, pc),
                    device_id_type=pl.DeviceIdType.MESH,
                )
                cp.start()
                sends.append((cp, k, peer, p))
        for cp, k, peer, p in sends:
            pltpu.make_async_remote_copy(
                src_ref=ins[k].at[peer] if scatter else ins[k],
                dst_ref=outs[k].at[peer],
                send_sem=send_sems.at[k, p - 1],
                recv_sem=recv_sems.at[k, p - 1],
                device_id=(x, y, c),
                device_id_type=pl.DeviceIdType.MESH,
            ).wait_recv()
        for cp, _, _, _ in sends:
            cp.wait_send()
        for own in copies:
            own.wait()

    any_spec = pl.BlockSpec(memory_space=pl.ANY)
    outs = pl.pallas_call(
        body,
        name=name,
        out_shape=out_shape,
        in_specs=[any_spec] * n,
        out_specs=tuple([any_spec] * n),
        scratch_shapes=[
            pltpu.SemaphoreType.DMA((n, N_DEV - 1)),
            pltpu.SemaphoreType.DMA((n, N_DEV - 1)),
            pltpu.SemaphoreType.DMA((n,)),
        ],
        compiler_params=pltpu.CompilerParams(has_side_effects=True),
    )(*arrays)
    return list(outs)


def _gather_two_level(arrays, *, name):
    n = len(arrays)
    out_shape = tuple(jax.ShapeDtypeStruct((N_DEV,) + a.shape, a.dtype) for a in arrays)

    def body(*refs):
        ins, outs = refs[:n], refs[n:2 * n]
        send_sems, recv_sems, local_sems = refs[2 * n:]
        x, y, c = lax.axis_index("x"), lax.axis_index("y"), lax.axis_index("c")
        me = 4 * x + 2 * y + c
        sibling = (x, y, 1 - c)
        chips = [(1 - x, y), (x, 1 - y), (1 - x, 1 - y)]

        def slot(px, py, pc):
            return 4 * px + 2 * py + pc

        def copy(k, q, block, to, src=None):
            return pltpu.make_async_remote_copy(
                src_ref=outs[k].at[slot(*block)] if src is None else src,
                dst_ref=outs[k].at[slot(*block)],
                send_sem=send_sems.at[k, q], recv_sem=recv_sems.at[k, q],
                device_id=to, device_id_type=pl.DeviceIdType.MESH)

        own = [pltpu.make_async_copy(ins[k], outs[k].at[me], local_sems.at[k]) for k in range(n)]
        for cp in own:
            cp.start()
        first = []
        for k in range(n):
            first.append(copy(k, 0, (x, y, c), sibling, src=ins[k]))
            first += [copy(k, 1 + j, (x, y, c), (*chip, c), src=ins[k]) for j, chip in enumerate(chips)]
        for cp in first:
            cp.start()
        passed = []
        for j, chip in enumerate(chips):
            for k in range(n):
                copy(k, 1 + j, (*chip, c), (x, y, c)).wait_recv()
                fwd = copy(k, 4 + j, (*chip, c), sibling)
                fwd.start()
                passed.append(fwd)
        for k in range(n):
            copy(k, 0, sibling, (x, y, c)).wait_recv()
            for j, chip in enumerate(chips):
                copy(k, 4 + j, (*chip, 1 - c), (x, y, c)).wait_recv()
        for cp in first + passed:
            cp.wait_send()
        for cp in own:
            cp.wait()

    any_spec = pl.BlockSpec(memory_space=pl.ANY)
    outs = pl.pallas_call(
        body, name=name, out_shape=out_shape, in_specs=[any_spec] * n, out_specs=tuple([any_spec] * n),
        scratch_shapes=[pltpu.SemaphoreType.DMA((n, N_DEV - 1)), pltpu.SemaphoreType.DMA((n, N_DEV - 1)),
                        pltpu.SemaphoreType.DMA((n,))],
        compiler_params=pltpu.CompilerParams(has_side_effects=True),
    )(*arrays)
    return list(outs)


def _peer(x, y, c, p):
    return x ^ ((p >> 2) & 1), y ^ ((p >> 1) & 1), c ^ (p & 1)


def _exchange_start(arrays, *, scatter, name, dep=None):
    n = len(arrays)
    deps = [] if dep is None else [dep]
    lands = [lax.empty(a.shape if scatter else (N_DEV,) + a.shape, a.dtype) for a in arrays]

    def body(*refs):
        ins, zones = refs[:n], refs[n:2 * n]
        send_sems, recv_sems = refs[2 * n + len(deps)], refs[2 * n + len(deps) + 1]
        token = refs[-1]
        x, y, c = lax.axis_index("x"), lax.axis_index("y"), lax.axis_index("c")
        me = 4 * x + 2 * y + c
        for p in range(1, N_DEV):
            px, py, pc = _peer(x, y, c, p)
            for k in range(n):
                pltpu.make_async_remote_copy(
                    src_ref=ins[k].at[4 * px + 2 * py + pc] if scatter else ins[k],
                    dst_ref=zones[k].at[me],
                    send_sem=send_sems.at[k * (N_DEV - 1) + p - 1],
                    recv_sem=recv_sems.at[k * (N_DEV - 1) + p - 1],
                    device_id=(px, py, pc),
                    device_id_type=pl.DeviceIdType.MESH,
                ).start()
        token[...] = jnp.zeros_like(token)

    hbm = pl.BlockSpec(memory_space=pltpu.HBM)
    sem = pl.BlockSpec(memory_space=pltpu.SEMAPHORE)
    outs = pl.pallas_call(
        body,
        name=name,
        out_shape=(pltpu.SemaphoreType.DMA((n * (N_DEV - 1),)), pltpu.SemaphoreType.DMA((n * (N_DEV - 1),)),
                   *[pltpu.HBM(a.shape, a.dtype) for a in arrays], *[pltpu.HBM(z.shape, z.dtype) for z in lands],
                   jax.ShapeDtypeStruct((8, LANES), F32)),
        in_specs=[hbm] * (2 * n) + [pl.BlockSpec(memory_space=pl.ANY)] * len(deps),
        out_specs=(sem, sem, *[hbm] * (2 * n), pl.BlockSpec(memory_space=pltpu.VMEM)),
        input_output_aliases={k: 2 + k for k in range(2 * n)},
        compiler_params=pltpu.CompilerParams(has_side_effects=pltpu.SideEffectType.DATAFLOW_SIDE_EFFECTING),
    )(*[pltpu.with_memory_space_constraint(a, pltpu.HBM) for a in arrays],
      *[pltpu.with_memory_space_constraint(z, pltpu.HBM) for z in lands], *deps)
    return outs[0], outs[1], list(outs[2:2 + n]), list(outs[2 + n:2 + 2 * n]), outs[-1]


def _exchange_wait(started, after, *, scatter, name):
    send_sems, recv_sems, srcs, lands, _ = started
    n = len(srcs)

    def body(*refs):
        ins, zones = refs[:n], refs[n:2 * n]
        s_sems, r_sems = refs[2 * n], refs[2 * n + 1]
        local_sems = refs[-1]
        x, y, c = lax.axis_index("x"), lax.axis_index("y"), lax.axis_index("c")
        me = 4 * x + 2 * y + c
        own = [pltpu.make_async_copy(ins[k].at[me] if scatter else ins[k], zones[k].at[me], local_sems.at[k])
               for k in range(n)]
        for cp in own:
            cp.start()
        for p in range(1, N_DEV):
            px, py, pc = _peer(x, y, c, p)
            peer = 4 * px + 2 * py + pc
            for k in range(n):
                cp = pltpu.make_async_remote_copy(
                    src_ref=ins[k].at[peer] if scatter else ins[k],
                    dst_ref=zones[k].at[peer],
                    send_sem=s_sems.at[k * (N_DEV - 1) + p - 1],
                    recv_sem=r_sems.at[k * (N_DEV - 1) + p - 1],
                    device_id=(px, py, pc),
                    device_id_type=pl.DeviceIdType.MESH,
                )
                cp.wait_send()
                cp.wait_recv()
        for cp in own:
            cp.wait()

    hbm = pl.BlockSpec(memory_space=pltpu.HBM)
    sem = pl.BlockSpec(memory_space=pltpu.SEMAPHORE)
    outs = pl.pallas_call(
        body,
        name=name,
        out_shape=tuple(pltpu.HBM(a.shape, a.dtype) for a in srcs + lands),
        in_specs=[hbm] * (2 * n) + [sem, sem, pl.BlockSpec(memory_space=pl.ANY)],
        out_specs=tuple([hbm] * (2 * n)),
        scratch_shapes=[pltpu.SemaphoreType.DMA((n,))],
        input_output_aliases={k: k for k in range(2 * n)},
        compiler_params=pltpu.CompilerParams(has_side_effects=pltpu.SideEffectType.DATAFLOW_SIDE_EFFECTING),
    )(*srcs, *lands, send_sems, recv_sems, after)
    return list(outs[:n]), list(outs[n:])


def _mm(a, b, *, mode, out_dtype, name, add=None, tm=512, tn=512, b_rows=None, dep=None):
    rows_b = b.shape[0] if b_rows is None else b_rows
    if mode == "nn":
        (m, kd), nd = a.shape, b.shape[1]
        assert kd == rows_b
    elif mode == "nt":
        (m, kd), nd = a.shape, rows_b
    else:
        (kd, m), nd = a.shape, b.shape[1]
    tm = _pick(m, tm, LANES if mode == "tn" else 16)
    tn = _pick(nd, tn)
    dims = {"nn": NN, "nt": NT, "tn": TN}[mode]
    ni, nj = m // tm, nd // tn
    a_bytes, b_bytes = a.size * a.dtype.itemsize, b.size * b.dtype.itemsize
    i_outer = a_bytes + ni * b_bytes <= b_bytes + nj * a_bytes
    ij = (lambda g0, g1: (g0, g1)) if i_outer else (lambda g0, g1: (g1, g0))
    a_spec = (pl.BlockSpec((kd, tm), lambda g0, g1: (0, ij(g0, g1)[0])) if mode == "tn"
              else pl.BlockSpec((tm, kd), lambda g0, g1: (ij(g0, g1)[0], 0)))
    b_spec = (pl.BlockSpec((tn, kd), lambda g0, g1: (ij(g0, g1)[1], 0)) if mode == "nt"
              else pl.BlockSpec((kd, tn), lambda g0, g1: (0, ij(g0, g1)[1])))
    o_spec = pl.BlockSpec((tm, tn), lambda g0, g1: ij(g0, g1))
    has_add = add is not None

    def body(*refs):
        a_ref, b_ref = refs[0], refs[1]
        o_ref = refs[-1]
        acc = _dotb(a_ref[...], b_ref[...], dims)
        if has_add:
            acc = acc + refs[2][...].astype(F32)
        o_ref[...] = acc.astype(o_ref.dtype)

    ins = [a, b] + ([add] if has_add else []) + ([] if dep is None else [dep])
    specs = ([a_spec, b_spec] + ([o_spec] if has_add else [])
             + ([] if dep is None else [pl.BlockSpec((8, LANES), lambda g0, g1: (0, 0))]))
    return pl.pallas_call(
        body, name=name, grid=(ni, nj) if i_outer else (nj, ni), in_specs=specs, out_specs=o_spec,
        out_shape=jax.ShapeDtypeStruct((m, nd), out_dtype),
        compiler_params=_params("parallel", "parallel"),
    )(*ins)


def _mm_resid(a, b, x, gate, *, name, tm=256, tn=1024):
    m, kd = a.shape
    nd = b.shape[1]
    tm = _pick(m, tm, 16)
    tn = _pick(nd, tn)
    o_spec = pl.BlockSpec((tm, tn), lambda i, j: (i, j))

    def body(a_ref, b_ref, x_ref, g_ref, xo_ref, y_ref):
        y = _dotb(a_ref[...], b_ref[...], NN)
        y_ref[...] = y
        xo_ref[...] = x_ref[...] + g_ref[...] * y

    return pl.pallas_call(
        body, name=name, grid=(m // tm, nd // tn),
        in_specs=[pl.BlockSpec((tm, kd), lambda i, j: (i, 0)), pl.BlockSpec((kd, tn), lambda i, j: (0, j)),
                  o_spec, pl.BlockSpec((1, tn), lambda i, j: (0, j))],
        out_specs=(o_spec, o_spec),
        out_shape=(jax.ShapeDtypeStruct((m, nd), F32), jax.ShapeDtypeStruct((m, nd), F32)),
        compiler_params=_params("parallel", "parallel"),
    )(a, b, x, gate)


ROWS = 256


def _row_spec(width, rows=ROWS):
    return pl.BlockSpec((rows, width), lambda i: (i, 0))


def _const_spec(shape):
    return pl.BlockSpec(shape, lambda i: tuple(0 for _ in shape))


def _adaln_fwd(x, g, scale, shift, *, name):
    t, d = x.shape

    def body(x_ref, g_ref, sc_ref, sh_ref, h_ref):
        xv = x_ref[...]
        r = lax.rsqrt(jnp.mean(xv * xv, axis=-1, keepdims=True) + EPS)
        h_ref[...] = (xv * r * g_ref[...] * (1.0 + sc_ref[...]) + sh_ref[...]).astype(h_ref.dtype)

    return pl.pallas_call(
        body, name=name, grid=(t // ROWS,),
        in_specs=[_row_spec(d), _const_spec((1, d)), _const_spec((1, d)), _const_spec((1, d))],
        out_specs=_row_spec(d), out_shape=jax.ShapeDtypeStruct((t, d), BF16),
        compiler_params=_params("parallel"),
    )(x, g, scale, shift)


def _adaln_bwd(x, g, scale, shift, dh, dres, dep, *, name):
    t, d = x.shape

    def body(x_ref, g_ref, sc_ref, sh_ref, dh_ref, dr_ref, dep_ref, dx_ref, st_ref):
        @pl.when(pl.program_id(0) == 0)
        def _():
            st_ref[...] = jnp.zeros_like(st_ref)

        xv = x_ref[...]
        dhv = dh_ref[...].astype(F32)
        gv = g_ref[...]
        r = lax.rsqrt(jnp.mean(xv * xv, axis=-1, keepdims=True) + EPS)
        xh = xv * r
        nv = xh * gv
        dn = dhv * (1.0 + sc_ref[...])
        dxh = dn * gv
        dx_ref[...] = dr_ref[...] + r * (dxh - xh * jnp.mean(dxh * xh, axis=-1, keepdims=True))
        st_ref[0:1, :] += jnp.sum(dn * xh, axis=0, keepdims=True)
        st_ref[1:2, :] += jnp.sum(dhv * nv, axis=0, keepdims=True)
        st_ref[2:3, :] += jnp.sum(dhv, axis=0, keepdims=True)

    return pl.pallas_call(
        body, name=name, grid=(t // ROWS,),
        in_specs=[_row_spec(d), _const_spec((1, d)), _const_spec((1, d)), _const_spec((1, d)),
                  _row_spec(d), _row_spec(d), _const_spec((8, LANES))],
        out_specs=(_row_spec(d), _const_spec((8, d))),
        out_shape=(jax.ShapeDtypeStruct((t, d), F32), jax.ShapeDtypeStruct((8, d), F32)),
        compiler_params=_params("arbitrary"),
    )(x, g, scale, shift, dh, dres, dep)


def _gate_bwd(dxo, y, gate, dep, *, name):
    t, d = dxo.shape

    def body(dx_ref, y_ref, g_ref, dep_ref, dy_ref, st_ref):
        @pl.when(pl.program_id(0) == 0)
        def _():
            st_ref[...] = jnp.zeros_like(st_ref)

        dxv = dx_ref[...]
        dy_ref[...] = (dxv * g_ref[...]).astype(dy_ref.dtype)
        st_ref[0:1, :] += jnp.sum(dxv * y_ref[...], axis=0, keepdims=True)

    return pl.pallas_call(
        body, name=name, grid=(t // ROWS,),
        in_specs=[_row_spec(d), _row_spec(d), _const_spec((1, d)), _const_spec((8, LANES))],
        out_specs=(_row_spec(d), _const_spec((8, d))),
        out_shape=(jax.ShapeDtypeStruct((t, d), BF16), jax.ShapeDtypeStruct((8, d), F32)),
        compiler_params=_params("arbitrary"),
    )(dxo, y, gate, dep)


def _loss_head(x, g, target, *, name):
    t, d = x.shape

    def body(x_ref, g_ref, t_ref, dx_ref, st_ref, ls_ref):
        @pl.when(pl.program_id(0) == 0)
        def _():
            st_ref[...] = jnp.zeros_like(st_ref)
            ls_ref[...] = jnp.zeros_like(ls_ref)

        xv = x_ref[...]
        gv = g_ref[...]
        r = lax.rsqrt(jnp.mean(xv * xv, axis=-1, keepdims=True) + EPS)
        xh = xv * r
        err = xh * gv - t_ref[...]
        ls_ref[...] += 0.5 * jnp.sum(jnp.mean(err * err, axis=-1, keepdims=True))
        dy = err * (1.0 / d)
        dxh = dy * gv
        dx_ref[...] = r * (dxh - xh * jnp.mean(dxh * xh, axis=-1, keepdims=True))
        st_ref[0:1, :] += jnp.sum(dy * xh, axis=0, keepdims=True)

    return pl.pallas_call(
        body, name=name, grid=(t // ROWS,),
        in_specs=[_row_spec(d), _const_spec((1, d)), _row_spec(d)],
        out_specs=(_row_spec(d), _const_spec((8, d)), _const_spec((8, LANES))),
        out_shape=(jax.ShapeDtypeStruct((t, d), F32), jax.ShapeDtypeStruct((8, d), F32),
                   jax.ShapeDtypeStruct((8, LANES), F32)),
        compiler_params=_params("arbitrary"),
    )(x, g, target)


FFN_BLOCK = D_FF // 2


def _ffn_gu_fwd(h, wg, wu, dep, *, name):
    t, d = h.shape
    tn = FFN_BLOCK

    def body(h_ref, wg_ref, wu_ref, dep_ref, s_ref, a_ref, b_ref):
        hv = h_ref[...]
        a = _dotb(hv, wg_ref[...], NT)
        b = _dotb(hv, wu_ref[...], NT)
        s_ref[...] = (a * _sigmoid(a) * b).astype(s_ref.dtype)
        a_ref[...] = a.astype(a_ref.dtype)
        b_ref[...] = b.astype(b_ref.dtype)

    w_spec = pl.BlockSpec((tn, d), lambda j, i: (j, 0))
    o_spec = pl.BlockSpec((ROWS, tn), lambda j, i: (i, j))
    return pl.pallas_call(
        body, name=name, grid=(D_FF // tn, t // ROWS),
        in_specs=[pl.BlockSpec((ROWS, d), lambda j, i: (i, 0)), w_spec, w_spec,
                  pl.BlockSpec((8, LANES), lambda j, i: (0, 0))],
        out_specs=(o_spec, o_spec, o_spec),
        out_shape=(jax.ShapeDtypeStruct((t, D_FF), BF16),) * 3,
        compiler_params=_params("parallel", "parallel"),
    )(h, wg, wu, dep)


def _ffn_down_dx(dy, w_down, a, b, *, name):
    t, d = dy.shape
    tn = FFN_BLOCK

    def body(dy_ref, w_ref, a_ref, b_ref, da_ref, db_ref):
        ds = _dotb(dy_ref[...], w_ref[...], NT)
        av = a_ref[...].astype(F32)
        sg = _sigmoid(av)
        da_ref[...] = (ds * b_ref[...].astype(F32) * sg * (1.0 + av * (1.0 - sg))).astype(da_ref.dtype)
        db_ref[...] = (ds * av * sg).astype(db_ref.dtype)

    o_spec = pl.BlockSpec((ROWS, tn), lambda j, i: (i, j))
    return pl.pallas_call(
        body, name=name, grid=(D_FF // tn, t // ROWS),
        in_specs=[pl.BlockSpec((ROWS, d), lambda j, i: (i, 0)), pl.BlockSpec((tn, d), lambda j, i: (j, 0)),
                  o_spec, o_spec],
        out_specs=(o_spec, o_spec),
        out_shape=(jax.ShapeDtypeStruct((t, D_FF), BF16),) * 2,
        compiler_params=_params("parallel", "parallel"),
    )(dy, w_down, a, b)


def _shift_rows(v, s, rows):
    if s == 0:
        return v
    return jnp.where(rows >= s, pltpu.roll(v, s, 0), 0.0)


def _unshift_rows(v, s, rows, t):
    if s == 0:
        return v
    return jnp.where(rows < t - s, pltpu.roll(v, t - s, 0), 0.0)


def _conv_silu(x, w, rows):
    z = w[GDN_CONV - 1:GDN_CONV, :] * x
    for j in range(GDN_CONV - 1):
        z = z + w[j:j + 1, :] * _shift_rows(x, GDN_CONV - 1 - j, rows)
    sg = _sigmoid(z)
    return z, sg, z * sg


def _gdn_prep_fwd(proj, conv_wt, *, name):
    t = proj.shape[0]
    nh = GDN_HEADS

    hp = GDN_PREP_HEADS
    wd = hp * LANES

    def body(x_ref, w_ref, y_ref):
        j = pl.program_id(0) * hp
        rows = lax.broadcasted_iota(jnp.int32, (t, LANES), 0)
        qscale = jnp.where(j < nh, GDN_HEAD_DIM ** -0.5, 1.0)
        for i in range(hp):
            sl = slice(i * LANES, (i + 1) * LANES)
            _, _, s = _conv_silu(x_ref[:, sl], w_ref[:, sl], rows)
            rs = lax.rsqrt(jnp.sum(s * s, axis=-1, keepdims=True) + EPS)
            y_ref[:, sl] = jnp.where(j < 2 * nh, s * rs * qscale, s)

    return pl.pallas_call(
        body, name=name, grid=(3 * nh // hp,),
        in_specs=[pl.BlockSpec((t, wd), lambda j: (0, j)), pl.BlockSpec((GDN_CONV, wd), lambda j: (0, j))],
        out_specs=pl.BlockSpec((t, wd), lambda j: (0, j)),
        out_shape=jax.ShapeDtypeStruct((t, 3 * GDN_KEY_DIM), F32),
        compiler_params=_params("parallel"),
    )(proj, conv_wt)


def _gdn_prep_bwd(proj, conv_wt, dy, *, name):
    t = proj.shape[0]
    nh = GDN_HEADS

    hp = GDN_PREP_HEADS
    wd = hp * LANES
    per_seg = nh // hp

    def body(x_ref, w_ref, dy_ref, dx_ref, dw_ref):
        j = pl.program_id(0) * hp
        rows = lax.broadcasted_iota(jnp.int32, (t, LANES), 0)
        qscale = jnp.where(j < nh, GDN_HEAD_DIM ** -0.5, 1.0)
        for i in range(hp):
            sl = slice(i * LANES, (i + 1) * LANES)
            x = x_ref[:, sl]
            w = w_ref[:, sl]
            z, sg, s = _conv_silu(x, w, rows)
            rs = lax.rsqrt(jnp.sum(s * s, axis=-1, keepdims=True) + EPS)
            dyv = dy_ref[:, sl]
            nv = s * rs
            de = dyv * qscale
            ds_qk = rs * (de - nv * jnp.sum(de * nv, axis=-1, keepdims=True))
            ds = jnp.where(j < 2 * nh, ds_qk, dyv)
            dz = ds * sg * (1.0 + z * (1.0 - sg))
            dx = w[GDN_CONV - 1:GDN_CONV, :] * dz
            dw_ref[GDN_CONV - 1:GDN_CONV, sl] = jnp.sum(dz * x, axis=0, keepdims=True)
            for k in range(GDN_CONV - 1):
                sh = GDN_CONV - 1 - k
                dx = dx + w[k:k + 1, :] * _unshift_rows(dz, sh, rows, t)
                dw_ref[k:k + 1, sl] = jnp.sum(dz * _shift_rows(x, sh, rows), axis=0, keepdims=True)
            dx_ref[:, sl] = dx.astype(dx_ref.dtype)

    return pl.pallas_call(
        body, name=name, grid=(3 * nh // hp,),
        in_specs=[pl.BlockSpec((t, wd), lambda j: (0, j)), pl.BlockSpec((GDN_CONV, wd), lambda j: (0, j)),
                  pl.BlockSpec((None, t, wd), lambda j: (j // per_seg, 0, j % per_seg))],
        out_specs=(pl.BlockSpec((t, wd), lambda j: (0, j)), pl.BlockSpec((GDN_CONV, wd), lambda j: (0, j))),
        out_shape=(jax.ShapeDtypeStruct((t, 3 * GDN_KEY_DIM), BF16),
                   jax.ShapeDtypeStruct((GDN_CONV, 3 * GDN_KEY_DIM), F32)),
        compiler_params=_params("parallel"),
    )(proj, conv_wt, dy)


def _softplus(z):
    return jnp.maximum(z, 0.0) + jnp.log(1.0 + jnp.exp(-jnp.abs(z)))


def _gdn_gate_fwd(ab, prm, *, name):
    t = ab.shape[0]

    def body(ab_ref, p_ref, o_ref):
        v = ab_ref[...]
        lane = lax.broadcasted_iota(jnp.int32, v.shape, 1)
        g = -jnp.exp(p_ref[0:1, :]) * _softplus(v + p_ref[1:2, :])
        o_ref[...] = jnp.where(lane < GDN_HEADS, g, jnp.where(lane < 2 * GDN_HEADS, _sigmoid(v), 0.0))

    return pl.pallas_call(
        body, name=name, grid=(t // ROWS,),
        in_specs=[_row_spec(LANES), _const_spec((8, LANES))], out_specs=_row_spec(LANES),
        out_shape=jax.ShapeDtypeStruct((t, LANES), F32), compiler_params=_params("parallel"),
    )(ab, prm)


def _gdn_gate_bwd(ab, prm, dgb, *, name):
    t = ab.shape[0]

    def body(ab_ref, p_ref, d_ref, o_ref, st_ref):
        @pl.when(pl.program_id(0) == 0)
        def _():
            st_ref[...] = jnp.zeros_like(st_ref)

        v = ab_ref[...]
        dv = d_ref[...]
        lane = lax.broadcasted_iota(jnp.int32, v.shape, 1)
        is_a = lane < GDN_HEADS
        is_b = jnp.logical_and(lane >= GDN_HEADS, lane < 2 * GDN_HEADS)
        a_exp = jnp.exp(p_ref[0:1, :])
        zz = v + p_ref[1:2, :]
        g = -a_exp * _softplus(zz)
        da = dv * (-a_exp) * _sigmoid(zz)
        beta = _sigmoid(v)
        db = dv * beta * (1.0 - beta)
        o_ref[...] = jnp.where(is_a, da, jnp.where(is_b, db, 0.0)).astype(o_ref.dtype)
        st_ref[0:1, :] += jnp.sum(jnp.where(is_a, dv * g, 0.0), axis=0, keepdims=True)
        st_ref[1:2, :] += jnp.sum(jnp.where(is_a, da, 0.0), axis=0, keepdims=True)

    return pl.pallas_call(
        body, name=name, grid=(t // ROWS,),
        in_specs=[_row_spec(LANES), _const_spec((8, LANES)), _row_spec(LANES)],
        out_specs=(_row_spec(LANES), _const_spec((8, LANES))),
        out_shape=(jax.ShapeDtypeStruct((t, LANES), BF16), jax.ShapeDtypeStruct((8, LANES), F32)),
        compiler_params=_params("arbitrary"),
    )(ab, prm, dgb)


def _gdn_local(qs, ks, vs, gbs, bbs):
    nh = len(qs)
    cs = qs[0].shape[0]
    hs = range(nh)
    r = lax.broadcasted_iota(jnp.int32, (cs, cs), 0)
    c = lax.broadcasted_iota(jnp.int32, (cs, cs), 1)
    tril, strict, eye = r >= c, r > c, r == c
    ident = jnp.where(eye, 1.0, 0.0)
    g_colb = [gbs[h][:, :cs] for h in hs]
    g_row = [jnp.sum(jnp.where(eye, g_colb[h], 0.0), axis=0, keepdims=True) for h in hs]
    gc_col = [jnp.sum(jnp.where(tril, g_row[h], 0.0), axis=1, keepdims=True) for h in hs]
    gc_row = [jnp.sum(jnp.where(r <= c, g_colb[h], 0.0), axis=0, keepdims=True) for h in hs]
    decay = [jnp.exp(jnp.where(tril, gc_col[h] - gc_row[h], NEG)) for h in hs]
    gamma = [jnp.exp(gc_col[h]) for h in hs]
    gcl = [gc_col[h][cs - 1:cs, :] for h in hs]
    gl = [jnp.exp(gcl[h]) for h in hs]
    kdec = [jnp.exp(gcl[h] - gc_col[h]) for h in hs]
    kb = [ks[h] * bbs[h] for h in hs]
    kk = [_dotb(kb[h], ks[h], NT) for h in hs]
    qk = [_dotb(qs[h], ks[h], NT) for h in hs]
    lmat = [jnp.where(strict, kk[h] * decay[h], 0.0) for h in hs]
    pmat = [jnp.where(tril, qk[h] * decay[h], 0.0) for h in hs]
    xm = [-lmat[h] for h in hs]
    tinv = [ident + xm[h] for h in hs]
    for _ in range(int(math.log2(cs)) - 1):
        xm = [_dotf(xm[h], xm[h], NN) for h in hs]
        tinv = [tinv[h] + _dotf(tinv[h], xm[h], NN) for h in hs]
    vb = [vs[h] * bbs[h] for h in hs]
    kg = [kb[h] * gamma[h] for h in hs]
    u = [_dotf(tinv[h], vb[h], NN) for h in hs]
    w = [_dotf(tinv[h], kg[h], NN) for h in hs]
    return [dict(tril=tril, strict=strict, eye=eye, r=r, c=c, decay=decay[h], gamma=gamma[h], gl=gl[h], kdec=kdec[h],
                 kb=kb[h], lmat=lmat[h], tinv=tinv[h], vb=vb[h], kg=kg[h], u=u[h], w=w[h], pmat=pmat[h],
                 qd=qs[h] * gamma[h], kd=ks[h] * kdec[h]) for h in hs]


def _gdn_chunk_fwd(qkv, gbc, bbc, *, name):
    t = qkv.shape[0]
    nh, cs, hd = GDN_HEADS, GDN_CHUNK, GDN_HEAD_DIM
    nc = t // cs

    hb = GDN_HEAD_BATCH
    ng = nh // hb

    def body(q_ref, k_ref, v_ref, g_ref, b_ref, o_ref, st_ref, s_ref):
        @pl.when(pl.program_id(1) == 0)
        def _():
            s_ref[...] = jnp.zeros_like(s_ref)

        sls = [slice(i * hd, (i + 1) * hd) for i in range(hb)]
        hs = range(hb)
        s = [s_ref[i] for i in hs]
        lo = _gdn_local([q_ref[:, sl] for sl in sls], [k_ref[:, sl] for sl in sls], [v_ref[:, sl] for sl in sls],
                        [g_ref[i] for i in hs], [b_ref[i] for i in hs])
        ws = [_dotb(lo[i]["w"], s[i], NN) for i in hs]
        qs = [_dotb(lo[i]["qd"], s[i], NN) for i in hs]
        vn = [lo[i]["u"] - ws[i] for i in hs]
        pv = [_dotb(lo[i]["pmat"], vn[i], NN) for i in hs]
        kv = [_dotb(lo[i]["kd"], vn[i], TN) for i in hs]
        for i, sl in enumerate(sls):
            st_ref[i, 0] = s[i]
            o_ref[:, sl] = qs[i] + pv[i]
            s_ref[i] = s[i] * lo[i]["gl"] + kv[i]

    gspec = pl.BlockSpec((hb, cs, LANES), lambda h, n: (h, n, 0))
    col = lambda off: pl.BlockSpec((cs, hb * hd), lambda h, n: (n, off + h))
    return pl.pallas_call(
        body, name=name, grid=(ng, nc),
        in_specs=[col(0), col(ng), col(2 * ng), gspec, gspec],
        out_specs=(col(0), pl.BlockSpec((hb, 1, hd, hd), lambda h, n: (h, n, 0, 0))),
        out_shape=(jax.ShapeDtypeStruct((t, nh * hd), F32), jax.ShapeDtypeStruct((nh, nc, hd, hd), F32)),
        scratch_shapes=[pltpu.VMEM((hb, hd, hd), F32)],
        compiler_params=_params("parallel", "arbitrary"),
    )(qkv, qkv, qkv, gbc, bbc)


def _gdn_chunk_bwd(qkv, gbc, bbc, states, do, *, name):
    t = qkv.shape[0]
    nh, cs, hd = GDN_HEADS, GDN_CHUNK, GDN_HEAD_DIM
    nc = t // cs

    hb = GDN_HEAD_BATCH
    ng = nh // hb

    def heads_bwd(q, k, v, gb, bb, s, dsn, dov):
        hs = range(len(q))
        lo = _gdn_local(q, k, v, gb, bb)
        tril, strict, eye, r, c = lo[0]["tril"], lo[0]["strict"], lo[0]["eye"], lo[0]["r"], lo[0]["c"]
        rowi = lax.broadcasted_iota(jnp.int32, (cs, 1), 0)
        get = lambda name: [lo[h][name] for h in hs]
        decay, gamma, gl, kdec = get("decay"), get("gamma"), get("gl"), get("kdec")
        kb, tinv, w, pmat, kd, qd = get("kb"), get("tinv"), get("w"), get("pmat"), get("kd"), get("qd")
        ws = [_dotb(w[h], s[h], NN) for h in hs]
        pdo = [_dotb(pmat[h], dov[h], TN) for h in hs]
        kds = [_dotb(kd[h], dsn[h], NN) for h in hs]
        dqd = [_dotb(dov[h], s[h], NT) for h in hs]
        qdo = [_dotb(qd[h], dov[h], TN) for h in hs]
        vn = [lo[h]["u"] - ws[h] for h in hs]
        dvn = [pdo[h] + kds[h] for h in hs]
        dp = [jnp.where(tril, _dotb(dov[h], vn[h], NT), 0.0) for h in hs]
        dkd = [_dotb(vn[h], dsn[h], NT) for h in hs]
        dw = [-_dotb(dvn[h], s[h], NT) for h in hs]
        wdv = [_dotb(w[h], dvn[h], TN) for h in hs]
        dvb = [_dotf(tinv[h], dvn[h], TN) for h in hs]
        dt1 = [_dotf(dvn[h], lo[h]["vb"], NT) for h in hs]
        dkg = [_dotf(tinv[h], dw[h], TN) for h in hs]
        dt2 = [_dotf(dw[h], lo[h]["kg"], NT) for h in hs]
        tdt = [_dotf(tinv[h], dt1[h] + dt2[h], TN) for h in hs]
        dl = [jnp.where(strict, -_dotf(tdt[h], tinv[h], NT), 0.0) for h in hs]
        dkk = [dl[h] * decay[h] for h in hs]
        dqk = [dp[h] * decay[h] for h in hs]
        dkb = [_dotb(dkk[h], k[h], NN) + dkg[h] * gamma[h] for h in hs]
        dk1 = [_dotb(dkk[h], kb[h], TN) for h in hs]
        dk2 = [_dotb(dqk[h], q[h], TN) for h in hs]
        dq1 = [_dotb(dqk[h], k[h], NN) for h in hs]
        out = []
        for h in hs:
            dgl = jnp.sum(jnp.sum(dsn[h] * s[h], axis=1, keepdims=True), axis=0, keepdims=True)
            ds_prev = gl[h] * dsn[h] + qdo[h] - wdv[h]
            dk = dk1[h] + dk2[h] + dkd[h] * kdec[h] + dkb[h] * bb[h]
            dq = dq1[h] + dqd[h] * gamma[h]
            dbeta = jnp.sum(dvb[h] * v[h], axis=-1, keepdims=True) + jnp.sum(dkb[h] * k[h], axis=-1, keepdims=True)
            e = dl[h] * lo[h]["lmat"] + dp[h] * pmat[h]
            e_col = jnp.sum(e, axis=0, keepdims=True)
            dgc = jnp.sum(e, axis=1, keepdims=True) - jnp.sum(jnp.where(eye, e_col, 0.0), axis=1, keepdims=True)
            dgamma = (jnp.sum(dqd[h] * q[h], axis=-1, keepdims=True)
                      + jnp.sum(dkg[h] * kb[h], axis=-1, keepdims=True))
            rk = jnp.sum(dkd[h] * k[h], axis=-1, keepdims=True) * kdec[h]
            dgcl = jnp.sum(rk, axis=0, keepdims=True) + dgl * gl[h]
            dgc = dgc + dgamma * gamma[h] - rk + jnp.where(rowi == cs - 1, dgcl, 0.0)
            dgc_row = jnp.sum(jnp.where(eye, dgc, 0.0), axis=0, keepdims=True)
            dg = jnp.sum(jnp.where(c >= r, dgc_row, 0.0), axis=1, keepdims=True)
            out.append((dq, dk, dvb[h] * bb[h], dbeta, dg, ds_prev))
        return out

    def body(q_ref, k_ref, v_ref, g_ref, b_ref, st_ref, do_ref, d_ref, dg_ref, db_ref, ds_ref):
        @pl.when(pl.program_id(1) == 0)
        def _():
            ds_ref[...] = jnp.zeros_like(ds_ref)

        sls = [slice(i * hd, (i + 1) * hd) for i in range(hb)]
        hs = range(hb)
        outs = heads_bwd([q_ref[:, sl] for sl in sls], [k_ref[:, sl] for sl in sls], [v_ref[:, sl] for sl in sls],
                         [g_ref[i] for i in hs], [b_ref[i] for i in hs], [st_ref[i, 0] for i in hs],
                         [ds_ref[i] for i in hs], [do_ref[:, sl] for sl in sls])
        for i, sl in enumerate(sls):
            dq, dk, dv, dbeta, dg, ds_prev = outs[i]
            d_ref[0, :, sl], d_ref[1, :, sl], d_ref[2, :, sl] = dq, dk, dv
            db_ref[i] = jnp.broadcast_to(dbeta, (cs, LANES))
            dg_ref[i] = jnp.broadcast_to(dg, (cs, LANES))
            ds_ref[i] = ds_prev

    gspec = pl.BlockSpec((hb, cs, LANES), lambda h, n: (h, nc - 1 - n, 0))
    col = lambda off: pl.BlockSpec((cs, hb * hd), lambda h, n: (nc - 1 - n, off + h))
    return pl.pallas_call(
        body, name=name, grid=(ng, nc),
        in_specs=[col(0), col(ng), col(2 * ng), gspec, gspec,
                  pl.BlockSpec((hb, 1, hd, hd), lambda h, n: (h, nc - 1 - n, 0, 0)), col(0)],
        out_specs=(pl.BlockSpec((3, cs, hb * hd), lambda h, n: (0, nc - 1 - n, h)), gspec, gspec),
        out_shape=(jax.ShapeDtypeStruct((3, t, nh * hd), F32),) + (jax.ShapeDtypeStruct((nh, t, LANES), F32),) * 2,
        scratch_shapes=[pltpu.VMEM((hb, hd, hd), F32)],
        compiler_params=_params("parallel", "arbitrary"),
    )(qkv, qkv, qkv, gbc, bbc, states, do)


def _gdn_onorm_fwd(o, proj, norm_g, *, name):
    t = o.shape[0]
    w = GDN_KEY_DIM
    goff = 3 * GDN_KEY_DIM // w

    def body(o_ref, gp_ref, g_ref, y_ref):
        gv = g_ref[...]
        for h in range(GDN_HEADS):
            sl = slice(h * GDN_HEAD_DIM, (h + 1) * GDN_HEAD_DIM)
            oh = o_ref[:, sl]
            gp = gp_ref[:, sl]
            r = lax.rsqrt(jnp.mean(oh * oh, axis=-1, keepdims=True) + EPS)
            y_ref[:, sl] = (oh * r * gv * gp * _sigmoid(gp)).astype(y_ref.dtype)

    return pl.pallas_call(
        body, name=name, grid=(t // ROWS,),
        in_specs=[_row_spec(w), pl.BlockSpec((ROWS, w), lambda i: (i, goff)), _const_spec((1, GDN_HEAD_DIM))],
        out_specs=_row_spec(w), out_shape=jax.ShapeDtypeStruct((t, w), BF16),
        compiler_params=_params("parallel"),
    )(o, proj, norm_g)


def _gdn_onorm_bwd(o, proj, norm_g, dy, *, name):
    t = o.shape[0]
    w = GDN_KEY_DIM
    goff = 3 * GDN_KEY_DIM // w

    def body(o_ref, gp_ref, g_ref, dy_ref, do_ref, dgp_ref, st_ref):
        @pl.when(pl.program_id(0) == 0)
        def _():
            st_ref[...] = jnp.zeros_like(st_ref)

        gv = g_ref[...]
        acc = jnp.zeros((1, GDN_HEAD_DIM), F32)
        for h in range(GDN_HEADS):
            sl = slice(h * GDN_HEAD_DIM, (h + 1) * GDN_HEAD_DIM)
            oh = o_ref[:, sl]
            gp = gp_ref[:, sl]
            dyv = dy_ref[:, sl].astype(F32)
            r = lax.rsqrt(jnp.mean(oh * oh, axis=-1, keepdims=True) + EPS)
            xh = oh * r
            sg = _sigmoid(gp)
            dn = dyv * gp * sg
            dgp_ref[:, sl] = (dyv * xh * gv * sg * (1.0 + gp * (1.0 - sg))).astype(dgp_ref.dtype)
            acc = acc + jnp.sum(dn * xh, axis=0, keepdims=True)
            dxh = dn * gv
            do_ref[:, sl] = r * (dxh - xh * jnp.mean(dxh * xh, axis=-1, keepdims=True))
        st_ref[0:1, :] += acc

    return pl.pallas_call(
        body, name=name, grid=(t // ROWS,),
        in_specs=[_row_spec(w), pl.BlockSpec((ROWS, w), lambda i: (i, goff)), _const_spec((1, GDN_HEAD_DIM)),
                  _row_spec(w)],
        out_specs=(_row_spec(w), _row_spec(w), _const_spec((8, GDN_HEAD_DIM))),
        out_shape=(jax.ShapeDtypeStruct((t, w), F32), jax.ShapeDtypeStruct((t, w), BF16),
                   jax.ShapeDtypeStruct((8, GDN_HEAD_DIM), F32)),
        compiler_params=_params("arbitrary"),
    )(o, proj, norm_g, dy)


def _mla_prep_fwd(proj, qg, kvg, *, name):
    t = proj.shape[0]
    q1, k1 = MLA_Q_RANK, MLA_Q_RANK + MLA_KV_RANK

    def body(p_ref, qg_ref, kg_ref, cq_ref, ck_ref):
        cq = p_ref[:, 0:q1]
        ck = p_ref[:, q1:k1]
        cq_ref[...] = (cq * lax.rsqrt(jnp.mean(cq * cq, axis=-1, keepdims=True) + EPS) * qg_ref[...]).astype(BF16)
        ck_ref[...] = (ck * lax.rsqrt(jnp.mean(ck * ck, axis=-1, keepdims=True) + EPS) * kg_ref[...]).astype(BF16)

    return pl.pallas_call(
        body, name=name, grid=(t // ROWS,),
        in_specs=[_row_spec(MLA_IN), _const_spec((1, MLA_Q_RANK)), _const_spec((1, MLA_KV_RANK))],
        out_specs=(_row_spec(MLA_Q_RANK), _row_spec(MLA_KV_RANK)),
        out_shape=(jax.ShapeDtypeStruct((t, MLA_Q_RANK), BF16), jax.ShapeDtypeStruct((t, MLA_KV_RANK), BF16)),
        compiler_params=_params("parallel"),
    )(proj, qg, kvg)


def _mla_prep_bwd(proj, qg, kvg, dcq, dck, dkr, *, name):
    t = proj.shape[0]
    q1, k1 = MLA_Q_RANK, MLA_Q_RANK + MLA_KV_RANK

    def body(p_ref, qg_ref, kg_ref, dq_ref, dk_ref, dr_ref, dp_ref, st_ref):
        @pl.when(pl.program_id(0) == 0)
        def _():
            st_ref[...] = jnp.zeros_like(st_ref)

        for lo, hi, g_ref, d_ref in ((0, q1, qg_ref, dq_ref), (q1, k1, kg_ref, dk_ref)):
            xv = p_ref[:, lo:hi]
            dn = d_ref[...]
            r = lax.rsqrt(jnp.mean(xv * xv, axis=-1, keepdims=True) + EPS)
            xh = xv * r
            dxh = dn * g_ref[...]
            dp_ref[:, lo:hi] = (r * (dxh - xh * jnp.mean(dxh * xh, axis=-1, keepdims=True))).astype(dp_ref.dtype)
            st_ref[0:1, lo:hi] += jnp.sum(dn * xh, axis=0, keepdims=True)
        dp_ref[:, k1:MLA_IN] = dr_ref[:, 0:MLA_ROPE].astype(dp_ref.dtype)

    return pl.pallas_call(
        body, name=name, grid=(t // ROWS,),
        in_specs=[_row_spec(MLA_IN), _const_spec((1, MLA_Q_RANK)), _const_spec((1, MLA_KV_RANK)),
                  _row_spec(MLA_Q_RANK), _row_spec(MLA_KV_RANK), _row_spec(LANES)],
        out_specs=(_row_spec(MLA_IN), _const_spec((8, MLA_IN))),
        out_shape=(jax.ShapeDtypeStruct((t, MLA_IN), BF16), jax.ShapeDtypeStruct((8, MLA_IN), F32)),
        compiler_params=_params("arbitrary"),
    )(proj, qg, kvg, dcq, dck, dkr)


def _rope(xr, cos_t, sin_t, *, name):
    t, w = xr.shape
    ns = w // LANES

    def body(x_ref, c_ref, s_ref, o_ref):
        cv, sv = c_ref[...], s_ref[...]
        lane = lax.broadcasted_iota(jnp.int32, (ROWS, LANES), 1)
        first = (lane % MLA_ROPE) < (MLA_ROPE // 2)
        for i in range(ns):
            sl = slice(i * LANES, (i + 1) * LANES)
            xv = x_ref[:, sl]
            sw = jnp.where(first, pltpu.roll(xv, LANES - MLA_ROPE // 2, 1), pltpu.roll(xv, MLA_ROPE // 2, 1))
            o_ref[:, sl] = xv * cv + sw * sv

    return pl.pallas_call(
        body, name=name, grid=(t // ROWS,),
        in_specs=[_row_spec(w), _row_spec(LANES), _row_spec(LANES)], out_specs=_row_spec(w),
        out_shape=jax.ShapeDtypeStruct((t, w), F32), compiler_params=_params("parallel"),
    )(xr, cos_t, sin_t)


def _rope_bwd(dr, cos_t, sin_t, *, name):
    t, w = dr.shape
    ns = w // LANES

    def body(d_ref, c_ref, s_ref, o_ref):
        cv, sv = c_ref[...], s_ref[...]
        lane = lax.broadcasted_iota(jnp.int32, (ROWS, LANES), 1)
        first = (lane % MLA_ROPE) < (MLA_ROPE // 2)
        for i in range(ns):
            sl = slice(i * LANES, (i + 1) * LANES)
            dv = d_ref[:, sl]
            ds = dv * sv
            sw = jnp.where(first, pltpu.roll(ds, LANES - MLA_ROPE // 2, 1), pltpu.roll(ds, MLA_ROPE // 2, 1))
            o_ref[:, sl] = dv * cv + sw

    return pl.pallas_call(
        body, name=name, grid=(t // ROWS,),
        in_specs=[_row_spec(w), _row_spec(LANES), _row_spec(LANES)], out_specs=_row_spec(w),
        out_shape=jax.ShapeDtypeStruct((t, w), F32), compiler_params=_params("parallel"),
    )(dr, cos_t, sin_t)


ATT_BLOCK = 256
ATT_HEAD_BATCH = 4
ATT_HEAD_BATCH_BWD = 2
ATT_SCALE = MLA_QK ** -0.5


def _causal_mask(i, j, blk):
    rows = i * blk + lax.broadcasted_iota(jnp.int32, (blk, blk), 0)
    cols = j * blk + lax.broadcasted_iota(jnp.int32, (blk, blk), 1)
    return cols <= rows


def _attn_fwd(q, k, v, *, name):
    nh, t, dk = q.shape
    dv = v.shape[-1]
    blk = min(ATT_BLOCK, t)

    hb = ATT_HEAD_BATCH
    hs = range(hb)

    def body(q_ref, k_ref, v_ref, o_ref, l_ref):
        i = pl.program_id(1)
        qv = [q_ref[h] for h in hs]

        def step(j, carry):
            m, l, acc = carry[:hb], carry[hb:2 * hb], carry[2 * hb:]
            off = pl.multiple_of(j * blk, blk)
            mask = _causal_mask(i, j, blk)
            s = [_dotb(qv[h], k_ref[h, pl.ds(off, blk), :], NT) for h in hs]
            s = [jnp.where(mask, s[h] * ATT_SCALE, NEG) for h in hs]
            m_new = [jnp.maximum(m[h], jnp.max(s[h], axis=-1, keepdims=True)) for h in hs]
            p = [jnp.exp(s[h] - m_new[h]) for h in hs]
            pv = [_dotb(p[h], v_ref[h, pl.ds(off, blk), :], NN) for h in hs]
            alpha = [jnp.exp(m[h] - m_new[h]) for h in hs]
            l = [alpha[h] * l[h] + jnp.sum(p[h], axis=-1, keepdims=True) for h in hs]
            acc = [alpha[h] * acc[h] + pv[h] for h in hs]
            return tuple(m_new) + tuple(l) + tuple(acc)

        init = ((jnp.full((blk, 1), NEG, F32),) * hb + (jnp.zeros((blk, 1), F32),) * hb
                + (jnp.zeros((blk, dv), F32),) * hb)
        out = lax.fori_loop(0, i + 1, step, init)
        for h in hs:
            m, l, acc = out[h], out[hb + h], out[2 * hb + h]
            o_ref[h] = acc / l
            l_ref[h] = jnp.broadcast_to(m + jnp.log(l), (blk, LANES))

    return pl.pallas_call(
        body, name=name, grid=(nh // hb, t // blk),
        in_specs=[pl.BlockSpec((hb, blk, dk), lambda h, i: (h, i, 0)), pl.BlockSpec((hb, t, dk), lambda h, i: (h, 0, 0)),
                  pl.BlockSpec((hb, t, dv), lambda h, i: (h, 0, 0))],
        out_specs=(pl.BlockSpec((hb, blk, dv), lambda h, i: (h, i, 0)),
                   pl.BlockSpec((hb, blk, LANES), lambda h, i: (h, i, 0))),
        out_shape=(jax.ShapeDtypeStruct((nh, t, dv), F32), jax.ShapeDtypeStruct((nh, t, LANES), F32)),
        compiler_params=_params("parallel", "parallel"),
    )(q, k, v)


def _attn_bwd(q, k, v, o, lse, do, *, name):
    nh, t, dk = q.shape
    dv = v.shape[-1]
    blk = min(ATT_BLOCK, t)
    nb = t // blk

    hb = ATT_HEAD_BATCH_BWD
    hs = range(hb)

    def body(q_ref, k_ref, v_ref, o_ref, l_ref, do_ref, dq_ref, dk_ref, dv_ref):
        j = pl.program_id(1)

        @pl.when(j == 0)
        def _():
            dq_ref[...] = jnp.zeros_like(dq_ref)

        kv = [k_ref[h] for h in hs]
        vv = [v_ref[h] for h in hs]

        def step(i, carry):
            dk_acc, dv_acc = carry[:hb], carry[hb:]
            off = pl.multiple_of(i * blk, blk)
            rows = pl.ds(off, blk)
            mask = _causal_mask(i, j, blk)
            qv = [q_ref[h, rows, :] for h in hs]
            dov = [do_ref[h, rows, :] for h in hs]
            s = [_dotb(qv[h], kv[h], NT) for h in hs]
            dp = [_dotb(dov[h], vv[h], NT) for h in hs]
            p = [jnp.exp(jnp.where(mask, s[h] * ATT_SCALE, NEG) - l_ref[h, rows, :][:, 0:1]) for h in hs]
            delta = [jnp.sum(dov[h] * o_ref[h, rows, :], axis=-1, keepdims=True) for h in hs]
            ds = [p[h] * (dp[h] - delta[h]) * ATT_SCALE for h in hs]
            dvn = [_dotb(p[h], dov[h], TN) for h in hs]
            dkn = [_dotb(ds[h], qv[h], TN) for h in hs]
            dqn = [_dotb(ds[h], kv[h], NN) for h in hs]
            for h in hs:
                dq_ref[h, rows, :] += dqn[h]
            return tuple(dk_acc[h] + dkn[h] for h in hs) + tuple(dv_acc[h] + dvn[h] for h in hs)

        out = lax.fori_loop(j, nb, step, (jnp.zeros((blk, dk), F32),) * hb + (jnp.zeros((blk, dv), F32),) * hb)
        for h in hs:
            dk_ref[h] = out[h]
            dv_ref[h] = out[hb + h]

    full = lambda w: pl.BlockSpec((hb, t, w), lambda h, j: (h, 0, 0))
    part = lambda w: pl.BlockSpec((hb, blk, w), lambda h, j: (h, j, 0))
    return pl.pallas_call(
        body, name=name, grid=(nh // hb, nb),
        in_specs=[full(dk), part(dk), part(dv), full(dv), full(LANES), full(dv)],
        out_specs=(full(dk), part(dk), part(dv)),
        out_shape=(jax.ShapeDtypeStruct((nh, t, dk), F32), jax.ShapeDtypeStruct((nh, t, dk), F32),
                   jax.ShapeDtypeStruct((nh, t, dv), F32)),
        compiler_params=_params("parallel", "arbitrary"),
    )(q, k, v, o, lse, do)


def _swap_halves(xv, first):
    return jnp.where(first, pltpu.roll(xv, LANES - MLA_ROPE // 2, 1), pltpu.roll(xv, MLA_ROPE // 2, 1))


def _rope_qk(qf, proj, cos_t, sin_t, *, name):
    t = qf.shape[0]
    nrope = MLA_HEADS * MLA_ROPE
    q_blk = MLA_HEADS * MLA_NOPE // nrope
    k_blk = (MLA_Q_RANK + MLA_KV_RANK) // LANES

    def body(q_ref, p_ref, c_ref, s_ref, qo_ref, ko_ref):
        cv, sv = c_ref[...], s_ref[...]
        lane = lax.broadcasted_iota(jnp.int32, (ROWS, LANES), 1)
        first = (lane % MLA_ROPE) < (MLA_ROPE // 2)
        for i in range(nrope // LANES):
            sl = slice(i * LANES, (i + 1) * LANES)
            xv = q_ref[:, sl]
            qo_ref[:, sl] = (xv * cv + _swap_halves(xv, first) * sv).astype(qo_ref.dtype)
        kv = jnp.where(lane < MLA_ROPE, p_ref[...], 0.0)
        ko_ref[...] = (kv * cv + _swap_halves(kv, first) * sv).astype(ko_ref.dtype)

    return pl.pallas_call(
        body, name=name, grid=(t // ROWS,),
        in_specs=[pl.BlockSpec((ROWS, nrope), lambda i: (i, q_blk)), pl.BlockSpec((ROWS, LANES), lambda i: (i, k_blk)),
                  _row_spec(LANES), _row_spec(LANES)],
        out_specs=(_row_spec(nrope), _row_spec(LANES)),
        out_shape=(jax.ShapeDtypeStruct((t, nrope), BF16), jax.ShapeDtypeStruct((t, LANES), BF16)),
        compiler_params=_params("parallel"),
    )(qf, proj, cos_t, sin_t)


def _rope_qk_bwd(dqr, dkr_parts, cos_t, sin_t, *, name):
    t, nrope = dqr.shape
    ng = dkr_parts.shape[0]

    def body(d_ref, k_ref, c_ref, s_ref, qo_ref, ko_ref):
        cv, sv = c_ref[...], s_ref[...]
        lane = lax.broadcasted_iota(jnp.int32, (ROWS, LANES), 1)
        first = (lane % MLA_ROPE) < (MLA_ROPE // 2)
        for i in range(nrope // LANES):
            sl = slice(i * LANES, (i + 1) * LANES)
            dv = d_ref[:, sl]
            qo_ref[:, sl] = (dv * cv + _swap_halves(dv * sv, first)).astype(qo_ref.dtype)
        dk = k_ref[0]
        for g in range(1, ng):
            dk = dk + k_ref[g]
        dk = jnp.where(lane < MLA_ROPE, dk, 0.0)
        ko_ref[...] = jnp.where(lane < MLA_ROPE, dk * cv + _swap_halves(dk * sv, first), 0.0)

    return pl.pallas_call(
        body, name=name, grid=(t // ROWS,),
        in_specs=[_row_spec(nrope), pl.BlockSpec((ng, ROWS, LANES), lambda i: (0, i, 0)), _row_spec(LANES),
                  _row_spec(LANES)],
        out_specs=(_row_spec(nrope), _row_spec(LANES)),
        out_shape=(jax.ShapeDtypeStruct((t, nrope), BF16), jax.ShapeDtypeStruct((t, LANES), F32)),
        compiler_params=_params("parallel"),
    )(dqr, dkr_parts, cos_t, sin_t)


def _attn_tm_fwd(qf, qr, kvf, kr, *, name):
    t = qf.shape[0]
    nh, dn, dr, dv = MLA_HEADS, MLA_NOPE, MLA_ROPE, MLA_V
    blk = min(ATT_BLOCK, t)
    hb = ATT_HEAD_BATCH
    hs = range(hb)

    def body(q_ref, qr_ref, kv_ref, kr_ref, o_ref, l_ref):
        i = pl.program_id(1)
        qn = [q_ref[:, h * dn:(h + 1) * dn].astype(MXU_DTYPE) for h in hs]
        qrh = [qr_ref[:, h * dr:(h + 1) * dr] for h in hs]

        def step(j, carry):
            m, l, acc = carry[:hb], carry[hb:2 * hb], carry[2 * hb:]
            rows = pl.ds(pl.multiple_of(j * blk, blk), blk)
            mask = _causal_mask(i, j, blk)
            krj = kr_ref[rows, 0:dr]
            s = [_dotb(qn[h], kv_ref[rows, h * (dn + dv):h * (dn + dv) + dn], NT) for h in hs]
            sr = [_dotb(qrh[h], krj, NT) for h in hs]
            s = [jnp.where(mask, (s[h] + sr[h]) * ATT_SCALE, NEG) for h in hs]
            m_new = [jnp.maximum(m[h], jnp.max(s[h], axis=-1, keepdims=True)) for h in hs]
            p = [jnp.exp(s[h] - m_new[h]) for h in hs]
            pv = [_dotb(p[h], kv_ref[rows, h * (dn + dv) + dn:(h + 1) * (dn + dv)], NN) for h in hs]
            alpha = [jnp.exp(m[h] - m_new[h]) for h in hs]
            l = [alpha[h] * l[h] + jnp.sum(p[h], axis=-1, keepdims=True) for h in hs]
            acc = [alpha[h] * acc[h] + pv[h] for h in hs]
            return tuple(m_new) + tuple(l) + tuple(acc)

        init = ((jnp.full((blk, 1), NEG, F32),) * hb + (jnp.zeros((blk, 1), F32),) * hb
                + (jnp.zeros((blk, dv), F32),) * hb)
        out = lax.fori_loop(0, i + 1, step, init)
        for h in hs:
            m, l, acc = out[h], out[hb + h], out[2 * hb + h]
            o_ref[:, h * dv:(h + 1) * dv] = (acc / l).astype(o_ref.dtype)
            l_ref[h] = jnp.broadcast_to(m + jnp.log(l), (blk, LANES))

    return pl.pallas_call(
        body, name=name, grid=(nh // hb, t // blk),
        in_specs=[pl.BlockSpec((blk, hb * dn), lambda g, i: (i, g)), pl.BlockSpec((blk, hb * dr), lambda g, i: (i, g)),
                  pl.BlockSpec((t, hb * (dn + dv)), lambda g, i: (0, g)), pl.BlockSpec((t, LANES), lambda g, i: (0, 0))],
        out_specs=(pl.BlockSpec((blk, hb * dv), lambda g, i: (i, g)),
                   pl.BlockSpec((hb, blk, LANES), lambda g, i: (g, i, 0))),
        out_shape=(jax.ShapeDtypeStruct((t, nh * dv), BF16), jax.ShapeDtypeStruct((nh, t, LANES), F32)),
        compiler_params=_params("parallel", "parallel"),
    )(qf, qr, kvf, kr)


def _attn_tm_bwd(qf, qr, kvf, kr, o, lse, do, *, name):
    t = qf.shape[0]
    nh, dn, dr, dv = MLA_HEADS, MLA_NOPE, MLA_ROPE, MLA_V
    blk = min(ATT_BLOCK, t)
    nb = t // blk
    hb = ATT_HEAD_BATCH_BWD
    hs = range(hb)
    ng = nh // hb

    def body(q_ref, qr_ref, kv_ref, kr_ref, o_ref, l_ref, do_ref, dqn_ref, dqr_ref, dkv_ref, dkr_ref):
        j = pl.program_id(1)

        @pl.when(j == 0)
        def _():
            dqn_ref[...] = jnp.zeros_like(dqn_ref)
            dqr_ref[...] = jnp.zeros_like(dqr_ref)

        kn = [kv_ref[:, h * (dn + dv):h * (dn + dv) + dn] for h in hs]
        vv = [kv_ref[:, h * (dn + dv) + dn:(h + 1) * (dn + dv)] for h in hs]
        krj = kr_ref[:, 0:dr]

        def step(i, carry):
            dkn_acc, dv_acc, dkr_acc = carry[:hb], carry[hb:2 * hb], carry[2 * hb]
            rows = pl.ds(pl.multiple_of(i * blk, blk), blk)
            mask = _causal_mask(i, j, blk)
            qn = [q_ref[rows, h * dn:(h + 1) * dn].astype(MXU_DTYPE) for h in hs]
            qrh = [qr_ref[rows, h * dr:(h + 1) * dr] for h in hs]
            dov = [do_ref[rows, h * dv:(h + 1) * dv] for h in hs]
            s = [_dotb(qn[h], kn[h], NT) for h in hs]
            sr = [_dotb(qrh[h], krj, NT) for h in hs]
            dp = [_dotb(dov[h], vv[h], NT) for h in hs]
            p = [jnp.exp(jnp.where(mask, (s[h] + sr[h]) * ATT_SCALE, NEG) - l_ref[h, rows, :][:, 0:1]) for h in hs]
            delta = [jnp.sum(dov[h].astype(F32) * o_ref[rows, h * dv:(h + 1) * dv].astype(F32), axis=-1, keepdims=True)
                     for h in hs]
            ds = [p[h] * (dp[h] - delta[h]) * ATT_SCALE for h in hs]
            dvn = [_dotb(p[h], dov[h], TN) for h in hs]
            dknn = [_dotb(ds[h], qn[h], TN) for h in hs]
            dkrn = [_dotb(ds[h], qrh[h], TN) for h in hs]
            dqnn = [_dotb(ds[h], kn[h], NN) for h in hs]
            dqrn = [_dotb(ds[h], krj, NN) for h in hs]
            for h in hs:
                dqn_ref[rows, h * dn:(h + 1) * dn] += dqnn[h]
                dqr_ref[rows, h * dr:(h + 1) * dr] += dqrn[h]
            dkr_new = dkr_acc
            for h in hs:
                dkr_new = dkr_new + dkrn[h]
            return (tuple(dkn_acc[h] + dknn[h] for h in hs) + tuple(dv_acc[h] + dvn[h] for h in hs) + (dkr_new,))

        init = (jnp.zeros((blk, dn), F32),) * hb + (jnp.zeros((blk, dv), F32),) * hb + (jnp.zeros((blk, dr), F32),)
        out = lax.fori_loop(j, nb, step, init)
        for h in hs:
            dkv_ref[:, h * (dn + dv):h * (dn + dv) + dn] = out[h].astype(dkv_ref.dtype)
            dkv_ref[:, h * (dn + dv) + dn:(h + 1) * (dn + dv)] = out[hb + h].astype(dkv_ref.dtype)
        dkr_ref[0, :, 0:dr] = out[2 * hb]
        dkr_ref[0, :, dr:LANES] = jnp.zeros((blk, LANES - dr), F32)

    full = lambda w: pl.BlockSpec((t, w), lambda g, j: (0, g))
    return pl.pallas_call(
        body, name=name, grid=(ng, nb),
        in_specs=[full(hb * dn), full(hb * dr), pl.BlockSpec((blk, hb * (dn + dv)), lambda g, j: (j, g)),
                  pl.BlockSpec((blk, LANES), lambda g, j: (j, 0)), full(hb * dv),
                  pl.BlockSpec((hb, t, LANES), lambda g, j: (g, 0, 0)), full(hb * dv)],
        out_specs=(full(hb * dn), full(hb * dr), pl.BlockSpec((blk, hb * (dn + dv)), lambda g, j: (j, g)),
                   pl.BlockSpec((1, blk, LANES), lambda g, j: (g, j, 0))),
        out_shape=(jax.ShapeDtypeStruct((t, nh * dn), F32), jax.ShapeDtypeStruct((t, nh * dr), F32),
                   jax.ShapeDtypeStruct((t, nh * (dn + dv)), BF16), jax.ShapeDtypeStruct((ng, t, LANES), F32)),
        compiler_params=_params("parallel", "arbitrary"),
    )(qf, qr, kvf, kr, o, lse, do)


def _ada_mod(c_all, ada_w, ada_b_cols, *, name):
    nl, d, wc = ada_w.shape

    def body(c_ref, w_ref, b_ref, o_ref):
        cv = c_ref[...]
        o_ref[0] = _dotb(cv * _sigmoid(cv), w_ref[0], NN) + b_ref[0]

    return pl.pallas_call(
        body, name=name, grid=(nl,),
        in_specs=[_const_spec((N_DEV, d)), pl.BlockSpec((1, d, wc), lambda l: (l, 0, 0)),
                  pl.BlockSpec((1, 1, wc), lambda l: (l, 0, 0))],
        out_specs=pl.BlockSpec((1, N_DEV, wc), lambda l: (l, 0, 0)),
        out_shape=jax.ShapeDtypeStruct((nl, N_DEV, wc), F32), compiler_params=_params("parallel"),
    )(c_all, ada_w, ada_b_cols)


def _adam_math(g, w, m, v):
    m2 = ADAM_B1 * m + (1.0 - ADAM_B1) * g
    v2 = ADAM_B2 * v + (1.0 - ADAM_B2) * (g * g)
    delta = -ADAM_LR * ((m2 / ADAM_BC1) / (jnp.sqrt(v2 / ADAM_BC2) + ADAM_EPS) + ADAM_WD * w)
    return delta, m2, v2


def _ada_grad_adamw(c_all, dmod_cols, w, m, v, *, name):
    nl, d, wc = w.shape
    tr = 256

    def body(c_ref, dm_ref, w_ref, m_ref, v_ref, g_ref, d_ref, m2_ref, v2_ref):
        cv = c_ref[...]
        g = _dotf(cv * _sigmoid(cv), dm_ref[0], TN)
        delta, m2, v2 = _adam_math(g, w_ref[0], m_ref[0], v_ref[0])
        g_ref[0], d_ref[0], m2_ref[0], v2_ref[0] = g, delta, m2, v2

    blk = pl.BlockSpec((1, tr, wc), lambda l, i: (l, i, 0))
    return pl.pallas_call(
        body, name=name, grid=(nl, d // tr),
        in_specs=[pl.BlockSpec((N_DEV, tr), lambda l, i: (0, i)), pl.BlockSpec((1, N_DEV, wc), lambda l, i: (l, 0, 0)),
                  blk, blk, blk],
        out_specs=(blk,) * 4, out_shape=(jax.ShapeDtypeStruct(w.shape, F32),) * 4,
        compiler_params=_params("parallel", "parallel"),
    )(c_all, dmod_cols, w, m, v)


def _adamw(parts, w, m, v, *, name):
    nl, r, c = w.shape
    ns = parts[0].shape[0]
    lanes_padded = -(-c // LANES) * LANES
    row_bytes = 2 * nl * ns * lanes_padded * parts[0].dtype.itemsize
    tr = _pick(r, min(256, max(16, (VMEM_LIMIT // 2) // row_bytes)), 16)
    tc = c
    if tr * row_bytes > VMEM_LIMIT // 2:
        tc = _pick(c, max(LANES, c * (VMEM_LIMIT // 2) // (tr * row_bytes)))

    def body(*refs):
        p_refs = refs[:nl]
        w_ref, m_ref, v_ref, g_ref, d_ref, m2_ref, v2_ref = refs[nl:]
        layer = pl.program_id(0)
        for q in range(nl):
            @pl.when(layer == q)
            def _(q=q):
                g = p_refs[q][0].astype(F32)
                for s in range(1, ns):
                    g = g + p_refs[q][s].astype(F32)
                delta, m2, v2 = _adam_math(g, w_ref[0], m_ref[0], v_ref[0])
                g_ref[0], d_ref[0], m2_ref[0], v2_ref[0] = g, delta, m2, v2

    blk = pl.BlockSpec((1, tr, tc), lambda l, i, j: (l, i, j))
    p_specs = [pl.BlockSpec((ns, tr, tc), lambda l, i, j, q=q: (0, jnp.where(l == q, i, 0), jnp.where(l == q, j, 0)))
               for q in range(nl)]
    return pl.pallas_call(
        body, name=name, grid=(nl, r // tr, c // tc),
        in_specs=p_specs + [blk, blk, blk],
        out_specs=(blk,) * 4, out_shape=(jax.ShapeDtypeStruct(w.shape, F32),) * 4,
        compiler_params=_params("arbitrary", "arbitrary", "arbitrary"),
    )(*parts, w, m, v)


def _sum_parts(parts, *, name):
    ns, r, c = parts.shape

    def body(p_ref, o_ref):
        acc = p_ref[0]
        for s in range(1, ns):
            acc = acc + p_ref[s]
        o_ref[...] = acc

    return pl.pallas_call(
        body, name=name, out_shape=jax.ShapeDtypeStruct((r, c), F32),
        in_specs=[pl.BlockSpec(memory_space=pltpu.VMEM)], out_specs=pl.BlockSpec(memory_space=pltpu.VMEM),
    )(parts)


def _pack(arrs):
    flat = jnp.concatenate([a.reshape(-1).astype(F32) for a in arrs])
    pad = (-flat.shape[0]) % (8 * LANES)
    return jnp.pad(flat, (0, pad)).reshape(-1, LANES)


def _unpack(packed, shapes, lead=()):
    flat = packed.reshape(lead + (-1,))
    out, off = [], 0
    for s in shapes:
        n = math.prod(s)
        out.append(flat[..., off:off + n].reshape(lead + tuple(s)))
        off += n
    return out


def _gather_cols(g):
    _, nl, r, cs = g.shape
    return jnp.transpose(g, (1, 2, 0, 3)).reshape(nl, r, N_DEV * cs)


def _gather_rows(g):
    _, nl, rs, c = g.shape
    return jnp.transpose(g, (1, 0, 2, 3)).reshape(nl, N_DEV * rs, c)


def _scatter_cols(full):
    nl, r, c = full.shape
    return jnp.transpose(full.reshape(nl, r, N_DEV, c // N_DEV), (2, 0, 1, 3))


def _scatter_rows(full):
    nl, r, c = full.shape
    return jnp.transpose(full.reshape(nl, N_DEV, r // N_DEV, c), (1, 0, 2, 3))


def _row(v):
    return v.reshape(1, -1)


def _local_step(x, target, mod, cos_t, sin_t, rep, get_weights, put_grads):
    t = x.shape[0]
    saved = []
    for layer in range(DEPTH):
        j = layer // 2
        tag = f"l{layer}"
        shift_m, scale_m, gate_m, shift_f, scale_f, gate_f = [_row(mod[layer, i]) for i in range(N_MOD)]
        lw = dict(get_weights(layer, "mix", x))
        rec = {"x0": x, "lw": lw}
        h = _adaln_fwd(x, _row(rep["norm_mix_g"][layer]), scale_m, shift_m, name=f"adaln_mix_{tag}")
        rec["h"] = h
        if layer % 2 == 0:
            proj = _mm(h, lw["wt_in"], mode="nt", out_dtype=F32, tm=256, tn=GDN_MAIN, b_rows=GDN_MAIN,
                       dep=lw["dep_mix"], name=f"gdn_in_{tag}")
            ab = _mm(h, lw["wt_ab"], mode="nt", out_dtype=F32, name=f"gdn_in_ab_{tag}")
            qkv = _gdn_prep_fwd(proj, rep["gdn_conv_wt"][j], name=f"gdn_prep_{tag}")
            gbeta = _gdn_gate_fwd(ab, rep["gdn_gate_prm"][j], name=f"gdn_gate_{tag}")
            gbc = jnp.broadcast_to(jnp.transpose(gbeta[:, 0:GDN_HEADS])[:, :, None], (GDN_HEADS, t, LANES))
            bbc = jnp.broadcast_to(jnp.transpose(gbeta[:, GDN_HEADS:2 * GDN_HEADS])[:, :, None],
                                   (GDN_HEADS, t, LANES))
            o, states = _gdn_chunk_fwd(qkv, gbc, bbc, name=f"gdn_chunk_{tag}")
            og = _gdn_onorm_fwd(o, proj, _row(rep["gdn_norm_g"][j]), name=f"gdn_onorm_{tag}")
            x, y = _mm_resid(og, lw["w_out"], x, gate_m, name=f"gdn_out_{tag}")
            rec.update(proj=proj, ab=ab, qkv=qkv, gbc=gbc, bbc=bbc, states=states, o=o, og=og, y=y)
        else:
            proj = _mm(h, lw["w_in"], mode="nn", out_dtype=F32, dep=lw["dep_mix"], name=f"mla_in_{tag}")
            cq, ck = _mla_prep_fwd(proj, _row(rep["mla_q_norm_g"][j]), _row(rep["mla_kv_norm_g"][j]),
                                   name=f"mla_prep_{tag}")
            qf = _mm(cq, lw["wt_uq"], mode="nt", out_dtype=F32, name=f"mla_uq_{tag}")
            kvf = _mm(ck, lw["w_ukv"], mode="nn", out_dtype=BF16, name=f"mla_ukv_{tag}")
            qr, kr = _rope_qk(qf, proj, cos_t, sin_t, name=f"rope_{tag}")
            oc, lse = _attn_tm_fwd(qf, qr, kvf, kr, name=f"attn_{tag}")
            x, y = _mm_resid(oc, lw["w_out"], x, gate_m, name=f"mla_out_{tag}")
            rec.update(proj=proj, cq=cq, ck=ck, qf=qf, qr=qr, kvf=kvf, kr=kr, lse=lse, oc=oc, y=y)
        rec["x1"] = x
        lw.update(get_weights(layer, "ffn", x))
        h2 = _adaln_fwd(x, _row(rep["norm_ffn_g"][layer]), scale_f, shift_f, name=f"adaln_ffn_{tag}")
        s, a2, b2 = _ffn_gu_fwd(h2, lw["wt_g"], lw["wt_u"], lw["dep_ffn"], name=f"ffn_gu_{tag}")
        x, y2 = _mm_resid(s, lw["w_down"], x, gate_f, name=f"ffn_down_{tag}")
        rec.update(h2=h2, a2=a2, b2=b2, s=s, y2=y2)
        saved.append(rec)

    dx, st, ls = _loss_head(x, _row(rep["final_norm_g"]), target, name="loss_head")
    loss = ls[0, 0]
    grads = {"final_norm_g": st[0]}
    per_layer = {k: [None] * DEPTH for k in ("norm_mix_g", "norm_ffn_g")}
    per_gdn = {k: [None] * 2 for k in ("gdn_conv_wt", "gdn_a_log", "gdn_dt_bias", "gdn_norm_g")}
    per_mla = {k: [None] * 2 for k in ("mla_q_norm_g", "mla_kv_norm_g")}
    dmod = [None] * DEPTH
    dep = jnp.zeros((8, LANES), F32)

    for layer in reversed(range(DEPTH)):
        j = layer // 2
        tag = f"l{layer}"
        rec = saved[layer]
        lw = rec["lw"]
        shift_m, scale_m, gate_m, shift_f, scale_f, gate_f = [_row(mod[layer, i]) for i in range(N_MOD)]
        dy2, st_g = _gate_bwd(dx, rec["y2"], gate_f, dep, name=f"gate_bwd_ffn_{tag}")
        dgate_f = st_g[0]
        dw_down = _mm(rec["s"], dy2, mode="tn", out_dtype=BF16, tm=256, tn=1024, name=f"ffn_down_dw_{tag}")
        da2, db2 = _ffn_down_dx(dy2, lw["w_down"], rec["a2"], rec["b2"], name=f"ffn_down_dx_{tag}")
        dwt_g = _mm(da2, rec["h2"], mode="tn", out_dtype=BF16, tm=256, tn=1024, name=f"ffn_g_dw_{tag}")
        dwt_u = _mm(db2, rec["h2"], mode="tn", out_dtype=BF16, tm=256, tn=1024, name=f"ffn_u_dw_{tag}")
        dep = put_grads(layer, "ffn", {"wt_g": dwt_g, "wt_u": dwt_u, "w_down": dw_down})
        dh2 = _mm(da2, lw["wt_g"], mode="nn", out_dtype=F32, tm=256, tn=1024, name=f"ffn_g_dx_{tag}")
        dh2 = _mm(db2, lw["wt_u"], mode="nn", out_dtype=BF16, add=dh2, tm=256, tn=1024, name=f"ffn_u_dx_{tag}")
        dx, st_n = _adaln_bwd(rec["x1"], _row(rep["norm_ffn_g"][layer]), scale_f, shift_f, dh2, dx, dep,
                              name=f"adaln_ffn_bwd_{tag}")
        per_layer["norm_ffn_g"][layer] = st_n[0]
        dscale_f, dshift_f = st_n[1], st_n[2]
        dy, st_g = _gate_bwd(dx, rec["y"], gate_m, dep, name=f"gate_bwd_mix_{tag}")
        dgate_m = st_g[0]
        big = {}
        if layer % 2 == 0:
            big["w_out"] = _mm(rec["og"], dy, mode="tn", out_dtype=BF16, name=f"gdn_out_dw_{tag}")
            dog = _mm(dy, lw["w_out"], mode="nt", out_dtype=BF16, name=f"gdn_out_dx_{tag}")
            do, dgp, st_o = _gdn_onorm_bwd(rec["o"], rec["proj"], _row(rep["gdn_norm_g"][j]), dog,
                                           name=f"gdn_onorm_bwd_{tag}")
            per_gdn["gdn_norm_g"][j] = st_o[0]
            dqkv, dgc_, dbc_ = _gdn_chunk_bwd(rec["qkv"], rec["gbc"], rec["bbc"], rec["states"], do,
                                               name=f"gdn_chunk_bwd_{tag}")
            dgb = jnp.concatenate([jnp.transpose(dgc_[:, :, 0]), jnp.transpose(dbc_[:, :, 0])], axis=1)
            dgb = jnp.pad(dgb, ((0, 0), (0, LANES - 2 * GDN_HEADS)))
            dab, st_a = _gdn_gate_bwd(rec["ab"], rep["gdn_gate_prm"][j], dgb, name=f"gdn_gate_bwd_{tag}")
            per_gdn["gdn_a_log"][j] = st_a[0, :GDN_HEADS]
            per_gdn["gdn_dt_bias"][j] = st_a[1, :GDN_HEADS]
            dpre, dcw = _gdn_prep_bwd(rec["proj"], rep["gdn_conv_wt"][j], dqkv, name=f"gdn_prep_bwd_{tag}")
            per_gdn["gdn_conv_wt"][j] = dcw
            dproj = jnp.concatenate([dpre, dgp], axis=1)
            dw_main = _mm(dproj, rec["h"], mode="tn", out_dtype=BF16, tm=512, tn=1024, name=f"gdn_in_dw_{tag}")
            dw_ab = _mm(dab, rec["h"], mode="tn", out_dtype=BF16, tn=1024, name=f"gdn_in_ab_dw_{tag}")
            big["wt_in"] = jnp.concatenate([dw_main, dw_ab[:2 * GDN_HEADS]], axis=0)
            dep = put_grads(layer, "gdn", big)
            dh_ab = _mm(dab, lw["wt_ab"], mode="nn", out_dtype=F32, tn=1024, name=f"gdn_in_ab_dx_{tag}")
            dh = _mm(dproj, lw["wt_in"], mode="nn", out_dtype=BF16, add=dh_ab, tm=256, tn=1024, b_rows=GDN_MAIN,
                     name=f"gdn_in_dx_{tag}")
        else:
            big["w_out"] = _mm(rec["oc"], dy, mode="tn", out_dtype=BF16, name=f"mla_out_dw_{tag}")
            doc = _mm(dy, lw["w_out"], mode="nt", out_dtype=BF16, name=f"mla_out_dx_{tag}")
            dqn, dqr, dkvf, dkr_parts = _attn_tm_bwd(rec["qf"], rec["qr"], rec["kvf"], rec["kr"], rec["oc"],
                                                     rec["lse"], doc, name=f"attn_bwd_{tag}")
            dqr_un, dkr_un = _rope_qk_bwd(dqr, dkr_parts, cos_t, sin_t, name=f"rope_bwd_{tag}")
            n_nope = MLA_HEADS * MLA_NOPE
            big["wt_uq"] = jnp.concatenate(
                [_mm(dqn, rec["cq"], mode="tn", out_dtype=BF16, name=f"mla_uq_dw_nope_{tag}"),
                 _mm(dqr_un, rec["cq"], mode="tn", out_dtype=BF16, name=f"mla_uq_dw_rope_{tag}")], axis=0)
            big["w_ukv"] = _mm(rec["ck"], dkvf, mode="tn", out_dtype=BF16, name=f"mla_ukv_dw_{tag}")
            dcq = _mm(dqr_un, lw["wt_uq"][n_nope:], mode="nn", out_dtype=F32, name=f"mla_uq_dx_rope_{tag}")
            dcq = _mm(dqn, lw["wt_uq"], mode="nn", out_dtype=F32, add=dcq, b_rows=n_nope,
                      name=f"mla_uq_dx_nope_{tag}")
            dck = _mm(dkvf, lw["w_ukv"], mode="nt", out_dtype=F32, name=f"mla_ukv_dx_{tag}")
            dproj, st_p = _mla_prep_bwd(rec["proj"], _row(rep["mla_q_norm_g"][j]), _row(rep["mla_kv_norm_g"][j]),
                                        dcq, dck, dkr_un, name=f"mla_prep_bwd_{tag}")
            per_mla["mla_q_norm_g"][j] = st_p[0, :MLA_Q_RANK]
            per_mla["mla_kv_norm_g"][j] = st_p[0, MLA_Q_RANK:MLA_Q_RANK + MLA_KV_RANK]
            big["w_in"] = _mm(rec["h"], dproj, mode="tn", out_dtype=BF16, name=f"mla_in_dw_{tag}")
            dep = put_grads(layer, "mla", big)
            dh = _mm(dproj, lw["w_in"], mode="nt", out_dtype=BF16, name=f"mla_in_dx_{tag}")
        dx, st_n = _adaln_bwd(rec["x0"], _row(rep["norm_mix_g"][layer]), scale_m, shift_m, dh, dx, dep,
                              name=f"adaln_mix_bwd_{tag}")
        per_layer["norm_mix_g"][layer] = st_n[0]
        dmod[layer] = jnp.stack([st_n[2], st_n[1], dgate_m, dshift_f, dscale_f, dgate_f])

    for d in (per_layer, per_gdn, per_mla):
        for k, v in d.items():
            grads[k] = jnp.stack(v)
    return loss, dx, jnp.stack(dmod), grads


BIG = ("gdn_w_in", "gdn_w_out", "mla_w_in", "mla_w_uq", "mla_w_ukv", "mla_w_out", "ffn_w_gate", "ffn_w_up",
       "ffn_w_down")
TRANSPOSED = ("gdn_w_in", "mla_w_uq", "ffn_w_gate", "ffn_w_up")
AHEAD = 2


def _view(k, a):
    return jnp.transpose(a, (0, 2, 1)) if k in TRANSPOSED else a
SMALL = ("ada_b", "norm_mix_g", "norm_ffn_g", "gdn_conv_w", "gdn_a_log", "gdn_dt_bias", "gdn_norm_g",
         "mla_q_norm_g", "mla_kv_norm_g", "final_norm_g")
WEIGHTS = ("ada_w", "ada_b", "norm_mix_g", "norm_ffn_g", "gdn_w_in", "gdn_conv_w", "gdn_a_log", "gdn_dt_bias",
           "gdn_norm_g", "gdn_w_out", "mla_w_in", "mla_q_norm_g", "mla_kv_norm_g", "mla_w_uq", "mla_w_ukv",
           "mla_w_out", "ffn_w_gate", "ffn_w_up", "ffn_w_down", "final_norm_g")


def _uq_to_kernel_layout(w, axis=-1):
    axis = axis % w.ndim
    lead, tail = w.shape[:axis], w.shape[axis + 1:]
    w4 = w.reshape(lead + (MLA_HEADS, MLA_QK) + tail)
    nope = lax.slice_in_dim(w4, 0, MLA_NOPE, axis=axis + 1).reshape(lead + (-1,) + tail)
    rope = lax.slice_in_dim(w4, MLA_NOPE, MLA_QK, axis=axis + 1).reshape(lead + (-1,) + tail)
    return jnp.concatenate([nope, rope], axis=axis)


def _uq_from_kernel_layout(w, axis=-1):
    axis = axis % w.ndim
    lead, tail = w.shape[:axis], w.shape[axis + 1:]
    nope = lax.slice_in_dim(w, 0, MLA_HEADS * MLA_NOPE, axis=axis).reshape(lead + (MLA_HEADS, MLA_NOPE) + tail)
    rope = lax.slice_in_dim(w, MLA_HEADS * MLA_NOPE, MLA_HEADS * MLA_QK, axis=axis).reshape(
        lead + (MLA_HEADS, MLA_ROPE) + tail)
    return jnp.concatenate([nope, rope], axis=axis + 1).reshape(lead + (-1,) + tail)


def _group_names(layer, kind):
    if kind == "ffn":
        return ("ffn_w_gate", "ffn_w_up", "ffn_w_down")
    return ("gdn_w_in", "gdn_w_out") if layer % 2 == 0 else ("mla_w_in", "mla_w_uq", "mla_w_ukv", "mla_w_out")


def _layer_index(name, layer):
    return layer if name.startswith("ffn") else layer // 2


def _cols(g):
    return jnp.transpose(g, (1, 0, 2)).reshape(g.shape[1], N_DEV * g.shape[2])


def _rows(g):
    return g.reshape(N_DEV * g.shape[1], g.shape[2])


def _uncols(full):
    r, c = full.shape
    return jnp.transpose(full.reshape(r, N_DEV, c // N_DEV), (1, 0, 2))


def _unrows(full):
    r, c = full.shape
    return full.reshape(N_DEV, r // N_DEV, c)


def _group_weights(layer, kind, got, token):
    if kind == "ffn":
        return {"wt_g": _rows(got["ffn_w_gate"]), "wt_u": _rows(got["ffn_w_up"]), "w_down": _rows(got["ffn_w_down"]),
                "dep_ffn": token}
    if layer % 2 == 0:
        wt_in = _rows(got["gdn_w_in"])
        return dict(wt_in=wt_in, wt_ab=jnp.pad(wt_in[GDN_MAIN:], ((0, LANES - 2 * GDN_HEADS), (0, 0))),
                    w_out=_rows(got["gdn_w_out"]), dep_mix=token)
    return dict(w_in=_rows(got["mla_w_in"]), wt_uq=_uq_to_kernel_layout(_rows(got["mla_w_uq"]), axis=0),
                w_ukv=_cols(got["mla_w_ukv"]), w_out=_rows(got["mla_w_out"]), dep_mix=token)


def _layer_grad_slots(kind, big):
    if kind == "ffn":
        return {"ffn_w_gate": _unrows(big["wt_g"]), "ffn_w_up": _unrows(big["wt_u"]),
                "ffn_w_down": _unrows(big["w_down"])}
    if kind == "gdn":
        return {"gdn_w_in": _unrows(big["wt_in"]), "gdn_w_out": _unrows(big["w_out"])}
    return {"mla_w_in": _unrows(big["w_in"]), "mla_w_uq": _unrows(_uq_from_kernel_layout(big["wt_uq"], axis=0)),
            "mla_w_ukv": _uncols(big["w_ukv"]), "mla_w_out": _unrows(big["w_out"])}


def _small_weights(tiny, rep):
    prm = jnp.zeros((2, 8, LANES), F32)
    prm = prm.at[:, 0, :GDN_HEADS].set(rep["gdn_a_log"]).at[:, 1, :GDN_HEADS].set(rep["gdn_dt_bias"])
    out = {
        "gdn_conv_wt": jnp.transpose(_gather_rows(tiny["gdn_conv_w"]), (0, 2, 1)),
        "mla_q_norm_g": jnp.transpose(tiny["mla_q_norm_g"], (1, 0, 2)).reshape(2, MLA_Q_RANK),
        "mla_kv_norm_g": jnp.transpose(tiny["mla_kv_norm_g"], (1, 0, 2)).reshape(2, MLA_KV_RANK),
        "gdn_gate_prm": prm,
    }
    for k in ("norm_mix_g", "norm_ffn_g", "gdn_norm_g", "final_norm_g"):
        out[k] = rep[k]
    return out


def _rope_tables(positions):
    inv_freq = ROPE_THETA ** (-jnp.arange(0, MLA_ROPE, 2, dtype=F32) / MLA_ROPE)
    ang = positions.astype(F32)[:, None] * inv_freq
    cos, sin = jnp.cos(ang), jnp.sin(ang)
    reps = LANES // MLA_ROPE
    return jnp.tile(jnp.concatenate([cos, cos], axis=1), (1, reps)), jnp.tile(
        jnp.concatenate([-sin, sin], axis=1), (1, reps))


def kernel(x, c, positions, ada_w, ada_b, norm_mix_g, norm_ffn_g, gdn_w_in, gdn_conv_w, gdn_a_log, gdn_dt_bias, gdn_norm_g, gdn_w_out, mla_w_in, mla_q_norm_g, mla_kv_norm_g, mla_w_uq, mla_w_ukv, mla_w_out, ffn_w_gate, ffn_w_up, ffn_w_down, final_norm_g, loss_target, m_ada_w, m_ada_b, m_norm_mix_g, m_norm_ffn_g, m_gdn_w_in, m_gdn_conv_w, m_gdn_a_log, m_gdn_dt_bias, m_gdn_norm_g, m_gdn_w_out, m_mla_w_in, m_mla_q_norm_g, m_mla_kv_norm_g, m_mla_w_uq, m_mla_w_ukv, m_mla_w_out, m_ffn_w_gate, m_ffn_w_up, m_ffn_w_down, m_final_norm_g, v_ada_w, v_ada_b, v_norm_mix_g, v_norm_ffn_g, v_gdn_w_in, v_gdn_conv_w, v_gdn_a_log, v_gdn_dt_bias, v_gdn_norm_g, v_gdn_w_out, v_mla_w_in, v_mla_q_norm_g, v_mla_kv_norm_g, v_mla_w_uq, v_mla_w_ukv, v_mla_w_out, v_ffn_w_gate, v_ffn_w_up, v_ffn_w_down, v_final_norm_g):
    W = dict(ada_w=ada_w, ada_b=ada_b, norm_mix_g=norm_mix_g, norm_ffn_g=norm_ffn_g, gdn_w_in=gdn_w_in,
             gdn_conv_w=gdn_conv_w, gdn_a_log=gdn_a_log, gdn_dt_bias=gdn_dt_bias, gdn_norm_g=gdn_norm_g,
             gdn_w_out=gdn_w_out, mla_w_in=mla_w_in, mla_q_norm_g=mla_q_norm_g, mla_kv_norm_g=mla_kv_norm_g,
             mla_w_uq=mla_w_uq, mla_w_ukv=mla_w_ukv, mla_w_out=mla_w_out, ffn_w_gate=ffn_w_gate,
             ffn_w_up=ffn_w_up, ffn_w_down=ffn_w_down, final_norm_g=final_norm_g)
    M = dict(ada_w=m_ada_w, ada_b=m_ada_b, norm_mix_g=m_norm_mix_g, norm_ffn_g=m_norm_ffn_g, gdn_w_in=m_gdn_w_in,
             gdn_conv_w=m_gdn_conv_w, gdn_a_log=m_gdn_a_log, gdn_dt_bias=m_gdn_dt_bias, gdn_norm_g=m_gdn_norm_g,
             gdn_w_out=m_gdn_w_out, mla_w_in=m_mla_w_in, mla_q_norm_g=m_mla_q_norm_g,
             mla_kv_norm_g=m_mla_kv_norm_g, mla_w_uq=m_mla_w_uq, mla_w_ukv=m_mla_w_ukv, mla_w_out=m_mla_w_out,
             ffn_w_gate=m_ffn_w_gate, ffn_w_up=m_ffn_w_up, ffn_w_down=m_ffn_w_down, final_norm_g=m_final_norm_g)
    V = dict(ada_w=v_ada_w, ada_b=v_ada_b, norm_mix_g=v_norm_mix_g, norm_ffn_g=v_norm_ffn_g, gdn_w_in=v_gdn_w_in,
             gdn_conv_w=v_gdn_conv_w, gdn_a_log=v_gdn_a_log, gdn_dt_bias=v_gdn_dt_bias, gdn_norm_g=v_gdn_norm_g,
             gdn_w_out=v_gdn_w_out, mla_w_in=v_mla_w_in, mla_q_norm_g=v_mla_q_norm_g,
             mla_kv_norm_g=v_mla_kv_norm_g, mla_w_uq=v_mla_w_uq, mla_w_ukv=v_mla_w_ukv, mla_w_out=v_mla_w_out,
             ffn_w_gate=v_ffn_w_gate, ffn_w_up=v_ffn_w_up, ffn_w_down=v_ffn_w_down, final_norm_g=v_final_norm_g)
    me = 4 * lax.axis_index("x") + 2 * lax.axis_index("y") + lax.axis_index("c")
    t = x.shape[1]
    wc = ada_w.shape[-1]

    groups = [(layer, kind) for layer in range(DEPTH) for kind in ("mix", "ffn")]

    def group_srcs(i):
        layer, kind = groups[i]
        return [_view(k, W[k])[_layer_index(k, layer)].astype(BF16) for k in _group_names(layer, kind)]

    tiny_shapes = [c.shape, gdn_conv_w.shape, mla_q_norm_g.shape, mla_kv_norm_g.shape]
    first = _gather_two_level([_pack([c, gdn_conv_w, mla_q_norm_g, mla_kv_norm_g])] + group_srcs(0),
                              name="gather_first")
    tiny_g = first[0]
    c_g, conv_g, qn_g, kvn_g = _unpack(tiny_g, tiny_shapes, lead=(N_DEV,))
    c_all = c_g.reshape(N_DEV, D_MODEL)
    rep = _small_weights({"gdn_conv_w": conv_g, "mla_q_norm_g": qn_g, "mla_kv_norm_g": kvn_g}, W)

    def start_group(i, dep):
        layer, kind = groups[i]
        return _exchange_start(group_srcs(i), scatter=False, name=f"gather_start_{kind}_l{layer}", dep=dep)


    b_cols = lax.dynamic_slice_in_dim(ada_b, me * wc, wc, axis=1).reshape(DEPTH, 1, wc)
    mod_part = _ada_mod(c_all, ada_w, b_cols, name="ada_mod")
    (mod_g,) = _exchange([mod_part], scatter=False, name="gather_mod")
    mod_mine = lax.dynamic_index_in_dim(mod_g, me, axis=2, keepdims=False)
    mod = jnp.transpose(mod_mine, (1, 0, 2)).reshape(DEPTH, N_MOD, D_MODEL)
    gather = {1: start_group(1, mod_g)}
    for i in range(2, AHEAD + 1):
        gather[i] = start_group(i, gather[i - 1][4])

    def get_weights(layer, kind, after):
        i = groups.index((layer, kind))
        names = _group_names(layer, kind)
        if i == 0:
            return _group_weights(layer, kind, dict(zip(names, first[1:])), gather[AHEAD][4])
        _, lands = _exchange_wait(gather[i], after, scatter=False, name=f"gather_wait_{kind}_l{layer}")
        token = jnp.zeros((8, LANES), F32)
        if i + AHEAD < len(groups):
            gather[i + AHEAD] = start_group(i + AHEAD, lands[0])
            token = gather[i + AHEAD][4]
        return _group_weights(layer, kind, dict(zip(names, lands)), token)

    scatter = []

    def put_grads(layer, kind, big):
        slots = _layer_grad_slots(kind, big)
        started = _exchange_start(list(slots.values()), scatter=True, name=f"scatter_start_{kind}_l{layer}")
        scatter.append((layer, kind, list(slots.keys()), started))
        return started[4]

    cos_t, sin_t = _rope_tables(positions[0])
    loss, dx, dmod, g = _local_step(x[0], loss_target[0], mod, cos_t, sin_t, rep, get_weights, put_grads)

    parts = {k: [None] * W[k].shape[0] for k in BIG}
    res = {}

    def wait_group(entry, after):
        layer, kind, names, started = entry
        _, lands = _exchange_wait(started, after, scatter=True, name=f"scatter_wait_{kind}_l{layer}")
        for k, z in zip(names, lands):
            parts[k][_layer_index(k, layer)] = z

    for entry in scatter[:-1]:
        wait_group(entry, dx)
    early = [k for k in BIG if k not in scatter[-1][2]]
    def update(k):
        outs = _adamw(parts[k], _view(k, W[k]), _view(k, M[k]), _view(k, V[k]), name=f"adamw_{k}")
        return tuple(_view(k, o) for o in outs)

    for k in early:
        res[k] = update(k)
    loss, dmod, done = lax.optimization_barrier((loss, dmod, [res[k] for k in early]))
    for k, r in zip(early, done):
        res[k] = r

    small_local = [dmod.reshape(DEPTH, N_MOD * D_MODEL), g["norm_mix_g"], g["norm_ffn_g"],
                   jnp.transpose(g["gdn_conv_wt"], (0, 2, 1)), g["gdn_a_log"], g["gdn_dt_bias"], g["gdn_norm_g"],
                   g["mla_q_norm_g"], g["mla_kv_norm_g"], g["final_norm_g"], loss.reshape(1)]
    small_shapes = [a.shape for a in small_local]
    (small_g,) = _exchange([_pack(small_local)], scatter=False, name="gather_small_grads")
    small_sum = _unpack(_sum_parts(small_g, name="sum_small_grads"), small_shapes)
    loss = small_sum[-1][0]
    dmod_all = _unpack(small_g, small_shapes[:1], lead=(N_DEV,))[0]
    sg = dict(zip(SMALL, small_sum))
    wait_group(scatter[-1], small_g)
    sg["gdn_conv_w"] = lax.dynamic_slice_in_dim(sg["gdn_conv_w"], me * gdn_conv_w.shape[1], gdn_conv_w.shape[1], 1)
    sg["mla_q_norm_g"] = lax.dynamic_slice_in_dim(sg["mla_q_norm_g"], me * mla_q_norm_g.shape[1],
                                                  mla_q_norm_g.shape[1], 1)
    sg["mla_kv_norm_g"] = lax.dynamic_slice_in_dim(sg["mla_kv_norm_g"], me * mla_kv_norm_g.shape[1],
                                                   mla_kv_norm_g.shape[1], 1)

    dmod_cols = jnp.transpose(lax.dynamic_slice_in_dim(dmod_all, me * wc, wc, axis=2), (1, 0, 2))
    res["ada_w"] = _ada_grad_adamw(c_all, dmod_cols, ada_w, m_ada_w, v_ada_w, name="ada_w_grad_adamw")
    for k in BIG:
        if k not in early:
            res[k] = update(k)
    shapes = [W[k].shape for k in SMALL]
    packed = [_pack([d[k] for k in SMALL]) for d in (sg, W, M, V)]
    outs = _adamw([packed[0][None]], packed[1][None], packed[2][None], packed[3][None], name="adamw_small")
    unpacked = [_unpack(o[0], shapes) for o in outs]
    for i, k in enumerate(SMALL):
        res[k] = tuple(u[i] for u in unpacked)

    return (loss, dx[None], *[res[k][0] for k in WEIGHTS], *[res[k][1] for k in WEIGHTS],
            *[res[k][2] for k in WEIGHTS], *[res[k][3] for k in WEIGHTS])
```

```python
import functools
import math

import jax
import jax.numpy as jnp
from jax import lax
from jax.experimental import pallas as pl
from jax.experimental.pallas import tpu as pltpu

F32 = jnp.float32
BF16 = jnp.bfloat16
MXU_DTYPE = jnp.bfloat16

N_DEV = 8
D_MODEL = 1024
DEPTH = 4
GDN_HEADS = 8
GDN_HEAD_DIM = 128
GDN_KEY_DIM = GDN_HEADS * GDN_HEAD_DIM
GDN_CHUNK = 64
GDN_HEAD_BATCH = 8
GDN_CONV = 4
GDN_PREP_HEADS = 2
GDN_MAIN = 4 * GDN_KEY_DIM
MLA_HEADS = 8
MLA_NOPE = 128
MLA_ROPE = 64
MLA_V = 128
MLA_Q_RANK = 384
MLA_KV_RANK = 256
MLA_IN = MLA_Q_RANK + MLA_KV_RANK + MLA_ROPE
MLA_QK = MLA_NOPE + MLA_ROPE
ROPE_THETA = 10000.0
D_FF = 2816
N_MOD = 6
EPS = 1e-6
LANES = 128
VMEM_LIMIT = 48 * 1024 * 1024

ADAM_LR = 0.001
ADAM_B1 = 0.9
ADAM_B2 = 0.999
ADAM_EPS = 1e-08
ADAM_WD = 0.01
ADAM_STEP = 10
ADAM_BC1 = 1.0 - ADAM_B1 ** ADAM_STEP
ADAM_BC2 = 1.0 - ADAM_B2 ** ADAM_STEP

NN = (((1,), (0,)), ((), ()))
NT = (((1,), (1,)), ((), ()))
TN = (((0,), (0,)), ((), ()))
NEG = -1e30


def _dotb(a, b, dims):
    return lax.dot_general(a.astype(MXU_DTYPE), b.astype(MXU_DTYPE), dims, preferred_element_type=F32)


def _split(a):
    hi = a.astype(BF16)
    return hi, (a - hi.astype(F32)).astype(BF16)


def _dotf(a, b, dims):
    ah, al = _split(a)
    bh, bl = _split(b)
    dot = lambda u, v: lax.dot_general(u, v, dims, preferred_element_type=F32)
    return dot(ah, bh) + (dot(ah, bl) + dot(al, bh))


def _params(*sem):
    return pltpu.CompilerParams(dimension_semantics=sem, vmem_limit_bytes=VMEM_LIMIT)


def _pick(n, pref, mult=LANES):
    best = None
    t = mult
    while t <= min(n, pref):
        if n % t == 0:
            best = t
        t += mult
    return best if best is not None else n


def _sigmoid(z):
    return 1.0 / (1.0 + jnp.exp(-z))


def _exchange(arrays, *, scatter, name):
    n = len(arrays)
    out_shape = tuple(
        jax.ShapeDtypeStruct(a.shape if scatter else (N_DEV,) + a.shape, a.dtype) for a in arrays)

    def body(*refs):
        ins, outs = refs[:n], refs[n:2 * n]
        send_sems, recv_sems, local_sems = refs[2 * n:]
        x, y, c = lax.axis_index("x"), lax.axis_index("y"), lax.axis_index("c")
        me = 4 * x + 2 * y + c
        copies = []
        for k in range(n):
            src_own = ins[k].at[me] if scatter else ins[k]
            own = pltpu.make_async_copy(src_own, outs[k].at[me], local_sems.at[k])
            own.start()
            copies.append(own)
        sends = []
        for p in range(1, N_DEV):
            px, py, pc = x ^ ((p >> 2) & 1), y ^ ((p >> 1) & 1), c ^ (p & 1)
            peer = 4 * px + 2 * py + pc
            for k in range(n):
                cp = pltpu.make_async_remote_copy(
                    src_ref=ins[k].at[peer] if scatter else ins[k],
                    dst_ref=outs[k].at[me],
                    send_sem=send_sems.at[k, p - 1],
                    recv_sem=recv_sems.at[k, p - 1],
                    device_id=(px, py, pc),
                    device_id_type=pl.DeviceIdType.MESH,
                )
                cp.start()
                sends.append((cp, k, peer, p))
        for cp, k, peer, p in sends:
            pltpu.make_async_remote_copy(
                src_ref=ins[k].at[peer] if scatter else ins[k],
                dst_ref=outs[k].at[peer],
                send_sem=send_sems.at[k, p - 1],
                recv_sem=recv_sems.at[k, p - 1],
                device_id=(x, y, c),
                device_id_type=pl.DeviceIdType.MESH,
            ).wait_recv()
        for cp, _, _, _ in sends:
            cp.wait_send()
        for own in copies:
            own.wait()

    any_spec = pl.BlockSpec(memory_space=pl.ANY)
    outs = pl.pallas_call(
        body,
        name=name,
        out_shape=out_shape,
        in_specs=[any_spec] * n,
        out_specs=tuple([any_spec] * n),
        scratch_shapes=[
            pltpu.SemaphoreType.DMA((n, N_DEV - 1)),
            pltpu.SemaphoreType.DMA((n, N_DEV - 1)),
            pltpu.SemaphoreType.DMA((n,)),
        ],
        compiler_params=pltpu.CompilerParams(has_side_effects=True),
    )(*arrays)
    return list(outs)


def _gather_two_level(arrays, *, name):
    n = len(arrays)
    out_shape = tuple(jax.ShapeDtypeStruct((N_DEV,) + a.shape, a.dtype) for a in arrays)

    def body(*refs):
        ins, outs = refs[:n], refs[n:2 * n]
        send_sems, recv_sems, local_sems = refs[2 * n:]
        x, y, c = lax.axis_index("x"), lax.axis_index("y"), lax.axis_index("c")
        me = 4 * x + 2 * y + c
        sibling = (x, y, 1 - c)
        chips = [(1 - x, y), (x, 1 - y), (1 - x, 1 - y)]

        def slot(px, py, pc):
            return 4 * px + 2 * py + pc

        def copy(k, q, block, to, src=None):
            return pltpu.make_async_remote_copy(
                src_ref=outs[k].at[slot(*block)] if src is None else src,
                dst_ref=outs[k].at[slot(*block)],
                send_sem=send_sems.at[k, q], recv_sem=recv_sems.at[k, q],
                device_id=to, device_id_type=pl.DeviceIdType.MESH)

        own = [pltpu.make_async_copy(ins[k], outs[k].at[me], local_sems.at[k]) for k in range(n)]
        for cp in own:
            cp.start()
        first = []
        for k in range(n):
            first.append(copy(k, 0, (x, y, c), sibling, src=ins[k]))
            first += [copy(k, 1 + j, (x, y, c), (*chip, c), src=ins[k]) for j, chip in enumerate(chips)]
        for cp in first:
            cp.start()
        passed = []
        for j, chip in enumerate(chips):
            for k in range(n):
                copy(k, 1 + j, (*chip, c), (x, y, c)).wait_recv()
                fwd = copy(k, 4 + j, (*chip, c), sibling)
                fwd.start()
                passed.append(fwd)
        for k in range(n):
            copy(k, 0, sibling, (x, y, c)).wait_recv()
            for j, chip in enumerate(chips):
                copy(k, 4 + j, (*chip, 1 - c), (x, y, c)).wait_recv()
        for cp in first + passed:
            cp.wait_send()
        for cp in own:
            cp.wait()

    any_spec = pl.BlockSpec(memory_space=pl.ANY)
    outs = pl.pallas_call(
        body, name=name, out_shape=out_shape, in_specs=[any_spec] * n, out_specs=tuple([any_spec] * n),
        scratch_shapes=[pltpu.SemaphoreType.DMA((n, N_DEV - 1)), pltpu.SemaphoreType.DMA((n, N_DEV - 1)),
                        pltpu.SemaphoreType.DMA((n,))],
        compiler_params=pltpu.CompilerParams(has_side_effects=True),
    )(*arrays)
    return list(outs)


def _peer(x, y, c, p):
    return x ^ ((p >> 2) & 1), y ^ ((p >> 1) & 1), c ^ (p & 1)


def _exchange_start(arrays, *, scatter, name, dep=None):
    n = len(arrays)
    deps = [] if dep is None else [dep]
    lands = [lax.empty(a.shape if scatter else (N_DEV,) + a.shape, a.dtype) for a in arrays]

    def body(*refs):
        ins, zones = refs[:n], refs[n:2 * n]
        send_sems, recv_sems = refs[2 * n + len(deps)], refs[2 * n + len(deps) + 1]
        token = refs[-1]
        x, y, c = lax.axis_index("x"), lax.axis_index("y"), lax.axis_index("c")
        me = 4 * x + 2 * y + c
        for p in range(1, N_DEV):
            px, py, pc = _peer(x, y, c, p)
            for k in range(n):
                pltpu.make_async_remote_copy(
                    src_ref=ins[k].at[4 * px + 2 * py + pc] if scatter else ins[k],
                    dst_ref=zones[k].at[me],
                    send_sem=send_sems.at[k * (N_DEV - 1) + p - 1],
                    recv_sem=recv_sems.at[k * (N_DEV - 1) + p - 1],
                    device_id=(px, py, pc),
                    device_id_type=pl.DeviceIdType.MESH,
                ).start()
        token[...] = jnp.zeros_like(token)

    hbm = pl.BlockSpec(memory_space=pltpu.HBM)
    sem = pl.BlockSpec(memory_space=pltpu.SEMAPHORE)
    outs = pl.pallas_call(
        body,
        name=name,
        out_shape=(pltpu.SemaphoreType.DMA((n * (N_DEV - 1),)), pltpu.SemaphoreType.DMA((n * (N_DEV - 1),)),
                   *[pltpu.HBM(a.shape, a.dtype) for a in arrays], *[pltpu.HBM(z.shape, z.dtype) for z in lands],
                   jax.ShapeDtypeStruct((8, LANES), F32)),
        in_specs=[hbm] * (2 * n) + [pl.BlockSpec(memory_space=pl.ANY)] * len(deps),
        out_specs=(sem, sem, *[hbm] * (2 * n), pl.BlockSpec(memory_space=pltpu.VMEM)),
        input_output_aliases={k: 2 + k for k in range(2 * n)},
        compiler_params=pltpu.CompilerParams(has_side_effects=pltpu.SideEffectType.DATAFLOW_SIDE_EFFECTING),
    )(*[pltpu.with_memory_space_constraint(a, pltpu.HBM) for a in arrays],
      *[pltpu.with_memory_space_constraint(z, pltpu.HBM) for z in lands], *deps)
    return outs[0], outs[1], list(outs[2:2 + n]), list(outs[2 + n:2 + 2 * n]), outs[-1]


def _exchange_wait(started, after, *, scatter, name):
    send_sems, recv_sems, srcs, lands, _ = started
    n = len(srcs)

    def body(*refs):
        ins, zones = refs[:n], refs[n:2 * n]
        s_sems, r_sems = refs[2 * n], refs[2 * n + 1]
        x, y, c = lax.axis_index("x"), lax.axis_index("y"), lax.axis_index("c")
        for p in range(1, N_DEV):
            px, py, pc = _peer(x, y, c, p)
            peer = 4 * px + 2 * py + pc
            for k in range(n):
                cp = pltpu.make_async_remote_copy(
                    src_ref=ins[k].at[peer] if scatter else ins[k],
                    dst_ref=zones[k].at[peer],
                    send_sem=s_sems.at[k * (N_DEV - 1) + p - 1],
                    recv_sem=r_sems.at[k * (N_DEV - 1) + p - 1],
                    device_id=(px, py, pc),
                    device_id_type=pl.DeviceIdType.MESH,
                )
                cp.wait_send()
                cp.wait_recv()

    hbm = pl.BlockSpec(memory_space=pltpu.HBM)
    sem = pl.BlockSpec(memory_space=pltpu.SEMAPHORE)
    outs = pl.pallas_call(
        body,
        name=name,
        out_shape=tuple(pltpu.HBM(a.shape, a.dtype) for a in srcs + lands),
        in_specs=[hbm] * (2 * n) + [sem, sem, pl.BlockSpec(memory_space=pl.ANY)],
        out_specs=tuple([hbm] * (2 * n)),
        input_output_aliases={k: k for k in range(2 * n)},
        compiler_params=pltpu.CompilerParams(has_side_effects=pltpu.SideEffectType.DATAFLOW_SIDE_EFFECTING),
    )(*srcs, *lands, send_sems, recv_sems, after)
    return list(outs[:n]), list(outs[n:])


def _mm(a, b, *, mode, out_dtype, name, add=None, tm=512, tn=512, b_rows=None, dep=None):
    rows_b = b.shape[0] if b_rows is None else b_rows
    if mode == "nn":
        (m, kd), nd = a.shape, b.shape[1]
        assert kd == rows_b
    elif mode == "nt":
        (m, kd), nd = a.shape, rows_b
    else:
        (kd, m), nd = a.shape, b.shape[1]
    tm = _pick(m, tm, LANES if mode == "tn" else 16)
    tn = _pick(nd, tn)
    dims = {"nn": NN, "nt": NT, "tn": TN}[mode]
    ni, nj = m // tm, nd // tn
    a_bytes, b_bytes = a.size * a.dtype.itemsize, b.size * b.dtype.itemsize
    i_outer = a_bytes + ni * b_bytes <= b_bytes + nj * a_bytes
    ij = (lambda g0, g1: (g0, g1)) if i_outer else (lambda g0, g1: (g1, g0))
    a_spec = (pl.BlockSpec((kd, tm), lambda g0, g1: (0, ij(g0, g1)[0])) if mode == "tn"
              else pl.BlockSpec((tm, kd), lambda g0, g1: (ij(g0, g1)[0], 0)))
    b_spec = (pl.BlockSpec((tn, kd), lambda g0, g1: (ij(g0, g1)[1], 0)) if mode == "nt"
              else pl.BlockSpec((kd, tn), lambda g0, g1: (0, ij(g0, g1)[1])))
    o_spec = pl.BlockSpec((tm, tn), lambda g0, g1: ij(g0, g1))
    has_add = add is not None

    def body(*refs):
        a_ref, b_ref = refs[0], refs[1]
        o_ref = refs[-1]
        acc = _dotb(a_ref[...], b_ref[...], dims)
        if has_add:
            acc = acc + refs[2][...].astype(F32)
        o_ref[...] = acc.astype(o_ref.dtype)

    ins = [a, b] + ([add] if has_add else []) + ([] if dep is None else [dep])
    specs = ([a_spec, b_spec] + ([o_spec] if has_add else [])
             + ([] if dep is None else [pl.BlockSpec((8, LANES), lambda g0, g1: (0, 0))]))
    return pl.pallas_call(
        body, name=name, grid=(ni, nj) if i_outer else (nj, ni), in_specs=specs, out_specs=o_spec,
        out_shape=jax.ShapeDtypeStruct((m, nd), out_dtype),
        compiler_params=_params("parallel", "parallel"),
    )(*ins)


def _mm_resid(a, b, x, gate, *, name, tm=256, tn=1024):
    m, kd = a.shape
    nd = b.shape[1]
    tm = _pick(m, tm, 16)
    tn = _pick(nd, tn)
    o_spec = pl.BlockSpec((tm, tn), lambda i, j: (i, j))

    def body(a_ref, b_ref, x_ref, g_ref, xo_ref, y_ref):
        y = _dotb(a_ref[...], b_ref[...], NN)
        y_ref[...] = y
        xo_ref[...] = x_ref[...] + g_ref[...] * y

    return pl.pallas_call(
        body, name=name, grid=(m // tm, nd // tn),
        in_specs=[pl.BlockSpec((tm, kd), lambda i, j: (i, 0)), pl.BlockSpec((kd, tn), lambda i, j: (0, j)),
                  o_spec, pl.BlockSpec((1, tn), lambda i, j: (0, j))],
        out_specs=(o_spec, o_spec),
        out_shape=(jax.ShapeDtypeStruct((m, nd), F32), jax.ShapeDtypeStruct((m, nd), F32)),
        compiler_params=_params("parallel", "parallel"),
    )(a, b, x, gate)


ROWS = 256


def _row_spec(width, rows=ROWS):
    return pl.BlockSpec((rows, width), lambda i: (i, 0))


def _const_spec(shape):
    return pl.BlockSpec(shape, lambda i: tuple(0 for _ in shape))


def _adaln_fwd(x, g, scale, shift, *, name):
    t, d = x.shape

    def body(x_ref, g_ref, sc_ref, sh_ref, h_ref):
        xv = x_ref[...]
        r = lax.rsqrt(jnp.mean(xv * xv, axis=-1, keepdims=True) + EPS)
        h_ref[...] = (xv * r * g_ref[...] * (1.0 + sc_ref[...]) + sh_ref[...]).astype(h_ref.dtype)

    return pl.pallas_call(
        body, name=name, grid=(t // ROWS,),
        in_specs=[_row_spec(d), _const_spec((1, d)), _const_spec((1, d)), _const_spec((1, d))],
        out_specs=_row_spec(d), out_shape=jax.ShapeDtypeStruct((t, d), BF16),
        compiler_params=_params("parallel"),
    )(x, g, scale, shift)


def _adaln_bwd(x, g, scale, shift, dh, dres, dep, *, name):
    t, d = x.shape

    def body(x_ref, g_ref, sc_ref, sh_ref, dh_ref, dr_ref, dep_ref, dx_ref, st_ref):
        @pl.when(pl.program_id(0) == 0)
        def _():
            st_ref[...] = jnp.zeros_like(st_ref)

        xv = x_ref[...]
        dhv = dh_ref[...].astype(F32)
        gv = g_ref[...]
        r = lax.rsqrt(jnp.mean(xv * xv, axis=-1, keepdims=True) + EPS)
        xh = xv * r
        nv = xh * gv
        dn = dhv * (1.0 + sc_ref[...])
        dxh = dn * gv
        dx_ref[...] = dr_ref[...] + r * (dxh - xh * jnp.mean(dxh * xh, axis=-1, keepdims=True))
        st_ref[0:1, :] += jnp.sum(dn * xh, axis=0, keepdims=True)
        st_ref[1:2, :] += jnp.sum(dhv * nv, axis=0, keepdims=True)
        st_ref[2:3, :] += jnp.sum(dhv, axis=0, keepdims=True)

    return pl.pallas_call(
        body, name=name, grid=(t // ROWS,),
        in_specs=[_row_spec(d), _const_spec((1, d)), _const_spec((1, d)), _const_spec((1, d)),
                  _row_spec(d), _row_spec(d), _const_spec((8, LANES))],
        out_specs=(_row_spec(d), _const_spec((8, d))),
        out_shape=(jax.ShapeDtypeStruct((t, d), F32), jax.ShapeDtypeStruct((8, d), F32)),
        compiler_params=_params("arbitrary"),
    )(x, g, scale, shift, dh, dres, dep)


def _gate_bwd(dxo, y, gate, dep, *, name):
    t, d = dxo.shape

    def body(dx_ref, y_ref, g_ref, dep_ref, dy_ref, st_ref):
        @pl.when(pl.program_id(0) == 0)
        def _():
            st_ref[...] = jnp.zeros_like(st_ref)

        dxv = dx_ref[...]
        dy_ref[...] = (dxv * g_ref[...]).astype(dy_ref.dtype)
        st_ref[0:1, :] += jnp.sum(dxv * y_ref[...], axis=0, keepdims=True)

    return pl.pallas_call(
        body, name=name, grid=(t // ROWS,),
        in_specs=[_row_spec(d), _row_spec(d), _const_spec((1, d)), _const_spec((8, LANES))],
        out_specs=(_row_spec(d), _const_spec((8, d))),
        out_shape=(jax.ShapeDtypeStruct((t, d), BF16), jax.ShapeDtypeStruct((8, d), F32)),
        compiler_params=_params("arbitrary"),
    )(dxo, y, gate, dep)


def _loss_head(x, g, target, *, name):
    t, d = x.shape

    def body(x_ref, g_ref, t_ref, dx_ref, st_ref, ls_ref):
        @pl.when(pl.program_id(0) == 0)
        def _():
            st_ref[...] = jnp.zeros_like(st_ref)
            ls_ref[...] = jnp.zeros_like(ls_ref)

        xv = x_ref[...]
        gv = g_ref[...]
        r = lax.rsqrt(jnp.mean(xv * xv, axis=-1, keepdims=True) + EPS)
        xh = xv * r
        err = xh * gv - t_ref[...]
        ls_ref[...] += 0.5 * jnp.sum(jnp.mean(err * err, axis=-1, keepdims=True))
        dy = err * (1.0 / d)
        dxh = dy * gv
        dx_ref[...] = r * (dxh - xh * jnp.mean(dxh * xh, axis=-1, keepdims=True))
        st_ref[0:1, :] += jnp.sum(dy * xh, axis=0, keepdims=True)

    return pl.pallas_call(
        body, name=name, grid=(t // ROWS,),
        in_specs=[_row_spec(d), _const_spec((1, d)), _row_spec(d)],
        out_specs=(_row_spec(d), _const_spec((8, d)), _const_spec((8, LANES))),
        out_shape=(jax.ShapeDtypeStruct((t, d), F32), jax.ShapeDtypeStruct((8, d), F32),
                   jax.ShapeDtypeStruct((8, LANES), F32)),
        compiler_params=_params("arbitrary"),
    )(x, g, target)


FFN_BLOCK = D_FF // 2


def _ffn_gu_fwd(h, wg, wu, dep, *, name):
    t, d = h.shape
    tn = FFN_BLOCK

    def body(h_ref, wg_ref, wu_ref, dep_ref, s_ref, a_ref, b_ref):
        hv = h_ref[...]
        a = _dotb(hv, wg_ref[...], NT)
        b = _dotb(hv, wu_ref[...], NT)
        s_ref[...] = (a * _sigmoid(a) * b).astype(s_ref.dtype)
        a_ref[...] = a.astype(a_ref.dtype)
        b_ref[...] = b.astype(b_ref.dtype)

    w_spec = pl.BlockSpec((tn, d), lambda j, i: (j, 0))
    o_spec = pl.BlockSpec((ROWS, tn), lambda j, i: (i, j))
    return pl.pallas_call(
        body, name=name, grid=(D_FF // tn, t // ROWS),
        in_specs=[pl.BlockSpec((ROWS, d), lambda j, i: (i, 0)), w_spec, w_spec,
                  pl.BlockSpec((8, LANES), lambda j, i: (0, 0))],
        out_specs=(o_spec, o_spec, o_spec),
        out_shape=(jax.ShapeDtypeStruct((t, D_FF), BF16),) * 3,
        compiler_params=_params("parallel", "parallel"),
    )(h, wg, wu, dep)


def _ffn_down_dx(dy, w_down, a, b, *, name):
    t, d = dy.shape
    tn = FFN_BLOCK

    edges = [min(tn, 3 * LANES * i) for i in range(tn // (3 * LANES) + 2)]
    chunks = [slice(lo, hi) for lo, hi in zip(edges[:-1], edges[1:]) if hi > lo]

    def body(dy_ref, w_ref, a_ref, b_ref, da_ref, db_ref):
        dyv = dy_ref[...]
        ds = [_dotb(dyv, w_ref[sl, :], NT) for sl in chunks]
        for sl, dsc in zip(chunks, ds):
            av = a_ref[:, sl].astype(F32)
            sg = _sigmoid(av)
            da_ref[:, sl] = (dsc * b_ref[:, sl].astype(F32) * sg * (1.0 + av * (1.0 - sg))).astype(da_ref.dtype)
            db_ref[:, sl] = (dsc * av * sg).astype(db_ref.dtype)

    o_spec = pl.BlockSpec((ROWS, tn), lambda j, i: (i, j))
    return pl.pallas_call(
        body, name=name, grid=(D_FF // tn, t // ROWS),
        in_specs=[pl.BlockSpec((ROWS, d), lambda j, i: (i, 0)), pl.BlockSpec((tn, d), lambda j, i: (j, 0)),
                  o_spec, o_spec],
        out_specs=(o_spec, o_spec),
        out_shape=(jax.ShapeDtypeStruct((t, D_FF), BF16),) * 2,
        compiler_params=_params("parallel", "parallel"),
    )(dy, w_down, a, b)


def _shift_rows(v, s, rows):
    if s == 0:
        return v
    return jnp.where(rows >= s, pltpu.roll(v, s, 0), 0.0)


def _unshift_rows(v, s, rows, t):
    if s == 0:
        return v
    return jnp.where(rows < t - s, pltpu.roll(v, t - s, 0), 0.0)


def _conv_silu(x, w, rows):
    z = w[GDN_CONV - 1:GDN_CONV, :] * x
    for j in range(GDN_CONV - 1):
        z = z + w[j:j + 1, :] * _shift_rows(x, GDN_CONV - 1 - j, rows)
    sg = _sigmoid(z)
    return z, sg, z * sg


def _gdn_prep_fwd(proj, conv_wt, *, name):
    t = proj.shape[0]
    nh = GDN_HEADS

    hp = GDN_PREP_HEADS
    wd = hp * LANES

    def body(x_ref, w_ref, y_ref):
        j = pl.program_id(0) * hp
        rows = lax.broadcasted_iota(jnp.int32, (t, LANES), 0)
        qscale = jnp.where(j < nh, GDN_HEAD_DIM ** -0.5, 1.0)
        for i in range(hp):
            sl = slice(i * LANES, (i + 1) * LANES)
            _, _, s = _conv_silu(x_ref[:, sl], w_ref[:, sl], rows)
            rs = lax.rsqrt(jnp.sum(s * s, axis=-1, keepdims=True) + EPS)
            y_ref[:, sl] = jnp.where(j < 2 * nh, s * rs * qscale, s)

    return pl.pallas_call(
        body, name=name, grid=(3 * nh // hp,),
        in_specs=[pl.BlockSpec((t, wd), lambda j: (0, j)), pl.BlockSpec((GDN_CONV, wd), lambda j: (0, j))],
        out_specs=pl.BlockSpec((t, wd), lambda j: (0, j)),
        out_shape=jax.ShapeDtypeStruct((t, 3 * GDN_KEY_DIM), F32),
        compiler_params=_params("parallel"),
    )(proj, conv_wt)


def _gdn_prep_bwd(proj, conv_wt, dy, *, name):
    t = proj.shape[0]
    nh = GDN_HEADS

    hp = GDN_PREP_HEADS
    wd = hp * LANES
    per_seg = nh // hp

    def body(x_ref, w_ref, dy_ref, dx_ref, dw_ref):
        j = pl.program_id(0) * hp
        rows = lax.broadcasted_iota(jnp.int32, (t, LANES), 0)
        qscale = jnp.where(j < nh, GDN_HEAD_DIM ** -0.5, 1.0)
        for i in range(hp):
            sl = slice(i * LANES, (i + 1) * LANES)
            x = x_ref[:, sl]
            w = w_ref[:, sl]
            z, sg, s = _conv_silu(x, w, rows)
            rs = lax.rsqrt(jnp.sum(s * s, axis=-1, keepdims=True) + EPS)
            dyv = dy_ref[:, sl]
            nv = s * rs
            de = dyv * qscale
            ds_qk = rs * (de - nv * jnp.sum(de * nv, axis=-1, keepdims=True))
            ds = jnp.where(j < 2 * nh, ds_qk, dyv)
            dz = ds * sg * (1.0 + z * (1.0 - sg))
            dx = w[GDN_CONV - 1:GDN_CONV, :] * dz
            dw_ref[GDN_CONV - 1:GDN_CONV, sl] = jnp.sum(dz * x, axis=0, keepdims=True)
            for k in range(GDN_CONV - 1):
                sh = GDN_CONV - 1 - k
                dx = dx + w[k:k + 1, :] * _unshift_rows(dz, sh, rows, t)
                dw_ref[k:k + 1, sl] = jnp.sum(dz * _shift_rows(x, sh, rows), axis=0, keepdims=True)
            dx_ref[:, sl] = dx.astype(dx_ref.dtype)

    return pl.pallas_call(
        body, name=name, grid=(3 * nh // hp,),
        in_specs=[pl.BlockSpec((t, wd), lambda j: (0, j)), pl.BlockSpec((GDN_CONV, wd), lambda j: (0, j)),
                  pl.BlockSpec((None, t, wd), lambda j: (j // per_seg, 0, j % per_seg))],
        out_specs=(pl.BlockSpec((t, wd), lambda j: (0, j)), pl.BlockSpec((GDN_CONV, wd), lambda j: (0, j))),
        out_shape=(jax.ShapeDtypeStruct((t, 3 * GDN_KEY_DIM), BF16),
                   jax.ShapeDtypeStruct((GDN_CONV, 3 * GDN_KEY_DIM), F32)),
        compiler_params=_params("parallel"),
    )(proj, conv_wt, dy)


def _softplus(z):
    return jnp.maximum(z, 0.0) + jnp.log(1.0 + jnp.exp(-jnp.abs(z)))


def _gdn_gate_fwd(ab, prm, *, name):
    t = ab.shape[0]

    def body(ab_ref, p_ref, o_ref):
        v = ab_ref[...]
        lane = lax.broadcasted_iota(jnp.int32, v.shape, 1)
        g = -jnp.exp(p_ref[0:1, :]) * _softplus(v + p_ref[1:2, :])
        o_ref[...] = jnp.where(lane < GDN_HEADS, g, jnp.where(lane < 2 * GDN_HEADS, _sigmoid(v), 0.0))

    return pl.pallas_call(
        body, name=name, grid=(t // ROWS,),
        in_specs=[_row_spec(LANES), _const_spec((8, LANES))], out_specs=_row_spec(LANES),
        out_shape=jax.ShapeDtypeStruct((t, LANES), F32), compiler_params=_params("parallel"),
    )(ab, prm)


def _gdn_gate_bwd(ab, prm, dgb, *, name):
    t = ab.shape[0]

    def body(ab_ref, p_ref, d_ref, o_ref, st_ref):
        @pl.when(pl.program_id(0) == 0)
        def _():
            st_ref[...] = jnp.zeros_like(st_ref)

        v = ab_ref[...]
        dv = d_ref[...]
        lane = lax.broadcasted_iota(jnp.int32, v.shape, 1)
        is_a = lane < GDN_HEADS
        is_b = jnp.logical_and(lane >= GDN_HEADS, lane < 2 * GDN_HEADS)
        a_exp = jnp.exp(p_ref[0:1, :])
        zz = v + p_ref[1:2, :]
        g = -a_exp * _softplus(zz)
        da = dv * (-a_exp) * _sigmoid(zz)
        beta = _sigmoid(v)
        db = dv * beta * (1.0 - beta)
        o_ref[...] = jnp.where(is_a, da, jnp.where(is_b, db, 0.0)).astype(o_ref.dtype)
        st_ref[0:1, :] += jnp.sum(jnp.where(is_a, dv * g, 0.0), axis=0, keepdims=True)
        st_ref[1:2, :] += jnp.sum(jnp.where(is_a, da, 0.0), axis=0, keepdims=True)

    return pl.pallas_call(
        body, name=name, grid=(t // ROWS,),
        in_specs=[_row_spec(LANES), _const_spec((8, LANES)), _row_spec(LANES)],
        out_specs=(_row_spec(LANES), _const_spec((8, LANES))),
        out_shape=(jax.ShapeDtypeStruct((t, LANES), BF16), jax.ShapeDtypeStruct((8, LANES), F32)),
        compiler_params=_params("arbitrary"),
    )(ab, prm, dgb)


def _gdn_local(qs, ks, vs, gbs, bbs):
    nh = len(qs)
    cs = qs[0].shape[0]
    hs = range(nh)
    r = lax.broadcasted_iota(jnp.int32, (cs, cs), 0)
    c = lax.broadcasted_iota(jnp.int32, (cs, cs), 1)
    tril, strict, eye = r >= c, r > c, r == c
    ident = jnp.where(eye, 1.0, 0.0)
    g_colb = [gbs[h][:, :cs] for h in hs]
    g_row = [jnp.sum(jnp.where(eye, g_colb[h], 0.0), axis=0, keepdims=True) for h in hs]
    gc_col = [jnp.sum(jnp.where(tril, g_row[h], 0.0), axis=1, keepdims=True) for h in hs]
    gc_row = [jnp.sum(jnp.where(r <= c, g_colb[h], 0.0), axis=0, keepdims=True) for h in hs]
    decay = [jnp.exp(jnp.where(tril, gc_col[h] - gc_row[h], NEG)) for h in hs]
    gamma = [jnp.exp(gc_col[h]) for h in hs]
    gcl = [gc_col[h][cs - 1:cs, :] for h in hs]
    gl = [jnp.exp(gcl[h]) for h in hs]
    kdec = [jnp.exp(gcl[h] - gc_col[h]) for h in hs]
    kb = [ks[h] * bbs[h] for h in hs]
    kk = [_dotb(kb[h], ks[h], NT) for h in hs]
    qk = [_dotb(qs[h], ks[h], NT) for h in hs]
    lmat = [jnp.where(strict, kk[h] * decay[h], 0.0) for h in hs]
    pmat = [jnp.where(tril, qk[h] * decay[h], 0.0) for h in hs]
    xm = [-lmat[h] for h in hs]
    tinv = [ident + xm[h] for h in hs]
    for _ in range(int(math.log2(cs)) - 1):
        xm = [_dotf(xm[h], xm[h], NN) for h in hs]
        tinv = [tinv[h] + _dotf(tinv[h], xm[h], NN) for h in hs]
    vb = [vs[h] * bbs[h] for h in hs]
    kg = [kb[h] * gamma[h] for h in hs]
    u = [_dotf(tinv[h], vb[h], NN) for h in hs]
    w = [_dotf(tinv[h], kg[h], NN) for h in hs]
    return [dict(tril=tril, strict=strict, eye=eye, r=r, c=c, decay=decay[h], gamma=gamma[h], gl=gl[h], kdec=kdec[h],
                 kb=kb[h], lmat=lmat[h], tinv=tinv[h], vb=vb[h], kg=kg[h], u=u[h], w=w[h], pmat=pmat[h],
                 qd=qs[h] * gamma[h], kd=ks[h] * kdec[h]) for h in hs]


def _gdn_chunk_fwd(qkv, gbc, bbc, *, name):
    t = qkv.shape[0]
    nh, cs, hd = GDN_HEADS, GDN_CHUNK, GDN_HEAD_DIM
    nc = t // cs

    hb = GDN_HEAD_BATCH
    ng = nh // hb

    def body(q_ref, k_ref, v_ref, g_ref, b_ref, o_ref, st_ref, s_ref):
        @pl.when(pl.program_id(1) == 0)
        def _():
            s_ref[...] = jnp.zeros_like(s_ref)

        sls = [slice(i * hd, (i + 1) * hd) for i in range(hb)]
        hs = range(hb)
        s = [s_ref[i] for i in hs]
        lo = _gdn_local([q_ref[:, sl] for sl in sls], [k_ref[:, sl] for sl in sls], [v_ref[:, sl] for sl in sls],
                        [g_ref[i] for i in hs], [b_ref[i] for i in hs])
        ws = [_dotb(lo[i]["w"], s[i], NN) for i in hs]
        qs = [_dotb(lo[i]["qd"], s[i], NN) for i in hs]
        vn = [lo[i]["u"] - ws[i] for i in hs]
        pv = [_dotb(lo[i]["pmat"], vn[i], NN) for i in hs]
        kv = [_dotb(lo[i]["kd"], vn[i], TN) for i in hs]
        for i, sl in enumerate(sls):
            st_ref[i, 0] = s[i]
            o_ref[:, sl] = qs[i] + pv[i]
            s_ref[i] = s[i] * lo[i]["gl"] + kv[i]

    gspec = pl.BlockSpec((hb, cs, LANES), lambda h, n: (h, n, 0))
    col = lambda off: pl.BlockSpec((cs, hb * hd), lambda h, n: (n, off + h))
    return pl.pallas_call(
        body, name=name, grid=(ng, nc),
        in_specs=[col(0), col(ng), col(2 * ng), gspec, gspec],
        out_specs=(col(0), pl.BlockSpec((hb, 1, hd, hd), lambda h, n: (h, n, 0, 0))),
        out_shape=(jax.ShapeDtypeStruct((t, nh * hd), F32), jax.ShapeDtypeStruct((nh, nc, hd, hd), F32)),
        scratch_shapes=[pltpu.VMEM((hb, hd, hd), F32)],
        compiler_params=_params("parallel", "arbitrary"),
    )(qkv, qkv, qkv, gbc, bbc)


def _gdn_chunk_bwd(qkv, gbc, bbc, states, do, *, name):
    t = qkv.shape[0]
    nh, cs, hd = GDN_HEADS, GDN_CHUNK, GDN_HEAD_DIM
    nc = t // cs

    hb = GDN_HEAD_BATCH
    ng = nh // hb

    def heads_bwd(q, k, v, gb, bb, s, dsn, dov):
        hs = range(len(q))
        lo = _gdn_local(q, k, v, gb, bb)
        tril, strict, eye, r, c = lo[0]["tril"], lo[0]["strict"], lo[0]["eye"], lo[0]["r"], lo[0]["c"]
        rowi = lax.broadcasted_iota(jnp.int32, (cs, 1), 0)
        get = lambda name: [lo[h][name] for h in hs]
        decay, gamma, gl, kdec = get("decay"), get("gamma"), get("gl"), get("kdec")
        kb, tinv, w, pmat, kd, qd = get("kb"), get("tinv"), get("w"), get("pmat"), get("kd"), get("qd")
        ws = [_dotb(w[h], s[h], NN) for h in hs]
        pdo = [_dotb(pmat[h], dov[h], TN) for h in hs]
        kds = [_dotb(kd[h], dsn[h], NN) for h in hs]
        dqd = [_dotb(dov[h], s[h], NT) for h in hs]
        qdo = [_dotb(qd[h], dov[h], TN) for h in hs]
        vn = [lo[h]["u"] - ws[h] for h in hs]
        dvn = [pdo[h] + kds[h] for h in hs]
        dp = [jnp.where(tril, _dotb(dov[h], vn[h], NT), 0.0) for h in hs]
        dkd = [_dotb(vn[h], dsn[h], NT) for h in hs]
        dw = [-_dotb(dvn[h], s[h], NT) for h in hs]
        wdv = [_dotb(w[h], dvn[h], TN) for h in hs]
        dvb = [_dotf(tinv[h], dvn[h], TN) for h in hs]
        dt1 = [_dotf(dvn[h], lo[h]["vb"], NT) for h in hs]
        dkg = [_dotf(tinv[h], dw[h], TN) for h in hs]
        dt2 = [_dotf(dw[h], lo[h]["kg"], NT) for h in hs]
        tdt = [_dotf(tinv[h], dt1[h] + dt2[h], TN) for h in hs]
        dl = [jnp.where(strict, -_dotf(tdt[h], tinv[h], NT), 0.0) for h in hs]
        dkk = [dl[h] * decay[h] for h in hs]
        dqk = [dp[h] * decay[h] for h in hs]
        dkb = [_dotb(dkk[h], k[h], NN) + dkg[h] * gamma[h] for h in hs]
        dk1 = [_dotb(dkk[h], kb[h], TN) for h in hs]
        dk2 = [_dotb(dqk[h], q[h], TN) for h in hs]
        dq1 = [_dotb(dqk[h], k[h], NN) for h in hs]
        out = []
        for h in hs:
            dgl = jnp.sum(jnp.sum(dsn[h] * s[h], axis=1, keepdims=True), axis=0, keepdims=True)
            ds_prev = gl[h] * dsn[h] + qdo[h] - wdv[h]
            dk = dk1[h] + dk2[h] + dkd[h] * kdec[h] + dkb[h] * bb[h]
            dq = dq1[h] + dqd[h] * gamma[h]
            dbeta = jnp.sum(dvb[h] * v[h], axis=-1, keepdims=True) + jnp.sum(dkb[h] * k[h], axis=-1, keepdims=True)
            e = dl[h] * lo[h]["lmat"] + dp[h] * pmat[h]
            e_col = jnp.sum(e, axis=0, keepdims=True)
            dgc = jnp.sum(e, axis=1, keepdims=True) - jnp.sum(jnp.where(eye, e_col, 0.0), axis=1, keepdims=True)
            dgamma = (jnp.sum(dqd[h] * q[h], axis=-1, keepdims=True)
                      + jnp.sum(dkg[h] * kb[h], axis=-1, keepdims=True))
            rk = jnp.sum(dkd[h] * k[h], axis=-1, keepdims=True) * kdec[h]
            dgcl = jnp.sum(rk, axis=0, keepdims=True) + dgl * gl[h]
            dgc = dgc + dgamma * gamma[h] - rk + jnp.where(rowi == cs - 1, dgcl, 0.0)
            dgc_row = jnp.sum(jnp.where(eye, dgc, 0.0), axis=0, keepdims=True)
            dg = jnp.sum(jnp.where(c >= r, dgc_row, 0.0), axis=1, keepdims=True)
            out.append((dq, dk, dvb[h] * bb[h], dbeta, dg, ds_prev))
        return out

    def body(q_ref, k_ref, v_ref, g_ref, b_ref, st_ref, do_ref, d_ref, dg_ref, db_ref, ds_ref):
        @pl.when(pl.program_id(1) == 0)
        def _():
            ds_ref[...] = jnp.zeros_like(ds_ref)

        sls = [slice(i * hd, (i + 1) * hd) for i in range(hb)]
        hs = range(hb)
        outs = heads_bwd([q_ref[:, sl] for sl in sls], [k_ref[:, sl] for sl in sls], [v_ref[:, sl] for sl in sls],
                         [g_ref[i] for i in hs], [b_ref[i] for i in hs], [st_ref[i, 0] for i in hs],
                         [ds_ref[i] for i in hs], [do_ref[:, sl] for sl in sls])
        for i, sl in enumerate(sls):
            dq, dk, dv, dbeta, dg, ds_prev = outs[i]
            d_ref[0, :, sl], d_ref[1, :, sl], d_ref[2, :, sl] = dq, dk, dv
            db_ref[i] = jnp.broadcast_to(dbeta, (cs, LANES))
            dg_ref[i] = jnp.broadcast_to(dg, (cs, LANES))
            ds_ref[i] = ds_prev

    gspec = pl.BlockSpec((hb, cs, LANES), lambda h, n: (h, nc - 1 - n, 0))
    col = lambda off: pl.BlockSpec((cs, hb * hd), lambda h, n: (nc - 1 - n, off + h))
    return pl.pallas_call(
        body, name=name, grid=(ng, nc),
        in_specs=[col(0), col(ng), col(2 * ng), gspec, gspec,
                  pl.BlockSpec((hb, 1, hd, hd), lambda h, n: (h, nc - 1 - n, 0, 0)), col(0)],
        out_specs=(pl.BlockSpec((3, cs, hb * hd), lambda h, n: (0, nc - 1 - n, h)), gspec, gspec),
        out_shape=(jax.ShapeDtypeStruct((3, t, nh * hd), F32),) + (jax.ShapeDtypeStruct((nh, t, LANES), F32),) * 2,
        scratch_shapes=[pltpu.VMEM((hb, hd, hd), F32)],
        compiler_params=_params("parallel", "arbitrary"),
    )(qkv, qkv, qkv, gbc, bbc, states, do)


def _gdn_onorm_fwd(o, proj, norm_g, *, name):
    t = o.shape[0]
    w = GDN_KEY_DIM
    goff = 3 * GDN_KEY_DIM // w

    def body(o_ref, gp_ref, g_ref, y_ref):
        gv = g_ref[...]
        for h in range(GDN_HEADS):
            sl = slice(h * GDN_HEAD_DIM, (h + 1) * GDN_HEAD_DIM)
            oh = o_ref[:, sl]
            gp = gp_ref[:, sl]
            r = lax.rsqrt(jnp.mean(oh * oh, axis=-1, keepdims=True) + EPS)
            y_ref[:, sl] = (oh * r * gv * gp * _sigmoid(gp)).astype(y_ref.dtype)

    return pl.pallas_call(
        body, name=name, grid=(t // ROWS,),
        in_specs=[_row_spec(w), pl.BlockSpec((ROWS, w), lambda i: (i, goff)), _const_spec((1, GDN_HEAD_DIM))],
        out_specs=_row_spec(w), out_shape=jax.ShapeDtypeStruct((t, w), BF16),
        compiler_params=_params("parallel"),
    )(o, proj, norm_g)


def _gdn_onorm_bwd(o, proj, norm_g, dy, *, name):
    t = o.shape[0]
    w = GDN_KEY_DIM
    goff = 3 * GDN_KEY_DIM // w

    def body(o_ref, gp_ref, g_ref, dy_ref, do_ref, dgp_ref, st_ref):
        @pl.when(pl.program_id(0) == 0)
        def _():
            st_ref[...] = jnp.zeros_like(st_ref)

        gv = g_ref[...]
        acc = jnp.zeros((1, GDN_HEAD_DIM), F32)
        for h in range(GDN_HEADS):
            sl = slice(h * GDN_HEAD_DIM, (h + 1) * GDN_HEAD_DIM)
            oh = o_ref[:, sl]
            gp = gp_ref[:, sl]
            dyv = dy_ref[:, sl].astype(F32)
            r = lax.rsqrt(jnp.mean(oh * oh, axis=-1, keepdims=True) + EPS)
            xh = oh * r
            sg = _sigmoid(gp)
            dn = dyv * gp * sg
            dgp_ref[:, sl] = (dyv * xh * gv * sg * (1.0 + gp * (1.0 - sg))).astype(dgp_ref.dtype)
            acc = acc + jnp.sum(dn * xh, axis=0, keepdims=True)
            dxh = dn * gv
            do_ref[:, sl] = r * (dxh - xh * jnp.mean(dxh * xh, axis=-1, keepdims=True))
        st_ref[0:1, :] += acc

    return pl.pallas_call(
        body, name=name, grid=(t // ROWS,),
        in_specs=[_row_spec(w), pl.BlockSpec((ROWS, w), lambda i: (i, goff)), _const_spec((1, GDN_HEAD_DIM)),
                  _row_spec(w)],
        out_specs=(_row_spec(w), _row_spec(w), _const_spec((8, GDN_HEAD_DIM))),
        out_shape=(jax.ShapeDtypeStruct((t, w), F32), jax.ShapeDtypeStruct((t, w), BF16),
                   jax.ShapeDtypeStruct((8, GDN_HEAD_DIM), F32)),
        compiler_params=_params("arbitrary"),
    )(o, proj, norm_g, dy)


def _mla_prep_fwd(proj, qg, kvg, *, name):
    t = proj.shape[0]
    q1, k1 = MLA_Q_RANK, MLA_Q_RANK + MLA_KV_RANK

    def body(p_ref, qg_ref, kg_ref, cq_ref, ck_ref):
        cq = p_ref[:, 0:q1]
        ck = p_ref[:, q1:k1]
        cq_ref[...] = (cq * lax.rsqrt(jnp.mean(cq * cq, axis=-1, keepdims=True) + EPS) * qg_ref[...]).astype(BF16)
        ck_ref[...] = (ck * lax.rsqrt(jnp.mean(ck * ck, axis=-1, keepdims=True) + EPS) * kg_ref[...]).astype(BF16)

    return pl.pallas_call(
        body, name=name, grid=(t // ROWS,),
        in_specs=[_row_spec(MLA_IN), _const_spec((1, MLA_Q_RANK)), _const_spec((1, MLA_KV_RANK))],
        out_specs=(_row_spec(MLA_Q_RANK), _row_spec(MLA_KV_RANK)),
        out_shape=(jax.ShapeDtypeStruct((t, MLA_Q_RANK), BF16), jax.ShapeDtypeStruct((t, MLA_KV_RANK), BF16)),
        compiler_params=_params("parallel"),
    )(proj, qg, kvg)


def _mla_prep_bwd(proj, qg, kvg, dcq, dck, dkr, *, name):
    t = proj.shape[0]
    q1, k1 = MLA_Q_RANK, MLA_Q_RANK + MLA_KV_RANK

    def body(p_ref, qg_ref, kg_ref, dq_ref, dk_ref, dr_ref, dp_ref, st_ref):
        @pl.when(pl.program_id(0) == 0)
        def _():
            st_ref[...] = jnp.zeros_like(st_ref)

        for lo, hi, g_ref, d_ref in ((0, q1, qg_ref, dq_ref), (q1, k1, kg_ref, dk_ref)):
            xv = p_ref[:, lo:hi]
            dn = d_ref[...]
            r = lax.rsqrt(jnp.mean(xv * xv, axis=-1, keepdims=True) + EPS)
            xh = xv * r
            dxh = dn * g_ref[...]
            dp_ref[:, lo:hi] = (r * (dxh - xh * jnp.mean(dxh * xh, axis=-1, keepdims=True))).astype(dp_ref.dtype)
            st_ref[0:1, lo:hi] += jnp.sum(dn * xh, axis=0, keepdims=True)
        dp_ref[:, k1:MLA_IN] = dr_ref[:, 0:MLA_ROPE].astype(dp_ref.dtype)

    return pl.pallas_call(
        body, name=name, grid=(t // ROWS,),
        in_specs=[_row_spec(MLA_IN), _const_spec((1, MLA_Q_RANK)), _const_spec((1, MLA_KV_RANK)),
                  _row_spec(MLA_Q_RANK), _row_spec(MLA_KV_RANK), _row_spec(LANES)],
        out_specs=(_row_spec(MLA_IN), _const_spec((8, MLA_IN))),
        out_shape=(jax.ShapeDtypeStruct((t, MLA_IN), BF16), jax.ShapeDtypeStruct((8, MLA_IN), F32)),
        compiler_params=_params("arbitrary"),
    )(proj, qg, kvg, dcq, dck, dkr)


def _rope(xr, cos_t, sin_t, *, name):
    t, w = xr.shape
    ns = w // LANES

    def body(x_ref, c_ref, s_ref, o_ref):
        cv, sv = c_ref[...], s_ref[...]
        lane = lax.broadcasted_iota(jnp.int32, (ROWS, LANES), 1)
        first = (lane % MLA_ROPE) < (MLA_ROPE // 2)
        for i in range(ns):
            sl = slice(i * LANES, (i + 1) * LANES)
            xv = x_ref[:, sl]
            sw = jnp.where(first, pltpu.roll(xv, LANES - MLA_ROPE // 2, 1), pltpu.roll(xv, MLA_ROPE // 2, 1))
            o_ref[:, sl] = xv * cv + sw * sv

    return pl.pallas_call(
        body, name=name, grid=(t // ROWS,),
        in_specs=[_row_spec(w), _row_spec(LANES), _row_spec(LANES)], out_specs=_row_spec(w),
        out_shape=jax.ShapeDtypeStruct((t, w), F32), compiler_params=_params("parallel"),
    )(xr, cos_t, sin_t)


def _rope_bwd(dr, cos_t, sin_t, *, name):
    t, w = dr.shape
    ns = w // LANES

    def body(d_ref, c_ref, s_ref, o_ref):
        cv, sv = c_ref[...], s_ref[...]
        lane = lax.broadcasted_iota(jnp.int32, (ROWS, LANES), 1)
        first = (lane % MLA_ROPE) < (MLA_ROPE // 2)
        for i in range(ns):
            sl = slice(i * LANES, (i + 1) * LANES)
            dv = d_ref[:, sl]
            ds = dv * sv
            sw = jnp.where(first, pltpu.roll(ds, LANES - MLA_ROPE // 2, 1), pltpu.roll(ds, MLA_ROPE // 2, 1))
            o_ref[:, sl] = dv * cv + sw

    return pl.pallas_call(
        body, name=name, grid=(t // ROWS,),
        in_specs=[_row_spec(w), _row_spec(LANES), _row_spec(LANES)], out_specs=_row_spec(w),
        out_shape=jax.ShapeDtypeStruct((t, w), F32), compiler_params=_params("parallel"),
    )(dr, cos_t, sin_t)


ATT_BLOCK = 256
ATT_HEAD_BATCH = 4
ATT_HEAD_BATCH_BWD = 4
ATT_SCALE = MLA_QK ** -0.5


def _causal_mask(i, j, blk):
    rows = i * blk + lax.broadcasted_iota(jnp.int32, (blk, blk), 0)
    cols = j * blk + lax.broadcasted_iota(jnp.int32, (blk, blk), 1)
    return cols <= rows


def _attn_fwd(q, k, v, *, name):
    nh, t, dk = q.shape
    dv = v.shape[-1]
    blk = min(ATT_BLOCK, t)

    hb = ATT_HEAD_BATCH
    hs = range(hb)

    def body(q_ref, k_ref, v_ref, o_ref, l_ref):
        i = pl.program_id(1)
        qv = [q_ref[h] for h in hs]

        def step(j, carry):
            m, l, acc = carry[:hb], carry[hb:2 * hb], carry[2 * hb:]
            off = pl.multiple_of(j * blk, blk)
            mask = _causal_mask(i, j, blk)
            s = [_dotb(qv[h], k_ref[h, pl.ds(off, blk), :], NT) for h in hs]
            s = [jnp.where(mask, s[h] * ATT_SCALE, NEG) for h in hs]
            m_new = [jnp.maximum(m[h], jnp.max(s[h], axis=-1, keepdims=True)) for h in hs]
            p = [jnp.exp(s[h] - m_new[h]) for h in hs]
            pv = [_dotb(p[h], v_ref[h, pl.ds(off, blk), :], NN) for h in hs]
            alpha = [jnp.exp(m[h] - m_new[h]) for h in hs]
            l = [alpha[h] * l[h] + jnp.sum(p[h], axis=-1, keepdims=True) for h in hs]
            acc = [alpha[h] * acc[h] + pv[h] for h in hs]
            return tuple(m_new) + tuple(l) + tuple(acc)

        init = ((jnp.full((blk, 1), NEG, F32),) * hb + (jnp.zeros((blk, 1), F32),) * hb
                + (jnp.zeros((blk, dv), F32),) * hb)
        out = lax.fori_loop(0, i + 1, step, init)
        for h in hs:
            m, l, acc = out[h], out[hb + h], out[2 * hb + h]
            o_ref[h] = acc / l
            l_ref[h] = jnp.broadcast_to(m + jnp.log(l), (blk, LANES))

    return pl.pallas_call(
        body, name=name, grid=(nh // hb, t // blk),
        in_specs=[pl.BlockSpec((hb, blk, dk), lambda h, i: (h, i, 0)), pl.BlockSpec((hb, t, dk), lambda h, i: (h, 0, 0)),
                  pl.BlockSpec((hb, t, dv), lambda h, i: (h, 0, 0))],
        out_specs=(pl.BlockSpec((hb, blk, dv), lambda h, i: (h, i, 0)),
                   pl.BlockSpec((hb, blk, LANES), lambda h, i: (h, i, 0))),
        out_shape=(jax.ShapeDtypeStruct((nh, t, dv), F32), jax.ShapeDtypeStruct((nh, t, LANES), F32)),
        compiler_params=_params("parallel", "parallel"),
    )(q, k, v)


def _attn_bwd(q, k, v, o, lse, do, *, name):
    nh, t, dk = q.shape
    dv = v.shape[-1]
    blk = min(ATT_BLOCK, t)
    nb = t // blk

    hb = ATT_HEAD_BATCH_BWD
    hs = range(hb)

    def body(q_ref, k_ref, v_ref, o_ref, l_ref, do_ref, dq_ref, dk_ref, dv_ref):
        j = pl.program_id(1)

        @pl.when(j == 0)
        def _():
            dq_ref[...] = jnp.zeros_like(dq_ref)

        kv = [k_ref[h] for h in hs]
        vv = [v_ref[h] for h in hs]

        def step(i, carry):
            dk_acc, dv_acc = carry[:hb], carry[hb:]
            off = pl.multiple_of(i * blk, blk)
            rows = pl.ds(off, blk)
            mask = _causal_mask(i, j, blk)
            qv = [q_ref[h, rows, :] for h in hs]
            dov = [do_ref[h, rows, :] for h in hs]
            s = [_dotb(qv[h], kv[h], NT) for h in hs]
            dp = [_dotb(dov[h], vv[h], NT) for h in hs]
            p = [jnp.exp(jnp.where(mask, s[h] * ATT_SCALE, NEG) - l_ref[h, rows, :][:, 0:1]) for h in hs]
            delta = [jnp.sum(dov[h] * o_ref[h, rows, :], axis=-1, keepdims=True) for h in hs]
            ds = [p[h] * (dp[h] - delta[h]) * ATT_SCALE for h in hs]
            dvn = [_dotb(p[h], dov[h], TN) for h in hs]
            dkn = [_dotb(ds[h], qv[h], TN) for h in hs]
            dqn = [_dotb(ds[h], kv[h], NN) for h in hs]
            for h in hs:
                dq_ref[h, rows, :] += dqn[h]
            return tuple(dk_acc[h] + dkn[h] for h in hs) + tuple(dv_acc[h] + dvn[h] for h in hs)

        out = lax.fori_loop(j, nb, step, (jnp.zeros((blk, dk), F32),) * hb + (jnp.zeros((blk, dv), F32),) * hb)
        for h in hs:
            dk_ref[h] = out[h]
            dv_ref[h] = out[hb + h]

    full = lambda w: pl.BlockSpec((hb, t, w), lambda h, j: (h, 0, 0))
    part = lambda w: pl.BlockSpec((hb, blk, w), lambda h, j: (h, j, 0))
    return pl.pallas_call(
        body, name=name, grid=(nh // hb, nb),
        in_specs=[full(dk), part(dk), part(dv), full(dv), full(LANES), full(dv)],
        out_specs=(full(dk), part(dk), part(dv)),
        out_shape=(jax.ShapeDtypeStruct((nh, t, dk), F32), jax.ShapeDtypeStruct((nh, t, dk), F32),
                   jax.ShapeDtypeStruct((nh, t, dv), F32)),
        compiler_params=_params("parallel", "arbitrary"),
    )(q, k, v, o, lse, do)


def _swap_halves(xv, first):
    return jnp.where(first, pltpu.roll(xv, LANES - MLA_ROPE // 2, 1), pltpu.roll(xv, MLA_ROPE // 2, 1))


def _rope_qk(qf, proj, cos_t, sin_t, *, name):
    t = qf.shape[0]
    nrope = MLA_HEADS * MLA_ROPE
    q_blk = MLA_HEADS * MLA_NOPE // nrope
    k_blk = (MLA_Q_RANK + MLA_KV_RANK) // LANES

    def body(q_ref, p_ref, c_ref, s_ref, qo_ref, ko_ref):
        cv, sv = c_ref[...], s_ref[...]
        lane = lax.broadcasted_iota(jnp.int32, (ROWS, LANES), 1)
        first = (lane % MLA_ROPE) < (MLA_ROPE // 2)
        for i in range(nrope // LANES):
            sl = slice(i * LANES, (i + 1) * LANES)
            xv = q_ref[:, sl].astype(F32)
            qo_ref[:, sl] = (xv * cv + _swap_halves(xv, first) * sv).astype(qo_ref.dtype)
        kv = jnp.where(lane < MLA_ROPE, p_ref[...], 0.0)
        ko_ref[...] = (kv * cv + _swap_halves(kv, first) * sv).astype(ko_ref.dtype)

    return pl.pallas_call(
        body, name=name, grid=(t // ROWS,),
        in_specs=[pl.BlockSpec((ROWS, nrope), lambda i: (i, q_blk)), pl.BlockSpec((ROWS, LANES), lambda i: (i, k_blk)),
                  _row_spec(LANES), _row_spec(LANES)],
        out_specs=(_row_spec(nrope), _row_spec(LANES)),
        out_shape=(jax.ShapeDtypeStruct((t, nrope), BF16), jax.ShapeDtypeStruct((t, LANES), BF16)),
        compiler_params=_params("parallel"),
    )(qf, proj, cos_t, sin_t)


def _rope_qk_bwd(dqr, dkr_parts, cos_t, sin_t, *, name):
    t, nrope = dqr.shape
    ng = dkr_parts.shape[0]

    def body(d_ref, k_ref, c_ref, s_ref, qo_ref, ko_ref):
        cv, sv = c_ref[...], s_ref[...]
        lane = lax.broadcasted_iota(jnp.int32, (ROWS, LANES), 1)
        first = (lane % MLA_ROPE) < (MLA_ROPE // 2)
        for i in range(nrope // LANES):
            sl = slice(i * LANES, (i + 1) * LANES)
            dv = d_ref[:, sl]
            qo_ref[:, sl] = (dv * cv + _swap_halves(dv * sv, first)).astype(qo_ref.dtype)
        dk = k_ref[0]
        for g in range(1, ng):
            dk = dk + k_ref[g]
        dk = jnp.where(lane < MLA_ROPE, dk, 0.0)
        ko_ref[...] = jnp.where(lane < MLA_ROPE, dk * cv + _swap_halves(dk * sv, first), 0.0)

    return pl.pallas_call(
        body, name=name, grid=(t // ROWS,),
        in_specs=[_row_spec(nrope), pl.BlockSpec((ng, ROWS, LANES), lambda i: (0, i, 0)), _row_spec(LANES),
                  _row_spec(LANES)],
        out_specs=(_row_spec(nrope), _row_spec(LANES)),
        out_shape=(jax.ShapeDtypeStruct((t, nrope), BF16), jax.ShapeDtypeStruct((t, LANES), F32)),
        compiler_params=_params("parallel"),
    )(dqr, dkr_parts, cos_t, sin_t)


def _attn_tm_fwd(qf, qr, kvf, kr, *, name):
    t = qf.shape[0]
    nh, dn, dr, dv = MLA_HEADS, MLA_NOPE, MLA_ROPE, MLA_V
    blk = min(ATT_BLOCK, t)
    hb = ATT_HEAD_BATCH
    hs = range(hb)

    def body(q_ref, qr_ref, kv_ref, kr_ref, o_ref, l_ref):
        i = pl.program_id(1)
        qn = [q_ref[:, h * dn:(h + 1) * dn].astype(MXU_DTYPE) for h in hs]
        qrh = [qr_ref[:, h * dr:(h + 1) * dr] for h in hs]

        def step(j, carry):
            m, l, acc = carry[:hb], carry[hb:2 * hb], carry[2 * hb:]
            rows = pl.ds(pl.multiple_of(j * blk, blk), blk)
            mask = _causal_mask(i, j, blk)
            krj = kr_ref[rows, 0:dr]
            s = [_dotb(qn[h], kv_ref[rows, h * (dn + dv):h * (dn + dv) + dn], NT) for h in hs]
            sr = [_dotb(qrh[h], krj, NT) for h in hs]
            s = [jnp.where(mask, (s[h] + sr[h]) * ATT_SCALE, NEG) for h in hs]
            m_new = [jnp.maximum(m[h], jnp.max(s[h], axis=-1, keepdims=True)) for h in hs]
            p = [jnp.exp(s[h] - m_new[h]) for h in hs]
            pv = [_dotb(p[h], kv_ref[rows, h * (dn + dv) + dn:(h + 1) * (dn + dv)], NN) for h in hs]
            alpha = [jnp.exp(m[h] - m_new[h]) for h in hs]
            l = [alpha[h] * l[h] + jnp.sum(p[h], axis=-1, keepdims=True) for h in hs]
            acc = [alpha[h] * acc[h] + pv[h] for h in hs]
            return tuple(m_new) + tuple(l) + tuple(acc)

        init = ((jnp.full((blk, 1), NEG, F32),) * hb + (jnp.zeros((blk, 1), F32),) * hb
                + (jnp.zeros((blk, dv), F32),) * hb)
        out = lax.fori_loop(0, i + 1, step, init)
        for h in hs:
            m, l, acc = out[h], out[hb + h], out[2 * hb + h]
            o_ref[:, h * dv:(h + 1) * dv] = (acc / l).astype(o_ref.dtype)
            l_ref[h] = jnp.broadcast_to(m + jnp.log(l), (blk, LANES))

    return pl.pallas_call(
        body, name=name, grid=(nh // hb, t // blk),
        in_specs=[pl.BlockSpec((blk, hb * dn), lambda g, i: (i, g)), pl.BlockSpec((blk, hb * dr), lambda g, i: (i, g)),
                  pl.BlockSpec((t, hb * (dn + dv)), lambda g, i: (0, g)), pl.BlockSpec((t, LANES), lambda g, i: (0, 0))],
        out_specs=(pl.BlockSpec((blk, hb * dv), lambda g, i: (i, g)),
                   pl.BlockSpec((hb, blk, LANES), lambda g, i: (g, i, 0))),
        out_shape=(jax.ShapeDtypeStruct((t, nh * dv), BF16), jax.ShapeDtypeStruct((nh, t, LANES), F32)),
        compiler_params=_params("parallel", "parallel"),
    )(qf, qr, kvf, kr)


def _attn_tm_bwd(qf, qr, kvf, kr, o, lse, do, *, name):
    t = qf.shape[0]
    nh, dn, dr, dv = MLA_HEADS, MLA_NOPE, MLA_ROPE, MLA_V
    blk = min(ATT_BLOCK, t)
    nb = t // blk
    hb = ATT_HEAD_BATCH_BWD
    hs = range(hb)
    ng = nh // hb

    def body(q_ref, qr_ref, kv_ref, kr_ref, o_ref, l_ref, do_ref, dqn_ref, dqr_ref, dkv_ref, dkr_ref):
        j = pl.program_id(1)

        @pl.when(j == 0)
        def _():
            dqn_ref[...] = jnp.zeros_like(dqn_ref)
            dqr_ref[...] = jnp.zeros_like(dqr_ref)

        kn = [kv_ref[:, h * (dn + dv):h * (dn + dv) + dn] for h in hs]
        vv = [kv_ref[:, h * (dn + dv) + dn:(h + 1) * (dn + dv)] for h in hs]
        krj = kr_ref[:, 0:dr]

        def step(i, carry):
            dkn_acc, dv_acc, dkr_acc = carry[:hb], carry[hb:2 * hb], carry[2 * hb]
            rows = pl.ds(pl.multiple_of(i * blk, blk), blk)
            mask = _causal_mask(i, j, blk)
            qn = [q_ref[rows, h * dn:(h + 1) * dn].astype(MXU_DTYPE) for h in hs]
            qrh = [qr_ref[rows, h * dr:(h + 1) * dr] for h in hs]
            dov = [do_ref[rows, h * dv:(h + 1) * dv] for h in hs]
            s = [_dotb(qn[h], kn[h], NT) for h in hs]
            sr = [_dotb(qrh[h], krj, NT) for h in hs]
            dp = [_dotb(dov[h], vv[h], NT) for h in hs]
            p = [jnp.exp(jnp.where(mask, (s[h] + sr[h]) * ATT_SCALE, NEG) - l_ref[h, rows, :][:, 0:1]) for h in hs]
            delta = [jnp.sum(dov[h].astype(F32) * o_ref[rows, h * dv:(h + 1) * dv].astype(F32), axis=-1, keepdims=True)
                     for h in hs]
            ds = [p[h] * (dp[h] - delta[h]) * ATT_SCALE for h in hs]
            dvn = [_dotb(p[h], dov[h], TN) for h in hs]
            dknn = [_dotb(ds[h], qn[h], TN) for h in hs]
            dkrn = [_dotb(ds[h], qrh[h], TN) for h in hs]
            dqnn = [_dotb(ds[h], kn[h], NN) for h in hs]
            dqrn = [_dotb(ds[h], krj, NN) for h in hs]
            for h in hs:
                dqn_ref[rows, h * dn:(h + 1) * dn] += dqnn[h]
                dqr_ref[rows, h * dr:(h + 1) * dr] += dqrn[h]
            dkr_new = dkr_acc
            for h in hs:
                dkr_new = dkr_new + dkrn[h]
            return (tuple(dkn_acc[h] + dknn[h] for h in hs) + tuple(dv_acc[h] + dvn[h] for h in hs) + (dkr_new,))

        init = (jnp.zeros((blk, dn), F32),) * hb + (jnp.zeros((blk, dv), F32),) * hb + (jnp.zeros((blk, dr), F32),)
        out = lax.fori_loop(j, nb, step, init)
        for h in hs:
            dkv_ref[:, h * (dn + dv):h * (dn + dv) + dn] = out[h].astype(dkv_ref.dtype)
            dkv_ref[:, h * (dn + dv) + dn:(h + 1) * (dn + dv)] = out[hb + h].astype(dkv_ref.dtype)
        dkr_ref[0, :, 0:dr] = out[2 * hb]
        dkr_ref[0, :, dr:LANES] = jnp.zeros((blk, LANES - dr), F32)

    full = lambda w: pl.BlockSpec((t, w), lambda g, j: (0, g))
    return pl.pallas_call(
        body, name=name, grid=(ng, nb),
        in_specs=[full(hb * dn), full(hb * dr), pl.BlockSpec((blk, hb * (dn + dv)), lambda g, j: (j, g)),
                  pl.BlockSpec((blk, LANES), lambda g, j: (j, 0)), full(hb * dv),
                  pl.BlockSpec((hb, t, LANES), lambda g, j: (g, 0, 0)), full(hb * dv)],
        out_specs=(full(hb * dn), full(hb * dr), pl.BlockSpec((blk, hb * (dn + dv)), lambda g, j: (j, g)),
                   pl.BlockSpec((1, blk, LANES), lambda g, j: (g, j, 0))),
        out_shape=(jax.ShapeDtypeStruct((t, nh * dn), F32), jax.ShapeDtypeStruct((t, nh * dr), F32),
                   jax.ShapeDtypeStruct((t, nh * (dn + dv)), BF16), jax.ShapeDtypeStruct((ng, t, LANES), F32)),
        compiler_params=_params("parallel", "arbitrary"),
    )(qf, qr, kvf, kr, o, lse, do)


def _ada_mod(c_all, ada_w, ada_b_cols, *, name):
    nl, d, wc = ada_w.shape

    def body(c_ref, w_ref, b_ref, o_ref):
        cv = c_ref[...]
        o_ref[0] = _dotb(cv * _sigmoid(cv), w_ref[0], NN) + b_ref[0]

    return pl.pallas_call(
        body, name=name, grid=(nl,),
        in_specs=[_const_spec((N_DEV, d)), pl.BlockSpec((1, d, wc), lambda l: (l, 0, 0)),
                  pl.BlockSpec((1, 1, wc), lambda l: (l, 0, 0))],
        out_specs=pl.BlockSpec((1, N_DEV, wc), lambda l: (l, 0, 0)),
        out_shape=jax.ShapeDtypeStruct((nl, N_DEV, wc), F32), compiler_params=_params("parallel"),
    )(c_all, ada_w, ada_b_cols)


def _adam_math(g, w, m, v):
    m2 = ADAM_B1 * m + (1.0 - ADAM_B1) * g
    v2 = ADAM_B2 * v + (1.0 - ADAM_B2) * (g * g)
    delta = -ADAM_LR * ((m2 / ADAM_BC1) / (jnp.sqrt(v2 / ADAM_BC2) + ADAM_EPS) + ADAM_WD * w)
    return delta, m2, v2


def _ada_grad_adamw(c_all, dmod_cols, w, m, v, *, name):
    nl, d, wc = w.shape
    tr = 256

    def body(c_ref, dm_ref, w_ref, m_ref, v_ref, g_ref, d_ref, m2_ref, v2_ref):
        cv = c_ref[...]
        g = _dotf(cv * _sigmoid(cv), dm_ref[0], TN)
        delta, m2, v2 = _adam_math(g, w_ref[0], m_ref[0], v_ref[0])
        g_ref[0], d_ref[0], m2_ref[0], v2_ref[0] = g, delta, m2, v2

    blk = pl.BlockSpec((1, tr, wc), lambda l, i: (l, i, 0))
    return pl.pallas_call(
        body, name=name, grid=(nl, d // tr),
        in_specs=[pl.BlockSpec((N_DEV, tr), lambda l, i: (0, i)), pl.BlockSpec((1, N_DEV, wc), lambda l, i: (l, 0, 0)),
                  blk, blk, blk],
        out_specs=(blk,) * 4, out_shape=(jax.ShapeDtypeStruct(w.shape, F32),) * 4,
        compiler_params=_params("parallel", "parallel"),
    )(c_all, dmod_cols, w, m, v)


def _adamw(parts, w, m, v, *, name):
    nl, r, c = w.shape
    ns = parts[0].shape[0]
    lanes_padded = -(-c // LANES) * LANES
    row_bytes = 2 * nl * ns * lanes_padded * parts[0].dtype.itemsize
    tr = _pick(r, min(256, max(16, (VMEM_LIMIT // 2) // row_bytes)), 16)
    tc = c
    if tr * row_bytes > VMEM_LIMIT // 2:
        tc = _pick(c, max(LANES, c * (VMEM_LIMIT // 2) // (tr * row_bytes)))

    def body(*refs):
        p_refs = refs[:nl]
        w_ref, m_ref, v_ref, g_ref, d_ref, m2_ref, v2_ref = refs[nl:]
        layer = pl.program_id(0)
        for q in range(nl):
            @pl.when(layer == q)
            def _(q=q):
                g = p_refs[q][0].astype(F32)
                for s in range(1, ns):
                    g = g + p_refs[q][s].astype(F32)
                delta, m2, v2 = _adam_math(g, w_ref[0], m_ref[0], v_ref[0])
                g_ref[0], d_ref[0], m2_ref[0], v2_ref[0] = g, delta, m2, v2

    blk = pl.BlockSpec((1, tr, tc), lambda l, i, j: (l, i, j))
    p_specs = [pl.BlockSpec((ns, tr, tc), lambda l, i, j, q=q: (0, jnp.where(l == q, i, 0), jnp.where(l == q, j, 0)))
               for q in range(nl)]
    return pl.pallas_call(
        body, name=name, grid=(nl, r // tr, c // tc),
        in_specs=p_specs + [blk, blk, blk],
        out_specs=(blk,) * 4, out_shape=(jax.ShapeDtypeStruct(w.shape, F32),) * 4,
        compiler_params=_params("arbitrary", "arbitrary", "arbitrary"),
    )(*parts, w, m, v)


def _sum_parts(parts, *, name):
    ns, r, c = parts.shape

    def body(p_ref, o_ref):
        acc = p_ref[0]
        for s in range(1, ns):
            acc = acc + p_ref[s]
        o_ref[...] = acc

    return pl.pallas_call(
        body, name=name, out_shape=jax.ShapeDtypeStruct((r, c), F32),
        in_specs=[pl.BlockSpec(memory_space=pltpu.VMEM)], out_specs=pl.BlockSpec(memory_space=pltpu.VMEM),
    )(parts)


def _pack(arrs):
    flat = jnp.concatenate([a.reshape(-1).astype(F32) for a in arrs])
    pad = (-flat.shape[0]) % (8 * LANES)
    return jnp.pad(flat, (0, pad)).reshape(-1, LANES)


def _unpack(packed, shapes, lead=()):
    flat = packed.reshape(lead + (-1,))
    out, off = [], 0
    for s in shapes:
        n = math.prod(s)
        out.append(flat[..., off:off + n].reshape(lead + tuple(s)))
        off += n
    return out


def _gather_cols(g):
    _, nl, r, cs = g.shape
    return jnp.transpose(g, (1, 2, 0, 3)).reshape(nl, r, N_DEV * cs)


def _gather_rows(g):
    _, nl, rs, c = g.shape
    return jnp.transpose(g, (1, 0, 2, 3)).reshape(nl, N_DEV * rs, c)


def _scatter_cols(full):
    nl, r, c = full.shape
    return jnp.transpose(full.reshape(nl, r, N_DEV, c // N_DEV), (2, 0, 1, 3))


def _scatter_rows(full):
    nl, r, c = full.shape
    return jnp.transpose(full.reshape(nl, N_DEV, r // N_DEV, c), (1, 0, 2, 3))


def _row(v):
    return v.reshape(1, -1)


def _local_step(x, target, mod, cos_t, sin_t, rep, get_weights, put_grads):
    t = x.shape[0]
    saved = []
    for layer in range(DEPTH):
        j = layer // 2
        tag = f"l{layer}"
        shift_m, scale_m, gate_m, shift_f, scale_f, gate_f = [_row(mod[layer, i]) for i in range(N_MOD)]
        lw = dict(get_weights(layer, "mix", x))
        rec = {"x0": x, "lw": lw}
        h = _adaln_fwd(x, _row(rep["norm_mix_g"][layer]), scale_m, shift_m, name=f"adaln_mix_{tag}")
        rec["h"] = h
        if layer % 2 == 0:
            proj = _mm(h, lw["wt_in"], mode="nt", out_dtype=F32, tm=256, tn=GDN_MAIN, b_rows=GDN_MAIN,
                       dep=lw["dep_mix"], name=f"gdn_in_{tag}")
            ab = _mm(h, lw["wt_ab"], mode="nt", out_dtype=F32, name=f"gdn_in_ab_{tag}")
            qkv = _gdn_prep_fwd(proj, rep["gdn_conv_wt"][j], name=f"gdn_prep_{tag}")
            gbeta = _gdn_gate_fwd(ab, rep["gdn_gate_prm"][j], name=f"gdn_gate_{tag}")
            gbc = jnp.broadcast_to(jnp.transpose(gbeta[:, 0:GDN_HEADS])[:, :, None], (GDN_HEADS, t, LANES))
            bbc = jnp.broadcast_to(jnp.transpose(gbeta[:, GDN_HEADS:2 * GDN_HEADS])[:, :, None],
                                   (GDN_HEADS, t, LANES))
            o, states = _gdn_chunk_fwd(qkv, gbc, bbc, name=f"gdn_chunk_{tag}")
            og = _gdn_onorm_fwd(o, proj, _row(rep["gdn_norm_g"][j]), name=f"gdn_onorm_{tag}")
            x, y = _mm_resid(og, lw["w_out"], x, gate_m, name=f"gdn_out_{tag}")
            rec.update(proj=proj, ab=ab, qkv=qkv, gbc=gbc, bbc=bbc, states=states, o=o, og=og, y=y)
        else:
            proj = _mm(h, lw["w_in"], mode="nn", out_dtype=F32, dep=lw["dep_mix"], name=f"mla_in_{tag}")
            cq, ck = _mla_prep_fwd(proj, _row(rep["mla_q_norm_g"][j]), _row(rep["mla_kv_norm_g"][j]),
                                   name=f"mla_prep_{tag}")
            qf = _mm(cq, lw["wt_uq"], mode="nt", out_dtype=BF16, name=f"mla_uq_{tag}")
            kvf = _mm(ck, lw["w_ukv"], mode="nn", out_dtype=BF16, name=f"mla_ukv_{tag}")
            qr, kr = _rope_qk(qf, proj, cos_t, sin_t, name=f"rope_{tag}")
            oc, lse = _attn_tm_fwd(qf, qr, kvf, kr, name=f"attn_{tag}")
            x, y = _mm_resid(oc, lw["w_out"], x, gate_m, name=f"mla_out_{tag}")
            rec.update(proj=proj, cq=cq, ck=ck, qf=qf, qr=qr, kvf=kvf, kr=kr, lse=lse, oc=oc, y=y)
        rec["x1"] = x
        lw.update(get_weights(layer, "ffn", x))
        h2 = _adaln_fwd(x, _row(rep["norm_ffn_g"][layer]), scale_f, shift_f, name=f"adaln_ffn_{tag}")
        s, a2, b2 = _ffn_gu_fwd(h2, lw["wt_g"], lw["wt_u"], lw["dep_ffn"], name=f"ffn_gu_{tag}")
        x, y2 = _mm_resid(s, lw["w_down"], x, gate_f, name=f"ffn_down_{tag}")
        rec.update(h2=h2, a2=a2, b2=b2, s=s, y2=y2)
        saved.append(rec)

    dx, st, ls = _loss_head(x, _row(rep["final_norm_g"]), target, name="loss_head")
    loss = ls[0, 0]
    grads = {"final_norm_g": st[0]}
    per_layer = {k: [None] * DEPTH for k in ("norm_mix_g", "norm_ffn_g")}
    per_gdn = {k: [None] * 2 for k in ("gdn_conv_wt", "gdn_a_log", "gdn_dt_bias", "gdn_norm_g")}
    per_mla = {k: [None] * 2 for k in ("mla_q_norm_g", "mla_kv_norm_g")}
    dmod = [None] * DEPTH
    dep = jnp.zeros((8, LANES), F32)

    for layer in reversed(range(DEPTH)):
        j = layer // 2
        tag = f"l{layer}"
        rec = saved[layer]
        lw = rec["lw"]
        shift_m, scale_m, gate_m, shift_f, scale_f, gate_f = [_row(mod[layer, i]) for i in range(N_MOD)]
        dy2, st_g = _gate_bwd(dx, rec["y2"], gate_f, dep, name=f"gate_bwd_ffn_{tag}")
        dgate_f = st_g[0]
        dw_down = _mm(rec["s"], dy2, mode="tn", out_dtype=BF16, tm=256, tn=1024, name=f"ffn_down_dw_{tag}")
        da2, db2 = _ffn_down_dx(dy2, lw["w_down"], rec["a2"], rec["b2"], name=f"ffn_down_dx_{tag}")
        dwt_g = _mm(da2, rec["h2"], mode="tn", out_dtype=BF16, tm=256, tn=1024, name=f"ffn_g_dw_{tag}")
        dwt_u = _mm(db2, rec["h2"], mode="tn", out_dtype=BF16, tm=256, tn=1024, name=f"ffn_u_dw_{tag}")
        dep = put_grads(layer, "ffn", {"wt_g": dwt_g, "wt_u": dwt_u, "w_down": dw_down})
        dh2 = _mm(da2, lw["wt_g"], mode="nn", out_dtype=F32, tm=256, tn=1024, name=f"ffn_g_dx_{tag}")
        dh2 = _mm(db2, lw["wt_u"], mode="nn", out_dtype=BF16, add=dh2, tm=256, tn=1024, name=f"ffn_u_dx_{tag}")
        dx, st_n = _adaln_bwd(rec["x1"], _row(rep["norm_ffn_g"][layer]), scale_f, shift_f, dh2, dx, dep,
                              name=f"adaln_ffn_bwd_{tag}")
        per_layer["norm_ffn_g"][layer] = st_n[0]
        dscale_f, dshift_f = st_n[1], st_n[2]
        dy, st_g = _gate_bwd(dx, rec["y"], gate_m, dep, name=f"gate_bwd_mix_{tag}")
        dgate_m = st_g[0]
        big = {}
        if layer % 2 == 0:
            big["w_out"] = _mm(rec["og"], dy, mode="tn", out_dtype=BF16, name=f"gdn_out_dw_{tag}")
            dog = _mm(dy, lw["w_out"], mode="nt", out_dtype=BF16, name=f"gdn_out_dx_{tag}")
            do, dgp, st_o = _gdn_onorm_bwd(rec["o"], rec["proj"], _row(rep["gdn_norm_g"][j]), dog,
                                           name=f"gdn_onorm_bwd_{tag}")
            per_gdn["gdn_norm_g"][j] = st_o[0]
            dqkv, dgc_, dbc_ = _gdn_chunk_bwd(rec["qkv"], rec["gbc"], rec["bbc"], rec["states"], do,
                                               name=f"gdn_chunk_bwd_{tag}")
            dgb = jnp.concatenate([jnp.transpose(dgc_[:, :, 0]), jnp.transpose(dbc_[:, :, 0])], axis=1)
            dgb = jnp.pad(dgb, ((0, 0), (0, LANES - 2 * GDN_HEADS)))
            dab, st_a = _gdn_gate_bwd(rec["ab"], rep["gdn_gate_prm"][j], dgb, name=f"gdn_gate_bwd_{tag}")
            per_gdn["gdn_a_log"][j] = st_a[0, :GDN_HEADS]
            per_gdn["gdn_dt_bias"][j] = st_a[1, :GDN_HEADS]
            dpre, dcw = _gdn_prep_bwd(rec["proj"], rep["gdn_conv_wt"][j], dqkv, name=f"gdn_prep_bwd_{tag}")
            per_gdn["gdn_conv_wt"][j] = dcw
            dproj = jnp.concatenate([dpre, dgp], axis=1)
            dw_main = _mm(dproj, rec["h"], mode="tn", out_dtype=BF16, tm=512, tn=1024, name=f"gdn_in_dw_{tag}")
            dw_ab = _mm(dab, rec["h"], mode="tn", out_dtype=BF16, tn=1024, name=f"gdn_in_ab_dw_{tag}")
            big["wt_in"] = jnp.concatenate([dw_main, dw_ab[:2 * GDN_HEADS]], axis=0)
            dep = put_grads(layer, "gdn", big)
            dh_ab = _mm(dab, lw["wt_ab"], mode="nn", out_dtype=F32, tn=1024, name=f"gdn_in_ab_dx_{tag}")
            dh = _mm(dproj, lw["wt_in"], mode="nn", out_dtype=BF16, add=dh_ab, tm=256, tn=1024, b_rows=GDN_MAIN,
                     name=f"gdn_in_dx_{tag}")
        else:
            big["w_out"] = _mm(rec["oc"], dy, mode="tn", out_dtype=BF16, name=f"mla_out_dw_{tag}")
            doc = _mm(dy, lw["w_out"], mode="nt", out_dtype=BF16, name=f"mla_out_dx_{tag}")
            dqn, dqr, dkvf, dkr_parts = _attn_tm_bwd(rec["qf"], rec["qr"], rec["kvf"], rec["kr"], rec["oc"],
                                                     rec["lse"], doc, name=f"attn_bwd_{tag}")
            dqr_un, dkr_un = _rope_qk_bwd(dqr, dkr_parts, cos_t, sin_t, name=f"rope_bwd_{tag}")
            n_nope = MLA_HEADS * MLA_NOPE
            big["wt_uq"] = jnp.concatenate(
                [_mm(dqn, rec["cq"], mode="tn", out_dtype=BF16, name=f"mla_uq_dw_nope_{tag}"),
                 _mm(dqr_un, rec["cq"], mode="tn", out_dtype=BF16, name=f"mla_uq_dw_rope_{tag}")], axis=0)
            big["w_ukv"] = _mm(rec["ck"], dkvf, mode="tn", out_dtype=BF16, name=f"mla_ukv_dw_{tag}")
            dcq = _mm(dqr_un, lw["wt_uq"][n_nope:], mode="nn", out_dtype=F32, name=f"mla_uq_dx_rope_{tag}")
            dcq = _mm(dqn, lw["wt_uq"], mode="nn", out_dtype=F32, add=dcq, b_rows=n_nope,
                      name=f"mla_uq_dx_nope_{tag}")
            dck = _mm(dkvf, lw["w_ukv"], mode="nt", out_dtype=F32, name=f"mla_ukv_dx_{tag}")
            dproj, st_p = _mla_prep_bwd(rec["proj"], _row(rep["mla_q_norm_g"][j]), _row(rep["mla_kv_norm_g"][j]),
                                        dcq, dck, dkr_un, name=f"mla_prep_bwd_{tag}")
            per_mla["mla_q_norm_g"][j] = st_p[0, :MLA_Q_RANK]
            per_mla["mla_kv_norm_g"][j] = st_p[0, MLA_Q_RANK:MLA_Q_RANK + MLA_KV_RANK]
            big["w_in"] = _mm(rec["h"], dproj, mode="tn", out_dtype=BF16, name=f"mla_in_dw_{tag}")
            dep = put_grads(layer, "mla", big)
            dh = _mm(dproj, lw["w_in"], mode="nt", out_dtype=BF16, name=f"mla_in_dx_{tag}")
        dx, st_n = _adaln_bwd(rec["x0"], _row(rep["norm_mix_g"][layer]), scale_m, shift_m, dh, dx, dep,
                              name=f"adaln_mix_bwd_{tag}")
        per_layer["norm_mix_g"][layer] = st_n[0]
        dmod[layer] = jnp.stack([st_n[2], st_n[1], dgate_m, dshift_f, dscale_f, dgate_f])

    for d in (per_layer, per_gdn, per_mla):
        for k, v in d.items():
            grads[k] = jnp.stack(v)
    return loss, dx, jnp.stack(dmod), grads


BIG = ("gdn_w_in", "gdn_w_out", "mla_w_in", "mla_w_uq", "mla_w_ukv", "mla_w_out", "ffn_w_gate", "ffn_w_up",
       "ffn_w_down")
TRANSPOSED = ("gdn_w_in", "mla_w_uq", "ffn_w_gate", "ffn_w_up")
AHEAD = 2


def _view(k, a):
    return jnp.transpose(a, (0, 2, 1)) if k in TRANSPOSED else a
SMALL = ("ada_b", "norm_mix_g", "norm_ffn_g", "gdn_conv_w", "gdn_a_log", "gdn_dt_bias", "gdn_norm_g",
         "mla_q_norm_g", "mla_kv_norm_g", "final_norm_g")
WEIGHTS = ("ada_w", "ada_b", "norm_mix_g", "norm_ffn_g", "gdn_w_in", "gdn_conv_w", "gdn_a_log", "gdn_dt_bias",
           "gdn_norm_g", "gdn_w_out", "mla_w_in", "mla_q_norm_g", "mla_kv_norm_g", "mla_w_uq", "mla_w_ukv",
           "mla_w_out", "ffn_w_gate", "ffn_w_up", "ffn_w_down", "final_norm_g")


def _uq_to_kernel_layout(w, axis=-1):
    axis = axis % w.ndim
    lead, tail = w.shape[:axis], w.shape[axis + 1:]
    w4 = w.reshape(lead + (MLA_HEADS, MLA_QK) + tail)
    nope = lax.slice_in_dim(w4, 0, MLA_NOPE, axis=axis + 1).reshape(lead + (-1,) + tail)
    rope = lax.slice_in_dim(w4, MLA_NOPE, MLA_QK, axis=axis + 1).reshape(lead + (-1,) + tail)
    return jnp.concatenate([nope, rope], axis=axis)


def _uq_from_kernel_layout(w, axis=-1):
    axis = axis % w.ndim
    lead, tail = w.shape[:axis], w.shape[axis + 1:]
    nope = lax.slice_in_dim(w, 0, MLA_HEADS * MLA_NOPE, axis=axis).reshape(lead + (MLA_HEADS, MLA_NOPE) + tail)
    rope = lax.slice_in_dim(w, MLA_HEADS * MLA_NOPE, MLA_HEADS * MLA_QK, axis=axis).reshape(
        lead + (MLA_HEADS, MLA_ROPE) + tail)
    return jnp.concatenate([nope, rope], axis=axis + 1).reshape(lead + (-1,) + tail)


def _group_names(layer, kind):
    if kind == "ffn":
        return ("ffn_w_gate", "ffn_w_up", "ffn_w_down")
    return ("gdn_w_in", "gdn_w_out") if layer % 2 == 0 else ("mla_w_in", "mla_w_uq", "mla_w_ukv", "mla_w_out")


def _layer_index(name, layer):
    return layer if name.startswith("ffn") else layer // 2


def _cols(g):
    return jnp.transpose(g, (1, 0, 2)).reshape(g.shape[1], N_DEV * g.shape[2])


def _rows(g):
    return g.reshape(N_DEV * g.shape[1], g.shape[2])


def _uncols(full):
    r, c = full.shape
    return jnp.transpose(full.reshape(r, N_DEV, c // N_DEV), (1, 0, 2))


def _unrows(full):
    r, c = full.shape
    return full.reshape(N_DEV, r // N_DEV, c)


def _group_weights(layer, kind, got, token):
    if kind == "ffn":
        return {"wt_g": _rows(got["ffn_w_gate"]), "wt_u": _rows(got["ffn_w_up"]), "w_down": _rows(got["ffn_w_down"]),
                "dep_ffn": token}
    if layer % 2 == 0:
        wt_in = _rows(got["gdn_w_in"])
        return dict(wt_in=wt_in, wt_ab=jnp.pad(wt_in[GDN_MAIN:], ((0, LANES - 2 * GDN_HEADS), (0, 0))),
                    w_out=_rows(got["gdn_w_out"]), dep_mix=token)
    return dict(w_in=_rows(got["mla_w_in"]), wt_uq=_uq_to_kernel_layout(_rows(got["mla_w_uq"]), axis=0),
                w_ukv=_cols(got["mla_w_ukv"]), w_out=_rows(got["mla_w_out"]), dep_mix=token)


def _layer_grad_slots(kind, big):
    if kind == "ffn":
        return {"ffn_w_gate": _unrows(big["wt_g"]), "ffn_w_up": _unrows(big["wt_u"]),
                "ffn_w_down": _unrows(big["w_down"])}
    if kind == "gdn":
        return {"gdn_w_in": _unrows(big["wt_in"]), "gdn_w_out": _unrows(big["w_out"])}
    return {"mla_w_in": _unrows(big["w_in"]), "mla_w_uq": _unrows(_uq_from_kernel_layout(big["wt_uq"], axis=0)),
            "mla_w_ukv": _uncols(big["w_ukv"]), "mla_w_out": _unrows(big["w_out"])}


def _small_weights(tiny, rep):
    prm = jnp.zeros((2, 8, LANES), F32)
    prm = prm.at[:, 0, :GDN_HEADS].set(rep["gdn_a_log"]).at[:, 1, :GDN_HEADS].set(rep["gdn_dt_bias"])
    out = {
        "gdn_conv_wt": jnp.transpose(_gather_rows(tiny["gdn_conv_w"]), (0, 2, 1)),
        "mla_q_norm_g": jnp.transpose(tiny["mla_q_norm_g"], (1, 0, 2)).reshape(2, MLA_Q_RANK),
        "mla_kv_norm_g": jnp.transpose(tiny["mla_kv_norm_g"], (1, 0, 2)).reshape(2, MLA_KV_RANK),
        "gdn_gate_prm": prm,
    }
    for k in ("norm_mix_g", "norm_ffn_g", "gdn_norm_g", "final_norm_g"):
        out[k] = rep[k]
    return out


def _rope_tables(positions):
    inv_freq = ROPE_THETA ** (-jnp.arange(0, MLA_ROPE, 2, dtype=F32) / MLA_ROPE)
    ang = positions.astype(F32)[:, None] * inv_freq
    cos, sin = jnp.cos(ang), jnp.sin(ang)
    reps = LANES // MLA_ROPE
    return jnp.tile(jnp.concatenate([cos, cos], axis=1), (1, reps)), jnp.tile(
        jnp.concatenate([-sin, sin], axis=1), (1, reps))


def kernel(x, c, positions, ada_w, ada_b, norm_mix_g, norm_ffn_g, gdn_w_in, gdn_conv_w, gdn_a_log, gdn_dt_bias, gdn_norm_g, gdn_w_out, mla_w_in, mla_q_norm_g, mla_kv_norm_g, mla_w_uq, mla_w_ukv, mla_w_out, ffn_w_gate, ffn_w_up, ffn_w_down, final_norm_g, loss_target, m_ada_w, m_ada_b, m_norm_mix_g, m_norm_ffn_g, m_gdn_w_in, m_gdn_conv_w, m_gdn_a_log, m_gdn_dt_bias, m_gdn_norm_g, m_gdn_w_out, m_mla_w_in, m_mla_q_norm_g, m_mla_kv_norm_g, m_mla_w_uq, m_mla_w_ukv, m_mla_w_out, m_ffn_w_gate, m_ffn_w_up, m_ffn_w_down, m_final_norm_g, v_ada_w, v_ada_b, v_norm_mix_g, v_norm_ffn_g, v_gdn_w_in, v_gdn_conv_w, v_gdn_a_log, v_gdn_dt_bias, v_gdn_norm_g, v_gdn_w_out, v_mla_w_in, v_mla_q_norm_g, v_mla_kv_norm_g, v_mla_w_uq, v_mla_w_ukv, v_mla_w_out, v_ffn_w_gate, v_ffn_w_up, v_ffn_w_down, v_final_norm_g):
    W = dict(ada_w=ada_w, ada_b=ada_b, norm_mix_g=norm_mix_g, norm_ffn_g=norm_ffn_g, gdn_w_in=gdn_w_in,
             gdn_conv_w=gdn_conv_w, gdn_a_log=gdn_a_log, gdn_dt_bias=gdn_dt_bias, gdn_norm_g=gdn_norm_g,
             gdn_w_out=gdn_w_out, mla_w_in=mla_w_in, mla_q_norm_g=mla_q_norm_g, mla_kv_norm_g=mla_kv_norm_g,
             mla_w_uq=mla_w_uq, mla_w_ukv=mla_w_ukv, mla_w_out=mla_w_out, ffn_w_gate=ffn_w_gate,
             ffn_w_up=ffn_w_up, ffn_w_down=ffn_w_down, final_norm_g=final_norm_g)
    M = dict(ada_w=m_ada_w, ada_b=m_ada_b, norm_mix_g=m_norm_mix_g, norm_ffn_g=m_norm_ffn_g, gdn_w_in=m_gdn_w_in,
             gdn_conv_w=m_gdn_conv_w, gdn_a_log=m_gdn_a_log, gdn_dt_bias=m_gdn_dt_bias, gdn_norm_g=m_gdn_norm_g,
             gdn_w_out=m_gdn_w_out, mla_w_in=m_mla_w_in, mla_q_norm_g=m_mla_q_norm_g,
             mla_kv_norm_g=m_mla_kv_norm_g, mla_w_uq=m_mla_w_uq, mla_w_ukv=m_mla_w_ukv, mla_w_out=m_mla_w_out,
             ffn_w_gate=m_ffn_w_gate, ffn_w_up=m_ffn_w_up, ffn_w_down=m_ffn_w_down, final_norm_g=m_final_norm_g)
    V = dict(ada_w=v_ada_w, ada_b=v_ada_b, norm_mix_g=v_norm_mix_g, norm_ffn_g=v_norm_ffn_g, gdn_w_in=v_gdn_w_in,
             gdn_conv_w=v_gdn_conv_w, gdn_a_log=v_gdn_a_log, gdn_dt_bias=v_gdn_dt_bias, gdn_norm_g=v_gdn_norm_g,
             gdn_w_out=v_gdn_w_out, mla_w_in=v_mla_w_in, mla_q_norm_g=v_mla_q_norm_g,
             mla_kv_norm_g=v_mla_kv_norm_g, mla_w_uq=v_mla_w_uq, mla_w_ukv=v_mla_w_ukv, mla_w_out=v_mla_w_out,
             ffn_w_gate=v_ffn_w_gate, ffn_w_up=v_ffn_w_up, ffn_w_down=v_ffn_w_down, final_norm_g=v_final_norm_g)
    me = 4 * lax.axis_index("x") + 2 * lax.axis_index("y") + lax.axis_index("c")
    t = x.shape[1]
    wc = ada_w.shape[-1]

    groups = [(layer, kind) for layer in range(DEPTH) for kind in ("mix", "ffn")]

    def group_srcs(i):
        layer, kind = groups[i]
        return [_view(k, W[k])[_layer_index(k, layer)].astype(BF16) for k in _group_names(layer, kind)]

    tiny_shapes = [c.shape, gdn_conv_w.shape, mla_q_norm_g.shape, mla_kv_norm_g.shape]
    first = _gather_two_level([_pack([c, gdn_conv_w, mla_q_norm_g, mla_kv_norm_g])] + group_srcs(0),
                              name="gather_first")
    tiny_g = first[0]
    c_g, conv_g, qn_g, kvn_g = _unpack(tiny_g, tiny_shapes, lead=(N_DEV,))
    c_all = c_g.reshape(N_DEV, D_MODEL)
    rep = _small_weights({"gdn_conv_w": conv_g, "mla_q_norm_g": qn_g, "mla_kv_norm_g": kvn_g}, W)

    def start_group(i, dep):
        layer, kind = groups[i]
        return _exchange_start(group_srcs(i), scatter=False, name=f"gather_start_{kind}_l{layer}", dep=dep)


    b_cols = lax.dynamic_slice_in_dim(ada_b, me * wc, wc, axis=1).reshape(DEPTH, 1, wc)
    mod_part = _ada_mod(c_all, ada_w, b_cols, name="ada_mod")
    (mod_g,) = _exchange([mod_part], scatter=False, name="gather_mod")
    mod_mine = lax.dynamic_index_in_dim(mod_g, me, axis=2, keepdims=False)
    mod = jnp.transpose(mod_mine, (1, 0, 2)).reshape(DEPTH, N_MOD, D_MODEL)
    gather = {1: start_group(1, mod_g)}
    for i in range(2, AHEAD + 1):
        gather[i] = start_group(i, gather[i - 1][4])

    def get_weights(layer, kind, after):
        i = groups.index((layer, kind))
        names = _group_names(layer, kind)
        if i == 0:
            return _group_weights(layer, kind, dict(zip(names, first[1:])), gather[AHEAD][4])
        srcs, lands = _exchange_wait(gather[i], after, scatter=False, name=f"gather_wait_{kind}_l{layer}")
        token = jnp.zeros((8, LANES), F32)
        if i + AHEAD < len(groups):
            gather[i + AHEAD] = start_group(i + AHEAD, lands[0])
            token = gather[i + AHEAD][4]
        got = {k: lax.dynamic_update_index_in_dim(z, s, me, 0) for k, s, z in zip(names, srcs, lands)}
        return _group_weights(layer, kind, got, token)

    scatter = []

    def put_grads(layer, kind, big):
        slots = _layer_grad_slots(kind, big)
        started = _exchange_start(list(slots.values()), scatter=True, name=f"scatter_start_{kind}_l{layer}")
        scatter.append((layer, kind, list(slots.keys()), started))
        return started[4]

    cos_t, sin_t = _rope_tables(positions[0])
    loss, dx, dmod, g = _local_step(x[0], loss_target[0], mod, cos_t, sin_t, rep, get_weights, put_grads)

    parts = {k: [None] * W[k].shape[0] for k in BIG}
    res = {}

    def wait_group(entry, after):
        layer, kind, names, started = entry
        srcs, lands = _exchange_wait(started, after, scatter=True, name=f"scatter_wait_{kind}_l{layer}")
        for k, s, z in zip(names, srcs, lands):
            own = lax.dynamic_index_in_dim(s, me, 0, keepdims=False)
            parts[k][_layer_index(k, layer)] = lax.dynamic_update_index_in_dim(z, own, me, 0)

    for entry in scatter[:-1]:
        wait_group(entry, dx)
    early = [k for k in BIG if k not in scatter[-1][2]]
    def update(k):
        outs = _adamw(parts[k], _view(k, W[k]), _view(k, M[k]), _view(k, V[k]), name=f"adamw_{k}")
        return tuple(_view(k, o) for o in outs)

    for k in early:
        res[k] = update(k)
    loss, dmod, done = lax.optimization_barrier((loss, dmod, [res[k] for k in early]))
    for k, r in zip(early, done):
        res[k] = r

    small_local = [dmod.reshape(DEPTH, N_MOD * D_MODEL), g["norm_mix_g"], g["norm_ffn_g"],
                   jnp.transpose(g["gdn_conv_wt"], (0, 2, 1)), g["gdn_a_log"], g["gdn_dt_bias"], g["gdn_norm_g"],
                   g["mla_q_norm_g"], g["mla_kv_norm_g"], g["final_norm_g"], loss.reshape(1)]
    small_shapes = [a.shape for a in small_local]
    (small_g,) = _exchange([_pack(small_local)], scatter=False, name="gather_small_grads")
    small_sum = _unpack(_sum_parts(small_g, name="sum_small_grads"), small_shapes)
    loss = small_sum[-1][0]
    dmod_all = _unpack(small_g, small_shapes[:1], lead=(N_DEV,))[0]
    sg = dict(zip(SMALL, small_sum))
    wait_group(scatter[-1], small_g)
    sg["gdn_conv_w"] = lax.dynamic_slice_in_dim(sg["gdn_conv_w"], me * gdn_conv_w.shape[1], gdn_conv_w.shape[1], 1)
    sg["mla_q_norm_g"] = lax.dynamic_slice_in_dim(sg["mla_q_norm_g"], me * mla_q_norm_g.shape[1],
                                                  mla_q_norm_g.shape[1], 1)
    sg["mla_kv_norm_g"] = lax.dynamic_slice_in_dim(sg["mla_kv_norm_g"], me * mla_kv_norm_g.shape[1],
                                                   mla_kv_norm_g.shape[1], 1)

    dmod_cols = jnp.transpose(lax.dynamic_slice_in_dim(dmod_all, me * wc, wc, axis=2), (1, 0, 2))
    res["ada_w"] = _ada_grad_adamw(c_all, dmod_cols, ada_w, m_ada_w, v_ada_w, name="ada_w_grad_adamw")
    for k in BIG:
        if k not in early:
            res[k] = update(k)
    shapes = [W[k].shape for k in SMALL]
    packed = [_pack([d[k] for k in SMALL]) for d in (sg, W, M, V)]
    outs = _adamw([packed[0][None]], packed[1][None], packed[2][None], packed[3][None], name="adamw_small")
    unpacked = [_unpack(o[0], shapes) for o in outs]
    for i, k in enumerate(SMALL):
        res[k] = tuple(u[i] for u in unpacked)

    return (loss, dx[None], *[res[k][0] for k in WEIGHTS], *[res[k][1] for k in WEIGHTS],
            *[res[k][2] for k in WEIGHTS], *[res[k][3] for k in WEIGHTS])
```

```python
import functools
import math

import jax
import jax.numpy as jnp
from jax import lax
from jax.experimental import pallas as pl
from jax.experimental.pallas import tpu as pltpu

F32 = jnp.float32
BF16 = jnp.bfloat16
MXU_DTYPE = jnp.bfloat16

N_DEV = 8
D_MODEL = 1024
DEPTH = 4
GDN_HEADS = 8
GDN_HEAD_DIM = 128
GDN_KEY_DIM = GDN_HEADS * GDN_HEAD_DIM
GDN_CHUNK = 64
GDN_HEAD_BATCH = 8
GDN_CONV = 4
GDN_PREP_HEADS = 2
GDN_MAIN = 4 * GDN_KEY_DIM
MLA_HEADS = 8
MLA_NOPE = 128
MLA_ROPE = 64
MLA_V = 128
MLA_Q_RANK = 384
MLA_KV_RANK = 256
MLA_IN = MLA_Q_RANK + MLA_KV_RANK + MLA_ROPE
MLA_QK = MLA_NOPE + MLA_ROPE
ROPE_THETA = 10000.0
D_FF = 2816
N_MOD = 6
EPS = 1e-6
LANES = 128
VMEM_LIMIT = 48 * 1024 * 1024

ADAM_LR = 0.001
ADAM_B1 = 0.9
ADAM_B2 = 0.999
ADAM_EPS = 1e-08
ADAM_WD = 0.01
ADAM_STEP = 10
ADAM_BC1 = 1.0 - ADAM_B1 ** ADAM_STEP
ADAM_BC2 = 1.0 - ADAM_B2 ** ADAM_STEP

NN = (((1,), (0,)), ((), ()))
NT = (((1,), (1,)), ((), ()))
TN = (((0,), (0,)), ((), ()))
NEG = -1e30


def _dotb(a, b, dims):
    return lax.dot_general(a.astype(MXU_DTYPE), b.astype(MXU_DTYPE), dims, preferred_element_type=F32)


def _split(a):
    hi = a.astype(BF16)
    return hi, (a - hi.astype(F32)).astype(BF16)


def _dotf(a, b, dims):
    ah, al = _split(a)
    bh, bl = _split(b)
    dot = lambda u, v: lax.dot_general(u, v, dims, preferred_element_type=F32)
    return dot(ah, bh) + (dot(ah, bl) + dot(al, bh))


def _params(*sem):
    return pltpu.CompilerParams(dimension_semantics=sem, vmem_limit_bytes=VMEM_LIMIT)


def _pick(n, pref, mult=LANES):
    best = None
    t = mult
    while t <= min(n, pref):
        if n % t == 0:
            best = t
        t += mult
    return best if best is not None else n


def _sigmoid(z):
    return 1.0 / (1.0 + jnp.exp(-z))


def _exchange(arrays, *, scatter, name):
    n = len(arrays)
    out_shape = tuple(
        jax.ShapeDtypeStruct(a.shape if scatter else (N_DEV,) + a.shape, a.dtype) for a in arrays)

    def body(*refs):
        ins, outs = refs[:n], refs[n:2 * n]
        send_sems, recv_sems, local_sems = refs[2 * n:]
        x, y, c = lax.axis_index("x"), lax.axis_index("y"), lax.axis_index("c")
        me = 4 * x + 2 * y + c
        copies = []
        for k in range(n):
            src_own = ins[k].at[me] if scatter else ins[k]
            own = pltpu.make_async_copy(src_own, outs[k].at[me], local_sems.at[k])
            own.start()
            copies.append(own)
        sends = []
        for p in range(1, N_DEV):
            px, py, pc = x ^ ((p >> 2) & 1), y ^ ((p >> 1) & 1), c ^ (p & 1)
            peer = 4 * px + 2 * py + pc
            for k in range(n):
                cp = pltpu.make_async_remote_copy(
                    src_ref=ins[k].at[peer] if scatter else ins[k],
                    dst_ref=outs[k].at[me],
                    send_sem=send_sems.at[k, p - 1],
                    recv_sem=recv_sems.at[k, p - 1],
                    device_id=(px, py, pc),
                    device_id_type=pl.DeviceIdType.MESH,
                )
                cp.start()
                sends.append((cp, k, peer, p))
        for cp, k, peer, p in sends:
            pltpu.make_async_remote_copy(
                src_ref=ins[k].at[peer] if scatter else ins[k],
                dst_ref=outs[k].at[peer],
                send_sem=send_sems.at[k, p - 1],
                recv_sem=recv_sems.at[k, p - 1],
                device_id=(x, y, c),
                device_id_type=pl.DeviceIdType.MESH,
            ).wait_recv()
        for cp, _, _, _ in sends:
            cp.wait_send()
        for own in copies:
            own.wait()

    any_spec = pl.BlockSpec(memory_space=pl.ANY)
    outs = pl.pallas_call(
        body,
        name=name,
        out_shape=out_shape,
        in_specs=[any_spec] * n,
        out_specs=tuple([any_spec] * n),
        scratch_shapes=[
            pltpu.SemaphoreType.DMA((n, N_DEV - 1)),
            pltpu.SemaphoreType.DMA((n, N_DEV - 1)),
            pltpu.SemaphoreType.DMA((n,)),
        ],
        compiler_params=pltpu.CompilerParams(has_side_effects=True),
    )(*arrays)
    return list(outs)


def _gather_two_level(arrays, *, name):
    n = len(arrays)
    out_shape = tuple(jax.ShapeDtypeStruct((N_DEV,) + a.shape, a.dtype) for a in arrays)

    def body(*refs):
        ins, outs = refs[:n], refs[n:2 * n]
        send_sems, recv_sems, local_sems = refs[2 * n:]
        x, y, c = lax.axis_index("x"), lax.axis_index("y"), lax.axis_index("c")
        me = 4 * x + 2 * y + c
        sibling = (x, y, 1 - c)
        chips = [(1 - x, y), (x, 1 - y), (1 - x, 1 - y)]

        def slot(px, py, pc):
            return 4 * px + 2 * py + pc

        def copy(k, q, block, to, src=None):
            return pltpu.make_async_remote_copy(
                src_ref=outs[k].at[slot(*block)] if src is None else src,
                dst_ref=outs[k].at[slot(*block)],
                send_sem=send_sems.at[k, q], recv_sem=recv_sems.at[k, q],
                device_id=to, device_id_type=pl.DeviceIdType.MESH)

        own = [pltpu.make_async_copy(ins[k], outs[k].at[me], local_sems.at[k]) for k in range(n)]
        for cp in own:
            cp.start()
        first = []
        for k in range(n):
            first.append(copy(k, 0, (x, y, c), sibling, src=ins[k]))
            first += [copy(k, 1 + j, (x, y, c), (*chip, c), src=ins[k]) for j, chip in enumerate(chips)]
        for cp in first:
            cp.start()
        passed = []
        for j, chip in enumerate(chips):
            for k in range(n):
                copy(k, 1 + j, (*chip, c), (x, y, c)).wait_recv()
                fwd = copy(k, 4 + j, (*chip, c), sibling)
                fwd.start()
                passed.append(fwd)
        for k in range(n):
            copy(k, 0, sibling, (x, y, c)).wait_recv()
            for j, chip in enumerate(chips):
                copy(k, 4 + j, (*chip, 1 - c), (x, y, c)).wait_recv()
        for cp in first + passed:
            cp.wait_send()
        for cp in own:
            cp.wait()

    any_spec = pl.BlockSpec(memory_space=pl.ANY)
    outs = pl.pallas_call(
        body, name=name, out_shape=out_shape, in_specs=[any_spec] * n, out_specs=tuple([any_spec] * n),
        scratch_shapes=[pltpu.SemaphoreType.DMA((n, N_DEV - 1)), pltpu.SemaphoreType.DMA((n, N_DEV - 1)),
                        pltpu.SemaphoreType.DMA((n,))],
        compiler_params=pltpu.CompilerParams(has_side_effects=True),
    )(*arrays)
    return list(outs)


def _peer(x, y, c, p):
    return x ^ ((p >> 2) & 1), y ^ ((p >> 1) & 1), c ^ (p & 1)


def _exchange_start(arrays, *, scatter, name, dep=None):
    n = len(arrays)
    deps = [] if dep is None else [dep]
    lands = [lax.empty(a.shape if scatter else (N_DEV,) + a.shape, a.dtype) for a in arrays]

    def body(*refs):
        ins, zones = refs[:n], refs[n:2 * n]
        send_sems, recv_sems = refs[2 * n + len(deps)], refs[2 * n + len(deps) + 1]
        token = refs[-1]
        x, y, c = lax.axis_index("x"), lax.axis_index("y"), lax.axis_index("c")
        me = 4 * x + 2 * y + c
        for p in range(1, N_DEV):
            px, py, pc = _peer(x, y, c, p)
            for k in range(n):
                pltpu.make_async_remote_copy(
                    src_ref=ins[k].at[4 * px + 2 * py + pc] if scatter else ins[k],
                    dst_ref=zones[k].at[me],
                    send_sem=send_sems.at[k * (N_DEV - 1) + p - 1],
                    recv_sem=recv_sems.at[k * (N_DEV - 1) + p - 1],
                    device_id=(px, py, pc),
                    device_id_type=pl.DeviceIdType.MESH,
                ).start()
        token[...] = jnp.zeros_like(token)

    hbm = pl.BlockSpec(memory_space=pltpu.HBM)
    sem = pl.BlockSpec(memory_space=pltpu.SEMAPHORE)
    outs = pl.pallas_call(
        body,
        name=name,
        out_shape=(pltpu.SemaphoreType.DMA((n * (N_DEV - 1),)), pltpu.SemaphoreType.DMA((n * (N_DEV - 1),)),
                   *[pltpu.HBM(a.shape, a.dtype) for a in arrays], *[pltpu.HBM(z.shape, z.dtype) for z in lands],
                   jax.ShapeDtypeStruct((8, LANES), F32)),
        in_specs=[hbm] * (2 * n) + [pl.BlockSpec(memory_space=pl.ANY)] * len(deps),
        out_specs=(sem, sem, *[hbm] * (2 * n), pl.BlockSpec(memory_space=pltpu.VMEM)),
        input_output_aliases={k: 2 + k for k in range(2 * n)},
        compiler_params=pltpu.CompilerParams(has_side_effects=pltpu.SideEffectType.DATAFLOW_SIDE_EFFECTING),
    )(*[pltpu.with_memory_space_constraint(a, pltpu.HBM) for a in arrays],
      *[pltpu.with_memory_space_constraint(z, pltpu.HBM) for z in lands], *deps)
    return outs[0], outs[1], list(outs[2:2 + n]), list(outs[2 + n:2 + 2 * n]), outs[-1]


def _exchange_wait(started, after, *, scatter, name):
    send_sems, recv_sems, srcs, lands, _ = started
    n = len(srcs)

    def body(*refs):
        ins, zones = refs[:n], refs[n:2 * n]
        s_sems, r_sems = refs[2 * n], refs[2 * n + 1]
        x, y, c = lax.axis_index("x"), lax.axis_index("y"), lax.axis_index("c")
        for p in range(1, N_DEV):
            px, py, pc = _peer(x, y, c, p)
            peer = 4 * px + 2 * py + pc
            for k in range(n):
                cp = pltpu.make_async_remote_copy(
                    src_ref=ins[k].at[peer] if scatter else ins[k],
                    dst_ref=zones[k].at[peer],
                    send_sem=s_sems.at[k * (N_DEV - 1) + p - 1],
                    recv_sem=r_sems.at[k * (N_DEV - 1) + p - 1],
                    device_id=(px, py, pc),
                    device_id_type=pl.DeviceIdType.MESH,
                )
                cp.wait_send()
                cp.wait_recv()

    hbm = pl.BlockSpec(memory_space=pltpu.HBM)
    sem = pl.BlockSpec(memory_space=pltpu.SEMAPHORE)
    outs = pl.pallas_call(
        body,
        name=name,
        out_shape=tuple(pltpu.HBM(a.shape, a.dtype) for a in srcs + lands),
        in_specs=[hbm] * (2 * n) + [sem, sem, pl.BlockSpec(memory_space=pl.ANY)],
        out_specs=tuple([hbm] * (2 * n)),
        input_output_aliases={k: k for k in range(2 * n)},
        compiler_params=pltpu.CompilerParams(has_side_effects=pltpu.SideEffectType.DATAFLOW_SIDE_EFFECTING),
    )(*srcs, *lands, send_sems, recv_sems, after)
    return list(outs[:n]), list(outs[n:])


def _mm(a, b, *, mode, out_dtype, name, add=None, tm=512, tn=512, b_rows=None, dep=None):
    rows_b = b.shape[0] if b_rows is None else b_rows
    if mode == "nn":
        (m, kd), nd = a.shape, b.shape[1]
        assert kd == rows_b
    elif mode == "nt":
        (m, kd), nd = a.shape, rows_b
    else:
        (kd, m), nd = a.shape, b.shape[1]
    tm = _pick(m, tm, LANES if mode == "tn" else 16)
    tn = _pick(nd, tn)
    dims = {"nn": NN, "nt": NT, "tn": TN}[mode]
    ni, nj = m // tm, nd // tn
    a_bytes, b_bytes = a.size * a.dtype.itemsize, b.size * b.dtype.itemsize
    i_outer = a_bytes + ni * b_bytes <= b_bytes + nj * a_bytes
    ij = (lambda g0, g1: (g0, g1)) if i_outer else (lambda g0, g1: (g1, g0))
    a_spec = (pl.BlockSpec((kd, tm), lambda g0, g1: (0, ij(g0, g1)[0])) if mode == "tn"
              else pl.BlockSpec((tm, kd), lambda g0, g1: (ij(g0, g1)[0], 0)))
    b_spec = (pl.BlockSpec((tn, kd), lambda g0, g1: (ij(g0, g1)[1], 0)) if mode == "nt"
              else pl.BlockSpec((kd, tn), lambda g0, g1: (0, ij(g0, g1)[1])))
    o_spec = pl.BlockSpec((tm, tn), lambda g0, g1: ij(g0, g1))
    has_add = add is not None

    def body(*refs):
        a_ref, b_ref = refs[0], refs[1]
        o_ref = refs[-1]
        acc = _dotb(a_ref[...], b_ref[...], dims)
        if has_add:
            acc = acc + refs[2][...].astype(F32)
        o_ref[...] = acc.astype(o_ref.dtype)

    ins = [a, b] + ([add] if has_add else []) + ([] if dep is None else [dep])
    specs = ([a_spec, b_spec] + ([o_spec] if has_add else [])
             + ([] if dep is None else [pl.BlockSpec((8, LANES), lambda g0, g1: (0, 0))]))
    return pl.pallas_call(
        body, name=name, grid=(ni, nj) if i_outer else (nj, ni), in_specs=specs, out_specs=o_spec,
        out_shape=jax.ShapeDtypeStruct((m, nd), out_dtype),
        compiler_params=_params("parallel", "parallel"),
    )(*ins)


def _mm_resid(a, b, x, gate, *, name, tm=256, tn=1024):
    m, kd = a.shape
    nd = b.shape[1]
    tm = _pick(m, tm, 16)
    tn = _pick(nd, tn)
    o_spec = pl.BlockSpec((tm, tn), lambda i, j: (i, j))

    def body(a_ref, b_ref, x_ref, g_ref, xo_ref, y_ref):
        y = _dotb(a_ref[...], b_ref[...], NN)
        y_ref[...] = y
        xo_ref[...] = x_ref[...] + g_ref[...] * y

    return pl.pallas_call(
        body, name=name, grid=(m // tm, nd // tn),
        in_specs=[pl.BlockSpec((tm, kd), lambda i, j: (i, 0)), pl.BlockSpec((kd, tn), lambda i, j: (0, j)),
                  o_spec, pl.BlockSpec((1, tn), lambda i, j: (0, j))],
        out_specs=(o_spec, o_spec),
        out_shape=(jax.ShapeDtypeStruct((m, nd), F32), jax.ShapeDtypeStruct((m, nd), F32)),
        compiler_params=_params("parallel", "parallel"),
    )(a, b, x, gate)


ROWS = 256


def _row_spec(width, rows=ROWS):
    return pl.BlockSpec((rows, width), lambda i: (i, 0))


def _const_spec(shape):
    return pl.BlockSpec(shape, lambda i: tuple(0 for _ in shape))


def _adaln_fwd(x, g, scale, shift, *, name):
    t, d = x.shape

    def body(x_ref, g_ref, sc_ref, sh_ref, h_ref):
        xv = x_ref[...]
        r = lax.rsqrt(jnp.mean(xv * xv, axis=-1, keepdims=True) + EPS)
        h_ref[...] = (xv * r * g_ref[...] * (1.0 + sc_ref[...]) + sh_ref[...]).astype(h_ref.dtype)

    return pl.pallas_call(
        body, name=name, grid=(t // ROWS,),
        in_specs=[_row_spec(d), _const_spec((1, d)), _const_spec((1, d)), _const_spec((1, d))],
        out_specs=_row_spec(d), out_shape=jax.ShapeDtypeStruct((t, d), BF16),
        compiler_params=_params("parallel"),
    )(x, g, scale, shift)


def _adaln_bwd(x, g, scale, shift, dh, dres, dep, *, name):
    t, d = x.shape

    def body(x_ref, g_ref, sc_ref, sh_ref, dh_ref, dr_ref, dep_ref, dx_ref, st_ref):
        @pl.when(pl.program_id(0) == 0)
        def _():
            st_ref[...] = jnp.zeros_like(st_ref)

        xv = x_ref[...]
        dhv = dh_ref[...].astype(F32)
        gv = g_ref[...]
        r = lax.rsqrt(jnp.mean(xv * xv, axis=-1, keepdims=True) + EPS)
        xh = xv * r
        nv = xh * gv
        dn = dhv * (1.0 + sc_ref[...])
        dxh = dn * gv
        dx_ref[...] = dr_ref[...] + r * (dxh - xh * jnp.mean(dxh * xh, axis=-1, keepdims=True))
        st_ref[0:1, :] += jnp.sum(dn * xh, axis=0, keepdims=True)
        st_ref[1:2, :] += jnp.sum(dhv * nv, axis=0, keepdims=True)
        st_ref[2:3, :] += jnp.sum(dhv, axis=0, keepdims=True)

    return pl.pallas_call(
        body, name=name, grid=(t // ROWS,),
        in_specs=[_row_spec(d), _const_spec((1, d)), _const_spec((1, d)), _const_spec((1, d)),
                  _row_spec(d), _row_spec(d), _const_spec((8, LANES))],
        out_specs=(_row_spec(d), _const_spec((8, d))),
        out_shape=(jax.ShapeDtypeStruct((t, d), F32), jax.ShapeDtypeStruct((8, d), F32)),
        compiler_params=_params("arbitrary"),
    )(x, g, scale, shift, dh, dres, dep)


def _gate_bwd(dxo, y, gate, dep, *, name):
    t, d = dxo.shape

    def body(dx_ref, y_ref, g_ref, dep_ref, dy_ref, st_ref):
        @pl.when(pl.program_id(0) == 0)
        def _():
            st_ref[...] = jnp.zeros_like(st_ref)

        dxv = dx_ref[...]
        dy_ref[...] = (dxv * g_ref[...]).astype(dy_ref.dtype)
        st_ref[0:1, :] += jnp.sum(dxv * y_ref[...], axis=0, keepdims=True)

    return pl.pallas_call(
        body, name=name, grid=(t // ROWS,),
        in_specs=[_row_spec(d), _row_spec(d), _const_spec((1, d)), _const_spec((8, LANES))],
        out_specs=(_row_spec(d), _const_spec((8, d))),
        out_shape=(jax.ShapeDtypeStruct((t, d), BF16), jax.ShapeDtypeStruct((8, d), F32)),
        compiler_params=_params("arbitrary"),
    )(dxo, y, gate, dep)


def _loss_head(x, g, target, *, name):
    t, d = x.shape

    def body(x_ref, g_ref, t_ref, dx_ref, st_ref, ls_ref):
        @pl.when(pl.program_id(0) == 0)
        def _():
            st_ref[...] = jnp.zeros_like(st_ref)
            ls_ref[...] = jnp.zeros_like(ls_ref)

        xv = x_ref[...]
        gv = g_ref[...]
        r = lax.rsqrt(jnp.mean(xv * xv, axis=-1, keepdims=True) + EPS)
        xh = xv * r
        err = xh * gv - t_ref[...]
        ls_ref[...] += 0.5 * jnp.sum(jnp.mean(err * err, axis=-1, keepdims=True))
        dy = err * (1.0 / d)
        dxh = dy * gv
        dx_ref[...] = r * (dxh - xh * jnp.mean(dxh * xh, axis=-1, keepdims=True))
        st_ref[0:1, :] += jnp.sum(dy * xh, axis=0, keepdims=True)

    return pl.pallas_call(
        body, name=name, grid=(t // ROWS,),
        in_specs=[_row_spec(d), _const_spec((1, d)), _row_spec(d)],
        out_specs=(_row_spec(d), _const_spec((8, d)), _const_spec((8, LANES))),
        out_shape=(jax.ShapeDtypeStruct((t, d), F32), jax.ShapeDtypeStruct((8, d), F32),
                   jax.ShapeDtypeStruct((8, LANES), F32)),
        compiler_params=_params("arbitrary"),
    )(x, g, target)


FFN_BLOCK = D_FF // 2


def _ffn_gu_fwd(h, wg, wu, dep, *, name):
    t, d = h.shape
    tn = FFN_BLOCK

    def body(h_ref, wg_ref, wu_ref, dep_ref, s_ref, a_ref, b_ref):
        hv = h_ref[...]
        a = _dotb(hv, wg_ref[...], NT)
        b = _dotb(hv, wu_ref[...], NT)
        s_ref[...] = (a * _sigmoid(a) * b).astype(s_ref.dtype)
        a_ref[...] = a.astype(a_ref.dtype)
        b_ref[...] = b.astype(b_ref.dtype)

    w_spec = pl.BlockSpec((tn, d), lambda j, i: (j, 0))
    o_spec = pl.BlockSpec((ROWS, tn), lambda j, i: (i, j))
    return pl.pallas_call(
        body, name=name, grid=(D_FF // tn, t // ROWS),
        in_specs=[pl.BlockSpec((ROWS, d), lambda j, i: (i, 0)), w_spec, w_spec,
                  pl.BlockSpec((8, LANES), lambda j, i: (0, 0))],
        out_specs=(o_spec, o_spec, o_spec),
        out_shape=(jax.ShapeDtypeStruct((t, D_FF), BF16),) * 3,
        compiler_params=_params("parallel", "parallel"),
    )(h, wg, wu, dep)


def _ffn_down_dx(dy, w_down, a, b, *, name):
    t, d = dy.shape
    tn = FFN_BLOCK

    edges = [min(tn, 3 * LANES * i) for i in range(tn // (3 * LANES) + 2)]
    chunks = [slice(lo, hi) for lo, hi in zip(edges[:-1], edges[1:]) if hi > lo]

    def body(dy_ref, w_ref, a_ref, b_ref, da_ref, db_ref):
        dyv = dy_ref[...]
        ds = [_dotb(dyv, w_ref[sl, :], NT) for sl in chunks]
        for sl, dsc in zip(chunks, ds):
            av = a_ref[:, sl].astype(F32)
            sg = _sigmoid(av)
            da_ref[:, sl] = (dsc * b_ref[:, sl].astype(F32) * sg * (1.0 + av * (1.0 - sg))).astype(da_ref.dtype)
            db_ref[:, sl] = (dsc * av * sg).astype(db_ref.dtype)

    o_spec = pl.BlockSpec((ROWS, tn), lambda j, i: (i, j))
    return pl.pallas_call(
        body, name=name, grid=(D_FF // tn, t // ROWS),
        in_specs=[pl.BlockSpec((ROWS, d), lambda j, i: (i, 0)), pl.BlockSpec((tn, d), lambda j, i: (j, 0)),
                  o_spec, o_spec],
        out_specs=(o_spec, o_spec),
        out_shape=(jax.ShapeDtypeStruct((t, D_FF), BF16),) * 2,
        compiler_params=_params("parallel", "parallel"),
    )(dy, w_down, a, b)


def _shift_rows(v, s, rows):
    if s == 0:
        return v
    return jnp.where(rows >= s, pltpu.roll(v, s, 0), 0.0)


def _unshift_rows(v, s, rows, t):
    if s == 0:
        return v
    return jnp.where(rows < t - s, pltpu.roll(v, t - s, 0), 0.0)


def _conv_silu(x, w, rows):
    z = w[GDN_CONV - 1:GDN_CONV, :] * x
    for j in range(GDN_CONV - 1):
        z = z + w[j:j + 1, :] * _shift_rows(x, GDN_CONV - 1 - j, rows)
    sg = _sigmoid(z)
    return z, sg, z * sg


def _gdn_prep_fwd(proj, conv_wt, *, name):
    t = proj.shape[0]
    nh = GDN_HEADS

    hp = GDN_PREP_HEADS
    wd = hp * LANES

    def body(x_ref, w_ref, y_ref):
        j = pl.program_id(0) * hp
        rows = lax.broadcasted_iota(jnp.int32, (t, LANES), 0)
        qscale = jnp.where(j < nh, GDN_HEAD_DIM ** -0.5, 1.0)
        for i in range(hp):
            sl = slice(i * LANES, (i + 1) * LANES)
            _, _, s = _conv_silu(x_ref[:, sl], w_ref[:, sl], rows)
            rs = lax.rsqrt(jnp.sum(s * s, axis=-1, keepdims=True) + EPS)
            y_ref[:, sl] = jnp.where(j < 2 * nh, s * rs * qscale, s)

    return pl.pallas_call(
        body, name=name, grid=(3 * nh // hp,),
        in_specs=[pl.BlockSpec((t, wd), lambda j: (0, j)), pl.BlockSpec((GDN_CONV, wd), lambda j: (0, j))],
        out_specs=pl.BlockSpec((t, wd), lambda j: (0, j)),
        out_shape=jax.ShapeDtypeStruct((t, 3 * GDN_KEY_DIM), F32),
        compiler_params=_params("parallel"),
    )(proj, conv_wt)


def _gdn_prep_bwd(proj, conv_wt, dy, *, name):
    t = proj.shape[0]
    nh = GDN_HEADS

    hp = GDN_PREP_HEADS
    wd = hp * LANES
    per_seg = nh // hp

    def body(x_ref, w_ref, dy_ref, dx_ref, dw_ref):
        j = pl.program_id(0) * hp
        rows = lax.broadcasted_iota(jnp.int32, (t, LANES), 0)
        qscale = jnp.where(j < nh, GDN_HEAD_DIM ** -0.5, 1.0)
        for i in range(hp):
            sl = slice(i * LANES, (i + 1) * LANES)
            x = x_ref[:, sl]
            w = w_ref[:, sl]
            z, sg, s = _conv_silu(x, w, rows)
            rs = lax.rsqrt(jnp.sum(s * s, axis=-1, keepdims=True) + EPS)
            dyv = dy_ref[:, sl]
            nv = s * rs
            de = dyv * qscale
            ds_qk = rs * (de - nv * jnp.sum(de * nv, axis=-1, keepdims=True))
            ds = jnp.where(j < 2 * nh, ds_qk, dyv)
            dz = ds * sg * (1.0 + z * (1.0 - sg))
            dx = w[GDN_CONV - 1:GDN_CONV, :] * dz
            dw_ref[GDN_CONV - 1:GDN_CONV, sl] = jnp.sum(dz * x, axis=0, keepdims=True)
            for k in range(GDN_CONV - 1):
                sh = GDN_CONV - 1 - k
                dx = dx + w[k:k + 1, :] * _unshift_rows(dz, sh, rows, t)
                dw_ref[k:k + 1, sl] = jnp.sum(dz * _shift_rows(x, sh, rows), axis=0, keepdims=True)
            dx_ref[:, sl] = dx.astype(dx_ref.dtype)

    return pl.pallas_call(
        body, name=name, grid=(3 * nh // hp,),
        in_specs=[pl.BlockSpec((t, wd), lambda j: (0, j)), pl.BlockSpec((GDN_CONV, wd), lambda j: (0, j)),
                  pl.BlockSpec((None, t, wd), lambda j: (j // per_seg, 0, j % per_seg))],
        out_specs=(pl.BlockSpec((t, wd), lambda j: (0, j)), pl.BlockSpec((GDN_CONV, wd), lambda j: (0, j))),
        out_shape=(jax.ShapeDtypeStruct((t, 3 * GDN_KEY_DIM), BF16),
                   jax.ShapeDtypeStruct((GDN_CONV, 3 * GDN_KEY_DIM), F32)),
        compiler_params=_params("parallel"),
    )(proj, conv_wt, dy)


def _softplus(z):
    return jnp.maximum(z, 0.0) + jnp.log(1.0 + jnp.exp(-jnp.abs(z)))


def _gdn_gate_fwd(ab, prm, *, name):
    t = ab.shape[0]

    def body(ab_ref, p_ref, o_ref):
        v = ab_ref[...]
        lane = lax.broadcasted_iota(jnp.int32, v.shape, 1)
        g = -jnp.exp(p_ref[0:1, :]) * _softplus(v + p_ref[1:2, :])
        o_ref[...] = jnp.where(lane < GDN_HEADS, g, jnp.where(lane < 2 * GDN_HEADS, _sigmoid(v), 0.0))

    return pl.pallas_call(
        body, name=name, grid=(t // ROWS,),
        in_specs=[_row_spec(LANES), _const_spec((8, LANES))], out_specs=_row_spec(LANES),
        out_shape=jax.ShapeDtypeStruct((t, LANES), F32), compiler_params=_params("parallel"),
    )(ab, prm)


def _gdn_gate_bwd(ab, prm, dgb, *, name):
    t = ab.shape[0]

    def body(ab_ref, p_ref, d_ref, o_ref, st_ref):
        @pl.when(pl.program_id(0) == 0)
        def _():
            st_ref[...] = jnp.zeros_like(st_ref)

        v = ab_ref[...]
        dv = d_ref[...]
        lane = lax.broadcasted_iota(jnp.int32, v.shape, 1)
        is_a = lane < GDN_HEADS
        is_b = jnp.logical_and(lane >= GDN_HEADS, lane < 2 * GDN_HEADS)
        a_exp = jnp.exp(p_ref[0:1, :])
        zz = v + p_ref[1:2, :]
        g = -a_exp * _softplus(zz)
        da = dv * (-a_exp) * _sigmoid(zz)
        beta = _sigmoid(v)
        db = dv * beta * (1.0 - beta)
        o_ref[...] = jnp.where(is_a, da, jnp.where(is_b, db, 0.0)).astype(o_ref.dtype)
        st_ref[0:1, :] += jnp.sum(jnp.where(is_a, dv * g, 0.0), axis=0, keepdims=True)
        st_ref[1:2, :] += jnp.sum(jnp.where(is_a, da, 0.0), axis=0, keepdims=True)

    return pl.pallas_call(
        body, name=name, grid=(t // ROWS,),
        in_specs=[_row_spec(LANES), _const_spec((8, LANES)), _row_spec(LANES)],
        out_specs=(_row_spec(LANES), _const_spec((8, LANES))),
        out_shape=(jax.ShapeDtypeStruct((t, LANES), BF16), jax.ShapeDtypeStruct((8, LANES), F32)),
        compiler_params=_params("arbitrary"),
    )(ab, prm, dgb)


def _gdn_local(qs, ks, vs, gbs, bbs, tinvs=None):
    nh = len(qs)
    cs = qs[0].shape[0]
    hs = range(nh)
    r = lax.broadcasted_iota(jnp.int32, (cs, cs), 0)
    c = lax.broadcasted_iota(jnp.int32, (cs, cs), 1)
    tril, strict, eye = r >= c, r > c, r == c
    ident = jnp.where(eye, 1.0, 0.0)
    g_colb = [gbs[h][:, :cs] for h in hs]
    g_row = [jnp.sum(jnp.where(eye, g_colb[h], 0.0), axis=0, keepdims=True) for h in hs]
    gc_col = [jnp.sum(jnp.where(tril, g_row[h], 0.0), axis=1, keepdims=True) for h in hs]
    gc_row = [jnp.sum(jnp.where(r <= c, g_colb[h], 0.0), axis=0, keepdims=True) for h in hs]
    decay = [jnp.exp(jnp.where(tril, gc_col[h] - gc_row[h], NEG)) for h in hs]
    gamma = [jnp.exp(gc_col[h]) for h in hs]
    gcl = [gc_col[h][cs - 1:cs, :] for h in hs]
    gl = [jnp.exp(gcl[h]) for h in hs]
    kdec = [jnp.exp(gcl[h] - gc_col[h]) for h in hs]
    kb = [ks[h] * bbs[h] for h in hs]
    kk = [_dotb(kb[h], ks[h], NT) for h in hs]
    qk = [_dotb(qs[h], ks[h], NT) for h in hs]
    lmat = [jnp.where(strict, kk[h] * decay[h], 0.0) for h in hs]
    pmat = [jnp.where(tril, qk[h] * decay[h], 0.0) for h in hs]
    if tinvs is None:
        xm = [-lmat[h] for h in hs]
        tinv = [ident + xm[h] for h in hs]
        for _ in range(int(math.log2(cs)) - 1):
            xm = [_dotf(xm[h], xm[h], NN) for h in hs]
            tinv = [tinv[h] + _dotf(tinv[h], xm[h], NN) for h in hs]
    else:
        tinv = tinvs
    vb = [vs[h] * bbs[h] for h in hs]
    kg = [kb[h] * gamma[h] for h in hs]
    u = [_dotf(tinv[h], vb[h], NN) for h in hs]
    w = [_dotf(tinv[h], kg[h], NN) for h in hs]
    return [dict(tril=tril, strict=strict, eye=eye, r=r, c=c, decay=decay[h], gamma=gamma[h], gl=gl[h], kdec=kdec[h],
                 kb=kb[h], lmat=lmat[h], tinv=tinv[h], vb=vb[h], kg=kg[h], u=u[h], w=w[h], pmat=pmat[h],
                 qd=qs[h] * gamma[h], kd=ks[h] * kdec[h]) for h in hs]


def _gdn_chunk_fwd(qkv, gbc, bbc, *, name):
    t = qkv.shape[0]
    nh, cs, hd = GDN_HEADS, GDN_CHUNK, GDN_HEAD_DIM
    nc = t // cs

    hb = GDN_HEAD_BATCH
    ng = nh // hb

    def body(q_ref, k_ref, v_ref, g_ref, b_ref, o_ref, st_ref, ti_ref, s_ref):
        @pl.when(pl.program_id(1) == 0)
        def _():
            s_ref[...] = jnp.zeros_like(s_ref)

        sls = [slice(i * hd, (i + 1) * hd) for i in range(hb)]
        hs = range(hb)
        s = [s_ref[i] for i in hs]
        lo = _gdn_local([q_ref[:, sl] for sl in sls], [k_ref[:, sl] for sl in sls], [v_ref[:, sl] for sl in sls],
                        [g_ref[i] for i in hs], [b_ref[i] for i in hs])
        ws = [_dotb(lo[i]["w"], s[i], NN) for i in hs]
        qs = [_dotb(lo[i]["qd"], s[i], NN) for i in hs]
        vn = [lo[i]["u"] - ws[i] for i in hs]
        pv = [_dotb(lo[i]["pmat"], vn[i], NN) for i in hs]
        kv = [_dotb(lo[i]["kd"], vn[i], TN) for i in hs]
        for i, sl in enumerate(sls):
            st_ref[i, 0] = s[i]
            ti_ref[i, 0] = lo[i]["tinv"]
            o_ref[:, sl] = qs[i] + pv[i]
            s_ref[i] = s[i] * lo[i]["gl"] + kv[i]

    gspec = pl.BlockSpec((hb, cs, LANES), lambda h, n: (h, n, 0))
    col = lambda off: pl.BlockSpec((cs, hb * hd), lambda h, n: (n, off + h))
    return pl.pallas_call(
        body, name=name, grid=(ng, nc),
        in_specs=[col(0), col(ng), col(2 * ng), gspec, gspec],
        out_specs=(col(0), pl.BlockSpec((hb, 1, hd, hd), lambda h, n: (h, n, 0, 0)),
                   pl.BlockSpec((hb, 1, cs, cs), lambda h, n: (h, n, 0, 0))),
        out_shape=(jax.ShapeDtypeStruct((t, nh * hd), F32), jax.ShapeDtypeStruct((nh, nc, hd, hd), F32),
                   jax.ShapeDtypeStruct((nh, nc, cs, cs), F32)),
        scratch_shapes=[pltpu.VMEM((hb, hd, hd), F32)],
        compiler_params=_params("parallel", "arbitrary"),
    )(qkv, qkv, qkv, gbc, bbc)


def _gdn_chunk_bwd(qkv, gbc, bbc, states, tinvs, do, *, name):
    t = qkv.shape[0]
    nh, cs, hd = GDN_HEADS, GDN_CHUNK, GDN_HEAD_DIM
    nc = t // cs

    hb = GDN_HEAD_BATCH
    ng = nh // hb

    def heads_bwd(q, k, v, gb, bb, s, ti, dsn, dov):
        hs = range(len(q))
        lo = _gdn_local(q, k, v, gb, bb, ti)
        tril, strict, eye, r, c = lo[0]["tril"], lo[0]["strict"], lo[0]["eye"], lo[0]["r"], lo[0]["c"]
        rowi = lax.broadcasted_iota(jnp.int32, (cs, 1), 0)
        get = lambda name: [lo[h][name] for h in hs]
        decay, gamma, gl, kdec = get("decay"), get("gamma"), get("gl"), get("kdec")
        kb, tinv, w, pmat, kd, qd = get("kb"), get("tinv"), get("w"), get("pmat"), get("kd"), get("qd")
        ws = [_dotb(w[h], s[h], NN) for h in hs]
        pdo = [_dotb(pmat[h], dov[h], TN) for h in hs]
        kds = [_dotb(kd[h], dsn[h], NN) for h in hs]
        dqd = [_dotb(dov[h], s[h], NT) for h in hs]
        qdo = [_dotb(qd[h], dov[h], TN) for h in hs]
        vn = [lo[h]["u"] - ws[h] for h in hs]
        dvn = [pdo[h] + kds[h] for h in hs]
        dp = [jnp.where(tril, _dotb(dov[h], vn[h], NT), 0.0) for h in hs]
        dkd = [_dotb(vn[h], dsn[h], NT) for h in hs]
        dw = [-_dotb(dvn[h], s[h], NT) for h in hs]
        wdv = [_dotb(w[h], dvn[h], TN) for h in hs]
        dvb = [_dotf(tinv[h], dvn[h], TN) for h in hs]
        dt1 = [_dotf(dvn[h], lo[h]["vb"], NT) for h in hs]
        dkg = [_dotf(tinv[h], dw[h], TN) for h in hs]
        dt2 = [_dotf(dw[h], lo[h]["kg"], NT) for h in hs]
        tdt = [_dotf(tinv[h], dt1[h] + dt2[h], TN) for h in hs]
        dl = [jnp.where(strict, -_dotf(tdt[h], tinv[h], NT), 0.0) for h in hs]
        dkk = [dl[h] * decay[h] for h in hs]
        dqk = [dp[h] * decay[h] for h in hs]
        dkb = [_dotb(dkk[h], k[h], NN) + dkg[h] * gamma[h] for h in hs]
        dk1 = [_dotb(dkk[h], kb[h], TN) for h in hs]
        dk2 = [_dotb(dqk[h], q[h], TN) for h in hs]
        dq1 = [_dotb(dqk[h], k[h], NN) for h in hs]
        out = []
        for h in hs:
            dgl = jnp.sum(jnp.sum(dsn[h] * s[h], axis=1, keepdims=True), axis=0, keepdims=True)
            ds_prev = gl[h] * dsn[h] + qdo[h] - wdv[h]
            dk = dk1[h] + dk2[h] + dkd[h] * kdec[h] + dkb[h] * bb[h]
            dq = dq1[h] + dqd[h] * gamma[h]
            dbeta = jnp.sum(dvb[h] * v[h], axis=-1, keepdims=True) + jnp.sum(dkb[h] * k[h], axis=-1, keepdims=True)
            e = dl[h] * lo[h]["lmat"] + dp[h] * pmat[h]
            e_col = jnp.sum(e, axis=0, keepdims=True)
            dgc = jnp.sum(e, axis=1, keepdims=True) - jnp.sum(jnp.where(eye, e_col, 0.0), axis=1, keepdims=True)
            dgamma = (jnp.sum(dqd[h] * q[h], axis=-1, keepdims=True)
                      + jnp.sum(dkg[h] * kb[h], axis=-1, keepdims=True))
            rk = jnp.sum(dkd[h] * k[h], axis=-1, keepdims=True) * kdec[h]
            dgcl = jnp.sum(rk, axis=0, keepdims=True) + dgl * gl[h]
            dgc = dgc + dgamma * gamma[h] - rk + jnp.where(rowi == cs - 1, dgcl, 0.0)
            dgc_row = jnp.sum(jnp.where(eye, dgc, 0.0), axis=0, keepdims=True)
            dg = jnp.sum(jnp.where(c >= r, dgc_row, 0.0), axis=1, keepdims=True)
            out.append((dq, dk, dvb[h] * bb[h], dbeta, dg, ds_prev))
        return out

    def body(q_ref, k_ref, v_ref, g_ref, b_ref, st_ref, ti_ref, do_ref, d_ref, dg_ref, db_ref, ds_ref):
        @pl.when(pl.program_id(1) == 0)
        def _():
            ds_ref[...] = jnp.zeros_like(ds_ref)

        sls = [slice(i * hd, (i + 1) * hd) for i in range(hb)]
        hs = range(hb)
        outs = heads_bwd([q_ref[:, sl] for sl in sls], [k_ref[:, sl] for sl in sls], [v_ref[:, sl] for sl in sls],
                         [g_ref[i] for i in hs], [b_ref[i] for i in hs], [st_ref[i, 0] for i in hs],
                         [ti_ref[i, 0] for i in hs], [ds_ref[i] for i in hs], [do_ref[:, sl] for sl in sls])
        for i, sl in enumerate(sls):
            dq, dk, dv, dbeta, dg, ds_prev = outs[i]
            d_ref[0, :, sl], d_ref[1, :, sl], d_ref[2, :, sl] = dq, dk, dv
            db_ref[i] = jnp.broadcast_to(dbeta, (cs, LANES))
            dg_ref[i] = jnp.broadcast_to(dg, (cs, LANES))
            ds_ref[i] = ds_prev

    gspec = pl.BlockSpec((hb, cs, LANES), lambda h, n: (h, nc - 1 - n, 0))
    col = lambda off: pl.BlockSpec((cs, hb * hd), lambda h, n: (nc - 1 - n, off + h))
    return pl.pallas_call(
        body, name=name, grid=(ng, nc),
        in_specs=[col(0), col(ng), col(2 * ng), gspec, gspec,
                  pl.BlockSpec((hb, 1, hd, hd), lambda h, n: (h, nc - 1 - n, 0, 0)),
                  pl.BlockSpec((hb, 1, cs, cs), lambda h, n: (h, nc - 1 - n, 0, 0)), col(0)],
        out_specs=(pl.BlockSpec((3, cs, hb * hd), lambda h, n: (0, nc - 1 - n, h)), gspec, gspec),
        out_shape=(jax.ShapeDtypeStruct((3, t, nh * hd), F32),) + (jax.ShapeDtypeStruct((nh, t, LANES), F32),) * 2,
        scratch_shapes=[pltpu.VMEM((hb, hd, hd), F32)],
        compiler_params=_params("parallel", "arbitrary"),
    )(qkv, qkv, qkv, gbc, bbc, states, tinvs, do)


def _gdn_onorm_fwd(o, proj, norm_g, *, name):
    t = o.shape[0]
    w = GDN_KEY_DIM
    goff = 3 * GDN_KEY_DIM // w

    def body(o_ref, gp_ref, g_ref, y_ref):
        gv = g_ref[...]
        for h in range(GDN_HEADS):
            sl = slice(h * GDN_HEAD_DIM, (h + 1) * GDN_HEAD_DIM)
            oh = o_ref[:, sl]
            gp = gp_ref[:, sl]
            r = lax.rsqrt(jnp.mean(oh * oh, axis=-1, keepdims=True) + EPS)
            y_ref[:, sl] = (oh * r * gv * gp * _sigmoid(gp)).astype(y_ref.dtype)

    return pl.pallas_call(
        body, name=name, grid=(t // ROWS,),
        in_specs=[_row_spec(w), pl.BlockSpec((ROWS, w), lambda i: (i, goff)), _const_spec((1, GDN_HEAD_DIM))],
        out_specs=_row_spec(w), out_shape=jax.ShapeDtypeStruct((t, w), BF16),
        compiler_params=_params("parallel"),
    )(o, proj, norm_g)


def _gdn_onorm_bwd(o, proj, norm_g, dy, *, name):
    t = o.shape[0]
    w = GDN_KEY_DIM
    goff = 3 * GDN_KEY_DIM // w

    def body(o_ref, gp_ref, g_ref, dy_ref, do_ref, dgp_ref, st_ref):
        @pl.when(pl.program_id(0) == 0)
        def _():
            st_ref[...] = jnp.zeros_like(st_ref)

        gv = g_ref[...]
        acc = jnp.zeros((1, GDN_HEAD_DIM), F32)
        for h in range(GDN_HEADS):
            sl = slice(h * GDN_HEAD_DIM, (h + 1) * GDN_HEAD_DIM)
            oh = o_ref[:, sl]
            gp = gp_ref[:, sl]
            dyv = dy_ref[:, sl].astype(F32)
            r = lax.rsqrt(jnp.mean(oh * oh, axis=-1, keepdims=True) + EPS)
            xh = oh * r
            sg = _sigmoid(gp)
            dn = dyv * gp * sg
            dgp_ref[:, sl] = (dyv * xh * gv * sg * (1.0 + gp * (1.0 - sg))).astype(dgp_ref.dtype)
            acc = acc + jnp.sum(dn * xh, axis=0, keepdims=True)
            dxh = dn * gv
            do_ref[:, sl] = r * (dxh - xh * jnp.mean(dxh * xh, axis=-1, keepdims=True))
        st_ref[0:1, :] += acc

    return pl.pallas_call(
        body, name=name, grid=(t // ROWS,),
        in_specs=[_row_spec(w), pl.BlockSpec((ROWS, w), lambda i: (i, goff)), _const_spec((1, GDN_HEAD_DIM)),
                  _row_spec(w)],
        out_specs=(_row_spec(w), _row_spec(w), _const_spec((8, GDN_HEAD_DIM))),
        out_shape=(jax.ShapeDtypeStruct((t, w), F32), jax.ShapeDtypeStruct((t, w), BF16),
                   jax.ShapeDtypeStruct((8, GDN_HEAD_DIM), F32)),
        compiler_params=_params("arbitrary"),
    )(o, proj, norm_g, dy)


def _mla_prep_fwd(proj, qg, kvg, *, name):
    t = proj.shape[0]
    q1, k1 = MLA_Q_RANK, MLA_Q_RANK + MLA_KV_RANK

    def body(p_ref, qg_ref, kg_ref, cq_ref, ck_ref):
        cq = p_ref[:, 0:q1]
        ck = p_ref[:, q1:k1]
        cq_ref[...] = (cq * lax.rsqrt(jnp.mean(cq * cq, axis=-1, keepdims=True) + EPS) * qg_ref[...]).astype(BF16)
        ck_ref[...] = (ck * lax.rsqrt(jnp.mean(ck * ck, axis=-1, keepdims=True) + EPS) * kg_ref[...]).astype(BF16)

    return pl.pallas_call(
        body, name=name, grid=(t // ROWS,),
        in_specs=[_row_spec(MLA_IN), _const_spec((1, MLA_Q_RANK)), _const_spec((1, MLA_KV_RANK))],
        out_specs=(_row_spec(MLA_Q_RANK), _row_spec(MLA_KV_RANK)),
        out_shape=(jax.ShapeDtypeStruct((t, MLA_Q_RANK), BF16), jax.ShapeDtypeStruct((t, MLA_KV_RANK), BF16)),
        compiler_params=_params("parallel"),
    )(proj, qg, kvg)


def _mla_prep_bwd(proj, qg, kvg, dcq, dck, dkr, *, name):
    t = proj.shape[0]
    q1, k1 = MLA_Q_RANK, MLA_Q_RANK + MLA_KV_RANK

    def body(p_ref, qg_ref, kg_ref, dq_ref, dk_ref, dr_ref, dp_ref, st_ref):
        @pl.when(pl.program_id(0) == 0)
        def _():
            st_ref[...] = jnp.zeros_like(st_ref)

        for lo, hi, g_ref, d_ref in ((0, q1, qg_ref, dq_ref), (q1, k1, kg_ref, dk_ref)):
            xv = p_ref[:, lo:hi]
            dn = d_ref[...]
            r = lax.rsqrt(jnp.mean(xv * xv, axis=-1, keepdims=True) + EPS)
            xh = xv * r
            dxh = dn * g_ref[...]
            dp_ref[:, lo:hi] = (r * (dxh - xh * jnp.mean(dxh * xh, axis=-1, keepdims=True))).astype(dp_ref.dtype)
            st_ref[0:1, lo:hi] += jnp.sum(dn * xh, axis=0, keepdims=True)
        dp_ref[:, k1:MLA_IN] = dr_ref[:, 0:MLA_ROPE].astype(dp_ref.dtype)

    return pl.pallas_call(
        body, name=name, grid=(t // ROWS,),
        in_specs=[_row_spec(MLA_IN), _const_spec((1, MLA_Q_RANK)), _const_spec((1, MLA_KV_RANK)),
                  _row_spec(MLA_Q_RANK), _row_spec(MLA_KV_RANK), _row_spec(LANES)],
        out_specs=(_row_spec(MLA_IN), _const_spec((8, MLA_IN))),
        out_shape=(jax.ShapeDtypeStruct((t, MLA_IN), BF16), jax.ShapeDtypeStruct((8, MLA_IN), F32)),
        compiler_params=_params("arbitrary"),
    )(proj, qg, kvg, dcq, dck, dkr)


def _rope(xr, cos_t, sin_t, *, name):
    t, w = xr.shape
    ns = w // LANES

    def body(x_ref, c_ref, s_ref, o_ref):
        cv, sv = c_ref[...], s_ref[...]
        lane = lax.broadcasted_iota(jnp.int32, (ROWS, LANES), 1)
        first = (lane % MLA_ROPE) < (MLA_ROPE // 2)
        for i in range(ns):
            sl = slice(i * LANES, (i + 1) * LANES)
            xv = x_ref[:, sl]
            sw = jnp.where(first, pltpu.roll(xv, LANES - MLA_ROPE // 2, 1), pltpu.roll(xv, MLA_ROPE // 2, 1))
            o_ref[:, sl] = xv * cv + sw * sv

    return pl.pallas_call(
        body, name=name, grid=(t // ROWS,),
        in_specs=[_row_spec(w), _row_spec(LANES), _row_spec(LANES)], out_specs=_row_spec(w),
        out_shape=jax.ShapeDtypeStruct((t, w), F32), compiler_params=_params("parallel"),
    )(xr, cos_t, sin_t)


def _rope_bwd(dr, cos_t, sin_t, *, name):
    t, w = dr.shape
    ns = w // LANES

    def body(d_ref, c_ref, s_ref, o_ref):
        cv, sv = c_ref[...], s_ref[...]
        lane = lax.broadcasted_iota(jnp.int32, (ROWS, LANES), 1)
        first = (lane % MLA_ROPE) < (MLA_ROPE // 2)
        for i in range(ns):
            sl = slice(i * LANES, (i + 1) * LANES)
            dv = d_ref[:, sl]
            ds = dv * sv
            sw = jnp.where(first, pltpu.roll(ds, LANES - MLA_ROPE // 2, 1), pltpu.roll(ds, MLA_ROPE // 2, 1))
            o_ref[:, sl] = dv * cv + sw

    return pl.pallas_call(
        body, name=name, grid=(t // ROWS,),
        in_specs=[_row_spec(w), _row_spec(LANES), _row_spec(LANES)], out_specs=_row_spec(w),
        out_shape=jax.ShapeDtypeStruct((t, w), F32), compiler_params=_params("parallel"),
    )(dr, cos_t, sin_t)


ATT_BLOCK = 256
ATT_HEAD_BATCH = 4
ATT_HEAD_BATCH_BWD = 4
ATT_SCALE = MLA_QK ** -0.5


def _causal_mask(i, j, blk):
    rows = i * blk + lax.broadcasted_iota(jnp.int32, (blk, blk), 0)
    cols = j * blk + lax.broadcasted_iota(jnp.int32, (blk, blk), 1)
    return cols <= rows


def _attn_fwd(q, k, v, *, name):
    nh, t, dk = q.shape
    dv = v.shape[-1]
    blk = min(ATT_BLOCK, t)

    hb = ATT_HEAD_BATCH
    hs = range(hb)

    def body(q_ref, k_ref, v_ref, o_ref, l_ref):
        i = pl.program_id(1)
        qv = [q_ref[h] for h in hs]

        def step(j, carry):
            m, l, acc = carry[:hb], carry[hb:2 * hb], carry[2 * hb:]
            off = pl.multiple_of(j * blk, blk)
            mask = _causal_mask(i, j, blk)
            s = [_dotb(qv[h], k_ref[h, pl.ds(off, blk), :], NT) for h in hs]
            s = [jnp.where(mask, s[h] * ATT_SCALE, NEG) for h in hs]
            m_new = [jnp.maximum(m[h], jnp.max(s[h], axis=-1, keepdims=True)) for h in hs]
            p = [jnp.exp(s[h] - m_new[h]) for h in hs]
            pv = [_dotb(p[h], v_ref[h, pl.ds(off, blk), :], NN) for h in hs]
            alpha = [jnp.exp(m[h] - m_new[h]) for h in hs]
            l = [alpha[h] * l[h] + jnp.sum(p[h], axis=-1, keepdims=True) for h in hs]
            acc = [alpha[h] * acc[h] + pv[h] for h in hs]
            return tuple(m_new) + tuple(l) + tuple(acc)

        init = ((jnp.full((blk, 1), NEG, F32),) * hb + (jnp.zeros((blk, 1), F32),) * hb
                + (jnp.zeros((blk, dv), F32),) * hb)
        out = lax.fori_loop(0, i + 1, step, init)
        for h in hs:
            m, l, acc = out[h], out[hb + h], out[2 * hb + h]
            o_ref[h] = acc / l
            l_ref[h] = jnp.broadcast_to(m + jnp.log(l), (blk, LANES))

    return pl.pallas_call(
        body, name=name, grid=(nh // hb, t // blk),
        in_specs=[pl.BlockSpec((hb, blk, dk), lambda h, i: (h, i, 0)), pl.BlockSpec((hb, t, dk), lambda h, i: (h, 0, 0)),
                  pl.BlockSpec((hb, t, dv), lambda h, i: (h, 0, 0))],
        out_specs=(pl.BlockSpec((hb, blk, dv), lambda h, i: (h, i, 0)),
                   pl.BlockSpec((hb, blk, LANES), lambda h, i: (h, i, 0))),
        out_shape=(jax.ShapeDtypeStruct((nh, t, dv), F32), jax.ShapeDtypeStruct((nh, t, LANES), F32)),
        compiler_params=_params("parallel", "parallel"),
    )(q, k, v)


def _attn_bwd(q, k, v, o, lse, do, *, name):
    nh, t, dk = q.shape
    dv = v.shape[-1]
    blk = min(ATT_BLOCK, t)
    nb = t // blk

    hb = ATT_HEAD_BATCH_BWD
    hs = range(hb)

    def body(q_ref, k_ref, v_ref, o_ref, l_ref, do_ref, dq_ref, dk_ref, dv_ref):
        j = pl.program_id(1)

        @pl.when(j == 0)
        def _():
            dq_ref[...] = jnp.zeros_like(dq_ref)

        kv = [k_ref[h] for h in hs]
        vv = [v_ref[h] for h in hs]

        def step(i, carry):
            dk_acc, dv_acc = carry[:hb], carry[hb:]
            off = pl.multiple_of(i * blk, blk)
            rows = pl.ds(off, blk)
            mask = _causal_mask(i, j, blk)
            qv = [q_ref[h, rows, :] for h in hs]
            dov = [do_ref[h, rows, :] for h in hs]
            s = [_dotb(qv[h], kv[h], NT) for h in hs]
            dp = [_dotb(dov[h], vv[h], NT) for h in hs]
            p = [jnp.exp(jnp.where(mask, s[h] * ATT_SCALE, NEG) - l_ref[h, rows, :][:, 0:1]) for h in hs]
            delta = [jnp.sum(dov[h] * o_ref[h, rows, :], axis=-1, keepdims=True) for h in hs]
            ds = [p[h] * (dp[h] - delta[h]) * ATT_SCALE for h in hs]
            dvn = [_dotb(p[h], dov[h], TN) for h in hs]
            dkn = [_dotb(ds[h], qv[h], TN) for h in hs]
            dqn = [_dotb(ds[h], kv[h], NN) for h in hs]
            for h in hs:
                dq_ref[h, rows, :] += dqn[h]
            return tuple(dk_acc[h] + dkn[h] for h in hs) + tuple(dv_acc[h] + dvn[h] for h in hs)

        out = lax.fori_loop(j, nb, step, (jnp.zeros((blk, dk), F32),) * hb + (jnp.zeros((blk, dv), F32),) * hb)
        for h in hs:
            dk_ref[h] = out[h]
            dv_ref[h] = out[hb + h]

    full = lambda w: pl.BlockSpec((hb, t, w), lambda h, j: (h, 0, 0))
    part = lambda w: pl.BlockSpec((hb, blk, w), lambda h, j: (h, j, 0))
    return pl.pallas_call(
        body, name=name, grid=(nh // hb, nb),
        in_specs=[full(dk), part(dk), part(dv), full(dv), full(LANES), full(dv)],
        out_specs=(full(dk), part(dk), part(dv)),
        out_shape=(jax.ShapeDtypeStruct((nh, t, dk), F32), jax.ShapeDtypeStruct((nh, t, dk), F32),
                   jax.ShapeDtypeStruct((nh, t, dv), F32)),
        compiler_params=_params("parallel", "arbitrary"),
    )(q, k, v, o, lse, do)


def _swap_halves(xv, first):
    return jnp.where(first, pltpu.roll(xv, LANES - MLA_ROPE // 2, 1), pltpu.roll(xv, MLA_ROPE // 2, 1))


def _rope_qk(qf, proj, cos_t, sin_t, *, name):
    t = qf.shape[0]
    nrope = MLA_HEADS * MLA_ROPE
    q_blk = MLA_HEADS * MLA_NOPE // nrope
    k_blk = (MLA_Q_RANK + MLA_KV_RANK) // LANES

    def body(q_ref, p_ref, c_ref, s_ref, qo_ref, ko_ref):
        cv, sv = c_ref[...], s_ref[...]
        lane = lax.broadcasted_iota(jnp.int32, (ROWS, LANES), 1)
        first = (lane % MLA_ROPE) < (MLA_ROPE // 2)
        for i in range(nrope // LANES):
            sl = slice(i * LANES, (i + 1) * LANES)
            xv = q_ref[:, sl].astype(F32)
            qo_ref[:, sl] = (xv * cv + _swap_halves(xv, first) * sv).astype(qo_ref.dtype)
        kv = jnp.where(lane < MLA_ROPE, p_ref[...], 0.0)
        ko_ref[...] = (kv * cv + _swap_halves(kv, first) * sv).astype(ko_ref.dtype)

    return pl.pallas_call(
        body, name=name, grid=(t // ROWS,),
        in_specs=[pl.BlockSpec((ROWS, nrope), lambda i: (i, q_blk)), pl.BlockSpec((ROWS, LANES), lambda i: (i, k_blk)),
                  _row_spec(LANES), _row_spec(LANES)],
        out_specs=(_row_spec(nrope), _row_spec(LANES)),
        out_shape=(jax.ShapeDtypeStruct((t, nrope), BF16), jax.ShapeDtypeStruct((t, LANES), BF16)),
        compiler_params=_params("parallel"),
    )(qf, proj, cos_t, sin_t)


def _rope_qk_bwd(dqr, dkr_parts, cos_t, sin_t, *, name):
    t, nrope = dqr.shape
    ng = dkr_parts.shape[0]

    def body(d_ref, k_ref, c_ref, s_ref, qo_ref, ko_ref):
        cv, sv = c_ref[...], s_ref[...]
        lane = lax.broadcasted_iota(jnp.int32, (ROWS, LANES), 1)
        first = (lane % MLA_ROPE) < (MLA_ROPE // 2)
        for i in range(nrope // LANES):
            sl = slice(i * LANES, (i + 1) * LANES)
            dv = d_ref[:, sl]
            qo_ref[:, sl] = (dv * cv + _swap_halves(dv * sv, first)).astype(qo_ref.dtype)
        dk = k_ref[0]
        for g in range(1, ng):
            dk = dk + k_ref[g]
        dk = jnp.where(lane < MLA_ROPE, dk, 0.0)
        ko_ref[...] = jnp.where(lane < MLA_ROPE, dk * cv + _swap_halves(dk * sv, first), 0.0)

    return pl.pallas_call(
        body, name=name, grid=(t // ROWS,),
        in_specs=[_row_spec(nrope), pl.BlockSpec((ng, ROWS, LANES), lambda i: (0, i, 0)), _row_spec(LANES),
                  _row_spec(LANES)],
        out_specs=(_row_spec(nrope), _row_spec(LANES)),
        out_shape=(jax.ShapeDtypeStruct((t, nrope), BF16), jax.ShapeDtypeStruct((t, LANES), F32)),
        compiler_params=_params("parallel"),
    )(dqr, dkr_parts, cos_t, sin_t)


def _attn_tm_fwd(qf, qr, kvf, kr, *, name):
    t = qf.shape[0]
    nh, dn, dr, dv = MLA_HEADS, MLA_NOPE, MLA_ROPE, MLA_V
    blk = min(ATT_BLOCK, t)
    hb = ATT_HEAD_BATCH
    hs = range(hb)

    def body(q_ref, qr_ref, kv_ref, kr_ref, o_ref, l_ref):
        i = pl.program_id(1)
        qn = [q_ref[:, h * dn:(h + 1) * dn].astype(MXU_DTYPE) for h in hs]
        qrh = [qr_ref[:, h * dr:(h + 1) * dr] for h in hs]

        def step(j, carry):
            m, l, acc = carry[:hb], carry[hb:2 * hb], carry[2 * hb:]
            rows = pl.ds(pl.multiple_of(j * blk, blk), blk)
            mask = _causal_mask(i, j, blk)
            krj = kr_ref[rows, 0:dr]
            s = [_dotb(qn[h], kv_ref[rows, h * (dn + dv):h * (dn + dv) + dn], NT) for h in hs]
            sr = [_dotb(qrh[h], krj, NT) for h in hs]
            s = [jnp.where(mask, (s[h] + sr[h]) * ATT_SCALE, NEG) for h in hs]
            m_new = [jnp.maximum(m[h], jnp.max(s[h], axis=-1, keepdims=True)) for h in hs]
            p = [jnp.exp(s[h] - m_new[h]) for h in hs]
            pv = [_dotb(p[h], kv_ref[rows, h * (dn + dv) + dn:(h + 1) * (dn + dv)], NN) for h in hs]
            alpha = [jnp.exp(m[h] - m_new[h]) for h in hs]
            l = [alpha[h] * l[h] + jnp.sum(p[h], axis=-1, keepdims=True) for h in hs]
            acc = [alpha[h] * acc[h] + pv[h] for h in hs]
            return tuple(m_new) + tuple(l) + tuple(acc)

        init = ((jnp.full((blk, 1), NEG, F32),) * hb + (jnp.zeros((blk, 1), F32),) * hb
                + (jnp.zeros((blk, dv), F32),) * hb)
        out = lax.fori_loop(0, i + 1, step, init)
        for h in hs:
            m, l, acc = out[h], out[hb + h], out[2 * hb + h]
            o_ref[:, h * dv:(h + 1) * dv] = (acc / l).astype(o_ref.dtype)
            l_ref[h] = jnp.broadcast_to(m + jnp.log(l), (blk, LANES))

    return pl.pallas_call(
        body, name=name, grid=(nh // hb, t // blk),
        in_specs=[pl.BlockSpec((blk, hb * dn), lambda g, i: (i, g)), pl.BlockSpec((blk, hb * dr), lambda g, i: (i, g)),
                  pl.BlockSpec((t, hb * (dn + dv)), lambda g, i: (0, g)), pl.BlockSpec((t, LANES), lambda g, i: (0, 0))],
        out_specs=(pl.BlockSpec((blk, hb * dv), lambda g, i: (i, g)),
                   pl.BlockSpec((hb, blk, LANES), lambda g, i: (g, i, 0))),
        out_shape=(jax.ShapeDtypeStruct((t, nh * dv), BF16), jax.ShapeDtypeStruct((nh, t, LANES), F32)),
        compiler_params=_params("parallel", "parallel"),
    )(qf, qr, kvf, kr)


def _attn_tm_bwd(qf, qr, kvf, kr, o, lse, do, *, name):
    t = qf.shape[0]
    nh, dn, dr, dv = MLA_HEADS, MLA_NOPE, MLA_ROPE, MLA_V
    blk = min(ATT_BLOCK, t)
    nb = t // blk
    hb = ATT_HEAD_BATCH_BWD
    hs = range(hb)
    ng = nh // hb

    def body(q_ref, qr_ref, kv_ref, kr_ref, o_ref, l_ref, do_ref, dqn_ref, dqr_ref, dkv_ref, dkr_ref):
        j = pl.program_id(1)

        @pl.when(j == 0)
        def _():
            dqn_ref[...] = jnp.zeros_like(dqn_ref)
            dqr_ref[...] = jnp.zeros_like(dqr_ref)

        kn = [kv_ref[:, h * (dn + dv):h * (dn + dv) + dn] for h in hs]
        vv = [kv_ref[:, h * (dn + dv) + dn:(h + 1) * (dn + dv)] for h in hs]
        krj = kr_ref[:, 0:dr]

        def step(i, carry):
            dkn_acc, dv_acc, dkr_acc = carry[:hb], carry[hb:2 * hb], carry[2 * hb]
            rows = pl.ds(pl.multiple_of(i * blk, blk), blk)
            mask = _causal_mask(i, j, blk)
            qn = [q_ref[rows, h * dn:(h + 1) * dn].astype(MXU_DTYPE) for h in hs]
            qrh = [qr_ref[rows, h * dr:(h + 1) * dr] for h in hs]
            dov = [do_ref[rows, h * dv:(h + 1) * dv] for h in hs]
            s = [_dotb(qn[h], kn[h], NT) for h in hs]
            sr = [_dotb(qrh[h], krj, NT) for h in hs]
            dp = [_dotb(dov[h], vv[h], NT) for h in hs]
            p = [jnp.exp(jnp.where(mask, (s[h] + sr[h]) * ATT_SCALE, NEG) - l_ref[h, rows, :][:, 0:1]) for h in hs]
            delta = [jnp.sum(dov[h].astype(F32) * o_ref[rows, h * dv:(h + 1) * dv].astype(F32), axis=-1, keepdims=True)
                     for h in hs]
            ds = [p[h] * (dp[h] - delta[h]) * ATT_SCALE for h in hs]
            dvn = [_dotb(p[h], dov[h], TN) for h in hs]
            dknn = [_dotb(ds[h], qn[h], TN) for h in hs]
            dkrn = [_dotb(ds[h], qrh[h], TN) for h in hs]
            dqnn = [_dotb(ds[h], kn[h], NN) for h in hs]
            dqrn = [_dotb(ds[h], krj, NN) for h in hs]
            for h in hs:
                dqn_ref[rows, h * dn:(h + 1) * dn] += dqnn[h]
                dqr_ref[rows, h * dr:(h + 1) * dr] += dqrn[h]
            dkr_new = dkr_acc
            for h in hs:
                dkr_new = dkr_new + dkrn[h]
            return (tuple(dkn_acc[h] + dknn[h] for h in hs) + tuple(dv_acc[h] + dvn[h] for h in hs) + (dkr_new,))

        init = (jnp.zeros((blk, dn), F32),) * hb + (jnp.zeros((blk, dv), F32),) * hb + (jnp.zeros((blk, dr), F32),)
        out = lax.fori_loop(j, nb, step, init)
        for h in hs:
            dkv_ref[:, h * (dn + dv):h * (dn + dv) + dn] = out[h].astype(dkv_ref.dtype)
            dkv_ref[:, h * (dn + dv) + dn:(h + 1) * (dn + dv)] = out[hb + h].astype(dkv_ref.dtype)
        dkr_ref[0, :, 0:dr] = out[2 * hb]
        dkr_ref[0, :, dr:LANES] = jnp.zeros((blk, LANES - dr), F32)

    full = lambda w: pl.BlockSpec((t, w), lambda g, j: (0, g))
    return pl.pallas_call(
        body, name=name, grid=(ng, nb),
        in_specs=[full(hb * dn), full(hb * dr), pl.BlockSpec((blk, hb * (dn + dv)), lambda g, j: (j, g)),
                  pl.BlockSpec((blk, LANES), lambda g, j: (j, 0)), full(hb * dv),
                  pl.BlockSpec((hb, t, LANES), lambda g, j: (g, 0, 0)), full(hb * dv)],
        out_specs=(full(hb * dn), full(hb * dr), pl.BlockSpec((blk, hb * (dn + dv)), lambda g, j: (j, g)),
                   pl.BlockSpec((1, blk, LANES), lambda g, j: (g, j, 0))),
        out_shape=(jax.ShapeDtypeStruct((t, nh * dn), F32), jax.ShapeDtypeStruct((t, nh * dr), F32),
                   jax.ShapeDtypeStruct((t, nh * (dn + dv)), BF16), jax.ShapeDtypeStruct((ng, t, LANES), F32)),
        compiler_params=_params("parallel", "arbitrary"),
    )(qf, qr, kvf, kr, o, lse, do)


def _ada_mod(c_all, ada_w, ada_b_cols, *, name):
    nl, d, wc = ada_w.shape

    def body(c_ref, w_ref, b_ref, o_ref):
        cv = c_ref[...]
        o_ref[0] = _dotb(cv * _sigmoid(cv), w_ref[0], NN) + b_ref[0]

    return pl.pallas_call(
        body, name=name, grid=(nl,),
        in_specs=[_const_spec((N_DEV, d)), pl.BlockSpec((1, d, wc), lambda l: (l, 0, 0)),
                  pl.BlockSpec((1, 1, wc), lambda l: (l, 0, 0))],
        out_specs=pl.BlockSpec((1, N_DEV, wc), lambda l: (l, 0, 0)),
        out_shape=jax.ShapeDtypeStruct((nl, N_DEV, wc), F32), compiler_params=_params("parallel"),
    )(c_all, ada_w, ada_b_cols)


def _adam_math(g, w, m, v):
    m2 = ADAM_B1 * m + (1.0 - ADAM_B1) * g
    v2 = ADAM_B2 * v + (1.0 - ADAM_B2) * (g * g)
    delta = -ADAM_LR * ((m2 / ADAM_BC1) / (jnp.sqrt(v2 / ADAM_BC2) + ADAM_EPS) + ADAM_WD * w)
    return delta, m2, v2


def _ada_grad_adamw(c_all, dmod_cols, w, m, v, *, name):
    nl, d, wc = w.shape
    tr = 256

    def body(c_ref, dm_ref, w_ref, m_ref, v_ref, g_ref, d_ref, m2_ref, v2_ref):
        cv = c_ref[...]
        g = _dotf(cv * _sigmoid(cv), dm_ref[0], TN)
        delta, m2, v2 = _adam_math(g, w_ref[0], m_ref[0], v_ref[0])
        g_ref[0], d_ref[0], m2_ref[0], v2_ref[0] = g, delta, m2, v2

    blk = pl.BlockSpec((1, tr, wc), lambda l, i: (l, i, 0))
    return pl.pallas_call(
        body, name=name, grid=(nl, d // tr),
        in_specs=[pl.BlockSpec((N_DEV, tr), lambda l, i: (0, i)), pl.BlockSpec((1, N_DEV, wc), lambda l, i: (l, 0, 0)),
                  blk, blk, blk],
        out_specs=(blk,) * 4, out_shape=(jax.ShapeDtypeStruct(w.shape, F32),) * 4,
        compiler_params=_params("parallel", "parallel"),
    )(c_all, dmod_cols, w, m, v)


def _adamw(parts, w, m, v, *, name):
    nl, r, c = w.shape
    ns = parts[0].shape[0]
    lanes_padded = -(-c // LANES) * LANES
    row_bytes = 2 * nl * ns * lanes_padded * parts[0].dtype.itemsize
    tr = _pick(r, min(256, max(16, (VMEM_LIMIT // 2) // row_bytes)), 16)
    tc = c
    if tr * row_bytes > VMEM_LIMIT // 2:
        tc = _pick(c, max(LANES, c * (VMEM_LIMIT // 2) // (tr * row_bytes)))

    def body(*refs):
        p_refs = refs[:nl]
        w_ref, m_ref, v_ref, g_ref, d_ref, m2_ref, v2_ref = refs[nl:]
        layer = pl.program_id(0)
        for q in range(nl):
            @pl.when(layer == q)
            def _(q=q):
                g = p_refs[q][0].astype(F32)
                for s in range(1, ns):
                    g = g + p_refs[q][s].astype(F32)
                delta, m2, v2 = _adam_math(g, w_ref[0], m_ref[0], v_ref[0])
                g_ref[0], d_ref[0], m2_ref[0], v2_ref[0] = g, delta, m2, v2

    blk = pl.BlockSpec((1, tr, tc), lambda l, i, j: (l, i, j))
    p_specs = [pl.BlockSpec((ns, tr, tc), lambda l, i, j, q=q: (0, jnp.where(l == q, i, 0), jnp.where(l == q, j, 0)))
               for q in range(nl)]
    return pl.pallas_call(
        body, name=name, grid=(nl, r // tr, c // tc),
        in_specs=p_specs + [blk, blk, blk],
        out_specs=(blk,) * 4, out_shape=(jax.ShapeDtypeStruct(w.shape, F32),) * 4,
        compiler_params=_params("arbitrary", "arbitrary", "arbitrary"),
    )(*parts, w, m, v)


def _sum_parts(parts, *, name):
    ns, r, c = parts.shape

    def body(p_ref, o_ref):
        acc = p_ref[0]
        for s in range(1, ns):
            acc = acc + p_ref[s]
        o_ref[...] = acc

    return pl.pallas_call(
        body, name=name, out_shape=jax.ShapeDtypeStruct((r, c), F32),
        in_specs=[pl.BlockSpec(memory_space=pltpu.VMEM)], out_specs=pl.BlockSpec(memory_space=pltpu.VMEM),
    )(parts)


def _pack(arrs):
    flat = jnp.concatenate([a.reshape(-1).astype(F32) for a in arrs])
    pad = (-flat.shape[0]) % (8 * LANES)
    return jnp.pad(flat, (0, pad)).reshape(-1, LANES)


def _unpack(packed, shapes, lead=()):
    flat = packed.reshape(lead + (-1,))
    out, off = [], 0
    for s in shapes:
        n = math.prod(s)
        out.append(flat[..., off:off + n].reshape(lead + tuple(s)))
        off += n
    return out


def _gather_cols(g):
    _, nl, r, cs = g.shape
    return jnp.transpose(g, (1, 2, 0, 3)).reshape(nl, r, N_DEV * cs)


def _gather_rows(g):
    _, nl, rs, c = g.shape
    return jnp.transpose(g, (1, 0, 2, 3)).reshape(nl, N_DEV * rs, c)


def _scatter_cols(full):
    nl, r, c = full.shape
    return jnp.transpose(full.reshape(nl, r, N_DEV, c // N_DEV), (2, 0, 1, 3))


def _scatter_rows(full):
    nl, r, c = full.shape
    return jnp.transpose(full.reshape(nl, N_DEV, r // N_DEV, c), (1, 0, 2, 3))


def _row(v):
    return v.reshape(1, -1)


def _local_step(x, target, mod, cos_t, sin_t, rep, get_weights, put_grads):
    t = x.shape[0]
    saved = []
    for layer in range(DEPTH):
        j = layer // 2
        tag = f"l{layer}"
        shift_m, scale_m, gate_m, shift_f, scale_f, gate_f = [_row(mod[layer, i]) for i in range(N_MOD)]
        lw = dict(get_weights(layer, "mix", x))
        rec = {"x0": x, "lw": lw}
        h = _adaln_fwd(x, _row(rep["norm_mix_g"][layer]), scale_m, shift_m, name=f"adaln_mix_{tag}")
        rec["h"] = h
        if layer % 2 == 0:
            proj = _mm(h, lw["wt_in"], mode="nt", out_dtype=F32, tm=256, tn=GDN_MAIN, b_rows=GDN_MAIN,
                       dep=lw["dep_mix"], name=f"gdn_in_{tag}")
            ab = _mm(h, lw["wt_ab"], mode="nt", out_dtype=F32, name=f"gdn_in_ab_{tag}")
            qkv = _gdn_prep_fwd(proj, rep["gdn_conv_wt"][j], name=f"gdn_prep_{tag}")
            gbeta = _gdn_gate_fwd(ab, rep["gdn_gate_prm"][j], name=f"gdn_gate_{tag}")
            gbc = jnp.broadcast_to(jnp.transpose(gbeta[:, 0:GDN_HEADS])[:, :, None], (GDN_HEADS, t, LANES))
            bbc = jnp.broadcast_to(jnp.transpose(gbeta[:, GDN_HEADS:2 * GDN_HEADS])[:, :, None],
                                   (GDN_HEADS, t, LANES))
            o, states, tinvs = _gdn_chunk_fwd(qkv, gbc, bbc, name=f"gdn_chunk_{tag}")
            og = _gdn_onorm_fwd(o, proj, _row(rep["gdn_norm_g"][j]), name=f"gdn_onorm_{tag}")
            x, y = _mm_resid(og, lw["w_out"], x, gate_m, name=f"gdn_out_{tag}")
            rec.update(proj=proj, ab=ab, qkv=qkv, gbc=gbc, bbc=bbc, states=states, tinvs=tinvs, o=o, og=og, y=y)
        else:
            proj = _mm(h, lw["w_in"], mode="nn", out_dtype=F32, dep=lw["dep_mix"], name=f"mla_in_{tag}")
            cq, ck = _mla_prep_fwd(proj, _row(rep["mla_q_norm_g"][j]), _row(rep["mla_kv_norm_g"][j]),
                                   name=f"mla_prep_{tag}")
            qf = _mm(cq, lw["wt_uq"], mode="nt", out_dtype=BF16, name=f"mla_uq_{tag}")
            kvf = _mm(ck, lw["w_ukv"], mode="nn", out_dtype=BF16, name=f"mla_ukv_{tag}")
            qr, kr = _rope_qk(qf, proj, cos_t, sin_t, name=f"rope_{tag}")
            oc, lse = _attn_tm_fwd(qf, qr, kvf, kr, name=f"attn_{tag}")
            x, y = _mm_resid(oc, lw["w_out"], x, gate_m, name=f"mla_out_{tag}")
            rec.update(proj=proj, cq=cq, ck=ck, qf=qf, qr=qr, kvf=kvf, kr=kr, lse=lse, oc=oc, y=y)
        rec["x1"] = x
        lw.update(get_weights(layer, "ffn", x))
        h2 = _adaln_fwd(x, _row(rep["norm_ffn_g"][layer]), scale_f, shift_f, name=f"adaln_ffn_{tag}")
        s, a2, b2 = _ffn_gu_fwd(h2, lw["wt_g"], lw["wt_u"], lw["dep_ffn"], name=f"ffn_gu_{tag}")
        x, y2 = _mm_resid(s, lw["w_down"], x, gate_f, name=f"ffn_down_{tag}")
        rec.update(h2=h2, a2=a2, b2=b2, s=s, y2=y2)
        saved.append(rec)

    dx, st, ls = _loss_head(x, _row(rep["final_norm_g"]), target, name="loss_head")
    loss = ls[0, 0]
    grads = {"final_norm_g": st[0]}
    per_layer = {k: [None] * DEPTH for k in ("norm_mix_g", "norm_ffn_g")}
    per_gdn = {k: [None] * 2 for k in ("gdn_conv_wt", "gdn_a_log", "gdn_dt_bias", "gdn_norm_g")}
    per_mla = {k: [None] * 2 for k in ("mla_q_norm_g", "mla_kv_norm_g")}
    dmod = [None] * DEPTH
    dep = jnp.zeros((8, LANES), F32)

    for layer in reversed(range(DEPTH)):
        j = layer // 2
        tag = f"l{layer}"
        rec = saved[layer]
        lw = rec["lw"]
        shift_m, scale_m, gate_m, shift_f, scale_f, gate_f = [_row(mod[layer, i]) for i in range(N_MOD)]
        dy2, st_g = _gate_bwd(dx, rec["y2"], gate_f, dep, name=f"gate_bwd_ffn_{tag}")
        dgate_f = st_g[0]
        dw_down = _mm(rec["s"], dy2, mode="tn", out_dtype=BF16, tm=256, tn=1024, name=f"ffn_down_dw_{tag}")
        da2, db2 = _ffn_down_dx(dy2, lw["w_down"], rec["a2"], rec["b2"], name=f"ffn_down_dx_{tag}")
        dwt_g = _mm(da2, rec["h2"], mode="tn", out_dtype=BF16, tm=256, tn=1024, name=f"ffn_g_dw_{tag}")
        dwt_u = _mm(db2, rec["h2"], mode="tn", out_dtype=BF16, tm=256, tn=1024, name=f"ffn_u_dw_{tag}")
        dep = put_grads(layer, "ffn", {"wt_g": dwt_g, "wt_u": dwt_u, "w_down": dw_down})
        dh2 = _mm(da2, lw["wt_g"], mode="nn", out_dtype=F32, tm=256, tn=1024, name=f"ffn_g_dx_{tag}")
        dh2 = _mm(db2, lw["wt_u"], mode="nn", out_dtype=BF16, add=dh2, tm=256, tn=1024, name=f"ffn_u_dx_{tag}")
        dx, st_n = _adaln_bwd(rec["x1"], _row(rep["norm_ffn_g"][layer]), scale_f, shift_f, dh2, dx, dep,
                              name=f"adaln_ffn_bwd_{tag}")
        per_layer["norm_ffn_g"][layer] = st_n[0]
        dscale_f, dshift_f = st_n[1], st_n[2]
        dy, st_g = _gate_bwd(dx, rec["y"], gate_m, dep, name=f"gate_bwd_mix_{tag}")
        dgate_m = st_g[0]
        big = {}
        if layer % 2 == 0:
            big["w_out"] = _mm(rec["og"], dy, mode="tn", out_dtype=BF16, name=f"gdn_out_dw_{tag}")
            dog = _mm(dy, lw["w_out"], mode="nt", out_dtype=BF16, name=f"gdn_out_dx_{tag}")
            do, dgp, st_o = _gdn_onorm_bwd(rec["o"], rec["proj"], _row(rep["gdn_norm_g"][j]), dog,
                                           name=f"gdn_onorm_bwd_{tag}")
            per_gdn["gdn_norm_g"][j] = st_o[0]
            dqkv, dgc_, dbc_ = _gdn_chunk_bwd(rec["qkv"], rec["gbc"], rec["bbc"], rec["states"], rec["tinvs"], do,
                                               name=f"gdn_chunk_bwd_{tag}")
            dgb = jnp.concatenate([jnp.transpose(dgc_[:, :, 0]), jnp.transpose(dbc_[:, :, 0])], axis=1)
            dgb = jnp.pad(dgb, ((0, 0), (0, LANES - 2 * GDN_HEADS)))
            dab, st_a = _gdn_gate_bwd(rec["ab"], rep["gdn_gate_prm"][j], dgb, name=f"gdn_gate_bwd_{tag}")
            per_gdn["gdn_a_log"][j] = st_a[0, :GDN_HEADS]
            per_gdn["gdn_dt_bias"][j] = st_a[1, :GDN_HEADS]
            dpre, dcw = _gdn_prep_bwd(rec["proj"], rep["gdn_conv_wt"][j], dqkv, name=f"gdn_prep_bwd_{tag}")
            per_gdn["gdn_conv_wt"][j] = dcw
            dproj = jnp.concatenate([dpre, dgp], axis=1)
            dw_main = _mm(dproj, rec["h"], mode="tn", out_dtype=BF16, tm=512, tn=1024, name=f"gdn_in_dw_{tag}")
            dw_ab = _mm(dab, rec["h"], mode="tn", out_dtype=BF16, tn=1024, name=f"gdn_in_ab_dw_{tag}")
            big["wt_in"] = jnp.concatenate([dw_main, dw_ab[:2 * GDN_HEADS]], axis=0)
            dep = put_grads(layer, "gdn", big)
            dh_ab = _mm(dab, lw["wt_ab"], mode="nn", out_dtype=F32, tn=1024, name=f"gdn_in_ab_dx_{tag}")
            dh = _mm(dproj, lw["wt_in"], mode="nn", out_dtype=BF16, add=dh_ab, tm=256, tn=1024, b_rows=GDN_MAIN,
                     name=f"gdn_in_dx_{tag}")
        else:
            big["w_out"] = _mm(rec["oc"], dy, mode="tn", out_dtype=BF16, name=f"mla_out_dw_{tag}")
            doc = _mm(dy, lw["w_out"], mode="nt", out_dtype=BF16, name=f"mla_out_dx_{tag}")
            dqn, dqr, dkvf, dkr_parts = _attn_tm_bwd(rec["qf"], rec["qr"], rec["kvf"], rec["kr"], rec["oc"],
                                                     rec["lse"], doc, name=f"attn_bwd_{tag}")
            dqr_un, dkr_un = _rope_qk_bwd(dqr, dkr_parts, cos_t, sin_t, name=f"rope_bwd_{tag}")
            n_nope = MLA_HEADS * MLA_NOPE
            big["wt_uq"] = jnp.concatenate(
                [_mm(dqn, rec["cq"], mode="tn", out_dtype=BF16, name=f"mla_uq_dw_nope_{tag}"),
                 _mm(dqr_un, rec["cq"], mode="tn", out_dtype=BF16, name=f"mla_uq_dw_rope_{tag}")], axis=0)
            big["w_ukv"] = _mm(rec["ck"], dkvf, mode="tn", out_dtype=BF16, name=f"mla_ukv_dw_{tag}")
            dcq = _mm(dqr_un, lw["wt_uq"][n_nope:], mode="nn", out_dtype=F32, name=f"mla_uq_dx_rope_{tag}")
            dcq = _mm(dqn, lw["wt_uq"], mode="nn", out_dtype=F32, add=dcq, b_rows=n_nope,
                      name=f"mla_uq_dx_nope_{tag}")
            dck = _mm(dkvf, lw["w_ukv"], mode="nt", out_dtype=F32, name=f"mla_ukv_dx_{tag}")
            dproj, st_p = _mla_prep_bwd(rec["proj"], _row(rep["mla_q_norm_g"][j]), _row(rep["mla_kv_norm_g"][j]),
                                        dcq, dck, dkr_un, name=f"mla_prep_bwd_{tag}")
            per_mla["mla_q_norm_g"][j] = st_p[0, :MLA_Q_RANK]
            per_mla["mla_kv_norm_g"][j] = st_p[0, MLA_Q_RANK:MLA_Q_RANK + MLA_KV_RANK]
            big["w_in"] = _mm(rec["h"], dproj, mode="tn", out_dtype=BF16, name=f"mla_in_dw_{tag}")
            dep = put_grads(layer, "mla", big)
            dh = _mm(dproj, lw["w_in"], mode="nt", out_dtype=BF16, name=f"mla_in_dx_{tag}")
        dx, st_n = _adaln_bwd(rec["x0"], _row(rep["norm_mix_g"][layer]), scale_m, shift_m, dh, dx, dep,
                              name=f"adaln_mix_bwd_{tag}")
        per_layer["norm_mix_g"][layer] = st_n[0]
        dmod[layer] = jnp.stack([st_n[2], st_n[1], dgate_m, dshift_f, dscale_f, dgate_f])

    for d in (per_layer, per_gdn, per_mla):
        for k, v in d.items():
            grads[k] = jnp.stack(v)
    return loss, dx, jnp.stack(dmod), grads


BIG = ("gdn_w_in", "gdn_w_out", "mla_w_in", "mla_w_uq", "mla_w_ukv", "mla_w_out", "ffn_w_gate", "ffn_w_up",
       "ffn_w_down")
TRANSPOSED = ("gdn_w_in", "mla_w_uq", "ffn_w_gate", "ffn_w_up")
AHEAD = 2


def _view(k, a):
    return jnp.transpose(a, (0, 2, 1)) if k in TRANSPOSED else a
SMALL = ("ada_b", "norm_mix_g", "norm_ffn_g", "gdn_conv_w", "gdn_a_log", "gdn_dt_bias", "gdn_norm_g",
         "mla_q_norm_g", "mla_kv_norm_g", "final_norm_g")
WEIGHTS = ("ada_w", "ada_b", "norm_mix_g", "norm_ffn_g", "gdn_w_in", "gdn_conv_w", "gdn_a_log", "gdn_dt_bias",
           "gdn_norm_g", "gdn_w_out", "mla_w_in", "mla_q_norm_g", "mla_kv_norm_g", "mla_w_uq", "mla_w_ukv",
           "mla_w_out", "ffn_w_gate", "ffn_w_up", "ffn_w_down", "final_norm_g")


def _uq_to_kernel_layout(w, axis=-1):
    axis = axis % w.ndim
    lead, tail = w.shape[:axis], w.shape[axis + 1:]
    w4 = w.reshape(lead + (MLA_HEADS, MLA_QK) + tail)
    nope = lax.slice_in_dim(w4, 0, MLA_NOPE, axis=axis + 1).reshape(lead + (-1,) + tail)
    rope = lax.slice_in_dim(w4, MLA_NOPE, MLA_QK, axis=axis + 1).reshape(lead + (-1,) + tail)
    return jnp.concatenate([nope, rope], axis=axis)


def _uq_from_kernel_layout(w, axis=-1):
    axis = axis % w.ndim
    lead, tail = w.shape[:axis], w.shape[axis + 1:]
    nope = lax.slice_in_dim(w, 0, MLA_HEADS * MLA_NOPE, axis=axis).reshape(lead + (MLA_HEADS, MLA_NOPE) + tail)
    rope = lax.slice_in_dim(w, MLA_HEADS * MLA_NOPE, MLA_HEADS * MLA_QK, axis=axis).reshape(
        lead + (MLA_HEADS, MLA_ROPE) + tail)
    return jnp.concatenate([nope, rope], axis=axis + 1).reshape(lead + (-1,) + tail)


def _group_names(layer, kind):
    if kind == "ffn":
        return ("ffn_w_gate", "ffn_w_up", "ffn_w_down")
    return ("gdn_w_in", "gdn_w_out") if layer % 2 == 0 else ("mla_w_in", "mla_w_uq", "mla_w_ukv", "mla_w_out")


def _layer_index(name, layer):
    return layer if name.startswith("ffn") else layer // 2


def _cols(g):
    return jnp.transpose(g, (1, 0, 2)).reshape(g.shape[1], N_DEV * g.shape[2])


def _rows(g):
    return g.reshape(N_DEV * g.shape[1], g.shape[2])


def _uncols(full):
    r, c = full.shape
    return jnp.transpose(full.reshape(r, N_DEV, c // N_DEV), (1, 0, 2))


def _unrows(full):
    r, c = full.shape
    return full.reshape(N_DEV, r // N_DEV, c)


def _group_weights(layer, kind, got, token):
    if kind == "ffn":
        return {"wt_g": _rows(got["ffn_w_gate"]), "wt_u": _rows(got["ffn_w_up"]), "w_down": _rows(got["ffn_w_down"]),
                "dep_ffn": token}
    if layer % 2 == 0:
        wt_in = _rows(got["gdn_w_in"])
        return dict(wt_in=wt_in, wt_ab=jnp.pad(wt_in[GDN_MAIN:], ((0, LANES - 2 * GDN_HEADS), (0, 0))),
                    w_out=_rows(got["gdn_w_out"]), dep_mix=token)
    return dict(w_in=_rows(got["mla_w_in"]), wt_uq=_uq_to_kernel_layout(_rows(got["mla_w_uq"]), axis=0),
                w_ukv=_cols(got["mla_w_ukv"]), w_out=_rows(got["mla_w_out"]), dep_mix=token)


def _layer_grad_slots(kind, big):
    if kind == "ffn":
        return {"ffn_w_gate": _unrows(big["wt_g"]), "ffn_w_up": _unrows(big["wt_u"]),
                "ffn_w_down": _unrows(big["w_down"])}
    if kind == "gdn":
        return {"gdn_w_in": _unrows(big["wt_in"]), "gdn_w_out": _unrows(big["w_out"])}
    return {"mla_w_in": _unrows(big["w_in"]), "mla_w_uq": _unrows(_uq_from_kernel_layout(big["wt_uq"], axis=0)),
            "mla_w_ukv": _uncols(big["w_ukv"]), "mla_w_out": _unrows(big["w_out"])}


def _small_weights(tiny, rep):
    prm = jnp.zeros((2, 8, LANES), F32)
    prm = prm.at[:, 0, :GDN_HEADS].set(rep["gdn_a_log"]).at[:, 1, :GDN_HEADS].set(rep["gdn_dt_bias"])
    out = {
        "gdn_conv_wt": jnp.transpose(_gather_rows(tiny["gdn_conv_w"]), (0, 2, 1)),
        "mla_q_norm_g": jnp.transpose(tiny["mla_q_norm_g"], (1, 0, 2)).reshape(2, MLA_Q_RANK),
        "mla_kv_norm_g": jnp.transpose(tiny["mla_kv_norm_g"], (1, 0, 2)).reshape(2, MLA_KV_RANK),
        "gdn_gate_prm": prm,
    }
    for k in ("norm_mix_g", "norm_ffn_g", "gdn_norm_g", "final_norm_g"):
        out[k] = rep[k]
    return out


def _rope_tables(positions):
    inv_freq = ROPE_THETA ** (-jnp.arange(0, MLA_ROPE, 2, dtype=F32) / MLA_ROPE)
    ang = positions.astype(F32)[:, None] * inv_freq
    cos, sin = jnp.cos(ang), jnp.sin(ang)
    reps = LANES // MLA_ROPE
    return jnp.tile(jnp.concatenate([cos, cos], axis=1), (1, reps)), jnp.tile(
        jnp.concatenate([-sin, sin], axis=1), (1, reps))


def kernel(x, c, positions, ada_w, ada_b, norm_mix_g, norm_ffn_g, gdn_w_in, gdn_conv_w, gdn_a_log, gdn_dt_bias, gdn_norm_g, gdn_w_out, mla_w_in, mla_q_norm_g, mla_kv_norm_g, mla_w_uq, mla_w_ukv, mla_w_out, ffn_w_gate, ffn_w_up, ffn_w_down, final_norm_g, loss_target, m_ada_w, m_ada_b, m_norm_mix_g, m_norm_ffn_g, m_gdn_w_in, m_gdn_conv_w, m_gdn_a_log, m_gdn_dt_bias, m_gdn_norm_g, m_gdn_w_out, m_mla_w_in, m_mla_q_norm_g, m_mla_kv_norm_g, m_mla_w_uq, m_mla_w_ukv, m_mla_w_out, m_ffn_w_gate, m_ffn_w_up, m_ffn_w_down, m_final_norm_g, v_ada_w, v_ada_b, v_norm_mix_g, v_norm_ffn_g, v_gdn_w_in, v_gdn_conv_w, v_gdn_a_log, v_gdn_dt_bias, v_gdn_norm_g, v_gdn_w_out, v_mla_w_in, v_mla_q_norm_g, v_mla_kv_norm_g, v_mla_w_uq, v_mla_w_ukv, v_mla_w_out, v_ffn_w_gate, v_ffn_w_up, v_ffn_w_down, v_final_norm_g):
    W = dict(ada_w=ada_w, ada_b=ada_b, norm_mix_g=norm_mix_g, norm_ffn_g=norm_ffn_g, gdn_w_in=gdn_w_in,
             gdn_conv_w=gdn_conv_w, gdn_a_log=gdn_a_log, gdn_dt_bias=gdn_dt_bias, gdn_norm_g=gdn_norm_g,
             gdn_w_out=gdn_w_out, mla_w_in=mla_w_in, mla_q_norm_g=mla_q_norm_g, mla_kv_norm_g=mla_kv_norm_g,
             mla_w_uq=mla_w_uq, mla_w_ukv=mla_w_ukv, mla_w_out=mla_w_out, ffn_w_gate=ffn_w_gate,
             ffn_w_up=ffn_w_up, ffn_w_down=ffn_w_down, final_norm_g=final_norm_g)
    M = dict(ada_w=m_ada_w, ada_b=m_ada_b, norm_mix_g=m_norm_mix_g, norm_ffn_g=m_norm_ffn_g, gdn_w_in=m_gdn_w_in,
             gdn_conv_w=m_gdn_conv_w, gdn_a_log=m_gdn_a_log, gdn_dt_bias=m_gdn_dt_bias, gdn_norm_g=m_gdn_norm_g,
             gdn_w_out=m_gdn_w_out, mla_w_in=m_mla_w_in, mla_q_norm_g=m_mla_q_norm_g,
             mla_kv_norm_g=m_mla_kv_norm_g, mla_w_uq=m_mla_w_uq, mla_w_ukv=m_mla_w_ukv, mla_w_out=m_mla_w_out,
             ffn_w_gate=m_ffn_w_gate, ffn_w_up=m_ffn_w_up, ffn_w_down=m_ffn_w_down, final_norm_g=m_final_norm_g)
    V = dict(ada_w=v_ada_w, ada_b=v_ada_b, norm_mix_g=v_norm_mix_g, norm_ffn_g=v_norm_ffn_g, gdn_w_in=v_gdn_w_in,
             gdn_conv_w=v_gdn_conv_w, gdn_a_log=v_gdn_a_log, gdn_dt_bias=v_gdn_dt_bias, gdn_norm_g=v_gdn_norm_g,
             gdn_w_out=v_gdn_w_out, mla_w_in=v_mla_w_in, mla_q_norm_g=v_mla_q_norm_g,
             mla_kv_norm_g=v_mla_kv_norm_g, mla_w_uq=v_mla_w_uq, mla_w_ukv=v_mla_w_ukv, mla_w_out=v_mla_w_out,
             ffn_w_gate=v_ffn_w_gate, ffn_w_up=v_ffn_w_up, ffn_w_down=v_ffn_w_down, final_norm_g=v_final_norm_g)
    me = 4 * lax.axis_index("x") + 2 * lax.axis_index("y") + lax.axis_index("c")
    t = x.shape[1]
    wc = ada_w.shape[-1]

    groups = [(layer, kind) for layer in range(DEPTH) for kind in ("mix", "ffn")]

    def group_srcs(i):
        layer, kind = groups[i]
        return [_view(k, W[k])[_layer_index(k, layer)].astype(BF16) for k in _group_names(layer, kind)]

    tiny_shapes = [c.shape, gdn_conv_w.shape, mla_q_norm_g.shape, mla_kv_norm_g.shape]
    first = _gather_two_level([_pack([c, gdn_conv_w, mla_q_norm_g, mla_kv_norm_g])] + group_srcs(0),
                              name="gather_first")
    tiny_g = first[0]
    c_g, conv_g, qn_g, kvn_g = _unpack(tiny_g, tiny_shapes, lead=(N_DEV,))
    c_all = c_g.reshape(N_DEV, D_MODEL)
    rep = _small_weights({"gdn_conv_w": conv_g, "mla_q_norm_g": qn_g, "mla_kv_norm_g": kvn_g}, W)

    def start_group(i, dep):
        layer, kind = groups[i]
        return _exchange_start(group_srcs(i), scatter=False, name=f"gather_start_{kind}_l{layer}", dep=dep)


    b_cols = lax.dynamic_slice_in_dim(ada_b, me * wc, wc, axis=1).reshape(DEPTH, 1, wc)
    mod_part = _ada_mod(c_all, ada_w, b_cols, name="ada_mod")
    (mod_g,) = _exchange([mod_part], scatter=False, name="gather_mod")
    mod_mine = lax.dynamic_index_in_dim(mod_g, me, axis=2, keepdims=False)
    mod = jnp.transpose(mod_mine, (1, 0, 2)).reshape(DEPTH, N_MOD, D_MODEL)
    gather = {1: start_group(1, mod_g)}
    for i in range(2, AHEAD + 1):
        gather[i] = start_group(i, gather[i - 1][4])

    def get_weights(layer, kind, after):
        i = groups.index((layer, kind))
        names = _group_names(layer, kind)
        if i == 0:
            return _group_weights(layer, kind, dict(zip(names, first[1:])), gather[AHEAD][4])
        srcs, lands = _exchange_wait(gather[i], after, scatter=False, name=f"gather_wait_{kind}_l{layer}")
        token = jnp.zeros((8, LANES), F32)
        if i + AHEAD < len(groups):
            gather[i + AHEAD] = start_group(i + AHEAD, lands[0])
            token = gather[i + AHEAD][4]
        got = {k: lax.dynamic_update_index_in_dim(z, s, me, 0) for k, s, z in zip(names, srcs, lands)}
        return _group_weights(layer, kind, got, token)

    scatter = []

    def put_grads(layer, kind, big):
        slots = _layer_grad_slots(kind, big)
        started = _exchange_start(list(slots.values()), scatter=True, name=f"scatter_start_{kind}_l{layer}")
        scatter.append((layer, kind, list(slots.keys()), started))
        return started[4]

    cos_t, sin_t = _rope_tables(positions[0])
    loss, dx, dmod, g = _local_step(x[0], loss_target[0], mod, cos_t, sin_t, rep, get_weights, put_grads)

    parts = {k: [None] * W[k].shape[0] for k in BIG}
    res = {}

    def wait_group(entry, after):
        layer, kind, names, started = entry
        srcs, lands = _exchange_wait(started, after, scatter=True, name=f"scatter_wait_{kind}_l{layer}")
        for k, s, z in zip(names, srcs, lands):
            own = lax.dynamic_index_in_dim(s, me, 0, keepdims=False)
            parts[k][_layer_index(k, layer)] = lax.dynamic_update_index_in_dim(z, own, me, 0)

    for entry in scatter[:-1]:
        wait_group(entry, dx)
    early = [k for k in BIG if k not in scatter[-1][2]]
    def update(k):
        outs = _adamw(parts[k], _view(k, W[k]), _view(k, M[k]), _view(k, V[k]), name=f"adamw_{k}")
        return tuple(_view(k, o) for o in outs)

    for k in early:
        res[k] = update(k)
    loss, dmod, done = lax.optimization_barrier((loss, dmod, [res[k] for k in early]))
    for k, r in zip(early, done):
        res[k] = r

    small_local = [dmod.reshape(DEPTH, N_MOD * D_MODEL), g["norm_mix_g"], g["norm_ffn_g"],
                   jnp.transpose(g["gdn_conv_wt"], (0, 2, 1)), g["gdn_a_log"], g["gdn_dt_bias"], g["gdn_norm_g"],
                   g["mla_q_norm_g"], g["mla_kv_norm_g"], g["final_norm_g"], loss.reshape(1)]
    small_shapes = [a.shape for a in small_local]
    (small_g,) = _exchange([_pack(small_local)], scatter=False, name="gather_small_grads")
    small_sum = _unpack(_sum_parts(small_g, name="sum_small_grads"), small_shapes)
    loss = small_sum[-1][0]
    dmod_all = _unpack(small_g, small_shapes[:1], lead=(N_DEV,))[0]
    sg = dict(zip(SMALL, small_sum))
    wait_group(scatter[-1], small_g)
    sg["gdn_conv_w"] = lax.dynamic_slice_in_dim(sg["gdn_conv_w"], me * gdn_conv_w.shape[1], gdn_conv_w.shape[1], 1)
    sg["mla_q_norm_g"] = lax.dynamic_slice_in_dim(sg["mla_q_norm_g"], me * mla_q_norm_g.shape[1],
                                                  mla_q_norm_g.shape[1], 1)
    sg["mla_kv_norm_g"] = lax.dynamic_slice_in_dim(sg["mla_kv_norm_g"], me * mla_kv_norm_g.shape[1],
                                                   mla_kv_norm_g.shape[1], 1)

    dmod_cols = jnp.transpose(lax.dynamic_slice_in_dim(dmod_all, me * wc, wc, axis=2), (1, 0, 2))
    res["ada_w"] = _ada_grad_adamw(c_all, dmod_cols, ada_w, m_ada_w, v_ada_w, name="ada_w_grad_adamw")
    for k in BIG:
        if k not in early:
            res[k] = update(k)
    shapes = [W[k].shape for k in SMALL]
    packed = [_pack([d[k] for k in SMALL]) for d in (sg, W, M, V)]
    outs = _adamw([packed[0][None]], packed[1][None], packed[2][None], packed[3][None], name="adamw_small")
    unpacked = [_unpack(o[0], shapes) for o in outs]
    for i, k in enumerate(SMALL):
        res[k] = tuple(u[i] for u in unpacked)

    return (loss, dx[None], *[res[k][0] for k in WEIGHTS], *[res[k][1] for k in WEIGHTS],
            *[res[k][2] for k in WEIGHTS], *[res[k][3] for k in WEIGHTS])
```

```python
import functools
import math

import jax
import jax.numpy as jnp
from jax import lax
from jax.experimental import pallas as pl
from jax.experimental.pallas import tpu as pltpu

F32 = jnp.float32
BF16 = jnp.bfloat16
MXU_DTYPE = jnp.bfloat16

N_DEV = 8
D_MODEL = 1024
DEPTH = 4
GDN_HEADS = 8
GDN_HEAD_DIM = 128
GDN_KEY_DIM = GDN_HEADS * GDN_HEAD_DIM
GDN_CHUNK = 64
GDN_HEAD_BATCH = 8
GDN_CONV = 4
GDN_PREP_HEADS = 2
GDN_MAIN = 4 * GDN_KEY_DIM
MLA_HEADS = 8
MLA_NOPE = 128
MLA_ROPE = 64
MLA_V = 128
MLA_Q_RANK = 384
MLA_KV_RANK = 256
MLA_IN = MLA_Q_RANK + MLA_KV_RANK + MLA_ROPE
MLA_QK = MLA_NOPE + MLA_ROPE
ROPE_THETA = 10000.0
D_FF = 2816
N_MOD = 6
EPS = 1e-6
LANES = 128
VMEM_LIMIT = 48 * 1024 * 1024

ADAM_LR = 0.001
ADAM_B1 = 0.9
ADAM_B2 = 0.999
ADAM_EPS = 1e-08
ADAM_WD = 0.01
ADAM_STEP = 10
ADAM_BC1 = 1.0 - ADAM_B1 ** ADAM_STEP
ADAM_BC2 = 1.0 - ADAM_B2 ** ADAM_STEP

NN = (((1,), (0,)), ((), ()))
NT = (((1,), (1,)), ((), ()))
TN = (((0,), (0,)), ((), ()))
NEG = -1e30


def _dotb(a, b, dims):
    return lax.dot_general(a.astype(MXU_DTYPE), b.astype(MXU_DTYPE), dims, preferred_element_type=F32)


def _split(a):
    hi = a.astype(BF16)
    return hi, (a - hi.astype(F32)).astype(BF16)


def _dotf(a, b, dims):
    ah, al = _split(a)
    bh, bl = _split(b)
    dot = lambda u, v: lax.dot_general(u, v, dims, preferred_element_type=F32)
    return dot(ah, bh) + (dot(ah, bl) + dot(al, bh))


def _params(*sem):
    return pltpu.CompilerParams(dimension_semantics=sem, vmem_limit_bytes=VMEM_LIMIT)


def _pick(n, pref, mult=LANES):
    best = None
    t = mult
    while t <= min(n, pref):
        if n % t == 0:
            best = t
        t += mult
    return best if best is not None else n


def _sigmoid(z):
    return 1.0 / (1.0 + jnp.exp(-z))


def _exchange(arrays, *, scatter, name):
    n = len(arrays)
    out_shape = tuple(
        jax.ShapeDtypeStruct(a.shape if scatter else (N_DEV,) + a.shape, a.dtype) for a in arrays)

    def body(*refs):
        ins, outs = refs[:n], refs[n:2 * n]
        send_sems, recv_sems, local_sems = refs[2 * n:]
        x, y, c = lax.axis_index("x"), lax.axis_index("y"), lax.axis_index("c")
        me = 4 * x + 2 * y + c
        copies = []
        for k in range(n):
            src_own = ins[k].at[me] if scatter else ins[k]
            own = pltpu.make_async_copy(src_own, outs[k].at[me], local_sems.at[k])
            own.start()
            copies.append(own)
        sends = []
        for p in range(1, N_DEV):
            px, py, pc = x ^ ((p >> 2) & 1), y ^ ((p >> 1) & 1), c ^ (p & 1)
            peer = 4 * px + 2 * py + pc
            for k in range(n):
                cp = pltpu.make_async_remote_copy(
                    src_ref=ins[k].at[peer] if scatter else ins[k],
                    dst_ref=outs[k].at[me],
                    send_sem=send_sems.at[k, p - 1],
                    recv_sem=recv_sems.at[k, p - 1],
                    device_id=(px, py, pc),
                    device_id_type=pl.DeviceIdType.MESH,
                )
                cp.start()
                sends.append((cp, k, peer, p))
        for cp, k, peer, p in sends:
            pltpu.make_async_remote_copy(
                src_ref=ins[k].at[peer] if scatter else ins[k],
                dst_ref=outs[k].at[peer],
                send_sem=send_sems.at[k, p - 1],
                recv_sem=recv_sems.at[k, p - 1],
                device_id=(x, y, c),
                device_id_type=pl.DeviceIdType.MESH,
            ).wait_recv()
        for cp, _, _, _ in sends:
            cp.wait_send()
        for own in copies:
            own.wait()

    any_spec = pl.BlockSpec(memory_space=pl.ANY)
    outs = pl.pallas_call(
        body,
        name=name,
        out_shape=out_shape,
        in_specs=[any_spec] * n,
        out_specs=tuple([any_spec] * n),
        scratch_shapes=[
            pltpu.SemaphoreType.DMA((n, N_DEV - 1)),
            pltpu.SemaphoreType.DMA((n, N_DEV - 1)),
            pltpu.SemaphoreType.DMA((n,)),
        ],
        compiler_params=pltpu.CompilerParams(has_side_effects=True),
    )(*arrays)
    return list(outs)


def _gather_two_level(arrays, *, name):
    n = len(arrays)
    out_shape = tuple(jax.ShapeDtypeStruct((N_DEV,) + a.shape, a.dtype) for a in arrays)

    def body(*refs):
        ins, outs = refs[:n], refs[n:2 * n]
        send_sems, recv_sems, local_sems = refs[2 * n:]
        x, y, c = lax.axis_index("x"), lax.axis_index("y"), lax.axis_index("c")
        me = 4 * x + 2 * y + c
        sibling = (x, y, 1 - c)
        chips = [(1 - x, y), (x, 1 - y), (1 - x, 1 - y)]

        def slot(px, py, pc):
            return 4 * px + 2 * py + pc

        def copy(k, q, block, to, src=None):
            return pltpu.make_async_remote_copy(
                src_ref=outs[k].at[slot(*block)] if src is None else src,
                dst_ref=outs[k].at[slot(*block)],
                send_sem=send_sems.at[k, q], recv_sem=recv_sems.at[k, q],
                device_id=to, device_id_type=pl.DeviceIdType.MESH)

        own = [pltpu.make_async_copy(ins[k], outs[k].at[me], local_sems.at[k]) for k in range(n)]
        for cp in own:
            cp.start()
        first = []
        for k in range(n):
            first.append(copy(k, 0, (x, y, c), sibling, src=ins[k]))
            first += [copy(k, 1 + j, (x, y, c), (*chip, c), src=ins[k]) for j, chip in enumerate(chips)]
        for cp in first:
            cp.start()
        passed = []
        for j, chip in enumerate(chips):
            for k in range(n):
                copy(k, 1 + j, (*chip, c), (x, y, c)).wait_recv()
                fwd = copy(k, 4 + j, (*chip, c), sibling)
                fwd.start()
                passed.append(fwd)
        for k in range(n):
            copy(k, 0, sibling, (x, y, c)).wait_recv()
            for j, chip in enumerate(chips):
                copy(k, 4 + j, (*chip, 1 - c), (x, y, c)).wait_recv()
        for cp in first + passed:
            cp.wait_send()
        for cp in own:
            cp.wait()

    any_spec = pl.BlockSpec(memory_space=pl.ANY)
    outs = pl.pallas_call(
        body, name=name, out_shape=out_shape, in_specs=[any_spec] * n, out_specs=tuple([any_spec] * n),
        scratch_shapes=[pltpu.SemaphoreType.DMA((n, N_DEV - 1)), pltpu.SemaphoreType.DMA((n, N_DEV - 1)),
                        pltpu.SemaphoreType.DMA((n,))],
        compiler_params=pltpu.CompilerParams(has_side_effects=True),
    )(*arrays)
    return list(outs)


def _peer(x, y, c, p):
    return x ^ ((p >> 2) & 1), y ^ ((p >> 1) & 1), c ^ (p & 1)


def _exchange_start(arrays, *, scatter, name, dep=None):
    n = len(arrays)
    deps = [] if dep is None else [dep]
    lands = [lax.empty(a.shape if scatter else (N_DEV,) + a.shape, a.dtype) for a in arrays]

    def body(*refs):
        ins, zones = refs[:n], refs[n:2 * n]
        send_sems, recv_sems = refs[2 * n + len(deps)], refs[2 * n + len(deps) + 1]
        token = refs[-1]
        x, y, c = lax.axis_index("x"), lax.axis_index("y"), lax.axis_index("c")
        me = 4 * x + 2 * y + c
        for p in range(1, N_DEV):
            px, py, pc = _peer(x, y, c, p)
            for k in range(n):
                pltpu.make_async_remote_copy(
                    src_ref=ins[k].at[4 * px + 2 * py + pc] if scatter else ins[k],
                    dst_ref=zones[k].at[me],
                    send_sem=send_sems.at[k * (N_DEV - 1) + p - 1],
                    recv_sem=recv_sems.at[k * (N_DEV - 1) + p - 1],
                    device_id=(px, py, pc),
                    device_id_type=pl.DeviceIdType.MESH,
                ).start()
        token[...] = jnp.zeros_like(token)

    hbm = pl.BlockSpec(memory_space=pltpu.HBM)
    sem = pl.BlockSpec(memory_space=pltpu.SEMAPHORE)
    outs = pl.pallas_call(
        body,
        name=name,
        out_shape=(pltpu.SemaphoreType.DMA((n * (N_DEV - 1),)), pltpu.SemaphoreType.DMA((n * (N_DEV - 1),)),
                   *[pltpu.HBM(a.shape, a.dtype) for a in arrays], *[pltpu.HBM(z.shape, z.dtype) for z in lands],
                   jax.ShapeDtypeStruct((8, LANES), F32)),
        in_specs=[hbm] * (2 * n) + [pl.BlockSpec(memory_space=pl.ANY)] * len(deps),
        out_specs=(sem, sem, *[hbm] * (2 * n), pl.BlockSpec(memory_space=pltpu.VMEM)),
        input_output_aliases={k: 2 + k for k in range(2 * n)},
        compiler_params=pltpu.CompilerParams(has_side_effects=pltpu.SideEffectType.DATAFLOW_SIDE_EFFECTING),
    )(*[pltpu.with_memory_space_constraint(a, pltpu.HBM) for a in arrays],
      *[pltpu.with_memory_space_constraint(z, pltpu.HBM) for z in lands], *deps)
    return outs[0], outs[1], list(outs[2:2 + n]), list(outs[2 + n:2 + 2 * n]), outs[-1]


def _exchange_wait(started, after, *, scatter, name):
    send_sems, recv_sems, srcs, lands, _ = started
    n = len(srcs)

    def body(*refs):
        ins, zones = refs[:n], refs[n:2 * n]
        s_sems, r_sems = refs[2 * n], refs[2 * n + 1]
        x, y, c = lax.axis_index("x"), lax.axis_index("y"), lax.axis_index("c")
        for p in range(1, N_DEV):
            px, py, pc = _peer(x, y, c, p)
            peer = 4 * px + 2 * py + pc
            for k in range(n):
                cp = pltpu.make_async_remote_copy(
                    src_ref=ins[k].at[peer] if scatter else ins[k],
                    dst_ref=zones[k].at[peer],
                    send_sem=s_sems.at[k * (N_DEV - 1) + p - 1],
                    recv_sem=r_sems.at[k * (N_DEV - 1) + p - 1],
                    device_id=(px, py, pc),
                    device_id_type=pl.DeviceIdType.MESH,
                )
                cp.wait_send()
                cp.wait_recv()

    hbm = pl.BlockSpec(memory_space=pltpu.HBM)
    sem = pl.BlockSpec(memory_space=pltpu.SEMAPHORE)
    outs = pl.pallas_call(
        body,
        name=name,
        out_shape=tuple(pltpu.HBM(a.shape, a.dtype) for a in srcs + lands),
        in_specs=[hbm] * (2 * n) + [sem, sem, pl.BlockSpec(memory_space=pl.ANY)],
        out_specs=tuple([hbm] * (2 * n)),
        input_output_aliases={k: k for k in range(2 * n)},
        compiler_params=pltpu.CompilerParams(has_side_effects=pltpu.SideEffectType.DATAFLOW_SIDE_EFFECTING),
    )(*srcs, *lands, send_sems, recv_sems, after)
    return list(outs[:n]), list(outs[n:])


def _mm(a, b, *, mode, out_dtype, name, add=None, tm=512, tn=512, b_rows=None, dep=None):
    rows_b = b.shape[0] if b_rows is None else b_rows
    if mode == "nn":
        (m, kd), nd = a.shape, b.shape[1]
        assert kd == rows_b
    elif mode == "nt":
        (m, kd), nd = a.shape, rows_b
    else:
        (kd, m), nd = a.shape, b.shape[1]
    tm = _pick(m, tm, LANES if mode == "tn" else 16)
    tn = _pick(nd, tn)
    dims = {"nn": NN, "nt": NT, "tn": TN}[mode]
    ni, nj = m // tm, nd // tn
    a_bytes, b_bytes = a.size * a.dtype.itemsize, b.size * b.dtype.itemsize
    i_outer = a_bytes + ni * b_bytes <= b_bytes + nj * a_bytes
    ij = (lambda g0, g1: (g0, g1)) if i_outer else (lambda g0, g1: (g1, g0))
    a_spec = (pl.BlockSpec((kd, tm), lambda g0, g1: (0, ij(g0, g1)[0])) if mode == "tn"
              else pl.BlockSpec((tm, kd), lambda g0, g1: (ij(g0, g1)[0], 0)))
    b_spec = (pl.BlockSpec((tn, kd), lambda g0, g1: (ij(g0, g1)[1], 0)) if mode == "nt"
              else pl.BlockSpec((kd, tn), lambda g0, g1: (0, ij(g0, g1)[1])))
    o_spec = pl.BlockSpec((tm, tn), lambda g0, g1: ij(g0, g1))
    has_add = add is not None

    def body(*refs):
        a_ref, b_ref = refs[0], refs[1]
        o_ref = refs[-1]
        acc = _dotb(a_ref[...], b_ref[...], dims)
        if has_add:
            acc = acc + refs[2][...].astype(F32)
        o_ref[...] = acc.astype(o_ref.dtype)

    ins = [a, b] + ([add] if has_add else []) + ([] if dep is None else [dep])
    specs = ([a_spec, b_spec] + ([o_spec] if has_add else [])
             + ([] if dep is None else [pl.BlockSpec((8, LANES), lambda g0, g1: (0, 0))]))
    return pl.pallas_call(
        body, name=name, grid=(ni, nj) if i_outer else (nj, ni), in_specs=specs, out_specs=o_spec,
        out_shape=jax.ShapeDtypeStruct((m, nd), out_dtype),
        compiler_params=_params("parallel", "parallel"),
    )(*ins)


def _mm_resid(a, b, x, gate, *, name, tm=256, tn=1024):
    m, kd = a.shape
    nd = b.shape[1]
    tm = _pick(m, tm, 16)
    tn = _pick(nd, tn)
    o_spec = pl.BlockSpec((tm, tn), lambda i, j: (i, j))

    def body(a_ref, b_ref, x_ref, g_ref, xo_ref, y_ref):
        y = _dotb(a_ref[...], b_ref[...], NN)
        y_ref[...] = y
        xo_ref[...] = x_ref[...] + g_ref[...] * y

    return pl.pallas_call(
        body, name=name, grid=(m // tm, nd // tn),
        in_specs=[pl.BlockSpec((tm, kd), lambda i, j: (i, 0)), pl.BlockSpec((kd, tn), lambda i, j: (0, j)),
                  o_spec, pl.BlockSpec((1, tn), lambda i, j: (0, j))],
        out_specs=(o_spec, o_spec),
        out_shape=(jax.ShapeDtypeStruct((m, nd), F32), jax.ShapeDtypeStruct((m, nd), F32)),
        compiler_params=_params("parallel", "parallel"),
    )(a, b, x, gate)


ROWS = 256


def _row_spec(width, rows=ROWS):
    return pl.BlockSpec((rows, width), lambda i: (i, 0))


def _const_spec(shape):
    return pl.BlockSpec(shape, lambda i: tuple(0 for _ in shape))


def _adaln_fwd(x, g, scale, shift, *, name):
    t, d = x.shape

    def body(x_ref, g_ref, sc_ref, sh_ref, h_ref):
        xv = x_ref[...]
        r = lax.rsqrt(jnp.mean(xv * xv, axis=-1, keepdims=True) + EPS)
        h_ref[...] = (xv * r * g_ref[...] * (1.0 + sc_ref[...]) + sh_ref[...]).astype(h_ref.dtype)

    return pl.pallas_call(
        body, name=name, grid=(t // ROWS,),
        in_specs=[_row_spec(d), _const_spec((1, d)), _const_spec((1, d)), _const_spec((1, d))],
        out_specs=_row_spec(d), out_shape=jax.ShapeDtypeStruct((t, d), BF16),
        compiler_params=_params("parallel"),
    )(x, g, scale, shift)


def _adaln_bwd(x, g, scale, shift, dh, dres, dep, *, name):
    t, d = x.shape

    def body(x_ref, g_ref, sc_ref, sh_ref, dh_ref, dr_ref, dep_ref, dx_ref, st_ref):
        @pl.when(pl.program_id(0) == 0)
        def _():
            st_ref[...] = jnp.zeros_like(st_ref)

        xv = x_ref[...]
        dhv = dh_ref[...].astype(F32)
        gv = g_ref[...]
        r = lax.rsqrt(jnp.mean(xv * xv, axis=-1, keepdims=True) + EPS)
        xh = xv * r
        nv = xh * gv
        dn = dhv * (1.0 + sc_ref[...])
        dxh = dn * gv
        dx_ref[...] = dr_ref[...] + r * (dxh - xh * jnp.mean(dxh * xh, axis=-1, keepdims=True))
        st_ref[0:1, :] += jnp.sum(dn * xh, axis=0, keepdims=True)
        st_ref[1:2, :] += jnp.sum(dhv * nv, axis=0, keepdims=True)
        st_ref[2:3, :] += jnp.sum(dhv, axis=0, keepdims=True)

    return pl.pallas_call(
        body, name=name, grid=(t // ROWS,),
        in_specs=[_row_spec(d), _const_spec((1, d)), _const_spec((1, d)), _const_spec((1, d)),
                  _row_spec(d), _row_spec(d), _const_spec((8, LANES))],
        out_specs=(_row_spec(d), _const_spec((8, d))),
        out_shape=(jax.ShapeDtypeStruct((t, d), F32), jax.ShapeDtypeStruct((8, d), F32)),
        compiler_params=_params("arbitrary"),
    )(x, g, scale, shift, dh, dres, dep)


def _gate_bwd(dxo, y, gate, dep, *, name):
    t, d = dxo.shape

    def body(dx_ref, y_ref, g_ref, dep_ref, dy_ref, st_ref):
        @pl.when(pl.program_id(0) == 0)
        def _():
            st_ref[...] = jnp.zeros_like(st_ref)

        dxv = dx_ref[...]
        dy_ref[...] = (dxv * g_ref[...]).astype(dy_ref.dtype)
        st_ref[0:1, :] += jnp.sum(dxv * y_ref[...], axis=0, keepdims=True)

    return pl.pallas_call(
        body, name=name, grid=(t // ROWS,),
        in_specs=[_row_spec(d), _row_spec(d), _const_spec((1, d)), _const_spec((8, LANES))],
        out_specs=(_row_spec(d), _const_spec((8, d))),
        out_shape=(jax.ShapeDtypeStruct((t, d), BF16), jax.ShapeDtypeStruct((8, d), F32)),
        compiler_params=_params("arbitrary"),
    )(dxo, y, gate, dep)


def _loss_head(x, g, target, *, name):
    t, d = x.shape

    def body(x_ref, g_ref, t_ref, dx_ref, st_ref, ls_ref):
        @pl.when(pl.program_id(0) == 0)
        def _():
            st_ref[...] = jnp.zeros_like(st_ref)
            ls_ref[...] = jnp.zeros_like(ls_ref)

        xv = x_ref[...]
        gv = g_ref[...]
        r = lax.rsqrt(jnp.mean(xv * xv, axis=-1, keepdims=True) + EPS)
        xh = xv * r
        err = xh * gv - t_ref[...]
        ls_ref[...] += 0.5 * jnp.sum(jnp.mean(err * err, axis=-1, keepdims=True))
        dy = err * (1.0 / d)
        dxh = dy * gv
        dx_ref[...] = r * (dxh - xh * jnp.mean(dxh * xh, axis=-1, keepdims=True))
        st_ref[0:1, :] += jnp.sum(dy * xh, axis=0, keepdims=True)

    return pl.pallas_call(
        body, name=name, grid=(t // ROWS,),
        in_specs=[_row_spec(d), _const_spec((1, d)), _row_spec(d)],
        out_specs=(_row_spec(d), _const_spec((8, d)), _const_spec((8, LANES))),
        out_shape=(jax.ShapeDtypeStruct((t, d), F32), jax.ShapeDtypeStruct((8, d), F32),
                   jax.ShapeDtypeStruct((8, LANES), F32)),
        compiler_params=_params("arbitrary"),
    )(x, g, target)


FFN_BLOCK = D_FF // 2
FFN_ROWS = 512


def _ffn_chunks(width):
    edges = [min(width, 3 * LANES * i) for i in range(width // (3 * LANES) + 2)]
    return [slice(lo, hi) for lo, hi in zip(edges[:-1], edges[1:]) if hi > lo]


def _ffn_gu_fwd(h, wg, wu, dep, *, name):
    t, d = h.shape
    tn = FFN_BLOCK

    chunks = _ffn_chunks(tn)
    rows = _pick(t, FFN_ROWS, 16)

    def body(h_ref, wg_ref, wu_ref, dep_ref, s_ref, a_ref, b_ref):
        hv = h_ref[...]
        ab = [(_dotb(hv, wg_ref[sl, :], NT), _dotb(hv, wu_ref[sl, :], NT)) for sl in chunks]
        for sl, (a, b) in zip(chunks, ab):
            s_ref[:, sl] = (a * _sigmoid(a) * b).astype(s_ref.dtype)
            a_ref[:, sl] = a.astype(a_ref.dtype)
            b_ref[:, sl] = b.astype(b_ref.dtype)

    w_spec = pl.BlockSpec((tn, d), lambda j, i: (j, 0))
    o_spec = pl.BlockSpec((rows, tn), lambda j, i: (i, j))
    return pl.pallas_call(
        body, name=name, grid=(D_FF // tn, t // rows),
        in_specs=[pl.BlockSpec((rows, d), lambda j, i: (i, 0)), w_spec, w_spec,
                  pl.BlockSpec((8, LANES), lambda j, i: (0, 0))],
        out_specs=(o_spec, o_spec, o_spec),
        out_shape=(jax.ShapeDtypeStruct((t, D_FF), BF16),) * 3,
        compiler_params=_params("parallel", "parallel"),
    )(h, wg, wu, dep)


def _ffn_down_dx(dy, w_down, a, b, *, name):
    t, d = dy.shape
    tn = FFN_BLOCK

    chunks = _ffn_chunks(tn)
    rows = _pick(t, FFN_ROWS, 16)

    def body(dy_ref, w_ref, a_ref, b_ref, da_ref, db_ref):
        dyv = dy_ref[...]
        ds = [_dotb(dyv, w_ref[sl, :], NT) for sl in chunks]
        for sl, dsc in zip(chunks, ds):
            av = a_ref[:, sl].astype(F32)
            sg = _sigmoid(av)
            da_ref[:, sl] = (dsc * b_ref[:, sl].astype(F32) * sg * (1.0 + av * (1.0 - sg))).astype(da_ref.dtype)
            db_ref[:, sl] = (dsc * av * sg).astype(db_ref.dtype)

    o_spec = pl.BlockSpec((rows, tn), lambda j, i: (i, j))
    return pl.pallas_call(
        body, name=name, grid=(D_FF // tn, t // rows),
        in_specs=[pl.BlockSpec((rows, d), lambda j, i: (i, 0)), pl.BlockSpec((tn, d), lambda j, i: (j, 0)),
                  o_spec, o_spec],
        out_specs=(o_spec, o_spec),
        out_shape=(jax.ShapeDtypeStruct((t, D_FF), BF16),) * 2,
        compiler_params=_params("parallel", "parallel"),
    )(dy, w_down, a, b)


def _shift_rows(v, s, rows):
    if s == 0:
        return v
    return jnp.where(rows >= s, pltpu.roll(v, s, 0), 0.0)


def _unshift_rows(v, s, rows, t):
    if s == 0:
        return v
    return jnp.where(rows < t - s, pltpu.roll(v, t - s, 0), 0.0)


def _conv_silu(x, w, rows):
    z = w[GDN_CONV - 1:GDN_CONV, :] * x
    for j in range(GDN_CONV - 1):
        z = z + w[j:j + 1, :] * _shift_rows(x, GDN_CONV - 1 - j, rows)
    sg = _sigmoid(z)
    return z, sg, z * sg


def _gdn_prep_fwd(proj, conv_wt, *, name):
    t = proj.shape[0]
    nh = GDN_HEADS

    hp = GDN_PREP_HEADS
    wd = hp * LANES

    def body(x_ref, w_ref, y_ref):
        j = pl.program_id(0) * hp
        rows = lax.broadcasted_iota(jnp.int32, (t, LANES), 0)
        qscale = jnp.where(j < nh, GDN_HEAD_DIM ** -0.5, 1.0)
        for i in range(hp):
            sl = slice(i * LANES, (i + 1) * LANES)
            _, _, s = _conv_silu(x_ref[:, sl], w_ref[:, sl], rows)
            rs = lax.rsqrt(jnp.sum(s * s, axis=-1, keepdims=True) + EPS)
            y_ref[:, sl] = jnp.where(j < 2 * nh, s * rs * qscale, s)

    return pl.pallas_call(
        body, name=name, grid=(3 * nh // hp,),
        in_specs=[pl.BlockSpec((t, wd), lambda j: (0, j)), pl.BlockSpec((GDN_CONV, wd), lambda j: (0, j))],
        out_specs=pl.BlockSpec((t, wd), lambda j: (0, j)),
        out_shape=jax.ShapeDtypeStruct((t, 3 * GDN_KEY_DIM), F32),
        compiler_params=_params("parallel"),
    )(proj, conv_wt)


def _gdn_prep_bwd(proj, conv_wt, dy, *, name):
    t = proj.shape[0]
    nh = GDN_HEADS

    hp = GDN_PREP_HEADS
    wd = hp * LANES
    per_seg = nh // hp

    def body(x_ref, w_ref, dy_ref, dx_ref, dw_ref):
        j = pl.program_id(0) * hp
        rows = lax.broadcasted_iota(jnp.int32, (t, LANES), 0)
        qscale = jnp.where(j < nh, GDN_HEAD_DIM ** -0.5, 1.0)
        for i in range(hp):
            sl = slice(i * LANES, (i + 1) * LANES)
            x = x_ref[:, sl]
            w = w_ref[:, sl]
            z, sg, s = _conv_silu(x, w, rows)
            rs = lax.rsqrt(jnp.sum(s * s, axis=-1, keepdims=True) + EPS)
            dyv = dy_ref[:, sl]
            nv = s * rs
            de = dyv * qscale
            ds_qk = rs * (de - nv * jnp.sum(de * nv, axis=-1, keepdims=True))
            ds = jnp.where(j < 2 * nh, ds_qk, dyv)
            dz = ds * sg * (1.0 + z * (1.0 - sg))
            dx = w[GDN_CONV - 1:GDN_CONV, :] * dz
            dw_ref[GDN_CONV - 1:GDN_CONV, sl] = jnp.sum(dz * x, axis=0, keepdims=True)
            for k in range(GDN_CONV - 1):
                sh = GDN_CONV - 1 - k
                dx = dx + w[k:k + 1, :] * _unshift_rows(dz, sh, rows, t)
                dw_ref[k:k + 1, sl] = jnp.sum(dz * _shift_rows(x, sh, rows), axis=0, keepdims=True)
            dx_ref[:, sl] = dx.astype(dx_ref.dtype)

    return pl.pallas_call(
        body, name=name, grid=(3 * nh // hp,),
        in_specs=[pl.BlockSpec((t, wd), lambda j: (0, j)), pl.BlockSpec((GDN_CONV, wd), lambda j: (0, j)),
                  pl.BlockSpec((None, t, wd), lambda j: (j // per_seg, 0, j % per_seg))],
        out_specs=(pl.BlockSpec((t, wd), lambda j: (0, j)), pl.BlockSpec((GDN_CONV, wd), lambda j: (0, j))),
        out_shape=(jax.ShapeDtypeStruct((t, 3 * GDN_KEY_DIM), BF16),
                   jax.ShapeDtypeStruct((GDN_CONV, 3 * GDN_KEY_DIM), F32)),
        compiler_params=_params("parallel"),
    )(proj, conv_wt, dy)


def _softplus(z):
    return jnp.maximum(z, 0.0) + jnp.log(1.0 + jnp.exp(-jnp.abs(z)))


def _gdn_gate_fwd(ab, prm, *, name):
    t = ab.shape[0]

    def body(ab_ref, p_ref, o_ref):
        v = ab_ref[...]
        lane = lax.broadcasted_iota(jnp.int32, v.shape, 1)
        g = -jnp.exp(p_ref[0:1, :]) * _softplus(v + p_ref[1:2, :])
        o_ref[...] = jnp.where(lane < GDN_HEADS, g, jnp.where(lane < 2 * GDN_HEADS, _sigmoid(v), 0.0))

    return pl.pallas_call(
        body, name=name, grid=(t // ROWS,),
        in_specs=[_row_spec(LANES), _const_spec((8, LANES))], out_specs=_row_spec(LANES),
        out_shape=jax.ShapeDtypeStruct((t, LANES), F32), compiler_params=_params("parallel"),
    )(ab, prm)


def _gdn_gate_bwd(ab, prm, dgb, *, name):
    t = ab.shape[0]

    def body(ab_ref, p_ref, d_ref, o_ref, st_ref):
        @pl.when(pl.program_id(0) == 0)
        def _():
            st_ref[...] = jnp.zeros_like(st_ref)

        v = ab_ref[...]
        dv = d_ref[...]
        lane = lax.broadcasted_iota(jnp.int32, v.shape, 1)
        is_a = lane < GDN_HEADS
        is_b = jnp.logical_and(lane >= GDN_HEADS, lane < 2 * GDN_HEADS)
        a_exp = jnp.exp(p_ref[0:1, :])
        zz = v + p_ref[1:2, :]
        g = -a_exp * _softplus(zz)
        da = dv * (-a_exp) * _sigmoid(zz)
        beta = _sigmoid(v)
        db = dv * beta * (1.0 - beta)
        o_ref[...] = jnp.where(is_a, da, jnp.where(is_b, db, 0.0)).astype(o_ref.dtype)
        st_ref[0:1, :] += jnp.sum(jnp.where(is_a, dv * g, 0.0), axis=0, keepdims=True)
        st_ref[1:2, :] += jnp.sum(jnp.where(is_a, da, 0.0), axis=0, keepdims=True)

    return pl.pallas_call(
        body, name=name, grid=(t // ROWS,),
        in_specs=[_row_spec(LANES), _const_spec((8, LANES)), _row_spec(LANES)],
        out_specs=(_row_spec(LANES), _const_spec((8, LANES))),
        out_shape=(jax.ShapeDtypeStruct((t, LANES), BF16), jax.ShapeDtypeStruct((8, LANES), F32)),
        compiler_params=_params("arbitrary"),
    )(ab, prm, dgb)


def _gdn_local(qs, ks, vs, gbs, bbs, tinvs=None):
    nh = len(qs)
    cs = qs[0].shape[0]
    hs = range(nh)
    r = lax.broadcasted_iota(jnp.int32, (cs, cs), 0)
    c = lax.broadcasted_iota(jnp.int32, (cs, cs), 1)
    tril, strict, eye = r >= c, r > c, r == c
    ident = jnp.where(eye, 1.0, 0.0)
    g_colb = [gbs[h][:, :cs] for h in hs]
    g_row = [jnp.sum(jnp.where(eye, g_colb[h], 0.0), axis=0, keepdims=True) for h in hs]
    gc_col = [jnp.sum(jnp.where(tril, g_row[h], 0.0), axis=1, keepdims=True) for h in hs]
    gc_row = [jnp.sum(jnp.where(r <= c, g_colb[h], 0.0), axis=0, keepdims=True) for h in hs]
    decay = [jnp.exp(jnp.where(tril, gc_col[h] - gc_row[h], NEG)) for h in hs]
    gamma = [jnp.exp(gc_col[h]) for h in hs]
    gcl = [gc_col[h][cs - 1:cs, :] for h in hs]
    gl = [jnp.exp(gcl[h]) for h in hs]
    kdec = [jnp.exp(gcl[h] - gc_col[h]) for h in hs]
    kb = [ks[h] * bbs[h] for h in hs]
    kk = [_dotb(kb[h], ks[h], NT) for h in hs]
    qk = [_dotb(qs[h], ks[h], NT) for h in hs]
    lmat = [jnp.where(strict, kk[h] * decay[h], 0.0) for h in hs]
    pmat = [jnp.where(tril, qk[h] * decay[h], 0.0) for h in hs]
    if tinvs is None:
        xm = [-lmat[h] for h in hs]
        tinv = [ident + xm[h] for h in hs]
        for _ in range(int(math.log2(cs)) - 1):
            xm = [_dotf(xm[h], xm[h], NN) for h in hs]
            tinv = [tinv[h] + _dotf(tinv[h], xm[h], NN) for h in hs]
    else:
        tinv = tinvs
    vb = [vs[h] * bbs[h] for h in hs]
    kg = [kb[h] * gamma[h] for h in hs]
    u = [_dotf(tinv[h], vb[h], NN) for h in hs]
    w = [_dotf(tinv[h], kg[h], NN) for h in hs]
    return [dict(tril=tril, strict=strict, eye=eye, r=r, c=c, decay=decay[h], gamma=gamma[h], gl=gl[h], kdec=kdec[h],
                 kb=kb[h], lmat=lmat[h], tinv=tinv[h], vb=vb[h], kg=kg[h], u=u[h], w=w[h], pmat=pmat[h],
                 qd=qs[h] * gamma[h], kd=ks[h] * kdec[h]) for h in hs]


def _gdn_chunk_fwd(qkv, gbc, bbc, *, name):
    t = qkv.shape[0]
    nh, cs, hd = GDN_HEADS, GDN_CHUNK, GDN_HEAD_DIM
    nc = t // cs

    hb = GDN_HEAD_BATCH
    ng = nh // hb

    def body(q_ref, k_ref, v_ref, g_ref, b_ref, o_ref, st_ref, ti_ref, s_ref):
        @pl.when(pl.program_id(1) == 0)
        def _():
            s_ref[...] = jnp.zeros_like(s_ref)

        sls = [slice(i * hd, (i + 1) * hd) for i in range(hb)]
        hs = range(hb)
        s = [s_ref[i] for i in hs]
        lo = _gdn_local([q_ref[:, sl] for sl in sls], [k_ref[:, sl] for sl in sls], [v_ref[:, sl] for sl in sls],
                        [g_ref[i] for i in hs], [b_ref[i] for i in hs])
        ws = [_dotb(lo[i]["w"], s[i], NN) for i in hs]
        qs = [_dotb(lo[i]["qd"], s[i], NN) for i in hs]
        vn = [lo[i]["u"] - ws[i] for i in hs]
        pv = [_dotb(lo[i]["pmat"], vn[i], NN) for i in hs]
        kv = [_dotb(lo[i]["kd"], vn[i], TN) for i in hs]
        for i, sl in enumerate(sls):
            st_ref[i, 0] = s[i]
            ti_ref[i, 0] = lo[i]["tinv"]
            o_ref[:, sl] = qs[i] + pv[i]
            s_ref[i] = s[i] * lo[i]["gl"] + kv[i]

    gspec = pl.BlockSpec((hb, cs, LANES), lambda h, n: (h, n, 0))
    col = lambda off: pl.BlockSpec((cs, hb * hd), lambda h, n: (n, off + h))
    return pl.pallas_call(
        body, name=name, grid=(ng, nc),
        in_specs=[col(0), col(ng), col(2 * ng), gspec, gspec],
        out_specs=(col(0), pl.BlockSpec((hb, 1, hd, hd), lambda h, n: (h, n, 0, 0)),
                   pl.BlockSpec((hb, 1, cs, cs), lambda h, n: (h, n, 0, 0))),
        out_shape=(jax.ShapeDtypeStruct((t, nh * hd), F32), jax.ShapeDtypeStruct((nh, nc, hd, hd), F32),
                   jax.ShapeDtypeStruct((nh, nc, cs, cs), F32)),
        scratch_shapes=[pltpu.VMEM((hb, hd, hd), F32)],
        compiler_params=_params("parallel", "arbitrary"),
    )(qkv, qkv, qkv, gbc, bbc)


def _gdn_chunk_bwd(qkv, gbc, bbc, states, tinvs, do, *, name):
    t = qkv.shape[0]
    nh, cs, hd = GDN_HEADS, GDN_CHUNK, GDN_HEAD_DIM
    nc = t // cs

    hb = GDN_HEAD_BATCH
    ng = nh // hb

    def heads_bwd(q, k, v, gb, bb, s, ti, dsn, dov):
        hs = range(len(q))
        lo = _gdn_local(q, k, v, gb, bb, ti)
        tril, strict, eye, r, c = lo[0]["tril"], lo[0]["strict"], lo[0]["eye"], lo[0]["r"], lo[0]["c"]
        rowi = lax.broadcasted_iota(jnp.int32, (cs, 1), 0)
        get = lambda name: [lo[h][name] for h in hs]
        decay, gamma, gl, kdec = get("decay"), get("gamma"), get("gl"), get("kdec")
        kb, tinv, w, pmat, kd, qd = get("kb"), get("tinv"), get("w"), get("pmat"), get("kd"), get("qd")
        ws = [_dotb(w[h], s[h], NN) for h in hs]
        pdo = [_dotb(pmat[h], dov[h], TN) for h in hs]
        kds = [_dotb(kd[h], dsn[h], NN) for h in hs]
        dqd = [_dotb(dov[h], s[h], NT) for h in hs]
        qdo = [_dotb(qd[h], dov[h], TN) for h in hs]
        vn = [lo[h]["u"] - ws[h] for h in hs]
        dvn = [pdo[h] + kds[h] for h in hs]
        dp = [jnp.where(tril, _dotb(dov[h], vn[h], NT), 0.0) for h in hs]
        dkd = [_dotb(vn[h], dsn[h], NT) for h in hs]
        dw = [-_dotb(dvn[h], s[h], NT) for h in hs]
        wdv = [_dotb(w[h], dvn[h], TN) for h in hs]
        dvb = [_dotf(tinv[h], dvn[h], TN) for h in hs]
        dt1 = [_dotf(dvn[h], lo[h]["vb"], NT) for h in hs]
        dkg = [_dotf(tinv[h], dw[h], TN) for h in hs]
        dt2 = [_dotf(dw[h], lo[h]["kg"], NT) for h in hs]
        tdt = [_dotf(tinv[h], dt1[h] + dt2[h], TN) for h in hs]
        dl = [jnp.where(strict, -_dotf(tdt[h], tinv[h], NT), 0.0) for h in hs]
        dkk = [dl[h] * decay[h] for h in hs]
        dqk = [dp[h] * decay[h] for h in hs]
        dkb = [_dotb(dkk[h], k[h], NN) + dkg[h] * gamma[h] for h in hs]
        dk1 = [_dotb(dkk[h], kb[h], TN) for h in hs]
        dk2 = [_dotb(dqk[h], q[h], TN) for h in hs]
        dq1 = [_dotb(dqk[h], k[h], NN) for h in hs]
        out = []
        for h in hs:
            dgl = jnp.sum(jnp.sum(dsn[h] * s[h], axis=1, keepdims=True), axis=0, keepdims=True)
            ds_prev = gl[h] * dsn[h] + qdo[h] - wdv[h]
            dk = dk1[h] + dk2[h] + dkd[h] * kdec[h] + dkb[h] * bb[h]
            dq = dq1[h] + dqd[h] * gamma[h]
            dbeta = jnp.sum(dvb[h] * v[h], axis=-1, keepdims=True) + jnp.sum(dkb[h] * k[h], axis=-1, keepdims=True)
            e = dl[h] * lo[h]["lmat"] + dp[h] * pmat[h]
            e_col = jnp.sum(e, axis=0, keepdims=True)
            dgc = jnp.sum(e, axis=1, keepdims=True) - jnp.sum(jnp.where(eye, e_col, 0.0), axis=1, keepdims=True)
            dgamma = (jnp.sum(dqd[h] * q[h], axis=-1, keepdims=True)
                      + jnp.sum(dkg[h] * kb[h], axis=-1, keepdims=True))
            rk = jnp.sum(dkd[h] * k[h], axis=-1, keepdims=True) * kdec[h]
            dgcl = jnp.sum(rk, axis=0, keepdims=True) + dgl * gl[h]
            dgc = dgc + dgamma * gamma[h] - rk + jnp.where(rowi == cs - 1, dgcl, 0.0)
            dgc_row = jnp.sum(jnp.where(eye, dgc, 0.0), axis=0, keepdims=True)
            dg = jnp.sum(jnp.where(c >= r, dgc_row, 0.0), axis=1, keepdims=True)
            out.append((dq, dk, dvb[h] * bb[h], dbeta, dg, ds_prev))
        return out

    def body(q_ref, k_ref, v_ref, g_ref, b_ref, st_ref, ti_ref, do_ref, d_ref, dg_ref, db_ref, ds_ref):
        @pl.when(pl.program_id(1) == 0)
        def _():
            ds_ref[...] = jnp.zeros_like(ds_ref)

        sls = [slice(i * hd, (i + 1) * hd) for i in range(hb)]
        hs = range(hb)
        outs = heads_bwd([q_ref[:, sl] for sl in sls], [k_ref[:, sl] for sl in sls], [v_ref[:, sl] for sl in sls],
                         [g_ref[i] for i in hs], [b_ref[i] for i in hs], [st_ref[i, 0] for i in hs],
                         [ti_ref[i, 0] for i in hs], [ds_ref[i] for i in hs], [do_ref[:, sl] for sl in sls])
        for i, sl in enumerate(sls):
            dq, dk, dv, dbeta, dg, ds_prev = outs[i]
            d_ref[0, :, sl], d_ref[1, :, sl], d_ref[2, :, sl] = dq, dk, dv
            db_ref[i] = jnp.broadcast_to(dbeta, (cs, LANES))
            dg_ref[i] = jnp.broadcast_to(dg, (cs, LANES))
            ds_ref[i] = ds_prev

    gspec = pl.BlockSpec((hb, cs, LANES), lambda h, n: (h, nc - 1 - n, 0))
    col = lambda off: pl.BlockSpec((cs, hb * hd), lambda h, n: (nc - 1 - n, off + h))
    return pl.pallas_call(
        body, name=name, grid=(ng, nc),
        in_specs=[col(0), col(ng), col(2 * ng), gspec, gspec,
                  pl.BlockSpec((hb, 1, hd, hd), lambda h, n: (h, nc - 1 - n, 0, 0)),
                  pl.BlockSpec((hb, 1, cs, cs), lambda h, n: (h, nc - 1 - n, 0, 0)), col(0)],
        out_specs=(pl.BlockSpec((3, cs, hb * hd), lambda h, n: (0, nc - 1 - n, h)), gspec, gspec),
        out_shape=(jax.ShapeDtypeStruct((3, t, nh * hd), F32),) + (jax.ShapeDtypeStruct((nh, t, LANES), F32),) * 2,
        scratch_shapes=[pltpu.VMEM((hb, hd, hd), F32)],
        compiler_params=_params("parallel", "arbitrary"),
    )(qkv, qkv, qkv, gbc, bbc, states, tinvs, do)


def _gdn_onorm_fwd(o, proj, norm_g, *, name):
    t = o.shape[0]
    w = GDN_KEY_DIM
    goff = 3 * GDN_KEY_DIM // w

    def body(o_ref, gp_ref, g_ref, y_ref):
        gv = g_ref[...]
        for h in range(GDN_HEADS):
            sl = slice(h * GDN_HEAD_DIM, (h + 1) * GDN_HEAD_DIM)
            oh = o_ref[:, sl]
            gp = gp_ref[:, sl]
            r = lax.rsqrt(jnp.mean(oh * oh, axis=-1, keepdims=True) + EPS)
            y_ref[:, sl] = (oh * r * gv * gp * _sigmoid(gp)).astype(y_ref.dtype)

    return pl.pallas_call(
        body, name=name, grid=(t // ROWS,),
        in_specs=[_row_spec(w), pl.BlockSpec((ROWS, w), lambda i: (i, goff)), _const_spec((1, GDN_HEAD_DIM))],
        out_specs=_row_spec(w), out_shape=jax.ShapeDtypeStruct((t, w), BF16),
        compiler_params=_params("parallel"),
    )(o, proj, norm_g)


def _gdn_onorm_bwd(o, proj, norm_g, dy, *, name):
    t = o.shape[0]
    w = GDN_KEY_DIM
    goff = 3 * GDN_KEY_DIM // w

    def body(o_ref, gp_ref, g_ref, dy_ref, do_ref, dgp_ref, st_ref):
        @pl.when(pl.program_id(0) == 0)
        def _():
            st_ref[...] = jnp.zeros_like(st_ref)

        gv = g_ref[...]
        acc = jnp.zeros((1, GDN_HEAD_DIM), F32)
        for h in range(GDN_HEADS):
            sl = slice(h * GDN_HEAD_DIM, (h + 1) * GDN_HEAD_DIM)
            oh = o_ref[:, sl]
            gp = gp_ref[:, sl]
            dyv = dy_ref[:, sl].astype(F32)
            r = lax.rsqrt(jnp.mean(oh * oh, axis=-1, keepdims=True) + EPS)
            xh = oh * r
            sg = _sigmoid(gp)
            dn = dyv * gp * sg
            dgp_ref[:, sl] = (dyv * xh * gv * sg * (1.0 + gp * (1.0 - sg))).astype(dgp_ref.dtype)
            acc = acc + jnp.sum(dn * xh, axis=0, keepdims=True)
            dxh = dn * gv
            do_ref[:, sl] = r * (dxh - xh * jnp.mean(dxh * xh, axis=-1, keepdims=True))
        st_ref[0:1, :] += acc

    return pl.pallas_call(
        body, name=name, grid=(t // ROWS,),
        in_specs=[_row_spec(w), pl.BlockSpec((ROWS, w), lambda i: (i, goff)), _const_spec((1, GDN_HEAD_DIM)),
                  _row_spec(w)],
        out_specs=(_row_spec(w), _row_spec(w), _const_spec((8, GDN_HEAD_DIM))),
        out_shape=(jax.ShapeDtypeStruct((t, w), F32), jax.ShapeDtypeStruct((t, w), BF16),
                   jax.ShapeDtypeStruct((8, GDN_HEAD_DIM), F32)),
        compiler_params=_params("arbitrary"),
    )(o, proj, norm_g, dy)


def _mla_prep_fwd(proj, qg, kvg, *, name):
    t = proj.shape[0]
    q1, k1 = MLA_Q_RANK, MLA_Q_RANK + MLA_KV_RANK

    def body(p_ref, qg_ref, kg_ref, cq_ref, ck_ref):
        cq = p_ref[:, 0:q1]
        ck = p_ref[:, q1:k1]
        cq_ref[...] = (cq * lax.rsqrt(jnp.mean(cq * cq, axis=-1, keepdims=True) + EPS) * qg_ref[...]).astype(BF16)
        ck_ref[...] = (ck * lax.rsqrt(jnp.mean(ck * ck, axis=-1, keepdims=True) + EPS) * kg_ref[...]).astype(BF16)

    return pl.pallas_call(
        body, name=name, grid=(t // ROWS,),
        in_specs=[_row_spec(MLA_IN), _const_spec((1, MLA_Q_RANK)), _const_spec((1, MLA_KV_RANK))],
        out_specs=(_row_spec(MLA_Q_RANK), _row_spec(MLA_KV_RANK)),
        out_shape=(jax.ShapeDtypeStruct((t, MLA_Q_RANK), BF16), jax.ShapeDtypeStruct((t, MLA_KV_RANK), BF16)),
        compiler_params=_params("parallel"),
    )(proj, qg, kvg)


def _mla_prep_bwd(proj, qg, kvg, dcq, dck, dkr, *, name):
    t = proj.shape[0]
    q1, k1 = MLA_Q_RANK, MLA_Q_RANK + MLA_KV_RANK

    def body(p_ref, qg_ref, kg_ref, dq_ref, dk_ref, dr_ref, dp_ref, st_ref):
        @pl.when(pl.program_id(0) == 0)
        def _():
            st_ref[...] = jnp.zeros_like(st_ref)

        for lo, hi, g_ref, d_ref in ((0, q1, qg_ref, dq_ref), (q1, k1, kg_ref, dk_ref)):
            xv = p_ref[:, lo:hi]
            dn = d_ref[...]
            r = lax.rsqrt(jnp.mean(xv * xv, axis=-1, keepdims=True) + EPS)
            xh = xv * r
            dxh = dn * g_ref[...]
            dp_ref[:, lo:hi] = (r * (dxh - xh * jnp.mean(dxh * xh, axis=-1, keepdims=True))).astype(dp_ref.dtype)
            st_ref[0:1, lo:hi] += jnp.sum(dn * xh, axis=0, keepdims=True)
        dp_ref[:, k1:MLA_IN] = dr_ref[:, 0:MLA_ROPE].astype(dp_ref.dtype)

    return pl.pallas_call(
        body, name=name, grid=(t // ROWS,),
        in_specs=[_row_spec(MLA_IN), _const_spec((1, MLA_Q_RANK)), _const_spec((1, MLA_KV_RANK)),
                  _row_spec(MLA_Q_RANK), _row_spec(MLA_KV_RANK), _row_spec(LANES)],
        out_specs=(_row_spec(MLA_IN), _const_spec((8, MLA_IN))),
        out_shape=(jax.ShapeDtypeStruct((t, MLA_IN), BF16), jax.ShapeDtypeStruct((8, MLA_IN), F32)),
        compiler_params=_params("arbitrary"),
    )(proj, qg, kvg, dcq, dck, dkr)


def _rope(xr, cos_t, sin_t, *, name):
    t, w = xr.shape
    ns = w // LANES

    def body(x_ref, c_ref, s_ref, o_ref):
        cv, sv = c_ref[...], s_ref[...]
        lane = lax.broadcasted_iota(jnp.int32, (ROWS, LANES), 1)
        first = (lane % MLA_ROPE) < (MLA_ROPE // 2)
        for i in range(ns):
            sl = slice(i * LANES, (i + 1) * LANES)
            xv = x_ref[:, sl]
            sw = jnp.where(first, pltpu.roll(xv, LANES - MLA_ROPE // 2, 1), pltpu.roll(xv, MLA_ROPE // 2, 1))
            o_ref[:, sl] = xv * cv + sw * sv

    return pl.pallas_call(
        body, name=name, grid=(t // ROWS,),
        in_specs=[_row_spec(w), _row_spec(LANES), _row_spec(LANES)], out_specs=_row_spec(w),
        out_shape=jax.ShapeDtypeStruct((t, w), F32), compiler_params=_params("parallel"),
    )(xr, cos_t, sin_t)


def _rope_bwd(dr, cos_t, sin_t, *, name):
    t, w = dr.shape
    ns = w // LANES

    def body(d_ref, c_ref, s_ref, o_ref):
        cv, sv = c_ref[...], s_ref[...]
        lane = lax.broadcasted_iota(jnp.int32, (ROWS, LANES), 1)
        first = (lane % MLA_ROPE) < (MLA_ROPE // 2)
        for i in range(ns):
            sl = slice(i * LANES, (i + 1) * LANES)
            dv = d_ref[:, sl]
            ds = dv * sv
            sw = jnp.where(first, pltpu.roll(ds, LANES - MLA_ROPE // 2, 1), pltpu.roll(ds, MLA_ROPE // 2, 1))
            o_ref[:, sl] = dv * cv + sw

    return pl.pallas_call(
        body, name=name, grid=(t // ROWS,),
        in_specs=[_row_spec(w), _row_spec(LANES), _row_spec(LANES)], out_specs=_row_spec(w),
        out_shape=jax.ShapeDtypeStruct((t, w), F32), compiler_params=_params("parallel"),
    )(dr, cos_t, sin_t)


ATT_BLOCK = 256
ATT_HEAD_BATCH = 4
ATT_HEAD_BATCH_BWD = 4
ATT_SCALE = MLA_QK ** -0.5


def _causal_mask(i, j, blk):
    rows = i * blk + lax.broadcasted_iota(jnp.int32, (blk, blk), 0)
    cols = j * blk + lax.broadcasted_iota(jnp.int32, (blk, blk), 1)
    return cols <= rows


def _attn_fwd(q, k, v, *, name):
    nh, t, dk = q.shape
    dv = v.shape[-1]
    blk = min(ATT_BLOCK, t)

    hb = ATT_HEAD_BATCH
    hs = range(hb)

    def body(q_ref, k_ref, v_ref, o_ref, l_ref):
        i = pl.program_id(1)
        qv = [q_ref[h] for h in hs]

        def step(j, carry):
            m, l, acc = carry[:hb], carry[hb:2 * hb], carry[2 * hb:]
            off = pl.multiple_of(j * blk, blk)
            mask = _causal_mask(i, j, blk)
            s = [_dotb(qv[h], k_ref[h, pl.ds(off, blk), :], NT) for h in hs]
            s = [jnp.where(mask, s[h] * ATT_SCALE, NEG) for h in hs]
            m_new = [jnp.maximum(m[h], jnp.max(s[h], axis=-1, keepdims=True)) for h in hs]
            p = [jnp.exp(s[h] - m_new[h]) for h in hs]
            pv = [_dotb(p[h], v_ref[h, pl.ds(off, blk), :], NN) for h in hs]
            alpha = [jnp.exp(m[h] - m_new[h]) for h in hs]
            l = [alpha[h] * l[h] + jnp.sum(p[h], axis=-1, keepdims=True) for h in hs]
            acc = [alpha[h] * acc[h] + pv[h] for h in hs]
            return tuple(m_new) + tuple(l) + tuple(acc)

        init = ((jnp.full((blk, 1), NEG, F32),) * hb + (jnp.zeros((blk, 1), F32),) * hb
                + (jnp.zeros((blk, dv), F32),) * hb)
        out = lax.fori_loop(0, i + 1, step, init)
        for h in hs:
            m, l, acc = out[h], out[hb + h], out[2 * hb + h]
            o_ref[h] = acc / l
            l_ref[h] = jnp.broadcast_to(m + jnp.log(l), (blk, LANES))

    return pl.pallas_call(
        body, name=name, grid=(nh // hb, t // blk),
        in_specs=[pl.BlockSpec((hb, blk, dk), lambda h, i: (h, i, 0)), pl.BlockSpec((hb, t, dk), lambda h, i: (h, 0, 0)),
                  pl.BlockSpec((hb, t, dv), lambda h, i: (h, 0, 0))],
        out_specs=(pl.BlockSpec((hb, blk, dv), lambda h, i: (h, i, 0)),
                   pl.BlockSpec((hb, blk, LANES), lambda h, i: (h, i, 0))),
        out_shape=(jax.ShapeDtypeStruct((nh, t, dv), F32), jax.ShapeDtypeStruct((nh, t, LANES), F32)),
        compiler_params=_params("parallel", "parallel"),
    )(q, k, v)


def _attn_bwd(q, k, v, o, lse, do, *, name):
    nh, t, dk = q.shape
    dv = v.shape[-1]
    blk = min(ATT_BLOCK, t)
    nb = t // blk

    hb = ATT_HEAD_BATCH_BWD
    hs = range(hb)

    def body(q_ref, k_ref, v_ref, o_ref, l_ref, do_ref, dq_ref, dk_ref, dv_ref):
        j = pl.program_id(1)

        @pl.when(j == 0)
        def _():
            dq_ref[...] = jnp.zeros_like(dq_ref)

        kv = [k_ref[h] for h in hs]
        vv = [v_ref[h] for h in hs]

        def step(i, carry):
            dk_acc, dv_acc = carry[:hb], carry[hb:]
            off = pl.multiple_of(i * blk, blk)
            rows = pl.ds(off, blk)
            mask = _causal_mask(i, j, blk)
            qv = [q_ref[h, rows, :] for h in hs]
            dov = [do_ref[h, rows, :] for h in hs]
            s = [_dotb(qv[h], kv[h], NT) for h in hs]
            dp = [_dotb(dov[h], vv[h], NT) for h in hs]
            p = [jnp.exp(jnp.where(mask, s[h] * ATT_SCALE, NEG) - l_ref[h, rows, :][:, 0:1]) for h in hs]
            delta = [jnp.sum(dov[h] * o_ref[h, rows, :], axis=-1, keepdims=True) for h in hs]
            ds = [p[h] * (dp[h] - delta[h]) * ATT_SCALE for h in hs]
            dvn = [_dotb(p[h], dov[h], TN) for h in hs]
            dkn = [_dotb(ds[h], qv[h], TN) for h in hs]
            dqn = [_dotb(ds[h], kv[h], NN) for h in hs]
            for h in hs:
                dq_ref[h, rows, :] += dqn[h]
            return tuple(dk_acc[h] + dkn[h] for h in hs) + tuple(dv_acc[h] + dvn[h] for h in hs)

        out = lax.fori_loop(j, nb, step, (jnp.zeros((blk, dk), F32),) * hb + (jnp.zeros((blk, dv), F32),) * hb)
        for h in hs:
            dk_ref[h] = out[h]
            dv_ref[h] = out[hb + h]

    full = lambda w: pl.BlockSpec((hb, t, w), lambda h, j: (h, 0, 0))
    part = lambda w: pl.BlockSpec((hb, blk, w), lambda h, j: (h, j, 0))
    return pl.pallas_call(
        body, name=name, grid=(nh // hb, nb),
        in_specs=[full(dk), part(dk), part(dv), full(dv), full(LANES), full(dv)],
        out_specs=(full(dk), part(dk), part(dv)),
        out_shape=(jax.ShapeDtypeStruct((nh, t, dk), F32), jax.ShapeDtypeStruct((nh, t, dk), F32),
                   jax.ShapeDtypeStruct((nh, t, dv), F32)),
        compiler_params=_params("parallel", "arbitrary"),
    )(q, k, v, o, lse, do)


def _swap_halves(xv, first):
    return jnp.where(first, pltpu.roll(xv, LANES - MLA_ROPE // 2, 1), pltpu.roll(xv, MLA_ROPE // 2, 1))


def _rope_qk(qf, proj, cos_t, sin_t, *, name):
    t = qf.shape[0]
    nrope = MLA_HEADS * MLA_ROPE
    q_blk = MLA_HEADS * MLA_NOPE // nrope
    k_blk = (MLA_Q_RANK + MLA_KV_RANK) // LANES

    def body(q_ref, p_ref, c_ref, s_ref, qo_ref, ko_ref):
        cv, sv = c_ref[...], s_ref[...]
        lane = lax.broadcasted_iota(jnp.int32, (ROWS, LANES), 1)
        first = (lane % MLA_ROPE) < (MLA_ROPE // 2)
        for i in range(nrope // LANES):
            sl = slice(i * LANES, (i + 1) * LANES)
            xv = q_ref[:, sl].astype(F32)
            qo_ref[:, sl] = (xv * cv + _swap_halves(xv, first) * sv).astype(qo_ref.dtype)
        kv = jnp.where(lane < MLA_ROPE, p_ref[...], 0.0)
        ko_ref[...] = (kv * cv + _swap_halves(kv, first) * sv).astype(ko_ref.dtype)

    return pl.pallas_call(
        body, name=name, grid=(t // ROWS,),
        in_specs=[pl.BlockSpec((ROWS, nrope), lambda i: (i, q_blk)), pl.BlockSpec((ROWS, LANES), lambda i: (i, k_blk)),
                  _row_spec(LANES), _row_spec(LANES)],
        out_specs=(_row_spec(nrope), _row_spec(LANES)),
        out_shape=(jax.ShapeDtypeStruct((t, nrope), BF16), jax.ShapeDtypeStruct((t, LANES), BF16)),
        compiler_params=_params("parallel"),
    )(qf, proj, cos_t, sin_t)


def _rope_qk_bwd(dqr, dkr_parts, cos_t, sin_t, *, name):
    t, nrope = dqr.shape
    ng = dkr_parts.shape[0]

    def body(d_ref, k_ref, c_ref, s_ref, qo_ref, ko_ref):
        cv, sv = c_ref[...], s_ref[...]
        lane = lax.broadcasted_iota(jnp.int32, (ROWS, LANES), 1)
        first = (lane % MLA_ROPE) < (MLA_ROPE // 2)
        for i in range(nrope // LANES):
            sl = slice(i * LANES, (i + 1) * LANES)
            dv = d_ref[:, sl]
            qo_ref[:, sl] = (dv * cv + _swap_halves(dv * sv, first)).astype(qo_ref.dtype)
        dk = k_ref[0]
        for g in range(1, ng):
            dk = dk + k_ref[g]
        dk = jnp.where(lane < MLA_ROPE, dk, 0.0)
        ko_ref[...] = jnp.where(lane < MLA_ROPE, dk * cv + _swap_halves(dk * sv, first), 0.0)

    return pl.pallas_call(
        body, name=name, grid=(t // ROWS,),
        in_specs=[_row_spec(nrope), pl.BlockSpec((ng, ROWS, LANES), lambda i: (0, i, 0)), _row_spec(LANES),
                  _row_spec(LANES)],
        out_specs=(_row_spec(nrope), _row_spec(LANES)),
        out_shape=(jax.ShapeDtypeStruct((t, nrope), BF16), jax.ShapeDtypeStruct((t, LANES), F32)),
        compiler_params=_params("parallel"),
    )(dqr, dkr_parts, cos_t, sin_t)


def _attn_tm_fwd(qf, qr, kvf, kr, *, name):
    t = qf.shape[0]
    nh, dn, dr, dv = MLA_HEADS, MLA_NOPE, MLA_ROPE, MLA_V
    blk = min(ATT_BLOCK, t)
    hb = ATT_HEAD_BATCH
    hs = range(hb)

    def body(q_ref, qr_ref, kv_ref, kr_ref, o_ref, l_ref):
        i = pl.program_id(1)
        qn = [q_ref[:, h * dn:(h + 1) * dn].astype(MXU_DTYPE) for h in hs]
        qrh = [qr_ref[:, h * dr:(h + 1) * dr] for h in hs]

        def step(j, carry, diagonal=False):
            m, l, acc = carry[:hb], carry[hb:2 * hb], carry[2 * hb:]
            rows = pl.ds(pl.multiple_of(j * blk, blk), blk)
            krj = kr_ref[rows, 0:dr]
            s = [_dotb(qn[h], kv_ref[rows, h * (dn + dv):h * (dn + dv) + dn], NT) for h in hs]
            sr = [_dotb(qrh[h], krj, NT) for h in hs]
            s = [(s[h] + sr[h]) * ATT_SCALE for h in hs]
            if diagonal:
                mask = _causal_mask(0, 0, blk)
                s = [jnp.where(mask, s[h], NEG) for h in hs]
            m_new = [jnp.maximum(m[h], jnp.max(s[h], axis=-1, keepdims=True)) for h in hs]
            p = [jnp.exp(s[h] - m_new[h]) for h in hs]
            pv = [_dotb(p[h], kv_ref[rows, h * (dn + dv) + dn:(h + 1) * (dn + dv)], NN) for h in hs]
            alpha = [jnp.exp(m[h] - m_new[h]) for h in hs]
            l = [alpha[h] * l[h] + jnp.sum(p[h], axis=-1, keepdims=True) for h in hs]
            acc = [alpha[h] * acc[h] + pv[h] for h in hs]
            return tuple(m_new) + tuple(l) + tuple(acc)

        init = ((jnp.full((blk, 1), NEG, F32),) * hb + (jnp.zeros((blk, 1), F32),) * hb
                + (jnp.zeros((blk, dv), F32),) * hb)
        out = step(i, lax.fori_loop(0, i, step, init), diagonal=True)
        for h in hs:
            m, l, acc = out[h], out[hb + h], out[2 * hb + h]
            o_ref[:, h * dv:(h + 1) * dv] = (acc / l).astype(o_ref.dtype)
            l_ref[h] = jnp.broadcast_to(m + jnp.log(l), (blk, LANES))

    return pl.pallas_call(
        body, name=name, grid=(nh // hb, t // blk),
        in_specs=[pl.BlockSpec((blk, hb * dn), lambda g, i: (i, g)), pl.BlockSpec((blk, hb * dr), lambda g, i: (i, g)),
                  pl.BlockSpec((t, hb * (dn + dv)), lambda g, i: (0, g)), pl.BlockSpec((t, LANES), lambda g, i: (0, 0))],
        out_specs=(pl.BlockSpec((blk, hb * dv), lambda g, i: (i, g)),
                   pl.BlockSpec((hb, blk, LANES), lambda g, i: (g, i, 0))),
        out_shape=(jax.ShapeDtypeStruct((t, nh * dv), BF16), jax.ShapeDtypeStruct((nh, t, LANES), F32)),
        compiler_params=_params("parallel", "parallel"),
    )(qf, qr, kvf, kr)


def _attn_tm_bwd(qf, qr, kvf, kr, o, lse, do, *, name):
    t = qf.shape[0]
    nh, dn, dr, dv = MLA_HEADS, MLA_NOPE, MLA_ROPE, MLA_V
    blk = min(ATT_BLOCK, t)
    nb = t // blk
    hb = ATT_HEAD_BATCH_BWD
    hs = range(hb)
    ng = nh // hb

    def body(q_ref, qr_ref, kv_ref, kr_ref, o_ref, l_ref, do_ref, dqn_ref, dqr_ref, dkv_ref, dkr_ref):
        j = pl.program_id(1)

        @pl.when(j == 0)
        def _():
            dqn_ref[...] = jnp.zeros_like(dqn_ref)
            dqr_ref[...] = jnp.zeros_like(dqr_ref)

        kn = [kv_ref[:, h * (dn + dv):h * (dn + dv) + dn] for h in hs]
        vv = [kv_ref[:, h * (dn + dv) + dn:(h + 1) * (dn + dv)] for h in hs]
        krj = kr_ref[:, 0:dr]

        def step(i, carry, diagonal=False):
            dkn_acc, dv_acc, dkr_acc = carry[:hb], carry[hb:2 * hb], carry[2 * hb]
            rows = pl.ds(pl.multiple_of(i * blk, blk), blk)
            qn = [q_ref[rows, h * dn:(h + 1) * dn].astype(MXU_DTYPE) for h in hs]
            qrh = [qr_ref[rows, h * dr:(h + 1) * dr] for h in hs]
            dov = [do_ref[rows, h * dv:(h + 1) * dv] for h in hs]
            s = [_dotb(qn[h], kn[h], NT) for h in hs]
            sr = [_dotb(qrh[h], krj, NT) for h in hs]
            dp = [_dotb(dov[h], vv[h], NT) for h in hs]
            s = [(s[h] + sr[h]) * ATT_SCALE for h in hs]
            if diagonal:
                mask = _causal_mask(0, 0, blk)
                s = [jnp.where(mask, s[h], NEG) for h in hs]
            p = [jnp.exp(s[h] - l_ref[h, rows, :][:, 0:1]) for h in hs]
            delta = [jnp.sum(dov[h].astype(F32) * o_ref[rows, h * dv:(h + 1) * dv].astype(F32), axis=-1, keepdims=True)
                     for h in hs]
            ds = [p[h] * (dp[h] - delta[h]) * ATT_SCALE for h in hs]
            dvn = [_dotb(p[h], dov[h], TN) for h in hs]
            dknn = [_dotb(ds[h], qn[h], TN) for h in hs]
            dkrn = [_dotb(ds[h], qrh[h], TN) for h in hs]
            dqnn = [_dotb(ds[h], kn[h], NN) for h in hs]
            dqrn = [_dotb(ds[h], krj, NN) for h in hs]
            for h in hs:
                dqn_ref[rows, h * dn:(h + 1) * dn] += dqnn[h]
                dqr_ref[rows, h * dr:(h + 1) * dr] += dqrn[h]
            dkr_new = dkr_acc
            for h in hs:
                dkr_new = dkr_new + dkrn[h]
            return (tuple(dkn_acc[h] + dknn[h] for h in hs) + tuple(dv_acc[h] + dvn[h] for h in hs) + (dkr_new,))

        init = (jnp.zeros((blk, dn), F32),) * hb + (jnp.zeros((blk, dv), F32),) * hb + (jnp.zeros((blk, dr), F32),)
        out = lax.fori_loop(j + 1, nb, step, step(j, init, diagonal=True))
        for h in hs:
            dkv_ref[:, h * (dn + dv):h * (dn + dv) + dn] = out[h].astype(dkv_ref.dtype)
            dkv_ref[:, h * (dn + dv) + dn:(h + 1) * (dn + dv)] = out[hb + h].astype(dkv_ref.dtype)
        dkr_ref[0, :, 0:dr] = out[2 * hb]
        dkr_ref[0, :, dr:LANES] = jnp.zeros((blk, LANES - dr), F32)

    full = lambda w: pl.BlockSpec((t, w), lambda g, j: (0, g))
    return pl.pallas_call(
        body, name=name, grid=(ng, nb),
        in_specs=[full(hb * dn), full(hb * dr), pl.BlockSpec((blk, hb * (dn + dv)), lambda g, j: (j, g)),
                  pl.BlockSpec((blk, LANES), lambda g, j: (j, 0)), full(hb * dv),
                  pl.BlockSpec((hb, t, LANES), lambda g, j: (g, 0, 0)), full(hb * dv)],
        out_specs=(full(hb * dn), full(hb * dr), pl.BlockSpec((blk, hb * (dn + dv)), lambda g, j: (j, g)),
                   pl.BlockSpec((1, blk, LANES), lambda g, j: (g, j, 0))),
        out_shape=(jax.ShapeDtypeStruct((t, nh * dn), F32), jax.ShapeDtypeStruct((t, nh * dr), F32),
                   jax.ShapeDtypeStruct((t, nh * (dn + dv)), BF16), jax.ShapeDtypeStruct((ng, t, LANES), F32)),
        compiler_params=_params("parallel", "arbitrary"),
    )(qf, qr, kvf, kr, o, lse, do)


def _ada_mod(c_all, ada_w, ada_b_cols, *, name):
    nl, d, wc = ada_w.shape

    def body(c_ref, w_ref, b_ref, o_ref):
        cv = c_ref[...]
        o_ref[0] = _dotb(cv * _sigmoid(cv), w_ref[0], NN) + b_ref[0]

    return pl.pallas_call(
        body, name=name, grid=(nl,),
        in_specs=[_const_spec((N_DEV, d)), pl.BlockSpec((1, d, wc), lambda l: (l, 0, 0)),
                  pl.BlockSpec((1, 1, wc), lambda l: (l, 0, 0))],
        out_specs=pl.BlockSpec((1, N_DEV, wc), lambda l: (l, 0, 0)),
        out_shape=jax.ShapeDtypeStruct((nl, N_DEV, wc), F32), compiler_params=_params("parallel"),
    )(c_all, ada_w, ada_b_cols)


def _adam_math(g, w, m, v):
    m2 = ADAM_B1 * m + (1.0 - ADAM_B1) * g
    v2 = ADAM_B2 * v + (1.0 - ADAM_B2) * (g * g)
    delta = -ADAM_LR * ((m2 / ADAM_BC1) / (jnp.sqrt(v2 / ADAM_BC2) + ADAM_EPS) + ADAM_WD * w)
    return delta, m2, v2


def _ada_grad_adamw(c_all, dmod_cols, w, m, v, *, name):
    nl, d, wc = w.shape
    tr = 256

    def body(c_ref, dm_ref, w_ref, m_ref, v_ref, g_ref, d_ref, m2_ref, v2_ref):
        cv = c_ref[...]
        g = _dotf(cv * _sigmoid(cv), dm_ref[0], TN)
        delta, m2, v2 = _adam_math(g, w_ref[0], m_ref[0], v_ref[0])
        g_ref[0], d_ref[0], m2_ref[0], v2_ref[0] = g, delta, m2, v2

    blk = pl.BlockSpec((1, tr, wc), lambda l, i: (l, i, 0))
    return pl.pallas_call(
        body, name=name, grid=(nl, d // tr),
        in_specs=[pl.BlockSpec((N_DEV, tr), lambda l, i: (0, i)), pl.BlockSpec((1, N_DEV, wc), lambda l, i: (l, 0, 0)),
                  blk, blk, blk],
        out_specs=(blk,) * 4, out_shape=(jax.ShapeDtypeStruct(w.shape, F32),) * 4,
        compiler_params=_params("parallel", "parallel"),
    )(c_all, dmod_cols, w, m, v)


def _adamw(parts, w, m, v, *, name):
    nl, r, c = w.shape
    ns = parts[0].shape[0]
    lanes_padded = -(-c // LANES) * LANES
    row_bytes = 2 * nl * ns * lanes_padded * parts[0].dtype.itemsize
    tr = _pick(r, min(256, max(16, (VMEM_LIMIT // 2) // row_bytes)), 16)
    tc = c
    if tr * row_bytes > VMEM_LIMIT // 2:
        tc = _pick(c, max(LANES, c * (VMEM_LIMIT // 2) // (tr * row_bytes)))

    def body(*refs):
        p_refs = refs[:nl]
        w_ref, m_ref, v_ref, g_ref, d_ref, m2_ref, v2_ref = refs[nl:]
        layer = pl.program_id(0)
        for q in range(nl):
            @pl.when(layer == q)
            def _(q=q):
                g = p_refs[q][0].astype(F32)
                for s in range(1, ns):
                    g = g + p_refs[q][s].astype(F32)
                delta, m2, v2 = _adam_math(g, w_ref[0], m_ref[0], v_ref[0])
                g_ref[0], d_ref[0], m2_ref[0], v2_ref[0] = g, delta, m2, v2

    blk = pl.BlockSpec((1, tr, tc), lambda l, i, j: (l, i, j))
    p_specs = [pl.BlockSpec((ns, tr, tc), lambda l, i, j, q=q: (0, jnp.where(l == q, i, 0), jnp.where(l == q, j, 0)))
               for q in range(nl)]
    return pl.pallas_call(
        body, name=name, grid=(nl, r // tr, c // tc),
        in_specs=p_specs + [blk, blk, blk],
        out_specs=(blk,) * 4, out_shape=(jax.ShapeDtypeStruct(w.shape, F32),) * 4,
        compiler_params=_params("arbitrary", "arbitrary", "arbitrary"),
    )(*parts, w, m, v)


def _sum_parts(parts, *, name):
    ns, r, c = parts.shape

    def body(p_ref, o_ref):
        acc = p_ref[0]
        for s in range(1, ns):
            acc = acc + p_ref[s]
        o_ref[...] = acc

    return pl.pallas_call(
        body, name=name, out_shape=jax.ShapeDtypeStruct((r, c), F32),
        in_specs=[pl.BlockSpec(memory_space=pltpu.VMEM)], out_specs=pl.BlockSpec(memory_space=pltpu.VMEM),
    )(parts)


def _pack(arrs):
    flat = jnp.concatenate([a.reshape(-1).astype(F32) for a in arrs])
    pad = (-flat.shape[0]) % (8 * LANES)
    return jnp.pad(flat, (0, pad)).reshape(-1, LANES)


def _unpack(packed, shapes, lead=()):
    flat = packed.reshape(lead + (-1,))
    out, off = [], 0
    for s in shapes:
        n = math.prod(s)
        out.append(flat[..., off:off + n].reshape(lead + tuple(s)))
        off += n
    return out


def _gather_cols(g):
    _, nl, r, cs = g.shape
    return jnp.transpose(g, (1, 2, 0, 3)).reshape(nl, r, N_DEV * cs)


def _gather_rows(g):
    _, nl, rs, c = g.shape
    return jnp.transpose(g, (1, 0, 2, 3)).reshape(nl, N_DEV * rs, c)


def _scatter_cols(full):
    nl, r, c = full.shape
    return jnp.transpose(full.reshape(nl, r, N_DEV, c // N_DEV), (2, 0, 1, 3))


def _scatter_rows(full):
    nl, r, c = full.shape
    return jnp.transpose(full.reshape(nl, N_DEV, r // N_DEV, c), (1, 0, 2, 3))


def _row(v):
    return v.reshape(1, -1)


def _local_step(x, target, mod, cos_t, sin_t, rep, get_weights, put_grads):
    t = x.shape[0]
    saved = []
    for layer in range(DEPTH):
        j = layer // 2
        tag = f"l{layer}"
        shift_m, scale_m, gate_m, shift_f, scale_f, gate_f = [_row(mod[layer, i]) for i in range(N_MOD)]
        lw = dict(get_weights(layer, "mix", x))
        rec = {"x0": x, "lw": lw}
        h = _adaln_fwd(x, _row(rep["norm_mix_g"][layer]), scale_m, shift_m, name=f"adaln_mix_{tag}")
        rec["h"] = h
        if layer % 2 == 0:
            proj = _mm(h, lw["wt_in"], mode="nt", out_dtype=F32, tm=256, tn=GDN_MAIN, b_rows=GDN_MAIN,
                       dep=lw["dep_mix"], name=f"gdn_in_{tag}")
            ab = _mm(h, lw["wt_ab"], mode="nt", out_dtype=F32, name=f"gdn_in_ab_{tag}")
            qkv = _gdn_prep_fwd(proj, rep["gdn_conv_wt"][j], name=f"gdn_prep_{tag}")
            gbeta = _gdn_gate_fwd(ab, rep["gdn_gate_prm"][j], name=f"gdn_gate_{tag}")
            gbc = jnp.broadcast_to(jnp.transpose(gbeta[:, 0:GDN_HEADS])[:, :, None], (GDN_HEADS, t, LANES))
            bbc = jnp.broadcast_to(jnp.transpose(gbeta[:, GDN_HEADS:2 * GDN_HEADS])[:, :, None],
                                   (GDN_HEADS, t, LANES))
            o, states, tinvs = _gdn_chunk_fwd(qkv, gbc, bbc, name=f"gdn_chunk_{tag}")
            og = _gdn_onorm_fwd(o, proj, _row(rep["gdn_norm_g"][j]), name=f"gdn_onorm_{tag}")
            x, y = _mm_resid(og, lw["w_out"], x, gate_m, name=f"gdn_out_{tag}")
            rec.update(proj=proj, ab=ab, qkv=qkv, gbc=gbc, bbc=bbc, states=states, tinvs=tinvs, o=o, og=og, y=y)
        else:
            proj = _mm(h, lw["w_in"], mode="nn", out_dtype=F32, dep=lw["dep_mix"], name=f"mla_in_{tag}")
            cq, ck = _mla_prep_fwd(proj, _row(rep["mla_q_norm_g"][j]), _row(rep["mla_kv_norm_g"][j]),
                                   name=f"mla_prep_{tag}")
            qf = _mm(cq, lw["wt_uq"], mode="nt", out_dtype=BF16, name=f"mla_uq_{tag}")
            kvf = _mm(ck, lw["w_ukv"], mode="nn", out_dtype=BF16, name=f"mla_ukv_{tag}")
            qr, kr = _rope_qk(qf, proj, cos_t, sin_t, name=f"rope_{tag}")
            oc, lse = _attn_tm_fwd(qf, qr, kvf, kr, name=f"attn_{tag}")
            x, y = _mm_resid(oc, lw["w_out"], x, gate_m, name=f"mla_out_{tag}")
            rec.update(proj=proj, cq=cq, ck=ck, qf=qf, qr=qr, kvf=kvf, kr=kr, lse=lse, oc=oc, y=y)
        rec["x1"] = x
        lw.update(get_weights(layer, "ffn", x))
        h2 = _adaln_fwd(x, _row(rep["norm_ffn_g"][layer]), scale_f, shift_f, name=f"adaln_ffn_{tag}")
        s, a2, b2 = _ffn_gu_fwd(h2, lw["wt_g"], lw["wt_u"], lw["dep_ffn"], name=f"ffn_gu_{tag}")
        x, y2 = _mm_resid(s, lw["w_down"], x, gate_f, tm=512, name=f"ffn_down_{tag}")
        rec.update(h2=h2, a2=a2, b2=b2, s=s, y2=y2)
        saved.append(rec)

    dx, st, ls = _loss_head(x, _row(rep["final_norm_g"]), target, name="loss_head")
    loss = ls[0, 0]
    grads = {"final_norm_g": st[0]}
    per_layer = {k: [None] * DEPTH for k in ("norm_mix_g", "norm_ffn_g")}
    per_gdn = {k: [None] * 2 for k in ("gdn_conv_wt", "gdn_a_log", "gdn_dt_bias", "gdn_norm_g")}
    per_mla = {k: [None] * 2 for k in ("mla_q_norm_g", "mla_kv_norm_g")}
    dmod = [None] * DEPTH
    dep = jnp.zeros((8, LANES), F32)

    for layer in reversed(range(DEPTH)):
        j = layer // 2
        tag = f"l{layer}"
        rec = saved[layer]
        lw = rec["lw"]
        shift_m, scale_m, gate_m, shift_f, scale_f, gate_f = [_row(mod[layer, i]) for i in range(N_MOD)]
        dy2, st_g = _gate_bwd(dx, rec["y2"], gate_f, dep, name=f"gate_bwd_ffn_{tag}")
        dgate_f = st_g[0]
        dw_down = _mm(rec["s"], dy2, mode="tn", out_dtype=BF16, tm=FFN_BLOCK, tn=1024, name=f"ffn_down_dw_{tag}")
        da2, db2 = _ffn_down_dx(dy2, lw["w_down"], rec["a2"], rec["b2"], name=f"ffn_down_dx_{tag}")
        dwt_g = _mm(da2, rec["h2"], mode="tn", out_dtype=BF16, tm=FFN_BLOCK, tn=1024, name=f"ffn_g_dw_{tag}")
        dwt_u = _mm(db2, rec["h2"], mode="tn", out_dtype=BF16, tm=FFN_BLOCK, tn=1024, name=f"ffn_u_dw_{tag}")
        dep = put_grads(layer, "ffn", {"wt_g": dwt_g, "wt_u": dwt_u, "w_down": dw_down})
        dh2 = _mm(da2, lw["wt_g"], mode="nn", out_dtype=F32, tm=512, tn=1024, name=f"ffn_g_dx_{tag}")
        dh2 = _mm(db2, lw["wt_u"], mode="nn", out_dtype=BF16, add=dh2, tm=512, tn=1024, name=f"ffn_u_dx_{tag}")
        dx, st_n = _adaln_bwd(rec["x1"], _row(rep["norm_ffn_g"][layer]), scale_f, shift_f, dh2, dx, dep,
                              name=f"adaln_ffn_bwd_{tag}")
        per_layer["norm_ffn_g"][layer] = st_n[0]
        dscale_f, dshift_f = st_n[1], st_n[2]
        dy, st_g = _gate_bwd(dx, rec["y"], gate_m, dep, name=f"gate_bwd_mix_{tag}")
        dgate_m = st_g[0]
        big = {}
        if layer % 2 == 0:
            big["w_out"] = _mm(rec["og"], dy, mode="tn", out_dtype=BF16, name=f"gdn_out_dw_{tag}")
            dog = _mm(dy, lw["w_out"], mode="nt", out_dtype=BF16, name=f"gdn_out_dx_{tag}")
            do, dgp, st_o = _gdn_onorm_bwd(rec["o"], rec["proj"], _row(rep["gdn_norm_g"][j]), dog,
                                           name=f"gdn_onorm_bwd_{tag}")
            per_gdn["gdn_norm_g"][j] = st_o[0]
            dqkv, dgc_, dbc_ = _gdn_chunk_bwd(rec["qkv"], rec["gbc"], rec["bbc"], rec["states"], rec["tinvs"], do,
                                               name=f"gdn_chunk_bwd_{tag}")
            dgb = jnp.concatenate([jnp.transpose(dgc_[:, :, 0]), jnp.transpose(dbc_[:, :, 0])], axis=1)
            dgb = jnp.pad(dgb, ((0, 0), (0, LANES - 2 * GDN_HEADS)))
            dab, st_a = _gdn_gate_bwd(rec["ab"], rep["gdn_gate_prm"][j], dgb, name=f"gdn_gate_bwd_{tag}")
            per_gdn["gdn_a_log"][j] = st_a[0, :GDN_HEADS]
            per_gdn["gdn_dt_bias"][j] = st_a[1, :GDN_HEADS]
            dpre, dcw = _gdn_prep_bwd(rec["proj"], rep["gdn_conv_wt"][j], dqkv, name=f"gdn_prep_bwd_{tag}")
            per_gdn["gdn_conv_wt"][j] = dcw
            dproj = jnp.concatenate([dpre, dgp], axis=1)
            dw_main = _mm(dproj, rec["h"], mode="tn", out_dtype=BF16, tm=512, tn=1024, name=f"gdn_in_dw_{tag}")
            dw_ab = _mm(dab, rec["h"], mode="tn", out_dtype=BF16, tn=1024, name=f"gdn_in_ab_dw_{tag}")
            big["wt_in"] = jnp.concatenate([dw_main, dw_ab[:2 * GDN_HEADS]], axis=0)
            dep = put_grads(layer, "gdn", big)
            dh_ab = _mm(dab, lw["wt_ab"], mode="nn", out_dtype=F32, tn=1024, name=f"gdn_in_ab_dx_{tag}")
            dh = _mm(dproj, lw["wt_in"], mode="nn", out_dtype=BF16, add=dh_ab, tm=256, tn=1024, b_rows=GDN_MAIN,
                     name=f"gdn_in_dx_{tag}")
        else:
            big["w_out"] = _mm(rec["oc"], dy, mode="tn", out_dtype=BF16, name=f"mla_out_dw_{tag}")
            doc = _mm(dy, lw["w_out"], mode="nt", out_dtype=BF16, name=f"mla_out_dx_{tag}")
            dqn, dqr, dkvf, dkr_parts = _attn_tm_bwd(rec["qf"], rec["qr"], rec["kvf"], rec["kr"], rec["oc"],
                                                     rec["lse"], doc, name=f"attn_bwd_{tag}")
            dqr_un, dkr_un = _rope_qk_bwd(dqr, dkr_parts, cos_t, sin_t, name=f"rope_bwd_{tag}")
            n_nope = MLA_HEADS * MLA_NOPE
            big["wt_uq"] = jnp.concatenate(
                [_mm(dqn, rec["cq"], mode="tn", out_dtype=BF16, name=f"mla_uq_dw_nope_{tag}"),
                 _mm(dqr_un, rec["cq"], mode="tn", out_dtype=BF16, name=f"mla_uq_dw_rope_{tag}")], axis=0)
            big["w_ukv"] = _mm(rec["ck"], dkvf, mode="tn", out_dtype=BF16, name=f"mla_ukv_dw_{tag}")
            dcq = _mm(dqr_un, lw["wt_uq"][n_nope:], mode="nn", out_dtype=F32, name=f"mla_uq_dx_rope_{tag}")
            dcq = _mm(dqn, lw["wt_uq"], mode="nn", out_dtype=F32, add=dcq, b_rows=n_nope,
                      name=f"mla_uq_dx_nope_{tag}")
            dck = _mm(dkvf, lw["w_ukv"], mode="nt", out_dtype=F32, name=f"mla_ukv_dx_{tag}")
            dproj, st_p = _mla_prep_bwd(rec["proj"], _row(rep["mla_q_norm_g"][j]), _row(rep["mla_kv_norm_g"][j]),
                                        dcq, dck, dkr_un, name=f"mla_prep_bwd_{tag}")
            per_mla["mla_q_norm_g"][j] = st_p[0, :MLA_Q_RANK]
            per_mla["mla_kv_norm_g"][j] = st_p[0, MLA_Q_RANK:MLA_Q_RANK + MLA_KV_RANK]
            big["w_in"] = _mm(rec["h"], dproj, mode="tn", out_dtype=BF16, name=f"mla_in_dw_{tag}")
            dep = put_grads(layer, "mla", big)
            dh = _mm(dproj, lw["w_in"], mode="nt", out_dtype=BF16, name=f"mla_in_dx_{tag}")
        dx, st_n = _adaln_bwd(rec["x0"], _row(rep["norm_mix_g"][layer]), scale_m, shift_m, dh, dx, dep,
                              name=f"adaln_mix_bwd_{tag}")
        per_layer["norm_mix_g"][layer] = st_n[0]
        dmod[layer] = jnp.stack([st_n[2], st_n[1], dgate_m, dshift_f, dscale_f, dgate_f])

    for d in (per_layer, per_gdn, per_mla):
        for k, v in d.items():
            grads[k] = jnp.stack(v)
    return loss, dx, jnp.stack(dmod), grads


BIG = ("gdn_w_in", "gdn_w_out", "mla_w_in", "mla_w_uq", "mla_w_ukv", "mla_w_out", "ffn_w_gate", "ffn_w_up",
       "ffn_w_down")
TRANSPOSED = ("gdn_w_in", "mla_w_uq", "ffn_w_gate", "ffn_w_up")
AHEAD = 2


def _view(k, a):
    return jnp.transpose(a, (0, 2, 1)) if k in TRANSPOSED else a
SMALL = ("ada_b", "norm_mix_g", "norm_ffn_g", "gdn_conv_w", "gdn_a_log", "gdn_dt_bias", "gdn_norm_g",
         "mla_q_norm_g", "mla_kv_norm_g", "final_norm_g")
WEIGHTS = ("ada_w", "ada_b", "norm_mix_g", "norm_ffn_g", "gdn_w_in", "gdn_conv_w", "gdn_a_log", "gdn_dt_bias",
           "gdn_norm_g", "gdn_w_out", "mla_w_in", "mla_q_norm_g", "mla_kv_norm_g", "mla_w_uq", "mla_w_ukv",
           "mla_w_out", "ffn_w_gate", "ffn_w_up", "ffn_w_down", "final_norm_g")


def _uq_to_kernel_layout(w, axis=-1):
    axis = axis % w.ndim
    lead, tail = w.shape[:axis], w.shape[axis + 1:]
    w4 = w.reshape(lead + (MLA_HEADS, MLA_QK) + tail)
    nope = lax.slice_in_dim(w4, 0, MLA_NOPE, axis=axis + 1).reshape(lead + (-1,) + tail)
    rope = lax.slice_in_dim(w4, MLA_NOPE, MLA_QK, axis=axis + 1).reshape(lead + (-1,) + tail)
    return jnp.concatenate([nope, rope], axis=axis)


def _uq_from_kernel_layout(w, axis=-1):
    axis = axis % w.ndim
    lead, tail = w.shape[:axis], w.shape[axis + 1:]
    nope = lax.slice_in_dim(w, 0, MLA_HEADS * MLA_NOPE, axis=axis).reshape(lead + (MLA_HEADS, MLA_NOPE) + tail)
    rope = lax.slice_in_dim(w, MLA_HEADS * MLA_NOPE, MLA_HEADS * MLA_QK, axis=axis).reshape(
        lead + (MLA_HEADS, MLA_ROPE) + tail)
    return jnp.concatenate([nope, rope], axis=axis + 1).reshape(lead + (-1,) + tail)


def _group_names(layer, kind):
    if kind == "ffn":
        return ("ffn_w_gate", "ffn_w_up", "ffn_w_down")
    return ("gdn_w_in", "gdn_w_out") if layer % 2 == 0 else ("mla_w_in", "mla_w_uq", "mla_w_ukv", "mla_w_out")


def _layer_index(name, layer):
    return layer if name.startswith("ffn") else layer // 2


def _cols(g):
    return jnp.transpose(g, (1, 0, 2)).reshape(g.shape[1], N_DEV * g.shape[2])


def _rows(g):
    return g.reshape(N_DEV * g.shape[1], g.shape[2])


def _uncols(full):
    r, c = full.shape
    return jnp.transpose(full.reshape(r, N_DEV, c // N_DEV), (1, 0, 2))


def _unrows(full):
    r, c = full.shape
    return full.reshape(N_DEV, r // N_DEV, c)


def _group_weights(layer, kind, got, token):
    if kind == "ffn":
        return {"wt_g": _rows(got["ffn_w_gate"]), "wt_u": _rows(got["ffn_w_up"]), "w_down": _rows(got["ffn_w_down"]),
                "dep_ffn": token}
    if layer % 2 == 0:
        wt_in = _rows(got["gdn_w_in"])
        return dict(wt_in=wt_in, wt_ab=jnp.pad(wt_in[GDN_MAIN:], ((0, LANES - 2 * GDN_HEADS), (0, 0))),
                    w_out=_rows(got["gdn_w_out"]), dep_mix=token)
    return dict(w_in=_rows(got["mla_w_in"]), wt_uq=_uq_to_kernel_layout(_rows(got["mla_w_uq"]), axis=0),
                w_ukv=_cols(got["mla_w_ukv"]), w_out=_rows(got["mla_w_out"]), dep_mix=token)


def _layer_grad_slots(kind, big):
    if kind == "ffn":
        return {"ffn_w_gate": _unrows(big["wt_g"]), "ffn_w_up": _unrows(big["wt_u"]),
                "ffn_w_down": _unrows(big["w_down"])}
    if kind == "gdn":
        return {"gdn_w_in": _unrows(big["wt_in"]), "gdn_w_out": _unrows(big["w_out"])}
    return {"mla_w_in": _unrows(big["w_in"]), "mla_w_uq": _unrows(_uq_from_kernel_layout(big["wt_uq"], axis=0)),
            "mla_w_ukv": _uncols(big["w_ukv"]), "mla_w_out": _unrows(big["w_out"])}


def _small_weights(tiny, rep):
    prm = jnp.zeros((2, 8, LANES), F32)
    prm = prm.at[:, 0, :GDN_HEADS].set(rep["gdn_a_log"]).at[:, 1, :GDN_HEADS].set(rep["gdn_dt_bias"])
    out = {
        "gdn_conv_wt": jnp.transpose(_gather_rows(tiny["gdn_conv_w"]), (0, 2, 1)),
        "mla_q_norm_g": jnp.transpose(tiny["mla_q_norm_g"], (1, 0, 2)).reshape(2, MLA_Q_RANK),
        "mla_kv_norm_g": jnp.transpose(tiny["mla_kv_norm_g"], (1, 0, 2)).reshape(2, MLA_KV_RANK),
        "gdn_gate_prm": prm,
    }
    for k in ("norm_mix_g", "norm_ffn_g", "gdn_norm_g", "final_norm_g"):
        out[k] = rep[k]
    return out


def _rope_tables(positions):
    inv_freq = ROPE_THETA ** (-jnp.arange(0, MLA_ROPE, 2, dtype=F32) / MLA_ROPE)
    ang = positions.astype(F32)[:, None] * inv_freq
    cos, sin = jnp.cos(ang), jnp.sin(ang)
    reps = LANES // MLA_ROPE
    return jnp.tile(jnp.concatenate([cos, cos], axis=1), (1, reps)), jnp.tile(
        jnp.concatenate([-sin, sin], axis=1), (1, reps))


def kernel(x, c, positions, ada_w, ada_b, norm_mix_g, norm_ffn_g, gdn_w_in, gdn_conv_w, gdn_a_log, gdn_dt_bias, gdn_norm_g, gdn_w_out, mla_w_in, mla_q_norm_g, mla_kv_norm_g, mla_w_uq, mla_w_ukv, mla_w_out, ffn_w_gate, ffn_w_up, ffn_w_down, final_norm_g, loss_target, m_ada_w, m_ada_b, m_norm_mix_g, m_norm_ffn_g, m_gdn_w_in, m_gdn_conv_w, m_gdn_a_log, m_gdn_dt_bias, m_gdn_norm_g, m_gdn_w_out, m_mla_w_in, m_mla_q_norm_g, m_mla_kv_norm_g, m_mla_w_uq, m_mla_w_ukv, m_mla_w_out, m_ffn_w_gate, m_ffn_w_up, m_ffn_w_down, m_final_norm_g, v_ada_w, v_ada_b, v_norm_mix_g, v_norm_ffn_g, v_gdn_w_in, v_gdn_conv_w, v_gdn_a_log, v_gdn_dt_bias, v_gdn_norm_g, v_gdn_w_out, v_mla_w_in, v_mla_q_norm_g, v_mla_kv_norm_g, v_mla_w_uq, v_mla_w_ukv, v_mla_w_out, v_ffn_w_gate, v_ffn_w_up, v_ffn_w_down, v_final_norm_g):
    W = dict(ada_w=ada_w, ada_b=ada_b, norm_mix_g=norm_mix_g, norm_ffn_g=norm_ffn_g, gdn_w_in=gdn_w_in,
             gdn_conv_w=gdn_conv_w, gdn_a_log=gdn_a_log, gdn_dt_bias=gdn_dt_bias, gdn_norm_g=gdn_norm_g,
             gdn_w_out=gdn_w_out, mla_w_in=mla_w_in, mla_q_norm_g=mla_q_norm_g, mla_kv_norm_g=mla_kv_norm_g,
             mla_w_uq=mla_w_uq, mla_w_ukv=mla_w_ukv, mla_w_out=mla_w_out, ffn_w_gate=ffn_w_gate,
             ffn_w_up=ffn_w_up, ffn_w_down=ffn_w_down, final_norm_g=final_norm_g)
    M = dict(ada_w=m_ada_w, ada_b=m_ada_b, norm_mix_g=m_norm_mix_g, norm_ffn_g=m_norm_ffn_g, gdn_w_in=m_gdn_w_in,
             gdn_conv_w=m_gdn_conv_w, gdn_a_log=m_gdn_a_log, gdn_dt_bias=m_gdn_dt_bias, gdn_norm_g=m_gdn_norm_g,
             gdn_w_out=m_gdn_w_out, mla_w_in=m_mla_w_in, mla_q_norm_g=m_mla_q_norm_g,
             mla_kv_norm_g=m_mla_kv_norm_g, mla_w_uq=m_mla_w_uq, mla_w_ukv=m_mla_w_ukv, mla_w_out=m_mla_w_out,
             ffn_w_gate=m_ffn_w_gate, ffn_w_up=m_ffn_w_up, ffn_w_down=m_ffn_w_down, final_norm_g=m_final_norm_g)
    V = dict(ada_w=v_ada_w, ada_b=v_ada_b, norm_mix_g=v_norm_mix_g, norm_ffn_g=v_norm_ffn_g, gdn_w_in=v_gdn_w_in,
             gdn_conv_w=v_gdn_conv_w, gdn_a_log=v_gdn_a_log, gdn_dt_bias=v_gdn_dt_bias, gdn_norm_g=v_gdn_norm_g,
             gdn_w_out=v_gdn_w_out, mla_w_in=v_mla_w_in, mla_q_norm_g=v_mla_q_norm_g,
             mla_kv_norm_g=v_mla_kv_norm_g, mla_w_uq=v_mla_w_uq, mla_w_ukv=v_mla_w_ukv, mla_w_out=v_mla_w_out,
             ffn_w_gate=v_ffn_w_gate, ffn_w_up=v_ffn_w_up, ffn_w_down=v_ffn_w_down, final_norm_g=v_final_norm_g)
    me = 4 * lax.axis_index("x") + 2 * lax.axis_index("y") + lax.axis_index("c")
    t = x.shape[1]
    wc = ada_w.shape[-1]

    groups = [(layer, kind) for layer in range(DEPTH) for kind in ("mix", "ffn")]

    def group_srcs(i):
        layer, kind = groups[i]
        return [_view(k, W[k])[_layer_index(k, layer)].astype(BF16) for k in _group_names(layer, kind)]

    tiny_shapes = [c.shape, gdn_conv_w.shape, mla_q_norm_g.shape, mla_kv_norm_g.shape]
    first = _gather_two_level([_pack([c, gdn_conv_w, mla_q_norm_g, mla_kv_norm_g])] + group_srcs(0),
                              name="gather_first")
    tiny_g = first[0]
    c_g, conv_g, qn_g, kvn_g = _unpack(tiny_g, tiny_shapes, lead=(N_DEV,))
    c_all = c_g.reshape(N_DEV, D_MODEL)
    rep = _small_weights({"gdn_conv_w": conv_g, "mla_q_norm_g": qn_g, "mla_kv_norm_g": kvn_g}, W)

    def start_group(i, dep):
        layer, kind = groups[i]
        return _exchange_start(group_srcs(i), scatter=False, name=f"gather_start_{kind}_l{layer}", dep=dep)


    b_cols = lax.dynamic_slice_in_dim(ada_b, me * wc, wc, axis=1).reshape(DEPTH, 1, wc)
    mod_part = _ada_mod(c_all, ada_w, b_cols, name="ada_mod")
    (mod_g,) = _exchange([mod_part], scatter=False, name="gather_mod")
    mod_mine = lax.dynamic_index_in_dim(mod_g, me, axis=2, keepdims=False)
    mod = jnp.transpose(mod_mine, (1, 0, 2)).reshape(DEPTH, N_MOD, D_MODEL)
    gather = {1: start_group(1, mod_g)}
    for i in range(2, AHEAD + 1):
        gather[i] = start_group(i, gather[i - 1][4])

    def get_weights(layer, kind, after):
        i = groups.index((layer, kind))
        names = _group_names(layer, kind)
        if i == 0:
            return _group_weights(layer, kind, dict(zip(names, first[1:])), gather[AHEAD][4])
        srcs, lands = _exchange_wait(gather[i], after, scatter=False, name=f"gather_wait_{kind}_l{layer}")
        token = jnp.zeros((8, LANES), F32)
        if i + AHEAD < len(groups):
            gather[i + AHEAD] = start_group(i + AHEAD, lands[0])
            token = gather[i + AHEAD][4]
        got = {k: lax.dynamic_update_index_in_dim(z, s, me, 0) for k, s, z in zip(names, srcs, lands)}
        return _group_weights(layer, kind, got, token)

    scatter = []

    def put_grads(layer, kind, big):
        slots = _layer_grad_slots(kind, big)
        started = _exchange_start(list(slots.values()), scatter=True, name=f"scatter_start_{kind}_l{layer}")
        scatter.append((layer, kind, list(slots.keys()), started))
        return started[4]

    cos_t, sin_t = _rope_tables(positions[0])
    loss, dx, dmod, g = _local_step(x[0], loss_target[0], mod, cos_t, sin_t, rep, get_weights, put_grads)

    parts = {k: [None] * W[k].shape[0] for k in BIG}
    res = {}

    def wait_group(entry, after):
        layer, kind, names, started = entry
        srcs, lands = _exchange_wait(started, after, scatter=True, name=f"scatter_wait_{kind}_l{layer}")
        for k, s, z in zip(names, srcs, lands):
            own = lax.dynamic_index_in_dim(s, me, 0, keepdims=False)
            parts[k][_layer_index(k, layer)] = lax.dynamic_update_index_in_dim(z, own, me, 0)

    for entry in scatter[:-1]:
        wait_group(entry, dx)
    early = [k for k in BIG if k not in scatter[-1][2]]
    def update(k):
        outs = _adamw(parts[k], _view(k, W[k]), _view(k, M[k]), _view(k, V[k]), name=f"adamw_{k}")
        return tuple(_view(k, o) for o in outs)

    for k in early:
        res[k] = update(k)
    loss, dmod, done = lax.optimization_barrier((loss, dmod, [res[k] for k in early]))
    for k, r in zip(early, done):
        res[k] = r

    small_local = [dmod.reshape(DEPTH, N_MOD * D_MODEL), g["norm_mix_g"], g["norm_ffn_g"],
                   jnp.transpose(g["gdn_conv_wt"], (0, 2, 1)), g["gdn_a_log"], g["gdn_dt_bias"], g["gdn_norm_g"],
                   g["mla_q_norm_g"], g["mla_kv_norm_g"], g["final_norm_g"], loss.reshape(1)]
    small_shapes = [a.shape for a in small_local]
    (small_g,) = _exchange([_pack(small_local)], scatter=False, name="gather_small_grads")
    small_sum = _unpack(_sum_parts(small_g, name="sum_small_grads"), small_shapes)
    loss = small_sum[-1][0]
    dmod_all = _unpack(small_g, small_shapes[:1], lead=(N_DEV,))[0]
    sg = dict(zip(SMALL, small_sum))
    wait_group(scatter[-1], small_g)
    sg["gdn_conv_w"] = lax.dynamic_slice_in_dim(sg["gdn_conv_w"], me * gdn_conv_w.shape[1], gdn_conv_w.shape[1], 1)
    sg["mla_q_norm_g"] = lax.dynamic_slice_in_dim(sg["mla_q_norm_g"], me * mla_q_norm_g.shape[1],
                                                  mla_q_norm_g.shape[1], 1)
    sg["mla_kv_norm_g"] = lax.dynamic_slice_in_dim(sg["mla_kv_norm_g"], me * mla_kv_norm_g.shape[1],
                                                   mla_kv_norm_g.shape[1], 1)

    dmod_cols = jnp.transpose(lax.dynamic_slice_in_dim(dmod_all, me * wc, wc, axis=2), (1, 0, 2))
    res["ada_w"] = _ada_grad_adamw(c_all, dmod_cols, ada_w, m_ada_w, v_ada_w, name="ada_w_grad_adamw")
    for k in BIG:
        if k not in early:
            res[k] = update(k)
    shapes = [W[k].shape for k in SMALL]
    packed = [_pack([d[k] for k in SMALL]) for d in (sg, W, M, V)]
    outs = _adamw([packed[0][None]], packed[1][None], packed[2][None], packed[3][None], name="adamw_small")
    unpacked = [_unpack(o[0], shapes) for o in outs]
    for i, k in enumerate(SMALL):
        res[k] = tuple(u[i] for u in unpacked)

    return (loss, dx[None], *[res[k][0] for k in WEIGHTS], *[res[k][1] for k in WEIGHTS],
            *[res[k][2] for k in WEIGHTS], *[res[k][3] for k in WEIGHTS])
```

```python
import math

import jax
import jax.numpy as jnp
from jax import lax
from jax.experimental import pallas as pl
from jax.experimental.pallas import tpu as pltpu

F32 = jnp.float32
BF16 = jnp.bfloat16
MXU_DTYPE = jnp.bfloat16

N_DEV = 8
D_MODEL = 1024
DEPTH = 4
GDN_HEADS = 8
GDN_HEAD_DIM = 128
GDN_KEY_DIM = GDN_HEADS * GDN_HEAD_DIM
GDN_CHUNK = 64
GDN_HEAD_BATCH = 8
GDN_CONV = 4
GDN_PREP_HEADS = 2
GDN_MAIN = 4 * GDN_KEY_DIM
MLA_HEADS = 8
MLA_NOPE = 128
MLA_ROPE = 64
MLA_V = 128
MLA_Q_RANK = 384
MLA_KV_RANK = 256
MLA_IN = MLA_Q_RANK + MLA_KV_RANK + MLA_ROPE
MLA_QK = MLA_NOPE + MLA_ROPE
ROPE_THETA = 10000.0
D_FF = 2816
N_MOD = 6
EPS = 1e-6
LANES = 128
VMEM_LIMIT = 48 * 1024 * 1024

ADAM_LR = 0.001
ADAM_B1 = 0.9
ADAM_B2 = 0.999
ADAM_EPS = 1e-08
ADAM_WD = 0.01
ADAM_STEP = 10
ADAM_BC1 = 1.0 - ADAM_B1 ** ADAM_STEP
ADAM_BC2 = 1.0 - ADAM_B2 ** ADAM_STEP

NN = (((1,), (0,)), ((), ()))
NT = (((1,), (1,)), ((), ()))
TN = (((0,), (0,)), ((), ()))
NEG = -1e30


def _dotb(a, b, dims):
    return lax.dot_general(a.astype(MXU_DTYPE), b.astype(MXU_DTYPE), dims, preferred_element_type=F32)


def _split(a):
    hi = a.astype(BF16)
    return hi, (a - hi.astype(F32)).astype(BF16)


def _dotf(a, b, dims):
    ah, al = _split(a)
    bh, bl = _split(b)
    dot = lambda u, v: lax.dot_general(u, v, dims, preferred_element_type=F32)
    return dot(ah, bh) + (dot(ah, bl) + dot(al, bh))


def _params(*sem):
    return pltpu.CompilerParams(dimension_semantics=sem, vmem_limit_bytes=VMEM_LIMIT)


def _pick(n, pref, mult=LANES):
    best = None
    t = mult
    while t <= min(n, pref):
        if n % t == 0:
            best = t
        t += mult
    return best if best is not None else n


def _sigmoid(z):
    return 1.0 / (1.0 + jnp.exp(-z))


def _exchange(arrays, *, scatter, name):
    n = len(arrays)
    out_shape = tuple(
        jax.ShapeDtypeStruct(a.shape if scatter else (N_DEV,) + a.shape, a.dtype) for a in arrays)

    def body(*refs):
        ins, outs = refs[:n], refs[n:2 * n]
        send_sems, recv_sems, local_sems = refs[2 * n:]
        x, y, c = lax.axis_index("x"), lax.axis_index("y"), lax.axis_index("c")
        me = 4 * x + 2 * y + c
        copies = []
        for k in range(n):
            src_own = ins[k].at[me] if scatter else ins[k]
            own = pltpu.make_async_copy(src_own, outs[k].at[me], local_sems.at[k])
            own.start()
            copies.append(own)
        sends = []
        for p in range(1, N_DEV):
            px, py, pc = x ^ ((p >> 2) & 1), y ^ ((p >> 1) & 1), c ^ (p & 1)
            peer = 4 * px + 2 * py + pc
            for k in range(n):
                cp = pltpu.make_async_remote_copy(
                    src_ref=ins[k].at[peer] if scatter else ins[k],
                    dst_ref=outs[k].at[me],
                    send_sem=send_sems.at[k, p - 1],
                    recv_sem=recv_sems.at[k, p - 1],
                    device_id=(px, py, pc),
                    device_id_type=pl.DeviceIdType.MESH,
                )
                cp.start()
                sends.append((cp, k, peer, p))
        for cp, k, peer, p in sends:
            pltpu.make_async_remote_copy(
                src_ref=ins[k].at[peer] if scatter else ins[k],
                dst_ref=outs[k].at[peer],
                send_sem=send_sems.at[k, p - 1],
                recv_sem=recv_sems.at[k, p - 1],
                device_id=(x, y, c),
                device_id_type=pl.DeviceIdType.MESH,
            ).wait_recv()
        for cp, _, _, _ in sends:
            cp.wait_send()
        for own in copies:
            own.wait()

    any_spec = pl.BlockSpec(memory_space=pl.ANY)
    outs = pl.pallas_call(
        body,
        name=name,
        out_shape=out_shape,
        in_specs=[any_spec] * n,
        out_specs=tuple([any_spec] * n),
        scratch_shapes=[
            pltpu.SemaphoreType.DMA((n, N_DEV - 1)),
            pltpu.SemaphoreType.DMA((n, N_DEV - 1)),
            pltpu.SemaphoreType.DMA((n,)),
        ],
        compiler_params=pltpu.CompilerParams(has_side_effects=True),
    )(*arrays)
    return list(outs)


def _gather_two_level(arrays, *, name):
    n = len(arrays)
    out_shape = tuple(jax.ShapeDtypeStruct((N_DEV,) + a.shape, a.dtype) for a in arrays)

    def body(*refs):
        ins, outs = refs[:n], refs[n:2 * n]
        send_sems, recv_sems, local_sems = refs[2 * n:]
        x, y, c = lax.axis_index("x"), lax.axis_index("y"), lax.axis_index("c")
        me = 4 * x + 2 * y + c
        sibling = (x, y, 1 - c)
        chips = [(1 - x, y), (x, 1 - y), (1 - x, 1 - y)]

        def slot(px, py, pc):
            return 4 * px + 2 * py + pc

        def copy(k, q, block, to, src=None):
            return pltpu.make_async_remote_copy(
                src_ref=outs[k].at[slot(*block)] if src is None else src,
                dst_ref=outs[k].at[slot(*block)],
                send_sem=send_sems.at[k, q], recv_sem=recv_sems.at[k, q],
                device_id=to, device_id_type=pl.DeviceIdType.MESH)

        own = [pltpu.make_async_copy(ins[k], outs[k].at[me], local_sems.at[k]) for k in range(n)]
        for cp in own:
            cp.start()
        first = []
        for k in range(n):
            first.append(copy(k, 0, (x, y, c), sibling, src=ins[k]))
            first += [copy(k, 1 + j, (x, y, c), (*chip, c), src=ins[k]) for j, chip in enumerate(chips)]
        for cp in first:
            cp.start()
        passed = []
        for j, chip in enumerate(chips):
            for k in range(n):
                copy(k, 1 + j, (*chip, c), (x, y, c)).wait_recv()
                fwd = copy(k, 4 + j, (*chip, c), sibling)
                fwd.start()
                passed.append(fwd)
        for k in range(n):
            copy(k, 0, sibling, (x, y, c)).wait_recv()
            for j, chip in enumerate(chips):
                copy(k, 4 + j, (*chip, 1 - c), (x, y, c)).wait_recv()
        for cp in first + passed:
            cp.wait_send()
        for cp in own:
            cp.wait()

    any_spec = pl.BlockSpec(memory_space=pl.ANY)
    outs = pl.pallas_call(
        body, name=name, out_shape=out_shape, in_specs=[any_spec] * n, out_specs=tuple([any_spec] * n),
        scratch_shapes=[pltpu.SemaphoreType.DMA((n, N_DEV - 1)), pltpu.SemaphoreType.DMA((n, N_DEV - 1)),
                        pltpu.SemaphoreType.DMA((n,))],
        compiler_params=pltpu.CompilerParams(has_side_effects=True),
    )(*arrays)
    return list(outs)


def _peer(x, y, c, p):
    return x ^ ((p >> 2) & 1), y ^ ((p >> 1) & 1), c ^ (p & 1)


def _exchange_start(arrays, *, scatter, name, dep=None):
    n = len(arrays)
    deps = [] if dep is None else [dep]
    lands = [lax.empty(a.shape if scatter else (N_DEV,) + a.shape, a.dtype) for a in arrays]

    def body(*refs):
        ins, zones = refs[:n], refs[n:2 * n]
        send_sems, recv_sems = refs[2 * n + len(deps)], refs[2 * n + len(deps) + 1]
        token = refs[-1]
        x, y, c = lax.axis_index("x"), lax.axis_index("y"), lax.axis_index("c")
        me = 4 * x + 2 * y + c
        for p in range(1, N_DEV):
            px, py, pc = _peer(x, y, c, p)
            for k in range(n):
                pltpu.make_async_remote_copy(
                    src_ref=ins[k].at[4 * px + 2 * py + pc] if scatter else ins[k],
                    dst_ref=zones[k].at[me],
                    send_sem=send_sems.at[k * (N_DEV - 1) + p - 1],
                    recv_sem=recv_sems.at[k * (N_DEV - 1) + p - 1],
                    device_id=(px, py, pc),
                    device_id_type=pl.DeviceIdType.MESH,
                ).start()
        token[...] = jnp.zeros_like(token)

    hbm = pl.BlockSpec(memory_space=pltpu.HBM)
    sem = pl.BlockSpec(memory_space=pltpu.SEMAPHORE)
    outs = pl.pallas_call(
        body,
        name=name,
        out_shape=(pltpu.SemaphoreType.DMA((n * (N_DEV - 1),)), pltpu.SemaphoreType.DMA((n * (N_DEV - 1),)),
                   *[pltpu.HBM(a.shape, a.dtype) for a in arrays], *[pltpu.HBM(z.shape, z.dtype) for z in lands],
                   jax.ShapeDtypeStruct((8, LANES), F32)),
        in_specs=[hbm] * (2 * n) + [pl.BlockSpec(memory_space=pl.ANY)] * len(deps),
        out_specs=(sem, sem, *[hbm] * (2 * n), pl.BlockSpec(memory_space=pltpu.VMEM)),
        input_output_aliases={k: 2 + k for k in range(2 * n)},
        compiler_params=pltpu.CompilerParams(has_side_effects=pltpu.SideEffectType.DATAFLOW_SIDE_EFFECTING),
    )(*[pltpu.with_memory_space_constraint(a, pltpu.HBM) for a in arrays],
      *[pltpu.with_memory_space_constraint(z, pltpu.HBM) for z in lands], *deps)
    return outs[0], outs[1], list(outs[2:2 + n]), list(outs[2 + n:2 + 2 * n]), outs[-1]


def _exchange_wait(started, after, *, scatter, name):
    send_sems, recv_sems, srcs, lands, _ = started
    n = len(srcs)

    def body(*refs):
        ins, zones = refs[:n], refs[n:2 * n]
        s_sems, r_sems = refs[2 * n], refs[2 * n + 1]
        x, y, c = lax.axis_index("x"), lax.axis_index("y"), lax.axis_index("c")
        for p in range(1, N_DEV):
            px, py, pc = _peer(x, y, c, p)
            peer = 4 * px + 2 * py + pc
            for k in range(n):
                cp = pltpu.make_async_remote_copy(
                    src_ref=ins[k].at[peer] if scatter else ins[k],
                    dst_ref=zones[k].at[peer],
                    send_sem=s_sems.at[k * (N_DEV - 1) + p - 1],
                    recv_sem=r_sems.at[k * (N_DEV - 1) + p - 1],
                    device_id=(px, py, pc),
                    device_id_type=pl.DeviceIdType.MESH,
                )
                cp.wait_send()
                cp.wait_recv()

    hbm = pl.BlockSpec(memory_space=pltpu.HBM)
    sem = pl.BlockSpec(memory_space=pltpu.SEMAPHORE)
    outs = pl.pallas_call(
        body,
        name=name,
        out_shape=tuple(pltpu.HBM(a.shape, a.dtype) for a in srcs + lands),
        in_specs=[hbm] * (2 * n) + [sem, sem, pl.BlockSpec(memory_space=pl.ANY)],
        out_specs=tuple([hbm] * (2 * n)),
        input_output_aliases={k: k for k in range(2 * n)},
        compiler_params=pltpu.CompilerParams(has_side_effects=pltpu.SideEffectType.DATAFLOW_SIDE_EFFECTING),
    )(*srcs, *lands, send_sems, recv_sems, after)
    return list(outs[:n]), list(outs[n:])


def _mm(a, b, *, mode, out_dtype, name, add=None, tm=512, tn=512, b_rows=None, dep=None):
    rows_b = b.shape[0] if b_rows is None else b_rows
    if mode == "nn":
        (m, kd), nd = a.shape, b.shape[1]
        assert kd == rows_b
    elif mode == "nt":
        (m, kd), nd = a.shape, rows_b
    else:
        (kd, m), nd = a.shape, b.shape[1]
    tm = _pick(m, tm, LANES if mode == "tn" else 16)
    tn = _pick(nd, tn)
    dims = {"nn": NN, "nt": NT, "tn": TN}[mode]
    ni, nj = m // tm, nd // tn
    a_bytes, b_bytes = a.size * a.dtype.itemsize, b.size * b.dtype.itemsize
    i_outer = a_bytes + ni * b_bytes <= b_bytes + nj * a_bytes
    ij = (lambda g0, g1: (g0, g1)) if i_outer else (lambda g0, g1: (g1, g0))
    a_spec = (pl.BlockSpec((kd, tm), lambda g0, g1: (0, ij(g0, g1)[0])) if mode == "tn"
              else pl.BlockSpec((tm, kd), lambda g0, g1: (ij(g0, g1)[0], 0)))
    b_spec = (pl.BlockSpec((tn, kd), lambda g0, g1: (ij(g0, g1)[1], 0)) if mode == "nt"
              else pl.BlockSpec((kd, tn), lambda g0, g1: (0, ij(g0, g1)[1])))
    o_spec = pl.BlockSpec((tm, tn), lambda g0, g1: ij(g0, g1))
    has_add = add is not None

    def body(*refs):
        a_ref, b_ref = refs[0], refs[1]
        o_ref = refs[-1]
        acc = _dotb(a_ref[...], b_ref[...], dims)
        if has_add:
            acc = acc + refs[2][...].astype(F32)
        o_ref[...] = acc.astype(o_ref.dtype)

    ins = [a, b] + ([add] if has_add else []) + ([] if dep is None else [dep])
    specs = ([a_spec, b_spec] + ([o_spec] if has_add else [])
             + ([] if dep is None else [pl.BlockSpec((8, LANES), lambda g0, g1: (0, 0))]))
    return pl.pallas_call(
        body, name=name, grid=(ni, nj) if i_outer else (nj, ni), in_specs=specs, out_specs=o_spec,
        out_shape=jax.ShapeDtypeStruct((m, nd), out_dtype),
        compiler_params=_params("parallel", "parallel"),
    )(*ins)


def _mm_resid(a, b, x, gate, *, name, tm=256, tn=1024):
    m, kd = a.shape
    nd = b.shape[1]
    tm = _pick(m, tm, 16)
    tn = _pick(nd, tn)
    o_spec = pl.BlockSpec((tm, tn), lambda i, j: (i, j))

    def body(a_ref, b_ref, x_ref, g_ref, xo_ref, y_ref):
        y = _dotb(a_ref[...], b_ref[...], NN)
        y_ref[...] = y
        xo_ref[...] = x_ref[...] + g_ref[...] * y

    return pl.pallas_call(
        body, name=name, grid=(m // tm, nd // tn),
        in_specs=[pl.BlockSpec((tm, kd), lambda i, j: (i, 0)), pl.BlockSpec((kd, tn), lambda i, j: (0, j)),
                  o_spec, pl.BlockSpec((1, tn), lambda i, j: (0, j))],
        out_specs=(o_spec, o_spec),
        out_shape=(jax.ShapeDtypeStruct((m, nd), F32), jax.ShapeDtypeStruct((m, nd), F32)),
        compiler_params=_params("parallel", "parallel"),
    )(a, b, x, gate)


ROWS = 256


def _row_spec(width, rows=ROWS):
    return pl.BlockSpec((rows, width), lambda i: (i, 0))


def _const_spec(shape):
    return pl.BlockSpec(shape, lambda i: tuple(0 for _ in shape))


def _adaln_fwd(x, g, scale, shift, *, name):
    t, d = x.shape

    def body(x_ref, g_ref, sc_ref, sh_ref, h_ref):
        xv = x_ref[...]
        r = lax.rsqrt(jnp.mean(xv * xv, axis=-1, keepdims=True) + EPS)
        h_ref[...] = (xv * r * g_ref[...] * (1.0 + sc_ref[...]) + sh_ref[...]).astype(h_ref.dtype)

    return pl.pallas_call(
        body, name=name, grid=(t // ROWS,),
        in_specs=[_row_spec(d), _const_spec((1, d)), _const_spec((1, d)), _const_spec((1, d))],
        out_specs=_row_spec(d), out_shape=jax.ShapeDtypeStruct((t, d), BF16),
        compiler_params=_params("parallel"),
    )(x, g, scale, shift)


def _adaln_bwd(x, g, scale, shift, dh, dres, dep, *, name):
    t, d = x.shape

    def body(x_ref, g_ref, sc_ref, sh_ref, dh_ref, dr_ref, dep_ref, dx_ref, st_ref):
        @pl.when(pl.program_id(0) == 0)
        def _():
            st_ref[...] = jnp.zeros_like(st_ref)

        xv = x_ref[...]
        dhv = dh_ref[...].astype(F32)
        gv = g_ref[...]
        r = lax.rsqrt(jnp.mean(xv * xv, axis=-1, keepdims=True) + EPS)
        xh = xv * r
        nv = xh * gv
        dn = dhv * (1.0 + sc_ref[...])
        dxh = dn * gv
        dx_ref[...] = dr_ref[...] + r * (dxh - xh * jnp.mean(dxh * xh, axis=-1, keepdims=True))
        st_ref[0:1, :] += jnp.sum(dn * xh, axis=0, keepdims=True)
        st_ref[1:2, :] += jnp.sum(dhv * nv, axis=0, keepdims=True)
        st_ref[2:3, :] += jnp.sum(dhv, axis=0, keepdims=True)

    return pl.pallas_call(
        body, name=name, grid=(t // ROWS,),
        in_specs=[_row_spec(d), _const_spec((1, d)), _const_spec((1, d)), _const_spec((1, d)),
                  _row_spec(d), _row_spec(d), _const_spec((8, LANES))],
        out_specs=(_row_spec(d), _const_spec((8, d))),
        out_shape=(jax.ShapeDtypeStruct((t, d), F32), jax.ShapeDtypeStruct((8, d), F32)),
        compiler_params=_params("arbitrary"),
    )(x, g, scale, shift, dh, dres, dep)


def _gate_bwd(dxo, y, gate, dep, *, name):
    t, d = dxo.shape

    def body(dx_ref, y_ref, g_ref, dep_ref, dy_ref, st_ref):
        @pl.when(pl.program_id(0) == 0)
        def _():
            st_ref[...] = jnp.zeros_like(st_ref)

        dxv = dx_ref[...]
        dy_ref[...] = (dxv * g_ref[...]).astype(dy_ref.dtype)
        st_ref[0:1, :] += jnp.sum(dxv * y_ref[...], axis=0, keepdims=True)

    return pl.pallas_call(
        body, name=name, grid=(t // ROWS,),
        in_specs=[_row_spec(d), _row_spec(d), _const_spec((1, d)), _const_spec((8, LANES))],
        out_specs=(_row_spec(d), _const_spec((8, d))),
        out_shape=(jax.ShapeDtypeStruct((t, d), BF16), jax.ShapeDtypeStruct((8, d), F32)),
        compiler_params=_params("arbitrary"),
    )(dxo, y, gate, dep)


def _loss_head(x, g, target, *, name):
    t, d = x.shape

    def body(x_ref, g_ref, t_ref, dx_ref, st_ref, ls_ref):
        @pl.when(pl.program_id(0) == 0)
        def _():
            st_ref[...] = jnp.zeros_like(st_ref)
            ls_ref[...] = jnp.zeros_like(ls_ref)

        xv = x_ref[...]
        gv = g_ref[...]
        r = lax.rsqrt(jnp.mean(xv * xv, axis=-1, keepdims=True) + EPS)
        xh = xv * r
        err = xh * gv - t_ref[...]
        ls_ref[...] += 0.5 * jnp.sum(jnp.mean(err * err, axis=-1, keepdims=True))
        dy = err * (1.0 / d)
        dxh = dy * gv
        dx_ref[...] = r * (dxh - xh * jnp.mean(dxh * xh, axis=-1, keepdims=True))
        st_ref[0:1, :] += jnp.sum(dy * xh, axis=0, keepdims=True)

    return pl.pallas_call(
        body, name=name, grid=(t // ROWS,),
        in_specs=[_row_spec(d), _const_spec((1, d)), _row_spec(d)],
        out_specs=(_row_spec(d), _const_spec((8, d)), _const_spec((8, LANES))),
        out_shape=(jax.ShapeDtypeStruct((t, d), F32), jax.ShapeDtypeStruct((8, d), F32),
                   jax.ShapeDtypeStruct((8, LANES), F32)),
        compiler_params=_params("arbitrary"),
    )(x, g, target)


FFN_BLOCK = D_FF // 2
FFN_ROWS = 512


def _ffn_chunks(width):
    edges = [min(width, 3 * LANES * i) for i in range(width // (3 * LANES) + 2)]
    return [slice(lo, hi) for lo, hi in zip(edges[:-1], edges[1:]) if hi > lo]


def _ffn_gu_fwd(h, wg, wu, dep, *, name):
    t, d = h.shape
    tn = FFN_BLOCK

    chunks = _ffn_chunks(tn)
    rows = _pick(t, FFN_ROWS, 16)

    def body(h_ref, wg_ref, wu_ref, dep_ref, s_ref, a_ref, b_ref):
        hv = h_ref[...]
        ab = [(_dotb(hv, wg_ref[sl, :], NT), _dotb(hv, wu_ref[sl, :], NT)) for sl in chunks]
        for sl, (a, b) in zip(chunks, ab):
            s_ref[:, sl] = (a * _sigmoid(a) * b).astype(s_ref.dtype)
            a_ref[:, sl] = a.astype(a_ref.dtype)
            b_ref[:, sl] = b.astype(b_ref.dtype)

    w_spec = pl.BlockSpec((tn, d), lambda j, i: (j, 0))
    o_spec = pl.BlockSpec((rows, tn), lambda j, i: (i, j))
    return pl.pallas_call(
        body, name=name, grid=(D_FF // tn, t // rows),
        in_specs=[pl.BlockSpec((rows, d), lambda j, i: (i, 0)), w_spec, w_spec,
                  pl.BlockSpec((8, LANES), lambda j, i: (0, 0))],
        out_specs=(o_spec, o_spec, o_spec),
        out_shape=(jax.ShapeDtypeStruct((t, D_FF), BF16),) * 3,
        compiler_params=_params("parallel", "parallel"),
    )(h, wg, wu, dep)


def _ffn_down_dx(dy, w_down, a, b, *, name):
    t, d = dy.shape
    tn = FFN_BLOCK

    chunks = _ffn_chunks(tn)
    rows = _pick(t, FFN_ROWS, 16)

    def body(dy_ref, w_ref, a_ref, b_ref, da_ref, db_ref):
        dyv = dy_ref[...]
        ds = [_dotb(dyv, w_ref[sl, :], NT) for sl in chunks]
        for sl, dsc in zip(chunks, ds):
            av = a_ref[:, sl].astype(F32)
            sg = _sigmoid(av)
            da_ref[:, sl] = (dsc * b_ref[:, sl].astype(F32) * sg * (1.0 + av * (1.0 - sg))).astype(da_ref.dtype)
            db_ref[:, sl] = (dsc * av * sg).astype(db_ref.dtype)

    o_spec = pl.BlockSpec((rows, tn), lambda j, i: (i, j))
    return pl.pallas_call(
        body, name=name, grid=(D_FF // tn, t // rows),
        in_specs=[pl.BlockSpec((rows, d), lambda j, i: (i, 0)), pl.BlockSpec((tn, d), lambda j, i: (j, 0)),
                  o_spec, o_spec],
        out_specs=(o_spec, o_spec),
        out_shape=(jax.ShapeDtypeStruct((t, D_FF), BF16),) * 2,
        compiler_params=_params("parallel", "parallel"),
    )(dy, w_down, a, b)


def _shift_rows(v, s, rows):
    if s == 0:
        return v
    return jnp.where(rows >= s, pltpu.roll(v, s, 0), 0.0)


def _unshift_rows(v, s, rows, t):
    if s == 0:
        return v
    return jnp.where(rows < t - s, pltpu.roll(v, t - s, 0), 0.0)


def _conv_silu(x, w, rows):
    z = w[GDN_CONV - 1:GDN_CONV, :] * x
    for j in range(GDN_CONV - 1):
        z = z + w[j:j + 1, :] * _shift_rows(x, GDN_CONV - 1 - j, rows)
    sg = _sigmoid(z)
    return z, sg, z * sg


def _gdn_prep_fwd(proj, conv_wt, *, name):
    t = proj.shape[0]
    nh = GDN_HEADS

    hp = GDN_PREP_HEADS
    wd = hp * LANES

    def body(x_ref, w_ref, y_ref):
        j = pl.program_id(0) * hp
        rows = lax.broadcasted_iota(jnp.int32, (t, LANES), 0)
        qscale = jnp.where(j < nh, GDN_HEAD_DIM ** -0.5, 1.0)
        for i in range(hp):
            sl = slice(i * LANES, (i + 1) * LANES)
            _, _, s = _conv_silu(x_ref[:, sl], w_ref[:, sl], rows)
            rs = lax.rsqrt(jnp.sum(s * s, axis=-1, keepdims=True) + EPS)
            y_ref[:, sl] = jnp.where(j < 2 * nh, s * rs * qscale, s)

    return pl.pallas_call(
        body, name=name, grid=(3 * nh // hp,),
        in_specs=[pl.BlockSpec((t, wd), lambda j: (0, j)), pl.BlockSpec((GDN_CONV, wd), lambda j: (0, j))],
        out_specs=pl.BlockSpec((t, wd), lambda j: (0, j)),
        out_shape=jax.ShapeDtypeStruct((t, 3 * GDN_KEY_DIM), F32),
        compiler_params=_params("parallel"),
    )(proj, conv_wt)


def _gdn_prep_bwd(proj, conv_wt, dy, *, name):
    t = proj.shape[0]
    nh = GDN_HEADS

    hp = GDN_PREP_HEADS
    wd = hp * LANES
    per_seg = nh // hp

    def body(x_ref, w_ref, dy_ref, dx_ref, dw_ref):
        j = pl.program_id(0) * hp
        rows = lax.broadcasted_iota(jnp.int32, (t, LANES), 0)
        qscale = jnp.where(j < nh, GDN_HEAD_DIM ** -0.5, 1.0)
        for i in range(hp):
            sl = slice(i * LANES, (i + 1) * LANES)
            x = x_ref[:, sl]
            w = w_ref[:, sl]
            z, sg, s = _conv_silu(x, w, rows)
            rs = lax.rsqrt(jnp.sum(s * s, axis=-1, keepdims=True) + EPS)
            dyv = dy_ref[:, sl]
            nv = s * rs
            de = dyv * qscale
            ds_qk = rs * (de - nv * jnp.sum(de * nv, axis=-1, keepdims=True))
            ds = jnp.where(j < 2 * nh, ds_qk, dyv)
            dz = ds * sg * (1.0 + z * (1.0 - sg))
            dx = w[GDN_CONV - 1:GDN_CONV, :] * dz
            dw_ref[GDN_CONV - 1:GDN_CONV, sl] = jnp.sum(dz * x, axis=0, keepdims=True)
            for k in range(GDN_CONV - 1):
                sh = GDN_CONV - 1 - k
                dx = dx + w[k:k + 1, :] * _unshift_rows(dz, sh, rows, t)
                dw_ref[k:k + 1, sl] = jnp.sum(dz * _shift_rows(x, sh, rows), axis=0, keepdims=True)
            dx_ref[:, sl] = dx.astype(dx_ref.dtype)

    return pl.pallas_call(
        body, name=name, grid=(3 * nh // hp,),
        in_specs=[pl.BlockSpec((t, wd), lambda j: (0, j)), pl.BlockSpec((GDN_CONV, wd), lambda j: (0, j)),
                  pl.BlockSpec((None, t, wd), lambda j: (j // per_seg, 0, j % per_seg))],
        out_specs=(pl.BlockSpec((t, wd), lambda j: (0, j)), pl.BlockSpec((GDN_CONV, wd), lambda j: (0, j))),
        out_shape=(jax.ShapeDtypeStruct((t, 3 * GDN_KEY_DIM), BF16),
                   jax.ShapeDtypeStruct((GDN_CONV, 3 * GDN_KEY_DIM), F32)),
        compiler_params=_params("parallel"),
    )(proj, conv_wt, dy)


def _softplus(z):
    return jnp.maximum(z, 0.0) + jnp.log(1.0 + jnp.exp(-jnp.abs(z)))


def _gdn_gate_fwd(ab, prm, *, name):
    t = ab.shape[0]

    def body(ab_ref, p_ref, o_ref):
        v = ab_ref[...]
        lane = lax.broadcasted_iota(jnp.int32, v.shape, 1)
        g = -jnp.exp(p_ref[0:1, :]) * _softplus(v + p_ref[1:2, :])
        o_ref[...] = jnp.where(lane < GDN_HEADS, g, jnp.where(lane < 2 * GDN_HEADS, _sigmoid(v), 0.0))

    return pl.pallas_call(
        body, name=name, grid=(t // ROWS,),
        in_specs=[_row_spec(LANES), _const_spec((8, LANES))], out_specs=_row_spec(LANES),
        out_shape=jax.ShapeDtypeStruct((t, LANES), F32), compiler_params=_params("parallel"),
    )(ab, prm)


def _gdn_gate_bwd(ab, prm, dgb, *, name):
    t = ab.shape[0]

    def body(ab_ref, p_ref, d_ref, o_ref, st_ref):
        @pl.when(pl.program_id(0) == 0)
        def _():
            st_ref[...] = jnp.zeros_like(st_ref)

        v = ab_ref[...]
        dv = d_ref[...]
        lane = lax.broadcasted_iota(jnp.int32, v.shape, 1)
        is_a = lane < GDN_HEADS
        is_b = jnp.logical_and(lane >= GDN_HEADS, lane < 2 * GDN_HEADS)
        a_exp = jnp.exp(p_ref[0:1, :])
        zz = v + p_ref[1:2, :]
        g = -a_exp * _softplus(zz)
        da = dv * (-a_exp) * _sigmoid(zz)
        beta = _sigmoid(v)
        db = dv * beta * (1.0 - beta)
        o_ref[...] = jnp.where(is_a, da, jnp.where(is_b, db, 0.0)).astype(o_ref.dtype)
        st_ref[0:1, :] += jnp.sum(jnp.where(is_a, dv * g, 0.0), axis=0, keepdims=True)
        st_ref[1:2, :] += jnp.sum(jnp.where(is_a, da, 0.0), axis=0, keepdims=True)

    return pl.pallas_call(
        body, name=name, grid=(t // ROWS,),
        in_specs=[_row_spec(LANES), _const_spec((8, LANES)), _row_spec(LANES)],
        out_specs=(_row_spec(LANES), _const_spec((8, LANES))),
        out_shape=(jax.ShapeDtypeStruct((t, LANES), BF16), jax.ShapeDtypeStruct((8, LANES), F32)),
        compiler_params=_params("arbitrary"),
    )(ab, prm, dgb)


def _gdn_local(qs, ks, vs, gbs, bbs, tinvs=None):
    nh = len(qs)
    cs = qs[0].shape[0]
    hs = range(nh)
    r = lax.broadcasted_iota(jnp.int32, (cs, cs), 0)
    c = lax.broadcasted_iota(jnp.int32, (cs, cs), 1)
    tril, strict, eye = r >= c, r > c, r == c
    ident = jnp.where(eye, 1.0, 0.0)
    g_colb = [gbs[h][:, :cs] for h in hs]
    g_row = [jnp.sum(jnp.where(eye, g_colb[h], 0.0), axis=0, keepdims=True) for h in hs]
    gc_col = [jnp.sum(jnp.where(tril, g_row[h], 0.0), axis=1, keepdims=True) for h in hs]
    gc_row = [jnp.sum(jnp.where(r <= c, g_colb[h], 0.0), axis=0, keepdims=True) for h in hs]
    decay = [jnp.exp(jnp.where(tril, gc_col[h] - gc_row[h], NEG)) for h in hs]
    gamma = [jnp.exp(gc_col[h]) for h in hs]
    gcl = [gc_col[h][cs - 1:cs, :] for h in hs]
    gl = [jnp.exp(gcl[h]) for h in hs]
    kdec = [jnp.exp(gcl[h] - gc_col[h]) for h in hs]
    kb = [ks[h] * bbs[h] for h in hs]
    kk = [_dotb(kb[h], ks[h], NT) for h in hs]
    qk = [_dotb(qs[h], ks[h], NT) for h in hs]
    lmat = [jnp.where(strict, kk[h] * decay[h], 0.0) for h in hs]
    pmat = [jnp.where(tril, qk[h] * decay[h], 0.0) for h in hs]
    if tinvs is None:
        xm = [-lmat[h] for h in hs]
        tinv = [ident + xm[h] for h in hs]
        for _ in range(int(math.log2(cs)) - 1):
            xm = [_dotf(xm[h], xm[h], NN) for h in hs]
            tinv = [tinv[h] + _dotf(tinv[h], xm[h], NN) for h in hs]
    else:
        tinv = tinvs
    vb = [vs[h] * bbs[h] for h in hs]
    kg = [kb[h] * gamma[h] for h in hs]
    u = [_dotf(tinv[h], vb[h], NN) for h in hs]
    w = [_dotf(tinv[h], kg[h], NN) for h in hs]
    return [dict(tril=tril, strict=strict, eye=eye, r=r, c=c, decay=decay[h], gamma=gamma[h], gl=gl[h], kdec=kdec[h],
                 kb=kb[h], lmat=lmat[h], tinv=tinv[h], vb=vb[h], kg=kg[h], u=u[h], w=w[h], pmat=pmat[h],
                 qd=qs[h] * gamma[h], kd=ks[h] * kdec[h]) for h in hs]


def _head_columns(gbeta, cs):
    gbs = [jnp.broadcast_to(gbeta[:, h:h + 1], (cs, LANES)) for h in range(GDN_HEADS)]
    bbs = [jnp.broadcast_to(gbeta[:, GDN_HEADS + h:GDN_HEADS + h + 1], (cs, LANES)) for h in range(GDN_HEADS)]
    return gbs, bbs


def _gdn_chunk_fwd(qkv, gbeta, *, name):
    t = qkv.shape[0]
    nh, cs, hd = GDN_HEADS, GDN_CHUNK, GDN_HEAD_DIM
    nc = t // cs

    hb = GDN_HEAD_BATCH
    ng = nh // hb
    assert ng == 1

    def body(q_ref, k_ref, v_ref, gb_ref, o_ref, st_ref, ti_ref, s_ref):
        @pl.when(pl.program_id(1) == 0)
        def _():
            s_ref[...] = jnp.zeros_like(s_ref)

        sls = [slice(i * hd, (i + 1) * hd) for i in range(hb)]
        hs = range(hb)
        s = [s_ref[i] for i in hs]
        gbs, bbs = _head_columns(gb_ref[...], cs)
        lo = _gdn_local([q_ref[:, sl] for sl in sls], [k_ref[:, sl] for sl in sls], [v_ref[:, sl] for sl in sls],
                        gbs, bbs)
        ws = [_dotb(lo[i]["w"], s[i], NN) for i in hs]
        qs = [_dotb(lo[i]["qd"], s[i], NN) for i in hs]
        vn = [lo[i]["u"] - ws[i] for i in hs]
        pv = [_dotb(lo[i]["pmat"], vn[i], NN) for i in hs]
        kv = [_dotb(lo[i]["kd"], vn[i], TN) for i in hs]
        for i, sl in enumerate(sls):
            st_ref[i, 0] = s[i]
            ti_ref[i, 0] = lo[i]["tinv"]
            o_ref[:, sl] = qs[i] + pv[i]
            s_ref[i] = s[i] * lo[i]["gl"] + kv[i]

    col = lambda off: pl.BlockSpec((cs, hb * hd), lambda h, n: (n, off + h))
    return pl.pallas_call(
        body, name=name, grid=(ng, nc),
        in_specs=[col(0), col(ng), col(2 * ng), pl.BlockSpec((cs, LANES), lambda h, n: (n, 0))],
        out_specs=(col(0), pl.BlockSpec((hb, 1, hd, hd), lambda h, n: (h, n, 0, 0)),
                   pl.BlockSpec((hb, 1, cs, cs), lambda h, n: (h, n, 0, 0))),
        out_shape=(jax.ShapeDtypeStruct((t, nh * hd), F32), jax.ShapeDtypeStruct((nh, nc, hd, hd), F32),
                   jax.ShapeDtypeStruct((nh, nc, cs, cs), F32)),
        scratch_shapes=[pltpu.VMEM((hb, hd, hd), F32)],
        compiler_params=_params("parallel", "arbitrary"),
    )(qkv, qkv, qkv, gbeta)


def _gdn_chunk_bwd(qkv, gbeta, states, tinvs, do, *, name):
    t = qkv.shape[0]
    nh, cs, hd = GDN_HEADS, GDN_CHUNK, GDN_HEAD_DIM
    nc = t // cs

    hb = GDN_HEAD_BATCH
    ng = nh // hb
    assert ng == 1

    def heads_bwd(q, k, v, gb, bb, s, ti, dsn, dov):
        hs = range(len(q))
        lo = _gdn_local(q, k, v, gb, bb, ti)
        tril, strict, eye, r, c = lo[0]["tril"], lo[0]["strict"], lo[0]["eye"], lo[0]["r"], lo[0]["c"]
        rowi = lax.broadcasted_iota(jnp.int32, (cs, 1), 0)
        get = lambda name: [lo[h][name] for h in hs]
        decay, gamma, gl, kdec = get("decay"), get("gamma"), get("gl"), get("kdec")
        kb, tinv, w, pmat, kd, qd = get("kb"), get("tinv"), get("w"), get("pmat"), get("kd"), get("qd")
        ws = [_dotb(w[h], s[h], NN) for h in hs]
        pdo = [_dotb(pmat[h], dov[h], TN) for h in hs]
        kds = [_dotb(kd[h], dsn[h], NN) for h in hs]
        dqd = [_dotb(dov[h], s[h], NT) for h in hs]
        qdo = [_dotb(qd[h], dov[h], TN) for h in hs]
        vn = [lo[h]["u"] - ws[h] for h in hs]
        dvn = [pdo[h] + kds[h] for h in hs]
        dp = [jnp.where(tril, _dotb(dov[h], vn[h], NT), 0.0) for h in hs]
        dkd = [_dotb(vn[h], dsn[h], NT) for h in hs]
        dw = [-_dotb(dvn[h], s[h], NT) for h in hs]
        wdv = [_dotb(w[h], dvn[h], TN) for h in hs]
        dvb = [_dotf(tinv[h], dvn[h], TN) for h in hs]
        dt1 = [_dotf(dvn[h], lo[h]["vb"], NT) for h in hs]
        dkg = [_dotf(tinv[h], dw[h], TN) for h in hs]
        dt2 = [_dotf(dw[h], lo[h]["kg"], NT) for h in hs]
        tdt = [_dotf(tinv[h], dt1[h] + dt2[h], TN) for h in hs]
        dl = [jnp.where(strict, -_dotf(tdt[h], tinv[h], NT), 0.0) for h in hs]
        dkk = [dl[h] * decay[h] for h in hs]
        dqk = [dp[h] * decay[h] for h in hs]
        dkb = [_dotb(dkk[h], k[h], NN) + dkg[h] * gamma[h] for h in hs]
        dk1 = [_dotb(dkk[h], kb[h], TN) for h in hs]
        dk2 = [_dotb(dqk[h], q[h], TN) for h in hs]
        dq1 = [_dotb(dqk[h], k[h], NN) for h in hs]
        out = []
        for h in hs:
            dgl = jnp.sum(jnp.sum(dsn[h] * s[h], axis=1, keepdims=True), axis=0, keepdims=True)
            ds_prev = gl[h] * dsn[h] + qdo[h] - wdv[h]
            dk = dk1[h] + dk2[h] + dkd[h] * kdec[h] + dkb[h] * bb[h]
            dq = dq1[h] + dqd[h] * gamma[h]
            dbeta = jnp.sum(dvb[h] * v[h], axis=-1, keepdims=True) + jnp.sum(dkb[h] * k[h], axis=-1, keepdims=True)
            e = dl[h] * lo[h]["lmat"] + dp[h] * pmat[h]
            e_col = jnp.sum(e, axis=0, keepdims=True)
            dgc = jnp.sum(e, axis=1, keepdims=True) - jnp.sum(jnp.where(eye, e_col, 0.0), axis=1, keepdims=True)
            dgamma = (jnp.sum(dqd[h] * q[h], axis=-1, keepdims=True)
                      + jnp.sum(dkg[h] * kb[h], axis=-1, keepdims=True))
            rk = jnp.sum(dkd[h] * k[h], axis=-1, keepdims=True) * kdec[h]
            dgcl = jnp.sum(rk, axis=0, keepdims=True) + dgl * gl[h]
            dgc = dgc + dgamma * gamma[h] - rk + jnp.where(rowi == cs - 1, dgcl, 0.0)
            dgc_row = jnp.sum(jnp.where(eye, dgc, 0.0), axis=0, keepdims=True)
            dg = jnp.sum(jnp.where(c >= r, dgc_row, 0.0), axis=1, keepdims=True)
            out.append((dq, dk, dvb[h] * bb[h], dbeta, dg, ds_prev))
        return out

    def body(q_ref, k_ref, v_ref, gb_ref, st_ref, ti_ref, do_ref, d_ref, dgb_ref, ds_ref):
        @pl.when(pl.program_id(1) == 0)
        def _():
            ds_ref[...] = jnp.zeros_like(ds_ref)

        sls = [slice(i * hd, (i + 1) * hd) for i in range(hb)]
        hs = range(hb)
        gbs, bbs = _head_columns(gb_ref[...], cs)
        outs = heads_bwd([q_ref[:, sl] for sl in sls], [k_ref[:, sl] for sl in sls], [v_ref[:, sl] for sl in sls],
                         gbs, bbs, [st_ref[i, 0] for i in hs],
                         [ti_ref[i, 0] for i in hs], [ds_ref[i] for i in hs], [do_ref[:, sl] for sl in sls])
        lane = lax.broadcasted_iota(jnp.int32, (cs, LANES), 1)
        dgb = jnp.zeros((cs, LANES), F32)
        for i, sl in enumerate(sls):
            dq, dk, dv, dbeta, dg, ds_prev = outs[i]
            d_ref[0, :, sl], d_ref[1, :, sl], d_ref[2, :, sl] = dq, dk, dv
            dgb = jnp.where(lane == i, dg, jnp.where(lane == nh + i, dbeta, dgb))
            ds_ref[i] = ds_prev
        dgb_ref[...] = dgb

    col = lambda off: pl.BlockSpec((cs, hb * hd), lambda h, n: (nc - 1 - n, off + h))
    gspec = pl.BlockSpec((cs, LANES), lambda h, n: (nc - 1 - n, 0))
    return pl.pallas_call(
        body, name=name, grid=(ng, nc),
        in_specs=[col(0), col(ng), col(2 * ng), gspec,
                  pl.BlockSpec((hb, 1, hd, hd), lambda h, n: (h, nc - 1 - n, 0, 0)),
                  pl.BlockSpec((hb, 1, cs, cs), lambda h, n: (h, nc - 1 - n, 0, 0)), col(0)],
        out_specs=(pl.BlockSpec((3, cs, hb * hd), lambda h, n: (0, nc - 1 - n, h)), gspec),
        out_shape=(jax.ShapeDtypeStruct((3, t, nh * hd), F32), jax.ShapeDtypeStruct((t, LANES), F32)),
        scratch_shapes=[pltpu.VMEM((hb, hd, hd), F32)],
        compiler_params=_params("parallel", "arbitrary"),
    )(qkv, qkv, qkv, gbeta, states, tinvs, do)


def _gdn_onorm_fwd(o, proj, norm_g, *, name):
    t = o.shape[0]
    w = GDN_KEY_DIM
    goff = 3 * GDN_KEY_DIM // w

    def body(o_ref, gp_ref, g_ref, y_ref):
        gv = g_ref[...]
        for h in range(GDN_HEADS):
            sl = slice(h * GDN_HEAD_DIM, (h + 1) * GDN_HEAD_DIM)
            oh = o_ref[:, sl]
            gp = gp_ref[:, sl]
            r = lax.rsqrt(jnp.mean(oh * oh, axis=-1, keepdims=True) + EPS)
            y_ref[:, sl] = (oh * r * gv * gp * _sigmoid(gp)).astype(y_ref.dtype)

    return pl.pallas_call(
        body, name=name, grid=(t // ROWS,),
        in_specs=[_row_spec(w), pl.BlockSpec((ROWS, w), lambda i: (i, goff)), _const_spec((1, GDN_HEAD_DIM))],
        out_specs=_row_spec(w), out_shape=jax.ShapeDtypeStruct((t, w), BF16),
        compiler_params=_params("parallel"),
    )(o, proj, norm_g)


def _gdn_onorm_bwd(o, proj, norm_g, dy, *, name):
    t = o.shape[0]
    w = GDN_KEY_DIM
    goff = 3 * GDN_KEY_DIM // w

    def body(o_ref, gp_ref, g_ref, dy_ref, do_ref, dgp_ref, st_ref):
        @pl.when(pl.program_id(0) == 0)
        def _():
            st_ref[...] = jnp.zeros_like(st_ref)

        gv = g_ref[...]
        acc = jnp.zeros((1, GDN_HEAD_DIM), F32)
        for h in range(GDN_HEADS):
            sl = slice(h * GDN_HEAD_DIM, (h + 1) * GDN_HEAD_DIM)
            oh = o_ref[:, sl]
            gp = gp_ref[:, sl]
            dyv = dy_ref[:, sl].astype(F32)
            r = lax.rsqrt(jnp.mean(oh * oh, axis=-1, keepdims=True) + EPS)
            xh = oh * r
            sg = _sigmoid(gp)
            dn = dyv * gp * sg
            dgp_ref[:, sl] = (dyv * xh * gv * sg * (1.0 + gp * (1.0 - sg))).astype(dgp_ref.dtype)
            acc = acc + jnp.sum(dn * xh, axis=0, keepdims=True)
            dxh = dn * gv
            do_ref[:, sl] = r * (dxh - xh * jnp.mean(dxh * xh, axis=-1, keepdims=True))
        st_ref[0:1, :] += acc

    return pl.pallas_call(
        body, name=name, grid=(t // ROWS,),
        in_specs=[_row_spec(w), pl.BlockSpec((ROWS, w), lambda i: (i, goff)), _const_spec((1, GDN_HEAD_DIM)),
                  _row_spec(w)],
        out_specs=(_row_spec(w), _row_spec(w), _const_spec((8, GDN_HEAD_DIM))),
        out_shape=(jax.ShapeDtypeStruct((t, w), F32), jax.ShapeDtypeStruct((t, w), BF16),
                   jax.ShapeDtypeStruct((8, GDN_HEAD_DIM), F32)),
        compiler_params=_params("arbitrary"),
    )(o, proj, norm_g, dy)


def _mla_prep_fwd(proj, qg, kvg, *, name):
    t = proj.shape[0]
    q1, k1 = MLA_Q_RANK, MLA_Q_RANK + MLA_KV_RANK

    def body(p_ref, qg_ref, kg_ref, cq_ref, ck_ref):
        cq = p_ref[:, 0:q1]
        ck = p_ref[:, q1:k1]
        cq_ref[...] = (cq * lax.rsqrt(jnp.mean(cq * cq, axis=-1, keepdims=True) + EPS) * qg_ref[...]).astype(BF16)
        ck_ref[...] = (ck * lax.rsqrt(jnp.mean(ck * ck, axis=-1, keepdims=True) + EPS) * kg_ref[...]).astype(BF16)

    return pl.pallas_call(
        body, name=name, grid=(t // ROWS,),
        in_specs=[_row_spec(MLA_IN), _const_spec((1, MLA_Q_RANK)), _const_spec((1, MLA_KV_RANK))],
        out_specs=(_row_spec(MLA_Q_RANK), _row_spec(MLA_KV_RANK)),
        out_shape=(jax.ShapeDtypeStruct((t, MLA_Q_RANK), BF16), jax.ShapeDtypeStruct((t, MLA_KV_RANK), BF16)),
        compiler_params=_params("parallel"),
    )(proj, qg, kvg)


def _mla_prep_bwd(proj, qg, kvg, dcq, dck, dkr, *, name):
    t = proj.shape[0]
    q1, k1 = MLA_Q_RANK, MLA_Q_RANK + MLA_KV_RANK

    def body(p_ref, qg_ref, kg_ref, dq_ref, dk_ref, dr_ref, dp_ref, st_ref):
        @pl.when(pl.program_id(0) == 0)
        def _():
            st_ref[...] = jnp.zeros_like(st_ref)

        for lo, hi, g_ref, d_ref in ((0, q1, qg_ref, dq_ref), (q1, k1, kg_ref, dk_ref)):
            xv = p_ref[:, lo:hi]
            dn = d_ref[...]
            r = lax.rsqrt(jnp.mean(xv * xv, axis=-1, keepdims=True) + EPS)
            xh = xv * r
            dxh = dn * g_ref[...]
            dp_ref[:, lo:hi] = (r * (dxh - xh * jnp.mean(dxh * xh, axis=-1, keepdims=True))).astype(dp_ref.dtype)
            st_ref[0:1, lo:hi] += jnp.sum(dn * xh, axis=0, keepdims=True)
        dp_ref[:, k1:MLA_IN] = dr_ref[:, 0:MLA_ROPE].astype(dp_ref.dtype)

    return pl.pallas_call(
        body, name=name, grid=(t // ROWS,),
        in_specs=[_row_spec(MLA_IN), _const_spec((1, MLA_Q_RANK)), _const_spec((1, MLA_KV_RANK)),
                  _row_spec(MLA_Q_RANK), _row_spec(MLA_KV_RANK), _row_spec(LANES)],
        out_specs=(_row_spec(MLA_IN), _const_spec((8, MLA_IN))),
        out_shape=(jax.ShapeDtypeStruct((t, MLA_IN), BF16), jax.ShapeDtypeStruct((8, MLA_IN), F32)),
        compiler_params=_params("arbitrary"),
    )(proj, qg, kvg, dcq, dck, dkr)


ATT_BLOCK = 256
ATT_HEAD_BATCH = 4
ATT_HEAD_BATCH_BWD = 4
ATT_SCALE = MLA_QK ** -0.5


def _diagonal_mask(blk):
    return lax.broadcasted_iota(jnp.int32, (blk, blk), 1) <= lax.broadcasted_iota(jnp.int32, (blk, blk), 0)


def _swap_halves(xv, first):
    return jnp.where(first, pltpu.roll(xv, LANES - MLA_ROPE // 2, 1), pltpu.roll(xv, MLA_ROPE // 2, 1))


def _rope_qk(qf, proj, cos_t, sin_t, *, name):
    t = qf.shape[0]
    nrope = MLA_HEADS * MLA_ROPE
    q_blk = MLA_HEADS * MLA_NOPE // nrope
    k_blk = (MLA_Q_RANK + MLA_KV_RANK) // LANES

    def body(q_ref, p_ref, c_ref, s_ref, qo_ref, ko_ref):
        cv, sv = c_ref[...], s_ref[...]
        lane = lax.broadcasted_iota(jnp.int32, (ROWS, LANES), 1)
        first = (lane % MLA_ROPE) < (MLA_ROPE // 2)
        for i in range(nrope // LANES):
            sl = slice(i * LANES, (i + 1) * LANES)
            xv = q_ref[:, sl].astype(F32)
            qo_ref[:, sl] = (xv * cv + _swap_halves(xv, first) * sv).astype(qo_ref.dtype)
        kv = jnp.where(lane < MLA_ROPE, p_ref[...], 0.0)
        ko_ref[...] = (kv * cv + _swap_halves(kv, first) * sv).astype(ko_ref.dtype)

    return pl.pallas_call(
        body, name=name, grid=(t // ROWS,),
        in_specs=[pl.BlockSpec((ROWS, nrope), lambda i: (i, q_blk)), pl.BlockSpec((ROWS, LANES), lambda i: (i, k_blk)),
                  _row_spec(LANES), _row_spec(LANES)],
        out_specs=(_row_spec(nrope), _row_spec(LANES)),
        out_shape=(jax.ShapeDtypeStruct((t, nrope), BF16), jax.ShapeDtypeStruct((t, LANES), BF16)),
        compiler_params=_params("parallel"),
    )(qf, proj, cos_t, sin_t)


def _rope_qk_bwd(dqr, dkr_parts, cos_t, sin_t, *, name):
    t, nrope = dqr.shape
    ng = dkr_parts.shape[0]

    def body(d_ref, k_ref, c_ref, s_ref, qo_ref, ko_ref):
        cv, sv = c_ref[...], s_ref[...]
        lane = lax.broadcasted_iota(jnp.int32, (ROWS, LANES), 1)
        first = (lane % MLA_ROPE) < (MLA_ROPE // 2)
        for i in range(nrope // LANES):
            sl = slice(i * LANES, (i + 1) * LANES)
            dv = d_ref[:, sl]
            qo_ref[:, sl] = (dv * cv + _swap_halves(dv * sv, first)).astype(qo_ref.dtype)
        dk = k_ref[0]
        for g in range(1, ng):
            dk = dk + k_ref[g]
        dk = jnp.where(lane < MLA_ROPE, dk, 0.0)
        ko_ref[...] = jnp.where(lane < MLA_ROPE, dk * cv + _swap_halves(dk * sv, first), 0.0)

    return pl.pallas_call(
        body, name=name, grid=(t // ROWS,),
        in_specs=[_row_spec(nrope), pl.BlockSpec((ng, ROWS, LANES), lambda i: (0, i, 0)), _row_spec(LANES),
                  _row_spec(LANES)],
        out_specs=(_row_spec(nrope), _row_spec(LANES)),
        out_shape=(jax.ShapeDtypeStruct((t, nrope), BF16), jax.ShapeDtypeStruct((t, LANES), F32)),
        compiler_params=_params("parallel"),
    )(dqr, dkr_parts, cos_t, sin_t)


def _attn_tm_fwd(qf, qr, kvf, kr, *, name):
    t = qf.shape[0]
    nh, dn, dr, dv = MLA_HEADS, MLA_NOPE, MLA_ROPE, MLA_V
    blk = min(ATT_BLOCK, t)
    hb = ATT_HEAD_BATCH
    hs = range(hb)

    def body(q_ref, qr_ref, kv_ref, kr_ref, o_ref, l_ref):
        i = pl.program_id(1)
        qn = [q_ref[:, h * dn:(h + 1) * dn].astype(MXU_DTYPE) for h in hs]
        qrh = [qr_ref[:, h * dr:(h + 1) * dr] for h in hs]

        def step(j, carry, diagonal=False):
            m, l, acc = carry[:hb], carry[hb:2 * hb], carry[2 * hb:]
            rows = pl.ds(pl.multiple_of(j * blk, blk), blk)
            krj = kr_ref[rows, 0:dr]
            s = [_dotb(qn[h], kv_ref[rows, h * (dn + dv):h * (dn + dv) + dn], NT) for h in hs]
            sr = [_dotb(qrh[h], krj, NT) for h in hs]
            s = [(s[h] + sr[h]) * ATT_SCALE for h in hs]
            if diagonal:
                mask = _diagonal_mask(blk)
                s = [jnp.where(mask, s[h], NEG) for h in hs]
            m_new = [jnp.maximum(m[h], jnp.max(s[h], axis=-1, keepdims=True)) for h in hs]
            p = [jnp.exp(s[h] - m_new[h]) for h in hs]
            pv = [_dotb(p[h], kv_ref[rows, h * (dn + dv) + dn:(h + 1) * (dn + dv)], NN) for h in hs]
            alpha = [jnp.exp(m[h] - m_new[h]) for h in hs]
            l = [alpha[h] * l[h] + jnp.sum(p[h], axis=-1, keepdims=True) for h in hs]
            acc = [alpha[h] * acc[h] + pv[h] for h in hs]
            return tuple(m_new) + tuple(l) + tuple(acc)

        init = ((jnp.full((blk, 1), NEG, F32),) * hb + (jnp.zeros((blk, 1), F32),) * hb
                + (jnp.zeros((blk, dv), F32),) * hb)
        out = step(i, lax.fori_loop(0, i, step, init), diagonal=True)
        for h in hs:
            m, l, acc = out[h], out[hb + h], out[2 * hb + h]
            o_ref[:, h * dv:(h + 1) * dv] = (acc / l).astype(o_ref.dtype)
            l_ref[h] = jnp.broadcast_to(m + jnp.log(l), (blk, LANES))

    return pl.pallas_call(
        body, name=name, grid=(nh // hb, t // blk),
        in_specs=[pl.BlockSpec((blk, hb * dn), lambda g, i: (i, g)), pl.BlockSpec((blk, hb * dr), lambda g, i: (i, g)),
                  pl.BlockSpec((t, hb * (dn + dv)), lambda g, i: (0, g)), pl.BlockSpec((t, LANES), lambda g, i: (0, 0))],
        out_specs=(pl.BlockSpec((blk, hb * dv), lambda g, i: (i, g)),
                   pl.BlockSpec((hb, blk, LANES), lambda g, i: (g, i, 0))),
        out_shape=(jax.ShapeDtypeStruct((t, nh * dv), BF16), jax.ShapeDtypeStruct((nh, t, LANES), F32)),
        compiler_params=_params("parallel", "parallel"),
    )(qf, qr, kvf, kr)


def _attn_tm_bwd(qf, qr, kvf, kr, o, lse, do, *, name):
    t = qf.shape[0]
    nh, dn, dr, dv = MLA_HEADS, MLA_NOPE, MLA_ROPE, MLA_V
    blk = min(ATT_BLOCK, t)
    nb = t // blk
    hb = ATT_HEAD_BATCH_BWD
    hs = range(hb)
    ng = nh // hb

    def body(q_ref, qr_ref, kv_ref, kr_ref, o_ref, l_ref, do_ref, dqn_ref, dqr_ref, dkv_ref, dkr_ref):
        j = pl.program_id(1)

        @pl.when(j == 0)
        def _():
            dqn_ref[...] = jnp.zeros_like(dqn_ref)
            dqr_ref[...] = jnp.zeros_like(dqr_ref)

        kn = [kv_ref[:, h * (dn + dv):h * (dn + dv) + dn] for h in hs]
        vv = [kv_ref[:, h * (dn + dv) + dn:(h + 1) * (dn + dv)] for h in hs]
        krj = kr_ref[:, 0:dr]

        def step(i, carry, diagonal=False):
            dkn_acc, dv_acc, dkr_acc = carry[:hb], carry[hb:2 * hb], carry[2 * hb]
            rows = pl.ds(pl.multiple_of(i * blk, blk), blk)
            qn = [q_ref[rows, h * dn:(h + 1) * dn].astype(MXU_DTYPE) for h in hs]
            qrh = [qr_ref[rows, h * dr:(h + 1) * dr] for h in hs]
            dov = [do_ref[rows, h * dv:(h + 1) * dv] for h in hs]
            s = [_dotb(qn[h], kn[h], NT) for h in hs]
            sr = [_dotb(qrh[h], krj, NT) for h in hs]
            dp = [_dotb(dov[h], vv[h], NT) for h in hs]
            s = [(s[h] + sr[h]) * ATT_SCALE for h in hs]
            if diagonal:
                mask = _diagonal_mask(blk)
                s = [jnp.where(mask, s[h], NEG) for h in hs]
            p = [jnp.exp(s[h] - l_ref[h, rows, :][:, 0:1]) for h in hs]
            delta = [jnp.sum(dov[h].astype(F32) * o_ref[rows, h * dv:(h + 1) * dv].astype(F32), axis=-1, keepdims=True)
                     for h in hs]
            ds = [p[h] * (dp[h] - delta[h]) * ATT_SCALE for h in hs]
            dvn = [_dotb(p[h], dov[h], TN) for h in hs]
            dknn = [_dotb(ds[h], qn[h], TN) for h in hs]
            dkrn = [_dotb(ds[h], qrh[h], TN) for h in hs]
            dqnn = [_dotb(ds[h], kn[h], NN) for h in hs]
            dqrn = [_dotb(ds[h], krj, NN) for h in hs]
            for h in hs:
                dqn_ref[rows, h * dn:(h + 1) * dn] += dqnn[h]
                dqr_ref[rows, h * dr:(h + 1) * dr] += dqrn[h]
            dkr_new = dkr_acc
            for h in hs:
                dkr_new = dkr_new + dkrn[h]
            return (tuple(dkn_acc[h] + dknn[h] for h in hs) + tuple(dv_acc[h] + dvn[h] for h in hs) + (dkr_new,))

        init = (jnp.zeros((blk, dn), F32),) * hb + (jnp.zeros((blk, dv), F32),) * hb + (jnp.zeros((blk, dr), F32),)
        out = lax.fori_loop(j + 1, nb, step, step(j, init, diagonal=True))
        for h in hs:
            dkv_ref[:, h * (dn + dv):h * (dn + dv) + dn] = out[h].astype(dkv_ref.dtype)
            dkv_ref[:, h * (dn + dv) + dn:(h + 1) * (dn + dv)] = out[hb + h].astype(dkv_ref.dtype)
        dkr_ref[0, :, 0:dr] = out[2 * hb]
        dkr_ref[0, :, dr:LANES] = jnp.zeros((blk, LANES - dr), F32)

    full = lambda w: pl.BlockSpec((t, w), lambda g, j: (0, g))
    return pl.pallas_call(
        body, name=name, grid=(ng, nb),
        in_specs=[full(hb * dn), full(hb * dr), pl.BlockSpec((blk, hb * (dn + dv)), lambda g, j: (j, g)),
                  pl.BlockSpec((blk, LANES), lambda g, j: (j, 0)), full(hb * dv),
                  pl.BlockSpec((hb, t, LANES), lambda g, j: (g, 0, 0)), full(hb * dv)],
        out_specs=(full(hb * dn), full(hb * dr), pl.BlockSpec((blk, hb * (dn + dv)), lambda g, j: (j, g)),
                   pl.BlockSpec((1, blk, LANES), lambda g, j: (g, j, 0))),
        out_shape=(jax.ShapeDtypeStruct((t, nh * dn), F32), jax.ShapeDtypeStruct((t, nh * dr), F32),
                   jax.ShapeDtypeStruct((t, nh * (dn + dv)), BF16), jax.ShapeDtypeStruct((ng, t, LANES), F32)),
        compiler_params=_params("parallel", "arbitrary"),
    )(qf, qr, kvf, kr, o, lse, do)


def _ada_mod(c_all, ada_w, ada_b_cols, *, name):
    nl, d, wc = ada_w.shape

    def body(c_ref, w_ref, b_ref, o_ref):
        cv = c_ref[...]
        o_ref[0] = _dotb(cv * _sigmoid(cv), w_ref[0], NN) + b_ref[0]

    return pl.pallas_call(
        body, name=name, grid=(nl,),
        in_specs=[_const_spec((N_DEV, d)), pl.BlockSpec((1, d, wc), lambda l: (l, 0, 0)),
                  pl.BlockSpec((1, 1, wc), lambda l: (l, 0, 0))],
        out_specs=pl.BlockSpec((1, N_DEV, wc), lambda l: (l, 0, 0)),
        out_shape=jax.ShapeDtypeStruct((nl, N_DEV, wc), F32), compiler_params=_params("parallel"),
    )(c_all, ada_w, ada_b_cols)


def _adam_math(g, w, m, v):
    m2 = ADAM_B1 * m + (1.0 - ADAM_B1) * g
    v2 = ADAM_B2 * v + (1.0 - ADAM_B2) * (g * g)
    delta = -ADAM_LR * ((m2 / ADAM_BC1) / (jnp.sqrt(v2 / ADAM_BC2) + ADAM_EPS) + ADAM_WD * w)
    return delta, m2, v2


def _ada_grad_adamw(c_all, dmod_cols, w, m, v, *, name):
    nl, d, wc = w.shape
    tr = 256

    def body(c_ref, dm_ref, w_ref, m_ref, v_ref, g_ref, d_ref, m2_ref, v2_ref):
        cv = c_ref[...]
        g = _dotf(cv * _sigmoid(cv), dm_ref[0], TN)
        delta, m2, v2 = _adam_math(g, w_ref[0], m_ref[0], v_ref[0])
        g_ref[0], d_ref[0], m2_ref[0], v2_ref[0] = g, delta, m2, v2

    blk = pl.BlockSpec((1, tr, wc), lambda l, i: (l, i, 0))
    return pl.pallas_call(
        body, name=name, grid=(nl, d // tr),
        in_specs=[pl.BlockSpec((N_DEV, tr), lambda l, i: (0, i)), pl.BlockSpec((1, N_DEV, wc), lambda l, i: (l, 0, 0)),
                  blk, blk, blk],
        out_specs=(blk,) * 4, out_shape=(jax.ShapeDtypeStruct(w.shape, F32),) * 4,
        compiler_params=_params("parallel", "parallel"),
    )(c_all, dmod_cols, w, m, v)


def _adamw(parts, w, m, v, *, name):
    nl, r, c = w.shape
    ns = parts[0].shape[0]
    lanes_padded = -(-c // LANES) * LANES
    row_bytes = 2 * nl * ns * lanes_padded * parts[0].dtype.itemsize
    tr = _pick(r, min(256, max(16, (VMEM_LIMIT // 2) // row_bytes)), 16)
    tc = c
    if tr * row_bytes > VMEM_LIMIT // 2:
        tc = _pick(c, max(LANES, c * (VMEM_LIMIT // 2) // (tr * row_bytes)))

    def body(*refs):
        p_refs = refs[:nl]
        w_ref, m_ref, v_ref, g_ref, d_ref, m2_ref, v2_ref = refs[nl:]
        layer = pl.program_id(0)
        for q in range(nl):
            @pl.when(layer == q)
            def _(q=q):
                g = p_refs[q][0].astype(F32)
                for s in range(1, ns):
                    g = g + p_refs[q][s].astype(F32)
                delta, m2, v2 = _adam_math(g, w_ref[0], m_ref[0], v_ref[0])
                g_ref[0], d_ref[0], m2_ref[0], v2_ref[0] = g, delta, m2, v2

    blk = pl.BlockSpec((1, tr, tc), lambda l, i, j: (l, i, j))
    p_specs = [pl.BlockSpec((ns, tr, tc), lambda l, i, j, q=q: (0, jnp.where(l == q, i, 0), jnp.where(l == q, j, 0)))
               for q in range(nl)]
    return pl.pallas_call(
        body, name=name, grid=(nl, r // tr, c // tc),
        in_specs=p_specs + [blk, blk, blk],
        out_specs=(blk,) * 4, out_shape=(jax.ShapeDtypeStruct(w.shape, F32),) * 4,
        compiler_params=_params("arbitrary", "arbitrary", "arbitrary"),
    )(*parts, w, m, v)


def _sum_parts(parts, *, name):
    ns, r, c = parts.shape

    def body(p_ref, o_ref):
        acc = p_ref[0]
        for s in range(1, ns):
            acc = acc + p_ref[s]
        o_ref[...] = acc

    return pl.pallas_call(
        body, name=name, out_shape=jax.ShapeDtypeStruct((r, c), F32),
        in_specs=[pl.BlockSpec(memory_space=pltpu.VMEM)], out_specs=pl.BlockSpec(memory_space=pltpu.VMEM),
    )(parts)


def _pack(arrs):
    flat = jnp.concatenate([a.reshape(-1).astype(F32) for a in arrs])
    pad = (-flat.shape[0]) % (8 * LANES)
    return jnp.pad(flat, (0, pad)).reshape(-1, LANES)


def _unpack(packed, shapes, lead=()):
    flat = packed.reshape(lead + (-1,))
    out, off = [], 0
    for s in shapes:
        n = math.prod(s)
        out.append(flat[..., off:off + n].reshape(lead + tuple(s)))
        off += n
    return out


def _gather_rows(g):
    _, nl, rs, c = g.shape
    return jnp.transpose(g, (1, 0, 2, 3)).reshape(nl, N_DEV * rs, c)


def _row(v):
    return v.reshape(1, -1)


def _local_step(x, target, mod, cos_t, sin_t, rep, get_weights, put_grads):
    t = x.shape[0]
    saved = []
    for layer in range(DEPTH):
        j = layer // 2
        tag = f"l{layer}"
        shift_m, scale_m, gate_m, shift_f, scale_f, gate_f = [_row(mod[layer, i]) for i in range(N_MOD)]
        lw = dict(get_weights(layer, "mix", x))
        rec = {"x0": x, "lw": lw}
        h = _adaln_fwd(x, _row(rep["norm_mix_g"][layer]), scale_m, shift_m, name=f"adaln_mix_{tag}")
        rec["h"] = h
        if layer % 2 == 0:
            proj = _mm(h, lw["wt_in"], mode="nt", out_dtype=F32, tm=256, tn=GDN_MAIN, b_rows=GDN_MAIN,
                       dep=lw["dep_mix"], name=f"gdn_in_{tag}")
            ab = _mm(h, lw["wt_ab"], mode="nt", out_dtype=F32, name=f"gdn_in_ab_{tag}")
            qkv = _gdn_prep_fwd(proj, rep["gdn_conv_wt"][j], name=f"gdn_prep_{tag}")
            gbeta = _gdn_gate_fwd(ab, rep["gdn_gate_prm"][j], name=f"gdn_gate_{tag}")
            o, states, tinvs = _gdn_chunk_fwd(qkv, gbeta, name=f"gdn_chunk_{tag}")
            og = _gdn_onorm_fwd(o, proj, _row(rep["gdn_norm_g"][j]), name=f"gdn_onorm_{tag}")
            x, y = _mm_resid(og, lw["w_out"], x, gate_m, name=f"gdn_out_{tag}")
            rec.update(proj=proj, ab=ab, qkv=qkv, gbeta=gbeta, states=states, tinvs=tinvs, o=o, og=og, y=y)
        else:
            proj = _mm(h, lw["w_in"], mode="nn", out_dtype=F32, dep=lw["dep_mix"], name=f"mla_in_{tag}")
            cq, ck = _mla_prep_fwd(proj, _row(rep["mla_q_norm_g"][j]), _row(rep["mla_kv_norm_g"][j]),
                                   name=f"mla_prep_{tag}")
            qf = _mm(cq, lw["wt_uq"], mode="nt", out_dtype=BF16, name=f"mla_uq_{tag}")
            kvf = _mm(ck, lw["w_ukv"], mode="nn", out_dtype=BF16, name=f"mla_ukv_{tag}")
            qr, kr = _rope_qk(qf, proj, cos_t, sin_t, name=f"rope_{tag}")
            oc, lse = _attn_tm_fwd(qf, qr, kvf, kr, name=f"attn_{tag}")
            x, y = _mm_resid(oc, lw["w_out"], x, gate_m, name=f"mla_out_{tag}")
            rec.update(proj=proj, cq=cq, ck=ck, qf=qf, qr=qr, kvf=kvf, kr=kr, lse=lse, oc=oc, y=y)
        rec["x1"] = x
        lw.update(get_weights(layer, "ffn", x))
        h2 = _adaln_fwd(x, _row(rep["norm_ffn_g"][layer]), scale_f, shift_f, name=f"adaln_ffn_{tag}")
        s, a2, b2 = _ffn_gu_fwd(h2, lw["wt_g"], lw["wt_u"], lw["dep_ffn"], name=f"ffn_gu_{tag}")
        x, y2 = _mm_resid(s, lw["w_down"], x, gate_f, tm=512, name=f"ffn_down_{tag}")
        rec.update(h2=h2, a2=a2, b2=b2, s=s, y2=y2)
        saved.append(rec)

    dx, st, ls = _loss_head(x, _row(rep["final_norm_g"]), target, name="loss_head")
    loss = ls[0, 0]
    grads = {"final_norm_g": st[0]}
    per_layer = {k: [None] * DEPTH for k in ("norm_mix_g", "norm_ffn_g")}
    per_gdn = {k: [None] * 2 for k in ("gdn_conv_wt", "gdn_a_log", "gdn_dt_bias", "gdn_norm_g")}
    per_mla = {k: [None] * 2 for k in ("mla_q_norm_g", "mla_kv_norm_g")}
    dmod = [None] * DEPTH
    dep = jnp.zeros((8, LANES), F32)

    for layer in reversed(range(DEPTH)):
        j = layer // 2
        tag = f"l{layer}"
        rec = saved[layer]
        lw = rec["lw"]
        shift_m, scale_m, gate_m, shift_f, scale_f, gate_f = [_row(mod[layer, i]) for i in range(N_MOD)]
        dy2, st_g = _gate_bwd(dx, rec["y2"], gate_f, dep, name=f"gate_bwd_ffn_{tag}")
        dgate_f = st_g[0]
        dw_down = _mm(rec["s"], dy2, mode="tn", out_dtype=BF16, tm=FFN_BLOCK, tn=1024, name=f"ffn_down_dw_{tag}")
        da2, db2 = _ffn_down_dx(dy2, lw["w_down"], rec["a2"], rec["b2"], name=f"ffn_down_dx_{tag}")
        dwt_g = _mm(da2, rec["h2"], mode="tn", out_dtype=BF16, tm=FFN_BLOCK, tn=1024, name=f"ffn_g_dw_{tag}")
        dwt_u = _mm(db2, rec["h2"], mode="tn", out_dtype=BF16, tm=FFN_BLOCK, tn=1024, name=f"ffn_u_dw_{tag}")
        dep = put_grads(layer, "ffn", {"wt_g": dwt_g, "wt_u": dwt_u, "w_down": dw_down})
        dh2 = _mm(da2, lw["wt_g"], mode="nn", out_dtype=F32, tm=512, tn=1024, name=f"ffn_g_dx_{tag}")
        dh2 = _mm(db2, lw["wt_u"], mode="nn", out_dtype=BF16, add=dh2, tm=512, tn=1024, name=f"ffn_u_dx_{tag}")
        dx, st_n = _adaln_bwd(rec["x1"], _row(rep["norm_ffn_g"][layer]), scale_f, shift_f, dh2, dx, dep,
                              name=f"adaln_ffn_bwd_{tag}")
        per_layer["norm_ffn_g"][layer] = st_n[0]
        dscale_f, dshift_f = st_n[1], st_n[2]
        dy, st_g = _gate_bwd(dx, rec["y"], gate_m, dep, name=f"gate_bwd_mix_{tag}")
        dgate_m = st_g[0]
        big = {}
        if layer % 2 == 0:
            big["w_out"] = _mm(rec["og"], dy, mode="tn", out_dtype=BF16, name=f"gdn_out_dw_{tag}")
            dog = _mm(dy, lw["w_out"], mode="nt", out_dtype=BF16, name=f"gdn_out_dx_{tag}")
            do, dgp, st_o = _gdn_onorm_bwd(rec["o"], rec["proj"], _row(rep["gdn_norm_g"][j]), dog,
                                           name=f"gdn_onorm_bwd_{tag}")
            per_gdn["gdn_norm_g"][j] = st_o[0]
            dqkv, dgb = _gdn_chunk_bwd(rec["qkv"], rec["gbeta"], rec["states"], rec["tinvs"], do,
                                       name=f"gdn_chunk_bwd_{tag}")
            dab, st_a = _gdn_gate_bwd(rec["ab"], rep["gdn_gate_prm"][j], dgb, name=f"gdn_gate_bwd_{tag}")
            per_gdn["gdn_a_log"][j] = st_a[0, :GDN_HEADS]
            per_gdn["gdn_dt_bias"][j] = st_a[1, :GDN_HEADS]
            dpre, dcw = _gdn_prep_bwd(rec["proj"], rep["gdn_conv_wt"][j], dqkv, name=f"gdn_prep_bwd_{tag}")
            per_gdn["gdn_conv_wt"][j] = dcw
            dproj = jnp.concatenate([dpre, dgp], axis=1)
            dw_main = _mm(dproj, rec["h"], mode="tn", out_dtype=BF16, tm=512, tn=1024, name=f"gdn_in_dw_{tag}")
            dw_ab = _mm(dab, rec["h"], mode="tn", out_dtype=BF16, tn=1024, name=f"gdn_in_ab_dw_{tag}")
            big["wt_in"] = jnp.concatenate([dw_main, dw_ab[:2 * GDN_HEADS]], axis=0)
            dep = put_grads(layer, "gdn", big)
            dh_ab = _mm(dab, lw["wt_ab"], mode="nn", out_dtype=F32, tn=1024, name=f"gdn_in_ab_dx_{tag}")
            dh = _mm(dproj, lw["wt_in"], mode="nn", out_dtype=BF16, add=dh_ab, tm=256, tn=1024, b_rows=GDN_MAIN,
                     name=f"gdn_in_dx_{tag}")
        else:
            big["w_out"] = _mm(rec["oc"], dy, mode="tn", out_dtype=BF16, name=f"mla_out_dw_{tag}")
            doc = _mm(dy, lw["w_out"], mode="nt", out_dtype=BF16, name=f"mla_out_dx_{tag}")
            dqn, dqr, dkvf, dkr_parts = _attn_tm_bwd(rec["qf"], rec["qr"], rec["kvf"], rec["kr"], rec["oc"],
                                                     rec["lse"], doc, name=f"attn_bwd_{tag}")
            dqr_un, dkr_un = _rope_qk_bwd(dqr, dkr_parts, cos_t, sin_t, name=f"rope_bwd_{tag}")
            n_nope = MLA_HEADS * MLA_NOPE
            big["wt_uq"] = jnp.concatenate(
                [_mm(dqn, rec["cq"], mode="tn", out_dtype=BF16, name=f"mla_uq_dw_nope_{tag}"),
                 _mm(dqr_un, rec["cq"], mode="tn", out_dtype=BF16, name=f"mla_uq_dw_rope_{tag}")], axis=0)
            big["w_ukv"] = _mm(rec["ck"], dkvf, mode="tn", out_dtype=BF16, name=f"mla_ukv_dw_{tag}")
            dcq = _mm(dqr_un, lw["wt_uq"][n_nope:], mode="nn", out_dtype=F32, name=f"mla_uq_dx_rope_{tag}")
            dcq = _mm(dqn, lw["wt_uq"], mode="nn", out_dtype=F32, add=dcq, b_rows=n_nope,
                      name=f"mla_uq_dx_nope_{tag}")
            dck = _mm(dkvf, lw["w_ukv"], mode="nt", out_dtype=F32, name=f"mla_ukv_dx_{tag}")
            dproj, st_p = _mla_prep_bwd(rec["proj"], _row(rep["mla_q_norm_g"][j]), _row(rep["mla_kv_norm_g"][j]),
                                        dcq, dck, dkr_un, name=f"mla_prep_bwd_{tag}")
            per_mla["mla_q_norm_g"][j] = st_p[0, :MLA_Q_RANK]
            per_mla["mla_kv_norm_g"][j] = st_p[0, MLA_Q_RANK:MLA_Q_RANK + MLA_KV_RANK]
            big["w_in"] = _mm(rec["h"], dproj, mode="tn", out_dtype=BF16, name=f"mla_in_dw_{tag}")
            dep = put_grads(layer, "mla", big)
            dh = _mm(dproj, lw["w_in"], mode="nt", out_dtype=BF16, name=f"mla_in_dx_{tag}")
        dx, st_n = _adaln_bwd(rec["x0"], _row(rep["norm_mix_g"][layer]), scale_m, shift_m, dh, dx, dep,
                              name=f"adaln_mix_bwd_{tag}")
        per_layer["norm_mix_g"][layer] = st_n[0]
        dmod[layer] = jnp.stack([st_n[2], st_n[1], dgate_m, dshift_f, dscale_f, dgate_f])

    for d in (per_layer, per_gdn, per_mla):
        for k, v in d.items():
            grads[k] = jnp.stack(v)
    return loss, dx, jnp.stack(dmod), grads


BIG = ("gdn_w_in", "gdn_w_out", "mla_w_in", "mla_w_uq", "mla_w_ukv", "mla_w_out", "ffn_w_gate", "ffn_w_up",
       "ffn_w_down")
TRANSPOSED = ("gdn_w_in", "mla_w_uq", "ffn_w_gate", "ffn_w_up")
AHEAD = 2


def _view(k, a):
    return jnp.transpose(a, (0, 2, 1)) if k in TRANSPOSED else a
SMALL = ("ada_b", "norm_mix_g", "norm_ffn_g", "gdn_conv_w", "gdn_a_log", "gdn_dt_bias", "gdn_norm_g",
         "mla_q_norm_g", "mla_kv_norm_g", "final_norm_g")
WEIGHTS = ("ada_w", "ada_b", "norm_mix_g", "norm_ffn_g", "gdn_w_in", "gdn_conv_w", "gdn_a_log", "gdn_dt_bias",
           "gdn_norm_g", "gdn_w_out", "mla_w_in", "mla_q_norm_g", "mla_kv_norm_g", "mla_w_uq", "mla_w_ukv",
           "mla_w_out", "ffn_w_gate", "ffn_w_up", "ffn_w_down", "final_norm_g")


def _uq_to_kernel_layout(w, axis=-1):
    axis = axis % w.ndim
    lead, tail = w.shape[:axis], w.shape[axis + 1:]
    w4 = w.reshape(lead + (MLA_HEADS, MLA_QK) + tail)
    nope = lax.slice_in_dim(w4, 0, MLA_NOPE, axis=axis + 1).reshape(lead + (-1,) + tail)
    rope = lax.slice_in_dim(w4, MLA_NOPE, MLA_QK, axis=axis + 1).reshape(lead + (-1,) + tail)
    return jnp.concatenate([nope, rope], axis=axis)


def _uq_from_kernel_layout(w, axis=-1):
    axis = axis % w.ndim
    lead, tail = w.shape[:axis], w.shape[axis + 1:]
    nope = lax.slice_in_dim(w, 0, MLA_HEADS * MLA_NOPE, axis=axis).reshape(lead + (MLA_HEADS, MLA_NOPE) + tail)
    rope = lax.slice_in_dim(w, MLA_HEADS * MLA_NOPE, MLA_HEADS * MLA_QK, axis=axis).reshape(
        lead + (MLA_HEADS, MLA_ROPE) + tail)
    return jnp.concatenate([nope, rope], axis=axis + 1).reshape(lead + (-1,) + tail)


def _group_names(layer, kind):
    if kind == "ffn":
        return ("ffn_w_gate", "ffn_w_up", "ffn_w_down")
    return ("gdn_w_in", "gdn_w_out") if layer % 2 == 0 else ("mla_w_in", "mla_w_uq", "mla_w_ukv", "mla_w_out")


def _layer_index(name, layer):
    return layer if name.startswith("ffn") else layer // 2


def _cols(g):
    return jnp.transpose(g, (1, 0, 2)).reshape(g.shape[1], N_DEV * g.shape[2])


def _rows(g):
    return g.reshape(N_DEV * g.shape[1], g.shape[2])


def _uncols(full):
    r, c = full.shape
    return jnp.transpose(full.reshape(r, N_DEV, c // N_DEV), (1, 0, 2))


def _unrows(full):
    r, c = full.shape
    return full.reshape(N_DEV, r // N_DEV, c)


def _group_weights(layer, kind, got, token):
    if kind == "ffn":
        return {"wt_g": _rows(got["ffn_w_gate"]), "wt_u": _rows(got["ffn_w_up"]), "w_down": _rows(got["ffn_w_down"]),
                "dep_ffn": token}
    if layer % 2 == 0:
        wt_in = _rows(got["gdn_w_in"])
        return dict(wt_in=wt_in, wt_ab=jnp.pad(wt_in[GDN_MAIN:], ((0, LANES - 2 * GDN_HEADS), (0, 0))),
                    w_out=_rows(got["gdn_w_out"]), dep_mix=token)
    return dict(w_in=_rows(got["mla_w_in"]), wt_uq=_uq_to_kernel_layout(_rows(got["mla_w_uq"]), axis=0),
                w_ukv=_cols(got["mla_w_ukv"]), w_out=_rows(got["mla_w_out"]), dep_mix=token)


def _layer_grad_slots(kind, big):
    if kind == "ffn":
        return {"ffn_w_gate": _unrows(big["wt_g"]), "ffn_w_up": _unrows(big["wt_u"]),
                "ffn_w_down": _unrows(big["w_down"])}
    if kind == "gdn":
        return {"gdn_w_in": _unrows(big["wt_in"]), "gdn_w_out": _unrows(big["w_out"])}
    return {"mla_w_in": _unrows(big["w_in"]), "mla_w_uq": _unrows(_uq_from_kernel_layout(big["wt_uq"], axis=0)),
            "mla_w_ukv": _uncols(big["w_ukv"]), "mla_w_out": _unrows(big["w_out"])}


def _small_weights(tiny, rep):
    prm = jnp.zeros((2, 8, LANES), F32)
    prm = prm.at[:, 0, :GDN_HEADS].set(rep["gdn_a_log"]).at[:, 1, :GDN_HEADS].set(rep["gdn_dt_bias"])
    out = {
        "gdn_conv_wt": jnp.transpose(_gather_rows(tiny["gdn_conv_w"]), (0, 2, 1)),
        "mla_q_norm_g": jnp.transpose(tiny["mla_q_norm_g"], (1, 0, 2)).reshape(2, MLA_Q_RANK),
        "mla_kv_norm_g": jnp.transpose(tiny["mla_kv_norm_g"], (1, 0, 2)).reshape(2, MLA_KV_RANK),
        "gdn_gate_prm": prm,
    }
    for k in ("norm_mix_g", "norm_ffn_g", "gdn_norm_g", "final_norm_g"):
        out[k] = rep[k]
    return out


def _rope_tables(positions):
    inv_freq = ROPE_THETA ** (-jnp.arange(0, MLA_ROPE, 2, dtype=F32) / MLA_ROPE)
    ang = positions.astype(F32)[:, None] * inv_freq
    cos, sin = jnp.cos(ang), jnp.sin(ang)
    reps = LANES // MLA_ROPE
    return jnp.tile(jnp.concatenate([cos, cos], axis=1), (1, reps)), jnp.tile(
        jnp.concatenate([-sin, sin], axis=1), (1, reps))


def kernel(x, c, positions, ada_w, ada_b, norm_mix_g, norm_ffn_g, gdn_w_in, gdn_conv_w, gdn_a_log, gdn_dt_bias, gdn_norm_g, gdn_w_out, mla_w_in, mla_q_norm_g, mla_kv_norm_g, mla_w_uq, mla_w_ukv, mla_w_out, ffn_w_gate, ffn_w_up, ffn_w_down, final_norm_g, loss_target, m_ada_w, m_ada_b, m_norm_mix_g, m_norm_ffn_g, m_gdn_w_in, m_gdn_conv_w, m_gdn_a_log, m_gdn_dt_bias, m_gdn_norm_g, m_gdn_w_out, m_mla_w_in, m_mla_q_norm_g, m_mla_kv_norm_g, m_mla_w_uq, m_mla_w_ukv, m_mla_w_out, m_ffn_w_gate, m_ffn_w_up, m_ffn_w_down, m_final_norm_g, v_ada_w, v_ada_b, v_norm_mix_g, v_norm_ffn_g, v_gdn_w_in, v_gdn_conv_w, v_gdn_a_log, v_gdn_dt_bias, v_gdn_norm_g, v_gdn_w_out, v_mla_w_in, v_mla_q_norm_g, v_mla_kv_norm_g, v_mla_w_uq, v_mla_w_ukv, v_mla_w_out, v_ffn_w_gate, v_ffn_w_up, v_ffn_w_down, v_final_norm_g):
    W = dict(ada_w=ada_w, ada_b=ada_b, norm_mix_g=norm_mix_g, norm_ffn_g=norm_ffn_g, gdn_w_in=gdn_w_in,
             gdn_conv_w=gdn_conv_w, gdn_a_log=gdn_a_log, gdn_dt_bias=gdn_dt_bias, gdn_norm_g=gdn_norm_g,
             gdn_w_out=gdn_w_out, mla_w_in=mla_w_in, mla_q_norm_g=mla_q_norm_g, mla_kv_norm_g=mla_kv_norm_g,
             mla_w_uq=mla_w_uq, mla_w_ukv=mla_w_ukv, mla_w_out=mla_w_out, ffn_w_gate=ffn_w_gate,
             ffn_w_up=ffn_w_up, ffn_w_down=ffn_w_down, final_norm_g=final_norm_g)
    M = dict(ada_w=m_ada_w, ada_b=m_ada_b, norm_mix_g=m_norm_mix_g, norm_ffn_g=m_norm_ffn_g, gdn_w_in=m_gdn_w_in,
             gdn_conv_w=m_gdn_conv_w, gdn_a_log=m_gdn_a_log, gdn_dt_bias=m_gdn_dt_bias, gdn_norm_g=m_gdn_norm_g,
             gdn_w_out=m_gdn_w_out, mla_w_in=m_mla_w_in, mla_q_norm_g=m_mla_q_norm_g,
             mla_kv_norm_g=m_mla_kv_norm_g, mla_w_uq=m_mla_w_uq, mla_w_ukv=m_mla_w_ukv, mla_w_out=m_mla_w_out,
             ffn_w_gate=m_ffn_w_gate, ffn_w_up=m_ffn_w_up, ffn_w_down=m_ffn_w_down, final_norm_g=m_final_norm_g)
    V = dict(ada_w=v_ada_w, ada_b=v_ada_b, norm_mix_g=v_norm_mix_g, norm_ffn_g=v_norm_ffn_g, gdn_w_in=v_gdn_w_in,
             gdn_conv_w=v_gdn_conv_w, gdn_a_log=v_gdn_a_log, gdn_dt_bias=v_gdn_dt_bias, gdn_norm_g=v_gdn_norm_g,
             gdn_w_out=v_gdn_w_out, mla_w_in=v_mla_w_in, mla_q_norm_g=v_mla_q_norm_g,
             mla_kv_norm_g=v_mla_kv_norm_g, mla_w_uq=v_mla_w_uq, mla_w_ukv=v_mla_w_ukv, mla_w_out=v_mla_w_out,
             ffn_w_gate=v_ffn_w_gate, ffn_w_up=v_ffn_w_up, ffn_w_down=v_ffn_w_down, final_norm_g=v_final_norm_g)
    me = 4 * lax.axis_index("x") + 2 * lax.axis_index("y") + lax.axis_index("c")
    t = x.shape[1]
    wc = ada_w.shape[-1]

    groups = [(layer, kind) for layer in range(DEPTH) for kind in ("mix", "ffn")]

    def group_srcs(i):
        layer, kind = groups[i]
        return [_view(k, W[k])[_layer_index(k, layer)].astype(BF16) for k in _group_names(layer, kind)]

    tiny_shapes = [c.shape, gdn_conv_w.shape, mla_q_norm_g.shape, mla_kv_norm_g.shape]
    first = _gather_two_level([_pack([c, gdn_conv_w, mla_q_norm_g, mla_kv_norm_g])] + group_srcs(0),
                              name="gather_first")
    tiny_g = first[0]
    c_g, conv_g, qn_g, kvn_g = _unpack(tiny_g, tiny_shapes, lead=(N_DEV,))
    c_all = c_g.reshape(N_DEV, D_MODEL)
    rep = _small_weights({"gdn_conv_w": conv_g, "mla_q_norm_g": qn_g, "mla_kv_norm_g": kvn_g}, W)

    def start_group(i, dep):
        layer, kind = groups[i]
        return _exchange_start(group_srcs(i), scatter=False, name=f"gather_start_{kind}_l{layer}", dep=dep)


    b_cols = lax.dynamic_slice_in_dim(ada_b, me * wc, wc, axis=1).reshape(DEPTH, 1, wc)
    mod_part = _ada_mod(c_all, ada_w, b_cols, name="ada_mod")
    (mod_g,) = _exchange([mod_part], scatter=False, name="gather_mod")
    mod_mine = lax.dynamic_index_in_dim(mod_g, me, axis=2, keepdims=False)
    mod = jnp.transpose(mod_mine, (1, 0, 2)).reshape(DEPTH, N_MOD, D_MODEL)
    gather = {1: start_group(1, mod_g)}
    for i in range(2, AHEAD + 1):
        gather[i] = start_group(i, gather[i - 1][4])

    def get_weights(layer, kind, after):
        i = groups.index((layer, kind))
        names = _group_names(layer, kind)
        if i == 0:
            return _group_weights(layer, kind, dict(zip(names, first[1:])), gather[AHEAD][4])
        srcs, lands = _exchange_wait(gather[i], after, scatter=False, name=f"gather_wait_{kind}_l{layer}")
        token = jnp.zeros((8, LANES), F32)
        if i + AHEAD < len(groups):
            gather[i + AHEAD] = start_group(i + AHEAD, lands[0])
            token = gather[i + AHEAD][4]
        got = {k: lax.dynamic_update_index_in_dim(z, s, me, 0) for k, s, z in zip(names, srcs, lands)}
        return _group_weights(layer, kind, got, token)

    scatter = []

    def put_grads(layer, kind, big):
        slots = _layer_grad_slots(kind, big)
        started = _exchange_start(list(slots.values()), scatter=True, name=f"scatter_start_{kind}_l{layer}")
        scatter.append((layer, kind, list(slots.keys()), started))
        return started[4]

    cos_t, sin_t = _rope_tables(positions[0])
    loss, dx, dmod, g = _local_step(x[0], loss_target[0], mod, cos_t, sin_t, rep, get_weights, put_grads)

    parts = {k: [None] * W[k].shape[0] for k in BIG}
    res = {}

    def wait_group(entry, after):
        layer, kind, names, started = entry
        srcs, lands = _exchange_wait(started, after, scatter=True, name=f"scatter_wait_{kind}_l{layer}")
        for k, s, z in zip(names, srcs, lands):
            own = lax.dynamic_index_in_dim(s, me, 0, keepdims=False)
            parts[k][_layer_index(k, layer)] = lax.dynamic_update_index_in_dim(z, own, me, 0)

    for entry in scatter[:-1]:
        wait_group(entry, dx)
    early = [k for k in BIG if k not in scatter[-1][2]]
    def update(k):
        outs = _adamw(parts[k], _view(k, W[k]), _view(k, M[k]), _view(k, V[k]), name=f"adamw_{k}")
        return tuple(_view(k, o) for o in outs)

    for k in early:
        res[k] = update(k)
    loss, dmod, done = lax.optimization_barrier((loss, dmod, [res[k] for k in early]))
    for k, r in zip(early, done):
        res[k] = r

    small_local = [dmod.reshape(DEPTH, N_MOD * D_MODEL), g["norm_mix_g"], g["norm_ffn_g"],
                   jnp.transpose(g["gdn_conv_wt"], (0, 2, 1)), g["gdn_a_log"], g["gdn_dt_bias"], g["gdn_norm_g"],
                   g["mla_q_norm_g"], g["mla_kv_norm_g"], g["final_norm_g"], loss.reshape(1)]
    small_shapes = [a.shape for a in small_local]
    (small_g,) = _exchange([_pack(small_local)], scatter=False, name="gather_small_grads")
    small_sum = _unpack(_sum_parts(small_g, name="sum_small_grads"), small_shapes)
    loss = small_sum[-1][0]
    dmod_all = _unpack(small_g, small_shapes[:1], lead=(N_DEV,))[0]
    sg = dict(zip(SMALL, small_sum))
    wait_group(scatter[-1], small_g)
    sg["gdn_conv_w"] = lax.dynamic_slice_in_dim(sg["gdn_conv_w"], me * gdn_conv_w.shape[1], gdn_conv_w.shape[1], 1)
    sg["mla_q_norm_g"] = lax.dynamic_slice_in_dim(sg["mla_q_norm_g"], me * mla_q_norm_g.shape[1],
                                                  mla_q_norm_g.shape[1], 1)
    sg["mla_kv_norm_g"] = lax.dynamic_slice_in_dim(sg["mla_kv_norm_g"], me * mla_kv_norm_g.shape[1],
                                                   mla_kv_norm_g.shape[1], 1)

    dmod_cols = jnp.transpose(lax.dynamic_slice_in_dim(dmod_all, me * wc, wc, axis=2), (1, 0, 2))
    res["ada_w"] = _ada_grad_adamw(c_all, dmod_cols, ada_w, m_ada_w, v_ada_w, name="ada_w_grad_adamw")
    for k in BIG:
        if k not in early:
            res[k] = update(k)
    shapes = [W[k].shape for k in SMALL]
    packed = [_pack([d[k] for k in SMALL]) for d in (sg, W, M, V)]
    outs = _adamw([packed[0][None]], packed[1][None], packed[2][None], packed[3][None], name="adamw_small")
    unpacked = [_unpack(o[0], shapes) for o in outs]
    for i, k in enumerate(SMALL):
        res[k] = tuple(u[i] for u in unpacked)

    return (loss, dx[None], *[res[k][0] for k in WEIGHTS], *[res[k][1] for k in WEIGHTS],
            *[res[k][2] for k in WEIGHTS], *[res[k][3] for k in WEIGHTS])
```

```python
import math

import jax
import jax.numpy as jnp
from jax import lax
from jax.experimental import pallas as pl
from jax.experimental.pallas import tpu as pltpu

F32 = jnp.float32
BF16 = jnp.bfloat16
MXU_DTYPE = jnp.bfloat16

N_DEV = 8
D_MODEL = 1024
DEPTH = 4
GDN_HEADS = 8
GDN_HEAD_DIM = 128
GDN_KEY_DIM = GDN_HEADS * GDN_HEAD_DIM
GDN_CHUNK = 64
GDN_HEAD_BATCH = 8
GDN_CONV = 4
GDN_PREP_HEADS = 2
GDN_MAIN = 4 * GDN_KEY_DIM
MLA_HEADS = 8
MLA_NOPE = 128
MLA_ROPE = 64
MLA_V = 128
MLA_Q_RANK = 384
MLA_KV_RANK = 256
MLA_IN = MLA_Q_RANK + MLA_KV_RANK + MLA_ROPE
MLA_QK = MLA_NOPE + MLA_ROPE
ROPE_THETA = 10000.0
D_FF = 2816
N_MOD = 6
EPS = 1e-6
LANES = 128
VMEM_LIMIT = 48 * 1024 * 1024

ADAM_LR = 0.001
ADAM_B1 = 0.9
ADAM_B2 = 0.999
ADAM_EPS = 1e-08
ADAM_WD = 0.01
ADAM_STEP = 10
ADAM_BC1 = 1.0 - ADAM_B1 ** ADAM_STEP
ADAM_BC2 = 1.0 - ADAM_B2 ** ADAM_STEP

NN = (((1,), (0,)), ((), ()))
NT = (((1,), (1,)), ((), ()))
TN = (((0,), (0,)), ((), ()))
NEG = -1e30


def _dotb(a, b, dims):
    return lax.dot_general(a.astype(MXU_DTYPE), b.astype(MXU_DTYPE), dims, preferred_element_type=F32)


def _split(a):
    hi = a.astype(BF16)
    return hi, (a - hi.astype(F32)).astype(BF16)


def _dotf(a, b, dims):
    ah, al = _split(a)
    bh, bl = _split(b)
    dot = lambda u, v: lax.dot_general(u, v, dims, preferred_element_type=F32)
    return dot(ah, bh) + (dot(ah, bl) + dot(al, bh))


def _params(*sem):
    return pltpu.CompilerParams(dimension_semantics=sem, vmem_limit_bytes=VMEM_LIMIT)


def _pick(n, pref, mult=LANES):
    best = None
    t = mult
    while t <= min(n, pref):
        if n % t == 0:
            best = t
        t += mult
    return best if best is not None else n


def _sigmoid(z):
    return 1.0 / (1.0 + jnp.exp(-z))


def _exchange(arrays, *, scatter, name):
    n = len(arrays)
    out_shape = tuple(
        jax.ShapeDtypeStruct(a.shape if scatter else (N_DEV,) + a.shape, a.dtype) for a in arrays)

    def body(*refs):
        ins, outs = refs[:n], refs[n:2 * n]
        send_sems, recv_sems, local_sems = refs[2 * n:]
        x, y, c = lax.axis_index("x"), lax.axis_index("y"), lax.axis_index("c")
        me = 4 * x + 2 * y + c
        copies = []
        for k in range(n):
            src_own = ins[k].at[me] if scatter else ins[k]
            own = pltpu.make_async_copy(src_own, outs[k].at[me], local_sems.at[k])
            own.start()
            copies.append(own)
        sends = []
        for p in range(1, N_DEV):
            px, py, pc = x ^ ((p >> 2) & 1), y ^ ((p >> 1) & 1), c ^ (p & 1)
            peer = 4 * px + 2 * py + pc
            for k in range(n):
                cp = pltpu.make_async_remote_copy(
                    src_ref=ins[k].at[peer] if scatter else ins[k],
                    dst_ref=outs[k].at[me],
                    send_sem=send_sems.at[k, p - 1],
                    recv_sem=recv_sems.at[k, p - 1],
                    device_id=(px, py, pc),
                    device_id_type=pl.DeviceIdType.MESH,
                )
                cp.start()
                sends.append((cp, k, peer, p))
        for cp, k, peer, p in sends:
            pltpu.make_async_remote_copy(
                src_ref=ins[k].at[peer] if scatter else ins[k],
                dst_ref=outs[k].at[peer],
                send_sem=send_sems.at[k, p - 1],
                recv_sem=recv_sems.at[k, p - 1],
                device_id=(x, y, c),
                device_id_type=pl.DeviceIdType.MESH,
            ).wait_recv()
        for cp, _, _, _ in sends:
            cp.wait_send()
        for own in copies:
            own.wait()

    any_spec = pl.BlockSpec(memory_space=pl.ANY)
    outs = pl.pallas_call(
        body,
        name=name,
        out_shape=out_shape,
        in_specs=[any_spec] * n,
        out_specs=tuple([any_spec] * n),
        scratch_shapes=[
            pltpu.SemaphoreType.DMA((n, N_DEV - 1)),
            pltpu.SemaphoreType.DMA((n, N_DEV - 1)),
            pltpu.SemaphoreType.DMA((n,)),
        ],
        compiler_params=pltpu.CompilerParams(has_side_effects=True),
    )(*arrays)
    return list(outs)


def _gather_two_level(arrays, *, name):
    n = len(arrays)
    out_shape = tuple(jax.ShapeDtypeStruct((N_DEV,) + a.shape, a.dtype) for a in arrays)

    def body(*refs):
        ins, outs = refs[:n], refs[n:2 * n]
        send_sems, recv_sems, local_sems = refs[2 * n:]
        x, y, c = lax.axis_index("x"), lax.axis_index("y"), lax.axis_index("c")
        me = 4 * x + 2 * y + c
        sibling = (x, y, 1 - c)
        chips = [(1 - x, y), (x, 1 - y), (1 - x, 1 - y)]

        def slot(px, py, pc):
            return 4 * px + 2 * py + pc

        def copy(k, q, block, to, src=None):
            return pltpu.make_async_remote_copy(
                src_ref=outs[k].at[slot(*block)] if src is None else src,
                dst_ref=outs[k].at[slot(*block)],
                send_sem=send_sems.at[k, q], recv_sem=recv_sems.at[k, q],
                device_id=to, device_id_type=pl.DeviceIdType.MESH)

        own = [pltpu.make_async_copy(ins[k], outs[k].at[me], local_sems.at[k]) for k in range(n)]
        for cp in own:
            cp.start()
        first = []
        for k in range(n):
            first.append(copy(k, 0, (x, y, c), sibling, src=ins[k]))
            first += [copy(k, 1 + j, (x, y, c), (*chip, c), src=ins[k]) for j, chip in enumerate(chips)]
        for cp in first:
            cp.start()
        passed = []
        for j, chip in enumerate(chips):
            for k in range(n):
                copy(k, 1 + j, (*chip, c), (x, y, c)).wait_recv()
                fwd = copy(k, 4 + j, (*chip, c), sibling)
                fwd.start()
                passed.append(fwd)
        for k in range(n):
            copy(k, 0, sibling, (x, y, c)).wait_recv()
            for j, chip in enumerate(chips):
                copy(k, 4 + j, (*chip, 1 - c), (x, y, c)).wait_recv()
        for cp in first + passed:
            cp.wait_send()
        for cp in own:
            cp.wait()

    any_spec = pl.BlockSpec(memory_space=pl.ANY)
    outs = pl.pallas_call(
        body, name=name, out_shape=out_shape, in_specs=[any_spec] * n, out_specs=tuple([any_spec] * n),
        scratch_shapes=[pltpu.SemaphoreType.DMA((n, N_DEV - 1)), pltpu.SemaphoreType.DMA((n, N_DEV - 1)),
                        pltpu.SemaphoreType.DMA((n,))],
        compiler_params=pltpu.CompilerParams(has_side_effects=True),
    )(*arrays)
    return list(outs)


def _peer(x, y, c, p):
    return x ^ ((p >> 2) & 1), y ^ ((p >> 1) & 1), c ^ (p & 1)


def _exchange_start(arrays, *, scatter, name, dep=None):
    n = len(arrays)
    deps = [] if dep is None else [dep]
    lands = [lax.empty(a.shape if scatter else (N_DEV,) + a.shape, a.dtype) for a in arrays]

    def body(*refs):
        ins, zones = refs[:n], refs[n:2 * n]
        send_sems, recv_sems = refs[2 * n + len(deps)], refs[2 * n + len(deps) + 1]
        token = refs[-1]
        x, y, c = lax.axis_index("x"), lax.axis_index("y"), lax.axis_index("c")
        me = 4 * x + 2 * y + c
        for p in range(1, N_DEV):
            px, py, pc = _peer(x, y, c, p)
            for k in range(n):
                pltpu.make_async_remote_copy(
                    src_ref=ins[k].at[4 * px + 2 * py + pc] if scatter else ins[k],
                    dst_ref=zones[k].at[me],
                    send_sem=send_sems.at[k * (N_DEV - 1) + p - 1],
                    recv_sem=recv_sems.at[k * (N_DEV - 1) + p - 1],
                    device_id=(px, py, pc),
                    device_id_type=pl.DeviceIdType.MESH,
                ).start()
        token[...] = jnp.zeros_like(token)

    hbm = pl.BlockSpec(memory_space=pltpu.HBM)
    sem = pl.BlockSpec(memory_space=pltpu.SEMAPHORE)
    outs = pl.pallas_call(
        body,
        name=name,
        out_shape=(pltpu.SemaphoreType.DMA((n * (N_DEV - 1),)), pltpu.SemaphoreType.DMA((n * (N_DEV - 1),)),
                   *[pltpu.HBM(a.shape, a.dtype) for a in arrays], *[pltpu.HBM(z.shape, z.dtype) for z in lands],
                   jax.ShapeDtypeStruct((8, LANES), F32)),
        in_specs=[hbm] * (2 * n) + [pl.BlockSpec(memory_space=pl.ANY)] * len(deps),
        out_specs=(sem, sem, *[hbm] * (2 * n), pl.BlockSpec(memory_space=pltpu.VMEM)),
        input_output_aliases={k: 2 + k for k in range(2 * n)},
        compiler_params=pltpu.CompilerParams(has_side_effects=pltpu.SideEffectType.DATAFLOW_SIDE_EFFECTING),
    )(*[pltpu.with_memory_space_constraint(a, pltpu.HBM) for a in arrays],
      *[pltpu.with_memory_space_constraint(z, pltpu.HBM) for z in lands], *deps)
    return outs[0], outs[1], list(outs[2:2 + n]), list(outs[2 + n:2 + 2 * n]), outs[-1]


def _exchange_wait(started, after, *, scatter, name):
    send_sems, recv_sems, srcs, lands, _ = started
    n = len(srcs)

    def body(*refs):
        ins, zones = refs[:n], refs[n:2 * n]
        s_sems, r_sems = refs[2 * n], refs[2 * n + 1]
        x, y, c = lax.axis_index("x"), lax.axis_index("y"), lax.axis_index("c")
        for p in range(1, N_DEV):
            px, py, pc = _peer(x, y, c, p)
            peer = 4 * px + 2 * py + pc
            for k in range(n):
                cp = pltpu.make_async_remote_copy(
                    src_ref=ins[k].at[peer] if scatter else ins[k],
                    dst_ref=zones[k].at[peer],
                    send_sem=s_sems.at[k * (N_DEV - 1) + p - 1],
                    recv_sem=r_sems.at[k * (N_DEV - 1) + p - 1],
                    device_id=(px, py, pc),
                    device_id_type=pl.DeviceIdType.MESH,
                )
                cp.wait_send()
                cp.wait_recv()

    hbm = pl.BlockSpec(memory_space=pltpu.HBM)
    sem = pl.BlockSpec(memory_space=pltpu.SEMAPHORE)
    outs = pl.pallas_call(
        body,
        name=name,
        out_shape=tuple(pltpu.HBM(a.shape, a.dtype) for a in srcs + lands),
        in_specs=[hbm] * (2 * n) + [sem, sem, pl.BlockSpec(memory_space=pl.ANY)],
        out_specs=tuple([hbm] * (2 * n)),
        input_output_aliases={k: k for k in range(2 * n)},
        compiler_params=pltpu.CompilerParams(has_side_effects=pltpu.SideEffectType.DATAFLOW_SIDE_EFFECTING),
    )(*srcs, *lands, send_sems, recv_sems, after)
    return list(outs[:n]), list(outs[n:])


def _mm(a, b, *, mode, out_dtype, name, add=None, tm=512, tn=512, b_rows=None, dep=None):
    rows_b = b.shape[0] if b_rows is None else b_rows
    if mode == "nn":
        (m, kd), nd = a.shape, b.shape[1]
        assert kd == rows_b
    elif mode == "nt":
        (m, kd), nd = a.shape, rows_b
    else:
        (kd, m), nd = a.shape, b.shape[1]
    tm = _pick(m, tm, LANES if mode == "tn" else 16)
    tn = _pick(nd, tn)
    dims = {"nn": NN, "nt": NT, "tn": TN}[mode]
    ni, nj = m // tm, nd // tn
    a_bytes, b_bytes = a.size * a.dtype.itemsize, b.size * b.dtype.itemsize
    i_outer = a_bytes + ni * b_bytes <= b_bytes + nj * a_bytes
    ij = (lambda g0, g1: (g0, g1)) if i_outer else (lambda g0, g1: (g1, g0))
    a_spec = (pl.BlockSpec((kd, tm), lambda g0, g1: (0, ij(g0, g1)[0])) if mode == "tn"
              else pl.BlockSpec((tm, kd), lambda g0, g1: (ij(g0, g1)[0], 0)))
    b_spec = (pl.BlockSpec((tn, kd), lambda g0, g1: (ij(g0, g1)[1], 0)) if mode == "nt"
              else pl.BlockSpec((kd, tn), lambda g0, g1: (0, ij(g0, g1)[1])))
    o_spec = pl.BlockSpec((tm, tn), lambda g0, g1: ij(g0, g1))
    has_add = add is not None

    def body(*refs):
        a_ref, b_ref = refs[0], refs[1]
        o_ref = refs[-1]
        acc = _dotb(a_ref[...], b_ref[...], dims)
        if has_add:
            acc = acc + refs[2][...].astype(F32)
        o_ref[...] = acc.astype(o_ref.dtype)

    ins = [a, b] + ([add] if has_add else []) + ([] if dep is None else [dep])
    specs = ([a_spec, b_spec] + ([o_spec] if has_add else [])
             + ([] if dep is None else [pl.BlockSpec((8, LANES), lambda g0, g1: (0, 0))]))
    return pl.pallas_call(
        body, name=name, grid=(ni, nj) if i_outer else (nj, ni), in_specs=specs, out_specs=o_spec,
        out_shape=jax.ShapeDtypeStruct((m, nd), out_dtype),
        compiler_params=_params("parallel", "parallel"),
    )(*ins)


def _mm_resid(a, b, x, gate, *, name, tm=256, tn=1024):
    m, kd = a.shape
    nd = b.shape[1]
    tm = _pick(m, tm, 16)
    tn = _pick(nd, tn)
    o_spec = pl.BlockSpec((tm, tn), lambda i, j: (i, j))

    def body(a_ref, b_ref, x_ref, g_ref, xo_ref, y_ref):
        y = _dotb(a_ref[...], b_ref[...], NN)
        y_ref[...] = y.astype(y_ref.dtype)
        xo_ref[...] = x_ref[...] + g_ref[...] * y

    return pl.pallas_call(
        body, name=name, grid=(m // tm, nd // tn),
        in_specs=[pl.BlockSpec((tm, kd), lambda i, j: (i, 0)), pl.BlockSpec((kd, tn), lambda i, j: (0, j)),
                  o_spec, pl.BlockSpec((1, tn), lambda i, j: (0, j))],
        out_specs=(o_spec, o_spec),
        out_shape=(jax.ShapeDtypeStruct((m, nd), F32), jax.ShapeDtypeStruct((m, nd), BF16)),
        compiler_params=_params("parallel", "parallel"),
    )(a, b, x, gate)


ROWS = 256


def _row_spec(width, rows=ROWS):
    return pl.BlockSpec((rows, width), lambda i: (i, 0))


def _const_spec(shape):
    return pl.BlockSpec(shape, lambda i: tuple(0 for _ in shape))


def _adaln_fwd(x, g, scale, shift, *, name):
    t, d = x.shape

    def body(x_ref, g_ref, sc_ref, sh_ref, h_ref):
        xv = x_ref[...]
        r = lax.rsqrt(jnp.mean(xv * xv, axis=-1, keepdims=True) + EPS)
        h_ref[...] = (xv * r * g_ref[...] * (1.0 + sc_ref[...]) + sh_ref[...]).astype(h_ref.dtype)

    return pl.pallas_call(
        body, name=name, grid=(t // ROWS,),
        in_specs=[_row_spec(d), _const_spec((1, d)), _const_spec((1, d)), _const_spec((1, d))],
        out_specs=_row_spec(d), out_shape=jax.ShapeDtypeStruct((t, d), BF16),
        compiler_params=_params("parallel"),
    )(x, g, scale, shift)


def _adaln_bwd(x, g, scale, shift, dh, dres, dep, *, name):
    t, d = x.shape

    def body(x_ref, g_ref, sc_ref, sh_ref, dh_ref, dr_ref, dep_ref, dx_ref, st_ref):
        @pl.when(pl.program_id(0) == 0)
        def _():
            st_ref[...] = jnp.zeros_like(st_ref)

        xv = x_ref[...]
        dhv = dh_ref[...].astype(F32)
        gv = g_ref[...]
        r = lax.rsqrt(jnp.mean(xv * xv, axis=-1, keepdims=True) + EPS)
        xh = xv * r
        nv = xh * gv
        dn = dhv * (1.0 + sc_ref[...])
        dxh = dn * gv
        dx_ref[...] = dr_ref[...] + r * (dxh - xh * jnp.mean(dxh * xh, axis=-1, keepdims=True))
        st_ref[0:1, :] += jnp.sum(dn * xh, axis=0, keepdims=True)
        st_ref[1:2, :] += jnp.sum(dhv * nv, axis=0, keepdims=True)
        st_ref[2:3, :] += jnp.sum(dhv, axis=0, keepdims=True)

    return pl.pallas_call(
        body, name=name, grid=(t // ROWS,),
        in_specs=[_row_spec(d), _const_spec((1, d)), _const_spec((1, d)), _const_spec((1, d)),
                  _row_spec(d), _row_spec(d), _const_spec((8, LANES))],
        out_specs=(_row_spec(d), _const_spec((8, d))),
        out_shape=(jax.ShapeDtypeStruct((t, d), F32), jax.ShapeDtypeStruct((8, d), F32)),
        compiler_params=_params("arbitrary"),
    )(x, g, scale, shift, dh, dres, dep)


def _adaln_gate_bwd(x, g, scale, shift, dh, dres, dep, y_up, gate_up, *, name):
    t, d = x.shape

    def body(x_ref, g_ref, sc_ref, sh_ref, dh_ref, dr_ref, dep_ref, y_ref, gu_ref, dx_ref, st_ref, dy_ref):
        @pl.when(pl.program_id(0) == 0)
        def _():
            st_ref[...] = jnp.zeros_like(st_ref)

        xv = x_ref[...]
        dhv = dh_ref[...].astype(F32)
        gv = g_ref[...]
        r = lax.rsqrt(jnp.mean(xv * xv, axis=-1, keepdims=True) + EPS)
        xh = xv * r
        nv = xh * gv
        dn = dhv * (1.0 + sc_ref[...])
        dxh = dn * gv
        dx = dr_ref[...] + r * (dxh - xh * jnp.mean(dxh * xh, axis=-1, keepdims=True))
        dx_ref[...] = dx
        dy_ref[...] = (dx * gu_ref[...]).astype(dy_ref.dtype)
        st_ref[0:1, :] += jnp.sum(dn * xh, axis=0, keepdims=True)
        st_ref[1:2, :] += jnp.sum(dhv * nv, axis=0, keepdims=True)
        st_ref[2:3, :] += jnp.sum(dhv, axis=0, keepdims=True)
        st_ref[3:4, :] += jnp.sum(dx * y_ref[...].astype(F32), axis=0, keepdims=True)

    return pl.pallas_call(
        body, name=name, grid=(t // ROWS,),
        in_specs=[_row_spec(d), _const_spec((1, d)), _const_spec((1, d)), _const_spec((1, d)),
                  _row_spec(d), _row_spec(d), _const_spec((8, LANES)), _row_spec(d), _const_spec((1, d))],
        out_specs=(_row_spec(d), _const_spec((8, d)), _row_spec(d)),
        out_shape=(jax.ShapeDtypeStruct((t, d), F32), jax.ShapeDtypeStruct((8, d), F32),
                   jax.ShapeDtypeStruct((t, d), BF16)),
        compiler_params=_params("arbitrary"),
    )(x, g, scale, shift, dh, dres, dep, y_up, gate_up)


def _gate_bwd(dxo, y, gate, dep, *, name):
    t, d = dxo.shape

    def body(dx_ref, y_ref, g_ref, dep_ref, dy_ref, st_ref):
        @pl.when(pl.program_id(0) == 0)
        def _():
            st_ref[...] = jnp.zeros_like(st_ref)

        dxv = dx_ref[...]
        dy_ref[...] = (dxv * g_ref[...]).astype(dy_ref.dtype)
        st_ref[0:1, :] += jnp.sum(dxv * y_ref[...], axis=0, keepdims=True)

    return pl.pallas_call(
        body, name=name, grid=(t // ROWS,),
        in_specs=[_row_spec(d), _row_spec(d), _const_spec((1, d)), _const_spec((8, LANES))],
        out_specs=(_row_spec(d), _const_spec((8, d))),
        out_shape=(jax.ShapeDtypeStruct((t, d), BF16), jax.ShapeDtypeStruct((8, d), F32)),
        compiler_params=_params("arbitrary"),
    )(dxo, y, gate, dep)


def _loss_head(x, g, target, *, name):
    t, d = x.shape

    def body(x_ref, g_ref, t_ref, dx_ref, st_ref, ls_ref):
        @pl.when(pl.program_id(0) == 0)
        def _():
            st_ref[...] = jnp.zeros_like(st_ref)
            ls_ref[...] = jnp.zeros_like(ls_ref)

        xv = x_ref[...]
        gv = g_ref[...]
        r = lax.rsqrt(jnp.mean(xv * xv, axis=-1, keepdims=True) + EPS)
        xh = xv * r
        err = xh * gv - t_ref[...]
        ls_ref[...] += 0.5 * jnp.sum(jnp.mean(err * err, axis=-1, keepdims=True))
        dy = err * (1.0 / d)
        dxh = dy * gv
        dx_ref[...] = r * (dxh - xh * jnp.mean(dxh * xh, axis=-1, keepdims=True))
        st_ref[0:1, :] += jnp.sum(dy * xh, axis=0, keepdims=True)

    return pl.pallas_call(
        body, name=name, grid=(t // ROWS,),
        in_specs=[_row_spec(d), _const_spec((1, d)), _row_spec(d)],
        out_specs=(_row_spec(d), _const_spec((8, d)), _const_spec((8, LANES))),
        out_shape=(jax.ShapeDtypeStruct((t, d), F32), jax.ShapeDtypeStruct((8, d), F32),
                   jax.ShapeDtypeStruct((8, LANES), F32)),
        compiler_params=_params("arbitrary"),
    )(x, g, target)


FFN_BLOCK = D_FF // 2
FFN_ROWS = 512


def _ffn_chunks(width):
    edges = [min(width, 3 * LANES * i) for i in range(width // (3 * LANES) + 2)]
    return [slice(lo, hi) for lo, hi in zip(edges[:-1], edges[1:]) if hi > lo]


def _ffn_gu_fwd(h, wg, wu, dep, *, name):
    t, d = h.shape
    tn = FFN_BLOCK

    chunks = _ffn_chunks(tn)
    rows = _pick(t, FFN_ROWS, 16)

    def body(h_ref, wg_ref, wu_ref, dep_ref, s_ref, a_ref, b_ref):
        hv = h_ref[...]
        ab = [(_dotb(hv, wg_ref[sl, :], NT), _dotb(hv, wu_ref[sl, :], NT)) for sl in chunks]
        for sl, (a, b) in zip(chunks, ab):
            s_ref[:, sl] = (a * _sigmoid(a) * b).astype(s_ref.dtype)
            a_ref[:, sl] = a.astype(a_ref.dtype)
            b_ref[:, sl] = b.astype(b_ref.dtype)

    w_spec = pl.BlockSpec((tn, d), lambda j, i: (j, 0))
    o_spec = pl.BlockSpec((rows, tn), lambda j, i: (i, j))
    return pl.pallas_call(
        body, name=name, grid=(D_FF // tn, t // rows),
        in_specs=[pl.BlockSpec((rows, d), lambda j, i: (i, 0)), w_spec, w_spec,
                  pl.BlockSpec((8, LANES), lambda j, i: (0, 0))],
        out_specs=(o_spec, o_spec, o_spec),
        out_shape=(jax.ShapeDtypeStruct((t, D_FF), BF16),) * 3,
        compiler_params=_params("parallel", "parallel"),
    )(h, wg, wu, dep)


def _ffn_down_dx(dy, w_down, a, b, *, name):
    t, d = dy.shape
    tn = FFN_BLOCK

    chunks = _ffn_chunks(tn)
    rows = _pick(t, FFN_ROWS, 16)

    def body(dy_ref, w_ref, a_ref, b_ref, da_ref, db_ref):
        dyv = dy_ref[...]
        ds = [_dotb(dyv, w_ref[sl, :], NT) for sl in chunks]
        for sl, dsc in zip(chunks, ds):
            av = a_ref[:, sl].astype(F32)
            sg = _sigmoid(av)
            da_ref[:, sl] = (dsc * b_ref[:, sl].astype(F32) * sg * (1.0 + av * (1.0 - sg))).astype(da_ref.dtype)
            db_ref[:, sl] = (dsc * av * sg).astype(db_ref.dtype)

    o_spec = pl.BlockSpec((rows, tn), lambda j, i: (i, j))
    return pl.pallas_call(
        body, name=name, grid=(D_FF // tn, t // rows),
        in_specs=[pl.BlockSpec((rows, d), lambda j, i: (i, 0)), pl.BlockSpec((tn, d), lambda j, i: (j, 0)),
                  o_spec, o_spec],
        out_specs=(o_spec, o_spec),
        out_shape=(jax.ShapeDtypeStruct((t, D_FF), BF16),) * 2,
        compiler_params=_params("parallel", "parallel"),
    )(dy, w_down, a, b)


def _shift_rows(v, s, rows):
    if s == 0:
        return v
    return jnp.where(rows >= s, pltpu.roll(v, s, 0), 0.0)


def _unshift_rows(v, s, rows, t):
    if s == 0:
        return v
    return jnp.where(rows < t - s, pltpu.roll(v, t - s, 0), 0.0)


def _conv_silu(x, w, rows):
    z = w[GDN_CONV - 1:GDN_CONV, :] * x
    for j in range(GDN_CONV - 1):
        z = z + w[j:j + 1, :] * _shift_rows(x, GDN_CONV - 1 - j, rows)
    sg = _sigmoid(z)
    return z, sg, z * sg


def _gdn_prep_fwd(proj, conv_wt, *, name):
    t = proj.shape[0]
    nh = GDN_HEADS

    hp = GDN_PREP_HEADS
    wd = hp * LANES

    def body(x_ref, w_ref, y_ref):
        j = pl.program_id(0) * hp
        rows = lax.broadcasted_iota(jnp.int32, (t, LANES), 0)
        qscale = jnp.where(j < nh, GDN_HEAD_DIM ** -0.5, 1.0)
        for i in range(hp):
            sl = slice(i * LANES, (i + 1) * LANES)
            _, _, s = _conv_silu(x_ref[:, sl], w_ref[:, sl], rows)
            rs = lax.rsqrt(jnp.sum(s * s, axis=-1, keepdims=True) + EPS)
            y_ref[:, sl] = jnp.where(j < 2 * nh, s * rs * qscale, s)

    return pl.pallas_call(
        body, name=name, grid=(3 * nh // hp,),
        in_specs=[pl.BlockSpec((t, wd), lambda j: (0, j)), pl.BlockSpec((GDN_CONV, wd), lambda j: (0, j))],
        out_specs=pl.BlockSpec((t, wd), lambda j: (0, j)),
        out_shape=jax.ShapeDtypeStruct((t, 3 * GDN_KEY_DIM), F32),
        compiler_params=_params("parallel"),
    )(proj, conv_wt)


def _gdn_prep_bwd(proj, conv_wt, dy, *, name):
    t = proj.shape[0]
    nh = GDN_HEADS

    hp = GDN_PREP_HEADS
    wd = hp * LANES
    per_seg = nh // hp

    def body(x_ref, w_ref, dy_ref, dx_ref, dw_ref):
        j = pl.program_id(0) * hp
        rows = lax.broadcasted_iota(jnp.int32, (t, LANES), 0)
        qscale = jnp.where(j < nh, GDN_HEAD_DIM ** -0.5, 1.0)
        for i in range(hp):
            sl = slice(i * LANES, (i + 1) * LANES)
            x = x_ref[:, sl]
            w = w_ref[:, sl]
            z, sg, s = _conv_silu(x, w, rows)
            rs = lax.rsqrt(jnp.sum(s * s, axis=-1, keepdims=True) + EPS)
            dyv = dy_ref[:, sl]
            nv = s * rs
            de = dyv * qscale
            ds_qk = rs * (de - nv * jnp.sum(de * nv, axis=-1, keepdims=True))
            ds = jnp.where(j < 2 * nh, ds_qk, dyv)
            dz = ds * sg * (1.0 + z * (1.0 - sg))
            dx = w[GDN_CONV - 1:GDN_CONV, :] * dz
            dw_ref[GDN_CONV - 1:GDN_CONV, sl] = jnp.sum(dz * x, axis=0, keepdims=True)
            for k in range(GDN_CONV - 1):
                sh = GDN_CONV - 1 - k
                dx = dx + w[k:k + 1, :] * _unshift_rows(dz, sh, rows, t)
                dw_ref[k:k + 1, sl] = jnp.sum(dz * _shift_rows(x, sh, rows), axis=0, keepdims=True)
            dx_ref[:, sl] = dx.astype(dx_ref.dtype)

    return pl.pallas_call(
        body, name=name, grid=(3 * nh // hp,),
        in_specs=[pl.BlockSpec((t, wd), lambda j: (0, j)), pl.BlockSpec((GDN_CONV, wd), lambda j: (0, j)),
                  pl.BlockSpec((None, t, wd), lambda j: (j // per_seg, 0, j % per_seg))],
        out_specs=(pl.BlockSpec((t, wd), lambda j: (0, j)), pl.BlockSpec((GDN_CONV, wd), lambda j: (0, j))),
        out_shape=(jax.ShapeDtypeStruct((t, 3 * GDN_KEY_DIM), BF16),
                   jax.ShapeDtypeStruct((GDN_CONV, 3 * GDN_KEY_DIM), F32)),
        compiler_params=_params("parallel"),
    )(proj, conv_wt, dy)


def _softplus(z):
    return jnp.maximum(z, 0.0) + jnp.log(1.0 + jnp.exp(-jnp.abs(z)))


def _gdn_gate_fwd(ab, prm, *, name):
    t = ab.shape[0]

    def body(ab_ref, p_ref, o_ref):
        v = ab_ref[...]
        lane = lax.broadcasted_iota(jnp.int32, v.shape, 1)
        g = -jnp.exp(p_ref[0:1, :]) * _softplus(v + p_ref[1:2, :])
        o_ref[...] = jnp.where(lane < GDN_HEADS, g, jnp.where(lane < 2 * GDN_HEADS, _sigmoid(v), 0.0))

    return pl.pallas_call(
        body, name=name, grid=(t // ROWS,),
        in_specs=[_row_spec(LANES), _const_spec((8, LANES))], out_specs=_row_spec(LANES),
        out_shape=jax.ShapeDtypeStruct((t, LANES), F32), compiler_params=_params("parallel"),
    )(ab, prm)


def _gdn_gate_bwd(ab, prm, dgb, *, name):
    t = ab.shape[0]

    def body(ab_ref, p_ref, d_ref, o_ref, st_ref):
        @pl.when(pl.program_id(0) == 0)
        def _():
            st_ref[...] = jnp.zeros_like(st_ref)

        v = ab_ref[...]
        dv = d_ref[...]
        lane = lax.broadcasted_iota(jnp.int32, v.shape, 1)
        is_a = lane < GDN_HEADS
        is_b = jnp.logical_and(lane >= GDN_HEADS, lane < 2 * GDN_HEADS)
        a_exp = jnp.exp(p_ref[0:1, :])
        zz = v + p_ref[1:2, :]
        g = -a_exp * _softplus(zz)
        da = dv * (-a_exp) * _sigmoid(zz)
        beta = _sigmoid(v)
        db = dv * beta * (1.0 - beta)
        o_ref[...] = jnp.where(is_a, da, jnp.where(is_b, db, 0.0)).astype(o_ref.dtype)
        st_ref[0:1, :] += jnp.sum(jnp.where(is_a, dv * g, 0.0), axis=0, keepdims=True)
        st_ref[1:2, :] += jnp.sum(jnp.where(is_a, da, 0.0), axis=0, keepdims=True)

    return pl.pallas_call(
        body, name=name, grid=(t // ROWS,),
        in_specs=[_row_spec(LANES), _const_spec((8, LANES)), _row_spec(LANES)],
        out_specs=(_row_spec(LANES), _const_spec((8, LANES))),
        out_shape=(jax.ShapeDtypeStruct((t, LANES), BF16), jax.ShapeDtypeStruct((8, LANES), F32)),
        compiler_params=_params("arbitrary"),
    )(ab, prm, dgb)


def _gdn_local(qs, ks, vs, gbs, bbs, tinvs=None):
    nh = len(qs)
    cs = qs[0].shape[0]
    hs = range(nh)
    r = lax.broadcasted_iota(jnp.int32, (cs, cs), 0)
    c = lax.broadcasted_iota(jnp.int32, (cs, cs), 1)
    tril, strict, eye = r >= c, r > c, r == c
    ident = jnp.where(eye, 1.0, 0.0)
    g_colb = [gbs[h][:, :cs] for h in hs]
    g_row = [jnp.sum(jnp.where(eye, g_colb[h], 0.0), axis=0, keepdims=True) for h in hs]
    gc_col = [jnp.sum(jnp.where(tril, g_row[h], 0.0), axis=1, keepdims=True) for h in hs]
    gc_row = [jnp.sum(jnp.where(r <= c, g_colb[h], 0.0), axis=0, keepdims=True) for h in hs]
    decay = [jnp.exp(jnp.where(tril, gc_col[h] - gc_row[h], NEG)) for h in hs]
    gamma = [jnp.exp(gc_col[h]) for h in hs]
    gcl = [gc_col[h][cs - 1:cs, :] for h in hs]
    gl = [jnp.exp(gcl[h]) for h in hs]
    kdec = [jnp.exp(gcl[h] - gc_col[h]) for h in hs]
    kb = [ks[h] * bbs[h] for h in hs]
    kk = [_dotb(kb[h], ks[h], NT) for h in hs]
    qk = [_dotb(qs[h], ks[h], NT) for h in hs]
    lmat = [jnp.where(strict, kk[h] * decay[h], 0.0) for h in hs]
    pmat = [jnp.where(tril, qk[h] * decay[h], 0.0) for h in hs]
    if tinvs is None:
        xm = [-lmat[h] for h in hs]
        tinv = [ident + xm[h] for h in hs]
        for _ in range(int(math.log2(cs)) - 1):
            xm = [_dotf(xm[h], xm[h], NN) for h in hs]
            tinv = [tinv[h] + _dotf(tinv[h], xm[h], NN) for h in hs]
    else:
        tinv = tinvs
    vb = [vs[h] * bbs[h] for h in hs]
    kg = [kb[h] * gamma[h] for h in hs]
    u = [_dotf(tinv[h], vb[h], NN) for h in hs]
    w = [_dotf(tinv[h], kg[h], NN) for h in hs]
    return [dict(tril=tril, strict=strict, eye=eye, r=r, c=c, decay=decay[h], gamma=gamma[h], gl=gl[h], kdec=kdec[h],
                 kb=kb[h], lmat=lmat[h], tinv=tinv[h], vb=vb[h], kg=kg[h], u=u[h], w=w[h], pmat=pmat[h],
                 qd=qs[h] * gamma[h], kd=ks[h] * kdec[h]) for h in hs]


def _head_columns(gbeta, cs):
    gbs = [jnp.broadcast_to(gbeta[:, h:h + 1], (cs, LANES)) for h in range(GDN_HEADS)]
    bbs = [jnp.broadcast_to(gbeta[:, GDN_HEADS + h:GDN_HEADS + h + 1], (cs, LANES)) for h in range(GDN_HEADS)]
    return gbs, bbs


def _gdn_chunk_fwd(qkv, gbeta, *, name):
    t = qkv.shape[0]
    nh, cs, hd = GDN_HEADS, GDN_CHUNK, GDN_HEAD_DIM
    nc = t // cs

    hb = GDN_HEAD_BATCH
    ng = nh // hb
    assert ng == 1

    def body(q_ref, k_ref, v_ref, gb_ref, o_ref, st_ref, ti_ref, s_ref):
        @pl.when(pl.program_id(1) == 0)
        def _():
            s_ref[...] = jnp.zeros_like(s_ref)

        sls = [slice(i * hd, (i + 1) * hd) for i in range(hb)]
        hs = range(hb)
        s = [s_ref[i] for i in hs]
        gbs, bbs = _head_columns(gb_ref[...], cs)
        lo = _gdn_local([q_ref[:, sl] for sl in sls], [k_ref[:, sl] for sl in sls], [v_ref[:, sl] for sl in sls],
                        gbs, bbs)
        ws = [_dotb(lo[i]["w"], s[i], NN) for i in hs]
        qs = [_dotb(lo[i]["qd"], s[i], NN) for i in hs]
        vn = [lo[i]["u"] - ws[i] for i in hs]
        pv = [_dotb(lo[i]["pmat"], vn[i], NN) for i in hs]
        kv = [_dotb(lo[i]["kd"], vn[i], TN) for i in hs]
        for i, sl in enumerate(sls):
            st_ref[i, 0] = s[i]
            ti_ref[i, 0] = lo[i]["tinv"]
            o_ref[:, sl] = qs[i] + pv[i]
            s_ref[i] = s[i] * lo[i]["gl"] + kv[i]

    col = lambda off: pl.BlockSpec((cs, hb * hd), lambda h, n: (n, off + h))
    return pl.pallas_call(
        body, name=name, grid=(ng, nc),
        in_specs=[col(0), col(ng), col(2 * ng), pl.BlockSpec((cs, LANES), lambda h, n: (n, 0))],
        out_specs=(col(0), pl.BlockSpec((hb, 1, hd, hd), lambda h, n: (h, n, 0, 0)),
                   pl.BlockSpec((hb, 1, cs, cs), lambda h, n: (h, n, 0, 0))),
        out_shape=(jax.ShapeDtypeStruct((t, nh * hd), F32), jax.ShapeDtypeStruct((nh, nc, hd, hd), F32),
                   jax.ShapeDtypeStruct((nh, nc, cs, cs), F32)),
        scratch_shapes=[pltpu.VMEM((hb, hd, hd), F32)],
        compiler_params=_params("parallel", "arbitrary"),
    )(qkv, qkv, qkv, gbeta)


def _gdn_chunk_bwd(qkv, gbeta, states, tinvs, do, *, name):
    t = qkv.shape[0]
    nh, cs, hd = GDN_HEADS, GDN_CHUNK, GDN_HEAD_DIM
    nc = t // cs

    hb = GDN_HEAD_BATCH
    ng = nh // hb
    assert ng == 1

    def heads_bwd(q, k, v, gb, bb, s, ti, dsn, dov):
        hs = range(len(q))
        lo = _gdn_local(q, k, v, gb, bb, ti)
        tril, strict, eye, r, c = lo[0]["tril"], lo[0]["strict"], lo[0]["eye"], lo[0]["r"], lo[0]["c"]
        rowi = lax.broadcasted_iota(jnp.int32, (cs, 1), 0)
        get = lambda name: [lo[h][name] for h in hs]
        decay, gamma, gl, kdec = get("decay"), get("gamma"), get("gl"), get("kdec")
        kb, tinv, w, pmat, kd, qd = get("kb"), get("tinv"), get("w"), get("pmat"), get("kd"), get("qd")
        ws = [_dotb(w[h], s[h], NN) for h in hs]
        pdo = [_dotb(pmat[h], dov[h], TN) for h in hs]
        kds = [_dotb(kd[h], dsn[h], NN) for h in hs]
        dqd = [_dotb(dov[h], s[h], NT) for h in hs]
        qdo = [_dotb(qd[h], dov[h], TN) for h in hs]
        vn = [lo[h]["u"] - ws[h] for h in hs]
        dvn = [pdo[h] + kds[h] for h in hs]
        dp = [jnp.where(tril, _dotb(dov[h], vn[h], NT), 0.0) for h in hs]
        dkd = [_dotb(vn[h], dsn[h], NT) for h in hs]
        dw = [-_dotb(dvn[h], s[h], NT) for h in hs]
        wdv = [_dotb(w[h], dvn[h], TN) for h in hs]
        dvb = [_dotf(tinv[h], dvn[h], TN) for h in hs]
        dt1 = [_dotf(dvn[h], lo[h]["vb"], NT) for h in hs]
        dkg = [_dotf(tinv[h], dw[h], TN) for h in hs]
        dt2 = [_dotf(dw[h], lo[h]["kg"], NT) for h in hs]
        tdt = [_dotf(tinv[h], dt1[h] + dt2[h], TN) for h in hs]
        dl = [jnp.where(strict, -_dotf(tdt[h], tinv[h], NT), 0.0) for h in hs]
        dkk = [dl[h] * decay[h] for h in hs]
        dqk = [dp[h] * decay[h] for h in hs]
        dkb = [_dotb(dkk[h], k[h], NN) + dkg[h] * gamma[h] for h in hs]
        dk1 = [_dotb(dkk[h], kb[h], TN) for h in hs]
        dk2 = [_dotb(dqk[h], q[h], TN) for h in hs]
        dq1 = [_dotb(dqk[h], k[h], NN) for h in hs]
        out = []
        for h in hs:
            dgl = jnp.sum(jnp.sum(dsn[h] * s[h], axis=1, keepdims=True), axis=0, keepdims=True)
            ds_prev = gl[h] * dsn[h] + qdo[h] - wdv[h]
            dk = dk1[h] + dk2[h] + dkd[h] * kdec[h] + dkb[h] * bb[h]
            dq = dq1[h] + dqd[h] * gamma[h]
            dbeta = jnp.sum(dvb[h] * v[h], axis=-1, keepdims=True) + jnp.sum(dkb[h] * k[h], axis=-1, keepdims=True)
            e = dl[h] * lo[h]["lmat"] + dp[h] * pmat[h]
            e_col = jnp.sum(e, axis=0, keepdims=True)
            dgc = jnp.sum(e, axis=1, keepdims=True) - jnp.sum(jnp.where(eye, e_col, 0.0), axis=1, keepdims=True)
            dgamma = (jnp.sum(dqd[h] * q[h], axis=-1, keepdims=True)
                      + jnp.sum(dkg[h] * kb[h], axis=-1, keepdims=True))
            rk = jnp.sum(dkd[h] * k[h], axis=-1, keepdims=True) * kdec[h]
            dgcl = jnp.sum(rk, axis=0, keepdims=True) + dgl * gl[h]
            dgc = dgc + dgamma * gamma[h] - rk + jnp.where(rowi == cs - 1, dgcl, 0.0)
            dgc_row = jnp.sum(jnp.where(eye, dgc, 0.0), axis=0, keepdims=True)
            dg = jnp.sum(jnp.where(c >= r, dgc_row, 0.0), axis=1, keepdims=True)
            out.append((dq, dk, dvb[h] * bb[h], dbeta, dg, ds_prev))
        return out

    def body(q_ref, k_ref, v_ref, gb_ref, st_ref, ti_ref, do_ref, d_ref, dgb_ref, ds_ref):
        @pl.when(pl.program_id(1) == 0)
        def _():
            ds_ref[...] = jnp.zeros_like(ds_ref)

        sls = [slice(i * hd, (i + 1) * hd) for i in range(hb)]
        hs = range(hb)
        gbs, bbs = _head_columns(gb_ref[...], cs)
        outs = heads_bwd([q_ref[:, sl] for sl in sls], [k_ref[:, sl] for sl in sls], [v_ref[:, sl] for sl in sls],
                         gbs, bbs, [st_ref[i, 0] for i in hs],
                         [ti_ref[i, 0] for i in hs], [ds_ref[i] for i in hs], [do_ref[:, sl] for sl in sls])
        lane = lax.broadcasted_iota(jnp.int32, (cs, LANES), 1)
        dgb = jnp.zeros((cs, LANES), F32)
        for i, sl in enumerate(sls):
            dq, dk, dv, dbeta, dg, ds_prev = outs[i]
            d_ref[0, :, sl], d_ref[1, :, sl], d_ref[2, :, sl] = dq, dk, dv
            dgb = jnp.where(lane == i, dg, jnp.where(lane == nh + i, dbeta, dgb))
            ds_ref[i] = ds_prev
        dgb_ref[...] = dgb

    col = lambda off: pl.BlockSpec((cs, hb * hd), lambda h, n: (nc - 1 - n, off + h))
    gspec = pl.BlockSpec((cs, LANES), lambda h, n: (nc - 1 - n, 0))
    return pl.pallas_call(
        body, name=name, grid=(ng, nc),
        in_specs=[col(0), col(ng), col(2 * ng), gspec,
                  pl.BlockSpec((hb, 1, hd, hd), lambda h, n: (h, nc - 1 - n, 0, 0)),
                  pl.BlockSpec((hb, 1, cs, cs), lambda h, n: (h, nc - 1 - n, 0, 0)), col(0)],
        out_specs=(pl.BlockSpec((3, cs, hb * hd), lambda h, n: (0, nc - 1 - n, h)), gspec),
        out_shape=(jax.ShapeDtypeStruct((3, t, nh * hd), F32), jax.ShapeDtypeStruct((t, LANES), F32)),
        scratch_shapes=[pltpu.VMEM((hb, hd, hd), F32)],
        compiler_params=_params("parallel", "arbitrary"),
    )(qkv, qkv, qkv, gbeta, states, tinvs, do)


def _gdn_onorm_fwd(o, proj, norm_g, *, name):
    t = o.shape[0]
    w = GDN_KEY_DIM
    goff = 3 * GDN_KEY_DIM // w

    def body(o_ref, gp_ref, g_ref, y_ref):
        gv = g_ref[...]
        for h in range(GDN_HEADS):
            sl = slice(h * GDN_HEAD_DIM, (h + 1) * GDN_HEAD_DIM)
            oh = o_ref[:, sl]
            gp = gp_ref[:, sl]
            r = lax.rsqrt(jnp.mean(oh * oh, axis=-1, keepdims=True) + EPS)
            y_ref[:, sl] = (oh * r * gv * gp * _sigmoid(gp)).astype(y_ref.dtype)

    return pl.pallas_call(
        body, name=name, grid=(t // ROWS,),
        in_specs=[_row_spec(w), pl.BlockSpec((ROWS, w), lambda i: (i, goff)), _const_spec((1, GDN_HEAD_DIM))],
        out_specs=_row_spec(w), out_shape=jax.ShapeDtypeStruct((t, w), BF16),
        compiler_params=_params("parallel"),
    )(o, proj, norm_g)


def _gdn_onorm_bwd(o, proj, norm_g, dy, *, name):
    t = o.shape[0]
    w = GDN_KEY_DIM
    goff = 3 * GDN_KEY_DIM // w

    def body(o_ref, gp_ref, g_ref, dy_ref, do_ref, dgp_ref, st_ref):
        @pl.when(pl.program_id(0) == 0)
        def _():
            st_ref[...] = jnp.zeros_like(st_ref)

        gv = g_ref[...]
        acc = jnp.zeros((1, GDN_HEAD_DIM), F32)
        for h in range(GDN_HEADS):
            sl = slice(h * GDN_HEAD_DIM, (h + 1) * GDN_HEAD_DIM)
            oh = o_ref[:, sl]
            gp = gp_ref[:, sl]
            dyv = dy_ref[:, sl].astype(F32)
            r = lax.rsqrt(jnp.mean(oh * oh, axis=-1, keepdims=True) + EPS)
            xh = oh * r
            sg = _sigmoid(gp)
            dn = dyv * gp * sg
            dgp_ref[:, sl] = (dyv * xh * gv * sg * (1.0 + gp * (1.0 - sg))).astype(dgp_ref.dtype)
            acc = acc + jnp.sum(dn * xh, axis=0, keepdims=True)
            dxh = dn * gv
            do_ref[:, sl] = r * (dxh - xh * jnp.mean(dxh * xh, axis=-1, keepdims=True))
        st_ref[0:1, :] += acc

    return pl.pallas_call(
        body, name=name, grid=(t // ROWS,),
        in_specs=[_row_spec(w), pl.BlockSpec((ROWS, w), lambda i: (i, goff)), _const_spec((1, GDN_HEAD_DIM)),
                  _row_spec(w)],
        out_specs=(_row_spec(w), _row_spec(w), _const_spec((8, GDN_HEAD_DIM))),
        out_shape=(jax.ShapeDtypeStruct((t, w), F32), jax.ShapeDtypeStruct((t, w), BF16),
                   jax.ShapeDtypeStruct((8, GDN_HEAD_DIM), F32)),
        compiler_params=_params("arbitrary"),
    )(o, proj, norm_g, dy)


def _mla_prep_fwd(proj, qg, kvg, *, name):
    t = proj.shape[0]
    q1, k1 = MLA_Q_RANK, MLA_Q_RANK + MLA_KV_RANK

    def body(p_ref, qg_ref, kg_ref, cq_ref, ck_ref):
        cq = p_ref[:, 0:q1]
        ck = p_ref[:, q1:k1]
        cq_ref[...] = (cq * lax.rsqrt(jnp.mean(cq * cq, axis=-1, keepdims=True) + EPS) * qg_ref[...]).astype(BF16)
        ck_ref[...] = (ck * lax.rsqrt(jnp.mean(ck * ck, axis=-1, keepdims=True) + EPS) * kg_ref[...]).astype(BF16)

    return pl.pallas_call(
        body, name=name, grid=(t // ROWS,),
        in_specs=[_row_spec(MLA_IN), _const_spec((1, MLA_Q_RANK)), _const_spec((1, MLA_KV_RANK))],
        out_specs=(_row_spec(MLA_Q_RANK), _row_spec(MLA_KV_RANK)),
        out_shape=(jax.ShapeDtypeStruct((t, MLA_Q_RANK), BF16), jax.ShapeDtypeStruct((t, MLA_KV_RANK), BF16)),
        compiler_params=_params("parallel"),
    )(proj, qg, kvg)


def _mla_prep_bwd(proj, qg, kvg, dcq, dck, dkr, *, name):
    t = proj.shape[0]
    q1, k1 = MLA_Q_RANK, MLA_Q_RANK + MLA_KV_RANK

    def body(p_ref, qg_ref, kg_ref, dq_ref, dk_ref, dr_ref, dp_ref, st_ref):
        @pl.when(pl.program_id(0) == 0)
        def _():
            st_ref[...] = jnp.zeros_like(st_ref)

        for lo, hi, g_ref, d_ref in ((0, q1, qg_ref, dq_ref), (q1, k1, kg_ref, dk_ref)):
            xv = p_ref[:, lo:hi]
            dn = d_ref[...]
            r = lax.rsqrt(jnp.mean(xv * xv, axis=-1, keepdims=True) + EPS)
            xh = xv * r
            dxh = dn * g_ref[...]
            dp_ref[:, lo:hi] = (r * (dxh - xh * jnp.mean(dxh * xh, axis=-1, keepdims=True))).astype(dp_ref.dtype)
            st_ref[0:1, lo:hi] += jnp.sum(dn * xh, axis=0, keepdims=True)
        dp_ref[:, k1:MLA_IN] = dr_ref[:, 0:MLA_ROPE].astype(dp_ref.dtype)

    return pl.pallas_call(
        body, name=name, grid=(t // ROWS,),
        in_specs=[_row_spec(MLA_IN), _const_spec((1, MLA_Q_RANK)), _const_spec((1, MLA_KV_RANK)),
                  _row_spec(MLA_Q_RANK), _row_spec(MLA_KV_RANK), _row_spec(LANES)],
        out_specs=(_row_spec(MLA_IN), _const_spec((8, MLA_IN))),
        out_shape=(jax.ShapeDtypeStruct((t, MLA_IN), BF16), jax.ShapeDtypeStruct((8, MLA_IN), F32)),
        compiler_params=_params("arbitrary"),
    )(proj, qg, kvg, dcq, dck, dkr)


ATT_BLOCK = 256
ATT_HEAD_BATCH = 4
ATT_HEAD_BATCH_BWD = 4
ATT_SCALE = MLA_QK ** -0.5


def _diagonal_mask(blk):
    return lax.broadcasted_iota(jnp.int32, (blk, blk), 1) <= lax.broadcasted_iota(jnp.int32, (blk, blk), 0)


def _swap_halves(xv, first):
    return jnp.where(first, pltpu.roll(xv, LANES - MLA_ROPE // 2, 1), pltpu.roll(xv, MLA_ROPE // 2, 1))


def _rope_qk(qf, proj, cos_t, sin_t, *, name):
    t = qf.shape[0]
    nrope = MLA_HEADS * MLA_ROPE
    q_blk = MLA_HEADS * MLA_NOPE // nrope
    k_blk = (MLA_Q_RANK + MLA_KV_RANK) // LANES

    def body(q_ref, p_ref, c_ref, s_ref, qo_ref, ko_ref):
        cv, sv = c_ref[...], s_ref[...]
        lane = lax.broadcasted_iota(jnp.int32, (ROWS, LANES), 1)
        first = (lane % MLA_ROPE) < (MLA_ROPE // 2)
        for i in range(nrope // LANES):
            sl = slice(i * LANES, (i + 1) * LANES)
            xv = q_ref[:, sl].astype(F32)
            qo_ref[:, sl] = (xv * cv + _swap_halves(xv, first) * sv).astype(qo_ref.dtype)
        kv = jnp.where(lane < MLA_ROPE, p_ref[...], 0.0)
        ko_ref[...] = (kv * cv + _swap_halves(kv, first) * sv).astype(ko_ref.dtype)

    return pl.pallas_call(
        body, name=name, grid=(t // ROWS,),
        in_specs=[pl.BlockSpec((ROWS, nrope), lambda i: (i, q_blk)), pl.BlockSpec((ROWS, LANES), lambda i: (i, k_blk)),
                  _row_spec(LANES), _row_spec(LANES)],
        out_specs=(_row_spec(nrope), _row_spec(LANES)),
        out_shape=(jax.ShapeDtypeStruct((t, nrope), BF16), jax.ShapeDtypeStruct((t, LANES), BF16)),
        compiler_params=_params("parallel"),
    )(qf, proj, cos_t, sin_t)


def _rope_qk_bwd(dqr, dkr_parts, cos_t, sin_t, *, name):
    t, nrope = dqr.shape
    ng = dkr_parts.shape[0]

    def body(d_ref, k_ref, c_ref, s_ref, qo_ref, ko_ref):
        cv, sv = c_ref[...], s_ref[...]
        lane = lax.broadcasted_iota(jnp.int32, (ROWS, LANES), 1)
        first = (lane % MLA_ROPE) < (MLA_ROPE // 2)
        for i in range(nrope // LANES):
            sl = slice(i * LANES, (i + 1) * LANES)
            dv = d_ref[:, sl]
            qo_ref[:, sl] = (dv * cv + _swap_halves(dv * sv, first)).astype(qo_ref.dtype)
        dk = k_ref[0]
        for g in range(1, ng):
            dk = dk + k_ref[g]
        dk = jnp.where(lane < MLA_ROPE, dk, 0.0)
        ko_ref[...] = jnp.where(lane < MLA_ROPE, dk * cv + _swap_halves(dk * sv, first), 0.0)

    return pl.pallas_call(
        body, name=name, grid=(t // ROWS,),
        in_specs=[_row_spec(nrope), pl.BlockSpec((ng, ROWS, LANES), lambda i: (0, i, 0)), _row_spec(LANES),
                  _row_spec(LANES)],
        out_specs=(_row_spec(nrope), _row_spec(LANES)),
        out_shape=(jax.ShapeDtypeStruct((t, nrope), BF16), jax.ShapeDtypeStruct((t, LANES), F32)),
        compiler_params=_params("parallel"),
    )(dqr, dkr_parts, cos_t, sin_t)


def _attn_tm_fwd(qf, qr, kvf, kr, *, name):
    t = qf.shape[0]
    nh, dn, dr, dv = MLA_HEADS, MLA_NOPE, MLA_ROPE, MLA_V
    blk = min(ATT_BLOCK, t)
    hb = ATT_HEAD_BATCH
    hs = range(hb)

    def body(q_ref, qr_ref, kv_ref, kr_ref, o_ref, l_ref):
        i = pl.program_id(1)
        qn = [q_ref[:, h * dn:(h + 1) * dn].astype(MXU_DTYPE) for h in hs]
        qrh = [qr_ref[:, h * dr:(h + 1) * dr] for h in hs]

        def step(j, carry, diagonal=False):
            m, l, acc = carry[:hb], carry[hb:2 * hb], carry[2 * hb:]
            rows = pl.ds(pl.multiple_of(j * blk, blk), blk)
            krj = kr_ref[rows, 0:dr]
            s = [_dotb(qn[h], kv_ref[rows, h * (dn + dv):h * (dn + dv) + dn], NT) for h in hs]
            sr = [_dotb(qrh[h], krj, NT) for h in hs]
            s = [(s[h] + sr[h]) * ATT_SCALE for h in hs]
            if diagonal:
                mask = _diagonal_mask(blk)
                s = [jnp.where(mask, s[h], NEG) for h in hs]
            m_new = [jnp.maximum(m[h], jnp.max(s[h], axis=-1, keepdims=True)) for h in hs]
            p = [jnp.exp(s[h] - m_new[h]) for h in hs]
            pv = [_dotb(p[h], kv_ref[rows, h * (dn + dv) + dn:(h + 1) * (dn + dv)], NN) for h in hs]
            alpha = [jnp.exp(m[h] - m_new[h]) for h in hs]
            l = [alpha[h] * l[h] + jnp.sum(p[h], axis=-1, keepdims=True) for h in hs]
            acc = [alpha[h] * acc[h] + pv[h] for h in hs]
            return tuple(m_new) + tuple(l) + tuple(acc)

        init = ((jnp.full((blk, 1), NEG, F32),) * hb + (jnp.zeros((blk, 1), F32),) * hb
                + (jnp.zeros((blk, dv), F32),) * hb)
        out = step(i, lax.fori_loop(0, i, step, init), diagonal=True)
        for h in hs:
            m, l, acc = out[h], out[hb + h], out[2 * hb + h]
            o_ref[:, h * dv:(h + 1) * dv] = (acc / l).astype(o_ref.dtype)
            l_ref[h] = jnp.broadcast_to(m + jnp.log(l), (blk, LANES))

    return pl.pallas_call(
        body, name=name, grid=(nh // hb, t // blk),
        in_specs=[pl.BlockSpec((blk, hb * dn), lambda g, i: (i, g)), pl.BlockSpec((blk, hb * dr), lambda g, i: (i, g)),
                  pl.BlockSpec((t, hb * (dn + dv)), lambda g, i: (0, g)), pl.BlockSpec((t, LANES), lambda g, i: (0, 0))],
        out_specs=(pl.BlockSpec((blk, hb * dv), lambda g, i: (i, g)),
                   pl.BlockSpec((hb, blk, LANES), lambda g, i: (g, i, 0))),
        out_shape=(jax.ShapeDtypeStruct((t, nh * dv), BF16), jax.ShapeDtypeStruct((nh, t, LANES), F32)),
        compiler_params=_params("parallel", "parallel"),
    )(qf, qr, kvf, kr)


def _attn_tm_bwd(qf, qr, kvf, kr, o, lse, do, *, name):
    t = qf.shape[0]
    nh, dn, dr, dv = MLA_HEADS, MLA_NOPE, MLA_ROPE, MLA_V
    blk = min(ATT_BLOCK, t)
    nb = t // blk
    hb = ATT_HEAD_BATCH_BWD
    hs = range(hb)
    ng = nh // hb

    def body(q_ref, qr_ref, kv_ref, kr_ref, o_ref, l_ref, do_ref, dqn_ref, dqr_ref, dkv_ref, dkr_ref):
        j = pl.program_id(1)

        @pl.when(j == 0)
        def _():
            dqn_ref[...] = jnp.zeros_like(dqn_ref)
            dqr_ref[...] = jnp.zeros_like(dqr_ref)

        kn = [kv_ref[:, h * (dn + dv):h * (dn + dv) + dn] for h in hs]
        vv = [kv_ref[:, h * (dn + dv) + dn:(h + 1) * (dn + dv)] for h in hs]
        krj = kr_ref[:, 0:dr]

        def step(i, carry, diagonal=False):
            dkn_acc, dv_acc, dkr_acc = carry[:hb], carry[hb:2 * hb], carry[2 * hb]
            rows = pl.ds(pl.multiple_of(i * blk, blk), blk)
            qn = [q_ref[rows, h * dn:(h + 1) * dn].astype(MXU_DTYPE) for h in hs]
            qrh = [qr_ref[rows, h * dr:(h + 1) * dr] for h in hs]
            dov = [do_ref[rows, h * dv:(h + 1) * dv] for h in hs]
            s = [_dotb(qn[h], kn[h], NT) for h in hs]
            sr = [_dotb(qrh[h], krj, NT) for h in hs]
            dp = [_dotb(dov[h], vv[h], NT) for h in hs]
            s = [(s[h] + sr[h]) * ATT_SCALE for h in hs]
            if diagonal:
                mask = _diagonal_mask(blk)
                s = [jnp.where(mask, s[h], NEG) for h in hs]
            p = [jnp.exp(s[h] - l_ref[h, rows, :][:, 0:1]) for h in hs]
            delta = [jnp.sum(dov[h].astype(F32) * o_ref[rows, h * dv:(h + 1) * dv].astype(F32), axis=-1, keepdims=True)
                     for h in hs]
            ds = [p[h] * (dp[h] - delta[h]) * ATT_SCALE for h in hs]
            dvn = [_dotb(p[h], dov[h], TN) for h in hs]
            dknn = [_dotb(ds[h], qn[h], TN) for h in hs]
            dkrn = [_dotb(ds[h], qrh[h], TN) for h in hs]
            dqnn = [_dotb(ds[h], kn[h], NN) for h in hs]
            dqrn = [_dotb(ds[h], krj, NN) for h in hs]
            for h in hs:
                dqn_ref[rows, h * dn:(h + 1) * dn] += dqnn[h]
                dqr_ref[rows, h * dr:(h + 1) * dr] += dqrn[h]
            dkr_new = dkr_acc
            for h in hs:
                dkr_new = dkr_new + dkrn[h]
            return (tuple(dkn_acc[h] + dknn[h] for h in hs) + tuple(dv_acc[h] + dvn[h] for h in hs) + (dkr_new,))

        init = (jnp.zeros((blk, dn), F32),) * hb + (jnp.zeros((blk, dv), F32),) * hb + (jnp.zeros((blk, dr), F32),)
        out = lax.fori_loop(j + 1, nb, step, step(j, init, diagonal=True))
        for h in hs:
            dkv_ref[:, h * (dn + dv):h * (dn + dv) + dn] = out[h].astype(dkv_ref.dtype)
            dkv_ref[:, h * (dn + dv) + dn:(h + 1) * (dn + dv)] = out[hb + h].astype(dkv_ref.dtype)
        dkr_ref[0, :, 0:dr] = out[2 * hb]
        dkr_ref[0, :, dr:LANES] = jnp.zeros((blk, LANES - dr), F32)

    full = lambda w: pl.BlockSpec((t, w), lambda g, j: (0, g))
    return pl.pallas_call(
        body, name=name, grid=(ng, nb),
        in_specs=[full(hb * dn), full(hb * dr), pl.BlockSpec((blk, hb * (dn + dv)), lambda g, j: (j, g)),
                  pl.BlockSpec((blk, LANES), lambda g, j: (j, 0)), full(hb * dv),
                  pl.BlockSpec((hb, t, LANES), lambda g, j: (g, 0, 0)), full(hb * dv)],
        out_specs=(full(hb * dn), full(hb * dr), pl.BlockSpec((blk, hb * (dn + dv)), lambda g, j: (j, g)),
                   pl.BlockSpec((1, blk, LANES), lambda g, j: (g, j, 0))),
        out_shape=(jax.ShapeDtypeStruct((t, nh * dn), F32), jax.ShapeDtypeStruct((t, nh * dr), F32),
                   jax.ShapeDtypeStruct((t, nh * (dn + dv)), BF16), jax.ShapeDtypeStruct((ng, t, LANES), F32)),
        compiler_params=_params("parallel", "arbitrary"),
    )(qf, qr, kvf, kr, o, lse, do)


def _ada_mod(c_all, ada_w, ada_b_cols, *, name):
    nl, d, wc = ada_w.shape

    def body(c_ref, w_ref, b_ref, o_ref):
        cv = c_ref[...]
        o_ref[0] = _dotb(cv * _sigmoid(cv), w_ref[0], NN) + b_ref[0]

    return pl.pallas_call(
        body, name=name, grid=(nl,),
        in_specs=[_const_spec((N_DEV, d)), pl.BlockSpec((1, d, wc), lambda l: (l, 0, 0)),
                  pl.BlockSpec((1, 1, wc), lambda l: (l, 0, 0))],
        out_specs=pl.BlockSpec((1, N_DEV, wc), lambda l: (l, 0, 0)),
        out_shape=jax.ShapeDtypeStruct((nl, N_DEV, wc), F32), compiler_params=_params("parallel"),
    )(c_all, ada_w, ada_b_cols)


def _adam_math(g, w, m, v):
    m2 = ADAM_B1 * m + (1.0 - ADAM_B1) * g
    v2 = ADAM_B2 * v + (1.0 - ADAM_B2) * (g * g)
    delta = -ADAM_LR * ((m2 / ADAM_BC1) / (jnp.sqrt(v2 / ADAM_BC2) + ADAM_EPS) + ADAM_WD * w)
    return delta, m2, v2


def _ada_grad_adamw(c_all, dmod_cols, w, m, v, *, name):
    nl, d, wc = w.shape
    tr = 256

    def body(c_ref, dm_ref, w_ref, m_ref, v_ref, g_ref, d_ref, m2_ref, v2_ref):
        cv = c_ref[...]
        g = _dotf(cv * _sigmoid(cv), dm_ref[0], TN)
        delta, m2, v2 = _adam_math(g, w_ref[0], m_ref[0], v_ref[0])
        g_ref[0], d_ref[0], m2_ref[0], v2_ref[0] = g, delta, m2, v2

    blk = pl.BlockSpec((1, tr, wc), lambda l, i: (l, i, 0))
    return pl.pallas_call(
        body, name=name, grid=(nl, d // tr),
        in_specs=[pl.BlockSpec((N_DEV, tr), lambda l, i: (0, i)), pl.BlockSpec((1, N_DEV, wc), lambda l, i: (l, 0, 0)),
                  blk, blk, blk],
        out_specs=(blk,) * 4, out_shape=(jax.ShapeDtypeStruct(w.shape, F32),) * 4,
        compiler_params=_params("parallel", "parallel"),
    )(c_all, dmod_cols, w, m, v)


def _adamw(parts, w, m, v, *, name):
    nl, r, c = w.shape
    ns = parts[0].shape[0]
    lanes_padded = -(-c // LANES) * LANES
    row_bytes = 2 * nl * ns * lanes_padded * parts[0].dtype.itemsize
    tr = _pick(r, min(256, max(16, (VMEM_LIMIT // 2) // row_bytes)), 16)
    tc = c
    if tr * row_bytes > VMEM_LIMIT // 2:
        tc = _pick(c, max(LANES, c * (VMEM_LIMIT // 2) // (tr * row_bytes)))

    def body(*refs):
        p_refs = refs[:nl]
        w_ref, m_ref, v_ref, g_ref, d_ref, m2_ref, v2_ref = refs[nl:]
        layer = pl.program_id(0)
        for q in range(nl):
            @pl.when(layer == q)
            def _(q=q):
                g = p_refs[q][0].astype(F32)
                for s in range(1, ns):
                    g = g + p_refs[q][s].astype(F32)
                delta, m2, v2 = _adam_math(g, w_ref[0], m_ref[0], v_ref[0])
                g_ref[0], d_ref[0], m2_ref[0], v2_ref[0] = g, delta, m2, v2

    blk = pl.BlockSpec((1, tr, tc), lambda l, i, j: (l, i, j))
    p_specs = [pl.BlockSpec((ns, tr, tc), lambda l, i, j, q=q: (0, jnp.where(l == q, i, 0), jnp.where(l == q, j, 0)))
               for q in range(nl)]
    return pl.pallas_call(
        body, name=name, grid=(nl, r // tr, c // tc),
        in_specs=p_specs + [blk, blk, blk],
        out_specs=(blk,) * 4, out_shape=(jax.ShapeDtypeStruct(w.shape, F32),) * 4,
        compiler_params=_params("arbitrary", "arbitrary", "arbitrary"),
    )(*parts, w, m, v)


def _sum_parts(parts, *, name):
    ns, r, c = parts.shape

    def body(p_ref, o_ref):
        acc = p_ref[0]
        for s in range(1, ns):
            acc = acc + p_ref[s]
        o_ref[...] = acc

    return pl.pallas_call(
        body, name=name, out_shape=jax.ShapeDtypeStruct((r, c), F32),
        in_specs=[pl.BlockSpec(memory_space=pltpu.VMEM)], out_specs=pl.BlockSpec(memory_space=pltpu.VMEM),
    )(parts)


def _pack(arrs):
    flat = jnp.concatenate([a.reshape(-1).astype(F32) for a in arrs])
    pad = (-flat.shape[0]) % (8 * LANES)
    return jnp.pad(flat, (0, pad)).reshape(-1, LANES)


def _unpack(packed, shapes, lead=()):
    flat = packed.reshape(lead + (-1,))
    out, off = [], 0
    for s in shapes:
        n = math.prod(s)
        out.append(flat[..., off:off + n].reshape(lead + tuple(s)))
        off += n
    return out


def _gather_rows(g):
    _, nl, rs, c = g.shape
    return jnp.transpose(g, (1, 0, 2, 3)).reshape(nl, N_DEV * rs, c)


def _row(v):
    return v.reshape(1, -1)


def _local_step(x, target, mod, cos_t, sin_t, rep, get_weights, put_grads):
    t = x.shape[0]
    saved = []
    for layer in range(DEPTH):
        j = layer // 2
        tag = f"l{layer}"
        shift_m, scale_m, gate_m, shift_f, scale_f, gate_f = [_row(mod[layer, i]) for i in range(N_MOD)]
        lw = dict(get_weights(layer, "mix", x))
        rec = {"x0": x, "lw": lw}
        h = _adaln_fwd(x, _row(rep["norm_mix_g"][layer]), scale_m, shift_m, name=f"adaln_mix_{tag}")
        rec["h"] = h
        if layer % 2 == 0:
            proj = _mm(h, lw["wt_in"], mode="nt", out_dtype=F32, tm=256, tn=GDN_MAIN, b_rows=GDN_MAIN,
                       dep=lw["dep_mix"], name=f"gdn_in_{tag}")
            ab = _mm(h, lw["wt_ab"], mode="nt", out_dtype=F32, name=f"gdn_in_ab_{tag}")
            qkv = _gdn_prep_fwd(proj, rep["gdn_conv_wt"][j], name=f"gdn_prep_{tag}")
            gbeta = _gdn_gate_fwd(ab, rep["gdn_gate_prm"][j], name=f"gdn_gate_{tag}")
            o, states, tinvs = _gdn_chunk_fwd(qkv, gbeta, name=f"gdn_chunk_{tag}")
            og = _gdn_onorm_fwd(o, proj, _row(rep["gdn_norm_g"][j]), name=f"gdn_onorm_{tag}")
            x, y = _mm_resid(og, lw["w_out"], x, gate_m, name=f"gdn_out_{tag}")
            rec.update(proj=proj, ab=ab, qkv=qkv, gbeta=gbeta, states=states, tinvs=tinvs, o=o, og=og, y=y)
        else:
            proj = _mm(h, lw["w_in"], mode="nn", out_dtype=F32, dep=lw["dep_mix"], name=f"mla_in_{tag}")
            cq, ck = _mla_prep_fwd(proj, _row(rep["mla_q_norm_g"][j]), _row(rep["mla_kv_norm_g"][j]),
                                   name=f"mla_prep_{tag}")
            qf = _mm(cq, lw["wt_uq"], mode="nt", out_dtype=BF16, name=f"mla_uq_{tag}")
            kvf = _mm(ck, lw["w_ukv"], mode="nn", out_dtype=BF16, name=f"mla_ukv_{tag}")
            qr, kr = _rope_qk(qf, proj, cos_t, sin_t, name=f"rope_{tag}")
            oc, lse = _attn_tm_fwd(qf, qr, kvf, kr, name=f"attn_{tag}")
            x, y = _mm_resid(oc, lw["w_out"], x, gate_m, name=f"mla_out_{tag}")
            rec.update(proj=proj, cq=cq, ck=ck, qf=qf, qr=qr, kvf=kvf, kr=kr, lse=lse, oc=oc, y=y)
        rec["x1"] = x
        lw.update(get_weights(layer, "ffn", x))
        h2 = _adaln_fwd(x, _row(rep["norm_ffn_g"][layer]), scale_f, shift_f, name=f"adaln_ffn_{tag}")
        s, a2, b2 = _ffn_gu_fwd(h2, lw["wt_g"], lw["wt_u"], lw["dep_ffn"], name=f"ffn_gu_{tag}")
        x, y2 = _mm_resid(s, lw["w_down"], x, gate_f, tm=512, name=f"ffn_down_{tag}")
        rec.update(h2=h2, a2=a2, b2=b2, s=s, y2=y2)
        saved.append(rec)

    dx, st, ls = _loss_head(x, _row(rep["final_norm_g"]), target, name="loss_head")
    loss = ls[0, 0]
    grads = {"final_norm_g": st[0]}
    per_layer = {k: [None] * DEPTH for k in ("norm_mix_g", "norm_ffn_g")}
    per_gdn = {k: [None] * 2 for k in ("gdn_conv_wt", "gdn_a_log", "gdn_dt_bias", "gdn_norm_g")}
    per_mla = {k: [None] * 2 for k in ("mla_q_norm_g", "mla_kv_norm_g")}
    dmod = [None] * DEPTH
    dep = jnp.zeros((8, LANES), F32)

    for layer in reversed(range(DEPTH)):
        j = layer // 2
        tag = f"l{layer}"
        rec = saved[layer]
        lw = rec["lw"]
        shift_m, scale_m, gate_m, shift_f, scale_f, gate_f = [_row(mod[layer, i]) for i in range(N_MOD)]
        if layer == DEPTH - 1:
            dy2, st_g = _gate_bwd(dx, rec["y2"], gate_f, dep, name=f"gate_bwd_ffn_{tag}")
            dgate_f = st_g[0]
        dw_down = _mm(rec["s"], dy2, mode="tn", out_dtype=BF16, tm=FFN_BLOCK, tn=1024, name=f"ffn_down_dw_{tag}")
        da2, db2 = _ffn_down_dx(dy2, lw["w_down"], rec["a2"], rec["b2"], name=f"ffn_down_dx_{tag}")
        dwt_g = _mm(da2, rec["h2"], mode="tn", out_dtype=BF16, tm=FFN_BLOCK, tn=1024, name=f"ffn_g_dw_{tag}")
        dwt_u = _mm(db2, rec["h2"], mode="tn", out_dtype=BF16, tm=FFN_BLOCK, tn=1024, name=f"ffn_u_dw_{tag}")
        dep = put_grads(layer, "ffn", {"wt_g": dwt_g, "wt_u": dwt_u, "w_down": dw_down})
        dh2 = _mm(da2, lw["wt_g"], mode="nn", out_dtype=F32, tm=512, tn=1024, name=f"ffn_g_dx_{tag}")
        dh2 = _mm(db2, lw["wt_u"], mode="nn", out_dtype=BF16, add=dh2, tm=512, tn=1024, name=f"ffn_u_dx_{tag}")
        dx, st_n, dy = _adaln_gate_bwd(rec["x1"], _row(rep["norm_ffn_g"][layer]), scale_f, shift_f, dh2, dx, dep,
                                       rec["y"], gate_m, name=f"adaln_ffn_bwd_{tag}")
        per_layer["norm_ffn_g"][layer] = st_n[0]
        dscale_f, dshift_f, dgate_m = st_n[1], st_n[2], st_n[3]
        big = {}
        if layer % 2 == 0:
            big["w_out"] = _mm(rec["og"], dy, mode="tn", out_dtype=BF16, name=f"gdn_out_dw_{tag}")
            dog = _mm(dy, lw["w_out"], mode="nt", out_dtype=BF16, name=f"gdn_out_dx_{tag}")
            do, dgp, st_o = _gdn_onorm_bwd(rec["o"], rec["proj"], _row(rep["gdn_norm_g"][j]), dog,
                                           name=f"gdn_onorm_bwd_{tag}")
            per_gdn["gdn_norm_g"][j] = st_o[0]
            dqkv, dgb = _gdn_chunk_bwd(rec["qkv"], rec["gbeta"], rec["states"], rec["tinvs"], do,
                                       name=f"gdn_chunk_bwd_{tag}")
            dab, st_a = _gdn_gate_bwd(rec["ab"], rep["gdn_gate_prm"][j], dgb, name=f"gdn_gate_bwd_{tag}")
            per_gdn["gdn_a_log"][j] = st_a[0, :GDN_HEADS]
            per_gdn["gdn_dt_bias"][j] = st_a[1, :GDN_HEADS]
            dpre, dcw = _gdn_prep_bwd(rec["proj"], rep["gdn_conv_wt"][j], dqkv, name=f"gdn_prep_bwd_{tag}")
            per_gdn["gdn_conv_wt"][j] = dcw
            dproj = jnp.concatenate([dpre, dgp], axis=1)
            dw_main = _mm(dproj, rec["h"], mode="tn", out_dtype=BF16, tm=512, tn=1024, name=f"gdn_in_dw_{tag}")
            dw_ab = _mm(dab, rec["h"], mode="tn", out_dtype=BF16, tn=1024, name=f"gdn_in_ab_dw_{tag}")
            big["wt_in"] = jnp.concatenate([dw_main, dw_ab[:2 * GDN_HEADS]], axis=0)
            dep = put_grads(layer, "gdn", big)
            dh_ab = _mm(dab, lw["wt_ab"], mode="nn", out_dtype=F32, tn=1024, name=f"gdn_in_ab_dx_{tag}")
            dh = _mm(dproj, lw["wt_in"], mode="nn", out_dtype=BF16, add=dh_ab, tm=256, tn=1024, b_rows=GDN_MAIN,
                     name=f"gdn_in_dx_{tag}")
        else:
            big["w_out"] = _mm(rec["oc"], dy, mode="tn", out_dtype=BF16, name=f"mla_out_dw_{tag}")
            doc = _mm(dy, lw["w_out"], mode="nt", out_dtype=BF16, name=f"mla_out_dx_{tag}")
            dqn, dqr, dkvf, dkr_parts = _attn_tm_bwd(rec["qf"], rec["qr"], rec["kvf"], rec["kr"], rec["oc"],
                                                     rec["lse"], doc, name=f"attn_bwd_{tag}")
            dqr_un, dkr_un = _rope_qk_bwd(dqr, dkr_parts, cos_t, sin_t, name=f"rope_bwd_{tag}")
            n_nope = MLA_HEADS * MLA_NOPE
            big["wt_uq"] = jnp.concatenate(
                [_mm(dqn, rec["cq"], mode="tn", out_dtype=BF16, name=f"mla_uq_dw_nope_{tag}"),
                 _mm(dqr_un, rec["cq"], mode="tn", out_dtype=BF16, name=f"mla_uq_dw_rope_{tag}")], axis=0)
            big["w_ukv"] = _mm(rec["ck"], dkvf, mode="tn", out_dtype=BF16, name=f"mla_ukv_dw_{tag}")
            dcq = _mm(dqr_un, lw["wt_uq"][n_nope:], mode="nn", out_dtype=F32, name=f"mla_uq_dx_rope_{tag}")
            dcq = _mm(dqn, lw["wt_uq"], mode="nn", out_dtype=F32, add=dcq, b_rows=n_nope,
                      name=f"mla_uq_dx_nope_{tag}")
            dck = _mm(dkvf, lw["w_ukv"], mode="nt", out_dtype=F32, name=f"mla_ukv_dx_{tag}")
            dproj, st_p = _mla_prep_bwd(rec["proj"], _row(rep["mla_q_norm_g"][j]), _row(rep["mla_kv_norm_g"][j]),
                                        dcq, dck, dkr_un, name=f"mla_prep_bwd_{tag}")
            per_mla["mla_q_norm_g"][j] = st_p[0, :MLA_Q_RANK]
            per_mla["mla_kv_norm_g"][j] = st_p[0, MLA_Q_RANK:MLA_Q_RANK + MLA_KV_RANK]
            big["w_in"] = _mm(rec["h"], dproj, mode="tn", out_dtype=BF16, name=f"mla_in_dw_{tag}")
            dep = put_grads(layer, "mla", big)
            dh = _mm(dproj, lw["w_in"], mode="nt", out_dtype=BF16, name=f"mla_in_dx_{tag}")
        if layer > 0:
            below = saved[layer - 1]
            dx, st_n, dy2 = _adaln_gate_bwd(rec["x0"], _row(rep["norm_mix_g"][layer]), scale_m, shift_m, dh, dx, dep,
                                            below["y2"], _row(mod[layer - 1, N_MOD - 1]),
                                            name=f"adaln_mix_bwd_{tag}")
        else:
            dx, st_n = _adaln_bwd(rec["x0"], _row(rep["norm_mix_g"][layer]), scale_m, shift_m, dh, dx, dep,
                                  name=f"adaln_mix_bwd_{tag}")
        per_layer["norm_mix_g"][layer] = st_n[0]
        dmod[layer] = jnp.stack([st_n[2], st_n[1], dgate_m, dshift_f, dscale_f, dgate_f])
        if layer > 0:
            dgate_f = st_n[3]

    for d in (per_layer, per_gdn, per_mla):
        for k, v in d.items():
            grads[k] = jnp.stack(v)
    return loss, dx, jnp.stack(dmod), grads


BIG = ("gdn_w_in", "gdn_w_out", "mla_w_in", "mla_w_uq", "mla_w_ukv", "mla_w_out", "ffn_w_gate", "ffn_w_up",
       "ffn_w_down")
TRANSPOSED = ("gdn_w_in", "mla_w_uq", "ffn_w_gate", "ffn_w_up")
AHEAD = 3


def _view(k, a):
    return jnp.transpose(a, (0, 2, 1)) if k in TRANSPOSED else a
SMALL = ("ada_b", "norm_mix_g", "norm_ffn_g", "gdn_conv_w", "gdn_a_log", "gdn_dt_bias", "gdn_norm_g",
         "mla_q_norm_g", "mla_kv_norm_g", "final_norm_g")
WEIGHTS = ("ada_w", "ada_b", "norm_mix_g", "norm_ffn_g", "gdn_w_in", "gdn_conv_w", "gdn_a_log", "gdn_dt_bias",
           "gdn_norm_g", "gdn_w_out", "mla_w_in", "mla_q_norm_g", "mla_kv_norm_g", "mla_w_uq", "mla_w_ukv",
           "mla_w_out", "ffn_w_gate", "ffn_w_up", "ffn_w_down", "final_norm_g")


def _uq_to_kernel_layout(w, axis=-1):
    axis = axis % w.ndim
    lead, tail = w.shape[:axis], w.shape[axis + 1:]
    w4 = w.reshape(lead + (MLA_HEADS, MLA_QK) + tail)
    nope = lax.slice_in_dim(w4, 0, MLA_NOPE, axis=axis + 1).reshape(lead + (-1,) + tail)
    rope = lax.slice_in_dim(w4, MLA_NOPE, MLA_QK, axis=axis + 1).reshape(lead + (-1,) + tail)
    return jnp.concatenate([nope, rope], axis=axis)


def _uq_from_kernel_layout(w, axis=-1):
    axis = axis % w.ndim
    lead, tail = w.shape[:axis], w.shape[axis + 1:]
    nope = lax.slice_in_dim(w, 0, MLA_HEADS * MLA_NOPE, axis=axis).reshape(lead + (MLA_HEADS, MLA_NOPE) + tail)
    rope = lax.slice_in_dim(w, MLA_HEADS * MLA_NOPE, MLA_HEADS * MLA_QK, axis=axis).reshape(
        lead + (MLA_HEADS, MLA_ROPE) + tail)
    return jnp.concatenate([nope, rope], axis=axis + 1).reshape(lead + (-1,) + tail)


def _group_names(layer, kind):
    if kind == "ffn":
        return ("ffn_w_gate", "ffn_w_up", "ffn_w_down")
    return ("gdn_w_in", "gdn_w_out") if layer % 2 == 0 else ("mla_w_in", "mla_w_uq", "mla_w_ukv", "mla_w_out")


def _layer_index(name, layer):
    return layer if name.startswith("ffn") else layer // 2


def _cols(g):
    return jnp.transpose(g, (1, 0, 2)).reshape(g.shape[1], N_DEV * g.shape[2])


def _rows(g):
    return g.reshape(N_DEV * g.shape[1], g.shape[2])


def _uncols(full):
    r, c = full.shape
    return jnp.transpose(full.reshape(r, N_DEV, c // N_DEV), (1, 0, 2))


def _unrows(full):
    r, c = full.shape
    return full.reshape(N_DEV, r // N_DEV, c)


def _group_weights(layer, kind, got, token):
    if kind == "ffn":
        return {"wt_g": _rows(got["ffn_w_gate"]), "wt_u": _rows(got["ffn_w_up"]), "w_down": _rows(got["ffn_w_down"]),
                "dep_ffn": token}
    if layer % 2 == 0:
        wt_in = _rows(got["gdn_w_in"])
        return dict(wt_in=wt_in, wt_ab=jnp.pad(wt_in[GDN_MAIN:], ((0, LANES - 2 * GDN_HEADS), (0, 0))),
                    w_out=_rows(got["gdn_w_out"]), dep_mix=token)
    return dict(w_in=_rows(got["mla_w_in"]), wt_uq=_uq_to_kernel_layout(_rows(got["mla_w_uq"]), axis=0),
                w_ukv=_cols(got["mla_w_ukv"]), w_out=_rows(got["mla_w_out"]), dep_mix=token)


def _layer_grad_slots(kind, big):
    if kind == "ffn":
        return {"ffn_w_gate": _unrows(big["wt_g"]), "ffn_w_up": _unrows(big["wt_u"]),
                "ffn_w_down": _unrows(big["w_down"])}
    if kind == "gdn":
        return {"gdn_w_in": _unrows(big["wt_in"]), "gdn_w_out": _unrows(big["w_out"])}
    return {"mla_w_in": _unrows(big["w_in"]), "mla_w_uq": _unrows(_uq_from_kernel_layout(big["wt_uq"], axis=0)),
            "mla_w_ukv": _uncols(big["w_ukv"]), "mla_w_out": _unrows(big["w_out"])}


def _small_weights(tiny, rep):
    prm = jnp.zeros((2, 8, LANES), F32)
    prm = prm.at[:, 0, :GDN_HEADS].set(rep["gdn_a_log"]).at[:, 1, :GDN_HEADS].set(rep["gdn_dt_bias"])
    out = {
        "gdn_conv_wt": jnp.transpose(_gather_rows(tiny["gdn_conv_w"]), (0, 2, 1)),
        "mla_q_norm_g": jnp.transpose(tiny["mla_q_norm_g"], (1, 0, 2)).reshape(2, MLA_Q_RANK),
        "mla_kv_norm_g": jnp.transpose(tiny["mla_kv_norm_g"], (1, 0, 2)).reshape(2, MLA_KV_RANK),
        "gdn_gate_prm": prm,
    }
    for k in ("norm_mix_g", "norm_ffn_g", "gdn_norm_g", "final_norm_g"):
        out[k] = rep[k]
    return out


def _rope_tables(positions):
    inv_freq = ROPE_THETA ** (-jnp.arange(0, MLA_ROPE, 2, dtype=F32) / MLA_ROPE)
    ang = positions.astype(F32)[:, None] * inv_freq
    cos, sin = jnp.cos(ang), jnp.sin(ang)
    reps = LANES // MLA_ROPE
    return jnp.tile(jnp.concatenate([cos, cos], axis=1), (1, reps)), jnp.tile(
        jnp.concatenate([-sin, sin], axis=1), (1, reps))


def kernel(x, c, positions, ada_w, ada_b, norm_mix_g, norm_ffn_g, gdn_w_in, gdn_conv_w, gdn_a_log, gdn_dt_bias, gdn_norm_g, gdn_w_out, mla_w_in, mla_q_norm_g, mla_kv_norm_g, mla_w_uq, mla_w_ukv, mla_w_out, ffn_w_gate, ffn_w_up, ffn_w_down, final_norm_g, loss_target, m_ada_w, m_ada_b, m_norm_mix_g, m_norm_ffn_g, m_gdn_w_in, m_gdn_conv_w, m_gdn_a_log, m_gdn_dt_bias, m_gdn_norm_g, m_gdn_w_out, m_mla_w_in, m_mla_q_norm_g, m_mla_kv_norm_g, m_mla_w_uq, m_mla_w_ukv, m_mla_w_out, m_ffn_w_gate, m_ffn_w_up, m_ffn_w_down, m_final_norm_g, v_ada_w, v_ada_b, v_norm_mix_g, v_norm_ffn_g, v_gdn_w_in, v_gdn_conv_w, v_gdn_a_log, v_gdn_dt_bias, v_gdn_norm_g, v_gdn_w_out, v_mla_w_in, v_mla_q_norm_g, v_mla_kv_norm_g, v_mla_w_uq, v_mla_w_ukv, v_mla_w_out, v_ffn_w_gate, v_ffn_w_up, v_ffn_w_down, v_final_norm_g):
    W = dict(ada_w=ada_w, ada_b=ada_b, norm_mix_g=norm_mix_g, norm_ffn_g=norm_ffn_g, gdn_w_in=gdn_w_in,
             gdn_conv_w=gdn_conv_w, gdn_a_log=gdn_a_log, gdn_dt_bias=gdn_dt_bias, gdn_norm_g=gdn_norm_g,
             gdn_w_out=gdn_w_out, mla_w_in=mla_w_in, mla_q_norm_g=mla_q_norm_g, mla_kv_norm_g=mla_kv_norm_g,
             mla_w_uq=mla_w_uq, mla_w_ukv=mla_w_ukv, mla_w_out=mla_w_out, ffn_w_gate=ffn_w_gate,
             ffn_w_up=ffn_w_up, ffn_w_down=ffn_w_down, final_norm_g=final_norm_g)
    M = dict(ada_w=m_ada_w, ada_b=m_ada_b, norm_mix_g=m_norm_mix_g, norm_ffn_g=m_norm_ffn_g, gdn_w_in=m_gdn_w_in,
             gdn_conv_w=m_gdn_conv_w, gdn_a_log=m_gdn_a_log, gdn_dt_bias=m_gdn_dt_bias, gdn_norm_g=m_gdn_norm_g,
             gdn_w_out=m_gdn_w_out, mla_w_in=m_mla_w_in, mla_q_norm_g=m_mla_q_norm_g,
             mla_kv_norm_g=m_mla_kv_norm_g, mla_w_uq=m_mla_w_uq, mla_w_ukv=m_mla_w_ukv, mla_w_out=m_mla_w_out,
             ffn_w_gate=m_ffn_w_gate, ffn_w_up=m_ffn_w_up, ffn_w_down=m_ffn_w_down, final_norm_g=m_final_norm_g)
    V = dict(ada_w=v_ada_w, ada_b=v_ada_b, norm_mix_g=v_norm_mix_g, norm_ffn_g=v_norm_ffn_g, gdn_w_in=v_gdn_w_in,
             gdn_conv_w=v_gdn_conv_w, gdn_a_log=v_gdn_a_log, gdn_dt_bias=v_gdn_dt_bias, gdn_norm_g=v_gdn_norm_g,
             gdn_w_out=v_gdn_w_out, mla_w_in=v_mla_w_in, mla_q_norm_g=v_mla_q_norm_g,
             mla_kv_norm_g=v_mla_kv_norm_g, mla_w_uq=v_mla_w_uq, mla_w_ukv=v_mla_w_ukv, mla_w_out=v_mla_w_out,
             ffn_w_gate=v_ffn_w_gate, ffn_w_up=v_ffn_w_up, ffn_w_down=v_ffn_w_down, final_norm_g=v_final_norm_g)
    me = 4 * lax.axis_index("x") + 2 * lax.axis_index("y") + lax.axis_index("c")
    t = x.shape[1]
    wc = ada_w.shape[-1]

    groups = [(layer, kind) for layer in range(DEPTH) for kind in ("mix", "ffn")]

    def group_srcs(i):
        layer, kind = groups[i]
        return [_view(k, W[k])[_layer_index(k, layer)].astype(BF16) for k in _group_names(layer, kind)]

    tiny_shapes = [c.shape, gdn_conv_w.shape, mla_q_norm_g.shape, mla_kv_norm_g.shape]
    first = _gather_two_level([_pack([c, gdn_conv_w, mla_q_norm_g, mla_kv_norm_g])] + group_srcs(0),
                              name="gather_first")
    tiny_g = first[0]
    c_g, conv_g, qn_g, kvn_g = _unpack(tiny_g, tiny_shapes, lead=(N_DEV,))
    c_all = c_g.reshape(N_DEV, D_MODEL)
    rep = _small_weights({"gdn_conv_w": conv_g, "mla_q_norm_g": qn_g, "mla_kv_norm_g": kvn_g}, W)

    def start_group(i, dep):
        layer, kind = groups[i]
        return _exchange_start(group_srcs(i), scatter=False, name=f"gather_start_{kind}_l{layer}", dep=dep)


    b_cols = lax.dynamic_slice_in_dim(ada_b, me * wc, wc, axis=1).reshape(DEPTH, 1, wc)
    mod_part = _ada_mod(c_all, ada_w, b_cols, name="ada_mod")
    (mod_g,) = _exchange([mod_part], scatter=False, name="gather_mod")
    mod_mine = lax.dynamic_index_in_dim(mod_g, me, axis=2, keepdims=False)
    mod = jnp.transpose(mod_mine, (1, 0, 2)).reshape(DEPTH, N_MOD, D_MODEL)
    gather = {1: start_group(1, mod_g)}
    for i in range(2, AHEAD + 1):
        gather[i] = start_group(i, gather[i - 1][4])

    def get_weights(layer, kind, after):
        i = groups.index((layer, kind))
        names = _group_names(layer, kind)
        if i == 0:
            return _group_weights(layer, kind, dict(zip(names, first[1:])), gather[AHEAD][4])
        srcs, lands = _exchange_wait(gather[i], after, scatter=False, name=f"gather_wait_{kind}_l{layer}")
        token = jnp.zeros((8, LANES), F32)
        if i + AHEAD < len(groups):
            gather[i + AHEAD] = start_group(i + AHEAD, lands[0])
            token = gather[i + AHEAD][4]
        got = {k: lax.dynamic_update_index_in_dim(z, s, me, 0) for k, s, z in zip(names, srcs, lands)}
        return _group_weights(layer, kind, got, token)

    scatter = []

    def put_grads(layer, kind, big):
        slots = _layer_grad_slots(kind, big)
        started = _exchange_start(list(slots.values()), scatter=True, name=f"scatter_start_{kind}_l{layer}")
        scatter.append((layer, kind, list(slots.keys()), started))
        return started[4]

    cos_t, sin_t = _rope_tables(positions[0])
    loss, dx, dmod, g = _local_step(x[0], loss_target[0], mod, cos_t, sin_t, rep, get_weights, put_grads)

    parts = {k: [None] * W[k].shape[0] for k in BIG}
    res = {}

    def wait_group(entry, after):
        layer, kind, names, started = entry
        srcs, lands = _exchange_wait(started, after, scatter=True, name=f"scatter_wait_{kind}_l{layer}")
        for k, s, z in zip(names, srcs, lands):
            own = lax.dynamic_index_in_dim(s, me, 0, keepdims=False)
            parts[k][_layer_index(k, layer)] = lax.dynamic_update_index_in_dim(z, own, me, 0)

    for entry in scatter[:-1]:
        wait_group(entry, dx)
    early = [k for k in BIG if k not in scatter[-1][2]]
    def update(k):
        outs = _adamw(parts[k], _view(k, W[k]), _view(k, M[k]), _view(k, V[k]), name=f"adamw_{k}")
        return tuple(_view(k, o) for o in outs)

    for k in early:
        res[k] = update(k)
    loss, dmod, done = lax.optimization_barrier((loss, dmod, [res[k] for k in early]))
    for k, r in zip(early, done):
        res[k] = r

    small_local = [dmod.reshape(DEPTH, N_MOD * D_MODEL), g["norm_mix_g"], g["norm_ffn_g"],
                   jnp.transpose(g["gdn_conv_wt"], (0, 2, 1)), g["gdn_a_log"], g["gdn_dt_bias"], g["gdn_norm_g"],
                   g["mla_q_norm_g"], g["mla_kv_norm_g"], g["final_norm_g"], loss.reshape(1)]
    small_shapes = [a.shape for a in small_local]
    (small_g,) = _exchange([_pack(small_local)], scatter=False, name="gather_small_grads")
    small_sum = _unpack(_sum_parts(small_g, name="sum_small_grads"), small_shapes)
    loss = small_sum[-1][0]
    dmod_all = _unpack(small_g, small_shapes[:1], lead=(N_DEV,))[0]
    sg = dict(zip(SMALL, small_sum))
    wait_group(scatter[-1], small_g)
    sg["gdn_conv_w"] = lax.dynamic_slice_in_dim(sg["gdn_conv_w"], me * gdn_conv_w.shape[1], gdn_conv_w.shape[1], 1)
    sg["mla_q_norm_g"] = lax.dynamic_slice_in_dim(sg["mla_q_norm_g"], me * mla_q_norm_g.shape[1],
                                                  mla_q_norm_g.shape[1], 1)
    sg["mla_kv_norm_g"] = lax.dynamic_slice_in_dim(sg["mla_kv_norm_g"], me * mla_kv_norm_g.shape[1],
                                                   mla_kv_norm_g.shape[1], 1)

    dmod_cols = jnp.transpose(lax.dynamic_slice_in_dim(dmod_all, me * wc, wc, axis=2), (1, 0, 2))
    res["ada_w"] = _ada_grad_adamw(c_all, dmod_cols, ada_w, m_ada_w, v_ada_w, name="ada_w_grad_adamw")
    for k in BIG:
        if k not in early:
            res[k] = update(k)
    shapes = [W[k].shape for k in SMALL]
    packed = [_pack([d[k] for k in SMALL]) for d in (sg, W, M, V)]
    outs = _adamw([packed[0][None]], packed[1][None], packed[2][None], packed[3][None], name="adamw_small")
    unpacked = [_unpack(o[0], shapes) for o in outs]
    for i, k in enumerate(SMALL):
        res[k] = tuple(u[i] for u in unpacked)

    return (loss, dx[None], *[res[k][0] for k in WEIGHTS], *[res[k][1] for k in WEIGHTS],
            *[res[k][2] for k in WEIGHTS], *[res[k][3] for k in WEIGHTS])
```

```python
import math

import jax
import jax.numpy as jnp
from jax import lax
from jax.experimental import pallas as pl
from jax.experimental.pallas import tpu as pltpu

F32 = jnp.float32
BF16 = jnp.bfloat16
MXU_DTYPE = jnp.bfloat16

N_DEV = 8
D_MODEL = 1024
DEPTH = 4
GDN_HEADS = 8
GDN_HEAD_DIM = 128
GDN_KEY_DIM = GDN_HEADS * GDN_HEAD_DIM
GDN_CHUNK = 64
GDN_HEAD_BATCH = 8
GDN_CONV = 4
GDN_PREP_HEADS = 2
GDN_MAIN = 4 * GDN_KEY_DIM
MLA_HEADS = 8
MLA_NOPE = 128
MLA_ROPE = 64
MLA_V = 128
MLA_Q_RANK = 384
MLA_KV_RANK = 256
MLA_IN = MLA_Q_RANK + MLA_KV_RANK + MLA_ROPE
MLA_QK = MLA_NOPE + MLA_ROPE
ROPE_THETA = 10000.0
D_FF = 2816
N_MOD = 6
EPS = 1e-6
LANES = 128
VMEM_LIMIT = 48 * 1024 * 1024

ADAM_LR = 0.001
ADAM_B1 = 0.9
ADAM_B2 = 0.999
ADAM_EPS = 1e-08
ADAM_WD = 0.01
ADAM_STEP = 10
ADAM_BC1 = 1.0 - ADAM_B1 ** ADAM_STEP
ADAM_BC2 = 1.0 - ADAM_B2 ** ADAM_STEP

NN = (((1,), (0,)), ((), ()))
NT = (((1,), (1,)), ((), ()))
TN = (((0,), (0,)), ((), ()))
NEG = -1e30


def _dotb(a, b, dims):
    return lax.dot_general(a.astype(MXU_DTYPE), b.astype(MXU_DTYPE), dims, preferred_element_type=F32)


def _split(a):
    hi = a.astype(BF16)
    return hi, (a - hi.astype(F32)).astype(BF16)


def _dotf(a, b, dims):
    ah, al = _split(a)
    bh, bl = _split(b)
    dot = lambda u, v: lax.dot_general(u, v, dims, preferred_element_type=F32)
    return dot(ah, bh) + (dot(ah, bl) + dot(al, bh))


def _params(*sem):
    return pltpu.CompilerParams(dimension_semantics=sem, vmem_limit_bytes=VMEM_LIMIT)


def _pick(n, pref, mult=LANES):
    best = None
    t = mult
    while t <= min(n, pref):
        if n % t == 0:
            best = t
        t += mult
    return best if best is not None else n


def _sigmoid(z):
    return 0.5 * jnp.tanh(0.5 * z) + 0.5


def _exchange(arrays, *, scatter, name):
    n = len(arrays)
    out_shape = tuple(
        jax.ShapeDtypeStruct(a.shape if scatter else (N_DEV,) + a.shape, a.dtype) for a in arrays)

    def body(*refs):
        ins, outs = refs[:n], refs[n:2 * n]
        send_sems, recv_sems, local_sems = refs[2 * n:]
        x, y, c = lax.axis_index("x"), lax.axis_index("y"), lax.axis_index("c")
        me = 4 * x + 2 * y + c
        copies = []
        for k in range(n):
            src_own = ins[k].at[me] if scatter else ins[k]
            own = pltpu.make_async_copy(src_own, outs[k].at[me], local_sems.at[k])
            own.start()
            copies.append(own)
        sends = []
        for p in range(1, N_DEV):
            px, py, pc = x ^ ((p >> 2) & 1), y ^ ((p >> 1) & 1), c ^ (p & 1)
            peer = 4 * px + 2 * py + pc
            for k in range(n):
                cp = pltpu.make_async_remote_copy(
                    src_ref=ins[k].at[peer] if scatter else ins[k],
                    dst_ref=outs[k].at[me],
                    send_sem=send_sems.at[k, p - 1],
                    recv_sem=recv_sems.at[k, p - 1],
                    device_id=(px, py, pc),
                    device_id_type=pl.DeviceIdType.MESH,
                )
                cp.start()
                sends.append((cp, k, peer, p))
        for cp, k, peer, p in sends:
            pltpu.make_async_remote_copy(
                src_ref=ins[k].at[peer] if scatter else ins[k],
                dst_ref=outs[k].at[peer],
                send_sem=send_sems.at[k, p - 1],
                recv_sem=recv_sems.at[k, p - 1],
                device_id=(x, y, c),
                device_id_type=pl.DeviceIdType.MESH,
            ).wait_recv()
        for cp, _, _, _ in sends:
            cp.wait_send()
        for own in copies:
            own.wait()

    any_spec = pl.BlockSpec(memory_space=pl.ANY)
    outs = pl.pallas_call(
        body,
        name=name,
        out_shape=out_shape,
        in_specs=[any_spec] * n,
        out_specs=tuple([any_spec] * n),
        scratch_shapes=[
            pltpu.SemaphoreType.DMA((n, N_DEV - 1)),
            pltpu.SemaphoreType.DMA((n, N_DEV - 1)),
            pltpu.SemaphoreType.DMA((n,)),
        ],
        compiler_params=pltpu.CompilerParams(has_side_effects=True),
    )(*arrays)
    return list(outs)


def _gather_two_level(arrays, *, name):
    n = len(arrays)
    out_shape = tuple(jax.ShapeDtypeStruct((N_DEV,) + a.shape, a.dtype) for a in arrays)

    def body(*refs):
        ins, outs = refs[:n], refs[n:2 * n]
        send_sems, recv_sems, local_sems = refs[2 * n:]
        x, y, c = lax.axis_index("x"), lax.axis_index("y"), lax.axis_index("c")
        me = 4 * x + 2 * y + c
        sibling = (x, y, 1 - c)
        chips = [(1 - x, y), (x, 1 - y), (1 - x, 1 - y)]

        def slot(px, py, pc):
            return 4 * px + 2 * py + pc

        def copy(k, q, block, to, src=None):
            return pltpu.make_async_remote_copy(
                src_ref=outs[k].at[slot(*block)] if src is None else src,
                dst_ref=outs[k].at[slot(*block)],
                send_sem=send_sems.at[k, q], recv_sem=recv_sems.at[k, q],
                device_id=to, device_id_type=pl.DeviceIdType.MESH)

        own = [pltpu.make_async_copy(ins[k], outs[k].at[me], local_sems.at[k]) for k in range(n)]
        for cp in own:
            cp.start()
        first = []
        for k in range(n):
            first.append(copy(k, 0, (x, y, c), sibling, src=ins[k]))
            first += [copy(k, 1 + j, (x, y, c), (*chip, c), src=ins[k]) for j, chip in enumerate(chips)]
        for cp in first:
            cp.start()
        passed = []
        for j, chip in enumerate(chips):
            for k in range(n):
                copy(k, 1 + j, (*chip, c), (x, y, c)).wait_recv()
                fwd = copy(k, 4 + j, (*chip, c), sibling)
                fwd.start()
                passed.append(fwd)
        for k in range(n):
            copy(k, 0, sibling, (x, y, c)).wait_recv()
            for j, chip in enumerate(chips):
                copy(k, 4 + j, (*chip, 1 - c), (x, y, c)).wait_recv()
        for cp in first + passed:
            cp.wait_send()
        for cp in own:
            cp.wait()

    any_spec = pl.BlockSpec(memory_space=pl.ANY)
    outs = pl.pallas_call(
        body, name=name, out_shape=out_shape, in_specs=[any_spec] * n, out_specs=tuple([any_spec] * n),
        scratch_shapes=[pltpu.SemaphoreType.DMA((n, N_DEV - 1)), pltpu.SemaphoreType.DMA((n, N_DEV - 1)),
                        pltpu.SemaphoreType.DMA((n,))],
        compiler_params=pltpu.CompilerParams(has_side_effects=True),
    )(*arrays)
    return list(outs)


def _peer(x, y, c, p):
    return x ^ ((p >> 2) & 1), y ^ ((p >> 1) & 1), c ^ (p & 1)


def _exchange_start(arrays, *, scatter, name, dep=None):
    n = len(arrays)
    deps = [] if dep is None else [dep]
    lands = [lax.empty(a.shape if scatter else (N_DEV,) + a.shape, a.dtype) for a in arrays]

    def body(*refs):
        ins, zones = refs[:n], refs[n:2 * n]
        send_sems, recv_sems = refs[2 * n + len(deps)], refs[2 * n + len(deps) + 1]
        token = refs[-1]
        x, y, c = lax.axis_index("x"), lax.axis_index("y"), lax.axis_index("c")
        me = 4 * x + 2 * y + c
        for p in range(1, N_DEV):
            px, py, pc = _peer(x, y, c, p)
            for k in range(n):
                pltpu.make_async_remote_copy(
                    src_ref=ins[k].at[4 * px + 2 * py + pc] if scatter else ins[k],
                    dst_ref=zones[k].at[me],
                    send_sem=send_sems.at[k * (N_DEV - 1) + p - 1],
                    recv_sem=recv_sems.at[k * (N_DEV - 1) + p - 1],
                    device_id=(px, py, pc),
                    device_id_type=pl.DeviceIdType.MESH,
                ).start()
        token[...] = jnp.zeros_like(token)

    hbm = pl.BlockSpec(memory_space=pltpu.HBM)
    sem = pl.BlockSpec(memory_space=pltpu.SEMAPHORE)
    outs = pl.pallas_call(
        body,
        name=name,
        out_shape=(pltpu.SemaphoreType.DMA((n * (N_DEV - 1),)), pltpu.SemaphoreType.DMA((n * (N_DEV - 1),)),
                   *[pltpu.HBM(a.shape, a.dtype) for a in arrays], *[pltpu.HBM(z.shape, z.dtype) for z in lands],
                   jax.ShapeDtypeStruct((8, LANES), F32)),
        in_specs=[hbm] * (2 * n) + [pl.BlockSpec(memory_space=pl.ANY)] * len(deps),
        out_specs=(sem, sem, *[hbm] * (2 * n), pl.BlockSpec(memory_space=pltpu.VMEM)),
        input_output_aliases={k: 2 + k for k in range(2 * n)},
        compiler_params=pltpu.CompilerParams(has_side_effects=pltpu.SideEffectType.DATAFLOW_SIDE_EFFECTING),
    )(*[pltpu.with_memory_space_constraint(a, pltpu.HBM) for a in arrays],
      *[pltpu.with_memory_space_constraint(z, pltpu.HBM) for z in lands], *deps)
    return outs[0], outs[1], list(outs[2:2 + n]), list(outs[2 + n:2 + 2 * n]), outs[-1]


def _exchange_wait(started, after, *, scatter, name):
    send_sems, recv_sems, srcs, lands, _ = started
    n = len(srcs)

    def body(*refs):
        ins, zones = refs[:n], refs[n:2 * n]
        s_sems, r_sems = refs[2 * n], refs[2 * n + 1]
        x, y, c = lax.axis_index("x"), lax.axis_index("y"), lax.axis_index("c")
        for p in range(1, N_DEV):
            px, py, pc = _peer(x, y, c, p)
            peer = 4 * px + 2 * py + pc
            for k in range(n):
                cp = pltpu.make_async_remote_copy(
                    src_ref=ins[k].at[peer] if scatter else ins[k],
                    dst_ref=zones[k].at[peer],
                    send_sem=s_sems.at[k * (N_DEV - 1) + p - 1],
                    recv_sem=r_sems.at[k * (N_DEV - 1) + p - 1],
                    device_id=(px, py, pc),
                    device_id_type=pl.DeviceIdType.MESH,
                )
                cp.wait_send()
                cp.wait_recv()

    hbm = pl.BlockSpec(memory_space=pltpu.HBM)
    sem = pl.BlockSpec(memory_space=pltpu.SEMAPHORE)
    outs = pl.pallas_call(
        body,
        name=name,
        out_shape=tuple(pltpu.HBM(a.shape, a.dtype) for a in srcs + lands),
        in_specs=[hbm] * (2 * n) + [sem, sem, pl.BlockSpec(memory_space=pl.ANY)],
        out_specs=tuple([hbm] * (2 * n)),
        input_output_aliases={k: k for k in range(2 * n)},
        compiler_params=pltpu.CompilerParams(has_side_effects=pltpu.SideEffectType.DATAFLOW_SIDE_EFFECTING),
    )(*srcs, *lands, send_sems, recv_sems, after)
    return list(outs[:n]), list(outs[n:])


def _mm(a, b, *, mode, out_dtype, name, add=None, tm=512, tn=512, b_rows=None, dep=None):
    rows_b = b.shape[0] if b_rows is None else b_rows
    if mode == "nn":
        (m, kd), nd = a.shape, b.shape[1]
        assert kd == rows_b
    elif mode == "nt":
        (m, kd), nd = a.shape, rows_b
    else:
        (kd, m), nd = a.shape, b.shape[1]
    tm = _pick(m, tm, LANES if mode == "tn" else 16)
    tn = _pick(nd, tn)
    dims = {"nn": NN, "nt": NT, "tn": TN}[mode]
    ni, nj = m // tm, nd // tn
    a_bytes, b_bytes = a.size * a.dtype.itemsize, b.size * b.dtype.itemsize
    i_outer = a_bytes + ni * b_bytes <= b_bytes + nj * a_bytes
    ij = (lambda g0, g1: (g0, g1)) if i_outer else (lambda g0, g1: (g1, g0))
    a_spec = (pl.BlockSpec((kd, tm), lambda g0, g1: (0, ij(g0, g1)[0])) if mode == "tn"
              else pl.BlockSpec((tm, kd), lambda g0, g1: (ij(g0, g1)[0], 0)))
    b_spec = (pl.BlockSpec((tn, kd), lambda g0, g1: (ij(g0, g1)[1], 0)) if mode == "nt"
              else pl.BlockSpec((kd, tn), lambda g0, g1: (0, ij(g0, g1)[1])))
    o_spec = pl.BlockSpec((tm, tn), lambda g0, g1: ij(g0, g1))
    has_add = add is not None

    def body(*refs):
        a_ref, b_ref = refs[0], refs[1]
        o_ref = refs[-1]
        acc = _dotb(a_ref[...], b_ref[...], dims)
        if has_add:
            acc = acc + refs[2][...].astype(F32)
        o_ref[...] = acc.astype(o_ref.dtype)

    ins = [a, b] + ([add] if has_add else []) + ([] if dep is None else [dep])
    specs = ([a_spec, b_spec] + ([o_spec] if has_add else [])
             + ([] if dep is None else [pl.BlockSpec((8, LANES), lambda g0, g1: (0, 0))]))
    return pl.pallas_call(
        body, name=name, grid=(ni, nj) if i_outer else (nj, ni), in_specs=specs, out_specs=o_spec,
        out_shape=jax.ShapeDtypeStruct((m, nd), out_dtype),
        compiler_params=_params("parallel", "parallel"),
    )(*ins)


def _mm_resid(a, b, x, gate, *, name, tm=256, tn=1024):
    m, kd = a.shape
    nd = b.shape[1]
    tm = _pick(m, tm, 16)
    tn = _pick(nd, tn)
    o_spec = pl.BlockSpec((tm, tn), lambda i, j: (i, j))

    def body(a_ref, b_ref, x_ref, g_ref, xo_ref, y_ref):
        y = _dotb(a_ref[...], b_ref[...], NN)
        y_ref[...] = y.astype(y_ref.dtype)
        xo_ref[...] = x_ref[...] + g_ref[...] * y

    return pl.pallas_call(
        body, name=name, grid=(m // tm, nd // tn),
        in_specs=[pl.BlockSpec((tm, kd), lambda i, j: (i, 0)), pl.BlockSpec((kd, tn), lambda i, j: (0, j)),
                  o_spec, pl.BlockSpec((1, tn), lambda i, j: (0, j))],
        out_specs=(o_spec, o_spec),
        out_shape=(jax.ShapeDtypeStruct((m, nd), F32), jax.ShapeDtypeStruct((m, nd), BF16)),
        compiler_params=_params("parallel", "parallel"),
    )(a, b, x, gate)


ROWS = 256


def _row_spec(width, rows=ROWS):
    return pl.BlockSpec((rows, width), lambda i: (i, 0))


def _const_spec(shape):
    return pl.BlockSpec(shape, lambda i: tuple(0 for _ in shape))


def _adaln_fwd(x, g, scale, shift, *, name):
    t, d = x.shape

    def body(x_ref, g_ref, sc_ref, sh_ref, h_ref):
        xv = x_ref[...]
        r = lax.rsqrt(jnp.mean(xv * xv, axis=-1, keepdims=True) + EPS)
        h_ref[...] = (xv * r * g_ref[...] * (1.0 + sc_ref[...]) + sh_ref[...]).astype(h_ref.dtype)

    return pl.pallas_call(
        body, name=name, grid=(t // ROWS,),
        in_specs=[_row_spec(d), _const_spec((1, d)), _const_spec((1, d)), _const_spec((1, d))],
        out_specs=_row_spec(d), out_shape=jax.ShapeDtypeStruct((t, d), BF16),
        compiler_params=_params("parallel"),
    )(x, g, scale, shift)


def _adaln_bwd(x, g, scale, shift, dh, dres, dep, *, name):
    t, d = x.shape

    def body(x_ref, g_ref, sc_ref, sh_ref, dh_ref, dr_ref, dep_ref, dx_ref, st_ref):
        @pl.when(pl.program_id(0) == 0)
        def _():
            st_ref[...] = jnp.zeros_like(st_ref)

        xv = x_ref[...]
        dhv = dh_ref[...].astype(F32)
        gv = g_ref[...]
        r = lax.rsqrt(jnp.mean(xv * xv, axis=-1, keepdims=True) + EPS)
        xh = xv * r
        nv = xh * gv
        dn = dhv * (1.0 + sc_ref[...])
        dxh = dn * gv
        dx_ref[...] = dr_ref[...] + r * (dxh - xh * jnp.mean(dxh * xh, axis=-1, keepdims=True))
        st_ref[0:1, :] += jnp.sum(dn * xh, axis=0, keepdims=True)
        st_ref[1:2, :] += jnp.sum(dhv * nv, axis=0, keepdims=True)
        st_ref[2:3, :] += jnp.sum(dhv, axis=0, keepdims=True)

    return pl.pallas_call(
        body, name=name, grid=(t // ROWS,),
        in_specs=[_row_spec(d), _const_spec((1, d)), _const_spec((1, d)), _const_spec((1, d)),
                  _row_spec(d), _row_spec(d), _const_spec((8, LANES))],
        out_specs=(_row_spec(d), _const_spec((8, d))),
        out_shape=(jax.ShapeDtypeStruct((t, d), F32), jax.ShapeDtypeStruct((8, d), F32)),
        compiler_params=_params("arbitrary"),
    )(x, g, scale, shift, dh, dres, dep)


def _adaln_gate_bwd(x, g, scale, shift, dh, dres, dep, y_up, gate_up, *, name):
    t, d = x.shape

    def body(x_ref, g_ref, sc_ref, sh_ref, dh_ref, dr_ref, dep_ref, y_ref, gu_ref, dx_ref, st_ref, dy_ref):
        @pl.when(pl.program_id(0) == 0)
        def _():
            st_ref[...] = jnp.zeros_like(st_ref)

        xv = x_ref[...]
        dhv = dh_ref[...].astype(F32)
        gv = g_ref[...]
        r = lax.rsqrt(jnp.mean(xv * xv, axis=-1, keepdims=True) + EPS)
        xh = xv * r
        nv = xh * gv
        dn = dhv * (1.0 + sc_ref[...])
        dxh = dn * gv
        dx = dr_ref[...] + r * (dxh - xh * jnp.mean(dxh * xh, axis=-1, keepdims=True))
        dx_ref[...] = dx
        dy_ref[...] = (dx * gu_ref[...]).astype(dy_ref.dtype)
        st_ref[0:1, :] += jnp.sum(dn * xh, axis=0, keepdims=True)
        st_ref[1:2, :] += jnp.sum(dhv * nv, axis=0, keepdims=True)
        st_ref[2:3, :] += jnp.sum(dhv, axis=0, keepdims=True)
        st_ref[3:4, :] += jnp.sum(dx * y_ref[...].astype(F32), axis=0, keepdims=True)

    return pl.pallas_call(
        body, name=name, grid=(t // ROWS,),
        in_specs=[_row_spec(d), _const_spec((1, d)), _const_spec((1, d)), _const_spec((1, d)),
                  _row_spec(d), _row_spec(d), _const_spec((8, LANES)), _row_spec(d), _const_spec((1, d))],
        out_specs=(_row_spec(d), _const_spec((8, d)), _row_spec(d)),
        out_shape=(jax.ShapeDtypeStruct((t, d), F32), jax.ShapeDtypeStruct((8, d), F32),
                   jax.ShapeDtypeStruct((t, d), BF16)),
        compiler_params=_params("arbitrary"),
    )(x, g, scale, shift, dh, dres, dep, y_up, gate_up)


def _gate_bwd(dxo, y, gate, dep, *, name):
    t, d = dxo.shape

    def body(dx_ref, y_ref, g_ref, dep_ref, dy_ref, st_ref):
        @pl.when(pl.program_id(0) == 0)
        def _():
            st_ref[...] = jnp.zeros_like(st_ref)

        dxv = dx_ref[...]
        dy_ref[...] = (dxv * g_ref[...]).astype(dy_ref.dtype)
        st_ref[0:1, :] += jnp.sum(dxv * y_ref[...], axis=0, keepdims=True)

    return pl.pallas_call(
        body, name=name, grid=(t // ROWS,),
        in_specs=[_row_spec(d), _row_spec(d), _const_spec((1, d)), _const_spec((8, LANES))],
        out_specs=(_row_spec(d), _const_spec((8, d))),
        out_shape=(jax.ShapeDtypeStruct((t, d), BF16), jax.ShapeDtypeStruct((8, d), F32)),
        compiler_params=_params("arbitrary"),
    )(dxo, y, gate, dep)


def _loss_head(x, g, target, *, name):
    t, d = x.shape

    def body(x_ref, g_ref, t_ref, dx_ref, st_ref, ls_ref):
        @pl.when(pl.program_id(0) == 0)
        def _():
            st_ref[...] = jnp.zeros_like(st_ref)
            ls_ref[...] = jnp.zeros_like(ls_ref)

        xv = x_ref[...]
        gv = g_ref[...]
        r = lax.rsqrt(jnp.mean(xv * xv, axis=-1, keepdims=True) + EPS)
        xh = xv * r
        err = xh * gv - t_ref[...]
        ls_ref[...] += 0.5 * jnp.sum(jnp.mean(err * err, axis=-1, keepdims=True))
        dy = err * (1.0 / d)
        dxh = dy * gv
        dx_ref[...] = r * (dxh - xh * jnp.mean(dxh * xh, axis=-1, keepdims=True))
        st_ref[0:1, :] += jnp.sum(dy * xh, axis=0, keepdims=True)

    return pl.pallas_call(
        body, name=name, grid=(t // ROWS,),
        in_specs=[_row_spec(d), _const_spec((1, d)), _row_spec(d)],
        out_specs=(_row_spec(d), _const_spec((8, d)), _const_spec((8, LANES))),
        out_shape=(jax.ShapeDtypeStruct((t, d), F32), jax.ShapeDtypeStruct((8, d), F32),
                   jax.ShapeDtypeStruct((8, LANES), F32)),
        compiler_params=_params("arbitrary"),
    )(x, g, target)


FFN_BLOCK = D_FF // 2
FFN_ROWS = 512


def _ffn_chunks(width):
    edges = [min(width, 3 * LANES * i) for i in range(width // (3 * LANES) + 2)]
    return [slice(lo, hi) for lo, hi in zip(edges[:-1], edges[1:]) if hi > lo]


def _ffn_gu_fwd(h, wg, wu, dep, *, name):
    t, d = h.shape
    tn = FFN_BLOCK

    chunks = _ffn_chunks(tn)
    rows = _pick(t, FFN_ROWS, 16)

    def body(h_ref, wg_ref, wu_ref, dep_ref, s_ref, a_ref, b_ref):
        hv = h_ref[...]
        ab = [(_dotb(hv, wg_ref[sl, :], NT), _dotb(hv, wu_ref[sl, :], NT)) for sl in chunks]
        for sl, (a, b) in zip(chunks, ab):
            s_ref[:, sl] = (a * _sigmoid(a) * b).astype(s_ref.dtype)
            a_ref[:, sl] = a.astype(a_ref.dtype)
            b_ref[:, sl] = b.astype(b_ref.dtype)

    w_spec = pl.BlockSpec((tn, d), lambda j, i: (j, 0))
    o_spec = pl.BlockSpec((rows, tn), lambda j, i: (i, j))
    return pl.pallas_call(
        body, name=name, grid=(D_FF // tn, t // rows),
        in_specs=[pl.BlockSpec((rows, d), lambda j, i: (i, 0)), w_spec, w_spec,
                  pl.BlockSpec((8, LANES), lambda j, i: (0, 0))],
        out_specs=(o_spec, o_spec, o_spec),
        out_shape=(jax.ShapeDtypeStruct((t, D_FF), BF16),) * 3,
        compiler_params=_params("parallel", "parallel"),
    )(h, wg, wu, dep)


def _ffn_down_dx(dy, w_down, a, b, *, name):
    t, d = dy.shape
    tn = FFN_BLOCK

    chunks = _ffn_chunks(tn)
    rows = _pick(t, FFN_ROWS, 16)

    def body(dy_ref, w_ref, a_ref, b_ref, da_ref, db_ref):
        dyv = dy_ref[...]
        ds = [_dotb(dyv, w_ref[sl, :], NT) for sl in chunks]
        for sl, dsc in zip(chunks, ds):
            av = a_ref[:, sl].astype(F32)
            sg = _sigmoid(av)
            da_ref[:, sl] = (dsc * b_ref[:, sl].astype(F32) * sg * (1.0 + av * (1.0 - sg))).astype(da_ref.dtype)
            db_ref[:, sl] = (dsc * av * sg).astype(db_ref.dtype)

    o_spec = pl.BlockSpec((rows, tn), lambda j, i: (i, j))
    return pl.pallas_call(
        body, name=name, grid=(D_FF // tn, t // rows),
        in_specs=[pl.BlockSpec((rows, d), lambda j, i: (i, 0)), pl.BlockSpec((tn, d), lambda j, i: (j, 0)),
                  o_spec, o_spec],
        out_specs=(o_spec, o_spec),
        out_shape=(jax.ShapeDtypeStruct((t, D_FF), BF16),) * 2,
        compiler_params=_params("parallel", "parallel"),
    )(dy, w_down, a, b)


def _shift_rows(v, s, rows):
    if s == 0:
        return v
    return jnp.where(rows >= s, pltpu.roll(v, s, 0), 0.0)


def _unshift_rows(v, s, rows, t):
    if s == 0:
        return v
    return jnp.where(rows < t - s, pltpu.roll(v, t - s, 0), 0.0)


def _conv_silu(x, w, rows):
    z = w[GDN_CONV - 1:GDN_CONV, :] * x
    for j in range(GDN_CONV - 1):
        z = z + w[j:j + 1, :] * _shift_rows(x, GDN_CONV - 1 - j, rows)
    sg = _sigmoid(z)
    return z, sg, z * sg


def _gdn_prep_fwd(proj, conv_wt, *, name):
    t = proj.shape[0]
    nh = GDN_HEADS

    hp = GDN_PREP_HEADS
    wd = hp * LANES

    def body(x_ref, w_ref, y_ref):
        j = pl.program_id(0) * hp
        rows = lax.broadcasted_iota(jnp.int32, (t, LANES), 0)
        qscale = jnp.where(j < nh, GDN_HEAD_DIM ** -0.5, 1.0)
        for i in range(hp):
            sl = slice(i * LANES, (i + 1) * LANES)
            _, _, s = _conv_silu(x_ref[:, sl], w_ref[:, sl], rows)
            rs = lax.rsqrt(jnp.sum(s * s, axis=-1, keepdims=True) + EPS)
            y_ref[:, sl] = jnp.where(j < 2 * nh, s * rs * qscale, s)

    return pl.pallas_call(
        body, name=name, grid=(3 * nh // hp,),
        in_specs=[pl.BlockSpec((t, wd), lambda j: (0, j)), pl.BlockSpec((GDN_CONV, wd), lambda j: (0, j))],
        out_specs=pl.BlockSpec((t, wd), lambda j: (0, j)),
        out_shape=jax.ShapeDtypeStruct((t, 3 * GDN_KEY_DIM), F32),
        compiler_params=_params("parallel"),
    )(proj, conv_wt)


def _gdn_prep_bwd(proj, conv_wt, dy, *, name):
    t = proj.shape[0]
    nh = GDN_HEADS

    hp = GDN_PREP_HEADS
    wd = hp * LANES
    per_seg = nh // hp

    def body(x_ref, w_ref, dy_ref, dx_ref, dw_ref):
        j = pl.program_id(0) * hp
        rows = lax.broadcasted_iota(jnp.int32, (t, LANES), 0)
        qscale = jnp.where(j < nh, GDN_HEAD_DIM ** -0.5, 1.0)
        for i in range(hp):
            sl = slice(i * LANES, (i + 1) * LANES)
            x = x_ref[:, sl]
            w = w_ref[:, sl]
            z, sg, s = _conv_silu(x, w, rows)
            rs = lax.rsqrt(jnp.sum(s * s, axis=-1, keepdims=True) + EPS)
            dyv = dy_ref[:, sl]
            nv = s * rs
            de = dyv * qscale
            ds_qk = rs * (de - nv * jnp.sum(de * nv, axis=-1, keepdims=True))
            ds = jnp.where(j < 2 * nh, ds_qk, dyv)
            dz = ds * sg * (1.0 + z * (1.0 - sg))
            dx = w[GDN_CONV - 1:GDN_CONV, :] * dz
            dw_ref[GDN_CONV - 1:GDN_CONV, sl] = jnp.sum(dz * x, axis=0, keepdims=True)
            for k in range(GDN_CONV - 1):
                sh = GDN_CONV - 1 - k
                dx = dx + w[k:k + 1, :] * _unshift_rows(dz, sh, rows, t)
                dw_ref[k:k + 1, sl] = jnp.sum(dz * _shift_rows(x, sh, rows), axis=0, keepdims=True)
            dx_ref[:, sl] = dx.astype(dx_ref.dtype)

    return pl.pallas_call(
        body, name=name, grid=(3 * nh // hp,),
        in_specs=[pl.BlockSpec((t, wd), lambda j: (0, j)), pl.BlockSpec((GDN_CONV, wd), lambda j: (0, j)),
                  pl.BlockSpec((None, t, wd), lambda j: (j // per_seg, 0, j % per_seg))],
        out_specs=(pl.BlockSpec((t, wd), lambda j: (0, j)), pl.BlockSpec((GDN_CONV, wd), lambda j: (0, j))),
        out_shape=(jax.ShapeDtypeStruct((t, 3 * GDN_KEY_DIM), BF16),
                   jax.ShapeDtypeStruct((GDN_CONV, 3 * GDN_KEY_DIM), F32)),
        compiler_params=_params("parallel"),
    )(proj, conv_wt, dy)


def _softplus(z):
    return jnp.maximum(z, 0.0) + jnp.log(1.0 + jnp.exp(-jnp.abs(z)))


def _gdn_gate_fwd(ab, prm, *, name):
    t = ab.shape[0]

    def body(ab_ref, p_ref, o_ref):
        v = ab_ref[...]
        lane = lax.broadcasted_iota(jnp.int32, v.shape, 1)
        g = -jnp.exp(p_ref[0:1, :]) * _softplus(v + p_ref[1:2, :])
        o_ref[...] = jnp.where(lane < GDN_HEADS, g, jnp.where(lane < 2 * GDN_HEADS, _sigmoid(v), 0.0))

    return pl.pallas_call(
        body, name=name, grid=(t // ROWS,),
        in_specs=[_row_spec(LANES), _const_spec((8, LANES))], out_specs=_row_spec(LANES),
        out_shape=jax.ShapeDtypeStruct((t, LANES), F32), compiler_params=_params("parallel"),
    )(ab, prm)


def _gdn_gate_bwd(ab, prm, dgb, *, name):
    t = ab.shape[0]

    def body(ab_ref, p_ref, d_ref, o_ref, st_ref):
        @pl.when(pl.program_id(0) == 0)
        def _():
            st_ref[...] = jnp.zeros_like(st_ref)

        v = ab_ref[...]
        dv = d_ref[...]
        lane = lax.broadcasted_iota(jnp.int32, v.shape, 1)
        is_a = lane < GDN_HEADS
        is_b = jnp.logical_and(lane >= GDN_HEADS, lane < 2 * GDN_HEADS)
        a_exp = jnp.exp(p_ref[0:1, :])
        zz = v + p_ref[1:2, :]
        g = -a_exp * _softplus(zz)
        da = dv * (-a_exp) * _sigmoid(zz)
        beta = _sigmoid(v)
        db = dv * beta * (1.0 - beta)
        o_ref[...] = jnp.where(is_a, da, jnp.where(is_b, db, 0.0)).astype(o_ref.dtype)
        st_ref[0:1, :] += jnp.sum(jnp.where(is_a, dv * g, 0.0), axis=0, keepdims=True)
        st_ref[1:2, :] += jnp.sum(jnp.where(is_a, da, 0.0), axis=0, keepdims=True)

    return pl.pallas_call(
        body, name=name, grid=(t // ROWS,),
        in_specs=[_row_spec(LANES), _const_spec((8, LANES)), _row_spec(LANES)],
        out_specs=(_row_spec(LANES), _const_spec((8, LANES))),
        out_shape=(jax.ShapeDtypeStruct((t, LANES), BF16), jax.ShapeDtypeStruct((8, LANES), F32)),
        compiler_params=_params("arbitrary"),
    )(ab, prm, dgb)


def _gdn_local(qs, ks, vs, gbs, bbs, tinvs=None):
    nh = len(qs)
    cs = qs[0].shape[0]
    hs = range(nh)
    r = lax.broadcasted_iota(jnp.int32, (cs, cs), 0)
    c = lax.broadcasted_iota(jnp.int32, (cs, cs), 1)
    tril, strict, eye = r >= c, r > c, r == c
    ident = jnp.where(eye, 1.0, 0.0)
    g_colb = [gbs[h][:, :cs] for h in hs]
    g_row = [jnp.sum(jnp.where(eye, g_colb[h], 0.0), axis=0, keepdims=True) for h in hs]
    gc_col = [jnp.sum(jnp.where(tril, g_row[h], 0.0), axis=1, keepdims=True) for h in hs]
    gc_row = [jnp.sum(jnp.where(r <= c, g_colb[h], 0.0), axis=0, keepdims=True) for h in hs]
    decay = [jnp.exp(jnp.where(tril, gc_col[h] - gc_row[h], NEG)) for h in hs]
    gamma = [jnp.exp(gc_col[h]) for h in hs]
    gcl = [gc_col[h][cs - 1:cs, :] for h in hs]
    gl = [jnp.exp(gcl[h]) for h in hs]
    kdec = [jnp.exp(gcl[h] - gc_col[h]) for h in hs]
    kb = [ks[h] * bbs[h] for h in hs]
    kk = [_dotb(kb[h], ks[h], NT) for h in hs]
    qk = [_dotb(qs[h], ks[h], NT) for h in hs]
    lmat = [jnp.where(strict, kk[h] * decay[h], 0.0) for h in hs]
    pmat = [jnp.where(tril, qk[h] * decay[h], 0.0) for h in hs]
    if tinvs is None:
        xm = [-lmat[h] for h in hs]
        tinv = [ident + xm[h] for h in hs]
        for _ in range(int(math.log2(cs)) - 1):
            xm = [_dotf(xm[h], xm[h], NN) for h in hs]
            tinv = [tinv[h] + _dotf(tinv[h], xm[h], NN) for h in hs]
    else:
        tinv = tinvs
    vb = [vs[h] * bbs[h] for h in hs]
    kg = [kb[h] * gamma[h] for h in hs]
    u = [_dotf(tinv[h], vb[h], NN) for h in hs]
    w = [_dotf(tinv[h], kg[h], NN) for h in hs]
    return [dict(tril=tril, strict=strict, eye=eye, r=r, c=c, decay=decay[h], gamma=gamma[h], gl=gl[h], kdec=kdec[h],
                 kb=kb[h], lmat=lmat[h], tinv=tinv[h], vb=vb[h], kg=kg[h], u=u[h], w=w[h], pmat=pmat[h],
                 qd=qs[h] * gamma[h], kd=ks[h] * kdec[h]) for h in hs]


def _head_columns(gbeta, cs):
    gbs = [jnp.broadcast_to(gbeta[:, h:h + 1], (cs, LANES)) for h in range(GDN_HEADS)]
    bbs = [jnp.broadcast_to(gbeta[:, GDN_HEADS + h:GDN_HEADS + h + 1], (cs, LANES)) for h in range(GDN_HEADS)]
    return gbs, bbs


def _gdn_chunk_fwd(qkv, gbeta, *, name):
    t = qkv.shape[0]
    nh, cs, hd = GDN_HEADS, GDN_CHUNK, GDN_HEAD_DIM
    nc = t // cs

    hb = GDN_HEAD_BATCH
    ng = nh // hb
    assert ng == 1

    def body(q_ref, k_ref, v_ref, gb_ref, o_ref, st_ref, ti_ref, s_ref):
        @pl.when(pl.program_id(1) == 0)
        def _():
            s_ref[...] = jnp.zeros_like(s_ref)

        sls = [slice(i * hd, (i + 1) * hd) for i in range(hb)]
        hs = range(hb)
        s = [s_ref[i] for i in hs]
        gbs, bbs = _head_columns(gb_ref[...], cs)
        lo = _gdn_local([q_ref[:, sl] for sl in sls], [k_ref[:, sl] for sl in sls], [v_ref[:, sl] for sl in sls],
                        gbs, bbs)
        ws = [_dotb(lo[i]["w"], s[i], NN) for i in hs]
        qs = [_dotb(lo[i]["qd"], s[i], NN) for i in hs]
        vn = [lo[i]["u"] - ws[i] for i in hs]
        pv = [_dotb(lo[i]["pmat"], vn[i], NN) for i in hs]
        kv = [_dotb(lo[i]["kd"], vn[i], TN) for i in hs]
        for i, sl in enumerate(sls):
            st_ref[i, 0] = s[i]
            ti_ref[i, 0] = lo[i]["tinv"]
            o_ref[:, sl] = qs[i] + pv[i]
            s_ref[i] = s[i] * lo[i]["gl"] + kv[i]

    col = lambda off: pl.BlockSpec((cs, hb * hd), lambda h, n: (n, off + h))
    return pl.pallas_call(
        body, name=name, grid=(ng, nc),
        in_specs=[col(0), col(ng), col(2 * ng), pl.BlockSpec((cs, LANES), lambda h, n: (n, 0))],
        out_specs=(col(0), pl.BlockSpec((hb, 1, hd, hd), lambda h, n: (h, n, 0, 0)),
                   pl.BlockSpec((hb, 1, cs, cs), lambda h, n: (h, n, 0, 0))),
        out_shape=(jax.ShapeDtypeStruct((t, nh * hd), F32), jax.ShapeDtypeStruct((nh, nc, hd, hd), F32),
                   jax.ShapeDtypeStruct((nh, nc, cs, cs), F32)),
        scratch_shapes=[pltpu.VMEM((hb, hd, hd), F32)],
        compiler_params=_params("parallel", "arbitrary"),
    )(qkv, qkv, qkv, gbeta)


def _gdn_chunk_bwd(qkv, gbeta, states, tinvs, do, *, name):
    t = qkv.shape[0]
    nh, cs, hd = GDN_HEADS, GDN_CHUNK, GDN_HEAD_DIM
    nc = t // cs

    hb = GDN_HEAD_BATCH
    ng = nh // hb
    assert ng == 1

    def heads_bwd(q, k, v, gb, bb, s, ti, dsn, dov):
        hs = range(len(q))
        lo = _gdn_local(q, k, v, gb, bb, ti)
        tril, strict, eye, r, c = lo[0]["tril"], lo[0]["strict"], lo[0]["eye"], lo[0]["r"], lo[0]["c"]
        rowi = lax.broadcasted_iota(jnp.int32, (cs, 1), 0)
        get = lambda name: [lo[h][name] for h in hs]
        decay, gamma, gl, kdec = get("decay"), get("gamma"), get("gl"), get("kdec")
        kb, tinv, w, pmat, kd, qd = get("kb"), get("tinv"), get("w"), get("pmat"), get("kd"), get("qd")
        ws = [_dotb(w[h], s[h], NN) for h in hs]
        pdo = [_dotb(pmat[h], dov[h], TN) for h in hs]
        kds = [_dotb(kd[h], dsn[h], NN) for h in hs]
        dqd = [_dotb(dov[h], s[h], NT) for h in hs]
        qdo = [_dotb(qd[h], dov[h], TN) for h in hs]
        vn = [lo[h]["u"] - ws[h] for h in hs]
        dvn = [pdo[h] + kds[h] for h in hs]
        dp = [jnp.where(tril, _dotb(dov[h], vn[h], NT), 0.0) for h in hs]
        dkd = [_dotb(vn[h], dsn[h], NT) for h in hs]
        dw = [-_dotb(dvn[h], s[h], NT) for h in hs]
        wdv = [_dotb(w[h], dvn[h], TN) for h in hs]
        dvb = [_dotf(tinv[h], dvn[h], TN) for h in hs]
        dt1 = [_dotf(dvn[h], lo[h]["vb"], NT) for h in hs]
        dkg = [_dotf(tinv[h], dw[h], TN) for h in hs]
        dt2 = [_dotf(dw[h], lo[h]["kg"], NT) for h in hs]
        tdt = [_dotf(tinv[h], dt1[h] + dt2[h], TN) for h in hs]
        dl = [jnp.where(strict, -_dotf(tdt[h], tinv[h], NT), 0.0) for h in hs]
        dkk = [dl[h] * decay[h] for h in hs]
        dqk = [dp[h] * decay[h] for h in hs]
        dkb = [_dotb(dkk[h], k[h], NN) + dkg[h] * gamma[h] for h in hs]
        dk1 = [_dotb(dkk[h], kb[h], TN) for h in hs]
        dk2 = [_dotb(dqk[h], q[h], TN) for h in hs]
        dq1 = [_dotb(dqk[h], k[h], NN) for h in hs]
        out = []
        for h in hs:
            dgl = jnp.sum(jnp.sum(dsn[h] * s[h], axis=1, keepdims=True), axis=0, keepdims=True)
            ds_prev = gl[h] * dsn[h] + qdo[h] - wdv[h]
            dk = dk1[h] + dk2[h] + dkd[h] * kdec[h] + dkb[h] * bb[h]
            dq = dq1[h] + dqd[h] * gamma[h]
            dbeta = jnp.sum(dvb[h] * v[h], axis=-1, keepdims=True) + jnp.sum(dkb[h] * k[h], axis=-1, keepdims=True)
            e = dl[h] * lo[h]["lmat"] + dp[h] * pmat[h]
            e_col = jnp.sum(e, axis=0, keepdims=True)
            dgc = jnp.sum(e, axis=1, keepdims=True) - jnp.sum(jnp.where(eye, e_col, 0.0), axis=1, keepdims=True)
            dgamma = (jnp.sum(dqd[h] * q[h], axis=-1, keepdims=True)
                      + jnp.sum(dkg[h] * kb[h], axis=-1, keepdims=True))
            rk = jnp.sum(dkd[h] * k[h], axis=-1, keepdims=True) * kdec[h]
            dgcl = jnp.sum(rk, axis=0, keepdims=True) + dgl * gl[h]
            dgc = dgc + dgamma * gamma[h] - rk + jnp.where(rowi == cs - 1, dgcl, 0.0)
            dgc_row = jnp.sum(jnp.where(eye, dgc, 0.0), axis=0, keepdims=True)
            dg = jnp.sum(jnp.where(c >= r, dgc_row, 0.0), axis=1, keepdims=True)
            out.append((dq, dk, dvb[h] * bb[h], dbeta, dg, ds_prev))
        return out

    def body(q_ref, k_ref, v_ref, gb_ref, st_ref, ti_ref, do_ref, d_ref, dgb_ref, ds_ref):
        @pl.when(pl.program_id(1) == 0)
        def _():
            ds_ref[...] = jnp.zeros_like(ds_ref)

        sls = [slice(i * hd, (i + 1) * hd) for i in range(hb)]
        hs = range(hb)
        gbs, bbs = _head_columns(gb_ref[...], cs)
        outs = heads_bwd([q_ref[:, sl] for sl in sls], [k_ref[:, sl] for sl in sls], [v_ref[:, sl] for sl in sls],
                         gbs, bbs, [st_ref[i, 0] for i in hs],
                         [ti_ref[i, 0] for i in hs], [ds_ref[i] for i in hs], [do_ref[:, sl] for sl in sls])
        lane = lax.broadcasted_iota(jnp.int32, (cs, LANES), 1)
        dgb = jnp.zeros((cs, LANES), F32)
        for i, sl in enumerate(sls):
            dq, dk, dv, dbeta, dg, ds_prev = outs[i]
            d_ref[0, :, sl], d_ref[1, :, sl], d_ref[2, :, sl] = dq, dk, dv
            dgb = jnp.where(lane == i, dg, jnp.where(lane == nh + i, dbeta, dgb))
            ds_ref[i] = ds_prev
        dgb_ref[...] = dgb

    col = lambda off: pl.BlockSpec((cs, hb * hd), lambda h, n: (nc - 1 - n, off + h))
    gspec = pl.BlockSpec((cs, LANES), lambda h, n: (nc - 1 - n, 0))
    return pl.pallas_call(
        body, name=name, grid=(ng, nc),
        in_specs=[col(0), col(ng), col(2 * ng), gspec,
                  pl.BlockSpec((hb, 1, hd, hd), lambda h, n: (h, nc - 1 - n, 0, 0)),
                  pl.BlockSpec((hb, 1, cs, cs), lambda h, n: (h, nc - 1 - n, 0, 0)), col(0)],
        out_specs=(pl.BlockSpec((3, cs, hb * hd), lambda h, n: (0, nc - 1 - n, h)), gspec),
        out_shape=(jax.ShapeDtypeStruct((3, t, nh * hd), F32), jax.ShapeDtypeStruct((t, LANES), F32)),
        scratch_shapes=[pltpu.VMEM((hb, hd, hd), F32)],
        compiler_params=_params("parallel", "arbitrary"),
    )(qkv, qkv, qkv, gbeta, states, tinvs, do)


def _gdn_onorm_fwd(o, proj, norm_g, *, name):
    t = o.shape[0]
    w = GDN_KEY_DIM
    goff = 3 * GDN_KEY_DIM // w

    def body(o_ref, gp_ref, g_ref, y_ref):
        gv = g_ref[...]
        for h in range(GDN_HEADS):
            sl = slice(h * GDN_HEAD_DIM, (h + 1) * GDN_HEAD_DIM)
            oh = o_ref[:, sl]
            gp = gp_ref[:, sl]
            r = lax.rsqrt(jnp.mean(oh * oh, axis=-1, keepdims=True) + EPS)
            y_ref[:, sl] = (oh * r * gv * gp * _sigmoid(gp)).astype(y_ref.dtype)

    return pl.pallas_call(
        body, name=name, grid=(t // ROWS,),
        in_specs=[_row_spec(w), pl.BlockSpec((ROWS, w), lambda i: (i, goff)), _const_spec((1, GDN_HEAD_DIM))],
        out_specs=_row_spec(w), out_shape=jax.ShapeDtypeStruct((t, w), BF16),
        compiler_params=_params("parallel"),
    )(o, proj, norm_g)


def _gdn_onorm_bwd(o, proj, norm_g, dy, *, name):
    t = o.shape[0]
    w = GDN_KEY_DIM
    goff = 3 * GDN_KEY_DIM // w

    def body(o_ref, gp_ref, g_ref, dy_ref, do_ref, dgp_ref, st_ref):
        @pl.when(pl.program_id(0) == 0)
        def _():
            st_ref[...] = jnp.zeros_like(st_ref)

        gv = g_ref[...]
        acc = jnp.zeros((1, GDN_HEAD_DIM), F32)
        for h in range(GDN_HEADS):
            sl = slice(h * GDN_HEAD_DIM, (h + 1) * GDN_HEAD_DIM)
            oh = o_ref[:, sl]
            gp = gp_ref[:, sl]
            dyv = dy_ref[:, sl].astype(F32)
            r = lax.rsqrt(jnp.mean(oh * oh, axis=-1, keepdims=True) + EPS)
            xh = oh * r
            sg = _sigmoid(gp)
            dn = dyv * gp * sg
            dgp_ref[:, sl] = (dyv * xh * gv * sg * (1.0 + gp * (1.0 - sg))).astype(dgp_ref.dtype)
            acc = acc + jnp.sum(dn * xh, axis=0, keepdims=True)
            dxh = dn * gv
            do_ref[:, sl] = r * (dxh - xh * jnp.mean(dxh * xh, axis=-1, keepdims=True))
        st_ref[0:1, :] += acc

    return pl.pallas_call(
        body, name=name, grid=(t // ROWS,),
        in_specs=[_row_spec(w), pl.BlockSpec((ROWS, w), lambda i: (i, goff)), _const_spec((1, GDN_HEAD_DIM)),
                  _row_spec(w)],
        out_specs=(_row_spec(w), _row_spec(w), _const_spec((8, GDN_HEAD_DIM))),
        out_shape=(jax.ShapeDtypeStruct((t, w), F32), jax.ShapeDtypeStruct((t, w), BF16),
                   jax.ShapeDtypeStruct((8, GDN_HEAD_DIM), F32)),
        compiler_params=_params("arbitrary"),
    )(o, proj, norm_g, dy)


def _mla_prep_fwd(proj, qg, kvg, *, name):
    t = proj.shape[0]
    q1, k1 = MLA_Q_RANK, MLA_Q_RANK + MLA_KV_RANK

    def body(p_ref, qg_ref, kg_ref, cq_ref, ck_ref):
        cq = p_ref[:, 0:q1]
        ck = p_ref[:, q1:k1]
        cq_ref[...] = (cq * lax.rsqrt(jnp.mean(cq * cq, axis=-1, keepdims=True) + EPS) * qg_ref[...]).astype(BF16)
        ck_ref[...] = (ck * lax.rsqrt(jnp.mean(ck * ck, axis=-1, keepdims=True) + EPS) * kg_ref[...]).astype(BF16)

    return pl.pallas_call(
        body, name=name, grid=(t // ROWS,),
        in_specs=[_row_spec(MLA_IN), _const_spec((1, MLA_Q_RANK)), _const_spec((1, MLA_KV_RANK))],
        out_specs=(_row_spec(MLA_Q_RANK), _row_spec(MLA_KV_RANK)),
        out_shape=(jax.ShapeDtypeStruct((t, MLA_Q_RANK), BF16), jax.ShapeDtypeStruct((t, MLA_KV_RANK), BF16)),
        compiler_params=_params("parallel"),
    )(proj, qg, kvg)


def _mla_prep_bwd(proj, qg, kvg, dcq, dck, dkr, *, name):
    t = proj.shape[0]
    q1, k1 = MLA_Q_RANK, MLA_Q_RANK + MLA_KV_RANK

    def body(p_ref, qg_ref, kg_ref, dq_ref, dk_ref, dr_ref, dp_ref, st_ref):
        @pl.when(pl.program_id(0) == 0)
        def _():
            st_ref[...] = jnp.zeros_like(st_ref)

        for lo, hi, g_ref, d_ref in ((0, q1, qg_ref, dq_ref), (q1, k1, kg_ref, dk_ref)):
            xv = p_ref[:, lo:hi]
            dn = d_ref[...]
            r = lax.rsqrt(jnp.mean(xv * xv, axis=-1, keepdims=True) + EPS)
            xh = xv * r
            dxh = dn * g_ref[...]
            dp_ref[:, lo:hi] = (r * (dxh - xh * jnp.mean(dxh * xh, axis=-1, keepdims=True))).astype(dp_ref.dtype)
            st_ref[0:1, lo:hi] += jnp.sum(dn * xh, axis=0, keepdims=True)
        dp_ref[:, k1:MLA_IN] = dr_ref[:, 0:MLA_ROPE].astype(dp_ref.dtype)

    return pl.pallas_call(
        body, name=name, grid=(t // ROWS,),
        in_specs=[_row_spec(MLA_IN), _const_spec((1, MLA_Q_RANK)), _const_spec((1, MLA_KV_RANK)),
                  _row_spec(MLA_Q_RANK), _row_spec(MLA_KV_RANK), _row_spec(LANES)],
        out_specs=(_row_spec(MLA_IN), _const_spec((8, MLA_IN))),
        out_shape=(jax.ShapeDtypeStruct((t, MLA_IN), BF16), jax.ShapeDtypeStruct((8, MLA_IN), F32)),
        compiler_params=_params("arbitrary"),
    )(proj, qg, kvg, dcq, dck, dkr)


ATT_BLOCK = 256
ATT_HEAD_BATCH = 4
ATT_HEAD_BATCH_BWD = 4
ATT_SCALE = MLA_QK ** -0.5


def _diagonal_mask(blk):
    return lax.broadcasted_iota(jnp.int32, (blk, blk), 1) <= lax.broadcasted_iota(jnp.int32, (blk, blk), 0)


def _swap_halves(xv, first):
    return jnp.where(first, pltpu.roll(xv, LANES - MLA_ROPE // 2, 1), pltpu.roll(xv, MLA_ROPE // 2, 1))


def _rope_qk(qf, proj, cos_t, sin_t, *, name):
    t = qf.shape[0]
    nrope = MLA_HEADS * MLA_ROPE
    q_blk = MLA_HEADS * MLA_NOPE // nrope
    k_blk = (MLA_Q_RANK + MLA_KV_RANK) // LANES

    def body(q_ref, p_ref, c_ref, s_ref, qo_ref, ko_ref):
        cv, sv = c_ref[...], s_ref[...]
        lane = lax.broadcasted_iota(jnp.int32, (ROWS, LANES), 1)
        first = (lane % MLA_ROPE) < (MLA_ROPE // 2)
        for i in range(nrope // LANES):
            sl = slice(i * LANES, (i + 1) * LANES)
            xv = q_ref[:, sl].astype(F32)
            qo_ref[:, sl] = (xv * cv + _swap_halves(xv, first) * sv).astype(qo_ref.dtype)
        kv = jnp.where(lane < MLA_ROPE, p_ref[...], 0.0)
        ko_ref[...] = (kv * cv + _swap_halves(kv, first) * sv).astype(ko_ref.dtype)

    return pl.pallas_call(
        body, name=name, grid=(t // ROWS,),
        in_specs=[pl.BlockSpec((ROWS, nrope), lambda i: (i, q_blk)), pl.BlockSpec((ROWS, LANES), lambda i: (i, k_blk)),
                  _row_spec(LANES), _row_spec(LANES)],
        out_specs=(_row_spec(nrope), _row_spec(LANES)),
        out_shape=(jax.ShapeDtypeStruct((t, nrope), BF16), jax.ShapeDtypeStruct((t, LANES), BF16)),
        compiler_params=_params("parallel"),
    )(qf, proj, cos_t, sin_t)


def _rope_qk_bwd(dqr, dkr_parts, cos_t, sin_t, *, name):
    t, nrope = dqr.shape
    ng = dkr_parts.shape[0]

    def body(d_ref, k_ref, c_ref, s_ref, qo_ref, ko_ref):
        cv, sv = c_ref[...], s_ref[...]
        lane = lax.broadcasted_iota(jnp.int32, (ROWS, LANES), 1)
        first = (lane % MLA_ROPE) < (MLA_ROPE // 2)
        for i in range(nrope // LANES):
            sl = slice(i * LANES, (i + 1) * LANES)
            dv = d_ref[:, sl]
            qo_ref[:, sl] = (dv * cv + _swap_halves(dv * sv, first)).astype(qo_ref.dtype)
        dk = k_ref[0]
        for g in range(1, ng):
            dk = dk + k_ref[g]
        dk = jnp.where(lane < MLA_ROPE, dk, 0.0)
        ko_ref[...] = jnp.where(lane < MLA_ROPE, dk * cv + _swap_halves(dk * sv, first), 0.0)

    return pl.pallas_call(
        body, name=name, grid=(t // ROWS,),
        in_specs=[_row_spec(nrope), pl.BlockSpec((ng, ROWS, LANES), lambda i: (0, i, 0)), _row_spec(LANES),
                  _row_spec(LANES)],
        out_specs=(_row_spec(nrope), _row_spec(LANES)),
        out_shape=(jax.ShapeDtypeStruct((t, nrope), BF16), jax.ShapeDtypeStruct((t, LANES), F32)),
        compiler_params=_params("parallel"),
    )(dqr, dkr_parts, cos_t, sin_t)


def _attn_tm_fwd(qf, qr, kvf, kr, *, name):
    t = qf.shape[0]
    nh, dn, dr, dv = MLA_HEADS, MLA_NOPE, MLA_ROPE, MLA_V
    blk = min(ATT_BLOCK, t)
    hb = ATT_HEAD_BATCH
    hs = range(hb)

    def body(q_ref, qr_ref, kv_ref, kr_ref, o_ref, l_ref):
        i = pl.program_id(1)
        qc = [jnp.concatenate([q_ref[:, h * dn:(h + 1) * dn].astype(MXU_DTYPE), qr_ref[:, h * dr:(h + 1) * dr]], axis=1)
              for h in hs]

        def step(j, carry, diagonal=False):
            m, l, acc = carry[:hb], carry[hb:2 * hb], carry[2 * hb:]
            rows = pl.ds(pl.multiple_of(j * blk, blk), blk)
            krj = kr_ref[rows, 0:dr]
            s = [_dotb(qc[h], jnp.concatenate([kv_ref[rows, h * (dn + dv):h * (dn + dv) + dn], krj], axis=1), NT)
                 for h in hs]
            s = [s[h] * ATT_SCALE for h in hs]
            if diagonal:
                mask = _diagonal_mask(blk)
                s = [jnp.where(mask, s[h], NEG) for h in hs]
            m_new = [jnp.maximum(m[h], jnp.max(s[h], axis=-1, keepdims=True)) for h in hs]
            p = [jnp.exp(s[h] - m_new[h]) for h in hs]
            pv = [_dotb(p[h], kv_ref[rows, h * (dn + dv) + dn:(h + 1) * (dn + dv)], NN) for h in hs]
            alpha = [jnp.exp(m[h] - m_new[h]) for h in hs]
            l = [alpha[h] * l[h] + jnp.sum(p[h], axis=-1, keepdims=True) for h in hs]
            acc = [alpha[h] * acc[h] + pv[h] for h in hs]
            return tuple(m_new) + tuple(l) + tuple(acc)

        init = ((jnp.full((blk, 1), NEG, F32),) * hb + (jnp.zeros((blk, 1), F32),) * hb
                + (jnp.zeros((blk, dv), F32),) * hb)
        out = step(i, lax.fori_loop(0, i, step, init), diagonal=True)
        for h in hs:
            m, l, acc = out[h], out[hb + h], out[2 * hb + h]
            o_ref[:, h * dv:(h + 1) * dv] = (acc / l).astype(o_ref.dtype)
            l_ref[h] = jnp.broadcast_to(m + jnp.log(l), (blk, LANES))

    return pl.pallas_call(
        body, name=name, grid=(nh // hb, t // blk),
        in_specs=[pl.BlockSpec((blk, hb * dn), lambda g, i: (i, g)), pl.BlockSpec((blk, hb * dr), lambda g, i: (i, g)),
                  pl.BlockSpec((t, hb * (dn + dv)), lambda g, i: (0, g)), pl.BlockSpec((t, LANES), lambda g, i: (0, 0))],
        out_specs=(pl.BlockSpec((blk, hb * dv), lambda g, i: (i, g)),
                   pl.BlockSpec((hb, blk, LANES), lambda g, i: (g, i, 0))),
        out_shape=(jax.ShapeDtypeStruct((t, nh * dv), BF16), jax.ShapeDtypeStruct((nh, t, LANES), F32)),
        compiler_params=_params("parallel", "parallel"),
    )(qf, qr, kvf, kr)


def _attn_tm_bwd(qf, qr, kvf, kr, o, lse, do, *, name):
    t = qf.shape[0]
    nh, dn, dr, dv = MLA_HEADS, MLA_NOPE, MLA_ROPE, MLA_V
    blk = min(ATT_BLOCK, t)
    nb = t // blk
    hb = ATT_HEAD_BATCH_BWD
    hs = range(hb)
    ng = nh // hb

    def body(q_ref, qr_ref, kv_ref, kr_ref, o_ref, l_ref, do_ref, dqn_ref, dqr_ref, dkv_ref, dkr_ref):
        j = pl.program_id(1)

        @pl.when(j == 0)
        def _():
            dqn_ref[...] = jnp.zeros_like(dqn_ref)
            dqr_ref[...] = jnp.zeros_like(dqr_ref)

        krj = kr_ref[:, 0:dr]
        kc = [jnp.concatenate([kv_ref[:, h * (dn + dv):h * (dn + dv) + dn], krj], axis=1) for h in hs]
        vv = [kv_ref[:, h * (dn + dv) + dn:(h + 1) * (dn + dv)] for h in hs]

        def step(i, carry, diagonal=False):
            dkn_acc, dv_acc, dkr_acc = carry[:hb], carry[hb:2 * hb], carry[2 * hb]
            rows = pl.ds(pl.multiple_of(i * blk, blk), blk)
            qc = [jnp.concatenate([q_ref[rows, h * dn:(h + 1) * dn].astype(MXU_DTYPE),
                                   qr_ref[rows, h * dr:(h + 1) * dr]], axis=1) for h in hs]
            dov = [do_ref[rows, h * dv:(h + 1) * dv] for h in hs]
            s = [_dotb(qc[h], kc[h], NT) for h in hs]
            dp = [_dotb(dov[h], vv[h], NT) for h in hs]
            s = [s[h] * ATT_SCALE for h in hs]
            if diagonal:
                mask = _diagonal_mask(blk)
                s = [jnp.where(mask, s[h], NEG) for h in hs]
            p = [jnp.exp(s[h] - l_ref[h, rows, :][:, 0:1]) for h in hs]
            delta = [jnp.sum(dov[h].astype(F32) * o_ref[rows, h * dv:(h + 1) * dv].astype(F32), axis=-1, keepdims=True)
                     for h in hs]
            ds = [p[h] * (dp[h] - delta[h]) * ATT_SCALE for h in hs]
            dvn = [_dotb(p[h], dov[h], TN) for h in hs]
            dkc = [_dotb(ds[h], qc[h], TN) for h in hs]
            dqc = [_dotb(ds[h], kc[h], NN) for h in hs]
            for h in hs:
                dqn_ref[rows, h * dn:(h + 1) * dn] += dqc[h][:, 0:dn]
                dqr_ref[rows, h * dr:(h + 1) * dr] += dqc[h][:, dn:dn + dr]
            dkr_new = dkr_acc
            for h in hs:
                dkr_new = dkr_new + dkc[h][:, dn:dn + dr]
            return (tuple(dkn_acc[h] + dkc[h][:, 0:dn] for h in hs) + tuple(dv_acc[h] + dvn[h] for h in hs)
                    + (dkr_new,))

        init = (jnp.zeros((blk, dn), F32),) * hb + (jnp.zeros((blk, dv), F32),) * hb + (jnp.zeros((blk, dr), F32),)
        out = lax.fori_loop(j + 1, nb, step, step(j, init, diagonal=True))
        for h in hs:
            dkv_ref[:, h * (dn + dv):h * (dn + dv) + dn] = out[h].astype(dkv_ref.dtype)
            dkv_ref[:, h * (dn + dv) + dn:(h + 1) * (dn + dv)] = out[hb + h].astype(dkv_ref.dtype)
        dkr_ref[0, :, 0:dr] = out[2 * hb]
        dkr_ref[0, :, dr:LANES] = jnp.zeros((blk, LANES - dr), F32)

    full = lambda w: pl.BlockSpec((t, w), lambda g, j: (0, g))
    return pl.pallas_call(
        body, name=name, grid=(ng, nb),
        in_specs=[full(hb * dn), full(hb * dr), pl.BlockSpec((blk, hb * (dn + dv)), lambda g, j: (j, g)),
                  pl.BlockSpec((blk, LANES), lambda g, j: (j, 0)), full(hb * dv),
                  pl.BlockSpec((hb, t, LANES), lambda g, j: (g, 0, 0)), full(hb * dv)],
        out_specs=(full(hb * dn), full(hb * dr), pl.BlockSpec((blk, hb * (dn + dv)), lambda g, j: (j, g)),
                   pl.BlockSpec((1, blk, LANES), lambda g, j: (g, j, 0))),
        out_shape=(jax.ShapeDtypeStruct((t, nh * dn), F32), jax.ShapeDtypeStruct((t, nh * dr), F32),
                   jax.ShapeDtypeStruct((t, nh * (dn + dv)), BF16), jax.ShapeDtypeStruct((ng, t, LANES), F32)),
        compiler_params=_params("parallel", "arbitrary"),
    )(qf, qr, kvf, kr, o, lse, do)


def _ada_mod(c_all, ada_w, ada_b_cols, *, name):
    nl, d, wc = ada_w.shape

    def body(c_ref, w_ref, b_ref, o_ref):
        cv = c_ref[...]
        o_ref[0] = _dotb(cv * _sigmoid(cv), w_ref[0], NN) + b_ref[0]

    return pl.pallas_call(
        body, name=name, grid=(nl,),
        in_specs=[_const_spec((N_DEV, d)), pl.BlockSpec((1, d, wc), lambda l: (l, 0, 0)),
                  pl.BlockSpec((1, 1, wc), lambda l: (l, 0, 0))],
        out_specs=pl.BlockSpec((1, N_DEV, wc), lambda l: (l, 0, 0)),
        out_shape=jax.ShapeDtypeStruct((nl, N_DEV, wc), F32), compiler_params=_params("parallel"),
    )(c_all, ada_w, ada_b_cols)


def _adam_math(g, w, m, v):
    m2 = ADAM_B1 * m + (1.0 - ADAM_B1) * g
    v2 = ADAM_B2 * v + (1.0 - ADAM_B2) * (g * g)
    delta = -ADAM_LR * ((m2 / ADAM_BC1) / (jnp.sqrt(v2 / ADAM_BC2) + ADAM_EPS) + ADAM_WD * w)
    return delta, m2, v2


def _ada_grad_adamw(c_all, dmod_cols, w, m, v, *, name):
    nl, d, wc = w.shape
    tr = 256

    def body(c_ref, dm_ref, w_ref, m_ref, v_ref, g_ref, d_ref, m2_ref, v2_ref):
        cv = c_ref[...]
        g = _dotf(cv * _sigmoid(cv), dm_ref[0], TN)
        delta, m2, v2 = _adam_math(g, w_ref[0], m_ref[0], v_ref[0])
        g_ref[0], d_ref[0], m2_ref[0], v2_ref[0] = g, delta, m2, v2

    blk = pl.BlockSpec((1, tr, wc), lambda l, i: (l, i, 0))
    return pl.pallas_call(
        body, name=name, grid=(nl, d // tr),
        in_specs=[pl.BlockSpec((N_DEV, tr), lambda l, i: (0, i)), pl.BlockSpec((1, N_DEV, wc), lambda l, i: (l, 0, 0)),
                  blk, blk, blk],
        out_specs=(blk,) * 4, out_shape=(jax.ShapeDtypeStruct(w.shape, F32),) * 4,
        compiler_params=_params("parallel", "parallel"),
    )(c_all, dmod_cols, w, m, v)


def _adamw(parts, w, m, v, *, name):
    nl, r, c = w.shape
    ns = parts[0].shape[0]
    lanes_padded = -(-c // LANES) * LANES
    row_bytes = 2 * nl * ns * lanes_padded * parts[0].dtype.itemsize
    tr = _pick(r, min(256, max(16, (VMEM_LIMIT // 2) // row_bytes)), 16)
    tc = c
    if tr * row_bytes > VMEM_LIMIT // 2:
        tc = _pick(c, max(LANES, c * (VMEM_LIMIT // 2) // (tr * row_bytes)))

    def body(*refs):
        p_refs = refs[:nl]
        w_ref, m_ref, v_ref, g_ref, d_ref, m2_ref, v2_ref = refs[nl:]
        layer = pl.program_id(0)
        for q in range(nl):
            @pl.when(layer == q)
            def _(q=q):
                g = p_refs[q][0].astype(F32)
                for s in range(1, ns):
                    g = g + p_refs[q][s].astype(F32)
                delta, m2, v2 = _adam_math(g, w_ref[0], m_ref[0], v_ref[0])
                g_ref[0], d_ref[0], m2_ref[0], v2_ref[0] = g, delta, m2, v2

    blk = pl.BlockSpec((1, tr, tc), lambda l, i, j: (l, i, j))
    p_specs = [pl.BlockSpec((ns, tr, tc), lambda l, i, j, q=q: (0, jnp.where(l == q, i, 0), jnp.where(l == q, j, 0)))
               for q in range(nl)]
    return pl.pallas_call(
        body, name=name, grid=(nl, r // tr, c // tc),
        in_specs=p_specs + [blk, blk, blk],
        out_specs=(blk,) * 4, out_shape=(jax.ShapeDtypeStruct(w.shape, F32),) * 4,
        compiler_params=_params("arbitrary", "arbitrary", "arbitrary"),
    )(*parts, w, m, v)


def _sum_parts(parts, *, name):
    ns, r, c = parts.shape

    def body(p_ref, o_ref):
        acc = p_ref[0]
        for s in range(1, ns):
            acc = acc + p_ref[s]
        o_ref[...] = acc

    return pl.pallas_call(
        body, name=name, out_shape=jax.ShapeDtypeStruct((r, c), F32),
        in_specs=[pl.BlockSpec(memory_space=pltpu.VMEM)], out_specs=pl.BlockSpec(memory_space=pltpu.VMEM),
    )(parts)


def _pack(arrs):
    flat = jnp.concatenate([a.reshape(-1).astype(F32) for a in arrs])
    pad = (-flat.shape[0]) % (8 * LANES)
    return jnp.pad(flat, (0, pad)).reshape(-1, LANES)


def _unpack(packed, shapes, lead=()):
    flat = packed.reshape(lead + (-1,))
    out, off = [], 0
    for s in shapes:
        n = math.prod(s)
        out.append(flat[..., off:off + n].reshape(lead + tuple(s)))
        off += n
    return out


def _gather_rows(g):
    _, nl, rs, c = g.shape
    return jnp.transpose(g, (1, 0, 2, 3)).reshape(nl, N_DEV * rs, c)


def _row(v):
    return v.reshape(1, -1)


def _local_step(x, target, mod, cos_t, sin_t, rep, get_weights, put_grads):
    t = x.shape[0]
    saved = []
    for layer in range(DEPTH):
        j = layer // 2
        tag = f"l{layer}"
        shift_m, scale_m, gate_m, shift_f, scale_f, gate_f = [_row(mod[layer, i]) for i in range(N_MOD)]
        lw = dict(get_weights(layer, "mix", x))
        rec = {"x0": x, "lw": lw}
        h = _adaln_fwd(x, _row(rep["norm_mix_g"][layer]), scale_m, shift_m, name=f"adaln_mix_{tag}")
        rec["h"] = h
        if layer % 2 == 0:
            proj = _mm(h, lw["wt_in"], mode="nt", out_dtype=F32, tm=256, tn=GDN_MAIN, b_rows=GDN_MAIN,
                       dep=lw["dep_mix"], name=f"gdn_in_{tag}")
            ab = _mm(h, lw["wt_ab"], mode="nt", out_dtype=F32, name=f"gdn_in_ab_{tag}")
            qkv = _gdn_prep_fwd(proj, rep["gdn_conv_wt"][j], name=f"gdn_prep_{tag}")
            gbeta = _gdn_gate_fwd(ab, rep["gdn_gate_prm"][j], name=f"gdn_gate_{tag}")
            o, states, tinvs = _gdn_chunk_fwd(qkv, gbeta, name=f"gdn_chunk_{tag}")
            og = _gdn_onorm_fwd(o, proj, _row(rep["gdn_norm_g"][j]), name=f"gdn_onorm_{tag}")
            x, y = _mm_resid(og, lw["w_out"], x, gate_m, name=f"gdn_out_{tag}")
            rec.update(proj=proj, ab=ab, qkv=qkv, gbeta=gbeta, states=states, tinvs=tinvs, o=o, og=og, y=y)
        else:
            proj = _mm(h, lw["w_in"], mode="nn", out_dtype=F32, dep=lw["dep_mix"], name=f"mla_in_{tag}")
            cq, ck = _mla_prep_fwd(proj, _row(rep["mla_q_norm_g"][j]), _row(rep["mla_kv_norm_g"][j]),
                                   name=f"mla_prep_{tag}")
            qf = _mm(cq, lw["wt_uq"], mode="nt", out_dtype=BF16, name=f"mla_uq_{tag}")
            kvf = _mm(ck, lw["w_ukv"], mode="nn", out_dtype=BF16, name=f"mla_ukv_{tag}")
            qr, kr = _rope_qk(qf, proj, cos_t, sin_t, name=f"rope_{tag}")
            oc, lse = _attn_tm_fwd(qf, qr, kvf, kr, name=f"attn_{tag}")
            x, y = _mm_resid(oc, lw["w_out"], x, gate_m, name=f"mla_out_{tag}")
            rec.update(proj=proj, cq=cq, ck=ck, qf=qf, qr=qr, kvf=kvf, kr=kr, lse=lse, oc=oc, y=y)
        rec["x1"] = x
        lw.update(get_weights(layer, "ffn", x))
        h2 = _adaln_fwd(x, _row(rep["norm_ffn_g"][layer]), scale_f, shift_f, name=f"adaln_ffn_{tag}")
        s, a2, b2 = _ffn_gu_fwd(h2, lw["wt_g"], lw["wt_u"], lw["dep_ffn"], name=f"ffn_gu_{tag}")
        x, y2 = _mm_resid(s, lw["w_down"], x, gate_f, tm=512, name=f"ffn_down_{tag}")
        rec.update(h2=h2, a2=a2, b2=b2, s=s, y2=y2)
        saved.append(rec)

    dx, st, ls = _loss_head(x, _row(rep["final_norm_g"]), target, name="loss_head")
    loss = ls[0, 0]
    grads = {"final_norm_g": st[0]}
    per_layer = {k: [None] * DEPTH for k in ("norm_mix_g", "norm_ffn_g")}
    per_gdn = {k: [None] * 2 for k in ("gdn_conv_wt", "gdn_a_log", "gdn_dt_bias", "gdn_norm_g")}
    per_mla = {k: [None] * 2 for k in ("mla_q_norm_g", "mla_kv_norm_g")}
    dmod = [None] * DEPTH
    dep = jnp.zeros((8, LANES), F32)

    for layer in reversed(range(DEPTH)):
        j = layer // 2
        tag = f"l{layer}"
        rec = saved[layer]
        lw = rec["lw"]
        shift_m, scale_m, gate_m, shift_f, scale_f, gate_f = [_row(mod[layer, i]) for i in range(N_MOD)]
        if layer == DEPTH - 1:
            dy2, st_g = _gate_bwd(dx, rec["y2"], gate_f, dep, name=f"gate_bwd_ffn_{tag}")
            dgate_f = st_g[0]
        dw_down = _mm(rec["s"], dy2, mode="tn", out_dtype=BF16, tm=FFN_BLOCK, tn=1024, name=f"ffn_down_dw_{tag}")
        da2, db2 = _ffn_down_dx(dy2, lw["w_down"], rec["a2"], rec["b2"], name=f"ffn_down_dx_{tag}")
        dwt_g = _mm(da2, rec["h2"], mode="tn", out_dtype=BF16, tm=FFN_BLOCK, tn=1024, name=f"ffn_g_dw_{tag}")
        dwt_u = _mm(db2, rec["h2"], mode="tn", out_dtype=BF16, tm=FFN_BLOCK, tn=1024, name=f"ffn_u_dw_{tag}")
        dep = put_grads(layer, "ffn", {"wt_g": dwt_g, "wt_u": dwt_u, "w_down": dw_down})
        dh2 = _mm(da2, lw["wt_g"], mode="nn", out_dtype=F32, tm=512, tn=1024, name=f"ffn_g_dx_{tag}")
        dh2 = _mm(db2, lw["wt_u"], mode="nn", out_dtype=BF16, add=dh2, tm=512, tn=1024, name=f"ffn_u_dx_{tag}")
        dx, st_n, dy = _adaln_gate_bwd(rec["x1"], _row(rep["norm_ffn_g"][layer]), scale_f, shift_f, dh2, dx, dep,
                                       rec["y"], gate_m, name=f"adaln_ffn_bwd_{tag}")
        per_layer["norm_ffn_g"][layer] = st_n[0]
        dscale_f, dshift_f, dgate_m = st_n[1], st_n[2], st_n[3]
        big = {}
        if layer % 2 == 0:
            big["w_out"] = _mm(rec["og"], dy, mode="tn", out_dtype=BF16, name=f"gdn_out_dw_{tag}")
            dog = _mm(dy, lw["w_out"], mode="nt", out_dtype=BF16, name=f"gdn_out_dx_{tag}")
            do, dgp, st_o = _gdn_onorm_bwd(rec["o"], rec["proj"], _row(rep["gdn_norm_g"][j]), dog,
                                           name=f"gdn_onorm_bwd_{tag}")
            per_gdn["gdn_norm_g"][j] = st_o[0]
            dqkv, dgb = _gdn_chunk_bwd(rec["qkv"], rec["gbeta"], rec["states"], rec["tinvs"], do,
                                       name=f"gdn_chunk_bwd_{tag}")
            dab, st_a = _gdn_gate_bwd(rec["ab"], rep["gdn_gate_prm"][j], dgb, name=f"gdn_gate_bwd_{tag}")
            per_gdn["gdn_a_log"][j] = st_a[0, :GDN_HEADS]
            per_gdn["gdn_dt_bias"][j] = st_a[1, :GDN_HEADS]
            dpre, dcw = _gdn_prep_bwd(rec["proj"], rep["gdn_conv_wt"][j], dqkv, name=f"gdn_prep_bwd_{tag}")
            per_gdn["gdn_conv_wt"][j] = dcw
            dproj = jnp.concatenate([dpre, dgp], axis=1)
            dw_main = _mm(dproj, rec["h"], mode="tn", out_dtype=BF16, tm=512, tn=1024, name=f"gdn_in_dw_{tag}")
            dw_ab = _mm(dab, rec["h"], mode="tn", out_dtype=BF16, tn=1024, name=f"gdn_in_ab_dw_{tag}")
            big["wt_in"] = jnp.concatenate([dw_main, dw_ab[:2 * GDN_HEADS]], axis=0)
            dep = put_grads(layer, "gdn", big)
            dh_ab = _mm(dab, lw["wt_ab"], mode="nn", out_dtype=F32, tn=1024, name=f"gdn_in_ab_dx_{tag}")
            dh = _mm(dproj, lw["wt_in"], mode="nn", out_dtype=BF16, add=dh_ab, tm=256, tn=1024, b_rows=GDN_MAIN,
                     name=f"gdn_in_dx_{tag}")
        else:
            big["w_out"] = _mm(rec["oc"], dy, mode="tn", out_dtype=BF16, name=f"mla_out_dw_{tag}")
            doc = _mm(dy, lw["w_out"], mode="nt", out_dtype=BF16, name=f"mla_out_dx_{tag}")
            dqn, dqr, dkvf, dkr_parts = _attn_tm_bwd(rec["qf"], rec["qr"], rec["kvf"], rec["kr"], rec["oc"],
                                                     rec["lse"], doc, name=f"attn_bwd_{tag}")
            dqr_un, dkr_un = _rope_qk_bwd(dqr, dkr_parts, cos_t, sin_t, name=f"rope_bwd_{tag}")
            n_nope = MLA_HEADS * MLA_NOPE
            big["wt_uq"] = jnp.concatenate(
                [_mm(dqn, rec["cq"], mode="tn", out_dtype=BF16, name=f"mla_uq_dw_nope_{tag}"),
                 _mm(dqr_un, rec["cq"], mode="tn", out_dtype=BF16, name=f"mla_uq_dw_rope_{tag}")], axis=0)
            big["w_ukv"] = _mm(rec["ck"], dkvf, mode="tn", out_dtype=BF16, name=f"mla_ukv_dw_{tag}")
            dcq = _mm(dqr_un, lw["wt_uq"][n_nope:], mode="nn", out_dtype=F32, name=f"mla_uq_dx_rope_{tag}")
            dcq = _mm(dqn, lw["wt_uq"], mode="nn", out_dtype=F32, add=dcq, b_rows=n_nope,
                      name=f"mla_uq_dx_nope_{tag}")
            dck = _mm(dkvf, lw["w_ukv"], mode="nt", out_dtype=F32, name=f"mla_ukv_dx_{tag}")
            dproj, st_p = _mla_prep_bwd(rec["proj"], _row(rep["mla_q_norm_g"][j]), _row(rep["mla_kv_norm_g"][j]),
                                        dcq, dck, dkr_un, name=f"mla_prep_bwd_{tag}")
            per_mla["mla_q_norm_g"][j] = st_p[0, :MLA_Q_RANK]
            per_mla["mla_kv_norm_g"][j] = st_p[0, MLA_Q_RANK:MLA_Q_RANK + MLA_KV_RANK]
            big["w_in"] = _mm(rec["h"], dproj, mode="tn", out_dtype=BF16, name=f"mla_in_dw_{tag}")
            dep = put_grads(layer, "mla", big)
            dh = _mm(dproj, lw["w_in"], mode="nt", out_dtype=BF16, name=f"mla_in_dx_{tag}")
        if layer > 0:
            below = saved[layer - 1]
            dx, st_n, dy2 = _adaln_gate_bwd(rec["x0"], _row(rep["norm_mix_g"][layer]), scale_m, shift_m, dh, dx, dep,
                                            below["y2"], _row(mod[layer - 1, N_MOD - 1]),
                                            name=f"adaln_mix_bwd_{tag}")
        else:
            dx, st_n = _adaln_bwd(rec["x0"], _row(rep["norm_mix_g"][layer]), scale_m, shift_m, dh, dx, dep,
                                  name=f"adaln_mix_bwd_{tag}")
        per_layer["norm_mix_g"][layer] = st_n[0]
        dmod[layer] = jnp.stack([st_n[2], st_n[1], dgate_m, dshift_f, dscale_f, dgate_f])
        if layer > 0:
            dgate_f = st_n[3]

    for d in (per_layer, per_gdn, per_mla):
        for k, v in d.items():
            grads[k] = jnp.stack(v)
    return loss, dx, jnp.stack(dmod), grads


BIG = ("gdn_w_in", "gdn_w_out", "mla_w_in", "mla_w_uq", "mla_w_ukv", "mla_w_out", "ffn_w_gate", "ffn_w_up",
       "ffn_w_down")
TRANSPOSED = ("gdn_w_in", "mla_w_uq", "ffn_w_gate", "ffn_w_up")
AHEAD = 3


def _view(k, a):
    return jnp.transpose(a, (0, 2, 1)) if k in TRANSPOSED else a
SMALL = ("ada_b", "norm_mix_g", "norm_ffn_g", "gdn_conv_w", "gdn_a_log", "gdn_dt_bias", "gdn_norm_g",
         "mla_q_norm_g", "mla_kv_norm_g", "final_norm_g")
WEIGHTS = ("ada_w", "ada_b", "norm_mix_g", "norm_ffn_g", "gdn_w_in", "gdn_conv_w", "gdn_a_log", "gdn_dt_bias",
           "gdn_norm_g", "gdn_w_out", "mla_w_in", "mla_q_norm_g", "mla_kv_norm_g", "mla_w_uq", "mla_w_ukv",
           "mla_w_out", "ffn_w_gate", "ffn_w_up", "ffn_w_down", "final_norm_g")


def _uq_to_kernel_layout(w, axis=-1):
    axis = axis % w.ndim
    lead, tail = w.shape[:axis], w.shape[axis + 1:]
    w4 = w.reshape(lead + (MLA_HEADS, MLA_QK) + tail)
    nope = lax.slice_in_dim(w4, 0, MLA_NOPE, axis=axis + 1).reshape(lead + (-1,) + tail)
    rope = lax.slice_in_dim(w4, MLA_NOPE, MLA_QK, axis=axis + 1).reshape(lead + (-1,) + tail)
    return jnp.concatenate([nope, rope], axis=axis)


def _uq_from_kernel_layout(w, axis=-1):
    axis = axis % w.ndim
    lead, tail = w.shape[:axis], w.shape[axis + 1:]
    nope = lax.slice_in_dim(w, 0, MLA_HEADS * MLA_NOPE, axis=axis).reshape(lead + (MLA_HEADS, MLA_NOPE) + tail)
    rope = lax.slice_in_dim(w, MLA_HEADS * MLA_NOPE, MLA_HEADS * MLA_QK, axis=axis).reshape(
        lead + (MLA_HEADS, MLA_ROPE) + tail)
    return jnp.concatenate([nope, rope], axis=axis + 1).reshape(lead + (-1,) + tail)


def _group_names(layer, kind):
    if kind == "ffn":
        return ("ffn_w_gate", "ffn_w_up", "ffn_w_down")
    return ("gdn_w_in", "gdn_w_out") if layer % 2 == 0 else ("mla_w_in", "mla_w_uq", "mla_w_ukv", "mla_w_out")


def _layer_index(name, layer):
    return layer if name.startswith("ffn") else layer // 2


def _cols(g):
    return jnp.transpose(g, (1, 0, 2)).reshape(g.shape[1], N_DEV * g.shape[2])


def _rows(g):
    return g.reshape(N_DEV * g.shape[1], g.shape[2])


def _uncols(full):
    r, c = full.shape
    return jnp.transpose(full.reshape(r, N_DEV, c // N_DEV), (1, 0, 2))


def _unrows(full):
    r, c = full.shape
    return full.reshape(N_DEV, r // N_DEV, c)


def _group_weights(layer, kind, got, token):
    if kind == "ffn":
        return {"wt_g": _rows(got["ffn_w_gate"]), "wt_u": _rows(got["ffn_w_up"]), "w_down": _rows(got["ffn_w_down"]),
                "dep_ffn": token}
    if layer % 2 == 0:
        wt_in = _rows(got["gdn_w_in"])
        return dict(wt_in=wt_in, wt_ab=jnp.pad(wt_in[GDN_MAIN:], ((0, LANES - 2 * GDN_HEADS), (0, 0))),
                    w_out=_rows(got["gdn_w_out"]), dep_mix=token)
    return dict(w_in=_rows(got["mla_w_in"]), wt_uq=_uq_to_kernel_layout(_rows(got["mla_w_uq"]), axis=0),
                w_ukv=_cols(got["mla_w_ukv"]), w_out=_rows(got["mla_w_out"]), dep_mix=token)


def _layer_grad_slots(kind, big):
    if kind == "ffn":
        return {"ffn_w_gate": _unrows(big["wt_g"]), "ffn_w_up": _unrows(big["wt_u"]),
                "ffn_w_down": _unrows(big["w_down"])}
    if kind == "gdn":
        return {"gdn_w_in": _unrows(big["wt_in"]), "gdn_w_out": _unrows(big["w_out"])}
    return {"mla_w_in": _unrows(big["w_in"]), "mla_w_uq": _unrows(_uq_from_kernel_layout(big["wt_uq"], axis=0)),
            "mla_w_ukv": _uncols(big["w_ukv"]), "mla_w_out": _unrows(big["w_out"])}


def _small_weights(tiny, rep):
    prm = jnp.zeros((2, 8, LANES), F32)
    prm = prm.at[:, 0, :GDN_HEADS].set(rep["gdn_a_log"]).at[:, 1, :GDN_HEADS].set(rep["gdn_dt_bias"])
    out = {
        "gdn_conv_wt": jnp.transpose(_gather_rows(tiny["gdn_conv_w"]), (0, 2, 1)),
        "mla_q_norm_g": jnp.transpose(tiny["mla_q_norm_g"], (1, 0, 2)).reshape(2, MLA_Q_RANK),
        "mla_kv_norm_g": jnp.transpose(tiny["mla_kv_norm_g"], (1, 0, 2)).reshape(2, MLA_KV_RANK),
        "gdn_gate_prm": prm,
    }
    for k in ("norm_mix_g", "norm_ffn_g", "gdn_norm_g", "final_norm_g"):
        out[k] = rep[k]
    return out


def _rope_tables(positions):
    inv_freq = ROPE_THETA ** (-jnp.arange(0, MLA_ROPE, 2, dtype=F32) / MLA_ROPE)
    ang = positions.astype(F32)[:, None] * inv_freq
    cos, sin = jnp.cos(ang), jnp.sin(ang)
    reps = LANES // MLA_ROPE
    return jnp.tile(jnp.concatenate([cos, cos], axis=1), (1, reps)), jnp.tile(
        jnp.concatenate([-sin, sin], axis=1), (1, reps))


def kernel(x, c, positions, ada_w, ada_b, norm_mix_g, norm_ffn_g, gdn_w_in, gdn_conv_w, gdn_a_log, gdn_dt_bias, gdn_norm_g, gdn_w_out, mla_w_in, mla_q_norm_g, mla_kv_norm_g, mla_w_uq, mla_w_ukv, mla_w_out, ffn_w_gate, ffn_w_up, ffn_w_down, final_norm_g, loss_target, m_ada_w, m_ada_b, m_norm_mix_g, m_norm_ffn_g, m_gdn_w_in, m_gdn_conv_w, m_gdn_a_log, m_gdn_dt_bias, m_gdn_norm_g, m_gdn_w_out, m_mla_w_in, m_mla_q_norm_g, m_mla_kv_norm_g, m_mla_w_uq, m_mla_w_ukv, m_mla_w_out, m_ffn_w_gate, m_ffn_w_up, m_ffn_w_down, m_final_norm_g, v_ada_w, v_ada_b, v_norm_mix_g, v_norm_ffn_g, v_gdn_w_in, v_gdn_conv_w, v_gdn_a_log, v_gdn_dt_bias, v_gdn_norm_g, v_gdn_w_out, v_mla_w_in, v_mla_q_norm_g, v_mla_kv_norm_g, v_mla_w_uq, v_mla_w_ukv, v_mla_w_out, v_ffn_w_gate, v_ffn_w_up, v_ffn_w_down, v_final_norm_g):
    W = dict(ada_w=ada_w, ada_b=ada_b, norm_mix_g=norm_mix_g, norm_ffn_g=norm_ffn_g, gdn_w_in=gdn_w_in,
             gdn_conv_w=gdn_conv_w, gdn_a_log=gdn_a_log, gdn_dt_bias=gdn_dt_bias, gdn_norm_g=gdn_norm_g,
             gdn_w_out=gdn_w_out, mla_w_in=mla_w_in, mla_q_norm_g=mla_q_norm_g, mla_kv_norm_g=mla_kv_norm_g,
             mla_w_uq=mla_w_uq, mla_w_ukv=mla_w_ukv, mla_w_out=mla_w_out, ffn_w_gate=ffn_w_gate,
             ffn_w_up=ffn_w_up, ffn_w_down=ffn_w_down, final_norm_g=final_norm_g)
    M = dict(ada_w=m_ada_w, ada_b=m_ada_b, norm_mix_g=m_norm_mix_g, norm_ffn_g=m_norm_ffn_g, gdn_w_in=m_gdn_w_in,
             gdn_conv_w=m_gdn_conv_w, gdn_a_log=m_gdn_a_log, gdn_dt_bias=m_gdn_dt_bias, gdn_norm_g=m_gdn_norm_g,
             gdn_w_out=m_gdn_w_out, mla_w_in=m_mla_w_in, mla_q_norm_g=m_mla_q_norm_g,
             mla_kv_norm_g=m_mla_kv_norm_g, mla_w_uq=m_mla_w_uq, mla_w_ukv=m_mla_w_ukv, mla_w_out=m_mla_w_out,
             ffn_w_gate=m_ffn_w_gate, ffn_w_up=m_ffn_w_up, ffn_w_down=m_ffn_w_down, final_norm_g=m_final_norm_g)
    V = dict(ada_w=v_ada_w, ada_b=v_ada_b, norm_mix_g=v_norm_mix_g, norm_ffn_g=v_norm_ffn_g, gdn_w_in=v_gdn_w_in,
             gdn_conv_w=v_gdn_conv_w, gdn_a_log=v_gdn_a_log, gdn_dt_bias=v_gdn_dt_bias, gdn_norm_g=v_gdn_norm_g,
             gdn_w_out=v_gdn_w_out, mla_w_in=v_mla_w_in, mla_q_norm_g=v_mla_q_norm_g,
             mla_kv_norm_g=v_mla_kv_norm_g, mla_w_uq=v_mla_w_uq, mla_w_ukv=v_mla_w_ukv, mla_w_out=v_mla_w_out,
             ffn_w_gate=v_ffn_w_gate, ffn_w_up=v_ffn_w_up, ffn_w_down=v_ffn_w_down, final_norm_g=v_final_norm_g)
    me = 4 * lax.axis_index("x") + 2 * lax.axis_index("y") + lax.axis_index("c")
    t = x.shape[1]
    wc = ada_w.shape[-1]

    groups = [(layer, kind) for layer in range(DEPTH) for kind in ("mix", "ffn")]

    def group_srcs(i):
        layer, kind = groups[i]
        return [_view(k, W[k])[_layer_index(k, layer)].astype(BF16) for k in _group_names(layer, kind)]

    tiny_shapes = [c.shape, gdn_conv_w.shape, mla_q_norm_g.shape, mla_kv_norm_g.shape]
    first = _gather_two_level([_pack([c, gdn_conv_w, mla_q_norm_g, mla_kv_norm_g])] + group_srcs(0),
                              name="gather_first")
    tiny_g = first[0]
    c_g, conv_g, qn_g, kvn_g = _unpack(tiny_g, tiny_shapes, lead=(N_DEV,))
    c_all = c_g.reshape(N_DEV, D_MODEL)
    rep = _small_weights({"gdn_conv_w": conv_g, "mla_q_norm_g": qn_g, "mla_kv_norm_g": kvn_g}, W)

    def start_group(i, dep):
        layer, kind = groups[i]
        return _exchange_start(group_srcs(i), scatter=False, name=f"gather_start_{kind}_l{layer}", dep=dep)


    b_cols = lax.dynamic_slice_in_dim(ada_b, me * wc, wc, axis=1).reshape(DEPTH, 1, wc)
    mod_part = _ada_mod(c_all, ada_w, b_cols, name="ada_mod")
    (mod_g,) = _exchange([mod_part], scatter=False, name="gather_mod")
    mod_mine = lax.dynamic_index_in_dim(mod_g, me, axis=2, keepdims=False)
    mod = jnp.transpose(mod_mine, (1, 0, 2)).reshape(DEPTH, N_MOD, D_MODEL)
    gather = {1: start_group(1, mod_g)}
    for i in range(2, AHEAD + 1):
        gather[i] = start_group(i, gather[i - 1][4])

    def get_weights(layer, kind, after):
        i = groups.index((layer, kind))
        names = _group_names(layer, kind)
        if i == 0:
            return _group_weights(layer, kind, dict(zip(names, first[1:])), gather[AHEAD][4])
        srcs, lands = _exchange_wait(gather[i], after, scatter=False, name=f"gather_wait_{kind}_l{layer}")
        token = jnp.zeros((8, LANES), F32)
        if i + AHEAD < len(groups):
            gather[i + AHEAD] = start_group(i + AHEAD, lands[0])
            token = gather[i + AHEAD][4]
        got = {k: lax.dynamic_update_index_in_dim(z, s, me, 0) for k, s, z in zip(names, srcs, lands)}
        return _group_weights(layer, kind, got, token)

    scatter = []

    def put_grads(layer, kind, big):
        slots = _layer_grad_slots(kind, big)
        started = _exchange_start(list(slots.values()), scatter=True, name=f"scatter_start_{kind}_l{layer}")
        scatter.append((layer, kind, list(slots.keys()), started))
        return started[4]

    cos_t, sin_t = _rope_tables(positions[0])
    loss, dx, dmod, g = _local_step(x[0], loss_target[0], mod, cos_t, sin_t, rep, get_weights, put_grads)

    parts = {k: [None] * W[k].shape[0] for k in BIG}
    res = {}

    def wait_group(entry, after):
        layer, kind, names, started = entry
        srcs, lands = _exchange_wait(started, after, scatter=True, name=f"scatter_wait_{kind}_l{layer}")
        for k, s, z in zip(names, srcs, lands):
            own = lax.dynamic_index_in_dim(s, me, 0, keepdims=False)
            parts[k][_layer_index(k, layer)] = lax.dynamic_update_index_in_dim(z, own, me, 0)

    for entry in scatter[:-1]:
        wait_group(entry, dx)
    early = [k for k in BIG if k not in scatter[-1][2]]
    def update(k):
        outs = _adamw(parts[k], _view(k, W[k]), _view(k, M[k]), _view(k, V[k]), name=f"adamw_{k}")
        return tuple(_view(k, o) for o in outs)

    for k in early:
        res[k] = update(k)
    loss, dmod, done = lax.optimization_barrier((loss, dmod, [res[k] for k in early]))
    for k, r in zip(early, done):
        res[k] = r

    small_local = [dmod.reshape(DEPTH, N_MOD * D_MODEL), g["norm_mix_g"], g["norm_ffn_g"],
                   jnp.transpose(g["gdn_conv_wt"], (0, 2, 1)), g["gdn_a_log"], g["gdn_dt_bias"], g["gdn_norm_g"],
                   g["mla_q_norm_g"], g["mla_kv_norm_g"], g["final_norm_g"], loss.reshape(1)]
    small_shapes = [a.shape for a in small_local]
    (small_g,) = _exchange([_pack(small_local)], scatter=False, name="gather_small_grads")
    small_sum = _unpack(_sum_parts(small_g, name="sum_small_grads"), small_shapes)
    loss = small_sum[-1][0]
    dmod_all = _unpack(small_g, small_shapes[:1], lead=(N_DEV,))[0]
    sg = dict(zip(SMALL, small_sum))
    wait_group(scatter[-1], small_g)
    sg["gdn_conv_w"] = lax.dynamic_slice_in_dim(sg["gdn_conv_w"], me * gdn_conv_w.shape[1], gdn_conv_w.shape[1], 1)
    sg["mla_q_norm_g"] = lax.dynamic_slice_in_dim(sg["mla_q_norm_g"], me * mla_q_norm_g.shape[1],
                                                  mla_q_norm_g.shape[1], 1)
    sg["mla_kv_norm_g"] = lax.dynamic_slice_in_dim(sg["mla_kv_norm_g"], me * mla_kv_norm_g.shape[1],
                                                   mla_kv_norm_g.shape[1], 1)

    dmod_cols = jnp.transpose(lax.dynamic_slice_in_dim(dmod_all, me * wc, wc, axis=2), (1, 0, 2))
    res["ada_w"] = _ada_grad_adamw(c_all, dmod_cols, ada_w, m_ada_w, v_ada_w, name="ada_w_grad_adamw")
    for k in BIG:
        if k not in early:
            res[k] = update(k)
    shapes = [W[k].shape for k in SMALL]
    packed = [_pack([d[k] for k in SMALL]) for d in (sg, W, M, V)]
    outs = _adamw([packed[0][None]], packed[1][None], packed[2][None], packed[3][None], name="adamw_small")
    unpacked = [_unpack(o[0], shapes) for o in outs]
    for i, k in enumerate(SMALL):
        res[k] = tuple(u[i] for u in unpacked)

    return (loss, dx[None], *[res[k][0] for k in WEIGHTS], *[res[k][1] for k in WEIGHTS],
            *[res[k][2] for k in WEIGHTS], *[res[k][3] for k in WEIGHTS])
```

```python
import math

import jax
import jax.numpy as jnp
from jax import lax
from jax.experimental import pallas as pl
from jax.experimental.pallas import tpu as pltpu

F32 = jnp.float32
BF16 = jnp.bfloat16
MXU_DTYPE = jnp.bfloat16

N_DEV = 8
D_MODEL = 1024
DEPTH = 4
GDN_HEADS = 8
GDN_HEAD_DIM = 128
GDN_KEY_DIM = GDN_HEADS * GDN_HEAD_DIM
GDN_CHUNK = 64
GDN_HEAD_BATCH = 8
GDN_CONV = 4
GDN_PREP_HEADS = 2
GDN_MAIN = 4 * GDN_KEY_DIM
MLA_HEADS = 8
MLA_NOPE = 128
MLA_ROPE = 64
MLA_V = 128
MLA_Q_RANK = 384
MLA_KV_RANK = 256
MLA_IN = MLA_Q_RANK + MLA_KV_RANK + MLA_ROPE
MLA_QK = MLA_NOPE + MLA_ROPE
ROPE_THETA = 10000.0
D_FF = 2816
N_MOD = 6
EPS = 1e-6
LANES = 128
VMEM_LIMIT = 48 * 1024 * 1024

ADAM_LR = 0.001
ADAM_B1 = 0.9
ADAM_B2 = 0.999
ADAM_EPS = 1e-08
ADAM_WD = 0.01
ADAM_STEP = 10
ADAM_BC1 = 1.0 - ADAM_B1 ** ADAM_STEP
ADAM_BC2 = 1.0 - ADAM_B2 ** ADAM_STEP

NN = (((1,), (0,)), ((), ()))
NT = (((1,), (1,)), ((), ()))
TN = (((0,), (0,)), ((), ()))
NEG = -1e30


def _dotb(a, b, dims):
    return lax.dot_general(a.astype(MXU_DTYPE), b.astype(MXU_DTYPE), dims, preferred_element_type=F32)


def _split(a):
    hi = a.astype(BF16)
    return hi, (a - hi.astype(F32)).astype(BF16)


def _dotf(a, b, dims):
    ah, al = _split(a)
    bh, bl = _split(b)
    dot = lambda u, v: lax.dot_general(u, v, dims, preferred_element_type=F32)
    return dot(ah, bh) + (dot(ah, bl) + dot(al, bh))


def _params(*sem):
    return pltpu.CompilerParams(dimension_semantics=sem, vmem_limit_bytes=VMEM_LIMIT)


def _pick(n, pref, mult=LANES):
    best = None
    t = mult
    while t <= min(n, pref):
        if n % t == 0:
            best = t
        t += mult
    return best if best is not None else n


def _sigmoid(z):
    return 0.5 * jnp.tanh(0.5 * z) + 0.5


def _exchange(arrays, *, scatter, name):
    n = len(arrays)
    out_shape = tuple(
        jax.ShapeDtypeStruct(a.shape if scatter else (N_DEV,) + a.shape, a.dtype) for a in arrays)

    def body(*refs):
        ins, outs = refs[:n], refs[n:2 * n]
        send_sems, recv_sems, local_sems = refs[2 * n:]
        x, y, c = lax.axis_index("x"), lax.axis_index("y"), lax.axis_index("c")
        me = 4 * x + 2 * y + c
        copies = []
        for k in range(n):
            src_own = ins[k].at[me] if scatter else ins[k]
            own = pltpu.make_async_copy(src_own, outs[k].at[me], local_sems.at[k])
            own.start()
            copies.append(own)
        sends = []
        for p in range(1, N_DEV):
            px, py, pc = x ^ ((p >> 2) & 1), y ^ ((p >> 1) & 1), c ^ (p & 1)
            peer = 4 * px + 2 * py + pc
            for k in range(n):
                cp = pltpu.make_async_remote_copy(
                    src_ref=ins[k].at[peer] if scatter else ins[k],
                    dst_ref=outs[k].at[me],
                    send_sem=send_sems.at[k, p - 1],
                    recv_sem=recv_sems.at[k, p - 1],
                    device_id=(px, py, pc),
                    device_id_type=pl.DeviceIdType.MESH,
                )
                cp.start()
                sends.append((cp, k, peer, p))
        for cp, k, peer, p in sends:
            pltpu.make_async_remote_copy(
                src_ref=ins[k].at[peer] if scatter else ins[k],
                dst_ref=outs[k].at[peer],
                send_sem=send_sems.at[k, p - 1],
                recv_sem=recv_sems.at[k, p - 1],
                device_id=(x, y, c),
                device_id_type=pl.DeviceIdType.MESH,
            ).wait_recv()
        for cp, _, _, _ in sends:
            cp.wait_send()
        for own in copies:
            own.wait()

    any_spec = pl.BlockSpec(memory_space=pl.ANY)
    outs = pl.pallas_call(
        body,
        name=name,
        out_shape=out_shape,
        in_specs=[any_spec] * n,
        out_specs=tuple([any_spec] * n),
        scratch_shapes=[
            pltpu.SemaphoreType.DMA((n, N_DEV - 1)),
            pltpu.SemaphoreType.DMA((n, N_DEV - 1)),
            pltpu.SemaphoreType.DMA((n,)),
        ],
        compiler_params=pltpu.CompilerParams(has_side_effects=True),
    )(*arrays)
    return list(outs)


def _gather_two_level(arrays, *, name):
    n = len(arrays)
    out_shape = tuple(jax.ShapeDtypeStruct((N_DEV,) + a.shape, a.dtype) for a in arrays)

    def body(*refs):
        ins, outs = refs[:n], refs[n:2 * n]
        send_sems, recv_sems, local_sems = refs[2 * n:]
        x, y, c = lax.axis_index("x"), lax.axis_index("y"), lax.axis_index("c")
        me = 4 * x + 2 * y + c
        sibling = (x, y, 1 - c)
        chips = [(1 - x, y), (x, 1 - y), (1 - x, 1 - y)]

        def slot(px, py, pc):
            return 4 * px + 2 * py + pc

        def copy(k, q, block, to, src=None):
            return pltpu.make_async_remote_copy(
                src_ref=outs[k].at[slot(*block)] if src is None else src,
                dst_ref=outs[k].at[slot(*block)],
                send_sem=send_sems.at[k, q], recv_sem=recv_sems.at[k, q],
                device_id=to, device_id_type=pl.DeviceIdType.MESH)

        own = [pltpu.make_async_copy(ins[k], outs[k].at[me], local_sems.at[k]) for k in range(n)]
        for cp in own:
            cp.start()
        first = []
        for k in range(n):
            first.append(copy(k, 0, (x, y, c), sibling, src=ins[k]))
            first += [copy(k, 1 + j, (x, y, c), (*chip, c), src=ins[k]) for j, chip in enumerate(chips)]
        for cp in first:
            cp.start()
        passed = []
        for j, chip in enumerate(chips):
            for k in range(n):
                copy(k, 1 + j, (*chip, c), (x, y, c)).wait_recv()
                fwd = copy(k, 4 + j, (*chip, c), sibling)
                fwd.start()
                passed.append(fwd)
        for k in range(n):
            copy(k, 0, sibling, (x, y, c)).wait_recv()
            for j, chip in enumerate(chips):
                copy(k, 4 + j, (*chip, 1 - c), (x, y, c)).wait_recv()
        for cp in first + passed:
            cp.wait_send()
        for cp in own:
            cp.wait()

    any_spec = pl.BlockSpec(memory_space=pl.ANY)
    outs = pl.pallas_call(
        body, name=name, out_shape=out_shape, in_specs=[any_spec] * n, out_specs=tuple([any_spec] * n),
        scratch_shapes=[pltpu.SemaphoreType.DMA((n, N_DEV - 1)), pltpu.SemaphoreType.DMA((n, N_DEV - 1)),
                        pltpu.SemaphoreType.DMA((n,))],
        compiler_params=pltpu.CompilerParams(has_side_effects=True),
    )(*arrays)
    return list(outs)


def _peer(x, y, c, p):
    return x ^ ((p >> 2) & 1), y ^ ((p >> 1) & 1), c ^ (p & 1)


def _exchange_start(arrays, *, scatter, name, dep=None):
    n = len(arrays)
    deps = [] if dep is None else [dep]
    lands = [lax.empty(a.shape if scatter else (N_DEV,) + a.shape, a.dtype) for a in arrays]

    def body(*refs):
        ins, zones = refs[:n], refs[n:2 * n]
        send_sems, recv_sems = refs[2 * n + len(deps)], refs[2 * n + len(deps) + 1]
        token = refs[-1]
        x, y, c = lax.axis_index("x"), lax.axis_index("y"), lax.axis_index("c")
        me = 4 * x + 2 * y + c
        for p in range(1, N_DEV):
            px, py, pc = _peer(x, y, c, p)
            for k in range(n):
                pltpu.make_async_remote_copy(
                    src_ref=ins[k].at[4 * px + 2 * py + pc] if scatter else ins[k],
                    dst_ref=zones[k].at[me],
                    send_sem=send_sems.at[k * (N_DEV - 1) + p - 1],
                    recv_sem=recv_sems.at[k * (N_DEV - 1) + p - 1],
                    device_id=(px, py, pc),
                    device_id_type=pl.DeviceIdType.MESH,
                ).start()
        token[...] = jnp.zeros_like(token)

    hbm = pl.BlockSpec(memory_space=pltpu.HBM)
    sem = pl.BlockSpec(memory_space=pltpu.SEMAPHORE)
    outs = pl.pallas_call(
        body,
        name=name,
        out_shape=(pltpu.SemaphoreType.DMA((n * (N_DEV - 1),)), pltpu.SemaphoreType.DMA((n * (N_DEV - 1),)),
                   *[pltpu.HBM(a.shape, a.dtype) for a in arrays], *[pltpu.HBM(z.shape, z.dtype) for z in lands],
                   jax.ShapeDtypeStruct((8, LANES), F32)),
        in_specs=[hbm] * (2 * n) + [pl.BlockSpec(memory_space=pl.ANY)] * len(deps),
        out_specs=(sem, sem, *[hbm] * (2 * n), pl.BlockSpec(memory_space=pltpu.VMEM)),
        input_output_aliases={k: 2 + k for k in range(2 * n)},
        compiler_params=pltpu.CompilerParams(has_side_effects=pltpu.SideEffectType.DATAFLOW_SIDE_EFFECTING),
    )(*[pltpu.with_memory_space_constraint(a, pltpu.HBM) for a in arrays],
      *[pltpu.with_memory_space_constraint(z, pltpu.HBM) for z in lands], *deps)
    return outs[0], outs[1], list(outs[2:2 + n]), list(outs[2 + n:2 + 2 * n]), outs[-1]


def _exchange_wait(started, after, *, scatter, name):
    send_sems, recv_sems, srcs, lands, _ = started
    n = len(srcs)

    def body(*refs):
        ins, zones = refs[:n], refs[n:2 * n]
        s_sems, r_sems = refs[2 * n], refs[2 * n + 1]
        x, y, c = lax.axis_index("x"), lax.axis_index("y"), lax.axis_index("c")
        for p in range(1, N_DEV):
            px, py, pc = _peer(x, y, c, p)
            peer = 4 * px + 2 * py + pc
            for k in range(n):
                cp = pltpu.make_async_remote_copy(
                    src_ref=ins[k].at[peer] if scatter else ins[k],
                    dst_ref=zones[k].at[peer],
                    send_sem=s_sems.at[k * (N_DEV - 1) + p - 1],
                    recv_sem=r_sems.at[k * (N_DEV - 1) + p - 1],
                    device_id=(px, py, pc),
                    device_id_type=pl.DeviceIdType.MESH,
                )
                cp.wait_send()
                cp.wait_recv()

    hbm = pl.BlockSpec(memory_space=pltpu.HBM)
    sem = pl.BlockSpec(memory_space=pltpu.SEMAPHORE)
    outs = pl.pallas_call(
        body,
        name=name,
        out_shape=tuple(pltpu.HBM(a.shape, a.dtype) for a in srcs + lands),
        in_specs=[hbm] * (2 * n) + [sem, sem, pl.BlockSpec(memory_space=pl.ANY)],
        out_specs=tuple([hbm] * (2 * n)),
        input_output_aliases={k: k for k in range(2 * n)},
        compiler_params=pltpu.CompilerParams(has_side_effects=pltpu.SideEffectType.DATAFLOW_SIDE_EFFECTING),
    )(*srcs, *lands, send_sems, recv_sems, after)
    return list(outs[:n]), list(outs[n:])


def _mm(a, b, *, mode, out_dtype, name, add=None, tm=512, tn=512, b_rows=None, dep=None):
    rows_b = b.shape[0] if b_rows is None else b_rows
    if mode == "nn":
        (m, kd), nd = a.shape, b.shape[1]
        assert kd == rows_b
    elif mode == "nt":
        (m, kd), nd = a.shape, rows_b
    else:
        (kd, m), nd = a.shape, b.shape[1]
    tm = _pick(m, tm, LANES if mode == "tn" else 16)
    tn = _pick(nd, tn)
    dims = {"nn": NN, "nt": NT, "tn": TN}[mode]
    ni, nj = m // tm, nd // tn
    a_bytes, b_bytes = a.size * a.dtype.itemsize, b.size * b.dtype.itemsize
    i_outer = a_bytes + ni * b_bytes <= b_bytes + nj * a_bytes
    ij = (lambda g0, g1: (g0, g1)) if i_outer else (lambda g0, g1: (g1, g0))
    a_spec = (pl.BlockSpec((kd, tm), lambda g0, g1: (0, ij(g0, g1)[0])) if mode == "tn"
              else pl.BlockSpec((tm, kd), lambda g0, g1: (ij(g0, g1)[0], 0)))
    b_spec = (pl.BlockSpec((tn, kd), lambda g0, g1: (ij(g0, g1)[1], 0)) if mode == "nt"
              else pl.BlockSpec((kd, tn), lambda g0, g1: (0, ij(g0, g1)[1])))
    o_spec = pl.BlockSpec((tm, tn), lambda g0, g1: ij(g0, g1))
    has_add = add is not None

    def body(*refs):
        a_ref, b_ref = refs[0], refs[1]
        o_ref = refs[-1]
        acc = _dotb(a_ref[...], b_ref[...], dims)
        if has_add:
            acc = acc + refs[2][...].astype(F32)
        o_ref[...] = acc.astype(o_ref.dtype)

    ins = [a, b] + ([add] if has_add else []) + ([] if dep is None else [dep])
    specs = ([a_spec, b_spec] + ([o_spec] if has_add else [])
             + ([] if dep is None else [pl.BlockSpec((8, LANES), lambda g0, g1: (0, 0))]))
    return pl.pallas_call(
        body, name=name, grid=(ni, nj) if i_outer else (nj, ni), in_specs=specs, out_specs=o_spec,
        out_shape=jax.ShapeDtypeStruct((m, nd), out_dtype),
        compiler_params=_params("parallel", "parallel"),
    )(*ins)


def _mm_resid(a, b, x, gate, *, name, tm=256, tn=1024):
    m, kd = a.shape
    nd = b.shape[1]
    tm = _pick(m, tm, 16)
    tn = _pick(nd, tn)
    o_spec = pl.BlockSpec((tm, tn), lambda i, j: (i, j))

    def body(a_ref, b_ref, x_ref, g_ref, xo_ref, y_ref):
        y = _dotb(a_ref[...], b_ref[...], NN)
        y_ref[...] = y.astype(y_ref.dtype)
        xo_ref[...] = x_ref[...] + g_ref[...] * y

    return pl.pallas_call(
        body, name=name, grid=(m // tm, nd // tn),
        in_specs=[pl.BlockSpec((tm, kd), lambda i, j: (i, 0)), pl.BlockSpec((kd, tn), lambda i, j: (0, j)),
                  o_spec, pl.BlockSpec((1, tn), lambda i, j: (0, j))],
        out_specs=(o_spec, o_spec),
        out_shape=(jax.ShapeDtypeStruct((m, nd), F32), jax.ShapeDtypeStruct((m, nd), BF16)),
        compiler_params=_params("parallel", "parallel"),
    )(a, b, x, gate)


ROWS = 256


def _row_spec(width, rows=ROWS):
    return pl.BlockSpec((rows, width), lambda i: (i, 0))


def _const_spec(shape):
    return pl.BlockSpec(shape, lambda i: tuple(0 for _ in shape))


def _adaln_fwd(x, g, scale, shift, *, name):
    t, d = x.shape

    def body(x_ref, g_ref, sc_ref, sh_ref, h_ref):
        xv = x_ref[...]
        r = lax.rsqrt(jnp.mean(xv * xv, axis=-1, keepdims=True) + EPS)
        h_ref[...] = (xv * r * g_ref[...] * (1.0 + sc_ref[...]) + sh_ref[...]).astype(h_ref.dtype)

    return pl.pallas_call(
        body, name=name, grid=(t // ROWS,),
        in_specs=[_row_spec(d), _const_spec((1, d)), _const_spec((1, d)), _const_spec((1, d))],
        out_specs=_row_spec(d), out_shape=jax.ShapeDtypeStruct((t, d), BF16),
        compiler_params=_params("parallel"),
    )(x, g, scale, shift)


def _adaln_bwd(x, g, scale, shift, dh, dres, dep, *, name):
    t, d = x.shape

    def body(x_ref, g_ref, sc_ref, sh_ref, dh_ref, dr_ref, dep_ref, dx_ref, st_ref):
        @pl.when(pl.program_id(0) == 0)
        def _():
            st_ref[...] = jnp.zeros_like(st_ref)

        xv = x_ref[...]
        dhv = dh_ref[...].astype(F32)
        gv = g_ref[...]
        r = lax.rsqrt(jnp.mean(xv * xv, axis=-1, keepdims=True) + EPS)
        xh = xv * r
        nv = xh * gv
        dn = dhv * (1.0 + sc_ref[...])
        dxh = dn * gv
        dx_ref[...] = dr_ref[...] + r * (dxh - xh * jnp.mean(dxh * xh, axis=-1, keepdims=True))
        st_ref[0:1, :] += jnp.sum(dn * xh, axis=0, keepdims=True)
        st_ref[1:2, :] += jnp.sum(dhv * nv, axis=0, keepdims=True)
        st_ref[2:3, :] += jnp.sum(dhv, axis=0, keepdims=True)

    return pl.pallas_call(
        body, name=name, grid=(t // ROWS,),
        in_specs=[_row_spec(d), _const_spec((1, d)), _const_spec((1, d)), _const_spec((1, d)),
                  _row_spec(d), _row_spec(d), _const_spec((8, LANES))],
        out_specs=(_row_spec(d), _const_spec((8, d))),
        out_shape=(jax.ShapeDtypeStruct((t, d), F32), jax.ShapeDtypeStruct((8, d), F32)),
        compiler_params=_params("arbitrary"),
    )(x, g, scale, shift, dh, dres, dep)


def _adaln_gate_bwd(x, g, scale, shift, dh, dres, dep, y_up, gate_up, *, name):
    t, d = x.shape

    def body(x_ref, g_ref, sc_ref, sh_ref, dh_ref, dr_ref, dep_ref, y_ref, gu_ref, dx_ref, st_ref, dy_ref):
        @pl.when(pl.program_id(0) == 0)
        def _():
            st_ref[...] = jnp.zeros_like(st_ref)

        xv = x_ref[...]
        dhv = dh_ref[...].astype(F32)
        gv = g_ref[...]
        r = lax.rsqrt(jnp.mean(xv * xv, axis=-1, keepdims=True) + EPS)
        xh = xv * r
        nv = xh * gv
        dn = dhv * (1.0 + sc_ref[...])
        dxh = dn * gv
        dx = dr_ref[...] + r * (dxh - xh * jnp.mean(dxh * xh, axis=-1, keepdims=True))
        dx_ref[...] = dx
        dy_ref[...] = (dx * gu_ref[...]).astype(dy_ref.dtype)
        st_ref[0:1, :] += jnp.sum(dn * xh, axis=0, keepdims=True)
        st_ref[1:2, :] += jnp.sum(dhv * nv, axis=0, keepdims=True)
        st_ref[2:3, :] += jnp.sum(dhv, axis=0, keepdims=True)
        st_ref[3:4, :] += jnp.sum(dx * y_ref[...].astype(F32), axis=0, keepdims=True)

    return pl.pallas_call(
        body, name=name, grid=(t // ROWS,),
        in_specs=[_row_spec(d), _const_spec((1, d)), _const_spec((1, d)), _const_spec((1, d)),
                  _row_spec(d), _row_spec(d), _const_spec((8, LANES)), _row_spec(d), _const_spec((1, d))],
        out_specs=(_row_spec(d), _const_spec((8, d)), _row_spec(d)),
        out_shape=(jax.ShapeDtypeStruct((t, d), F32), jax.ShapeDtypeStruct((8, d), F32),
                   jax.ShapeDtypeStruct((t, d), BF16)),
        compiler_params=_params("arbitrary"),
    )(x, g, scale, shift, dh, dres, dep, y_up, gate_up)


def _gate_bwd(dxo, y, gate, dep, *, name):
    t, d = dxo.shape

    def body(dx_ref, y_ref, g_ref, dep_ref, dy_ref, st_ref):
        @pl.when(pl.program_id(0) == 0)
        def _():
            st_ref[...] = jnp.zeros_like(st_ref)

        dxv = dx_ref[...]
        dy_ref[...] = (dxv * g_ref[...]).astype(dy_ref.dtype)
        st_ref[0:1, :] += jnp.sum(dxv * y_ref[...], axis=0, keepdims=True)

    return pl.pallas_call(
        body, name=name, grid=(t // ROWS,),
        in_specs=[_row_spec(d), _row_spec(d), _const_spec((1, d)), _const_spec((8, LANES))],
        out_specs=(_row_spec(d), _const_spec((8, d))),
        out_shape=(jax.ShapeDtypeStruct((t, d), BF16), jax.ShapeDtypeStruct((8, d), F32)),
        compiler_params=_params("arbitrary"),
    )(dxo, y, gate, dep)


def _loss_head(x, g, target, *, name):
    t, d = x.shape

    def body(x_ref, g_ref, t_ref, dx_ref, st_ref, ls_ref):
        @pl.when(pl.program_id(0) == 0)
        def _():
            st_ref[...] = jnp.zeros_like(st_ref)
            ls_ref[...] = jnp.zeros_like(ls_ref)

        xv = x_ref[...]
        gv = g_ref[...]
        r = lax.rsqrt(jnp.mean(xv * xv, axis=-1, keepdims=True) + EPS)
        xh = xv * r
        err = xh * gv - t_ref[...]
        ls_ref[...] += 0.5 * jnp.sum(jnp.mean(err * err, axis=-1, keepdims=True))
        dy = err * (1.0 / d)
        dxh = dy * gv
        dx_ref[...] = r * (dxh - xh * jnp.mean(dxh * xh, axis=-1, keepdims=True))
        st_ref[0:1, :] += jnp.sum(dy * xh, axis=0, keepdims=True)

    return pl.pallas_call(
        body, name=name, grid=(t // ROWS,),
        in_specs=[_row_spec(d), _const_spec((1, d)), _row_spec(d)],
        out_specs=(_row_spec(d), _const_spec((8, d)), _const_spec((8, LANES))),
        out_shape=(jax.ShapeDtypeStruct((t, d), F32), jax.ShapeDtypeStruct((8, d), F32),
                   jax.ShapeDtypeStruct((8, LANES), F32)),
        compiler_params=_params("arbitrary"),
    )(x, g, target)


FFN_BLOCK = D_FF // 2
FFN_ROWS = 512


def _ffn_chunks(width):
    edges = [min(width, 3 * LANES * i) for i in range(width // (3 * LANES) + 2)]
    return [slice(lo, hi) for lo, hi in zip(edges[:-1], edges[1:]) if hi > lo]


def _ffn_gu_fwd(h, wg, wu, dep, *, name):
    t, d = h.shape
    tn = FFN_BLOCK

    chunks = _ffn_chunks(tn)
    rows = _pick(t, FFN_ROWS, 16)

    def body(h_ref, wg_ref, wu_ref, dep_ref, s_ref, a_ref, b_ref):
        hv = h_ref[...]
        ab = [(_dotb(hv, wg_ref[sl, :], NT), _dotb(hv, wu_ref[sl, :], NT)) for sl in chunks]
        for sl, (a, b) in zip(chunks, ab):
            s_ref[:, sl] = (a * _sigmoid(a) * b).astype(s_ref.dtype)
            a_ref[:, sl] = a.astype(a_ref.dtype)
            b_ref[:, sl] = b.astype(b_ref.dtype)

    w_spec = pl.BlockSpec((tn, d), lambda j, i: (j, 0))
    o_spec = pl.BlockSpec((rows, tn), lambda j, i: (i, j))
    return pl.pallas_call(
        body, name=name, grid=(D_FF // tn, t // rows),
        in_specs=[pl.BlockSpec((rows, d), lambda j, i: (i, 0)), w_spec, w_spec,
                  pl.BlockSpec((8, LANES), lambda j, i: (0, 0))],
        out_specs=(o_spec, o_spec, o_spec),
        out_shape=(jax.ShapeDtypeStruct((t, D_FF), BF16),) * 3,
        compiler_params=_params("parallel", "parallel"),
    )(h, wg, wu, dep)


def _ffn_down_dx(dy, w_down, a, b, *, name):
    t, d = dy.shape
    tn = FFN_BLOCK

    chunks = _ffn_chunks(tn)
    rows = _pick(t, FFN_ROWS, 16)

    def body(dy_ref, w_ref, a_ref, b_ref, da_ref, db_ref):
        dyv = dy_ref[...]
        ds = [_dotb(dyv, w_ref[sl, :], NT) for sl in chunks]
        for sl, dsc in zip(chunks, ds):
            av = a_ref[:, sl].astype(F32)
            sg = _sigmoid(av)
            da_ref[:, sl] = (dsc * b_ref[:, sl].astype(F32) * sg * (1.0 + av * (1.0 - sg))).astype(da_ref.dtype)
            db_ref[:, sl] = (dsc * av * sg).astype(db_ref.dtype)

    o_spec = pl.BlockSpec((rows, tn), lambda j, i: (i, j))
    return pl.pallas_call(
        body, name=name, grid=(D_FF // tn, t // rows),
        in_specs=[pl.BlockSpec((rows, d), lambda j, i: (i, 0)), pl.BlockSpec((tn, d), lambda j, i: (j, 0)),
                  o_spec, o_spec],
        out_specs=(o_spec, o_spec),
        out_shape=(jax.ShapeDtypeStruct((t, D_FF), BF16),) * 2,
        compiler_params=_params("parallel", "parallel"),
    )(dy, w_down, a, b)


def _shift_rows(v, s, rows):
    if s == 0:
        return v
    return jnp.where(rows >= s, pltpu.roll(v, s, 0), 0.0)


def _unshift_rows(v, s, rows, t):
    if s == 0:
        return v
    return jnp.where(rows < t - s, pltpu.roll(v, t - s, 0), 0.0)


def _conv_silu(x, w, rows):
    z = w[GDN_CONV - 1:GDN_CONV, :] * x
    for j in range(GDN_CONV - 1):
        z = z + w[j:j + 1, :] * _shift_rows(x, GDN_CONV - 1 - j, rows)
    sg = _sigmoid(z)
    return z, sg, z * sg


def _gdn_prep_fwd(proj, conv_wt, *, name):
    t = proj.shape[0]
    nh = GDN_HEADS

    hp = GDN_PREP_HEADS
    wd = hp * LANES

    def body(x_ref, w_ref, y_ref):
        j = pl.program_id(0) * hp
        rows = lax.broadcasted_iota(jnp.int32, (t, LANES), 0)
        qscale = jnp.where(j < nh, GDN_HEAD_DIM ** -0.5, 1.0)
        for i in range(hp):
            sl = slice(i * LANES, (i + 1) * LANES)
            _, _, s = _conv_silu(x_ref[:, sl], w_ref[:, sl], rows)
            rs = lax.rsqrt(jnp.sum(s * s, axis=-1, keepdims=True) + EPS)
            y_ref[:, sl] = jnp.where(j < 2 * nh, s * rs * qscale, s)

    return pl.pallas_call(
        body, name=name, grid=(3 * nh // hp,),
        in_specs=[pl.BlockSpec((t, wd), lambda j: (0, j)), pl.BlockSpec((GDN_CONV, wd), lambda j: (0, j))],
        out_specs=pl.BlockSpec((t, wd), lambda j: (0, j)),
        out_shape=jax.ShapeDtypeStruct((t, 3 * GDN_KEY_DIM), F32),
        compiler_params=_params("parallel"),
    )(proj, conv_wt)


def _gdn_prep_bwd(proj, conv_wt, dy, *, name):
    t = proj.shape[0]
    nh = GDN_HEADS

    hp = GDN_PREP_HEADS
    wd = hp * LANES
    per_seg = nh // hp

    def body(x_ref, w_ref, dy_ref, dx_ref, dw_ref):
        j = pl.program_id(0) * hp
        rows = lax.broadcasted_iota(jnp.int32, (t, LANES), 0)
        qscale = jnp.where(j < nh, GDN_HEAD_DIM ** -0.5, 1.0)
        for i in range(hp):
            sl = slice(i * LANES, (i + 1) * LANES)
            x = x_ref[:, sl]
            w = w_ref[:, sl]
            z, sg, s = _conv_silu(x, w, rows)
            rs = lax.rsqrt(jnp.sum(s * s, axis=-1, keepdims=True) + EPS)
            dyv = dy_ref[:, sl]
            nv = s * rs
            de = dyv * qscale
            ds_qk = rs * (de - nv * jnp.sum(de * nv, axis=-1, keepdims=True))
            ds = jnp.where(j < 2 * nh, ds_qk, dyv)
            dz = ds * sg * (1.0 + z * (1.0 - sg))
            dx = w[GDN_CONV - 1:GDN_CONV, :] * dz
            dw_ref[GDN_CONV - 1:GDN_CONV, sl] = jnp.sum(dz * x, axis=0, keepdims=True)
            for k in range(GDN_CONV - 1):
                sh = GDN_CONV - 1 - k
                dx = dx + w[k:k + 1, :] * _unshift_rows(dz, sh, rows, t)
                dw_ref[k:k + 1, sl] = jnp.sum(dz * _shift_rows(x, sh, rows), axis=0, keepdims=True)
            dx_ref[:, sl] = dx.astype(dx_ref.dtype)

    return pl.pallas_call(
        body, name=name, grid=(3 * nh // hp,),
        in_specs=[pl.BlockSpec((t, wd), lambda j: (0, j)), pl.BlockSpec((GDN_CONV, wd), lambda j: (0, j)),
                  pl.BlockSpec((None, t, wd), lambda j: (j // per_seg, 0, j % per_seg))],
        out_specs=(pl.BlockSpec((t, wd), lambda j: (0, j)), pl.BlockSpec((GDN_CONV, wd), lambda j: (0, j))),
        out_shape=(jax.ShapeDtypeStruct((t, 3 * GDN_KEY_DIM), BF16),
                   jax.ShapeDtypeStruct((GDN_CONV, 3 * GDN_KEY_DIM), F32)),
        compiler_params=_params("parallel"),
    )(proj, conv_wt, dy)


def _softplus(z):
    return jnp.maximum(z, 0.0) + jnp.log(1.0 + jnp.exp(-jnp.abs(z)))


def _gdn_gate_fwd(ab, prm, *, name):
    t = ab.shape[0]

    def body(ab_ref, p_ref, o_ref):
        v = ab_ref[...]
        lane = lax.broadcasted_iota(jnp.int32, v.shape, 1)
        g = -jnp.exp(p_ref[0:1, :]) * _softplus(v + p_ref[1:2, :])
        o_ref[...] = jnp.where(lane < GDN_HEADS, g, jnp.where(lane < 2 * GDN_HEADS, _sigmoid(v), 0.0))

    return pl.pallas_call(
        body, name=name, grid=(t // ROWS,),
        in_specs=[_row_spec(LANES), _const_spec((8, LANES))], out_specs=_row_spec(LANES),
        out_shape=jax.ShapeDtypeStruct((t, LANES), F32), compiler_params=_params("parallel"),
    )(ab, prm)


def _gdn_gate_bwd(ab, prm, dgb, *, name):
    t = ab.shape[0]

    def body(ab_ref, p_ref, d_ref, o_ref, st_ref):
        @pl.when(pl.program_id(0) == 0)
        def _():
            st_ref[...] = jnp.zeros_like(st_ref)

        v = ab_ref[...]
        dv = d_ref[...]
        lane = lax.broadcasted_iota(jnp.int32, v.shape, 1)
        is_a = lane < GDN_HEADS
        is_b = jnp.logical_and(lane >= GDN_HEADS, lane < 2 * GDN_HEADS)
        a_exp = jnp.exp(p_ref[0:1, :])
        zz = v + p_ref[1:2, :]
        g = -a_exp * _softplus(zz)
        da = dv * (-a_exp) * _sigmoid(zz)
        beta = _sigmoid(v)
        db = dv * beta * (1.0 - beta)
        o_ref[...] = jnp.where(is_a, da, jnp.where(is_b, db, 0.0)).astype(o_ref.dtype)
        st_ref[0:1, :] += jnp.sum(jnp.where(is_a, dv * g, 0.0), axis=0, keepdims=True)
        st_ref[1:2, :] += jnp.sum(jnp.where(is_a, da, 0.0), axis=0, keepdims=True)

    return pl.pallas_call(
        body, name=name, grid=(t // ROWS,),
        in_specs=[_row_spec(LANES), _const_spec((8, LANES)), _row_spec(LANES)],
        out_specs=(_row_spec(LANES), _const_spec((8, LANES))),
        out_shape=(jax.ShapeDtypeStruct((t, LANES), BF16), jax.ShapeDtypeStruct((8, LANES), F32)),
        compiler_params=_params("arbitrary"),
    )(ab, prm, dgb)


def _gdn_local(qs, ks, vs, gbs, bbs, tinvs=None):
    nh = len(qs)
    cs = qs[0].shape[0]
    hs = range(nh)
    r = lax.broadcasted_iota(jnp.int32, (cs, cs), 0)
    c = lax.broadcasted_iota(jnp.int32, (cs, cs), 1)
    tril, strict, eye = r >= c, r > c, r == c
    ident = jnp.where(eye, 1.0, 0.0)
    g_colb = [gbs[h][:, :cs] for h in hs]
    g_row = [jnp.sum(jnp.where(eye, g_colb[h], 0.0), axis=0, keepdims=True) for h in hs]
    gc_col = [jnp.sum(jnp.where(tril, g_row[h], 0.0), axis=1, keepdims=True) for h in hs]
    gc_row = [jnp.sum(jnp.where(r <= c, g_colb[h], 0.0), axis=0, keepdims=True) for h in hs]
    decay = [jnp.exp(jnp.where(tril, gc_col[h] - gc_row[h], NEG)) for h in hs]
    gamma = [jnp.exp(gc_col[h]) for h in hs]
    gcl = [gc_col[h][cs - 1:cs, :] for h in hs]
    gl = [jnp.exp(gcl[h]) for h in hs]
    kdec = [jnp.exp(gcl[h] - gc_col[h]) for h in hs]
    kb = [ks[h] * bbs[h] for h in hs]
    kk = [_dotb(kb[h], ks[h], NT) for h in hs]
    qk = [_dotb(qs[h], ks[h], NT) for h in hs]
    lmat = [jnp.where(strict, kk[h] * decay[h], 0.0) for h in hs]
    pmat = [jnp.where(tril, qk[h] * decay[h], 0.0) for h in hs]
    if tinvs is None:
        xm = [-lmat[h] for h in hs]
        tinv = [ident + xm[h] for h in hs]
        for _ in range(int(math.log2(cs)) - 1):
            xm = [_dotf(xm[h], xm[h], NN) for h in hs]
            tinv = [tinv[h] + _dotf(tinv[h], xm[h], NN) for h in hs]
    else:
        tinv = tinvs
    vb = [vs[h] * bbs[h] for h in hs]
    kg = [kb[h] * gamma[h] for h in hs]
    u = [_dotf(tinv[h], vb[h], NN) for h in hs]
    w = [_dotf(tinv[h], kg[h], NN) for h in hs]
    return [dict(tril=tril, strict=strict, eye=eye, r=r, c=c, decay=decay[h], gamma=gamma[h], gl=gl[h], kdec=kdec[h],
                 kb=kb[h], lmat=lmat[h], tinv=tinv[h], vb=vb[h], kg=kg[h], u=u[h], w=w[h], pmat=pmat[h],
                 qd=qs[h] * gamma[h], kd=ks[h] * kdec[h]) for h in hs]


def _head_columns(gbeta, cs):
    gbs = [jnp.broadcast_to(gbeta[:, h:h + 1], (cs, LANES)) for h in range(GDN_HEADS)]
    bbs = [jnp.broadcast_to(gbeta[:, GDN_HEADS + h:GDN_HEADS + h + 1], (cs, LANES)) for h in range(GDN_HEADS)]
    return gbs, bbs


def _gdn_chunk_fwd(qkv, gbeta, *, name):
    t = qkv.shape[0]
    nh, cs, hd = GDN_HEADS, GDN_CHUNK, GDN_HEAD_DIM
    nc = t // cs

    hb = GDN_HEAD_BATCH
    ng = nh // hb
    assert ng == 1

    def body(q_ref, k_ref, v_ref, gb_ref, o_ref, st_ref, ti_ref, s_ref):
        @pl.when(pl.program_id(1) == 0)
        def _():
            s_ref[...] = jnp.zeros_like(s_ref)

        sls = [slice(i * hd, (i + 1) * hd) for i in range(hb)]
        hs = range(hb)
        s = [s_ref[i] for i in hs]
        gbs, bbs = _head_columns(gb_ref[...], cs)
        lo = _gdn_local([q_ref[:, sl] for sl in sls], [k_ref[:, sl] for sl in sls], [v_ref[:, sl] for sl in sls],
                        gbs, bbs)
        ws = [_dotb(lo[i]["w"], s[i], NN) for i in hs]
        qs = [_dotb(lo[i]["qd"], s[i], NN) for i in hs]
        vn = [lo[i]["u"] - ws[i] for i in hs]
        pv = [_dotb(lo[i]["pmat"], vn[i], NN) for i in hs]
        kv = [_dotb(lo[i]["kd"], vn[i], TN) for i in hs]
        for i, sl in enumerate(sls):
            st_ref[i, 0] = s[i]
            ti_ref[i, 0] = lo[i]["tinv"]
            o_ref[:, sl] = qs[i] + pv[i]
            s_ref[i] = s[i] * lo[i]["gl"] + kv[i]

    col = lambda off: pl.BlockSpec((cs, hb * hd), lambda h, n: (n, off + h))
    return pl.pallas_call(
        body, name=name, grid=(ng, nc),
        in_specs=[col(0), col(ng), col(2 * ng), pl.BlockSpec((cs, LANES), lambda h, n: (n, 0))],
        out_specs=(col(0), pl.BlockSpec((hb, 1, hd, hd), lambda h, n: (h, n, 0, 0)),
                   pl.BlockSpec((hb, 1, cs, cs), lambda h, n: (h, n, 0, 0))),
        out_shape=(jax.ShapeDtypeStruct((t, nh * hd), F32), jax.ShapeDtypeStruct((nh, nc, hd, hd), F32),
                   jax.ShapeDtypeStruct((nh, nc, cs, cs), F32)),
        scratch_shapes=[pltpu.VMEM((hb, hd, hd), F32)],
        compiler_params=_params("parallel", "arbitrary"),
    )(qkv, qkv, qkv, gbeta)


def _gdn_chunk_bwd(qkv, gbeta, states, tinvs, do, *, name):
    t = qkv.shape[0]
    nh, cs, hd = GDN_HEADS, GDN_CHUNK, GDN_HEAD_DIM
    nc = t // cs

    hb = GDN_HEAD_BATCH
    ng = nh // hb
    assert ng == 1

    def heads_bwd(q, k, v, gb, bb, s, ti, dsn, dov):
        hs = range(len(q))
        lo = _gdn_local(q, k, v, gb, bb, ti)
        tril, strict, eye, r, c = lo[0]["tril"], lo[0]["strict"], lo[0]["eye"], lo[0]["r"], lo[0]["c"]
        rowi = lax.broadcasted_iota(jnp.int32, (cs, 1), 0)
        get = lambda name: [lo[h][name] for h in hs]
        decay, gamma, gl, kdec = get("decay"), get("gamma"), get("gl"), get("kdec")
        kb, tinv, w, pmat, kd, qd = get("kb"), get("tinv"), get("w"), get("pmat"), get("kd"), get("qd")
        ws = [_dotb(w[h], s[h], NN) for h in hs]
        pdo = [_dotb(pmat[h], dov[h], TN) for h in hs]
        kds = [_dotb(kd[h], dsn[h], NN) for h in hs]
        dqd = [_dotb(dov[h], s[h], NT) for h in hs]
        qdo = [_dotb(qd[h], dov[h], TN) for h in hs]
        vn = [lo[h]["u"] - ws[h] for h in hs]
        dvn = [pdo[h] + kds[h] for h in hs]
        dp = [jnp.where(tril, _dotb(dov[h], vn[h], NT), 0.0) for h in hs]
        dkd = [_dotb(vn[h], dsn[h], NT) for h in hs]
        dw = [-_dotb(dvn[h], s[h], NT) for h in hs]
        wdv = [_dotb(w[h], dvn[h], TN) for h in hs]
        dvb = [_dotf(tinv[h], dvn[h], TN) for h in hs]
        dt1 = [_dotf(dvn[h], lo[h]["vb"], NT) for h in hs]
        dkg = [_dotf(tinv[h], dw[h], TN) for h in hs]
        dt2 = [_dotf(dw[h], lo[h]["kg"], NT) for h in hs]
        tdt = [_dotf(tinv[h], dt1[h] + dt2[h], TN) for h in hs]
        dl = [jnp.where(strict, -_dotf(tdt[h], tinv[h], NT), 0.0) for h in hs]
        dkk = [dl[h] * decay[h] for h in hs]
        dqk = [dp[h] * decay[h] for h in hs]
        dkb = [_dotb(dkk[h], k[h], NN) + dkg[h] * gamma[h] for h in hs]
        dk1 = [_dotb(dkk[h], kb[h], TN) for h in hs]
        dk2 = [_dotb(dqk[h], q[h], TN) for h in hs]
        dq1 = [_dotb(dqk[h], k[h], NN) for h in hs]
        out = []
        for h in hs:
            dgl = jnp.sum(jnp.sum(dsn[h] * s[h], axis=1, keepdims=True), axis=0, keepdims=True)
            ds_prev = gl[h] * dsn[h] + qdo[h] - wdv[h]
            dk = dk1[h] + dk2[h] + dkd[h] * kdec[h] + dkb[h] * bb[h]
            dq = dq1[h] + dqd[h] * gamma[h]
            dbeta = jnp.sum(dvb[h] * v[h], axis=-1, keepdims=True) + jnp.sum(dkb[h] * k[h], axis=-1, keepdims=True)
            e = dl[h] * lo[h]["lmat"] + dp[h] * pmat[h]
            e_col = jnp.sum(e, axis=0, keepdims=True)
            dgc = jnp.sum(e, axis=1, keepdims=True) - jnp.sum(jnp.where(eye, e_col, 0.0), axis=1, keepdims=True)
            dgamma = (jnp.sum(dqd[h] * q[h], axis=-1, keepdims=True)
                      + jnp.sum(dkg[h] * kb[h], axis=-1, keepdims=True))
            rk = jnp.sum(dkd[h] * k[h], axis=-1, keepdims=True) * kdec[h]
            dgcl = jnp.sum(rk, axis=0, keepdims=True) + dgl * gl[h]
            dgc = dgc + dgamma * gamma[h] - rk + jnp.where(rowi == cs - 1, dgcl, 0.0)
            dgc_row = jnp.sum(jnp.where(eye, dgc, 0.0), axis=0, keepdims=True)
            dg = jnp.sum(jnp.where(c >= r, dgc_row, 0.0), axis=1, keepdims=True)
            out.append((dq, dk, dvb[h] * bb[h], dbeta, dg, ds_prev))
        return out

    def body(q_ref, k_ref, v_ref, gb_ref, st_ref, ti_ref, do_ref, d_ref, dgb_ref, ds_ref):
        @pl.when(pl.program_id(1) == 0)
        def _():
            ds_ref[...] = jnp.zeros_like(ds_ref)

        sls = [slice(i * hd, (i + 1) * hd) for i in range(hb)]
        hs = range(hb)
        gbs, bbs = _head_columns(gb_ref[...], cs)
        outs = heads_bwd([q_ref[:, sl] for sl in sls], [k_ref[:, sl] for sl in sls], [v_ref[:, sl] for sl in sls],
                         gbs, bbs, [st_ref[i, 0] for i in hs],
                         [ti_ref[i, 0] for i in hs], [ds_ref[i] for i in hs], [do_ref[:, sl] for sl in sls])
        lane = lax.broadcasted_iota(jnp.int32, (cs, LANES), 1)
        dgb = jnp.zeros((cs, LANES), F32)
        for i, sl in enumerate(sls):
            dq, dk, dv, dbeta, dg, ds_prev = outs[i]
            d_ref[0, :, sl], d_ref[1, :, sl], d_ref[2, :, sl] = dq, dk, dv
            dgb = jnp.where(lane == i, dg, jnp.where(lane == nh + i, dbeta, dgb))
            ds_ref[i] = ds_prev
        dgb_ref[...] = dgb

    col = lambda off: pl.BlockSpec((cs, hb * hd), lambda h, n: (nc - 1 - n, off + h))
    gspec = pl.BlockSpec((cs, LANES), lambda h, n: (nc - 1 - n, 0))
    return pl.pallas_call(
        body, name=name, grid=(ng, nc),
        in_specs=[col(0), col(ng), col(2 * ng), gspec,
                  pl.BlockSpec((hb, 1, hd, hd), lambda h, n: (h, nc - 1 - n, 0, 0)),
                  pl.BlockSpec((hb, 1, cs, cs), lambda h, n: (h, nc - 1 - n, 0, 0)), col(0)],
        out_specs=(pl.BlockSpec((3, cs, hb * hd), lambda h, n: (0, nc - 1 - n, h)), gspec),
        out_shape=(jax.ShapeDtypeStruct((3, t, nh * hd), F32), jax.ShapeDtypeStruct((t, LANES), F32)),
        scratch_shapes=[pltpu.VMEM((hb, hd, hd), F32)],
        compiler_params=_params("parallel", "arbitrary"),
    )(qkv, qkv, qkv, gbeta, states, tinvs, do)


def _gdn_onorm_fwd(o, proj, norm_g, *, name):
    t = o.shape[0]
    w = GDN_KEY_DIM
    goff = 3 * GDN_KEY_DIM // w

    def body(o_ref, gp_ref, g_ref, y_ref):
        gv = g_ref[...]
        for h in range(GDN_HEADS):
            sl = slice(h * GDN_HEAD_DIM, (h + 1) * GDN_HEAD_DIM)
            oh = o_ref[:, sl]
            gp = gp_ref[:, sl]
            r = lax.rsqrt(jnp.mean(oh * oh, axis=-1, keepdims=True) + EPS)
            y_ref[:, sl] = (oh * r * gv * gp * _sigmoid(gp)).astype(y_ref.dtype)

    return pl.pallas_call(
        body, name=name, grid=(t // ROWS,),
        in_specs=[_row_spec(w), pl.BlockSpec((ROWS, w), lambda i: (i, goff)), _const_spec((1, GDN_HEAD_DIM))],
        out_specs=_row_spec(w), out_shape=jax.ShapeDtypeStruct((t, w), BF16),
        compiler_params=_params("parallel"),
    )(o, proj, norm_g)


def _gdn_onorm_bwd(o, proj, norm_g, dy, *, name):
    t = o.shape[0]
    w = GDN_KEY_DIM
    goff = 3 * GDN_KEY_DIM // w

    def body(o_ref, gp_ref, g_ref, dy_ref, do_ref, dgp_ref, st_ref):
        @pl.when(pl.program_id(0) == 0)
        def _():
            st_ref[...] = jnp.zeros_like(st_ref)

        gv = g_ref[...]
        acc = jnp.zeros((1, GDN_HEAD_DIM), F32)
        for h in range(GDN_HEADS):
            sl = slice(h * GDN_HEAD_DIM, (h + 1) * GDN_HEAD_DIM)
            oh = o_ref[:, sl]
            gp = gp_ref[:, sl]
            dyv = dy_ref[:, sl].astype(F32)
            r = lax.rsqrt(jnp.mean(oh * oh, axis=-1, keepdims=True) + EPS)
            xh = oh * r
            sg = _sigmoid(gp)
            dn = dyv * gp * sg
            dgp_ref[:, sl] = (dyv * xh * gv * sg * (1.0 + gp * (1.0 - sg))).astype(dgp_ref.dtype)
            acc = acc + jnp.sum(dn * xh, axis=0, keepdims=True)
            dxh = dn * gv
            do_ref[:, sl] = r * (dxh - xh * jnp.mean(dxh * xh, axis=-1, keepdims=True))
        st_ref[0:1, :] += acc

    return pl.pallas_call(
        body, name=name, grid=(t // ROWS,),
        in_specs=[_row_spec(w), pl.BlockSpec((ROWS, w), lambda i: (i, goff)), _const_spec((1, GDN_HEAD_DIM)),
                  _row_spec(w)],
        out_specs=(_row_spec(w), _row_spec(w), _const_spec((8, GDN_HEAD_DIM))),
        out_shape=(jax.ShapeDtypeStruct((t, w), F32), jax.ShapeDtypeStruct((t, w), BF16),
                   jax.ShapeDtypeStruct((8, GDN_HEAD_DIM), F32)),
        compiler_params=_params("arbitrary"),
    )(o, proj, norm_g, dy)


def _mla_prep_fwd(proj, qg, kvg, *, name):
    t = proj.shape[0]
    q1, k1 = MLA_Q_RANK, MLA_Q_RANK + MLA_KV_RANK

    def body(p_ref, qg_ref, kg_ref, cq_ref, ck_ref):
        cq = p_ref[:, 0:q1]
        ck = p_ref[:, q1:k1]
        cq_ref[...] = (cq * lax.rsqrt(jnp.mean(cq * cq, axis=-1, keepdims=True) + EPS) * qg_ref[...]).astype(BF16)
        ck_ref[...] = (ck * lax.rsqrt(jnp.mean(ck * ck, axis=-1, keepdims=True) + EPS) * kg_ref[...]).astype(BF16)

    return pl.pallas_call(
        body, name=name, grid=(t // ROWS,),
        in_specs=[_row_spec(MLA_IN), _const_spec((1, MLA_Q_RANK)), _const_spec((1, MLA_KV_RANK))],
        out_specs=(_row_spec(MLA_Q_RANK), _row_spec(MLA_KV_RANK)),
        out_shape=(jax.ShapeDtypeStruct((t, MLA_Q_RANK), BF16), jax.ShapeDtypeStruct((t, MLA_KV_RANK), BF16)),
        compiler_params=_params("parallel"),
    )(proj, qg, kvg)


def _mla_prep_bwd(proj, qg, kvg, dcq, dck, dkr, *, name):
    t = proj.shape[0]
    q1, k1 = MLA_Q_RANK, MLA_Q_RANK + MLA_KV_RANK

    def body(p_ref, qg_ref, kg_ref, dq_ref, dk_ref, dr_ref, dp_ref, st_ref):
        @pl.when(pl.program_id(0) == 0)
        def _():
            st_ref[...] = jnp.zeros_like(st_ref)

        for lo, hi, g_ref, d_ref in ((0, q1, qg_ref, dq_ref), (q1, k1, kg_ref, dk_ref)):
            xv = p_ref[:, lo:hi]
            dn = d_ref[...]
            r = lax.rsqrt(jnp.mean(xv * xv, axis=-1, keepdims=True) + EPS)
            xh = xv * r
            dxh = dn * g_ref[...]
            dp_ref[:, lo:hi] = (r * (dxh - xh * jnp.mean(dxh * xh, axis=-1, keepdims=True))).astype(dp_ref.dtype)
            st_ref[0:1, lo:hi] += jnp.sum(dn * xh, axis=0, keepdims=True)
        dp_ref[:, k1:MLA_IN] = dr_ref[:, 0:MLA_ROPE].astype(dp_ref.dtype)

    return pl.pallas_call(
        body, name=name, grid=(t // ROWS,),
        in_specs=[_row_spec(MLA_IN), _const_spec((1, MLA_Q_RANK)), _const_spec((1, MLA_KV_RANK)),
                  _row_spec(MLA_Q_RANK), _row_spec(MLA_KV_RANK), _row_spec(LANES)],
        out_specs=(_row_spec(MLA_IN), _const_spec((8, MLA_IN))),
        out_shape=(jax.ShapeDtypeStruct((t, MLA_IN), BF16), jax.ShapeDtypeStruct((8, MLA_IN), F32)),
        compiler_params=_params("arbitrary"),
    )(proj, qg, kvg, dcq, dck, dkr)


ATT_BLOCK = 256
ATT_HEAD_BATCH = 8
ATT_HEAD_BATCH_BWD = 4
ATT_SCALE = MLA_QK ** -0.5


def _diagonal_mask(blk):
    return lax.broadcasted_iota(jnp.int32, (blk, blk), 1) <= lax.broadcasted_iota(jnp.int32, (blk, blk), 0)


def _swap_halves(xv, first):
    return jnp.where(first, pltpu.roll(xv, LANES - MLA_ROPE // 2, 1), pltpu.roll(xv, MLA_ROPE // 2, 1))


def _rope_qk(qf, proj, cos_t, sin_t, *, name):
    t = qf.shape[0]
    nrope = MLA_HEADS * MLA_ROPE
    q_blk = MLA_HEADS * MLA_NOPE // nrope
    k_blk = (MLA_Q_RANK + MLA_KV_RANK) // LANES

    def body(q_ref, p_ref, c_ref, s_ref, qo_ref, ko_ref):
        cv, sv = c_ref[...], s_ref[...]
        lane = lax.broadcasted_iota(jnp.int32, (ROWS, LANES), 1)
        first = (lane % MLA_ROPE) < (MLA_ROPE // 2)
        for i in range(nrope // LANES):
            sl = slice(i * LANES, (i + 1) * LANES)
            xv = q_ref[:, sl].astype(F32)
            qo_ref[:, sl] = (xv * cv + _swap_halves(xv, first) * sv).astype(qo_ref.dtype)
        kv = jnp.where(lane < MLA_ROPE, p_ref[...], 0.0)
        ko_ref[...] = (kv * cv + _swap_halves(kv, first) * sv).astype(ko_ref.dtype)

    return pl.pallas_call(
        body, name=name, grid=(t // ROWS,),
        in_specs=[pl.BlockSpec((ROWS, nrope), lambda i: (i, q_blk)), pl.BlockSpec((ROWS, LANES), lambda i: (i, k_blk)),
                  _row_spec(LANES), _row_spec(LANES)],
        out_specs=(_row_spec(nrope), _row_spec(LANES)),
        out_shape=(jax.ShapeDtypeStruct((t, nrope), BF16), jax.ShapeDtypeStruct((t, LANES), BF16)),
        compiler_params=_params("parallel"),
    )(qf, proj, cos_t, sin_t)


def _rope_qk_bwd(dqr, dkr_parts, cos_t, sin_t, *, name):
    t, nrope = dqr.shape
    ng = dkr_parts.shape[0]

    def body(d_ref, k_ref, c_ref, s_ref, qo_ref, ko_ref):
        cv, sv = c_ref[...], s_ref[...]
        lane = lax.broadcasted_iota(jnp.int32, (ROWS, LANES), 1)
        first = (lane % MLA_ROPE) < (MLA_ROPE // 2)
        for i in range(nrope // LANES):
            sl = slice(i * LANES, (i + 1) * LANES)
            dv = d_ref[:, sl]
            qo_ref[:, sl] = (dv * cv + _swap_halves(dv * sv, first)).astype(qo_ref.dtype)
        dk = k_ref[0]
        for g in range(1, ng):
            dk = dk + k_ref[g]
        dk = jnp.where(lane < MLA_ROPE, dk, 0.0)
        ko_ref[...] = jnp.where(lane < MLA_ROPE, dk * cv + _swap_halves(dk * sv, first), 0.0)

    return pl.pallas_call(
        body, name=name, grid=(t // ROWS,),
        in_specs=[_row_spec(nrope), pl.BlockSpec((ng, ROWS, LANES), lambda i: (0, i, 0)), _row_spec(LANES),
                  _row_spec(LANES)],
        out_specs=(_row_spec(nrope), _row_spec(LANES)),
        out_shape=(jax.ShapeDtypeStruct((t, nrope), BF16), jax.ShapeDtypeStruct((t, LANES), F32)),
        compiler_params=_params("parallel"),
    )(dqr, dkr_parts, cos_t, sin_t)


def _attn_tm_fwd(qf, qr, kvf, kr, *, name):
    t = qf.shape[0]
    nh, dn, dr, dv = MLA_HEADS, MLA_NOPE, MLA_ROPE, MLA_V
    blk = min(ATT_BLOCK, t)
    hb = ATT_HEAD_BATCH
    hs = range(hb)

    def body(q_ref, qr_ref, kv_ref, kr_ref, o_ref, l_ref):
        i = pl.program_id(1)
        qc = [jnp.concatenate([q_ref[:, h * dn:(h + 1) * dn].astype(MXU_DTYPE), qr_ref[:, h * dr:(h + 1) * dr]], axis=1)
              for h in hs]

        def step(j, carry, diagonal=False):
            m, l, acc = carry[:hb], carry[hb:2 * hb], carry[2 * hb:]
            rows = pl.ds(pl.multiple_of(j * blk, blk), blk)
            krj = kr_ref[rows, 0:dr]
            s = [_dotb(qc[h], jnp.concatenate([kv_ref[rows, h * (dn + dv):h * (dn + dv) + dn], krj], axis=1), NT)
                 for h in hs]
            s = [s[h] * ATT_SCALE for h in hs]
            if diagonal:
                mask = _diagonal_mask(blk)
                s = [jnp.where(mask, s[h], NEG) for h in hs]
            m_new = [jnp.maximum(m[h], jnp.max(s[h], axis=-1, keepdims=True)) for h in hs]
            p = [jnp.exp(s[h] - m_new[h]) for h in hs]
            pv = [_dotb(p[h], kv_ref[rows, h * (dn + dv) + dn:(h + 1) * (dn + dv)], NN) for h in hs]
            alpha = [jnp.exp(m[h] - m_new[h]) for h in hs]
            l = [alpha[h] * l[h] + jnp.sum(p[h], axis=-1, keepdims=True) for h in hs]
            acc = [alpha[h] * acc[h] + pv[h] for h in hs]
            return tuple(m_new) + tuple(l) + tuple(acc)

        init = ((jnp.full((blk, 1), NEG, F32),) * hb + (jnp.zeros((blk, 1), F32),) * hb
                + (jnp.zeros((blk, dv), F32),) * hb)
        out = step(i, lax.fori_loop(0, i, step, init), diagonal=True)
        for h in hs:
            m, l, acc = out[h], out[hb + h], out[2 * hb + h]
            o_ref[:, h * dv:(h + 1) * dv] = (acc / l).astype(o_ref.dtype)
            l_ref[h] = jnp.broadcast_to(m + jnp.log(l), (blk, LANES))

    return pl.pallas_call(
        body, name=name, grid=(nh // hb, t // blk),
        in_specs=[pl.BlockSpec((blk, hb * dn), lambda g, i: (i, g)), pl.BlockSpec((blk, hb * dr), lambda g, i: (i, g)),
                  pl.BlockSpec((t, hb * (dn + dv)), lambda g, i: (0, g)), pl.BlockSpec((t, LANES), lambda g, i: (0, 0))],
        out_specs=(pl.BlockSpec((blk, hb * dv), lambda g, i: (i, g)),
                   pl.BlockSpec((hb, blk, LANES), lambda g, i: (g, i, 0))),
        out_shape=(jax.ShapeDtypeStruct((t, nh * dv), BF16), jax.ShapeDtypeStruct((nh, t, LANES), F32)),
        compiler_params=_params("parallel", "parallel"),
    )(qf, qr, kvf, kr)


def _attn_tm_bwd(qf, qr, kvf, kr, o, lse, do, *, name):
    t = qf.shape[0]
    nh, dn, dr, dv = MLA_HEADS, MLA_NOPE, MLA_ROPE, MLA_V
    blk = min(ATT_BLOCK, t)
    nb = t // blk
    hb = ATT_HEAD_BATCH_BWD
    hs = range(hb)
    ng = nh // hb

    def body(q_ref, qr_ref, kv_ref, kr_ref, o_ref, l_ref, do_ref, dqn_ref, dqr_ref, dkv_ref, dkr_ref):
        j = pl.program_id(1)

        @pl.when(j == 0)
        def _():
            dqn_ref[...] = jnp.zeros_like(dqn_ref)
            dqr_ref[...] = jnp.zeros_like(dqr_ref)

        krj = kr_ref[:, 0:dr]
        kc = [jnp.concatenate([kv_ref[:, h * (dn + dv):h * (dn + dv) + dn], krj], axis=1) for h in hs]
        vv = [kv_ref[:, h * (dn + dv) + dn:(h + 1) * (dn + dv)] for h in hs]

        def step(i, carry, diagonal=False):
            dkn_acc, dv_acc, dkr_acc = carry[:hb], carry[hb:2 * hb], carry[2 * hb]
            rows = pl.ds(pl.multiple_of(i * blk, blk), blk)
            qc = [jnp.concatenate([q_ref[rows, h * dn:(h + 1) * dn].astype(MXU_DTYPE),
                                   qr_ref[rows, h * dr:(h + 1) * dr]], axis=1) for h in hs]
            dov = [do_ref[rows, h * dv:(h + 1) * dv] for h in hs]
            s = [_dotb(qc[h], kc[h], NT) for h in hs]
            dp = [_dotb(dov[h], vv[h], NT) for h in hs]
            s = [s[h] * ATT_SCALE for h in hs]
            if diagonal:
                mask = _diagonal_mask(blk)
                s = [jnp.where(mask, s[h], NEG) for h in hs]
            p = [jnp.exp(s[h] - l_ref[h, rows, :][:, 0:1]) for h in hs]
            delta = [jnp.sum(dov[h].astype(F32) * o_ref[rows, h * dv:(h + 1) * dv].astype(F32), axis=-1, keepdims=True)
                     for h in hs]
            ds = [p[h] * (dp[h] - delta[h]) * ATT_SCALE for h in hs]
            dvn = [_dotb(p[h], dov[h], TN) for h in hs]
            dkc = [_dotb(ds[h], qc[h], TN) for h in hs]
            dqc = [_dotb(ds[h], kc[h], NN) for h in hs]
            for h in hs:
                dqn_ref[rows, h * dn:(h + 1) * dn] += dqc[h][:, 0:dn]
                dqr_ref[rows, h * dr:(h + 1) * dr] += dqc[h][:, dn:dn + dr]
            dkr_new = dkr_acc
            for h in hs:
                dkr_new = dkr_new + dkc[h][:, dn:dn + dr]
            return (tuple(dkn_acc[h] + dkc[h][:, 0:dn] for h in hs) + tuple(dv_acc[h] + dvn[h] for h in hs)
                    + (dkr_new,))

        init = (jnp.zeros((blk, dn), F32),) * hb + (jnp.zeros((blk, dv), F32),) * hb + (jnp.zeros((blk, dr), F32),)
        out = lax.fori_loop(j + 1, nb, step, step(j, init, diagonal=True))
        for h in hs:
            dkv_ref[:, h * (dn + dv):h * (dn + dv) + dn] = out[h].astype(dkv_ref.dtype)
            dkv_ref[:, h * (dn + dv) + dn:(h + 1) * (dn + dv)] = out[hb + h].astype(dkv_ref.dtype)
        dkr_ref[0, :, 0:dr] = out[2 * hb]
        dkr_ref[0, :, dr:LANES] = jnp.zeros((blk, LANES - dr), F32)

    full = lambda w: pl.BlockSpec((t, w), lambda g, j: (0, g))
    return pl.pallas_call(
        body, name=name, grid=(ng, nb),
        in_specs=[full(hb * dn), full(hb * dr), pl.BlockSpec((blk, hb * (dn + dv)), lambda g, j: (j, g)),
                  pl.BlockSpec((blk, LANES), lambda g, j: (j, 0)), full(hb * dv),
                  pl.BlockSpec((hb, t, LANES), lambda g, j: (g, 0, 0)), full(hb * dv)],
        out_specs=(full(hb * dn), full(hb * dr), pl.BlockSpec((blk, hb * (dn + dv)), lambda g, j: (j, g)),
                   pl.BlockSpec((1, blk, LANES), lambda g, j: (g, j, 0))),
        out_shape=(jax.ShapeDtypeStruct((t, nh * dn), F32), jax.ShapeDtypeStruct((t, nh * dr), F32),
                   jax.ShapeDtypeStruct((t, nh * (dn + dv)), BF16), jax.ShapeDtypeStruct((ng, t, LANES), F32)),
        compiler_params=_params("parallel", "arbitrary"),
    )(qf, qr, kvf, kr, o, lse, do)


def _ada_mod(c_all, ada_w, ada_b_cols, *, name):
    nl, d, wc = ada_w.shape

    def body(c_ref, w_ref, b_ref, o_ref):
        cv = c_ref[...]
        o_ref[0] = _dotb(cv * _sigmoid(cv), w_ref[0], NN) + b_ref[0]

    return pl.pallas_call(
        body, name=name, grid=(nl,),
        in_specs=[_const_spec((N_DEV, d)), pl.BlockSpec((1, d, wc), lambda l: (l, 0, 0)),
                  pl.BlockSpec((1, 1, wc), lambda l: (l, 0, 0))],
        out_specs=pl.BlockSpec((1, N_DEV, wc), lambda l: (l, 0, 0)),
        out_shape=jax.ShapeDtypeStruct((nl, N_DEV, wc), F32), compiler_params=_params("parallel"),
    )(c_all, ada_w, ada_b_cols)


def _adam_math(g, w, m, v):
    m2 = ADAM_B1 * m + (1.0 - ADAM_B1) * g
    v2 = ADAM_B2 * v + (1.0 - ADAM_B2) * (g * g)
    delta = -ADAM_LR * ((m2 / ADAM_BC1) / (jnp.sqrt(v2 / ADAM_BC2) + ADAM_EPS) + ADAM_WD * w)
    return delta, m2, v2


def _ada_grad_adamw(c_all, dmod_cols, w, m, v, *, name):
    nl, d, wc = w.shape
    tr = 256

    def body(c_ref, dm_ref, w_ref, m_ref, v_ref, g_ref, d_ref, m2_ref, v2_ref):
        cv = c_ref[...]
        g = _dotf(cv * _sigmoid(cv), dm_ref[0], TN)
        delta, m2, v2 = _adam_math(g, w_ref[0], m_ref[0], v_ref[0])
        g_ref[0], d_ref[0], m2_ref[0], v2_ref[0] = g, delta, m2, v2

    blk = pl.BlockSpec((1, tr, wc), lambda l, i: (l, i, 0))
    return pl.pallas_call(
        body, name=name, grid=(nl, d // tr),
        in_specs=[pl.BlockSpec((N_DEV, tr), lambda l, i: (0, i)), pl.BlockSpec((1, N_DEV, wc), lambda l, i: (l, 0, 0)),
                  blk, blk, blk],
        out_specs=(blk,) * 4, out_shape=(jax.ShapeDtypeStruct(w.shape, F32),) * 4,
        compiler_params=_params("parallel", "parallel"),
    )(c_all, dmod_cols, w, m, v)


def _adamw(parts, w, m, v, *, name):
    nl, r, c = w.shape
    ns = parts[0].shape[0]
    lanes_padded = -(-c // LANES) * LANES
    row_bytes = 2 * nl * ns * lanes_padded * parts[0].dtype.itemsize
    tr = _pick(r, min(256, max(16, (VMEM_LIMIT // 2) // row_bytes)), 16)
    tc = c
    if tr * row_bytes > VMEM_LIMIT // 2:
        tc = _pick(c, max(LANES, c * (VMEM_LIMIT // 2) // (tr * row_bytes)))

    def body(*refs):
        p_refs = refs[:nl]
        w_ref, m_ref, v_ref, g_ref, d_ref, m2_ref, v2_ref = refs[nl:]
        layer = pl.program_id(0)
        for q in range(nl):
            @pl.when(layer == q)
            def _(q=q):
                g = p_refs[q][0].astype(F32)
                for s in range(1, ns):
                    g = g + p_refs[q][s].astype(F32)
                delta, m2, v2 = _adam_math(g, w_ref[0], m_ref[0], v_ref[0])
                g_ref[0], d_ref[0], m2_ref[0], v2_ref[0] = g, delta, m2, v2

    blk = pl.BlockSpec((1, tr, tc), lambda l, i, j: (l, i, j))
    p_specs = [pl.BlockSpec((ns, tr, tc), lambda l, i, j, q=q: (0, jnp.where(l == q, i, 0), jnp.where(l == q, j, 0)))
               for q in range(nl)]
    return pl.pallas_call(
        body, name=name, grid=(nl, r // tr, c // tc),
        in_specs=p_specs + [blk, blk, blk],
        out_specs=(blk,) * 4, out_shape=(jax.ShapeDtypeStruct(w.shape, F32),) * 4,
        compiler_params=_params("arbitrary", "arbitrary", "arbitrary"),
    )(*parts, w, m, v)


def _sum_parts(parts, *, name):
    ns, r, c = parts.shape

    def body(p_ref, o_ref):
        acc = p_ref[0]
        for s in range(1, ns):
            acc = acc + p_ref[s]
        o_ref[...] = acc

    return pl.pallas_call(
        body, name=name, out_shape=jax.ShapeDtypeStruct((r, c), F32),
        in_specs=[pl.BlockSpec(memory_space=pltpu.VMEM)], out_specs=pl.BlockSpec(memory_space=pltpu.VMEM),
    )(parts)


def _pack(arrs):
    flat = jnp.concatenate([a.reshape(-1).astype(F32) for a in arrs])
    pad = (-flat.shape[0]) % (8 * LANES)
    return jnp.pad(flat, (0, pad)).reshape(-1, LANES)


def _unpack(packed, shapes, lead=()):
    flat = packed.reshape(lead + (-1,))
    out, off = [], 0
    for s in shapes:
        n = math.prod(s)
        out.append(flat[..., off:off + n].reshape(lead + tuple(s)))
        off += n
    return out


def _gather_rows(g):
    _, nl, rs, c = g.shape
    return jnp.transpose(g, (1, 0, 2, 3)).reshape(nl, N_DEV * rs, c)


def _row(v):
    return v.reshape(1, -1)


def _local_step(x, target, mod, cos_t, sin_t, rep, get_weights, put_grads):
    t = x.shape[0]
    saved = []
    for layer in range(DEPTH):
        j = layer // 2
        tag = f"l{layer}"
        shift_m, scale_m, gate_m, shift_f, scale_f, gate_f = [_row(mod[layer, i]) for i in range(N_MOD)]
        lw = dict(get_weights(layer, "mix", x))
        rec = {"x0": x, "lw": lw}
        h = _adaln_fwd(x, _row(rep["norm_mix_g"][layer]), scale_m, shift_m, name=f"adaln_mix_{tag}")
        rec["h"] = h
        if layer % 2 == 0:
            proj = _mm(h, lw["wt_in"], mode="nt", out_dtype=F32, tm=256, tn=GDN_MAIN, b_rows=GDN_MAIN,
                       dep=lw["dep_mix"], name=f"gdn_in_{tag}")
            ab = _mm(h, lw["wt_ab"], mode="nt", out_dtype=F32, name=f"gdn_in_ab_{tag}")
            qkv = _gdn_prep_fwd(proj, rep["gdn_conv_wt"][j], name=f"gdn_prep_{tag}")
            gbeta = _gdn_gate_fwd(ab, rep["gdn_gate_prm"][j], name=f"gdn_gate_{tag}")
            o, states, tinvs = _gdn_chunk_fwd(qkv, gbeta, name=f"gdn_chunk_{tag}")
            og = _gdn_onorm_fwd(o, proj, _row(rep["gdn_norm_g"][j]), name=f"gdn_onorm_{tag}")
            x, y = _mm_resid(og, lw["w_out"], x, gate_m, name=f"gdn_out_{tag}")
            rec.update(proj=proj, ab=ab, qkv=qkv, gbeta=gbeta, states=states, tinvs=tinvs, o=o, og=og, y=y)
        else:
            proj = _mm(h, lw["w_in"], mode="nn", out_dtype=F32, dep=lw["dep_mix"], name=f"mla_in_{tag}")
            cq, ck = _mla_prep_fwd(proj, _row(rep["mla_q_norm_g"][j]), _row(rep["mla_kv_norm_g"][j]),
                                   name=f"mla_prep_{tag}")
            qf = _mm(cq, lw["wt_uq"], mode="nt", out_dtype=BF16, name=f"mla_uq_{tag}")
            kvf = _mm(ck, lw["w_ukv"], mode="nn", out_dtype=BF16, name=f"mla_ukv_{tag}")
            qr, kr = _rope_qk(qf, proj, cos_t, sin_t, name=f"rope_{tag}")
            oc, lse = _attn_tm_fwd(qf, qr, kvf, kr, name=f"attn_{tag}")
            x, y = _mm_resid(oc, lw["w_out"], x, gate_m, name=f"mla_out_{tag}")
            rec.update(proj=proj, cq=cq, ck=ck, qf=qf, qr=qr, kvf=kvf, kr=kr, lse=lse, oc=oc, y=y)
        rec["x1"] = x
        lw.update(get_weights(layer, "ffn", x))
        h2 = _adaln_fwd(x, _row(rep["norm_ffn_g"][layer]), scale_f, shift_f, name=f"adaln_ffn_{tag}")
        s, a2, b2 = _ffn_gu_fwd(h2, lw["wt_g"], lw["wt_u"], lw["dep_ffn"], name=f"ffn_gu_{tag}")
        x, y2 = _mm_resid(s, lw["w_down"], x, gate_f, tm=512, name=f"ffn_down_{tag}")
        rec.update(h2=h2, a2=a2, b2=b2, s=s, y2=y2)
        saved.append(rec)

    dx, st, ls = _loss_head(x, _row(rep["final_norm_g"]), target, name="loss_head")
    loss = ls[0, 0]
    grads = {"final_norm_g": st[0]}
    per_layer = {k: [None] * DEPTH for k in ("norm_mix_g", "norm_ffn_g")}
    per_gdn = {k: [None] * 2 for k in ("gdn_conv_wt", "gdn_a_log", "gdn_dt_bias", "gdn_norm_g")}
    per_mla = {k: [None] * 2 for k in ("mla_q_norm_g", "mla_kv_norm_g")}
    dmod = [None] * DEPTH
    dep = jnp.zeros((8, LANES), F32)

    for layer in reversed(range(DEPTH)):
        j = layer // 2
        tag = f"l{layer}"
        rec = saved[layer]
        lw = rec["lw"]
        shift_m, scale_m, gate_m, shift_f, scale_f, gate_f = [_row(mod[layer, i]) for i in range(N_MOD)]
        if layer == DEPTH - 1:
            dy2, st_g = _gate_bwd(dx, rec["y2"], gate_f, dep, name=f"gate_bwd_ffn_{tag}")
            dgate_f = st_g[0]
        dw_down = _mm(rec["s"], dy2, mode="tn", out_dtype=BF16, tm=FFN_BLOCK, tn=1024, name=f"ffn_down_dw_{tag}")
        da2, db2 = _ffn_down_dx(dy2, lw["w_down"], rec["a2"], rec["b2"], name=f"ffn_down_dx_{tag}")
        dwt_g = _mm(da2, rec["h2"], mode="tn", out_dtype=BF16, tm=FFN_BLOCK, tn=1024, name=f"ffn_g_dw_{tag}")
        dwt_u = _mm(db2, rec["h2"], mode="tn", out_dtype=BF16, tm=FFN_BLOCK, tn=1024, name=f"ffn_u_dw_{tag}")
        dep = put_grads(layer, "ffn", {"wt_g": dwt_g, "wt_u": dwt_u, "w_down": dw_down})
        dh2 = _mm(da2, lw["wt_g"], mode="nn", out_dtype=F32, tm=512, tn=1024, name=f"ffn_g_dx_{tag}")
        dh2 = _mm(db2, lw["wt_u"], mode="nn", out_dtype=BF16, add=dh2, tm=512, tn=1024, name=f"ffn_u_dx_{tag}")
        dx, st_n, dy = _adaln_gate_bwd(rec["x1"], _row(rep["norm_ffn_g"][layer]), scale_f, shift_f, dh2, dx, dep,
                                       rec["y"], gate_m, name=f"adaln_ffn_bwd_{tag}")
        per_layer["norm_ffn_g"][layer] = st_n[0]
        dscale_f, dshift_f, dgate_m = st_n[1], st_n[2], st_n[3]
        big = {}
        if layer % 2 == 0:
            big["w_out"] = _mm(rec["og"], dy, mode="tn", out_dtype=BF16, name=f"gdn_out_dw_{tag}")
            dog = _mm(dy, lw["w_out"], mode="nt", out_dtype=BF16, name=f"gdn_out_dx_{tag}")
            do, dgp, st_o = _gdn_onorm_bwd(rec["o"], rec["proj"], _row(rep["gdn_norm_g"][j]), dog,
                                           name=f"gdn_onorm_bwd_{tag}")
            per_gdn["gdn_norm_g"][j] = st_o[0]
            dqkv, dgb = _gdn_chunk_bwd(rec["qkv"], rec["gbeta"], rec["states"], rec["tinvs"], do,
                                       name=f"gdn_chunk_bwd_{tag}")
            dab, st_a = _gdn_gate_bwd(rec["ab"], rep["gdn_gate_prm"][j], dgb, name=f"gdn_gate_bwd_{tag}")
            per_gdn["gdn_a_log"][j] = st_a[0, :GDN_HEADS]
            per_gdn["gdn_dt_bias"][j] = st_a[1, :GDN_HEADS]
            dpre, dcw = _gdn_prep_bwd(rec["proj"], rep["gdn_conv_wt"][j], dqkv, name=f"gdn_prep_bwd_{tag}")
            per_gdn["gdn_conv_wt"][j] = dcw
            dproj = jnp.concatenate([dpre, dgp], axis=1)
            dw_main = _mm(dproj, rec["h"], mode="tn", out_dtype=BF16, tm=512, tn=1024, name=f"gdn_in_dw_{tag}")
            dw_ab = _mm(dab, rec["h"], mode="tn", out_dtype=BF16, tn=1024, name=f"gdn_in_ab_dw_{tag}")
            big["wt_in"] = jnp.concatenate([dw_main, dw_ab[:2 * GDN_HEADS]], axis=0)
            dep = put_grads(layer, "gdn", big)
            dh_ab = _mm(dab, lw["wt_ab"], mode="nn", out_dtype=F32, tn=1024, name=f"gdn_in_ab_dx_{tag}")
            dh = _mm(dproj, lw["wt_in"], mode="nn", out_dtype=BF16, add=dh_ab, tm=256, tn=1024, b_rows=GDN_MAIN,
                     name=f"gdn_in_dx_{tag}")
        else:
            big["w_out"] = _mm(rec["oc"], dy, mode="tn", out_dtype=BF16, name=f"mla_out_dw_{tag}")
            doc = _mm(dy, lw["w_out"], mode="nt", out_dtype=BF16, name=f"mla_out_dx_{tag}")
            dqn, dqr, dkvf, dkr_parts = _attn_tm_bwd(rec["qf"], rec["qr"], rec["kvf"], rec["kr"], rec["oc"],
                                                     rec["lse"], doc, name=f"attn_bwd_{tag}")
            dqr_un, dkr_un = _rope_qk_bwd(dqr, dkr_parts, cos_t, sin_t, name=f"rope_bwd_{tag}")
            n_nope = MLA_HEADS * MLA_NOPE
            big["wt_uq"] = jnp.concatenate(
                [_mm(dqn, rec["cq"], mode="tn", out_dtype=BF16, name=f"mla_uq_dw_nope_{tag}"),
                 _mm(dqr_un, rec["cq"], mode="tn", out_dtype=BF16, name=f"mla_uq_dw_rope_{tag}")], axis=0)
            big["w_ukv"] = _mm(rec["ck"], dkvf, mode="tn", out_dtype=BF16, name=f"mla_ukv_dw_{tag}")
            dcq = _mm(dqr_un, lw["wt_uq"][n_nope:], mode="nn", out_dtype=F32, name=f"mla_uq_dx_rope_{tag}")
            dcq = _mm(dqn, lw["wt_uq"], mode="nn", out_dtype=F32, add=dcq, b_rows=n_nope,
                      name=f"mla_uq_dx_nope_{tag}")
            dck = _mm(dkvf, lw["w_ukv"], mode="nt", out_dtype=F32, name=f"mla_ukv_dx_{tag}")
            dproj, st_p = _mla_prep_bwd(rec["proj"], _row(rep["mla_q_norm_g"][j]), _row(rep["mla_kv_norm_g"][j]),
                                        dcq, dck, dkr_un, name=f"mla_prep_bwd_{tag}")
            per_mla["mla_q_norm_g"][j] = st_p[0, :MLA_Q_RANK]
            per_mla["mla_kv_norm_g"][j] = st_p[0, MLA_Q_RANK:MLA_Q_RANK + MLA_KV_RANK]
            big["w_in"] = _mm(rec["h"], dproj, mode="tn", out_dtype=BF16, name=f"mla_in_dw_{tag}")
            dep = put_grads(layer, "mla", big)
            dh = _mm(dproj, lw["w_in"], mode="nt", out_dtype=BF16, name=f"mla_in_dx_{tag}")
        if layer > 0:
            below = saved[layer - 1]
            dx, st_n, dy2 = _adaln_gate_bwd(rec["x0"], _row(rep["norm_mix_g"][layer]), scale_m, shift_m, dh, dx, dep,
                                            below["y2"], _row(mod[layer - 1, N_MOD - 1]),
                                            name=f"adaln_mix_bwd_{tag}")
        else:
            dx, st_n = _adaln_bwd(rec["x0"], _row(rep["norm_mix_g"][layer]), scale_m, shift_m, dh, dx, dep,
                                  name=f"adaln_mix_bwd_{tag}")
        per_layer["norm_mix_g"][layer] = st_n[0]
        dmod[layer] = jnp.stack([st_n[2], st_n[1], dgate_m, dshift_f, dscale_f, dgate_f])
        if layer > 0:
            dgate_f = st_n[3]

    for d in (per_layer, per_gdn, per_mla):
        for k, v in d.items():
            grads[k] = jnp.stack(v)
    return loss, dx, jnp.stack(dmod), grads


BIG = ("gdn_w_in", "gdn_w_out", "mla_w_in", "mla_w_uq", "mla_w_ukv", "mla_w_out", "ffn_w_gate", "ffn_w_up",
       "ffn_w_down")
TRANSPOSED = ("gdn_w_in", "mla_w_uq", "ffn_w_gate", "ffn_w_up")
AHEAD = 3


def _view(k, a):
    return jnp.transpose(a, (0, 2, 1)) if k in TRANSPOSED else a
SMALL = ("ada_b", "norm_mix_g", "norm_ffn_g", "gdn_conv_w", "gdn_a_log", "gdn_dt_bias", "gdn_norm_g",
         "mla_q_norm_g", "mla_kv_norm_g", "final_norm_g")
WEIGHTS = ("ada_w", "ada_b", "norm_mix_g", "norm_ffn_g", "gdn_w_in", "gdn_conv_w", "gdn_a_log", "gdn_dt_bias",
           "gdn_norm_g", "gdn_w_out", "mla_w_in", "mla_q_norm_g", "mla_kv_norm_g", "mla_w_uq", "mla_w_ukv",
           "mla_w_out", "ffn_w_gate", "ffn_w_up", "ffn_w_down", "final_norm_g")


def _uq_to_kernel_layout(w, axis=-1):
    axis = axis % w.ndim
    lead, tail = w.shape[:axis], w.shape[axis + 1:]
    w4 = w.reshape(lead + (MLA_HEADS, MLA_QK) + tail)
    nope = lax.slice_in_dim(w4, 0, MLA_NOPE, axis=axis + 1).reshape(lead + (-1,) + tail)
    rope = lax.slice_in_dim(w4, MLA_NOPE, MLA_QK, axis=axis + 1).reshape(lead + (-1,) + tail)
    return jnp.concatenate([nope, rope], axis=axis)


def _uq_from_kernel_layout(w, axis=-1):
    axis = axis % w.ndim
    lead, tail = w.shape[:axis], w.shape[axis + 1:]
    nope = lax.slice_in_dim(w, 0, MLA_HEADS * MLA_NOPE, axis=axis).reshape(lead + (MLA_HEADS, MLA_NOPE) + tail)
    rope = lax.slice_in_dim(w, MLA_HEADS * MLA_NOPE, MLA_HEADS * MLA_QK, axis=axis).reshape(
        lead + (MLA_HEADS, MLA_ROPE) + tail)
    return jnp.concatenate([nope, rope], axis=axis + 1).reshape(lead + (-1,) + tail)


def _group_names(layer, kind):
    if kind == "ffn":
        return ("ffn_w_gate", "ffn_w_up", "ffn_w_down")
    return ("gdn_w_in", "gdn_w_out") if layer % 2 == 0 else ("mla_w_in", "mla_w_uq", "mla_w_ukv", "mla_w_out")


def _layer_index(name, layer):
    return layer if name.startswith("ffn") else layer // 2


def _cols(g):
    return jnp.transpose(g, (1, 0, 2)).reshape(g.shape[1], N_DEV * g.shape[2])


def _rows(g):
    return g.reshape(N_DEV * g.shape[1], g.shape[2])


def _uncols(full):
    r, c = full.shape
    return jnp.transpose(full.reshape(r, N_DEV, c // N_DEV), (1, 0, 2))


def _unrows(full):
    r, c = full.shape
    return full.reshape(N_DEV, r // N_DEV, c)


def _group_weights(layer, kind, got, token):
    if kind == "ffn":
        return {"wt_g": _rows(got["ffn_w_gate"]), "wt_u": _rows(got["ffn_w_up"]), "w_down": _rows(got["ffn_w_down"]),
                "dep_ffn": token}
    if layer % 2 == 0:
        wt_in = _rows(got["gdn_w_in"])
        return dict(wt_in=wt_in, wt_ab=jnp.pad(wt_in[GDN_MAIN:], ((0, LANES - 2 * GDN_HEADS), (0, 0))),
                    w_out=_rows(got["gdn_w_out"]), dep_mix=token)
    return dict(w_in=_rows(got["mla_w_in"]), wt_uq=_uq_to_kernel_layout(_rows(got["mla_w_uq"]), axis=0),
                w_ukv=_cols(got["mla_w_ukv"]), w_out=_rows(got["mla_w_out"]), dep_mix=token)


def _layer_grad_slots(kind, big):
    if kind == "ffn":
        return {"ffn_w_gate": _unrows(big["wt_g"]), "ffn_w_up": _unrows(big["wt_u"]),
                "ffn_w_down": _unrows(big["w_down"])}
    if kind == "gdn":
        return {"gdn_w_in": _unrows(big["wt_in"]), "gdn_w_out": _unrows(big["w_out"])}
    return {"mla_w_in": _unrows(big["w_in"]), "mla_w_uq": _unrows(_uq_from_kernel_layout(big["wt_uq"], axis=0)),
            "mla_w_ukv": _uncols(big["w_ukv"]), "mla_w_out": _unrows(big["w_out"])}


def _small_weights(tiny, rep):
    prm = jnp.zeros((2, 8, LANES), F32)
    prm = prm.at[:, 0, :GDN_HEADS].set(rep["gdn_a_log"]).at[:, 1, :GDN_HEADS].set(rep["gdn_dt_bias"])
    out = {
        "gdn_conv_wt": jnp.transpose(_gather_rows(tiny["gdn_conv_w"]), (0, 2, 1)),
        "mla_q_norm_g": jnp.transpose(tiny["mla_q_norm_g"], (1, 0, 2)).reshape(2, MLA_Q_RANK),
        "mla_kv_norm_g": jnp.transpose(tiny["mla_kv_norm_g"], (1, 0, 2)).reshape(2, MLA_KV_RANK),
        "gdn_gate_prm": prm,
    }
    for k in ("norm_mix_g", "norm_ffn_g", "gdn_norm_g", "final_norm_g"):
        out[k] = rep[k]
    return out


def _rope_tables(positions):
    inv_freq = ROPE_THETA ** (-jnp.arange(0, MLA_ROPE, 2, dtype=F32) / MLA_ROPE)
    ang = positions.astype(F32)[:, None] * inv_freq
    cos, sin = jnp.cos(ang), jnp.sin(ang)
    reps = LANES // MLA_ROPE
    return jnp.tile(jnp.concatenate([cos, cos], axis=1), (1, reps)), jnp.tile(
        jnp.concatenate([-sin, sin], axis=1), (1, reps))


def kernel(x, c, positions, ada_w, ada_b, norm_mix_g, norm_ffn_g, gdn_w_in, gdn_conv_w, gdn_a_log, gdn_dt_bias, gdn_norm_g, gdn_w_out, mla_w_in, mla_q_norm_g, mla_kv_norm_g, mla_w_uq, mla_w_ukv, mla_w_out, ffn_w_gate, ffn_w_up, ffn_w_down, final_norm_g, loss_target, m_ada_w, m_ada_b, m_norm_mix_g, m_norm_ffn_g, m_gdn_w_in, m_gdn_conv_w, m_gdn_a_log, m_gdn_dt_bias, m_gdn_norm_g, m_gdn_w_out, m_mla_w_in, m_mla_q_norm_g, m_mla_kv_norm_g, m_mla_w_uq, m_mla_w_ukv, m_mla_w_out, m_ffn_w_gate, m_ffn_w_up, m_ffn_w_down, m_final_norm_g, v_ada_w, v_ada_b, v_norm_mix_g, v_norm_ffn_g, v_gdn_w_in, v_gdn_conv_w, v_gdn_a_log, v_gdn_dt_bias, v_gdn_norm_g, v_gdn_w_out, v_mla_w_in, v_mla_q_norm_g, v_mla_kv_norm_g, v_mla_w_uq, v_mla_w_ukv, v_mla_w_out, v_ffn_w_gate, v_ffn_w_up, v_ffn_w_down, v_final_norm_g):
    W = dict(ada_w=ada_w, ada_b=ada_b, norm_mix_g=norm_mix_g, norm_ffn_g=norm_ffn_g, gdn_w_in=gdn_w_in,
             gdn_conv_w=gdn_conv_w, gdn_a_log=gdn_a_log, gdn_dt_bias=gdn_dt_bias, gdn_norm_g=gdn_norm_g,
             gdn_w_out=gdn_w_out, mla_w_in=mla_w_in, mla_q_norm_g=mla_q_norm_g, mla_kv_norm_g=mla_kv_norm_g,
             mla_w_uq=mla_w_uq, mla_w_ukv=mla_w_ukv, mla_w_out=mla_w_out, ffn_w_gate=ffn_w_gate,
             ffn_w_up=ffn_w_up, ffn_w_down=ffn_w_down, final_norm_g=final_norm_g)
    M = dict(ada_w=m_ada_w, ada_b=m_ada_b, norm_mix_g=m_norm_mix_g, norm_ffn_g=m_norm_ffn_g, gdn_w_in=m_gdn_w_in,
             gdn_conv_w=m_gdn_conv_w, gdn_a_log=m_gdn_a_log, gdn_dt_bias=m_gdn_dt_bias, gdn_norm_g=m_gdn_norm_g,
             gdn_w_out=m_gdn_w_out, mla_w_in=m_mla_w_in, mla_q_norm_g=m_mla_q_norm_g,
             mla_kv_norm_g=m_mla_kv_norm_g, mla_w_uq=m_mla_w_uq, mla_w_ukv=m_mla_w_ukv, mla_w_out=m_mla_w_out,
             ffn_w_gate=m_ffn_w_gate, ffn_w_up=m_ffn_w_up, ffn_w_down=m_ffn_w_down, final_norm_g=m_final_norm_g)
    V = dict(ada_w=v_ada_w, ada_b=v_ada_b, norm_mix_g=v_norm_mix_g, norm_ffn_g=v_norm_ffn_g, gdn_w_in=v_gdn_w_in,
             gdn_conv_w=v_gdn_conv_w, gdn_a_log=v_gdn_a_log, gdn_dt_bias=v_gdn_dt_bias, gdn_norm_g=v_gdn_norm_g,
             gdn_w_out=v_gdn_w_out, mla_w_in=v_mla_w_in, mla_q_norm_g=v_mla_q_norm_g,
             mla_kv_norm_g=v_mla_kv_norm_g, mla_w_uq=v_mla_w_uq, mla_w_ukv=v_mla_w_ukv, mla_w_out=v_mla_w_out,
             ffn_w_gate=v_ffn_w_gate, ffn_w_up=v_ffn_w_up, ffn_w_down=v_ffn_w_down, final_norm_g=v_final_norm_g)
    me = 4 * lax.axis_index("x") + 2 * lax.axis_index("y") + lax.axis_index("c")
    t = x.shape[1]
    wc = ada_w.shape[-1]

    groups = [(layer, kind) for layer in range(DEPTH) for kind in ("mix", "ffn")]

    def group_srcs(i):
        layer, kind = groups[i]
        return [_view(k, W[k])[_layer_index(k, layer)].astype(BF16) for k in _group_names(layer, kind)]

    tiny_shapes = [c.shape, gdn_conv_w.shape, mla_q_norm_g.shape, mla_kv_norm_g.shape]
    first = _gather_two_level([_pack([c, gdn_conv_w, mla_q_norm_g, mla_kv_norm_g])] + group_srcs(0),
                              name="gather_first")
    tiny_g = first[0]
    c_g, conv_g, qn_g, kvn_g = _unpack(tiny_g, tiny_shapes, lead=(N_DEV,))
    c_all = c_g.reshape(N_DEV, D_MODEL)
    rep = _small_weights({"gdn_conv_w": conv_g, "mla_q_norm_g": qn_g, "mla_kv_norm_g": kvn_g}, W)

    def start_group(i, dep):
        layer, kind = groups[i]
        return _exchange_start(group_srcs(i), scatter=False, name=f"gather_start_{kind}_l{layer}", dep=dep)


    b_cols = lax.dynamic_slice_in_dim(ada_b, me * wc, wc, axis=1).reshape(DEPTH, 1, wc)
    mod_part = _ada_mod(c_all, ada_w, b_cols, name="ada_mod")
    (mod_g,) = _exchange([mod_part], scatter=False, name="gather_mod")
    mod_mine = lax.dynamic_index_in_dim(mod_g, me, axis=2, keepdims=False)
    mod = jnp.transpose(mod_mine, (1, 0, 2)).reshape(DEPTH, N_MOD, D_MODEL)
    gather = {1: start_group(1, mod_g)}
    for i in range(2, AHEAD + 1):
        gather[i] = start_group(i, gather[i - 1][4])

    def get_weights(layer, kind, after):
        i = groups.index((layer, kind))
        names = _group_names(layer, kind)
        if i == 0:
            return _group_weights(layer, kind, dict(zip(names, first[1:])), gather[AHEAD][4])
        srcs, lands = _exchange_wait(gather[i], after, scatter=False, name=f"gather_wait_{kind}_l{layer}")
        token = jnp.zeros((8, LANES), F32)
        if i + AHEAD < len(groups):
            gather[i + AHEAD] = start_group(i + AHEAD, lands[0])
            token = gather[i + AHEAD][4]
        got = {k: lax.dynamic_update_index_in_dim(z, s, me, 0) for k, s, z in zip(names, srcs, lands)}
        return _group_weights(layer, kind, got, token)

    scatter = []

    def put_grads(layer, kind, big):
        slots = _layer_grad_slots(kind, big)
        started = _exchange_start(list(slots.values()), scatter=True, name=f"scatter_start_{kind}_l{layer}")
        scatter.append((layer, kind, list(slots.keys()), started))
        return started[4]

    cos_t, sin_t = _rope_tables(positions[0])
    loss, dx, dmod, g = _local_step(x[0], loss_target[0], mod, cos_t, sin_t, rep, get_weights, put_grads)

    parts = {k: [None] * W[k].shape[0] for k in BIG}
    res = {}

    def wait_group(entry, after):
        layer, kind, names, started = entry
        srcs, lands = _exchange_wait(started, after, scatter=True, name=f"scatter_wait_{kind}_l{layer}")
        for k, s, z in zip(names, srcs, lands):
            own = lax.dynamic_index_in_dim(s, me, 0, keepdims=False)
            parts[k][_layer_index(k, layer)] = lax.dynamic_update_index_in_dim(z, own, me, 0)

    for entry in scatter[:-1]:
        wait_group(entry, dx)
    early = [k for k in BIG if k not in scatter[-1][2]]
    def update(k):
        outs = _adamw(parts[k], _view(k, W[k]), _view(k, M[k]), _view(k, V[k]), name=f"adamw_{k}")
        return tuple(_view(k, o) for o in outs)

    for k in early:
        res[k] = update(k)
    loss, dmod, done = lax.optimization_barrier((loss, dmod, [res[k] for k in early]))
    for k, r in zip(early, done):
        res[k] = r

    small_local = [dmod.reshape(DEPTH, N_MOD * D_MODEL), g["norm_mix_g"], g["norm_ffn_g"],
                   jnp.transpose(g["gdn_conv_wt"], (0, 2, 1)), g["gdn_a_log"], g["gdn_dt_bias"], g["gdn_norm_g"],
                   g["mla_q_norm_g"], g["mla_kv_norm_g"], g["final_norm_g"], loss.reshape(1)]
    small_shapes = [a.shape for a in small_local]
    (small_g,) = _exchange([_pack(small_local)], scatter=False, name="gather_small_grads")
    small_sum = _unpack(_sum_parts(small_g, name="sum_small_grads"), small_shapes)
    loss = small_sum[-1][0]
    dmod_all = _unpack(small_g, small_shapes[:1], lead=(N_DEV,))[0]
    sg = dict(zip(SMALL, small_sum))
    wait_group(scatter[-1], small_g)
    sg["gdn_conv_w"] = lax.dynamic_slice_in_dim(sg["gdn_conv_w"], me * gdn_conv_w.shape[1], gdn_conv_w.shape[1], 1)
    sg["mla_q_norm_g"] = lax.dynamic_slice_in_dim(sg["mla_q_norm_g"], me * mla_q_norm_g.shape[1],
                                                  mla_q_norm_g.shape[1], 1)
    sg["mla_kv_norm_g"] = lax.dynamic_slice_in_dim(sg["mla_kv_norm_g"], me * mla_kv_norm_g.shape[1],
                                                   mla_kv_norm_g.shape[1], 1)

    dmod_cols = jnp.transpose(lax.dynamic_slice_in_dim(dmod_all, me * wc, wc, axis=2), (1, 0, 2))
    res["ada_w"] = _ada_grad_adamw(c_all, dmod_cols, ada_w, m_ada_w, v_ada_w, name="ada_w_grad_adamw")
    for k in BIG:
        if k not in early:
            res[k] = update(k)
    shapes = [W[k].shape for k in SMALL]
    packed = [_pack([d[k] for k in SMALL]) for d in (sg, W, M, V)]
    outs = _adamw([packed[0][None]], packed[1][None], packed[2][None], packed[3][None], name="adamw_small")
    unpacked = [_unpack(o[0], shapes) for o in outs]
    for i, k in enumerate(SMALL):
        res[k] = tuple(u[i] for u in unpacked)

    return (loss, dx[None], *[res[k][0] for k in WEIGHTS], *[res[k][1] for k in WEIGHTS],
            *[res[k][2] for k in WEIGHTS], *[res[k][3] for k in WEIGHTS])
```

```python
import math

import jax
import jax.numpy as jnp
from jax import lax
from jax.experimental import pallas as pl
from jax.experimental.pallas import tpu as pltpu

F32 = jnp.float32
BF16 = jnp.bfloat16
MXU_DTYPE = jnp.bfloat16

N_DEV = 8
D_MODEL = 1024
DEPTH = 4
GDN_HEADS = 8
GDN_HEAD_DIM = 128
GDN_KEY_DIM = GDN_HEADS * GDN_HEAD_DIM
GDN_CHUNK = 64
GDN_HEAD_BATCH = 8
GDN_CONV = 4
GDN_PREP_HEADS = 2
GDN_MAIN = 4 * GDN_KEY_DIM
MLA_HEADS = 8
MLA_NOPE = 128
MLA_ROPE = 64
MLA_V = 128
MLA_Q_RANK = 384
MLA_KV_RANK = 256
MLA_IN = MLA_Q_RANK + MLA_KV_RANK + MLA_ROPE
MLA_QK = MLA_NOPE + MLA_ROPE
ROPE_THETA = 10000.0
D_FF = 2816
N_MOD = 6
EPS = 1e-6
LANES = 128
VMEM_LIMIT = 48 * 1024 * 1024

ADAM_LR = 0.001
ADAM_B1 = 0.9
ADAM_B2 = 0.999
ADAM_EPS = 1e-08
ADAM_WD = 0.01
ADAM_STEP = 10
ADAM_BC1 = 1.0 - ADAM_B1 ** ADAM_STEP
ADAM_BC2 = 1.0 - ADAM_B2 ** ADAM_STEP

NN = (((1,), (0,)), ((), ()))
NT = (((1,), (1,)), ((), ()))
TN = (((0,), (0,)), ((), ()))
NEG = -1e30


def _dotb(a, b, dims):
    return lax.dot_general(a.astype(MXU_DTYPE), b.astype(MXU_DTYPE), dims, preferred_element_type=F32)


def _split(a):
    hi = a.astype(BF16)
    return hi, (a - hi.astype(F32)).astype(BF16)


def _dotf(a, b, dims):
    ah, al = _split(a)
    bh, bl = _split(b)
    dot = lambda u, v: lax.dot_general(u, v, dims, preferred_element_type=F32)
    return dot(ah, bh) + (dot(ah, bl) + dot(al, bh))


def _params(*sem):
    return pltpu.CompilerParams(dimension_semantics=sem, vmem_limit_bytes=VMEM_LIMIT)


def _pick(n, pref, mult=LANES):
    best = None
    t = mult
    while t <= min(n, pref):
        if n % t == 0:
            best = t
        t += mult
    return best if best is not None else n


def _sigmoid(z):
    return 0.5 * jnp.tanh(0.5 * z) + 0.5


def _exchange(arrays, *, scatter, name):
    n = len(arrays)
    out_shape = tuple(
        jax.ShapeDtypeStruct(a.shape if scatter else (N_DEV,) + a.shape, a.dtype) for a in arrays)

    def body(*refs):
        ins, outs = refs[:n], refs[n:2 * n]
        send_sems, recv_sems, local_sems = refs[2 * n:]
        x, y, c = lax.axis_index("x"), lax.axis_index("y"), lax.axis_index("c")
        me = 4 * x + 2 * y + c
        copies = []
        for k in range(n):
            src_own = ins[k].at[me] if scatter else ins[k]
            own = pltpu.make_async_copy(src_own, outs[k].at[me], local_sems.at[k])
            own.start()
            copies.append(own)
        sends = []
        for p in range(1, N_DEV):
            px, py, pc = x ^ ((p >> 2) & 1), y ^ ((p >> 1) & 1), c ^ (p & 1)
            peer = 4 * px + 2 * py + pc
            for k in range(n):
                cp = pltpu.make_async_remote_copy(
                    src_ref=ins[k].at[peer] if scatter else ins[k],
                    dst_ref=outs[k].at[me],
                    send_sem=send_sems.at[k, p - 1],
                    recv_sem=recv_sems.at[k, p - 1],
                    device_id=(px, py, pc),
                    device_id_type=pl.DeviceIdType.MESH,
                )
                cp.start()
                sends.append((cp, k, peer, p))
        for cp, k, peer, p in sends:
            pltpu.make_async_remote_copy(
                src_ref=ins[k].at[peer] if scatter else ins[k],
                dst_ref=outs[k].at[peer],
                send_sem=send_sems.at[k, p - 1],
                recv_sem=recv_sems.at[k, p - 1],
                device_id=(x, y, c),
                device_id_type=pl.DeviceIdType.MESH,
            ).wait_recv()
        for cp, _, _, _ in sends:
            cp.wait_send()
        for own in copies:
            own.wait()

    any_spec = pl.BlockSpec(memory_space=pl.ANY)
    outs = pl.pallas_call(
        body,
        name=name,
        out_shape=out_shape,
        in_specs=[any_spec] * n,
        out_specs=tuple([any_spec] * n),
        scratch_shapes=[
            pltpu.SemaphoreType.DMA((n, N_DEV - 1)),
            pltpu.SemaphoreType.DMA((n, N_DEV - 1)),
            pltpu.SemaphoreType.DMA((n,)),
        ],
        compiler_params=pltpu.CompilerParams(has_side_effects=True),
    )(*arrays)
    return list(outs)


def _gather_two_level(arrays, *, name):
    n = len(arrays)
    out_shape = tuple(jax.ShapeDtypeStruct((N_DEV,) + a.shape, a.dtype) for a in arrays)

    def body(*refs):
        ins, outs = refs[:n], refs[n:2 * n]
        send_sems, recv_sems, local_sems = refs[2 * n:]
        x, y, c = lax.axis_index("x"), lax.axis_index("y"), lax.axis_index("c")
        me = 4 * x + 2 * y + c
        sibling = (x, y, 1 - c)
        chips = [(1 - x, y), (x, 1 - y), (1 - x, 1 - y)]

        def slot(px, py, pc):
            return 4 * px + 2 * py + pc

        def copy(k, q, block, to, src=None):
            return pltpu.make_async_remote_copy(
                src_ref=outs[k].at[slot(*block)] if src is None else src,
                dst_ref=outs[k].at[slot(*block)],
                send_sem=send_sems.at[k, q], recv_sem=recv_sems.at[k, q],
                device_id=to, device_id_type=pl.DeviceIdType.MESH)

        own = [pltpu.make_async_copy(ins[k], outs[k].at[me], local_sems.at[k]) for k in range(n)]
        for cp in own:
            cp.start()
        first = []
        for k in range(n):
            first.append(copy(k, 0, (x, y, c), sibling, src=ins[k]))
            first += [copy(k, 1 + j, (x, y, c), (*chip, c), src=ins[k]) for j, chip in enumerate(chips)]
        for cp in first:
            cp.start()
        passed = []
        for j, chip in enumerate(chips):
            for k in range(n):
                copy(k, 1 + j, (*chip, c), (x, y, c)).wait_recv()
                fwd = copy(k, 4 + j, (*chip, c), sibling)
                fwd.start()
                passed.append(fwd)
        for k in range(n):
            copy(k, 0, sibling, (x, y, c)).wait_recv()
            for j, chip in enumerate(chips):
                copy(k, 4 + j, (*chip, 1 - c), (x, y, c)).wait_recv()
        for cp in first + passed:
            cp.wait_send()
        for cp in own:
            cp.wait()

    any_spec = pl.BlockSpec(memory_space=pl.ANY)
    outs = pl.pallas_call(
        body, name=name, out_shape=out_shape, in_specs=[any_spec] * n, out_specs=tuple([any_spec] * n),
        scratch_shapes=[pltpu.SemaphoreType.DMA((n, N_DEV - 1)), pltpu.SemaphoreType.DMA((n, N_DEV - 1)),
                        pltpu.SemaphoreType.DMA((n,))],
        compiler_params=pltpu.CompilerParams(has_side_effects=True),
    )(*arrays)
    return list(outs)


def _peer(x, y, c, p):
    return x ^ ((p >> 2) & 1), y ^ ((p >> 1) & 1), c ^ (p & 1)


def _exchange_start(arrays, *, scatter, name, dep=None):
    n = len(arrays)
    deps = [] if dep is None else [dep]
    lands = [lax.empty(a.shape if scatter else (N_DEV,) + a.shape, a.dtype) for a in arrays]

    def body(*refs):
        ins, zones = refs[:n], refs[n:2 * n]
        send_sems, recv_sems = refs[2 * n + len(deps)], refs[2 * n + len(deps) + 1]
        token = refs[-1]
        x, y, c = lax.axis_index("x"), lax.axis_index("y"), lax.axis_index("c")
        me = 4 * x + 2 * y + c
        for p in range(1, N_DEV):
            px, py, pc = _peer(x, y, c, p)
            for k in range(n):
                pltpu.make_async_remote_copy(
                    src_ref=ins[k].at[4 * px + 2 * py + pc] if scatter else ins[k],
                    dst_ref=zones[k].at[me],
                    send_sem=send_sems.at[k * (N_DEV - 1) + p - 1],
                    recv_sem=recv_sems.at[k * (N_DEV - 1) + p - 1],
                    device_id=(px, py, pc),
                    device_id_type=pl.DeviceIdType.MESH,
                ).start()
        token[...] = jnp.zeros_like(token)

    hbm = pl.BlockSpec(memory_space=pltpu.HBM)
    sem = pl.BlockSpec(memory_space=pltpu.SEMAPHORE)
    outs = pl.pallas_call(
        body,
        name=name,
        out_shape=(pltpu.SemaphoreType.DMA((n * (N_DEV - 1),)), pltpu.SemaphoreType.DMA((n * (N_DEV - 1),)),
                   *[pltpu.HBM(a.shape, a.dtype) for a in arrays], *[pltpu.HBM(z.shape, z.dtype) for z in lands],
                   jax.ShapeDtypeStruct((8, LANES), F32)),
        in_specs=[hbm] * (2 * n) + [pl.BlockSpec(memory_space=pl.ANY)] * len(deps),
        out_specs=(sem, sem, *[hbm] * (2 * n), pl.BlockSpec(memory_space=pltpu.VMEM)),
        input_output_aliases={k: 2 + k for k in range(2 * n)},
        compiler_params=pltpu.CompilerParams(has_side_effects=pltpu.SideEffectType.DATAFLOW_SIDE_EFFECTING),
    )(*[pltpu.with_memory_space_constraint(a, pltpu.HBM) for a in arrays],
      *[pltpu.with_memory_space_constraint(z, pltpu.HBM) for z in lands], *deps)
    return outs[0], outs[1], list(outs[2:2 + n]), list(outs[2 + n:2 + 2 * n]), outs[-1]


def _exchange_wait(started, after, *, scatter, name):
    send_sems, recv_sems, srcs, lands, _ = started
    n = len(srcs)

    def body(*refs):
        ins, zones = refs[:n], refs[n:2 * n]
        s_sems, r_sems = refs[2 * n], refs[2 * n + 1]
        x, y, c = lax.axis_index("x"), lax.axis_index("y"), lax.axis_index("c")
        for p in range(1, N_DEV):
            px, py, pc = _peer(x, y, c, p)
            peer = 4 * px + 2 * py + pc
            for k in range(n):
                cp = pltpu.make_async_remote_copy(
                    src_ref=ins[k].at[peer] if scatter else ins[k],
                    dst_ref=zones[k].at[peer],
                    send_sem=s_sems.at[k * (N_DEV - 1) + p - 1],
                    recv_sem=r_sems.at[k * (N_DEV - 1) + p - 1],
                    device_id=(px, py, pc),
                    device_id_type=pl.DeviceIdType.MESH,
                )
                cp.wait_send()
                cp.wait_recv()

    hbm = pl.BlockSpec(memory_space=pltpu.HBM)
    sem = pl.BlockSpec(memory_space=pltpu.SEMAPHORE)
    outs = pl.pallas_call(
        body,
        name=name,
        out_shape=tuple(pltpu.HBM(a.shape, a.dtype) for a in srcs + lands),
        in_specs=[hbm] * (2 * n) + [sem, sem, pl.BlockSpec(memory_space=pl.ANY)],
        out_specs=tuple([hbm] * (2 * n)),
        input_output_aliases={k: k for k in range(2 * n)},
        compiler_params=pltpu.CompilerParams(has_side_effects=pltpu.SideEffectType.DATAFLOW_SIDE_EFFECTING),
    )(*srcs, *lands, send_sems, recv_sems, after)
    return list(outs[:n]), list(outs[n:])


def _mm(a, b, *, mode, out_dtype, name, add=None, tm=512, tn=512, b_rows=None, dep=None):
    rows_b = b.shape[0] if b_rows is None else b_rows
    if mode == "nn":
        (m, kd), nd = a.shape, b.shape[1]
        assert kd == rows_b
    elif mode == "nt":
        (m, kd), nd = a.shape, rows_b
    else:
        (kd, m), nd = a.shape, b.shape[1]
    tm = _pick(m, tm, LANES if mode == "tn" else 16)
    tn = _pick(nd, tn)
    dims = {"nn": NN, "nt": NT, "tn": TN}[mode]
    ni, nj = m // tm, nd // tn
    a_bytes, b_bytes = a.size * a.dtype.itemsize, b.size * b.dtype.itemsize
    i_outer = a_bytes + ni * b_bytes <= b_bytes + nj * a_bytes
    ij = (lambda g0, g1: (g0, g1)) if i_outer else (lambda g0, g1: (g1, g0))
    a_spec = (pl.BlockSpec((kd, tm), lambda g0, g1: (0, ij(g0, g1)[0])) if mode == "tn"
              else pl.BlockSpec((tm, kd), lambda g0, g1: (ij(g0, g1)[0], 0)))
    b_spec = (pl.BlockSpec((tn, kd), lambda g0, g1: (ij(g0, g1)[1], 0)) if mode == "nt"
              else pl.BlockSpec((kd, tn), lambda g0, g1: (0, ij(g0, g1)[1])))
    o_spec = pl.BlockSpec((tm, tn), lambda g0, g1: ij(g0, g1))
    has_add = add is not None

    def body(*refs):
        a_ref, b_ref = refs[0], refs[1]
        o_ref = refs[-1]
        acc = _dotb(a_ref[...], b_ref[...], dims)
        if has_add:
            acc = acc + refs[2][...].astype(F32)
        o_ref[...] = acc.astype(o_ref.dtype)

    ins = [a, b] + ([add] if has_add else []) + ([] if dep is None else [dep])
    specs = ([a_spec, b_spec] + ([o_spec] if has_add else [])
             + ([] if dep is None else [pl.BlockSpec((8, LANES), lambda g0, g1: (0, 0))]))
    return pl.pallas_call(
        body, name=name, grid=(ni, nj) if i_outer else (nj, ni), in_specs=specs, out_specs=o_spec,
        out_shape=jax.ShapeDtypeStruct((m, nd), out_dtype),
        compiler_params=_params("parallel", "parallel"),
    )(*ins)


def _mm_resid(a, b, x, gate, *, name, tm=256, tn=1024):
    m, kd = a.shape
    nd = b.shape[1]
    tm = _pick(m, tm, 16)
    tn = _pick(nd, tn)
    o_spec = pl.BlockSpec((tm, tn), lambda i, j: (i, j))

    def body(a_ref, b_ref, x_ref, g_ref, xo_ref, y_ref):
        y = _dotb(a_ref[...], b_ref[...], NN)
        y_ref[...] = y.astype(y_ref.dtype)
        xo_ref[...] = x_ref[...] + g_ref[...] * y

    return pl.pallas_call(
        body, name=name, grid=(m // tm, nd // tn),
        in_specs=[pl.BlockSpec((tm, kd), lambda i, j: (i, 0)), pl.BlockSpec((kd, tn), lambda i, j: (0, j)),
                  o_spec, pl.BlockSpec((1, tn), lambda i, j: (0, j))],
        out_specs=(o_spec, o_spec),
        out_shape=(jax.ShapeDtypeStruct((m, nd), F32), jax.ShapeDtypeStruct((m, nd), BF16)),
        compiler_params=_params("parallel", "parallel"),
    )(a, b, x, gate)


ROWS = 256


def _row_spec(width, rows=ROWS):
    return pl.BlockSpec((rows, width), lambda i: (i, 0))


def _const_spec(shape):
    return pl.BlockSpec(shape, lambda i: tuple(0 for _ in shape))


def _adaln_fwd(x, g, scale, shift, *, name):
    t, d = x.shape

    def body(x_ref, g_ref, sc_ref, sh_ref, h_ref):
        xv = x_ref[...]
        r = lax.rsqrt(jnp.mean(xv * xv, axis=-1, keepdims=True) + EPS)
        h_ref[...] = (xv * r * g_ref[...] * (1.0 + sc_ref[...]) + sh_ref[...]).astype(h_ref.dtype)

    return pl.pallas_call(
        body, name=name, grid=(t // ROWS,),
        in_specs=[_row_spec(d), _const_spec((1, d)), _const_spec((1, d)), _const_spec((1, d))],
        out_specs=_row_spec(d), out_shape=jax.ShapeDtypeStruct((t, d), BF16),
        compiler_params=_params("parallel"),
    )(x, g, scale, shift)


def _adaln_bwd(x, g, scale, shift, dh, dres, dep, *, name):
    t, d = x.shape

    def body(x_ref, g_ref, sc_ref, sh_ref, dh_ref, dr_ref, dep_ref, dx_ref, st_ref):
        @pl.when(pl.program_id(0) == 0)
        def _():
            st_ref[...] = jnp.zeros_like(st_ref)

        xv = x_ref[...]
        dhv = dh_ref[...].astype(F32)
        gv = g_ref[...]
        r = lax.rsqrt(jnp.mean(xv * xv, axis=-1, keepdims=True) + EPS)
        xh = xv * r
        nv = xh * gv
        dn = dhv * (1.0 + sc_ref[...])
        dxh = dn * gv
        dx_ref[...] = dr_ref[...] + r * (dxh - xh * jnp.mean(dxh * xh, axis=-1, keepdims=True))
        st_ref[0:1, :] += jnp.sum(dn * xh, axis=0, keepdims=True)
        st_ref[1:2, :] += jnp.sum(dhv * nv, axis=0, keepdims=True)
        st_ref[2:3, :] += jnp.sum(dhv, axis=0, keepdims=True)

    return pl.pallas_call(
        body, name=name, grid=(t // ROWS,),
        in_specs=[_row_spec(d), _const_spec((1, d)), _const_spec((1, d)), _const_spec((1, d)),
                  _row_spec(d), _row_spec(d), _const_spec((8, LANES))],
        out_specs=(_row_spec(d), _const_spec((8, d))),
        out_shape=(jax.ShapeDtypeStruct((t, d), F32), jax.ShapeDtypeStruct((8, d), F32)),
        compiler_params=_params("arbitrary"),
    )(x, g, scale, shift, dh, dres, dep)


def _adaln_gate_bwd(x, g, scale, shift, dh, dres, dep, y_up, gate_up, *, name):
    t, d = x.shape

    def body(x_ref, g_ref, sc_ref, sh_ref, dh_ref, dr_ref, dep_ref, y_ref, gu_ref, dx_ref, st_ref, dy_ref):
        @pl.when(pl.program_id(0) == 0)
        def _():
            st_ref[...] = jnp.zeros_like(st_ref)

        xv = x_ref[...]
        dhv = dh_ref[...].astype(F32)
        gv = g_ref[...]
        r = lax.rsqrt(jnp.mean(xv * xv, axis=-1, keepdims=True) + EPS)
        xh = xv * r
        nv = xh * gv
        dn = dhv * (1.0 + sc_ref[...])
        dxh = dn * gv
        dx = dr_ref[...] + r * (dxh - xh * jnp.mean(dxh * xh, axis=-1, keepdims=True))
        dx_ref[...] = dx
        dy_ref[...] = (dx * gu_ref[...]).astype(dy_ref.dtype)
        st_ref[0:1, :] += jnp.sum(dn * xh, axis=0, keepdims=True)
        st_ref[1:2, :] += jnp.sum(dhv * nv, axis=0, keepdims=True)
        st_ref[2:3, :] += jnp.sum(dhv, axis=0, keepdims=True)
        st_ref[3:4, :] += jnp.sum(dx * y_ref[...].astype(F32), axis=0, keepdims=True)

    return pl.pallas_call(
        body, name=name, grid=(t // ROWS,),
        in_specs=[_row_spec(d), _const_spec((1, d)), _const_spec((1, d)), _const_spec((1, d)),
                  _row_spec(d), _row_spec(d), _const_spec((8, LANES)), _row_spec(d), _const_spec((1, d))],
        out_specs=(_row_spec(d), _const_spec((8, d)), _row_spec(d)),
        out_shape=(jax.ShapeDtypeStruct((t, d), F32), jax.ShapeDtypeStruct((8, d), F32),
                   jax.ShapeDtypeStruct((t, d), BF16)),
        compiler_params=_params("arbitrary"),
    )(x, g, scale, shift, dh, dres, dep, y_up, gate_up)


def _gate_bwd(dxo, y, gate, dep, *, name):
    t, d = dxo.shape

    def body(dx_ref, y_ref, g_ref, dep_ref, dy_ref, st_ref):
        @pl.when(pl.program_id(0) == 0)
        def _():
            st_ref[...] = jnp.zeros_like(st_ref)

        dxv = dx_ref[...]
        dy_ref[...] = (dxv * g_ref[...]).astype(dy_ref.dtype)
        st_ref[0:1, :] += jnp.sum(dxv * y_ref[...], axis=0, keepdims=True)

    return pl.pallas_call(
        body, name=name, grid=(t // ROWS,),
        in_specs=[_row_spec(d), _row_spec(d), _const_spec((1, d)), _const_spec((8, LANES))],
        out_specs=(_row_spec(d), _const_spec((8, d))),
        out_shape=(jax.ShapeDtypeStruct((t, d), BF16), jax.ShapeDtypeStruct((8, d), F32)),
        compiler_params=_params("arbitrary"),
    )(dxo, y, gate, dep)


def _loss_head(x, g, target, *, name):
    t, d = x.shape

    def body(x_ref, g_ref, t_ref, dx_ref, st_ref, ls_ref):
        @pl.when(pl.program_id(0) == 0)
        def _():
            st_ref[...] = jnp.zeros_like(st_ref)
            ls_ref[...] = jnp.zeros_like(ls_ref)

        xv = x_ref[...]
        gv = g_ref[...]
        r = lax.rsqrt(jnp.mean(xv * xv, axis=-1, keepdims=True) + EPS)
        xh = xv * r
        err = xh * gv - t_ref[...]
        ls_ref[...] += 0.5 * jnp.sum(jnp.mean(err * err, axis=-1, keepdims=True))
        dy = err * (1.0 / d)
        dxh = dy * gv
        dx_ref[...] = r * (dxh - xh * jnp.mean(dxh * xh, axis=-1, keepdims=True))
        st_ref[0:1, :] += jnp.sum(dy * xh, axis=0, keepdims=True)

    return pl.pallas_call(
        body, name=name, grid=(t // ROWS,),
        in_specs=[_row_spec(d), _const_spec((1, d)), _row_spec(d)],
        out_specs=(_row_spec(d), _const_spec((8, d)), _const_spec((8, LANES))),
        out_shape=(jax.ShapeDtypeStruct((t, d), F32), jax.ShapeDtypeStruct((8, d), F32),
                   jax.ShapeDtypeStruct((8, LANES), F32)),
        compiler_params=_params("arbitrary"),
    )(x, g, target)


FFN_BLOCK = D_FF // 2
FFN_ROWS = 512


def _ffn_chunks(width):
    edges = [min(width, 3 * LANES * i) for i in range(width // (3 * LANES) + 2)]
    return [slice(lo, hi) for lo, hi in zip(edges[:-1], edges[1:]) if hi > lo]


def _ffn_gu_fwd(h, wg, wu, dep, *, name):
    t, d = h.shape
    tn = FFN_BLOCK

    chunks = _ffn_chunks(tn)
    rows = _pick(t, FFN_ROWS, 16)

    def body(h_ref, wg_ref, wu_ref, dep_ref, s_ref, a_ref, b_ref):
        hv = h_ref[...]
        ab = [(_dotb(hv, wg_ref[sl, :], NT), _dotb(hv, wu_ref[sl, :], NT)) for sl in chunks]
        for sl, (a, b) in zip(chunks, ab):
            s_ref[:, sl] = (a * _sigmoid(a) * b).astype(s_ref.dtype)
            a_ref[:, sl] = a.astype(a_ref.dtype)
            b_ref[:, sl] = b.astype(b_ref.dtype)

    w_spec = pl.BlockSpec((tn, d), lambda j, i: (j, 0))
    o_spec = pl.BlockSpec((rows, tn), lambda j, i: (i, j))
    return pl.pallas_call(
        body, name=name, grid=(D_FF // tn, t // rows),
        in_specs=[pl.BlockSpec((rows, d), lambda j, i: (i, 0)), w_spec, w_spec,
                  pl.BlockSpec((8, LANES), lambda j, i: (0, 0))],
        out_specs=(o_spec, o_spec, o_spec),
        out_shape=(jax.ShapeDtypeStruct((t, D_FF), BF16),) * 3,
        compiler_params=_params("parallel", "parallel"),
    )(h, wg, wu, dep)


def _ffn_down_dx(dy, w_down, a, b, *, name):
    t, d = dy.shape
    tn = FFN_BLOCK

    chunks = _ffn_chunks(tn)
    rows = _pick(t, FFN_ROWS, 16)

    def body(dy_ref, w_ref, a_ref, b_ref, da_ref, db_ref):
        dyv = dy_ref[...]
        ds = [_dotb(dyv, w_ref[sl, :], NT) for sl in chunks]
        for sl, dsc in zip(chunks, ds):
            av = a_ref[:, sl].astype(F32)
            sg = _sigmoid(av)
            da_ref[:, sl] = (dsc * b_ref[:, sl].astype(F32) * sg * (1.0 + av * (1.0 - sg))).astype(da_ref.dtype)
            db_ref[:, sl] = (dsc * av * sg).astype(db_ref.dtype)

    o_spec = pl.BlockSpec((rows, tn), lambda j, i: (i, j))
    return pl.pallas_call(
        body, name=name, grid=(D_FF // tn, t // rows),
        in_specs=[pl.BlockSpec((rows, d), lambda j, i: (i, 0)), pl.BlockSpec((tn, d), lambda j, i: (j, 0)),
                  o_spec, o_spec],
        out_specs=(o_spec, o_spec),
        out_shape=(jax.ShapeDtypeStruct((t, D_FF), BF16),) * 2,
        compiler_params=_params("parallel", "parallel"),
    )(dy, w_down, a, b)


def _shift_rows(v, s, rows):
    if s == 0:
        return v
    return jnp.where(rows >= s, pltpu.roll(v, s, 0), 0.0)


def _unshift_rows(v, s, rows, t):
    if s == 0:
        return v
    return jnp.where(rows < t - s, pltpu.roll(v, t - s, 0), 0.0)


def _conv_taps(x, rows):
    return [_shift_rows(x, GDN_CONV - 1 - j, rows) for j in range(GDN_CONV)]


def _conv_silu(xs, w):
    z = w[0:1, :] * xs[0]
    for j in range(1, GDN_CONV):
        z = z + w[j:j + 1, :] * xs[j]
    sg = _sigmoid(z)
    return z, sg, z * sg


def _gdn_prep_fwd(proj, conv_wt, *, name):
    t = proj.shape[0]
    nh = GDN_HEADS

    hp = GDN_PREP_HEADS
    wd = hp * LANES

    def body(x_ref, w_ref, y_ref):
        j = pl.program_id(0) * hp
        rows = lax.broadcasted_iota(jnp.int32, (t, LANES), 0)
        qscale = jnp.where(j < nh, GDN_HEAD_DIM ** -0.5, 1.0)
        for i in range(hp):
            sl = slice(i * LANES, (i + 1) * LANES)
            _, _, s = _conv_silu(_conv_taps(x_ref[:, sl], rows), w_ref[:, sl])
            rs = lax.rsqrt(jnp.sum(s * s, axis=-1, keepdims=True) + EPS)
            y_ref[:, sl] = jnp.where(j < 2 * nh, s * rs * qscale, s)

    return pl.pallas_call(
        body, name=name, grid=(3 * nh // hp,),
        in_specs=[pl.BlockSpec((t, wd), lambda j: (0, j)), pl.BlockSpec((GDN_CONV, wd), lambda j: (0, j))],
        out_specs=pl.BlockSpec((t, wd), lambda j: (0, j)),
        out_shape=jax.ShapeDtypeStruct((t, 3 * GDN_KEY_DIM), F32),
        compiler_params=_params("parallel"),
    )(proj, conv_wt)


def _gdn_prep_bwd(proj, conv_wt, dy, *, name):
    t = proj.shape[0]
    nh = GDN_HEADS

    hp = GDN_PREP_HEADS
    wd = hp * LANES
    per_seg = nh // hp

    def body(x_ref, w_ref, dy_ref, dx_ref, dw_ref):
        j = pl.program_id(0) * hp
        rows = lax.broadcasted_iota(jnp.int32, (t, LANES), 0)
        qscale = jnp.where(j < nh, GDN_HEAD_DIM ** -0.5, 1.0)
        for i in range(hp):
            sl = slice(i * LANES, (i + 1) * LANES)
            w = w_ref[:, sl]
            xs = _conv_taps(x_ref[:, sl], rows)
            z, sg, s = _conv_silu(xs, w)
            rs = lax.rsqrt(jnp.sum(s * s, axis=-1, keepdims=True) + EPS)
            dyv = dy_ref[:, sl]
            nv = s * rs
            de = dyv * qscale
            ds_qk = rs * (de - nv * jnp.sum(de * nv, axis=-1, keepdims=True))
            ds = jnp.where(j < 2 * nh, ds_qk, dyv)
            dz = ds * sg * (1.0 + z * (1.0 - sg))
            dx = w[GDN_CONV - 1:GDN_CONV, :] * dz
            dw_ref[GDN_CONV - 1:GDN_CONV, sl] = jnp.sum(dz * xs[GDN_CONV - 1], axis=0, keepdims=True)
            for k in range(GDN_CONV - 1):
                dx = dx + w[k:k + 1, :] * _unshift_rows(dz, GDN_CONV - 1 - k, rows, t)
                dw_ref[k:k + 1, sl] = jnp.sum(dz * xs[k], axis=0, keepdims=True)
            dx_ref[:, sl] = dx.astype(dx_ref.dtype)

    return pl.pallas_call(
        body, name=name, grid=(3 * nh // hp,),
        in_specs=[pl.BlockSpec((t, wd), lambda j: (0, j)), pl.BlockSpec((GDN_CONV, wd), lambda j: (0, j)),
                  pl.BlockSpec((None, t, wd), lambda j: (j // per_seg, 0, j % per_seg))],
        out_specs=(pl.BlockSpec((t, wd), lambda j: (0, j)), pl.BlockSpec((GDN_CONV, wd), lambda j: (0, j))),
        out_shape=(jax.ShapeDtypeStruct((t, 3 * GDN_KEY_DIM), BF16),
                   jax.ShapeDtypeStruct((GDN_CONV, 3 * GDN_KEY_DIM), F32)),
        compiler_params=_params("parallel"),
    )(proj, conv_wt, dy)


def _softplus(z):
    return jnp.maximum(z, 0.0) + jnp.log(1.0 + jnp.exp(-jnp.abs(z)))


def _gdn_gate_fwd(ab, prm, *, name):
    t = ab.shape[0]

    def body(ab_ref, p_ref, o_ref):
        v = ab_ref[...]
        lane = lax.broadcasted_iota(jnp.int32, v.shape, 1)
        g = -jnp.exp(p_ref[0:1, :]) * _softplus(v + p_ref[1:2, :])
        o_ref[...] = jnp.where(lane < GDN_HEADS, g, jnp.where(lane < 2 * GDN_HEADS, _sigmoid(v), 0.0))

    return pl.pallas_call(
        body, name=name, grid=(t // ROWS,),
        in_specs=[_row_spec(LANES), _const_spec((8, LANES))], out_specs=_row_spec(LANES),
        out_shape=jax.ShapeDtypeStruct((t, LANES), F32), compiler_params=_params("parallel"),
    )(ab, prm)


def _gdn_gate_bwd(ab, prm, dgb, *, name):
    t = ab.shape[0]

    def body(ab_ref, p_ref, d_ref, o_ref, st_ref):
        @pl.when(pl.program_id(0) == 0)
        def _():
            st_ref[...] = jnp.zeros_like(st_ref)

        v = ab_ref[...]
        dv = d_ref[...]
        lane = lax.broadcasted_iota(jnp.int32, v.shape, 1)
        is_a = lane < GDN_HEADS
        is_b = jnp.logical_and(lane >= GDN_HEADS, lane < 2 * GDN_HEADS)
        a_exp = jnp.exp(p_ref[0:1, :])
        zz = v + p_ref[1:2, :]
        g = -a_exp * _softplus(zz)
        da = dv * (-a_exp) * _sigmoid(zz)
        beta = _sigmoid(v)
        db = dv * beta * (1.0 - beta)
        o_ref[...] = jnp.where(is_a, da, jnp.where(is_b, db, 0.0)).astype(o_ref.dtype)
        st_ref[0:1, :] += jnp.sum(jnp.where(is_a, dv * g, 0.0), axis=0, keepdims=True)
        st_ref[1:2, :] += jnp.sum(jnp.where(is_a, da, 0.0), axis=0, keepdims=True)

    return pl.pallas_call(
        body, name=name, grid=(t // ROWS,),
        in_specs=[_row_spec(LANES), _const_spec((8, LANES)), _row_spec(LANES)],
        out_specs=(_row_spec(LANES), _const_spec((8, LANES))),
        out_shape=(jax.ShapeDtypeStruct((t, LANES), BF16), jax.ShapeDtypeStruct((8, LANES), F32)),
        compiler_params=_params("arbitrary"),
    )(ab, prm, dgb)


def _gdn_local(qs, ks, vs, gbs, bbs, tinvs=None):
    nh = len(qs)
    cs = qs[0].shape[0]
    hs = range(nh)
    r = lax.broadcasted_iota(jnp.int32, (cs, cs), 0)
    c = lax.broadcasted_iota(jnp.int32, (cs, cs), 1)
    tril, strict, eye = r >= c, r > c, r == c
    ident = jnp.where(eye, 1.0, 0.0)
    g_colb = [gbs[h][:, :cs] for h in hs]
    g_row = [jnp.sum(jnp.where(eye, g_colb[h], 0.0), axis=0, keepdims=True) for h in hs]
    gc_col = [jnp.sum(jnp.where(tril, g_row[h], 0.0), axis=1, keepdims=True) for h in hs]
    gc_row = [jnp.sum(jnp.where(r <= c, g_colb[h], 0.0), axis=0, keepdims=True) for h in hs]
    decay = [jnp.exp(jnp.where(tril, gc_col[h] - gc_row[h], NEG)) for h in hs]
    gamma = [jnp.exp(gc_col[h]) for h in hs]
    gcl = [gc_col[h][cs - 1:cs, :] for h in hs]
    gl = [jnp.exp(gcl[h]) for h in hs]
    kdec = [jnp.exp(gcl[h] - gc_col[h]) for h in hs]
    kb = [ks[h] * bbs[h] for h in hs]
    kk = [_dotb(kb[h], ks[h], NT) for h in hs]
    qk = [_dotb(qs[h], ks[h], NT) for h in hs]
    lmat = [jnp.where(strict, kk[h] * decay[h], 0.0) for h in hs]
    pmat = [jnp.where(tril, qk[h] * decay[h], 0.0) for h in hs]
    if tinvs is None:
        xm = [-lmat[h] for h in hs]
        tinv = [ident + xm[h] for h in hs]
        for _ in range(int(math.log2(cs)) - 1):
            xm = [_dotf(xm[h], xm[h], NN) for h in hs]
            tinv = [tinv[h] + _dotf(tinv[h], xm[h], NN) for h in hs]
    else:
        tinv = tinvs
    vb = [vs[h] * bbs[h] for h in hs]
    kg = [kb[h] * gamma[h] for h in hs]
    u = [_dotf(tinv[h], vb[h], NN) for h in hs]
    w = [_dotf(tinv[h], kg[h], NN) for h in hs]
    return [dict(tril=tril, strict=strict, eye=eye, r=r, c=c, decay=decay[h], gamma=gamma[h], gl=gl[h], kdec=kdec[h],
                 kb=kb[h], lmat=lmat[h], tinv=tinv[h], vb=vb[h], kg=kg[h], u=u[h], w=w[h], pmat=pmat[h],
                 qd=qs[h] * gamma[h], kd=ks[h] * kdec[h]) for h in hs]


def _head_columns(gbeta, cs):
    gbs = [jnp.broadcast_to(gbeta[:, h:h + 1], (cs, LANES)) for h in range(GDN_HEADS)]
    bbs = [jnp.broadcast_to(gbeta[:, GDN_HEADS + h:GDN_HEADS + h + 1], (cs, LANES)) for h in range(GDN_HEADS)]
    return gbs, bbs


def _gdn_chunk_fwd(qkv, gbeta, *, name):
    t = qkv.shape[0]
    nh, cs, hd = GDN_HEADS, GDN_CHUNK, GDN_HEAD_DIM
    nc = t // cs

    hb = GDN_HEAD_BATCH
    ng = nh // hb
    assert ng == 1

    def body(q_ref, k_ref, v_ref, gb_ref, o_ref, st_ref, ti_ref, s_ref):
        @pl.when(pl.program_id(1) == 0)
        def _():
            s_ref[...] = jnp.zeros_like(s_ref)

        sls = [slice(i * hd, (i + 1) * hd) for i in range(hb)]
        hs = range(hb)
        s = [s_ref[i] for i in hs]
        gbs, bbs = _head_columns(gb_ref[...], cs)
        lo = _gdn_local([q_ref[:, sl] for sl in sls], [k_ref[:, sl] for sl in sls], [v_ref[:, sl] for sl in sls],
                        gbs, bbs)
        ws = [_dotb(lo[i]["w"], s[i], NN) for i in hs]
        qs = [_dotb(lo[i]["qd"], s[i], NN) for i in hs]
        vn = [lo[i]["u"] - ws[i] for i in hs]
        pv = [_dotb(lo[i]["pmat"], vn[i], NN) for i in hs]
        kv = [_dotb(lo[i]["kd"], vn[i], TN) for i in hs]
        for i, sl in enumerate(sls):
            st_ref[i, 0] = s[i]
            ti_ref[i, 0] = lo[i]["tinv"]
            o_ref[:, sl] = qs[i] + pv[i]
            s_ref[i] = s[i] * lo[i]["gl"] + kv[i]

    col = lambda off: pl.BlockSpec((cs, hb * hd), lambda h, n: (n, off + h))
    return pl.pallas_call(
        body, name=name, grid=(ng, nc),
        in_specs=[col(0), col(ng), col(2 * ng), pl.BlockSpec((cs, LANES), lambda h, n: (n, 0))],
        out_specs=(col(0), pl.BlockSpec((hb, 1, hd, hd), lambda h, n: (h, n, 0, 0)),
                   pl.BlockSpec((hb, 1, cs, cs), lambda h, n: (h, n, 0, 0))),
        out_shape=(jax.ShapeDtypeStruct((t, nh * hd), F32), jax.ShapeDtypeStruct((nh, nc, hd, hd), F32),
                   jax.ShapeDtypeStruct((nh, nc, cs, cs), F32)),
        scratch_shapes=[pltpu.VMEM((hb, hd, hd), F32)],
        compiler_params=_params("parallel", "arbitrary"),
    )(qkv, qkv, qkv, gbeta)


def _gdn_chunk_bwd(qkv, gbeta, states, tinvs, do, *, name):
    t = qkv.shape[0]
    nh, cs, hd = GDN_HEADS, GDN_CHUNK, GDN_HEAD_DIM
    nc = t // cs

    hb = GDN_HEAD_BATCH
    ng = nh // hb
    assert ng == 1

    def heads_bwd(q, k, v, gb, bb, s, ti, dsn, dov):
        hs = range(len(q))
        lo = _gdn_local(q, k, v, gb, bb, ti)
        tril, strict, eye, r, c = lo[0]["tril"], lo[0]["strict"], lo[0]["eye"], lo[0]["r"], lo[0]["c"]
        rowi = lax.broadcasted_iota(jnp.int32, (cs, 1), 0)
        get = lambda name: [lo[h][name] for h in hs]
        decay, gamma, gl, kdec = get("decay"), get("gamma"), get("gl"), get("kdec")
        kb, tinv, w, pmat, kd, qd = get("kb"), get("tinv"), get("w"), get("pmat"), get("kd"), get("qd")
        ws = [_dotb(w[h], s[h], NN) for h in hs]
        pdo = [_dotb(pmat[h], dov[h], TN) for h in hs]
        kds = [_dotb(kd[h], dsn[h], NN) for h in hs]
        dqd = [_dotb(dov[h], s[h], NT) for h in hs]
        qdo = [_dotb(qd[h], dov[h], TN) for h in hs]
        vn = [lo[h]["u"] - ws[h] for h in hs]
        dvn = [pdo[h] + kds[h] for h in hs]
        dp = [jnp.where(tril, _dotb(dov[h], vn[h], NT), 0.0) for h in hs]
        dkd = [_dotb(vn[h], dsn[h], NT) for h in hs]
        dw = [-_dotb(dvn[h], s[h], NT) for h in hs]
        wdv = [_dotb(w[h], dvn[h], TN) for h in hs]
        dvb = [_dotf(tinv[h], dvn[h], TN) for h in hs]
        dt1 = [_dotf(dvn[h], lo[h]["vb"], NT) for h in hs]
        dkg = [_dotf(tinv[h], dw[h], TN) for h in hs]
        dt2 = [_dotf(dw[h], lo[h]["kg"], NT) for h in hs]
        tdt = [_dotf(tinv[h], dt1[h] + dt2[h], TN) for h in hs]
        dl = [jnp.where(strict, -_dotf(tdt[h], tinv[h], NT), 0.0) for h in hs]
        dkk = [dl[h] * decay[h] for h in hs]
        dqk = [dp[h] * decay[h] for h in hs]
        dkb = [_dotb(dkk[h], k[h], NN) + dkg[h] * gamma[h] for h in hs]
        dk1 = [_dotb(dkk[h], kb[h], TN) for h in hs]
        dk2 = [_dotb(dqk[h], q[h], TN) for h in hs]
        dq1 = [_dotb(dqk[h], k[h], NN) for h in hs]
        out = []
        for h in hs:
            dgl = jnp.sum(jnp.sum(dsn[h] * s[h], axis=1, keepdims=True), axis=0, keepdims=True)
            ds_prev = gl[h] * dsn[h] + qdo[h] - wdv[h]
            dk = dk1[h] + dk2[h] + dkd[h] * kdec[h] + dkb[h] * bb[h]
            dq = dq1[h] + dqd[h] * gamma[h]
            dbeta = jnp.sum(dvb[h] * v[h], axis=-1, keepdims=True) + jnp.sum(dkb[h] * k[h], axis=-1, keepdims=True)
            e = dl[h] * lo[h]["lmat"] + dp[h] * pmat[h]
            e_col = jnp.sum(e, axis=0, keepdims=True)
            dgc = jnp.sum(e, axis=1, keepdims=True) - jnp.sum(jnp.where(eye, e_col, 0.0), axis=1, keepdims=True)
            dgamma = (jnp.sum(dqd[h] * q[h], axis=-1, keepdims=True)
                      + jnp.sum(dkg[h] * kb[h], axis=-1, keepdims=True))
            rk = jnp.sum(dkd[h] * k[h], axis=-1, keepdims=True) * kdec[h]
            dgcl = jnp.sum(rk, axis=0, keepdims=True) + dgl * gl[h]
            dgc = dgc + dgamma * gamma[h] - rk + jnp.where(rowi == cs - 1, dgcl, 0.0)
            dgc_row = jnp.sum(jnp.where(eye, dgc, 0.0), axis=0, keepdims=True)
            dg = jnp.sum(jnp.where(c >= r, dgc_row, 0.0), axis=1, keepdims=True)
            out.append((dq, dk, dvb[h] * bb[h], dbeta, dg, ds_prev))
        return out

    def body(q_ref, k_ref, v_ref, gb_ref, st_ref, ti_ref, do_ref, d_ref, dgb_ref, ds_ref):
        @pl.when(pl.program_id(1) == 0)
        def _():
            ds_ref[...] = jnp.zeros_like(ds_ref)

        sls = [slice(i * hd, (i + 1) * hd) for i in range(hb)]
        hs = range(hb)
        gbs, bbs = _head_columns(gb_ref[...], cs)
        outs = heads_bwd([q_ref[:, sl] for sl in sls], [k_ref[:, sl] for sl in sls], [v_ref[:, sl] for sl in sls],
                         gbs, bbs, [st_ref[i, 0] for i in hs],
                         [ti_ref[i, 0] for i in hs], [ds_ref[i] for i in hs], [do_ref[:, sl] for sl in sls])
        lane = lax.broadcasted_iota(jnp.int32, (cs, LANES), 1)
        dgb = jnp.zeros((cs, LANES), F32)
        for i, sl in enumerate(sls):
            dq, dk, dv, dbeta, dg, ds_prev = outs[i]
            d_ref[0, :, sl], d_ref[1, :, sl], d_ref[2, :, sl] = dq, dk, dv
            dgb = jnp.where(lane == i, dg, jnp.where(lane == nh + i, dbeta, dgb))
            ds_ref[i] = ds_prev
        dgb_ref[...] = dgb

    col = lambda off: pl.BlockSpec((cs, hb * hd), lambda h, n: (nc - 1 - n, off + h))
    gspec = pl.BlockSpec((cs, LANES), lambda h, n: (nc - 1 - n, 0))
    return pl.pallas_call(
        body, name=name, grid=(ng, nc),
        in_specs=[col(0), col(ng), col(2 * ng), gspec,
                  pl.BlockSpec((hb, 1, hd, hd), lambda h, n: (h, nc - 1 - n, 0, 0)),
                  pl.BlockSpec((hb, 1, cs, cs), lambda h, n: (h, nc - 1 - n, 0, 0)), col(0)],
        out_specs=(pl.BlockSpec((3, cs, hb * hd), lambda h, n: (0, nc - 1 - n, h)), gspec),
        out_shape=(jax.ShapeDtypeStruct((3, t, nh * hd), F32), jax.ShapeDtypeStruct((t, LANES), F32)),
        scratch_shapes=[pltpu.VMEM((hb, hd, hd), F32)],
        compiler_params=_params("parallel", "arbitrary"),
    )(qkv, qkv, qkv, gbeta, states, tinvs, do)


def _gdn_onorm_fwd(o, proj, norm_g, *, name):
    t = o.shape[0]
    w = GDN_KEY_DIM
    goff = 3 * GDN_KEY_DIM // w

    def body(o_ref, gp_ref, g_ref, y_ref):
        gv = g_ref[...]
        for h in range(GDN_HEADS):
            sl = slice(h * GDN_HEAD_DIM, (h + 1) * GDN_HEAD_DIM)
            oh = o_ref[:, sl]
            gp = gp_ref[:, sl]
            r = lax.rsqrt(jnp.mean(oh * oh, axis=-1, keepdims=True) + EPS)
            y_ref[:, sl] = (oh * r * gv * gp * _sigmoid(gp)).astype(y_ref.dtype)

    return pl.pallas_call(
        body, name=name, grid=(t // ROWS,),
        in_specs=[_row_spec(w), pl.BlockSpec((ROWS, w), lambda i: (i, goff)), _const_spec((1, GDN_HEAD_DIM))],
        out_specs=_row_spec(w), out_shape=jax.ShapeDtypeStruct((t, w), BF16),
        compiler_params=_params("parallel"),
    )(o, proj, norm_g)


def _gdn_onorm_bwd(o, proj, norm_g, dy, *, name):
    t = o.shape[0]
    w = GDN_KEY_DIM
    goff = 3 * GDN_KEY_DIM // w

    def body(o_ref, gp_ref, g_ref, dy_ref, do_ref, dgp_ref, st_ref):
        @pl.when(pl.program_id(0) == 0)
        def _():
            st_ref[...] = jnp.zeros_like(st_ref)

        gv = g_ref[...]
        acc = jnp.zeros((1, GDN_HEAD_DIM), F32)
        for h in range(GDN_HEADS):
            sl = slice(h * GDN_HEAD_DIM, (h + 1) * GDN_HEAD_DIM)
            oh = o_ref[:, sl]
            gp = gp_ref[:, sl]
            dyv = dy_ref[:, sl].astype(F32)
            r = lax.rsqrt(jnp.mean(oh * oh, axis=-1, keepdims=True) + EPS)
            xh = oh * r
            sg = _sigmoid(gp)
            dn = dyv * gp * sg
            dgp_ref[:, sl] = (dyv * xh * gv * sg * (1.0 + gp * (1.0 - sg))).astype(dgp_ref.dtype)
            acc = acc + jnp.sum(dn * xh, axis=0, keepdims=True)
            dxh = dn * gv
            do_ref[:, sl] = r * (dxh - xh * jnp.mean(dxh * xh, axis=-1, keepdims=True))
        st_ref[0:1, :] += acc

    return pl.pallas_call(
        body, name=name, grid=(t // ROWS,),
        in_specs=[_row_spec(w), pl.BlockSpec((ROWS, w), lambda i: (i, goff)), _const_spec((1, GDN_HEAD_DIM)),
                  _row_spec(w)],
        out_specs=(_row_spec(w), _row_spec(w), _const_spec((8, GDN_HEAD_DIM))),
        out_shape=(jax.ShapeDtypeStruct((t, w), F32), jax.ShapeDtypeStruct((t, w), BF16),
                   jax.ShapeDtypeStruct((8, GDN_HEAD_DIM), F32)),
        compiler_params=_params("arbitrary"),
    )(o, proj, norm_g, dy)


def _mla_prep_fwd(proj, qg, kvg, *, name):
    t = proj.shape[0]
    q1, k1 = MLA_Q_RANK, MLA_Q_RANK + MLA_KV_RANK

    def body(p_ref, qg_ref, kg_ref, cq_ref, ck_ref):
        cq = p_ref[:, 0:q1]
        ck = p_ref[:, q1:k1]
        cq_ref[...] = (cq * lax.rsqrt(jnp.mean(cq * cq, axis=-1, keepdims=True) + EPS) * qg_ref[...]).astype(BF16)
        ck_ref[...] = (ck * lax.rsqrt(jnp.mean(ck * ck, axis=-1, keepdims=True) + EPS) * kg_ref[...]).astype(BF16)

    return pl.pallas_call(
        body, name=name, grid=(t // ROWS,),
        in_specs=[_row_spec(MLA_IN), _const_spec((1, MLA_Q_RANK)), _const_spec((1, MLA_KV_RANK))],
        out_specs=(_row_spec(MLA_Q_RANK), _row_spec(MLA_KV_RANK)),
        out_shape=(jax.ShapeDtypeStruct((t, MLA_Q_RANK), BF16), jax.ShapeDtypeStruct((t, MLA_KV_RANK), BF16)),
        compiler_params=_params("parallel"),
    )(proj, qg, kvg)


def _mla_prep_bwd(proj, qg, kvg, dcq, dck, dkr, *, name):
    t = proj.shape[0]
    q1, k1 = MLA_Q_RANK, MLA_Q_RANK + MLA_KV_RANK

    def body(p_ref, qg_ref, kg_ref, dq_ref, dk_ref, dr_ref, dp_ref, st_ref):
        @pl.when(pl.program_id(0) == 0)
        def _():
            st_ref[...] = jnp.zeros_like(st_ref)

        for lo, hi, g_ref, d_ref in ((0, q1, qg_ref, dq_ref), (q1, k1, kg_ref, dk_ref)):
            xv = p_ref[:, lo:hi]
            dn = d_ref[...]
            r = lax.rsqrt(jnp.mean(xv * xv, axis=-1, keepdims=True) + EPS)
            xh = xv * r
            dxh = dn * g_ref[...]
            dp_ref[:, lo:hi] = (r * (dxh - xh * jnp.mean(dxh * xh, axis=-1, keepdims=True))).astype(dp_ref.dtype)
            st_ref[0:1, lo:hi] += jnp.sum(dn * xh, axis=0, keepdims=True)
        dp_ref[:, k1:MLA_IN] = dr_ref[:, 0:MLA_ROPE].astype(dp_ref.dtype)

    return pl.pallas_call(
        body, name=name, grid=(t // ROWS,),
        in_specs=[_row_spec(MLA_IN), _const_spec((1, MLA_Q_RANK)), _const_spec((1, MLA_KV_RANK)),
                  _row_spec(MLA_Q_RANK), _row_spec(MLA_KV_RANK), _row_spec(LANES)],
        out_specs=(_row_spec(MLA_IN), _const_spec((8, MLA_IN))),
        out_shape=(jax.ShapeDtypeStruct((t, MLA_IN), BF16), jax.ShapeDtypeStruct((8, MLA_IN), F32)),
        compiler_params=_params("arbitrary"),
    )(proj, qg, kvg, dcq, dck, dkr)


ATT_BLOCK = 256
ATT_HEAD_BATCH = 8
ATT_HEAD_BATCH_BWD = 4
ATT_SCALE = MLA_QK ** -0.5


def _diagonal_mask(blk):
    return lax.broadcasted_iota(jnp.int32, (blk, blk), 1) <= lax.broadcasted_iota(jnp.int32, (blk, blk), 0)


def _swap_halves(xv, first):
    return jnp.where(first, pltpu.roll(xv, LANES - MLA_ROPE // 2, 1), pltpu.roll(xv, MLA_ROPE // 2, 1))


def _rope_qk(qf, proj, cos_t, sin_t, *, name):
    t = qf.shape[0]
    nrope = MLA_HEADS * MLA_ROPE
    q_blk = MLA_HEADS * MLA_NOPE // nrope
    k_blk = (MLA_Q_RANK + MLA_KV_RANK) // LANES

    def body(q_ref, p_ref, c_ref, s_ref, qo_ref, ko_ref):
        cv, sv = c_ref[...], s_ref[...]
        lane = lax.broadcasted_iota(jnp.int32, (ROWS, LANES), 1)
        first = (lane % MLA_ROPE) < (MLA_ROPE // 2)
        for i in range(nrope // LANES):
            sl = slice(i * LANES, (i + 1) * LANES)
            xv = q_ref[:, sl].astype(F32)
            qo_ref[:, sl] = (xv * cv + _swap_halves(xv, first) * sv).astype(qo_ref.dtype)
        kv = jnp.where(lane < MLA_ROPE, p_ref[...], 0.0)
        ko_ref[...] = (kv * cv + _swap_halves(kv, first) * sv).astype(ko_ref.dtype)

    return pl.pallas_call(
        body, name=name, grid=(t // ROWS,),
        in_specs=[pl.BlockSpec((ROWS, nrope), lambda i: (i, q_blk)), pl.BlockSpec((ROWS, LANES), lambda i: (i, k_blk)),
                  _row_spec(LANES), _row_spec(LANES)],
        out_specs=(_row_spec(nrope), _row_spec(LANES)),
        out_shape=(jax.ShapeDtypeStruct((t, nrope), BF16), jax.ShapeDtypeStruct((t, LANES), BF16)),
        compiler_params=_params("parallel"),
    )(qf, proj, cos_t, sin_t)


def _rope_qk_bwd(dqr, dkr_parts, cos_t, sin_t, *, name):
    t, nrope = dqr.shape
    ng = dkr_parts.shape[0]

    def body(d_ref, k_ref, c_ref, s_ref, qo_ref, ko_ref):
        cv, sv = c_ref[...], s_ref[...]
        lane = lax.broadcasted_iota(jnp.int32, (ROWS, LANES), 1)
        first = (lane % MLA_ROPE) < (MLA_ROPE // 2)
        for i in range(nrope // LANES):
            sl = slice(i * LANES, (i + 1) * LANES)
            dv = d_ref[:, sl]
            qo_ref[:, sl] = (dv * cv + _swap_halves(dv * sv, first)).astype(qo_ref.dtype)
        dk = k_ref[0]
        for g in range(1, ng):
            dk = dk + k_ref[g]
        dk = jnp.where(lane < MLA_ROPE, dk, 0.0)
        ko_ref[...] = jnp.where(lane < MLA_ROPE, dk * cv + _swap_halves(dk * sv, first), 0.0)

    return pl.pallas_call(
        body, name=name, grid=(t // ROWS,),
        in_specs=[_row_spec(nrope), pl.BlockSpec((ng, ROWS, LANES), lambda i: (0, i, 0)), _row_spec(LANES),
                  _row_spec(LANES)],
        out_specs=(_row_spec(nrope), _row_spec(LANES)),
        out_shape=(jax.ShapeDtypeStruct((t, nrope), BF16), jax.ShapeDtypeStruct((t, LANES), F32)),
        compiler_params=_params("parallel"),
    )(dqr, dkr_parts, cos_t, sin_t)


def _attn_tm_fwd(qf, qr, kvf, kr, *, name):
    t = qf.shape[0]
    nh, dn, dr, dv = MLA_HEADS, MLA_NOPE, MLA_ROPE, MLA_V
    blk = min(ATT_BLOCK, t)
    hb = ATT_HEAD_BATCH
    hs = range(hb)

    def body(q_ref, qr_ref, kv_ref, kr_ref, o_ref, l_ref):
        i = pl.program_id(1)
        qc = [jnp.concatenate([q_ref[:, h * dn:(h + 1) * dn].astype(MXU_DTYPE), qr_ref[:, h * dr:(h + 1) * dr]], axis=1)
              for h in hs]

        def step(j, carry, diagonal=False):
            m, l, acc = carry[:hb], carry[hb:2 * hb], carry[2 * hb:]
            rows = pl.ds(pl.multiple_of(j * blk, blk), blk)
            krj = kr_ref[rows, 0:dr]
            s = [_dotb(qc[h], jnp.concatenate([kv_ref[rows, h * (dn + dv):h * (dn + dv) + dn], krj], axis=1), NT)
                 for h in hs]
            s = [s[h] * ATT_SCALE for h in hs]
            if diagonal:
                mask = _diagonal_mask(blk)
                s = [jnp.where(mask, s[h], NEG) for h in hs]
            m_new = [jnp.maximum(m[h], jnp.max(s[h], axis=-1, keepdims=True)) for h in hs]
            p = [jnp.exp(s[h] - m_new[h]) for h in hs]
            pv = [_dotb(p[h], kv_ref[rows, h * (dn + dv) + dn:(h + 1) * (dn + dv)], NN) for h in hs]
            alpha = [jnp.exp(m[h] - m_new[h]) for h in hs]
            l = [alpha[h] * l[h] + jnp.sum(p[h], axis=-1, keepdims=True) for h in hs]
            acc = [alpha[h] * acc[h] + pv[h] for h in hs]
            return tuple(m_new) + tuple(l) + tuple(acc)

        init = ((jnp.full((blk, 1), NEG, F32),) * hb + (jnp.zeros((blk, 1), F32),) * hb
                + (jnp.zeros((blk, dv), F32),) * hb)
        out = step(i, lax.fori_loop(0, i, step, init), diagonal=True)
        for h in hs:
            m, l, acc = out[h], out[hb + h], out[2 * hb + h]
            o_ref[:, h * dv:(h + 1) * dv] = (acc / l).astype(o_ref.dtype)
            l_ref[h] = jnp.broadcast_to(m + jnp.log(l), (blk, LANES))

    return pl.pallas_call(
        body, name=name, grid=(nh // hb, t // blk),
        in_specs=[pl.BlockSpec((blk, hb * dn), lambda g, i: (i, g)), pl.BlockSpec((blk, hb * dr), lambda g, i: (i, g)),
                  pl.BlockSpec((t, hb * (dn + dv)), lambda g, i: (0, g)), pl.BlockSpec((t, LANES), lambda g, i: (0, 0))],
        out_specs=(pl.BlockSpec((blk, hb * dv), lambda g, i: (i, g)),
                   pl.BlockSpec((hb, blk, LANES), lambda g, i: (g, i, 0))),
        out_shape=(jax.ShapeDtypeStruct((t, nh * dv), BF16), jax.ShapeDtypeStruct((nh, t, LANES), F32)),
        compiler_params=_params("parallel", "parallel"),
    )(qf, qr, kvf, kr)


def _attn_tm_bwd(qf, qr, kvf, kr, o, lse, do, *, name):
    t = qf.shape[0]
    nh, dn, dr, dv = MLA_HEADS, MLA_NOPE, MLA_ROPE, MLA_V
    blk = min(ATT_BLOCK, t)
    nb = t // blk
    hb = ATT_HEAD_BATCH_BWD
    hs = range(hb)
    ng = nh // hb

    def body(q_ref, qr_ref, kv_ref, kr_ref, o_ref, l_ref, do_ref, dqn_ref, dqr_ref, dkv_ref, dkr_ref):
        j = pl.program_id(1)

        @pl.when(j == 0)
        def _():
            dqn_ref[...] = jnp.zeros_like(dqn_ref)
            dqr_ref[...] = jnp.zeros_like(dqr_ref)

        krj = kr_ref[:, 0:dr]
        kc = [jnp.concatenate([kv_ref[:, h * (dn + dv):h * (dn + dv) + dn], krj], axis=1) for h in hs]
        vv = [kv_ref[:, h * (dn + dv) + dn:(h + 1) * (dn + dv)] for h in hs]

        def step(i, carry, diagonal=False):
            dkn_acc, dv_acc, dkr_acc = carry[:hb], carry[hb:2 * hb], carry[2 * hb]
            rows = pl.ds(pl.multiple_of(i * blk, blk), blk)
            qc = [jnp.concatenate([q_ref[rows, h * dn:(h + 1) * dn].astype(MXU_DTYPE),
                                   qr_ref[rows, h * dr:(h + 1) * dr]], axis=1) for h in hs]
            dov = [do_ref[rows, h * dv:(h + 1) * dv] for h in hs]
            s = [_dotb(qc[h], kc[h], NT) for h in hs]
            dp = [_dotb(dov[h], vv[h], NT) for h in hs]
            s = [s[h] * ATT_SCALE for h in hs]
            if diagonal:
                mask = _diagonal_mask(blk)
                s = [jnp.where(mask, s[h], NEG) for h in hs]
            p = [jnp.exp(s[h] - l_ref[h, rows, :][:, 0:1]) for h in hs]
            delta = [jnp.sum(dov[h].astype(F32) * o_ref[rows, h * dv:(h + 1) * dv].astype(F32), axis=-1, keepdims=True)
                     for h in hs]
            ds = [p[h] * (dp[h] - delta[h]) * ATT_SCALE for h in hs]
            dvn = [_dotb(p[h], dov[h], TN) for h in hs]
            dkc = [_dotb(ds[h], qc[h], TN) for h in hs]
            dqc = [_dotb(ds[h], kc[h], NN) for h in hs]
            for h in hs:
                dqn_ref[rows, h * dn:(h + 1) * dn] += dqc[h][:, 0:dn]
                dqr_ref[rows, h * dr:(h + 1) * dr] += dqc[h][:, dn:dn + dr]
            dkr_new = dkr_acc
            for h in hs:
                dkr_new = dkr_new + dkc[h][:, dn:dn + dr]
            return (tuple(dkn_acc[h] + dkc[h][:, 0:dn] for h in hs) + tuple(dv_acc[h] + dvn[h] for h in hs)
                    + (dkr_new,))

        init = (jnp.zeros((blk, dn), F32),) * hb + (jnp.zeros((blk, dv), F32),) * hb + (jnp.zeros((blk, dr), F32),)
        out = lax.fori_loop(j + 1, nb, step, step(j, init, diagonal=True))
        for h in hs:
            dkv_ref[:, h * (dn + dv):h * (dn + dv) + dn] = out[h].astype(dkv_ref.dtype)
            dkv_ref[:, h * (dn + dv) + dn:(h + 1) * (dn + dv)] = out[hb + h].astype(dkv_ref.dtype)
        dkr_ref[0, :, 0:dr] = out[2 * hb]
        dkr_ref[0, :, dr:LANES] = jnp.zeros((blk, LANES - dr), F32)

    full = lambda w: pl.BlockSpec((t, w), lambda g, j: (0, g))
    return pl.pallas_call(
        body, name=name, grid=(ng, nb),
        in_specs=[full(hb * dn), full(hb * dr), pl.BlockSpec((blk, hb * (dn + dv)), lambda g, j: (j, g)),
                  pl.BlockSpec((blk, LANES), lambda g, j: (j, 0)), full(hb * dv),
                  pl.BlockSpec((hb, t, LANES), lambda g, j: (g, 0, 0)), full(hb * dv)],
        out_specs=(full(hb * dn), full(hb * dr), pl.BlockSpec((blk, hb * (dn + dv)), lambda g, j: (j, g)),
                   pl.BlockSpec((1, blk, LANES), lambda g, j: (g, j, 0))),
        out_shape=(jax.ShapeDtypeStruct((t, nh * dn), F32), jax.ShapeDtypeStruct((t, nh * dr), F32),
                   jax.ShapeDtypeStruct((t, nh * (dn + dv)), BF16), jax.ShapeDtypeStruct((ng, t, LANES), F32)),
        compiler_params=_params("parallel", "arbitrary"),
    )(qf, qr, kvf, kr, o, lse, do)


def _ada_mod(c_all, ada_w, ada_b_cols, *, name):
    nl, d, wc = ada_w.shape

    def body(c_ref, w_ref, b_ref, o_ref):
        cv = c_ref[...]
        o_ref[0] = _dotb(cv * _sigmoid(cv), w_ref[0], NN) + b_ref[0]

    return pl.pallas_call(
        body, name=name, grid=(nl,),
        in_specs=[_const_spec((N_DEV, d)), pl.BlockSpec((1, d, wc), lambda l: (l, 0, 0)),
                  pl.BlockSpec((1, 1, wc), lambda l: (l, 0, 0))],
        out_specs=pl.BlockSpec((1, N_DEV, wc), lambda l: (l, 0, 0)),
        out_shape=jax.ShapeDtypeStruct((nl, N_DEV, wc), F32), compiler_params=_params("parallel"),
    )(c_all, ada_w, ada_b_cols)


def _adam_math(g, w, m, v):
    m2 = ADAM_B1 * m + (1.0 - ADAM_B1) * g
    v2 = ADAM_B2 * v + (1.0 - ADAM_B2) * (g * g)
    delta = -ADAM_LR * ((m2 / ADAM_BC1) / (jnp.sqrt(v2 / ADAM_BC2) + ADAM_EPS) + ADAM_WD * w)
    return delta, m2, v2


def _ada_grad_adamw(c_all, dmod_cols, w, m, v, *, name):
    nl, d, wc = w.shape
    tr = 256

    def body(c_ref, dm_ref, w_ref, m_ref, v_ref, g_ref, d_ref, m2_ref, v2_ref):
        cv = c_ref[...]
        g = _dotf(cv * _sigmoid(cv), dm_ref[0], TN)
        delta, m2, v2 = _adam_math(g, w_ref[0], m_ref[0], v_ref[0])
        g_ref[0], d_ref[0], m2_ref[0], v2_ref[0] = g, delta, m2, v2

    blk = pl.BlockSpec((1, tr, wc), lambda l, i: (l, i, 0))
    return pl.pallas_call(
        body, name=name, grid=(nl, d // tr),
        in_specs=[pl.BlockSpec((N_DEV, tr), lambda l, i: (0, i)), pl.BlockSpec((1, N_DEV, wc), lambda l, i: (l, 0, 0)),
                  blk, blk, blk],
        out_specs=(blk,) * 4, out_shape=(jax.ShapeDtypeStruct(w.shape, F32),) * 4,
        compiler_params=_params("parallel", "parallel"),
    )(c_all, dmod_cols, w, m, v)


def _adamw(parts, w, m, v, *, name):
    nl, r, c = w.shape
    ns = parts[0].shape[0]
    lanes_padded = -(-c // LANES) * LANES
    row_bytes = 2 * nl * ns * lanes_padded * parts[0].dtype.itemsize
    tr = _pick(r, min(256, max(16, (VMEM_LIMIT // 2) // row_bytes)), 16)
    tc = c
    if tr * row_bytes > VMEM_LIMIT // 2:
        tc = _pick(c, max(LANES, c * (VMEM_LIMIT // 2) // (tr * row_bytes)))

    def body(*refs):
        p_refs = refs[:nl]
        w_ref, m_ref, v_ref, g_ref, d_ref, m2_ref, v2_ref = refs[nl:]
        layer = pl.program_id(0)
        for q in range(nl):
            @pl.when(layer == q)
            def _(q=q):
                g = p_refs[q][0].astype(F32)
                for s in range(1, ns):
                    g = g + p_refs[q][s].astype(F32)
                delta, m2, v2 = _adam_math(g, w_ref[0], m_ref[0], v_ref[0])
                g_ref[0], d_ref[0], m2_ref[0], v2_ref[0] = g, delta, m2, v2

    blk = pl.BlockSpec((1, tr, tc), lambda l, i, j: (l, i, j))
    p_specs = [pl.BlockSpec((ns, tr, tc), lambda l, i, j, q=q: (0, jnp.where(l == q, i, 0), jnp.where(l == q, j, 0)))
               for q in range(nl)]
    return pl.pallas_call(
        body, name=name, grid=(nl, r // tr, c // tc),
        in_specs=p_specs + [blk, blk, blk],
        out_specs=(blk,) * 4, out_shape=(jax.ShapeDtypeStruct(w.shape, F32),) * 4,
        compiler_params=_params("arbitrary", "arbitrary", "arbitrary"),
    )(*parts, w, m, v)


def _sum_parts(parts, *, name):
    ns, r, c = parts.shape

    def body(p_ref, o_ref):
        acc = p_ref[0]
        for s in range(1, ns):
            acc = acc + p_ref[s]
        o_ref[...] = acc

    return pl.pallas_call(
        body, name=name, out_shape=jax.ShapeDtypeStruct((r, c), F32),
        in_specs=[pl.BlockSpec(memory_space=pltpu.VMEM)], out_specs=pl.BlockSpec(memory_space=pltpu.VMEM),
    )(parts)


def _pack(arrs):
    flat = jnp.concatenate([a.reshape(-1).astype(F32) for a in arrs])
    pad = (-flat.shape[0]) % (8 * LANES)
    return jnp.pad(flat, (0, pad)).reshape(-1, LANES)


def _unpack(packed, shapes, lead=()):
    flat = packed.reshape(lead + (-1,))
    out, off = [], 0
    for s in shapes:
        n = math.prod(s)
        out.append(flat[..., off:off + n].reshape(lead + tuple(s)))
        off += n
    return out


def _gather_rows(g):
    _, nl, rs, c = g.shape
    return jnp.transpose(g, (1, 0, 2, 3)).reshape(nl, N_DEV * rs, c)


def _row(v):
    return v.reshape(1, -1)


def _local_step(x, target, mod, cos_t, sin_t, rep, get_weights, put_grads):
    t = x.shape[0]
    saved = []
    for layer in range(DEPTH):
        j = layer // 2
        tag = f"l{layer}"
        shift_m, scale_m, gate_m, shift_f, scale_f, gate_f = [_row(mod[layer, i]) for i in range(N_MOD)]
        lw = dict(get_weights(layer, "mix", x))
        rec = {"x0": x, "lw": lw}
        h = _adaln_fwd(x, _row(rep["norm_mix_g"][layer]), scale_m, shift_m, name=f"adaln_mix_{tag}")
        rec["h"] = h
        if layer % 2 == 0:
            proj = _mm(h, lw["wt_in"], mode="nt", out_dtype=F32, tm=256, tn=GDN_MAIN, b_rows=GDN_MAIN,
                       dep=lw["dep_mix"], name=f"gdn_in_{tag}")
            ab = _mm(h, lw["wt_ab"], mode="nt", out_dtype=F32, name=f"gdn_in_ab_{tag}")
            qkv = _gdn_prep_fwd(proj, rep["gdn_conv_wt"][j], name=f"gdn_prep_{tag}")
            gbeta = _gdn_gate_fwd(ab, rep["gdn_gate_prm"][j], name=f"gdn_gate_{tag}")
            o, states, tinvs = _gdn_chunk_fwd(qkv, gbeta, name=f"gdn_chunk_{tag}")
            og = _gdn_onorm_fwd(o, proj, _row(rep["gdn_norm_g"][j]), name=f"gdn_onorm_{tag}")
            x, y = _mm_resid(og, lw["w_out"], x, gate_m, name=f"gdn_out_{tag}")
            rec.update(proj=proj, ab=ab, qkv=qkv, gbeta=gbeta, states=states, tinvs=tinvs, o=o, og=og, y=y)
        else:
            proj = _mm(h, lw["w_in"], mode="nn", out_dtype=F32, dep=lw["dep_mix"], name=f"mla_in_{tag}")
            cq, ck = _mla_prep_fwd(proj, _row(rep["mla_q_norm_g"][j]), _row(rep["mla_kv_norm_g"][j]),
                                   name=f"mla_prep_{tag}")
            qf = _mm(cq, lw["wt_uq"], mode="nt", out_dtype=BF16, name=f"mla_uq_{tag}")
            kvf = _mm(ck, lw["w_ukv"], mode="nn", out_dtype=BF16, name=f"mla_ukv_{tag}")
            qr, kr = _rope_qk(qf, proj, cos_t, sin_t, name=f"rope_{tag}")
            oc, lse = _attn_tm_fwd(qf, qr, kvf, kr, name=f"attn_{tag}")
            x, y = _mm_resid(oc, lw["w_out"], x, gate_m, name=f"mla_out_{tag}")
            rec.update(proj=proj, cq=cq, ck=ck, qf=qf, qr=qr, kvf=kvf, kr=kr, lse=lse, oc=oc, y=y)
        rec["x1"] = x
        lw.update(get_weights(layer, "ffn", x))
        h2 = _adaln_fwd(x, _row(rep["norm_ffn_g"][layer]), scale_f, shift_f, name=f"adaln_ffn_{tag}")
        s, a2, b2 = _ffn_gu_fwd(h2, lw["wt_g"], lw["wt_u"], lw["dep_ffn"], name=f"ffn_gu_{tag}")
        x, y2 = _mm_resid(s, lw["w_down"], x, gate_f, tm=512, name=f"ffn_down_{tag}")
        rec.update(h2=h2, a2=a2, b2=b2, s=s, y2=y2)
        saved.append(rec)

    dx, st, ls = _loss_head(x, _row(rep["final_norm_g"]), target, name="loss_head")
    loss = ls[0, 0]
    grads = {"final_norm_g": st[0]}
    per_layer = {k: [None] * DEPTH for k in ("norm_mix_g", "norm_ffn_g")}
    per_gdn = {k: [None] * 2 for k in ("gdn_conv_wt", "gdn_a_log", "gdn_dt_bias", "gdn_norm_g")}
    per_mla = {k: [None] * 2 for k in ("mla_q_norm_g", "mla_kv_norm_g")}
    dmod = [None] * DEPTH
    dep = jnp.zeros((8, LANES), F32)

    for layer in reversed(range(DEPTH)):
        j = layer // 2
        tag = f"l{layer}"
        rec = saved[layer]
        lw = rec["lw"]
        shift_m, scale_m, gate_m, shift_f, scale_f, gate_f = [_row(mod[layer, i]) for i in range(N_MOD)]
        if layer == DEPTH - 1:
            dy2, st_g = _gate_bwd(dx, rec["y2"], gate_f, dep, name=f"gate_bwd_ffn_{tag}")
            dgate_f = st_g[0]
        dw_down = _mm(rec["s"], dy2, mode="tn", out_dtype=BF16, tm=FFN_BLOCK, tn=1024, name=f"ffn_down_dw_{tag}")
        da2, db2 = _ffn_down_dx(dy2, lw["w_down"], rec["a2"], rec["b2"], name=f"ffn_down_dx_{tag}")
        dwt_g = _mm(da2, rec["h2"], mode="tn", out_dtype=BF16, tm=FFN_BLOCK, tn=1024, name=f"ffn_g_dw_{tag}")
        dwt_u = _mm(db2, rec["h2"], mode="tn", out_dtype=BF16, tm=FFN_BLOCK, tn=1024, name=f"ffn_u_dw_{tag}")
        dep = put_grads(layer, "ffn", {"wt_g": dwt_g, "wt_u": dwt_u, "w_down": dw_down})
        dh2 = _mm(da2, lw["wt_g"], mode="nn", out_dtype=F32, tm=512, tn=1024, name=f"ffn_g_dx_{tag}")
        dh2 = _mm(db2, lw["wt_u"], mode="nn", out_dtype=BF16, add=dh2, tm=512, tn=1024, name=f"ffn_u_dx_{tag}")
        dx, st_n, dy = _adaln_gate_bwd(rec["x1"], _row(rep["norm_ffn_g"][layer]), scale_f, shift_f, dh2, dx, dep,
                                       rec["y"], gate_m, name=f"adaln_ffn_bwd_{tag}")
        per_layer["norm_ffn_g"][layer] = st_n[0]
        dscale_f, dshift_f, dgate_m = st_n[1], st_n[2], st_n[3]
        big = {}
        if layer % 2 == 0:
            big["w_out"] = _mm(rec["og"], dy, mode="tn", out_dtype=BF16, name=f"gdn_out_dw_{tag}")
            dog = _mm(dy, lw["w_out"], mode="nt", out_dtype=BF16, name=f"gdn_out_dx_{tag}")
            do, dgp, st_o = _gdn_onorm_bwd(rec["o"], rec["proj"], _row(rep["gdn_norm_g"][j]), dog,
                                           name=f"gdn_onorm_bwd_{tag}")
            per_gdn["gdn_norm_g"][j] = st_o[0]
            dqkv, dgb = _gdn_chunk_bwd(rec["qkv"], rec["gbeta"], rec["states"], rec["tinvs"], do,
                                       name=f"gdn_chunk_bwd_{tag}")
            dab, st_a = _gdn_gate_bwd(rec["ab"], rep["gdn_gate_prm"][j], dgb, name=f"gdn_gate_bwd_{tag}")
            per_gdn["gdn_a_log"][j] = st_a[0, :GDN_HEADS]
            per_gdn["gdn_dt_bias"][j] = st_a[1, :GDN_HEADS]
            dpre, dcw = _gdn_prep_bwd(rec["proj"], rep["gdn_conv_wt"][j], dqkv, name=f"gdn_prep_bwd_{tag}")
            per_gdn["gdn_conv_wt"][j] = dcw
            dproj = jnp.concatenate([dpre, dgp], axis=1)
            dw_main = _mm(dproj, rec["h"], mode="tn", out_dtype=BF16, tm=512, tn=1024, name=f"gdn_in_dw_{tag}")
            dw_ab = _mm(dab, rec["h"], mode="tn", out_dtype=BF16, tn=1024, name=f"gdn_in_ab_dw_{tag}")
            big["wt_in"] = jnp.concatenate([dw_main, dw_ab[:2 * GDN_HEADS]], axis=0)
            dep = put_grads(layer, "gdn", big)
            dh_ab = _mm(dab, lw["wt_ab"], mode="nn", out_dtype=F32, tn=1024, name=f"gdn_in_ab_dx_{tag}")
            dh = _mm(dproj, lw["wt_in"], mode="nn", out_dtype=BF16, add=dh_ab, tm=256, tn=1024, b_rows=GDN_MAIN,
                     name=f"gdn_in_dx_{tag}")
        else:
            big["w_out"] = _mm(rec["oc"], dy, mode="tn", out_dtype=BF16, name=f"mla_out_dw_{tag}")
            doc = _mm(dy, lw["w_out"], mode="nt", out_dtype=BF16, name=f"mla_out_dx_{tag}")
            dqn, dqr, dkvf, dkr_parts = _attn_tm_bwd(rec["qf"], rec["qr"], rec["kvf"], rec["kr"], rec["oc"],
                                                     rec["lse"], doc, name=f"attn_bwd_{tag}")
            dqr_un, dkr_un = _rope_qk_bwd(dqr, dkr_parts, cos_t, sin_t, name=f"rope_bwd_{tag}")
            n_nope = MLA_HEADS * MLA_NOPE
            big["wt_uq"] = jnp.concatenate(
                [_mm(dqn, rec["cq"], mode="tn", out_dtype=BF16, name=f"mla_uq_dw_nope_{tag}"),
                 _mm(dqr_un, rec["cq"], mode="tn", out_dtype=BF16, name=f"mla_uq_dw_rope_{tag}")], axis=0)
            big["w_ukv"] = _mm(rec["ck"], dkvf, mode="tn", out_dtype=BF16, name=f"mla_ukv_dw_{tag}")
            dcq = _mm(dqr_un, lw["wt_uq"][n_nope:], mode="nn", out_dtype=F32, name=f"mla_uq_dx_rope_{tag}")
            dcq = _mm(dqn, lw["wt_uq"], mode="nn", out_dtype=F32, add=dcq, b_rows=n_nope,
                      name=f"mla_uq_dx_nope_{tag}")
            dck = _mm(dkvf, lw["w_ukv"], mode="nt", out_dtype=F32, name=f"mla_ukv_dx_{tag}")
            dproj, st_p = _mla_prep_bwd(rec["proj"], _row(rep["mla_q_norm_g"][j]), _row(rep["mla_kv_norm_g"][j]),
                                        dcq, dck, dkr_un, name=f"mla_prep_bwd_{tag}")
            per_mla["mla_q_norm_g"][j] = st_p[0, :MLA_Q_RANK]
            per_mla["mla_kv_norm_g"][j] = st_p[0, MLA_Q_RANK:MLA_Q_RANK + MLA_KV_RANK]
            big["w_in"] = _mm(rec["h"], dproj, mode="tn", out_dtype=BF16, name=f"mla_in_dw_{tag}")
            dep = put_grads(layer, "mla", big)
            dh = _mm(dproj, lw["w_in"], mode="nt", out_dtype=BF16, name=f"mla_in_dx_{tag}")
        if layer > 0:
            below = saved[layer - 1]
            dx, st_n, dy2 = _adaln_gate_bwd(rec["x0"], _row(rep["norm_mix_g"][layer]), scale_m, shift_m, dh, dx, dep,
                                            below["y2"], _row(mod[layer - 1, N_MOD - 1]),
                                            name=f"adaln_mix_bwd_{tag}")
        else:
            dx, st_n = _adaln_bwd(rec["x0"], _row(rep["norm_mix_g"][layer]), scale_m, shift_m, dh, dx, dep,
                                  name=f"adaln_mix_bwd_{tag}")
        per_layer["norm_mix_g"][layer] = st_n[0]
        dmod[layer] = jnp.stack([st_n[2], st_n[1], dgate_m, dshift_f, dscale_f, dgate_f])
        if layer > 0:
            dgate_f = st_n[3]

    for d in (per_layer, per_gdn, per_mla):
        for k, v in d.items():
            grads[k] = jnp.stack(v)
    return loss, dx, jnp.stack(dmod), grads


BIG = ("gdn_w_in", "gdn_w_out", "mla_w_in", "mla_w_uq", "mla_w_ukv", "mla_w_out", "ffn_w_gate", "ffn_w_up",
       "ffn_w_down")
TRANSPOSED = ("gdn_w_in", "mla_w_uq", "ffn_w_gate", "ffn_w_up")
AHEAD = 3


def _view(k, a):
    return jnp.transpose(a, (0, 2, 1)) if k in TRANSPOSED else a
SMALL = ("ada_b", "norm_mix_g", "norm_ffn_g", "gdn_conv_w", "gdn_a_log", "gdn_dt_bias", "gdn_norm_g",
         "mla_q_norm_g", "mla_kv_norm_g", "final_norm_g")
WEIGHTS = ("ada_w", "ada_b", "norm_mix_g", "norm_ffn_g", "gdn_w_in", "gdn_conv_w", "gdn_a_log", "gdn_dt_bias",
           "gdn_norm_g", "gdn_w_out", "mla_w_in", "mla_q_norm_g", "mla_kv_norm_g", "mla_w_uq", "mla_w_ukv",
           "mla_w_out", "ffn_w_gate", "ffn_w_up", "ffn_w_down", "final_norm_g")


def _uq_to_kernel_layout(w, axis=-1):
    axis = axis % w.ndim
    lead, tail = w.shape[:axis], w.shape[axis + 1:]
    w4 = w.reshape(lead + (MLA_HEADS, MLA_QK) + tail)
    nope = lax.slice_in_dim(w4, 0, MLA_NOPE, axis=axis + 1).reshape(lead + (-1,) + tail)
    rope = lax.slice_in_dim(w4, MLA_NOPE, MLA_QK, axis=axis + 1).reshape(lead + (-1,) + tail)
    return jnp.concatenate([nope, rope], axis=axis)


def _uq_from_kernel_layout(w, axis=-1):
    axis = axis % w.ndim
    lead, tail = w.shape[:axis], w.shape[axis + 1:]
    nope = lax.slice_in_dim(w, 0, MLA_HEADS * MLA_NOPE, axis=axis).reshape(lead + (MLA_HEADS, MLA_NOPE) + tail)
    rope = lax.slice_in_dim(w, MLA_HEADS * MLA_NOPE, MLA_HEADS * MLA_QK, axis=axis).reshape(
        lead + (MLA_HEADS, MLA_ROPE) + tail)
    return jnp.concatenate([nope, rope], axis=axis + 1).reshape(lead + (-1,) + tail)


def _group_names(layer, kind):
    if kind == "ffn":
        return ("ffn_w_gate", "ffn_w_up", "ffn_w_down")
    return ("gdn_w_in", "gdn_w_out") if layer % 2 == 0 else ("mla_w_in", "mla_w_uq", "mla_w_ukv", "mla_w_out")


def _layer_index(name, layer):
    return layer if name.startswith("ffn") else layer // 2


def _cols(g):
    return jnp.transpose(g, (1, 0, 2)).reshape(g.shape[1], N_DEV * g.shape[2])


def _rows(g):
    return g.reshape(N_DEV * g.shape[1], g.shape[2])


def _uncols(full):
    r, c = full.shape
    return jnp.transpose(full.reshape(r, N_DEV, c // N_DEV), (1, 0, 2))


def _unrows(full):
    r, c = full.shape
    return full.reshape(N_DEV, r // N_DEV, c)


def _group_weights(layer, kind, got, token):
    if kind == "ffn":
        return {"wt_g": _rows(got["ffn_w_gate"]), "wt_u": _rows(got["ffn_w_up"]), "w_down": _rows(got["ffn_w_down"]),
                "dep_ffn": token}
    if layer % 2 == 0:
        wt_in = _rows(got["gdn_w_in"])
        return dict(wt_in=wt_in, wt_ab=jnp.pad(wt_in[GDN_MAIN:], ((0, LANES - 2 * GDN_HEADS), (0, 0))),
                    w_out=_rows(got["gdn_w_out"]), dep_mix=token)
    return dict(w_in=_rows(got["mla_w_in"]), wt_uq=_uq_to_kernel_layout(_rows(got["mla_w_uq"]), axis=0),
                w_ukv=_cols(got["mla_w_ukv"]), w_out=_rows(got["mla_w_out"]), dep_mix=token)


def _layer_grad_slots(kind, big):
    if kind == "ffn":
        return {"ffn_w_gate": _unrows(big["wt_g"]), "ffn_w_up": _unrows(big["wt_u"]),
                "ffn_w_down": _unrows(big["w_down"])}
    if kind == "gdn":
        return {"gdn_w_in": _unrows(big["wt_in"]), "gdn_w_out": _unrows(big["w_out"])}
    return {"mla_w_in": _unrows(big["w_in"]), "mla_w_uq": _unrows(_uq_from_kernel_layout(big["wt_uq"], axis=0)),
            "mla_w_ukv": _uncols(big["w_ukv"]), "mla_w_out": _unrows(big["w_out"])}


def _small_weights(tiny, rep):
    prm = jnp.zeros((2, 8, LANES), F32)
    prm = prm.at[:, 0, :GDN_HEADS].set(rep["gdn_a_log"]).at[:, 1, :GDN_HEADS].set(rep["gdn_dt_bias"])
    out = {
        "gdn_conv_wt": jnp.transpose(_gather_rows(tiny["gdn_conv_w"]), (0, 2, 1)),
        "mla_q_norm_g": jnp.transpose(tiny["mla_q_norm_g"], (1, 0, 2)).reshape(2, MLA_Q_RANK),
        "mla_kv_norm_g": jnp.transpose(tiny["mla_kv_norm_g"], (1, 0, 2)).reshape(2, MLA_KV_RANK),
        "gdn_gate_prm": prm,
    }
    for k in ("norm_mix_g", "norm_ffn_g", "gdn_norm_g", "final_norm_g"):
        out[k] = rep[k]
    return out


def _rope_tables(positions):
    inv_freq = ROPE_THETA ** (-jnp.arange(0, MLA_ROPE, 2, dtype=F32) / MLA_ROPE)
    ang = positions.astype(F32)[:, None] * inv_freq
    cos, sin = jnp.cos(ang), jnp.sin(ang)
    reps = LANES // MLA_ROPE
    return jnp.tile(jnp.concatenate([cos, cos], axis=1), (1, reps)), jnp.tile(
        jnp.concatenate([-sin, sin], axis=1), (1, reps))


def kernel(x, c, positions, ada_w, ada_b, norm_mix_g, norm_ffn_g, gdn_w_in, gdn_conv_w, gdn_a_log, gdn_dt_bias, gdn_norm_g, gdn_w_out, mla_w_in, mla_q_norm_g, mla_kv_norm_g, mla_w_uq, mla_w_ukv, mla_w_out, ffn_w_gate, ffn_w_up, ffn_w_down, final_norm_g, loss_target, m_ada_w, m_ada_b, m_norm_mix_g, m_norm_ffn_g, m_gdn_w_in, m_gdn_conv_w, m_gdn_a_log, m_gdn_dt_bias, m_gdn_norm_g, m_gdn_w_out, m_mla_w_in, m_mla_q_norm_g, m_mla_kv_norm_g, m_mla_w_uq, m_mla_w_ukv, m_mla_w_out, m_ffn_w_gate, m_ffn_w_up, m_ffn_w_down, m_final_norm_g, v_ada_w, v_ada_b, v_norm_mix_g, v_norm_ffn_g, v_gdn_w_in, v_gdn_conv_w, v_gdn_a_log, v_gdn_dt_bias, v_gdn_norm_g, v_gdn_w_out, v_mla_w_in, v_mla_q_norm_g, v_mla_kv_norm_g, v_mla_w_uq, v_mla_w_ukv, v_mla_w_out, v_ffn_w_gate, v_ffn_w_up, v_ffn_w_down, v_final_norm_g):
    W = dict(ada_w=ada_w, ada_b=ada_b, norm_mix_g=norm_mix_g, norm_ffn_g=norm_ffn_g, gdn_w_in=gdn_w_in,
             gdn_conv_w=gdn_conv_w, gdn_a_log=gdn_a_log, gdn_dt_bias=gdn_dt_bias, gdn_norm_g=gdn_norm_g,
             gdn_w_out=gdn_w_out, mla_w_in=mla_w_in, mla_q_norm_g=mla_q_norm_g, mla_kv_norm_g=mla_kv_norm_g,
             mla_w_uq=mla_w_uq, mla_w_ukv=mla_w_ukv, mla_w_out=mla_w_out, ffn_w_gate=ffn_w_gate,
             ffn_w_up=ffn_w_up, ffn_w_down=ffn_w_down, final_norm_g=final_norm_g)
    M = dict(ada_w=m_ada_w, ada_b=m_ada_b, norm_mix_g=m_norm_mix_g, norm_ffn_g=m_norm_ffn_g, gdn_w_in=m_gdn_w_in,
             gdn_conv_w=m_gdn_conv_w, gdn_a_log=m_gdn_a_log, gdn_dt_bias=m_gdn_dt_bias, gdn_norm_g=m_gdn_norm_g,
             gdn_w_out=m_gdn_w_out, mla_w_in=m_mla_w_in, mla_q_norm_g=m_mla_q_norm_g,
             mla_kv_norm_g=m_mla_kv_norm_g, mla_w_uq=m_mla_w_uq, mla_w_ukv=m_mla_w_ukv, mla_w_out=m_mla_w_out,
             ffn_w_gate=m_ffn_w_gate, ffn_w_up=m_ffn_w_up, ffn_w_down=m_ffn_w_down, final_norm_g=m_final_norm_g)
    V = dict(ada_w=v_ada_w, ada_b=v_ada_b, norm_mix_g=v_norm_mix_g, norm_ffn_g=v_norm_ffn_g, gdn_w_in=v_gdn_w_in,
             gdn_conv_w=v_gdn_conv_w, gdn_a_log=v_gdn_a_log, gdn_dt_bias=v_gdn_dt_bias, gdn_norm_g=v_gdn_norm_g,
             gdn_w_out=v_gdn_w_out, mla_w_in=v_mla_w_in, mla_q_norm_g=v_mla_q_norm_g,
             mla_kv_norm_g=v_mla_kv_norm_g, mla_w_uq=v_mla_w_uq, mla_w_ukv=v_mla_w_ukv, mla_w_out=v_mla_w_out,
             ffn_w_gate=v_ffn_w_gate, ffn_w_up=v_ffn_w_up, ffn_w_down=v_ffn_w_down, final_norm_g=v_final_norm_g)
    me = 4 * lax.axis_index("x") + 2 * lax.axis_index("y") + lax.axis_index("c")
    t = x.shape[1]
    wc = ada_w.shape[-1]

    groups = [(layer, kind) for layer in range(DEPTH) for kind in ("mix", "ffn")]

    def group_srcs(i):
        layer, kind = groups[i]
        return [_view(k, W[k])[_layer_index(k, layer)].astype(BF16) for k in _group_names(layer, kind)]

    tiny_shapes = [c.shape, gdn_conv_w.shape, mla_q_norm_g.shape, mla_kv_norm_g.shape]
    first = _gather_two_level([_pack([c, gdn_conv_w, mla_q_norm_g, mla_kv_norm_g])] + group_srcs(0),
                              name="gather_first")
    tiny_g = first[0]
    c_g, conv_g, qn_g, kvn_g = _unpack(tiny_g, tiny_shapes, lead=(N_DEV,))
    c_all = c_g.reshape(N_DEV, D_MODEL)
    rep = _small_weights({"gdn_conv_w": conv_g, "mla_q_norm_g": qn_g, "mla_kv_norm_g": kvn_g}, W)

    def start_group(i, dep):
        layer, kind = groups[i]
        return _exchange_start(group_srcs(i), scatter=False, name=f"gather_start_{kind}_l{layer}", dep=dep)


    b_cols = lax.dynamic_slice_in_dim(ada_b, me * wc, wc, axis=1).reshape(DEPTH, 1, wc)
    mod_part = _ada_mod(c_all, ada_w, b_cols, name="ada_mod")
    (mod_g,) = _exchange([mod_part], scatter=False, name="gather_mod")
    mod_mine = lax.dynamic_index_in_dim(mod_g, me, axis=2, keepdims=False)
    mod = jnp.transpose(mod_mine, (1, 0, 2)).reshape(DEPTH, N_MOD, D_MODEL)
    gather = {1: start_group(1, mod_g)}
    for i in range(2, AHEAD + 1):
        gather[i] = start_group(i, gather[i - 1][4])

    def get_weights(layer, kind, after):
        i = groups.index((layer, kind))
        names = _group_names(layer, kind)
        if i == 0:
            return _group_weights(layer, kind, dict(zip(names, first[1:])), gather[AHEAD][4])
        srcs, lands = _exchange_wait(gather[i], after, scatter=False, name=f"gather_wait_{kind}_l{layer}")
        token = jnp.zeros((8, LANES), F32)
        if i + AHEAD < len(groups):
            gather[i + AHEAD] = start_group(i + AHEAD, lands[0])
            token = gather[i + AHEAD][4]
        got = {k: lax.dynamic_update_index_in_dim(z, s, me, 0) for k, s, z in zip(names, srcs, lands)}
        return _group_weights(layer, kind, got, token)

    scatter = []

    def put_grads(layer, kind, big):
        slots = _layer_grad_slots(kind, big)
        started = _exchange_start(list(slots.values()), scatter=True, name=f"scatter_start_{kind}_l{layer}")
        scatter.append((layer, kind, list(slots.keys()), started))
        return started[4]

    cos_t, sin_t = _rope_tables(positions[0])
    loss, dx, dmod, g = _local_step(x[0], loss_target[0], mod, cos_t, sin_t, rep, get_weights, put_grads)

    parts = {k: [None] * W[k].shape[0] for k in BIG}
    res = {}

    def wait_group(entry, after):
        layer, kind, names, started = entry
        srcs, lands = _exchange_wait(started, after, scatter=True, name=f"scatter_wait_{kind}_l{layer}")
        for k, s, z in zip(names, srcs, lands):
            own = lax.dynamic_index_in_dim(s, me, 0, keepdims=False)
            parts[k][_layer_index(k, layer)] = lax.dynamic_update_index_in_dim(z, own, me, 0)

    for entry in scatter[:-1]:
        wait_group(entry, dx)
    early = [k for k in BIG if k not in scatter[-1][2]]
    def update(k):
        outs = _adamw(parts[k], _view(k, W[k]), _view(k, M[k]), _view(k, V[k]), name=f"adamw_{k}")
        return tuple(_view(k, o) for o in outs)

    for k in early:
        res[k] = update(k)
    loss, dmod, done = lax.optimization_barrier((loss, dmod, [res[k] for k in early]))
    for k, r in zip(early, done):
        res[k] = r

    small_local = [dmod.reshape(DEPTH, N_MOD * D_MODEL), g["norm_mix_g"], g["norm_ffn_g"],
                   jnp.transpose(g["gdn_conv_wt"], (0, 2, 1)), g["gdn_a_log"], g["gdn_dt_bias"], g["gdn_norm_g"],
                   g["mla_q_norm_g"], g["mla_kv_norm_g"], g["final_norm_g"], loss.reshape(1)]
    small_shapes = [a.shape for a in small_local]
    (small_g,) = _exchange([_pack(small_local)], scatter=False, name="gather_small_grads")
    small_sum = _unpack(_sum_parts(small_g, name="sum_small_grads"), small_shapes)
    loss = small_sum[-1][0]
    dmod_all = _unpack(small_g, small_shapes[:1], lead=(N_DEV,))[0]
    sg = dict(zip(SMALL, small_sum))
    wait_group(scatter[-1], small_g)
    sg["gdn_conv_w"] = lax.dynamic_slice_in_dim(sg["gdn_conv_w"], me * gdn_conv_w.shape[1], gdn_conv_w.shape[1], 1)
    sg["mla_q_norm_g"] = lax.dynamic_slice_in_dim(sg["mla_q_norm_g"], me * mla_q_norm_g.shape[1],
                                                  mla_q_norm_g.shape[1], 1)
    sg["mla_kv_norm_g"] = lax.dynamic_slice_in_dim(sg["mla_kv_norm_g"], me * mla_kv_norm_g.shape[1],
                                                   mla_kv_norm_g.shape[1], 1)

    dmod_cols = jnp.transpose(lax.dynamic_slice_in_dim(dmod_all, me * wc, wc, axis=2), (1, 0, 2))
    res["ada_w"] = _ada_grad_adamw(c_all, dmod_cols, ada_w, m_ada_w, v_ada_w, name="ada_w_grad_adamw")
    for k in BIG:
        if k not in early:
            res[k] = update(k)
    shapes = [W[k].shape for k in SMALL]
    packed = [_pack([d[k] for k in SMALL]) for d in (sg, W, M, V)]
    outs = _adamw([packed[0][None]], packed[1][None], packed[2][None], packed[3][None], name="adamw_small")
    unpacked = [_unpack(o[0], shapes) for o in outs]
    for i, k in enumerate(SMALL):
        res[k] = tuple(u[i] for u in unpacked)

    return (loss, dx[None], *[res[k][0] for k in WEIGHTS], *[res[k][1] for k in WEIGHTS],
            *[res[k][2] for k in WEIGHTS], *[res[k][3] for k in WEIGHTS])
```

```python
import math

import jax
import jax.numpy as jnp
from jax import lax
from jax.experimental import pallas as pl
from jax.experimental.pallas import tpu as pltpu

F32 = jnp.float32
BF16 = jnp.bfloat16
MXU_DTYPE = jnp.bfloat16

N_DEV = 8
D_MODEL = 1024
DEPTH = 4
GDN_HEADS = 8
GDN_HEAD_DIM = 128
GDN_KEY_DIM = GDN_HEADS * GDN_HEAD_DIM
GDN_CHUNK = 64
GDN_HEAD_BATCH = 8
GDN_CONV = 4
GDN_PREP_HEADS = 2
GDN_MAIN = 4 * GDN_KEY_DIM
MLA_HEADS = 8
MLA_NOPE = 128
MLA_ROPE = 64
MLA_V = 128
MLA_Q_RANK = 384
MLA_KV_RANK = 256
MLA_IN = MLA_Q_RANK + MLA_KV_RANK + MLA_ROPE
MLA_QK = MLA_NOPE + MLA_ROPE
ROPE_THETA = 10000.0
D_FF = 2816
N_MOD = 6
EPS = 1e-6
LANES = 128
VMEM_LIMIT = 48 * 1024 * 1024

ADAM_LR = 0.001
ADAM_B1 = 0.9
ADAM_B2 = 0.999
ADAM_EPS = 1e-08
ADAM_WD = 0.01
ADAM_STEP = 10
ADAM_BC1 = 1.0 - ADAM_B1 ** ADAM_STEP
ADAM_BC2 = 1.0 - ADAM_B2 ** ADAM_STEP

NN = (((1,), (0,)), ((), ()))
NT = (((1,), (1,)), ((), ()))
TN = (((0,), (0,)), ((), ()))
NEG = -1e30


def _dotb(a, b, dims):
    return lax.dot_general(a.astype(MXU_DTYPE), b.astype(MXU_DTYPE), dims, preferred_element_type=F32)


def _split(a):
    hi = a.astype(BF16)
    return hi, (a - hi.astype(F32)).astype(BF16)


def _dotf(a, b, dims):
    ah, al = _split(a)
    bh, bl = _split(b)
    dot = lambda u, v: lax.dot_general(u, v, dims, preferred_element_type=F32)
    return dot(ah, bh) + (dot(ah, bl) + dot(al, bh))


def _params(*sem):
    return pltpu.CompilerParams(dimension_semantics=sem, vmem_limit_bytes=VMEM_LIMIT)


def _pick(n, pref, mult=LANES):
    best = None
    t = mult
    while t <= min(n, pref):
        if n % t == 0:
            best = t
        t += mult
    return best if best is not None else n


def _sigmoid(z):
    return 0.5 * jnp.tanh(0.5 * z) + 0.5


def _exchange(arrays, *, scatter, name):
    n = len(arrays)
    out_shape = tuple(
        jax.ShapeDtypeStruct(a.shape if scatter else (N_DEV,) + a.shape, a.dtype) for a in arrays)

    def body(*refs):
        ins, outs = refs[:n], refs[n:2 * n]
        send_sems, recv_sems, local_sems = refs[2 * n:]
        x, y, c = lax.axis_index("x"), lax.axis_index("y"), lax.axis_index("c")
        me = 4 * x + 2 * y + c
        copies = []
        for k in range(n):
            src_own = ins[k].at[me] if scatter else ins[k]
            own = pltpu.make_async_copy(src_own, outs[k].at[me], local_sems.at[k])
            own.start()
            copies.append(own)
        sends = []
        for p in range(1, N_DEV):
            px, py, pc = x ^ ((p >> 2) & 1), y ^ ((p >> 1) & 1), c ^ (p & 1)
            peer = 4 * px + 2 * py + pc
            for k in range(n):
                cp = pltpu.make_async_remote_copy(
                    src_ref=ins[k].at[peer] if scatter else ins[k],
                    dst_ref=outs[k].at[me],
                    send_sem=send_sems.at[k, p - 1],
                    recv_sem=recv_sems.at[k, p - 1],
                    device_id=(px, py, pc),
                    device_id_type=pl.DeviceIdType.MESH,
                )
                cp.start()
                sends.append((cp, k, peer, p))
        for cp, k, peer, p in sends:
            pltpu.make_async_remote_copy(
                src_ref=ins[k].at[peer] if scatter else ins[k],
                dst_ref=outs[k].at[peer],
                send_sem=send_sems.at[k, p - 1],
                recv_sem=recv_sems.at[k, p - 1],
                device_id=(x, y, c),
                device_id_type=pl.DeviceIdType.MESH,
            ).wait_recv()
        for cp, _, _, _ in sends:
            cp.wait_send()
        for own in copies:
            own.wait()

    any_spec = pl.BlockSpec(memory_space=pl.ANY)
    outs = pl.pallas_call(
        body,
        name=name,
        out_shape=out_shape,
        in_specs=[any_spec] * n,
        out_specs=tuple([any_spec] * n),
        scratch_shapes=[
            pltpu.SemaphoreType.DMA((n, N_DEV - 1)),
            pltpu.SemaphoreType.DMA((n, N_DEV - 1)),
            pltpu.SemaphoreType.DMA((n,)),
        ],
        compiler_params=pltpu.CompilerParams(has_side_effects=True),
    )(*arrays)
    return list(outs)


def _gather_two_level(arrays, *, name):
    n = len(arrays)
    out_shape = tuple(jax.ShapeDtypeStruct((N_DEV,) + a.shape, a.dtype) for a in arrays)

    def body(*refs):
        ins, outs = refs[:n], refs[n:2 * n]
        send_sems, recv_sems, local_sems = refs[2 * n:]
        x, y, c = lax.axis_index("x"), lax.axis_index("y"), lax.axis_index("c")
        me = 4 * x + 2 * y + c
        sibling = (x, y, 1 - c)
        chips = [(1 - x, y), (x, 1 - y), (1 - x, 1 - y)]

        def slot(px, py, pc):
            return 4 * px + 2 * py + pc

        def copy(k, q, block, to, src=None):
            return pltpu.make_async_remote_copy(
                src_ref=outs[k].at[slot(*block)] if src is None else src,
                dst_ref=outs[k].at[slot(*block)],
                send_sem=send_sems.at[k, q], recv_sem=recv_sems.at[k, q],
                device_id=to, device_id_type=pl.DeviceIdType.MESH)

        own = [pltpu.make_async_copy(ins[k], outs[k].at[me], local_sems.at[k]) for k in range(n)]
        for cp in own:
            cp.start()
        first = []
        for k in range(n):
            first.append(copy(k, 0, (x, y, c), sibling, src=ins[k]))
            first += [copy(k, 1 + j, (x, y, c), (*chip, c), src=ins[k]) for j, chip in enumerate(chips)]
        for cp in first:
            cp.start()
        passed = []
        for j, chip in enumerate(chips):
            for k in range(n):
                copy(k, 1 + j, (*chip, c), (x, y, c)).wait_recv()
                fwd = copy(k, 4 + j, (*chip, c), sibling)
                fwd.start()
                passed.append(fwd)
        for k in range(n):
            copy(k, 0, sibling, (x, y, c)).wait_recv()
            for j, chip in enumerate(chips):
                copy(k, 4 + j, (*chip, 1 - c), (x, y, c)).wait_recv()
        for cp in first + passed:
            cp.wait_send()
        for cp in own:
            cp.wait()

    any_spec = pl.BlockSpec(memory_space=pl.ANY)
    outs = pl.pallas_call(
        body, name=name, out_shape=out_shape, in_specs=[any_spec] * n, out_specs=tuple([any_spec] * n),
        scratch_shapes=[pltpu.SemaphoreType.DMA((n, N_DEV - 1)), pltpu.SemaphoreType.DMA((n, N_DEV - 1)),
                        pltpu.SemaphoreType.DMA((n,))],
        compiler_params=pltpu.CompilerParams(has_side_effects=True),
    )(*arrays)
    return list(outs)


def _peer(x, y, c, p):
    return x ^ ((p >> 2) & 1), y ^ ((p >> 1) & 1), c ^ (p & 1)


def _exchange_start(arrays, *, scatter, name, dep=None):
    n = len(arrays)
    deps = [] if dep is None else [dep]
    lands = [lax.empty(a.shape if scatter else (N_DEV,) + a.shape, a.dtype) for a in arrays]

    def body(*refs):
        ins, zones = refs[:n], refs[n:2 * n]
        send_sems, recv_sems = refs[2 * n + len(deps)], refs[2 * n + len(deps) + 1]
        token = refs[-1]
        x, y, c = lax.axis_index("x"), lax.axis_index("y"), lax.axis_index("c")
        me = 4 * x + 2 * y + c
        for p in range(1, N_DEV):
            px, py, pc = _peer(x, y, c, p)
            for k in range(n):
                pltpu.make_async_remote_copy(
                    src_ref=ins[k].at[4 * px + 2 * py + pc] if scatter else ins[k],
                    dst_ref=zones[k].at[me],
                    send_sem=send_sems.at[k * (N_DEV - 1) + p - 1],
                    recv_sem=recv_sems.at[k * (N_DEV - 1) + p - 1],
                    device_id=(px, py, pc),
                    device_id_type=pl.DeviceIdType.MESH,
                ).start()
        token[...] = jnp.zeros_like(token)

    hbm = pl.BlockSpec(memory_space=pltpu.HBM)
    sem = pl.BlockSpec(memory_space=pltpu.SEMAPHORE)
    outs = pl.pallas_call(
        body,
        name=name,
        out_shape=(pltpu.SemaphoreType.DMA((n * (N_DEV - 1),)), pltpu.SemaphoreType.DMA((n * (N_DEV - 1),)),
                   *[pltpu.HBM(a.shape, a.dtype) for a in arrays], *[pltpu.HBM(z.shape, z.dtype) for z in lands],
                   jax.ShapeDtypeStruct((8, LANES), F32)),
        in_specs=[hbm] * (2 * n) + [pl.BlockSpec(memory_space=pl.ANY)] * len(deps),
        out_specs=(sem, sem, *[hbm] * (2 * n), pl.BlockSpec(memory_space=pltpu.VMEM)),
        input_output_aliases={k: 2 + k for k in range(2 * n)},
        compiler_params=pltpu.CompilerParams(has_side_effects=pltpu.SideEffectType.DATAFLOW_SIDE_EFFECTING),
    )(*[pltpu.with_memory_space_constraint(a, pltpu.HBM) for a in arrays],
      *[pltpu.with_memory_space_constraint(z, pltpu.HBM) for z in lands], *deps)
    return outs[0], outs[1], list(outs[2:2 + n]), list(outs[2 + n:2 + 2 * n]), outs[-1]


def _exchange_wait(started, after, *, scatter, name):
    send_sems, recv_sems, srcs, lands, _ = started
    n = len(srcs)

    def body(*refs):
        ins, zones = refs[:n], refs[n:2 * n]
        s_sems, r_sems = refs[2 * n], refs[2 * n + 1]
        x, y, c = lax.axis_index("x"), lax.axis_index("y"), lax.axis_index("c")
        for p in range(1, N_DEV):
            px, py, pc = _peer(x, y, c, p)
            peer = 4 * px + 2 * py + pc
            for k in range(n):
                cp = pltpu.make_async_remote_copy(
                    src_ref=ins[k].at[peer] if scatter else ins[k],
                    dst_ref=zones[k].at[peer],
                    send_sem=s_sems.at[k * (N_DEV - 1) + p - 1],
                    recv_sem=r_sems.at[k * (N_DEV - 1) + p - 1],
                    device_id=(px, py, pc),
                    device_id_type=pl.DeviceIdType.MESH,
                )
                cp.wait_send()
                cp.wait_recv()

    hbm = pl.BlockSpec(memory_space=pltpu.HBM)
    sem = pl.BlockSpec(memory_space=pltpu.SEMAPHORE)
    outs = pl.pallas_call(
        body,
        name=name,
        out_shape=tuple(pltpu.HBM(a.shape, a.dtype) for a in srcs + lands),
        in_specs=[hbm] * (2 * n) + [sem, sem, pl.BlockSpec(memory_space=pl.ANY)],
        out_specs=tuple([hbm] * (2 * n)),
        input_output_aliases={k: k for k in range(2 * n)},
        compiler_params=pltpu.CompilerParams(has_side_effects=pltpu.SideEffectType.DATAFLOW_SIDE_EFFECTING),
    )(*srcs, *lands, send_sems, recv_sems, after)
    return list(outs[:n]), list(outs[n:])


def _mm(a, b, *, mode, out_dtype, name, add=None, tm=512, tn=512, b_rows=None, dep=None):
    rows_b = b.shape[0] if b_rows is None else b_rows
    if mode == "nn":
        (m, kd), nd = a.shape, b.shape[1]
        assert kd == rows_b
    elif mode == "nt":
        (m, kd), nd = a.shape, rows_b
    else:
        (kd, m), nd = a.shape, b.shape[1]
    tm = _pick(m, tm, LANES if mode == "tn" else 16)
    tn = _pick(nd, tn)
    dims = {"nn": NN, "nt": NT, "tn": TN}[mode]
    ni, nj = m // tm, nd // tn
    a_bytes, b_bytes = a.size * a.dtype.itemsize, b.size * b.dtype.itemsize
    i_outer = a_bytes + ni * b_bytes <= b_bytes + nj * a_bytes
    ij = (lambda g0, g1: (g0, g1)) if i_outer else (lambda g0, g1: (g1, g0))
    a_spec = (pl.BlockSpec((kd, tm), lambda g0, g1: (0, ij(g0, g1)[0])) if mode == "tn"
              else pl.BlockSpec((tm, kd), lambda g0, g1: (ij(g0, g1)[0], 0)))
    b_spec = (pl.BlockSpec((tn, kd), lambda g0, g1: (ij(g0, g1)[1], 0)) if mode == "nt"
              else pl.BlockSpec((kd, tn), lambda g0, g1: (0, ij(g0, g1)[1])))
    o_spec = pl.BlockSpec((tm, tn), lambda g0, g1: ij(g0, g1))
    has_add = add is not None

    def body(*refs):
        a_ref, b_ref = refs[0], refs[1]
        o_ref = refs[-1]
        acc = _dotb(a_ref[...], b_ref[...], dims)
        if has_add:
            acc = acc + refs[2][...].astype(F32)
        o_ref[...] = acc.astype(o_ref.dtype)

    ins = [a, b] + ([add] if has_add else []) + ([] if dep is None else [dep])
    specs = ([a_spec, b_spec] + ([o_spec] if has_add else [])
             + ([] if dep is None else [pl.BlockSpec((8, LANES), lambda g0, g1: (0, 0))]))
    return pl.pallas_call(
        body, name=name, grid=(ni, nj) if i_outer else (nj, ni), in_specs=specs, out_specs=o_spec,
        out_shape=jax.ShapeDtypeStruct((m, nd), out_dtype),
        compiler_params=_params("parallel", "parallel"),
    )(*ins)


def _mm_resid(a, b, x, gate, *, name, tm=256, tn=1024):
    m, kd = a.shape
    nd = b.shape[1]
    tm = _pick(m, tm, 16)
    tn = _pick(nd, tn)
    o_spec = pl.BlockSpec((tm, tn), lambda i, j: (i, j))

    def body(a_ref, b_ref, x_ref, g_ref, xo_ref, y_ref):
        y = _dotb(a_ref[...], b_ref[...], NN)
        y_ref[...] = y.astype(y_ref.dtype)
        xo_ref[...] = x_ref[...] + g_ref[...] * y

    return pl.pallas_call(
        body, name=name, grid=(m // tm, nd // tn),
        in_specs=[pl.BlockSpec((tm, kd), lambda i, j: (i, 0)), pl.BlockSpec((kd, tn), lambda i, j: (0, j)),
                  o_spec, pl.BlockSpec((1, tn), lambda i, j: (0, j))],
        out_specs=(o_spec, o_spec),
        out_shape=(jax.ShapeDtypeStruct((m, nd), F32), jax.ShapeDtypeStruct((m, nd), BF16)),
        compiler_params=_params("parallel", "parallel"),
    )(a, b, x, gate)


ROWS = 256


def _row_spec(width, rows=ROWS):
    return pl.BlockSpec((rows, width), lambda i: (i, 0))


def _const_spec(shape):
    return pl.BlockSpec(shape, lambda i: tuple(0 for _ in shape))


def _adaln_fwd(x, g, scale, shift, *, name):
    t, d = x.shape

    def body(x_ref, g_ref, sc_ref, sh_ref, h_ref):
        xv = x_ref[...]
        r = lax.rsqrt(jnp.mean(xv * xv, axis=-1, keepdims=True) + EPS)
        h_ref[...] = (xv * r * g_ref[...] * (1.0 + sc_ref[...]) + sh_ref[...]).astype(h_ref.dtype)

    return pl.pallas_call(
        body, name=name, grid=(t // ROWS,),
        in_specs=[_row_spec(d), _const_spec((1, d)), _const_spec((1, d)), _const_spec((1, d))],
        out_specs=_row_spec(d), out_shape=jax.ShapeDtypeStruct((t, d), BF16),
        compiler_params=_params("parallel"),
    )(x, g, scale, shift)


def _adaln_bwd(x, g, scale, shift, dh, dres, dep, *, name):
    t, d = x.shape

    def body(x_ref, g_ref, sc_ref, sh_ref, dh_ref, dr_ref, dep_ref, dx_ref, st_ref):
        @pl.when(pl.program_id(0) == 0)
        def _():
            st_ref[...] = jnp.zeros_like(st_ref)

        xv = x_ref[...]
        dhv = dh_ref[...].astype(F32)
        gv = g_ref[...]
        r = lax.rsqrt(jnp.mean(xv * xv, axis=-1, keepdims=True) + EPS)
        xh = xv * r
        nv = xh * gv
        dn = dhv * (1.0 + sc_ref[...])
        dxh = dn * gv
        dx_ref[...] = dr_ref[...] + r * (dxh - xh * jnp.mean(dxh * xh, axis=-1, keepdims=True))
        st_ref[0:1, :] += jnp.sum(dn * xh, axis=0, keepdims=True)
        st_ref[1:2, :] += jnp.sum(dhv * nv, axis=0, keepdims=True)
        st_ref[2:3, :] += jnp.sum(dhv, axis=0, keepdims=True)

    return pl.pallas_call(
        body, name=name, grid=(t // ROWS,),
        in_specs=[_row_spec(d), _const_spec((1, d)), _const_spec((1, d)), _const_spec((1, d)),
                  _row_spec(d), _row_spec(d), _const_spec((8, LANES))],
        out_specs=(_row_spec(d), _const_spec((8, d))),
        out_shape=(jax.ShapeDtypeStruct((t, d), F32), jax.ShapeDtypeStruct((8, d), F32)),
        compiler_params=_params("arbitrary"),
    )(x, g, scale, shift, dh, dres, dep)


def _adaln_gate_bwd(x, g, scale, shift, dh, dres, dep, y_up, gate_up, *, name):
    t, d = x.shape

    def body(x_ref, g_ref, sc_ref, sh_ref, dh_ref, dr_ref, dep_ref, y_ref, gu_ref, dx_ref, st_ref, dy_ref):
        @pl.when(pl.program_id(0) == 0)
        def _():
            st_ref[...] = jnp.zeros_like(st_ref)

        xv = x_ref[...]
        dhv = dh_ref[...].astype(F32)
        gv = g_ref[...]
        r = lax.rsqrt(jnp.mean(xv * xv, axis=-1, keepdims=True) + EPS)
        xh = xv * r
        nv = xh * gv
        dn = dhv * (1.0 + sc_ref[...])
        dxh = dn * gv
        dx = dr_ref[...] + r * (dxh - xh * jnp.mean(dxh * xh, axis=-1, keepdims=True))
        dx_ref[...] = dx
        dy_ref[...] = (dx * gu_ref[...]).astype(dy_ref.dtype)
        st_ref[0:1, :] += jnp.sum(dn * xh, axis=0, keepdims=True)
        st_ref[1:2, :] += jnp.sum(dhv * nv, axis=0, keepdims=True)
        st_ref[2:3, :] += jnp.sum(dhv, axis=0, keepdims=True)
        st_ref[3:4, :] += jnp.sum(dx * y_ref[...].astype(F32), axis=0, keepdims=True)

    return pl.pallas_call(
        body, name=name, grid=(t // ROWS,),
        in_specs=[_row_spec(d), _const_spec((1, d)), _const_spec((1, d)), _const_spec((1, d)),
                  _row_spec(d), _row_spec(d), _const_spec((8, LANES)), _row_spec(d), _const_spec((1, d))],
        out_specs=(_row_spec(d), _const_spec((8, d)), _row_spec(d)),
        out_shape=(jax.ShapeDtypeStruct((t, d), F32), jax.ShapeDtypeStruct((8, d), F32),
                   jax.ShapeDtypeStruct((t, d), BF16)),
        compiler_params=_params("arbitrary"),
    )(x, g, scale, shift, dh, dres, dep, y_up, gate_up)


def _gate_bwd(dxo, y, gate, dep, *, name):
    t, d = dxo.shape

    def body(dx_ref, y_ref, g_ref, dep_ref, dy_ref, st_ref):
        @pl.when(pl.program_id(0) == 0)
        def _():
            st_ref[...] = jnp.zeros_like(st_ref)

        dxv = dx_ref[...]
        dy_ref[...] = (dxv * g_ref[...]).astype(dy_ref.dtype)
        st_ref[0:1, :] += jnp.sum(dxv * y_ref[...], axis=0, keepdims=True)

    return pl.pallas_call(
        body, name=name, grid=(t // ROWS,),
        in_specs=[_row_spec(d), _row_spec(d), _const_spec((1, d)), _const_spec((8, LANES))],
        out_specs=(_row_spec(d), _const_spec((8, d))),
        out_shape=(jax.ShapeDtypeStruct((t, d), BF16), jax.ShapeDtypeStruct((8, d), F32)),
        compiler_params=_params("arbitrary"),
    )(dxo, y, gate, dep)


def _loss_head(x, g, target, *, name):
    t, d = x.shape

    def body(x_ref, g_ref, t_ref, dx_ref, st_ref, ls_ref):
        @pl.when(pl.program_id(0) == 0)
        def _():
            st_ref[...] = jnp.zeros_like(st_ref)
            ls_ref[...] = jnp.zeros_like(ls_ref)

        xv = x_ref[...]
        gv = g_ref[...]
        r = lax.rsqrt(jnp.mean(xv * xv, axis=-1, keepdims=True) + EPS)
        xh = xv * r
        err = xh * gv - t_ref[...]
        ls_ref[...] += 0.5 * jnp.sum(jnp.mean(err * err, axis=-1, keepdims=True))
        dy = err * (1.0 / d)
        dxh = dy * gv
        dx_ref[...] = r * (dxh - xh * jnp.mean(dxh * xh, axis=-1, keepdims=True))
        st_ref[0:1, :] += jnp.sum(dy * xh, axis=0, keepdims=True)

    return pl.pallas_call(
        body, name=name, grid=(t // ROWS,),
        in_specs=[_row_spec(d), _const_spec((1, d)), _row_spec(d)],
        out_specs=(_row_spec(d), _const_spec((8, d)), _const_spec((8, LANES))),
        out_shape=(jax.ShapeDtypeStruct((t, d), F32), jax.ShapeDtypeStruct((8, d), F32),
                   jax.ShapeDtypeStruct((8, LANES), F32)),
        compiler_params=_params("arbitrary"),
    )(x, g, target)


FFN_BLOCK = D_FF // 2
FFN_ROWS = 512


def _ffn_chunks(width):
    edges = [min(width, 3 * LANES * i) for i in range(width // (3 * LANES) + 2)]
    return [slice(lo, hi) for lo, hi in zip(edges[:-1], edges[1:]) if hi > lo]


def _ffn_gu_fwd(h, wg, wu, dep, *, name):
    t, d = h.shape
    tn = FFN_BLOCK

    chunks = _ffn_chunks(tn)
    rows = _pick(t, FFN_ROWS, 16)

    def body(h_ref, wg_ref, wu_ref, dep_ref, s_ref, a_ref, b_ref):
        hv = h_ref[...]
        ab = [(_dotb(hv, wg_ref[sl, :], NT), _dotb(hv, wu_ref[sl, :], NT)) for sl in chunks]
        for sl, (a, b) in zip(chunks, ab):
            s_ref[:, sl] = (a * _sigmoid(a) * b).astype(s_ref.dtype)
            a_ref[:, sl] = a.astype(a_ref.dtype)
            b_ref[:, sl] = b.astype(b_ref.dtype)

    w_spec = pl.BlockSpec((tn, d), lambda j, i: (j, 0))
    o_spec = pl.BlockSpec((rows, tn), lambda j, i: (i, j))
    return pl.pallas_call(
        body, name=name, grid=(D_FF // tn, t // rows),
        in_specs=[pl.BlockSpec((rows, d), lambda j, i: (i, 0)), w_spec, w_spec,
                  pl.BlockSpec((8, LANES), lambda j, i: (0, 0))],
        out_specs=(o_spec, o_spec, o_spec),
        out_shape=(jax.ShapeDtypeStruct((t, D_FF), BF16),) * 3,
        compiler_params=_params("parallel", "parallel"),
    )(h, wg, wu, dep)


def _ffn_down_dx(dy, w_down, a, b, *, name):
    t, d = dy.shape
    tn = FFN_BLOCK

    chunks = _ffn_chunks(tn)
    rows = _pick(t, FFN_ROWS, 16)

    def body(dy_ref, w_ref, a_ref, b_ref, da_ref, db_ref):
        dyv = dy_ref[...]
        ds = [_dotb(dyv, w_ref[sl, :], NT) for sl in chunks]
        for sl, dsc in zip(chunks, ds):
            av = a_ref[:, sl].astype(F32)
            sg = _sigmoid(av)
            da_ref[:, sl] = (dsc * b_ref[:, sl].astype(F32) * sg * (1.0 + av * (1.0 - sg))).astype(da_ref.dtype)
            db_ref[:, sl] = (dsc * av * sg).astype(db_ref.dtype)

    o_spec = pl.BlockSpec((rows, tn), lambda j, i: (i, j))
    return pl.pallas_call(
        body, name=name, grid=(D_FF // tn, t // rows),
        in_specs=[pl.BlockSpec((rows, d), lambda j, i: (i, 0)), pl.BlockSpec((tn, d), lambda j, i: (j, 0)),
                  o_spec, o_spec],
        out_specs=(o_spec, o_spec),
        out_shape=(jax.ShapeDtypeStruct((t, D_FF), BF16),) * 2,
        compiler_params=_params("parallel", "parallel"),
    )(dy, w_down, a, b)


def _shift_rows(v, s, rows):
    if s == 0:
        return v
    return jnp.where(rows >= s, pltpu.roll(v, s, 0), 0.0)


def _unshift_rows(v, s, rows, t):
    if s == 0:
        return v
    return jnp.where(rows < t - s, pltpu.roll(v, t - s, 0), 0.0)


def _conv_taps(x, rows):
    return [_shift_rows(x, GDN_CONV - 1 - j, rows) for j in range(GDN_CONV)]


def _conv_silu(xs, w):
    z = w[0:1, :] * xs[0]
    for j in range(1, GDN_CONV):
        z = z + w[j:j + 1, :] * xs[j]
    sg = _sigmoid(z)
    return z, sg, z * sg


def _gdn_prep_fwd(proj, conv_wt, *, name):
    t = proj.shape[0]
    nh = GDN_HEADS

    hp = GDN_PREP_HEADS
    wd = hp * LANES

    def body(x_ref, w_ref, y_ref):
        j = pl.program_id(0) * hp
        rows = lax.broadcasted_iota(jnp.int32, (t, LANES), 0)
        qscale = jnp.where(j < nh, GDN_HEAD_DIM ** -0.5, 1.0)
        for i in range(hp):
            sl = slice(i * LANES, (i + 1) * LANES)
            _, _, s = _conv_silu(_conv_taps(x_ref[:, sl], rows), w_ref[:, sl])
            rs = lax.rsqrt(jnp.sum(s * s, axis=-1, keepdims=True) + EPS)
            y_ref[:, sl] = jnp.where(j < 2 * nh, s * rs * qscale, s)

    return pl.pallas_call(
        body, name=name, grid=(3 * nh // hp,),
        in_specs=[pl.BlockSpec((t, wd), lambda j: (0, j)), pl.BlockSpec((GDN_CONV, wd), lambda j: (0, j))],
        out_specs=pl.BlockSpec((t, wd), lambda j: (0, j)),
        out_shape=jax.ShapeDtypeStruct((t, 3 * GDN_KEY_DIM), F32),
        compiler_params=_params("parallel"),
    )(proj, conv_wt)


def _gdn_prep_bwd(proj, conv_wt, dy, *, name):
    t = proj.shape[0]
    nh = GDN_HEADS

    hp = GDN_PREP_HEADS
    wd = hp * LANES
    per_seg = nh // hp

    def body(x_ref, w_ref, dy_ref, dx_ref, dw_ref):
        j = pl.program_id(0) * hp
        rows = lax.broadcasted_iota(jnp.int32, (t, LANES), 0)
        qscale = jnp.where(j < nh, GDN_HEAD_DIM ** -0.5, 1.0)
        for i in range(hp):
            sl = slice(i * LANES, (i + 1) * LANES)
            w = w_ref[:, sl]
            xs = _conv_taps(x_ref[:, sl], rows)
            z, sg, s = _conv_silu(xs, w)
            rs = lax.rsqrt(jnp.sum(s * s, axis=-1, keepdims=True) + EPS)
            dyv = dy_ref[:, sl]
            nv = s * rs
            de = dyv * qscale
            ds_qk = rs * (de - nv * jnp.sum(de * nv, axis=-1, keepdims=True))
            ds = jnp.where(j < 2 * nh, ds_qk, dyv)
            dz = ds * sg * (1.0 + z * (1.0 - sg))
            dx = w[GDN_CONV - 1:GDN_CONV, :] * dz
            dw_ref[GDN_CONV - 1:GDN_CONV, sl] = jnp.sum(dz * xs[GDN_CONV - 1], axis=0, keepdims=True)
            for k in range(GDN_CONV - 1):
                dx = dx + w[k:k + 1, :] * _unshift_rows(dz, GDN_CONV - 1 - k, rows, t)
                dw_ref[k:k + 1, sl] = jnp.sum(dz * xs[k], axis=0, keepdims=True)
            dx_ref[:, sl] = dx.astype(dx_ref.dtype)

    return pl.pallas_call(
        body, name=name, grid=(3 * nh // hp,),
        in_specs=[pl.BlockSpec((t, wd), lambda j: (0, j)), pl.BlockSpec((GDN_CONV, wd), lambda j: (0, j)),
                  pl.BlockSpec((None, t, wd), lambda j: (j // per_seg, 0, j % per_seg))],
        out_specs=(pl.BlockSpec((t, wd), lambda j: (0, j)), pl.BlockSpec((GDN_CONV, wd), lambda j: (0, j))),
        out_shape=(jax.ShapeDtypeStruct((t, 3 * GDN_KEY_DIM), BF16),
                   jax.ShapeDtypeStruct((GDN_CONV, 3 * GDN_KEY_DIM), F32)),
        compiler_params=_params("parallel"),
    )(proj, conv_wt, dy)


def _softplus(z):
    return jnp.maximum(z, 0.0) + jnp.log(1.0 + jnp.exp(-jnp.abs(z)))


def _gdn_gate_fwd(ab, prm, *, name):
    t = ab.shape[0]

    def body(ab_ref, p_ref, o_ref):
        v = ab_ref[...]
        lane = lax.broadcasted_iota(jnp.int32, v.shape, 1)
        g = -jnp.exp(p_ref[0:1, :]) * _softplus(v + p_ref[1:2, :])
        o_ref[...] = jnp.where(lane < GDN_HEADS, g, jnp.where(lane < 2 * GDN_HEADS, _sigmoid(v), 0.0))

    return pl.pallas_call(
        body, name=name, grid=(t // ROWS,),
        in_specs=[_row_spec(LANES), _const_spec((8, LANES))], out_specs=_row_spec(LANES),
        out_shape=jax.ShapeDtypeStruct((t, LANES), F32), compiler_params=_params("parallel"),
    )(ab, prm)


def _gdn_gate_bwd(ab, prm, dgb, *, name):
    t = ab.shape[0]

    def body(ab_ref, p_ref, d_ref, o_ref, st_ref):
        @pl.when(pl.program_id(0) == 0)
        def _():
            st_ref[...] = jnp.zeros_like(st_ref)

        v = ab_ref[...]
        dv = d_ref[...]
        lane = lax.broadcasted_iota(jnp.int32, v.shape, 1)
        is_a = lane < GDN_HEADS
        is_b = jnp.logical_and(lane >= GDN_HEADS, lane < 2 * GDN_HEADS)
        a_exp = jnp.exp(p_ref[0:1, :])
        zz = v + p_ref[1:2, :]
        g = -a_exp * _softplus(zz)
        da = dv * (-a_exp) * _sigmoid(zz)
        beta = _sigmoid(v)
        db = dv * beta * (1.0 - beta)
        o_ref[...] = jnp.where(is_a, da, jnp.where(is_b, db, 0.0)).astype(o_ref.dtype)
        st_ref[0:1, :] += jnp.sum(jnp.where(is_a, dv * g, 0.0), axis=0, keepdims=True)
        st_ref[1:2, :] += jnp.sum(jnp.where(is_a, da, 0.0), axis=0, keepdims=True)

    return pl.pallas_call(
        body, name=name, grid=(t // ROWS,),
        in_specs=[_row_spec(LANES), _const_spec((8, LANES)), _row_spec(LANES)],
        out_specs=(_row_spec(LANES), _const_spec((8, LANES))),
        out_shape=(jax.ShapeDtypeStruct((t, LANES), BF16), jax.ShapeDtypeStruct((8, LANES), F32)),
        compiler_params=_params("arbitrary"),
    )(ab, prm, dgb)


def _gdn_local(qs, ks, vs, gbs, bbs, tinvs=None):
    nh = len(qs)
    cs = qs[0].shape[0]
    hs = range(nh)
    r = lax.broadcasted_iota(jnp.int32, (cs, cs), 0)
    c = lax.broadcasted_iota(jnp.int32, (cs, cs), 1)
    tril, strict, eye = r >= c, r > c, r == c
    ident = jnp.where(eye, 1.0, 0.0)
    g_colb = [gbs[h][:, :cs] for h in hs]
    g_row = [jnp.sum(jnp.where(eye, g_colb[h], 0.0), axis=0, keepdims=True) for h in hs]
    gc_col = [jnp.sum(jnp.where(tril, g_row[h], 0.0), axis=1, keepdims=True) for h in hs]
    gc_row = [jnp.sum(jnp.where(r <= c, g_colb[h], 0.0), axis=0, keepdims=True) for h in hs]
    decay = [jnp.exp(jnp.where(tril, gc_col[h] - gc_row[h], NEG)) for h in hs]
    gamma = [jnp.exp(gc_col[h]) for h in hs]
    gcl = [gc_col[h][cs - 1:cs, :] for h in hs]
    gl = [jnp.exp(gcl[h]) for h in hs]
    kdec = [jnp.exp(gcl[h] - gc_col[h]) for h in hs]
    kb = [ks[h] * bbs[h] for h in hs]
    kk = [_dotb(kb[h], ks[h], NT) for h in hs]
    qk = [_dotb(qs[h], ks[h], NT) for h in hs]
    lmat = [jnp.where(strict, kk[h] * decay[h], 0.0) for h in hs]
    pmat = [jnp.where(tril, qk[h] * decay[h], 0.0) for h in hs]
    if tinvs is None:
        xm = [-lmat[h] for h in hs]
        tinv = [ident + xm[h] for h in hs]
        for _ in range(int(math.log2(cs)) - 1):
            xm = [_dotf(xm[h], xm[h], NN) for h in hs]
            tinv = [tinv[h] + _dotf(tinv[h], xm[h], NN) for h in hs]
    else:
        tinv = tinvs
    vb = [vs[h] * bbs[h] for h in hs]
    kg = [kb[h] * gamma[h] for h in hs]
    u = [_dotf(tinv[h], vb[h], NN) for h in hs]
    w = [_dotf(tinv[h], kg[h], NN) for h in hs]
    return [dict(tril=tril, strict=strict, eye=eye, r=r, c=c, decay=decay[h], gamma=gamma[h], gl=gl[h], kdec=kdec[h],
                 kb=kb[h], lmat=lmat[h], tinv=tinv[h], vb=vb[h], kg=kg[h], u=u[h], w=w[h], pmat=pmat[h],
                 qd=qs[h] * gamma[h], kd=ks[h] * kdec[h]) for h in hs]


def _head_columns(gbeta, cs):
    gbs = [jnp.broadcast_to(gbeta[:, h:h + 1], (cs, LANES)) for h in range(GDN_HEADS)]
    bbs = [jnp.broadcast_to(gbeta[:, GDN_HEADS + h:GDN_HEADS + h + 1], (cs, LANES)) for h in range(GDN_HEADS)]
    return gbs, bbs


def _gdn_chunk_fwd(qkv, gbeta, *, name):
    t = qkv.shape[0]
    nh, cs, hd = GDN_HEADS, GDN_CHUNK, GDN_HEAD_DIM
    nc = t // cs

    hb = GDN_HEAD_BATCH
    ng = nh // hb
    assert ng == 1

    def body(q_ref, k_ref, v_ref, gb_ref, o_ref, st_ref, ti_ref, s_ref):
        @pl.when(pl.program_id(1) == 0)
        def _():
            s_ref[...] = jnp.zeros_like(s_ref)

        sls = [slice(i * hd, (i + 1) * hd) for i in range(hb)]
        hs = range(hb)
        s = [s_ref[i] for i in hs]
        gbs, bbs = _head_columns(gb_ref[...], cs)
        lo = _gdn_local([q_ref[:, sl] for sl in sls], [k_ref[:, sl] for sl in sls], [v_ref[:, sl] for sl in sls],
                        gbs, bbs)
        ws = [_dotb(lo[i]["w"], s[i], NN) for i in hs]
        qs = [_dotb(lo[i]["qd"], s[i], NN) for i in hs]
        vn = [lo[i]["u"] - ws[i] for i in hs]
        pv = [_dotb(lo[i]["pmat"], vn[i], NN) for i in hs]
        kv = [_dotb(lo[i]["kd"], vn[i], TN) for i in hs]
        for i, sl in enumerate(sls):
            st_ref[i, 0] = s[i]
            ti_ref[i, 0] = lo[i]["tinv"]
            o_ref[:, sl] = qs[i] + pv[i]
            s_ref[i] = s[i] * lo[i]["gl"] + kv[i]

    col = lambda off: pl.BlockSpec((cs, hb * hd), lambda h, n: (n, off + h))
    return pl.pallas_call(
        body, name=name, grid=(ng, nc),
        in_specs=[col(0), col(ng), col(2 * ng), pl.BlockSpec((cs, LANES), lambda h, n: (n, 0))],
        out_specs=(col(0), pl.BlockSpec((hb, 1, hd, hd), lambda h, n: (h, n, 0, 0)),
                   pl.BlockSpec((hb, 1, cs, cs), lambda h, n: (h, n, 0, 0))),
        out_shape=(jax.ShapeDtypeStruct((t, nh * hd), F32), jax.ShapeDtypeStruct((nh, nc, hd, hd), F32),
                   jax.ShapeDtypeStruct((nh, nc, cs, cs), F32)),
        scratch_shapes=[pltpu.VMEM((hb, hd, hd), F32)],
        compiler_params=_params("parallel", "arbitrary"),
    )(qkv, qkv, qkv, gbeta)


def _gdn_chunk_bwd(qkv, gbeta, states, tinvs, do, *, name):
    t = qkv.shape[0]
    nh, cs, hd = GDN_HEADS, GDN_CHUNK, GDN_HEAD_DIM
    nc = t // cs

    hb = GDN_HEAD_BATCH
    ng = nh // hb
    assert ng == 1

    def heads_bwd(q, k, v, gb, bb, s, ti, dsn, dov):
        hs = range(len(q))
        lo = _gdn_local(q, k, v, gb, bb, ti)
        tril, strict, eye, r, c = lo[0]["tril"], lo[0]["strict"], lo[0]["eye"], lo[0]["r"], lo[0]["c"]
        rowi = lax.broadcasted_iota(jnp.int32, (cs, 1), 0)
        get = lambda name: [lo[h][name] for h in hs]
        decay, gamma, gl, kdec = get("decay"), get("gamma"), get("gl"), get("kdec")
        kb, tinv, w, pmat, kd, qd = get("kb"), get("tinv"), get("w"), get("pmat"), get("kd"), get("qd")
        ws = [_dotb(w[h], s[h], NN) for h in hs]
        pdo = [_dotb(pmat[h], dov[h], TN) for h in hs]
        kds = [_dotb(kd[h], dsn[h], NN) for h in hs]
        dqd = [_dotb(dov[h], s[h], NT) for h in hs]
        qdo = [_dotb(qd[h], dov[h], TN) for h in hs]
        vn = [lo[h]["u"] - ws[h] for h in hs]
        dvn = [pdo[h] + kds[h] for h in hs]
        dp = [jnp.where(tril, _dotb(dov[h], vn[h], NT), 0.0) for h in hs]
        dkd = [_dotb(vn[h], dsn[h], NT) for h in hs]
        dw = [-_dotb(dvn[h], s[h], NT) for h in hs]
        wdv = [_dotb(w[h], dvn[h], TN) for h in hs]
        dvb = [_dotf(tinv[h], dvn[h], TN) for h in hs]
        dt1 = [_dotf(dvn[h], lo[h]["vb"], NT) for h in hs]
        dkg = [_dotf(tinv[h], dw[h], TN) for h in hs]
        dt2 = [_dotf(dw[h], lo[h]["kg"], NT) for h in hs]
        tdt = [_dotf(tinv[h], dt1[h] + dt2[h], TN) for h in hs]
        dl = [jnp.where(strict, -_dotf(tdt[h], tinv[h], NT), 0.0) for h in hs]
        dkk = [dl[h] * decay[h] for h in hs]
        dqk = [dp[h] * decay[h] for h in hs]
        dkb = [_dotb(dkk[h], k[h], NN) + dkg[h] * gamma[h] for h in hs]
        dk1 = [_dotb(dkk[h], kb[h], TN) for h in hs]
        dk2 = [_dotb(dqk[h], q[h], TN) for h in hs]
        dq1 = [_dotb(dqk[h], k[h], NN) for h in hs]
        out = []
        for h in hs:
            dgl = jnp.sum(jnp.sum(dsn[h] * s[h], axis=1, keepdims=True), axis=0, keepdims=True)
            ds_prev = gl[h] * dsn[h] + qdo[h] - wdv[h]
            dk = dk1[h] + dk2[h] + dkd[h] * kdec[h] + dkb[h] * bb[h]
            dq = dq1[h] + dqd[h] * gamma[h]
            dbeta = jnp.sum(dvb[h] * v[h], axis=-1, keepdims=True) + jnp.sum(dkb[h] * k[h], axis=-1, keepdims=True)
            e = dl[h] * lo[h]["lmat"] + dp[h] * pmat[h]
            e_col = jnp.sum(e, axis=0, keepdims=True)
            dgc = jnp.sum(e, axis=1, keepdims=True) - jnp.sum(jnp.where(eye, e_col, 0.0), axis=1, keepdims=True)
            dgamma = (jnp.sum(dqd[h] * q[h], axis=-1, keepdims=True)
                      + jnp.sum(dkg[h] * kb[h], axis=-1, keepdims=True))
            rk = jnp.sum(dkd[h] * k[h], axis=-1, keepdims=True) * kdec[h]
            dgcl = jnp.sum(rk, axis=0, keepdims=True) + dgl * gl[h]
            dgc = dgc + dgamma * gamma[h] - rk + jnp.where(rowi == cs - 1, dgcl, 0.0)
            dgc_row = jnp.sum(jnp.where(eye, dgc, 0.0), axis=0, keepdims=True)
            dg = jnp.sum(jnp.where(c >= r, dgc_row, 0.0), axis=1, keepdims=True)
            out.append((dq, dk, dvb[h] * bb[h], dbeta, dg, ds_prev))
        return out

    def body(q_ref, k_ref, v_ref, gb_ref, st_ref, ti_ref, do_ref, d_ref, dgb_ref, ds_ref):
        @pl.when(pl.program_id(1) == 0)
        def _():
            ds_ref[...] = jnp.zeros_like(ds_ref)

        sls = [slice(i * hd, (i + 1) * hd) for i in range(hb)]
        hs = range(hb)
        gbs, bbs = _head_columns(gb_ref[...], cs)
        outs = heads_bwd([q_ref[:, sl] for sl in sls], [k_ref[:, sl] for sl in sls], [v_ref[:, sl] for sl in sls],
                         gbs, bbs, [st_ref[i, 0] for i in hs],
                         [ti_ref[i, 0] for i in hs], [ds_ref[i] for i in hs], [do_ref[:, sl] for sl in sls])
        lane = lax.broadcasted_iota(jnp.int32, (cs, LANES), 1)
        dgb = jnp.zeros((cs, LANES), F32)
        for i, sl in enumerate(sls):
            dq, dk, dv, dbeta, dg, ds_prev = outs[i]
            d_ref[0, :, sl], d_ref[1, :, sl], d_ref[2, :, sl] = dq, dk, dv
            dgb = jnp.where(lane == i, dg, jnp.where(lane == nh + i, dbeta, dgb))
            ds_ref[i] = ds_prev
        dgb_ref[...] = dgb

    col = lambda off: pl.BlockSpec((cs, hb * hd), lambda h, n: (nc - 1 - n, off + h))
    gspec = pl.BlockSpec((cs, LANES), lambda h, n: (nc - 1 - n, 0))
    return pl.pallas_call(
        body, name=name, grid=(ng, nc),
        in_specs=[col(0), col(ng), col(2 * ng), gspec,
                  pl.BlockSpec((hb, 1, hd, hd), lambda h, n: (h, nc - 1 - n, 0, 0)),
                  pl.BlockSpec((hb, 1, cs, cs), lambda h, n: (h, nc - 1 - n, 0, 0)), col(0)],
        out_specs=(pl.BlockSpec((3, cs, hb * hd), lambda h, n: (0, nc - 1 - n, h)), gspec),
        out_shape=(jax.ShapeDtypeStruct((3, t, nh * hd), F32), jax.ShapeDtypeStruct((t, LANES), F32)),
        scratch_shapes=[pltpu.VMEM((hb, hd, hd), F32)],
        compiler_params=_params("parallel", "arbitrary"),
    )(qkv, qkv, qkv, gbeta, states, tinvs, do)


def _gdn_onorm_fwd(o, proj, norm_g, *, name):
    t = o.shape[0]
    w = GDN_KEY_DIM
    goff = 3 * GDN_KEY_DIM // w

    def body(o_ref, gp_ref, g_ref, y_ref):
        gv = g_ref[...]
        for h in range(GDN_HEADS):
            sl = slice(h * GDN_HEAD_DIM, (h + 1) * GDN_HEAD_DIM)
            oh = o_ref[:, sl]
            gp = gp_ref[:, sl]
            r = lax.rsqrt(jnp.mean(oh * oh, axis=-1, keepdims=True) + EPS)
            y_ref[:, sl] = (oh * r * gv * gp * _sigmoid(gp)).astype(y_ref.dtype)

    return pl.pallas_call(
        body, name=name, grid=(t // ROWS,),
        in_specs=[_row_spec(w), pl.BlockSpec((ROWS, w), lambda i: (i, goff)), _const_spec((1, GDN_HEAD_DIM))],
        out_specs=_row_spec(w), out_shape=jax.ShapeDtypeStruct((t, w), BF16),
        compiler_params=_params("parallel"),
    )(o, proj, norm_g)


def _gdn_onorm_bwd(o, proj, norm_g, dy, *, name):
    t = o.shape[0]
    w = GDN_KEY_DIM
    goff = 3 * GDN_KEY_DIM // w

    def body(o_ref, gp_ref, g_ref, dy_ref, do_ref, dgp_ref, st_ref):
        @pl.when(pl.program_id(0) == 0)
        def _():
            st_ref[...] = jnp.zeros_like(st_ref)

        gv = g_ref[...]
        acc = jnp.zeros((1, GDN_HEAD_DIM), F32)
        for h in range(GDN_HEADS):
            sl = slice(h * GDN_HEAD_DIM, (h + 1) * GDN_HEAD_DIM)
            oh = o_ref[:, sl]
            gp = gp_ref[:, sl]
            dyv = dy_ref[:, sl].astype(F32)
            r = lax.rsqrt(jnp.mean(oh * oh, axis=-1, keepdims=True) + EPS)
            xh = oh * r
            sg = _sigmoid(gp)
            dn = dyv * gp * sg
            dgp_ref[:, sl] = (dyv * xh * gv * sg * (1.0 + gp * (1.0 - sg))).astype(dgp_ref.dtype)
            acc = acc + jnp.sum(dn * xh, axis=0, keepdims=True)
            dxh = dn * gv
            do_ref[:, sl] = r * (dxh - xh * jnp.mean(dxh * xh, axis=-1, keepdims=True))
        st_ref[0:1, :] += acc

    return pl.pallas_call(
        body, name=name, grid=(t // ROWS,),
        in_specs=[_row_spec(w), pl.BlockSpec((ROWS, w), lambda i: (i, goff)), _const_spec((1, GDN_HEAD_DIM)),
                  _row_spec(w)],
        out_specs=(_row_spec(w), _row_spec(w), _const_spec((8, GDN_HEAD_DIM))),
        out_shape=(jax.ShapeDtypeStruct((t, w), F32), jax.ShapeDtypeStruct((t, w), BF16),
                   jax.ShapeDtypeStruct((8, GDN_HEAD_DIM), F32)),
        compiler_params=_params("arbitrary"),
    )(o, proj, norm_g, dy)


def _mla_prep_fwd(proj, qg, kvg, *, name):
    t = proj.shape[0]
    q1, k1 = MLA_Q_RANK, MLA_Q_RANK + MLA_KV_RANK

    def body(p_ref, qg_ref, kg_ref, cq_ref, ck_ref):
        cq = p_ref[:, 0:q1]
        ck = p_ref[:, q1:k1]
        cq_ref[...] = (cq * lax.rsqrt(jnp.mean(cq * cq, axis=-1, keepdims=True) + EPS) * qg_ref[...]).astype(BF16)
        ck_ref[...] = (ck * lax.rsqrt(jnp.mean(ck * ck, axis=-1, keepdims=True) + EPS) * kg_ref[...]).astype(BF16)

    return pl.pallas_call(
        body, name=name, grid=(t // ROWS,),
        in_specs=[_row_spec(MLA_IN), _const_spec((1, MLA_Q_RANK)), _const_spec((1, MLA_KV_RANK))],
        out_specs=(_row_spec(MLA_Q_RANK), _row_spec(MLA_KV_RANK)),
        out_shape=(jax.ShapeDtypeStruct((t, MLA_Q_RANK), BF16), jax.ShapeDtypeStruct((t, MLA_KV_RANK), BF16)),
        compiler_params=_params("parallel"),
    )(proj, qg, kvg)


def _mla_prep_bwd(proj, qg, kvg, dcq, dck, dkr, *, name):
    t = proj.shape[0]
    q1, k1 = MLA_Q_RANK, MLA_Q_RANK + MLA_KV_RANK

    def body(p_ref, qg_ref, kg_ref, dq_ref, dk_ref, dr_ref, dp_ref, st_ref):
        @pl.when(pl.program_id(0) == 0)
        def _():
            st_ref[...] = jnp.zeros_like(st_ref)

        for lo, hi, g_ref, d_ref in ((0, q1, qg_ref, dq_ref), (q1, k1, kg_ref, dk_ref)):
            xv = p_ref[:, lo:hi]
            dn = d_ref[...]
            r = lax.rsqrt(jnp.mean(xv * xv, axis=-1, keepdims=True) + EPS)
            xh = xv * r
            dxh = dn * g_ref[...]
            dp_ref[:, lo:hi] = (r * (dxh - xh * jnp.mean(dxh * xh, axis=-1, keepdims=True))).astype(dp_ref.dtype)
            st_ref[0:1, lo:hi] += jnp.sum(dn * xh, axis=0, keepdims=True)
        dp_ref[:, k1:MLA_IN] = dr_ref[:, 0:MLA_ROPE].astype(dp_ref.dtype)

    return pl.pallas_call(
        body, name=name, grid=(t // ROWS,),
        in_specs=[_row_spec(MLA_IN), _const_spec((1, MLA_Q_RANK)), _const_spec((1, MLA_KV_RANK)),
                  _row_spec(MLA_Q_RANK), _row_spec(MLA_KV_RANK), _row_spec(LANES)],
        out_specs=(_row_spec(MLA_IN), _const_spec((8, MLA_IN))),
        out_shape=(jax.ShapeDtypeStruct((t, MLA_IN), BF16), jax.ShapeDtypeStruct((8, MLA_IN), F32)),
        compiler_params=_params("arbitrary"),
    )(proj, qg, kvg, dcq, dck, dkr)


ATT_BLOCK = 256
ATT_HEAD_BATCH = 8
ATT_HEAD_BATCH_BWD = 4
ATT_SCALE = MLA_QK ** -0.5


def _diagonal_mask(blk):
    return lax.broadcasted_iota(jnp.int32, (blk, blk), 1) <= lax.broadcasted_iota(jnp.int32, (blk, blk), 0)


def _swap_halves(xv, first):
    return jnp.where(first, pltpu.roll(xv, LANES - MLA_ROPE // 2, 1), pltpu.roll(xv, MLA_ROPE // 2, 1))


def _rope_qk(qf, proj, cos_t, sin_t, *, name):
    t = qf.shape[0]
    nrope = MLA_HEADS * MLA_ROPE
    q_blk = MLA_HEADS * MLA_NOPE // nrope
    k_blk = (MLA_Q_RANK + MLA_KV_RANK) // LANES

    def body(q_ref, p_ref, c_ref, s_ref, qo_ref, ko_ref):
        cv, sv = c_ref[...], s_ref[...]
        lane = lax.broadcasted_iota(jnp.int32, (ROWS, LANES), 1)
        first = (lane % MLA_ROPE) < (MLA_ROPE // 2)
        for i in range(nrope // LANES):
            sl = slice(i * LANES, (i + 1) * LANES)
            xv = q_ref[:, sl].astype(F32)
            qo_ref[:, sl] = (xv * cv + _swap_halves(xv, first) * sv).astype(qo_ref.dtype)
        kv = jnp.where(lane < MLA_ROPE, p_ref[...], 0.0)
        ko_ref[...] = (kv * cv + _swap_halves(kv, first) * sv).astype(ko_ref.dtype)

    return pl.pallas_call(
        body, name=name, grid=(t // ROWS,),
        in_specs=[pl.BlockSpec((ROWS, nrope), lambda i: (i, q_blk)), pl.BlockSpec((ROWS, LANES), lambda i: (i, k_blk)),
                  _row_spec(LANES), _row_spec(LANES)],
        out_specs=(_row_spec(nrope), _row_spec(LANES)),
        out_shape=(jax.ShapeDtypeStruct((t, nrope), BF16), jax.ShapeDtypeStruct((t, LANES), BF16)),
        compiler_params=_params("parallel"),
    )(qf, proj, cos_t, sin_t)


def _rope_qk_bwd(dqr, dkr_parts, cos_t, sin_t, *, name):
    t, nrope = dqr.shape
    ng = dkr_parts.shape[0]

    def body(d_ref, k_ref, c_ref, s_ref, qo_ref, ko_ref):
        cv, sv = c_ref[...], s_ref[...]
        lane = lax.broadcasted_iota(jnp.int32, (ROWS, LANES), 1)
        first = (lane % MLA_ROPE) < (MLA_ROPE // 2)
        for i in range(nrope // LANES):
            sl = slice(i * LANES, (i + 1) * LANES)
            dv = d_ref[:, sl]
            qo_ref[:, sl] = (dv * cv + _swap_halves(dv * sv, first)).astype(qo_ref.dtype)
        dk = k_ref[0]
        for g in range(1, ng):
            dk = dk + k_ref[g]
        dk = jnp.where(lane < MLA_ROPE, dk, 0.0)
        ko_ref[...] = jnp.where(lane < MLA_ROPE, dk * cv + _swap_halves(dk * sv, first), 0.0)

    return pl.pallas_call(
        body, name=name, grid=(t // ROWS,),
        in_specs=[_row_spec(nrope), pl.BlockSpec((ng, ROWS, LANES), lambda i: (0, i, 0)), _row_spec(LANES),
                  _row_spec(LANES)],
        out_specs=(_row_spec(nrope), _row_spec(LANES)),
        out_shape=(jax.ShapeDtypeStruct((t, nrope), BF16), jax.ShapeDtypeStruct((t, LANES), F32)),
        compiler_params=_params("parallel"),
    )(dqr, dkr_parts, cos_t, sin_t)


def _attn_tm_fwd(qf, qr, kvf, kr, *, name):
    t = qf.shape[0]
    nh, dn, dr, dv = MLA_HEADS, MLA_NOPE, MLA_ROPE, MLA_V
    blk = min(ATT_BLOCK, t)
    hb = ATT_HEAD_BATCH
    hs = range(hb)

    def body(q_ref, qr_ref, kv_ref, kr_ref, o_ref, l_ref):
        i = pl.program_id(1)
        qc = [jnp.concatenate([q_ref[:, h * dn:(h + 1) * dn].astype(MXU_DTYPE), qr_ref[:, h * dr:(h + 1) * dr]], axis=1)
              for h in hs]

        def step(j, carry, diagonal=False):
            m, l, acc = carry[:hb], carry[hb:2 * hb], carry[2 * hb:]
            rows = pl.ds(pl.multiple_of(j * blk, blk), blk)
            krj = kr_ref[rows, 0:dr]
            s = [_dotb(qc[h], jnp.concatenate([kv_ref[rows, h * (dn + dv):h * (dn + dv) + dn], krj], axis=1), NT)
                 for h in hs]
            s = [s[h] * ATT_SCALE for h in hs]
            if diagonal:
                mask = _diagonal_mask(blk)
                s = [jnp.where(mask, s[h], NEG) for h in hs]
            m_new = [jnp.maximum(m[h], jnp.max(s[h], axis=-1, keepdims=True)) for h in hs]
            p = [jnp.exp(s[h] - m_new[h]) for h in hs]
            pv = [_dotb(p[h], kv_ref[rows, h * (dn + dv) + dn:(h + 1) * (dn + dv)], NN) for h in hs]
            alpha = [jnp.exp(m[h] - m_new[h]) for h in hs]
            l = [alpha[h] * l[h] + jnp.sum(p[h], axis=-1, keepdims=True) for h in hs]
            acc = [alpha[h] * acc[h] + pv[h] for h in hs]
            return tuple(m_new) + tuple(l) + tuple(acc)

        init = ((jnp.full((blk, 1), NEG, F32),) * hb + (jnp.zeros((blk, 1), F32),) * hb
                + (jnp.zeros((blk, dv), F32),) * hb)
        out = step(i, lax.fori_loop(0, i, step, init), diagonal=True)
        for h in hs:
            m, l, acc = out[h], out[hb + h], out[2 * hb + h]
            o_ref[:, h * dv:(h + 1) * dv] = (acc / l).astype(o_ref.dtype)
            l_ref[h] = jnp.broadcast_to(m + jnp.log(l), (blk, LANES))

    return pl.pallas_call(
        body, name=name, grid=(nh // hb, t // blk),
        in_specs=[pl.BlockSpec((blk, hb * dn), lambda g, i: (i, g)), pl.BlockSpec((blk, hb * dr), lambda g, i: (i, g)),
                  pl.BlockSpec((t, hb * (dn + dv)), lambda g, i: (0, g)), pl.BlockSpec((t, LANES), lambda g, i: (0, 0))],
        out_specs=(pl.BlockSpec((blk, hb * dv), lambda g, i: (i, g)),
                   pl.BlockSpec((hb, blk, LANES), lambda g, i: (g, i, 0))),
        out_shape=(jax.ShapeDtypeStruct((t, nh * dv), BF16), jax.ShapeDtypeStruct((nh, t, LANES), F32)),
        compiler_params=_params("parallel", "parallel"),
    )(qf, qr, kvf, kr)


def _attn_tm_bwd(qf, qr, kvf, kr, o, lse, do, *, name):
    t = qf.shape[0]
    nh, dn, dr, dv = MLA_HEADS, MLA_NOPE, MLA_ROPE, MLA_V
    blk = min(ATT_BLOCK, t)
    nb = t // blk
    hb = ATT_HEAD_BATCH_BWD
    hs = range(hb)
    ng = nh // hb

    def body(q_ref, qr_ref, kv_ref, kr_ref, o_ref, l_ref, do_ref, dqn_ref, dqr_ref, dkv_ref, dkr_ref):
        j = pl.program_id(1)

        @pl.when(j == 0)
        def _():
            dqn_ref[...] = jnp.zeros_like(dqn_ref)
            dqr_ref[...] = jnp.zeros_like(dqr_ref)

        krj = kr_ref[:, 0:dr]
        kc = [jnp.concatenate([kv_ref[:, h * (dn + dv):h * (dn + dv) + dn], krj], axis=1) for h in hs]
        vv = [kv_ref[:, h * (dn + dv) + dn:(h + 1) * (dn + dv)] for h in hs]

        def step(i, carry, diagonal=False):
            dkn_acc, dv_acc, dkr_acc = carry[:hb], carry[hb:2 * hb], carry[2 * hb]
            rows = pl.ds(pl.multiple_of(i * blk, blk), blk)
            qc = [jnp.concatenate([q_ref[rows, h * dn:(h + 1) * dn].astype(MXU_DTYPE),
                                   qr_ref[rows, h * dr:(h + 1) * dr]], axis=1) for h in hs]
            dov = [do_ref[rows, h * dv:(h + 1) * dv] for h in hs]
            s = [_dotb(qc[h], kc[h], NT) for h in hs]
            dp = [_dotb(dov[h], vv[h], NT) for h in hs]
            s = [s[h] * ATT_SCALE for h in hs]
            if diagonal:
                mask = _diagonal_mask(blk)
                s = [jnp.where(mask, s[h], NEG) for h in hs]
            p = [jnp.exp(s[h] - l_ref[h, rows, :][:, 0:1]) for h in hs]
            delta = [jnp.sum(dov[h].astype(F32) * o_ref[rows, h * dv:(h + 1) * dv].astype(F32), axis=-1, keepdims=True)
                     for h in hs]
            ds = [p[h] * (dp[h] - delta[h]) * ATT_SCALE for h in hs]
            dvn = [_dotb(p[h], dov[h], TN) for h in hs]
            dkc = [_dotb(ds[h], qc[h], TN) for h in hs]
            dqc = [_dotb(ds[h], kc[h], NN) for h in hs]
            for h in hs:
                dqn_ref[rows, h * dn:(h + 1) * dn] += dqc[h][:, 0:dn]
                dqr_ref[rows, h * dr:(h + 1) * dr] += dqc[h][:, dn:dn + dr]
            dkr_new = dkr_acc
            for h in hs:
                dkr_new = dkr_new + dkc[h][:, dn:dn + dr]
            return (tuple(dkn_acc[h] + dkc[h][:, 0:dn] for h in hs) + tuple(dv_acc[h] + dvn[h] for h in hs)
                    + (dkr_new,))

        init = (jnp.zeros((blk, dn), F32),) * hb + (jnp.zeros((blk, dv), F32),) * hb + (jnp.zeros((blk, dr), F32),)
        out = lax.fori_loop(j + 1, nb, step, step(j, init, diagonal=True))
        for h in hs:
            dkv_ref[:, h * (dn + dv):h * (dn + dv) + dn] = out[h].astype(dkv_ref.dtype)
            dkv_ref[:, h * (dn + dv) + dn:(h + 1) * (dn + dv)] = out[hb + h].astype(dkv_ref.dtype)
        dkr_ref[0, :, 0:dr] = out[2 * hb]
        dkr_ref[0, :, dr:LANES] = jnp.zeros((blk, LANES - dr), F32)

    full = lambda w: pl.BlockSpec((t, w), lambda g, j: (0, g))
    return pl.pallas_call(
        body, name=name, grid=(ng, nb),
        in_specs=[full(hb * dn), full(hb * dr), pl.BlockSpec((blk, hb * (dn + dv)), lambda g, j: (j, g)),
                  pl.BlockSpec((blk, LANES), lambda g, j: (j, 0)), full(hb * dv),
                  pl.BlockSpec((hb, t, LANES), lambda g, j: (g, 0, 0)), full(hb * dv)],
        out_specs=(full(hb * dn), full(hb * dr), pl.BlockSpec((blk, hb * (dn + dv)), lambda g, j: (j, g)),
                   pl.BlockSpec((1, blk, LANES), lambda g, j: (g, j, 0))),
        out_shape=(jax.ShapeDtypeStruct((t, nh * dn), F32), jax.ShapeDtypeStruct((t, nh * dr), F32),
                   jax.ShapeDtypeStruct((t, nh * (dn + dv)), BF16), jax.ShapeDtypeStruct((ng, t, LANES), F32)),
        compiler_params=_params("parallel", "arbitrary"),
    )(qf, qr, kvf, kr, o, lse, do)


def _ada_mod(c_all, ada_w, ada_b_cols, *, name):
    nl, d, wc = ada_w.shape

    def body(c_ref, w_ref, b_ref, o_ref):
        cv = c_ref[...]
        o_ref[0] = _dotb(cv * _sigmoid(cv), w_ref[0], NN) + b_ref[0]

    return pl.pallas_call(
        body, name=name, grid=(nl,),
        in_specs=[_const_spec((N_DEV, d)), pl.BlockSpec((1, d, wc), lambda l: (l, 0, 0)),
                  pl.BlockSpec((1, 1, wc), lambda l: (l, 0, 0))],
        out_specs=pl.BlockSpec((1, N_DEV, wc), lambda l: (l, 0, 0)),
        out_shape=jax.ShapeDtypeStruct((nl, N_DEV, wc), F32), compiler_params=_params("parallel"),
    )(c_all, ada_w, ada_b_cols)


def _adam_math(g, w, m, v):
    m2 = ADAM_B1 * m + (1.0 - ADAM_B1) * g
    v2 = ADAM_B2 * v + (1.0 - ADAM_B2) * (g * g)
    delta = -ADAM_LR * ((m2 / ADAM_BC1) / (jnp.sqrt(v2 / ADAM_BC2) + ADAM_EPS) + ADAM_WD * w)
    return delta, m2, v2


def _ada_grad_adamw(c_all, dmod_cols, w, m, v, *, name):
    nl, d, wc = w.shape
    tr = 256

    def body(c_ref, dm_ref, w_ref, m_ref, v_ref, g_ref, d_ref, m2_ref, v2_ref):
        cv = c_ref[...]
        g = _dotf(cv * _sigmoid(cv), dm_ref[0], TN)
        delta, m2, v2 = _adam_math(g, w_ref[0], m_ref[0], v_ref[0])
        g_ref[0], d_ref[0], m2_ref[0], v2_ref[0] = g, delta, m2, v2

    blk = pl.BlockSpec((1, tr, wc), lambda l, i: (l, i, 0))
    return pl.pallas_call(
        body, name=name, grid=(nl, d // tr),
        in_specs=[pl.BlockSpec((N_DEV, tr), lambda l, i: (0, i)), pl.BlockSpec((1, N_DEV, wc), lambda l, i: (l, 0, 0)),
                  blk, blk, blk],
        out_specs=(blk,) * 4, out_shape=(jax.ShapeDtypeStruct(w.shape, F32),) * 4,
        compiler_params=_params("parallel", "parallel"),
    )(c_all, dmod_cols, w, m, v)


def _adamw(parts, w, m, v, *, name):
    nl, r, c = w.shape
    ns = parts[0].shape[0]
    lanes_padded = -(-c // LANES) * LANES
    row_bytes = 2 * nl * ns * lanes_padded * parts[0].dtype.itemsize
    tr = _pick(r, min(256, max(16, (VMEM_LIMIT // 2) // row_bytes)), 16)
    tc = c
    if tr * row_bytes > VMEM_LIMIT // 2:
        tc = _pick(c, max(LANES, c * (VMEM_LIMIT // 2) // (tr * row_bytes)))

    def body(*refs):
        p_refs = refs[:nl]
        w_ref, m_ref, v_ref, g_ref, d_ref, m2_ref, v2_ref = refs[nl:]
        layer = pl.program_id(0)
        for q in range(nl):
            @pl.when(layer == q)
            def _(q=q):
                g = p_refs[q][0].astype(F32)
                for s in range(1, ns):
                    g = g + p_refs[q][s].astype(F32)
                delta, m2, v2 = _adam_math(g, w_ref[0], m_ref[0], v_ref[0])
                g_ref[0], d_ref[0], m2_ref[0], v2_ref[0] = g, delta, m2, v2

    blk = pl.BlockSpec((1, tr, tc), lambda l, i, j: (l, i, j))
    p_specs = [pl.BlockSpec((ns, tr, tc), lambda l, i, j, q=q: (0, jnp.where(l == q, i, 0), jnp.where(l == q, j, 0)))
               for q in range(nl)]
    return pl.pallas_call(
        body, name=name, grid=(nl, r // tr, c // tc),
        in_specs=p_specs + [blk, blk, blk],
        out_specs=(blk,) * 4, out_shape=(jax.ShapeDtypeStruct(w.shape, F32),) * 4,
        compiler_params=_params("arbitrary", "arbitrary", "arbitrary"),
    )(*parts, w, m, v)


def _sum_parts(parts, *, name):
    ns, r, c = parts.shape

    def body(p_ref, o_ref):
        acc = p_ref[0]
        for s in range(1, ns):
            acc = acc + p_ref[s]
        o_ref[...] = acc

    return pl.pallas_call(
        body, name=name, out_shape=jax.ShapeDtypeStruct((r, c), F32),
        in_specs=[pl.BlockSpec(memory_space=pltpu.VMEM)], out_specs=pl.BlockSpec(memory_space=pltpu.VMEM),
    )(parts)


def _pack(arrs):
    flat = jnp.concatenate([a.reshape(-1).astype(F32) for a in arrs])
    pad = (-flat.shape[0]) % (8 * LANES)
    return jnp.pad(flat, (0, pad)).reshape(-1, LANES)


def _unpack(packed, shapes, lead=()):
    flat = packed.reshape(lead + (-1,))
    out, off = [], 0
    for s in shapes:
        n = math.prod(s)
        out.append(flat[..., off:off + n].reshape(lead + tuple(s)))
        off += n
    return out


def _gather_rows(g):
    _, nl, rs, c = g.shape
    return jnp.transpose(g, (1, 0, 2, 3)).reshape(nl, N_DEV * rs, c)


def _row(v):
    return v.reshape(1, -1)


def _local_step(x, target, mod, cos_t, sin_t, rep, get_weights, put_grads):
    t = x.shape[0]
    saved = []
    for layer in range(DEPTH):
        j = layer // 2
        tag = f"l{layer}"
        shift_m, scale_m, gate_m, shift_f, scale_f, gate_f = [_row(mod[layer, i]) for i in range(N_MOD)]
        lw = dict(get_weights(layer, "mix", x))
        rec = {"x0": x, "lw": lw}
        h = _adaln_fwd(x, _row(rep["norm_mix_g"][layer]), scale_m, shift_m, name=f"adaln_mix_{tag}")
        rec["h"] = h
        if layer % 2 == 0:
            proj = _mm(h, lw["wt_in"], mode="nt", out_dtype=F32, tm=256, tn=GDN_MAIN, b_rows=GDN_MAIN,
                       dep=lw["dep_mix"], name=f"gdn_in_{tag}")
            ab = _mm(h, lw["wt_ab"], mode="nt", out_dtype=F32, name=f"gdn_in_ab_{tag}")
            qkv = _gdn_prep_fwd(proj, rep["gdn_conv_wt"][j], name=f"gdn_prep_{tag}")
            gbeta = _gdn_gate_fwd(ab, rep["gdn_gate_prm"][j], name=f"gdn_gate_{tag}")
            o, states, tinvs = _gdn_chunk_fwd(qkv, gbeta, name=f"gdn_chunk_{tag}")
            og = _gdn_onorm_fwd(o, proj, _row(rep["gdn_norm_g"][j]), name=f"gdn_onorm_{tag}")
            x, y = _mm_resid(og, lw["w_out"], x, gate_m, name=f"gdn_out_{tag}")
            rec.update(proj=proj, ab=ab, qkv=qkv, gbeta=gbeta, states=states, tinvs=tinvs, o=o, og=og, y=y)
        else:
            proj = _mm(h, lw["w_in"], mode="nn", out_dtype=F32, dep=lw["dep_mix"], name=f"mla_in_{tag}")
            cq, ck = _mla_prep_fwd(proj, _row(rep["mla_q_norm_g"][j]), _row(rep["mla_kv_norm_g"][j]),
                                   name=f"mla_prep_{tag}")
            qf = _mm(cq, lw["wt_uq"], mode="nt", out_dtype=BF16, name=f"mla_uq_{tag}")
            kvf = _mm(ck, lw["w_ukv"], mode="nn", out_dtype=BF16, name=f"mla_ukv_{tag}")
            qr, kr = _rope_qk(qf, proj, cos_t, sin_t, name=f"rope_{tag}")
            oc, lse = _attn_tm_fwd(qf, qr, kvf, kr, name=f"attn_{tag}")
            x, y = _mm_resid(oc, lw["w_out"], x, gate_m, name=f"mla_out_{tag}")
            rec.update(proj=proj, cq=cq, ck=ck, qf=qf, qr=qr, kvf=kvf, kr=kr, lse=lse, oc=oc, y=y)
        rec["x1"] = x
        lw.update(get_weights(layer, "ffn", x))
        h2 = _adaln_fwd(x, _row(rep["norm_ffn_g"][layer]), scale_f, shift_f, name=f"adaln_ffn_{tag}")
        s, a2, b2 = _ffn_gu_fwd(h2, lw["wt_g"], lw["wt_u"], lw["dep_ffn"], name=f"ffn_gu_{tag}")
        x, y2 = _mm_resid(s, lw["w_down"], x, gate_f, tm=512, name=f"ffn_down_{tag}")
        rec.update(h2=h2, a2=a2, b2=b2, s=s, y2=y2)
        saved.append(rec)

    dx, st, ls = _loss_head(x, _row(rep["final_norm_g"]), target, name="loss_head")
    loss = ls[0, 0]
    grads = {"final_norm_g": st[0]}
    per_layer = {k: [None] * DEPTH for k in ("norm_mix_g", "norm_ffn_g")}
    per_gdn = {k: [None] * 2 for k in ("gdn_conv_wt", "gdn_a_log", "gdn_dt_bias", "gdn_norm_g")}
    per_mla = {k: [None] * 2 for k in ("mla_q_norm_g", "mla_kv_norm_g")}
    dmod = [None] * DEPTH
    dep = jnp.zeros((8, LANES), F32)

    for layer in reversed(range(DEPTH)):
        j = layer // 2
        tag = f"l{layer}"
        rec = saved[layer]
        lw = rec["lw"]
        shift_m, scale_m, gate_m, shift_f, scale_f, gate_f = [_row(mod[layer, i]) for i in range(N_MOD)]
        if layer == DEPTH - 1:
            dy2, st_g = _gate_bwd(dx, rec["y2"], gate_f, dep, name=f"gate_bwd_ffn_{tag}")
            dgate_f = st_g[0]
        dw_down = _mm(rec["s"], dy2, mode="tn", out_dtype=BF16, tm=FFN_BLOCK, tn=1024, name=f"ffn_down_dw_{tag}")
        da2, db2 = _ffn_down_dx(dy2, lw["w_down"], rec["a2"], rec["b2"], name=f"ffn_down_dx_{tag}")
        dwt_g = _mm(da2, rec["h2"], mode="tn", out_dtype=BF16, tm=FFN_BLOCK, tn=1024, name=f"ffn_g_dw_{tag}")
        dwt_u = _mm(db2, rec["h2"], mode="tn", out_dtype=BF16, tm=FFN_BLOCK, tn=1024, name=f"ffn_u_dw_{tag}")
        dep = put_grads(layer, "ffn", {"wt_g": dwt_g, "wt_u": dwt_u, "w_down": dw_down})
        dh2 = _mm(da2, lw["wt_g"], mode="nn", out_dtype=F32, tm=512, tn=1024, name=f"ffn_g_dx_{tag}")
        dh2 = _mm(db2, lw["wt_u"], mode="nn", out_dtype=BF16, add=dh2, tm=512, tn=1024, name=f"ffn_u_dx_{tag}")
        dx, st_n, dy = _adaln_gate_bwd(rec["x1"], _row(rep["norm_ffn_g"][layer]), scale_f, shift_f, dh2, dx, dep,
                                       rec["y"], gate_m, name=f"adaln_ffn_bwd_{tag}")
        per_layer["norm_ffn_g"][layer] = st_n[0]
        dscale_f, dshift_f, dgate_m = st_n[1], st_n[2], st_n[3]
        big = {}
        if layer % 2 == 0:
            big["w_out"] = _mm(rec["og"], dy, mode="tn", out_dtype=BF16, name=f"gdn_out_dw_{tag}")
            dog = _mm(dy, lw["w_out"], mode="nt", out_dtype=BF16, name=f"gdn_out_dx_{tag}")
            do, dgp, st_o = _gdn_onorm_bwd(rec["o"], rec["proj"], _row(rep["gdn_norm_g"][j]), dog,
                                           name=f"gdn_onorm_bwd_{tag}")
            per_gdn["gdn_norm_g"][j] = st_o[0]
            dqkv, dgb = _gdn_chunk_bwd(rec["qkv"], rec["gbeta"], rec["states"], rec["tinvs"], do,
                                       name=f"gdn_chunk_bwd_{tag}")
            dab, st_a = _gdn_gate_bwd(rec["ab"], rep["gdn_gate_prm"][j], dgb, name=f"gdn_gate_bwd_{tag}")
            per_gdn["gdn_a_log"][j] = st_a[0, :GDN_HEADS]
            per_gdn["gdn_dt_bias"][j] = st_a[1, :GDN_HEADS]
            dpre, dcw = _gdn_prep_bwd(rec["proj"], rep["gdn_conv_wt"][j], dqkv, name=f"gdn_prep_bwd_{tag}")
            per_gdn["gdn_conv_wt"][j] = dcw
            dproj = jnp.concatenate([dpre, dgp], axis=1)
            dw_main = _mm(dproj, rec["h"], mode="tn", out_dtype=BF16, tm=512, tn=1024, name=f"gdn_in_dw_{tag}")
            dw_ab = _mm(dab, rec["h"], mode="tn", out_dtype=BF16, tn=1024, name=f"gdn_in_ab_dw_{tag}")
            big["wt_in"] = jnp.concatenate([dw_main, dw_ab[:2 * GDN_HEADS]], axis=0)
            dep = put_grads(layer, "gdn", big)
            dh_ab = _mm(dab, lw["wt_ab"], mode="nn", out_dtype=F32, tn=1024, name=f"gdn_in_ab_dx_{tag}")
            dh = _mm(dproj, lw["wt_in"], mode="nn", out_dtype=BF16, add=dh_ab, tm=256, tn=1024, b_rows=GDN_MAIN,
                     name=f"gdn_in_dx_{tag}")
        else:
            big["w_out"] = _mm(rec["oc"], dy, mode="tn", out_dtype=BF16, name=f"mla_out_dw_{tag}")
            doc = _mm(dy, lw["w_out"], mode="nt", out_dtype=BF16, name=f"mla_out_dx_{tag}")
            dqn, dqr, dkvf, dkr_parts = _attn_tm_bwd(rec["qf"], rec["qr"], rec["kvf"], rec["kr"], rec["oc"],
                                                     rec["lse"], doc, name=f"attn_bwd_{tag}")
            dqr_un, dkr_un = _rope_qk_bwd(dqr, dkr_parts, cos_t, sin_t, name=f"rope_bwd_{tag}")
            n_nope = MLA_HEADS * MLA_NOPE
            big["wt_uq"] = jnp.concatenate(
                [_mm(dqn, rec["cq"], mode="tn", out_dtype=BF16, name=f"mla_uq_dw_nope_{tag}"),
                 _mm(dqr_un, rec["cq"], mode="tn", out_dtype=BF16, name=f"mla_uq_dw_rope_{tag}")], axis=0)
            big["w_ukv"] = _mm(rec["ck"], dkvf, mode="tn", out_dtype=BF16, name=f"mla_ukv_dw_{tag}")
            dcq = _mm(dqr_un, lw["wt_uq"][n_nope:], mode="nn", out_dtype=F32, name=f"mla_uq_dx_rope_{tag}")
            dcq = _mm(dqn, lw["wt_uq"], mode="nn", out_dtype=F32, add=dcq, b_rows=n_nope,
                      name=f"mla_uq_dx_nope_{tag}")
            dck = _mm(dkvf, lw["w_ukv"], mode="nt", out_dtype=F32, name=f"mla_ukv_dx_{tag}")
            dproj, st_p = _mla_prep_bwd(rec["proj"], _row(rep["mla_q_norm_g"][j]), _row(rep["mla_kv_norm_g"][j]),
                                        dcq, dck, dkr_un, name=f"mla_prep_bwd_{tag}")
            per_mla["mla_q_norm_g"][j] = st_p[0, :MLA_Q_RANK]
            per_mla["mla_kv_norm_g"][j] = st_p[0, MLA_Q_RANK:MLA_Q_RANK + MLA_KV_RANK]
            big["w_in"] = _mm(rec["h"], dproj, mode="tn", out_dtype=BF16, name=f"mla_in_dw_{tag}")
            dep = put_grads(layer, "mla", big)
            dh = _mm(dproj, lw["w_in"], mode="nt", out_dtype=BF16, name=f"mla_in_dx_{tag}")
        if layer > 0:
            below = saved[layer - 1]
            dx, st_n, dy2 = _adaln_gate_bwd(rec["x0"], _row(rep["norm_mix_g"][layer]), scale_m, shift_m, dh, dx, dep,
                                            below["y2"], _row(mod[layer - 1, N_MOD - 1]),
                                            name=f"adaln_mix_bwd_{tag}")
        else:
            dx, st_n = _adaln_bwd(rec["x0"], _row(rep["norm_mix_g"][layer]), scale_m, shift_m, dh, dx, dep,
                                  name=f"adaln_mix_bwd_{tag}")
        per_layer["norm_mix_g"][layer] = st_n[0]
        dmod[layer] = jnp.stack([st_n[2], st_n[1], dgate_m, dshift_f, dscale_f, dgate_f])
        if layer > 0:
            dgate_f = st_n[3]

    for d in (per_layer, per_gdn, per_mla):
        for k, v in d.items():
            grads[k] = jnp.stack(v)
    return loss, dx, jnp.stack(dmod), grads


BIG = ("gdn_w_in", "gdn_w_out", "mla_w_in", "mla_w_uq", "mla_w_ukv", "mla_w_out", "ffn_w_gate", "ffn_w_up",
       "ffn_w_down")
TRANSPOSED = ("gdn_w_in", "mla_w_uq", "ffn_w_gate", "ffn_w_up")
AHEAD = 4


def _view(k, a):
    return jnp.transpose(a, (0, 2, 1)) if k in TRANSPOSED else a
SMALL = ("ada_b", "norm_mix_g", "norm_ffn_g", "gdn_conv_w", "gdn_a_log", "gdn_dt_bias", "gdn_norm_g",
         "mla_q_norm_g", "mla_kv_norm_g", "final_norm_g")
WEIGHTS = ("ada_w", "ada_b", "norm_mix_g", "norm_ffn_g", "gdn_w_in", "gdn_conv_w", "gdn_a_log", "gdn_dt_bias",
           "gdn_norm_g", "gdn_w_out", "mla_w_in", "mla_q_norm_g", "mla_kv_norm_g", "mla_w_uq", "mla_w_ukv",
           "mla_w_out", "ffn_w_gate", "ffn_w_up", "ffn_w_down", "final_norm_g")


def _uq_to_kernel_layout(w, axis=-1):
    axis = axis % w.ndim
    lead, tail = w.shape[:axis], w.shape[axis + 1:]
    w4 = w.reshape(lead + (MLA_HEADS, MLA_QK) + tail)
    nope = lax.slice_in_dim(w4, 0, MLA_NOPE, axis=axis + 1).reshape(lead + (-1,) + tail)
    rope = lax.slice_in_dim(w4, MLA_NOPE, MLA_QK, axis=axis + 1).reshape(lead + (-1,) + tail)
    return jnp.concatenate([nope, rope], axis=axis)


def _uq_from_kernel_layout(w, axis=-1):
    axis = axis % w.ndim
    lead, tail = w.shape[:axis], w.shape[axis + 1:]
    nope = lax.slice_in_dim(w, 0, MLA_HEADS * MLA_NOPE, axis=axis).reshape(lead + (MLA_HEADS, MLA_NOPE) + tail)
    rope = lax.slice_in_dim(w, MLA_HEADS * MLA_NOPE, MLA_HEADS * MLA_QK, axis=axis).reshape(
        lead + (MLA_HEADS, MLA_ROPE) + tail)
    return jnp.concatenate([nope, rope], axis=axis + 1).reshape(lead + (-1,) + tail)


def _group_names(layer, kind):
    if kind == "ffn":
        return ("ffn_w_gate", "ffn_w_up", "ffn_w_down")
    return ("gdn_w_in", "gdn_w_out") if layer % 2 == 0 else ("mla_w_in", "mla_w_uq", "mla_w_ukv", "mla_w_out")


def _layer_index(name, layer):
    return layer if name.startswith("ffn") else layer // 2


def _cols(g):
    return jnp.transpose(g, (1, 0, 2)).reshape(g.shape[1], N_DEV * g.shape[2])


def _rows(g):
    return g.reshape(N_DEV * g.shape[1], g.shape[2])


def _uncols(full):
    r, c = full.shape
    return jnp.transpose(full.reshape(r, N_DEV, c // N_DEV), (1, 0, 2))


def _unrows(full):
    r, c = full.shape
    return full.reshape(N_DEV, r // N_DEV, c)


def _group_weights(layer, kind, got, token):
    if kind == "ffn":
        return {"wt_g": _rows(got["ffn_w_gate"]), "wt_u": _rows(got["ffn_w_up"]), "w_down": _rows(got["ffn_w_down"]),
                "dep_ffn": token}
    if layer % 2 == 0:
        wt_in = _rows(got["gdn_w_in"])
        return dict(wt_in=wt_in, wt_ab=jnp.pad(wt_in[GDN_MAIN:], ((0, LANES - 2 * GDN_HEADS), (0, 0))),
                    w_out=_rows(got["gdn_w_out"]), dep_mix=token)
    return dict(w_in=_rows(got["mla_w_in"]), wt_uq=_uq_to_kernel_layout(_rows(got["mla_w_uq"]), axis=0),
                w_ukv=_cols(got["mla_w_ukv"]), w_out=_rows(got["mla_w_out"]), dep_mix=token)


def _layer_grad_slots(kind, big):
    if kind == "ffn":
        return {"ffn_w_gate": _unrows(big["wt_g"]), "ffn_w_up": _unrows(big["wt_u"]),
                "ffn_w_down": _unrows(big["w_down"])}
    if kind == "gdn":
        return {"gdn_w_in": _unrows(big["wt_in"]), "gdn_w_out": _unrows(big["w_out"])}
    return {"mla_w_in": _unrows(big["w_in"]), "mla_w_uq": _unrows(_uq_from_kernel_layout(big["wt_uq"], axis=0)),
            "mla_w_ukv": _uncols(big["w_ukv"]), "mla_w_out": _unrows(big["w_out"])}


def _small_weights(tiny, rep):
    prm = jnp.zeros((2, 8, LANES), F32)
    prm = prm.at[:, 0, :GDN_HEADS].set(rep["gdn_a_log"]).at[:, 1, :GDN_HEADS].set(rep["gdn_dt_bias"])
    out = {
        "gdn_conv_wt": jnp.transpose(_gather_rows(tiny["gdn_conv_w"]), (0, 2, 1)),
        "mla_q_norm_g": jnp.transpose(tiny["mla_q_norm_g"], (1, 0, 2)).reshape(2, MLA_Q_RANK),
        "mla_kv_norm_g": jnp.transpose(tiny["mla_kv_norm_g"], (1, 0, 2)).reshape(2, MLA_KV_RANK),
        "gdn_gate_prm": prm,
    }
    for k in ("norm_mix_g", "norm_ffn_g", "gdn_norm_g", "final_norm_g"):
        out[k] = rep[k]
    return out


def _rope_tables(positions):
    inv_freq = ROPE_THETA ** (-jnp.arange(0, MLA_ROPE, 2, dtype=F32) / MLA_ROPE)
    ang = positions.astype(F32)[:, None] * inv_freq
    cos, sin = jnp.cos(ang), jnp.sin(ang)
    reps = LANES // MLA_ROPE
    return jnp.tile(jnp.concatenate([cos, cos], axis=1), (1, reps)), jnp.tile(
        jnp.concatenate([-sin, sin], axis=1), (1, reps))


def kernel(x, c, positions, ada_w, ada_b, norm_mix_g, norm_ffn_g, gdn_w_in, gdn_conv_w, gdn_a_log, gdn_dt_bias, gdn_norm_g, gdn_w_out, mla_w_in, mla_q_norm_g, mla_kv_norm_g, mla_w_uq, mla_w_ukv, mla_w_out, ffn_w_gate, ffn_w_up, ffn_w_down, final_norm_g, loss_target, m_ada_w, m_ada_b, m_norm_mix_g, m_norm_ffn_g, m_gdn_w_in, m_gdn_conv_w, m_gdn_a_log, m_gdn_dt_bias, m_gdn_norm_g, m_gdn_w_out, m_mla_w_in, m_mla_q_norm_g, m_mla_kv_norm_g, m_mla_w_uq, m_mla_w_ukv, m_mla_w_out, m_ffn_w_gate, m_ffn_w_up, m_ffn_w_down, m_final_norm_g, v_ada_w, v_ada_b, v_norm_mix_g, v_norm_ffn_g, v_gdn_w_in, v_gdn_conv_w, v_gdn_a_log, v_gdn_dt_bias, v_gdn_norm_g, v_gdn_w_out, v_mla_w_in, v_mla_q_norm_g, v_mla_kv_norm_g, v_mla_w_uq, v_mla_w_ukv, v_mla_w_out, v_ffn_w_gate, v_ffn_w_up, v_ffn_w_down, v_final_norm_g):
    W = dict(ada_w=ada_w, ada_b=ada_b, norm_mix_g=norm_mix_g, norm_ffn_g=norm_ffn_g, gdn_w_in=gdn_w_in,
             gdn_conv_w=gdn_conv_w, gdn_a_log=gdn_a_log, gdn_dt_bias=gdn_dt_bias, gdn_norm_g=gdn_norm_g,
             gdn_w_out=gdn_w_out, mla_w_in=mla_w_in, mla_q_norm_g=mla_q_norm_g, mla_kv_norm_g=mla_kv_norm_g,
             mla_w_uq=mla_w_uq, mla_w_ukv=mla_w_ukv, mla_w_out=mla_w_out, ffn_w_gate=ffn_w_gate,
             ffn_w_up=ffn_w_up, ffn_w_down=ffn_w_down, final_norm_g=final_norm_g)
    M = dict(ada_w=m_ada_w, ada_b=m_ada_b, norm_mix_g=m_norm_mix_g, norm_ffn_g=m_norm_ffn_g, gdn_w_in=m_gdn_w_in,
             gdn_conv_w=m_gdn_conv_w, gdn_a_log=m_gdn_a_log, gdn_dt_bias=m_gdn_dt_bias, gdn_norm_g=m_gdn_norm_g,
             gdn_w_out=m_gdn_w_out, mla_w_in=m_mla_w_in, mla_q_norm_g=m_mla_q_norm_g,
             mla_kv_norm_g=m_mla_kv_norm_g, mla_w_uq=m_mla_w_uq, mla_w_ukv=m_mla_w_ukv, mla_w_out=m_mla_w_out,
             ffn_w_gate=m_ffn_w_gate, ffn_w_up=m_ffn_w_up, ffn_w_down=m_ffn_w_down, final_norm_g=m_final_norm_g)
    V = dict(ada_w=v_ada_w, ada_b=v_ada_b, norm_mix_g=v_norm_mix_g, norm_ffn_g=v_norm_ffn_g, gdn_w_in=v_gdn_w_in,
             gdn_conv_w=v_gdn_conv_w, gdn_a_log=v_gdn_a_log, gdn_dt_bias=v_gdn_dt_bias, gdn_norm_g=v_gdn_norm_g,
             gdn_w_out=v_gdn_w_out, mla_w_in=v_mla_w_in, mla_q_norm_g=v_mla_q_norm_g,
             mla_kv_norm_g=v_mla_kv_norm_g, mla_w_uq=v_mla_w_uq, mla_w_ukv=v_mla_w_ukv, mla_w_out=v_mla_w_out,
             ffn_w_gate=v_ffn_w_gate, ffn_w_up=v_ffn_w_up, ffn_w_down=v_ffn_w_down, final_norm_g=v_final_norm_g)
    me = 4 * lax.axis_index("x") + 2 * lax.axis_index("y") + lax.axis_index("c")
    t = x.shape[1]
    wc = ada_w.shape[-1]

    groups = [(layer, kind) for layer in range(DEPTH) for kind in ("mix", "ffn")]

    def group_srcs(i):
        layer, kind = groups[i]
        return [_view(k, W[k])[_layer_index(k, layer)].astype(BF16) for k in _group_names(layer, kind)]

    tiny_shapes = [c.shape, gdn_conv_w.shape, mla_q_norm_g.shape, mla_kv_norm_g.shape]
    first = _gather_two_level([_pack([c, gdn_conv_w, mla_q_norm_g, mla_kv_norm_g])] + group_srcs(0),
                              name="gather_first")
    tiny_g = first[0]
    c_g, conv_g, qn_g, kvn_g = _unpack(tiny_g, tiny_shapes, lead=(N_DEV,))
    c_all = c_g.reshape(N_DEV, D_MODEL)
    rep = _small_weights({"gdn_conv_w": conv_g, "mla_q_norm_g": qn_g, "mla_kv_norm_g": kvn_g}, W)

    def start_group(i, dep):
        layer, kind = groups[i]
        return _exchange_start(group_srcs(i), scatter=False, name=f"gather_start_{kind}_l{layer}", dep=dep)


    b_cols = lax.dynamic_slice_in_dim(ada_b, me * wc, wc, axis=1).reshape(DEPTH, 1, wc)
    mod_part = _ada_mod(c_all, ada_w, b_cols, name="ada_mod")
    (mod_g,) = _exchange([mod_part], scatter=False, name="gather_mod")
    mod_mine = lax.dynamic_index_in_dim(mod_g, me, axis=2, keepdims=False)
    mod = jnp.transpose(mod_mine, (1, 0, 2)).reshape(DEPTH, N_MOD, D_MODEL)
    gather = {1: start_group(1, mod_g)}
    for i in range(2, AHEAD + 1):
        gather[i] = start_group(i, gather[i - 1][4])

    def get_weights(layer, kind, after):
        i = groups.index((layer, kind))
        names = _group_names(layer, kind)
        if i == 0:
            return _group_weights(layer, kind, dict(zip(names, first[1:])), gather[AHEAD][4])
        srcs, lands = _exchange_wait(gather[i], after, scatter=False, name=f"gather_wait_{kind}_l{layer}")
        token = jnp.zeros((8, LANES), F32)
        if i + AHEAD < len(groups):
            gather[i + AHEAD] = start_group(i + AHEAD, lands[0])
            token = gather[i + AHEAD][4]
        got = {k: lax.dynamic_update_index_in_dim(z, s, me, 0) for k, s, z in zip(names, srcs, lands)}
        return _group_weights(layer, kind, got, token)

    scatter = []

    def put_grads(layer, kind, big):
        slots = _layer_grad_slots(kind, big)
        started = _exchange_start(list(slots.values()), scatter=True, name=f"scatter_start_{kind}_l{layer}")
        scatter.append((layer, kind, list(slots.keys()), started))
        return started[4]

    cos_t, sin_t = _rope_tables(positions[0])
    loss, dx, dmod, g = _local_step(x[0], loss_target[0], mod, cos_t, sin_t, rep, get_weights, put_grads)

    parts = {k: [None] * W[k].shape[0] for k in BIG}
    res = {}

    def wait_group(entry, after):
        layer, kind, names, started = entry
        srcs, lands = _exchange_wait(started, after, scatter=True, name=f"scatter_wait_{kind}_l{layer}")
        for k, s, z in zip(names, srcs, lands):
            own = lax.dynamic_index_in_dim(s, me, 0, keepdims=False)
            parts[k][_layer_index(k, layer)] = lax.dynamic_update_index_in_dim(z, own, me, 0)

    for entry in scatter[:-1]:
        wait_group(entry, dx)
    early = [k for k in BIG if k not in scatter[-1][2]]
    def update(k):
        outs = _adamw(parts[k], _view(k, W[k]), _view(k, M[k]), _view(k, V[k]), name=f"adamw_{k}")
        return tuple(_view(k, o) for o in outs)

    for k in early:
        res[k] = update(k)
    loss, dmod, done = lax.optimization_barrier((loss, dmod, [res[k] for k in early]))
    for k, r in zip(early, done):
        res[k] = r

    small_local = [dmod.reshape(DEPTH, N_MOD * D_MODEL), g["norm_mix_g"], g["norm_ffn_g"],
                   jnp.transpose(g["gdn_conv_wt"], (0, 2, 1)), g["gdn_a_log"], g["gdn_dt_bias"], g["gdn_norm_g"],
                   g["mla_q_norm_g"], g["mla_kv_norm_g"], g["final_norm_g"], loss.reshape(1)]
    small_shapes = [a.shape for a in small_local]
    (small_g,) = _exchange([_pack(small_local)], scatter=False, name="gather_small_grads")
    small_sum = _unpack(_sum_parts(small_g, name="sum_small_grads"), small_shapes)
    loss = small_sum[-1][0]
    dmod_all = _unpack(small_g, small_shapes[:1], lead=(N_DEV,))[0]
    sg = dict(zip(SMALL, small_sum))
    wait_group(scatter[-1], small_g)
    sg["gdn_conv_w"] = lax.dynamic_slice_in_dim(sg["gdn_conv_w"], me * gdn_conv_w.shape[1], gdn_conv_w.shape[1], 1)
    sg["mla_q_norm_g"] = lax.dynamic_slice_in_dim(sg["mla_q_norm_g"], me * mla_q_norm_g.shape[1],
                                                  mla_q_norm_g.shape[1], 1)
    sg["mla_kv_norm_g"] = lax.dynamic_slice_in_dim(sg["mla_kv_norm_g"], me * mla_kv_norm_g.shape[1],
                                                   mla_kv_norm_g.shape[1], 1)

    dmod_cols = jnp.transpose(lax.dynamic_slice_in_dim(dmod_all, me * wc, wc, axis=2), (1, 0, 2))
    res["ada_w"] = _ada_grad_adamw(c_all, dmod_cols, ada_w, m_ada_w, v_ada_w, name="ada_w_grad_adamw")
    for k in BIG:
        if k not in early:
            res[k] = update(k)
    shapes = [W[k].shape for k in SMALL]
    packed = [_pack([d[k] for k in SMALL]) for d in (sg, W, M, V)]
    outs = _adamw([packed[0][None]], packed[1][None], packed[2][None], packed[3][None], name="adamw_small")
    unpacked = [_unpack(o[0], shapes) for o in outs]
    for i, k in enumerate(SMALL):
        res[k] = tuple(u[i] for u in unpacked)

    return (loss, dx[None], *[res[k][0] for k in WEIGHTS], *[res[k][1] for k in WEIGHTS],
            *[res[k][2] for k in WEIGHTS], *[res[k][3] for k in WEIGHTS])
```

```python
import math

import jax
import jax.numpy as jnp
from jax import lax
from jax.experimental import pallas as pl
from jax.experimental.pallas import tpu as pltpu

F32 = jnp.float32
BF16 = jnp.bfloat16
MXU_DTYPE = jnp.bfloat16

N_DEV = 8
D_MODEL = 1024
DEPTH = 4
GDN_HEADS = 8
GDN_HEAD_DIM = 128
GDN_KEY_DIM = GDN_HEADS * GDN_HEAD_DIM
GDN_CHUNK = 64
GDN_HEAD_BATCH = 8
GDN_CONV = 4
GDN_PREP_HEADS = 2
GDN_MAIN = 4 * GDN_KEY_DIM
MLA_HEADS = 8
MLA_NOPE = 128
MLA_ROPE = 64
MLA_V = 128
MLA_Q_RANK = 384
MLA_KV_RANK = 256
MLA_IN = MLA_Q_RANK + MLA_KV_RANK + MLA_ROPE
MLA_QK = MLA_NOPE + MLA_ROPE
ROPE_THETA = 10000.0
D_FF = 2816
N_MOD = 6
EPS = 1e-6
LANES = 128
VMEM_LIMIT = 48 * 1024 * 1024

ADAM_LR = 0.001
ADAM_B1 = 0.9
ADAM_B2 = 0.999
ADAM_EPS = 1e-08
ADAM_WD = 0.01
ADAM_STEP = 10
ADAM_BC1 = 1.0 - ADAM_B1 ** ADAM_STEP
ADAM_BC2 = 1.0 - ADAM_B2 ** ADAM_STEP

NN = (((1,), (0,)), ((), ()))
NT = (((1,), (1,)), ((), ()))
TN = (((0,), (0,)), ((), ()))
NEG = -1e30


def _dotb(a, b, dims):
    return lax.dot_general(a.astype(MXU_DTYPE), b.astype(MXU_DTYPE), dims, preferred_element_type=F32)


def _split(a):
    hi = a.astype(BF16)
    return hi, (a - hi.astype(F32)).astype(BF16)


def _dotf(a, b, dims):
    ah, al = _split(a)
    bh, bl = _split(b)
    dot = lambda u, v: lax.dot_general(u, v, dims, preferred_element_type=F32)
    return dot(ah, bh) + (dot(ah, bl) + dot(al, bh))


def _params(*sem):
    return pltpu.CompilerParams(dimension_semantics=sem, vmem_limit_bytes=VMEM_LIMIT)


def _pick(n, pref, mult=LANES):
    best = None
    t = mult
    while t <= min(n, pref):
        if n % t == 0:
            best = t
        t += mult
    return best if best is not None else n


def _sigmoid(z):
    return 0.5 * jnp.tanh(0.5 * z) + 0.5


def _exchange(arrays, *, scatter, name):
    n = len(arrays)
    out_shape = tuple(
        jax.ShapeDtypeStruct(a.shape if scatter else (N_DEV,) + a.shape, a.dtype) for a in arrays)

    def body(*refs):
        ins, outs = refs[:n], refs[n:2 * n]
        send_sems, recv_sems, local_sems = refs[2 * n:]
        x, y, c = lax.axis_index("x"), lax.axis_index("y"), lax.axis_index("c")
        me = 4 * x + 2 * y + c
        copies = []
        for k in range(n):
            src_own = ins[k].at[me] if scatter else ins[k]
            own = pltpu.make_async_copy(src_own, outs[k].at[me], local_sems.at[k])
            own.start()
            copies.append(own)
        sends = []
        for p in range(1, N_DEV):
            px, py, pc = x ^ ((p >> 2) & 1), y ^ ((p >> 1) & 1), c ^ (p & 1)
            peer = 4 * px + 2 * py + pc
            for k in range(n):
                cp = pltpu.make_async_remote_copy(
                    src_ref=ins[k].at[peer] if scatter else ins[k],
                    dst_ref=outs[k].at[me],
                    send_sem=send_sems.at[k, p - 1],
                    recv_sem=recv_sems.at[k, p - 1],
                    device_id=(px, py, pc),
                    device_id_type=pl.DeviceIdType.MESH,
                )
                cp.start()
                sends.append((cp, k, peer, p))
        for cp, k, peer, p in sends:
            pltpu.make_async_remote_copy(
                src_ref=ins[k].at[peer] if scatter else ins[k],
                dst_ref=outs[k].at[peer],
                send_sem=send_sems.at[k, p - 1],
                recv_sem=recv_sems.at[k, p - 1],
                device_id=(x, y, c),
                device_id_type=pl.DeviceIdType.MESH,
            ).wait_recv()
        for cp, _, _, _ in sends:
            cp.wait_send()
        for own in copies:
            own.wait()

    any_spec = pl.BlockSpec(memory_space=pl.ANY)
    outs = pl.pallas_call(
        body,
        name=name,
        out_shape=out_shape,
        in_specs=[any_spec] * n,
        out_specs=tuple([any_spec] * n),
        scratch_shapes=[
            pltpu.SemaphoreType.DMA((n, N_DEV - 1)),
            pltpu.SemaphoreType.DMA((n, N_DEV - 1)),
            pltpu.SemaphoreType.DMA((n,)),
        ],
        compiler_params=pltpu.CompilerParams(has_side_effects=True),
    )(*arrays)
    return list(outs)


def _gather_two_level(arrays, *, name):
    n = len(arrays)
    out_shape = tuple(jax.ShapeDtypeStruct((N_DEV,) + a.shape, a.dtype) for a in arrays)

    def body(*refs):
        ins, outs = refs[:n], refs[n:2 * n]
        send_sems, recv_sems, local_sems = refs[2 * n:]
        x, y, c = lax.axis_index("x"), lax.axis_index("y"), lax.axis_index("c")
        me = 4 * x + 2 * y + c
        sibling = (x, y, 1 - c)
        chips = [(1 - x, y), (x, 1 - y), (1 - x, 1 - y)]

        def slot(px, py, pc):
            return 4 * px + 2 * py + pc

        def copy(k, q, block, to, src=None):
            return pltpu.make_async_remote_copy(
                src_ref=outs[k].at[slot(*block)] if src is None else src,
                dst_ref=outs[k].at[slot(*block)],
                send_sem=send_sems.at[k, q], recv_sem=recv_sems.at[k, q],
                device_id=to, device_id_type=pl.DeviceIdType.MESH)

        own = [pltpu.make_async_copy(ins[k], outs[k].at[me], local_sems.at[k]) for k in range(n)]
        for cp in own:
            cp.start()
        first = []
        for k in range(n):
            first.append(copy(k, 0, (x, y, c), sibling, src=ins[k]))
            first += [copy(k, 1 + j, (x, y, c), (*chip, c), src=ins[k]) for j, chip in enumerate(chips)]
        for cp in first:
            cp.start()
        passed = []
        for j, chip in enumerate(chips):
            for k in range(n):
                copy(k, 1 + j, (*chip, c), (x, y, c)).wait_recv()
                fwd = copy(k, 4 + j, (*chip, c), sibling)
                fwd.start()
                passed.append(fwd)
        for k in range(n):
            copy(k, 0, sibling, (x, y, c)).wait_recv()
            for j, chip in enumerate(chips):
                copy(k, 4 + j, (*chip, 1 - c), (x, y, c)).wait_recv()
        for cp in first + passed:
            cp.wait_send()
        for cp in own:
            cp.wait()

    any_spec = pl.BlockSpec(memory_space=pl.ANY)
    outs = pl.pallas_call(
        body, name=name, out_shape=out_shape, in_specs=[any_spec] * n, out_specs=tuple([any_spec] * n),
        scratch_shapes=[pltpu.SemaphoreType.DMA((n, N_DEV - 1)), pltpu.SemaphoreType.DMA((n, N_DEV - 1)),
                        pltpu.SemaphoreType.DMA((n,))],
        compiler_params=pltpu.CompilerParams(has_side_effects=True),
    )(*arrays)
    return list(outs)


def _peer(x, y, c, p):
    return x ^ ((p >> 2) & 1), y ^ ((p >> 1) & 1), c ^ (p & 1)


def _exchange_start(arrays, *, scatter, name, dep=None):
    n = len(arrays)
    deps = [] if dep is None else [dep]
    lands = [lax.empty(a.shape if scatter else (N_DEV,) + a.shape, a.dtype) for a in arrays]

    def body(*refs):
        ins, zones = refs[:n], refs[n:2 * n]
        send_sems, recv_sems = refs[2 * n + len(deps)], refs[2 * n + len(deps) + 1]
        token = refs[-1]
        x, y, c = lax.axis_index("x"), lax.axis_index("y"), lax.axis_index("c")
        me = 4 * x + 2 * y + c
        for p in range(1, N_DEV):
            px, py, pc = _peer(x, y, c, p)
            for k in range(n):
                pltpu.make_async_remote_copy(
                    src_ref=ins[k].at[4 * px + 2 * py + pc] if scatter else ins[k],
                    dst_ref=zones[k].at[me],
                    send_sem=send_sems.at[k * (N_DEV - 1) + p - 1],
                    recv_sem=recv_sems.at[k * (N_DEV - 1) + p - 1],
                    device_id=(px, py, pc),
                    device_id_type=pl.DeviceIdType.MESH,
                ).start()
        token[...] = jnp.zeros_like(token)

    hbm = pl.BlockSpec(memory_space=pltpu.HBM)
    sem = pl.BlockSpec(memory_space=pltpu.SEMAPHORE)
    outs = pl.pallas_call(
        body,
        name=name,
        out_shape=(pltpu.SemaphoreType.DMA((n * (N_DEV - 1),)), pltpu.SemaphoreType.DMA((n * (N_DEV - 1),)),
                   *[pltpu.HBM(a.shape, a.dtype) for a in arrays], *[pltpu.HBM(z.shape, z.dtype) for z in lands],
                   jax.ShapeDtypeStruct((8, LANES), F32)),
        in_specs=[hbm] * (2 * n) + [pl.BlockSpec(memory_space=pl.ANY)] * len(deps),
        out_specs=(sem, sem, *[hbm] * (2 * n), pl.BlockSpec(memory_space=pltpu.VMEM)),
        input_output_aliases={k: 2 + k for k in range(2 * n)},
        compiler_params=pltpu.CompilerParams(has_side_effects=pltpu.SideEffectType.DATAFLOW_SIDE_EFFECTING),
    )(*[pltpu.with_memory_space_constraint(a, pltpu.HBM) for a in arrays],
      *[pltpu.with_memory_space_constraint(z, pltpu.HBM) for z in lands], *deps)
    return outs[0], outs[1], list(outs[2:2 + n]), list(outs[2 + n:2 + 2 * n]), outs[-1]


def _exchange_wait(started, after, *, scatter, name):
    send_sems, recv_sems, srcs, lands, _ = started
    n = len(srcs)

    def body(*refs):
        ins, zones = refs[:n], refs[n:2 * n]
        s_sems, r_sems = refs[2 * n], refs[2 * n + 1]
        x, y, c = lax.axis_index("x"), lax.axis_index("y"), lax.axis_index("c")
        for p in range(1, N_DEV):
            px, py, pc = _peer(x, y, c, p)
            peer = 4 * px + 2 * py + pc
            for k in range(n):
                cp = pltpu.make_async_remote_copy(
                    src_ref=ins[k].at[peer] if scatter else ins[k],
                    dst_ref=zones[k].at[peer],
                    send_sem=s_sems.at[k * (N_DEV - 1) + p - 1],
                    recv_sem=r_sems.at[k * (N_DEV - 1) + p - 1],
                    device_id=(px, py, pc),
                    device_id_type=pl.DeviceIdType.MESH,
                )
                cp.wait_send()
                cp.wait_recv()

    hbm = pl.BlockSpec(memory_space=pltpu.HBM)
    sem = pl.BlockSpec(memory_space=pltpu.SEMAPHORE)
    outs = pl.pallas_call(
        body,
        name=name,
        out_shape=tuple(pltpu.HBM(a.shape, a.dtype) for a in srcs + lands),
        in_specs=[hbm] * (2 * n) + [sem, sem, pl.BlockSpec(memory_space=pl.ANY)],
        out_specs=tuple([hbm] * (2 * n)),
        input_output_aliases={k: k for k in range(2 * n)},
        compiler_params=pltpu.CompilerParams(has_side_effects=pltpu.SideEffectType.DATAFLOW_SIDE_EFFECTING),
    )(*srcs, *lands, send_sems, recv_sems, after)
    return list(outs[:n]), list(outs[n:])


def _mm(a, b, *, mode, out_dtype, name, add=None, tm=512, tn=512, b_rows=None, dep=None):
    rows_b = b.shape[0] if b_rows is None else b_rows
    if mode == "nn":
        (m, kd), nd = a.shape, b.shape[1]
        assert kd == rows_b
    elif mode == "nt":
        (m, kd), nd = a.shape, rows_b
    else:
        (kd, m), nd = a.shape, b.shape[1]
    tm = _pick(m, tm, LANES if mode == "tn" else 16)
    tn = _pick(nd, tn)
    dims = {"nn": NN, "nt": NT, "tn": TN}[mode]
    ni, nj = m // tm, nd // tn
    a_bytes, b_bytes = a.size * a.dtype.itemsize, b.size * b.dtype.itemsize
    i_outer = a_bytes + ni * b_bytes <= b_bytes + nj * a_bytes
    ij = (lambda g0, g1: (g0, g1)) if i_outer else (lambda g0, g1: (g1, g0))
    a_spec = (pl.BlockSpec((kd, tm), lambda g0, g1: (0, ij(g0, g1)[0])) if mode == "tn"
              else pl.BlockSpec((tm, kd), lambda g0, g1: (ij(g0, g1)[0], 0)))
    b_spec = (pl.BlockSpec((tn, kd), lambda g0, g1: (ij(g0, g1)[1], 0)) if mode == "nt"
              else pl.BlockSpec((kd, tn), lambda g0, g1: (0, ij(g0, g1)[1])))
    o_spec = pl.BlockSpec((tm, tn), lambda g0, g1: ij(g0, g1))
    has_add = add is not None

    def body(*refs):
        a_ref, b_ref = refs[0], refs[1]
        o_ref = refs[-1]
        acc = _dotb(a_ref[...], b_ref[...], dims)
        if has_add:
            acc = acc + refs[2][...].astype(F32)
        o_ref[...] = acc.astype(o_ref.dtype)

    ins = [a, b] + ([add] if has_add else []) + ([] if dep is None else [dep])
    specs = ([a_spec, b_spec] + ([o_spec] if has_add else [])
             + ([] if dep is None else [pl.BlockSpec((8, LANES), lambda g0, g1: (0, 0))]))
    return pl.pallas_call(
        body, name=name, grid=(ni, nj) if i_outer else (nj, ni), in_specs=specs, out_specs=o_spec,
        out_shape=jax.ShapeDtypeStruct((m, nd), out_dtype),
        compiler_params=_params("parallel", "parallel"),
    )(*ins)


def _mm_resid(a, b, x, gate, *, name, tm=256, tn=1024):
    m, kd = a.shape
    nd = b.shape[1]
    tm = _pick(m, tm, 16)
    tn = _pick(nd, tn)
    o_spec = pl.BlockSpec((tm, tn), lambda i, j: (i, j))

    def body(a_ref, b_ref, x_ref, g_ref, xo_ref, y_ref):
        y = _dotb(a_ref[...], b_ref[...], NN)
        y_ref[...] = y.astype(y_ref.dtype)
        xo_ref[...] = x_ref[...] + g_ref[...] * y

    return pl.pallas_call(
        body, name=name, grid=(m // tm, nd // tn),
        in_specs=[pl.BlockSpec((tm, kd), lambda i, j: (i, 0)), pl.BlockSpec((kd, tn), lambda i, j: (0, j)),
                  o_spec, pl.BlockSpec((1, tn), lambda i, j: (0, j))],
        out_specs=(o_spec, o_spec),
        out_shape=(jax.ShapeDtypeStruct((m, nd), F32), jax.ShapeDtypeStruct((m, nd), BF16)),
        compiler_params=_params("parallel", "parallel"),
    )(a, b, x, gate)


ROWS = 256


def _row_spec(width, rows=ROWS):
    return pl.BlockSpec((rows, width), lambda i: (i, 0))


def _const_spec(shape):
    return pl.BlockSpec(shape, lambda i: tuple(0 for _ in shape))


def _adaln_fwd(x, g, scale, shift, *, name):
    t, d = x.shape

    def body(x_ref, g_ref, sc_ref, sh_ref, h_ref):
        xv = x_ref[...]
        r = lax.rsqrt(jnp.mean(xv * xv, axis=-1, keepdims=True) + EPS)
        h_ref[...] = (xv * r * g_ref[...] * (1.0 + sc_ref[...]) + sh_ref[...]).astype(h_ref.dtype)

    return pl.pallas_call(
        body, name=name, grid=(t // ROWS,),
        in_specs=[_row_spec(d), _const_spec((1, d)), _const_spec((1, d)), _const_spec((1, d))],
        out_specs=_row_spec(d), out_shape=jax.ShapeDtypeStruct((t, d), BF16),
        compiler_params=_params("parallel"),
    )(x, g, scale, shift)


def _adaln_bwd(x, g, scale, shift, dh, dres, dep, *, name):
    t, d = x.shape

    def body(x_ref, g_ref, sc_ref, sh_ref, dh_ref, dr_ref, dep_ref, dx_ref, st_ref):
        @pl.when(pl.program_id(0) == 0)
        def _():
            st_ref[...] = jnp.zeros_like(st_ref)

        xv = x_ref[...]
        dhv = dh_ref[...].astype(F32)
        gv = g_ref[...]
        r = lax.rsqrt(jnp.mean(xv * xv, axis=-1, keepdims=True) + EPS)
        xh = xv * r
        nv = xh * gv
        dn = dhv * (1.0 + sc_ref[...])
        dxh = dn * gv
        dx_ref[...] = dr_ref[...] + r * (dxh - xh * jnp.mean(dxh * xh, axis=-1, keepdims=True))
        st_ref[0:1, :] += jnp.sum(dn * xh, axis=0, keepdims=True)
        st_ref[1:2, :] += jnp.sum(dhv * nv, axis=0, keepdims=True)
        st_ref[2:3, :] += jnp.sum(dhv, axis=0, keepdims=True)

    return pl.pallas_call(
        body, name=name, grid=(t // ROWS,),
        in_specs=[_row_spec(d), _const_spec((1, d)), _const_spec((1, d)), _const_spec((1, d)),
                  _row_spec(d), _row_spec(d), _const_spec((8, LANES))],
        out_specs=(_row_spec(d), _const_spec((8, d))),
        out_shape=(jax.ShapeDtypeStruct((t, d), F32), jax.ShapeDtypeStruct((8, d), F32)),
        compiler_params=_params("arbitrary"),
    )(x, g, scale, shift, dh, dres, dep)


def _adaln_gate_bwd(x, g, scale, shift, dh, dres, dep, y_up, gate_up, *, name):
    t, d = x.shape

    def body(x_ref, g_ref, sc_ref, sh_ref, dh_ref, dr_ref, dep_ref, y_ref, gu_ref, dx_ref, st_ref, dy_ref):
        @pl.when(pl.program_id(0) == 0)
        def _():
            st_ref[...] = jnp.zeros_like(st_ref)

        xv = x_ref[...]
        dhv = dh_ref[...].astype(F32)
        gv = g_ref[...]
        r = lax.rsqrt(jnp.mean(xv * xv, axis=-1, keepdims=True) + EPS)
        xh = xv * r
        nv = xh * gv
        dn = dhv * (1.0 + sc_ref[...])
        dxh = dn * gv
        dx = dr_ref[...] + r * (dxh - xh * jnp.mean(dxh * xh, axis=-1, keepdims=True))
        dx_ref[...] = dx
        dy_ref[...] = (dx * gu_ref[...]).astype(dy_ref.dtype)
        st_ref[0:1, :] += jnp.sum(dn * xh, axis=0, keepdims=True)
        st_ref[1:2, :] += jnp.sum(dhv * nv, axis=0, keepdims=True)
        st_ref[2:3, :] += jnp.sum(dhv, axis=0, keepdims=True)
        st_ref[3:4, :] += jnp.sum(dx * y_ref[...].astype(F32), axis=0, keepdims=True)

    return pl.pallas_call(
        body, name=name, grid=(t // ROWS,),
        in_specs=[_row_spec(d), _const_spec((1, d)), _const_spec((1, d)), _const_spec((1, d)),
                  _row_spec(d), _row_spec(d), _const_spec((8, LANES)), _row_spec(d), _const_spec((1, d))],
        out_specs=(_row_spec(d), _const_spec((8, d)), _row_spec(d)),
        out_shape=(jax.ShapeDtypeStruct((t, d), F32), jax.ShapeDtypeStruct((8, d), F32),
                   jax.ShapeDtypeStruct((t, d), BF16)),
        compiler_params=_params("arbitrary"),
    )(x, g, scale, shift, dh, dres, dep, y_up, gate_up)


def _gate_bwd(dxo, y, gate, dep, *, name):
    t, d = dxo.shape

    def body(dx_ref, y_ref, g_ref, dep_ref, dy_ref, st_ref):
        @pl.when(pl.program_id(0) == 0)
        def _():
            st_ref[...] = jnp.zeros_like(st_ref)

        dxv = dx_ref[...]
        dy_ref[...] = (dxv * g_ref[...]).astype(dy_ref.dtype)
        st_ref[0:1, :] += jnp.sum(dxv * y_ref[...], axis=0, keepdims=True)

    return pl.pallas_call(
        body, name=name, grid=(t // ROWS,),
        in_specs=[_row_spec(d), _row_spec(d), _const_spec((1, d)), _const_spec((8, LANES))],
        out_specs=(_row_spec(d), _const_spec((8, d))),
        out_shape=(jax.ShapeDtypeStruct((t, d), BF16), jax.ShapeDtypeStruct((8, d), F32)),
        compiler_params=_params("arbitrary"),
    )(dxo, y, gate, dep)


def _loss_head(x, g, target, *, name):
    t, d = x.shape

    def body(x_ref, g_ref, t_ref, dx_ref, st_ref, ls_ref):
        @pl.when(pl.program_id(0) == 0)
        def _():
            st_ref[...] = jnp.zeros_like(st_ref)
            ls_ref[...] = jnp.zeros_like(ls_ref)

        xv = x_ref[...]
        gv = g_ref[...]
        r = lax.rsqrt(jnp.mean(xv * xv, axis=-1, keepdims=True) + EPS)
        xh = xv * r
        err = xh * gv - t_ref[...]
        ls_ref[...] += 0.5 * jnp.sum(jnp.mean(err * err, axis=-1, keepdims=True))
        dy = err * (1.0 / d)
        dxh = dy * gv
        dx_ref[...] = r * (dxh - xh * jnp.mean(dxh * xh, axis=-1, keepdims=True))
        st_ref[0:1, :] += jnp.sum(dy * xh, axis=0, keepdims=True)

    return pl.pallas_call(
        body, name=name, grid=(t // ROWS,),
        in_specs=[_row_spec(d), _const_spec((1, d)), _row_spec(d)],
        out_specs=(_row_spec(d), _const_spec((8, d)), _const_spec((8, LANES))),
        out_shape=(jax.ShapeDtypeStruct((t, d), F32), jax.ShapeDtypeStruct((8, d), F32),
                   jax.ShapeDtypeStruct((8, LANES), F32)),
        compiler_params=_params("arbitrary"),
    )(x, g, target)


FFN_BLOCK = D_FF // 2
FFN_ROWS = 512


def _ffn_chunks(width):
    edges = [min(width, 3 * LANES * i) for i in range(width // (3 * LANES) + 2)]
    return [slice(lo, hi) for lo, hi in zip(edges[:-1], edges[1:]) if hi > lo]


def _ffn_gu_fwd(h, wg, wu, dep, *, name):
    t, d = h.shape
    tn = FFN_BLOCK

    chunks = _ffn_chunks(tn)
    rows = _pick(t, FFN_ROWS, 16)

    def body(h_ref, wg_ref, wu_ref, dep_ref, s_ref, a_ref, b_ref):
        hv = h_ref[...]
        ab = [(_dotb(hv, wg_ref[sl, :], NT), _dotb(hv, wu_ref[sl, :], NT)) for sl in chunks]
        for sl, (a, b) in zip(chunks, ab):
            s_ref[:, sl] = (a * _sigmoid(a) * b).astype(s_ref.dtype)
            a_ref[:, sl] = a.astype(a_ref.dtype)
            b_ref[:, sl] = b.astype(b_ref.dtype)

    w_spec = pl.BlockSpec((tn, d), lambda j, i: (j, 0))
    o_spec = pl.BlockSpec((rows, tn), lambda j, i: (i, j))
    return pl.pallas_call(
        body, name=name, grid=(D_FF // tn, t // rows),
        in_specs=[pl.BlockSpec((rows, d), lambda j, i: (i, 0)), w_spec, w_spec,
                  pl.BlockSpec((8, LANES), lambda j, i: (0, 0))],
        out_specs=(o_spec, o_spec, o_spec),
        out_shape=(jax.ShapeDtypeStruct((t, D_FF), BF16),) * 3,
        compiler_params=_params("parallel", "parallel"),
    )(h, wg, wu, dep)


def _ffn_gu_dx(da, db, wg, wu, *, name):
    t, f = da.shape
    d = wg.shape[1]
    rows = _pick(t, ROWS, 16)

    def body(da_ref, db_ref, wg_ref, wu_ref, o_ref):
        o_ref[...] = (_dotb(da_ref[...], wg_ref[...], NN) + _dotb(db_ref[...], wu_ref[...], NN)).astype(o_ref.dtype)

    a_spec = pl.BlockSpec((rows, f), lambda i: (i, 0))
    w_spec = pl.BlockSpec((f, d), lambda i: (0, 0))
    return pl.pallas_call(
        body, name=name, grid=(t // rows,), in_specs=[a_spec, a_spec, w_spec, w_spec],
        out_specs=pl.BlockSpec((rows, d), lambda i: (i, 0)), out_shape=jax.ShapeDtypeStruct((t, d), BF16),
        compiler_params=_params("parallel"),
    )(da, db, wg, wu)


def _ffn_down_dx(dy, w_down, a, b, *, name):
    t, d = dy.shape
    tn = FFN_BLOCK

    chunks = _ffn_chunks(tn)
    rows = _pick(t, FFN_ROWS, 16)

    def body(dy_ref, w_ref, a_ref, b_ref, da_ref, db_ref):
        dyv = dy_ref[...]
        ds = [_dotb(dyv, w_ref[sl, :], NT) for sl in chunks]
        for sl, dsc in zip(chunks, ds):
            av = a_ref[:, sl].astype(F32)
            sg = _sigmoid(av)
            da_ref[:, sl] = (dsc * b_ref[:, sl].astype(F32) * sg * (1.0 + av * (1.0 - sg))).astype(da_ref.dtype)
            db_ref[:, sl] = (dsc * av * sg).astype(db_ref.dtype)

    o_spec = pl.BlockSpec((rows, tn), lambda j, i: (i, j))
    return pl.pallas_call(
        body, name=name, grid=(D_FF // tn, t // rows),
        in_specs=[pl.BlockSpec((rows, d), lambda j, i: (i, 0)), pl.BlockSpec((tn, d), lambda j, i: (j, 0)),
                  o_spec, o_spec],
        out_specs=(o_spec, o_spec),
        out_shape=(jax.ShapeDtypeStruct((t, D_FF), BF16),) * 2,
        compiler_params=_params("parallel", "parallel"),
    )(dy, w_down, a, b)


def _shift_rows(v, s, rows):
    if s == 0:
        return v
    return jnp.where(rows >= s, pltpu.roll(v, s, 0), 0.0)


def _unshift_rows(v, s, rows, t):
    if s == 0:
        return v
    return jnp.where(rows < t - s, pltpu.roll(v, t - s, 0), 0.0)


def _conv_taps(x, rows):
    return [_shift_rows(x, GDN_CONV - 1 - j, rows) for j in range(GDN_CONV)]


def _conv_silu(xs, w):
    z = w[0:1, :] * xs[0]
    for j in range(1, GDN_CONV):
        z = z + w[j:j + 1, :] * xs[j]
    sg = _sigmoid(z)
    return z, sg, z * sg


def _gdn_prep_fwd(proj, conv_wt, *, name):
    t = proj.shape[0]
    nh = GDN_HEADS

    hp = GDN_PREP_HEADS
    wd = hp * LANES

    def body(x_ref, w_ref, y_ref):
        j = pl.program_id(0) * hp
        rows = lax.broadcasted_iota(jnp.int32, (t, LANES), 0)
        qscale = jnp.where(j < nh, GDN_HEAD_DIM ** -0.5, 1.0)
        for i in range(hp):
            sl = slice(i * LANES, (i + 1) * LANES)
            _, _, s = _conv_silu(_conv_taps(x_ref[:, sl], rows), w_ref[:, sl])
            rs = lax.rsqrt(jnp.sum(s * s, axis=-1, keepdims=True) + EPS)
            y_ref[:, sl] = jnp.where(j < 2 * nh, s * rs * qscale, s)

    return pl.pallas_call(
        body, name=name, grid=(3 * nh // hp,),
        in_specs=[pl.BlockSpec((t, wd), lambda j: (0, j)), pl.BlockSpec((GDN_CONV, wd), lambda j: (0, j))],
        out_specs=pl.BlockSpec((t, wd), lambda j: (0, j)),
        out_shape=jax.ShapeDtypeStruct((t, 3 * GDN_KEY_DIM), F32),
        compiler_params=_params("parallel"),
    )(proj, conv_wt)


def _gdn_prep_bwd(proj, conv_wt, dy, *, name):
    t = proj.shape[0]
    nh = GDN_HEADS

    hp = GDN_PREP_HEADS
    wd = hp * LANES
    per_seg = nh // hp

    def body(x_ref, w_ref, dy_ref, dx_ref, dw_ref):
        j = pl.program_id(0) * hp
        rows = lax.broadcasted_iota(jnp.int32, (t, LANES), 0)
        qscale = jnp.where(j < nh, GDN_HEAD_DIM ** -0.5, 1.0)
        for i in range(hp):
            sl = slice(i * LANES, (i + 1) * LANES)
            w = w_ref[:, sl]
            xs = _conv_taps(x_ref[:, sl], rows)
            z, sg, s = _conv_silu(xs, w)
            rs = lax.rsqrt(jnp.sum(s * s, axis=-1, keepdims=True) + EPS)
            dyv = dy_ref[:, sl]
            nv = s * rs
            de = dyv * qscale
            ds_qk = rs * (de - nv * jnp.sum(de * nv, axis=-1, keepdims=True))
            ds = jnp.where(j < 2 * nh, ds_qk, dyv)
            dz = ds * sg * (1.0 + z * (1.0 - sg))
            dx = w[GDN_CONV - 1:GDN_CONV, :] * dz
            dw_ref[GDN_CONV - 1:GDN_CONV, sl] = jnp.sum(dz * xs[GDN_CONV - 1], axis=0, keepdims=True)
            for k in range(GDN_CONV - 1):
                dx = dx + w[k:k + 1, :] * _unshift_rows(dz, GDN_CONV - 1 - k, rows, t)
                dw_ref[k:k + 1, sl] = jnp.sum(dz * xs[k], axis=0, keepdims=True)
            dx_ref[:, sl] = dx.astype(dx_ref.dtype)

    return pl.pallas_call(
        body, name=name, grid=(3 * nh // hp,),
        in_specs=[pl.BlockSpec((t, wd), lambda j: (0, j)), pl.BlockSpec((GDN_CONV, wd), lambda j: (0, j)),
                  pl.BlockSpec((None, t, wd), lambda j: (j // per_seg, 0, j % per_seg))],
        out_specs=(pl.BlockSpec((t, wd), lambda j: (0, j)), pl.BlockSpec((GDN_CONV, wd), lambda j: (0, j))),
        out_shape=(jax.ShapeDtypeStruct((t, 3 * GDN_KEY_DIM), BF16),
                   jax.ShapeDtypeStruct((GDN_CONV, 3 * GDN_KEY_DIM), F32)),
        compiler_params=_params("parallel"),
    )(proj, conv_wt, dy)


def _softplus(z):
    return jnp.maximum(z, 0.0) + jnp.log(1.0 + jnp.exp(-jnp.abs(z)))


def _gdn_gate_fwd(ab, prm, *, name):
    t = ab.shape[0]

    def body(ab_ref, p_ref, o_ref):
        v = ab_ref[...]
        lane = lax.broadcasted_iota(jnp.int32, v.shape, 1)
        g = -jnp.exp(p_ref[0:1, :]) * _softplus(v + p_ref[1:2, :])
        o_ref[...] = jnp.where(lane < GDN_HEADS, g, jnp.where(lane < 2 * GDN_HEADS, _sigmoid(v), 0.0))

    return pl.pallas_call(
        body, name=name, grid=(t // ROWS,),
        in_specs=[_row_spec(LANES), _const_spec((8, LANES))], out_specs=_row_spec(LANES),
        out_shape=jax.ShapeDtypeStruct((t, LANES), F32), compiler_params=_params("parallel"),
    )(ab, prm)


def _gdn_gate_bwd(ab, prm, dgb, *, name):
    t = ab.shape[0]

    def body(ab_ref, p_ref, d_ref, o_ref, st_ref):
        @pl.when(pl.program_id(0) == 0)
        def _():
            st_ref[...] = jnp.zeros_like(st_ref)

        v = ab_ref[...]
        dv = d_ref[...]
        lane = lax.broadcasted_iota(jnp.int32, v.shape, 1)
        is_a = lane < GDN_HEADS
        is_b = jnp.logical_and(lane >= GDN_HEADS, lane < 2 * GDN_HEADS)
        a_exp = jnp.exp(p_ref[0:1, :])
        zz = v + p_ref[1:2, :]
        g = -a_exp * _softplus(zz)
        da = dv * (-a_exp) * _sigmoid(zz)
        beta = _sigmoid(v)
        db = dv * beta * (1.0 - beta)
        o_ref[...] = jnp.where(is_a, da, jnp.where(is_b, db, 0.0)).astype(o_ref.dtype)
        st_ref[0:1, :] += jnp.sum(jnp.where(is_a, dv * g, 0.0), axis=0, keepdims=True)
        st_ref[1:2, :] += jnp.sum(jnp.where(is_a, da, 0.0), axis=0, keepdims=True)

    return pl.pallas_call(
        body, name=name, grid=(t // ROWS,),
        in_specs=[_row_spec(LANES), _const_spec((8, LANES)), _row_spec(LANES)],
        out_specs=(_row_spec(LANES), _const_spec((8, LANES))),
        out_shape=(jax.ShapeDtypeStruct((t, LANES), BF16), jax.ShapeDtypeStruct((8, LANES), F32)),
        compiler_params=_params("arbitrary"),
    )(ab, prm, dgb)


def _gdn_local(qs, ks, vs, gbs, bbs, tinvs=None):
    nh = len(qs)
    cs = qs[0].shape[0]
    hs = range(nh)
    r = lax.broadcasted_iota(jnp.int32, (cs, cs), 0)
    c = lax.broadcasted_iota(jnp.int32, (cs, cs), 1)
    tril, strict, eye = r >= c, r > c, r == c
    ident = jnp.where(eye, 1.0, 0.0)
    g_colb = [gbs[h][:, :cs] for h in hs]
    g_row = [jnp.sum(jnp.where(eye, g_colb[h], 0.0), axis=0, keepdims=True) for h in hs]
    gc_col = [jnp.sum(jnp.where(tril, g_row[h], 0.0), axis=1, keepdims=True) for h in hs]
    gc_row = [jnp.sum(jnp.where(r <= c, g_colb[h], 0.0), axis=0, keepdims=True) for h in hs]
    decay = [jnp.exp(jnp.where(tril, gc_col[h] - gc_row[h], NEG)) for h in hs]
    gamma = [jnp.exp(gc_col[h]) for h in hs]
    gcl = [gc_col[h][cs - 1:cs, :] for h in hs]
    gl = [jnp.exp(gcl[h]) for h in hs]
    kdec = [jnp.exp(gcl[h] - gc_col[h]) for h in hs]
    kb = [ks[h] * bbs[h] for h in hs]
    kk = [_dotb(kb[h], ks[h], NT) for h in hs]
    qk = [_dotb(qs[h], ks[h], NT) for h in hs]
    lmat = [jnp.where(strict, kk[h] * decay[h], 0.0) for h in hs]
    pmat = [jnp.where(tril, qk[h] * decay[h], 0.0) for h in hs]
    if tinvs is None:
        xm = [-lmat[h] for h in hs]
        tinv = [ident + xm[h] for h in hs]
        for _ in range(int(math.log2(cs)) - 1):
            xm = [_dotf(xm[h], xm[h], NN) for h in hs]
            tinv = [tinv[h] + _dotf(tinv[h], xm[h], NN) for h in hs]
    else:
        tinv = tinvs
    vb = [vs[h] * bbs[h] for h in hs]
    kg = [kb[h] * gamma[h] for h in hs]
    u = [_dotf(tinv[h], vb[h], NN) for h in hs]
    w = [_dotf(tinv[h], kg[h], NN) for h in hs]
    return [dict(tril=tril, strict=strict, eye=eye, r=r, c=c, decay=decay[h], gamma=gamma[h], gl=gl[h], kdec=kdec[h],
                 kb=kb[h], lmat=lmat[h], tinv=tinv[h], vb=vb[h], kg=kg[h], u=u[h], w=w[h], pmat=pmat[h],
                 qd=qs[h] * gamma[h], kd=ks[h] * kdec[h]) for h in hs]


def _head_columns(gbeta, cs):
    gbs = [jnp.broadcast_to(gbeta[:, h:h + 1], (cs, LANES)) for h in range(GDN_HEADS)]
    bbs = [jnp.broadcast_to(gbeta[:, GDN_HEADS + h:GDN_HEADS + h + 1], (cs, LANES)) for h in range(GDN_HEADS)]
    return gbs, bbs


def _gdn_chunk_fwd(qkv, gbeta, *, name):
    t = qkv.shape[0]
    nh, cs, hd = GDN_HEADS, GDN_CHUNK, GDN_HEAD_DIM
    nc = t // cs

    hb = GDN_HEAD_BATCH
    ng = nh // hb
    assert ng == 1

    def body(q_ref, k_ref, v_ref, gb_ref, o_ref, st_ref, ti_ref, s_ref):
        @pl.when(pl.program_id(1) == 0)
        def _():
            s_ref[...] = jnp.zeros_like(s_ref)

        sls = [slice(i * hd, (i + 1) * hd) for i in range(hb)]
        hs = range(hb)
        s = [s_ref[i] for i in hs]
        gbs, bbs = _head_columns(gb_ref[...], cs)
        lo = _gdn_local([q_ref[:, sl] for sl in sls], [k_ref[:, sl] for sl in sls], [v_ref[:, sl] for sl in sls],
                        gbs, bbs)
        ws = [_dotb(lo[i]["w"], s[i], NN) for i in hs]
        qs = [_dotb(lo[i]["qd"], s[i], NN) for i in hs]
        vn = [lo[i]["u"] - ws[i] for i in hs]
        pv = [_dotb(lo[i]["pmat"], vn[i], NN) for i in hs]
        kv = [_dotb(lo[i]["kd"], vn[i], TN) for i in hs]
        for i, sl in enumerate(sls):
            st_ref[i, 0] = s[i]
            ti_ref[i, 0] = lo[i]["tinv"]
            o_ref[:, sl] = qs[i] + pv[i]
            s_ref[i] = s[i] * lo[i]["gl"] + kv[i]

    col = lambda off: pl.BlockSpec((cs, hb * hd), lambda h, n: (n, off + h))
    return pl.pallas_call(
        body, name=name, grid=(ng, nc),
        in_specs=[col(0), col(ng), col(2 * ng), pl.BlockSpec((cs, LANES), lambda h, n: (n, 0))],
        out_specs=(col(0), pl.BlockSpec((hb, 1, hd, hd), lambda h, n: (h, n, 0, 0)),
                   pl.BlockSpec((hb, 1, cs, cs), lambda h, n: (h, n, 0, 0))),
        out_shape=(jax.ShapeDtypeStruct((t, nh * hd), F32), jax.ShapeDtypeStruct((nh, nc, hd, hd), F32),
                   jax.ShapeDtypeStruct((nh, nc, cs, cs), F32)),
        scratch_shapes=[pltpu.VMEM((hb, hd, hd), F32)],
        compiler_params=_params("parallel", "arbitrary"),
    )(qkv, qkv, qkv, gbeta)


def _gdn_chunk_bwd(qkv, gbeta, states, tinvs, do, *, name):
    t = qkv.shape[0]
    nh, cs, hd = GDN_HEADS, GDN_CHUNK, GDN_HEAD_DIM
    nc = t // cs

    hb = GDN_HEAD_BATCH
    ng = nh // hb
    assert ng == 1

    def heads_bwd(q, k, v, gb, bb, s, ti, dsn, dov):
        hs = range(len(q))
        lo = _gdn_local(q, k, v, gb, bb, ti)
        tril, strict, eye, r, c = lo[0]["tril"], lo[0]["strict"], lo[0]["eye"], lo[0]["r"], lo[0]["c"]
        rowi = lax.broadcasted_iota(jnp.int32, (cs, 1), 0)
        get = lambda name: [lo[h][name] for h in hs]
        decay, gamma, gl, kdec = get("decay"), get("gamma"), get("gl"), get("kdec")
        kb, tinv, w, pmat, kd, qd = get("kb"), get("tinv"), get("w"), get("pmat"), get("kd"), get("qd")
        ws = [_dotb(w[h], s[h], NN) for h in hs]
        pdo = [_dotb(pmat[h], dov[h], TN) for h in hs]
        kds = [_dotb(kd[h], dsn[h], NN) for h in hs]
        dqd = [_dotb(dov[h], s[h], NT) for h in hs]
        qdo = [_dotb(qd[h], dov[h], TN) for h in hs]
        vn = [lo[h]["u"] - ws[h] for h in hs]
        dvn = [pdo[h] + kds[h] for h in hs]
        dp = [jnp.where(tril, _dotb(dov[h], vn[h], NT), 0.0) for h in hs]
        dkd = [_dotb(vn[h], dsn[h], NT) for h in hs]
        dw = [-_dotb(dvn[h], s[h], NT) for h in hs]
        wdv = [_dotb(w[h], dvn[h], TN) for h in hs]
        dvb = [_dotf(tinv[h], dvn[h], TN) for h in hs]
        dt1 = [_dotf(dvn[h], lo[h]["vb"], NT) for h in hs]
        dkg = [_dotf(tinv[h], dw[h], TN) for h in hs]
        dt2 = [_dotf(dw[h], lo[h]["kg"], NT) for h in hs]
        tdt = [_dotf(tinv[h], dt1[h] + dt2[h], TN) for h in hs]
        dl = [jnp.where(strict, -_dotf(tdt[h], tinv[h], NT), 0.0) for h in hs]
        dkk = [dl[h] * decay[h] for h in hs]
        dqk = [dp[h] * decay[h] for h in hs]
        dkb = [_dotb(dkk[h], k[h], NN) + dkg[h] * gamma[h] for h in hs]
        dk1 = [_dotb(dkk[h], kb[h], TN) for h in hs]
        dk2 = [_dotb(dqk[h], q[h], TN) for h in hs]
        dq1 = [_dotb(dqk[h], k[h], NN) for h in hs]
        out = []
        for h in hs:
            dgl = jnp.sum(jnp.sum(dsn[h] * s[h], axis=1, keepdims=True), axis=0, keepdims=True)
            ds_prev = gl[h] * dsn[h] + qdo[h] - wdv[h]
            dk = dk1[h] + dk2[h] + dkd[h] * kdec[h] + dkb[h] * bb[h]
            dq = dq1[h] + dqd[h] * gamma[h]
            dbeta = jnp.sum(dvb[h] * v[h], axis=-1, keepdims=True) + jnp.sum(dkb[h] * k[h], axis=-1, keepdims=True)
            e = dl[h] * lo[h]["lmat"] + dp[h] * pmat[h]
            e_col = jnp.sum(e, axis=0, keepdims=True)
            dgc = jnp.sum(e, axis=1, keepdims=True) - jnp.sum(jnp.where(eye, e_col, 0.0), axis=1, keepdims=True)
            dgamma = (jnp.sum(dqd[h] * q[h], axis=-1, keepdims=True)
                      + jnp.sum(dkg[h] * kb[h], axis=-1, keepdims=True))
            rk = jnp.sum(dkd[h] * k[h], axis=-1, keepdims=True) * kdec[h]
            dgcl = jnp.sum(rk, axis=0, keepdims=True) + dgl * gl[h]
            dgc = dgc + dgamma * gamma[h] - rk + jnp.where(rowi == cs - 1, dgcl, 0.0)
            dgc_row = jnp.sum(jnp.where(eye, dgc, 0.0), axis=0, keepdims=True)
            dg = jnp.sum(jnp.where(c >= r, dgc_row, 0.0), axis=1, keepdims=True)
            out.append((dq, dk, dvb[h] * bb[h], dbeta, dg, ds_prev))
        return out

    def body(q_ref, k_ref, v_ref, gb_ref, st_ref, ti_ref, do_ref, d_ref, dgb_ref, ds_ref):
        @pl.when(pl.program_id(1) == 0)
        def _():
            ds_ref[...] = jnp.zeros_like(ds_ref)

        sls = [slice(i * hd, (i + 1) * hd) for i in range(hb)]
        hs = range(hb)
        gbs, bbs = _head_columns(gb_ref[...], cs)
        outs = heads_bwd([q_ref[:, sl] for sl in sls], [k_ref[:, sl] for sl in sls], [v_ref[:, sl] for sl in sls],
                         gbs, bbs, [st_ref[i, 0] for i in hs],
                         [ti_ref[i, 0] for i in hs], [ds_ref[i] for i in hs], [do_ref[:, sl] for sl in sls])
        lane = lax.broadcasted_iota(jnp.int32, (cs, LANES), 1)
        dgb = jnp.zeros((cs, LANES), F32)
        for i, sl in enumerate(sls):
            dq, dk, dv, dbeta, dg, ds_prev = outs[i]
            d_ref[0, :, sl], d_ref[1, :, sl], d_ref[2, :, sl] = dq, dk, dv
            dgb = jnp.where(lane == i, dg, jnp.where(lane == nh + i, dbeta, dgb))
            ds_ref[i] = ds_prev
        dgb_ref[...] = dgb

    col = lambda off: pl.BlockSpec((cs, hb * hd), lambda h, n: (nc - 1 - n, off + h))
    gspec = pl.BlockSpec((cs, LANES), lambda h, n: (nc - 1 - n, 0))
    return pl.pallas_call(
        body, name=name, grid=(ng, nc),
        in_specs=[col(0), col(ng), col(2 * ng), gspec,
                  pl.BlockSpec((hb, 1, hd, hd), lambda h, n: (h, nc - 1 - n, 0, 0)),
                  pl.BlockSpec((hb, 1, cs, cs), lambda h, n: (h, nc - 1 - n, 0, 0)), col(0)],
        out_specs=(pl.BlockSpec((3, cs, hb * hd), lambda h, n: (0, nc - 1 - n, h)), gspec),
        out_shape=(jax.ShapeDtypeStruct((3, t, nh * hd), F32), jax.ShapeDtypeStruct((t, LANES), F32)),
        scratch_shapes=[pltpu.VMEM((hb, hd, hd), F32)],
        compiler_params=_params("parallel", "arbitrary"),
    )(qkv, qkv, qkv, gbeta, states, tinvs, do)


def _gdn_onorm_fwd(o, proj, norm_g, *, name):
    t = o.shape[0]
    w = GDN_KEY_DIM
    goff = 3 * GDN_KEY_DIM // w

    def body(o_ref, gp_ref, g_ref, y_ref):
        gv = g_ref[...]
        for h in range(GDN_HEADS):
            sl = slice(h * GDN_HEAD_DIM, (h + 1) * GDN_HEAD_DIM)
            oh = o_ref[:, sl]
            gp = gp_ref[:, sl]
            r = lax.rsqrt(jnp.mean(oh * oh, axis=-1, keepdims=True) + EPS)
            y_ref[:, sl] = (oh * r * gv * gp * _sigmoid(gp)).astype(y_ref.dtype)

    return pl.pallas_call(
        body, name=name, grid=(t // ROWS,),
        in_specs=[_row_spec(w), pl.BlockSpec((ROWS, w), lambda i: (i, goff)), _const_spec((1, GDN_HEAD_DIM))],
        out_specs=_row_spec(w), out_shape=jax.ShapeDtypeStruct((t, w), BF16),
        compiler_params=_params("parallel"),
    )(o, proj, norm_g)


def _gdn_onorm_bwd(o, proj, norm_g, dy, *, name):
    t = o.shape[0]
    w = GDN_KEY_DIM
    goff = 3 * GDN_KEY_DIM // w

    def body(o_ref, gp_ref, g_ref, dy_ref, do_ref, dgp_ref, st_ref):
        @pl.when(pl.program_id(0) == 0)
        def _():
            st_ref[...] = jnp.zeros_like(st_ref)

        gv = g_ref[...]
        acc = jnp.zeros((1, GDN_HEAD_DIM), F32)
        for h in range(GDN_HEADS):
            sl = slice(h * GDN_HEAD_DIM, (h + 1) * GDN_HEAD_DIM)
            oh = o_ref[:, sl]
            gp = gp_ref[:, sl]
            dyv = dy_ref[:, sl].astype(F32)
            r = lax.rsqrt(jnp.mean(oh * oh, axis=-1, keepdims=True) + EPS)
            xh = oh * r
            sg = _sigmoid(gp)
            dn = dyv * gp * sg
            dgp_ref[:, sl] = (dyv * xh * gv * sg * (1.0 + gp * (1.0 - sg))).astype(dgp_ref.dtype)
            acc = acc + jnp.sum(dn * xh, axis=0, keepdims=True)
            dxh = dn * gv
            do_ref[:, sl] = r * (dxh - xh * jnp.mean(dxh * xh, axis=-1, keepdims=True))
        st_ref[0:1, :] += acc

    return pl.pallas_call(
        body, name=name, grid=(t // ROWS,),
        in_specs=[_row_spec(w), pl.BlockSpec((ROWS, w), lambda i: (i, goff)), _const_spec((1, GDN_HEAD_DIM)),
                  _row_spec(w)],
        out_specs=(_row_spec(w), _row_spec(w), _const_spec((8, GDN_HEAD_DIM))),
        out_shape=(jax.ShapeDtypeStruct((t, w), F32), jax.ShapeDtypeStruct((t, w), BF16),
                   jax.ShapeDtypeStruct((8, GDN_HEAD_DIM), F32)),
        compiler_params=_params("arbitrary"),
    )(o, proj, norm_g, dy)


def _mla_prep_fwd(proj, qg, kvg, *, name):
    t = proj.shape[0]
    q1, k1 = MLA_Q_RANK, MLA_Q_RANK + MLA_KV_RANK

    def body(p_ref, qg_ref, kg_ref, cq_ref, ck_ref):
        cq = p_ref[:, 0:q1]
        ck = p_ref[:, q1:k1]
        cq_ref[...] = (cq * lax.rsqrt(jnp.mean(cq * cq, axis=-1, keepdims=True) + EPS) * qg_ref[...]).astype(BF16)
        ck_ref[...] = (ck * lax.rsqrt(jnp.mean(ck * ck, axis=-1, keepdims=True) + EPS) * kg_ref[...]).astype(BF16)

    return pl.pallas_call(
        body, name=name, grid=(t // ROWS,),
        in_specs=[_row_spec(MLA_IN), _const_spec((1, MLA_Q_RANK)), _const_spec((1, MLA_KV_RANK))],
        out_specs=(_row_spec(MLA_Q_RANK), _row_spec(MLA_KV_RANK)),
        out_shape=(jax.ShapeDtypeStruct((t, MLA_Q_RANK), BF16), jax.ShapeDtypeStruct((t, MLA_KV_RANK), BF16)),
        compiler_params=_params("parallel"),
    )(proj, qg, kvg)


def _mla_prep_bwd(proj, qg, kvg, dcq, dck, dkr, *, name):
    t = proj.shape[0]
    q1, k1 = MLA_Q_RANK, MLA_Q_RANK + MLA_KV_RANK

    def body(p_ref, qg_ref, kg_ref, dq_ref, dk_ref, dr_ref, dp_ref, st_ref):
        @pl.when(pl.program_id(0) == 0)
        def _():
            st_ref[...] = jnp.zeros_like(st_ref)

        for lo, hi, g_ref, d_ref in ((0, q1, qg_ref, dq_ref), (q1, k1, kg_ref, dk_ref)):
            xv = p_ref[:, lo:hi]
            dn = d_ref[...]
            r = lax.rsqrt(jnp.mean(xv * xv, axis=-1, keepdims=True) + EPS)
            xh = xv * r
            dxh = dn * g_ref[...]
            dp_ref[:, lo:hi] = (r * (dxh - xh * jnp.mean(dxh * xh, axis=-1, keepdims=True))).astype(dp_ref.dtype)
            st_ref[0:1, lo:hi] += jnp.sum(dn * xh, axis=0, keepdims=True)
        dp_ref[:, k1:MLA_IN] = dr_ref[:, 0:MLA_ROPE].astype(dp_ref.dtype)

    return pl.pallas_call(
        body, name=name, grid=(t // ROWS,),
        in_specs=[_row_spec(MLA_IN), _const_spec((1, MLA_Q_RANK)), _const_spec((1, MLA_KV_RANK)),
                  _row_spec(MLA_Q_RANK), _row_spec(MLA_KV_RANK), _row_spec(LANES)],
        out_specs=(_row_spec(MLA_IN), _const_spec((8, MLA_IN))),
        out_shape=(jax.ShapeDtypeStruct((t, MLA_IN), BF16), jax.ShapeDtypeStruct((8, MLA_IN), F32)),
        compiler_params=_params("arbitrary"),
    )(proj, qg, kvg, dcq, dck, dkr)


ATT_BLOCK = 256
ATT_HEAD_BATCH = 8
ATT_HEAD_BATCH_BWD = 4
ATT_SCALE = MLA_QK ** -0.5


def _diagonal_mask(blk):
    return lax.broadcasted_iota(jnp.int32, (blk, blk), 1) <= lax.broadcasted_iota(jnp.int32, (blk, blk), 0)


def _swap_halves(xv, first):
    return jnp.where(first, pltpu.roll(xv, LANES - MLA_ROPE // 2, 1), pltpu.roll(xv, MLA_ROPE // 2, 1))


def _rope_qk(qf, proj, cos_t, sin_t, *, name):
    t = qf.shape[0]
    nrope = MLA_HEADS * MLA_ROPE
    q_blk = MLA_HEADS * MLA_NOPE // nrope
    k_blk = (MLA_Q_RANK + MLA_KV_RANK) // LANES

    def body(q_ref, p_ref, c_ref, s_ref, qo_ref, ko_ref):
        cv, sv = c_ref[...], s_ref[...]
        lane = lax.broadcasted_iota(jnp.int32, (ROWS, LANES), 1)
        first = (lane % MLA_ROPE) < (MLA_ROPE // 2)
        for i in range(nrope // LANES):
            sl = slice(i * LANES, (i + 1) * LANES)
            xv = q_ref[:, sl].astype(F32)
            qo_ref[:, sl] = (xv * cv + _swap_halves(xv, first) * sv).astype(qo_ref.dtype)
        kv = jnp.where(lane < MLA_ROPE, p_ref[...], 0.0)
        ko_ref[...] = (kv * cv + _swap_halves(kv, first) * sv).astype(ko_ref.dtype)

    return pl.pallas_call(
        body, name=name, grid=(t // ROWS,),
        in_specs=[pl.BlockSpec((ROWS, nrope), lambda i: (i, q_blk)), pl.BlockSpec((ROWS, LANES), lambda i: (i, k_blk)),
                  _row_spec(LANES), _row_spec(LANES)],
        out_specs=(_row_spec(nrope), _row_spec(LANES)),
        out_shape=(jax.ShapeDtypeStruct((t, nrope), BF16), jax.ShapeDtypeStruct((t, LANES), BF16)),
        compiler_params=_params("parallel"),
    )(qf, proj, cos_t, sin_t)


def _rope_qk_bwd(dqr, dkr_parts, cos_t, sin_t, *, name):
    t, nrope = dqr.shape
    ng = dkr_parts.shape[0]

    def body(d_ref, k_ref, c_ref, s_ref, qo_ref, ko_ref):
        cv, sv = c_ref[...], s_ref[...]
        lane = lax.broadcasted_iota(jnp.int32, (ROWS, LANES), 1)
        first = (lane % MLA_ROPE) < (MLA_ROPE // 2)
        for i in range(nrope // LANES):
            sl = slice(i * LANES, (i + 1) * LANES)
            dv = d_ref[:, sl]
            qo_ref[:, sl] = (dv * cv + _swap_halves(dv * sv, first)).astype(qo_ref.dtype)
        dk = k_ref[0]
        for g in range(1, ng):
            dk = dk + k_ref[g]
        dk = jnp.where(lane < MLA_ROPE, dk, 0.0)
        ko_ref[...] = jnp.where(lane < MLA_ROPE, dk * cv + _swap_halves(dk * sv, first), 0.0)

    return pl.pallas_call(
        body, name=name, grid=(t // ROWS,),
        in_specs=[_row_spec(nrope), pl.BlockSpec((ng, ROWS, LANES), lambda i: (0, i, 0)), _row_spec(LANES),
                  _row_spec(LANES)],
        out_specs=(_row_spec(nrope), _row_spec(LANES)),
        out_shape=(jax.ShapeDtypeStruct((t, nrope), BF16), jax.ShapeDtypeStruct((t, LANES), F32)),
        compiler_params=_params("parallel"),
    )(dqr, dkr_parts, cos_t, sin_t)


def _attn_tm_fwd(qf, qr, kvf, kr, *, name):
    t = qf.shape[0]
    nh, dn, dr, dv = MLA_HEADS, MLA_NOPE, MLA_ROPE, MLA_V
    blk = min(ATT_BLOCK, t)
    hb = ATT_HEAD_BATCH
    hs = range(hb)

    def body(q_ref, qr_ref, kv_ref, kr_ref, o_ref, l_ref):
        i = pl.program_id(1)
        qc = [jnp.concatenate([q_ref[:, h * dn:(h + 1) * dn].astype(MXU_DTYPE), qr_ref[:, h * dr:(h + 1) * dr]], axis=1)
              for h in hs]

        def step(j, carry, diagonal=False):
            m, l, acc = carry[:hb], carry[hb:2 * hb], carry[2 * hb:]
            rows = pl.ds(pl.multiple_of(j * blk, blk), blk)
            krj = kr_ref[rows, 0:dr]
            s = [_dotb(qc[h], jnp.concatenate([kv_ref[rows, h * (dn + dv):h * (dn + dv) + dn], krj], axis=1), NT)
                 for h in hs]
            s = [s[h] * ATT_SCALE for h in hs]
            if diagonal:
                mask = _diagonal_mask(blk)
                s = [jnp.where(mask, s[h], NEG) for h in hs]
            m_new = [jnp.maximum(m[h], jnp.max(s[h], axis=-1, keepdims=True)) for h in hs]
            p = [jnp.exp(s[h] - m_new[h]) for h in hs]
            pv = [_dotb(p[h], kv_ref[rows, h * (dn + dv) + dn:(h + 1) * (dn + dv)], NN) for h in hs]
            alpha = [jnp.exp(m[h] - m_new[h]) for h in hs]
            l = [alpha[h] * l[h] + jnp.sum(p[h], axis=-1, keepdims=True) for h in hs]
            acc = [alpha[h] * acc[h] + pv[h] for h in hs]
            return tuple(m_new) + tuple(l) + tuple(acc)

        init = ((jnp.full((blk, 1), NEG, F32),) * hb + (jnp.zeros((blk, 1), F32),) * hb
                + (jnp.zeros((blk, dv), F32),) * hb)
        out = step(i, lax.fori_loop(0, i, step, init), diagonal=True)
        for h in hs:
            m, l, acc = out[h], out[hb + h], out[2 * hb + h]
            o_ref[:, h * dv:(h + 1) * dv] = (acc / l).astype(o_ref.dtype)
            l_ref[h] = jnp.broadcast_to(m + jnp.log(l), (blk, LANES))

    return pl.pallas_call(
        body, name=name, grid=(nh // hb, t // blk),
        in_specs=[pl.BlockSpec((blk, hb * dn), lambda g, i: (i, g)), pl.BlockSpec((blk, hb * dr), lambda g, i: (i, g)),
                  pl.BlockSpec((t, hb * (dn + dv)), lambda g, i: (0, g)), pl.BlockSpec((t, LANES), lambda g, i: (0, 0))],
        out_specs=(pl.BlockSpec((blk, hb * dv), lambda g, i: (i, g)),
                   pl.BlockSpec((hb, blk, LANES), lambda g, i: (g, i, 0))),
        out_shape=(jax.ShapeDtypeStruct((t, nh * dv), BF16), jax.ShapeDtypeStruct((nh, t, LANES), F32)),
        compiler_params=_params("parallel", "parallel"),
    )(qf, qr, kvf, kr)


def _attn_tm_bwd(qf, qr, kvf, kr, o, lse, do, *, name):
    t = qf.shape[0]
    nh, dn, dr, dv = MLA_HEADS, MLA_NOPE, MLA_ROPE, MLA_V
    blk = min(ATT_BLOCK, t)
    nb = t // blk
    hb = ATT_HEAD_BATCH_BWD
    hs = range(hb)
    ng = nh // hb

    def body(q_ref, qr_ref, kv_ref, kr_ref, o_ref, l_ref, do_ref, dqn_ref, dqr_ref, dkv_ref, dkr_ref):
        j = pl.program_id(1)

        @pl.when(j == 0)
        def _():
            dqn_ref[...] = jnp.zeros_like(dqn_ref)
            dqr_ref[...] = jnp.zeros_like(dqr_ref)

        krj = kr_ref[:, 0:dr]
        kc = [jnp.concatenate([kv_ref[:, h * (dn + dv):h * (dn + dv) + dn], krj], axis=1) for h in hs]
        vv = [kv_ref[:, h * (dn + dv) + dn:(h + 1) * (dn + dv)] for h in hs]

        def step(i, carry, diagonal=False):
            dkn_acc, dv_acc, dkr_acc = carry[:hb], carry[hb:2 * hb], carry[2 * hb]
            rows = pl.ds(pl.multiple_of(i * blk, blk), blk)
            qc = [jnp.concatenate([q_ref[rows, h * dn:(h + 1) * dn].astype(MXU_DTYPE),
                                   qr_ref[rows, h * dr:(h + 1) * dr]], axis=1) for h in hs]
            dov = [do_ref[rows, h * dv:(h + 1) * dv] for h in hs]
            s = [_dotb(qc[h], kc[h], NT) for h in hs]
            dp = [_dotb(dov[h], vv[h], NT) for h in hs]
            s = [s[h] * ATT_SCALE for h in hs]
            if diagonal:
                mask = _diagonal_mask(blk)
                s = [jnp.where(mask, s[h], NEG) for h in hs]
            p = [jnp.exp(s[h] - l_ref[h, rows, :][:, 0:1]) for h in hs]
            delta = [jnp.sum(dov[h].astype(F32) * o_ref[rows, h * dv:(h + 1) * dv].astype(F32), axis=-1, keepdims=True)
                     for h in hs]
            ds = [p[h] * (dp[h] - delta[h]) * ATT_SCALE for h in hs]
            dvn = [_dotb(p[h], dov[h], TN) for h in hs]
            dkc = [_dotb(ds[h], qc[h], TN) for h in hs]
            dqc = [_dotb(ds[h], kc[h], NN) for h in hs]
            for h in hs:
                dqn_ref[rows, h * dn:(h + 1) * dn] += dqc[h][:, 0:dn]
                dqr_ref[rows, h * dr:(h + 1) * dr] += dqc[h][:, dn:dn + dr]
            dkr_new = dkr_acc
            for h in hs:
                dkr_new = dkr_new + dkc[h][:, dn:dn + dr]
            return (tuple(dkn_acc[h] + dkc[h][:, 0:dn] for h in hs) + tuple(dv_acc[h] + dvn[h] for h in hs)
                    + (dkr_new,))

        init = (jnp.zeros((blk, dn), F32),) * hb + (jnp.zeros((blk, dv), F32),) * hb + (jnp.zeros((blk, dr), F32),)
        out = lax.fori_loop(j + 1, nb, step, step(j, init, diagonal=True))
        for h in hs:
            dkv_ref[:, h * (dn + dv):h * (dn + dv) + dn] = out[h].astype(dkv_ref.dtype)
            dkv_ref[:, h * (dn + dv) + dn:(h + 1) * (dn + dv)] = out[hb + h].astype(dkv_ref.dtype)
        dkr_ref[0, :, 0:dr] = out[2 * hb]
        dkr_ref[0, :, dr:LANES] = jnp.zeros((blk, LANES - dr), F32)

    full = lambda w: pl.BlockSpec((t, w), lambda g, j: (0, g))
    return pl.pallas_call(
        body, name=name, grid=(ng, nb),
        in_specs=[full(hb * dn), full(hb * dr), pl.BlockSpec((blk, hb * (dn + dv)), lambda g, j: (j, g)),
                  pl.BlockSpec((blk, LANES), lambda g, j: (j, 0)), full(hb * dv),
                  pl.BlockSpec((hb, t, LANES), lambda g, j: (g, 0, 0)), full(hb * dv)],
        out_specs=(full(hb * dn), full(hb * dr), pl.BlockSpec((blk, hb * (dn + dv)), lambda g, j: (j, g)),
                   pl.BlockSpec((1, blk, LANES), lambda g, j: (g, j, 0))),
        out_shape=(jax.ShapeDtypeStruct((t, nh * dn), F32), jax.ShapeDtypeStruct((t, nh * dr), F32),
                   jax.ShapeDtypeStruct((t, nh * (dn + dv)), BF16), jax.ShapeDtypeStruct((ng, t, LANES), F32)),
        compiler_params=_params("parallel", "arbitrary"),
    )(qf, qr, kvf, kr, o, lse, do)


def _ada_mod(c_all, ada_w, ada_b_cols, *, name):
    nl, d, wc = ada_w.shape

    def body(c_ref, w_ref, b_ref, o_ref):
        cv = c_ref[...]
        o_ref[0] = _dotb(cv * _sigmoid(cv), w_ref[0], NN) + b_ref[0]

    return pl.pallas_call(
        body, name=name, grid=(nl,),
        in_specs=[_const_spec((N_DEV, d)), pl.BlockSpec((1, d, wc), lambda l: (l, 0, 0)),
                  pl.BlockSpec((1, 1, wc), lambda l: (l, 0, 0))],
        out_specs=pl.BlockSpec((1, N_DEV, wc), lambda l: (l, 0, 0)),
        out_shape=jax.ShapeDtypeStruct((nl, N_DEV, wc), F32), compiler_params=_params("parallel"),
    )(c_all, ada_w, ada_b_cols)


def _adam_math(g, w, m, v):
    m2 = ADAM_B1 * m + (1.0 - ADAM_B1) * g
    v2 = ADAM_B2 * v + (1.0 - ADAM_B2) * (g * g)
    delta = -ADAM_LR * ((m2 / ADAM_BC1) / (jnp.sqrt(v2 / ADAM_BC2) + ADAM_EPS) + ADAM_WD * w)
    return delta, m2, v2


def _ada_grad_adamw(c_all, dmod_cols, w, m, v, *, name):
    nl, d, wc = w.shape
    tr = 256

    def body(c_ref, dm_ref, w_ref, m_ref, v_ref, g_ref, d_ref, m2_ref, v2_ref):
        cv = c_ref[...]
        g = _dotf(cv * _sigmoid(cv), dm_ref[0], TN)
        delta, m2, v2 = _adam_math(g, w_ref[0], m_ref[0], v_ref[0])
        g_ref[0], d_ref[0], m2_ref[0], v2_ref[0] = g, delta, m2, v2

    blk = pl.BlockSpec((1, tr, wc), lambda l, i: (l, i, 0))
    return pl.pallas_call(
        body, name=name, grid=(nl, d // tr),
        in_specs=[pl.BlockSpec((N_DEV, tr), lambda l, i: (0, i)), pl.BlockSpec((1, N_DEV, wc), lambda l, i: (l, 0, 0)),
                  blk, blk, blk],
        out_specs=(blk,) * 4, out_shape=(jax.ShapeDtypeStruct(w.shape, F32),) * 4,
        compiler_params=_params("parallel", "parallel"),
    )(c_all, dmod_cols, w, m, v)


def _adamw(parts, w, m, v, *, name):
    nl, r, c = w.shape
    ns = parts[0].shape[0]
    lanes_padded = -(-c // LANES) * LANES
    row_bytes = 2 * nl * ns * lanes_padded * parts[0].dtype.itemsize
    tr = _pick(r, min(256, max(16, (VMEM_LIMIT // 2) // row_bytes)), 16)
    tc = c
    if tr * row_bytes > VMEM_LIMIT // 2:
        tc = _pick(c, max(LANES, c * (VMEM_LIMIT // 2) // (tr * row_bytes)))

    def body(*refs):
        p_refs = refs[:nl]
        w_ref, m_ref, v_ref, g_ref, d_ref, m2_ref, v2_ref = refs[nl:]
        layer = pl.program_id(0)
        for q in range(nl):
            @pl.when(layer == q)
            def _(q=q):
                g = p_refs[q][0].astype(F32)
                for s in range(1, ns):
                    g = g + p_refs[q][s].astype(F32)
                delta, m2, v2 = _adam_math(g, w_ref[0], m_ref[0], v_ref[0])
                g_ref[0], d_ref[0], m2_ref[0], v2_ref[0] = g, delta, m2, v2

    blk = pl.BlockSpec((1, tr, tc), lambda l, i, j: (l, i, j))
    p_specs = [pl.BlockSpec((ns, tr, tc), lambda l, i, j, q=q: (0, jnp.where(l == q, i, 0), jnp.where(l == q, j, 0)))
               for q in range(nl)]
    return pl.pallas_call(
        body, name=name, grid=(nl, r // tr, c // tc),
        in_specs=p_specs + [blk, blk, blk],
        out_specs=(blk,) * 4, out_shape=(jax.ShapeDtypeStruct(w.shape, F32),) * 4,
        compiler_params=_params("arbitrary", "arbitrary", "arbitrary"),
    )(*parts, w, m, v)


def _sum_parts(parts, *, name):
    ns, r, c = parts.shape

    def body(p_ref, o_ref):
        acc = p_ref[0]
        for s in range(1, ns):
            acc = acc + p_ref[s]
        o_ref[...] = acc

    return pl.pallas_call(
        body, name=name, out_shape=jax.ShapeDtypeStruct((r, c), F32),
        in_specs=[pl.BlockSpec(memory_space=pltpu.VMEM)], out_specs=pl.BlockSpec(memory_space=pltpu.VMEM),
    )(parts)


def _pack(arrs):
    flat = jnp.concatenate([a.reshape(-1).astype(F32) for a in arrs])
    pad = (-flat.shape[0]) % (8 * LANES)
    return jnp.pad(flat, (0, pad)).reshape(-1, LANES)


def _unpack(packed, shapes, lead=()):
    flat = packed.reshape(lead + (-1,))
    out, off = [], 0
    for s in shapes:
        n = math.prod(s)
        out.append(flat[..., off:off + n].reshape(lead + tuple(s)))
        off += n
    return out


def _gather_rows(g):
    _, nl, rs, c = g.shape
    return jnp.transpose(g, (1, 0, 2, 3)).reshape(nl, N_DEV * rs, c)


def _row(v):
    return v.reshape(1, -1)


def _local_step(x, target, mod, cos_t, sin_t, rep, get_weights, put_grads):
    t = x.shape[0]
    saved = []
    for layer in range(DEPTH):
        j = layer // 2
        tag = f"l{layer}"
        shift_m, scale_m, gate_m, shift_f, scale_f, gate_f = [_row(mod[layer, i]) for i in range(N_MOD)]
        lw = dict(get_weights(layer, "mix", x))
        rec = {"x0": x, "lw": lw}
        h = _adaln_fwd(x, _row(rep["norm_mix_g"][layer]), scale_m, shift_m, name=f"adaln_mix_{tag}")
        rec["h"] = h
        if layer % 2 == 0:
            proj = _mm(h, lw["wt_in"], mode="nt", out_dtype=F32, tm=256, tn=GDN_MAIN, b_rows=GDN_MAIN,
                       dep=lw["dep_mix"], name=f"gdn_in_{tag}")
            ab = _mm(h, lw["wt_ab"], mode="nt", out_dtype=F32, name=f"gdn_in_ab_{tag}")
            qkv = _gdn_prep_fwd(proj, rep["gdn_conv_wt"][j], name=f"gdn_prep_{tag}")
            gbeta = _gdn_gate_fwd(ab, rep["gdn_gate_prm"][j], name=f"gdn_gate_{tag}")
            o, states, tinvs = _gdn_chunk_fwd(qkv, gbeta, name=f"gdn_chunk_{tag}")
            og = _gdn_onorm_fwd(o, proj, _row(rep["gdn_norm_g"][j]), name=f"gdn_onorm_{tag}")
            x, y = _mm_resid(og, lw["w_out"], x, gate_m, name=f"gdn_out_{tag}")
            rec.update(proj=proj, ab=ab, qkv=qkv, gbeta=gbeta, states=states, tinvs=tinvs, o=o, og=og, y=y)
        else:
            proj = _mm(h, lw["w_in"], mode="nn", out_dtype=F32, dep=lw["dep_mix"], name=f"mla_in_{tag}")
            cq, ck = _mla_prep_fwd(proj, _row(rep["mla_q_norm_g"][j]), _row(rep["mla_kv_norm_g"][j]),
                                   name=f"mla_prep_{tag}")
            qf = _mm(cq, lw["wt_uq"], mode="nt", out_dtype=BF16, name=f"mla_uq_{tag}")
            kvf = _mm(ck, lw["w_ukv"], mode="nn", out_dtype=BF16, name=f"mla_ukv_{tag}")
            qr, kr = _rope_qk(qf, proj, cos_t, sin_t, name=f"rope_{tag}")
            oc, lse = _attn_tm_fwd(qf, qr, kvf, kr, name=f"attn_{tag}")
            x, y = _mm_resid(oc, lw["w_out"], x, gate_m, name=f"mla_out_{tag}")
            rec.update(proj=proj, cq=cq, ck=ck, qf=qf, qr=qr, kvf=kvf, kr=kr, lse=lse, oc=oc, y=y)
        rec["x1"] = x
        lw.update(get_weights(layer, "ffn", x))
        h2 = _adaln_fwd(x, _row(rep["norm_ffn_g"][layer]), scale_f, shift_f, name=f"adaln_ffn_{tag}")
        s, a2, b2 = _ffn_gu_fwd(h2, lw["wt_g"], lw["wt_u"], lw["dep_ffn"], name=f"ffn_gu_{tag}")
        x, y2 = _mm_resid(s, lw["w_down"], x, gate_f, tm=512, name=f"ffn_down_{tag}")
        rec.update(h2=h2, a2=a2, b2=b2, s=s, y2=y2)
        saved.append(rec)

    dx, st, ls = _loss_head(x, _row(rep["final_norm_g"]), target, name="loss_head")
    loss = ls[0, 0]
    grads = {"final_norm_g": st[0]}
    per_layer = {k: [None] * DEPTH for k in ("norm_mix_g", "norm_ffn_g")}
    per_gdn = {k: [None] * 2 for k in ("gdn_conv_wt", "gdn_a_log", "gdn_dt_bias", "gdn_norm_g")}
    per_mla = {k: [None] * 2 for k in ("mla_q_norm_g", "mla_kv_norm_g")}
    dmod = [None] * DEPTH
    dep = jnp.zeros((8, LANES), F32)

    for layer in reversed(range(DEPTH)):
        j = layer // 2
        tag = f"l{layer}"
        rec = saved[layer]
        lw = rec["lw"]
        shift_m, scale_m, gate_m, shift_f, scale_f, gate_f = [_row(mod[layer, i]) for i in range(N_MOD)]
        if layer == DEPTH - 1:
            dy2, st_g = _gate_bwd(dx, rec["y2"], gate_f, dep, name=f"gate_bwd_ffn_{tag}")
            dgate_f = st_g[0]
        dw_down = _mm(rec["s"], dy2, mode="tn", out_dtype=BF16, tm=FFN_BLOCK, tn=1024, name=f"ffn_down_dw_{tag}")
        da2, db2 = _ffn_down_dx(dy2, lw["w_down"], rec["a2"], rec["b2"], name=f"ffn_down_dx_{tag}")
        dwt_g = _mm(da2, rec["h2"], mode="tn", out_dtype=BF16, tm=FFN_BLOCK, tn=1024, name=f"ffn_g_dw_{tag}")
        dwt_u = _mm(db2, rec["h2"], mode="tn", out_dtype=BF16, tm=FFN_BLOCK, tn=1024, name=f"ffn_u_dw_{tag}")
        dep = put_grads(layer, "ffn", {"wt_g": dwt_g, "wt_u": dwt_u, "w_down": dw_down})
        dh2 = _ffn_gu_dx(da2, db2, lw["wt_g"], lw["wt_u"], name=f"ffn_gu_dx_{tag}")
        dx, st_n, dy = _adaln_gate_bwd(rec["x1"], _row(rep["norm_ffn_g"][layer]), scale_f, shift_f, dh2, dx, dep,
                                       rec["y"], gate_m, name=f"adaln_ffn_bwd_{tag}")
        per_layer["norm_ffn_g"][layer] = st_n[0]
        dscale_f, dshift_f, dgate_m = st_n[1], st_n[2], st_n[3]
        big = {}
        if layer % 2 == 0:
            big["w_out"] = _mm(rec["og"], dy, mode="tn", out_dtype=BF16, name=f"gdn_out_dw_{tag}")
            dog = _mm(dy, lw["w_out"], mode="nt", out_dtype=BF16, name=f"gdn_out_dx_{tag}")
            do, dgp, st_o = _gdn_onorm_bwd(rec["o"], rec["proj"], _row(rep["gdn_norm_g"][j]), dog,
                                           name=f"gdn_onorm_bwd_{tag}")
            per_gdn["gdn_norm_g"][j] = st_o[0]
            dqkv, dgb = _gdn_chunk_bwd(rec["qkv"], rec["gbeta"], rec["states"], rec["tinvs"], do,
                                       name=f"gdn_chunk_bwd_{tag}")
            dab, st_a = _gdn_gate_bwd(rec["ab"], rep["gdn_gate_prm"][j], dgb, name=f"gdn_gate_bwd_{tag}")
            per_gdn["gdn_a_log"][j] = st_a[0, :GDN_HEADS]
            per_gdn["gdn_dt_bias"][j] = st_a[1, :GDN_HEADS]
            dpre, dcw = _gdn_prep_bwd(rec["proj"], rep["gdn_conv_wt"][j], dqkv, name=f"gdn_prep_bwd_{tag}")
            per_gdn["gdn_conv_wt"][j] = dcw
            dproj = jnp.concatenate([dpre, dgp], axis=1)
            dw_main = _mm(dproj, rec["h"], mode="tn", out_dtype=BF16, tm=512, tn=1024, name=f"gdn_in_dw_{tag}")
            dw_ab = _mm(dab, rec["h"], mode="tn", out_dtype=BF16, tn=1024, name=f"gdn_in_ab_dw_{tag}")
            big["wt_in"] = jnp.concatenate([dw_main, dw_ab[:2 * GDN_HEADS]], axis=0)
            dep = put_grads(layer, "gdn", big)
            dh_ab = _mm(dab, lw["wt_ab"], mode="nn", out_dtype=F32, tn=1024, name=f"gdn_in_ab_dx_{tag}")
            dh = _mm(dproj, lw["wt_in"], mode="nn", out_dtype=BF16, add=dh_ab, tm=256, tn=1024, b_rows=GDN_MAIN,
                     name=f"gdn_in_dx_{tag}")
        else:
            big["w_out"] = _mm(rec["oc"], dy, mode="tn", out_dtype=BF16, name=f"mla_out_dw_{tag}")
            doc = _mm(dy, lw["w_out"], mode="nt", out_dtype=BF16, name=f"mla_out_dx_{tag}")
            dqn, dqr, dkvf, dkr_parts = _attn_tm_bwd(rec["qf"], rec["qr"], rec["kvf"], rec["kr"], rec["oc"],
                                                     rec["lse"], doc, name=f"attn_bwd_{tag}")
            dqr_un, dkr_un = _rope_qk_bwd(dqr, dkr_parts, cos_t, sin_t, name=f"rope_bwd_{tag}")
            n_nope = MLA_HEADS * MLA_NOPE
            big["wt_uq"] = jnp.concatenate(
                [_mm(dqn, rec["cq"], mode="tn", out_dtype=BF16, name=f"mla_uq_dw_nope_{tag}"),
                 _mm(dqr_un, rec["cq"], mode="tn", out_dtype=BF16, name=f"mla_uq_dw_rope_{tag}")], axis=0)
            big["w_ukv"] = _mm(rec["ck"], dkvf, mode="tn", out_dtype=BF16, name=f"mla_ukv_dw_{tag}")
            dcq = _mm(dqr_un, lw["wt_uq"][n_nope:], mode="nn", out_dtype=F32, name=f"mla_uq_dx_rope_{tag}")
            dcq = _mm(dqn, lw["wt_uq"], mode="nn", out_dtype=F32, add=dcq, b_rows=n_nope,
                      name=f"mla_uq_dx_nope_{tag}")
            dck = _mm(dkvf, lw["w_ukv"], mode="nt", out_dtype=F32, name=f"mla_ukv_dx_{tag}")
            dproj, st_p = _mla_prep_bwd(rec["proj"], _row(rep["mla_q_norm_g"][j]), _row(rep["mla_kv_norm_g"][j]),
                                        dcq, dck, dkr_un, name=f"mla_prep_bwd_{tag}")
            per_mla["mla_q_norm_g"][j] = st_p[0, :MLA_Q_RANK]
            per_mla["mla_kv_norm_g"][j] = st_p[0, MLA_Q_RANK:MLA_Q_RANK + MLA_KV_RANK]
            big["w_in"] = _mm(rec["h"], dproj, mode="tn", out_dtype=BF16, name=f"mla_in_dw_{tag}")
            dep = put_grads(layer, "mla", big)
            dh = _mm(dproj, lw["w_in"], mode="nt", out_dtype=BF16, name=f"mla_in_dx_{tag}")
        if layer > 0:
            below = saved[layer - 1]
            dx, st_n, dy2 = _adaln_gate_bwd(rec["x0"], _row(rep["norm_mix_g"][layer]), scale_m, shift_m, dh, dx, dep,
                                            below["y2"], _row(mod[layer - 1, N_MOD - 1]),
                                            name=f"adaln_mix_bwd_{tag}")
        else:
            dx, st_n = _adaln_bwd(rec["x0"], _row(rep["norm_mix_g"][layer]), scale_m, shift_m, dh, dx, dep,
                                  name=f"adaln_mix_bwd_{tag}")
        per_layer["norm_mix_g"][layer] = st_n[0]
        dmod[layer] = jnp.stack([st_n[2], st_n[1], dgate_m, dshift_f, dscale_f, dgate_f])
        if layer > 0:
            dgate_f = st_n[3]

    for d in (per_layer, per_gdn, per_mla):
        for k, v in d.items():
            grads[k] = jnp.stack(v)
    return loss, dx, jnp.stack(dmod), grads


BIG = ("gdn_w_in", "gdn_w_out", "mla_w_in", "mla_w_uq", "mla_w_ukv", "mla_w_out", "ffn_w_gate", "ffn_w_up",
       "ffn_w_down")
TRANSPOSED = ("gdn_w_in", "mla_w_uq", "ffn_w_gate", "ffn_w_up")
AHEAD = 4


def _view(k, a):
    return jnp.transpose(a, (0, 2, 1)) if k in TRANSPOSED else a
SMALL = ("ada_b", "norm_mix_g", "norm_ffn_g", "gdn_conv_w", "gdn_a_log", "gdn_dt_bias", "gdn_norm_g",
         "mla_q_norm_g", "mla_kv_norm_g", "final_norm_g")
WEIGHTS = ("ada_w", "ada_b", "norm_mix_g", "norm_ffn_g", "gdn_w_in", "gdn_conv_w", "gdn_a_log", "gdn_dt_bias",
           "gdn_norm_g", "gdn_w_out", "mla_w_in", "mla_q_norm_g", "mla_kv_norm_g", "mla_w_uq", "mla_w_ukv",
           "mla_w_out", "ffn_w_gate", "ffn_w_up", "ffn_w_down", "final_norm_g")


def _uq_to_kernel_layout(w, axis=-1):
    axis = axis % w.ndim
    lead, tail = w.shape[:axis], w.shape[axis + 1:]
    w4 = w.reshape(lead + (MLA_HEADS, MLA_QK) + tail)
    nope = lax.slice_in_dim(w4, 0, MLA_NOPE, axis=axis + 1).reshape(lead + (-1,) + tail)
    rope = lax.slice_in_dim(w4, MLA_NOPE, MLA_QK, axis=axis + 1).reshape(lead + (-1,) + tail)
    return jnp.concatenate([nope, rope], axis=axis)


def _uq_from_kernel_layout(w, axis=-1):
    axis = axis % w.ndim
    lead, tail = w.shape[:axis], w.shape[axis + 1:]
    nope = lax.slice_in_dim(w, 0, MLA_HEADS * MLA_NOPE, axis=axis).reshape(lead + (MLA_HEADS, MLA_NOPE) + tail)
    rope = lax.slice_in_dim(w, MLA_HEADS * MLA_NOPE, MLA_HEADS * MLA_QK, axis=axis).reshape(
        lead + (MLA_HEADS, MLA_ROPE) + tail)
    return jnp.concatenate([nope, rope], axis=axis + 1).reshape(lead + (-1,) + tail)


def _group_names(layer, kind):
    if kind == "ffn":
        return ("ffn_w_gate", "ffn_w_up", "ffn_w_down")
    return ("gdn_w_in", "gdn_w_out") if layer % 2 == 0 else ("mla_w_in", "mla_w_uq", "mla_w_ukv", "mla_w_out")


def _layer_index(name, layer):
    return layer if name.startswith("ffn") else layer // 2


def _cols(g):
    return jnp.transpose(g, (1, 0, 2)).reshape(g.shape[1], N_DEV * g.shape[2])


def _rows(g):
    return g.reshape(N_DEV * g.shape[1], g.shape[2])


def _uncols(full):
    r, c = full.shape
    return jnp.transpose(full.reshape(r, N_DEV, c // N_DEV), (1, 0, 2))


def _unrows(full):
    r, c = full.shape
    return full.reshape(N_DEV, r // N_DEV, c)


def _group_weights(layer, kind, got, token):
    if kind == "ffn":
        return {"wt_g": _rows(got["ffn_w_gate"]), "wt_u": _rows(got["ffn_w_up"]), "w_down": _rows(got["ffn_w_down"]),
                "dep_ffn": token}
    if layer % 2 == 0:
        wt_in = _rows(got["gdn_w_in"])
        return dict(wt_in=wt_in, wt_ab=jnp.pad(wt_in[GDN_MAIN:], ((0, LANES - 2 * GDN_HEADS), (0, 0))),
                    w_out=_rows(got["gdn_w_out"]), dep_mix=token)
    return dict(w_in=_rows(got["mla_w_in"]), wt_uq=_uq_to_kernel_layout(_rows(got["mla_w_uq"]), axis=0),
                w_ukv=_cols(got["mla_w_ukv"]), w_out=_rows(got["mla_w_out"]), dep_mix=token)


def _layer_grad_slots(kind, big):
    if kind == "ffn":
        return {"ffn_w_gate": _unrows(big["wt_g"]), "ffn_w_up": _unrows(big["wt_u"]),
                "ffn_w_down": _unrows(big["w_down"])}
    if kind == "gdn":
        return {"gdn_w_in": _unrows(big["wt_in"]), "gdn_w_out": _unrows(big["w_out"])}
    return {"mla_w_in": _unrows(big["w_in"]), "mla_w_uq": _unrows(_uq_from_kernel_layout(big["wt_uq"], axis=0)),
            "mla_w_ukv": _uncols(big["w_ukv"]), "mla_w_out": _unrows(big["w_out"])}


def _small_weights(tiny, rep):
    prm = jnp.zeros((2, 8, LANES), F32)
    prm = prm.at[:, 0, :GDN_HEADS].set(rep["gdn_a_log"]).at[:, 1, :GDN_HEADS].set(rep["gdn_dt_bias"])
    out = {
        "gdn_conv_wt": jnp.transpose(_gather_rows(tiny["gdn_conv_w"]), (0, 2, 1)),
        "mla_q_norm_g": jnp.transpose(tiny["mla_q_norm_g"], (1, 0, 2)).reshape(2, MLA_Q_RANK),
        "mla_kv_norm_g": jnp.transpose(tiny["mla_kv_norm_g"], (1, 0, 2)).reshape(2, MLA_KV_RANK),
        "gdn_gate_prm": prm,
    }
    for k in ("norm_mix_g", "norm_ffn_g", "gdn_norm_g", "final_norm_g"):
        out[k] = rep[k]
    return out


def _rope_tables(positions):
    inv_freq = ROPE_THETA ** (-jnp.arange(0, MLA_ROPE, 2, dtype=F32) / MLA_ROPE)
    ang = positions.astype(F32)[:, None] * inv_freq
    cos, sin = jnp.cos(ang), jnp.sin(ang)
    reps = LANES // MLA_ROPE
    return jnp.tile(jnp.concatenate([cos, cos], axis=1), (1, reps)), jnp.tile(
        jnp.concatenate([-sin, sin], axis=1), (1, reps))


def kernel(x, c, positions, ada_w, ada_b, norm_mix_g, norm_ffn_g, gdn_w_in, gdn_conv_w, gdn_a_log, gdn_dt_bias, gdn_norm_g, gdn_w_out, mla_w_in, mla_q_norm_g, mla_kv_norm_g, mla_w_uq, mla_w_ukv, mla_w_out, ffn_w_gate, ffn_w_up, ffn_w_down, final_norm_g, loss_target, m_ada_w, m_ada_b, m_norm_mix_g, m_norm_ffn_g, m_gdn_w_in, m_gdn_conv_w, m_gdn_a_log, m_gdn_dt_bias, m_gdn_norm_g, m_gdn_w_out, m_mla_w_in, m_mla_q_norm_g, m_mla_kv_norm_g, m_mla_w_uq, m_mla_w_ukv, m_mla_w_out, m_ffn_w_gate, m_ffn_w_up, m_ffn_w_down, m_final_norm_g, v_ada_w, v_ada_b, v_norm_mix_g, v_norm_ffn_g, v_gdn_w_in, v_gdn_conv_w, v_gdn_a_log, v_gdn_dt_bias, v_gdn_norm_g, v_gdn_w_out, v_mla_w_in, v_mla_q_norm_g, v_mla_kv_norm_g, v_mla_w_uq, v_mla_w_ukv, v_mla_w_out, v_ffn_w_gate, v_ffn_w_up, v_ffn_w_down, v_final_norm_g):
    W = dict(ada_w=ada_w, ada_b=ada_b, norm_mix_g=norm_mix_g, norm_ffn_g=norm_ffn_g, gdn_w_in=gdn_w_in,
             gdn_conv_w=gdn_conv_w, gdn_a_log=gdn_a_log, gdn_dt_bias=gdn_dt_bias, gdn_norm_g=gdn_norm_g,
             gdn_w_out=gdn_w_out, mla_w_in=mla_w_in, mla_q_norm_g=mla_q_norm_g, mla_kv_norm_g=mla_kv_norm_g,
             mla_w_uq=mla_w_uq, mla_w_ukv=mla_w_ukv, mla_w_out=mla_w_out, ffn_w_gate=ffn_w_gate,
             ffn_w_up=ffn_w_up, ffn_w_down=ffn_w_down, final_norm_g=final_norm_g)
    M = dict(ada_w=m_ada_w, ada_b=m_ada_b, norm_mix_g=m_norm_mix_g, norm_ffn_g=m_norm_ffn_g, gdn_w_in=m_gdn_w_in,
             gdn_conv_w=m_gdn_conv_w, gdn_a_log=m_gdn_a_log, gdn_dt_bias=m_gdn_dt_bias, gdn_norm_g=m_gdn_norm_g,
             gdn_w_out=m_gdn_w_out, mla_w_in=m_mla_w_in, mla_q_norm_g=m_mla_q_norm_g,
             mla_kv_norm_g=m_mla_kv_norm_g, mla_w_uq=m_mla_w_uq, mla_w_ukv=m_mla_w_ukv, mla_w_out=m_mla_w_out,
             ffn_w_gate=m_ffn_w_gate, ffn_w_up=m_ffn_w_up, ffn_w_down=m_ffn_w_down, final_norm_g=m_final_norm_g)
    V = dict(ada_w=v_ada_w, ada_b=v_ada_b, norm_mix_g=v_norm_mix_g, norm_ffn_g=v_norm_ffn_g, gdn_w_in=v_gdn_w_in,
             gdn_conv_w=v_gdn_conv_w, gdn_a_log=v_gdn_a_log, gdn_dt_bias=v_gdn_dt_bias, gdn_norm_g=v_gdn_norm_g,
             gdn_w_out=v_gdn_w_out, mla_w_in=v_mla_w_in, mla_q_norm_g=v_mla_q_norm_g,
             mla_kv_norm_g=v_mla_kv_norm_g, mla_w_uq=v_mla_w_uq, mla_w_ukv=v_mla_w_ukv, mla_w_out=v_mla_w_out,
             ffn_w_gate=v_ffn_w_gate, ffn_w_up=v_ffn_w_up, ffn_w_down=v_ffn_w_down, final_norm_g=v_final_norm_g)
    me = 4 * lax.axis_index("x") + 2 * lax.axis_index("y") + lax.axis_index("c")
    t = x.shape[1]
    wc = ada_w.shape[-1]

    groups = [(layer, kind) for layer in range(DEPTH) for kind in ("mix", "ffn")]

    def group_srcs(i):
        layer, kind = groups[i]
        return [_view(k, W[k])[_layer_index(k, layer)].astype(BF16) for k in _group_names(layer, kind)]

    tiny_shapes = [c.shape, gdn_conv_w.shape, mla_q_norm_g.shape, mla_kv_norm_g.shape]
    first = _gather_two_level([_pack([c, gdn_conv_w, mla_q_norm_g, mla_kv_norm_g])] + group_srcs(0),
                              name="gather_first")
    tiny_g = first[0]
    c_g, conv_g, qn_g, kvn_g = _unpack(tiny_g, tiny_shapes, lead=(N_DEV,))
    c_all = c_g.reshape(N_DEV, D_MODEL)
    rep = _small_weights({"gdn_conv_w": conv_g, "mla_q_norm_g": qn_g, "mla_kv_norm_g": kvn_g}, W)

    def start_group(i, dep):
        layer, kind = groups[i]
        return _exchange_start(group_srcs(i), scatter=False, name=f"gather_start_{kind}_l{layer}", dep=dep)


    b_cols = lax.dynamic_slice_in_dim(ada_b, me * wc, wc, axis=1).reshape(DEPTH, 1, wc)
    mod_part = _ada_mod(c_all, ada_w, b_cols, name="ada_mod")
    (mod_g,) = _exchange([mod_part], scatter=False, name="gather_mod")
    mod_mine = lax.dynamic_index_in_dim(mod_g, me, axis=2, keepdims=False)
    mod = jnp.transpose(mod_mine, (1, 0, 2)).reshape(DEPTH, N_MOD, D_MODEL)
    gather = {1: start_group(1, mod_g)}
    for i in range(2, AHEAD + 1):
        gather[i] = start_group(i, gather[i - 1][4])

    def get_weights(layer, kind, after):
        i = groups.index((layer, kind))
        names = _group_names(layer, kind)
        if i == 0:
            return _group_weights(layer, kind, dict(zip(names, first[1:])), gather[AHEAD][4])
        srcs, lands = _exchange_wait(gather[i], after, scatter=False, name=f"gather_wait_{kind}_l{layer}")
        token = jnp.zeros((8, LANES), F32)
        if i + AHEAD < len(groups):
            gather[i + AHEAD] = start_group(i + AHEAD, lands[0])
            token = gather[i + AHEAD][4]
        got = {k: lax.dynamic_update_index_in_dim(z, s, me, 0) for k, s, z in zip(names, srcs, lands)}
        return _group_weights(layer, kind, got, token)

    scatter = []

    def put_grads(layer, kind, big):
        slots = _layer_grad_slots(kind, big)
        started = _exchange_start(list(slots.values()), scatter=True, name=f"scatter_start_{kind}_l{layer}")
        scatter.append((layer, kind, list(slots.keys()), started))
        return started[4]

    cos_t, sin_t = _rope_tables(positions[0])
    loss, dx, dmod, g = _local_step(x[0], loss_target[0], mod, cos_t, sin_t, rep, get_weights, put_grads)

    parts = {k: [None] * W[k].shape[0] for k in BIG}
    res = {}

    def wait_group(entry, after):
        layer, kind, names, started = entry
        srcs, lands = _exchange_wait(started, after, scatter=True, name=f"scatter_wait_{kind}_l{layer}")
        for k, s, z in zip(names, srcs, lands):
            own = lax.dynamic_index_in_dim(s, me, 0, keepdims=False)
            parts[k][_layer_index(k, layer)] = lax.dynamic_update_index_in_dim(z, own, me, 0)

    for entry in scatter[:-1]:
        wait_group(entry, dx)
    early = [k for k in BIG if k not in scatter[-1][2]]
    def update(k):
        outs = _adamw(parts[k], _view(k, W[k]), _view(k, M[k]), _view(k, V[k]), name=f"adamw_{k}")
        return tuple(_view(k, o) for o in outs)

    for k in early:
        res[k] = update(k)
    loss, dmod, done = lax.optimization_barrier((loss, dmod, [res[k] for k in early]))
    for k, r in zip(early, done):
        res[k] = r

    small_local = [dmod.reshape(DEPTH, N_MOD * D_MODEL), g["norm_mix_g"], g["norm_ffn_g"],
                   jnp.transpose(g["gdn_conv_wt"], (0, 2, 1)), g["gdn_a_log"], g["gdn_dt_bias"], g["gdn_norm_g"],
                   g["mla_q_norm_g"], g["mla_kv_norm_g"], g["final_norm_g"], loss.reshape(1)]
    small_shapes = [a.shape for a in small_local]
    (small_g,) = _exchange([_pack(small_local)], scatter=False, name="gather_small_grads")
    small_sum = _unpack(_sum_parts(small_g, name="sum_small_grads"), small_shapes)
    loss = small_sum[-1][0]
    dmod_all = _unpack(small_g, small_shapes[:1], lead=(N_DEV,))[0]
    sg = dict(zip(SMALL, small_sum))
    wait_group(scatter[-1], small_g)
    sg["gdn_conv_w"] = lax.dynamic_slice_in_dim(sg["gdn_conv_w"], me * gdn_conv_w.shape[1], gdn_conv_w.shape[1], 1)
    sg["mla_q_norm_g"] = lax.dynamic_slice_in_dim(sg["mla_q_norm_g"], me * mla_q_norm_g.shape[1],
                                                  mla_q_norm_g.shape[1], 1)
    sg["mla_kv_norm_g"] = lax.dynamic_slice_in_dim(sg["mla_kv_norm_g"], me * mla_kv_norm_g.shape[1],
                                                   mla_kv_norm_g.shape[1], 1)

    dmod_cols = jnp.transpose(lax.dynamic_slice_in_dim(dmod_all, me * wc, wc, axis=2), (1, 0, 2))
    res["ada_w"] = _ada_grad_adamw(c_all, dmod_cols, ada_w, m_ada_w, v_ada_w, name="ada_w_grad_adamw")
    for k in BIG:
        if k not in early:
            res[k] = update(k)
    shapes = [W[k].shape for k in SMALL]
    packed = [_pack([d[k] for k in SMALL]) for d in (sg, W, M, V)]
    outs = _adamw([packed[0][None]], packed[1][None], packed[2][None], packed[3][None], name="adamw_small")
    unpacked = [_unpack(o[0], shapes) for o in outs]
    for i, k in enumerate(SMALL):
        res[k] = tuple(u[i] for u in unpacked)

    return (loss, dx[None], *[res[k][0] for k in WEIGHTS], *[res[k][1] for k in WEIGHTS],
            *[res[k][2] for k in WEIGHTS], *[res[k][3] for k in WEIGHTS])
```

```python
import math

import jax
import jax.numpy as jnp
from jax import lax
from jax.experimental import pallas as pl
from jax.experimental.pallas import tpu as pltpu

F32 = jnp.float32
BF16 = jnp.bfloat16
MXU_DTYPE = jnp.bfloat16

N_DEV = 8
D_MODEL = 1024
DEPTH = 4
GDN_HEADS = 8
GDN_HEAD_DIM = 128
GDN_KEY_DIM = GDN_HEADS * GDN_HEAD_DIM
GDN_CHUNK = 64
GDN_HEAD_BATCH = 8
GDN_CONV = 4
GDN_PREP_HEADS = 2
GDN_MAIN = 4 * GDN_KEY_DIM
MLA_HEADS = 8
MLA_NOPE = 128
MLA_ROPE = 64
MLA_V = 128
MLA_Q_RANK = 384
MLA_KV_RANK = 256
MLA_IN = MLA_Q_RANK + MLA_KV_RANK + MLA_ROPE
MLA_QK = MLA_NOPE + MLA_ROPE
ROPE_THETA = 10000.0
D_FF = 2816
N_MOD = 6
EPS = 1e-6
LANES = 128
VMEM_LIMIT = 48 * 1024 * 1024

ADAM_LR = 0.001
ADAM_B1 = 0.9
ADAM_B2 = 0.999
ADAM_EPS = 1e-08
ADAM_WD = 0.01
ADAM_STEP = 10
ADAM_BC1 = 1.0 - ADAM_B1 ** ADAM_STEP
ADAM_BC2 = 1.0 - ADAM_B2 ** ADAM_STEP

NN = (((1,), (0,)), ((), ()))
NT = (((1,), (1,)), ((), ()))
TN = (((0,), (0,)), ((), ()))
NEG = -1e30


def _dotb(a, b, dims):
    return lax.dot_general(a.astype(MXU_DTYPE), b.astype(MXU_DTYPE), dims, preferred_element_type=F32)


def _split(a):
    hi = a.astype(BF16)
    return hi, (a - hi.astype(F32)).astype(BF16)


def _dotf(a, b, dims):
    ah, al = _split(a)
    bh, bl = _split(b)
    dot = lambda u, v: lax.dot_general(u, v, dims, preferred_element_type=F32)
    return dot(ah, bh) + (dot(ah, bl) + dot(al, bh))


def _params(*sem):
    return pltpu.CompilerParams(dimension_semantics=sem, vmem_limit_bytes=VMEM_LIMIT)


def _pick(n, pref, mult=LANES):
    best = None
    t = mult
    while t <= min(n, pref):
        if n % t == 0:
            best = t
        t += mult
    return best if best is not None else n


def _sigmoid(z):
    return 0.5 * jnp.tanh(0.5 * z) + 0.5


def _exchange(arrays, *, scatter, name):
    n = len(arrays)
    out_shape = tuple(
        jax.ShapeDtypeStruct(a.shape if scatter else (N_DEV,) + a.shape, a.dtype) for a in arrays)

    def body(*refs):
        ins, outs = refs[:n], refs[n:2 * n]
        send_sems, recv_sems, local_sems = refs[2 * n:]
        x, y, c = lax.axis_index("x"), lax.axis_index("y"), lax.axis_index("c")
        me = 4 * x + 2 * y + c
        copies = []
        for k in range(n):
            src_own = ins[k].at[me] if scatter else ins[k]
            own = pltpu.make_async_copy(src_own, outs[k].at[me], local_sems.at[k])
            own.start()
            copies.append(own)
        sends = []
        for p in range(1, N_DEV):
            px, py, pc = x ^ ((p >> 2) & 1), y ^ ((p >> 1) & 1), c ^ (p & 1)
            peer = 4 * px + 2 * py + pc
            for k in range(n):
                cp = pltpu.make_async_remote_copy(
                    src_ref=ins[k].at[peer] if scatter else ins[k],
                    dst_ref=outs[k].at[me],
                    send_sem=send_sems.at[k, p - 1],
                    recv_sem=recv_sems.at[k, p - 1],
                    device_id=(px, py, pc),
                    device_id_type=pl.DeviceIdType.MESH,
                )
                cp.start()
                sends.append((cp, k, peer, p))
        for cp, k, peer, p in sends:
            pltpu.make_async_remote_copy(
                src_ref=ins[k].at[peer] if scatter else ins[k],
                dst_ref=outs[k].at[peer],
                send_sem=send_sems.at[k, p - 1],
                recv_sem=recv_sems.at[k, p - 1],
                device_id=(x, y, c),
                device_id_type=pl.DeviceIdType.MESH,
            ).wait_recv()
        for cp, _, _, _ in sends:
            cp.wait_send()
        for own in copies:
            own.wait()

    any_spec = pl.BlockSpec(memory_space=pl.ANY)
    outs = pl.pallas_call(
        body,
        name=name,
        out_shape=out_shape,
        in_specs=[any_spec] * n,
        out_specs=tuple([any_spec] * n),
        scratch_shapes=[
            pltpu.SemaphoreType.DMA((n, N_DEV - 1)),
            pltpu.SemaphoreType.DMA((n, N_DEV - 1)),
            pltpu.SemaphoreType.DMA((n,)),
        ],
        compiler_params=pltpu.CompilerParams(has_side_effects=True),
    )(*arrays)
    return list(outs)


def _gather_two_level(arrays, *, name):
    n = len(arrays)
    out_shape = tuple(jax.ShapeDtypeStruct((N_DEV,) + a.shape, a.dtype) for a in arrays)

    def body(*refs):
        ins, outs = refs[:n], refs[n:2 * n]
        send_sems, recv_sems, local_sems = refs[2 * n:]
        x, y, c = lax.axis_index("x"), lax.axis_index("y"), lax.axis_index("c")
        me = 4 * x + 2 * y + c
        sibling = (x, y, 1 - c)
        chips = [(1 - x, y), (x, 1 - y), (1 - x, 1 - y)]

        def slot(px, py, pc):
            return 4 * px + 2 * py + pc

        def copy(k, q, block, to, src=None):
            return pltpu.make_async_remote_copy(
                src_ref=outs[k].at[slot(*block)] if src is None else src,
                dst_ref=outs[k].at[slot(*block)],
                send_sem=send_sems.at[k, q], recv_sem=recv_sems.at[k, q],
                device_id=to, device_id_type=pl.DeviceIdType.MESH)

        own = [pltpu.make_async_copy(ins[k], outs[k].at[me], local_sems.at[k]) for k in range(n)]
        for cp in own:
            cp.start()
        first = []
        for k in range(n):
            first.append(copy(k, 0, (x, y, c), sibling, src=ins[k]))
            first += [copy(k, 1 + j, (x, y, c), (*chip, c), src=ins[k]) for j, chip in enumerate(chips)]
        for cp in first:
            cp.start()
        passed = []
        for j, chip in enumerate(chips):
            for k in range(n):
                copy(k, 1 + j, (*chip, c), (x, y, c)).wait_recv()
                fwd = copy(k, 4 + j, (*chip, c), sibling)
                fwd.start()
                passed.append(fwd)
        for k in range(n):
            copy(k, 0, sibling, (x, y, c)).wait_recv()
            for j, chip in enumerate(chips):
                copy(k, 4 + j, (*chip, 1 - c), (x, y, c)).wait_recv()
        for cp in first + passed:
            cp.wait_send()
        for cp in own:
            cp.wait()

    any_spec = pl.BlockSpec(memory_space=pl.ANY)
    outs = pl.pallas_call(
        body, name=name, out_shape=out_shape, in_specs=[any_spec] * n, out_specs=tuple([any_spec] * n),
        scratch_shapes=[pltpu.SemaphoreType.DMA((n, N_DEV - 1)), pltpu.SemaphoreType.DMA((n, N_DEV - 1)),
                        pltpu.SemaphoreType.DMA((n,))],
        compiler_params=pltpu.CompilerParams(has_side_effects=True),
    )(*arrays)
    return list(outs)


def _peer(x, y, c, p):
    return x ^ ((p >> 2) & 1), y ^ ((p >> 1) & 1), c ^ (p & 1)


def _exchange_start(arrays, *, scatter, name, dep=None):
    n = len(arrays)
    deps = [] if dep is None else [dep]
    lands = [lax.empty(a.shape if scatter else (N_DEV,) + a.shape, a.dtype) for a in arrays]

    def body(*refs):
        ins, zones = refs[:n], refs[n:2 * n]
        send_sems, recv_sems = refs[2 * n + len(deps)], refs[2 * n + len(deps) + 1]
        token = refs[-1]
        x, y, c = lax.axis_index("x"), lax.axis_index("y"), lax.axis_index("c")
        me = 4 * x + 2 * y + c
        for p in range(1, N_DEV):
            px, py, pc = _peer(x, y, c, p)
            for k in range(n):
                pltpu.make_async_remote_copy(
                    src_ref=ins[k].at[4 * px + 2 * py + pc] if scatter else ins[k],
                    dst_ref=zones[k].at[me],
                    send_sem=send_sems.at[k * (N_DEV - 1) + p - 1],
                    recv_sem=recv_sems.at[k * (N_DEV - 1) + p - 1],
                    device_id=(px, py, pc),
                    device_id_type=pl.DeviceIdType.MESH,
                ).start()
        token[...] = jnp.zeros_like(token)

    hbm = pl.BlockSpec(memory_space=pltpu.HBM)
    sem = pl.BlockSpec(memory_space=pltpu.SEMAPHORE)
    outs = pl.pallas_call(
        body,
        name=name,
        out_shape=(pltpu.SemaphoreType.DMA((n * (N_DEV - 1),)), pltpu.SemaphoreType.DMA((n * (N_DEV - 1),)),
                   *[pltpu.HBM(a.shape, a.dtype) for a in arrays], *[pltpu.HBM(z.shape, z.dtype) for z in lands],
                   jax.ShapeDtypeStruct((8, LANES), F32)),
        in_specs=[hbm] * (2 * n) + [pl.BlockSpec(memory_space=pl.ANY)] * len(deps),
        out_specs=(sem, sem, *[hbm] * (2 * n), pl.BlockSpec(memory_space=pltpu.VMEM)),
        input_output_aliases={k: 2 + k for k in range(2 * n)},
        compiler_params=pltpu.CompilerParams(has_side_effects=pltpu.SideEffectType.DATAFLOW_SIDE_EFFECTING),
    )(*[pltpu.with_memory_space_constraint(a, pltpu.HBM) for a in arrays],
      *[pltpu.with_memory_space_constraint(z, pltpu.HBM) for z in lands], *deps)
    return outs[0], outs[1], list(outs[2:2 + n]), list(outs[2 + n:2 + 2 * n]), outs[-1]


def _exchange_wait(started, after, *, scatter, name):
    send_sems, recv_sems, srcs, lands, _ = started
    n = len(srcs)

    def body(*refs):
        ins, zones = refs[:n], refs[n:2 * n]
        s_sems, r_sems = refs[2 * n], refs[2 * n + 1]
        x, y, c = lax.axis_index("x"), lax.axis_index("y"), lax.axis_index("c")
        for p in range(1, N_DEV):
            px, py, pc = _peer(x, y, c, p)
            peer = 4 * px + 2 * py + pc
            for k in range(n):
                cp = pltpu.make_async_remote_copy(
                    src_ref=ins[k].at[peer] if scatter else ins[k],
                    dst_ref=zones[k].at[peer],
                    send_sem=s_sems.at[k * (N_DEV - 1) + p - 1],
                    recv_sem=r_sems.at[k * (N_DEV - 1) + p - 1],
                    device_id=(px, py, pc),
                    device_id_type=pl.DeviceIdType.MESH,
                )
                cp.wait_send()
                cp.wait_recv()

    hbm = pl.BlockSpec(memory_space=pltpu.HBM)
    sem = pl.BlockSpec(memory_space=pltpu.SEMAPHORE)
    outs = pl.pallas_call(
        body,
        name=name,
        out_shape=tuple(pltpu.HBM(a.shape, a.dtype) for a in srcs + lands),
        in_specs=[hbm] * (2 * n) + [sem, sem, pl.BlockSpec(memory_space=pl.ANY)],
        out_specs=tuple([hbm] * (2 * n)),
        input_output_aliases={k: k for k in range(2 * n)},
        compiler_params=pltpu.CompilerParams(has_side_effects=pltpu.SideEffectType.DATAFLOW_SIDE_EFFECTING),
    )(*srcs, *lands, send_sems, recv_sems, after)
    return list(outs[:n]), list(outs[n:])


def _mm(a, b, *, mode, out_dtype, name, add=None, tm=512, tn=512, b_rows=None, dep=None):
    rows_b = b.shape[0] if b_rows is None else b_rows
    if mode == "nn":
        (m, kd), nd = a.shape, b.shape[1]
        assert kd == rows_b
    elif mode == "nt":
        (m, kd), nd = a.shape, rows_b
    else:
        (kd, m), nd = a.shape, b.shape[1]
    tm = _pick(m, tm, LANES if mode == "tn" else 16)
    tn = _pick(nd, tn)
    dims = {"nn": NN, "nt": NT, "tn": TN}[mode]
    ni, nj = m // tm, nd // tn
    a_bytes, b_bytes = a.size * a.dtype.itemsize, b.size * b.dtype.itemsize
    i_outer = a_bytes + ni * b_bytes <= b_bytes + nj * a_bytes
    ij = (lambda g0, g1: (g0, g1)) if i_outer else (lambda g0, g1: (g1, g0))
    a_spec = (pl.BlockSpec((kd, tm), lambda g0, g1: (0, ij(g0, g1)[0])) if mode == "tn"
              else pl.BlockSpec((tm, kd), lambda g0, g1: (ij(g0, g1)[0], 0)))
    b_spec = (pl.BlockSpec((tn, kd), lambda g0, g1: (ij(g0, g1)[1], 0)) if mode == "nt"
              else pl.BlockSpec((kd, tn), lambda g0, g1: (0, ij(g0, g1)[1])))
    o_spec = pl.BlockSpec((tm, tn), lambda g0, g1: ij(g0, g1))
    has_add = add is not None

    def body(*refs):
        a_ref, b_ref = refs[0], refs[1]
        o_ref = refs[-1]
        acc = _dotb(a_ref[...], b_ref[...], dims)
        if has_add:
            acc = acc + refs[2][...].astype(F32)
        o_ref[...] = acc.astype(o_ref.dtype)

    ins = [a, b] + ([add] if has_add else []) + ([] if dep is None else [dep])
    specs = ([a_spec, b_spec] + ([o_spec] if has_add else [])
             + ([] if dep is None else [pl.BlockSpec((8, LANES), lambda g0, g1: (0, 0))]))
    return pl.pallas_call(
        body, name=name, grid=(ni, nj) if i_outer else (nj, ni), in_specs=specs, out_specs=o_spec,
        out_shape=jax.ShapeDtypeStruct((m, nd), out_dtype),
        compiler_params=_params("parallel", "parallel"),
    )(*ins)


def _mm_pair(a1, b1, a2, b2, *, out_dtype, name, b1_rows=None, tm=256):
    m, k1 = a1.shape
    k2, nd = b2.shape
    assert k1 == (b1.shape[0] if b1_rows is None else b1_rows) and a2.shape == (m, k2) and b1.shape[1] == nd
    tm = _pick(m, tm, 16)

    def body(a1_ref, b1_ref, a2_ref, b2_ref, o_ref):
        o_ref[...] = (_dotb(a1_ref[...], b1_ref[...], NN) + _dotb(a2_ref[...], b2_ref[...], NN)).astype(o_ref.dtype)

    return pl.pallas_call(
        body, name=name, grid=(m // tm,),
        in_specs=[pl.BlockSpec((tm, k1), lambda i: (i, 0)), pl.BlockSpec((k1, nd), lambda i: (0, 0)),
                  pl.BlockSpec((tm, k2), lambda i: (i, 0)), pl.BlockSpec((k2, nd), lambda i: (0, 0))],
        out_specs=pl.BlockSpec((tm, nd), lambda i: (i, 0)), out_shape=jax.ShapeDtypeStruct((m, nd), out_dtype),
        compiler_params=_params("parallel"),
    )(a1, b1, a2, b2)


def _mm_resid(a, b, x, gate, *, name, tm=256, tn=1024):
    m, kd = a.shape
    nd = b.shape[1]
    tm = _pick(m, tm, 16)
    tn = _pick(nd, tn)
    o_spec = pl.BlockSpec((tm, tn), lambda i, j: (i, j))

    def body(a_ref, b_ref, x_ref, g_ref, xo_ref, y_ref):
        y = _dotb(a_ref[...], b_ref[...], NN)
        y_ref[...] = y.astype(y_ref.dtype)
        xo_ref[...] = x_ref[...] + g_ref[...] * y

    return pl.pallas_call(
        body, name=name, grid=(m // tm, nd // tn),
        in_specs=[pl.BlockSpec((tm, kd), lambda i, j: (i, 0)), pl.BlockSpec((kd, tn), lambda i, j: (0, j)),
                  o_spec, pl.BlockSpec((1, tn), lambda i, j: (0, j))],
        out_specs=(o_spec, o_spec),
        out_shape=(jax.ShapeDtypeStruct((m, nd), F32), jax.ShapeDtypeStruct((m, nd), BF16)),
        compiler_params=_params("parallel", "parallel"),
    )(a, b, x, gate)


ROWS = 256


def _row_spec(width, rows=ROWS):
    return pl.BlockSpec((rows, width), lambda i: (i, 0))


def _const_spec(shape):
    return pl.BlockSpec(shape, lambda i: tuple(0 for _ in shape))


def _adaln_fwd(x, g, scale, shift, *, name):
    t, d = x.shape

    def body(x_ref, g_ref, sc_ref, sh_ref, h_ref):
        xv = x_ref[...]
        r = lax.rsqrt(jnp.mean(xv * xv, axis=-1, keepdims=True) + EPS)
        h_ref[...] = (xv * r * g_ref[...] * (1.0 + sc_ref[...]) + sh_ref[...]).astype(h_ref.dtype)

    return pl.pallas_call(
        body, name=name, grid=(t // ROWS,),
        in_specs=[_row_spec(d), _const_spec((1, d)), _const_spec((1, d)), _const_spec((1, d))],
        out_specs=_row_spec(d), out_shape=jax.ShapeDtypeStruct((t, d), BF16),
        compiler_params=_params("parallel"),
    )(x, g, scale, shift)


def _adaln_bwd(x, g, scale, shift, dh, dres, dep, *, name):
    t, d = x.shape

    def body(x_ref, g_ref, sc_ref, sh_ref, dh_ref, dr_ref, dep_ref, dx_ref, st_ref):
        @pl.when(pl.program_id(0) == 0)
        def _():
            st_ref[...] = jnp.zeros_like(st_ref)

        xv = x_ref[...]
        dhv = dh_ref[...].astype(F32)
        gv = g_ref[...]
        r = lax.rsqrt(jnp.mean(xv * xv, axis=-1, keepdims=True) + EPS)
        xh = xv * r
        nv = xh * gv
        dn = dhv * (1.0 + sc_ref[...])
        dxh = dn * gv
        dx_ref[...] = dr_ref[...] + r * (dxh - xh * jnp.mean(dxh * xh, axis=-1, keepdims=True))
        st_ref[0:1, :] += jnp.sum(dn * xh, axis=0, keepdims=True)
        st_ref[1:2, :] += jnp.sum(dhv * nv, axis=0, keepdims=True)
        st_ref[2:3, :] += jnp.sum(dhv, axis=0, keepdims=True)

    return pl.pallas_call(
        body, name=name, grid=(t // ROWS,),
        in_specs=[_row_spec(d), _const_spec((1, d)), _const_spec((1, d)), _const_spec((1, d)),
                  _row_spec(d), _row_spec(d), _const_spec((8, LANES))],
        out_specs=(_row_spec(d), _const_spec((8, d))),
        out_shape=(jax.ShapeDtypeStruct((t, d), F32), jax.ShapeDtypeStruct((8, d), F32)),
        compiler_params=_params("arbitrary"),
    )(x, g, scale, shift, dh, dres, dep)


def _adaln_gate_bwd(x, g, scale, shift, dh, dres, dep, y_up, gate_up, *, name):
    t, d = x.shape

    def body(x_ref, g_ref, sc_ref, sh_ref, dh_ref, dr_ref, dep_ref, y_ref, gu_ref, dx_ref, st_ref, dy_ref):
        @pl.when(pl.program_id(0) == 0)
        def _():
            st_ref[...] = jnp.zeros_like(st_ref)

        xv = x_ref[...]
        dhv = dh_ref[...].astype(F32)
        gv = g_ref[...]
        r = lax.rsqrt(jnp.mean(xv * xv, axis=-1, keepdims=True) + EPS)
        xh = xv * r
        nv = xh * gv
        dn = dhv * (1.0 + sc_ref[...])
        dxh = dn * gv
        dx = dr_ref[...] + r * (dxh - xh * jnp.mean(dxh * xh, axis=-1, keepdims=True))
        dx_ref[...] = dx
        dy_ref[...] = (dx * gu_ref[...]).astype(dy_ref.dtype)
        st_ref[0:1, :] += jnp.sum(dn * xh, axis=0, keepdims=True)
        st_ref[1:2, :] += jnp.sum(dhv * nv, axis=0, keepdims=True)
        st_ref[2:3, :] += jnp.sum(dhv, axis=0, keepdims=True)
        st_ref[3:4, :] += jnp.sum(dx * y_ref[...].astype(F32), axis=0, keepdims=True)

    return pl.pallas_call(
        body, name=name, grid=(t // ROWS,),
        in_specs=[_row_spec(d), _const_spec((1, d)), _const_spec((1, d)), _const_spec((1, d)),
                  _row_spec(d), _row_spec(d), _const_spec((8, LANES)), _row_spec(d), _const_spec((1, d))],
        out_specs=(_row_spec(d), _const_spec((8, d)), _row_spec(d)),
        out_shape=(jax.ShapeDtypeStruct((t, d), F32), jax.ShapeDtypeStruct((8, d), F32),
                   jax.ShapeDtypeStruct((t, d), BF16)),
        compiler_params=_params("arbitrary"),
    )(x, g, scale, shift, dh, dres, dep, y_up, gate_up)


def _gate_bwd(dxo, y, gate, dep, *, name):
    t, d = dxo.shape

    def body(dx_ref, y_ref, g_ref, dep_ref, dy_ref, st_ref):
        @pl.when(pl.program_id(0) == 0)
        def _():
            st_ref[...] = jnp.zeros_like(st_ref)

        dxv = dx_ref[...]
        dy_ref[...] = (dxv * g_ref[...]).astype(dy_ref.dtype)
        st_ref[0:1, :] += jnp.sum(dxv * y_ref[...], axis=0, keepdims=True)

    return pl.pallas_call(
        body, name=name, grid=(t // ROWS,),
        in_specs=[_row_spec(d), _row_spec(d), _const_spec((1, d)), _const_spec((8, LANES))],
        out_specs=(_row_spec(d), _const_spec((8, d))),
        out_shape=(jax.ShapeDtypeStruct((t, d), BF16), jax.ShapeDtypeStruct((8, d), F32)),
        compiler_params=_params("arbitrary"),
    )(dxo, y, gate, dep)


def _loss_head(x, g, target, *, name):
    t, d = x.shape

    def body(x_ref, g_ref, t_ref, dx_ref, st_ref, ls_ref):
        @pl.when(pl.program_id(0) == 0)
        def _():
            st_ref[...] = jnp.zeros_like(st_ref)
            ls_ref[...] = jnp.zeros_like(ls_ref)

        xv = x_ref[...]
        gv = g_ref[...]
        r = lax.rsqrt(jnp.mean(xv * xv, axis=-1, keepdims=True) + EPS)
        xh = xv * r
        err = xh * gv - t_ref[...]
        ls_ref[...] += 0.5 * jnp.sum(jnp.mean(err * err, axis=-1, keepdims=True))
        dy = err * (1.0 / d)
        dxh = dy * gv
        dx_ref[...] = r * (dxh - xh * jnp.mean(dxh * xh, axis=-1, keepdims=True))
        st_ref[0:1, :] += jnp.sum(dy * xh, axis=0, keepdims=True)

    return pl.pallas_call(
        body, name=name, grid=(t // ROWS,),
        in_specs=[_row_spec(d), _const_spec((1, d)), _row_spec(d)],
        out_specs=(_row_spec(d), _const_spec((8, d)), _const_spec((8, LANES))),
        out_shape=(jax.ShapeDtypeStruct((t, d), F32), jax.ShapeDtypeStruct((8, d), F32),
                   jax.ShapeDtypeStruct((8, LANES), F32)),
        compiler_params=_params("arbitrary"),
    )(x, g, target)


FFN_BLOCK = D_FF // 2
FFN_ROWS = 512


def _ffn_chunks(width):
    edges = [min(width, 3 * LANES * i) for i in range(width // (3 * LANES) + 2)]
    return [slice(lo, hi) for lo, hi in zip(edges[:-1], edges[1:]) if hi > lo]


def _ffn_gu_fwd(h, wg, wu, dep, *, name):
    t, d = h.shape
    tn = FFN_BLOCK

    chunks = _ffn_chunks(tn)
    rows = _pick(t, FFN_ROWS, 16)

    def body(h_ref, wg_ref, wu_ref, dep_ref, s_ref, a_ref, b_ref):
        hv = h_ref[...]
        ab = [(_dotb(hv, wg_ref[sl, :], NT), _dotb(hv, wu_ref[sl, :], NT)) for sl in chunks]
        for sl, (a, b) in zip(chunks, ab):
            s_ref[:, sl] = (a * _sigmoid(a) * b).astype(s_ref.dtype)
            a_ref[:, sl] = a.astype(a_ref.dtype)
            b_ref[:, sl] = b.astype(b_ref.dtype)

    w_spec = pl.BlockSpec((tn, d), lambda j, i: (j, 0))
    o_spec = pl.BlockSpec((rows, tn), lambda j, i: (i, j))
    return pl.pallas_call(
        body, name=name, grid=(D_FF // tn, t // rows),
        in_specs=[pl.BlockSpec((rows, d), lambda j, i: (i, 0)), w_spec, w_spec,
                  pl.BlockSpec((8, LANES), lambda j, i: (0, 0))],
        out_specs=(o_spec, o_spec, o_spec),
        out_shape=(jax.ShapeDtypeStruct((t, D_FF), BF16),) * 3,
        compiler_params=_params("parallel", "parallel"),
    )(h, wg, wu, dep)


def _ffn_down_dx(dy, w_down, a, b, *, name):
    t, d = dy.shape
    tn = FFN_BLOCK

    chunks = _ffn_chunks(tn)
    rows = _pick(t, FFN_ROWS, 16)

    def body(dy_ref, w_ref, a_ref, b_ref, da_ref, db_ref):
        dyv = dy_ref[...]
        ds = [_dotb(dyv, w_ref[sl, :], NT) for sl in chunks]
        for sl, dsc in zip(chunks, ds):
            av = a_ref[:, sl].astype(F32)
            sg = _sigmoid(av)
            da_ref[:, sl] = (dsc * b_ref[:, sl].astype(F32) * sg * (1.0 + av * (1.0 - sg))).astype(da_ref.dtype)
            db_ref[:, sl] = (dsc * av * sg).astype(db_ref.dtype)

    o_spec = pl.BlockSpec((rows, tn), lambda j, i: (i, j))
    return pl.pallas_call(
        body, name=name, grid=(D_FF // tn, t // rows),
        in_specs=[pl.BlockSpec((rows, d), lambda j, i: (i, 0)), pl.BlockSpec((tn, d), lambda j, i: (j, 0)),
                  o_spec, o_spec],
        out_specs=(o_spec, o_spec),
        out_shape=(jax.ShapeDtypeStruct((t, D_FF), BF16),) * 2,
        compiler_params=_params("parallel", "parallel"),
    )(dy, w_down, a, b)


def _shift_rows(v, s, rows):
    if s == 0:
        return v
    return jnp.where(rows >= s, pltpu.roll(v, s, 0), 0.0)


def _unshift_rows(v, s, rows, t):
    if s == 0:
        return v
    return jnp.where(rows < t - s, pltpu.roll(v, t - s, 0), 0.0)


def _conv_taps(x, rows):
    return [_shift_rows(x, GDN_CONV - 1 - j, rows) for j in range(GDN_CONV)]


def _conv_silu(xs, w):
    z = w[0:1, :] * xs[0]
    for j in range(1, GDN_CONV):
        z = z + w[j:j + 1, :] * xs[j]
    sg = _sigmoid(z)
    return z, sg, z * sg


def _gdn_prep_fwd(proj, conv_wt, *, name):
    t = proj.shape[0]
    nh = GDN_HEADS

    hp = GDN_PREP_HEADS
    wd = hp * LANES

    def body(x_ref, w_ref, y_ref):
        j = pl.program_id(0) * hp
        rows = lax.broadcasted_iota(jnp.int32, (t, LANES), 0)
        qscale = jnp.where(j < nh, GDN_HEAD_DIM ** -0.5, 1.0)
        for i in range(hp):
            sl = slice(i * LANES, (i + 1) * LANES)
            _, _, s = _conv_silu(_conv_taps(x_ref[:, sl], rows), w_ref[:, sl])
            rs = lax.rsqrt(jnp.sum(s * s, axis=-1, keepdims=True) + EPS)
            y_ref[:, sl] = jnp.where(j < 2 * nh, s * rs * qscale, s)

    return pl.pallas_call(
        body, name=name, grid=(3 * nh // hp,),
        in_specs=[pl.BlockSpec((t, wd), lambda j: (0, j)), pl.BlockSpec((GDN_CONV, wd), lambda j: (0, j))],
        out_specs=pl.BlockSpec((t, wd), lambda j: (0, j)),
        out_shape=jax.ShapeDtypeStruct((t, 3 * GDN_KEY_DIM), F32),
        compiler_params=_params("parallel"),
    )(proj, conv_wt)


def _gdn_prep_bwd(proj, conv_wt, dy, *, name):
    t = proj.shape[0]
    nh = GDN_HEADS

    hp = GDN_PREP_HEADS
    wd = hp * LANES
    per_seg = nh // hp

    def body(x_ref, w_ref, dy_ref, dx_ref, dw_ref):
        j = pl.program_id(0) * hp
        rows = lax.broadcasted_iota(jnp.int32, (t, LANES), 0)
        qscale = jnp.where(j < nh, GDN_HEAD_DIM ** -0.5, 1.0)
        for i in range(hp):
            sl = slice(i * LANES, (i + 1) * LANES)
            w = w_ref[:, sl]
            xs = _conv_taps(x_ref[:, sl], rows)
            z, sg, s = _conv_silu(xs, w)
            rs = lax.rsqrt(jnp.sum(s * s, axis=-1, keepdims=True) + EPS)
            dyv = dy_ref[:, sl]
            nv = s * rs
            de = dyv * qscale
            ds_qk = rs * (de - nv * jnp.sum(de * nv, axis=-1, keepdims=True))
            ds = jnp.where(j < 2 * nh, ds_qk, dyv)
            dz = ds * sg * (1.0 + z * (1.0 - sg))
            dx = w[GDN_CONV - 1:GDN_CONV, :] * dz
            dw_ref[GDN_CONV - 1:GDN_CONV, sl] = jnp.sum(dz * xs[GDN_CONV - 1], axis=0, keepdims=True)
            for k in range(GDN_CONV - 1):
                dx = dx + w[k:k + 1, :] * _unshift_rows(dz, GDN_CONV - 1 - k, rows, t)
                dw_ref[k:k + 1, sl] = jnp.sum(dz * xs[k], axis=0, keepdims=True)
            dx_ref[:, sl] = dx.astype(dx_ref.dtype)

    return pl.pallas_call(
        body, name=name, grid=(3 * nh // hp,),
        in_specs=[pl.BlockSpec((t, wd), lambda j: (0, j)), pl.BlockSpec((GDN_CONV, wd), lambda j: (0, j)),
                  pl.BlockSpec((None, t, wd), lambda j: (j // per_seg, 0, j % per_seg))],
        out_specs=(pl.BlockSpec((t, wd), lambda j: (0, j)), pl.BlockSpec((GDN_CONV, wd), lambda j: (0, j))),
        out_shape=(jax.ShapeDtypeStruct((t, 3 * GDN_KEY_DIM), BF16),
                   jax.ShapeDtypeStruct((GDN_CONV, 3 * GDN_KEY_DIM), F32)),
        compiler_params=_params("parallel"),
    )(proj, conv_wt, dy)


def _softplus(z):
    return jnp.maximum(z, 0.0) + jnp.log(1.0 + jnp.exp(-jnp.abs(z)))


def _gdn_gate_fwd(ab, prm, *, name):
    t = ab.shape[0]

    def body(ab_ref, p_ref, o_ref):
        v = ab_ref[...]
        lane = lax.broadcasted_iota(jnp.int32, v.shape, 1)
        g = -jnp.exp(p_ref[0:1, :]) * _softplus(v + p_ref[1:2, :])
        o_ref[...] = jnp.where(lane < GDN_HEADS, g, jnp.where(lane < 2 * GDN_HEADS, _sigmoid(v), 0.0))

    return pl.pallas_call(
        body, name=name, grid=(t // ROWS,),
        in_specs=[_row_spec(LANES), _const_spec((8, LANES))], out_specs=_row_spec(LANES),
        out_shape=jax.ShapeDtypeStruct((t, LANES), F32), compiler_params=_params("parallel"),
    )(ab, prm)


def _gdn_gate_bwd(ab, prm, dgb, *, name):
    t = ab.shape[0]

    def body(ab_ref, p_ref, d_ref, o_ref, st_ref):
        @pl.when(pl.program_id(0) == 0)
        def _():
            st_ref[...] = jnp.zeros_like(st_ref)

        v = ab_ref[...]
        dv = d_ref[...]
        lane = lax.broadcasted_iota(jnp.int32, v.shape, 1)
        is_a = lane < GDN_HEADS
        is_b = jnp.logical_and(lane >= GDN_HEADS, lane < 2 * GDN_HEADS)
        a_exp = jnp.exp(p_ref[0:1, :])
        zz = v + p_ref[1:2, :]
        g = -a_exp * _softplus(zz)
        da = dv * (-a_exp) * _sigmoid(zz)
        beta = _sigmoid(v)
        db = dv * beta * (1.0 - beta)
        o_ref[...] = jnp.where(is_a, da, jnp.where(is_b, db, 0.0)).astype(o_ref.dtype)
        st_ref[0:1, :] += jnp.sum(jnp.where(is_a, dv * g, 0.0), axis=0, keepdims=True)
        st_ref[1:2, :] += jnp.sum(jnp.where(is_a, da, 0.0), axis=0, keepdims=True)

    return pl.pallas_call(
        body, name=name, grid=(t // ROWS,),
        in_specs=[_row_spec(LANES), _const_spec((8, LANES)), _row_spec(LANES)],
        out_specs=(_row_spec(LANES), _const_spec((8, LANES))),
        out_shape=(jax.ShapeDtypeStruct((t, LANES), BF16), jax.ShapeDtypeStruct((8, LANES), F32)),
        compiler_params=_params("arbitrary"),
    )(ab, prm, dgb)


def _gdn_local(qs, ks, vs, gbs, bbs, tinvs=None):
    nh = len(qs)
    cs = qs[0].shape[0]
    hs = range(nh)
    r = lax.broadcasted_iota(jnp.int32, (cs, cs), 0)
    c = lax.broadcasted_iota(jnp.int32, (cs, cs), 1)
    tril, strict, eye = r >= c, r > c, r == c
    ident = jnp.where(eye, 1.0, 0.0)
    g_colb = [gbs[h][:, :cs] for h in hs]
    g_row = [jnp.sum(jnp.where(eye, g_colb[h], 0.0), axis=0, keepdims=True) for h in hs]
    gc_col = [jnp.sum(jnp.where(tril, g_row[h], 0.0), axis=1, keepdims=True) for h in hs]
    gc_row = [jnp.sum(jnp.where(r <= c, g_colb[h], 0.0), axis=0, keepdims=True) for h in hs]
    decay = [jnp.exp(jnp.where(tril, gc_col[h] - gc_row[h], NEG)) for h in hs]
    gamma = [jnp.exp(gc_col[h]) for h in hs]
    gcl = [gc_col[h][cs - 1:cs, :] for h in hs]
    gl = [jnp.exp(gcl[h]) for h in hs]
    kdec = [jnp.exp(gcl[h] - gc_col[h]) for h in hs]
    kb = [ks[h] * bbs[h] for h in hs]
    kk = [_dotb(kb[h], ks[h], NT) for h in hs]
    qk = [_dotb(qs[h], ks[h], NT) for h in hs]
    lmat = [jnp.where(strict, kk[h] * decay[h], 0.0) for h in hs]
    pmat = [jnp.where(tril, qk[h] * decay[h], 0.0) for h in hs]
    if tinvs is None:
        xm = [-lmat[h] for h in hs]
        tinv = [ident + xm[h] for h in hs]
        for _ in range(int(math.log2(cs)) - 1):
            xm = [_dotf(xm[h], xm[h], NN) for h in hs]
            tinv = [tinv[h] + _dotf(tinv[h], xm[h], NN) for h in hs]
    else:
        tinv = tinvs
    vb = [vs[h] * bbs[h] for h in hs]
    kg = [kb[h] * gamma[h] for h in hs]
    u = [_dotf(tinv[h], vb[h], NN) for h in hs]
    w = [_dotf(tinv[h], kg[h], NN) for h in hs]
    return [dict(tril=tril, strict=strict, eye=eye, r=r, c=c, decay=decay[h], gamma=gamma[h], gl=gl[h], kdec=kdec[h],
                 kb=kb[h], lmat=lmat[h], tinv=tinv[h], vb=vb[h], kg=kg[h], u=u[h], w=w[h], pmat=pmat[h],
                 qd=qs[h] * gamma[h], kd=ks[h] * kdec[h]) for h in hs]


def _head_columns(gbeta, cs):
    gbs = [jnp.broadcast_to(gbeta[:, h:h + 1], (cs, LANES)) for h in range(GDN_HEADS)]
    bbs = [jnp.broadcast_to(gbeta[:, GDN_HEADS + h:GDN_HEADS + h + 1], (cs, LANES)) for h in range(GDN_HEADS)]
    return gbs, bbs


def _gdn_chunk_fwd(qkv, gbeta, *, name):
    t = qkv.shape[0]
    nh, cs, hd = GDN_HEADS, GDN_CHUNK, GDN_HEAD_DIM
    nc = t // cs

    hb = GDN_HEAD_BATCH
    ng = nh // hb
    assert ng == 1

    def body(q_ref, k_ref, v_ref, gb_ref, o_ref, st_ref, ti_ref, s_ref):
        @pl.when(pl.program_id(1) == 0)
        def _():
            s_ref[...] = jnp.zeros_like(s_ref)

        sls = [slice(i * hd, (i + 1) * hd) for i in range(hb)]
        hs = range(hb)
        s = [s_ref[i] for i in hs]
        gbs, bbs = _head_columns(gb_ref[...], cs)
        lo = _gdn_local([q_ref[:, sl] for sl in sls], [k_ref[:, sl] for sl in sls], [v_ref[:, sl] for sl in sls],
                        gbs, bbs)
        ws = [_dotb(lo[i]["w"], s[i], NN) for i in hs]
        qs = [_dotb(lo[i]["qd"], s[i], NN) for i in hs]
        vn = [lo[i]["u"] - ws[i] for i in hs]
        pv = [_dotb(lo[i]["pmat"], vn[i], NN) for i in hs]
        kv = [_dotb(lo[i]["kd"], vn[i], TN) for i in hs]
        for i, sl in enumerate(sls):
            st_ref[i, 0] = s[i]
            ti_ref[i, 0] = lo[i]["tinv"]
            o_ref[:, sl] = qs[i] + pv[i]
            s_ref[i] = s[i] * lo[i]["gl"] + kv[i]

    col = lambda off: pl.BlockSpec((cs, hb * hd), lambda h, n: (n, off + h))
    return pl.pallas_call(
        body, name=name, grid=(ng, nc),
        in_specs=[col(0), col(ng), col(2 * ng), pl.BlockSpec((cs, LANES), lambda h, n: (n, 0))],
        out_specs=(col(0), pl.BlockSpec((hb, 1, hd, hd), lambda h, n: (h, n, 0, 0)),
                   pl.BlockSpec((hb, 1, cs, cs), lambda h, n: (h, n, 0, 0))),
        out_shape=(jax.ShapeDtypeStruct((t, nh * hd), F32), jax.ShapeDtypeStruct((nh, nc, hd, hd), F32),
                   jax.ShapeDtypeStruct((nh, nc, cs, cs), F32)),
        scratch_shapes=[pltpu.VMEM((hb, hd, hd), F32)],
        compiler_params=_params("parallel", "arbitrary"),
    )(qkv, qkv, qkv, gbeta)


def _gdn_chunk_bwd(qkv, gbeta, states, tinvs, do, *, name):
    t = qkv.shape[0]
    nh, cs, hd = GDN_HEADS, GDN_CHUNK, GDN_HEAD_DIM
    nc = t // cs

    hb = GDN_HEAD_BATCH
    ng = nh // hb
    assert ng == 1

    def heads_bwd(q, k, v, gb, bb, s, ti, dsn, dov):
        hs = range(len(q))
        lo = _gdn_local(q, k, v, gb, bb, ti)
        tril, strict, eye, r, c = lo[0]["tril"], lo[0]["strict"], lo[0]["eye"], lo[0]["r"], lo[0]["c"]
        rowi = lax.broadcasted_iota(jnp.int32, (cs, 1), 0)
        get = lambda name: [lo[h][name] for h in hs]
        decay, gamma, gl, kdec = get("decay"), get("gamma"), get("gl"), get("kdec")
        kb, tinv, w, pmat, kd, qd = get("kb"), get("tinv"), get("w"), get("pmat"), get("kd"), get("qd")
        ws = [_dotb(w[h], s[h], NN) for h in hs]
        pdo = [_dotb(pmat[h], dov[h], TN) for h in hs]
        kds = [_dotb(kd[h], dsn[h], NN) for h in hs]
        dqd = [_dotb(dov[h], s[h], NT) for h in hs]
        qdo = [_dotb(qd[h], dov[h], TN) for h in hs]
        vn = [lo[h]["u"] - ws[h] for h in hs]
        dvn = [pdo[h] + kds[h] for h in hs]
        dp = [jnp.where(tril, _dotb(dov[h], vn[h], NT), 0.0) for h in hs]
        dkd = [_dotb(vn[h], dsn[h], NT) for h in hs]
        dw = [-_dotb(dvn[h], s[h], NT) for h in hs]
        wdv = [_dotb(w[h], dvn[h], TN) for h in hs]
        dvb = [_dotf(tinv[h], dvn[h], TN) for h in hs]
        dt1 = [_dotf(dvn[h], lo[h]["vb"], NT) for h in hs]
        dkg = [_dotf(tinv[h], dw[h], TN) for h in hs]
        dt2 = [_dotf(dw[h], lo[h]["kg"], NT) for h in hs]
        tdt = [_dotf(tinv[h], dt1[h] + dt2[h], TN) for h in hs]
        dl = [jnp.where(strict, -_dotf(tdt[h], tinv[h], NT), 0.0) for h in hs]
        dkk = [dl[h] * decay[h] for h in hs]
        dqk = [dp[h] * decay[h] for h in hs]
        dkb = [_dotb(dkk[h], k[h], NN) + dkg[h] * gamma[h] for h in hs]
        dk1 = [_dotb(dkk[h], kb[h], TN) for h in hs]
        dk2 = [_dotb(dqk[h], q[h], TN) for h in hs]
        dq1 = [_dotb(dqk[h], k[h], NN) for h in hs]
        out = []
        for h in hs:
            dgl = jnp.sum(jnp.sum(dsn[h] * s[h], axis=1, keepdims=True), axis=0, keepdims=True)
            ds_prev = gl[h] * dsn[h] + qdo[h] - wdv[h]
            dk = dk1[h] + dk2[h] + dkd[h] * kdec[h] + dkb[h] * bb[h]
            dq = dq1[h] + dqd[h] * gamma[h]
            dbeta = jnp.sum(dvb[h] * v[h], axis=-1, keepdims=True) + jnp.sum(dkb[h] * k[h], axis=-1, keepdims=True)
            e = dl[h] * lo[h]["lmat"] + dp[h] * pmat[h]
            e_col = jnp.sum(e, axis=0, keepdims=True)
            dgc = jnp.sum(e, axis=1, keepdims=True) - jnp.sum(jnp.where(eye, e_col, 0.0), axis=1, keepdims=True)
            dgamma = (jnp.sum(dqd[h] * q[h], axis=-1, keepdims=True)
                      + jnp.sum(dkg[h] * kb[h], axis=-1, keepdims=True))
            rk = jnp.sum(dkd[h] * k[h], axis=-1, keepdims=True) * kdec[h]
            dgcl = jnp.sum(rk, axis=0, keepdims=True) + dgl * gl[h]
            dgc = dgc + dgamma * gamma[h] - rk + jnp.where(rowi == cs - 1, dgcl, 0.0)
            dgc_row = jnp.sum(jnp.where(eye, dgc, 0.0), axis=0, keepdims=True)
            dg = jnp.sum(jnp.where(c >= r, dgc_row, 0.0), axis=1, keepdims=True)
            out.append((dq, dk, dvb[h] * bb[h], dbeta, dg, ds_prev))
        return out

    def body(q_ref, k_ref, v_ref, gb_ref, st_ref, ti_ref, do_ref, d_ref, dgb_ref, ds_ref):
        @pl.when(pl.program_id(1) == 0)
        def _():
            ds_ref[...] = jnp.zeros_like(ds_ref)

        sls = [slice(i * hd, (i + 1) * hd) for i in range(hb)]
        hs = range(hb)
        gbs, bbs = _head_columns(gb_ref[...], cs)
        outs = heads_bwd([q_ref[:, sl] for sl in sls], [k_ref[:, sl] for sl in sls], [v_ref[:, sl] for sl in sls],
                         gbs, bbs, [st_ref[i, 0] for i in hs],
                         [ti_ref[i, 0] for i in hs], [ds_ref[i] for i in hs], [do_ref[:, sl] for sl in sls])
        lane = lax.broadcasted_iota(jnp.int32, (cs, LANES), 1)
        dgb = jnp.zeros((cs, LANES), F32)
        for i, sl in enumerate(sls):
            dq, dk, dv, dbeta, dg, ds_prev = outs[i]
            d_ref[0, :, sl], d_ref[1, :, sl], d_ref[2, :, sl] = dq, dk, dv
            dgb = jnp.where(lane == i, dg, jnp.where(lane == nh + i, dbeta, dgb))
            ds_ref[i] = ds_prev
        dgb_ref[...] = dgb

    col = lambda off: pl.BlockSpec((cs, hb * hd), lambda h, n: (nc - 1 - n, off + h))
    gspec = pl.BlockSpec((cs, LANES), lambda h, n: (nc - 1 - n, 0))
    return pl.pallas_call(
        body, name=name, grid=(ng, nc),
        in_specs=[col(0), col(ng), col(2 * ng), gspec,
                  pl.BlockSpec((hb, 1, hd, hd), lambda h, n: (h, nc - 1 - n, 0, 0)),
                  pl.BlockSpec((hb, 1, cs, cs), lambda h, n: (h, nc - 1 - n, 0, 0)), col(0)],
        out_specs=(pl.BlockSpec((3, cs, hb * hd), lambda h, n: (0, nc - 1 - n, h)), gspec),
        out_shape=(jax.ShapeDtypeStruct((3, t, nh * hd), F32), jax.ShapeDtypeStruct((t, LANES), F32)),
        scratch_shapes=[pltpu.VMEM((hb, hd, hd), F32)],
        compiler_params=_params("parallel", "arbitrary"),
    )(qkv, qkv, qkv, gbeta, states, tinvs, do)


def _gdn_onorm_fwd(o, proj, norm_g, *, name):
    t = o.shape[0]
    w = GDN_KEY_DIM
    goff = 3 * GDN_KEY_DIM // w

    def body(o_ref, gp_ref, g_ref, y_ref):
        gv = g_ref[...]
        for h in range(GDN_HEADS):
            sl = slice(h * GDN_HEAD_DIM, (h + 1) * GDN_HEAD_DIM)
            oh = o_ref[:, sl]
            gp = gp_ref[:, sl]
            r = lax.rsqrt(jnp.mean(oh * oh, axis=-1, keepdims=True) + EPS)
            y_ref[:, sl] = (oh * r * gv * gp * _sigmoid(gp)).astype(y_ref.dtype)

    return pl.pallas_call(
        body, name=name, grid=(t // ROWS,),
        in_specs=[_row_spec(w), pl.BlockSpec((ROWS, w), lambda i: (i, goff)), _const_spec((1, GDN_HEAD_DIM))],
        out_specs=_row_spec(w), out_shape=jax.ShapeDtypeStruct((t, w), BF16),
        compiler_params=_params("parallel"),
    )(o, proj, norm_g)


def _gdn_onorm_bwd(o, proj, norm_g, dy, *, name):
    t = o.shape[0]
    w = GDN_KEY_DIM
    goff = 3 * GDN_KEY_DIM // w

    def body(o_ref, gp_ref, g_ref, dy_ref, do_ref, dgp_ref, st_ref):
        @pl.when(pl.program_id(0) == 0)
        def _():
            st_ref[...] = jnp.zeros_like(st_ref)

        gv = g_ref[...]
        acc = jnp.zeros((1, GDN_HEAD_DIM), F32)
        for h in range(GDN_HEADS):
            sl = slice(h * GDN_HEAD_DIM, (h + 1) * GDN_HEAD_DIM)
            oh = o_ref[:, sl]
            gp = gp_ref[:, sl]
            dyv = dy_ref[:, sl].astype(F32)
            r = lax.rsqrt(jnp.mean(oh * oh, axis=-1, keepdims=True) + EPS)
            xh = oh * r
            sg = _sigmoid(gp)
            dn = dyv * gp * sg
            dgp_ref[:, sl] = (dyv * xh * gv * sg * (1.0 + gp * (1.0 - sg))).astype(dgp_ref.dtype)
            acc = acc + jnp.sum(dn * xh, axis=0, keepdims=True)
            dxh = dn * gv
            do_ref[:, sl] = r * (dxh - xh * jnp.mean(dxh * xh, axis=-1, keepdims=True))
        st_ref[0:1, :] += acc

    return pl.pallas_call(
        body, name=name, grid=(t // ROWS,),
        in_specs=[_row_spec(w), pl.BlockSpec((ROWS, w), lambda i: (i, goff)), _const_spec((1, GDN_HEAD_DIM)),
                  _row_spec(w)],
        out_specs=(_row_spec(w), _row_spec(w), _const_spec((8, GDN_HEAD_DIM))),
        out_shape=(jax.ShapeDtypeStruct((t, w), F32), jax.ShapeDtypeStruct((t, w), BF16),
                   jax.ShapeDtypeStruct((8, GDN_HEAD_DIM), F32)),
        compiler_params=_params("arbitrary"),
    )(o, proj, norm_g, dy)


def _mla_prep_fwd(proj, qg, kvg, *, name):
    t = proj.shape[0]
    q1, k1 = MLA_Q_RANK, MLA_Q_RANK + MLA_KV_RANK

    def body(p_ref, qg_ref, kg_ref, cq_ref, ck_ref):
        cq = p_ref[:, 0:q1]
        ck = p_ref[:, q1:k1]
        cq_ref[...] = (cq * lax.rsqrt(jnp.mean(cq * cq, axis=-1, keepdims=True) + EPS) * qg_ref[...]).astype(BF16)
        ck_ref[...] = (ck * lax.rsqrt(jnp.mean(ck * ck, axis=-1, keepdims=True) + EPS) * kg_ref[...]).astype(BF16)

    return pl.pallas_call(
        body, name=name, grid=(t // ROWS,),
        in_specs=[_row_spec(MLA_IN), _const_spec((1, MLA_Q_RANK)), _const_spec((1, MLA_KV_RANK))],
        out_specs=(_row_spec(MLA_Q_RANK), _row_spec(MLA_KV_RANK)),
        out_shape=(jax.ShapeDtypeStruct((t, MLA_Q_RANK), BF16), jax.ShapeDtypeStruct((t, MLA_KV_RANK), BF16)),
        compiler_params=_params("parallel"),
    )(proj, qg, kvg)


def _mla_prep_bwd(proj, qg, kvg, dcq, dck, dkr, *, name):
    t = proj.shape[0]
    q1, k1 = MLA_Q_RANK, MLA_Q_RANK + MLA_KV_RANK

    def body(p_ref, qg_ref, kg_ref, dq_ref, dk_ref, dr_ref, dp_ref, st_ref):
        @pl.when(pl.program_id(0) == 0)
        def _():
            st_ref[...] = jnp.zeros_like(st_ref)

        for lo, hi, g_ref, d_ref in ((0, q1, qg_ref, dq_ref), (q1, k1, kg_ref, dk_ref)):
            xv = p_ref[:, lo:hi]
            dn = d_ref[...]
            r = lax.rsqrt(jnp.mean(xv * xv, axis=-1, keepdims=True) + EPS)
            xh = xv * r
            dxh = dn * g_ref[...]
            dp_ref[:, lo:hi] = (r * (dxh - xh * jnp.mean(dxh * xh, axis=-1, keepdims=True))).astype(dp_ref.dtype)
            st_ref[0:1, lo:hi] += jnp.sum(dn * xh, axis=0, keepdims=True)
        dp_ref[:, k1:MLA_IN] = dr_ref[:, 0:MLA_ROPE].astype(dp_ref.dtype)

    return pl.pallas_call(
        body, name=name, grid=(t // ROWS,),
        in_specs=[_row_spec(MLA_IN), _const_spec((1, MLA_Q_RANK)), _const_spec((1, MLA_KV_RANK)),
                  _row_spec(MLA_Q_RANK), _row_spec(MLA_KV_RANK), _row_spec(LANES)],
        out_specs=(_row_spec(MLA_IN), _const_spec((8, MLA_IN))),
        out_shape=(jax.ShapeDtypeStruct((t, MLA_IN), BF16), jax.ShapeDtypeStruct((8, MLA_IN), F32)),
        compiler_params=_params("arbitrary"),
    )(proj, qg, kvg, dcq, dck, dkr)


ATT_BLOCK = 256
ATT_HEAD_BATCH = 8
ATT_HEAD_BATCH_BWD = 4
ATT_SCALE = MLA_QK ** -0.5


def _diagonal_mask(blk):
    return lax.broadcasted_iota(jnp.int32, (blk, blk), 1) <= lax.broadcasted_iota(jnp.int32, (blk, blk), 0)


def _swap_halves(xv, first):
    return jnp.where(first, pltpu.roll(xv, LANES - MLA_ROPE // 2, 1), pltpu.roll(xv, MLA_ROPE // 2, 1))


def _rope_qk(qf, proj, cos_t, sin_t, *, name):
    t = qf.shape[0]
    nrope = MLA_HEADS * MLA_ROPE
    q_blk = MLA_HEADS * MLA_NOPE // nrope
    k_blk = (MLA_Q_RANK + MLA_KV_RANK) // LANES

    def body(q_ref, p_ref, c_ref, s_ref, qo_ref, ko_ref):
        cv, sv = c_ref[...], s_ref[...]
        lane = lax.broadcasted_iota(jnp.int32, (ROWS, LANES), 1)
        first = (lane % MLA_ROPE) < (MLA_ROPE // 2)
        for i in range(nrope // LANES):
            sl = slice(i * LANES, (i + 1) * LANES)
            xv = q_ref[:, sl].astype(F32)
            qo_ref[:, sl] = (xv * cv + _swap_halves(xv, first) * sv).astype(qo_ref.dtype)
        kv = jnp.where(lane < MLA_ROPE, p_ref[...], 0.0)
        ko_ref[...] = (kv * cv + _swap_halves(kv, first) * sv).astype(ko_ref.dtype)

    return pl.pallas_call(
        body, name=name, grid=(t // ROWS,),
        in_specs=[pl.BlockSpec((ROWS, nrope), lambda i: (i, q_blk)), pl.BlockSpec((ROWS, LANES), lambda i: (i, k_blk)),
                  _row_spec(LANES), _row_spec(LANES)],
        out_specs=(_row_spec(nrope), _row_spec(LANES)),
        out_shape=(jax.ShapeDtypeStruct((t, nrope), BF16), jax.ShapeDtypeStruct((t, LANES), BF16)),
        compiler_params=_params("parallel"),
    )(qf, proj, cos_t, sin_t)


def _rope_qk_bwd(dqr, dkr_parts, cos_t, sin_t, *, name):
    t, nrope = dqr.shape
    ng = dkr_parts.shape[0]

    def body(d_ref, k_ref, c_ref, s_ref, qo_ref, ko_ref):
        cv, sv = c_ref[...], s_ref[...]
        lane = lax.broadcasted_iota(jnp.int32, (ROWS, LANES), 1)
        first = (lane % MLA_ROPE) < (MLA_ROPE // 2)
        for i in range(nrope // LANES):
            sl = slice(i * LANES, (i + 1) * LANES)
            dv = d_ref[:, sl]
            qo_ref[:, sl] = (dv * cv + _swap_halves(dv * sv, first)).astype(qo_ref.dtype)
        dk = k_ref[0]
        for g in range(1, ng):
            dk = dk + k_ref[g]
        dk = jnp.where(lane < MLA_ROPE, dk, 0.0)
        ko_ref[...] = jnp.where(lane < MLA_ROPE, dk * cv + _swap_halves(dk * sv, first), 0.0)

    return pl.pallas_call(
        body, name=name, grid=(t // ROWS,),
        in_specs=[_row_spec(nrope), pl.BlockSpec((ng, ROWS, LANES), lambda i: (0, i, 0)), _row_spec(LANES),
                  _row_spec(LANES)],
        out_specs=(_row_spec(nrope), _row_spec(LANES)),
        out_shape=(jax.ShapeDtypeStruct((t, nrope), BF16), jax.ShapeDtypeStruct((t, LANES), F32)),
        compiler_params=_params("parallel"),
    )(dqr, dkr_parts, cos_t, sin_t)


def _attn_tm_fwd(qf, qr, kvf, kr, *, name):
    t = qf.shape[0]
    nh, dn, dr, dv = MLA_HEADS, MLA_NOPE, MLA_ROPE, MLA_V
    blk = min(ATT_BLOCK, t)
    hb = ATT_HEAD_BATCH
    hs = range(hb)

    def body(q_ref, qr_ref, kv_ref, kr_ref, o_ref, l_ref):
        i = pl.program_id(1)
        qc = [jnp.concatenate([q_ref[:, h * dn:(h + 1) * dn].astype(MXU_DTYPE), qr_ref[:, h * dr:(h + 1) * dr]], axis=1)
              for h in hs]

        def step(j, carry, diagonal=False):
            m, l, acc = carry[:hb], carry[hb:2 * hb], carry[2 * hb:]
            rows = pl.ds(pl.multiple_of(j * blk, blk), blk)
            krj = kr_ref[rows, 0:dr]
            s = [_dotb(qc[h], jnp.concatenate([kv_ref[rows, h * (dn + dv):h * (dn + dv) + dn], krj], axis=1), NT)
                 for h in hs]
            s = [s[h] * ATT_SCALE for h in hs]
            if diagonal:
                mask = _diagonal_mask(blk)
                s = [jnp.where(mask, s[h], NEG) for h in hs]
            m_new = [jnp.maximum(m[h], jnp.max(s[h], axis=-1, keepdims=True)) for h in hs]
            p = [jnp.exp(s[h] - m_new[h]) for h in hs]
            pv = [_dotb(p[h], kv_ref[rows, h * (dn + dv) + dn:(h + 1) * (dn + dv)], NN) for h in hs]
            alpha = [jnp.exp(m[h] - m_new[h]) for h in hs]
            l = [alpha[h] * l[h] + jnp.sum(p[h], axis=-1, keepdims=True) for h in hs]
            acc = [alpha[h] * acc[h] + pv[h] for h in hs]
            return tuple(m_new) + tuple(l) + tuple(acc)

        init = ((jnp.full((blk, 1), NEG, F32),) * hb + (jnp.zeros((blk, 1), F32),) * hb
                + (jnp.zeros((blk, dv), F32),) * hb)
        out = step(i, lax.fori_loop(0, i, step, init), diagonal=True)
        for h in hs:
            m, l, acc = out[h], out[hb + h], out[2 * hb + h]
            o_ref[:, h * dv:(h + 1) * dv] = (acc / l).astype(o_ref.dtype)
            l_ref[h] = jnp.broadcast_to(m + jnp.log(l), (blk, LANES))

    return pl.pallas_call(
        body, name=name, grid=(nh // hb, t // blk),
        in_specs=[pl.BlockSpec((blk, hb * dn), lambda g, i: (i, g)), pl.BlockSpec((blk, hb * dr), lambda g, i: (i, g)),
                  pl.BlockSpec((t, hb * (dn + dv)), lambda g, i: (0, g)), pl.BlockSpec((t, LANES), lambda g, i: (0, 0))],
        out_specs=(pl.BlockSpec((blk, hb * dv), lambda g, i: (i, g)),
                   pl.BlockSpec((hb, blk, LANES), lambda g, i: (g, i, 0))),
        out_shape=(jax.ShapeDtypeStruct((t, nh * dv), BF16), jax.ShapeDtypeStruct((nh, t, LANES), F32)),
        compiler_params=_params("parallel", "parallel"),
    )(qf, qr, kvf, kr)


def _attn_tm_bwd(qf, qr, kvf, kr, o, lse, do, *, name):
    t = qf.shape[0]
    nh, dn, dr, dv = MLA_HEADS, MLA_NOPE, MLA_ROPE, MLA_V
    blk = min(ATT_BLOCK, t)
    nb = t // blk
    hb = ATT_HEAD_BATCH_BWD
    hs = range(hb)
    ng = nh // hb

    def body(q_ref, qr_ref, kv_ref, kr_ref, o_ref, l_ref, do_ref, dqn_ref, dqr_ref, dkv_ref, dkr_ref):
        j = pl.program_id(1)

        @pl.when(j == 0)
        def _():
            dqn_ref[...] = jnp.zeros_like(dqn_ref)
            dqr_ref[...] = jnp.zeros_like(dqr_ref)

        krj = kr_ref[:, 0:dr]
        kc = [jnp.concatenate([kv_ref[:, h * (dn + dv):h * (dn + dv) + dn], krj], axis=1) for h in hs]
        vv = [kv_ref[:, h * (dn + dv) + dn:(h + 1) * (dn + dv)] for h in hs]

        def step(i, carry, diagonal=False):
            dkn_acc, dv_acc, dkr_acc = carry[:hb], carry[hb:2 * hb], carry[2 * hb]
            rows = pl.ds(pl.multiple_of(i * blk, blk), blk)
            qc = [jnp.concatenate([q_ref[rows, h * dn:(h + 1) * dn].astype(MXU_DTYPE),
                                   qr_ref[rows, h * dr:(h + 1) * dr]], axis=1) for h in hs]
            dov = [do_ref[rows, h * dv:(h + 1) * dv] for h in hs]
            s = [_dotb(qc[h], kc[h], NT) for h in hs]
            dp = [_dotb(dov[h], vv[h], NT) for h in hs]
            s = [s[h] * ATT_SCALE for h in hs]
            if diagonal:
                mask = _diagonal_mask(blk)
                s = [jnp.where(mask, s[h], NEG) for h in hs]
            p = [jnp.exp(s[h] - l_ref[h, rows, :][:, 0:1]) for h in hs]
            delta = [jnp.sum(dov[h].astype(F32) * o_ref[rows, h * dv:(h + 1) * dv].astype(F32), axis=-1, keepdims=True)
                     for h in hs]
            ds = [p[h] * (dp[h] - delta[h]) * ATT_SCALE for h in hs]
            dvn = [_dotb(p[h], dov[h], TN) for h in hs]
            dkc = [_dotb(ds[h], qc[h], TN) for h in hs]
            dqc = [_dotb(ds[h], kc[h], NN) for h in hs]
            for h in hs:
                dqn_ref[rows, h * dn:(h + 1) * dn] += dqc[h][:, 0:dn]
                dqr_ref[rows, h * dr:(h + 1) * dr] += dqc[h][:, dn:dn + dr]
            dkr_new = dkr_acc
            for h in hs:
                dkr_new = dkr_new + dkc[h][:, dn:dn + dr]
            return (tuple(dkn_acc[h] + dkc[h][:, 0:dn] for h in hs) + tuple(dv_acc[h] + dvn[h] for h in hs)
                    + (dkr_new,))

        init = (jnp.zeros((blk, dn), F32),) * hb + (jnp.zeros((blk, dv), F32),) * hb + (jnp.zeros((blk, dr), F32),)
        out = lax.fori_loop(j + 1, nb, step, step(j, init, diagonal=True))
        for h in hs:
            dkv_ref[:, h * (dn + dv):h * (dn + dv) + dn] = out[h].astype(dkv_ref.dtype)
            dkv_ref[:, h * (dn + dv) + dn:(h + 1) * (dn + dv)] = out[hb + h].astype(dkv_ref.dtype)
        dkr_ref[0, :, 0:dr] = out[2 * hb]
        dkr_ref[0, :, dr:LANES] = jnp.zeros((blk, LANES - dr), F32)

    full = lambda w: pl.BlockSpec((t, w), lambda g, j: (0, g))
    return pl.pallas_call(
        body, name=name, grid=(ng, nb),
        in_specs=[full(hb * dn), full(hb * dr), pl.BlockSpec((blk, hb * (dn + dv)), lambda g, j: (j, g)),
                  pl.BlockSpec((blk, LANES), lambda g, j: (j, 0)), full(hb * dv),
                  pl.BlockSpec((hb, t, LANES), lambda g, j: (g, 0, 0)), full(hb * dv)],
        out_specs=(full(hb * dn), full(hb * dr), pl.BlockSpec((blk, hb * (dn + dv)), lambda g, j: (j, g)),
                   pl.BlockSpec((1, blk, LANES), lambda g, j: (g, j, 0))),
        out_shape=(jax.ShapeDtypeStruct((t, nh * dn), F32), jax.ShapeDtypeStruct((t, nh * dr), F32),
                   jax.ShapeDtypeStruct((t, nh * (dn + dv)), BF16), jax.ShapeDtypeStruct((ng, t, LANES), F32)),
        compiler_params=_params("parallel", "arbitrary"),
    )(qf, qr, kvf, kr, o, lse, do)


def _ada_mod(c_all, ada_w, ada_b_cols, *, name):
    nl, d, wc = ada_w.shape

    def body(c_ref, w_ref, b_ref, o_ref):
        cv = c_ref[...]
        o_ref[0] = _dotb(cv * _sigmoid(cv), w_ref[0], NN) + b_ref[0]

    return pl.pallas_call(
        body, name=name, grid=(nl,),
        in_specs=[_const_spec((N_DEV, d)), pl.BlockSpec((1, d, wc), lambda l: (l, 0, 0)),
                  pl.BlockSpec((1, 1, wc), lambda l: (l, 0, 0))],
        out_specs=pl.BlockSpec((1, N_DEV, wc), lambda l: (l, 0, 0)),
        out_shape=jax.ShapeDtypeStruct((nl, N_DEV, wc), F32), compiler_params=_params("parallel"),
    )(c_all, ada_w, ada_b_cols)


def _adam_math(g, w, m, v):
    m2 = ADAM_B1 * m + (1.0 - ADAM_B1) * g
    v2 = ADAM_B2 * v + (1.0 - ADAM_B2) * (g * g)
    delta = -ADAM_LR * ((m2 / ADAM_BC1) / (jnp.sqrt(v2 / ADAM_BC2) + ADAM_EPS) + ADAM_WD * w)
    return delta, m2, v2


def _ada_grad_adamw(c_all, dmod_cols, w, m, v, *, name):
    nl, d, wc = w.shape
    tr = 256

    def body(c_ref, dm_ref, w_ref, m_ref, v_ref, g_ref, d_ref, m2_ref, v2_ref):
        cv = c_ref[...]
        g = _dotf(cv * _sigmoid(cv), dm_ref[0], TN)
        delta, m2, v2 = _adam_math(g, w_ref[0], m_ref[0], v_ref[0])
        g_ref[0], d_ref[0], m2_ref[0], v2_ref[0] = g, delta, m2, v2

    blk = pl.BlockSpec((1, tr, wc), lambda l, i: (l, i, 0))
    return pl.pallas_call(
        body, name=name, grid=(nl, d // tr),
        in_specs=[pl.BlockSpec((N_DEV, tr), lambda l, i: (0, i)), pl.BlockSpec((1, N_DEV, wc), lambda l, i: (l, 0, 0)),
                  blk, blk, blk],
        out_specs=(blk,) * 4, out_shape=(jax.ShapeDtypeStruct(w.shape, F32),) * 4,
        compiler_params=_params("parallel", "parallel"),
    )(c_all, dmod_cols, w, m, v)


def _adamw(parts, w, m, v, *, name):
    nl, r, c = w.shape
    ns = parts[0].shape[0]
    lanes_padded = -(-c // LANES) * LANES
    row_bytes = 2 * nl * ns * lanes_padded * parts[0].dtype.itemsize
    tr = _pick(r, min(256, max(16, (VMEM_LIMIT // 2) // row_bytes)), 16)
    tc = c
    if tr * row_bytes > VMEM_LIMIT // 2:
        tc = _pick(c, max(LANES, c * (VMEM_LIMIT // 2) // (tr * row_bytes)))

    def body(*refs):
        p_refs = refs[:nl]
        w_ref, m_ref, v_ref, g_ref, d_ref, m2_ref, v2_ref = refs[nl:]
        layer = pl.program_id(0)
        for q in range(nl):
            @pl.when(layer == q)
            def _(q=q):
                g = p_refs[q][0].astype(F32)
                for s in range(1, ns):
                    g = g + p_refs[q][s].astype(F32)
                delta, m2, v2 = _adam_math(g, w_ref[0], m_ref[0], v_ref[0])
                g_ref[0], d_ref[0], m2_ref[0], v2_ref[0] = g, delta, m2, v2

    blk = pl.BlockSpec((1, tr, tc), lambda l, i, j: (l, i, j))
    p_specs = [pl.BlockSpec((ns, tr, tc), lambda l, i, j, q=q: (0, jnp.where(l == q, i, 0), jnp.where(l == q, j, 0)))
               for q in range(nl)]
    return pl.pallas_call(
        body, name=name, grid=(nl, r // tr, c // tc),
        in_specs=p_specs + [blk, blk, blk],
        out_specs=(blk,) * 4, out_shape=(jax.ShapeDtypeStruct(w.shape, F32),) * 4,
        compiler_params=_params("arbitrary", "arbitrary", "arbitrary"),
    )(*parts, w, m, v)


def _sum_parts(parts, *, name):
    ns, r, c = parts.shape

    def body(p_ref, o_ref):
        acc = p_ref[0]
        for s in range(1, ns):
            acc = acc + p_ref[s]
        o_ref[...] = acc

    return pl.pallas_call(
        body, name=name, out_shape=jax.ShapeDtypeStruct((r, c), F32),
        in_specs=[pl.BlockSpec(memory_space=pltpu.VMEM)], out_specs=pl.BlockSpec(memory_space=pltpu.VMEM),
    )(parts)


def _pack(arrs):
    flat = jnp.concatenate([a.reshape(-1).astype(F32) for a in arrs])
    pad = (-flat.shape[0]) % (8 * LANES)
    return jnp.pad(flat, (0, pad)).reshape(-1, LANES)


def _unpack(packed, shapes, lead=()):
    flat = packed.reshape(lead + (-1,))
    out, off = [], 0
    for s in shapes:
        n = math.prod(s)
        out.append(flat[..., off:off + n].reshape(lead + tuple(s)))
        off += n
    return out


def _gather_rows(g):
    _, nl, rs, c = g.shape
    return jnp.transpose(g, (1, 0, 2, 3)).reshape(nl, N_DEV * rs, c)


def _row(v):
    return v.reshape(1, -1)


def _local_step(x, target, mod, cos_t, sin_t, rep, get_weights, put_grads):
    t = x.shape[0]
    saved = []
    for layer in range(DEPTH):
        j = layer // 2
        tag = f"l{layer}"
        shift_m, scale_m, gate_m, shift_f, scale_f, gate_f = [_row(mod[layer, i]) for i in range(N_MOD)]
        lw = dict(get_weights(layer, "mix", x))
        rec = {"x0": x, "lw": lw}
        h = _adaln_fwd(x, _row(rep["norm_mix_g"][layer]), scale_m, shift_m, name=f"adaln_mix_{tag}")
        rec["h"] = h
        if layer % 2 == 0:
            proj = _mm(h, lw["wt_in"], mode="nt", out_dtype=F32, tm=256, tn=GDN_MAIN, b_rows=GDN_MAIN,
                       dep=lw["dep_mix"], name=f"gdn_in_{tag}")
            ab = _mm(h, lw["wt_ab"], mode="nt", out_dtype=F32, name=f"gdn_in_ab_{tag}")
            qkv = _gdn_prep_fwd(proj, rep["gdn_conv_wt"][j], name=f"gdn_prep_{tag}")
            gbeta = _gdn_gate_fwd(ab, rep["gdn_gate_prm"][j], name=f"gdn_gate_{tag}")
            o, states, tinvs = _gdn_chunk_fwd(qkv, gbeta, name=f"gdn_chunk_{tag}")
            og = _gdn_onorm_fwd(o, proj, _row(rep["gdn_norm_g"][j]), name=f"gdn_onorm_{tag}")
            x, y = _mm_resid(og, lw["w_out"], x, gate_m, name=f"gdn_out_{tag}")
            rec.update(proj=proj, ab=ab, qkv=qkv, gbeta=gbeta, states=states, tinvs=tinvs, o=o, og=og, y=y)
        else:
            proj = _mm(h, lw["w_in"], mode="nn", out_dtype=F32, dep=lw["dep_mix"], name=f"mla_in_{tag}")
            cq, ck = _mla_prep_fwd(proj, _row(rep["mla_q_norm_g"][j]), _row(rep["mla_kv_norm_g"][j]),
                                   name=f"mla_prep_{tag}")
            qf = _mm(cq, lw["wt_uq"], mode="nt", out_dtype=BF16, name=f"mla_uq_{tag}")
            kvf = _mm(ck, lw["w_ukv"], mode="nn", out_dtype=BF16, name=f"mla_ukv_{tag}")
            qr, kr = _rope_qk(qf, proj, cos_t, sin_t, name=f"rope_{tag}")
            oc, lse = _attn_tm_fwd(qf, qr, kvf, kr, name=f"attn_{tag}")
            x, y = _mm_resid(oc, lw["w_out"], x, gate_m, name=f"mla_out_{tag}")
            rec.update(proj=proj, cq=cq, ck=ck, qf=qf, qr=qr, kvf=kvf, kr=kr, lse=lse, oc=oc, y=y)
        rec["x1"] = x
        lw.update(get_weights(layer, "ffn", x))
        h2 = _adaln_fwd(x, _row(rep["norm_ffn_g"][layer]), scale_f, shift_f, name=f"adaln_ffn_{tag}")
        s, a2, b2 = _ffn_gu_fwd(h2, lw["wt_g"], lw["wt_u"], lw["dep_ffn"], name=f"ffn_gu_{tag}")
        x, y2 = _mm_resid(s, lw["w_down"], x, gate_f, tm=512, name=f"ffn_down_{tag}")
        rec.update(h2=h2, a2=a2, b2=b2, s=s, y2=y2)
        saved.append(rec)

    dx, st, ls = _loss_head(x, _row(rep["final_norm_g"]), target, name="loss_head")
    loss = ls[0, 0]
    grads = {"final_norm_g": st[0]}
    per_layer = {k: [None] * DEPTH for k in ("norm_mix_g", "norm_ffn_g")}
    per_gdn = {k: [None] * 2 for k in ("gdn_conv_wt", "gdn_a_log", "gdn_dt_bias", "gdn_norm_g")}
    per_mla = {k: [None] * 2 for k in ("mla_q_norm_g", "mla_kv_norm_g")}
    dmod = [None] * DEPTH
    dep = jnp.zeros((8, LANES), F32)

    for layer in reversed(range(DEPTH)):
        j = layer // 2
        tag = f"l{layer}"
        rec = saved[layer]
        lw = rec["lw"]
        shift_m, scale_m, gate_m, shift_f, scale_f, gate_f = [_row(mod[layer, i]) for i in range(N_MOD)]
        if layer == DEPTH - 1:
            dy2, st_g = _gate_bwd(dx, rec["y2"], gate_f, dep, name=f"gate_bwd_ffn_{tag}")
            dgate_f = st_g[0]
        dw_down = _mm(rec["s"], dy2, mode="tn", out_dtype=BF16, tm=FFN_BLOCK, tn=1024, name=f"ffn_down_dw_{tag}")
        da2, db2 = _ffn_down_dx(dy2, lw["w_down"], rec["a2"], rec["b2"], name=f"ffn_down_dx_{tag}")
        dwt_g = _mm(da2, rec["h2"], mode="tn", out_dtype=BF16, tm=FFN_BLOCK, tn=1024, name=f"ffn_g_dw_{tag}")
        dwt_u = _mm(db2, rec["h2"], mode="tn", out_dtype=BF16, tm=FFN_BLOCK, tn=1024, name=f"ffn_u_dw_{tag}")
        dep = put_grads(layer, "ffn", {"wt_g": dwt_g, "wt_u": dwt_u, "w_down": dw_down})
        dh2 = _mm_pair(da2, lw["wt_g"], db2, lw["wt_u"], out_dtype=BF16, name=f"ffn_gu_dx_{tag}")
        dx, st_n, dy = _adaln_gate_bwd(rec["x1"], _row(rep["norm_ffn_g"][layer]), scale_f, shift_f, dh2, dx, dep,
                                       rec["y"], gate_m, name=f"adaln_ffn_bwd_{tag}")
        per_layer["norm_ffn_g"][layer] = st_n[0]
        dscale_f, dshift_f, dgate_m = st_n[1], st_n[2], st_n[3]
        big = {}
        if layer % 2 == 0:
            big["w_out"] = _mm(rec["og"], dy, mode="tn", out_dtype=BF16, name=f"gdn_out_dw_{tag}")
            dog = _mm(dy, lw["w_out"], mode="nt", out_dtype=BF16, name=f"gdn_out_dx_{tag}")
            do, dgp, st_o = _gdn_onorm_bwd(rec["o"], rec["proj"], _row(rep["gdn_norm_g"][j]), dog,
                                           name=f"gdn_onorm_bwd_{tag}")
            per_gdn["gdn_norm_g"][j] = st_o[0]
            dqkv, dgb = _gdn_chunk_bwd(rec["qkv"], rec["gbeta"], rec["states"], rec["tinvs"], do,
                                       name=f"gdn_chunk_bwd_{tag}")
            dab, st_a = _gdn_gate_bwd(rec["ab"], rep["gdn_gate_prm"][j], dgb, name=f"gdn_gate_bwd_{tag}")
            per_gdn["gdn_a_log"][j] = st_a[0, :GDN_HEADS]
            per_gdn["gdn_dt_bias"][j] = st_a[1, :GDN_HEADS]
            dpre, dcw = _gdn_prep_bwd(rec["proj"], rep["gdn_conv_wt"][j], dqkv, name=f"gdn_prep_bwd_{tag}")
            per_gdn["gdn_conv_wt"][j] = dcw
            dproj = jnp.concatenate([dpre, dgp], axis=1)
            dw_main = _mm(dproj, rec["h"], mode="tn", out_dtype=BF16, tm=512, tn=1024, name=f"gdn_in_dw_{tag}")
            dw_ab = _mm(dab, rec["h"], mode="tn", out_dtype=BF16, tn=1024, name=f"gdn_in_ab_dw_{tag}")
            big["wt_in"] = jnp.concatenate([dw_main, dw_ab[:2 * GDN_HEADS]], axis=0)
            dep = put_grads(layer, "gdn", big)
            dh = _mm_pair(dproj, lw["wt_in"], dab, lw["wt_ab"], out_dtype=BF16, b1_rows=GDN_MAIN,
                          name=f"gdn_in_dx_{tag}")
        else:
            big["w_out"] = _mm(rec["oc"], dy, mode="tn", out_dtype=BF16, name=f"mla_out_dw_{tag}")
            doc = _mm(dy, lw["w_out"], mode="nt", out_dtype=BF16, name=f"mla_out_dx_{tag}")
            dqn, dqr, dkvf, dkr_parts = _attn_tm_bwd(rec["qf"], rec["qr"], rec["kvf"], rec["kr"], rec["oc"],
                                                     rec["lse"], doc, name=f"attn_bwd_{tag}")
            dqr_un, dkr_un = _rope_qk_bwd(dqr, dkr_parts, cos_t, sin_t, name=f"rope_bwd_{tag}")
            n_nope = MLA_HEADS * MLA_NOPE
            big["wt_uq"] = jnp.concatenate(
                [_mm(dqn, rec["cq"], mode="tn", out_dtype=BF16, name=f"mla_uq_dw_nope_{tag}"),
                 _mm(dqr_un, rec["cq"], mode="tn", out_dtype=BF16, name=f"mla_uq_dw_rope_{tag}")], axis=0)
            big["w_ukv"] = _mm(rec["ck"], dkvf, mode="tn", out_dtype=BF16, name=f"mla_ukv_dw_{tag}")
            dcq = _mm_pair(dqn, lw["wt_uq"], dqr_un, lw["wt_uq"][n_nope:], out_dtype=F32, b1_rows=n_nope,
                           name=f"mla_uq_dx_{tag}")
            dck = _mm(dkvf, lw["w_ukv"], mode="nt", out_dtype=F32, name=f"mla_ukv_dx_{tag}")
            dproj, st_p = _mla_prep_bwd(rec["proj"], _row(rep["mla_q_norm_g"][j]), _row(rep["mla_kv_norm_g"][j]),
                                        dcq, dck, dkr_un, name=f"mla_prep_bwd_{tag}")
            per_mla["mla_q_norm_g"][j] = st_p[0, :MLA_Q_RANK]
            per_mla["mla_kv_norm_g"][j] = st_p[0, MLA_Q_RANK:MLA_Q_RANK + MLA_KV_RANK]
            big["w_in"] = _mm(rec["h"], dproj, mode="tn", out_dtype=BF16, name=f"mla_in_dw_{tag}")
            dep = put_grads(layer, "mla", big)
            dh = _mm(dproj, lw["w_in"], mode="nt", out_dtype=BF16, name=f"mla_in_dx_{tag}")
        if layer > 0:
            below = saved[layer - 1]
            dx, st_n, dy2 = _adaln_gate_bwd(rec["x0"], _row(rep["norm_mix_g"][layer]), scale_m, shift_m, dh, dx, dep,
                                            below["y2"], _row(mod[layer - 1, N_MOD - 1]),
                                            name=f"adaln_mix_bwd_{tag}")
        else:
            dx, st_n = _adaln_bwd(rec["x0"], _row(rep["norm_mix_g"][layer]), scale_m, shift_m, dh, dx, dep,
                                  name=f"adaln_mix_bwd_{tag}")
        per_layer["norm_mix_g"][layer] = st_n[0]
        dmod[layer] = jnp.stack([st_n[2], st_n[1], dgate_m, dshift_f, dscale_f, dgate_f])
        if layer > 0:
            dgate_f = st_n[3]

    for d in (per_layer, per_gdn, per_mla):
        for k, v in d.items():
            grads[k] = jnp.stack(v)
    return loss, dx, jnp.stack(dmod), grads


BIG = ("gdn_w_in", "gdn_w_out", "mla_w_in", "mla_w_uq", "mla_w_ukv", "mla_w_out", "ffn_w_gate", "ffn_w_up",
       "ffn_w_down")
TRANSPOSED = ("gdn_w_in", "mla_w_uq", "ffn_w_gate", "ffn_w_up")
AHEAD = 4


def _view(k, a):
    return jnp.transpose(a, (0, 2, 1)) if k in TRANSPOSED else a
SMALL = ("ada_b", "norm_mix_g", "norm_ffn_g", "gdn_conv_w", "gdn_a_log", "gdn_dt_bias", "gdn_norm_g",
         "mla_q_norm_g", "mla_kv_norm_g", "final_norm_g")
WEIGHTS = ("ada_w", "ada_b", "norm_mix_g", "norm_ffn_g", "gdn_w_in", "gdn_conv_w", "gdn_a_log", "gdn_dt_bias",
           "gdn_norm_g", "gdn_w_out", "mla_w_in", "mla_q_norm_g", "mla_kv_norm_g", "mla_w_uq", "mla_w_ukv",
           "mla_w_out", "ffn_w_gate", "ffn_w_up", "ffn_w_down", "final_norm_g")


def _uq_to_kernel_layout(w, axis=-1):
    axis = axis % w.ndim
    lead, tail = w.shape[:axis], w.shape[axis + 1:]
    w4 = w.reshape(lead + (MLA_HEADS, MLA_QK) + tail)
    nope = lax.slice_in_dim(w4, 0, MLA_NOPE, axis=axis + 1).reshape(lead + (-1,) + tail)
    rope = lax.slice_in_dim(w4, MLA_NOPE, MLA_QK, axis=axis + 1).reshape(lead + (-1,) + tail)
    return jnp.concatenate([nope, rope], axis=axis)


def _uq_from_kernel_layout(w, axis=-1):
    axis = axis % w.ndim
    lead, tail = w.shape[:axis], w.shape[axis + 1:]
    nope = lax.slice_in_dim(w, 0, MLA_HEADS * MLA_NOPE, axis=axis).reshape(lead + (MLA_HEADS, MLA_NOPE) + tail)
    rope = lax.slice_in_dim(w, MLA_HEADS * MLA_NOPE, MLA_HEADS * MLA_QK, axis=axis).reshape(
        lead + (MLA_HEADS, MLA_ROPE) + tail)
    return jnp.concatenate([nope, rope], axis=axis + 1).reshape(lead + (-1,) + tail)


def _group_names(layer, kind):
    if kind == "ffn":
        return ("ffn_w_gate", "ffn_w_up", "ffn_w_down")
    return ("gdn_w_in", "gdn_w_out") if layer % 2 == 0 else ("mla_w_in", "mla_w_uq", "mla_w_ukv", "mla_w_out")


def _layer_index(name, layer):
    return layer if name.startswith("ffn") else layer // 2


def _cols(g):
    return jnp.transpose(g, (1, 0, 2)).reshape(g.shape[1], N_DEV * g.shape[2])


def _rows(g):
    return g.reshape(N_DEV * g.shape[1], g.shape[2])


def _uncols(full):
    r, c = full.shape
    return jnp.transpose(full.reshape(r, N_DEV, c // N_DEV), (1, 0, 2))


def _unrows(full):
    r, c = full.shape
    return full.reshape(N_DEV, r // N_DEV, c)


def _group_weights(layer, kind, got, token):
    if kind == "ffn":
        return {"wt_g": _rows(got["ffn_w_gate"]), "wt_u": _rows(got["ffn_w_up"]), "w_down": _rows(got["ffn_w_down"]),
                "dep_ffn": token}
    if layer % 2 == 0:
        wt_in = _rows(got["gdn_w_in"])
        return dict(wt_in=wt_in, wt_ab=jnp.pad(wt_in[GDN_MAIN:], ((0, LANES - 2 * GDN_HEADS), (0, 0))),
                    w_out=_rows(got["gdn_w_out"]), dep_mix=token)
    return dict(w_in=_rows(got["mla_w_in"]), wt_uq=_uq_to_kernel_layout(_rows(got["mla_w_uq"]), axis=0),
                w_ukv=_cols(got["mla_w_ukv"]), w_out=_rows(got["mla_w_out"]), dep_mix=token)


def _layer_grad_slots(kind, big):
    if kind == "ffn":
        return {"ffn_w_gate": _unrows(big["wt_g"]), "ffn_w_up": _unrows(big["wt_u"]),
                "ffn_w_down": _unrows(big["w_down"])}
    if kind == "gdn":
        return {"gdn_w_in": _unrows(big["wt_in"]), "gdn_w_out": _unrows(big["w_out"])}
    return {"mla_w_in": _unrows(big["w_in"]), "mla_w_uq": _unrows(_uq_from_kernel_layout(big["wt_uq"], axis=0)),
            "mla_w_ukv": _uncols(big["w_ukv"]), "mla_w_out": _unrows(big["w_out"])}


def _small_weights(tiny, rep):
    prm = jnp.zeros((2, 8, LANES), F32)
    prm = prm.at[:, 0, :GDN_HEADS].set(rep["gdn_a_log"]).at[:, 1, :GDN_HEADS].set(rep["gdn_dt_bias"])
    out = {
        "gdn_conv_wt": jnp.transpose(_gather_rows(tiny["gdn_conv_w"]), (0, 2, 1)),
        "mla_q_norm_g": jnp.transpose(tiny["mla_q_norm_g"], (1, 0, 2)).reshape(2, MLA_Q_RANK),
        "mla_kv_norm_g": jnp.transpose(tiny["mla_kv_norm_g"], (1, 0, 2)).reshape(2, MLA_KV_RANK),
        "gdn_gate_prm": prm,
    }
    for k in ("norm_mix_g", "norm_ffn_g", "gdn_norm_g", "final_norm_g"):
        out[k] = rep[k]
    return out


def _rope_tables(positions):
    inv_freq = ROPE_THETA ** (-jnp.arange(0, MLA_ROPE, 2, dtype=F32) / MLA_ROPE)
    ang = positions.astype(F32)[:, None] * inv_freq
    cos, sin = jnp.cos(ang), jnp.sin(ang)
    reps = LANES // MLA_ROPE
    return jnp.tile(jnp.concatenate([cos, cos], axis=1), (1, reps)), jnp.tile(
        jnp.concatenate([-sin, sin], axis=1), (1, reps))


def kernel(x, c, positions, ada_w, ada_b, norm_mix_g, norm_ffn_g, gdn_w_in, gdn_conv_w, gdn_a_log, gdn_dt_bias, gdn_norm_g, gdn_w_out, mla_w_in, mla_q_norm_g, mla_kv_norm_g, mla_w_uq, mla_w_ukv, mla_w_out, ffn_w_gate, ffn_w_up, ffn_w_down, final_norm_g, loss_target, m_ada_w, m_ada_b, m_norm_mix_g, m_norm_ffn_g, m_gdn_w_in, m_gdn_conv_w, m_gdn_a_log, m_gdn_dt_bias, m_gdn_norm_g, m_gdn_w_out, m_mla_w_in, m_mla_q_norm_g, m_mla_kv_norm_g, m_mla_w_uq, m_mla_w_ukv, m_mla_w_out, m_ffn_w_gate, m_ffn_w_up, m_ffn_w_down, m_final_norm_g, v_ada_w, v_ada_b, v_norm_mix_g, v_norm_ffn_g, v_gdn_w_in, v_gdn_conv_w, v_gdn_a_log, v_gdn_dt_bias, v_gdn_norm_g, v_gdn_w_out, v_mla_w_in, v_mla_q_norm_g, v_mla_kv_norm_g, v_mla_w_uq, v_mla_w_ukv, v_mla_w_out, v_ffn_w_gate, v_ffn_w_up, v_ffn_w_down, v_final_norm_g):
    W = dict(ada_w=ada_w, ada_b=ada_b, norm_mix_g=norm_mix_g, norm_ffn_g=norm_ffn_g, gdn_w_in=gdn_w_in,
             gdn_conv_w=gdn_conv_w, gdn_a_log=gdn_a_log, gdn_dt_bias=gdn_dt_bias, gdn_norm_g=gdn_norm_g,
             gdn_w_out=gdn_w_out, mla_w_in=mla_w_in, mla_q_norm_g=mla_q_norm_g, mla_kv_norm_g=mla_kv_norm_g,
             mla_w_uq=mla_w_uq, mla_w_ukv=mla_w_ukv, mla_w_out=mla_w_out, ffn_w_gate=ffn_w_gate,
             ffn_w_up=ffn_w_up, ffn_w_down=ffn_w_down, final_norm_g=final_norm_g)
    M = dict(ada_w=m_ada_w, ada_b=m_ada_b, norm_mix_g=m_norm_mix_g, norm_ffn_g=m_norm_ffn_g, gdn_w_in=m_gdn_w_in,
             gdn_conv_w=m_gdn_conv_w, gdn_a_log=m_gdn_a_log, gdn_dt_bias=m_gdn_dt_bias, gdn_norm_g=m_gdn_norm_g,
             gdn_w_out=m_gdn_w_out, mla_w_in=m_mla_w_in, mla_q_norm_g=m_mla_q_norm_g,
             mla_kv_norm_g=m_mla_kv_norm_g, mla_w_uq=m_mla_w_uq, mla_w_ukv=m_mla_w_ukv, mla_w_out=m_mla_w_out,
             ffn_w_gate=m_ffn_w_gate, ffn_w_up=m_ffn_w_up, ffn_w_down=m_ffn_w_down, final_norm_g=m_final_norm_g)
    V = dict(ada_w=v_ada_w, ada_b=v_ada_b, norm_mix_g=v_norm_mix_g, norm_ffn_g=v_norm_ffn_g, gdn_w_in=v_gdn_w_in,
             gdn_conv_w=v_gdn_conv_w, gdn_a_log=v_gdn_a_log, gdn_dt_bias=v_gdn_dt_bias, gdn_norm_g=v_gdn_norm_g,
             gdn_w_out=v_gdn_w_out, mla_w_in=v_mla_w_in, mla_q_norm_g=v_mla_q_norm_g,
             mla_kv_norm_g=v_mla_kv_norm_g, mla_w_uq=v_mla_w_uq, mla_w_ukv=v_mla_w_ukv, mla_w_out=v_mla_w_out,
             ffn_w_gate=v_ffn_w_gate, ffn_w_up=v_ffn_w_up, ffn_w_down=v_ffn_w_down, final_norm_g=v_final_norm_g)
    me = 4 * lax.axis_index("x") + 2 * lax.axis_index("y") + lax.axis_index("c")
    t = x.shape[1]
    wc = ada_w.shape[-1]

    groups = [(layer, kind) for layer in range(DEPTH) for kind in ("mix", "ffn")]

    def group_srcs(i):
        layer, kind = groups[i]
        return [_view(k, W[k])[_layer_index(k, layer)].astype(BF16) for k in _group_names(layer, kind)]

    tiny_shapes = [c.shape, gdn_conv_w.shape, mla_q_norm_g.shape, mla_kv_norm_g.shape]
    first = _gather_two_level([_pack([c, gdn_conv_w, mla_q_norm_g, mla_kv_norm_g])] + group_srcs(0),
                              name="gather_first")
    tiny_g = first[0]
    c_g, conv_g, qn_g, kvn_g = _unpack(tiny_g, tiny_shapes, lead=(N_DEV,))
    c_all = c_g.reshape(N_DEV, D_MODEL)
    rep = _small_weights({"gdn_conv_w": conv_g, "mla_q_norm_g": qn_g, "mla_kv_norm_g": kvn_g}, W)

    def start_group(i, dep):
        layer, kind = groups[i]
        return _exchange_start(group_srcs(i), scatter=False, name=f"gather_start_{kind}_l{layer}", dep=dep)


    b_cols = lax.dynamic_slice_in_dim(ada_b, me * wc, wc, axis=1).reshape(DEPTH, 1, wc)
    mod_part = _ada_mod(c_all, ada_w, b_cols, name="ada_mod")
    (mod_g,) = _exchange([mod_part], scatter=False, name="gather_mod")
    mod_mine = lax.dynamic_index_in_dim(mod_g, me, axis=2, keepdims=False)
    mod = jnp.transpose(mod_mine, (1, 0, 2)).reshape(DEPTH, N_MOD, D_MODEL)
    gather = {1: start_group(1, mod_g)}
    for i in range(2, AHEAD + 1):
        gather[i] = start_group(i, gather[i - 1][4])

    def get_weights(layer, kind, after):
        i = groups.index((layer, kind))
        names = _group_names(layer, kind)
        if i == 0:
            return _group_weights(layer, kind, dict(zip(names, first[1:])), gather[AHEAD][4])
        srcs, lands = _exchange_wait(gather[i], after, scatter=False, name=f"gather_wait_{kind}_l{layer}")
        token = jnp.zeros((8, LANES), F32)
        if i + AHEAD < len(groups):
            gather[i + AHEAD] = start_group(i + AHEAD, lands[0])
            token = gather[i + AHEAD][4]
        got = {k: lax.dynamic_update_index_in_dim(z, s, me, 0) for k, s, z in zip(names, srcs, lands)}
        return _group_weights(layer, kind, got, token)

    scatter = []

    def put_grads(layer, kind, big):
        slots = _layer_grad_slots(kind, big)
        started = _exchange_start(list(slots.values()), scatter=True, name=f"scatter_start_{kind}_l{layer}")
        scatter.append((layer, kind, list(slots.keys()), started))
        return started[4]

    cos_t, sin_t = _rope_tables(positions[0])
    loss, dx, dmod, g = _local_step(x[0], loss_target[0], mod, cos_t, sin_t, rep, get_weights, put_grads)

    parts = {k: [None] * W[k].shape[0] for k in BIG}
    res = {}

    def wait_group(entry, after):
        layer, kind, names, started = entry
        srcs, lands = _exchange_wait(started, after, scatter=True, name=f"scatter_wait_{kind}_l{layer}")
        for k, s, z in zip(names, srcs, lands):
            own = lax.dynamic_index_in_dim(s, me, 0, keepdims=False)
            parts[k][_layer_index(k, layer)] = lax.dynamic_update_index_in_dim(z, own, me, 0)

    for entry in scatter[:-1]:
        wait_group(entry, dx)
    early = [k for k in BIG if k not in scatter[-1][2]]
    def update(k):
        outs = _adamw(parts[k], _view(k, W[k]), _view(k, M[k]), _view(k, V[k]), name=f"adamw_{k}")
        return tuple(_view(k, o) for o in outs)

    for k in early:
        res[k] = update(k)
    loss, dmod, done = lax.optimization_barrier((loss, dmod, [res[k] for k in early]))
    for k, r in zip(early, done):
        res[k] = r

    small_local = [dmod.reshape(DEPTH, N_MOD * D_MODEL), g["norm_mix_g"], g["norm_ffn_g"],
                   jnp.transpose(g["gdn_conv_wt"], (0, 2, 1)), g["gdn_a_log"], g["gdn_dt_bias"], g["gdn_norm_g"],
                   g["mla_q_norm_g"], g["mla_kv_norm_g"], g["final_norm_g"], loss.reshape(1)]
    small_shapes = [a.shape for a in small_local]
    (small_g,) = _exchange([_pack(small_local)], scatter=False, name="gather_small_grads")
    small_sum = _unpack(_sum_parts(small_g, name="sum_small_grads"), small_shapes)
    loss = small_sum[-1][0]
    dmod_all = _unpack(small_g, small_shapes[:1], lead=(N_DEV,))[0]
    sg = dict(zip(SMALL, small_sum))
    wait_group(scatter[-1], small_g)
    sg["gdn_conv_w"] = lax.dynamic_slice_in_dim(sg["gdn_conv_w"], me * gdn_conv_w.shape[1], gdn_conv_w.shape[1], 1)
    sg["mla_q_norm_g"] = lax.dynamic_slice_in_dim(sg["mla_q_norm_g"], me * mla_q_norm_g.shape[1],
                                                  mla_q_norm_g.shape[1], 1)
    sg["mla_kv_norm_g"] = lax.dynamic_slice_in_dim(sg["mla_kv_norm_g"], me * mla_kv_norm_g.shape[1],
                                                   mla_kv_norm_g.shape[1], 1)

    dmod_cols = jnp.transpose(lax.dynamic_slice_in_dim(dmod_all, me * wc, wc, axis=2), (1, 0, 2))
    res["ada_w"] = _ada_grad_adamw(c_all, dmod_cols, ada_w, m_ada_w, v_ada_w, name="ada_w_grad_adamw")
    for k in BIG:
        if k not in early:
            res[k] = update(k)
    shapes = [W[k].shape for k in SMALL]
    packed = [_pack([d[k] for k in SMALL]) for d in (sg, W, M, V)]
    outs = _adamw([packed[0][None]], packed[1][None], packed[2][None], packed[3][None], name="adamw_small")
    unpacked = [_unpack(o[0], shapes) for o in outs]
    for i, k in enumerate(SMALL):
        res[k] = tuple(u[i] for u in unpacked)

    return (loss, dx[None], *[res[k][0] for k in WEIGHTS], *[res[k][1] for k in WEIGHTS],
            *[res[k][2] for k in WEIGHTS], *[res[k][3] for k in WEIGHTS])
```

```python
import math

import jax
import jax.numpy as jnp
from jax import lax
from jax.experimental import pallas as pl
from jax.experimental.pallas import tpu as pltpu

F32 = jnp.float32
BF16 = jnp.bfloat16
MXU_DTYPE = jnp.bfloat16

N_DEV = 8
D_MODEL = 1024
DEPTH = 4
GDN_HEADS = 8
GDN_HEAD_DIM = 128
GDN_KEY_DIM = GDN_HEADS * GDN_HEAD_DIM
GDN_CHUNK = 64
GDN_HEAD_BATCH = 8
GDN_CONV = 4
GDN_PREP_HEADS = 2
GDN_MAIN = 4 * GDN_KEY_DIM
MLA_HEADS = 8
MLA_NOPE = 128
MLA_ROPE = 64
MLA_V = 128
MLA_Q_RANK = 384
MLA_KV_RANK = 256
MLA_IN = MLA_Q_RANK + MLA_KV_RANK + MLA_ROPE
MLA_QK = MLA_NOPE + MLA_ROPE
ROPE_THETA = 10000.0
D_FF = 2816
N_MOD = 6
EPS = 1e-6
LANES = 128
VMEM_LIMIT = 48 * 1024 * 1024

ADAM_LR = 0.001
ADAM_B1 = 0.9
ADAM_B2 = 0.999
ADAM_EPS = 1e-08
ADAM_WD = 0.01
ADAM_STEP = 10
ADAM_BC1 = 1.0 - ADAM_B1 ** ADAM_STEP
ADAM_BC2 = 1.0 - ADAM_B2 ** ADAM_STEP

NN = (((1,), (0,)), ((), ()))
NT = (((1,), (1,)), ((), ()))
TN = (((0,), (0,)), ((), ()))
NEG = -1e30


def _dotb(a, b, dims):
    return lax.dot_general(a.astype(MXU_DTYPE), b.astype(MXU_DTYPE), dims, preferred_element_type=F32)


def _split(a):
    hi = a.astype(BF16)
    return hi, (a - hi.astype(F32)).astype(BF16)


def _dotf(a, b, dims):
    ah, al = _split(a)
    bh, bl = _split(b)
    dot = lambda u, v: lax.dot_general(u, v, dims, preferred_element_type=F32)
    return dot(ah, bh) + (dot(ah, bl) + dot(al, bh))


def _params(*sem):
    return pltpu.CompilerParams(dimension_semantics=sem, vmem_limit_bytes=VMEM_LIMIT)


def _pick(n, pref, mult=LANES):
    best = None
    t = mult
    while t <= min(n, pref):
        if n % t == 0:
            best = t
        t += mult
    return best if best is not None else n


def _sigmoid(z):
    return 0.5 * jnp.tanh(0.5 * z) + 0.5


def _exchange(arrays, *, scatter, name):
    n = len(arrays)
    out_shape = tuple(
        jax.ShapeDtypeStruct(a.shape if scatter else (N_DEV,) + a.shape, a.dtype) for a in arrays)

    def body(*refs):
        ins, outs = refs[:n], refs[n:2 * n]
        send_sems, recv_sems, local_sems = refs[2 * n:]
        x, y, c = lax.axis_index("x"), lax.axis_index("y"), lax.axis_index("c")
        me = 4 * x + 2 * y + c
        copies = []
        for k in range(n):
            src_own = ins[k].at[me] if scatter else ins[k]
            own = pltpu.make_async_copy(src_own, outs[k].at[me], local_sems.at[k])
            own.start()
            copies.append(own)
        sends = []
        for p in range(1, N_DEV):
            px, py, pc = x ^ ((p >> 2) & 1), y ^ ((p >> 1) & 1), c ^ (p & 1)
            peer = 4 * px + 2 * py + pc
            for k in range(n):
                cp = pltpu.make_async_remote_copy(
                    src_ref=ins[k].at[peer] if scatter else ins[k],
                    dst_ref=outs[k].at[me],
                    send_sem=send_sems.at[k, p - 1],
                    recv_sem=recv_sems.at[k, p - 1],
                    device_id=(px, py, pc),
                    device_id_type=pl.DeviceIdType.MESH,
                )
                cp.start()
                sends.append((cp, k, peer, p))
        for cp, k, peer, p in sends:
            pltpu.make_async_remote_copy(
                src_ref=ins[k].at[peer] if scatter else ins[k],
                dst_ref=outs[k].at[peer],
                send_sem=send_sems.at[k, p - 1],
                recv_sem=recv_sems.at[k, p - 1],
                device_id=(x, y, c),
                device_id_type=pl.DeviceIdType.MESH,
            ).wait_recv()
        for cp, _, _, _ in sends:
            cp.wait_send()
        for own in copies:
            own.wait()

    any_spec = pl.BlockSpec(memory_space=pl.ANY)
    outs = pl.pallas_call(
        body,
        name=name,
        out_shape=out_shape,
        in_specs=[any_spec] * n,
        out_specs=tuple([any_spec] * n),
        scratch_shapes=[
            pltpu.SemaphoreType.DMA((n, N_DEV - 1)),
            pltpu.SemaphoreType.DMA((n, N_DEV - 1)),
            pltpu.SemaphoreType.DMA((n,)),
        ],
        compiler_params=pltpu.CompilerParams(has_side_effects=True),
    )(*arrays)
    return list(outs)


def _gather_two_level(arrays, *, name):
    n = len(arrays)
    out_shape = tuple(jax.ShapeDtypeStruct((N_DEV,) + a.shape, a.dtype) for a in arrays)

    def body(*refs):
        ins, outs = refs[:n], refs[n:2 * n]
        send_sems, recv_sems, local_sems = refs[2 * n:]
        x, y, c = lax.axis_index("x"), lax.axis_index("y"), lax.axis_index("c")
        me = 4 * x + 2 * y + c
        sibling = (x, y, 1 - c)
        chips = [(1 - x, y), (x, 1 - y), (1 - x, 1 - y)]

        def slot(px, py, pc):
            return 4 * px + 2 * py + pc

        def copy(k, q, block, to, src=None):
            return pltpu.make_async_remote_copy(
                src_ref=outs[k].at[slot(*block)] if src is None else src,
                dst_ref=outs[k].at[slot(*block)],
                send_sem=send_sems.at[k, q], recv_sem=recv_sems.at[k, q],
                device_id=to, device_id_type=pl.DeviceIdType.MESH)

        own = [pltpu.make_async_copy(ins[k], outs[k].at[me], local_sems.at[k]) for k in range(n)]
        for cp in own:
            cp.start()
        first = []
        for k in range(n):
            first.append(copy(k, 0, (x, y, c), sibling, src=ins[k]))
            first += [copy(k, 1 + j, (x, y, c), (*chip, c), src=ins[k]) for j, chip in enumerate(chips)]
        for cp in first:
            cp.start()
        passed = []
        for j, chip in enumerate(chips):
            for k in range(n):
                copy(k, 1 + j, (*chip, c), (x, y, c)).wait_recv()
                fwd = copy(k, 4 + j, (*chip, c), sibling)
                fwd.start()
                passed.append(fwd)
        for k in range(n):
            copy(k, 0, sibling, (x, y, c)).wait_recv()
            for j, chip in enumerate(chips):
                copy(k, 4 + j, (*chip, 1 - c), (x, y, c)).wait_recv()
        for cp in first + passed:
            cp.wait_send()
        for cp in own:
            cp.wait()

    any_spec = pl.BlockSpec(memory_space=pl.ANY)
    outs = pl.pallas_call(
        body, name=name, out_shape=out_shape, in_specs=[any_spec] * n, out_specs=tuple([any_spec] * n),
        scratch_shapes=[pltpu.SemaphoreType.DMA((n, N_DEV - 1)), pltpu.SemaphoreType.DMA((n, N_DEV - 1)),
                        pltpu.SemaphoreType.DMA((n,))],
        compiler_params=pltpu.CompilerParams(has_side_effects=True),
    )(*arrays)
    return list(outs)


def _peer(x, y, c, p):
    return x ^ ((p >> 2) & 1), y ^ ((p >> 1) & 1), c ^ (p & 1)


def _exchange_start(arrays, *, scatter, name, dep=None):
    n = len(arrays)
    deps = [] if dep is None else [dep]
    lands = [lax.empty(a.shape if scatter else (N_DEV,) + a.shape, a.dtype) for a in arrays]

    def body(*refs):
        ins, zones = refs[:n], refs[n:2 * n]
        send_sems, recv_sems = refs[2 * n + len(deps)], refs[2 * n + len(deps) + 1]
        token = refs[-1]
        x, y, c = lax.axis_index("x"), lax.axis_index("y"), lax.axis_index("c")
        me = 4 * x + 2 * y + c
        for p in range(1, N_DEV):
            px, py, pc = _peer(x, y, c, p)
            for k in range(n):
                pltpu.make_async_remote_copy(
                    src_ref=ins[k].at[4 * px + 2 * py + pc] if scatter else ins[k],
                    dst_ref=zones[k].at[me],
                    send_sem=send_sems.at[k * (N_DEV - 1) + p - 1],
                    recv_sem=recv_sems.at[k * (N_DEV - 1) + p - 1],
                    device_id=(px, py, pc),
                    device_id_type=pl.DeviceIdType.MESH,
                ).start()
        token[...] = jnp.zeros_like(token)

    hbm = pl.BlockSpec(memory_space=pltpu.HBM)
    sem = pl.BlockSpec(memory_space=pltpu.SEMAPHORE)
    outs = pl.pallas_call(
        body,
        name=name,
        out_shape=(pltpu.SemaphoreType.DMA((n * (N_DEV - 1),)), pltpu.SemaphoreType.DMA((n * (N_DEV - 1),)),
                   *[pltpu.HBM(a.shape, a.dtype) for a in arrays], *[pltpu.HBM(z.shape, z.dtype) for z in lands],
                   jax.ShapeDtypeStruct((8, LANES), F32)),
        in_specs=[hbm] * (2 * n) + [pl.BlockSpec(memory_space=pl.ANY)] * len(deps),
        out_specs=(sem, sem, *[hbm] * (2 * n), pl.BlockSpec(memory_space=pltpu.VMEM)),
        input_output_aliases={k: 2 + k for k in range(2 * n)},
        compiler_params=pltpu.CompilerParams(has_side_effects=pltpu.SideEffectType.DATAFLOW_SIDE_EFFECTING),
    )(*[pltpu.with_memory_space_constraint(a, pltpu.HBM) for a in arrays],
      *[pltpu.with_memory_space_constraint(z, pltpu.HBM) for z in lands], *deps)
    return outs[0], outs[1], list(outs[2:2 + n]), list(outs[2 + n:2 + 2 * n]), outs[-1]


def _exchange_wait(started, after, *, scatter, name):
    send_sems, recv_sems, srcs, lands, _ = started
    n = len(srcs)

    def body(*refs):
        ins, zones = refs[:n], refs[n:2 * n]
        s_sems, r_sems = refs[2 * n], refs[2 * n + 1]
        x, y, c = lax.axis_index("x"), lax.axis_index("y"), lax.axis_index("c")
        for p in range(1, N_DEV):
            px, py, pc = _peer(x, y, c, p)
            peer = 4 * px + 2 * py + pc
            for k in range(n):
                cp = pltpu.make_async_remote_copy(
                    src_ref=ins[k].at[peer] if scatter else ins[k],
                    dst_ref=zones[k].at[peer],
                    send_sem=s_sems.at[k * (N_DEV - 1) + p - 1],
                    recv_sem=r_sems.at[k * (N_DEV - 1) + p - 1],
                    device_id=(px, py, pc),
                    device_id_type=pl.DeviceIdType.MESH,
                )
                cp.wait_send()
                cp.wait_recv()

    hbm = pl.BlockSpec(memory_space=pltpu.HBM)
    sem = pl.BlockSpec(memory_space=pltpu.SEMAPHORE)
    outs = pl.pallas_call(
        body,
        name=name,
        out_shape=tuple(pltpu.HBM(a.shape, a.dtype) for a in srcs + lands),
        in_specs=[hbm] * (2 * n) + [sem, sem, pl.BlockSpec(memory_space=pl.ANY)],
        out_specs=tuple([hbm] * (2 * n)),
        input_output_aliases={k: k for k in range(2 * n)},
        compiler_params=pltpu.CompilerParams(has_side_effects=pltpu.SideEffectType.DATAFLOW_SIDE_EFFECTING),
    )(*srcs, *lands, send_sems, recv_sems, after)
    return list(outs[:n]), list(outs[n:])


def _mm(a, b, *, mode, out_dtype, name, add=None, tm=512, tn=512, b_rows=None, dep=None):
    rows_b = b.shape[0] if b_rows is None else b_rows
    if mode == "nn":
        (m, kd), nd = a.shape, b.shape[1]
        assert kd == rows_b
    elif mode == "nt":
        (m, kd), nd = a.shape, rows_b
    else:
        (kd, m), nd = a.shape, b.shape[1]
    tm = _pick(m, tm, LANES if mode == "tn" else 16)
    tn = _pick(nd, tn)
    dims = {"nn": NN, "nt": NT, "tn": TN}[mode]
    ni, nj = m // tm, nd // tn
    a_bytes, b_bytes = a.size * a.dtype.itemsize, b.size * b.dtype.itemsize
    i_outer = a_bytes + ni * b_bytes <= b_bytes + nj * a_bytes
    ij = (lambda g0, g1: (g0, g1)) if i_outer else (lambda g0, g1: (g1, g0))
    a_spec = (pl.BlockSpec((kd, tm), lambda g0, g1: (0, ij(g0, g1)[0])) if mode == "tn"
              else pl.BlockSpec((tm, kd), lambda g0, g1: (ij(g0, g1)[0], 0)))
    b_spec = (pl.BlockSpec((tn, kd), lambda g0, g1: (ij(g0, g1)[1], 0)) if mode == "nt"
              else pl.BlockSpec((kd, tn), lambda g0, g1: (0, ij(g0, g1)[1])))
    o_spec = pl.BlockSpec((tm, tn), lambda g0, g1: ij(g0, g1))
    has_add = add is not None

    def body(*refs):
        a_ref, b_ref = refs[0], refs[1]
        o_ref = refs[-1]
        acc = _dotb(a_ref[...], b_ref[...], dims)
        if has_add:
            acc = acc + refs[2][...].astype(F32)
        o_ref[...] = acc.astype(o_ref.dtype)

    ins = [a, b] + ([add] if has_add else []) + ([] if dep is None else [dep])
    specs = ([a_spec, b_spec] + ([o_spec] if has_add else [])
             + ([] if dep is None else [pl.BlockSpec((8, LANES), lambda g0, g1: (0, 0))]))
    return pl.pallas_call(
        body, name=name, grid=(ni, nj) if i_outer else (nj, ni), in_specs=specs, out_specs=o_spec,
        out_shape=jax.ShapeDtypeStruct((m, nd), out_dtype),
        compiler_params=_params("parallel", "parallel"),
    )(*ins)


def _mm_pair(a1, b1, a2, b2, *, out_dtype, name, b1_rows=None, tm=256):
    m, k1 = a1.shape
    k2, nd = b2.shape
    assert k1 == (b1.shape[0] if b1_rows is None else b1_rows) and a2.shape == (m, k2) and b1.shape[1] == nd
    tm = _pick(m, tm, 16)

    def body(a1_ref, b1_ref, a2_ref, b2_ref, o_ref):
        o_ref[...] = (_dotb(a1_ref[...], b1_ref[...], NN) + _dotb(a2_ref[...], b2_ref[...], NN)).astype(o_ref.dtype)

    return pl.pallas_call(
        body, name=name, grid=(m // tm,),
        in_specs=[pl.BlockSpec((tm, k1), lambda i: (i, 0)), pl.BlockSpec((k1, nd), lambda i: (0, 0)),
                  pl.BlockSpec((tm, k2), lambda i: (i, 0)), pl.BlockSpec((k2, nd), lambda i: (0, 0))],
        out_specs=pl.BlockSpec((tm, nd), lambda i: (i, 0)), out_shape=jax.ShapeDtypeStruct((m, nd), out_dtype),
        compiler_params=_params("parallel"),
    )(a1, b1, a2, b2)


def _mm_resid(a, b, x, gate, *, name, tm=256, tn=1024):
    m, kd = a.shape
    nd = b.shape[1]
    tm = _pick(m, tm, 16)
    tn = _pick(nd, tn)
    o_spec = pl.BlockSpec((tm, tn), lambda i, j: (i, j))

    def body(a_ref, b_ref, x_ref, g_ref, xo_ref, y_ref):
        y = _dotb(a_ref[...], b_ref[...], NN)
        y_ref[...] = y.astype(y_ref.dtype)
        xo_ref[...] = x_ref[...] + g_ref[...] * y

    return pl.pallas_call(
        body, name=name, grid=(m // tm, nd // tn),
        in_specs=[pl.BlockSpec((tm, kd), lambda i, j: (i, 0)), pl.BlockSpec((kd, tn), lambda i, j: (0, j)),
                  o_spec, pl.BlockSpec((1, tn), lambda i, j: (0, j))],
        out_specs=(o_spec, o_spec),
        out_shape=(jax.ShapeDtypeStruct((m, nd), F32), jax.ShapeDtypeStruct((m, nd), BF16)),
        compiler_params=_params("parallel", "parallel"),
    )(a, b, x, gate)


ROWS = 256


def _row_spec(width, rows=ROWS):
    return pl.BlockSpec((rows, width), lambda i: (i, 0))


def _const_spec(shape):
    return pl.BlockSpec(shape, lambda i: tuple(0 for _ in shape))


def _adaln_fwd(x, g, scale, shift, *, name):
    t, d = x.shape

    def body(x_ref, g_ref, sc_ref, sh_ref, h_ref):
        xv = x_ref[...]
        r = lax.rsqrt(jnp.mean(xv * xv, axis=-1, keepdims=True) + EPS)
        h_ref[...] = (xv * r * g_ref[...] * (1.0 + sc_ref[...]) + sh_ref[...]).astype(h_ref.dtype)

    return pl.pallas_call(
        body, name=name, grid=(t // ROWS,),
        in_specs=[_row_spec(d), _const_spec((1, d)), _const_spec((1, d)), _const_spec((1, d))],
        out_specs=_row_spec(d), out_shape=jax.ShapeDtypeStruct((t, d), BF16),
        compiler_params=_params("parallel"),
    )(x, g, scale, shift)


def _adaln_bwd(x, g, scale, shift, dh, dres, dep, *, name):
    t, d = x.shape

    def body(x_ref, g_ref, sc_ref, sh_ref, dh_ref, dr_ref, dep_ref, dx_ref, st_ref):
        @pl.when(pl.program_id(0) == 0)
        def _():
            st_ref[...] = jnp.zeros_like(st_ref)

        xv = x_ref[...]
        dhv = dh_ref[...].astype(F32)
        gv = g_ref[...]
        r = lax.rsqrt(jnp.mean(xv * xv, axis=-1, keepdims=True) + EPS)
        xh = xv * r
        nv = xh * gv
        dn = dhv * (1.0 + sc_ref[...])
        dxh = dn * gv
        dx_ref[...] = dr_ref[...] + r * (dxh - xh * jnp.mean(dxh * xh, axis=-1, keepdims=True))
        st_ref[0:1, :] += jnp.sum(dn * xh, axis=0, keepdims=True)
        st_ref[1:2, :] += jnp.sum(dhv * nv, axis=0, keepdims=True)
        st_ref[2:3, :] += jnp.sum(dhv, axis=0, keepdims=True)

    return pl.pallas_call(
        body, name=name, grid=(t // ROWS,),
        in_specs=[_row_spec(d), _const_spec((1, d)), _const_spec((1, d)), _const_spec((1, d)),
                  _row_spec(d), _row_spec(d), _const_spec((8, LANES))],
        out_specs=(_row_spec(d), _const_spec((8, d))),
        out_shape=(jax.ShapeDtypeStruct((t, d), F32), jax.ShapeDtypeStruct((8, d), F32)),
        compiler_params=_params("arbitrary"),
    )(x, g, scale, shift, dh, dres, dep)


def _adaln_gate_bwd(x, g, scale, shift, dh, dres, dep, y_up, gate_up, *, name):
    t, d = x.shape

    def body(x_ref, g_ref, sc_ref, sh_ref, dh_ref, dr_ref, dep_ref, y_ref, gu_ref, dx_ref, st_ref, dy_ref):
        @pl.when(pl.program_id(0) == 0)
        def _():
            st_ref[...] = jnp.zeros_like(st_ref)

        xv = x_ref[...]
        dhv = dh_ref[...].astype(F32)
        gv = g_ref[...]
        r = lax.rsqrt(jnp.mean(xv * xv, axis=-1, keepdims=True) + EPS)
        xh = xv * r
        nv = xh * gv
        dn = dhv * (1.0 + sc_ref[...])
        dxh = dn * gv
        dx = dr_ref[...] + r * (dxh - xh * jnp.mean(dxh * xh, axis=-1, keepdims=True))
        dx_ref[...] = dx
        dy_ref[...] = (dx * gu_ref[...]).astype(dy_ref.dtype)
        st_ref[0:1, :] += jnp.sum(dn * xh, axis=0, keepdims=True)
        st_ref[1:2, :] += jnp.sum(dhv * nv, axis=0, keepdims=True)
        st_ref[2:3, :] += jnp.sum(dhv, axis=0, keepdims=True)
        st_ref[3:4, :] += jnp.sum(dx * y_ref[...].astype(F32), axis=0, keepdims=True)

    return pl.pallas_call(
        body, name=name, grid=(t // ROWS,),
        in_specs=[_row_spec(d), _const_spec((1, d)), _const_spec((1, d)), _const_spec((1, d)),
                  _row_spec(d), _row_spec(d), _const_spec((8, LANES)), _row_spec(d), _const_spec((1, d))],
        out_specs=(_row_spec(d), _const_spec((8, d)), _row_spec(d)),
        out_shape=(jax.ShapeDtypeStruct((t, d), F32), jax.ShapeDtypeStruct((8, d), F32),
                   jax.ShapeDtypeStruct((t, d), BF16)),
        compiler_params=_params("arbitrary"),
    )(x, g, scale, shift, dh, dres, dep, y_up, gate_up)


def _loss_head(x, g, target, y_up, gate_up, *, name):
    t, d = x.shape

    def body(x_ref, g_ref, t_ref, y_ref, gu_ref, dx_ref, st_ref, ls_ref, dy_ref):
        @pl.when(pl.program_id(0) == 0)
        def _():
            st_ref[...] = jnp.zeros_like(st_ref)
            ls_ref[...] = jnp.zeros_like(ls_ref)

        xv = x_ref[...]
        gv = g_ref[...]
        r = lax.rsqrt(jnp.mean(xv * xv, axis=-1, keepdims=True) + EPS)
        xh = xv * r
        err = xh * gv - t_ref[...]
        ls_ref[...] += 0.5 * jnp.sum(jnp.mean(err * err, axis=-1, keepdims=True))
        dy = err * (1.0 / d)
        dxh = dy * gv
        dx = r * (dxh - xh * jnp.mean(dxh * xh, axis=-1, keepdims=True))
        dx_ref[...] = dx
        dy_ref[...] = (dx * gu_ref[...]).astype(dy_ref.dtype)
        st_ref[0:1, :] += jnp.sum(dy * xh, axis=0, keepdims=True)
        st_ref[3:4, :] += jnp.sum(dx * y_ref[...].astype(F32), axis=0, keepdims=True)

    return pl.pallas_call(
        body, name=name, grid=(t // ROWS,),
        in_specs=[_row_spec(d), _const_spec((1, d)), _row_spec(d), _row_spec(d), _const_spec((1, d))],
        out_specs=(_row_spec(d), _const_spec((8, d)), _const_spec((8, LANES)), _row_spec(d)),
        out_shape=(jax.ShapeDtypeStruct((t, d), F32), jax.ShapeDtypeStruct((8, d), F32),
                   jax.ShapeDtypeStruct((8, LANES), F32), jax.ShapeDtypeStruct((t, d), BF16)),
        compiler_params=_params("arbitrary"),
    )(x, g, target, y_up, gate_up)


FFN_BLOCK = D_FF // 2
FFN_ROWS = 512


def _ffn_chunks(width):
    edges = [min(width, 3 * LANES * i) for i in range(width // (3 * LANES) + 2)]
    return [slice(lo, hi) for lo, hi in zip(edges[:-1], edges[1:]) if hi > lo]


def _ffn_gu_fwd(h, wg, wu, dep, *, name):
    t, d = h.shape
    tn = FFN_BLOCK

    chunks = _ffn_chunks(tn)
    rows = _pick(t, FFN_ROWS, 16)

    def body(h_ref, wg_ref, wu_ref, dep_ref, s_ref, a_ref, b_ref):
        hv = h_ref[...]
        ab = [(_dotb(hv, wg_ref[sl, :], NT), _dotb(hv, wu_ref[sl, :], NT)) for sl in chunks]
        for sl, (a, b) in zip(chunks, ab):
            s_ref[:, sl] = (a * _sigmoid(a) * b).astype(s_ref.dtype)
            a_ref[:, sl] = a.astype(a_ref.dtype)
            b_ref[:, sl] = b.astype(b_ref.dtype)

    w_spec = pl.BlockSpec((tn, d), lambda j, i: (j, 0))
    o_spec = pl.BlockSpec((rows, tn), lambda j, i: (i, j))
    return pl.pallas_call(
        body, name=name, grid=(D_FF // tn, t // rows),
        in_specs=[pl.BlockSpec((rows, d), lambda j, i: (i, 0)), w_spec, w_spec,
                  pl.BlockSpec((8, LANES), lambda j, i: (0, 0))],
        out_specs=(o_spec, o_spec, o_spec),
        out_shape=(jax.ShapeDtypeStruct((t, D_FF), BF16),) * 3,
        compiler_params=_params("parallel", "parallel"),
    )(h, wg, wu, dep)


def _ffn_down_dx(dy, w_down, a, b, *, name):
    t, d = dy.shape
    tn = FFN_BLOCK

    chunks = _ffn_chunks(tn)
    rows = _pick(t, FFN_ROWS, 16)

    def body(dy_ref, w_ref, a_ref, b_ref, da_ref, db_ref):
        dyv = dy_ref[...]
        ds = [_dotb(dyv, w_ref[sl, :], NT) for sl in chunks]
        for sl, dsc in zip(chunks, ds):
            av = a_ref[:, sl].astype(F32)
            sg = _sigmoid(av)
            da_ref[:, sl] = (dsc * b_ref[:, sl].astype(F32) * sg * (1.0 + av * (1.0 - sg))).astype(da_ref.dtype)
            db_ref[:, sl] = (dsc * av * sg).astype(db_ref.dtype)

    o_spec = pl.BlockSpec((rows, tn), lambda j, i: (i, j))
    return pl.pallas_call(
        body, name=name, grid=(D_FF // tn, t // rows),
        in_specs=[pl.BlockSpec((rows, d), lambda j, i: (i, 0)), pl.BlockSpec((tn, d), lambda j, i: (j, 0)),
                  o_spec, o_spec],
        out_specs=(o_spec, o_spec),
        out_shape=(jax.ShapeDtypeStruct((t, D_FF), BF16),) * 2,
        compiler_params=_params("parallel", "parallel"),
    )(dy, w_down, a, b)


def _shift_rows(v, s, rows):
    if s == 0:
        return v
    return jnp.where(rows >= s, pltpu.roll(v, s, 0), 0.0)


def _unshift_rows(v, s, rows, t):
    if s == 0:
        return v
    return jnp.where(rows < t - s, pltpu.roll(v, t - s, 0), 0.0)


def _conv_taps(x, rows):
    return [_shift_rows(x, GDN_CONV - 1 - j, rows) for j in range(GDN_CONV)]


def _conv_silu(xs, w):
    z = w[0:1, :] * xs[0]
    for j in range(1, GDN_CONV):
        z = z + w[j:j + 1, :] * xs[j]
    sg = _sigmoid(z)
    return z, sg, z * sg


def _gdn_prep_fwd(proj, conv_wt, *, name):
    t = proj.shape[0]
    nh = GDN_HEADS

    hp = GDN_PREP_HEADS
    wd = hp * LANES

    def body(x_ref, w_ref, y_ref):
        j = pl.program_id(0) * hp
        rows = lax.broadcasted_iota(jnp.int32, (t, LANES), 0)
        qscale = jnp.where(j < nh, GDN_HEAD_DIM ** -0.5, 1.0)
        for i in range(hp):
            sl = slice(i * LANES, (i + 1) * LANES)
            _, _, s = _conv_silu(_conv_taps(x_ref[:, sl], rows), w_ref[:, sl])
            rs = lax.rsqrt(jnp.sum(s * s, axis=-1, keepdims=True) + EPS)
            y_ref[:, sl] = jnp.where(j < 2 * nh, s * rs * qscale, s)

    return pl.pallas_call(
        body, name=name, grid=(3 * nh // hp,),
        in_specs=[pl.BlockSpec((t, wd), lambda j: (0, j)), pl.BlockSpec((GDN_CONV, wd), lambda j: (0, j))],
        out_specs=pl.BlockSpec((t, wd), lambda j: (0, j)),
        out_shape=jax.ShapeDtypeStruct((t, 3 * GDN_KEY_DIM), F32),
        compiler_params=_params("parallel"),
    )(proj, conv_wt)


def _gdn_prep_bwd(proj, conv_wt, dy, *, name):
    t = proj.shape[0]
    nh = GDN_HEADS

    hp = GDN_PREP_HEADS
    wd = hp * LANES
    per_seg = nh // hp

    def body(x_ref, w_ref, dy_ref, dx_ref, dw_ref):
        j = pl.program_id(0) * hp
        rows = lax.broadcasted_iota(jnp.int32, (t, LANES), 0)
        qscale = jnp.where(j < nh, GDN_HEAD_DIM ** -0.5, 1.0)
        for i in range(hp):
            sl = slice(i * LANES, (i + 1) * LANES)
            w = w_ref[:, sl]
            xs = _conv_taps(x_ref[:, sl], rows)
            z, sg, s = _conv_silu(xs, w)
            rs = lax.rsqrt(jnp.sum(s * s, axis=-1, keepdims=True) + EPS)
            dyv = dy_ref[:, sl]
            nv = s * rs
            de = dyv * qscale
            ds_qk = rs * (de - nv * jnp.sum(de * nv, axis=-1, keepdims=True))
            ds = jnp.where(j < 2 * nh, ds_qk, dyv)
            dz = ds * sg * (1.0 + z * (1.0 - sg))
            dx = w[GDN_CONV - 1:GDN_CONV, :] * dz
            dw_ref[GDN_CONV - 1:GDN_CONV, sl] = jnp.sum(dz * xs[GDN_CONV - 1], axis=0, keepdims=True)
            for k in range(GDN_CONV - 1):
                dx = dx + w[k:k + 1, :] * _unshift_rows(dz, GDN_CONV - 1 - k, rows, t)
                dw_ref[k:k + 1, sl] = jnp.sum(dz * xs[k], axis=0, keepdims=True)
            dx_ref[:, sl] = dx.astype(dx_ref.dtype)

    return pl.pallas_call(
        body, name=name, grid=(3 * nh // hp,),
        in_specs=[pl.BlockSpec((t, wd), lambda j: (0, j)), pl.BlockSpec((GDN_CONV, wd), lambda j: (0, j)),
                  pl.BlockSpec((None, t, wd), lambda j: (j // per_seg, 0, j % per_seg))],
        out_specs=(pl.BlockSpec((t, wd), lambda j: (0, j)), pl.BlockSpec((GDN_CONV, wd), lambda j: (0, j))),
        out_shape=(jax.ShapeDtypeStruct((t, 3 * GDN_KEY_DIM), BF16),
                   jax.ShapeDtypeStruct((GDN_CONV, 3 * GDN_KEY_DIM), F32)),
        compiler_params=_params("parallel"),
    )(proj, conv_wt, dy)


def _softplus(z):
    return jnp.maximum(z, 0.0) + jnp.log(1.0 + jnp.exp(-jnp.abs(z)))


def _gdn_gate_fwd(ab, prm, *, name):
    t = ab.shape[0]

    def body(ab_ref, p_ref, o_ref):
        v = ab_ref[...]
        lane = lax.broadcasted_iota(jnp.int32, v.shape, 1)
        g = -jnp.exp(p_ref[0:1, :]) * _softplus(v + p_ref[1:2, :])
        o_ref[...] = jnp.where(lane < GDN_HEADS, g, jnp.where(lane < 2 * GDN_HEADS, _sigmoid(v), 0.0))

    return pl.pallas_call(
        body, name=name, grid=(t // ROWS,),
        in_specs=[_row_spec(LANES), _const_spec((8, LANES))], out_specs=_row_spec(LANES),
        out_shape=jax.ShapeDtypeStruct((t, LANES), F32), compiler_params=_params("parallel"),
    )(ab, prm)


def _gdn_gate_bwd(ab, prm, dgb, *, name):
    t = ab.shape[0]

    def body(ab_ref, p_ref, d_ref, o_ref, st_ref):
        @pl.when(pl.program_id(0) == 0)
        def _():
            st_ref[...] = jnp.zeros_like(st_ref)

        v = ab_ref[...]
        dv = d_ref[...]
        lane = lax.broadcasted_iota(jnp.int32, v.shape, 1)
        is_a = lane < GDN_HEADS
        is_b = jnp.logical_and(lane >= GDN_HEADS, lane < 2 * GDN_HEADS)
        a_exp = jnp.exp(p_ref[0:1, :])
        zz = v + p_ref[1:2, :]
        g = -a_exp * _softplus(zz)
        da = dv * (-a_exp) * _sigmoid(zz)
        beta = _sigmoid(v)
        db = dv * beta * (1.0 - beta)
        o_ref[...] = jnp.where(is_a, da, jnp.where(is_b, db, 0.0)).astype(o_ref.dtype)
        st_ref[0:1, :] += jnp.sum(jnp.where(is_a, dv * g, 0.0), axis=0, keepdims=True)
        st_ref[1:2, :] += jnp.sum(jnp.where(is_a, da, 0.0), axis=0, keepdims=True)

    return pl.pallas_call(
        body, name=name, grid=(t // ROWS,),
        in_specs=[_row_spec(LANES), _const_spec((8, LANES)), _row_spec(LANES)],
        out_specs=(_row_spec(LANES), _const_spec((8, LANES))),
        out_shape=(jax.ShapeDtypeStruct((t, LANES), BF16), jax.ShapeDtypeStruct((8, LANES), F32)),
        compiler_params=_params("arbitrary"),
    )(ab, prm, dgb)


def _gdn_local(qs, ks, vs, gbs, bbs, tinvs=None):
    nh = len(qs)
    cs = qs[0].shape[0]
    hs = range(nh)
    r = lax.broadcasted_iota(jnp.int32, (cs, cs), 0)
    c = lax.broadcasted_iota(jnp.int32, (cs, cs), 1)
    tril, strict, eye = r >= c, r > c, r == c
    ident = jnp.where(eye, 1.0, 0.0)
    g_colb = [gbs[h][:, :cs] for h in hs]
    g_row = [jnp.sum(jnp.where(eye, g_colb[h], 0.0), axis=0, keepdims=True) for h in hs]
    gc_col = [jnp.sum(jnp.where(tril, g_row[h], 0.0), axis=1, keepdims=True) for h in hs]
    gc_row = [jnp.sum(jnp.where(r <= c, g_colb[h], 0.0), axis=0, keepdims=True) for h in hs]
    decay = [jnp.exp(jnp.where(tril, gc_col[h] - gc_row[h], NEG)) for h in hs]
    gamma = [jnp.exp(gc_col[h]) for h in hs]
    gcl = [gc_col[h][cs - 1:cs, :] for h in hs]
    gl = [jnp.exp(gcl[h]) for h in hs]
    kdec = [jnp.exp(gcl[h] - gc_col[h]) for h in hs]
    kb = [ks[h] * bbs[h] for h in hs]
    kk = [_dotb(kb[h], ks[h], NT) for h in hs]
    qk = [_dotb(qs[h], ks[h], NT) for h in hs]
    lmat = [jnp.where(strict, kk[h] * decay[h], 0.0) for h in hs]
    pmat = [jnp.where(tril, qk[h] * decay[h], 0.0) for h in hs]
    if tinvs is None:
        xm = [-lmat[h] for h in hs]
        tinv = [ident + xm[h] for h in hs]
        for _ in range(int(math.log2(cs)) - 1):
            xm = [_dotf(xm[h], xm[h], NN) for h in hs]
            tinv = [tinv[h] + _dotf(tinv[h], xm[h], NN) for h in hs]
    else:
        tinv = tinvs
    vb = [vs[h] * bbs[h] for h in hs]
    kg = [kb[h] * gamma[h] for h in hs]
    u = [_dotf(tinv[h], vb[h], NN) for h in hs]
    w = [_dotf(tinv[h], kg[h], NN) for h in hs]
    return [dict(tril=tril, strict=strict, eye=eye, r=r, c=c, decay=decay[h], gamma=gamma[h], gl=gl[h], kdec=kdec[h],
                 kb=kb[h], lmat=lmat[h], tinv=tinv[h], vb=vb[h], kg=kg[h], u=u[h], w=w[h], pmat=pmat[h],
                 qd=qs[h] * gamma[h], kd=ks[h] * kdec[h]) for h in hs]


def _head_columns(gbeta, cs):
    gbs = [jnp.broadcast_to(gbeta[:, h:h + 1], (cs, LANES)) for h in range(GDN_HEADS)]
    bbs = [jnp.broadcast_to(gbeta[:, GDN_HEADS + h:GDN_HEADS + h + 1], (cs, LANES)) for h in range(GDN_HEADS)]
    return gbs, bbs


def _gdn_chunk_fwd(qkv, gbeta, *, name):
    t = qkv.shape[0]
    nh, cs, hd = GDN_HEADS, GDN_CHUNK, GDN_HEAD_DIM
    nc = t // cs

    hb = GDN_HEAD_BATCH
    ng = nh // hb
    assert ng == 1

    def body(q_ref, k_ref, v_ref, gb_ref, o_ref, st_ref, ti_ref, s_ref):
        @pl.when(pl.program_id(1) == 0)
        def _():
            s_ref[...] = jnp.zeros_like(s_ref)

        sls = [slice(i * hd, (i + 1) * hd) for i in range(hb)]
        hs = range(hb)
        s = [s_ref[i] for i in hs]
        gbs, bbs = _head_columns(gb_ref[...], cs)
        lo = _gdn_local([q_ref[:, sl] for sl in sls], [k_ref[:, sl] for sl in sls], [v_ref[:, sl] for sl in sls],
                        gbs, bbs)
        ws = [_dotb(lo[i]["w"], s[i], NN) for i in hs]
        qs = [_dotb(lo[i]["qd"], s[i], NN) for i in hs]
        vn = [lo[i]["u"] - ws[i] for i in hs]
        pv = [_dotb(lo[i]["pmat"], vn[i], NN) for i in hs]
        kv = [_dotb(lo[i]["kd"], vn[i], TN) for i in hs]
        for i, sl in enumerate(sls):
            st_ref[i, 0] = s[i]
            ti_ref[i, 0] = lo[i]["tinv"]
            o_ref[:, sl] = qs[i] + pv[i]
            s_ref[i] = s[i] * lo[i]["gl"] + kv[i]

    col = lambda off: pl.BlockSpec((cs, hb * hd), lambda h, n: (n, off + h))
    return pl.pallas_call(
        body, name=name, grid=(ng, nc),
        in_specs=[col(0), col(ng), col(2 * ng), pl.BlockSpec((cs, LANES), lambda h, n: (n, 0))],
        out_specs=(col(0), pl.BlockSpec((hb, 1, hd, hd), lambda h, n: (h, n, 0, 0)),
                   pl.BlockSpec((hb, 1, cs, cs), lambda h, n: (h, n, 0, 0))),
        out_shape=(jax.ShapeDtypeStruct((t, nh * hd), F32), jax.ShapeDtypeStruct((nh, nc, hd, hd), F32),
                   jax.ShapeDtypeStruct((nh, nc, cs, cs), F32)),
        scratch_shapes=[pltpu.VMEM((hb, hd, hd), F32)],
        compiler_params=_params("parallel", "arbitrary"),
    )(qkv, qkv, qkv, gbeta)


def _gdn_chunk_bwd(qkv, gbeta, states, tinvs, do, *, name):
    t = qkv.shape[0]
    nh, cs, hd = GDN_HEADS, GDN_CHUNK, GDN_HEAD_DIM
    nc = t // cs

    hb = GDN_HEAD_BATCH
    ng = nh // hb
    assert ng == 1

    def heads_bwd(q, k, v, gb, bb, s, ti, dsn, dov):
        hs = range(len(q))
        lo = _gdn_local(q, k, v, gb, bb, ti)
        tril, strict, eye, r, c = lo[0]["tril"], lo[0]["strict"], lo[0]["eye"], lo[0]["r"], lo[0]["c"]
        rowi = lax.broadcasted_iota(jnp.int32, (cs, 1), 0)
        get = lambda name: [lo[h][name] for h in hs]
        decay, gamma, gl, kdec = get("decay"), get("gamma"), get("gl"), get("kdec")
        kb, tinv, w, pmat, kd, qd = get("kb"), get("tinv"), get("w"), get("pmat"), get("kd"), get("qd")
        ws = [_dotb(w[h], s[h], NN) for h in hs]
        pdo = [_dotb(pmat[h], dov[h], TN) for h in hs]
        kds = [_dotb(kd[h], dsn[h], NN) for h in hs]
        dqd = [_dotb(dov[h], s[h], NT) for h in hs]
        qdo = [_dotb(qd[h], dov[h], TN) for h in hs]
        vn = [lo[h]["u"] - ws[h] for h in hs]
        dvn = [pdo[h] + kds[h] for h in hs]
        dp = [jnp.where(tril, _dotb(dov[h], vn[h], NT), 0.0) for h in hs]
        dkd = [_dotb(vn[h], dsn[h], NT) for h in hs]
        dw = [-_dotb(dvn[h], s[h], NT) for h in hs]
        wdv = [_dotb(w[h], dvn[h], TN) for h in hs]
        dvb = [_dotf(tinv[h], dvn[h], TN) for h in hs]
        dt1 = [_dotf(dvn[h], lo[h]["vb"], NT) for h in hs]
        dkg = [_dotf(tinv[h], dw[h], TN) for h in hs]
        dt2 = [_dotf(dw[h], lo[h]["kg"], NT) for h in hs]
        tdt = [_dotf(tinv[h], dt1[h] + dt2[h], TN) for h in hs]
        dl = [jnp.where(strict, -_dotf(tdt[h], tinv[h], NT), 0.0) for h in hs]
        dkk = [dl[h] * decay[h] for h in hs]
        dqk = [dp[h] * decay[h] for h in hs]
        dkb = [_dotb(dkk[h], k[h], NN) + dkg[h] * gamma[h] for h in hs]
        dk1 = [_dotb(dkk[h], kb[h], TN) for h in hs]
        dk2 = [_dotb(dqk[h], q[h], TN) for h in hs]
        dq1 = [_dotb(dqk[h], k[h], NN) for h in hs]
        out = []
        for h in hs:
            dgl = jnp.sum(jnp.sum(dsn[h] * s[h], axis=1, keepdims=True), axis=0, keepdims=True)
            ds_prev = gl[h] * dsn[h] + qdo[h] - wdv[h]
            dk = dk1[h] + dk2[h] + dkd[h] * kdec[h] + dkb[h] * bb[h]
            dq = dq1[h] + dqd[h] * gamma[h]
            dbeta = jnp.sum(dvb[h] * v[h], axis=-1, keepdims=True) + jnp.sum(dkb[h] * k[h], axis=-1, keepdims=True)
            e = dl[h] * lo[h]["lmat"] + dp[h] * pmat[h]
            e_col = jnp.sum(e, axis=0, keepdims=True)
            dgc = jnp.sum(e, axis=1, keepdims=True) - jnp.sum(jnp.where(eye, e_col, 0.0), axis=1, keepdims=True)
            dgamma = (jnp.sum(dqd[h] * q[h], axis=-1, keepdims=True)
                      + jnp.sum(dkg[h] * kb[h], axis=-1, keepdims=True))
            rk = jnp.sum(dkd[h] * k[h], axis=-1, keepdims=True) * kdec[h]
            dgcl = jnp.sum(rk, axis=0, keepdims=True) + dgl * gl[h]
            dgc = dgc + dgamma * gamma[h] - rk + jnp.where(rowi == cs - 1, dgcl, 0.0)
            dgc_row = jnp.sum(jnp.where(eye, dgc, 0.0), axis=0, keepdims=True)
            dg = jnp.sum(jnp.where(c >= r, dgc_row, 0.0), axis=1, keepdims=True)
            out.append((dq, dk, dvb[h] * bb[h], dbeta, dg, ds_prev))
        return out

    def body(q_ref, k_ref, v_ref, gb_ref, st_ref, ti_ref, do_ref, d_ref, dgb_ref, ds_ref):
        @pl.when(pl.program_id(1) == 0)
        def _():
            ds_ref[...] = jnp.zeros_like(ds_ref)

        sls = [slice(i * hd, (i + 1) * hd) for i in range(hb)]
        hs = range(hb)
        gbs, bbs = _head_columns(gb_ref[...], cs)
        outs = heads_bwd([q_ref[:, sl] for sl in sls], [k_ref[:, sl] for sl in sls], [v_ref[:, sl] for sl in sls],
                         gbs, bbs, [st_ref[i, 0] for i in hs],
                         [ti_ref[i, 0] for i in hs], [ds_ref[i] for i in hs], [do_ref[:, sl] for sl in sls])
        lane = lax.broadcasted_iota(jnp.int32, (cs, LANES), 1)
        dgb = jnp.zeros((cs, LANES), F32)
        for i, sl in enumerate(sls):
            dq, dk, dv, dbeta, dg, ds_prev = outs[i]
            d_ref[0, :, sl], d_ref[1, :, sl], d_ref[2, :, sl] = dq, dk, dv
            dgb = jnp.where(lane == i, dg, jnp.where(lane == nh + i, dbeta, dgb))
            ds_ref[i] = ds_prev
        dgb_ref[...] = dgb

    col = lambda off: pl.BlockSpec((cs, hb * hd), lambda h, n: (nc - 1 - n, off + h))
    gspec = pl.BlockSpec((cs, LANES), lambda h, n: (nc - 1 - n, 0))
    return pl.pallas_call(
        body, name=name, grid=(ng, nc),
        in_specs=[col(0), col(ng), col(2 * ng), gspec,
                  pl.BlockSpec((hb, 1, hd, hd), lambda h, n: (h, nc - 1 - n, 0, 0)),
                  pl.BlockSpec((hb, 1, cs, cs), lambda h, n: (h, nc - 1 - n, 0, 0)), col(0)],
        out_specs=(pl.BlockSpec((3, cs, hb * hd), lambda h, n: (0, nc - 1 - n, h)), gspec),
        out_shape=(jax.ShapeDtypeStruct((3, t, nh * hd), F32), jax.ShapeDtypeStruct((t, LANES), F32)),
        scratch_shapes=[pltpu.VMEM((hb, hd, hd), F32)],
        compiler_params=_params("parallel", "arbitrary"),
    )(qkv, qkv, qkv, gbeta, states, tinvs, do)


def _gdn_onorm_fwd(o, proj, norm_g, *, name):
    t = o.shape[0]
    w = GDN_KEY_DIM
    goff = 3 * GDN_KEY_DIM // w

    def body(o_ref, gp_ref, g_ref, y_ref):
        gv = g_ref[...]
        for h in range(GDN_HEADS):
            sl = slice(h * GDN_HEAD_DIM, (h + 1) * GDN_HEAD_DIM)
            oh = o_ref[:, sl]
            gp = gp_ref[:, sl]
            r = lax.rsqrt(jnp.mean(oh * oh, axis=-1, keepdims=True) + EPS)
            y_ref[:, sl] = (oh * r * gv * gp * _sigmoid(gp)).astype(y_ref.dtype)

    return pl.pallas_call(
        body, name=name, grid=(t // ROWS,),
        in_specs=[_row_spec(w), pl.BlockSpec((ROWS, w), lambda i: (i, goff)), _const_spec((1, GDN_HEAD_DIM))],
        out_specs=_row_spec(w), out_shape=jax.ShapeDtypeStruct((t, w), BF16),
        compiler_params=_params("parallel"),
    )(o, proj, norm_g)


def _gdn_onorm_bwd(o, proj, norm_g, dy, *, name):
    t = o.shape[0]
    w = GDN_KEY_DIM
    goff = 3 * GDN_KEY_DIM // w

    def body(o_ref, gp_ref, g_ref, dy_ref, do_ref, dgp_ref, st_ref):
        @pl.when(pl.program_id(0) == 0)
        def _():
            st_ref[...] = jnp.zeros_like(st_ref)

        gv = g_ref[...]
        acc = jnp.zeros((1, GDN_HEAD_DIM), F32)
        for h in range(GDN_HEADS):
            sl = slice(h * GDN_HEAD_DIM, (h + 1) * GDN_HEAD_DIM)
            oh = o_ref[:, sl]
            gp = gp_ref[:, sl]
            dyv = dy_ref[:, sl].astype(F32)
            r = lax.rsqrt(jnp.mean(oh * oh, axis=-1, keepdims=True) + EPS)
            xh = oh * r
            sg = _sigmoid(gp)
            dn = dyv * gp * sg
            dgp_ref[:, sl] = (dyv * xh * gv * sg * (1.0 + gp * (1.0 - sg))).astype(dgp_ref.dtype)
            acc = acc + jnp.sum(dn * xh, axis=0, keepdims=True)
            dxh = dn * gv
            do_ref[:, sl] = r * (dxh - xh * jnp.mean(dxh * xh, axis=-1, keepdims=True))
        st_ref[0:1, :] += acc

    return pl.pallas_call(
        body, name=name, grid=(t // ROWS,),
        in_specs=[_row_spec(w), pl.BlockSpec((ROWS, w), lambda i: (i, goff)), _const_spec((1, GDN_HEAD_DIM)),
                  _row_spec(w)],
        out_specs=(_row_spec(w), _row_spec(w), _const_spec((8, GDN_HEAD_DIM))),
        out_shape=(jax.ShapeDtypeStruct((t, w), F32), jax.ShapeDtypeStruct((t, w), BF16),
                   jax.ShapeDtypeStruct((8, GDN_HEAD_DIM), F32)),
        compiler_params=_params("arbitrary"),
    )(o, proj, norm_g, dy)


def _mla_prep_fwd(proj, qg, kvg, *, name):
    t = proj.shape[0]
    q1, k1 = MLA_Q_RANK, MLA_Q_RANK + MLA_KV_RANK

    def body(p_ref, qg_ref, kg_ref, cq_ref, ck_ref):
        cq = p_ref[:, 0:q1]
        ck = p_ref[:, q1:k1]
        cq_ref[...] = (cq * lax.rsqrt(jnp.mean(cq * cq, axis=-1, keepdims=True) + EPS) * qg_ref[...]).astype(BF16)
        ck_ref[...] = (ck * lax.rsqrt(jnp.mean(ck * ck, axis=-1, keepdims=True) + EPS) * kg_ref[...]).astype(BF16)

    return pl.pallas_call(
        body, name=name, grid=(t // ROWS,),
        in_specs=[_row_spec(MLA_IN), _const_spec((1, MLA_Q_RANK)), _const_spec((1, MLA_KV_RANK))],
        out_specs=(_row_spec(MLA_Q_RANK), _row_spec(MLA_KV_RANK)),
        out_shape=(jax.ShapeDtypeStruct((t, MLA_Q_RANK), BF16), jax.ShapeDtypeStruct((t, MLA_KV_RANK), BF16)),
        compiler_params=_params("parallel"),
    )(proj, qg, kvg)


def _mla_prep_bwd(proj, qg, kvg, dcq, dck, dkr, *, name):
    t = proj.shape[0]
    q1, k1 = MLA_Q_RANK, MLA_Q_RANK + MLA_KV_RANK

    def body(p_ref, qg_ref, kg_ref, dq_ref, dk_ref, dr_ref, dp_ref, st_ref):
        @pl.when(pl.program_id(0) == 0)
        def _():
            st_ref[...] = jnp.zeros_like(st_ref)

        for lo, hi, g_ref, d_ref in ((0, q1, qg_ref, dq_ref), (q1, k1, kg_ref, dk_ref)):
            xv = p_ref[:, lo:hi]
            dn = d_ref[...]
            r = lax.rsqrt(jnp.mean(xv * xv, axis=-1, keepdims=True) + EPS)
            xh = xv * r
            dxh = dn * g_ref[...]
            dp_ref[:, lo:hi] = (r * (dxh - xh * jnp.mean(dxh * xh, axis=-1, keepdims=True))).astype(dp_ref.dtype)
            st_ref[0:1, lo:hi] += jnp.sum(dn * xh, axis=0, keepdims=True)
        dp_ref[:, k1:MLA_IN] = dr_ref[:, 0:MLA_ROPE].astype(dp_ref.dtype)

    return pl.pallas_call(
        body, name=name, grid=(t // ROWS,),
        in_specs=[_row_spec(MLA_IN), _const_spec((1, MLA_Q_RANK)), _const_spec((1, MLA_KV_RANK)),
                  _row_spec(MLA_Q_RANK), _row_spec(MLA_KV_RANK), _row_spec(LANES)],
        out_specs=(_row_spec(MLA_IN), _const_spec((8, MLA_IN))),
        out_shape=(jax.ShapeDtypeStruct((t, MLA_IN), BF16), jax.ShapeDtypeStruct((8, MLA_IN), F32)),
        compiler_params=_params("arbitrary"),
    )(proj, qg, kvg, dcq, dck, dkr)


ATT_BLOCK = 256
ATT_HEAD_BATCH = 8
ATT_HEAD_BATCH_BWD = 4
ATT_SCALE = MLA_QK ** -0.5


def _diagonal_mask(blk):
    return lax.broadcasted_iota(jnp.int32, (blk, blk), 1) <= lax.broadcasted_iota(jnp.int32, (blk, blk), 0)


def _swap_halves(xv, first):
    return jnp.where(first, pltpu.roll(xv, LANES - MLA_ROPE // 2, 1), pltpu.roll(xv, MLA_ROPE // 2, 1))


def _rope_qk(qf, proj, cos_t, sin_t, *, name):
    t = qf.shape[0]
    nrope = MLA_HEADS * MLA_ROPE
    q_blk = MLA_HEADS * MLA_NOPE // nrope
    k_blk = (MLA_Q_RANK + MLA_KV_RANK) // LANES

    def body(q_ref, p_ref, c_ref, s_ref, qo_ref, ko_ref):
        cv, sv = c_ref[...], s_ref[...]
        lane = lax.broadcasted_iota(jnp.int32, (ROWS, LANES), 1)
        first = (lane % MLA_ROPE) < (MLA_ROPE // 2)
        for i in range(nrope // LANES):
            sl = slice(i * LANES, (i + 1) * LANES)
            xv = q_ref[:, sl].astype(F32)
            qo_ref[:, sl] = (xv * cv + _swap_halves(xv, first) * sv).astype(qo_ref.dtype)
        kv = jnp.where(lane < MLA_ROPE, p_ref[...], 0.0)
        ko_ref[...] = (kv * cv + _swap_halves(kv, first) * sv).astype(ko_ref.dtype)

    return pl.pallas_call(
        body, name=name, grid=(t // ROWS,),
        in_specs=[pl.BlockSpec((ROWS, nrope), lambda i: (i, q_blk)), pl.BlockSpec((ROWS, LANES), lambda i: (i, k_blk)),
                  _row_spec(LANES), _row_spec(LANES)],
        out_specs=(_row_spec(nrope), _row_spec(LANES)),
        out_shape=(jax.ShapeDtypeStruct((t, nrope), BF16), jax.ShapeDtypeStruct((t, LANES), BF16)),
        compiler_params=_params("parallel"),
    )(qf, proj, cos_t, sin_t)


def _rope_qk_bwd(dqr, dkr_parts, cos_t, sin_t, *, name):
    t, nrope = dqr.shape
    ng = dkr_parts.shape[0]

    def body(d_ref, k_ref, c_ref, s_ref, qo_ref, ko_ref):
        cv, sv = c_ref[...], s_ref[...]
        lane = lax.broadcasted_iota(jnp.int32, (ROWS, LANES), 1)
        first = (lane % MLA_ROPE) < (MLA_ROPE // 2)
        for i in range(nrope // LANES):
            sl = slice(i * LANES, (i + 1) * LANES)
            dv = d_ref[:, sl]
            qo_ref[:, sl] = (dv * cv + _swap_halves(dv * sv, first)).astype(qo_ref.dtype)
        dk = k_ref[0]
        for g in range(1, ng):
            dk = dk + k_ref[g]
        dk = jnp.where(lane < MLA_ROPE, dk, 0.0)
        ko_ref[...] = jnp.where(lane < MLA_ROPE, dk * cv + _swap_halves(dk * sv, first), 0.0)

    return pl.pallas_call(
        body, name=name, grid=(t // ROWS,),
        in_specs=[_row_spec(nrope), pl.BlockSpec((ng, ROWS, LANES), lambda i: (0, i, 0)), _row_spec(LANES),
                  _row_spec(LANES)],
        out_specs=(_row_spec(nrope), _row_spec(LANES)),
        out_shape=(jax.ShapeDtypeStruct((t, nrope), BF16), jax.ShapeDtypeStruct((t, LANES), F32)),
        compiler_params=_params("parallel"),
    )(dqr, dkr_parts, cos_t, sin_t)


def _attn_tm_fwd(qf, qr, kvf, kr, *, name):
    t = qf.shape[0]
    nh, dn, dr, dv = MLA_HEADS, MLA_NOPE, MLA_ROPE, MLA_V
    blk = min(ATT_BLOCK, t)
    hb = ATT_HEAD_BATCH
    hs = range(hb)

    def body(q_ref, qr_ref, kv_ref, kr_ref, o_ref, l_ref):
        i = pl.program_id(1)
        qc = [jnp.concatenate([q_ref[:, h * dn:(h + 1) * dn].astype(MXU_DTYPE), qr_ref[:, h * dr:(h + 1) * dr]], axis=1)
              for h in hs]

        def step(j, carry, diagonal=False):
            m, l, acc = carry[:hb], carry[hb:2 * hb], carry[2 * hb:]
            rows = pl.ds(pl.multiple_of(j * blk, blk), blk)
            krj = kr_ref[rows, 0:dr]
            s = [_dotb(qc[h], jnp.concatenate([kv_ref[rows, h * (dn + dv):h * (dn + dv) + dn], krj], axis=1), NT)
                 for h in hs]
            s = [s[h] * ATT_SCALE for h in hs]
            if diagonal:
                mask = _diagonal_mask(blk)
                s = [jnp.where(mask, s[h], NEG) for h in hs]
            m_new = [jnp.maximum(m[h], jnp.max(s[h], axis=-1, keepdims=True)) for h in hs]
            p = [jnp.exp(s[h] - m_new[h]) for h in hs]
            pv = [_dotb(p[h], kv_ref[rows, h * (dn + dv) + dn:(h + 1) * (dn + dv)], NN) for h in hs]
            alpha = [jnp.exp(m[h] - m_new[h]) for h in hs]
            l = [alpha[h] * l[h] + jnp.sum(p[h], axis=-1, keepdims=True) for h in hs]
            acc = [alpha[h] * acc[h] + pv[h] for h in hs]
            return tuple(m_new) + tuple(l) + tuple(acc)

        init = ((jnp.full((blk, 1), NEG, F32),) * hb + (jnp.zeros((blk, 1), F32),) * hb
                + (jnp.zeros((blk, dv), F32),) * hb)
        out = step(i, lax.fori_loop(0, i, step, init), diagonal=True)
        for h in hs:
            m, l, acc = out[h], out[hb + h], out[2 * hb + h]
            o_ref[:, h * dv:(h + 1) * dv] = (acc / l).astype(o_ref.dtype)
            l_ref[h] = jnp.broadcast_to(m + jnp.log(l), (blk, LANES))

    return pl.pallas_call(
        body, name=name, grid=(nh // hb, t // blk),
        in_specs=[pl.BlockSpec((blk, hb * dn), lambda g, i: (i, g)), pl.BlockSpec((blk, hb * dr), lambda g, i: (i, g)),
                  pl.BlockSpec((t, hb * (dn + dv)), lambda g, i: (0, g)), pl.BlockSpec((t, LANES), lambda g, i: (0, 0))],
        out_specs=(pl.BlockSpec((blk, hb * dv), lambda g, i: (i, g)),
                   pl.BlockSpec((hb, blk, LANES), lambda g, i: (g, i, 0))),
        out_shape=(jax.ShapeDtypeStruct((t, nh * dv), BF16), jax.ShapeDtypeStruct((nh, t, LANES), F32)),
        compiler_params=_params("parallel", "parallel"),
    )(qf, qr, kvf, kr)


def _attn_tm_bwd(qf, qr, kvf, kr, o, lse, do, *, name):
    t = qf.shape[0]
    nh, dn, dr, dv = MLA_HEADS, MLA_NOPE, MLA_ROPE, MLA_V
    blk = min(ATT_BLOCK, t)
    nb = t // blk
    hb = ATT_HEAD_BATCH_BWD
    hs = range(hb)
    ng = nh // hb

    def body(q_ref, qr_ref, kv_ref, kr_ref, o_ref, l_ref, do_ref, dqn_ref, dqr_ref, dkv_ref, dkr_ref):
        j = pl.program_id(1)

        @pl.when(j == 0)
        def _():
            dqn_ref[...] = jnp.zeros_like(dqn_ref)
            dqr_ref[...] = jnp.zeros_like(dqr_ref)

        krj = kr_ref[:, 0:dr]
        kc = [jnp.concatenate([kv_ref[:, h * (dn + dv):h * (dn + dv) + dn], krj], axis=1) for h in hs]
        vv = [kv_ref[:, h * (dn + dv) + dn:(h + 1) * (dn + dv)] for h in hs]

        def step(i, carry, diagonal=False):
            dkn_acc, dv_acc, dkr_acc = carry[:hb], carry[hb:2 * hb], carry[2 * hb]
            rows = pl.ds(pl.multiple_of(i * blk, blk), blk)
            qc = [jnp.concatenate([q_ref[rows, h * dn:(h + 1) * dn].astype(MXU_DTYPE),
                                   qr_ref[rows, h * dr:(h + 1) * dr]], axis=1) for h in hs]
            dov = [do_ref[rows, h * dv:(h + 1) * dv] for h in hs]
            s = [_dotb(qc[h], kc[h], NT) for h in hs]
            dp = [_dotb(dov[h], vv[h], NT) for h in hs]
            s = [s[h] * ATT_SCALE for h in hs]
            if diagonal:
                mask = _diagonal_mask(blk)
                s = [jnp.where(mask, s[h], NEG) for h in hs]
            p = [jnp.exp(s[h] - l_ref[h, rows, :][:, 0:1]) for h in hs]
            delta = [jnp.sum(dov[h].astype(F32) * o_ref[rows, h * dv:(h + 1) * dv].astype(F32), axis=-1, keepdims=True)
                     for h in hs]
            ds = [p[h] * (dp[h] - delta[h]) * ATT_SCALE for h in hs]
            dvn = [_dotb(p[h], dov[h], TN) for h in hs]
            dkc = [_dotb(ds[h], qc[h], TN) for h in hs]
            dqc = [_dotb(ds[h], kc[h], NN) for h in hs]
            for h in hs:
                dqn_ref[rows, h * dn:(h + 1) * dn] += dqc[h][:, 0:dn]
                dqr_ref[rows, h * dr:(h + 1) * dr] += dqc[h][:, dn:dn + dr]
            dkr_new = dkr_acc
            for h in hs:
                dkr_new = dkr_new + dkc[h][:, dn:dn + dr]
            return (tuple(dkn_acc[h] + dkc[h][:, 0:dn] for h in hs) + tuple(dv_acc[h] + dvn[h] for h in hs)
                    + (dkr_new,))

        init = (jnp.zeros((blk, dn), F32),) * hb + (jnp.zeros((blk, dv), F32),) * hb + (jnp.zeros((blk, dr), F32),)
        out = lax.fori_loop(j + 1, nb, step, step(j, init, diagonal=True))
        for h in hs:
            dkv_ref[:, h * (dn + dv):h * (dn + dv) + dn] = out[h].astype(dkv_ref.dtype)
            dkv_ref[:, h * (dn + dv) + dn:(h + 1) * (dn + dv)] = out[hb + h].astype(dkv_ref.dtype)
        dkr_ref[0, :, 0:dr] = out[2 * hb]
        dkr_ref[0, :, dr:LANES] = jnp.zeros((blk, LANES - dr), F32)

    full = lambda w: pl.BlockSpec((t, w), lambda g, j: (0, g))
    return pl.pallas_call(
        body, name=name, grid=(ng, nb),
        in_specs=[full(hb * dn), full(hb * dr), pl.BlockSpec((blk, hb * (dn + dv)), lambda g, j: (j, g)),
                  pl.BlockSpec((blk, LANES), lambda g, j: (j, 0)), full(hb * dv),
                  pl.BlockSpec((hb, t, LANES), lambda g, j: (g, 0, 0)), full(hb * dv)],
        out_specs=(full(hb * dn), full(hb * dr), pl.BlockSpec((blk, hb * (dn + dv)), lambda g, j: (j, g)),
                   pl.BlockSpec((1, blk, LANES), lambda g, j: (g, j, 0))),
        out_shape=(jax.ShapeDtypeStruct((t, nh * dn), F32), jax.ShapeDtypeStruct((t, nh * dr), F32),
                   jax.ShapeDtypeStruct((t, nh * (dn + dv)), BF16), jax.ShapeDtypeStruct((ng, t, LANES), F32)),
        compiler_params=_params("parallel", "arbitrary"),
    )(qf, qr, kvf, kr, o, lse, do)


def _ada_mod(c_all, ada_w, ada_b_cols, *, name):
    nl, d, wc = ada_w.shape

    def body(c_ref, w_ref, b_ref, o_ref):
        cv = c_ref[...]
        o_ref[0] = _dotb(cv * _sigmoid(cv), w_ref[0], NN) + b_ref[0]

    return pl.pallas_call(
        body, name=name, grid=(nl,),
        in_specs=[_const_spec((N_DEV, d)), pl.BlockSpec((1, d, wc), lambda l: (l, 0, 0)),
                  pl.BlockSpec((1, 1, wc), lambda l: (l, 0, 0))],
        out_specs=pl.BlockSpec((1, N_DEV, wc), lambda l: (l, 0, 0)),
        out_shape=jax.ShapeDtypeStruct((nl, N_DEV, wc), F32), compiler_params=_params("parallel"),
    )(c_all, ada_w, ada_b_cols)


def _adam_math(g, w, m, v):
    m2 = ADAM_B1 * m + (1.0 - ADAM_B1) * g
    v2 = ADAM_B2 * v + (1.0 - ADAM_B2) * (g * g)
    delta = -ADAM_LR * ((m2 / ADAM_BC1) / (jnp.sqrt(v2 / ADAM_BC2) + ADAM_EPS) + ADAM_WD * w)
    return delta, m2, v2


def _ada_grad_adamw(c_all, dmod_cols, w, m, v, *, name):
    nl, d, wc = w.shape
    tr = 256

    def body(c_ref, dm_ref, w_ref, m_ref, v_ref, g_ref, d_ref, m2_ref, v2_ref):
        cv = c_ref[...]
        g = _dotf(cv * _sigmoid(cv), dm_ref[0], TN)
        delta, m2, v2 = _adam_math(g, w_ref[0], m_ref[0], v_ref[0])
        g_ref[0], d_ref[0], m2_ref[0], v2_ref[0] = g, delta, m2, v2

    blk = pl.BlockSpec((1, tr, wc), lambda l, i: (l, i, 0))
    return pl.pallas_call(
        body, name=name, grid=(nl, d // tr),
        in_specs=[pl.BlockSpec((N_DEV, tr), lambda l, i: (0, i)), pl.BlockSpec((1, N_DEV, wc), lambda l, i: (l, 0, 0)),
                  blk, blk, blk],
        out_specs=(blk,) * 4, out_shape=(jax.ShapeDtypeStruct(w.shape, F32),) * 4,
        compiler_params=_params("parallel", "parallel"),
    )(c_all, dmod_cols, w, m, v)


def _adamw(parts, w, m, v, *, name):
    nl, r, c = w.shape
    ns = parts[0].shape[0]
    lanes_padded = -(-c // LANES) * LANES
    row_bytes = 2 * nl * ns * lanes_padded * parts[0].dtype.itemsize
    tr = _pick(r, min(256, max(16, (VMEM_LIMIT // 2) // row_bytes)), 16)
    tc = c
    if tr * row_bytes > VMEM_LIMIT // 2:
        tc = _pick(c, max(LANES, c * (VMEM_LIMIT // 2) // (tr * row_bytes)))

    def body(*refs):
        p_refs = refs[:nl]
        w_ref, m_ref, v_ref, g_ref, d_ref, m2_ref, v2_ref = refs[nl:]
        layer = pl.program_id(0)
        for q in range(nl):
            @pl.when(layer == q)
            def _(q=q):
                g = p_refs[q][0].astype(F32)
                for s in range(1, ns):
                    g = g + p_refs[q][s].astype(F32)
                delta, m2, v2 = _adam_math(g, w_ref[0], m_ref[0], v_ref[0])
                g_ref[0], d_ref[0], m2_ref[0], v2_ref[0] = g, delta, m2, v2

    blk = pl.BlockSpec((1, tr, tc), lambda l, i, j: (l, i, j))
    p_specs = [pl.BlockSpec((ns, tr, tc), lambda l, i, j, q=q: (0, jnp.where(l == q, i, 0), jnp.where(l == q, j, 0)))
               for q in range(nl)]
    return pl.pallas_call(
        body, name=name, grid=(nl, r // tr, c // tc),
        in_specs=p_specs + [blk, blk, blk],
        out_specs=(blk,) * 4, out_shape=(jax.ShapeDtypeStruct(w.shape, F32),) * 4,
        compiler_params=_params("arbitrary", "arbitrary", "arbitrary"),
    )(*parts, w, m, v)


def _sum_parts(parts, *, name):
    ns, r, c = parts.shape

    def body(p_ref, o_ref):
        acc = p_ref[0]
        for s in range(1, ns):
            acc = acc + p_ref[s]
        o_ref[...] = acc

    return pl.pallas_call(
        body, name=name, out_shape=jax.ShapeDtypeStruct((r, c), F32),
        in_specs=[pl.BlockSpec(memory_space=pltpu.VMEM)], out_specs=pl.BlockSpec(memory_space=pltpu.VMEM),
    )(parts)


def _pack(arrs):
    flat = jnp.concatenate([a.reshape(-1).astype(F32) for a in arrs])
    pad = (-flat.shape[0]) % (8 * LANES)
    return jnp.pad(flat, (0, pad)).reshape(-1, LANES)


def _unpack(packed, shapes, lead=()):
    flat = packed.reshape(lead + (-1,))
    out, off = [], 0
    for s in shapes:
        n = math.prod(s)
        out.append(flat[..., off:off + n].reshape(lead + tuple(s)))
        off += n
    return out


def _gather_rows(g):
    _, nl, rs, c = g.shape
    return jnp.transpose(g, (1, 0, 2, 3)).reshape(nl, N_DEV * rs, c)


def _row(v):
    return v.reshape(1, -1)


def _local_step(x, target, mod, cos_t, sin_t, rep, get_weights, put_grads):
    t = x.shape[0]
    saved = []
    for layer in range(DEPTH):
        j = layer // 2
        tag = f"l{layer}"
        shift_m, scale_m, gate_m, shift_f, scale_f, gate_f = [_row(mod[layer, i]) for i in range(N_MOD)]
        lw = dict(get_weights(layer, "mix", x))
        rec = {"x0": x, "lw": lw}
        h = _adaln_fwd(x, _row(rep["norm_mix_g"][layer]), scale_m, shift_m, name=f"adaln_mix_{tag}")
        rec["h"] = h
        if layer % 2 == 0:
            proj = _mm(h, lw["wt_in"], mode="nt", out_dtype=F32, tm=256, tn=GDN_MAIN, b_rows=GDN_MAIN,
                       dep=lw["dep_mix"], name=f"gdn_in_{tag}")
            ab = _mm(h, lw["wt_ab"], mode="nt", out_dtype=F32, name=f"gdn_in_ab_{tag}")
            qkv = _gdn_prep_fwd(proj, rep["gdn_conv_wt"][j], name=f"gdn_prep_{tag}")
            gbeta = _gdn_gate_fwd(ab, rep["gdn_gate_prm"][j], name=f"gdn_gate_{tag}")
            o, states, tinvs = _gdn_chunk_fwd(qkv, gbeta, name=f"gdn_chunk_{tag}")
            og = _gdn_onorm_fwd(o, proj, _row(rep["gdn_norm_g"][j]), name=f"gdn_onorm_{tag}")
            x, y = _mm_resid(og, lw["w_out"], x, gate_m, name=f"gdn_out_{tag}")
            rec.update(proj=proj, ab=ab, qkv=qkv, gbeta=gbeta, states=states, tinvs=tinvs, o=o, og=og, y=y)
        else:
            proj = _mm(h, lw["w_in"], mode="nn", out_dtype=F32, dep=lw["dep_mix"], name=f"mla_in_{tag}")
            cq, ck = _mla_prep_fwd(proj, _row(rep["mla_q_norm_g"][j]), _row(rep["mla_kv_norm_g"][j]),
                                   name=f"mla_prep_{tag}")
            qf = _mm(cq, lw["wt_uq"], mode="nt", out_dtype=BF16, name=f"mla_uq_{tag}")
            kvf = _mm(ck, lw["w_ukv"], mode="nn", out_dtype=BF16, name=f"mla_ukv_{tag}")
            qr, kr = _rope_qk(qf, proj, cos_t, sin_t, name=f"rope_{tag}")
            oc, lse = _attn_tm_fwd(qf, qr, kvf, kr, name=f"attn_{tag}")
            x, y = _mm_resid(oc, lw["w_out"], x, gate_m, name=f"mla_out_{tag}")
            rec.update(proj=proj, cq=cq, ck=ck, qf=qf, qr=qr, kvf=kvf, kr=kr, lse=lse, oc=oc, y=y)
        rec["x1"] = x
        lw.update(get_weights(layer, "ffn", x))
        h2 = _adaln_fwd(x, _row(rep["norm_ffn_g"][layer]), scale_f, shift_f, name=f"adaln_ffn_{tag}")
        s, a2, b2 = _ffn_gu_fwd(h2, lw["wt_g"], lw["wt_u"], lw["dep_ffn"], name=f"ffn_gu_{tag}")
        x, y2 = _mm_resid(s, lw["w_down"], x, gate_f, tm=512, name=f"ffn_down_{tag}")
        rec.update(h2=h2, a2=a2, b2=b2, s=s, y2=y2)
        saved.append(rec)

    dx, st, ls, dy2 = _loss_head(x, _row(rep["final_norm_g"]), target, saved[-1]["y2"],
                                 _row(mod[DEPTH - 1, N_MOD - 1]), name="loss_head")
    loss = ls[0, 0]
    dgate_f = st[3]
    grads = {"final_norm_g": st[0]}
    per_layer = {k: [None] * DEPTH for k in ("norm_mix_g", "norm_ffn_g")}
    per_gdn = {k: [None] * 2 for k in ("gdn_conv_wt", "gdn_a_log", "gdn_dt_bias", "gdn_norm_g")}
    per_mla = {k: [None] * 2 for k in ("mla_q_norm_g", "mla_kv_norm_g")}
    dmod = [None] * DEPTH

    for layer in reversed(range(DEPTH)):
        j = layer // 2
        tag = f"l{layer}"
        rec = saved[layer]
        lw = rec["lw"]
        shift_m, scale_m, gate_m, shift_f, scale_f, gate_f = [_row(mod[layer, i]) for i in range(N_MOD)]
        dw_down = _mm(rec["s"], dy2, mode="tn", out_dtype=BF16, tm=FFN_BLOCK, tn=1024, name=f"ffn_down_dw_{tag}")
        da2, db2 = _ffn_down_dx(dy2, lw["w_down"], rec["a2"], rec["b2"], name=f"ffn_down_dx_{tag}")
        dwt_g = _mm(da2, rec["h2"], mode="tn", out_dtype=BF16, tm=FFN_BLOCK, tn=1024, name=f"ffn_g_dw_{tag}")
        dwt_u = _mm(db2, rec["h2"], mode="tn", out_dtype=BF16, tm=FFN_BLOCK, tn=1024, name=f"ffn_u_dw_{tag}")
        dep = put_grads(layer, "ffn", {"wt_g": dwt_g, "wt_u": dwt_u, "w_down": dw_down})
        dh2 = _mm_pair(da2, lw["wt_g"], db2, lw["wt_u"], out_dtype=BF16, name=f"ffn_gu_dx_{tag}")
        dx, st_n, dy = _adaln_gate_bwd(rec["x1"], _row(rep["norm_ffn_g"][layer]), scale_f, shift_f, dh2, dx, dep,
                                       rec["y"], gate_m, name=f"adaln_ffn_bwd_{tag}")
        per_layer["norm_ffn_g"][layer] = st_n[0]
        dscale_f, dshift_f, dgate_m = st_n[1], st_n[2], st_n[3]
        big = {}
        if layer % 2 == 0:
            big["w_out"] = _mm(rec["og"], dy, mode="tn", out_dtype=BF16, name=f"gdn_out_dw_{tag}")
            dog = _mm(dy, lw["w_out"], mode="nt", out_dtype=BF16, name=f"gdn_out_dx_{tag}")
            do, dgp, st_o = _gdn_onorm_bwd(rec["o"], rec["proj"], _row(rep["gdn_norm_g"][j]), dog,
                                           name=f"gdn_onorm_bwd_{tag}")
            per_gdn["gdn_norm_g"][j] = st_o[0]
            dqkv, dgb = _gdn_chunk_bwd(rec["qkv"], rec["gbeta"], rec["states"], rec["tinvs"], do,
                                       name=f"gdn_chunk_bwd_{tag}")
            dab, st_a = _gdn_gate_bwd(rec["ab"], rep["gdn_gate_prm"][j], dgb, name=f"gdn_gate_bwd_{tag}")
            per_gdn["gdn_a_log"][j] = st_a[0, :GDN_HEADS]
            per_gdn["gdn_dt_bias"][j] = st_a[1, :GDN_HEADS]
            dpre, dcw = _gdn_prep_bwd(rec["proj"], rep["gdn_conv_wt"][j], dqkv, name=f"gdn_prep_bwd_{tag}")
            per_gdn["gdn_conv_wt"][j] = dcw
            dproj = jnp.concatenate([dpre, dgp], axis=1)
            dw_main = _mm(dproj, rec["h"], mode="tn", out_dtype=BF16, tm=512, tn=1024, name=f"gdn_in_dw_{tag}")
            dw_ab = _mm(dab, rec["h"], mode="tn", out_dtype=BF16, tn=1024, name=f"gdn_in_ab_dw_{tag}")
            big["wt_in"] = jnp.concatenate([dw_main, dw_ab[:2 * GDN_HEADS]], axis=0)
            dep = put_grads(layer, "gdn", big)
            dh = _mm_pair(dproj, lw["wt_in"], dab, lw["wt_ab"], out_dtype=BF16, b1_rows=GDN_MAIN,
                          name=f"gdn_in_dx_{tag}")
        else:
            big["w_out"] = _mm(rec["oc"], dy, mode="tn", out_dtype=BF16, name=f"mla_out_dw_{tag}")
            doc = _mm(dy, lw["w_out"], mode="nt", out_dtype=BF16, name=f"mla_out_dx_{tag}")
            dqn, dqr, dkvf, dkr_parts = _attn_tm_bwd(rec["qf"], rec["qr"], rec["kvf"], rec["kr"], rec["oc"],
                                                     rec["lse"], doc, name=f"attn_bwd_{tag}")
            dqr_un, dkr_un = _rope_qk_bwd(dqr, dkr_parts, cos_t, sin_t, name=f"rope_bwd_{tag}")
            n_nope = MLA_HEADS * MLA_NOPE
            big["wt_uq"] = jnp.concatenate(
                [_mm(dqn, rec["cq"], mode="tn", out_dtype=BF16, name=f"mla_uq_dw_nope_{tag}"),
                 _mm(dqr_un, rec["cq"], mode="tn", out_dtype=BF16, name=f"mla_uq_dw_rope_{tag}")], axis=0)
            big["w_ukv"] = _mm(rec["ck"], dkvf, mode="tn", out_dtype=BF16, name=f"mla_ukv_dw_{tag}")
            dcq = _mm_pair(dqn, lw["wt_uq"], dqr_un, lw["wt_uq"][n_nope:], out_dtype=F32, b1_rows=n_nope,
                           name=f"mla_uq_dx_{tag}")
            dck = _mm(dkvf, lw["w_ukv"], mode="nt", out_dtype=F32, name=f"mla_ukv_dx_{tag}")
            dproj, st_p = _mla_prep_bwd(rec["proj"], _row(rep["mla_q_norm_g"][j]), _row(rep["mla_kv_norm_g"][j]),
                                        dcq, dck, dkr_un, name=f"mla_prep_bwd_{tag}")
            per_mla["mla_q_norm_g"][j] = st_p[0, :MLA_Q_RANK]
            per_mla["mla_kv_norm_g"][j] = st_p[0, MLA_Q_RANK:MLA_Q_RANK + MLA_KV_RANK]
            big["w_in"] = _mm(rec["h"], dproj, mode="tn", out_dtype=BF16, name=f"mla_in_dw_{tag}")
            dep = put_grads(layer, "mla", big)
            dh = _mm(dproj, lw["w_in"], mode="nt", out_dtype=BF16, name=f"mla_in_dx_{tag}")
        if layer > 0:
            below = saved[layer - 1]
            dx, st_n, dy2 = _adaln_gate_bwd(rec["x0"], _row(rep["norm_mix_g"][layer]), scale_m, shift_m, dh, dx, dep,
                                            below["y2"], _row(mod[layer - 1, N_MOD - 1]),
                                            name=f"adaln_mix_bwd_{tag}")
        else:
            dx, st_n = _adaln_bwd(rec["x0"], _row(rep["norm_mix_g"][layer]), scale_m, shift_m, dh, dx, dep,
                                  name=f"adaln_mix_bwd_{tag}")
        per_layer["norm_mix_g"][layer] = st_n[0]
        dmod[layer] = jnp.stack([st_n[2], st_n[1], dgate_m, dshift_f, dscale_f, dgate_f])
        if layer > 0:
            dgate_f = st_n[3]

    for d in (per_layer, per_gdn, per_mla):
        for k, v in d.items():
            grads[k] = jnp.stack(v)
    return loss, dx, jnp.stack(dmod), grads


BIG = ("gdn_w_in", "gdn_w_out", "mla_w_in", "mla_w_uq", "mla_w_ukv", "mla_w_out", "ffn_w_gate", "ffn_w_up",
       "ffn_w_down")
TRANSPOSED = ("gdn_w_in", "mla_w_uq", "ffn_w_gate", "ffn_w_up")
AHEAD = 4


def _view(k, a):
    return jnp.transpose(a, (0, 2, 1)) if k in TRANSPOSED else a
SMALL = ("ada_b", "norm_mix_g", "norm_ffn_g", "gdn_conv_w", "gdn_a_log", "gdn_dt_bias", "gdn_norm_g",
         "mla_q_norm_g", "mla_kv_norm_g", "final_norm_g")
WEIGHTS = ("ada_w", "ada_b", "norm_mix_g", "norm_ffn_g", "gdn_w_in", "gdn_conv_w", "gdn_a_log", "gdn_dt_bias",
           "gdn_norm_g", "gdn_w_out", "mla_w_in", "mla_q_norm_g", "mla_kv_norm_g", "mla_w_uq", "mla_w_ukv",
           "mla_w_out", "ffn_w_gate", "ffn_w_up", "ffn_w_down", "final_norm_g")


def _uq_to_kernel_layout(w, axis=-1):
    axis = axis % w.ndim
    lead, tail = w.shape[:axis], w.shape[axis + 1:]
    w4 = w.reshape(lead + (MLA_HEADS, MLA_QK) + tail)
    nope = lax.slice_in_dim(w4, 0, MLA_NOPE, axis=axis + 1).reshape(lead + (-1,) + tail)
    rope = lax.slice_in_dim(w4, MLA_NOPE, MLA_QK, axis=axis + 1).reshape(lead + (-1,) + tail)
    return jnp.concatenate([nope, rope], axis=axis)


def _uq_from_kernel_layout(w, axis=-1):
    axis = axis % w.ndim
    lead, tail = w.shape[:axis], w.shape[axis + 1:]
    nope = lax.slice_in_dim(w, 0, MLA_HEADS * MLA_NOPE, axis=axis).reshape(lead + (MLA_HEADS, MLA_NOPE) + tail)
    rope = lax.slice_in_dim(w, MLA_HEADS * MLA_NOPE, MLA_HEADS * MLA_QK, axis=axis).reshape(
        lead + (MLA_HEADS, MLA_ROPE) + tail)
    return jnp.concatenate([nope, rope], axis=axis + 1).reshape(lead + (-1,) + tail)


def _group_names(layer, kind):
    if kind == "ffn":
        return ("ffn_w_gate", "ffn_w_up", "ffn_w_down")
    return ("gdn_w_in", "gdn_w_out") if layer % 2 == 0 else ("mla_w_in", "mla_w_uq", "mla_w_ukv", "mla_w_out")


def _layer_index(name, layer):
    return layer if name.startswith("ffn") else layer // 2


def _cols(g):
    return jnp.transpose(g, (1, 0, 2)).reshape(g.shape[1], N_DEV * g.shape[2])


def _rows(g):
    return g.reshape(N_DEV * g.shape[1], g.shape[2])


def _uncols(full):
    r, c = full.shape
    return jnp.transpose(full.reshape(r, N_DEV, c // N_DEV), (1, 0, 2))


def _unrows(full):
    r, c = full.shape
    return full.reshape(N_DEV, r // N_DEV, c)


def _group_weights(layer, kind, got, token):
    if kind == "ffn":
        return {"wt_g": _rows(got["ffn_w_gate"]), "wt_u": _rows(got["ffn_w_up"]), "w_down": _rows(got["ffn_w_down"]),
                "dep_ffn": token}
    if layer % 2 == 0:
        wt_in = _rows(got["gdn_w_in"])
        return dict(wt_in=wt_in, wt_ab=jnp.pad(wt_in[GDN_MAIN:], ((0, LANES - 2 * GDN_HEADS), (0, 0))),
                    w_out=_rows(got["gdn_w_out"]), dep_mix=token)
    return dict(w_in=_rows(got["mla_w_in"]), wt_uq=_uq_to_kernel_layout(_rows(got["mla_w_uq"]), axis=0),
                w_ukv=_cols(got["mla_w_ukv"]), w_out=_rows(got["mla_w_out"]), dep_mix=token)


def _layer_grad_slots(kind, big):
    if kind == "ffn":
        return {"ffn_w_gate": _unrows(big["wt_g"]), "ffn_w_up": _unrows(big["wt_u"]),
                "ffn_w_down": _unrows(big["w_down"])}
    if kind == "gdn":
        return {"gdn_w_in": _unrows(big["wt_in"]), "gdn_w_out": _unrows(big["w_out"])}
    return {"mla_w_in": _unrows(big["w_in"]), "mla_w_uq": _unrows(_uq_from_kernel_layout(big["wt_uq"], axis=0)),
            "mla_w_ukv": _uncols(big["w_ukv"]), "mla_w_out": _unrows(big["w_out"])}


def _small_weights(tiny, rep):
    prm = jnp.zeros((2, 8, LANES), F32)
    prm = prm.at[:, 0, :GDN_HEADS].set(rep["gdn_a_log"]).at[:, 1, :GDN_HEADS].set(rep["gdn_dt_bias"])
    out = {
        "gdn_conv_wt": jnp.transpose(_gather_rows(tiny["gdn_conv_w"]), (0, 2, 1)),
        "mla_q_norm_g": jnp.transpose(tiny["mla_q_norm_g"], (1, 0, 2)).reshape(2, MLA_Q_RANK),
        "mla_kv_norm_g": jnp.transpose(tiny["mla_kv_norm_g"], (1, 0, 2)).reshape(2, MLA_KV_RANK),
        "gdn_gate_prm": prm,
    }
    for k in ("norm_mix_g", "norm_ffn_g", "gdn_norm_g", "final_norm_g"):
        out[k] = rep[k]
    return out


def _rope_tables(positions):
    inv_freq = ROPE_THETA ** (-jnp.arange(0, MLA_ROPE, 2, dtype=F32) / MLA_ROPE)
    ang = positions.astype(F32)[:, None] * inv_freq
    cos, sin = jnp.cos(ang), jnp.sin(ang)
    reps = LANES // MLA_ROPE
    return jnp.tile(jnp.concatenate([cos, cos], axis=1), (1, reps)), jnp.tile(
        jnp.concatenate([-sin, sin], axis=1), (1, reps))


def kernel(x, c, positions, ada_w, ada_b, norm_mix_g, norm_ffn_g, gdn_w_in, gdn_conv_w, gdn_a_log, gdn_dt_bias, gdn_norm_g, gdn_w_out, mla_w_in, mla_q_norm_g, mla_kv_norm_g, mla_w_uq, mla_w_ukv, mla_w_out, ffn_w_gate, ffn_w_up, ffn_w_down, final_norm_g, loss_target, m_ada_w, m_ada_b, m_norm_mix_g, m_norm_ffn_g, m_gdn_w_in, m_gdn_conv_w, m_gdn_a_log, m_gdn_dt_bias, m_gdn_norm_g, m_gdn_w_out, m_mla_w_in, m_mla_q_norm_g, m_mla_kv_norm_g, m_mla_w_uq, m_mla_w_ukv, m_mla_w_out, m_ffn_w_gate, m_ffn_w_up, m_ffn_w_down, m_final_norm_g, v_ada_w, v_ada_b, v_norm_mix_g, v_norm_ffn_g, v_gdn_w_in, v_gdn_conv_w, v_gdn_a_log, v_gdn_dt_bias, v_gdn_norm_g, v_gdn_w_out, v_mla_w_in, v_mla_q_norm_g, v_mla_kv_norm_g, v_mla_w_uq, v_mla_w_ukv, v_mla_w_out, v_ffn_w_gate, v_ffn_w_up, v_ffn_w_down, v_final_norm_g):
    W = dict(ada_w=ada_w, ada_b=ada_b, norm_mix_g=norm_mix_g, norm_ffn_g=norm_ffn_g, gdn_w_in=gdn_w_in,
             gdn_conv_w=gdn_conv_w, gdn_a_log=gdn_a_log, gdn_dt_bias=gdn_dt_bias, gdn_norm_g=gdn_norm_g,
             gdn_w_out=gdn_w_out, mla_w_in=mla_w_in, mla_q_norm_g=mla_q_norm_g, mla_kv_norm_g=mla_kv_norm_g,
             mla_w_uq=mla_w_uq, mla_w_ukv=mla_w_ukv, mla_w_out=mla_w_out, ffn_w_gate=ffn_w_gate,
             ffn_w_up=ffn_w_up, ffn_w_down=ffn_w_down, final_norm_g=final_norm_g)
    M = dict(ada_w=m_ada_w, ada_b=m_ada_b, norm_mix_g=m_norm_mix_g, norm_ffn_g=m_norm_ffn_g, gdn_w_in=m_gdn_w_in,
             gdn_conv_w=m_gdn_conv_w, gdn_a_log=m_gdn_a_log, gdn_dt_bias=m_gdn_dt_bias, gdn_norm_g=m_gdn_norm_g,
             gdn_w_out=m_gdn_w_out, mla_w_in=m_mla_w_in, mla_q_norm_g=m_mla_q_norm_g,
             mla_kv_norm_g=m_mla_kv_norm_g, mla_w_uq=m_mla_w_uq, mla_w_ukv=m_mla_w_ukv, mla_w_out=m_mla_w_out,
             ffn_w_gate=m_ffn_w_gate, ffn_w_up=m_ffn_w_up, ffn_w_down=m_ffn_w_down, final_norm_g=m_final_norm_g)
    V = dict(ada_w=v_ada_w, ada_b=v_ada_b, norm_mix_g=v_norm_mix_g, norm_ffn_g=v_norm_ffn_g, gdn_w_in=v_gdn_w_in,
             gdn_conv_w=v_gdn_conv_w, gdn_a_log=v_gdn_a_log, gdn_dt_bias=v_gdn_dt_bias, gdn_norm_g=v_gdn_norm_g,
             gdn_w_out=v_gdn_w_out, mla_w_in=v_mla_w_in, mla_q_norm_g=v_mla_q_norm_g,
             mla_kv_norm_g=v_mla_kv_norm_g, mla_w_uq=v_mla_w_uq, mla_w_ukv=v_mla_w_ukv, mla_w_out=v_mla_w_out,
             ffn_w_gate=v_ffn_w_gate, ffn_w_up=v_ffn_w_up, ffn_w_down=v_ffn_w_down, final_norm_g=v_final_norm_g)
    me = 4 * lax.axis_index("x") + 2 * lax.axis_index("y") + lax.axis_index("c")
    t = x.shape[1]
    wc = ada_w.shape[-1]

    groups = [(layer, kind) for layer in range(DEPTH) for kind in ("mix", "ffn")]

    def group_srcs(i):
        layer, kind = groups[i]
        return [_view(k, W[k])[_layer_index(k, layer)].astype(BF16) for k in _group_names(layer, kind)]

    tiny_shapes = [c.shape, gdn_conv_w.shape, mla_q_norm_g.shape, mla_kv_norm_g.shape]
    first = _gather_two_level([_pack([c, gdn_conv_w, mla_q_norm_g, mla_kv_norm_g])] + group_srcs(0),
                              name="gather_first")
    tiny_g = first[0]
    c_g, conv_g, qn_g, kvn_g = _unpack(tiny_g, tiny_shapes, lead=(N_DEV,))
    c_all = c_g.reshape(N_DEV, D_MODEL)
    rep = _small_weights({"gdn_conv_w": conv_g, "mla_q_norm_g": qn_g, "mla_kv_norm_g": kvn_g}, W)

    def start_group(i, dep):
        layer, kind = groups[i]
        return _exchange_start(group_srcs(i), scatter=False, name=f"gather_start_{kind}_l{layer}", dep=dep)


    b_cols = lax.dynamic_slice_in_dim(ada_b, me * wc, wc, axis=1).reshape(DEPTH, 1, wc)
    mod_part = _ada_mod(c_all, ada_w, b_cols, name="ada_mod")
    (mod_g,) = _exchange([mod_part], scatter=False, name="gather_mod")
    mod_mine = lax.dynamic_index_in_dim(mod_g, me, axis=2, keepdims=False)
    mod = jnp.transpose(mod_mine, (1, 0, 2)).reshape(DEPTH, N_MOD, D_MODEL)
    gather = {1: start_group(1, mod_g)}
    for i in range(2, AHEAD + 1):
        gather[i] = start_group(i, gather[i - 1][4])

    def get_weights(layer, kind, after):
        i = groups.index((layer, kind))
        names = _group_names(layer, kind)
        if i == 0:
            return _group_weights(layer, kind, dict(zip(names, first[1:])), gather[AHEAD][4])
        srcs, lands = _exchange_wait(gather[i], after, scatter=False, name=f"gather_wait_{kind}_l{layer}")
        token = jnp.zeros((8, LANES), F32)
        if i + AHEAD < len(groups):
            gather[i + AHEAD] = start_group(i + AHEAD, lands[0])
            token = gather[i + AHEAD][4]
        got = {k: lax.dynamic_update_index_in_dim(z, s, me, 0) for k, s, z in zip(names, srcs, lands)}
        return _group_weights(layer, kind, got, token)

    scatter = []

    def put_grads(layer, kind, big):
        slots = _layer_grad_slots(kind, big)
        started = _exchange_start(list(slots.values()), scatter=True, name=f"scatter_start_{kind}_l{layer}")
        scatter.append((layer, kind, list(slots.keys()), started))
        return started[4]

    cos_t, sin_t = _rope_tables(positions[0])
    loss, dx, dmod, g = _local_step(x[0], loss_target[0], mod, cos_t, sin_t, rep, get_weights, put_grads)

    parts = {k: [None] * W[k].shape[0] for k in BIG}
    res = {}

    def wait_group(entry, after):
        layer, kind, names, started = entry
        srcs, lands = _exchange_wait(started, after, scatter=True, name=f"scatter_wait_{kind}_l{layer}")
        for k, s, z in zip(names, srcs, lands):
            own = lax.dynamic_index_in_dim(s, me, 0, keepdims=False)
            parts[k][_layer_index(k, layer)] = lax.dynamic_update_index_in_dim(z, own, me, 0)

    for entry in scatter[:-1]:
        wait_group(entry, dx)
    early = [k for k in BIG if k not in scatter[-1][2]]
    def update(k):
        outs = _adamw(parts[k], _view(k, W[k]), _view(k, M[k]), _view(k, V[k]), name=f"adamw_{k}")
        return tuple(_view(k, o) for o in outs)

    for k in early:
        res[k] = update(k)
    loss, dmod, done = lax.optimization_barrier((loss, dmod, [res[k] for k in early]))
    for k, r in zip(early, done):
        res[k] = r

    small_local = [dmod.reshape(DEPTH, N_MOD * D_MODEL), g["norm_mix_g"], g["norm_ffn_g"],
                   jnp.transpose(g["gdn_conv_wt"], (0, 2, 1)), g["gdn_a_log"], g["gdn_dt_bias"], g["gdn_norm_g"],
                   g["mla_q_norm_g"], g["mla_kv_norm_g"], g["final_norm_g"], loss.reshape(1)]
    small_shapes = [a.shape for a in small_local]
    (small_g,) = _exchange([_pack(small_local)], scatter=False, name="gather_small_grads")
    small_sum = _unpack(_sum_parts(small_g, name="sum_small_grads"), small_shapes)
    loss = small_sum[-1][0]
    dmod_all = _unpack(small_g, small_shapes[:1], lead=(N_DEV,))[0]
    sg = dict(zip(SMALL, small_sum))
    wait_group(scatter[-1], small_g)
    sg["gdn_conv_w"] = lax.dynamic_slice_in_dim(sg["gdn_conv_w"], me * gdn_conv_w.shape[1], gdn_conv_w.shape[1], 1)
    sg["mla_q_norm_g"] = lax.dynamic_slice_in_dim(sg["mla_q_norm_g"], me * mla_q_norm_g.shape[1],
                                                  mla_q_norm_g.shape[1], 1)
    sg["mla_kv_norm_g"] = lax.dynamic_slice_in_dim(sg["mla_kv_norm_g"], me * mla_kv_norm_g.shape[1],
                                                   mla_kv_norm_g.shape[1], 1)

    dmod_cols = jnp.transpose(lax.dynamic_slice_in_dim(dmod_all, me * wc, wc, axis=2), (1, 0, 2))
    res["ada_w"] = _ada_grad_adamw(c_all, dmod_cols, ada_w, m_ada_w, v_ada_w, name="ada_w_grad_adamw")
    for k in BIG:
        if k not in early:
            res[k] = update(k)
    shapes = [W[k].shape for k in SMALL]
    packed = [_pack([d[k] for k in SMALL]) for d in (sg, W, M, V)]
    outs = _adamw([packed[0][None]], packed[1][None], packed[2][None], packed[3][None], name="adamw_small")
    unpacked = [_unpack(o[0], shapes) for o in outs]
    for i, k in enumerate(SMALL):
        res[k] = tuple(u[i] for u in unpacked)

    return (loss, dx[None], *[res[k][0] for k in WEIGHTS], *[res[k][1] for k in WEIGHTS],
            *[res[k][2] for k in WEIGHTS], *[res[k][3] for k in WEIGHTS])
```

```python
import math

import jax
import jax.numpy as jnp
from jax import lax
from jax.experimental import pallas as pl
from jax.experimental.pallas import tpu as pltpu

F32 = jnp.float32
BF16 = jnp.bfloat16
MXU_DTYPE = jnp.bfloat16

N_DEV = 8
D_MODEL = 1024
DEPTH = 4
GDN_HEADS = 8
GDN_HEAD_DIM = 128
GDN_KEY_DIM = GDN_HEADS * GDN_HEAD_DIM
GDN_CHUNK = 64
GDN_HEAD_BATCH = 8
GDN_CONV = 4
GDN_PREP_HEADS = 2
GDN_MAIN = 4 * GDN_KEY_DIM
MLA_HEADS = 8
MLA_NOPE = 128
MLA_ROPE = 64
MLA_V = 128
MLA_Q_RANK = 384
MLA_KV_RANK = 256
MLA_IN = MLA_Q_RANK + MLA_KV_RANK + MLA_ROPE
MLA_QK = MLA_NOPE + MLA_ROPE
ROPE_THETA = 10000.0
D_FF = 2816
N_MOD = 6
EPS = 1e-6
LANES = 128
VMEM_LIMIT = 48 * 1024 * 1024

ADAM_LR = 0.001
ADAM_B1 = 0.9
ADAM_B2 = 0.999
ADAM_EPS = 1e-08
ADAM_WD = 0.01
ADAM_STEP = 10
ADAM_BC1 = 1.0 - ADAM_B1 ** ADAM_STEP
ADAM_BC2 = 1.0 - ADAM_B2 ** ADAM_STEP

NN = (((1,), (0,)), ((), ()))
NT = (((1,), (1,)), ((), ()))
TN = (((0,), (0,)), ((), ()))
NEG = -1e30


def _dotb(a, b, dims):
    return lax.dot_general(a.astype(MXU_DTYPE), b.astype(MXU_DTYPE), dims, preferred_element_type=F32)


def _split(a):
    hi = a.astype(BF16)
    return hi, (a - hi.astype(F32)).astype(BF16)


def _dotf(a, b, dims):
    ah, al = _split(a)
    bh, bl = _split(b)
    dot = lambda u, v: lax.dot_general(u, v, dims, preferred_element_type=F32)
    return dot(ah, bh) + (dot(ah, bl) + dot(al, bh))


def _params(*sem):
    return pltpu.CompilerParams(dimension_semantics=sem, vmem_limit_bytes=VMEM_LIMIT)


def _pick(n, pref, mult=LANES):
    best = None
    t = mult
    while t <= min(n, pref):
        if n % t == 0:
            best = t
        t += mult
    return best if best is not None else n


def _sigmoid(z):
    return 0.5 * jnp.tanh(0.5 * z) + 0.5


def _exchange(arrays, *, scatter, name):
    n = len(arrays)
    out_shape = tuple(
        jax.ShapeDtypeStruct(a.shape if scatter else (N_DEV,) + a.shape, a.dtype) for a in arrays)

    def body(*refs):
        ins, outs = refs[:n], refs[n:2 * n]
        send_sems, recv_sems, local_sems = refs[2 * n:]
        x, y, c = lax.axis_index("x"), lax.axis_index("y"), lax.axis_index("c")
        me = 4 * x + 2 * y + c
        copies = []
        for k in range(n):
            src_own = ins[k].at[me] if scatter else ins[k]
            own = pltpu.make_async_copy(src_own, outs[k].at[me], local_sems.at[k])
            own.start()
            copies.append(own)
        sends = []
        for p in range(1, N_DEV):
            px, py, pc = x ^ ((p >> 2) & 1), y ^ ((p >> 1) & 1), c ^ (p & 1)
            peer = 4 * px + 2 * py + pc
            for k in range(n):
                cp = pltpu.make_async_remote_copy(
                    src_ref=ins[k].at[peer] if scatter else ins[k],
                    dst_ref=outs[k].at[me],
                    send_sem=send_sems.at[k, p - 1],
                    recv_sem=recv_sems.at[k, p - 1],
                    device_id=(px, py, pc),
                    device_id_type=pl.DeviceIdType.MESH,
                )
                cp.start()
                sends.append((cp, k, peer, p))
        for cp, k, peer, p in sends:
            pltpu.make_async_remote_copy(
                src_ref=ins[k].at[peer] if scatter else ins[k],
                dst_ref=outs[k].at[peer],
                send_sem=send_sems.at[k, p - 1],
                recv_sem=recv_sems.at[k, p - 1],
                device_id=(x, y, c),
                device_id_type=pl.DeviceIdType.MESH,
            ).wait_recv()
        for cp, _, _, _ in sends:
            cp.wait_send()
        for own in copies:
            own.wait()

    any_spec = pl.BlockSpec(memory_space=pl.ANY)
    outs = pl.pallas_call(
        body,
        name=name,
        out_shape=out_shape,
        in_specs=[any_spec] * n,
        out_specs=tuple([any_spec] * n),
        scratch_shapes=[
            pltpu.SemaphoreType.DMA((n, N_DEV - 1)),
            pltpu.SemaphoreType.DMA((n, N_DEV - 1)),
            pltpu.SemaphoreType.DMA((n,)),
        ],
        compiler_params=pltpu.CompilerParams(has_side_effects=True),
    )(*arrays)
    return list(outs)


def _gather_two_level(arrays, *, name):
    n = len(arrays)
    out_shape = tuple(jax.ShapeDtypeStruct((N_DEV,) + a.shape, a.dtype) for a in arrays)

    def body(*refs):
        ins, outs = refs[:n], refs[n:2 * n]
        send_sems, recv_sems, local_sems = refs[2 * n:]
        x, y, c = lax.axis_index("x"), lax.axis_index("y"), lax.axis_index("c")
        me = 4 * x + 2 * y + c
        sibling = (x, y, 1 - c)
        chips = [(1 - x, y), (x, 1 - y), (1 - x, 1 - y)]

        def slot(px, py, pc):
            return 4 * px + 2 * py + pc

        def copy(k, q, block, to, src=None):
            return pltpu.make_async_remote_copy(
                src_ref=outs[k].at[slot(*block)] if src is None else src,
                dst_ref=outs[k].at[slot(*block)],
                send_sem=send_sems.at[k, q], recv_sem=recv_sems.at[k, q],
                device_id=to, device_id_type=pl.DeviceIdType.MESH)

        own = [pltpu.make_async_copy(ins[k], outs[k].at[me], local_sems.at[k]) for k in range(n)]
        for cp in own:
            cp.start()
        first = []
        for k in range(n):
            first.append(copy(k, 0, (x, y, c), sibling, src=ins[k]))
            first += [copy(k, 1 + j, (x, y, c), (*chip, c), src=ins[k]) for j, chip in enumerate(chips)]
        for cp in first:
            cp.start()
        passed = []
        for j, chip in enumerate(chips):
            for k in range(n):
                copy(k, 1 + j, (*chip, c), (x, y, c)).wait_recv()
                fwd = copy(k, 4 + j, (*chip, c), sibling)
                fwd.start()
                passed.append(fwd)
        for k in range(n):
            copy(k, 0, sibling, (x, y, c)).wait_recv()
            for j, chip in enumerate(chips):
                copy(k, 4 + j, (*chip, 1 - c), (x, y, c)).wait_recv()
        for cp in first + passed:
            cp.wait_send()
        for cp in own:
            cp.wait()

    any_spec = pl.BlockSpec(memory_space=pl.ANY)
    outs = pl.pallas_call(
        body, name=name, out_shape=out_shape, in_specs=[any_spec] * n, out_specs=tuple([any_spec] * n),
        scratch_shapes=[pltpu.SemaphoreType.DMA((n, N_DEV - 1)), pltpu.SemaphoreType.DMA((n, N_DEV - 1)),
                        pltpu.SemaphoreType.DMA((n,))],
        compiler_params=pltpu.CompilerParams(has_side_effects=True),
    )(*arrays)
    return list(outs)


def _peer(x, y, c, p):
    return x ^ ((p >> 2) & 1), y ^ ((p >> 1) & 1), c ^ (p & 1)


def _exchange_start(arrays, *, scatter, name, dep=None):
    n = len(arrays)
    deps = [] if dep is None else [dep]
    lands = [lax.empty(a.shape if scatter else (N_DEV,) + a.shape, a.dtype) for a in arrays]

    def body(*refs):
        ins, zones = refs[:n], refs[n:2 * n]
        send_sems, recv_sems = refs[2 * n + len(deps)], refs[2 * n + len(deps) + 1]
        token = refs[-1]
        x, y, c = lax.axis_index("x"), lax.axis_index("y"), lax.axis_index("c")
        me = 4 * x + 2 * y + c
        for p in range(1, N_DEV):
            px, py, pc = _peer(x, y, c, p)
            for k in range(n):
                pltpu.make_async_remote_copy(
                    src_ref=ins[k].at[4 * px + 2 * py + pc] if scatter else ins[k],
                    dst_ref=zones[k].at[me],
                    send_sem=send_sems.at[k * (N_DEV - 1) + p - 1],
                    recv_sem=recv_sems.at[k * (N_DEV - 1) + p - 1],
                    device_id=(px, py, pc),
                    device_id_type=pl.DeviceIdType.MESH,
                ).start()
        token[...] = jnp.zeros_like(token)

    hbm = pl.BlockSpec(memory_space=pltpu.HBM)
    sem = pl.BlockSpec(memory_space=pltpu.SEMAPHORE)
    outs = pl.pallas_call(
        body,
        name=name,
        out_shape=(pltpu.SemaphoreType.DMA((n * (N_DEV - 1),)), pltpu.SemaphoreType.DMA((n * (N_DEV - 1),)),
                   *[pltpu.HBM(a.shape, a.dtype) for a in arrays], *[pltpu.HBM(z.shape, z.dtype) for z in lands],
                   jax.ShapeDtypeStruct((8, LANES), F32)),
        in_specs=[hbm] * (2 * n) + [pl.BlockSpec(memory_space=pl.ANY)] * len(deps),
        out_specs=(sem, sem, *[hbm] * (2 * n), pl.BlockSpec(memory_space=pltpu.VMEM)),
        input_output_aliases={k: 2 + k for k in range(2 * n)},
        compiler_params=pltpu.CompilerParams(has_side_effects=pltpu.SideEffectType.DATAFLOW_SIDE_EFFECTING),
    )(*[pltpu.with_memory_space_constraint(a, pltpu.HBM) for a in arrays],
      *[pltpu.with_memory_space_constraint(z, pltpu.HBM) for z in lands], *deps)
    return outs[0], outs[1], list(outs[2:2 + n]), list(outs[2 + n:2 + 2 * n]), outs[-1]


def _exchange_wait(started, after, *, scatter, name):
    send_sems, recv_sems, srcs, lands, _ = started
    n = len(srcs)

    def body(*refs):
        ins, zones = refs[:n], refs[n:2 * n]
        s_sems, r_sems = refs[2 * n], refs[2 * n + 1]
        x, y, c = lax.axis_index("x"), lax.axis_index("y"), lax.axis_index("c")
        for p in range(1, N_DEV):
            px, py, pc = _peer(x, y, c, p)
            peer = 4 * px + 2 * py + pc
            for k in range(n):
                cp = pltpu.make_async_remote_copy(
                    src_ref=ins[k].at[peer] if scatter else ins[k],
                    dst_ref=zones[k].at[peer],
                    send_sem=s_sems.at[k * (N_DEV - 1) + p - 1],
                    recv_sem=r_sems.at[k * (N_DEV - 1) + p - 1],
                    device_id=(px, py, pc),
                    device_id_type=pl.DeviceIdType.MESH,
                )
                cp.wait_send()
                cp.wait_recv()

    hbm = pl.BlockSpec(memory_space=pltpu.HBM)
    sem = pl.BlockSpec(memory_space=pltpu.SEMAPHORE)
    outs = pl.pallas_call(
        body,
        name=name,
        out_shape=tuple(pltpu.HBM(a.shape, a.dtype) for a in srcs + lands),
        in_specs=[hbm] * (2 * n) + [sem, sem, pl.BlockSpec(memory_space=pl.ANY)],
        out_specs=tuple([hbm] * (2 * n)),
        input_output_aliases={k: k for k in range(2 * n)},
        compiler_params=pltpu.CompilerParams(has_side_effects=pltpu.SideEffectType.DATAFLOW_SIDE_EFFECTING),
    )(*srcs, *lands, send_sems, recv_sems, after)
    return list(outs[:n]), list(outs[n:])


def _mm(a, b, *, mode, out_dtype, name, add=None, tm=512, tn=512, b_rows=None, dep=None):
    rows_b = b.shape[0] if b_rows is None else b_rows
    if mode == "nn":
        (m, kd), nd = a.shape, b.shape[1]
        assert kd == rows_b
    elif mode == "nt":
        (m, kd), nd = a.shape, rows_b
    else:
        (kd, m), nd = a.shape, b.shape[1]
    tm = _pick(m, tm, LANES if mode == "tn" else 16)
    tn = _pick(nd, tn)
    dims = {"nn": NN, "nt": NT, "tn": TN}[mode]
    ni, nj = m // tm, nd // tn
    a_bytes, b_bytes = a.size * a.dtype.itemsize, b.size * b.dtype.itemsize
    i_outer = a_bytes + ni * b_bytes <= b_bytes + nj * a_bytes
    ij = (lambda g0, g1: (g0, g1)) if i_outer else (lambda g0, g1: (g1, g0))
    a_spec = (pl.BlockSpec((kd, tm), lambda g0, g1: (0, ij(g0, g1)[0])) if mode == "tn"
              else pl.BlockSpec((tm, kd), lambda g0, g1: (ij(g0, g1)[0], 0)))
    b_spec = (pl.BlockSpec((tn, kd), lambda g0, g1: (ij(g0, g1)[1], 0)) if mode == "nt"
              else pl.BlockSpec((kd, tn), lambda g0, g1: (0, ij(g0, g1)[1])))
    o_spec = pl.BlockSpec((tm, tn), lambda g0, g1: ij(g0, g1))
    has_add = add is not None

    def body(*refs):
        a_ref, b_ref = refs[0], refs[1]
        o_ref = refs[-1]
        acc = _dotb(a_ref[...], b_ref[...], dims)
        if has_add:
            acc = acc + refs[2][...].astype(F32)
        o_ref[...] = acc.astype(o_ref.dtype)

    ins = [a, b] + ([add] if has_add else []) + ([] if dep is None else [dep])
    specs = ([a_spec, b_spec] + ([o_spec] if has_add else [])
             + ([] if dep is None else [pl.BlockSpec((8, LANES), lambda g0, g1: (0, 0))]))
    return pl.pallas_call(
        body, name=name, grid=(ni, nj) if i_outer else (nj, ni), in_specs=specs, out_specs=o_spec,
        out_shape=jax.ShapeDtypeStruct((m, nd), out_dtype),
        compiler_params=_params("parallel", "parallel"),
    )(*ins)


def _mm_pair(a1, b1, a2, b2, *, out_dtype, name, b1_rows=None, tm=256):
    m, k1 = a1.shape
    k2, nd = b2.shape
    assert k1 == (b1.shape[0] if b1_rows is None else b1_rows) and a2.shape == (m, k2) and b1.shape[1] == nd
    tm = _pick(m, tm, 16)

    def body(a1_ref, b1_ref, a2_ref, b2_ref, o_ref):
        o_ref[...] = (_dotb(a1_ref[...], b1_ref[...], NN) + _dotb(a2_ref[...], b2_ref[...], NN)).astype(o_ref.dtype)

    return pl.pallas_call(
        body, name=name, grid=(m // tm,),
        in_specs=[pl.BlockSpec((tm, k1), lambda i: (i, 0)), pl.BlockSpec((k1, nd), lambda i: (0, 0)),
                  pl.BlockSpec((tm, k2), lambda i: (i, 0)), pl.BlockSpec((k2, nd), lambda i: (0, 0))],
        out_specs=pl.BlockSpec((tm, nd), lambda i: (i, 0)), out_shape=jax.ShapeDtypeStruct((m, nd), out_dtype),
        compiler_params=_params("parallel"),
    )(a1, b1, a2, b2)


def _mm_resid(a, b, x, gate, *, name, tm=256, tn=1024):
    m, kd = a.shape
    nd = b.shape[1]
    tm = _pick(m, tm, 16)
    tn = _pick(nd, tn)
    o_spec = pl.BlockSpec((tm, tn), lambda i, j: (i, j))

    def body(a_ref, b_ref, x_ref, g_ref, xo_ref, y_ref):
        y = _dotb(a_ref[...], b_ref[...], NN)
        y_ref[...] = y.astype(y_ref.dtype)
        xo_ref[...] = x_ref[...] + g_ref[...] * y

    return pl.pallas_call(
        body, name=name, grid=(m // tm, nd // tn),
        in_specs=[pl.BlockSpec((tm, kd), lambda i, j: (i, 0)), pl.BlockSpec((kd, tn), lambda i, j: (0, j)),
                  o_spec, pl.BlockSpec((1, tn), lambda i, j: (0, j))],
        out_specs=(o_spec, o_spec),
        out_shape=(jax.ShapeDtypeStruct((m, nd), F32), jax.ShapeDtypeStruct((m, nd), BF16)),
        compiler_params=_params("parallel", "parallel"),
    )(a, b, x, gate)


ROWS = 512


def _row_spec(width, rows=ROWS):
    return pl.BlockSpec((rows, width), lambda i: (i, 0))


def _const_spec(shape):
    return pl.BlockSpec(shape, lambda i: tuple(0 for _ in shape))


def _adaln_fwd(x, g, scale, shift, *, name):
    t, d = x.shape

    def body(x_ref, g_ref, sc_ref, sh_ref, h_ref):
        xv = x_ref[...]
        r = lax.rsqrt(jnp.mean(xv * xv, axis=-1, keepdims=True) + EPS)
        h_ref[...] = (xv * r * g_ref[...] * (1.0 + sc_ref[...]) + sh_ref[...]).astype(h_ref.dtype)

    return pl.pallas_call(
        body, name=name, grid=(t // ROWS,),
        in_specs=[_row_spec(d), _const_spec((1, d)), _const_spec((1, d)), _const_spec((1, d))],
        out_specs=_row_spec(d), out_shape=jax.ShapeDtypeStruct((t, d), BF16),
        compiler_params=_params("parallel"),
    )(x, g, scale, shift)


def _adaln_bwd(x, g, scale, shift, dh, dres, dep, *, name):
    t, d = x.shape

    def body(x_ref, g_ref, sc_ref, sh_ref, dh_ref, dr_ref, dep_ref, dx_ref, st_ref):
        @pl.when(pl.program_id(0) == 0)
        def _():
            st_ref[...] = jnp.zeros_like(st_ref)

        xv = x_ref[...]
        dhv = dh_ref[...].astype(F32)
        gv = g_ref[...]
        r = lax.rsqrt(jnp.mean(xv * xv, axis=-1, keepdims=True) + EPS)
        xh = xv * r
        nv = xh * gv
        dn = dhv * (1.0 + sc_ref[...])
        dxh = dn * gv
        dx_ref[...] = dr_ref[...] + r * (dxh - xh * jnp.mean(dxh * xh, axis=-1, keepdims=True))
        st_ref[0:1, :] += jnp.sum(dn * xh, axis=0, keepdims=True)
        st_ref[1:2, :] += jnp.sum(dhv * nv, axis=0, keepdims=True)
        st_ref[2:3, :] += jnp.sum(dhv, axis=0, keepdims=True)

    return pl.pallas_call(
        body, name=name, grid=(t // ROWS,),
        in_specs=[_row_spec(d), _const_spec((1, d)), _const_spec((1, d)), _const_spec((1, d)),
                  _row_spec(d), _row_spec(d), _const_spec((8, LANES))],
        out_specs=(_row_spec(d), _const_spec((8, d))),
        out_shape=(jax.ShapeDtypeStruct((t, d), F32), jax.ShapeDtypeStruct((8, d), F32)),
        compiler_params=_params("arbitrary"),
    )(x, g, scale, shift, dh, dres, dep)


def _adaln_gate_bwd(x, g, scale, shift, dh, dres, dep, y_up, gate_up, *, name):
    t, d = x.shape

    def body(x_ref, g_ref, sc_ref, sh_ref, dh_ref, dr_ref, dep_ref, y_ref, gu_ref, dx_ref, st_ref, dy_ref):
        @pl.when(pl.program_id(0) == 0)
        def _():
            st_ref[...] = jnp.zeros_like(st_ref)

        xv = x_ref[...]
        dhv = dh_ref[...].astype(F32)
        gv = g_ref[...]
        r = lax.rsqrt(jnp.mean(xv * xv, axis=-1, keepdims=True) + EPS)
        xh = xv * r
        nv = xh * gv
        dn = dhv * (1.0 + sc_ref[...])
        dxh = dn * gv
        dx = dr_ref[...] + r * (dxh - xh * jnp.mean(dxh * xh, axis=-1, keepdims=True))
        dx_ref[...] = dx
        dy_ref[...] = (dx * gu_ref[...]).astype(dy_ref.dtype)
        st_ref[0:1, :] += jnp.sum(dn * xh, axis=0, keepdims=True)
        st_ref[1:2, :] += jnp.sum(dhv * nv, axis=0, keepdims=True)
        st_ref[2:3, :] += jnp.sum(dhv, axis=0, keepdims=True)
        st_ref[3:4, :] += jnp.sum(dx * y_ref[...].astype(F32), axis=0, keepdims=True)

    return pl.pallas_call(
        body, name=name, grid=(t // ROWS,),
        in_specs=[_row_spec(d), _const_spec((1, d)), _const_spec((1, d)), _const_spec((1, d)),
                  _row_spec(d), _row_spec(d), _const_spec((8, LANES)), _row_spec(d), _const_spec((1, d))],
        out_specs=(_row_spec(d), _const_spec((8, d)), _row_spec(d)),
        out_shape=(jax.ShapeDtypeStruct((t, d), F32), jax.ShapeDtypeStruct((8, d), F32),
                   jax.ShapeDtypeStruct((t, d), BF16)),
        compiler_params=_params("arbitrary"),
    )(x, g, scale, shift, dh, dres, dep, y_up, gate_up)


def _loss_head(x, g, target, y_up, gate_up, *, name):
    t, d = x.shape

    def body(x_ref, g_ref, t_ref, y_ref, gu_ref, dx_ref, st_ref, ls_ref, dy_ref):
        @pl.when(pl.program_id(0) == 0)
        def _():
            st_ref[...] = jnp.zeros_like(st_ref)
            ls_ref[...] = jnp.zeros_like(ls_ref)

        xv = x_ref[...]
        gv = g_ref[...]
        r = lax.rsqrt(jnp.mean(xv * xv, axis=-1, keepdims=True) + EPS)
        xh = xv * r
        err = xh * gv - t_ref[...]
        ls_ref[...] += 0.5 * jnp.sum(jnp.mean(err * err, axis=-1, keepdims=True))
        dy = err * (1.0 / d)
        dxh = dy * gv
        dx = r * (dxh - xh * jnp.mean(dxh * xh, axis=-1, keepdims=True))
        dx_ref[...] = dx
        dy_ref[...] = (dx * gu_ref[...]).astype(dy_ref.dtype)
        st_ref[0:1, :] += jnp.sum(dy * xh, axis=0, keepdims=True)
        st_ref[3:4, :] += jnp.sum(dx * y_ref[...].astype(F32), axis=0, keepdims=True)

    return pl.pallas_call(
        body, name=name, grid=(t // ROWS,),
        in_specs=[_row_spec(d), _const_spec((1, d)), _row_spec(d), _row_spec(d), _const_spec((1, d))],
        out_specs=(_row_spec(d), _const_spec((8, d)), _const_spec((8, LANES)), _row_spec(d)),
        out_shape=(jax.ShapeDtypeStruct((t, d), F32), jax.ShapeDtypeStruct((8, d), F32),
                   jax.ShapeDtypeStruct((8, LANES), F32), jax.ShapeDtypeStruct((t, d), BF16)),
        compiler_params=_params("arbitrary"),
    )(x, g, target, y_up, gate_up)


FFN_BLOCK = D_FF // 2
FFN_ROWS = 512


def _ffn_chunks(width):
    edges = [min(width, 3 * LANES * i) for i in range(width // (3 * LANES) + 2)]
    return [slice(lo, hi) for lo, hi in zip(edges[:-1], edges[1:]) if hi > lo]


def _ffn_gu_fwd(h, wg, wu, dep, *, name):
    t, d = h.shape
    tn = FFN_BLOCK

    chunks = _ffn_chunks(tn)
    rows = _pick(t, FFN_ROWS, 16)

    def body(h_ref, wg_ref, wu_ref, dep_ref, s_ref, a_ref, b_ref):
        hv = h_ref[...]
        ab = [(_dotb(hv, wg_ref[sl, :], NT), _dotb(hv, wu_ref[sl, :], NT)) for sl in chunks]
        for sl, (a, b) in zip(chunks, ab):
            s_ref[:, sl] = (a * _sigmoid(a) * b).astype(s_ref.dtype)
            a_ref[:, sl] = a.astype(a_ref.dtype)
            b_ref[:, sl] = b.astype(b_ref.dtype)

    w_spec = pl.BlockSpec((tn, d), lambda j, i: (j, 0))
    o_spec = pl.BlockSpec((rows, tn), lambda j, i: (i, j))
    return pl.pallas_call(
        body, name=name, grid=(D_FF // tn, t // rows),
        in_specs=[pl.BlockSpec((rows, d), lambda j, i: (i, 0)), w_spec, w_spec,
                  pl.BlockSpec((8, LANES), lambda j, i: (0, 0))],
        out_specs=(o_spec, o_spec, o_spec),
        out_shape=(jax.ShapeDtypeStruct((t, D_FF), BF16),) * 3,
        compiler_params=_params("parallel", "parallel"),
    )(h, wg, wu, dep)


def _ffn_down_dx(dy, w_down, a, b, *, name):
    t, d = dy.shape
    tn = FFN_BLOCK

    chunks = _ffn_chunks(tn)
    rows = _pick(t, FFN_ROWS, 16)

    def body(dy_ref, w_ref, a_ref, b_ref, da_ref, db_ref):
        dyv = dy_ref[...]
        ds = [_dotb(dyv, w_ref[sl, :], NT) for sl in chunks]
        for sl, dsc in zip(chunks, ds):
            av = a_ref[:, sl].astype(F32)
            sg = _sigmoid(av)
            da_ref[:, sl] = (dsc * b_ref[:, sl].astype(F32) * sg * (1.0 + av * (1.0 - sg))).astype(da_ref.dtype)
            db_ref[:, sl] = (dsc * av * sg).astype(db_ref.dtype)

    o_spec = pl.BlockSpec((rows, tn), lambda j, i: (i, j))
    return pl.pallas_call(
        body, name=name, grid=(D_FF // tn, t // rows),
        in_specs=[pl.BlockSpec((rows, d), lambda j, i: (i, 0)), pl.BlockSpec((tn, d), lambda j, i: (j, 0)),
                  o_spec, o_spec],
        out_specs=(o_spec, o_spec),
        out_shape=(jax.ShapeDtypeStruct((t, D_FF), BF16),) * 2,
        compiler_params=_params("parallel", "parallel"),
    )(dy, w_down, a, b)


def _shift_rows(v, s, rows):
    if s == 0:
        return v
    return jnp.where(rows >= s, pltpu.roll(v, s, 0), 0.0)


def _unshift_rows(v, s, rows, t):
    if s == 0:
        return v
    return jnp.where(rows < t - s, pltpu.roll(v, t - s, 0), 0.0)


def _conv_taps(x, rows):
    return [_shift_rows(x, GDN_CONV - 1 - j, rows) for j in range(GDN_CONV)]


def _conv_silu(xs, w):
    z = w[0:1, :] * xs[0]
    for j in range(1, GDN_CONV):
        z = z + w[j:j + 1, :] * xs[j]
    sg = _sigmoid(z)
    return z, sg, z * sg


def _gdn_prep_fwd(proj, conv_wt, *, name):
    t = proj.shape[0]
    nh = GDN_HEADS

    hp = GDN_PREP_HEADS
    wd = hp * LANES

    def body(x_ref, w_ref, y_ref):
        j = pl.program_id(0) * hp
        rows = lax.broadcasted_iota(jnp.int32, (t, LANES), 0)
        qscale = jnp.where(j < nh, GDN_HEAD_DIM ** -0.5, 1.0)
        for i in range(hp):
            sl = slice(i * LANES, (i + 1) * LANES)
            _, _, s = _conv_silu(_conv_taps(x_ref[:, sl], rows), w_ref[:, sl])
            rs = lax.rsqrt(jnp.sum(s * s, axis=-1, keepdims=True) + EPS)
            y_ref[:, sl] = jnp.where(j < 2 * nh, s * rs * qscale, s)

    return pl.pallas_call(
        body, name=name, grid=(3 * nh // hp,),
        in_specs=[pl.BlockSpec((t, wd), lambda j: (0, j)), pl.BlockSpec((GDN_CONV, wd), lambda j: (0, j))],
        out_specs=pl.BlockSpec((t, wd), lambda j: (0, j)),
        out_shape=jax.ShapeDtypeStruct((t, 3 * GDN_KEY_DIM), F32),
        compiler_params=_params("parallel"),
    )(proj, conv_wt)


def _gdn_prep_bwd(proj, conv_wt, dy, *, name):
    t = proj.shape[0]
    nh = GDN_HEADS

    hp = GDN_PREP_HEADS
    wd = hp * LANES
    per_seg = nh // hp

    def body(x_ref, w_ref, dy_ref, dx_ref, dw_ref):
        j = pl.program_id(0) * hp
        rows = lax.broadcasted_iota(jnp.int32, (t, LANES), 0)
        qscale = jnp.where(j < nh, GDN_HEAD_DIM ** -0.5, 1.0)
        for i in range(hp):
            sl = slice(i * LANES, (i + 1) * LANES)
            w = w_ref[:, sl]
            xs = _conv_taps(x_ref[:, sl], rows)
            z, sg, s = _conv_silu(xs, w)
            rs = lax.rsqrt(jnp.sum(s * s, axis=-1, keepdims=True) + EPS)
            dyv = dy_ref[:, sl]
            nv = s * rs
            de = dyv * qscale
            ds_qk = rs * (de - nv * jnp.sum(de * nv, axis=-1, keepdims=True))
            ds = jnp.where(j < 2 * nh, ds_qk, dyv)
            dz = ds * sg * (1.0 + z * (1.0 - sg))
            dx = w[GDN_CONV - 1:GDN_CONV, :] * dz
            dw_ref[GDN_CONV - 1:GDN_CONV, sl] = jnp.sum(dz * xs[GDN_CONV - 1], axis=0, keepdims=True)
            for k in range(GDN_CONV - 1):
                dx = dx + w[k:k + 1, :] * _unshift_rows(dz, GDN_CONV - 1 - k, rows, t)
                dw_ref[k:k + 1, sl] = jnp.sum(dz * xs[k], axis=0, keepdims=True)
            dx_ref[:, sl] = dx.astype(dx_ref.dtype)

    return pl.pallas_call(
        body, name=name, grid=(3 * nh // hp,),
        in_specs=[pl.BlockSpec((t, wd), lambda j: (0, j)), pl.BlockSpec((GDN_CONV, wd), lambda j: (0, j)),
                  pl.BlockSpec((None, t, wd), lambda j: (j // per_seg, 0, j % per_seg))],
        out_specs=(pl.BlockSpec((t, wd), lambda j: (0, j)), pl.BlockSpec((GDN_CONV, wd), lambda j: (0, j))),
        out_shape=(jax.ShapeDtypeStruct((t, 3 * GDN_KEY_DIM), BF16),
                   jax.ShapeDtypeStruct((GDN_CONV, 3 * GDN_KEY_DIM), F32)),
        compiler_params=_params("parallel"),
    )(proj, conv_wt, dy)


def _softplus(z):
    return jnp.maximum(z, 0.0) + jnp.log(1.0 + jnp.exp(-jnp.abs(z)))


def _gdn_gate_fwd(ab, prm, *, name):
    t = ab.shape[0]

    def body(ab_ref, p_ref, o_ref):
        v = ab_ref[...]
        lane = lax.broadcasted_iota(jnp.int32, v.shape, 1)
        g = -jnp.exp(p_ref[0:1, :]) * _softplus(v + p_ref[1:2, :])
        o_ref[...] = jnp.where(lane < GDN_HEADS, g, jnp.where(lane < 2 * GDN_HEADS, _sigmoid(v), 0.0))

    return pl.pallas_call(
        body, name=name, grid=(t // ROWS,),
        in_specs=[_row_spec(LANES), _const_spec((8, LANES))], out_specs=_row_spec(LANES),
        out_shape=jax.ShapeDtypeStruct((t, LANES), F32), compiler_params=_params("parallel"),
    )(ab, prm)


def _gdn_gate_bwd(ab, prm, dgb, *, name):
    t = ab.shape[0]

    def body(ab_ref, p_ref, d_ref, o_ref, st_ref):
        @pl.when(pl.program_id(0) == 0)
        def _():
            st_ref[...] = jnp.zeros_like(st_ref)

        v = ab_ref[...]
        dv = d_ref[...]
        lane = lax.broadcasted_iota(jnp.int32, v.shape, 1)
        is_a = lane < GDN_HEADS
        is_b = jnp.logical_and(lane >= GDN_HEADS, lane < 2 * GDN_HEADS)
        a_exp = jnp.exp(p_ref[0:1, :])
        zz = v + p_ref[1:2, :]
        g = -a_exp * _softplus(zz)
        da = dv * (-a_exp) * _sigmoid(zz)
        beta = _sigmoid(v)
        db = dv * beta * (1.0 - beta)
        o_ref[...] = jnp.where(is_a, da, jnp.where(is_b, db, 0.0)).astype(o_ref.dtype)
        st_ref[0:1, :] += jnp.sum(jnp.where(is_a, dv * g, 0.0), axis=0, keepdims=True)
        st_ref[1:2, :] += jnp.sum(jnp.where(is_a, da, 0.0), axis=0, keepdims=True)

    return pl.pallas_call(
        body, name=name, grid=(t // ROWS,),
        in_specs=[_row_spec(LANES), _const_spec((8, LANES)), _row_spec(LANES)],
        out_specs=(_row_spec(LANES), _const_spec((8, LANES))),
        out_shape=(jax.ShapeDtypeStruct((t, LANES), BF16), jax.ShapeDtypeStruct((8, LANES), F32)),
        compiler_params=_params("arbitrary"),
    )(ab, prm, dgb)


def _gdn_local(qs, ks, vs, gbs, bbs, tinvs=None):
    nh = len(qs)
    cs = qs[0].shape[0]
    hs = range(nh)
    r = lax.broadcasted_iota(jnp.int32, (cs, cs), 0)
    c = lax.broadcasted_iota(jnp.int32, (cs, cs), 1)
    tril, strict, eye = r >= c, r > c, r == c
    ident = jnp.where(eye, 1.0, 0.0)
    g_colb = [gbs[h][:, :cs] for h in hs]
    g_row = [jnp.sum(jnp.where(eye, g_colb[h], 0.0), axis=0, keepdims=True) for h in hs]
    gc_col = [jnp.sum(jnp.where(tril, g_row[h], 0.0), axis=1, keepdims=True) for h in hs]
    gc_row = [jnp.sum(jnp.where(r <= c, g_colb[h], 0.0), axis=0, keepdims=True) for h in hs]
    decay = [jnp.exp(jnp.where(tril, gc_col[h] - gc_row[h], NEG)) for h in hs]
    gamma = [jnp.exp(gc_col[h]) for h in hs]
    gcl = [gc_col[h][cs - 1:cs, :] for h in hs]
    gl = [jnp.exp(gcl[h]) for h in hs]
    kdec = [jnp.exp(gcl[h] - gc_col[h]) for h in hs]
    kb = [ks[h] * bbs[h] for h in hs]
    kk = [_dotb(kb[h], ks[h], NT) for h in hs]
    qk = [_dotb(qs[h], ks[h], NT) for h in hs]
    lmat = [jnp.where(strict, kk[h] * decay[h], 0.0) for h in hs]
    pmat = [jnp.where(tril, qk[h] * decay[h], 0.0) for h in hs]
    if tinvs is None:
        xm = [-lmat[h] for h in hs]
        tinv = [ident + xm[h] for h in hs]
        for _ in range(int(math.log2(cs)) - 1):
            xm = [_dotf(xm[h], xm[h], NN) for h in hs]
            tinv = [tinv[h] + _dotf(tinv[h], xm[h], NN) for h in hs]
    else:
        tinv = tinvs
    vb = [vs[h] * bbs[h] for h in hs]
    kg = [kb[h] * gamma[h] for h in hs]
    u = [_dotf(tinv[h], vb[h], NN) for h in hs]
    w = [_dotf(tinv[h], kg[h], NN) for h in hs]
    return [dict(tril=tril, strict=strict, eye=eye, r=r, c=c, decay=decay[h], gamma=gamma[h], gl=gl[h], kdec=kdec[h],
                 kb=kb[h], lmat=lmat[h], tinv=tinv[h], vb=vb[h], kg=kg[h], u=u[h], w=w[h], pmat=pmat[h],
                 qd=qs[h] * gamma[h], kd=ks[h] * kdec[h]) for h in hs]


def _head_columns(gbeta, cs):
    gbs = [jnp.broadcast_to(gbeta[:, h:h + 1], (cs, LANES)) for h in range(GDN_HEADS)]
    bbs = [jnp.broadcast_to(gbeta[:, GDN_HEADS + h:GDN_HEADS + h + 1], (cs, LANES)) for h in range(GDN_HEADS)]
    return gbs, bbs


def _gdn_chunk_fwd(qkv, gbeta, *, name):
    t = qkv.shape[0]
    nh, cs, hd = GDN_HEADS, GDN_CHUNK, GDN_HEAD_DIM
    nc = t // cs

    hb = GDN_HEAD_BATCH
    ng = nh // hb
    assert ng == 1

    def body(q_ref, k_ref, v_ref, gb_ref, o_ref, st_ref, ti_ref, s_ref):
        @pl.when(pl.program_id(1) == 0)
        def _():
            s_ref[...] = jnp.zeros_like(s_ref)

        sls = [slice(i * hd, (i + 1) * hd) for i in range(hb)]
        hs = range(hb)
        s = [s_ref[i] for i in hs]
        gbs, bbs = _head_columns(gb_ref[...], cs)
        lo = _gdn_local([q_ref[:, sl] for sl in sls], [k_ref[:, sl] for sl in sls], [v_ref[:, sl] for sl in sls],
                        gbs, bbs)
        ws = [_dotb(lo[i]["w"], s[i], NN) for i in hs]
        qs = [_dotb(lo[i]["qd"], s[i], NN) for i in hs]
        vn = [lo[i]["u"] - ws[i] for i in hs]
        pv = [_dotb(lo[i]["pmat"], vn[i], NN) for i in hs]
        kv = [_dotb(lo[i]["kd"], vn[i], TN) for i in hs]
        for i, sl in enumerate(sls):
            st_ref[i, 0] = s[i]
            ti_ref[i, 0] = lo[i]["tinv"]
            o_ref[:, sl] = qs[i] + pv[i]
            s_ref[i] = s[i] * lo[i]["gl"] + kv[i]

    col = lambda off: pl.BlockSpec((cs, hb * hd), lambda h, n: (n, off + h))
    return pl.pallas_call(
        body, name=name, grid=(ng, nc),
        in_specs=[col(0), col(ng), col(2 * ng), pl.BlockSpec((cs, LANES), lambda h, n: (n, 0))],
        out_specs=(col(0), pl.BlockSpec((hb, 1, hd, hd), lambda h, n: (h, n, 0, 0)),
                   pl.BlockSpec((hb, 1, cs, cs), lambda h, n: (h, n, 0, 0))),
        out_shape=(jax.ShapeDtypeStruct((t, nh * hd), F32), jax.ShapeDtypeStruct((nh, nc, hd, hd), F32),
                   jax.ShapeDtypeStruct((nh, nc, cs, cs), F32)),
        scratch_shapes=[pltpu.VMEM((hb, hd, hd), F32)],
        compiler_params=_params("parallel", "arbitrary"),
    )(qkv, qkv, qkv, gbeta)


def _gdn_chunk_bwd(qkv, gbeta, states, tinvs, do, *, name):
    t = qkv.shape[0]
    nh, cs, hd = GDN_HEADS, GDN_CHUNK, GDN_HEAD_DIM
    nc = t // cs

    hb = GDN_HEAD_BATCH
    ng = nh // hb
    assert ng == 1

    def heads_bwd(q, k, v, gb, bb, s, ti, dsn, dov):
        hs = range(len(q))
        lo = _gdn_local(q, k, v, gb, bb, ti)
        tril, strict, eye, r, c = lo[0]["tril"], lo[0]["strict"], lo[0]["eye"], lo[0]["r"], lo[0]["c"]
        rowi = lax.broadcasted_iota(jnp.int32, (cs, 1), 0)
        get = lambda name: [lo[h][name] for h in hs]
        decay, gamma, gl, kdec = get("decay"), get("gamma"), get("gl"), get("kdec")
        kb, tinv, w, pmat, kd, qd = get("kb"), get("tinv"), get("w"), get("pmat"), get("kd"), get("qd")
        ws = [_dotb(w[h], s[h], NN) for h in hs]
        pdo = [_dotb(pmat[h], dov[h], TN) for h in hs]
        kds = [_dotb(kd[h], dsn[h], NN) for h in hs]
        dqd = [_dotb(dov[h], s[h], NT) for h in hs]
        qdo = [_dotb(qd[h], dov[h], TN) for h in hs]
        vn = [lo[h]["u"] - ws[h] for h in hs]
        dvn = [pdo[h] + kds[h] for h in hs]
        dp = [jnp.where(tril, _dotb(dov[h], vn[h], NT), 0.0) for h in hs]
        dkd = [_dotb(vn[h], dsn[h], NT) for h in hs]
        dw = [-_dotb(dvn[h], s[h], NT) for h in hs]
        wdv = [_dotb(w[h], dvn[h], TN) for h in hs]
        dvb = [_dotf(tinv[h], dvn[h], TN) for h in hs]
        dt1 = [_dotf(dvn[h], lo[h]["vb"], NT) for h in hs]
        dkg = [_dotf(tinv[h], dw[h], TN) for h in hs]
        dt2 = [_dotf(dw[h], lo[h]["kg"], NT) for h in hs]
        tdt = [_dotf(tinv[h], dt1[h] + dt2[h], TN) for h in hs]
        dl = [jnp.where(strict, -_dotf(tdt[h], tinv[h], NT), 0.0) for h in hs]
        dkk = [dl[h] * decay[h] for h in hs]
        dqk = [dp[h] * decay[h] for h in hs]
        dkb = [_dotb(dkk[h], k[h], NN) + dkg[h] * gamma[h] for h in hs]
        dk1 = [_dotb(dkk[h], kb[h], TN) for h in hs]
        dk2 = [_dotb(dqk[h], q[h], TN) for h in hs]
        dq1 = [_dotb(dqk[h], k[h], NN) for h in hs]
        out = []
        for h in hs:
            dgl = jnp.sum(jnp.sum(dsn[h] * s[h], axis=1, keepdims=True), axis=0, keepdims=True)
            ds_prev = gl[h] * dsn[h] + qdo[h] - wdv[h]
            dk = dk1[h] + dk2[h] + dkd[h] * kdec[h] + dkb[h] * bb[h]
            dq = dq1[h] + dqd[h] * gamma[h]
            dbeta = jnp.sum(dvb[h] * v[h], axis=-1, keepdims=True) + jnp.sum(dkb[h] * k[h], axis=-1, keepdims=True)
            e = dl[h] * lo[h]["lmat"] + dp[h] * pmat[h]
            e_col = jnp.sum(e, axis=0, keepdims=True)
            dgc = jnp.sum(e, axis=1, keepdims=True) - jnp.sum(jnp.where(eye, e_col, 0.0), axis=1, keepdims=True)
            dgamma = (jnp.sum(dqd[h] * q[h], axis=-1, keepdims=True)
                      + jnp.sum(dkg[h] * kb[h], axis=-1, keepdims=True))
            rk = jnp.sum(dkd[h] * k[h], axis=-1, keepdims=True) * kdec[h]
            dgcl = jnp.sum(rk, axis=0, keepdims=True) + dgl * gl[h]
            dgc = dgc + dgamma * gamma[h] - rk + jnp.where(rowi == cs - 1, dgcl, 0.0)
            dgc_row = jnp.sum(jnp.where(eye, dgc, 0.0), axis=0, keepdims=True)
            dg = jnp.sum(jnp.where(c >= r, dgc_row, 0.0), axis=1, keepdims=True)
            out.append((dq, dk, dvb[h] * bb[h], dbeta, dg, ds_prev))
        return out

    def body(q_ref, k_ref, v_ref, gb_ref, st_ref, ti_ref, do_ref, d_ref, dgb_ref, ds_ref):
        @pl.when(pl.program_id(1) == 0)
        def _():
            ds_ref[...] = jnp.zeros_like(ds_ref)

        sls = [slice(i * hd, (i + 1) * hd) for i in range(hb)]
        hs = range(hb)
        gbs, bbs = _head_columns(gb_ref[...], cs)
        outs = heads_bwd([q_ref[:, sl] for sl in sls], [k_ref[:, sl] for sl in sls], [v_ref[:, sl] for sl in sls],
                         gbs, bbs, [st_ref[i, 0] for i in hs],
                         [ti_ref[i, 0] for i in hs], [ds_ref[i] for i in hs], [do_ref[:, sl] for sl in sls])
        lane = lax.broadcasted_iota(jnp.int32, (cs, LANES), 1)
        dgb = jnp.zeros((cs, LANES), F32)
        for i, sl in enumerate(sls):
            dq, dk, dv, dbeta, dg, ds_prev = outs[i]
            d_ref[0, :, sl], d_ref[1, :, sl], d_ref[2, :, sl] = dq, dk, dv
            dgb = jnp.where(lane == i, dg, jnp.where(lane == nh + i, dbeta, dgb))
            ds_ref[i] = ds_prev
        dgb_ref[...] = dgb

    col = lambda off: pl.BlockSpec((cs, hb * hd), lambda h, n: (nc - 1 - n, off + h))
    gspec = pl.BlockSpec((cs, LANES), lambda h, n: (nc - 1 - n, 0))
    return pl.pallas_call(
        body, name=name, grid=(ng, nc),
        in_specs=[col(0), col(ng), col(2 * ng), gspec,
                  pl.BlockSpec((hb, 1, hd, hd), lambda h, n: (h, nc - 1 - n, 0, 0)),
                  pl.BlockSpec((hb, 1, cs, cs), lambda h, n: (h, nc - 1 - n, 0, 0)), col(0)],
        out_specs=(pl.BlockSpec((3, cs, hb * hd), lambda h, n: (0, nc - 1 - n, h)), gspec),
        out_shape=(jax.ShapeDtypeStruct((3, t, nh * hd), F32), jax.ShapeDtypeStruct((t, LANES), F32)),
        scratch_shapes=[pltpu.VMEM((hb, hd, hd), F32)],
        compiler_params=_params("parallel", "arbitrary"),
    )(qkv, qkv, qkv, gbeta, states, tinvs, do)


def _gdn_onorm_fwd(o, proj, norm_g, *, name):
    t = o.shape[0]
    w = GDN_KEY_DIM
    goff = 3 * GDN_KEY_DIM // w

    def body(o_ref, gp_ref, g_ref, y_ref):
        gv = g_ref[...]
        for h in range(GDN_HEADS):
            sl = slice(h * GDN_HEAD_DIM, (h + 1) * GDN_HEAD_DIM)
            oh = o_ref[:, sl]
            gp = gp_ref[:, sl]
            r = lax.rsqrt(jnp.mean(oh * oh, axis=-1, keepdims=True) + EPS)
            y_ref[:, sl] = (oh * r * gv * gp * _sigmoid(gp)).astype(y_ref.dtype)

    return pl.pallas_call(
        body, name=name, grid=(t // ROWS,),
        in_specs=[_row_spec(w), pl.BlockSpec((ROWS, w), lambda i: (i, goff)), _const_spec((1, GDN_HEAD_DIM))],
        out_specs=_row_spec(w), out_shape=jax.ShapeDtypeStruct((t, w), BF16),
        compiler_params=_params("parallel"),
    )(o, proj, norm_g)


def _gdn_onorm_bwd(o, proj, norm_g, dy, *, name):
    t = o.shape[0]
    w = GDN_KEY_DIM
    goff = 3 * GDN_KEY_DIM // w

    def body(o_ref, gp_ref, g_ref, dy_ref, do_ref, dgp_ref, st_ref):
        @pl.when(pl.program_id(0) == 0)
        def _():
            st_ref[...] = jnp.zeros_like(st_ref)

        gv = g_ref[...]
        acc = jnp.zeros((1, GDN_HEAD_DIM), F32)
        for h in range(GDN_HEADS):
            sl = slice(h * GDN_HEAD_DIM, (h + 1) * GDN_HEAD_DIM)
            oh = o_ref[:, sl]
            gp = gp_ref[:, sl]
            dyv = dy_ref[:, sl].astype(F32)
            r = lax.rsqrt(jnp.mean(oh * oh, axis=-1, keepdims=True) + EPS)
            xh = oh * r
            sg = _sigmoid(gp)
            dn = dyv * gp * sg
            dgp_ref[:, sl] = (dyv * xh * gv * sg * (1.0 + gp * (1.0 - sg))).astype(dgp_ref.dtype)
            acc = acc + jnp.sum(dn * xh, axis=0, keepdims=True)
            dxh = dn * gv
            do_ref[:, sl] = r * (dxh - xh * jnp.mean(dxh * xh, axis=-1, keepdims=True))
        st_ref[0:1, :] += acc

    return pl.pallas_call(
        body, name=name, grid=(t // ROWS,),
        in_specs=[_row_spec(w), pl.BlockSpec((ROWS, w), lambda i: (i, goff)), _const_spec((1, GDN_HEAD_DIM)),
                  _row_spec(w)],
        out_specs=(_row_spec(w), _row_spec(w), _const_spec((8, GDN_HEAD_DIM))),
        out_shape=(jax.ShapeDtypeStruct((t, w), F32), jax.ShapeDtypeStruct((t, w), BF16),
                   jax.ShapeDtypeStruct((8, GDN_HEAD_DIM), F32)),
        compiler_params=_params("arbitrary"),
    )(o, proj, norm_g, dy)


def _mla_prep_fwd(proj, qg, kvg, *, name):
    t = proj.shape[0]
    q1, k1 = MLA_Q_RANK, MLA_Q_RANK + MLA_KV_RANK

    def body(p_ref, qg_ref, kg_ref, cq_ref, ck_ref):
        cq = p_ref[:, 0:q1]
        ck = p_ref[:, q1:k1]
        cq_ref[...] = (cq * lax.rsqrt(jnp.mean(cq * cq, axis=-1, keepdims=True) + EPS) * qg_ref[...]).astype(BF16)
        ck_ref[...] = (ck * lax.rsqrt(jnp.mean(ck * ck, axis=-1, keepdims=True) + EPS) * kg_ref[...]).astype(BF16)

    return pl.pallas_call(
        body, name=name, grid=(t // ROWS,),
        in_specs=[_row_spec(MLA_IN), _const_spec((1, MLA_Q_RANK)), _const_spec((1, MLA_KV_RANK))],
        out_specs=(_row_spec(MLA_Q_RANK), _row_spec(MLA_KV_RANK)),
        out_shape=(jax.ShapeDtypeStruct((t, MLA_Q_RANK), BF16), jax.ShapeDtypeStruct((t, MLA_KV_RANK), BF16)),
        compiler_params=_params("parallel"),
    )(proj, qg, kvg)


def _mla_prep_bwd(proj, qg, kvg, dcq, dck, dkr, *, name):
    t = proj.shape[0]
    q1, k1 = MLA_Q_RANK, MLA_Q_RANK + MLA_KV_RANK

    def body(p_ref, qg_ref, kg_ref, dq_ref, dk_ref, dr_ref, dp_ref, st_ref):
        @pl.when(pl.program_id(0) == 0)
        def _():
            st_ref[...] = jnp.zeros_like(st_ref)

        for lo, hi, g_ref, d_ref in ((0, q1, qg_ref, dq_ref), (q1, k1, kg_ref, dk_ref)):
            xv = p_ref[:, lo:hi]
            dn = d_ref[...]
            r = lax.rsqrt(jnp.mean(xv * xv, axis=-1, keepdims=True) + EPS)
            xh = xv * r
            dxh = dn * g_ref[...]
            dp_ref[:, lo:hi] = (r * (dxh - xh * jnp.mean(dxh * xh, axis=-1, keepdims=True))).astype(dp_ref.dtype)
            st_ref[0:1, lo:hi] += jnp.sum(dn * xh, axis=0, keepdims=True)
        dp_ref[:, k1:MLA_IN] = dr_ref[:, 0:MLA_ROPE].astype(dp_ref.dtype)

    return pl.pallas_call(
        body, name=name, grid=(t // ROWS,),
        in_specs=[_row_spec(MLA_IN), _const_spec((1, MLA_Q_RANK)), _const_spec((1, MLA_KV_RANK)),
                  _row_spec(MLA_Q_RANK), _row_spec(MLA_KV_RANK), _row_spec(LANES)],
        out_specs=(_row_spec(MLA_IN), _const_spec((8, MLA_IN))),
        out_shape=(jax.ShapeDtypeStruct((t, MLA_IN), BF16), jax.ShapeDtypeStruct((8, MLA_IN), F32)),
        compiler_params=_params("arbitrary"),
    )(proj, qg, kvg, dcq, dck, dkr)


ATT_BLOCK = 256
ATT_HEAD_BATCH = 8
ATT_HEAD_BATCH_BWD = 4
ATT_SCALE = MLA_QK ** -0.5


def _diagonal_mask(blk):
    return lax.broadcasted_iota(jnp.int32, (blk, blk), 1) <= lax.broadcasted_iota(jnp.int32, (blk, blk), 0)


def _swap_halves(xv, first):
    return jnp.where(first, pltpu.roll(xv, LANES - MLA_ROPE // 2, 1), pltpu.roll(xv, MLA_ROPE // 2, 1))


def _rope_qk(qf, proj, cos_t, sin_t, *, name):
    t = qf.shape[0]
    nrope = MLA_HEADS * MLA_ROPE
    q_blk = MLA_HEADS * MLA_NOPE // nrope
    k_blk = (MLA_Q_RANK + MLA_KV_RANK) // LANES

    def body(q_ref, p_ref, c_ref, s_ref, qo_ref, ko_ref):
        cv, sv = c_ref[...], s_ref[...]
        lane = lax.broadcasted_iota(jnp.int32, (ROWS, LANES), 1)
        first = (lane % MLA_ROPE) < (MLA_ROPE // 2)
        for i in range(nrope // LANES):
            sl = slice(i * LANES, (i + 1) * LANES)
            xv = q_ref[:, sl].astype(F32)
            qo_ref[:, sl] = (xv * cv + _swap_halves(xv, first) * sv).astype(qo_ref.dtype)
        kv = jnp.where(lane < MLA_ROPE, p_ref[...], 0.0)
        ko_ref[...] = (kv * cv + _swap_halves(kv, first) * sv).astype(ko_ref.dtype)

    return pl.pallas_call(
        body, name=name, grid=(t // ROWS,),
        in_specs=[pl.BlockSpec((ROWS, nrope), lambda i: (i, q_blk)), pl.BlockSpec((ROWS, LANES), lambda i: (i, k_blk)),
                  _row_spec(LANES), _row_spec(LANES)],
        out_specs=(_row_spec(nrope), _row_spec(LANES)),
        out_shape=(jax.ShapeDtypeStruct((t, nrope), BF16), jax.ShapeDtypeStruct((t, LANES), BF16)),
        compiler_params=_params("parallel"),
    )(qf, proj, cos_t, sin_t)


def _rope_qk_bwd(dqr, dkr_parts, cos_t, sin_t, *, name):
    t, nrope = dqr.shape
    ng = dkr_parts.shape[0]

    def body(d_ref, k_ref, c_ref, s_ref, qo_ref, ko_ref):
        cv, sv = c_ref[...], s_ref[...]
        lane = lax.broadcasted_iota(jnp.int32, (ROWS, LANES), 1)
        first = (lane % MLA_ROPE) < (MLA_ROPE // 2)
        for i in range(nrope // LANES):
            sl = slice(i * LANES, (i + 1) * LANES)
            dv = d_ref[:, sl]
            qo_ref[:, sl] = (dv * cv + _swap_halves(dv * sv, first)).astype(qo_ref.dtype)
        dk = k_ref[0]
        for g in range(1, ng):
            dk = dk + k_ref[g]
        dk = jnp.where(lane < MLA_ROPE, dk, 0.0)
        ko_ref[...] = jnp.where(lane < MLA_ROPE, dk * cv + _swap_halves(dk * sv, first), 0.0)

    return pl.pallas_call(
        body, name=name, grid=(t // ROWS,),
        in_specs=[_row_spec(nrope), pl.BlockSpec((ng, ROWS, LANES), lambda i: (0, i, 0)), _row_spec(LANES),
                  _row_spec(LANES)],
        out_specs=(_row_spec(nrope), _row_spec(LANES)),
        out_shape=(jax.ShapeDtypeStruct((t, nrope), BF16), jax.ShapeDtypeStruct((t, LANES), F32)),
        compiler_params=_params("parallel"),
    )(dqr, dkr_parts, cos_t, sin_t)


def _attn_tm_fwd(qf, qr, kvf, kr, *, name):
    t = qf.shape[0]
    nh, dn, dr, dv = MLA_HEADS, MLA_NOPE, MLA_ROPE, MLA_V
    blk = min(ATT_BLOCK, t)
    hb = ATT_HEAD_BATCH
    hs = range(hb)

    def body(q_ref, qr_ref, kv_ref, kr_ref, o_ref, l_ref):
        i = pl.program_id(1)
        qc = [jnp.concatenate([q_ref[:, h * dn:(h + 1) * dn].astype(MXU_DTYPE), qr_ref[:, h * dr:(h + 1) * dr]], axis=1)
              for h in hs]

        def step(j, carry, diagonal=False):
            m, l, acc = carry[:hb], carry[hb:2 * hb], carry[2 * hb:]
            rows = pl.ds(pl.multiple_of(j * blk, blk), blk)
            krj = kr_ref[rows, 0:dr]
            s = [_dotb(qc[h], jnp.concatenate([kv_ref[rows, h * (dn + dv):h * (dn + dv) + dn], krj], axis=1), NT)
                 for h in hs]
            s = [s[h] * ATT_SCALE for h in hs]
            if diagonal:
                mask = _diagonal_mask(blk)
                s = [jnp.where(mask, s[h], NEG) for h in hs]
            m_new = [jnp.maximum(m[h], jnp.max(s[h], axis=-1, keepdims=True)) for h in hs]
            p = [jnp.exp(s[h] - m_new[h]) for h in hs]
            pv = [_dotb(p[h], kv_ref[rows, h * (dn + dv) + dn:(h + 1) * (dn + dv)], NN) for h in hs]
            alpha = [jnp.exp(m[h] - m_new[h]) for h in hs]
            l = [alpha[h] * l[h] + jnp.sum(p[h], axis=-1, keepdims=True) for h in hs]
            acc = [alpha[h] * acc[h] + pv[h] for h in hs]
            return tuple(m_new) + tuple(l) + tuple(acc)

        init = ((jnp.full((blk, 1), NEG, F32),) * hb + (jnp.zeros((blk, 1), F32),) * hb
                + (jnp.zeros((blk, dv), F32),) * hb)
        out = step(i, lax.fori_loop(0, i, step, init), diagonal=True)
        for h in hs:
            m, l, acc = out[h], out[hb + h], out[2 * hb + h]
            o_ref[:, h * dv:(h + 1) * dv] = (acc / l).astype(o_ref.dtype)
            l_ref[h] = jnp.broadcast_to(m + jnp.log(l), (blk, LANES))

    return pl.pallas_call(
        body, name=name, grid=(nh // hb, t // blk),
        in_specs=[pl.BlockSpec((blk, hb * dn), lambda g, i: (i, g)), pl.BlockSpec((blk, hb * dr), lambda g, i: (i, g)),
                  pl.BlockSpec((t, hb * (dn + dv)), lambda g, i: (0, g)), pl.BlockSpec((t, LANES), lambda g, i: (0, 0))],
        out_specs=(pl.BlockSpec((blk, hb * dv), lambda g, i: (i, g)),
                   pl.BlockSpec((hb, blk, LANES), lambda g, i: (g, i, 0))),
        out_shape=(jax.ShapeDtypeStruct((t, nh * dv), BF16), jax.ShapeDtypeStruct((nh, t, LANES), F32)),
        compiler_params=_params("parallel", "parallel"),
    )(qf, qr, kvf, kr)


def _attn_tm_bwd(qf, qr, kvf, kr, o, lse, do, *, name):
    t = qf.shape[0]
    nh, dn, dr, dv = MLA_HEADS, MLA_NOPE, MLA_ROPE, MLA_V
    blk = min(ATT_BLOCK, t)
    nb = t // blk
    hb = ATT_HEAD_BATCH_BWD
    hs = range(hb)
    ng = nh // hb

    def body(q_ref, qr_ref, kv_ref, kr_ref, o_ref, l_ref, do_ref, dqn_ref, dqr_ref, dkv_ref, dkr_ref):
        j = pl.program_id(1)

        @pl.when(j == 0)
        def _():
            dqn_ref[...] = jnp.zeros_like(dqn_ref)
            dqr_ref[...] = jnp.zeros_like(dqr_ref)

        krj = kr_ref[:, 0:dr]
        kc = [jnp.concatenate([kv_ref[:, h * (dn + dv):h * (dn + dv) + dn], krj], axis=1) for h in hs]
        vv = [kv_ref[:, h * (dn + dv) + dn:(h + 1) * (dn + dv)] for h in hs]

        def step(i, carry, diagonal=False):
            dkn_acc, dv_acc, dkr_acc = carry[:hb], carry[hb:2 * hb], carry[2 * hb]
            rows = pl.ds(pl.multiple_of(i * blk, blk), blk)
            qc = [jnp.concatenate([q_ref[rows, h * dn:(h + 1) * dn].astype(MXU_DTYPE),
                                   qr_ref[rows, h * dr:(h + 1) * dr]], axis=1) for h in hs]
            dov = [do_ref[rows, h * dv:(h + 1) * dv] for h in hs]
            s = [_dotb(qc[h], kc[h], NT) for h in hs]
            dp = [_dotb(dov[h], vv[h], NT) for h in hs]
            s = [s[h] * ATT_SCALE for h in hs]
            if diagonal:
                mask = _diagonal_mask(blk)
                s = [jnp.where(mask, s[h], NEG) for h in hs]
            p = [jnp.exp(s[h] - l_ref[h, rows, :][:, 0:1]) for h in hs]
            delta = [jnp.sum(dov[h].astype(F32) * o_ref[rows, h * dv:(h + 1) * dv].astype(F32), axis=-1, keepdims=True)
                     for h in hs]
            ds = [p[h] * (dp[h] - delta[h]) * ATT_SCALE for h in hs]
            dvn = [_dotb(p[h], dov[h], TN) for h in hs]
            dkc = [_dotb(ds[h], qc[h], TN) for h in hs]
            dqc = [_dotb(ds[h], kc[h], NN) for h in hs]
            for h in hs:
                dqn_ref[rows, h * dn:(h + 1) * dn] += dqc[h][:, 0:dn]
                dqr_ref[rows, h * dr:(h + 1) * dr] += dqc[h][:, dn:dn + dr]
            dkr_new = dkr_acc
            for h in hs:
                dkr_new = dkr_new + dkc[h][:, dn:dn + dr]
            return (tuple(dkn_acc[h] + dkc[h][:, 0:dn] for h in hs) + tuple(dv_acc[h] + dvn[h] for h in hs)
                    + (dkr_new,))

        init = (jnp.zeros((blk, dn), F32),) * hb + (jnp.zeros((blk, dv), F32),) * hb + (jnp.zeros((blk, dr), F32),)
        out = lax.fori_loop(j + 1, nb, step, step(j, init, diagonal=True))
        for h in hs:
            dkv_ref[:, h * (dn + dv):h * (dn + dv) + dn] = out[h].astype(dkv_ref.dtype)
            dkv_ref[:, h * (dn + dv) + dn:(h + 1) * (dn + dv)] = out[hb + h].astype(dkv_ref.dtype)
        dkr_ref[0, :, 0:dr] = out[2 * hb]
        dkr_ref[0, :, dr:LANES] = jnp.zeros((blk, LANES - dr), F32)

    full = lambda w: pl.BlockSpec((t, w), lambda g, j: (0, g))
    return pl.pallas_call(
        body, name=name, grid=(ng, nb),
        in_specs=[full(hb * dn), full(hb * dr), pl.BlockSpec((blk, hb * (dn + dv)), lambda g, j: (j, g)),
                  pl.BlockSpec((blk, LANES), lambda g, j: (j, 0)), full(hb * dv),
                  pl.BlockSpec((hb, t, LANES), lambda g, j: (g, 0, 0)), full(hb * dv)],
        out_specs=(full(hb * dn), full(hb * dr), pl.BlockSpec((blk, hb * (dn + dv)), lambda g, j: (j, g)),
                   pl.BlockSpec((1, blk, LANES), lambda g, j: (g, j, 0))),
        out_shape=(jax.ShapeDtypeStruct((t, nh * dn), F32), jax.ShapeDtypeStruct((t, nh * dr), F32),
                   jax.ShapeDtypeStruct((t, nh * (dn + dv)), BF16), jax.ShapeDtypeStruct((ng, t, LANES), F32)),
        compiler_params=_params("parallel", "arbitrary"),
    )(qf, qr, kvf, kr, o, lse, do)


def _ada_mod(c_all, ada_w, ada_b_cols, *, name):
    nl, d, wc = ada_w.shape

    def body(c_ref, w_ref, b_ref, o_ref):
        cv = c_ref[...]
        o_ref[0] = _dotb(cv * _sigmoid(cv), w_ref[0], NN) + b_ref[0]

    return pl.pallas_call(
        body, name=name, grid=(nl,),
        in_specs=[_const_spec((N_DEV, d)), pl.BlockSpec((1, d, wc), lambda l: (l, 0, 0)),
                  pl.BlockSpec((1, 1, wc), lambda l: (l, 0, 0))],
        out_specs=pl.BlockSpec((1, N_DEV, wc), lambda l: (l, 0, 0)),
        out_shape=jax.ShapeDtypeStruct((nl, N_DEV, wc), F32), compiler_params=_params("parallel"),
    )(c_all, ada_w, ada_b_cols)


def _adam_math(g, w, m, v):
    m2 = ADAM_B1 * m + (1.0 - ADAM_B1) * g
    v2 = ADAM_B2 * v + (1.0 - ADAM_B2) * (g * g)
    delta = -ADAM_LR * ((m2 / ADAM_BC1) / (jnp.sqrt(v2 / ADAM_BC2) + ADAM_EPS) + ADAM_WD * w)
    return delta, m2, v2


def _ada_grad_adamw(c_all, dmod_cols, w, m, v, *, name):
    nl, d, wc = w.shape
    tr = 256

    def body(c_ref, dm_ref, w_ref, m_ref, v_ref, g_ref, d_ref, m2_ref, v2_ref):
        cv = c_ref[...]
        g = _dotf(cv * _sigmoid(cv), dm_ref[0], TN)
        delta, m2, v2 = _adam_math(g, w_ref[0], m_ref[0], v_ref[0])
        g_ref[0], d_ref[0], m2_ref[0], v2_ref[0] = g, delta, m2, v2

    blk = pl.BlockSpec((1, tr, wc), lambda l, i: (l, i, 0))
    return pl.pallas_call(
        body, name=name, grid=(nl, d // tr),
        in_specs=[pl.BlockSpec((N_DEV, tr), lambda l, i: (0, i)), pl.BlockSpec((1, N_DEV, wc), lambda l, i: (l, 0, 0)),
                  blk, blk, blk],
        out_specs=(blk,) * 4, out_shape=(jax.ShapeDtypeStruct(w.shape, F32),) * 4,
        compiler_params=_params("parallel", "parallel"),
    )(c_all, dmod_cols, w, m, v)


def _adamw(parts, w, m, v, *, name):
    nl, r, c = w.shape
    ns = parts[0].shape[0]
    lanes_padded = -(-c // LANES) * LANES
    row_bytes = 2 * nl * ns * lanes_padded * parts[0].dtype.itemsize
    tr = _pick(r, min(256, max(16, (VMEM_LIMIT // 2) // row_bytes)), 16)
    tc = c
    if tr * row_bytes > VMEM_LIMIT // 2:
        tc = _pick(c, max(LANES, c * (VMEM_LIMIT // 2) // (tr * row_bytes)))

    def body(*refs):
        p_refs = refs[:nl]
        w_ref, m_ref, v_ref, g_ref, d_ref, m2_ref, v2_ref = refs[nl:]
        layer = pl.program_id(0)
        for q in range(nl):
            @pl.when(layer == q)
            def _(q=q):
                g = p_refs[q][0].astype(F32)
                for s in range(1, ns):
                    g = g + p_refs[q][s].astype(F32)
                delta, m2, v2 = _adam_math(g, w_ref[0], m_ref[0], v_ref[0])
                g_ref[0], d_ref[0], m2_ref[0], v2_ref[0] = g, delta, m2, v2

    blk = pl.BlockSpec((1, tr, tc), lambda l, i, j: (l, i, j))
    p_specs = [pl.BlockSpec((ns, tr, tc), lambda l, i, j, q=q: (0, jnp.where(l == q, i, 0), jnp.where(l == q, j, 0)))
               for q in range(nl)]
    return pl.pallas_call(
        body, name=name, grid=(nl, r // tr, c // tc),
        in_specs=p_specs + [blk, blk, blk],
        out_specs=(blk,) * 4, out_shape=(jax.ShapeDtypeStruct(w.shape, F32),) * 4,
        compiler_params=_params("arbitrary", "arbitrary", "arbitrary"),
    )(*parts, w, m, v)


def _sum_parts(parts, *, name):
    ns, r, c = parts.shape

    def body(p_ref, o_ref):
        acc = p_ref[0]
        for s in range(1, ns):
            acc = acc + p_ref[s]
        o_ref[...] = acc

    return pl.pallas_call(
        body, name=name, out_shape=jax.ShapeDtypeStruct((r, c), F32),
        in_specs=[pl.BlockSpec(memory_space=pltpu.VMEM)], out_specs=pl.BlockSpec(memory_space=pltpu.VMEM),
    )(parts)


def _pack(arrs):
    flat = jnp.concatenate([a.reshape(-1).astype(F32) for a in arrs])
    pad = (-flat.shape[0]) % (8 * LANES)
    return jnp.pad(flat, (0, pad)).reshape(-1, LANES)


def _unpack(packed, shapes, lead=()):
    flat = packed.reshape(lead + (-1,))
    out, off = [], 0
    for s in shapes:
        n = math.prod(s)
        out.append(flat[..., off:off + n].reshape(lead + tuple(s)))
        off += n
    return out


def _gather_rows(g):
    _, nl, rs, c = g.shape
    return jnp.transpose(g, (1, 0, 2, 3)).reshape(nl, N_DEV * rs, c)


def _row(v):
    return v.reshape(1, -1)


def _local_step(x, target, mod, cos_t, sin_t, rep, get_weights, put_grads):
    t = x.shape[0]
    saved = []
    for layer in range(DEPTH):
        j = layer // 2
        tag = f"l{layer}"
        shift_m, scale_m, gate_m, shift_f, scale_f, gate_f = [_row(mod[layer, i]) for i in range(N_MOD)]
        lw = dict(get_weights(layer, "mix", x))
        rec = {"x0": x, "lw": lw}
        h = _adaln_fwd(x, _row(rep["norm_mix_g"][layer]), scale_m, shift_m, name=f"adaln_mix_{tag}")
        rec["h"] = h
        if layer % 2 == 0:
            proj = _mm(h, lw["wt_in"], mode="nt", out_dtype=F32, tm=256, tn=GDN_MAIN, b_rows=GDN_MAIN,
                       dep=lw["dep_mix"], name=f"gdn_in_{tag}")
            ab = _mm(h, lw["wt_ab"], mode="nt", out_dtype=F32, name=f"gdn_in_ab_{tag}")
            qkv = _gdn_prep_fwd(proj, rep["gdn_conv_wt"][j], name=f"gdn_prep_{tag}")
            gbeta = _gdn_gate_fwd(ab, rep["gdn_gate_prm"][j], name=f"gdn_gate_{tag}")
            o, states, tinvs = _gdn_chunk_fwd(qkv, gbeta, name=f"gdn_chunk_{tag}")
            og = _gdn_onorm_fwd(o, proj, _row(rep["gdn_norm_g"][j]), name=f"gdn_onorm_{tag}")
            x, y = _mm_resid(og, lw["w_out"], x, gate_m, name=f"gdn_out_{tag}")
            rec.update(proj=proj, ab=ab, qkv=qkv, gbeta=gbeta, states=states, tinvs=tinvs, o=o, og=og, y=y)
        else:
            proj = _mm(h, lw["w_in"], mode="nn", out_dtype=F32, dep=lw["dep_mix"], name=f"mla_in_{tag}")
            cq, ck = _mla_prep_fwd(proj, _row(rep["mla_q_norm_g"][j]), _row(rep["mla_kv_norm_g"][j]),
                                   name=f"mla_prep_{tag}")
            qf = _mm(cq, lw["wt_uq"], mode="nt", out_dtype=BF16, name=f"mla_uq_{tag}")
            kvf = _mm(ck, lw["w_ukv"], mode="nn", out_dtype=BF16, name=f"mla_ukv_{tag}")
            qr, kr = _rope_qk(qf, proj, cos_t, sin_t, name=f"rope_{tag}")
            oc, lse = _attn_tm_fwd(qf, qr, kvf, kr, name=f"attn_{tag}")
            x, y = _mm_resid(oc, lw["w_out"], x, gate_m, name=f"mla_out_{tag}")
            rec.update(proj=proj, cq=cq, ck=ck, qf=qf, qr=qr, kvf=kvf, kr=kr, lse=lse, oc=oc, y=y)
        rec["x1"] = x
        lw.update(get_weights(layer, "ffn", x))
        h2 = _adaln_fwd(x, _row(rep["norm_ffn_g"][layer]), scale_f, shift_f, name=f"adaln_ffn_{tag}")
        s, a2, b2 = _ffn_gu_fwd(h2, lw["wt_g"], lw["wt_u"], lw["dep_ffn"], name=f"ffn_gu_{tag}")
        x, y2 = _mm_resid(s, lw["w_down"], x, gate_f, tm=512, name=f"ffn_down_{tag}")
        rec.update(h2=h2, a2=a2, b2=b2, s=s, y2=y2)
        saved.append(rec)

    dx, st, ls, dy2 = _loss_head(x, _row(rep["final_norm_g"]), target, saved[-1]["y2"],
                                 _row(mod[DEPTH - 1, N_MOD - 1]), name="loss_head")
    loss = ls[0, 0]
    dgate_f = st[3]
    grads = {"final_norm_g": st[0]}
    per_layer = {k: [None] * DEPTH for k in ("norm_mix_g", "norm_ffn_g")}
    per_gdn = {k: [None] * 2 for k in ("gdn_conv_wt", "gdn_a_log", "gdn_dt_bias", "gdn_norm_g")}
    per_mla = {k: [None] * 2 for k in ("mla_q_norm_g", "mla_kv_norm_g")}
    dmod = [None] * DEPTH

    for layer in reversed(range(DEPTH)):
        j = layer // 2
        tag = f"l{layer}"
        rec = saved[layer]
        lw = rec["lw"]
        shift_m, scale_m, gate_m, shift_f, scale_f, gate_f = [_row(mod[layer, i]) for i in range(N_MOD)]
        dw_down = _mm(rec["s"], dy2, mode="tn", out_dtype=BF16, tm=FFN_BLOCK, tn=1024, name=f"ffn_down_dw_{tag}")
        da2, db2 = _ffn_down_dx(dy2, lw["w_down"], rec["a2"], rec["b2"], name=f"ffn_down_dx_{tag}")
        dwt_g = _mm(da2, rec["h2"], mode="tn", out_dtype=BF16, tm=FFN_BLOCK, tn=1024, name=f"ffn_g_dw_{tag}")
        dwt_u = _mm(db2, rec["h2"], mode="tn", out_dtype=BF16, tm=FFN_BLOCK, tn=1024, name=f"ffn_u_dw_{tag}")
        dep = put_grads(layer, "ffn", {"wt_g": dwt_g, "wt_u": dwt_u, "w_down": dw_down})
        dh2 = _mm_pair(da2, lw["wt_g"], db2, lw["wt_u"], out_dtype=BF16, name=f"ffn_gu_dx_{tag}")
        dx, st_n, dy = _adaln_gate_bwd(rec["x1"], _row(rep["norm_ffn_g"][layer]), scale_f, shift_f, dh2, dx, dep,
                                       rec["y"], gate_m, name=f"adaln_ffn_bwd_{tag}")
        per_layer["norm_ffn_g"][layer] = st_n[0]
        dscale_f, dshift_f, dgate_m = st_n[1], st_n[2], st_n[3]
        big = {}
        if layer % 2 == 0:
            big["w_out"] = _mm(rec["og"], dy, mode="tn", out_dtype=BF16, name=f"gdn_out_dw_{tag}")
            dog = _mm(dy, lw["w_out"], mode="nt", out_dtype=BF16, name=f"gdn_out_dx_{tag}")
            do, dgp, st_o = _gdn_onorm_bwd(rec["o"], rec["proj"], _row(rep["gdn_norm_g"][j]), dog,
                                           name=f"gdn_onorm_bwd_{tag}")
            per_gdn["gdn_norm_g"][j] = st_o[0]
            dqkv, dgb = _gdn_chunk_bwd(rec["qkv"], rec["gbeta"], rec["states"], rec["tinvs"], do,
                                       name=f"gdn_chunk_bwd_{tag}")
            dab, st_a = _gdn_gate_bwd(rec["ab"], rep["gdn_gate_prm"][j], dgb, name=f"gdn_gate_bwd_{tag}")
            per_gdn["gdn_a_log"][j] = st_a[0, :GDN_HEADS]
            per_gdn["gdn_dt_bias"][j] = st_a[1, :GDN_HEADS]
            dpre, dcw = _gdn_prep_bwd(rec["proj"], rep["gdn_conv_wt"][j], dqkv, name=f"gdn_prep_bwd_{tag}")
            per_gdn["gdn_conv_wt"][j] = dcw
            dproj = jnp.concatenate([dpre, dgp], axis=1)
            dw_main = _mm(dproj, rec["h"], mode="tn", out_dtype=BF16, tm=512, tn=1024, name=f"gdn_in_dw_{tag}")
            dw_ab = _mm(dab, rec["h"], mode="tn", out_dtype=BF16, tn=1024, name=f"gdn_in_ab_dw_{tag}")
            big["wt_in"] = jnp.concatenate([dw_main, dw_ab[:2 * GDN_HEADS]], axis=0)
            dep = put_grads(layer, "gdn", big)
            dh = _mm_pair(dproj, lw["wt_in"], dab, lw["wt_ab"], out_dtype=BF16, b1_rows=GDN_MAIN,
                          name=f"gdn_in_dx_{tag}")
        else:
            big["w_out"] = _mm(rec["oc"], dy, mode="tn", out_dtype=BF16, name=f"mla_out_dw_{tag}")
            doc = _mm(dy, lw["w_out"], mode="nt", out_dtype=BF16, name=f"mla_out_dx_{tag}")
            dqn, dqr, dkvf, dkr_parts = _attn_tm_bwd(rec["qf"], rec["qr"], rec["kvf"], rec["kr"], rec["oc"],
                                                     rec["lse"], doc, name=f"attn_bwd_{tag}")
            dqr_un, dkr_un = _rope_qk_bwd(dqr, dkr_parts, cos_t, sin_t, name=f"rope_bwd_{tag}")
            n_nope = MLA_HEADS * MLA_NOPE
            big["wt_uq"] = jnp.concatenate(
                [_mm(dqn, rec["cq"], mode="tn", out_dtype=BF16, name=f"mla_uq_dw_nope_{tag}"),
                 _mm(dqr_un, rec["cq"], mode="tn", out_dtype=BF16, name=f"mla_uq_dw_rope_{tag}")], axis=0)
            big["w_ukv"] = _mm(rec["ck"], dkvf, mode="tn", out_dtype=BF16, name=f"mla_ukv_dw_{tag}")
            dcq = _mm_pair(dqn, lw["wt_uq"], dqr_un, lw["wt_uq"][n_nope:], out_dtype=F32, b1_rows=n_nope,
                           name=f"mla_uq_dx_{tag}")
            dck = _mm(dkvf, lw["w_ukv"], mode="nt", out_dtype=F32, name=f"mla_ukv_dx_{tag}")
            dproj, st_p = _mla_prep_bwd(rec["proj"], _row(rep["mla_q_norm_g"][j]), _row(rep["mla_kv_norm_g"][j]),
                                        dcq, dck, dkr_un, name=f"mla_prep_bwd_{tag}")
            per_mla["mla_q_norm_g"][j] = st_p[0, :MLA_Q_RANK]
            per_mla["mla_kv_norm_g"][j] = st_p[0, MLA_Q_RANK:MLA_Q_RANK + MLA_KV_RANK]
            big["w_in"] = _mm(rec["h"], dproj, mode="tn", out_dtype=BF16, name=f"mla_in_dw_{tag}")
            dep = put_grads(layer, "mla", big)
            dh = _mm(dproj, lw["w_in"], mode="nt", out_dtype=BF16, name=f"mla_in_dx_{tag}")
        if layer > 0:
            below = saved[layer - 1]
            dx, st_n, dy2 = _adaln_gate_bwd(rec["x0"], _row(rep["norm_mix_g"][layer]), scale_m, shift_m, dh, dx, dep,
                                            below["y2"], _row(mod[layer - 1, N_MOD - 1]),
                                            name=f"adaln_mix_bwd_{tag}")
        else:
            dx, st_n = _adaln_bwd(rec["x0"], _row(rep["norm_mix_g"][layer]), scale_m, shift_m, dh, dx, dep,
                                  name=f"adaln_mix_bwd_{tag}")
        per_layer["norm_mix_g"][layer] = st_n[0]
        dmod[layer] = jnp.stack([st_n[2], st_n[1], dgate_m, dshift_f, dscale_f, dgate_f])
        if layer > 0:
            dgate_f = st_n[3]

    for d in (per_layer, per_gdn, per_mla):
        for k, v in d.items():
            grads[k] = jnp.stack(v)
    return loss, dx, jnp.stack(dmod), grads


BIG = ("gdn_w_in", "gdn_w_out", "mla_w_in", "mla_w_uq", "mla_w_ukv", "mla_w_out", "ffn_w_gate", "ffn_w_up",
       "ffn_w_down")
TRANSPOSED = ("gdn_w_in", "mla_w_uq", "ffn_w_gate", "ffn_w_up")
AHEAD = 4


def _view(k, a):
    return jnp.transpose(a, (0, 2, 1)) if k in TRANSPOSED else a
SMALL = ("ada_b", "norm_mix_g", "norm_ffn_g", "gdn_conv_w", "gdn_a_log", "gdn_dt_bias", "gdn_norm_g",
         "mla_q_norm_g", "mla_kv_norm_g", "final_norm_g")
WEIGHTS = ("ada_w", "ada_b", "norm_mix_g", "norm_ffn_g", "gdn_w_in", "gdn_conv_w", "gdn_a_log", "gdn_dt_bias",
           "gdn_norm_g", "gdn_w_out", "mla_w_in", "mla_q_norm_g", "mla_kv_norm_g", "mla_w_uq", "mla_w_ukv",
           "mla_w_out", "ffn_w_gate", "ffn_w_up", "ffn_w_down", "final_norm_g")


def _uq_to_kernel_layout(w, axis=-1):
    axis = axis % w.ndim
    lead, tail = w.shape[:axis], w.shape[axis + 1:]
    w4 = w.reshape(lead + (MLA_HEADS, MLA_QK) + tail)
    nope = lax.slice_in_dim(w4, 0, MLA_NOPE, axis=axis + 1).reshape(lead + (-1,) + tail)
    rope = lax.slice_in_dim(w4, MLA_NOPE, MLA_QK, axis=axis + 1).reshape(lead + (-1,) + tail)
    return jnp.concatenate([nope, rope], axis=axis)


def _uq_from_kernel_layout(w, axis=-1):
    axis = axis % w.ndim
    lead, tail = w.shape[:axis], w.shape[axis + 1:]
    nope = lax.slice_in_dim(w, 0, MLA_HEADS * MLA_NOPE, axis=axis).reshape(lead + (MLA_HEADS, MLA_NOPE) + tail)
    rope = lax.slice_in_dim(w, MLA_HEADS * MLA_NOPE, MLA_HEADS * MLA_QK, axis=axis).reshape(
        lead + (MLA_HEADS, MLA_ROPE) + tail)
    return jnp.concatenate([nope, rope], axis=axis + 1).reshape(lead + (-1,) + tail)


def _group_names(layer, kind):
    if kind == "ffn":
        return ("ffn_w_gate", "ffn_w_up", "ffn_w_down")
    return ("gdn_w_in", "gdn_w_out") if layer % 2 == 0 else ("mla_w_in", "mla_w_uq", "mla_w_ukv", "mla_w_out")


def _layer_index(name, layer):
    return layer if name.startswith("ffn") else layer // 2


def _cols(g):
    return jnp.transpose(g, (1, 0, 2)).reshape(g.shape[1], N_DEV * g.shape[2])


def _rows(g):
    return g.reshape(N_DEV * g.shape[1], g.shape[2])


def _uncols(full):
    r, c = full.shape
    return jnp.transpose(full.reshape(r, N_DEV, c // N_DEV), (1, 0, 2))


def _unrows(full):
    r, c = full.shape
    return full.reshape(N_DEV, r // N_DEV, c)


def _group_weights(layer, kind, got, token):
    if kind == "ffn":
        return {"wt_g": _rows(got["ffn_w_gate"]), "wt_u": _rows(got["ffn_w_up"]), "w_down": _rows(got["ffn_w_down"]),
                "dep_ffn": token}
    if layer % 2 == 0:
        wt_in = _rows(got["gdn_w_in"])
        return dict(wt_in=wt_in, wt_ab=jnp.pad(wt_in[GDN_MAIN:], ((0, LANES - 2 * GDN_HEADS), (0, 0))),
                    w_out=_rows(got["gdn_w_out"]), dep_mix=token)
    return dict(w_in=_rows(got["mla_w_in"]), wt_uq=_uq_to_kernel_layout(_rows(got["mla_w_uq"]), axis=0),
                w_ukv=_cols(got["mla_w_ukv"]), w_out=_rows(got["mla_w_out"]), dep_mix=token)


def _layer_grad_slots(kind, big):
    if kind == "ffn":
        return {"ffn_w_gate": _unrows(big["wt_g"]), "ffn_w_up": _unrows(big["wt_u"]),
                "ffn_w_down": _unrows(big["w_down"])}
    if kind == "gdn":
        return {"gdn_w_in": _unrows(big["wt_in"]), "gdn_w_out": _unrows(big["w_out"])}
    return {"mla_w_in": _unrows(big["w_in"]), "mla_w_uq": _unrows(_uq_from_kernel_layout(big["wt_uq"], axis=0)),
            "mla_w_ukv": _uncols(big["w_ukv"]), "mla_w_out": _unrows(big["w_out"])}


def _small_weights(tiny, rep):
    prm = jnp.zeros((2, 8, LANES), F32)
    prm = prm.at[:, 0, :GDN_HEADS].set(rep["gdn_a_log"]).at[:, 1, :GDN_HEADS].set(rep["gdn_dt_bias"])
    out = {
        "gdn_conv_wt": jnp.transpose(_gather_rows(tiny["gdn_conv_w"]), (0, 2, 1)),
        "mla_q_norm_g": jnp.transpose(tiny["mla_q_norm_g"], (1, 0, 2)).reshape(2, MLA_Q_RANK),
        "mla_kv_norm_g": jnp.transpose(tiny["mla_kv_norm_g"], (1, 0, 2)).reshape(2, MLA_KV_RANK),
        "gdn_gate_prm": prm,
    }
    for k in ("norm_mix_g", "norm_ffn_g", "gdn_norm_g", "final_norm_g"):
        out[k] = rep[k]
    return out


def _rope_tables(positions):
    inv_freq = ROPE_THETA ** (-jnp.arange(0, MLA_ROPE, 2, dtype=F32) / MLA_ROPE)
    ang = positions.astype(F32)[:, None] * inv_freq
    cos, sin = jnp.cos(ang), jnp.sin(ang)
    reps = LANES // MLA_ROPE
    return jnp.tile(jnp.concatenate([cos, cos], axis=1), (1, reps)), jnp.tile(
        jnp.concatenate([-sin, sin], axis=1), (1, reps))


def kernel(x, c, positions, ada_w, ada_b, norm_mix_g, norm_ffn_g, gdn_w_in, gdn_conv_w, gdn_a_log, gdn_dt_bias, gdn_norm_g, gdn_w_out, mla_w_in, mla_q_norm_g, mla_kv_norm_g, mla_w_uq, mla_w_ukv, mla_w_out, ffn_w_gate, ffn_w_up, ffn_w_down, final_norm_g, loss_target, m_ada_w, m_ada_b, m_norm_mix_g, m_norm_ffn_g, m_gdn_w_in, m_gdn_conv_w, m_gdn_a_log, m_gdn_dt_bias, m_gdn_norm_g, m_gdn_w_out, m_mla_w_in, m_mla_q_norm_g, m_mla_kv_norm_g, m_mla_w_uq, m_mla_w_ukv, m_mla_w_out, m_ffn_w_gate, m_ffn_w_up, m_ffn_w_down, m_final_norm_g, v_ada_w, v_ada_b, v_norm_mix_g, v_norm_ffn_g, v_gdn_w_in, v_gdn_conv_w, v_gdn_a_log, v_gdn_dt_bias, v_gdn_norm_g, v_gdn_w_out, v_mla_w_in, v_mla_q_norm_g, v_mla_kv_norm_g, v_mla_w_uq, v_mla_w_ukv, v_mla_w_out, v_ffn_w_gate, v_ffn_w_up, v_ffn_w_down, v_final_norm_g):
    W = dict(ada_w=ada_w, ada_b=ada_b, norm_mix_g=norm_mix_g, norm_ffn_g=norm_ffn_g, gdn_w_in=gdn_w_in,
             gdn_conv_w=gdn_conv_w, gdn_a_log=gdn_a_log, gdn_dt_bias=gdn_dt_bias, gdn_norm_g=gdn_norm_g,
             gdn_w_out=gdn_w_out, mla_w_in=mla_w_in, mla_q_norm_g=mla_q_norm_g, mla_kv_norm_g=mla_kv_norm_g,
             mla_w_uq=mla_w_uq, mla_w_ukv=mla_w_ukv, mla_w_out=mla_w_out, ffn_w_gate=ffn_w_gate,
             ffn_w_up=ffn_w_up, ffn_w_down=ffn_w_down, final_norm_g=final_norm_g)
    M = dict(ada_w=m_ada_w, ada_b=m_ada_b, norm_mix_g=m_norm_mix_g, norm_ffn_g=m_norm_ffn_g, gdn_w_in=m_gdn_w_in,
             gdn_conv_w=m_gdn_conv_w, gdn_a_log=m_gdn_a_log, gdn_dt_bias=m_gdn_dt_bias, gdn_norm_g=m_gdn_norm_g,
             gdn_w_out=m_gdn_w_out, mla_w_in=m_mla_w_in, mla_q_norm_g=m_mla_q_norm_g,
             mla_kv_norm_g=m_mla_kv_norm_g, mla_w_uq=m_mla_w_uq, mla_w_ukv=m_mla_w_ukv, mla_w_out=m_mla_w_out,
             ffn_w_gate=m_ffn_w_gate, ffn_w_up=m_ffn_w_up, ffn_w_down=m_ffn_w_down, final_norm_g=m_final_norm_g)
    V = dict(ada_w=v_ada_w, ada_b=v_ada_b, norm_mix_g=v_norm_mix_g, norm_ffn_g=v_norm_ffn_g, gdn_w_in=v_gdn_w_in,
             gdn_conv_w=v_gdn_conv_w, gdn_a_log=v_gdn_a_log, gdn_dt_bias=v_gdn_dt_bias, gdn_norm_g=v_gdn_norm_g,
             gdn_w_out=v_gdn_w_out, mla_w_in=v_mla_w_in, mla_q_norm_g=v_mla_q_norm_g,
             mla_kv_norm_g=v_mla_kv_norm_g, mla_w_uq=v_mla_w_uq, mla_w_ukv=v_mla_w_ukv, mla_w_out=v_mla_w_out,
             ffn_w_gate=v_ffn_w_gate, ffn_w_up=v_ffn_w_up, ffn_w_down=v_ffn_w_down, final_norm_g=v_final_norm_g)
    me = 4 * lax.axis_index("x") + 2 * lax.axis_index("y") + lax.axis_index("c")
    t = x.shape[1]
    wc = ada_w.shape[-1]

    groups = [(layer, kind) for layer in range(DEPTH) for kind in ("mix", "ffn")]

    def group_srcs(i):
        layer, kind = groups[i]
        return [_view(k, W[k])[_layer_index(k, layer)].astype(BF16) for k in _group_names(layer, kind)]

    tiny_shapes = [c.shape, gdn_conv_w.shape, mla_q_norm_g.shape, mla_kv_norm_g.shape]
    first = _gather_two_level([_pack([c, gdn_conv_w, mla_q_norm_g, mla_kv_norm_g])] + group_srcs(0),
                              name="gather_first")
    tiny_g = first[0]
    c_g, conv_g, qn_g, kvn_g = _unpack(tiny_g, tiny_shapes, lead=(N_DEV,))
    c_all = c_g.reshape(N_DEV, D_MODEL)
    rep = _small_weights({"gdn_conv_w": conv_g, "mla_q_norm_g": qn_g, "mla_kv_norm_g": kvn_g}, W)

    def start_group(i, dep):
        layer, kind = groups[i]
        return _exchange_start(group_srcs(i), scatter=False, name=f"gather_start_{kind}_l{layer}", dep=dep)


    b_cols = lax.dynamic_slice_in_dim(ada_b, me * wc, wc, axis=1).reshape(DEPTH, 1, wc)
    mod_part = _ada_mod(c_all, ada_w, b_cols, name="ada_mod")
    (mod_g,) = _exchange([mod_part], scatter=False, name="gather_mod")
    mod_mine = lax.dynamic_index_in_dim(mod_g, me, axis=2, keepdims=False)
    mod = jnp.transpose(mod_mine, (1, 0, 2)).reshape(DEPTH, N_MOD, D_MODEL)
    gather = {1: start_group(1, mod_g)}
    for i in range(2, AHEAD + 1):
        gather[i] = start_group(i, gather[i - 1][4])

    def get_weights(layer, kind, after):
        i = groups.index((layer, kind))
        names = _group_names(layer, kind)
        if i == 0:
            return _group_weights(layer, kind, dict(zip(names, first[1:])), gather[AHEAD][4])
        srcs, lands = _exchange_wait(gather[i], after, scatter=False, name=f"gather_wait_{kind}_l{layer}")
        token = jnp.zeros((8, LANES), F32)
        if i + AHEAD < len(groups):
            gather[i + AHEAD] = start_group(i + AHEAD, lands[0])
            token = gather[i + AHEAD][4]
        got = {k: lax.dynamic_update_index_in_dim(z, s, me, 0) for k, s, z in zip(names, srcs, lands)}
        return _group_weights(layer, kind, got, token)

    scatter = []

    def put_grads(layer, kind, big):
        slots = _layer_grad_slots(kind, big)
        started = _exchange_start(list(slots.values()), scatter=True, name=f"scatter_start_{kind}_l{layer}")
        scatter.append((layer, kind, list(slots.keys()), started))
        return started[4]

    cos_t, sin_t = _rope_tables(positions[0])
    loss, dx, dmod, g = _local_step(x[0], loss_target[0], mod, cos_t, sin_t, rep, get_weights, put_grads)

    parts = {k: [None] * W[k].shape[0] for k in BIG}
    res = {}

    def wait_group(entry, after):
        layer, kind, names, started = entry
        srcs, lands = _exchange_wait(started, after, scatter=True, name=f"scatter_wait_{kind}_l{layer}")
        for k, s, z in zip(names, srcs, lands):
            own = lax.dynamic_index_in_dim(s, me, 0, keepdims=False)
            parts[k][_layer_index(k, layer)] = lax.dynamic_update_index_in_dim(z, own, me, 0)

    for entry in scatter[:-1]:
        wait_group(entry, dx)
    early = [k for k in BIG if k not in scatter[-1][2]]
    def update(k):
        outs = _adamw(parts[k], _view(k, W[k]), _view(k, M[k]), _view(k, V[k]), name=f"adamw_{k}")
        return tuple(_view(k, o) for o in outs)

    for k in early:
        res[k] = update(k)
    loss, dmod, done = lax.optimization_barrier((loss, dmod, [res[k] for k in early]))
    for k, r in zip(early, done):
        res[k] = r

    small_local = [dmod.reshape(DEPTH, N_MOD * D_MODEL), g["norm_mix_g"], g["norm_ffn_g"],
                   jnp.transpose(g["gdn_conv_wt"], (0, 2, 1)), g["gdn_a_log"], g["gdn_dt_bias"], g["gdn_norm_g"],
                   g["mla_q_norm_g"], g["mla_kv_norm_g"], g["final_norm_g"], loss.reshape(1)]
    small_shapes = [a.shape for a in small_local]
    (small_g,) = _exchange([_pack(small_local)], scatter=False, name="gather_small_grads")
    small_sum = _unpack(_sum_parts(small_g, name="sum_small_grads"), small_shapes)
    loss = small_sum[-1][0]
    dmod_all = _unpack(small_g, small_shapes[:1], lead=(N_DEV,))[0]
    sg = dict(zip(SMALL, small_sum))
    wait_group(scatter[-1], small_g)
    sg["gdn_conv_w"] = lax.dynamic_slice_in_dim(sg["gdn_conv_w"], me * gdn_conv_w.shape[1], gdn_conv_w.shape[1], 1)
    sg["mla_q_norm_g"] = lax.dynamic_slice_in_dim(sg["mla_q_norm_g"], me * mla_q_norm_g.shape[1],
                                                  mla_q_norm_g.shape[1], 1)
    sg["mla_kv_norm_g"] = lax.dynamic_slice_in_dim(sg["mla_kv_norm_g"], me * mla_kv_norm_g.shape[1],
                                                   mla_kv_norm_g.shape[1], 1)

    dmod_cols = jnp.transpose(lax.dynamic_slice_in_dim(dmod_all, me * wc, wc, axis=2), (1, 0, 2))
    res["ada_w"] = _ada_grad_adamw(c_all, dmod_cols, ada_w, m_ada_w, v_ada_w, name="ada_w_grad_adamw")
    for k in BIG:
        if k not in early:
            res[k] = update(k)
    shapes = [W[k].shape for k in SMALL]
    packed = [_pack([d[k] for k in SMALL]) for d in (sg, W, M, V)]
    outs = _adamw([packed[0][None]], packed[1][None], packed[2][None], packed[3][None], name="adamw_small")
    unpacked = [_unpack(o[0], shapes) for o in outs]
    for i, k in enumerate(SMALL):
        res[k] = tuple(u[i] for u in unpacked)

    return (loss, dx[None], *[res[k][0] for k in WEIGHTS], *[res[k][1] for k in WEIGHTS],
            *[res[k][2] for k in WEIGHTS], *[res[k][3] for k in WEIGHTS])
```

```python
import math

import jax
import jax.numpy as jnp
from jax import lax
from jax.experimental import pallas as pl
from jax.experimental.pallas import tpu as pltpu

F32 = jnp.float32
BF16 = jnp.bfloat16
MXU_DTYPE = jnp.bfloat16

N_DEV = 8
D_MODEL = 1024
DEPTH = 4
GDN_HEADS = 8
GDN_HEAD_DIM = 128
GDN_KEY_DIM = GDN_HEADS * GDN_HEAD_DIM
GDN_CHUNK = 64
GDN_HEAD_BATCH = 8
GDN_CONV = 4
GDN_PREP_HEADS = 2
GDN_MAIN = 4 * GDN_KEY_DIM
MLA_HEADS = 8
MLA_NOPE = 128
MLA_ROPE = 64
MLA_V = 128
MLA_Q_RANK = 384
MLA_KV_RANK = 256
MLA_IN = MLA_Q_RANK + MLA_KV_RANK + MLA_ROPE
MLA_QK = MLA_NOPE + MLA_ROPE
ROPE_THETA = 10000.0
D_FF = 2816
N_MOD = 6
EPS = 1e-6
LANES = 128
VMEM_LIMIT = 48 * 1024 * 1024

ADAM_LR = 0.001
ADAM_B1 = 0.9
ADAM_B2 = 0.999
ADAM_EPS = 1e-08
ADAM_WD = 0.01
ADAM_STEP = 10
ADAM_BC1 = 1.0 - ADAM_B1 ** ADAM_STEP
ADAM_BC2 = 1.0 - ADAM_B2 ** ADAM_STEP

NN = (((1,), (0,)), ((), ()))
NT = (((1,), (1,)), ((), ()))
TN = (((0,), (0,)), ((), ()))
NEG = -1e30


def _dotb(a, b, dims):
    return lax.dot_general(a.astype(MXU_DTYPE), b.astype(MXU_DTYPE), dims, preferred_element_type=F32)


def _split(a):
    hi = a.astype(BF16)
    return hi, (a - hi.astype(F32)).astype(BF16)


def _dotf(a, b, dims):
    ah, al = _split(a)
    bh, bl = _split(b)
    dot = lambda u, v: lax.dot_general(u, v, dims, preferred_element_type=F32)
    return dot(ah, bh) + (dot(ah, bl) + dot(al, bh))


def _params(*sem):
    return pltpu.CompilerParams(dimension_semantics=sem, vmem_limit_bytes=VMEM_LIMIT)


def _pick(n, pref, mult=LANES):
    best = None
    t = mult
    while t <= min(n, pref):
        if n % t == 0:
            best = t
        t += mult
    return best if best is not None else n


def _sigmoid(z):
    return 0.5 * jnp.tanh(0.5 * z) + 0.5


def _exchange(arrays, *, scatter, name):
    n = len(arrays)
    out_shape = tuple(
        jax.ShapeDtypeStruct(a.shape if scatter else (N_DEV,) + a.shape, a.dtype) for a in arrays)

    def body(*refs):
        ins, outs = refs[:n], refs[n:2 * n]
        send_sems, recv_sems, local_sems = refs[2 * n:]
        x, y, c = lax.axis_index("x"), lax.axis_index("y"), lax.axis_index("c")
        me = 4 * x + 2 * y + c
        copies = []
        for k in range(n):
            src_own = ins[k].at[me] if scatter else ins[k]
            own = pltpu.make_async_copy(src_own, outs[k].at[me], local_sems.at[k])
            own.start()
            copies.append(own)
        sends = []
        for p in range(1, N_DEV):
            px, py, pc = x ^ ((p >> 2) & 1), y ^ ((p >> 1) & 1), c ^ (p & 1)
            peer = 4 * px + 2 * py + pc
            for k in range(n):
                cp = pltpu.make_async_remote_copy(
                    src_ref=ins[k].at[peer] if scatter else ins[k],
                    dst_ref=outs[k].at[me],
                    send_sem=send_sems.at[k, p - 1],
                    recv_sem=recv_sems.at[k, p - 1],
                    device_id=(px, py, pc),
                    device_id_type=pl.DeviceIdType.MESH,
                )
                cp.start()
                sends.append((cp, k, peer, p))
        for cp, k, peer, p in sends:
            pltpu.make_async_remote_copy(
                src_ref=ins[k].at[peer] if scatter else ins[k],
                dst_ref=outs[k].at[peer],
                send_sem=send_sems.at[k, p - 1],
                recv_sem=recv_sems.at[k, p - 1],
                device_id=(x, y, c),
                device_id_type=pl.DeviceIdType.MESH,
            ).wait_recv()
        for cp, _, _, _ in sends:
            cp.wait_send()
        for own in copies:
            own.wait()

    any_spec = pl.BlockSpec(memory_space=pl.ANY)
    outs = pl.pallas_call(
        body,
        name=name,
        out_shape=out_shape,
        in_specs=[any_spec] * n,
        out_specs=tuple([any_spec] * n),
        scratch_shapes=[
            pltpu.SemaphoreType.DMA((n, N_DEV - 1)),
            pltpu.SemaphoreType.DMA((n, N_DEV - 1)),
            pltpu.SemaphoreType.DMA((n,)),
        ],
        compiler_params=pltpu.CompilerParams(has_side_effects=True),
    )(*arrays)
    return list(outs)


def _gather_two_level(arrays, *, name):
    n = len(arrays)
    out_shape = tuple(jax.ShapeDtypeStruct((N_DEV,) + a.shape, a.dtype) for a in arrays)

    def body(*refs):
        ins, outs = refs[:n], refs[n:2 * n]
        send_sems, recv_sems, local_sems = refs[2 * n:]
        x, y, c = lax.axis_index("x"), lax.axis_index("y"), lax.axis_index("c")
        me = 4 * x + 2 * y + c
        sibling = (x, y, 1 - c)
        chips = [(1 - x, y), (x, 1 - y), (1 - x, 1 - y)]

        def slot(px, py, pc):
            return 4 * px + 2 * py + pc

        def copy(k, q, block, to, src=None):
            return pltpu.make_async_remote_copy(
                src_ref=outs[k].at[slot(*block)] if src is None else src,
                dst_ref=outs[k].at[slot(*block)],
                send_sem=send_sems.at[k, q], recv_sem=recv_sems.at[k, q],
                device_id=to, device_id_type=pl.DeviceIdType.MESH)

        own = [pltpu.make_async_copy(ins[k], outs[k].at[me], local_sems.at[k]) for k in range(n)]
        for cp in own:
            cp.start()
        first = []
        for k in range(n):
            first.append(copy(k, 0, (x, y, c), sibling, src=ins[k]))
            first += [copy(k, 1 + j, (x, y, c), (*chip, c), src=ins[k]) for j, chip in enumerate(chips)]
        for cp in first:
            cp.start()
        passed = []
        for j, chip in enumerate(chips):
            for k in range(n):
                copy(k, 1 + j, (*chip, c), (x, y, c)).wait_recv()
                fwd = copy(k, 4 + j, (*chip, c), sibling)
                fwd.start()
                passed.append(fwd)
        for k in range(n):
            copy(k, 0, sibling, (x, y, c)).wait_recv()
            for j, chip in enumerate(chips):
                copy(k, 4 + j, (*chip, 1 - c), (x, y, c)).wait_recv()
        for cp in first + passed:
            cp.wait_send()
        for cp in own:
            cp.wait()

    any_spec = pl.BlockSpec(memory_space=pl.ANY)
    outs = pl.pallas_call(
        body, name=name, out_shape=out_shape, in_specs=[any_spec] * n, out_specs=tuple([any_spec] * n),
        scratch_shapes=[pltpu.SemaphoreType.DMA((n, N_DEV - 1)), pltpu.SemaphoreType.DMA((n, N_DEV - 1)),
                        pltpu.SemaphoreType.DMA((n,))],
        compiler_params=pltpu.CompilerParams(has_side_effects=True),
    )(*arrays)
    return list(outs)


def _peer(x, y, c, p):
    return x ^ ((p >> 2) & 1), y ^ ((p >> 1) & 1), c ^ (p & 1)


def _exchange_start(arrays, *, scatter, name, dep=None):
    n = len(arrays)
    deps = [] if dep is None else [dep]
    lands = [lax.empty(a.shape if scatter else (N_DEV,) + a.shape, a.dtype) for a in arrays]

    def body(*refs):
        ins, zones = refs[:n], refs[n:2 * n]
        send_sems, recv_sems = refs[2 * n + len(deps)], refs[2 * n + len(deps) + 1]
        token = refs[-1]
        x, y, c = lax.axis_index("x"), lax.axis_index("y"), lax.axis_index("c")
        me = 4 * x + 2 * y + c
        for p in range(1, N_DEV):
            px, py, pc = _peer(x, y, c, p)
            for k in range(n):
                pltpu.make_async_remote_copy(
                    src_ref=ins[k].at[4 * px + 2 * py + pc] if scatter else ins[k],
                    dst_ref=zones[k].at[me],
                    send_sem=send_sems.at[k * (N_DEV - 1) + p - 1],
                    recv_sem=recv_sems.at[k * (N_DEV - 1) + p - 1],
                    device_id=(px, py, pc),
                    device_id_type=pl.DeviceIdType.MESH,
                ).start()
        token[...] = jnp.zeros_like(token)

    hbm = pl.BlockSpec(memory_space=pltpu.HBM)
    sem = pl.BlockSpec(memory_space=pltpu.SEMAPHORE)
    outs = pl.pallas_call(
        body,
        name=name,
        out_shape=(pltpu.SemaphoreType.DMA((n * (N_DEV - 1),)), pltpu.SemaphoreType.DMA((n * (N_DEV - 1),)),
                   *[pltpu.HBM(a.shape, a.dtype) for a in arrays], *[pltpu.HBM(z.shape, z.dtype) for z in lands],
                   jax.ShapeDtypeStruct((8, LANES), F32)),
        in_specs=[hbm] * (2 * n) + [pl.BlockSpec(memory_space=pl.ANY)] * len(deps),
        out_specs=(sem, sem, *[hbm] * (2 * n), pl.BlockSpec(memory_space=pltpu.VMEM)),
        input_output_aliases={k: 2 + k for k in range(2 * n)},
        compiler_params=pltpu.CompilerParams(has_side_effects=pltpu.SideEffectType.DATAFLOW_SIDE_EFFECTING),
    )(*[pltpu.with_memory_space_constraint(a, pltpu.HBM) for a in arrays],
      *[pltpu.with_memory_space_constraint(z, pltpu.HBM) for z in lands], *deps)
    return outs[0], outs[1], list(outs[2:2 + n]), list(outs[2 + n:2 + 2 * n]), outs[-1]


def _exchange_wait(started, after, *, scatter, name):
    send_sems, recv_sems, srcs, lands, _ = started
    n = len(srcs)

    def body(*refs):
        ins, zones = refs[:n], refs[n:2 * n]
        s_sems, r_sems = refs[2 * n], refs[2 * n + 1]
        x, y, c = lax.axis_index("x"), lax.axis_index("y"), lax.axis_index("c")
        for p in range(1, N_DEV):
            px, py, pc = _peer(x, y, c, p)
            peer = 4 * px + 2 * py + pc
            for k in range(n):
                cp = pltpu.make_async_remote_copy(
                    src_ref=ins[k].at[peer] if scatter else ins[k],
                    dst_ref=zones[k].at[peer],
                    send_sem=s_sems.at[k * (N_DEV - 1) + p - 1],
                    recv_sem=r_sems.at[k * (N_DEV - 1) + p - 1],
                    device_id=(px, py, pc),
                    device_id_type=pl.DeviceIdType.MESH,
                )
                cp.wait_send()
                cp.wait_recv()

    hbm = pl.BlockSpec(memory_space=pltpu.HBM)
    sem = pl.BlockSpec(memory_space=pltpu.SEMAPHORE)
    outs = pl.pallas_call(
        body,
        name=name,
        out_shape=tuple(pltpu.HBM(a.shape, a.dtype) for a in srcs + lands),
        in_specs=[hbm] * (2 * n) + [sem, sem, pl.BlockSpec(memory_space=pl.ANY)],
        out_specs=tuple([hbm] * (2 * n)),
        input_output_aliases={k: k for k in range(2 * n)},
        compiler_params=pltpu.CompilerParams(has_side_effects=pltpu.SideEffectType.DATAFLOW_SIDE_EFFECTING),
    )(*srcs, *lands, send_sems, recv_sems, after)
    return list(outs[:n]), list(outs[n:])


def _mm(a, b, *, mode, out_dtype, name, add=None, tm=512, tn=512, b_rows=None, dep=None):
    rows_b = b.shape[0] if b_rows is None else b_rows
    if mode == "nn":
        (m, kd), nd = a.shape, b.shape[1]
        assert kd == rows_b
    elif mode == "nt":
        (m, kd), nd = a.shape, rows_b
    else:
        (kd, m), nd = a.shape, b.shape[1]
    tm = _pick(m, tm, LANES if mode == "tn" else 16)
    tn = _pick(nd, tn)
    dims = {"nn": NN, "nt": NT, "tn": TN}[mode]
    ni, nj = m // tm, nd // tn
    a_bytes, b_bytes = a.size * a.dtype.itemsize, b.size * b.dtype.itemsize
    i_outer = a_bytes + ni * b_bytes <= b_bytes + nj * a_bytes
    ij = (lambda g0, g1: (g0, g1)) if i_outer else (lambda g0, g1: (g1, g0))
    a_spec = (pl.BlockSpec((kd, tm), lambda g0, g1: (0, ij(g0, g1)[0])) if mode == "tn"
              else pl.BlockSpec((tm, kd), lambda g0, g1: (ij(g0, g1)[0], 0)))
    b_spec = (pl.BlockSpec((tn, kd), lambda g0, g1: (ij(g0, g1)[1], 0)) if mode == "nt"
              else pl.BlockSpec((kd, tn), lambda g0, g1: (0, ij(g0, g1)[1])))
    o_spec = pl.BlockSpec((tm, tn), lambda g0, g1: ij(g0, g1))
    has_add = add is not None

    def body(*refs):
        a_ref, b_ref = refs[0], refs[1]
        o_ref = refs[-1]
        acc = _dotb(a_ref[...], b_ref[...], dims)
        if has_add:
            acc = acc + refs[2][...].astype(F32)
        o_ref[...] = acc.astype(o_ref.dtype)

    ins = [a, b] + ([add] if has_add else []) + ([] if dep is None else [dep])
    specs = ([a_spec, b_spec] + ([o_spec] if has_add else [])
             + ([] if dep is None else [pl.BlockSpec((8, LANES), lambda g0, g1: (0, 0))]))
    return pl.pallas_call(
        body, name=name, grid=(ni, nj) if i_outer else (nj, ni), in_specs=specs, out_specs=o_spec,
        out_shape=jax.ShapeDtypeStruct((m, nd), out_dtype),
        compiler_params=_params("parallel", "parallel"),
    )(*ins)


def _mm_pair(a1, b1, a2, b2, *, out_dtype, name, b1_rows=None, tm=256):
    m, k1 = a1.shape
    k2, nd = b2.shape
    assert k1 == (b1.shape[0] if b1_rows is None else b1_rows) and a2.shape == (m, k2) and b1.shape[1] == nd
    tm = _pick(m, tm, 16)

    def body(a1_ref, b1_ref, a2_ref, b2_ref, o_ref):
        o_ref[...] = (_dotb(a1_ref[...], b1_ref[...], NN) + _dotb(a2_ref[...], b2_ref[...], NN)).astype(o_ref.dtype)

    return pl.pallas_call(
        body, name=name, grid=(m // tm,),
        in_specs=[pl.BlockSpec((tm, k1), lambda i: (i, 0)), pl.BlockSpec((k1, nd), lambda i: (0, 0)),
                  pl.BlockSpec((tm, k2), lambda i: (i, 0)), pl.BlockSpec((k2, nd), lambda i: (0, 0))],
        out_specs=pl.BlockSpec((tm, nd), lambda i: (i, 0)), out_shape=jax.ShapeDtypeStruct((m, nd), out_dtype),
        compiler_params=_params("parallel"),
    )(a1, b1, a2, b2)


def _mm_resid(a, b, x, gate, *, name, tm=256, tn=1024):
    m, kd = a.shape
    nd = b.shape[1]
    tm = _pick(m, tm, 16)
    tn = _pick(nd, tn)
    o_spec = pl.BlockSpec((tm, tn), lambda i, j: (i, j))

    def body(a_ref, b_ref, x_ref, g_ref, xo_ref, y_ref):
        y = _dotb(a_ref[...], b_ref[...], NN)
        y_ref[...] = y.astype(y_ref.dtype)
        xo_ref[...] = x_ref[...] + g_ref[...] * y

    return pl.pallas_call(
        body, name=name, grid=(m // tm, nd // tn),
        in_specs=[pl.BlockSpec((tm, kd), lambda i, j: (i, 0)), pl.BlockSpec((kd, tn), lambda i, j: (0, j)),
                  o_spec, pl.BlockSpec((1, tn), lambda i, j: (0, j))],
        out_specs=(o_spec, o_spec),
        out_shape=(jax.ShapeDtypeStruct((m, nd), F32), jax.ShapeDtypeStruct((m, nd), BF16)),
        compiler_params=_params("parallel", "parallel"),
    )(a, b, x, gate)


ROWS = 1024


def _row_spec(width, rows=ROWS):
    return pl.BlockSpec((rows, width), lambda i: (i, 0))


def _const_spec(shape):
    return pl.BlockSpec(shape, lambda i: tuple(0 for _ in shape))


def _adaln_fwd(x, g, scale, shift, *, name):
    t, d = x.shape

    def body(x_ref, g_ref, sc_ref, sh_ref, h_ref):
        xv = x_ref[...]
        r = lax.rsqrt(jnp.mean(xv * xv, axis=-1, keepdims=True) + EPS)
        h_ref[...] = (xv * r * g_ref[...] * (1.0 + sc_ref[...]) + sh_ref[...]).astype(h_ref.dtype)

    return pl.pallas_call(
        body, name=name, grid=(t // ROWS,),
        in_specs=[_row_spec(d), _const_spec((1, d)), _const_spec((1, d)), _const_spec((1, d))],
        out_specs=_row_spec(d), out_shape=jax.ShapeDtypeStruct((t, d), BF16),
        compiler_params=_params("parallel"),
    )(x, g, scale, shift)


def _adaln_bwd(x, g, scale, shift, dh, dres, dep, *, name):
    t, d = x.shape

    def body(x_ref, g_ref, sc_ref, sh_ref, dh_ref, dr_ref, dep_ref, dx_ref, st_ref):
        @pl.when(pl.program_id(0) == 0)
        def _():
            st_ref[...] = jnp.zeros_like(st_ref)

        xv = x_ref[...]
        dhv = dh_ref[...].astype(F32)
        gv = g_ref[...]
        r = lax.rsqrt(jnp.mean(xv * xv, axis=-1, keepdims=True) + EPS)
        xh = xv * r
        nv = xh * gv
        dn = dhv * (1.0 + sc_ref[...])
        dxh = dn * gv
        dx_ref[...] = dr_ref[...] + r * (dxh - xh * jnp.mean(dxh * xh, axis=-1, keepdims=True))
        st_ref[0:1, :] += jnp.sum(dn * xh, axis=0, keepdims=True)
        st_ref[1:2, :] += jnp.sum(dhv * nv, axis=0, keepdims=True)
        st_ref[2:3, :] += jnp.sum(dhv, axis=0, keepdims=True)

    return pl.pallas_call(
        body, name=name, grid=(t // ROWS,),
        in_specs=[_row_spec(d), _const_spec((1, d)), _const_spec((1, d)), _const_spec((1, d)),
                  _row_spec(d), _row_spec(d), _const_spec((8, LANES))],
        out_specs=(_row_spec(d), _const_spec((8, d))),
        out_shape=(jax.ShapeDtypeStruct((t, d), F32), jax.ShapeDtypeStruct((8, d), F32)),
        compiler_params=_params("arbitrary"),
    )(x, g, scale, shift, dh, dres, dep)


def _adaln_gate_bwd(x, g, scale, shift, dh, dres, dep, y_up, gate_up, *, name):
    t, d = x.shape

    def body(x_ref, g_ref, sc_ref, sh_ref, dh_ref, dr_ref, dep_ref, y_ref, gu_ref, dx_ref, st_ref, dy_ref):
        @pl.when(pl.program_id(0) == 0)
        def _():
            st_ref[...] = jnp.zeros_like(st_ref)

        xv = x_ref[...]
        dhv = dh_ref[...].astype(F32)
        gv = g_ref[...]
        r = lax.rsqrt(jnp.mean(xv * xv, axis=-1, keepdims=True) + EPS)
        xh = xv * r
        nv = xh * gv
        dn = dhv * (1.0 + sc_ref[...])
        dxh = dn * gv
        dx = dr_ref[...] + r * (dxh - xh * jnp.mean(dxh * xh, axis=-1, keepdims=True))
        dx_ref[...] = dx
        dy_ref[...] = (dx * gu_ref[...]).astype(dy_ref.dtype)
        st_ref[0:1, :] += jnp.sum(dn * xh, axis=0, keepdims=True)
        st_ref[1:2, :] += jnp.sum(dhv * nv, axis=0, keepdims=True)
        st_ref[2:3, :] += jnp.sum(dhv, axis=0, keepdims=True)
        st_ref[3:4, :] += jnp.sum(dx * y_ref[...].astype(F32), axis=0, keepdims=True)

    return pl.pallas_call(
        body, name=name, grid=(t // ROWS,),
        in_specs=[_row_spec(d), _const_spec((1, d)), _const_spec((1, d)), _const_spec((1, d)),
                  _row_spec(d), _row_spec(d), _const_spec((8, LANES)), _row_spec(d), _const_spec((1, d))],
        out_specs=(_row_spec(d), _const_spec((8, d)), _row_spec(d)),
        out_shape=(jax.ShapeDtypeStruct((t, d), F32), jax.ShapeDtypeStruct((8, d), F32),
                   jax.ShapeDtypeStruct((t, d), BF16)),
        compiler_params=_params("arbitrary"),
    )(x, g, scale, shift, dh, dres, dep, y_up, gate_up)


def _loss_head(x, g, target, y_up, gate_up, *, name):
    t, d = x.shape

    def body(x_ref, g_ref, t_ref, y_ref, gu_ref, dx_ref, st_ref, ls_ref, dy_ref):
        @pl.when(pl.program_id(0) == 0)
        def _():
            st_ref[...] = jnp.zeros_like(st_ref)
            ls_ref[...] = jnp.zeros_like(ls_ref)

        xv = x_ref[...]
        gv = g_ref[...]
        r = lax.rsqrt(jnp.mean(xv * xv, axis=-1, keepdims=True) + EPS)
        xh = xv * r
        err = xh * gv - t_ref[...]
        ls_ref[...] += 0.5 * jnp.sum(jnp.mean(err * err, axis=-1, keepdims=True))
        dy = err * (1.0 / d)
        dxh = dy * gv
        dx = r * (dxh - xh * jnp.mean(dxh * xh, axis=-1, keepdims=True))
        dx_ref[...] = dx
        dy_ref[...] = (dx * gu_ref[...]).astype(dy_ref.dtype)
        st_ref[0:1, :] += jnp.sum(dy * xh, axis=0, keepdims=True)
        st_ref[3:4, :] += jnp.sum(dx * y_ref[...].astype(F32), axis=0, keepdims=True)

    return pl.pallas_call(
        body, name=name, grid=(t // ROWS,),
        in_specs=[_row_spec(d), _const_spec((1, d)), _row_spec(d), _row_spec(d), _const_spec((1, d))],
        out_specs=(_row_spec(d), _const_spec((8, d)), _const_spec((8, LANES)), _row_spec(d)),
        out_shape=(jax.ShapeDtypeStruct((t, d), F32), jax.ShapeDtypeStruct((8, d), F32),
                   jax.ShapeDtypeStruct((8, LANES), F32), jax.ShapeDtypeStruct((t, d), BF16)),
        compiler_params=_params("arbitrary"),
    )(x, g, target, y_up, gate_up)


FFN_BLOCK = D_FF // 2
FFN_ROWS = 512


def _ffn_chunks(width):
    edges = [min(width, 3 * LANES * i) for i in range(width // (3 * LANES) + 2)]
    return [slice(lo, hi) for lo, hi in zip(edges[:-1], edges[1:]) if hi > lo]


def _ffn_gu_fwd(h, wg, wu, dep, *, name):
    t, d = h.shape
    tn = FFN_BLOCK

    chunks = _ffn_chunks(tn)
    rows = _pick(t, FFN_ROWS, 16)

    def body(h_ref, wg_ref, wu_ref, dep_ref, s_ref, a_ref, b_ref):
        hv = h_ref[...]
        ab = [(_dotb(hv, wg_ref[sl, :], NT), _dotb(hv, wu_ref[sl, :], NT)) for sl in chunks]
        for sl, (a, b) in zip(chunks, ab):
            s_ref[:, sl] = (a * _sigmoid(a) * b).astype(s_ref.dtype)
            a_ref[:, sl] = a.astype(a_ref.dtype)
            b_ref[:, sl] = b.astype(b_ref.dtype)

    w_spec = pl.BlockSpec((tn, d), lambda j, i: (j, 0))
    o_spec = pl.BlockSpec((rows, tn), lambda j, i: (i, j))
    return pl.pallas_call(
        body, name=name, grid=(D_FF // tn, t // rows),
        in_specs=[pl.BlockSpec((rows, d), lambda j, i: (i, 0)), w_spec, w_spec,
                  pl.BlockSpec((8, LANES), lambda j, i: (0, 0))],
        out_specs=(o_spec, o_spec, o_spec),
        out_shape=(jax.ShapeDtypeStruct((t, D_FF), BF16),) * 3,
        compiler_params=_params("parallel", "parallel"),
    )(h, wg, wu, dep)


def _ffn_down_dx(dy, w_down, a, b, *, name):
    t, d = dy.shape
    tn = FFN_BLOCK

    chunks = _ffn_chunks(tn)
    rows = _pick(t, FFN_ROWS, 16)

    def body(dy_ref, w_ref, a_ref, b_ref, da_ref, db_ref):
        dyv = dy_ref[...]
        ds = [_dotb(dyv, w_ref[sl, :], NT) for sl in chunks]
        for sl, dsc in zip(chunks, ds):
            av = a_ref[:, sl].astype(F32)
            sg = _sigmoid(av)
            da_ref[:, sl] = (dsc * b_ref[:, sl].astype(F32) * sg * (1.0 + av * (1.0 - sg))).astype(da_ref.dtype)
            db_ref[:, sl] = (dsc * av * sg).astype(db_ref.dtype)

    o_spec = pl.BlockSpec((rows, tn), lambda j, i: (i, j))
    return pl.pallas_call(
        body, name=name, grid=(D_FF // tn, t // rows),
        in_specs=[pl.BlockSpec((rows, d), lambda j, i: (i, 0)), pl.BlockSpec((tn, d), lambda j, i: (j, 0)),
                  o_spec, o_spec],
        out_specs=(o_spec, o_spec),
        out_shape=(jax.ShapeDtypeStruct((t, D_FF), BF16),) * 2,
        compiler_params=_params("parallel", "parallel"),
    )(dy, w_down, a, b)


def _shift_rows(v, s, rows):
    if s == 0:
        return v
    return jnp.where(rows >= s, pltpu.roll(v, s, 0), 0.0)


def _unshift_rows(v, s, rows, t):
    if s == 0:
        return v
    return jnp.where(rows < t - s, pltpu.roll(v, t - s, 0), 0.0)


def _conv_taps(x, rows):
    return [_shift_rows(x, GDN_CONV - 1 - j, rows) for j in range(GDN_CONV)]


def _conv_silu(xs, w):
    z = w[0:1, :] * xs[0]
    for j in range(1, GDN_CONV):
        z = z + w[j:j + 1, :] * xs[j]
    sg = _sigmoid(z)
    return z, sg, z * sg


def _gdn_prep_fwd(proj, conv_wt, *, name):
    t = proj.shape[0]
    nh = GDN_HEADS

    hp = GDN_PREP_HEADS
    wd = hp * LANES

    def body(x_ref, w_ref, y_ref):
        j = pl.program_id(0) * hp
        rows = lax.broadcasted_iota(jnp.int32, (t, LANES), 0)
        qscale = jnp.where(j < nh, GDN_HEAD_DIM ** -0.5, 1.0)
        for i in range(hp):
            sl = slice(i * LANES, (i + 1) * LANES)
            _, _, s = _conv_silu(_conv_taps(x_ref[:, sl], rows), w_ref[:, sl])
            rs = lax.rsqrt(jnp.sum(s * s, axis=-1, keepdims=True) + EPS)
            y_ref[:, sl] = jnp.where(j < 2 * nh, s * rs * qscale, s)

    return pl.pallas_call(
        body, name=name, grid=(3 * nh // hp,),
        in_specs=[pl.BlockSpec((t, wd), lambda j: (0, j)), pl.BlockSpec((GDN_CONV, wd), lambda j: (0, j))],
        out_specs=pl.BlockSpec((t, wd), lambda j: (0, j)),
        out_shape=jax.ShapeDtypeStruct((t, 3 * GDN_KEY_DIM), F32),
        compiler_params=_params("parallel"),
    )(proj, conv_wt)


def _gdn_prep_bwd(proj, conv_wt, dy, *, name):
    t = proj.shape[0]
    nh = GDN_HEADS

    hp = GDN_PREP_HEADS
    wd = hp * LANES
    per_seg = nh // hp

    def body(x_ref, w_ref, dy_ref, dx_ref, dw_ref):
        j = pl.program_id(0) * hp
        rows = lax.broadcasted_iota(jnp.int32, (t, LANES), 0)
        qscale = jnp.where(j < nh, GDN_HEAD_DIM ** -0.5, 1.0)
        for i in range(hp):
            sl = slice(i * LANES, (i + 1) * LANES)
            w = w_ref[:, sl]
            xs = _conv_taps(x_ref[:, sl], rows)
            z, sg, s = _conv_silu(xs, w)
            rs = lax.rsqrt(jnp.sum(s * s, axis=-1, keepdims=True) + EPS)
            dyv = dy_ref[:, sl]
            nv = s * rs
            de = dyv * qscale
            ds_qk = rs * (de - nv * jnp.sum(de * nv, axis=-1, keepdims=True))
            ds = jnp.where(j < 2 * nh, ds_qk, dyv)
            dz = ds * sg * (1.0 + z * (1.0 - sg))
            dx = w[GDN_CONV - 1:GDN_CONV, :] * dz
            dw_ref[GDN_CONV - 1:GDN_CONV, sl] = jnp.sum(dz * xs[GDN_CONV - 1], axis=0, keepdims=True)
            for k in range(GDN_CONV - 1):
                dx = dx + w[k:k + 1, :] * _unshift_rows(dz, GDN_CONV - 1 - k, rows, t)
                dw_ref[k:k + 1, sl] = jnp.sum(dz * xs[k], axis=0, keepdims=True)
            dx_ref[:, sl] = dx.astype(dx_ref.dtype)

    return pl.pallas_call(
        body, name=name, grid=(3 * nh // hp,),
        in_specs=[pl.BlockSpec((t, wd), lambda j: (0, j)), pl.BlockSpec((GDN_CONV, wd), lambda j: (0, j)),
                  pl.BlockSpec((None, t, wd), lambda j: (j // per_seg, 0, j % per_seg))],
        out_specs=(pl.BlockSpec((t, wd), lambda j: (0, j)), pl.BlockSpec((GDN_CONV, wd), lambda j: (0, j))),
        out_shape=(jax.ShapeDtypeStruct((t, 3 * GDN_KEY_DIM), BF16),
                   jax.ShapeDtypeStruct((GDN_CONV, 3 * GDN_KEY_DIM), F32)),
        compiler_params=_params("parallel"),
    )(proj, conv_wt, dy)


def _softplus(z):
    return jnp.maximum(z, 0.0) + jnp.log(1.0 + jnp.exp(-jnp.abs(z)))


def _gdn_gate_fwd(ab, prm, *, name):
    t = ab.shape[0]

    def body(ab_ref, p_ref, o_ref):
        v = ab_ref[...]
        lane = lax.broadcasted_iota(jnp.int32, v.shape, 1)
        g = -jnp.exp(p_ref[0:1, :]) * _softplus(v + p_ref[1:2, :])
        o_ref[...] = jnp.where(lane < GDN_HEADS, g, jnp.where(lane < 2 * GDN_HEADS, _sigmoid(v), 0.0))

    return pl.pallas_call(
        body, name=name, grid=(t // ROWS,),
        in_specs=[_row_spec(LANES), _const_spec((8, LANES))], out_specs=_row_spec(LANES),
        out_shape=jax.ShapeDtypeStruct((t, LANES), F32), compiler_params=_params("parallel"),
    )(ab, prm)


def _gdn_gate_bwd(ab, prm, dgb, *, name):
    t = ab.shape[0]

    def body(ab_ref, p_ref, d_ref, o_ref, st_ref):
        @pl.when(pl.program_id(0) == 0)
        def _():
            st_ref[...] = jnp.zeros_like(st_ref)

        v = ab_ref[...]
        dv = d_ref[...]
        lane = lax.broadcasted_iota(jnp.int32, v.shape, 1)
        is_a = lane < GDN_HEADS
        is_b = jnp.logical_and(lane >= GDN_HEADS, lane < 2 * GDN_HEADS)
        a_exp = jnp.exp(p_ref[0:1, :])
        zz = v + p_ref[1:2, :]
        g = -a_exp * _softplus(zz)
        da = dv * (-a_exp) * _sigmoid(zz)
        beta = _sigmoid(v)
        db = dv * beta * (1.0 - beta)
        o_ref[...] = jnp.where(is_a, da, jnp.where(is_b, db, 0.0)).astype(o_ref.dtype)
        st_ref[0:1, :] += jnp.sum(jnp.where(is_a, dv * g, 0.0), axis=0, keepdims=True)
        st_ref[1:2, :] += jnp.sum(jnp.where(is_a, da, 0.0), axis=0, keepdims=True)

    return pl.pallas_call(
        body, name=name, grid=(t // ROWS,),
        in_specs=[_row_spec(LANES), _const_spec((8, LANES)), _row_spec(LANES)],
        out_specs=(_row_spec(LANES), _const_spec((8, LANES))),
        out_shape=(jax.ShapeDtypeStruct((t, LANES), BF16), jax.ShapeDtypeStruct((8, LANES), F32)),
        compiler_params=_params("arbitrary"),
    )(ab, prm, dgb)


def _gdn_local(qs, ks, vs, gbs, bbs, tinvs=None):
    nh = len(qs)
    cs = qs[0].shape[0]
    hs = range(nh)
    r = lax.broadcasted_iota(jnp.int32, (cs, cs), 0)
    c = lax.broadcasted_iota(jnp.int32, (cs, cs), 1)
    tril, strict, eye = r >= c, r > c, r == c
    ident = jnp.where(eye, 1.0, 0.0)
    g_colb = [gbs[h][:, :cs] for h in hs]
    g_row = [jnp.sum(jnp.where(eye, g_colb[h], 0.0), axis=0, keepdims=True) for h in hs]
    gc_col = [jnp.sum(jnp.where(tril, g_row[h], 0.0), axis=1, keepdims=True) for h in hs]
    gc_row = [jnp.sum(jnp.where(r <= c, g_colb[h], 0.0), axis=0, keepdims=True) for h in hs]
    decay = [jnp.exp(jnp.where(tril, gc_col[h] - gc_row[h], NEG)) for h in hs]
    gamma = [jnp.exp(gc_col[h]) for h in hs]
    gcl = [gc_col[h][cs - 1:cs, :] for h in hs]
    gl = [jnp.exp(gcl[h]) for h in hs]
    kdec = [jnp.exp(gcl[h] - gc_col[h]) for h in hs]
    kb = [ks[h] * bbs[h] for h in hs]
    kk = [_dotb(kb[h], ks[h], NT) for h in hs]
    qk = [_dotb(qs[h], ks[h], NT) for h in hs]
    lmat = [jnp.where(strict, kk[h] * decay[h], 0.0) for h in hs]
    pmat = [jnp.where(tril, qk[h] * decay[h], 0.0) for h in hs]
    if tinvs is None:
        xm = [-lmat[h] for h in hs]
        tinv = [ident + xm[h] for h in hs]
        for _ in range(int(math.log2(cs)) - 1):
            xm = [_dotf(xm[h], xm[h], NN) for h in hs]
            tinv = [tinv[h] + _dotf(tinv[h], xm[h], NN) for h in hs]
    else:
        tinv = tinvs
    vb = [vs[h] * bbs[h] for h in hs]
    kg = [kb[h] * gamma[h] for h in hs]
    u = [_dotf(tinv[h], vb[h], NN) for h in hs]
    w = [_dotf(tinv[h], kg[h], NN) for h in hs]
    return [dict(tril=tril, strict=strict, eye=eye, r=r, c=c, decay=decay[h], gamma=gamma[h], gl=gl[h], kdec=kdec[h],
                 kb=kb[h], lmat=lmat[h], tinv=tinv[h], vb=vb[h], kg=kg[h], u=u[h], w=w[h], pmat=pmat[h],
                 qd=qs[h] * gamma[h], kd=ks[h] * kdec[h]) for h in hs]


def _head_columns(gbeta, cs):
    gbs = [jnp.broadcast_to(gbeta[:, h:h + 1], (cs, LANES)) for h in range(GDN_HEADS)]
    bbs = [jnp.broadcast_to(gbeta[:, GDN_HEADS + h:GDN_HEADS + h + 1], (cs, LANES)) for h in range(GDN_HEADS)]
    return gbs, bbs


def _gdn_chunk_fwd(qkv, gbeta, *, name):
    t = qkv.shape[0]
    nh, cs, hd = GDN_HEADS, GDN_CHUNK, GDN_HEAD_DIM
    nc = t // cs

    hb = GDN_HEAD_BATCH
    ng = nh // hb
    assert ng == 1

    def body(q_ref, k_ref, v_ref, gb_ref, o_ref, st_ref, ti_ref, s_ref):
        @pl.when(pl.program_id(1) == 0)
        def _():
            s_ref[...] = jnp.zeros_like(s_ref)

        sls = [slice(i * hd, (i + 1) * hd) for i in range(hb)]
        hs = range(hb)
        s = [s_ref[i] for i in hs]
        gbs, bbs = _head_columns(gb_ref[...], cs)
        lo = _gdn_local([q_ref[:, sl] for sl in sls], [k_ref[:, sl] for sl in sls], [v_ref[:, sl] for sl in sls],
                        gbs, bbs)
        ws = [_dotb(lo[i]["w"], s[i], NN) for i in hs]
        qs = [_dotb(lo[i]["qd"], s[i], NN) for i in hs]
        vn = [lo[i]["u"] - ws[i] for i in hs]
        pv = [_dotb(lo[i]["pmat"], vn[i], NN) for i in hs]
        kv = [_dotb(lo[i]["kd"], vn[i], TN) for i in hs]
        for i, sl in enumerate(sls):
            st_ref[i, 0] = s[i]
            ti_ref[i, 0] = lo[i]["tinv"]
            o_ref[:, sl] = qs[i] + pv[i]
            s_ref[i] = s[i] * lo[i]["gl"] + kv[i]

    col = lambda off: pl.BlockSpec((cs, hb * hd), lambda h, n: (n, off + h))
    return pl.pallas_call(
        body, name=name, grid=(ng, nc),
        in_specs=[col(0), col(ng), col(2 * ng), pl.BlockSpec((cs, LANES), lambda h, n: (n, 0))],
        out_specs=(col(0), pl.BlockSpec((hb, 1, hd, hd), lambda h, n: (h, n, 0, 0)),
                   pl.BlockSpec((hb, 1, cs, cs), lambda h, n: (h, n, 0, 0))),
        out_shape=(jax.ShapeDtypeStruct((t, nh * hd), F32), jax.ShapeDtypeStruct((nh, nc, hd, hd), F32),
                   jax.ShapeDtypeStruct((nh, nc, cs, cs), F32)),
        scratch_shapes=[pltpu.VMEM((hb, hd, hd), F32)],
        compiler_params=_params("parallel", "arbitrary"),
    )(qkv, qkv, qkv, gbeta)


def _gdn_chunk_bwd(qkv, gbeta, states, tinvs, do, *, name):
    t = qkv.shape[0]
    nh, cs, hd = GDN_HEADS, GDN_CHUNK, GDN_HEAD_DIM
    nc = t // cs

    hb = GDN_HEAD_BATCH
    ng = nh // hb
    assert ng == 1

    def heads_bwd(q, k, v, gb, bb, s, ti, dsn, dov):
        hs = range(len(q))
        lo = _gdn_local(q, k, v, gb, bb, ti)
        tril, strict, eye, r, c = lo[0]["tril"], lo[0]["strict"], lo[0]["eye"], lo[0]["r"], lo[0]["c"]
        rowi = lax.broadcasted_iota(jnp.int32, (cs, 1), 0)
        get = lambda name: [lo[h][name] for h in hs]
        decay, gamma, gl, kdec = get("decay"), get("gamma"), get("gl"), get("kdec")
        kb, tinv, w, pmat, kd, qd = get("kb"), get("tinv"), get("w"), get("pmat"), get("kd"), get("qd")
        ws = [_dotb(w[h], s[h], NN) for h in hs]
        pdo = [_dotb(pmat[h], dov[h], TN) for h in hs]
        kds = [_dotb(kd[h], dsn[h], NN) for h in hs]
        dqd = [_dotb(dov[h], s[h], NT) for h in hs]
        qdo = [_dotb(qd[h], dov[h], TN) for h in hs]
        vn = [lo[h]["u"] - ws[h] for h in hs]
        dvn = [pdo[h] + kds[h] for h in hs]
        dp = [jnp.where(tril, _dotb(dov[h], vn[h], NT), 0.0) for h in hs]
        dkd = [_dotb(vn[h], dsn[h], NT) for h in hs]
        dw = [-_dotb(dvn[h], s[h], NT) for h in hs]
        wdv = [_dotb(w[h], dvn[h], TN) for h in hs]
        dvb = [_dotf(tinv[h], dvn[h], TN) for h in hs]
        dt1 = [_dotf(dvn[h], lo[h]["vb"], NT) for h in hs]
        dkg = [_dotf(tinv[h], dw[h], TN) for h in hs]
        dt2 = [_dotf(dw[h], lo[h]["kg"], NT) for h in hs]
        tdt = [_dotf(tinv[h], dt1[h] + dt2[h], TN) for h in hs]
        dl = [jnp.where(strict, -_dotf(tdt[h], tinv[h], NT), 0.0) for h in hs]
        dkk = [dl[h] * decay[h] for h in hs]
        dqk = [dp[h] * decay[h] for h in hs]
        dkb = [_dotb(dkk[h], k[h], NN) + dkg[h] * gamma[h] for h in hs]
        dk1 = [_dotb(dkk[h], kb[h], TN) for h in hs]
        dk2 = [_dotb(dqk[h], q[h], TN) for h in hs]
        dq1 = [_dotb(dqk[h], k[h], NN) for h in hs]
        out = []
        for h in hs:
            dgl = jnp.sum(jnp.sum(dsn[h] * s[h], axis=1, keepdims=True), axis=0, keepdims=True)
            ds_prev = gl[h] * dsn[h] + qdo[h] - wdv[h]
            dk = dk1[h] + dk2[h] + dkd[h] * kdec[h] + dkb[h] * bb[h]
            dq = dq1[h] + dqd[h] * gamma[h]
            dbeta = jnp.sum(dvb[h] * v[h], axis=-1, keepdims=True) + jnp.sum(dkb[h] * k[h], axis=-1, keepdims=True)
            e = dl[h] * lo[h]["lmat"] + dp[h] * pmat[h]
            e_col = jnp.sum(e, axis=0, keepdims=True)
            dgc = jnp.sum(e, axis=1, keepdims=True) - jnp.sum(jnp.where(eye, e_col, 0.0), axis=1, keepdims=True)
            dgamma = (jnp.sum(dqd[h] * q[h], axis=-1, keepdims=True)
                      + jnp.sum(dkg[h] * kb[h], axis=-1, keepdims=True))
            rk = jnp.sum(dkd[h] * k[h], axis=-1, keepdims=True) * kdec[h]
            dgcl = jnp.sum(rk, axis=0, keepdims=True) + dgl * gl[h]
            dgc = dgc + dgamma * gamma[h] - rk + jnp.where(rowi == cs - 1, dgcl, 0.0)
            dgc_row = jnp.sum(jnp.where(eye, dgc, 0.0), axis=0, keepdims=True)
            dg = jnp.sum(jnp.where(c >= r, dgc_row, 0.0), axis=1, keepdims=True)
            out.append((dq, dk, dvb[h] * bb[h], dbeta, dg, ds_prev))
        return out

    def body(q_ref, k_ref, v_ref, gb_ref, st_ref, ti_ref, do_ref, d_ref, dgb_ref, ds_ref):
        @pl.when(pl.program_id(1) == 0)
        def _():
            ds_ref[...] = jnp.zeros_like(ds_ref)

        sls = [slice(i * hd, (i + 1) * hd) for i in range(hb)]
        hs = range(hb)
        gbs, bbs = _head_columns(gb_ref[...], cs)
        outs = heads_bwd([q_ref[:, sl] for sl in sls], [k_ref[:, sl] for sl in sls], [v_ref[:, sl] for sl in sls],
                         gbs, bbs, [st_ref[i, 0] for i in hs],
                         [ti_ref[i, 0] for i in hs], [ds_ref[i] for i in hs], [do_ref[:, sl] for sl in sls])
        lane = lax.broadcasted_iota(jnp.int32, (cs, LANES), 1)
        dgb = jnp.zeros((cs, LANES), F32)
        for i, sl in enumerate(sls):
            dq, dk, dv, dbeta, dg, ds_prev = outs[i]
            d_ref[0, :, sl], d_ref[1, :, sl], d_ref[2, :, sl] = dq, dk, dv
            dgb = jnp.where(lane == i, dg, jnp.where(lane == nh + i, dbeta, dgb))
            ds_ref[i] = ds_prev
        dgb_ref[...] = dgb

    col = lambda off: pl.BlockSpec((cs, hb * hd), lambda h, n: (nc - 1 - n, off + h))
    gspec = pl.BlockSpec((cs, LANES), lambda h, n: (nc - 1 - n, 0))
    return pl.pallas_call(
        body, name=name, grid=(ng, nc),
        in_specs=[col(0), col(ng), col(2 * ng), gspec,
                  pl.BlockSpec((hb, 1, hd, hd), lambda h, n: (h, nc - 1 - n, 0, 0)),
                  pl.BlockSpec((hb, 1, cs, cs), lambda h, n: (h, nc - 1 - n, 0, 0)), col(0)],
        out_specs=(pl.BlockSpec((3, cs, hb * hd), lambda h, n: (0, nc - 1 - n, h)), gspec),
        out_shape=(jax.ShapeDtypeStruct((3, t, nh * hd), F32), jax.ShapeDtypeStruct((t, LANES), F32)),
        scratch_shapes=[pltpu.VMEM((hb, hd, hd), F32)],
        compiler_params=_params("parallel", "arbitrary"),
    )(qkv, qkv, qkv, gbeta, states, tinvs, do)


def _gdn_onorm_fwd(o, proj, norm_g, *, name):
    t = o.shape[0]
    w = GDN_KEY_DIM
    goff = 3 * GDN_KEY_DIM // w

    def body(o_ref, gp_ref, g_ref, y_ref):
        gv = g_ref[...]
        for h in range(GDN_HEADS):
            sl = slice(h * GDN_HEAD_DIM, (h + 1) * GDN_HEAD_DIM)
            oh = o_ref[:, sl]
            gp = gp_ref[:, sl]
            r = lax.rsqrt(jnp.mean(oh * oh, axis=-1, keepdims=True) + EPS)
            y_ref[:, sl] = (oh * r * gv * gp * _sigmoid(gp)).astype(y_ref.dtype)

    return pl.pallas_call(
        body, name=name, grid=(t // ROWS,),
        in_specs=[_row_spec(w), pl.BlockSpec((ROWS, w), lambda i: (i, goff)), _const_spec((1, GDN_HEAD_DIM))],
        out_specs=_row_spec(w), out_shape=jax.ShapeDtypeStruct((t, w), BF16),
        compiler_params=_params("parallel"),
    )(o, proj, norm_g)


def _gdn_onorm_bwd(o, proj, norm_g, dy, *, name):
    t = o.shape[0]
    w = GDN_KEY_DIM
    goff = 3 * GDN_KEY_DIM // w

    def body(o_ref, gp_ref, g_ref, dy_ref, do_ref, dgp_ref, st_ref):
        @pl.when(pl.program_id(0) == 0)
        def _():
            st_ref[...] = jnp.zeros_like(st_ref)

        gv = g_ref[...]
        acc = jnp.zeros((1, GDN_HEAD_DIM), F32)
        for h in range(GDN_HEADS):
            sl = slice(h * GDN_HEAD_DIM, (h + 1) * GDN_HEAD_DIM)
            oh = o_ref[:, sl]
            gp = gp_ref[:, sl]
            dyv = dy_ref[:, sl].astype(F32)
            r = lax.rsqrt(jnp.mean(oh * oh, axis=-1, keepdims=True) + EPS)
            xh = oh * r
            sg = _sigmoid(gp)
            dn = dyv * gp * sg
            dgp_ref[:, sl] = (dyv * xh * gv * sg * (1.0 + gp * (1.0 - sg))).astype(dgp_ref.dtype)
            acc = acc + jnp.sum(dn * xh, axis=0, keepdims=True)
            dxh = dn * gv
            do_ref[:, sl] = r * (dxh - xh * jnp.mean(dxh * xh, axis=-1, keepdims=True))
        st_ref[0:1, :] += acc

    return pl.pallas_call(
        body, name=name, grid=(t // ROWS,),
        in_specs=[_row_spec(w), pl.BlockSpec((ROWS, w), lambda i: (i, goff)), _const_spec((1, GDN_HEAD_DIM)),
                  _row_spec(w)],
        out_specs=(_row_spec(w), _row_spec(w), _const_spec((8, GDN_HEAD_DIM))),
        out_shape=(jax.ShapeDtypeStruct((t, w), F32), jax.ShapeDtypeStruct((t, w), BF16),
                   jax.ShapeDtypeStruct((8, GDN_HEAD_DIM), F32)),
        compiler_params=_params("arbitrary"),
    )(o, proj, norm_g, dy)


def _mla_prep_fwd(proj, qg, kvg, *, name):
    t = proj.shape[0]
    q1, k1 = MLA_Q_RANK, MLA_Q_RANK + MLA_KV_RANK

    def body(p_ref, qg_ref, kg_ref, cq_ref, ck_ref):
        cq = p_ref[:, 0:q1]
        ck = p_ref[:, q1:k1]
        cq_ref[...] = (cq * lax.rsqrt(jnp.mean(cq * cq, axis=-1, keepdims=True) + EPS) * qg_ref[...]).astype(BF16)
        ck_ref[...] = (ck * lax.rsqrt(jnp.mean(ck * ck, axis=-1, keepdims=True) + EPS) * kg_ref[...]).astype(BF16)

    return pl.pallas_call(
        body, name=name, grid=(t // ROWS,),
        in_specs=[_row_spec(MLA_IN), _const_spec((1, MLA_Q_RANK)), _const_spec((1, MLA_KV_RANK))],
        out_specs=(_row_spec(MLA_Q_RANK), _row_spec(MLA_KV_RANK)),
        out_shape=(jax.ShapeDtypeStruct((t, MLA_Q_RANK), BF16), jax.ShapeDtypeStruct((t, MLA_KV_RANK), BF16)),
        compiler_params=_params("parallel"),
    )(proj, qg, kvg)


def _mla_prep_bwd(proj, qg, kvg, dcq, dck, dkr, *, name):
    t = proj.shape[0]
    q1, k1 = MLA_Q_RANK, MLA_Q_RANK + MLA_KV_RANK

    def body(p_ref, qg_ref, kg_ref, dq_ref, dk_ref, dr_ref, dp_ref, st_ref):
        @pl.when(pl.program_id(0) == 0)
        def _():
            st_ref[...] = jnp.zeros_like(st_ref)

        for lo, hi, g_ref, d_ref in ((0, q1, qg_ref, dq_ref), (q1, k1, kg_ref, dk_ref)):
            xv = p_ref[:, lo:hi]
            dn = d_ref[...]
            r = lax.rsqrt(jnp.mean(xv * xv, axis=-1, keepdims=True) + EPS)
            xh = xv * r
            dxh = dn * g_ref[...]
            dp_ref[:, lo:hi] = (r * (dxh - xh * jnp.mean(dxh * xh, axis=-1, keepdims=True))).astype(dp_ref.dtype)
            st_ref[0:1, lo:hi] += jnp.sum(dn * xh, axis=0, keepdims=True)
        dp_ref[:, k1:MLA_IN] = dr_ref[:, 0:MLA_ROPE].astype(dp_ref.dtype)

    return pl.pallas_call(
        body, name=name, grid=(t // ROWS,),
        in_specs=[_row_spec(MLA_IN), _const_spec((1, MLA_Q_RANK)), _const_spec((1, MLA_KV_RANK)),
                  _row_spec(MLA_Q_RANK), _row_spec(MLA_KV_RANK), _row_spec(LANES)],
        out_specs=(_row_spec(MLA_IN), _const_spec((8, MLA_IN))),
        out_shape=(jax.ShapeDtypeStruct((t, MLA_IN), BF16), jax.ShapeDtypeStruct((8, MLA_IN), F32)),
        compiler_params=_params("arbitrary"),
    )(proj, qg, kvg, dcq, dck, dkr)


ATT_BLOCK = 256
ATT_HEAD_BATCH = 8
ATT_HEAD_BATCH_BWD = 4
ATT_SCALE = MLA_QK ** -0.5


def _diagonal_mask(blk):
    return lax.broadcasted_iota(jnp.int32, (blk, blk), 1) <= lax.broadcasted_iota(jnp.int32, (blk, blk), 0)


def _swap_halves(xv, first):
    return jnp.where(first, pltpu.roll(xv, LANES - MLA_ROPE // 2, 1), pltpu.roll(xv, MLA_ROPE // 2, 1))


def _rope_qk(qf, proj, cos_t, sin_t, *, name):
    t = qf.shape[0]
    nrope = MLA_HEADS * MLA_ROPE
    q_blk = MLA_HEADS * MLA_NOPE // nrope
    k_blk = (MLA_Q_RANK + MLA_KV_RANK) // LANES

    def body(q_ref, p_ref, c_ref, s_ref, qo_ref, ko_ref):
        cv, sv = c_ref[...], s_ref[...]
        lane = lax.broadcasted_iota(jnp.int32, (ROWS, LANES), 1)
        first = (lane % MLA_ROPE) < (MLA_ROPE // 2)
        for i in range(nrope // LANES):
            sl = slice(i * LANES, (i + 1) * LANES)
            xv = q_ref[:, sl].astype(F32)
            qo_ref[:, sl] = (xv * cv + _swap_halves(xv, first) * sv).astype(qo_ref.dtype)
        kv = jnp.where(lane < MLA_ROPE, p_ref[...], 0.0)
        ko_ref[...] = (kv * cv + _swap_halves(kv, first) * sv).astype(ko_ref.dtype)

    return pl.pallas_call(
        body, name=name, grid=(t // ROWS,),
        in_specs=[pl.BlockSpec((ROWS, nrope), lambda i: (i, q_blk)), pl.BlockSpec((ROWS, LANES), lambda i: (i, k_blk)),
                  _row_spec(LANES), _row_spec(LANES)],
        out_specs=(_row_spec(nrope), _row_spec(LANES)),
        out_shape=(jax.ShapeDtypeStruct((t, nrope), BF16), jax.ShapeDtypeStruct((t, LANES), BF16)),
        compiler_params=_params("parallel"),
    )(qf, proj, cos_t, sin_t)


def _rope_qk_bwd(dqr, dkr_parts, cos_t, sin_t, *, name):
    t, nrope = dqr.shape
    ng = dkr_parts.shape[0]

    def body(d_ref, k_ref, c_ref, s_ref, qo_ref, ko_ref):
        cv, sv = c_ref[...], s_ref[...]
        lane = lax.broadcasted_iota(jnp.int32, (ROWS, LANES), 1)
        first = (lane % MLA_ROPE) < (MLA_ROPE // 2)
        for i in range(nrope // LANES):
            sl = slice(i * LANES, (i + 1) * LANES)
            dv = d_ref[:, sl]
            qo_ref[:, sl] = (dv * cv + _swap_halves(dv * sv, first)).astype(qo_ref.dtype)
        dk = k_ref[0]
        for g in range(1, ng):
            dk = dk + k_ref[g]
        dk = jnp.where(lane < MLA_ROPE, dk, 0.0)
        ko_ref[...] = jnp.where(lane < MLA_ROPE, dk * cv + _swap_halves(dk * sv, first), 0.0)

    return pl.pallas_call(
        body, name=name, grid=(t // ROWS,),
        in_specs=[_row_spec(nrope), pl.BlockSpec((ng, ROWS, LANES), lambda i: (0, i, 0)), _row_spec(LANES),
                  _row_spec(LANES)],
        out_specs=(_row_spec(nrope), _row_spec(LANES)),
        out_shape=(jax.ShapeDtypeStruct((t, nrope), BF16), jax.ShapeDtypeStruct((t, LANES), F32)),
        compiler_params=_params("parallel"),
    )(dqr, dkr_parts, cos_t, sin_t)


def _attn_tm_fwd(qf, qr, kvf, kr, *, name):
    t = qf.shape[0]
    nh, dn, dr, dv = MLA_HEADS, MLA_NOPE, MLA_ROPE, MLA_V
    blk = min(ATT_BLOCK, t)
    hb = ATT_HEAD_BATCH
    hs = range(hb)

    def body(q_ref, qr_ref, kv_ref, kr_ref, o_ref, l_ref):
        i = pl.program_id(1)
        qc = [jnp.concatenate([q_ref[:, h * dn:(h + 1) * dn].astype(MXU_DTYPE), qr_ref[:, h * dr:(h + 1) * dr]], axis=1)
              for h in hs]

        def step(j, carry, diagonal=False):
            m, l, acc = carry[:hb], carry[hb:2 * hb], carry[2 * hb:]
            rows = pl.ds(pl.multiple_of(j * blk, blk), blk)
            krj = kr_ref[rows, 0:dr]
            s = [_dotb(qc[h], jnp.concatenate([kv_ref[rows, h * (dn + dv):h * (dn + dv) + dn], krj], axis=1), NT)
                 for h in hs]
            s = [s[h] * ATT_SCALE for h in hs]
            if diagonal:
                mask = _diagonal_mask(blk)
                s = [jnp.where(mask, s[h], NEG) for h in hs]
            m_new = [jnp.maximum(m[h], jnp.max(s[h], axis=-1, keepdims=True)) for h in hs]
            p = [jnp.exp(s[h] - m_new[h]) for h in hs]
            pv = [_dotb(p[h], kv_ref[rows, h * (dn + dv) + dn:(h + 1) * (dn + dv)], NN) for h in hs]
            alpha = [jnp.exp(m[h] - m_new[h]) for h in hs]
            l = [alpha[h] * l[h] + jnp.sum(p[h], axis=-1, keepdims=True) for h in hs]
            acc = [alpha[h] * acc[h] + pv[h] for h in hs]
            return tuple(m_new) + tuple(l) + tuple(acc)

        init = ((jnp.full((blk, 1), NEG, F32),) * hb + (jnp.zeros((blk, 1), F32),) * hb
                + (jnp.zeros((blk, dv), F32),) * hb)
        out = step(i, lax.fori_loop(0, i, step, init), diagonal=True)
        for h in hs:
            m, l, acc = out[h], out[hb + h], out[2 * hb + h]
            o_ref[:, h * dv:(h + 1) * dv] = (acc / l).astype(o_ref.dtype)
            l_ref[h] = jnp.broadcast_to(m + jnp.log(l), (blk, LANES))

    return pl.pallas_call(
        body, name=name, grid=(nh // hb, t // blk),
        in_specs=[pl.BlockSpec((blk, hb * dn), lambda g, i: (i, g)), pl.BlockSpec((blk, hb * dr), lambda g, i: (i, g)),
                  pl.BlockSpec((t, hb * (dn + dv)), lambda g, i: (0, g)), pl.BlockSpec((t, LANES), lambda g, i: (0, 0))],
        out_specs=(pl.BlockSpec((blk, hb * dv), lambda g, i: (i, g)),
                   pl.BlockSpec((hb, blk, LANES), lambda g, i: (g, i, 0))),
        out_shape=(jax.ShapeDtypeStruct((t, nh * dv), BF16), jax.ShapeDtypeStruct((nh, t, LANES), F32)),
        compiler_params=_params("parallel", "parallel"),
    )(qf, qr, kvf, kr)


def _attn_tm_bwd(qf, qr, kvf, kr, o, lse, do, *, name):
    t = qf.shape[0]
    nh, dn, dr, dv = MLA_HEADS, MLA_NOPE, MLA_ROPE, MLA_V
    blk = min(ATT_BLOCK, t)
    nb = t // blk
    hb = ATT_HEAD_BATCH_BWD
    hs = range(hb)
    ng = nh // hb

    def body(q_ref, qr_ref, kv_ref, kr_ref, o_ref, l_ref, do_ref, dqn_ref, dqr_ref, dkv_ref, dkr_ref):
        j = pl.program_id(1)

        @pl.when(j == 0)
        def _():
            dqn_ref[...] = jnp.zeros_like(dqn_ref)
            dqr_ref[...] = jnp.zeros_like(dqr_ref)

        krj = kr_ref[:, 0:dr]
        kc = [jnp.concatenate([kv_ref[:, h * (dn + dv):h * (dn + dv) + dn], krj], axis=1) for h in hs]
        vv = [kv_ref[:, h * (dn + dv) + dn:(h + 1) * (dn + dv)] for h in hs]

        def step(i, carry, diagonal=False):
            dkn_acc, dv_acc, dkr_acc = carry[:hb], carry[hb:2 * hb], carry[2 * hb]
            rows = pl.ds(pl.multiple_of(i * blk, blk), blk)
            qc = [jnp.concatenate([q_ref[rows, h * dn:(h + 1) * dn].astype(MXU_DTYPE),
                                   qr_ref[rows, h * dr:(h + 1) * dr]], axis=1) for h in hs]
            dov = [do_ref[rows, h * dv:(h + 1) * dv] for h in hs]
            s = [_dotb(qc[h], kc[h], NT) for h in hs]
            dp = [_dotb(dov[h], vv[h], NT) for h in hs]
            s = [s[h] * ATT_SCALE for h in hs]
            if diagonal:
                mask = _diagonal_mask(blk)
                s = [jnp.where(mask, s[h], NEG) for h in hs]
            p = [jnp.exp(s[h] - l_ref[h, rows, :][:, 0:1]) for h in hs]
            delta = [jnp.sum(dov[h].astype(F32) * o_ref[rows, h * dv:(h + 1) * dv].astype(F32), axis=-1, keepdims=True)
                     for h in hs]
            ds = [p[h] * (dp[h] - delta[h]) * ATT_SCALE for h in hs]
            dvn = [_dotb(p[h], dov[h], TN) for h in hs]
            dkc = [_dotb(ds[h], qc[h], TN) for h in hs]
            dqc = [_dotb(ds[h], kc[h], NN) for h in hs]
            for h in hs:
                dqn_ref[rows, h * dn:(h + 1) * dn] += dqc[h][:, 0:dn]
                dqr_ref[rows, h * dr:(h + 1) * dr] += dqc[h][:, dn:dn + dr]
            dkr_new = dkr_acc
            for h in hs:
                dkr_new = dkr_new + dkc[h][:, dn:dn + dr]
            return (tuple(dkn_acc[h] + dkc[h][:, 0:dn] for h in hs) + tuple(dv_acc[h] + dvn[h] for h in hs)
                    + (dkr_new,))

        init = (jnp.zeros((blk, dn), F32),) * hb + (jnp.zeros((blk, dv), F32),) * hb + (jnp.zeros((blk, dr), F32),)
        out = lax.fori_loop(j + 1, nb, step, step(j, init, diagonal=True))
        for h in hs:
            dkv_ref[:, h * (dn + dv):h * (dn + dv) + dn] = out[h].astype(dkv_ref.dtype)
            dkv_ref[:, h * (dn + dv) + dn:(h + 1) * (dn + dv)] = out[hb + h].astype(dkv_ref.dtype)
        dkr_ref[0, :, 0:dr] = out[2 * hb]
        dkr_ref[0, :, dr:LANES] = jnp.zeros((blk, LANES - dr), F32)

    full = lambda w: pl.BlockSpec((t, w), lambda g, j: (0, g))
    return pl.pallas_call(
        body, name=name, grid=(ng, nb),
        in_specs=[full(hb * dn), full(hb * dr), pl.BlockSpec((blk, hb * (dn + dv)), lambda g, j: (j, g)),
                  pl.BlockSpec((blk, LANES), lambda g, j: (j, 0)), full(hb * dv),
                  pl.BlockSpec((hb, t, LANES), lambda g, j: (g, 0, 0)), full(hb * dv)],
        out_specs=(full(hb * dn), full(hb * dr), pl.BlockSpec((blk, hb * (dn + dv)), lambda g, j: (j, g)),
                   pl.BlockSpec((1, blk, LANES), lambda g, j: (g, j, 0))),
        out_shape=(jax.ShapeDtypeStruct((t, nh * dn), F32), jax.ShapeDtypeStruct((t, nh * dr), F32),
                   jax.ShapeDtypeStruct((t, nh * (dn + dv)), BF16), jax.ShapeDtypeStruct((ng, t, LANES), F32)),
        compiler_params=_params("parallel", "arbitrary"),
    )(qf, qr, kvf, kr, o, lse, do)


def _ada_mod(c_all, ada_w, ada_b_cols, *, name):
    nl, d, wc = ada_w.shape

    def body(c_ref, w_ref, b_ref, o_ref):
        cv = c_ref[...]
        o_ref[0] = _dotb(cv * _sigmoid(cv), w_ref[0], NN) + b_ref[0]

    return pl.pallas_call(
        body, name=name, grid=(nl,),
        in_specs=[_const_spec((N_DEV, d)), pl.BlockSpec((1, d, wc), lambda l: (l, 0, 0)),
                  pl.BlockSpec((1, 1, wc), lambda l: (l, 0, 0))],
        out_specs=pl.BlockSpec((1, N_DEV, wc), lambda l: (l, 0, 0)),
        out_shape=jax.ShapeDtypeStruct((nl, N_DEV, wc), F32), compiler_params=_params("parallel"),
    )(c_all, ada_w, ada_b_cols)


def _adam_math(g, w, m, v):
    m2 = ADAM_B1 * m + (1.0 - ADAM_B1) * g
    v2 = ADAM_B2 * v + (1.0 - ADAM_B2) * (g * g)
    delta = -ADAM_LR * ((m2 / ADAM_BC1) / (jnp.sqrt(v2 / ADAM_BC2) + ADAM_EPS) + ADAM_WD * w)
    return delta, m2, v2


def _ada_grad_adamw(c_all, dmod_cols, w, m, v, *, name):
    nl, d, wc = w.shape
    tr = 256

    def body(c_ref, dm_ref, w_ref, m_ref, v_ref, g_ref, d_ref, m2_ref, v2_ref):
        cv = c_ref[...]
        g = _dotf(cv * _sigmoid(cv), dm_ref[0], TN)
        delta, m2, v2 = _adam_math(g, w_ref[0], m_ref[0], v_ref[0])
        g_ref[0], d_ref[0], m2_ref[0], v2_ref[0] = g, delta, m2, v2

    blk = pl.BlockSpec((1, tr, wc), lambda l, i: (l, i, 0))
    return pl.pallas_call(
        body, name=name, grid=(nl, d // tr),
        in_specs=[pl.BlockSpec((N_DEV, tr), lambda l, i: (0, i)), pl.BlockSpec((1, N_DEV, wc), lambda l, i: (l, 0, 0)),
                  blk, blk, blk],
        out_specs=(blk,) * 4, out_shape=(jax.ShapeDtypeStruct(w.shape, F32),) * 4,
        compiler_params=_params("parallel", "parallel"),
    )(c_all, dmod_cols, w, m, v)


def _adamw(parts, w, m, v, *, name):
    nl, r, c = w.shape
    ns = parts[0].shape[0]
    lanes_padded = -(-c // LANES) * LANES
    row_bytes = 2 * nl * ns * lanes_padded * parts[0].dtype.itemsize
    tr = _pick(r, min(256, max(16, (VMEM_LIMIT // 2) // row_bytes)), 16)
    tc = c
    if tr * row_bytes > VMEM_LIMIT // 2:
        tc = _pick(c, max(LANES, c * (VMEM_LIMIT // 2) // (tr * row_bytes)))

    def body(*refs):
        p_refs = refs[:nl]
        w_ref, m_ref, v_ref, g_ref, d_ref, m2_ref, v2_ref = refs[nl:]
        layer = pl.program_id(0)
        for q in range(nl):
            @pl.when(layer == q)
            def _(q=q):
                g = p_refs[q][0].astype(F32)
                for s in range(1, ns):
                    g = g + p_refs[q][s].astype(F32)
                delta, m2, v2 = _adam_math(g, w_ref[0], m_ref[0], v_ref[0])
                g_ref[0], d_ref[0], m2_ref[0], v2_ref[0] = g, delta, m2, v2

    blk = pl.BlockSpec((1, tr, tc), lambda l, i, j: (l, i, j))
    p_specs = [pl.BlockSpec((ns, tr, tc), lambda l, i, j, q=q: (0, jnp.where(l == q, i, 0), jnp.where(l == q, j, 0)))
               for q in range(nl)]
    return pl.pallas_call(
        body, name=name, grid=(nl, r // tr, c // tc),
        in_specs=p_specs + [blk, blk, blk],
        out_specs=(blk,) * 4, out_shape=(jax.ShapeDtypeStruct(w.shape, F32),) * 4,
        compiler_params=_params("arbitrary", "arbitrary", "arbitrary"),
    )(*parts, w, m, v)


def _sum_parts(parts, *, name):
    ns, r, c = parts.shape

    def body(p_ref, o_ref):
        acc = p_ref[0]
        for s in range(1, ns):
            acc = acc + p_ref[s]
        o_ref[...] = acc

    return pl.pallas_call(
        body, name=name, out_shape=jax.ShapeDtypeStruct((r, c), F32),
        in_specs=[pl.BlockSpec(memory_space=pltpu.VMEM)], out_specs=pl.BlockSpec(memory_space=pltpu.VMEM),
    )(parts)


def _pack(arrs):
    flat = jnp.concatenate([a.reshape(-1).astype(F32) for a in arrs])
    pad = (-flat.shape[0]) % (8 * LANES)
    return jnp.pad(flat, (0, pad)).reshape(-1, LANES)


def _unpack(packed, shapes, lead=()):
    flat = packed.reshape(lead + (-1,))
    out, off = [], 0
    for s in shapes:
        n = math.prod(s)
        out.append(flat[..., off:off + n].reshape(lead + tuple(s)))
        off += n
    return out


def _gather_rows(g):
    _, nl, rs, c = g.shape
    return jnp.transpose(g, (1, 0, 2, 3)).reshape(nl, N_DEV * rs, c)


def _row(v):
    return v.reshape(1, -1)


def _local_step(x, target, mod, cos_t, sin_t, rep, get_weights, put_grads):
    t = x.shape[0]
    saved = []
    for layer in range(DEPTH):
        j = layer // 2
        tag = f"l{layer}"
        shift_m, scale_m, gate_m, shift_f, scale_f, gate_f = [_row(mod[layer, i]) for i in range(N_MOD)]
        lw = dict(get_weights(layer, "mix", x))
        rec = {"x0": x, "lw": lw}
        h = _adaln_fwd(x, _row(rep["norm_mix_g"][layer]), scale_m, shift_m, name=f"adaln_mix_{tag}")
        rec["h"] = h
        if layer % 2 == 0:
            proj = _mm(h, lw["wt_in"], mode="nt", out_dtype=F32, tm=256, tn=GDN_MAIN, b_rows=GDN_MAIN,
                       dep=lw["dep_mix"], name=f"gdn_in_{tag}")
            ab = _mm(h, lw["wt_ab"], mode="nt", out_dtype=F32, name=f"gdn_in_ab_{tag}")
            qkv = _gdn_prep_fwd(proj, rep["gdn_conv_wt"][j], name=f"gdn_prep_{tag}")
            gbeta = _gdn_gate_fwd(ab, rep["gdn_gate_prm"][j], name=f"gdn_gate_{tag}")
            o, states, tinvs = _gdn_chunk_fwd(qkv, gbeta, name=f"gdn_chunk_{tag}")
            og = _gdn_onorm_fwd(o, proj, _row(rep["gdn_norm_g"][j]), name=f"gdn_onorm_{tag}")
            x, y = _mm_resid(og, lw["w_out"], x, gate_m, name=f"gdn_out_{tag}")
            rec.update(proj=proj, ab=ab, qkv=qkv, gbeta=gbeta, states=states, tinvs=tinvs, o=o, og=og, y=y)
        else:
            proj = _mm(h, lw["w_in"], mode="nn", out_dtype=F32, dep=lw["dep_mix"], name=f"mla_in_{tag}")
            cq, ck = _mla_prep_fwd(proj, _row(rep["mla_q_norm_g"][j]), _row(rep["mla_kv_norm_g"][j]),
                                   name=f"mla_prep_{tag}")
            qf = _mm(cq, lw["wt_uq"], mode="nt", out_dtype=BF16, name=f"mla_uq_{tag}")
            kvf = _mm(ck, lw["w_ukv"], mode="nn", out_dtype=BF16, name=f"mla_ukv_{tag}")
            qr, kr = _rope_qk(qf, proj, cos_t, sin_t, name=f"rope_{tag}")
            oc, lse = _attn_tm_fwd(qf, qr, kvf, kr, name=f"attn_{tag}")
            x, y = _mm_resid(oc, lw["w_out"], x, gate_m, name=f"mla_out_{tag}")
            rec.update(proj=proj, cq=cq, ck=ck, qf=qf, qr=qr, kvf=kvf, kr=kr, lse=lse, oc=oc, y=y)
        rec["x1"] = x
        lw.update(get_weights(layer, "ffn", x))
        h2 = _adaln_fwd(x, _row(rep["norm_ffn_g"][layer]), scale_f, shift_f, name=f"adaln_ffn_{tag}")
        s, a2, b2 = _ffn_gu_fwd(h2, lw["wt_g"], lw["wt_u"], lw["dep_ffn"], name=f"ffn_gu_{tag}")
        x, y2 = _mm_resid(s, lw["w_down"], x, gate_f, tm=512, name=f"ffn_down_{tag}")
        rec.update(h2=h2, a2=a2, b2=b2, s=s, y2=y2)
        saved.append(rec)

    dx, st, ls, dy2 = _loss_head(x, _row(rep["final_norm_g"]), target, saved[-1]["y2"],
                                 _row(mod[DEPTH - 1, N_MOD - 1]), name="loss_head")
    loss = ls[0, 0]
    dgate_f = st[3]
    grads = {"final_norm_g": st[0]}
    per_layer = {k: [None] * DEPTH for k in ("norm_mix_g", "norm_ffn_g")}
    per_gdn = {k: [None] * 2 for k in ("gdn_conv_wt", "gdn_a_log", "gdn_dt_bias", "gdn_norm_g")}
    per_mla = {k: [None] * 2 for k in ("mla_q_norm_g", "mla_kv_norm_g")}
    dmod = [None] * DEPTH

    for layer in reversed(range(DEPTH)):
        j = layer // 2
        tag = f"l{layer}"
        rec = saved[layer]
        lw = rec["lw"]
        shift_m, scale_m, gate_m, shift_f, scale_f, gate_f = [_row(mod[layer, i]) for i in range(N_MOD)]
        dw_down = _mm(rec["s"], dy2, mode="tn", out_dtype=BF16, tm=FFN_BLOCK, tn=1024, name=f"ffn_down_dw_{tag}")
        da2, db2 = _ffn_down_dx(dy2, lw["w_down"], rec["a2"], rec["b2"], name=f"ffn_down_dx_{tag}")
        dwt_g = _mm(da2, rec["h2"], mode="tn", out_dtype=BF16, tm=FFN_BLOCK, tn=1024, name=f"ffn_g_dw_{tag}")
        dwt_u = _mm(db2, rec["h2"], mode="tn", out_dtype=BF16, tm=FFN_BLOCK, tn=1024, name=f"ffn_u_dw_{tag}")
        dep = put_grads(layer, "ffn", {"wt_g": dwt_g, "wt_u": dwt_u, "w_down": dw_down})
        dh2 = _mm_pair(da2, lw["wt_g"], db2, lw["wt_u"], out_dtype=BF16, name=f"ffn_gu_dx_{tag}")
        dx, st_n, dy = _adaln_gate_bwd(rec["x1"], _row(rep["norm_ffn_g"][layer]), scale_f, shift_f, dh2, dx, dep,
                                       rec["y"], gate_m, name=f"adaln_ffn_bwd_{tag}")
        per_layer["norm_ffn_g"][layer] = st_n[0]
        dscale_f, dshift_f, dgate_m = st_n[1], st_n[2], st_n[3]
        big = {}
        if layer % 2 == 0:
            big["w_out"] = _mm(rec["og"], dy, mode="tn", out_dtype=BF16, name=f"gdn_out_dw_{tag}")
            dog = _mm(dy, lw["w_out"], mode="nt", out_dtype=BF16, name=f"gdn_out_dx_{tag}")
            do, dgp, st_o = _gdn_onorm_bwd(rec["o"], rec["proj"], _row(rep["gdn_norm_g"][j]), dog,
                                           name=f"gdn_onorm_bwd_{tag}")
            per_gdn["gdn_norm_g"][j] = st_o[0]
            dqkv, dgb = _gdn_chunk_bwd(rec["qkv"], rec["gbeta"], rec["states"], rec["tinvs"], do,
                                       name=f"gdn_chunk_bwd_{tag}")
            dab, st_a = _gdn_gate_bwd(rec["ab"], rep["gdn_gate_prm"][j], dgb, name=f"gdn_gate_bwd_{tag}")
            per_gdn["gdn_a_log"][j] = st_a[0, :GDN_HEADS]
            per_gdn["gdn_dt_bias"][j] = st_a[1, :GDN_HEADS]
            dpre, dcw = _gdn_prep_bwd(rec["proj"], rep["gdn_conv_wt"][j], dqkv, name=f"gdn_prep_bwd_{tag}")
            per_gdn["gdn_conv_wt"][j] = dcw
            dproj = jnp.concatenate([dpre, dgp], axis=1)
            dw_main = _mm(dproj, rec["h"], mode="tn", out_dtype=BF16, tm=512, tn=1024, name=f"gdn_in_dw_{tag}")
            dw_ab = _mm(dab, rec["h"], mode="tn", out_dtype=BF16, tn=1024, name=f"gdn_in_ab_dw_{tag}")
            big["wt_in"] = jnp.concatenate([dw_main, dw_ab[:2 * GDN_HEADS]], axis=0)
            dep = put_grads(layer, "gdn", big)
            dh = _mm_pair(dproj, lw["wt_in"], dab, lw["wt_ab"], out_dtype=BF16, b1_rows=GDN_MAIN,
                          name=f"gdn_in_dx_{tag}")
        else:
            big["w_out"] = _mm(rec["oc"], dy, mode="tn", out_dtype=BF16, name=f"mla_out_dw_{tag}")
            doc = _mm(dy, lw["w_out"], mode="nt", out_dtype=BF16, name=f"mla_out_dx_{tag}")
            dqn, dqr, dkvf, dkr_parts = _attn_tm_bwd(rec["qf"], rec["qr"], rec["kvf"], rec["kr"], rec["oc"],
                                                     rec["lse"], doc, name=f"attn_bwd_{tag}")
            dqr_un, dkr_un = _rope_qk_bwd(dqr, dkr_parts, cos_t, sin_t, name=f"rope_bwd_{tag}")
            n_nope = MLA_HEADS * MLA_NOPE
            big["wt_uq"] = jnp.concatenate(
                [_mm(dqn, rec["cq"], mode="tn", out_dtype=BF16, name=f"mla_uq_dw_nope_{tag}"),
                 _mm(dqr_un, rec["cq"], mode="tn", out_dtype=BF16, name=f"mla_uq_dw_rope_{tag}")], axis=0)
            big["w_ukv"] = _mm(rec["ck"], dkvf, mode="tn", out_dtype=BF16, name=f"mla_ukv_dw_{tag}")
            dcq = _mm_pair(dqn, lw["wt_uq"], dqr_un, lw["wt_uq"][n_nope:], out_dtype=F32, b1_rows=n_nope,
                           name=f"mla_uq_dx_{tag}")
            dck = _mm(dkvf, lw["w_ukv"], mode="nt", out_dtype=F32, name=f"mla_ukv_dx_{tag}")
            dproj, st_p = _mla_prep_bwd(rec["proj"], _row(rep["mla_q_norm_g"][j]), _row(rep["mla_kv_norm_g"][j]),
                                        dcq, dck, dkr_un, name=f"mla_prep_bwd_{tag}")
            per_mla["mla_q_norm_g"][j] = st_p[0, :MLA_Q_RANK]
            per_mla["mla_kv_norm_g"][j] = st_p[0, MLA_Q_RANK:MLA_Q_RANK + MLA_KV_RANK]
            big["w_in"] = _mm(rec["h"], dproj, mode="tn", out_dtype=BF16, name=f"mla_in_dw_{tag}")
            dep = put_grads(layer, "mla", big)
            dh = _mm(dproj, lw["w_in"], mode="nt", out_dtype=BF16, name=f"mla_in_dx_{tag}")
        if layer > 0:
            below = saved[layer - 1]
            dx, st_n, dy2 = _adaln_gate_bwd(rec["x0"], _row(rep["norm_mix_g"][layer]), scale_m, shift_m, dh, dx, dep,
                                            below["y2"], _row(mod[layer - 1, N_MOD - 1]),
                                            name=f"adaln_mix_bwd_{tag}")
        else:
            dx, st_n = _adaln_bwd(rec["x0"], _row(rep["norm_mix_g"][layer]), scale_m, shift_m, dh, dx, dep,
                                  name=f"adaln_mix_bwd_{tag}")
        per_layer["norm_mix_g"][layer] = st_n[0]
        dmod[layer] = jnp.stack([st_n[2], st_n[1], dgate_m, dshift_f, dscale_f, dgate_f])
        if layer > 0:
            dgate_f = st_n[3]

    for d in (per_layer, per_gdn, per_mla):
        for k, v in d.items():
            grads[k] = jnp.stack(v)
    return loss, dx, jnp.stack(dmod), grads


BIG = ("gdn_w_in", "gdn_w_out", "mla_w_in", "mla_w_uq", "mla_w_ukv", "mla_w_out", "ffn_w_gate", "ffn_w_up",
       "ffn_w_down")
TRANSPOSED = ("gdn_w_in", "mla_w_uq", "ffn_w_gate", "ffn_w_up")
AHEAD = 4


def _view(k, a):
    return jnp.transpose(a, (0, 2, 1)) if k in TRANSPOSED else a
SMALL = ("ada_b", "norm_mix_g", "norm_ffn_g", "gdn_conv_w", "gdn_a_log", "gdn_dt_bias", "gdn_norm_g",
         "mla_q_norm_g", "mla_kv_norm_g", "final_norm_g")
WEIGHTS = ("ada_w", "ada_b", "norm_mix_g", "norm_ffn_g", "gdn_w_in", "gdn_conv_w", "gdn_a_log", "gdn_dt_bias",
           "gdn_norm_g", "gdn_w_out", "mla_w_in", "mla_q_norm_g", "mla_kv_norm_g", "mla_w_uq", "mla_w_ukv",
           "mla_w_out", "ffn_w_gate", "ffn_w_up", "ffn_w_down", "final_norm_g")


def _uq_to_kernel_layout(w, axis=-1):
    axis = axis % w.ndim
    lead, tail = w.shape[:axis], w.shape[axis + 1:]
    w4 = w.reshape(lead + (MLA_HEADS, MLA_QK) + tail)
    nope = lax.slice_in_dim(w4, 0, MLA_NOPE, axis=axis + 1).reshape(lead + (-1,) + tail)
    rope = lax.slice_in_dim(w4, MLA_NOPE, MLA_QK, axis=axis + 1).reshape(lead + (-1,) + tail)
    return jnp.concatenate([nope, rope], axis=axis)


def _uq_from_kernel_layout(w, axis=-1):
    axis = axis % w.ndim
    lead, tail = w.shape[:axis], w.shape[axis + 1:]
    nope = lax.slice_in_dim(w, 0, MLA_HEADS * MLA_NOPE, axis=axis).reshape(lead + (MLA_HEADS, MLA_NOPE) + tail)
    rope = lax.slice_in_dim(w, MLA_HEADS * MLA_NOPE, MLA_HEADS * MLA_QK, axis=axis).reshape(
        lead + (MLA_HEADS, MLA_ROPE) + tail)
    return jnp.concatenate([nope, rope], axis=axis + 1).reshape(lead + (-1,) + tail)


def _group_names(layer, kind):
    if kind == "ffn":
        return ("ffn_w_gate", "ffn_w_up", "ffn_w_down")
    return ("gdn_w_in", "gdn_w_out") if layer % 2 == 0 else ("mla_w_in", "mla_w_uq", "mla_w_ukv", "mla_w_out")


def _layer_index(name, layer):
    return layer if name.startswith("ffn") else layer // 2


def _cols(g):
    return jnp.transpose(g, (1, 0, 2)).reshape(g.shape[1], N_DEV * g.shape[2])


def _rows(g):
    return g.reshape(N_DEV * g.shape[1], g.shape[2])


def _uncols(full):
    r, c = full.shape
    return jnp.transpose(full.reshape(r, N_DEV, c // N_DEV), (1, 0, 2))


def _unrows(full):
    r, c = full.shape
    return full.reshape(N_DEV, r // N_DEV, c)


def _group_weights(layer, kind, got, token):
    if kind == "ffn":
        return {"wt_g": _rows(got["ffn_w_gate"]), "wt_u": _rows(got["ffn_w_up"]), "w_down": _rows(got["ffn_w_down"]),
                "dep_ffn": token}
    if layer % 2 == 0:
        wt_in = _rows(got["gdn_w_in"])
        return dict(wt_in=wt_in, wt_ab=jnp.pad(wt_in[GDN_MAIN:], ((0, LANES - 2 * GDN_HEADS), (0, 0))),
                    w_out=_rows(got["gdn_w_out"]), dep_mix=token)
    return dict(w_in=_rows(got["mla_w_in"]), wt_uq=_uq_to_kernel_layout(_rows(got["mla_w_uq"]), axis=0),
                w_ukv=_cols(got["mla_w_ukv"]), w_out=_rows(got["mla_w_out"]), dep_mix=token)


def _layer_grad_slots(kind, big):
    if kind == "ffn":
        return {"ffn_w_gate": _unrows(big["wt_g"]), "ffn_w_up": _unrows(big["wt_u"]),
                "ffn_w_down": _unrows(big["w_down"])}
    if kind == "gdn":
        return {"gdn_w_in": _unrows(big["wt_in"]), "gdn_w_out": _unrows(big["w_out"])}
    return {"mla_w_in": _unrows(big["w_in"]), "mla_w_uq": _unrows(_uq_from_kernel_layout(big["wt_uq"], axis=0)),
            "mla_w_ukv": _uncols(big["w_ukv"]), "mla_w_out": _unrows(big["w_out"])}


def _small_weights(tiny, rep):
    prm = jnp.zeros((2, 8, LANES), F32)
    prm = prm.at[:, 0, :GDN_HEADS].set(rep["gdn_a_log"]).at[:, 1, :GDN_HEADS].set(rep["gdn_dt_bias"])
    out = {
        "gdn_conv_wt": jnp.transpose(_gather_rows(tiny["gdn_conv_w"]), (0, 2, 1)),
        "mla_q_norm_g": jnp.transpose(tiny["mla_q_norm_g"], (1, 0, 2)).reshape(2, MLA_Q_RANK),
        "mla_kv_norm_g": jnp.transpose(tiny["mla_kv_norm_g"], (1, 0, 2)).reshape(2, MLA_KV_RANK),
        "gdn_gate_prm": prm,
    }
    for k in ("norm_mix_g", "norm_ffn_g", "gdn_norm_g", "final_norm_g"):
        out[k] = rep[k]
    return out


def _rope_tables(positions):
    inv_freq = ROPE_THETA ** (-jnp.arange(0, MLA_ROPE, 2, dtype=F32) / MLA_ROPE)
    ang = positions.astype(F32)[:, None] * inv_freq
    cos, sin = jnp.cos(ang), jnp.sin(ang)
    reps = LANES // MLA_ROPE
    return jnp.tile(jnp.concatenate([cos, cos], axis=1), (1, reps)), jnp.tile(
        jnp.concatenate([-sin, sin], axis=1), (1, reps))


def kernel(x, c, positions, ada_w, ada_b, norm_mix_g, norm_ffn_g, gdn_w_in, gdn_conv_w, gdn_a_log, gdn_dt_bias, gdn_norm_g, gdn_w_out, mla_w_in, mla_q_norm_g, mla_kv_norm_g, mla_w_uq, mla_w_ukv, mla_w_out, ffn_w_gate, ffn_w_up, ffn_w_down, final_norm_g, loss_target, m_ada_w, m_ada_b, m_norm_mix_g, m_norm_ffn_g, m_gdn_w_in, m_gdn_conv_w, m_gdn_a_log, m_gdn_dt_bias, m_gdn_norm_g, m_gdn_w_out, m_mla_w_in, m_mla_q_norm_g, m_mla_kv_norm_g, m_mla_w_uq, m_mla_w_ukv, m_mla_w_out, m_ffn_w_gate, m_ffn_w_up, m_ffn_w_down, m_final_norm_g, v_ada_w, v_ada_b, v_norm_mix_g, v_norm_ffn_g, v_gdn_w_in, v_gdn_conv_w, v_gdn_a_log, v_gdn_dt_bias, v_gdn_norm_g, v_gdn_w_out, v_mla_w_in, v_mla_q_norm_g, v_mla_kv_norm_g, v_mla_w_uq, v_mla_w_ukv, v_mla_w_out, v_ffn_w_gate, v_ffn_w_up, v_ffn_w_down, v_final_norm_g):
    W = dict(ada_w=ada_w, ada_b=ada_b, norm_mix_g=norm_mix_g, norm_ffn_g=norm_ffn_g, gdn_w_in=gdn_w_in,
             gdn_conv_w=gdn_conv_w, gdn_a_log=gdn_a_log, gdn_dt_bias=gdn_dt_bias, gdn_norm_g=gdn_norm_g,
             gdn_w_out=gdn_w_out, mla_w_in=mla_w_in, mla_q_norm_g=mla_q_norm_g, mla_kv_norm_g=mla_kv_norm_g,
             mla_w_uq=mla_w_uq, mla_w_ukv=mla_w_ukv, mla_w_out=mla_w_out, ffn_w_gate=ffn_w_gate,
             ffn_w_up=ffn_w_up, ffn_w_down=ffn_w_down, final_norm_g=final_norm_g)
    M = dict(ada_w=m_ada_w, ada_b=m_ada_b, norm_mix_g=m_norm_mix_g, norm_ffn_g=m_norm_ffn_g, gdn_w_in=m_gdn_w_in,
             gdn_conv_w=m_gdn_conv_w, gdn_a_log=m_gdn_a_log, gdn_dt_bias=m_gdn_dt_bias, gdn_norm_g=m_gdn_norm_g,
             gdn_w_out=m_gdn_w_out, mla_w_in=m_mla_w_in, mla_q_norm_g=m_mla_q_norm_g,
             mla_kv_norm_g=m_mla_kv_norm_g, mla_w_uq=m_mla_w_uq, mla_w_ukv=m_mla_w_ukv, mla_w_out=m_mla_w_out,
             ffn_w_gate=m_ffn_w_gate, ffn_w_up=m_ffn_w_up, ffn_w_down=m_ffn_w_down, final_norm_g=m_final_norm_g)
    V = dict(ada_w=v_ada_w, ada_b=v_ada_b, norm_mix_g=v_norm_mix_g, norm_ffn_g=v_norm_ffn_g, gdn_w_in=v_gdn_w_in,
             gdn_conv_w=v_gdn_conv_w, gdn_a_log=v_gdn_a_log, gdn_dt_bias=v_gdn_dt_bias, gdn_norm_g=v_gdn_norm_g,
             gdn_w_out=v_gdn_w_out, mla_w_in=v_mla_w_in, mla_q_norm_g=v_mla_q_norm_g,
             mla_kv_norm_g=v_mla_kv_norm_g, mla_w_uq=v_mla_w_uq, mla_w_ukv=v_mla_w_ukv, mla_w_out=v_mla_w_out,
             ffn_w_gate=v_ffn_w_gate, ffn_w_up=v_ffn_w_up, ffn_w_down=v_ffn_w_down, final_norm_g=v_final_norm_g)
    me = 4 * lax.axis_index("x") + 2 * lax.axis_index("y") + lax.axis_index("c")
    t = x.shape[1]
    wc = ada_w.shape[-1]

    groups = [(layer, kind) for layer in range(DEPTH) for kind in ("mix", "ffn")]

    def group_srcs(i):
        layer, kind = groups[i]
        return [_view(k, W[k])[_layer_index(k, layer)].astype(BF16) for k in _group_names(layer, kind)]

    tiny_shapes = [c.shape, gdn_conv_w.shape, mla_q_norm_g.shape, mla_kv_norm_g.shape]
    first = _gather_two_level([_pack([c, gdn_conv_w, mla_q_norm_g, mla_kv_norm_g])] + group_srcs(0),
                              name="gather_first")
    tiny_g = first[0]
    c_g, conv_g, qn_g, kvn_g = _unpack(tiny_g, tiny_shapes, lead=(N_DEV,))
    c_all = c_g.reshape(N_DEV, D_MODEL)
    rep = _small_weights({"gdn_conv_w": conv_g, "mla_q_norm_g": qn_g, "mla_kv_norm_g": kvn_g}, W)

    def start_group(i, dep):
        layer, kind = groups[i]
        return _exchange_start(group_srcs(i), scatter=False, name=f"gather_start_{kind}_l{layer}", dep=dep)


    b_cols = lax.dynamic_slice_in_dim(ada_b, me * wc, wc, axis=1).reshape(DEPTH, 1, wc)
    mod_part = _ada_mod(c_all, ada_w, b_cols, name="ada_mod")
    (mod_g,) = _exchange([mod_part], scatter=False, name="gather_mod")
    mod_mine = lax.dynamic_index_in_dim(mod_g, me, axis=2, keepdims=False)
    mod = jnp.transpose(mod_mine, (1, 0, 2)).reshape(DEPTH, N_MOD, D_MODEL)
    gather = {1: start_group(1, mod_g)}
    for i in range(2, AHEAD + 1):
        gather[i] = start_group(i, gather[i - 1][4])

    def get_weights(layer, kind, after):
        i = groups.index((layer, kind))
        names = _group_names(layer, kind)
        if i == 0:
            return _group_weights(layer, kind, dict(zip(names, first[1:])), gather[AHEAD][4])
        srcs, lands = _exchange_wait(gather[i], after, scatter=False, name=f"gather_wait_{kind}_l{layer}")
        token = jnp.zeros((8, LANES), F32)
        if i + AHEAD < len(groups):
            gather[i + AHEAD] = start_group(i + AHEAD, lands[0])
            token = gather[i + AHEAD][4]
        got = {k: lax.dynamic_update_index_in_dim(z, s, me, 0) for k, s, z in zip(names, srcs, lands)}
        return _group_weights(layer, kind, got, token)

    scatter = []

    def put_grads(layer, kind, big):
        slots = _layer_grad_slots(kind, big)
        started = _exchange_start(list(slots.values()), scatter=True, name=f"scatter_start_{kind}_l{layer}")
        scatter.append((layer, kind, list(slots.keys()), started))
        return started[4]

    cos_t, sin_t = _rope_tables(positions[0])
    loss, dx, dmod, g = _local_step(x[0], loss_target[0], mod, cos_t, sin_t, rep, get_weights, put_grads)

    parts = {k: [None] * W[k].shape[0] for k in BIG}
    res = {}

    def wait_group(entry, after):
        layer, kind, names, started = entry
        srcs, lands = _exchange_wait(started, after, scatter=True, name=f"scatter_wait_{kind}_l{layer}")
        for k, s, z in zip(names, srcs, lands):
            own = lax.dynamic_index_in_dim(s, me, 0, keepdims=False)
            parts[k][_layer_index(k, layer)] = lax.dynamic_update_index_in_dim(z, own, me, 0)

    for entry in scatter[:-1]:
        wait_group(entry, dx)
    early = [k for k in BIG if k not in scatter[-1][2]]
    def update(k):
        outs = _adamw(parts[k], _view(k, W[k]), _view(k, M[k]), _view(k, V[k]), name=f"adamw_{k}")
        return tuple(_view(k, o) for o in outs)

    for k in early:
        res[k] = update(k)
    loss, dmod, done = lax.optimization_barrier((loss, dmod, [res[k] for k in early]))
    for k, r in zip(early, done):
        res[k] = r

    small_local = [dmod.reshape(DEPTH, N_MOD * D_MODEL), g["norm_mix_g"], g["norm_ffn_g"],
                   jnp.transpose(g["gdn_conv_wt"], (0, 2, 1)), g["gdn_a_log"], g["gdn_dt_bias"], g["gdn_norm_g"],
                   g["mla_q_norm_g"], g["mla_kv_norm_g"], g["final_norm_g"], loss.reshape(1)]
    small_shapes = [a.shape for a in small_local]
    (small_g,) = _exchange([_pack(small_local)], scatter=False, name="gather_small_grads")
    small_sum = _unpack(_sum_parts(small_g, name="sum_small_grads"), small_shapes)
    loss = small_sum[-1][0]
    dmod_all = _unpack(small_g, small_shapes[:1], lead=(N_DEV,))[0]
    sg = dict(zip(SMALL, small_sum))
    wait_group(scatter[-1], small_g)
    sg["gdn_conv_w"] = lax.dynamic_slice_in_dim(sg["gdn_conv_w"], me * gdn_conv_w.shape[1], gdn_conv_w.shape[1], 1)
    sg["mla_q_norm_g"] = lax.dynamic_slice_in_dim(sg["mla_q_norm_g"], me * mla_q_norm_g.shape[1],
                                                  mla_q_norm_g.shape[1], 1)
    sg["mla_kv_norm_g"] = lax.dynamic_slice_in_dim(sg["mla_kv_norm_g"], me * mla_kv_norm_g.shape[1],
                                                   mla_kv_norm_g.shape[1], 1)

    dmod_cols = jnp.transpose(lax.dynamic_slice_in_dim(dmod_all, me * wc, wc, axis=2), (1, 0, 2))
    res["ada_w"] = _ada_grad_adamw(c_all, dmod_cols, ada_w, m_ada_w, v_ada_w, name="ada_w_grad_adamw")
    for k in BIG:
        if k not in early:
            res[k] = update(k)
    shapes = [W[k].shape for k in SMALL]
    packed = [_pack([d[k] for k in SMALL]) for d in (sg, W, M, V)]
    outs = _adamw([packed[0][None]], packed[1][None], packed[2][None], packed[3][None], name="adamw_small")
    unpacked = [_unpack(o[0], shapes) for o in outs]
    for i, k in enumerate(SMALL):
        res[k] = tuple(u[i] for u in unpacked)

    return (loss, dx[None], *[res[k][0] for k in WEIGHTS], *[res[k][1] for k in WEIGHTS],
            *[res[k][2] for k in WEIGHTS], *[res[k][3] for k in WEIGHTS])
```
